```python
import jax, jax.numpy as jnp
from jax import lax
import numpy as np

D_MODEL = 1024
BATCH = 8
SEQ = 8192
DEPTH = 1

D_MIX = D_MODEL
D_LRU = D_MIX // 2
LRU_HEADS = 8
LRU_HEAD_DIM = D_LRU // LRU_HEADS
LRU_CONV = 4
LRU_C = 8.0
D_RET = D_MIX - D_LRU
RET_HEADS = 4
RET_HEAD_DIM = D_RET // RET_HEADS
RET_CHUNK = 128
ROPE_BASE = 10000.0
D_IN = 2 * D_LRU + 4 * D_RET
SPLITS = (D_LRU, 2 * D_LRU, 2 * D_LRU + D_RET, 2 * D_LRU + 2 * D_RET, 2 * D_LRU + 3 * D_RET)
D_FF = 3 * D_MODEL
FFN_CONV = 3
NORM_EPS = 1e-6

kernel_name = "hymba_rglru_retention_convffn_block"


def rms_norm(x, gain):
    xf = x.astype(jnp.float32)
    y = xf * lax.rsqrt(jnp.mean(xf * xf, axis=-1, keepdims=True) + NORM_EPS)
    return (y * gain.astype(jnp.float32)).astype(x.dtype)


def causal_depthwise_conv(x, w, b):
    width, ch = w.shape
    y = lax.conv_general_dilated(
        x, w[:, None, :].astype(x.dtype), window_strides=(1,),
        padding=[(width - 1, 0)], dimension_numbers=("NWC", "WIO", "NWC"),
        feature_group_count=ch)
    return y + b.astype(x.dtype)


def _linear_recurrence_combine(left, right):
    a1, b1 = left
    a2, b2 = right
    return a1 * a2, a2 * b1 + b2


def rg_lru(x, wa, ba, wx, bx, lam):
    bsz, slen, _ = x.shape
    xf = x.astype(jnp.float32)
    xh = xf.reshape(bsz, slen, LRU_HEADS, LRU_HEAD_DIM)
    r = jax.nn.sigmoid(jnp.einsum("bshi,hij->bshj", xh, wa.astype(jnp.float32)).reshape(bsz, slen, D_LRU)
                       + ba.astype(jnp.float32))
    i = jax.nn.sigmoid(jnp.einsum("bshi,hij->bshj", xh, wx.astype(jnp.float32)).reshape(bsz, slen, D_LRU)
                       + bx.astype(jnp.float32))
    log_a = -LRU_C * r * jax.nn.softplus(-lam.astype(jnp.float32))
    a = jnp.exp(log_a)
    inp = jnp.sqrt(-jnp.expm1(2.0 * log_a)) * (i * xf)
    _, h = lax.associative_scan(_linear_recurrence_combine, (a, inp), axis=1)
    return h


def rotary(x, cos, sin):
    half = x.shape[-1] // 2
    x1, x2 = x[..., :half], x[..., half:]
    return jnp.concatenate([x1 * cos - x2 * sin, x2 * cos + x1 * sin], axis=-1)


def retention_chunkwise(q, k, v):
    bsz, slen, nh, dk = q.shape
    dv = v.shape[-1]
    c = RET_CHUNK
    n = slen // c
    log_g = jnp.log1p(-jnp.exp2(-5.0 - jnp.arange(nh, dtype=jnp.float32)))
    q = q.reshape(bsz, n, c, nh, dk)
    k = (k * (dk ** -0.5)).reshape(bsz, n, c, nh, dk)
    v = v.reshape(bsz, n, c, nh, dv)
    idx = jnp.arange(c, dtype=jnp.float32)
    diff = idx[:, None] - idx[None, :]
    decay_in = jnp.where(diff[None] >= 0, jnp.exp(jnp.maximum(diff, 0.0)[None] * log_g[:, None, None]), 0.0)
    scores = jnp.einsum("bnihd,bnjhd->bnhij", q, k) * decay_in[None, None]
    inner = jnp.einsum("bnhij,bnjhe->bnihe", scores, v)
    zeta = jnp.exp((c - 1 - idx)[None, :] * log_g[:, None])
    kv = jnp.einsum("bnjhd,bnjhe,hj->nbhde", k, v, zeta)
    g_chunk = jnp.exp(c * log_g)[None, :, None, None]

    def step(state, kv_n):
        return state * g_chunk + kv_n, state

    init = jnp.zeros((bsz, nh, dk, dv), jnp.float32)
    _, prev = lax.scan(step, init, kv)
    xi = jnp.exp((idx + 1.0)[None, :] * log_g[:, None])
    cross = jnp.einsum("bnihd,nbhde,hi->bnihe", q, prev, xi)
    return (inner + cross).reshape(bsz, slen, nh, dv)


def head_group_norm(o, gain):
    mu = jnp.mean(o, axis=-1, keepdims=True)
    var = jnp.mean(jnp.square(o - mu), axis=-1, keepdims=True)
    return (o - mu) * lax.rsqrt(var + NORM_EPS) * gain.astype(jnp.float32).reshape(RET_HEADS, RET_HEAD_DIM)


def _fwd_setup_inputs(seed: int = 0) -> dict:
    key = jax.random.key(seed)
    ks = jax.random.split(key, 20)
    f32 = jnp.float32

    def nrm(k, shape, fan_in):
        return jax.random.normal(k, shape, f32) * (fan_in ** -0.5)

    def gain(k, shape):
        return 1.0 + 0.05 * jax.random.normal(k, shape, f32)

    def bias(k, shape):
        return 0.02 * jax.random.normal(k, shape, f32)

    a0 = jax.random.uniform(ks[9], (DEPTH, D_LRU), f32, minval=0.9, maxval=0.999)
    return {
        "x": jax.random.normal(ks[0], (BATCH, SEQ, D_MODEL), f32),
        "norm1_gain": gain(ks[1], (DEPTH, D_MODEL)),
        "w_in": nrm(ks[2], (DEPTH, D_MODEL, D_IN), D_MODEL),
        "lru_conv_w": nrm(ks[3], (DEPTH, LRU_CONV, D_LRU), LRU_CONV),
        "lru_conv_b": bias(ks[4], (DEPTH, D_LRU)),
        "lru_gate_a_w": nrm(ks[5], (DEPTH, LRU_HEADS, LRU_HEAD_DIM, LRU_HEAD_DIM), LRU_HEAD_DIM),
        "lru_gate_a_b": bias(ks[6], (DEPTH, D_LRU)),
        "lru_gate_x_w": nrm(ks[7], (DEPTH, LRU_HEADS, LRU_HEAD_DIM, LRU_HEAD_DIM), LRU_HEAD_DIM),
        "lru_gate_x_b": bias(ks[8], (DEPTH, D_LRU)),
        "lru_lambda": jnp.log(a0) - jnp.log1p(-a0),
        "lru_norm_gain": gain(ks[10], (DEPTH, D_LRU)),
        "ret_norm_gain": gain(ks[11], (DEPTH, D_RET)),
        "w_out": nrm(ks[12], (DEPTH, D_MIX, D_MODEL), D_MIX),
        "norm2_gain": gain(ks[13], (DEPTH, D_MODEL)),
        "ffn_up_w": nrm(ks[14], (DEPTH, D_MODEL, 2 * D_FF), D_MODEL),
        "ffn_conv_w": nrm(ks[15], (DEPTH, FFN_CONV, 2 * D_FF), FFN_CONV),
        "ffn_conv_b": bias(ks[16], (DEPTH, 2 * D_FF)),
        "ffn_down_w": nrm(ks[17], (DEPTH, D_FF, D_MODEL), D_FF),
        "final_norm_gain": gain(ks[18], (D_MODEL,)),
    }


def _fwd_reference(x, norm1_gain, w_in, lru_conv_w, lru_conv_b, lru_gate_a_w, lru_gate_a_b,
              lru_gate_x_w, lru_gate_x_b, lru_lambda, lru_norm_gain, ret_norm_gain, w_out,
              norm2_gain, ffn_up_w, ffn_conv_w, ffn_conv_b, ffn_down_w, final_norm_gain):
    bsz, slen, _ = x.shape
    dt = x.dtype
    pos = jnp.arange(slen, dtype=jnp.float32)
    inv_freq = ROPE_BASE ** (-jnp.arange(0, RET_HEAD_DIM, 2, dtype=jnp.float32) / RET_HEAD_DIM)
    ang = pos[:, None] * inv_freq[None, :]
    cos = jnp.cos(ang)[:, None, :]
    sin = jnp.sin(ang)[:, None, :]

    h = x
    for l in range(DEPTH):
        u = rms_norm(h, norm1_gain[l])
        proj = u @ w_in[l].astype(dt)
        x_lru, g_lru, q, k, v, g_ret = jnp.split(proj, SPLITS, axis=-1)

        xc = causal_depthwise_conv(x_lru, lru_conv_w[l], lru_conv_b[l])
        hl = rg_lru(xc, lru_gate_a_w[l], lru_gate_a_b[l], lru_gate_x_w[l], lru_gate_x_b[l], lru_lambda[l])
        y_lru = rms_norm(hl.astype(dt) * jax.nn.gelu(g_lru), lru_norm_gain[l])

        qh = rotary(q.astype(jnp.float32).reshape(bsz, slen, RET_HEADS, RET_HEAD_DIM), cos, sin)
        kh = rotary(k.astype(jnp.float32).reshape(bsz, slen, RET_HEADS, RET_HEAD_DIM), cos, sin)
        vh = v.astype(jnp.float32).reshape(bsz, slen, RET_HEADS, RET_HEAD_DIM)
        o = head_group_norm(retention_chunkwise(qh, kh, vh), ret_norm_gain[l]).reshape(bsz, slen, D_RET)
        y_ret = o.astype(dt) * jax.nn.silu(g_ret)

        mixed = jnp.concatenate([y_lru, y_ret], axis=-1)
        h = h + mixed @ w_out[l].astype(dt)

        u = rms_norm(h, norm2_gain[l])
        up = causal_depthwise_conv(u @ ffn_up_w[l].astype(dt), ffn_conv_w[l], ffn_conv_b[l])
        a_branch, v_branch = jnp.split(up, 2, axis=-1)
        h = h + (jax.nn.gelu(a_branch) * v_branch) @ ffn_down_w[l].astype(dt)

    return rms_norm(h, final_norm_gain)


import jax as _jax
import jax.numpy as _jnp

TWIN_FORMAT = 'train_step'
FWD_PARAMS = ['x', 'norm1_gain', 'w_in', 'lru_conv_w', 'lru_conv_b', 'lru_gate_a_w', 'lru_gate_a_b', 'lru_gate_x_w', 'lru_gate_x_b', 'lru_lambda', 'lru_norm_gain', 'ret_norm_gain', 'w_out', 'norm2_gain', 'ffn_up_w', 'ffn_conv_w', 'ffn_conv_b', 'ffn_down_w', 'final_norm_gain']
TWIN_WEIGHTS = ['norm1_gain', 'w_in', 'lru_conv_w', 'lru_conv_b', 'lru_gate_a_w', 'lru_gate_a_b', 'lru_gate_x_w', 'lru_gate_x_b', 'lru_lambda', 'lru_norm_gain', 'ret_norm_gain', 'w_out', 'norm2_gain', 'ffn_up_w', 'ffn_conv_w', 'ffn_conv_b', 'ffn_down_w', 'final_norm_gain']
TWIN_DIFF_INPUT = 'x'
TWIN_INPUTS = ['x', 'norm1_gain', 'w_in', 'lru_conv_w', 'lru_conv_b', 'lru_gate_a_w', 'lru_gate_a_b', 'lru_gate_x_w', 'lru_gate_x_b', 'lru_lambda', 'lru_norm_gain', 'ret_norm_gain', 'w_out', 'norm2_gain', 'ffn_up_w', 'ffn_conv_w', 'ffn_conv_b', 'ffn_down_w', 'final_norm_gain', 'loss_target', 'm_norm1_gain', 'm_w_in', 'm_lru_conv_w', 'm_lru_conv_b', 'm_lru_gate_a_w', 'm_lru_gate_a_b', 'm_lru_gate_x_w', 'm_lru_gate_x_b', 'm_lru_lambda', 'm_lru_norm_gain', 'm_ret_norm_gain', 'm_w_out', 'm_norm2_gain', 'm_ffn_up_w', 'm_ffn_conv_w', 'm_ffn_conv_b', 'm_ffn_down_w', 'm_final_norm_gain', 'v_norm1_gain', 'v_w_in', 'v_lru_conv_w', 'v_lru_conv_b', 'v_lru_gate_a_w', 'v_lru_gate_a_b', 'v_lru_gate_x_w', 'v_lru_gate_x_b', 'v_lru_lambda', 'v_lru_norm_gain', 'v_ret_norm_gain', 'v_w_out', 'v_norm2_gain', 'v_ffn_up_w', 'v_ffn_conv_w', 'v_ffn_conv_b', 'v_ffn_down_w', 'v_final_norm_gain']
TWIN_OUTPUTS = ['loss', 'grad_x', 'grad_norm1_gain', 'grad_w_in', 'grad_lru_conv_w', 'grad_lru_conv_b', 'grad_lru_gate_a_w', 'grad_lru_gate_a_b', 'grad_lru_gate_x_w', 'grad_lru_gate_x_b', 'grad_lru_lambda', 'grad_lru_norm_gain', 'grad_ret_norm_gain', 'grad_w_out', 'grad_norm2_gain', 'grad_ffn_up_w', 'grad_ffn_conv_w', 'grad_ffn_conv_b', 'grad_ffn_down_w', 'grad_final_norm_gain', 'delta_norm1_gain', 'delta_w_in', 'delta_lru_conv_w', 'delta_lru_conv_b', 'delta_lru_gate_a_w', 'delta_lru_gate_a_b', 'delta_lru_gate_x_w', 'delta_lru_gate_x_b', 'delta_lru_lambda', 'delta_lru_norm_gain', 'delta_ret_norm_gain', 'delta_w_out', 'delta_norm2_gain', 'delta_ffn_up_w', 'delta_ffn_conv_w', 'delta_ffn_conv_b', 'delta_ffn_down_w', 'delta_final_norm_gain', 'new_m_norm1_gain', 'new_m_w_in', 'new_m_lru_conv_w', 'new_m_lru_conv_b', 'new_m_lru_gate_a_w', 'new_m_lru_gate_a_b', 'new_m_lru_gate_x_w', 'new_m_lru_gate_x_b', 'new_m_lru_lambda', 'new_m_lru_norm_gain', 'new_m_ret_norm_gain', 'new_m_w_out', 'new_m_norm2_gain', 'new_m_ffn_up_w', 'new_m_ffn_conv_w', 'new_m_ffn_conv_b', 'new_m_ffn_down_w', 'new_m_final_norm_gain', 'new_v_norm1_gain', 'new_v_w_in', 'new_v_lru_conv_w', 'new_v_lru_conv_b', 'new_v_lru_gate_a_w', 'new_v_lru_gate_a_b', 'new_v_lru_gate_x_w', 'new_v_lru_gate_x_b', 'new_v_lru_lambda', 'new_v_lru_norm_gain', 'new_v_ret_norm_gain', 'new_v_w_out', 'new_v_norm2_gain', 'new_v_ffn_up_w', 'new_v_ffn_conv_w', 'new_v_ffn_conv_b', 'new_v_ffn_down_w', 'new_v_final_norm_gain']
TWIN_LEAF_KINDS = {'loss': 'loss', 'grad_x': 'grad_x', 'grad_norm1_gain': 'grad_w', 'grad_w_in': 'grad_w', 'grad_lru_conv_w': 'grad_w', 'grad_lru_conv_b': 'grad_w', 'grad_lru_gate_a_w': 'grad_w', 'grad_lru_gate_a_b': 'grad_w', 'grad_lru_gate_x_w': 'grad_w', 'grad_lru_gate_x_b': 'grad_w', 'grad_lru_lambda': 'grad_w', 'grad_lru_norm_gain': 'grad_w', 'grad_ret_norm_gain': 'grad_w', 'grad_w_out': 'grad_w', 'grad_norm2_gain': 'grad_w', 'grad_ffn_up_w': 'grad_w', 'grad_ffn_conv_w': 'grad_w', 'grad_ffn_conv_b': 'grad_w', 'grad_ffn_down_w': 'grad_w', 'grad_final_norm_gain': 'grad_w', 'delta_norm1_gain': 'delta_w', 'delta_w_in': 'delta_w', 'delta_lru_conv_w': 'delta_w', 'delta_lru_conv_b': 'delta_w', 'delta_lru_gate_a_w': 'delta_w', 'delta_lru_gate_a_b': 'delta_w', 'delta_lru_gate_x_w': 'delta_w', 'delta_lru_gate_x_b': 'delta_w', 'delta_lru_lambda': 'delta_w', 'delta_lru_norm_gain': 'delta_w', 'delta_ret_norm_gain': 'delta_w', 'delta_w_out': 'delta_w', 'delta_norm2_gain': 'delta_w', 'delta_ffn_up_w': 'delta_w', 'delta_ffn_conv_w': 'delta_w', 'delta_ffn_conv_b': 'delta_w', 'delta_ffn_down_w': 'delta_w', 'delta_final_norm_gain': 'delta_w', 'new_m_norm1_gain': 'new_m', 'new_m_w_in': 'new_m', 'new_m_lru_conv_w': 'new_m', 'new_m_lru_conv_b': 'new_m', 'new_m_lru_gate_a_w': 'new_m', 'new_m_lru_gate_a_b': 'new_m', 'new_m_lru_gate_x_w': 'new_m', 'new_m_lru_gate_x_b': 'new_m', 'new_m_lru_lambda': 'new_m', 'new_m_lru_norm_gain': 'new_m', 'new_m_ret_norm_gain': 'new_m', 'new_m_w_out': 'new_m', 'new_m_norm2_gain': 'new_m', 'new_m_ffn_up_w': 'new_m', 'new_m_ffn_conv_w': 'new_m', 'new_m_ffn_conv_b': 'new_m', 'new_m_ffn_down_w': 'new_m', 'new_m_final_norm_gain': 'new_m', 'new_v_norm1_gain': 'new_v', 'new_v_w_in': 'new_v', 'new_v_lru_conv_w': 'new_v', 'new_v_lru_conv_b': 'new_v', 'new_v_lru_gate_a_w': 'new_v', 'new_v_lru_gate_a_b': 'new_v', 'new_v_lru_gate_x_w': 'new_v', 'new_v_lru_gate_x_b': 'new_v', 'new_v_lru_lambda': 'new_v', 'new_v_lru_norm_gain': 'new_v', 'new_v_ret_norm_gain': 'new_v', 'new_v_w_out': 'new_v', 'new_v_norm2_gain': 'new_v', 'new_v_ffn_up_w': 'new_v', 'new_v_ffn_conv_w': 'new_v', 'new_v_ffn_conv_b': 'new_v', 'new_v_ffn_down_w': 'new_v', 'new_v_final_norm_gain': 'new_v'}


def _forward(args):
    return _fwd_reference(*[args[k] for k in FWD_PARAMS])


def _output_shape():
    def fwd():
        inp = _fwd_setup_inputs(0)
        return _fwd_reference(*[inp[k] for k in FWD_PARAMS])
    out = _jax.eval_shape(fwd)
    return out.shape, out.dtype

N_MICROBATCH = 1
ADAM_LR = 0.001
ADAM_B1 = 0.9
ADAM_B2 = 0.999
ADAM_EPS = 1e-08
ADAM_WD = 0.01
ADAM_STEP = 10
PER_EXAMPLE_BATCH_AXIS = {'x': 0, 'loss_target': 0}
SHARED_INPUTS = []
_WEIGHT_DTYPES = {'norm1_gain': _jnp.float32, 'w_in': _jnp.float32, 'lru_conv_w': _jnp.float32, 'lru_conv_b': _jnp.float32, 'lru_gate_a_w': _jnp.float32, 'lru_gate_a_b': _jnp.float32, 'lru_gate_x_w': _jnp.float32, 'lru_gate_x_b': _jnp.float32, 'lru_lambda': _jnp.float32, 'lru_norm_gain': _jnp.float32, 'ret_norm_gain': _jnp.float32, 'w_out': _jnp.float32, 'norm2_gain': _jnp.float32, 'ffn_up_w': _jnp.float32, 'ffn_conv_w': _jnp.float32, 'ffn_conv_b': _jnp.float32, 'ffn_down_w': _jnp.float32, 'final_norm_gain': _jnp.float32}
MOMENT_SCALE = {'norm1_gain': 2.790530e-01, 'w_in': 1.697877e-01, 'lru_conv_w': 2.585225e-01, 'lru_conv_b': 1.285190e+00, 'lru_gate_a_w': 5.993510e-02, 'lru_gate_a_b': 6.681176e-02, 'lru_gate_x_w': 1.068261e-01, 'lru_gate_x_b': 8.400132e-02, 'lru_lambda': 1.280045e-01, 'lru_norm_gain': 2.199758e-01, 'ret_norm_gain': 1.318100e-01, 'w_out': 1.864120e-01, 'norm2_gain': 1.640023e-01, 'ffn_up_w': 6.851954e-02, 'ffn_conv_w': 6.789225e-02, 'ffn_conv_b': 6.964249e-02, 'ffn_down_w': 1.191107e-01, 'final_norm_gain': 6.416168e+01}


def _to_microbatches(a, axis):
    t = _jnp.moveaxis(a, axis, 0)
    t = t.reshape((N_MICROBATCH, t.shape[0] // N_MICROBATCH) + t.shape[1:])
    return _jnp.moveaxis(t, 1, axis + 1)


def setup_inputs(seed: int = 0) -> dict:
    inp = _fwd_setup_inputs(seed)
    key = _jax.random.fold_in(_jax.random.key(seed), 7919)
    shape, _ = _output_shape()
    out = dict(inp)
    out["loss_target"] = _jax.random.normal(_jax.random.fold_in(key, 0), shape, _jnp.float32)
    for i, name in enumerate(TWIN_WEIGHTS):
        w = inp[name].astype(_jnp.float32)
        if MOMENT_SCALE is None:
            s = _jnp.sqrt(_jnp.mean(_jnp.square(w)) + 1e-30)
        else:
            s = MOMENT_SCALE[name]
        km, kv = _jax.random.split(_jax.random.fold_in(key, i + 1))
        out[name] = w
        out["m_" + name] = s * _jax.random.normal(km, w.shape, _jnp.float32)
        out["v_" + name] = (s * s) * _jax.random.uniform(kv, w.shape, _jnp.float32, 0.5, 1.5)
    if N_MICROBATCH > 1:
        for name, axis in PER_EXAMPLE_BATCH_AXIS.items():
            out[name] = _to_microbatches(out[name], axis)
    return {'x': out['x'], 'norm1_gain': out['norm1_gain'], 'w_in': out['w_in'], 'lru_conv_w': out['lru_conv_w'], 'lru_conv_b': out['lru_conv_b'], 'lru_gate_a_w': out['lru_gate_a_w'], 'lru_gate_a_b': out['lru_gate_a_b'], 'lru_gate_x_w': out['lru_gate_x_w'], 'lru_gate_x_b': out['lru_gate_x_b'], 'lru_lambda': out['lru_lambda'], 'lru_norm_gain': out['lru_norm_gain'], 'ret_norm_gain': out['ret_norm_gain'], 'w_out': out['w_out'], 'norm2_gain': out['norm2_gain'], 'ffn_up_w': out['ffn_up_w'], 'ffn_conv_w': out['ffn_conv_w'], 'ffn_conv_b': out['ffn_conv_b'], 'ffn_down_w': out['ffn_down_w'], 'final_norm_gain': out['final_norm_gain'], 'loss_target': out['loss_target'], 'm_norm1_gain': out['m_norm1_gain'], 'm_w_in': out['m_w_in'], 'm_lru_conv_w': out['m_lru_conv_w'], 'm_lru_conv_b': out['m_lru_conv_b'], 'm_lru_gate_a_w': out['m_lru_gate_a_w'], 'm_lru_gate_a_b': out['m_lru_gate_a_b'], 'm_lru_gate_x_w': out['m_lru_gate_x_w'], 'm_lru_gate_x_b': out['m_lru_gate_x_b'], 'm_lru_lambda': out['m_lru_lambda'], 'm_lru_norm_gain': out['m_lru_norm_gain'], 'm_ret_norm_gain': out['m_ret_norm_gain'], 'm_w_out': out['m_w_out'], 'm_norm2_gain': out['m_norm2_gain'], 'm_ffn_up_w': out['m_ffn_up_w'], 'm_ffn_conv_w': out['m_ffn_conv_w'], 'm_ffn_conv_b': out['m_ffn_conv_b'], 'm_ffn_down_w': out['m_ffn_down_w'], 'm_final_norm_gain': out['m_final_norm_gain'], 'v_norm1_gain': out['v_norm1_gain'], 'v_w_in': out['v_w_in'], 'v_lru_conv_w': out['v_lru_conv_w'], 'v_lru_conv_b': out['v_lru_conv_b'], 'v_lru_gate_a_w': out['v_lru_gate_a_w'], 'v_lru_gate_a_b': out['v_lru_gate_a_b'], 'v_lru_gate_x_w': out['v_lru_gate_x_w'], 'v_lru_gate_x_b': out['v_lru_gate_x_b'], 'v_lru_lambda': out['v_lru_lambda'], 'v_lru_norm_gain': out['v_lru_norm_gain'], 'v_ret_norm_gain': out['v_ret_norm_gain'], 'v_w_out': out['v_w_out'], 'v_norm2_gain': out['v_norm2_gain'], 'v_ffn_up_w': out['v_ffn_up_w'], 'v_ffn_conv_w': out['v_ffn_conv_w'], 'v_ffn_conv_b': out['v_ffn_conv_b'], 'v_ffn_down_w': out['v_ffn_down_w'], 'v_final_norm_gain': out['v_final_norm_gain']}


def _loss(weights, diff, rest, loss_target):
    with _jax.named_scope("forward"):
        args = {**rest, TWIN_DIFF_INPUT: diff, **{k: w.astype(_WEIGHT_DTYPES[k]) for k, w in weights.items()}}
        y = _forward(args)
    with _jax.named_scope("loss_head"):
        err = _jnp.square(y.astype(_jnp.float32) - loss_target)
        return 0.5 * _jnp.sum(_jnp.mean(err, axis=-1)) if err.ndim else 0.5 * err


def _adamw(w, g, m, v):
    m = ADAM_B1 * m + (1.0 - ADAM_B1) * g
    v = ADAM_B2 * v + (1.0 - ADAM_B2) * _jnp.square(g)
    m_hat = m / (1.0 - ADAM_B1 ** ADAM_STEP)
    v_hat = v / (1.0 - ADAM_B2 ** ADAM_STEP)
    delta = -ADAM_LR * (m_hat / (_jnp.sqrt(v_hat) + ADAM_EPS) + ADAM_WD * w)
    return delta, m, v


def reference(x, norm1_gain, w_in, lru_conv_w, lru_conv_b, lru_gate_a_w, lru_gate_a_b, lru_gate_x_w, lru_gate_x_b, lru_lambda, lru_norm_gain, ret_norm_gain, w_out, norm2_gain, ffn_up_w, ffn_conv_w, ffn_conv_b, ffn_down_w, final_norm_gain, loss_target, m_norm1_gain, m_w_in, m_lru_conv_w, m_lru_conv_b, m_lru_gate_a_w, m_lru_gate_a_b, m_lru_gate_x_w, m_lru_gate_x_b, m_lru_lambda, m_lru_norm_gain, m_ret_norm_gain, m_w_out, m_norm2_gain, m_ffn_up_w, m_ffn_conv_w, m_ffn_conv_b, m_ffn_down_w, m_final_norm_gain, v_norm1_gain, v_w_in, v_lru_conv_w, v_lru_conv_b, v_lru_gate_a_w, v_lru_gate_a_b, v_lru_gate_x_w, v_lru_gate_x_b, v_lru_lambda, v_lru_norm_gain, v_ret_norm_gain, v_w_out, v_norm2_gain, v_ffn_up_w, v_ffn_conv_w, v_ffn_conv_b, v_ffn_down_w, v_final_norm_gain):
    given = dict(x=x, norm1_gain=norm1_gain, w_in=w_in, lru_conv_w=lru_conv_w, lru_conv_b=lru_conv_b, lru_gate_a_w=lru_gate_a_w, lru_gate_a_b=lru_gate_a_b, lru_gate_x_w=lru_gate_x_w, lru_gate_x_b=lru_gate_x_b, lru_lambda=lru_lambda, lru_norm_gain=lru_norm_gain, ret_norm_gain=ret_norm_gain, w_out=w_out, norm2_gain=norm2_gain, ffn_up_w=ffn_up_w, ffn_conv_w=ffn_conv_w, ffn_conv_b=ffn_conv_b, ffn_down_w=ffn_down_w, final_norm_gain=final_norm_gain, loss_target=loss_target, m_norm1_gain=m_norm1_gain, m_w_in=m_w_in, m_lru_conv_w=m_lru_conv_w, m_lru_conv_b=m_lru_conv_b, m_lru_gate_a_w=m_lru_gate_a_w, m_lru_gate_a_b=m_lru_gate_a_b, m_lru_gate_x_w=m_lru_gate_x_w, m_lru_gate_x_b=m_lru_gate_x_b, m_lru_lambda=m_lru_lambda, m_lru_norm_gain=m_lru_norm_gain, m_ret_norm_gain=m_ret_norm_gain, m_w_out=m_w_out, m_norm2_gain=m_norm2_gain, m_ffn_up_w=m_ffn_up_w, m_ffn_conv_w=m_ffn_conv_w, m_ffn_conv_b=m_ffn_conv_b, m_ffn_down_w=m_ffn_down_w, m_final_norm_gain=m_final_norm_gain, v_norm1_gain=v_norm1_gain, v_w_in=v_w_in, v_lru_conv_w=v_lru_conv_w, v_lru_conv_b=v_lru_conv_b, v_lru_gate_a_w=v_lru_gate_a_w, v_lru_gate_a_b=v_lru_gate_a_b, v_lru_gate_x_w=v_lru_gate_x_w, v_lru_gate_x_b=v_lru_gate_x_b, v_lru_lambda=v_lru_lambda, v_lru_norm_gain=v_lru_norm_gain, v_ret_norm_gain=v_ret_norm_gain, v_w_out=v_w_out, v_norm2_gain=v_norm2_gain, v_ffn_up_w=v_ffn_up_w, v_ffn_conv_w=v_ffn_conv_w, v_ffn_conv_b=v_ffn_conv_b, v_ffn_down_w=v_ffn_down_w, v_final_norm_gain=v_final_norm_gain)
    weights = {n: given[n] for n in TWIN_WEIGHTS}
    shared = {n: given[n] for n in SHARED_INPUTS}
    per_example = {n: given[n] for n in ['x']}
    grad_fn = _jax.value_and_grad(_loss, argnums=(0, 1))

    def one_microbatch(ex, loss_target):
        ex = dict(ex)
        diff = ex.pop(TWIN_DIFF_INPUT)
        return grad_fn(weights, diff, {**shared, **ex}, loss_target)

    if N_MICROBATCH == 1:
        loss, (grad_w, grad_x) = one_microbatch(per_example, given["loss_target"])
    else:
        def body(carry, xs):
            loss_sum, grad_sum = carry
            l_k, (gw_k, gx_k) = one_microbatch(xs[0], xs[1])
            with _jax.named_scope("update"):
                return (loss_sum + l_k, _jax.tree.map(_jnp.add, grad_sum, gw_k)), gx_k

        init = (_jnp.zeros((), _jnp.float32), _jax.tree.map(_jnp.zeros_like, weights))
        (loss, grad_w), grad_x = _jax.lax.scan(body, init, (per_example, given["loss_target"]))
    with _jax.named_scope("update"):
        delta_w, new_m, new_v = {}, {}, {}
        for n in TWIN_WEIGHTS:
            delta_w[n], new_m[n], new_v[n] = _adamw(weights[n], grad_w[n], given["m_" + n], given["v_" + n])
    return (loss, grad_x, *[grad_w[n] for n in TWIN_WEIGHTS], *[delta_w[n] for n in TWIN_WEIGHTS],
            *[new_m[n] for n in TWIN_WEIGHTS], *[new_v[n] for n in TWIN_WEIGHTS])
```

```python
import functools
import math

import numpy as np
import jax
import jax.numpy as jnp
from jax import lax
from jax.experimental import pallas as pl
from jax.experimental.pallas import tpu as pltpu

F32 = jnp.float32
BF16 = jnp.bfloat16
MXU_DTYPE = jnp.bfloat16

N_DEV = 8
D_MODEL = 1024
D_LRU = 512
LRU_HEADS = 8
LRU_HEAD_DIM = 64
LRU_CONV = 4
LRU_C = 8.0
D_RET = 512
RET_HEADS = 4
RET_HEAD_DIM = 128
RET_CHUNK = 128
ROPE_BASE = 10000.0
D_IN = 3072
D_FF = 3072
FFN_CONV = 3
NORM_EPS = 1e-6

ADAM_LR = 0.001
ADAM_B1 = 0.9
ADAM_B2 = 0.999
ADAM_EPS = 1e-08
ADAM_WD = 0.01
ADAM_STEP = 10

SUBLANES = 8
VMEM_LIMIT = 48 * 1024 * 1024


def _params(*sem):
    return pltpu.CompilerParams(dimension_semantics=sem, vmem_limit_bytes=VMEM_LIMIT)


def _dot(a, b):
    return jnp.dot(a.astype(MXU_DTYPE), b.astype(MXU_DTYPE), preferred_element_type=F32)


def _dot_nt(a, b):
    return lax.dot_general(a.astype(MXU_DTYPE), b.astype(MXU_DTYPE), (((1,), (1,)), ((), ())),
                           preferred_element_type=F32)


def _dot_tn(a, b):
    return lax.dot_general(a.astype(MXU_DTYPE), b.astype(MXU_DTYPE), (((0,), (0,)), ((), ())),
                           preferred_element_type=F32)


def _sigmoid(x):
    return 1.0 / (1.0 + jnp.exp(-x))


_GELU_C = math.sqrt(2.0 / math.pi)


def _gelu_parts(x):
    x2 = x * x
    t = jnp.tanh(_GELU_C * (x + 0.044715 * (x2 * x)))
    cdf = 0.5 * (1.0 + t)
    g = x * cdf
    dg = cdf + 0.5 * x * (1.0 - t * t) * (_GELU_C * (1.0 + 3.0 * 0.044715 * x2))
    return g, dg


def _gelu(x):
    t = jnp.tanh(_GELU_C * (x + 0.044715 * (x * x * x)))
    return x * (0.5 * (1.0 + t))


def _neg_expm1(x):
    series = x * (1.0 + x * (1.0 / 2.0) * (1.0 + x * (1.0 / 3.0) * (1.0 + x * (1.0 / 4.0) * (
        1.0 + x * (1.0 / 5.0) * (1.0 + x * (1.0 / 6.0) * (1.0 + x * (1.0 / 7.0)))))))
    return jnp.where(x > -0.25, -series, 1.0 - jnp.exp(x))


def _softplus(x):
    return jnp.maximum(x, 0.0) + jnp.log1p(jnp.exp(-jnp.abs(x)))


def _bcast_row(x, r, rows=SUBLANES):
    return jnp.broadcast_to(x[r:r + 1, :], (rows, x.shape[1]))


def _colsum8(x):
    return jnp.broadcast_to(jnp.sum(x, axis=0, keepdims=True), (SUBLANES, x.shape[1]))


def _shift_down(prev8, tile, s):
    if s == 0:
        return tile
    ext = jnp.concatenate([prev8, tile], axis=0)
    return pltpu.roll(ext, s, 0)[SUBLANES:, :]


def _shift_up(tile, next8, s):
    if s == 0:
        return tile
    ext = jnp.concatenate([tile, next8], axis=0)
    n = ext.shape[0]
    return pltpu.roll(ext, n - s, 0)[:tile.shape[0], :]


def _group_scan(a, b, reverse):
    n = a.shape[0]
    row = lax.broadcasted_iota(jnp.int32, a.shape, 0) & (SUBLANES - 1)
    for s in (1, 2, 4):
        shift = (n - s) if reverse else s
        a_sh = pltpu.roll(a, shift, 0)
        b_sh = pltpu.roll(b, shift, 0)
        m = (row <= SUBLANES - 1 - s) if reverse else (row >= s)
        b = jnp.where(m, a * b_sh + b, b)
        a = jnp.where(m, a * a_sh, a)
    return a, b


def _carry_scan(a_ref, b_ref, out_ref, carry0, reverse):
    n_groups = a_ref.shape[0] // SUBLANES

    def body(i, carry):
        g = (n_groups - 1 - i) if reverse else i
        r0 = pl.multiple_of(g * SUBLANES, SUBLANES)
        hg = a_ref[pl.ds(r0, SUBLANES), :] * carry + b_ref[pl.ds(r0, SUBLANES), :]
        out_ref[pl.ds(r0, SUBLANES), :] = hg
        return _bcast_row(hg, 0 if reverse else SUBLANES - 1)

    return lax.fori_loop(0, n_groups, body, carry0)


def _rms_fwd(h, gain):
    rstd = lax.rsqrt(jnp.mean(h * h, axis=-1, keepdims=True) + NORM_EPS)
    n = h * rstd
    return n, rstd, n * gain


def _rms_bwd(dy, n, rstd, gain):
    dn = dy * gain
    dh = rstd * (dn - n * jnp.mean(dn * n, axis=-1, keepdims=True))
    return dh, _colsum8(dy * n)


def _halo_map(tile_rows, col):
    per = tile_rows // SUBLANES
    return lambda i: (jnp.maximum(i * per - 1, 0), col)


def _mm(a, b, name, out_dtype, nt, tm=512, tn=512):
    m, k = a.shape
    n = b.shape[0] if nt else b.shape[1]
    tm, tn = min(tm, m), min(tn, n)

    def body(a_ref, b_ref, o_ref):
        f = _dot_nt if nt else _dot
        o_ref[...] = f(a_ref[...], b_ref[...]).astype(o_ref.dtype)

    b_spec = pl.BlockSpec((tn, k), lambda j, i: (j, 0)) if nt else pl.BlockSpec((k, tn), lambda j, i: (0, j))
    return pl.pallas_call(
        body, name=name, grid=(n // tn, m // tm),
        in_specs=[pl.BlockSpec((tm, k), lambda j, i: (i, 0)), b_spec],
        out_specs=pl.BlockSpec((tm, tn), lambda j, i: (i, j)),
        out_shape=jax.ShapeDtypeStruct((m, n), out_dtype),
        compiler_params=_params("arbitrary", "arbitrary"),
    )(a, b)


def _mm_tn(a, b, name, tn=512, tk=512):
    t, m = a.shape
    n = b.shape[1]
    tm, tn, tk = min(1024, m), min(tn, n), min(tk, t)
    nk = t // tk

    def body(a_ref, b_ref, o_ref):
        @pl.when(pl.program_id(2) == 0)
        def _():
            o_ref[...] = jnp.zeros_like(o_ref)
        o_ref[...] += _dot_tn(a_ref[...], b_ref[...])

    return pl.pallas_call(
        body, name=name, grid=(m // tm, n // tn, nk),
        in_specs=[pl.BlockSpec((tk, tm), lambda i, j, k: (k, i)), pl.BlockSpec((tk, tn), lambda i, j, k: (k, j))],
        out_specs=pl.BlockSpec((tm, tn), lambda i, j, k: (i, j)),
        out_shape=jax.ShapeDtypeStruct((m, n), F32),
        compiler_params=_params("arbitrary", "arbitrary", "arbitrary"),
    )(a, b)


def _inproj_fwd(x, g1, w_in):
    t = x.shape[0]
    tm = min(256, t)

    def body(x_ref, g_ref, w_ref, u_ref, p_ref):
        _, _, u = _rms_fwd(x_ref[...], g_ref[...])
        u = u.astype(MXU_DTYPE)
        u_ref[...] = u
        p_ref[...] = _dot(u, w_ref[...])

    return pl.pallas_call(
        body, name="inproj_fwd", grid=(t // tm,),
        in_specs=[pl.BlockSpec((tm, D_MODEL), lambda i: (i, 0)), pl.BlockSpec((1, D_MODEL), lambda i: (0, 0)),
                  pl.BlockSpec((D_MODEL, D_IN), lambda i: (0, 0))],
        out_specs=[pl.BlockSpec((tm, D_MODEL), lambda i: (i, 0)), pl.BlockSpec((tm, D_IN), lambda i: (i, 0))],
        out_shape=[jax.ShapeDtypeStruct((t, D_MODEL), MXU_DTYPE), jax.ShapeDtypeStruct((t, D_IN), F32)],
        compiler_params=_params("arbitrary"),
    )(x, g1, w_in)


def _lru_gates(xc, wa, ba, wx, bx, sp):
    r = _sigmoid(_dot(xc, wa) + ba)
    ig = _sigmoid(_dot(xc, wx) + bx)
    log_a = (-LRU_C) * r * sp
    a = jnp.exp(log_a)
    m = jnp.sqrt(_neg_expm1(2.0 * log_a))
    return r, ig, a, m


def _lru_fwd(proj, conv_w, conv_b, wa, ba, wx, bx, lam, gain):
    t = proj.shape[0]
    tm = min(256, t)
    c = D_LRU

    def body(x_ref, xh_ref, g_ref, cw_ref, cb_ref, wa_ref, ba_ref, wx_ref, bx_ref, lam_ref, gain_ref,
             xc_ref, h_ref, y_ref, a_scr, b_scr, carry):
        i = pl.program_id(0)

        @pl.when(i == 0)
        def _():
            carry[...] = jnp.zeros_like(carry)

        x = x_ref[...]
        prev = jnp.where(i == 0, 0.0, xh_ref[...])
        cw = cw_ref[...]
        xc = cb_ref[...] + cw[LRU_CONV - 1:LRU_CONV, :] * x
        for k in range(LRU_CONV - 1):
            xc = xc + cw[k:k + 1, :] * _shift_down(prev, x, LRU_CONV - 1 - k)
        xc_ref[...] = xc
        sp = _softplus(-lam_ref[...])
        _, ig, a, m = _lru_gates(xc, wa_ref[...], ba_ref[...], wx_ref[...], bx_ref[...], sp)
        ga, gb = _group_scan(a, m * (ig * xc), reverse=False)
        a_scr[...] = ga
        b_scr[...] = gb
        carry[...] = _carry_scan(a_scr, b_scr, h_ref, carry[...], reverse=False)
        z = h_ref[...] * _gelu(g_ref[...])
        _, _, y = _rms_fwd(z, gain_ref[...])
        y_ref[...] = y.astype(y_ref.dtype)

    row = lambda i: (i, 0)
    full = lambda i: (0, 0)
    vec = pl.BlockSpec((1, c), full)
    return pl.pallas_call(
        body, name="lru_fwd", grid=(t // tm,),
        in_specs=[pl.BlockSpec((tm, c), row), pl.BlockSpec((SUBLANES, c), _halo_map(tm, 0)),
                  pl.BlockSpec((tm, c), lambda i: (i, 1)),
                  pl.BlockSpec((LRU_CONV, c), full), vec, pl.BlockSpec((c, c), full), vec,
                  pl.BlockSpec((c, c), full), vec, vec, vec],
        out_specs=[pl.BlockSpec((tm, c), row), pl.BlockSpec((tm, c), row), pl.BlockSpec((tm, c), row)],
        out_shape=[jax.ShapeDtypeStruct((t, c), F32), jax.ShapeDtypeStruct((t, c), F32),
                   jax.ShapeDtypeStruct((t, c), MXU_DTYPE)],
        scratch_shapes=[pltpu.VMEM((tm, c), F32), pltpu.VMEM((tm, c), F32), pltpu.VMEM((SUBLANES, c), F32)],
        compiler_params=_params("arbitrary"),
    )(proj, proj, proj, conv_w, conv_b, wa, ba, wx, bx, lam, gain)


def _ret_consts():
    c = RET_CHUNK
    log_g = jnp.log1p(-jnp.exp2(-5.0 - jnp.arange(RET_HEADS, dtype=F32)))
    idx = jnp.arange(c, dtype=F32)
    diff = idx[:, None] - idx[None, :]
    decay = jnp.where(diff[None] >= 0, jnp.exp(jnp.maximum(diff, 0.0)[None] * log_g[:, None, None]), 0.0)
    zeta = jnp.exp((c - 1 - idx)[None, :] * log_g[:, None])
    xi = jnp.exp((idx + 1.0)[None, :] * log_g[:, None])
    spread = lambda v: jnp.repeat(v.T, RET_HEAD_DIM, axis=1)
    log_g_np = np.log1p(-np.exp2(-5.0 - np.arange(RET_HEADS, dtype=np.float32))).astype(np.float32)
    g_chunk = [float(np.exp(np.float32(c) * lg)) for lg in log_g_np]
    return decay, spread(xi), spread(zeta), g_chunk


def _rope_tables(t):
    pos = jnp.arange(t, dtype=F32)
    inv_freq = ROPE_BASE ** (-jnp.arange(0, RET_HEAD_DIM, 2, dtype=F32) / RET_HEAD_DIM)
    ang = pos[:, None] * inv_freq[None, :]
    cos, sin = jnp.cos(ang), jnp.sin(ang)
    return jnp.concatenate([cos, cos], axis=-1), jnp.concatenate([-sin, sin], axis=-1)


def _rope(x, cos2, sin_signed):
    return x * cos2 + pltpu.roll(x, RET_HEAD_DIM // 2, 1) * sin_signed


def _rope_bwd(d, cos2, sin_signed):
    return d * cos2 + pltpu.roll(d * sin_signed, RET_HEAD_DIM // 2, 1)


RET_SCALE = RET_HEAD_DIM ** -0.5


def _ret_fwd(proj, cos2, sin_signed, gain):
    t = proj.shape[0]
    c, d, nh = RET_CHUNK, RET_HEAD_DIM, RET_HEADS
    n_chunks = t // c
    decay, xi, zeta, g_chunk = _ret_consts()

    def body(qk_ref, vg_ref, cos_ref, sin_ref, dec_ref, xi_ref, zeta_ref, gain_ref, o_ref, y_ref, st_ref, state):
        @pl.when(pl.program_id(0) == 0)
        def _():
            state[...] = jnp.zeros_like(state)

        cos2, sin_s = cos_ref[...], sin_ref[...]
        for h in range(nh):
            lo = h * d
            q = _rope(qk_ref[:, lo:lo + d], cos2, sin_s)
            k = _rope(qk_ref[:, D_RET + lo:D_RET + lo + d], cos2, sin_s) * RET_SCALE
            v = vg_ref[:, lo:lo + d]
            g = vg_ref[:, D_RET + lo:D_RET + lo + d]
            s_prev = state[h]
            st_ref[0, h] = s_prev
            scores = _dot_nt(q, k) * dec_ref[h]
            o = _dot(scores, v) + _dot(q * xi_ref[:, lo:lo + d], s_prev)
            state[h] = s_prev * g_chunk[h] + _dot_tn(k * zeta_ref[:, lo:lo + d], v)
            o_ref[:, lo:lo + d] = o
            mu = jnp.mean(o, axis=-1, keepdims=True)
            oc = o - mu
            on = oc * lax.rsqrt(jnp.mean(oc * oc, axis=-1, keepdims=True) + NORM_EPS)
            y_ref[:, lo:lo + d] = (on * gain_ref[:, lo:lo + d] * (g * _sigmoid(g))).astype(y_ref.dtype)

    full2 = lambda i: (0, 0)
    return pl.pallas_call(
        body, name="ret_fwd", grid=(n_chunks,),
        in_specs=[pl.BlockSpec((c, 2 * D_RET), lambda i: (i, 1)), pl.BlockSpec((c, 2 * D_RET), lambda i: (i, 2)),
                  pl.BlockSpec((c, d), lambda i: (i, 0)), pl.BlockSpec((c, d), lambda i: (i, 0)),
                  pl.BlockSpec((nh, c, c), lambda i: (0, 0, 0)), pl.BlockSpec((c, D_RET), full2),
                  pl.BlockSpec((c, D_RET), full2), pl.BlockSpec((1, D_RET), full2)],
        out_specs=[pl.BlockSpec((c, D_RET), lambda i: (i, 0)), pl.BlockSpec((c, D_RET), lambda i: (i, 0)),
                   pl.BlockSpec((1, nh, d, d), lambda i: (i, 0, 0, 0))],
        out_shape=[jax.ShapeDtypeStruct((t, D_RET), F32), jax.ShapeDtypeStruct((t, D_RET), MXU_DTYPE),
                   jax.ShapeDtypeStruct((n_chunks, nh, d, d), F32)],
        scratch_shapes=[pltpu.VMEM((nh, d, d), F32)],
        compiler_params=_params("arbitrary"),
    )(proj, proj, cos2, sin_signed, decay, xi, zeta, gain)


def _outproj_fwd(x, y_lru, y_ret, w_out, g2):
    t = x.shape[0]
    tm = min(512, t)

    def body(x_ref, yl_ref, yr_ref, w_ref, g_ref, h1_ref, u2_ref):
        h1 = x_ref[...] + _dot(yl_ref[...], w_ref[:D_LRU, :]) + _dot(yr_ref[...], w_ref[D_LRU:, :])
        h1_ref[...] = h1
        _, _, u = _rms_fwd(h1, g_ref[...])
        u2_ref[...] = u.astype(u2_ref.dtype)

    row = lambda i: (i, 0)
    return pl.pallas_call(
        body, name="outproj_fwd", grid=(t // tm,),
        in_specs=[pl.BlockSpec((tm, D_MODEL), row), pl.BlockSpec((tm, D_LRU), row), pl.BlockSpec((tm, D_RET), row),
                  pl.BlockSpec((D_MODEL, D_MODEL), lambda i: (0, 0)), pl.BlockSpec((1, D_MODEL), lambda i: (0, 0))],
        out_specs=[pl.BlockSpec((tm, D_MODEL), row), pl.BlockSpec((tm, D_MODEL), row)],
        out_shape=[jax.ShapeDtypeStruct((t, D_MODEL), F32), jax.ShapeDtypeStruct((t, D_MODEL), MXU_DTYPE)],
        compiler_params=_params("arbitrary"),
    )(x, y_lru, y_ret, w_out, g2)


FFN_TN = 768
FFN_NJ = D_FF // FFN_TN


def _ffn_conv(up_ref, halo_ref, cw_ref, cb_ref, first):
    x = up_ref[...]
    prev = jnp.where(first, 0.0, halo_ref[...])
    cw = cw_ref[...]
    y = cb_ref[...] + cw[FFN_CONV - 1:FFN_CONV, :] * x
    for k in range(FFN_CONV - 1):
        y = y + cw[k:k + 1, :] * _shift_down(prev, x, FFN_CONV - 1 - k)
    return y, x, prev


def _ffn_fwd(up, conv_w, conv_b, w_down, h1, gf, target):
    t = up.shape[0]
    tm = min(256, t)
    tn, nj = FFN_TN, FFN_NJ

    def body(ua_ref, uah_ref, uv_ref, uvh_ref, cwa_ref, cwv_ref, cba_ref, cbv_ref, wd_ref, h1_ref, gf_ref, tg_ref,
             act_ref, dh_ref, dhb_ref, dgf_ref, loss_ref, acc):
        i, j = pl.program_id(0), pl.program_id(1)

        @pl.when((i == 0) & (j == 0))
        def _():
            dgf_ref[...] = jnp.zeros_like(dgf_ref)
            loss_ref[...] = jnp.zeros_like(loss_ref)

        @pl.when(j == 0)
        def _():
            acc[...] = jnp.zeros_like(acc)

        a, _, _ = _ffn_conv(ua_ref, uah_ref, cwa_ref, cba_ref, i == 0)
        v, _, _ = _ffn_conv(uv_ref, uvh_ref, cwv_ref, cbv_ref, i == 0)
        act = (_gelu(a) * v).astype(act_ref.dtype)
        act_ref[...] = act
        acc[...] += _dot(act, wd_ref[...])

        @pl.when(j == nj - 1)
        def _():
            n, rstd, y = _rms_fwd(h1_ref[...] + acc[...], gf_ref[...])
            err = y - tg_ref[...]
            loss_ref[...] += (0.5 / D_MODEL) * jnp.sum(err * err)
            dh, dgf = _rms_bwd(err * (1.0 / D_MODEL), n, rstd, gf_ref[...])
            dgf_ref[...] += dgf
            dh_ref[...] = dh
            dhb_ref[...] = dh.astype(dhb_ref.dtype)

    per = tm // SUBLANES
    halo = lambda off: (lambda i, j: (jnp.maximum(i * per - 1, 0), j + off))
    row = lambda i, j: (i, 0)
    const = lambda i, j: (0, 0)
    return pl.pallas_call(
        body, name="ffn_fwd", grid=(t // tm, nj),
        in_specs=[pl.BlockSpec((tm, tn), lambda i, j: (i, j)), pl.BlockSpec((SUBLANES, tn), halo(0)),
                  pl.BlockSpec((tm, tn), lambda i, j: (i, j + nj)), pl.BlockSpec((SUBLANES, tn), halo(nj)),
                  pl.BlockSpec((FFN_CONV, tn), lambda i, j: (0, j)), pl.BlockSpec((FFN_CONV, tn), lambda i, j: (0, j + nj)),
                  pl.BlockSpec((1, tn), lambda i, j: (0, j)), pl.BlockSpec((1, tn), lambda i, j: (0, j + nj)),
                  pl.BlockSpec((tn, D_MODEL), lambda i, j: (j, 0)),
                  pl.BlockSpec((tm, D_MODEL), row), pl.BlockSpec((1, D_MODEL), const), pl.BlockSpec((tm, D_MODEL), row)],
        out_specs=[pl.BlockSpec((tm, tn), lambda i, j: (i, j)), pl.BlockSpec((tm, D_MODEL), row),
                   pl.BlockSpec((tm, D_MODEL), row), pl.BlockSpec((SUBLANES, D_MODEL), const),
                   pl.BlockSpec((SUBLANES, 128), const)],
        out_shape=[jax.ShapeDtypeStruct((t, D_FF), MXU_DTYPE), jax.ShapeDtypeStruct((t, D_MODEL), F32),
                   jax.ShapeDtypeStruct((t, D_MODEL), MXU_DTYPE), jax.ShapeDtypeStruct((SUBLANES, D_MODEL), F32),
                   jax.ShapeDtypeStruct((SUBLANES, 128), F32)],
        scratch_shapes=[pltpu.VMEM((tm, D_MODEL), F32)],
        compiler_params=_params("arbitrary", "arbitrary"),
    )(up, up, up, up, conv_w, conv_w, conv_b, conv_b, w_down, h1, gf, target)


FFN_ACC_ROWS = SUBLANES * (FFN_CONV + 1)


def _ffn_bwd(dh2_b, w_down, up, conv_w, conv_b):
    t = up.shape[0]
    tm = min(256, t)
    tn, nj = FFN_TN, FFN_NJ
    ni = t // tm

    def conv_bwd(dy, x, prev, cw, acc_ref, carry_ref, dup_ref):
        nxt = carry_ref[...]
        carry_ref[...] = dy[:SUBLANES, :]
        dx = cw[FFN_CONV - 1:FFN_CONV, :] * dy
        for k in range(FFN_CONV - 1):
            dx = dx + cw[k:k + 1, :] * _shift_up(dy, nxt, FFN_CONV - 1 - k)
        dup_ref[...] = dx.astype(dup_ref.dtype)
        for k in range(FFN_CONV):
            acc_ref[k * SUBLANES:(k + 1) * SUBLANES, :] += _colsum8(dy * _shift_down(prev, x, FFN_CONV - 1 - k))
        acc_ref[FFN_CONV * SUBLANES:, :] += _colsum8(dy)

    def body(dh_ref, wd_ref, ua_ref, uah_ref, uv_ref, uvh_ref, cwa_ref, cwv_ref, cba_ref, cbv_ref,
             dua_ref, duv_ref, acca_ref, accv_ref, carry_a, carry_v):
        i = pl.program_id(1)
        r = ni - 1 - i

        @pl.when(i == 0)
        def _():
            acca_ref[...] = jnp.zeros_like(acca_ref)
            accv_ref[...] = jnp.zeros_like(accv_ref)
            carry_a[...] = jnp.zeros_like(carry_a)
            carry_v[...] = jnp.zeros_like(carry_v)

        a, xa, pa = _ffn_conv(ua_ref, uah_ref, cwa_ref, cba_ref, r == 0)
        v, xv, pv = _ffn_conv(uv_ref, uvh_ref, cwv_ref, cbv_ref, r == 0)
        g, dg = _gelu_parts(a)
        dact = _dot_nt(dh_ref[...], wd_ref[...])
        conv_bwd(dact * v * dg, xa, pa, cwa_ref[...], acca_ref, carry_a, dua_ref)
        conv_bwd(dact * g, xv, pv, cwv_ref[...], accv_ref, carry_v, duv_ref)

    per = tm // SUBLANES
    rev = lambda off: (lambda j, i: (ni - 1 - i, j + off))
    halo = lambda off: (lambda j, i: (jnp.maximum((ni - 1 - i) * per - 1, 0), j + off))
    colj = lambda off: (lambda j, i: (0, j + off))
    return pl.pallas_call(
        body, name="ffn_bwd", grid=(nj, ni),
        in_specs=[pl.BlockSpec((tm, D_MODEL), lambda j, i: (ni - 1 - i, 0)), pl.BlockSpec((tn, D_MODEL), lambda j, i: (j, 0)),
                  pl.BlockSpec((tm, tn), rev(0)), pl.BlockSpec((SUBLANES, tn), halo(0)),
                  pl.BlockSpec((tm, tn), rev(nj)), pl.BlockSpec((SUBLANES, tn), halo(nj)),
                  pl.BlockSpec((FFN_CONV, tn), colj(0)), pl.BlockSpec((FFN_CONV, tn), colj(nj)),
                  pl.BlockSpec((1, tn), colj(0)), pl.BlockSpec((1, tn), colj(nj))],
        out_specs=[pl.BlockSpec((tm, tn), rev(0)), pl.BlockSpec((tm, tn), rev(0)),
                   pl.BlockSpec((FFN_ACC_ROWS, tn), colj(0)), pl.BlockSpec((FFN_ACC_ROWS, tn), colj(0))],
        out_shape=[jax.ShapeDtypeStruct((t, D_FF), MXU_DTYPE), jax.ShapeDtypeStruct((t, D_FF), MXU_DTYPE),
                   jax.ShapeDtypeStruct((FFN_ACC_ROWS, D_FF), F32), jax.ShapeDtypeStruct((FFN_ACC_ROWS, D_FF), F32)],
        scratch_shapes=[pltpu.VMEM((SUBLANES, tn), F32), pltpu.VMEM((SUBLANES, tn), F32)],
        compiler_params=_params("arbitrary", "arbitrary"),
    )(dh2_b, w_down, up, up, up, up, conv_w, conv_w, conv_b, conv_b)


def _norm_bwd_matmul(parts, w, col_offsets, h, gain, d_res, name, tk):
    t = h.shape[0]
    tm = min(512, t)
    steps = []
    for p, a in enumerate(parts):
        for kb in range(a.shape[1] // tk):
            steps.append((p, kb, col_offsets[p] // tk + kb))
    nk = len(steps)
    first_step = [min(s for s, st in enumerate(steps) if st[0] == p) for p in range(len(parts))]
    n_parts = len(parts)

    def body(*refs):
        a_refs = refs[:n_parts]
        w_ref, h_ref, g_ref, dres_ref, dh_ref, dhb_ref, dg_ref, acc = refs[n_parts:]
        i, k = pl.program_id(0), pl.program_id(1)

        @pl.when((i == 0) & (k == 0))
        def _():
            dg_ref[...] = jnp.zeros_like(dg_ref)

        @pl.when(k == 0)
        def _():
            acc[...] = jnp.zeros_like(acc)

        for p in range(n_parts):
            lo = first_step[p]
            hi = lo + parts[p].shape[1] // tk

            @pl.when((k >= lo) & (k < hi))
            def _(p=p):
                acc[...] += _dot_nt(a_refs[p][...], w_ref[...])

        @pl.when(k == nk - 1)
        def _():
            n, rstd, _ = _rms_fwd(h_ref[...], g_ref[...])
            dh, dg = _rms_bwd(acc[...], n, rstd, g_ref[...])
            dh = dh + dres_ref[...]
            dg_ref[...] += dg
            dh_ref[...] = dh
            dhb_ref[...] = dh.astype(dhb_ref.dtype)

    def part_map(p):
        lo = first_step[p]
        nb = parts[p].shape[1] // tk
        return lambda i, k: (i, jnp.clip(k - lo, 0, nb - 1))

    assert [st[2] for st in steps] == list(range(nk)), "the parts must face consecutive columns of w"
    w_map = lambda i, k: (0, k)
    row = lambda i, k: (i, 0)
    const = lambda i, k: (0, 0)
    return pl.pallas_call(
        body, name=name, grid=(t // tm, nk),
        in_specs=[pl.BlockSpec((tm, tk), part_map(p)) for p in range(n_parts)] + [
            pl.BlockSpec((D_MODEL, tk), w_map), pl.BlockSpec((tm, D_MODEL), row), pl.BlockSpec((1, D_MODEL), const),
            pl.BlockSpec((tm, D_MODEL), row)],
        out_specs=[pl.BlockSpec((tm, D_MODEL), row), pl.BlockSpec((tm, D_MODEL), row),
                   pl.BlockSpec((SUBLANES, D_MODEL), const)],
        out_shape=[jax.ShapeDtypeStruct((t, D_MODEL), F32), jax.ShapeDtypeStruct((t, D_MODEL), MXU_DTYPE),
                   jax.ShapeDtypeStruct((SUBLANES, D_MODEL), F32)],
        scratch_shapes=[pltpu.VMEM((tm, D_MODEL), F32)],
        compiler_params=_params("arbitrary", "arbitrary"),
    )(*parts, w, h, gain, d_res)


def _ret_bwd(proj, cos2, sin_signed, gain, o, states, dmixed):
    t = proj.shape[0]
    c, d, nh = RET_CHUNK, RET_HEAD_DIM, RET_HEADS
    n_chunks = t // c
    decay, xi, zeta, g_chunk = _ret_consts()

    def body(qk_ref, vg_ref, cos_ref, sin_ref, dec_ref, xi_ref, zeta_ref, gain_ref, o_ref, st_ref, dy_ref,
             dp_ref, dgain_ref, gstate):
        @pl.when(pl.program_id(0) == 0)
        def _():
            gstate[...] = jnp.zeros_like(gstate)
            dgain_ref[...] = jnp.zeros_like(dgain_ref)

        cos2, sin_s = cos_ref[...], sin_ref[...]
        for h in range(nh):
            lo = h * d
            q = _rope(qk_ref[:, lo:lo + d], cos2, sin_s)
            k = _rope(qk_ref[:, D_RET + lo:D_RET + lo + d], cos2, sin_s) * RET_SCALE
            v = vg_ref[:, lo:lo + d]
            g = vg_ref[:, D_RET + lo:D_RET + lo + d]
            gain_h = gain_ref[:, lo:lo + d]
            xi_h, zeta_h, dec = xi_ref[:, lo:lo + d], zeta_ref[:, lo:lo + d], dec_ref[h]
            dy = dy_ref[:, lo:lo + d]
            sg = _sigmoid(g)
            o_h = o_ref[:, lo:lo + d]
            oc = o_h - jnp.mean(o_h, axis=-1, keepdims=True)
            rstd = lax.rsqrt(jnp.mean(oc * oc, axis=-1, keepdims=True) + NORM_EPS)
            on = oc * rstd
            dp_ref[:, 3 * D_RET + lo:3 * D_RET + lo + d] = (
                dy * on * gain_h * (sg * (1.0 + g * (1.0 - sg)))).astype(dp_ref.dtype)
            don_g = dy * (g * sg)
            dgain_ref[:, lo:lo + d] += _colsum8(don_g * on)
            don = don_g * gain_h
            do = rstd * (don - jnp.mean(don, axis=-1, keepdims=True) - on * jnp.mean(don * on, axis=-1, keepdims=True))
            s_prev = st_ref[0, h]
            g_next = gstate[h]
            p = _dot_nt(q, k) * dec
            dpm = _dot_nt(do, v) * dec
            dq = _dot(dpm, k) + _dot_nt(do, s_prev) * xi_h
            dk = _dot_tn(dpm, q) + _dot_nt(v, g_next) * zeta_h
            dv = _dot_tn(p, do) + _dot(k * zeta_h, g_next)
            gstate[h] = g_next * g_chunk[h] + _dot_tn(q * xi_h, do)
            dp_ref[:, lo:lo + d] = _rope_bwd(dq, cos2, sin_s).astype(dp_ref.dtype)
            dp_ref[:, D_RET + lo:D_RET + lo + d] = _rope_bwd(dk * RET_SCALE, cos2, sin_s).astype(dp_ref.dtype)
            dp_ref[:, 2 * D_RET + lo:2 * D_RET + lo + d] = dv.astype(dp_ref.dtype)

    rev = lambda col: (lambda i: (n_chunks - 1 - i, col))
    full2 = lambda i: (0, 0)
    return pl.pallas_call(
        body, name="ret_bwd", grid=(n_chunks,),
        in_specs=[pl.BlockSpec((c, 2 * D_RET), rev(1)), pl.BlockSpec((c, 2 * D_RET), rev(2)),
                  pl.BlockSpec((c, d), rev(0)), pl.BlockSpec((c, d), rev(0)),
                  pl.BlockSpec((nh, c, c), lambda i: (0, 0, 0)), pl.BlockSpec((c, D_RET), full2),
                  pl.BlockSpec((c, D_RET), full2), pl.BlockSpec((1, D_RET), full2),
                  pl.BlockSpec((c, D_RET), rev(0)), pl.BlockSpec((1, nh, d, d), lambda i: (n_chunks - 1 - i, 0, 0, 0)),
                  pl.BlockSpec((c, D_RET), rev(1))],
        out_specs=[pl.BlockSpec((c, 4 * D_RET), rev(0)), pl.BlockSpec((SUBLANES, D_RET), full2)],
        out_shape=[jax.ShapeDtypeStruct((t, 4 * D_RET), MXU_DTYPE), jax.ShapeDtypeStruct((SUBLANES, D_RET), F32)],
        scratch_shapes=[pltpu.VMEM((nh, d, d), F32)],
        compiler_params=_params("arbitrary"),
    )(proj, proj, cos2, sin_signed, decay, xi, zeta, gain, o, states, dmixed)


LRU_ACC = {"conv_w": 0, "conv_b": LRU_CONV, "gate_a_b": LRU_CONV + 1, "gate_x_b": LRU_CONV + 2,
           "lambda": LRU_CONV + 3, "norm_gain": LRU_CONV + 4}
LRU_ACC_ROWS = SUBLANES * (LRU_CONV + 5)


def _lru_bwd(proj, xc_all, h_all, dmixed, conv_w, wa, ba, wx, bx, lam, gain):
    t = proj.shape[0]
    tm = min(256, t)
    c = D_LRU
    ni = t // tm

    def body(x_ref, xh_ref, g_ref, xc_ref, h_ref, hh_ref, dy_ref, cw_ref, wa_ref, ba_ref, wx_ref, bx_ref, lam_ref,
             gain_ref, dp_ref, acc_ref, dwa_ref, dwx_ref, a_scr, b_scr, mu_scr, carry_mu, carry_dxc):
        i = pl.program_id(0)
        r = ni - 1 - i

        @pl.when(i == 0)
        def _():
            acc_ref[...] = jnp.zeros_like(acc_ref)
            dwa_ref[...] = jnp.zeros_like(dwa_ref)
            dwx_ref[...] = jnp.zeros_like(dwx_ref)
            carry_mu[...] = jnp.zeros_like(carry_mu)
            carry_dxc[...] = jnp.zeros_like(carry_dxc)

        def add(name, val, k=0):
            lo = (LRU_ACC[name] + k) * SUBLANES
            acc_ref[lo:lo + SUBLANES, :] += _colsum8(val)

        xc, h = xc_ref[...], h_ref[...]
        lam_v = lam_ref[...]
        sp = _softplus(-lam_v)
        rg, ig, a, m = _lru_gates(xc, wa_ref[...], ba_ref[...], wx_ref[...], bx_ref[...], sp)
        gl, dgl = _gelu_parts(g_ref[...])
        zn, rstd, _ = _rms_fwd(h * gl, gain_ref[...])
        dy = dy_ref[...]
        dz, dgain = _rms_bwd(dy, zn, rstd, gain_ref[...])
        lo = LRU_ACC["norm_gain"] * SUBLANES
        acc_ref[lo:lo + SUBLANES, :] += dgain
        dp_ref[:, c:] = (dz * h * dgl).astype(dp_ref.dtype)
        dh = dz * gl
        ga, gb = _group_scan(a, a * dh, reverse=True)
        a_scr[...] = ga
        b_scr[...] = gb
        mu_next_tile = carry_mu[...]
        carry_mu[...] = _carry_scan(a_scr, b_scr, mu_scr, mu_next_tile, reverse=True)
        lam_t = dh + _shift_up(mu_scr[...], mu_next_tile, 1)
        h_prev = _shift_down(jnp.where(r == 0, 0.0, hh_ref[...]), h, 1)
        da = lam_t * h_prev
        dig = lam_t * m * xc
        dxc = lam_t * m * ig
        dlog_a = da * a - (lam_t * ig * xc) * (a * a) / m
        dpr = dlog_a * ((-LRU_C) * sp) * rg * (1.0 - rg)
        add("lambda", dlog_a * ((-LRU_C) * rg) * (-_sigmoid(-lam_v)))
        dpi = dig * ig * (1.0 - ig)
        add("gate_a_b", dpr)
        add("gate_x_b", dpi)
        dwa_ref[...] += _dot_tn(xc, dpr)
        dwx_ref[...] += _dot_tn(xc, dpi)
        dxc = dxc + _dot_nt(dpr, wa_ref[...]) + _dot_nt(dpi, wx_ref[...])
        add("conv_b", dxc)
        x = x_ref[...]
        prev = jnp.where(r == 0, 0.0, xh_ref[...])
        cw = cw_ref[...]
        nxt = carry_dxc[...]
        carry_dxc[...] = dxc[:SUBLANES, :]
        dx = cw[LRU_CONV - 1:LRU_CONV, :] * dxc
        for k in range(LRU_CONV - 1):
            dx = dx + cw[k:k + 1, :] * _shift_up(dxc, nxt, LRU_CONV - 1 - k)
        for k in range(LRU_CONV):
            add("conv_w", dxc * _shift_down(prev, x, LRU_CONV - 1 - k), k)
        dp_ref[:, :c] = dx.astype(dp_ref.dtype)

    per = tm // SUBLANES
    rev = lambda col: (lambda i: (ni - 1 - i, col))
    halo = lambda i: (jnp.maximum((ni - 1 - i) * per - 1, 0), 0)
    full = lambda i: (0, 0)
    vec = pl.BlockSpec((1, c), full)
    mat = pl.BlockSpec((c, c), full)
    return pl.pallas_call(
        body, name="lru_bwd", grid=(ni,),
        in_specs=[pl.BlockSpec((tm, c), rev(0)), pl.BlockSpec((SUBLANES, c), halo), pl.BlockSpec((tm, c), rev(1)),
                  pl.BlockSpec((tm, c), rev(0)), pl.BlockSpec((tm, c), rev(0)), pl.BlockSpec((SUBLANES, c), halo),
                  pl.BlockSpec((tm, c), rev(0)), pl.BlockSpec((LRU_CONV, c), full), mat, vec, mat, vec, vec, vec],
        out_specs=[pl.BlockSpec((tm, 2 * c), rev(0)), pl.BlockSpec((LRU_ACC_ROWS, c), full), mat, mat],
        out_shape=[jax.ShapeDtypeStruct((t, 2 * c), MXU_DTYPE), jax.ShapeDtypeStruct((LRU_ACC_ROWS, c), F32),
                   jax.ShapeDtypeStruct((c, c), F32), jax.ShapeDtypeStruct((c, c), F32)],
        scratch_shapes=[pltpu.VMEM((tm, c), F32), pltpu.VMEM((tm, c), F32), pltpu.VMEM((tm, c), F32),
                        pltpu.VMEM((SUBLANES, c), F32), pltpu.VMEM((SUBLANES, c), F32)],
        compiler_params=_params("arbitrary"),
    )(proj, proj, proj, xc_all, h_all, h_all, dmixed, conv_w, wa, ba, wx, bx, lam, gain)


def _block_diag(w):
    nh, d, _ = w.shape
    eye = jnp.eye(nh, dtype=w.dtype)
    return (w[:, :, None, :] * eye[:, None, :, None]).reshape(nh * d, nh * d)


def _diag_blocks(dense, nh):
    d = dense.shape[0] // nh
    blocks = dense.reshape(nh, d, nh, d)
    return jnp.stack([blocks[h, :, h, :] for h in range(nh)], axis=0)


def _local_step(x, target, w):
    t = x.shape[0]
    cos2, sin_signed = _rope_tables(t)
    wa = _block_diag(w["lru_gate_a_w"]).astype(MXU_DTYPE)
    wx = _block_diag(w["lru_gate_x_w"]).astype(MXU_DTYPE)

    u1, proj = _inproj_fwd(x, w["norm1_gain"], w["w_in"])
    xc, h_lru, y_lru = _lru_fwd(proj, w["lru_conv_w"], w["lru_conv_b"], wa, w["lru_gate_a_b"], wx, w["lru_gate_x_b"],
                                w["lru_lambda"], w["lru_norm_gain"])
    o_ret, y_ret, states = _ret_fwd(proj, cos2, sin_signed, w["ret_norm_gain"])
    h1, u2 = _outproj_fwd(x, y_lru, y_ret, w["w_out"], w["norm2_gain"])
    up = _mm(u2, w["ffn_up_w"], "ffn_up_fwd", F32, nt=False, tn=FFN_TN)
    act, dh2, dh2_b, dgf, loss = _ffn_fwd(up, w["ffn_conv_w"], w["ffn_conv_b"], w["ffn_down_w"], h1,
                                          w["final_norm_gain"], target)

    g = {"final_norm_gain": dgf[0]}
    g["ffn_down_w"] = _mm_tn(act, dh2_b, "ffn_down_wgrad")
    dup_a, dup_v, acc_a, acc_v = _ffn_bwd(dh2_b, w["ffn_down_w"], up, w["ffn_conv_w"], w["ffn_conv_b"])
    acc = jnp.concatenate([acc_a, acc_v], axis=1)[::SUBLANES]
    g["ffn_conv_w"], g["ffn_conv_b"] = acc[:FFN_CONV], acc[FFN_CONV:]
    g["ffn_up_w"] = jnp.concatenate([_mm_tn(u2, dup_a, "ffn_up_wgrad_a", tn=FFN_TN),
                                     _mm_tn(u2, dup_v, "ffn_up_wgrad_v", tn=FFN_TN)], axis=1)
    dh1, dh1_b, dg2 = _norm_bwd_matmul([dup_a, dup_v], w["ffn_up_w"], [0, D_FF], h1, w["norm2_gain"], dh2,
                                       "ffn_up_bwd", FFN_TN)
    g["norm2_gain"] = dg2[:1]
    g["w_out"] = jnp.concatenate([_mm_tn(y_lru, dh1_b, "w_out_wgrad_lru"), _mm_tn(y_ret, dh1_b, "w_out_wgrad_ret")],
                                 axis=0)
    dmixed = _mm(dh1_b, w["w_out"], "outproj_bwd", F32, nt=True)
    dp_ret, dgain_ret = _ret_bwd(proj, cos2, sin_signed, w["ret_norm_gain"], o_ret, states, dmixed)
    g["ret_norm_gain"] = dgain_ret[:1]
    dp_lru, lru_acc, dwa, dwx = _lru_bwd(proj, xc, h_lru, dmixed, w["lru_conv_w"], wa, w["lru_gate_a_b"], wx,
                                         w["lru_gate_x_b"], w["lru_lambda"], w["lru_norm_gain"])
    lru_acc = lru_acc[::SUBLANES]
    g["lru_conv_w"] = lru_acc[:LRU_CONV]
    for name in ("conv_b", "gate_a_b", "gate_x_b", "lambda", "norm_gain"):
        g["lru_" + name] = lru_acc[LRU_ACC[name]:LRU_ACC[name] + 1]
    g["lru_gate_a_w"] = _diag_blocks(dwa, LRU_HEADS)
    g["lru_gate_x_w"] = _diag_blocks(dwx, LRU_HEADS)
    g["w_in"] = jnp.concatenate([_mm_tn(u1, dp_lru, "w_in_wgrad_lru"), _mm_tn(u1, dp_ret, "w_in_wgrad_ret")], axis=1)
    grad_x, _, dg1 = _norm_bwd_matmul([dp_lru, dp_ret], w["w_in"], [0, 2 * D_LRU], x, w["norm1_gain"], dh1,
                                      "inproj_bwd", 1024)
    g["norm1_gain"] = dg1[:1]
    return loss[0, 0], grad_x, g


MESH = pl.DeviceIdType.MESH
ANY = pl.BlockSpec(memory_space=pl.ANY)


def _mesh_place():
    x, y, c = lax.axis_index("x"), lax.axis_index("y"), lax.axis_index("c")
    return x, y, c


def _logical(px, py, pc):
    return 4 * px + 2 * py + pc


def _all_gather(shard, name):
    def body(x_ref, out_ref, send_sems, recv_sems, local_sem):
        x, y, c = _mesh_place()
        me, sibling = (x, y, c), (x, y, 1 - c)
        chips = [(1 - x, y), (x, 1 - y), (1 - x, 1 - y)]

        def slot(place):
            return out_ref.at[_logical(*place)]

        def copy(k, block, to, src=None):
            return pltpu.make_async_remote_copy(
                src_ref=slot(block) if src is None else src, dst_ref=slot(block),
                send_sem=send_sems.at[k], recv_sem=recv_sems.at[k], device_id=to, device_id_type=MESH)

        mine = pltpu.make_async_copy(x_ref, slot(me), local_sem)
        mine.start()
        first = [copy(0, me, sibling, src=x_ref)]
        first += [copy(1 + j, me, (*chip, c), src=x_ref) for j, chip in enumerate(chips)]
        for cp in first:
            cp.start()
        passed = [copy(4 + j, (*chip, c), sibling) for j, chip in enumerate(chips)]
        for j, chip in enumerate(chips):
            copy(1 + j, (*chip, c), me).wait_recv()
            passed[j].start()
        copy(0, sibling, me).wait_recv()
        for j, chip in enumerate(chips):
            copy(4 + j, (*chip, 1 - c), me).wait_recv()
        for cp in first + passed:
            cp.wait_send()
        mine.wait()

    return pl.pallas_call(
        body, name=name,
        out_shape=jax.ShapeDtypeStruct((N_DEV,) + shard.shape, shard.dtype),
        in_specs=[ANY], out_specs=ANY,
        scratch_shapes=[pltpu.SemaphoreType.DMA((N_DEV - 1,)), pltpu.SemaphoreType.DMA((N_DEV - 1,)),
                        pltpu.SemaphoreType.DMA],
    )(shard)


def _all_to_all(blocks, name):
    n = len(blocks)

    def body(*refs):
        src, dst = refs[:n], refs[n:2 * n]
        send_sems, recv_sems, local_sems = refs[2 * n:]
        x, y, c = _mesh_place()
        me = _logical(x, y, c)
        local = [pltpu.make_async_copy(src[t].at[me], dst[t].at[me], local_sems.at[t]) for t in range(n)]
        for cp in local:
            cp.start()
        sends, recvs = [], []
        for k in range(1, N_DEV):
            peer = (1 - x if k & 4 else x, 1 - y if k & 2 else y, 1 - c if k & 1 else c)
            p = _logical(*peer)
            for t in range(n):
                s = t * (N_DEV - 1) + k - 1
                out = pltpu.make_async_remote_copy(
                    src_ref=src[t].at[p], dst_ref=dst[t].at[me], send_sem=send_sems.at[s], recv_sem=recv_sems.at[s],
                    device_id=peer, device_id_type=MESH)
                out.start()
                sends.append(out)
                recvs.append(pltpu.make_async_remote_copy(
                    src_ref=src[t].at[p], dst_ref=dst[t].at[p], send_sem=send_sems.at[s], recv_sem=recv_sems.at[s],
                    device_id=peer, device_id_type=MESH))
        for cp in recvs:
            cp.wait_recv()
        for cp in sends:
            cp.wait_send()
        for cp in local:
            cp.wait()

    return pl.pallas_call(
        body, name=name,
        out_shape=[jax.ShapeDtypeStruct(b.shape, b.dtype) for b in blocks],
        in_specs=[ANY] * n, out_specs=[ANY] * n,
        scratch_shapes=[pltpu.SemaphoreType.DMA((n * (N_DEV - 1),)), pltpu.SemaphoreType.DMA((n * (N_DEV - 1),)),
                        pltpu.SemaphoreType.DMA((n,))],
    )(*blocks)


ADAMW_BLOCK_BYTES = 4 * 1024 * 1024


def _sum_adamw(parts, w, m, v, name):
    _, r, c = parts.shape
    tr = r
    while N_DEV * tr * c * 4 > ADAMW_BLOCK_BYTES and tr % (2 * SUBLANES) == 0:
        tr //= 2

    def body(p_ref, w_ref, m_ref, v_ref, g_ref, d_ref, nm_ref, nv_ref):
        g = p_ref[0].astype(F32)
        for s in range(1, N_DEV):
            g = g + p_ref[s].astype(F32)
        nm = ADAM_B1 * m_ref[...] + (1.0 - ADAM_B1) * g
        nv = ADAM_B2 * v_ref[...] + (1.0 - ADAM_B2) * (g * g)
        m_hat = nm / (1.0 - ADAM_B1 ** ADAM_STEP)
        v_hat = nv / (1.0 - ADAM_B2 ** ADAM_STEP)
        g_ref[...] = g
        d_ref[...] = -ADAM_LR * (m_hat / (jnp.sqrt(v_hat) + ADAM_EPS) + ADAM_WD * w_ref[...])
        nm_ref[...] = nm
        nv_ref[...] = nv

    row = pl.BlockSpec((tr, c), lambda i: (i, 0))
    return pl.pallas_call(
        body, name=name, grid=(r // tr,),
        in_specs=[pl.BlockSpec((N_DEV, tr, c), lambda i: (0, i, 0)), row, row, row],
        out_specs=[row, row, row, row],
        out_shape=[jax.ShapeDtypeStruct((r, c), F32)] * 4,
        compiler_params=_params("arbitrary"),
    )(parts, w, m, v)


LANES = 128
MATRICES = ("w_in", "w_out", "ffn_up_w", "ffn_down_w")
COLUMN_SHARDED = ("w_in", "lru_conv_w", "ffn_up_w", "ffn_conv_w")
REPLICATED = ("norm1_gain", "lru_conv_b", "lru_gate_a_w", "lru_gate_a_b", "lru_gate_x_w", "lru_gate_x_b", "lru_lambda",
              "lru_norm_gain", "ret_norm_gain", "norm2_gain", "ffn_conv_b", "final_norm_gain")
WEIGHTS = ("norm1_gain", "w_in", "lru_conv_w", "lru_conv_b", "lru_gate_a_w", "lru_gate_a_b", "lru_gate_x_w",
           "lru_gate_x_b", "lru_lambda", "lru_norm_gain", "ret_norm_gain", "w_out", "norm2_gain", "ffn_up_w",
           "ffn_conv_w", "ffn_conv_b", "ffn_down_w", "final_norm_gain")


def _rows(a, pad_to):
    a = a.reshape(-1, LANES)
    pad = (-a.shape[0]) % pad_to
    return jnp.pad(a, ((0, pad), (0, 0))) if pad else a


def _pack(arrays, pad_to):
    rows, layout, at = [], [], 0
    for a in arrays:
        r = _rows(a, pad_to)
        layout.append((at, a.size // LANES, a.shape))
        rows.append(r)
        at += r.shape[0]
    return jnp.concatenate(rows, axis=0), layout


def _unpack(packed, layout):
    lead = packed.shape[:-2]
    return [packed[..., at:at + n, :].reshape(lead + shape) for at, n, shape in layout]


def _to_blocks(full, name):
    r, c = full.shape
    if name in COLUMN_SHARDED:
        return full.reshape(r, N_DEV, c // N_DEV).transpose(1, 0, 2)
    return full.reshape(N_DEV, r // N_DEV, c)


def _from_blocks(blocks, name):
    _, r, c = blocks.shape
    if name in COLUMN_SHARDED:
        return blocks.transpose(1, 0, 2).reshape(r, N_DEV * c)
    return blocks.reshape(N_DEV * r, c)


def _f32_as_bf16(a):
    return lax.bitcast_convert_type(a, BF16).reshape(a.shape[:-1] + (2 * a.shape[-1],))


def _bf16_as_f32(a):
    return lax.bitcast_convert_type(a.reshape(a.shape[:-1] + (a.shape[-1] // 2, 2)), F32)


def kernel(x, norm1_gain, w_in, lru_conv_w, lru_conv_b, lru_gate_a_w, lru_gate_a_b, lru_gate_x_w, lru_gate_x_b, lru_lambda, lru_norm_gain, ret_norm_gain, w_out, norm2_gain, ffn_up_w, ffn_conv_w, ffn_conv_b, ffn_down_w, final_norm_gain, loss_target, m_norm1_gain, m_w_in, m_lru_conv_w, m_lru_conv_b, m_lru_gate_a_w, m_lru_gate_a_b, m_lru_gate_x_w, m_lru_gate_x_b, m_lru_lambda, m_lru_norm_gain, m_ret_norm_gain, m_w_out, m_norm2_gain, m_ffn_up_w, m_ffn_conv_w, m_ffn_conv_b, m_ffn_down_w, m_final_norm_gain, v_norm1_gain, v_w_in, v_lru_conv_w, v_lru_conv_b, v_lru_gate_a_w, v_lru_gate_a_b, v_lru_gate_x_w, v_lru_gate_x_b, v_lru_lambda, v_lru_norm_gain, v_ret_norm_gain, v_w_out, v_norm2_gain, v_ffn_up_w, v_ffn_conv_w, v_ffn_conv_b, v_ffn_down_w, v_final_norm_gain):
    args = dict(locals())
    given = {n: args[n] for n in WEIGHTS}
    mom_m = {n: args["m_" + n] for n in WEIGHTS}
    mom_v = {n: args["v_" + n] for n in WEIGHTS}
    out_shape = {n: given[n].shape for n in WEIGHTS}

    def plain(a, name):
        if a.ndim <= 2:
            return a.reshape(1, -1)
        return a[0]

    shard = {n: plain(given[n], n) for n in WEIGHTS}

    payload = [shard[n].astype(BF16) for n in MATRICES] + [_f32_as_bf16(shard[n].reshape(1, -1)) for n in
                                                             ("lru_conv_w", "ffn_conv_w")]
    packed, layout = _pack(payload, 2 * SUBLANES)
    gathered = _unpack(_all_gather(packed, "weights_all_gather"), layout)
    w = {n: shard[n] for n in REPLICATED}
    for n, blocks in zip(MATRICES, gathered[:len(MATRICES)]):
        w[n] = _from_blocks(blocks, n)
    for n, blocks in zip(("lru_conv_w", "ffn_conv_w"), gathered[len(MATRICES):]):
        w[n] = _from_blocks(_bf16_as_f32(blocks).reshape((N_DEV,) + shard[n].shape), n)

    loss, grad_x, g = _local_step(x[0], loss_target[0], w)
    loss = lax.psum(loss, ("x", "y", "c"))

    conv_blocks = jnp.concatenate([_to_blocks(g[n], n).reshape(N_DEV, -1) for n in ("lru_conv_w", "ffn_conv_w")], axis=1)
    conv_rows = conv_blocks.shape[1] // LANES
    conv_pad = (-conv_rows) % SUBLANES
    conv_blocks = jnp.pad(conv_blocks.reshape(N_DEV, conv_rows, LANES), ((0, 0), (0, conv_pad), (0, 0)))
    landed = _all_to_all([_to_blocks(g[n], n) for n in MATRICES] + [conv_blocks], "grads_all_to_all")
    rep_packed, rep_layout = _pack([g[n] for n in REPLICATED], SUBLANES)
    rep_parts = _all_gather(rep_packed, "small_grads_all_gather")

    res = {}
    for n, parts in zip(MATRICES, landed[:len(MATRICES)]):
        res[n] = _sum_adamw(parts, shard[n], plain(mom_m[n], n), plain(mom_v[n], n), "adamw_" + n)

    def conv_rows_of(d):
        flat = jnp.concatenate([plain(d[n], n).reshape(-1) for n in ("lru_conv_w", "ffn_conv_w")])
        return jnp.pad(flat.reshape(conv_rows, LANES), ((0, conv_pad), (0, 0)))

    conv_res = _sum_adamw(landed[-1], conv_rows_of(given), conv_rows_of(mom_m), conv_rows_of(mom_v), "adamw_conv")
    n_lru = shard["lru_conv_w"].size
    for n, lo, hi in (("lru_conv_w", 0, n_lru), ("ffn_conv_w", n_lru, n_lru + shard["ffn_conv_w"].size)):
        res[n] = [r.reshape(-1)[lo:hi].reshape(shard[n].shape) for r in conv_res]
    rep_res = _sum_adamw(rep_parts, _pack([shard[n] for n in REPLICATED], SUBLANES)[0],
                         _pack([plain(mom_m[n], n) for n in REPLICATED], SUBLANES)[0],
                         _pack([plain(mom_v[n], n) for n in REPLICATED], SUBLANES)[0], "adamw_replicated")
    for k in range(4):
        for n, a in zip(REPLICATED, _unpack(rep_res[k], rep_layout)):
            res.setdefault(n, [None] * 4)[k] = a

    outs = [loss, grad_x[None]]
    for k in range(4):
        outs += [res[n][k].reshape(out_shape[n]) for n in WEIGHTS]
    return tuple(outs)
```

```python
import functools
import math

import numpy as np
import jax
import jax.numpy as jnp
from jax import lax
from jax.experimental import pallas as pl
from jax.experimental.pallas import tpu as pltpu

F32 = jnp.float32
BF16 = jnp.bfloat16
MXU_DTYPE = jnp.bfloat16
GRAD_DTYPE = jnp.bfloat16

N_DEV = 8
D_MODEL = 1024
D_LRU = 512
LRU_HEADS = 8
LRU_HEAD_DIM = 64
LRU_CONV = 4
LRU_C = 8.0
D_RET = 512
RET_HEADS = 4
RET_HEAD_DIM = 128
RET_CHUNK = 128
ROPE_BASE = 10000.0
D_IN = 3072
D_FF = 3072
FFN_CONV = 3
NORM_EPS = 1e-6

ADAM_LR = 0.001
ADAM_B1 = 0.9
ADAM_B2 = 0.999
ADAM_EPS = 1e-08
ADAM_WD = 0.01
ADAM_STEP = 10

SUBLANES = 8
VMEM_LIMIT = 48 * 1024 * 1024


def _params(*sem):
    return pltpu.CompilerParams(dimension_semantics=sem, vmem_limit_bytes=VMEM_LIMIT)


def _dot(a, b):
    return jnp.dot(a.astype(MXU_DTYPE), b.astype(MXU_DTYPE), preferred_element_type=F32)


def _dot_nt(a, b):
    return lax.dot_general(a.astype(MXU_DTYPE), b.astype(MXU_DTYPE), (((1,), (1,)), ((), ())),
                           preferred_element_type=F32)


def _dot_tn(a, b):
    return lax.dot_general(a.astype(MXU_DTYPE), b.astype(MXU_DTYPE), (((0,), (0,)), ((), ())),
                           preferred_element_type=F32)


def _sigmoid(x):
    return 1.0 / (1.0 + jnp.exp(-x))


_GELU_C = math.sqrt(2.0 / math.pi)


def _gelu_parts(x):
    x2 = x * x
    t = jnp.tanh(_GELU_C * (x + 0.044715 * (x2 * x)))
    cdf = 0.5 * (1.0 + t)
    g = x * cdf
    dg = cdf + 0.5 * x * (1.0 - t * t) * (_GELU_C * (1.0 + 3.0 * 0.044715 * x2))
    return g, dg


def _gelu(x):
    t = jnp.tanh(_GELU_C * (x + 0.044715 * (x * x * x)))
    return x * (0.5 * (1.0 + t))


def _neg_expm1(x):
    series = x * (1.0 + x * (1.0 / 2.0) * (1.0 + x * (1.0 / 3.0) * (1.0 + x * (1.0 / 4.0) * (
        1.0 + x * (1.0 / 5.0) * (1.0 + x * (1.0 / 6.0) * (1.0 + x * (1.0 / 7.0)))))))
    return jnp.where(x > -0.25, -series, 1.0 - jnp.exp(x))


def _softplus(x):
    return jnp.maximum(x, 0.0) + jnp.log1p(jnp.exp(-jnp.abs(x)))


def _bcast_row(x, r, rows=SUBLANES):
    return jnp.broadcast_to(x[r:r + 1, :], (rows, x.shape[1]))


def _colsum8(x):
    return jnp.broadcast_to(jnp.sum(x, axis=0, keepdims=True), (SUBLANES, x.shape[1]))


def _shift_down(prev8, tile, s):
    if s == 0:
        return tile
    ext = jnp.concatenate([prev8, tile], axis=0)
    return pltpu.roll(ext, s, 0)[SUBLANES:, :]


def _shift_up(tile, next8, s):
    if s == 0:
        return tile
    ext = jnp.concatenate([tile, next8], axis=0)
    n = ext.shape[0]
    return pltpu.roll(ext, n - s, 0)[:tile.shape[0], :]


def _group_scan(a, b, reverse):
    n = a.shape[0]
    row = lax.broadcasted_iota(jnp.int32, a.shape, 0) & (SUBLANES - 1)
    for s in (1, 2, 4):
        shift = (n - s) if reverse else s
        a_sh = pltpu.roll(a, shift, 0)
        b_sh = pltpu.roll(b, shift, 0)
        m = (row <= SUBLANES - 1 - s) if reverse else (row >= s)
        b = jnp.where(m, a * b_sh + b, b)
        a = jnp.where(m, a * a_sh, a)
    return a, b


def _carry_scan(a_ref, b_ref, out_ref, carry0, reverse):
    n_groups = a_ref.shape[0] // SUBLANES

    def body(i, carry):
        g = (n_groups - 1 - i) if reverse else i
        r0 = pl.multiple_of(g * SUBLANES, SUBLANES)
        hg = a_ref[pl.ds(r0, SUBLANES), :] * carry + b_ref[pl.ds(r0, SUBLANES), :]
        out_ref[pl.ds(r0, SUBLANES), :] = hg
        return _bcast_row(hg, 0 if reverse else SUBLANES - 1)

    return lax.fori_loop(0, n_groups, body, carry0)


def _rms_fwd(h, gain):
    rstd = lax.rsqrt(jnp.mean(h * h, axis=-1, keepdims=True) + NORM_EPS)
    n = h * rstd
    return n, rstd, n * gain


def _rms_bwd(dy, n, rstd, gain):
    dn = dy * gain
    dh = rstd * (dn - n * jnp.mean(dn * n, axis=-1, keepdims=True))
    return dh, _colsum8(dy * n)


def _halo_map(tile_rows, col):
    per = tile_rows // SUBLANES
    return lambda i: (jnp.maximum(i * per - 1, 0), col)


def _mm(a, b, name, out_dtype, nt, tm=512, tn=512):
    m, k = a.shape
    n = b.shape[0] if nt else b.shape[1]
    tm, tn = min(tm, m), min(tn, n)

    def body(a_ref, b_ref, o_ref):
        f = _dot_nt if nt else _dot
        o_ref[...] = f(a_ref[...], b_ref[...]).astype(o_ref.dtype)

    b_spec = pl.BlockSpec((tn, k), lambda j, i: (j, 0)) if nt else pl.BlockSpec((k, tn), lambda j, i: (0, j))
    return pl.pallas_call(
        body, name=name, grid=(n // tn, m // tm),
        in_specs=[pl.BlockSpec((tm, k), lambda j, i: (i, 0)), b_spec],
        out_specs=pl.BlockSpec((tm, tn), lambda j, i: (i, j)),
        out_shape=jax.ShapeDtypeStruct((m, n), out_dtype),
        compiler_params=_params("arbitrary", "arbitrary"),
    )(a, b)


def _mm_tn(a, b, name, tn=1024, tk=2048):
    t, m = a.shape
    n = b.shape[1]
    tm, tn, tk = min(1024, m), min(tn, n), min(tk, t)
    nk = t // tk

    def body(a_ref, b_ref, o_ref, acc):
        k = pl.program_id(2)

        @pl.when(k == 0)
        def _():
            acc[...] = jnp.zeros_like(acc)
        acc[...] += _dot_tn(a_ref[...], b_ref[...])

        @pl.when(k == nk - 1)
        def _():
            o_ref[...] = acc[...].astype(o_ref.dtype)

    return pl.pallas_call(
        body, name=name, grid=(m // tm, n // tn, nk),
        in_specs=[pl.BlockSpec((tk, tm), lambda i, j, k: (k, i)), pl.BlockSpec((tk, tn), lambda i, j, k: (k, j))],
        out_specs=pl.BlockSpec((tm, tn), lambda i, j, k: (i, j)),
        out_shape=jax.ShapeDtypeStruct((m, n), GRAD_DTYPE),
        scratch_shapes=[pltpu.VMEM((tm, tn), F32)],
        compiler_params=_params("arbitrary", "arbitrary", "arbitrary"),
    )(a, b)


def _inproj_fwd(x, g1, w_in):
    t = x.shape[0]
    tm = min(512, t)

    def body(x_ref, g_ref, w_ref, u_ref, p_ref):
        _, _, u = _rms_fwd(x_ref[...], g_ref[...])
        u = u.astype(MXU_DTYPE)
        u_ref[...] = u
        p_ref[...] = _dot(u, w_ref[...])

    return pl.pallas_call(
        body, name="inproj_fwd", grid=(t // tm,),
        in_specs=[pl.BlockSpec((tm, D_MODEL), lambda i: (i, 0)), pl.BlockSpec((1, D_MODEL), lambda i: (0, 0)),
                  _resident((D_MODEL, D_IN))],
        out_specs=[pl.BlockSpec((tm, D_MODEL), lambda i: (i, 0)), pl.BlockSpec((tm, D_IN), lambda i: (i, 0))],
        out_shape=[jax.ShapeDtypeStruct((t, D_MODEL), MXU_DTYPE), jax.ShapeDtypeStruct((t, D_IN), F32)],
        compiler_params=_params("arbitrary"),
    )(x, g1, w_in)


def _lru_gates(xc, wa, ba, wx, bx, sp):
    r = _sigmoid(_dot(xc, wa) + ba)
    ig = _sigmoid(_dot(xc, wx) + bx)
    log_a = (-LRU_C) * r * sp
    a = jnp.exp(log_a)
    m = jnp.sqrt(_neg_expm1(2.0 * log_a))
    return r, ig, a, m


def _lru_fwd(proj, conv_w, conv_b, wa, ba, wx, bx, lam, gain):
    t = proj.shape[0]
    tm = min(256, t)
    c = D_LRU

    def body(x_ref, xh_ref, g_ref, cw_ref, cb_ref, wa_ref, ba_ref, wx_ref, bx_ref, lam_ref, gain_ref,
             xc_ref, h_ref, y_ref, a_scr, b_scr, carry):
        i = pl.program_id(0)

        @pl.when(i == 0)
        def _():
            carry[...] = jnp.zeros_like(carry)

        x = x_ref[...]
        prev = jnp.where(i == 0, 0.0, xh_ref[...])
        cw = cw_ref[...]
        xc = cb_ref[...] + cw[LRU_CONV - 1:LRU_CONV, :] * x
        for k in range(LRU_CONV - 1):
            xc = xc + cw[k:k + 1, :] * _shift_down(prev, x, LRU_CONV - 1 - k)
        xc_ref[...] = xc
        sp = _softplus(-lam_ref[...])
        _, ig, a, m = _lru_gates(xc, wa_ref[...], ba_ref[...], wx_ref[...], bx_ref[...], sp)
        ga, gb = _group_scan(a, m * (ig * xc), reverse=False)
        a_scr[...] = ga
        b_scr[...] = gb
        carry[...] = _carry_scan(a_scr, b_scr, h_ref, carry[...], reverse=False)
        z = h_ref[...] * _gelu(g_ref[...])
        _, _, y = _rms_fwd(z, gain_ref[...])
        y_ref[...] = y.astype(y_ref.dtype)

    row = lambda i: (i, 0)
    full = lambda i: (0, 0)
    vec = pl.BlockSpec((1, c), full)
    return pl.pallas_call(
        body, name="lru_fwd", grid=(t // tm,),
        in_specs=[pl.BlockSpec((tm, c), row), pl.BlockSpec((SUBLANES, c), _halo_map(tm, 0)),
                  pl.BlockSpec((tm, c), lambda i: (i, 1)),
                  pl.BlockSpec((LRU_CONV, c), full), vec, pl.BlockSpec((c, c), full), vec,
                  pl.BlockSpec((c, c), full), vec, vec, vec],
        out_specs=[pl.BlockSpec((tm, c), row), pl.BlockSpec((tm, c), row), pl.BlockSpec((tm, c), row)],
        out_shape=[jax.ShapeDtypeStruct((t, c), F32), jax.ShapeDtypeStruct((t, c), F32),
                   jax.ShapeDtypeStruct((t, c), MXU_DTYPE)],
        scratch_shapes=[pltpu.VMEM((tm, c), F32), pltpu.VMEM((tm, c), F32), pltpu.VMEM((SUBLANES, c), F32)],
        compiler_params=_params("arbitrary"),
    )(proj, proj, proj, conv_w, conv_b, wa, ba, wx, bx, lam, gain)


def _ret_consts():
    c = RET_CHUNK
    log_g = jnp.log1p(-jnp.exp2(-5.0 - jnp.arange(RET_HEADS, dtype=F32)))
    idx = jnp.arange(c, dtype=F32)
    diff = idx[:, None] - idx[None, :]
    decay = jnp.where(diff[None] >= 0, jnp.exp(jnp.maximum(diff, 0.0)[None] * log_g[:, None, None]), 0.0)
    zeta = jnp.exp((c - 1 - idx)[None, :] * log_g[:, None])
    xi = jnp.exp((idx + 1.0)[None, :] * log_g[:, None])
    spread = lambda v: jnp.repeat(v.T, RET_HEAD_DIM, axis=1)
    log_g_np = np.log1p(-np.exp2(-5.0 - np.arange(RET_HEADS, dtype=np.float32))).astype(np.float32)
    g_chunk = [float(np.exp(np.float32(c) * lg)) for lg in log_g_np]
    return decay, spread(xi), spread(zeta), g_chunk


def _rope_tables(t):
    pos = jnp.arange(t, dtype=F32)
    inv_freq = ROPE_BASE ** (-jnp.arange(0, RET_HEAD_DIM, 2, dtype=F32) / RET_HEAD_DIM)
    ang = pos[:, None] * inv_freq[None, :]
    cos, sin = jnp.cos(ang), jnp.sin(ang)
    return jnp.concatenate([cos, cos], axis=-1), jnp.concatenate([-sin, sin], axis=-1)


def _rope(x, cos2, sin_signed):
    return x * cos2 + pltpu.roll(x, RET_HEAD_DIM // 2, 1) * sin_signed


def _rope_bwd(d, cos2, sin_signed):
    return d * cos2 + pltpu.roll(d * sin_signed, RET_HEAD_DIM // 2, 1)


RET_SCALE = RET_HEAD_DIM ** -0.5


def _ret_fwd(proj, cos2, sin_signed, gain):
    t = proj.shape[0]
    c, d, nh = RET_CHUNK, RET_HEAD_DIM, RET_HEADS
    n_chunks = t // c
    decay, xi, zeta, g_chunk = _ret_consts()

    def body(qk_ref, vg_ref, cos_ref, sin_ref, dec_ref, xi_ref, zeta_ref, gain_ref, o_ref, y_ref, st_ref, state):
        @pl.when(pl.program_id(0) == 0)
        def _():
            state[...] = jnp.zeros_like(state)

        cos2, sin_s = cos_ref[...], sin_ref[...]
        for h in range(nh):
            lo = h * d
            q = _rope(qk_ref[:, lo:lo + d], cos2, sin_s)
            k = _rope(qk_ref[:, D_RET + lo:D_RET + lo + d], cos2, sin_s) * RET_SCALE
            v = vg_ref[:, lo:lo + d]
            g = vg_ref[:, D_RET + lo:D_RET + lo + d]
            s_prev = state[h]
            st_ref[0, h] = s_prev
            scores = _dot_nt(q, k) * dec_ref[h]
            o = _dot(scores, v) + _dot(q * xi_ref[:, lo:lo + d], s_prev)
            state[h] = s_prev * g_chunk[h] + _dot_tn(k * zeta_ref[:, lo:lo + d], v)
            o_ref[:, lo:lo + d] = o
            mu = jnp.mean(o, axis=-1, keepdims=True)
            oc = o - mu
            on = oc * lax.rsqrt(jnp.mean(oc * oc, axis=-1, keepdims=True) + NORM_EPS)
            y_ref[:, lo:lo + d] = (on * gain_ref[:, lo:lo + d] * (g * _sigmoid(g))).astype(y_ref.dtype)

    full2 = lambda i: (0, 0)
    return pl.pallas_call(
        body, name="ret_fwd", grid=(n_chunks,),
        in_specs=[pl.BlockSpec((c, 2 * D_RET), lambda i: (i, 1)), pl.BlockSpec((c, 2 * D_RET), lambda i: (i, 2)),
                  pl.BlockSpec((c, d), lambda i: (i, 0)), pl.BlockSpec((c, d), lambda i: (i, 0)),
                  pl.BlockSpec((nh, c, c), lambda i: (0, 0, 0)), pl.BlockSpec((c, D_RET), full2),
                  pl.BlockSpec((c, D_RET), full2), pl.BlockSpec((1, D_RET), full2)],
        out_specs=[pl.BlockSpec((c, D_RET), lambda i: (i, 0)), pl.BlockSpec((c, D_RET), lambda i: (i, 0)),
                   pl.BlockSpec((1, nh, d, d), lambda i: (i, 0, 0, 0))],
        out_shape=[jax.ShapeDtypeStruct((t, D_RET), F32), jax.ShapeDtypeStruct((t, D_RET), MXU_DTYPE),
                   jax.ShapeDtypeStruct((n_chunks, nh, d, d), F32)],
        scratch_shapes=[pltpu.VMEM((nh, d, d), F32)],
        compiler_params=_params("arbitrary"),
    )(proj, proj, cos2, sin_signed, decay, xi, zeta, gain)


def _outproj_fwd(x, y_lru, y_ret, w_out, g2):
    t = x.shape[0]
    tm = min(512, t)

    def body(x_ref, yl_ref, yr_ref, w_ref, g_ref, h1_ref, u2_ref):
        h1 = x_ref[...] + _dot(yl_ref[...], w_ref[:D_LRU, :]) + _dot(yr_ref[...], w_ref[D_LRU:, :])
        h1_ref[...] = h1
        _, _, u = _rms_fwd(h1, g_ref[...])
        u2_ref[...] = u.astype(u2_ref.dtype)

    row = lambda i: (i, 0)
    return pl.pallas_call(
        body, name="outproj_fwd", grid=(t // tm,),
        in_specs=[pl.BlockSpec((tm, D_MODEL), row), pl.BlockSpec((tm, D_LRU), row), pl.BlockSpec((tm, D_RET), row),
                  pl.BlockSpec((D_MODEL, D_MODEL), lambda i: (0, 0)), pl.BlockSpec((1, D_MODEL), lambda i: (0, 0))],
        out_specs=[pl.BlockSpec((tm, D_MODEL), row), pl.BlockSpec((tm, D_MODEL), row)],
        out_shape=[jax.ShapeDtypeStruct((t, D_MODEL), F32), jax.ShapeDtypeStruct((t, D_MODEL), MXU_DTYPE)],
        compiler_params=_params("arbitrary"),
    )(x, y_lru, y_ret, w_out, g2)


FFN_TN = 768
FFN_NJ = D_FF // FFN_TN


def _ffn_conv(up_ref, halo_ref, cw_ref, cb_ref, first):
    x = up_ref[...]
    prev = jnp.where(first, 0.0, halo_ref[...])
    cw = cw_ref[...]
    y = cb_ref[...] + cw[FFN_CONV - 1:FFN_CONV, :] * x
    for k in range(FFN_CONV - 1):
        y = y + cw[k:k + 1, :] * _shift_down(prev, x, FFN_CONV - 1 - k)
    return y, x, prev


def _ffn_fwd(up, conv_w, conv_b, w_down, h1, gf, target):
    t = up.shape[0]
    tm = min(256, t)
    tn, nj = FFN_TN, FFN_NJ

    def body(ua_ref, uah_ref, uv_ref, uvh_ref, cwa_ref, cwv_ref, cba_ref, cbv_ref, wd_ref, h1_ref, gf_ref, tg_ref,
             act_ref, dh_ref, dhb_ref, dgf_ref, loss_ref, acc):
        i, j = pl.program_id(0), pl.program_id(1)

        @pl.when((i == 0) & (j == 0))
        def _():
            dgf_ref[...] = jnp.zeros_like(dgf_ref)
            loss_ref[...] = jnp.zeros_like(loss_ref)

        @pl.when(j == 0)
        def _():
            acc[...] = jnp.zeros_like(acc)

        a, _, _ = _ffn_conv(ua_ref, uah_ref, cwa_ref, cba_ref, i == 0)
        v, _, _ = _ffn_conv(uv_ref, uvh_ref, cwv_ref, cbv_ref, i == 0)
        act = (_gelu(a) * v).astype(act_ref.dtype)
        act_ref[...] = act
        acc[...] += _dot(act, wd_ref[...])

        @pl.when(j == nj - 1)
        def _():
            n, rstd, y = _rms_fwd(h1_ref[...] + acc[...], gf_ref[...])
            err = y - tg_ref[...]
            loss_ref[...] += (0.5 / D_MODEL) * jnp.sum(err * err)
            dh, dgf = _rms_bwd(err * (1.0 / D_MODEL), n, rstd, gf_ref[...])
            dgf_ref[...] += dgf
            dh_ref[...] = dh
            dhb_ref[...] = dh.astype(dhb_ref.dtype)

    per = tm // SUBLANES
    halo = lambda off: (lambda i, j: (jnp.maximum(i * per - 1, 0), j + off))
    row = lambda i, j: (i, 0)
    const = lambda i, j: (0, 0)
    return pl.pallas_call(
        body, name="ffn_fwd", grid=(t // tm, nj),
        in_specs=[pl.BlockSpec((tm, tn), lambda i, j: (i, j)), pl.BlockSpec((SUBLANES, tn), halo(0)),
                  pl.BlockSpec((tm, tn), lambda i, j: (i, j + nj)), pl.BlockSpec((SUBLANES, tn), halo(nj)),
                  pl.BlockSpec((FFN_CONV, tn), lambda i, j: (0, j)), pl.BlockSpec((FFN_CONV, tn), lambda i, j: (0, j + nj)),
                  pl.BlockSpec((1, tn), lambda i, j: (0, j)), pl.BlockSpec((1, tn), lambda i, j: (0, j + nj)),
                  pl.BlockSpec((tn, D_MODEL), lambda i, j: (j, 0)),
                  pl.BlockSpec((tm, D_MODEL), row), pl.BlockSpec((1, D_MODEL), const), pl.BlockSpec((tm, D_MODEL), row)],
        out_specs=[pl.BlockSpec((tm, tn), lambda i, j: (i, j)), pl.BlockSpec((tm, D_MODEL), row),
                   pl.BlockSpec((tm, D_MODEL), row), pl.BlockSpec((SUBLANES, D_MODEL), const),
                   pl.BlockSpec((SUBLANES, 128), const)],
        out_shape=[jax.ShapeDtypeStruct((t, D_FF), MXU_DTYPE), jax.ShapeDtypeStruct((t, D_MODEL), F32),
                   jax.ShapeDtypeStruct((t, D_MODEL), MXU_DTYPE), jax.ShapeDtypeStruct((SUBLANES, D_MODEL), F32),
                   jax.ShapeDtypeStruct((SUBLANES, 128), F32)],
        scratch_shapes=[pltpu.VMEM((tm, D_MODEL), F32)],
        compiler_params=_params("arbitrary", "arbitrary"),
    )(up, up, up, up, conv_w, conv_w, conv_b, conv_b, w_down, h1, gf, target)


FFN_ACC_ROWS = SUBLANES * (FFN_CONV + 1)


def _ffn_bwd(dh2_b, w_down, up, conv_w, conv_b):
    t = up.shape[0]
    tm = min(256, t)
    tn, nj = FFN_TN, FFN_NJ
    ni = t // tm

    def conv_bwd(dy, x, prev, cw, acc_ref, carry_ref, dup_ref):
        nxt = carry_ref[...]
        carry_ref[...] = dy[:SUBLANES, :]
        dx = cw[FFN_CONV - 1:FFN_CONV, :] * dy
        for k in range(FFN_CONV - 1):
            dx = dx + cw[k:k + 1, :] * _shift_up(dy, nxt, FFN_CONV - 1 - k)
        dup_ref[...] = dx.astype(dup_ref.dtype)
        for k in range(FFN_CONV):
            acc_ref[k * SUBLANES:(k + 1) * SUBLANES, :] += _colsum8(dy * _shift_down(prev, x, FFN_CONV - 1 - k))
        acc_ref[FFN_CONV * SUBLANES:, :] += _colsum8(dy)

    def body(dh_ref, wd_ref, ua_ref, uah_ref, uv_ref, uvh_ref, cwa_ref, cwv_ref, cba_ref, cbv_ref,
             dua_ref, duv_ref, acca_ref, accv_ref, carry_a, carry_v):
        i = pl.program_id(1)
        r = ni - 1 - i

        @pl.when(i == 0)
        def _():
            acca_ref[...] = jnp.zeros_like(acca_ref)
            accv_ref[...] = jnp.zeros_like(accv_ref)
            carry_a[...] = jnp.zeros_like(carry_a)
            carry_v[...] = jnp.zeros_like(carry_v)

        a, xa, pa = _ffn_conv(ua_ref, uah_ref, cwa_ref, cba_ref, r == 0)
        v, xv, pv = _ffn_conv(uv_ref, uvh_ref, cwv_ref, cbv_ref, r == 0)
        g, dg = _gelu_parts(a)
        dact = _dot_nt(dh_ref[...], wd_ref[...])
        conv_bwd(dact * v * dg, xa, pa, cwa_ref[...], acca_ref, carry_a, dua_ref)
        conv_bwd(dact * g, xv, pv, cwv_ref[...], accv_ref, carry_v, duv_ref)

    per = tm // SUBLANES
    rev = lambda off: (lambda j, i: (ni - 1 - i, j + off))
    halo = lambda off: (lambda j, i: (jnp.maximum((ni - 1 - i) * per - 1, 0), j + off))
    colj = lambda off: (lambda j, i: (0, j + off))
    return pl.pallas_call(
        body, name="ffn_bwd", grid=(nj, ni),
        in_specs=[pl.BlockSpec((tm, D_MODEL), lambda j, i: (ni - 1 - i, 0)), pl.BlockSpec((tn, D_MODEL), lambda j, i: (j, 0)),
                  pl.BlockSpec((tm, tn), rev(0)), pl.BlockSpec((SUBLANES, tn), halo(0)),
                  pl.BlockSpec((tm, tn), rev(nj)), pl.BlockSpec((SUBLANES, tn), halo(nj)),
                  pl.BlockSpec((FFN_CONV, tn), colj(0)), pl.BlockSpec((FFN_CONV, tn), colj(nj)),
                  pl.BlockSpec((1, tn), colj(0)), pl.BlockSpec((1, tn), colj(nj))],
        out_specs=[pl.BlockSpec((tm, tn), rev(0)), pl.BlockSpec((tm, tn), rev(0)),
                   pl.BlockSpec((FFN_ACC_ROWS, tn), colj(0)), pl.BlockSpec((FFN_ACC_ROWS, tn), colj(0))],
        out_shape=[jax.ShapeDtypeStruct((t, D_FF), MXU_DTYPE), jax.ShapeDtypeStruct((t, D_FF), MXU_DTYPE),
                   jax.ShapeDtypeStruct((FFN_ACC_ROWS, D_FF), F32), jax.ShapeDtypeStruct((FFN_ACC_ROWS, D_FF), F32)],
        scratch_shapes=[pltpu.VMEM((SUBLANES, tn), F32), pltpu.VMEM((SUBLANES, tn), F32)],
        compiler_params=_params("arbitrary", "arbitrary"),
    )(dh2_b, w_down, up, up, up, up, conv_w, conv_w, conv_b, conv_b)


def _resident(shape):
    return pl.BlockSpec(shape, lambda *_: (0,) * len(shape), pipeline_mode=pl.Buffered(1))


def _norm_bwd_matmul(parts, w, h, gain, d_res, name):
    t = h.shape[0]
    tm = min(256, t)
    n_parts = len(parts)
    widths = [a.shape[1] for a in parts]
    assert sum(widths) == w.shape[1]

    def body(*refs):
        a_refs = refs[:n_parts]
        w_ref, h_ref, g_ref, dres_ref, dh_ref, dhb_ref, dg_ref = refs[n_parts:]

        @pl.when(pl.program_id(0) == 0)
        def _():
            dg_ref[...] = jnp.zeros_like(dg_ref)

        du, lo = None, 0
        for a_ref, width in zip(a_refs, widths):
            term = _dot_nt(a_ref[...], w_ref[:, lo:lo + width])
            du = term if du is None else du + term
            lo += width
        n, rstd, _ = _rms_fwd(h_ref[...], g_ref[...])
        dh, dg = _rms_bwd(du, n, rstd, g_ref[...])
        dh = dh + dres_ref[...]
        dg_ref[...] += dg
        dh_ref[...] = dh
        dhb_ref[...] = dh.astype(dhb_ref.dtype)

    row = lambda i: (i, 0)
    const = lambda i: (0, 0)
    return pl.pallas_call(
        body, name=name, grid=(t // tm,),
        in_specs=[pl.BlockSpec((tm, width), row) for width in widths] + [
            _resident(w.shape), pl.BlockSpec((tm, D_MODEL), row), pl.BlockSpec((1, D_MODEL), const),
            pl.BlockSpec((tm, D_MODEL), row)],
        out_specs=[pl.BlockSpec((tm, D_MODEL), row), pl.BlockSpec((tm, D_MODEL), row),
                   pl.BlockSpec((SUBLANES, D_MODEL), const)],
        out_shape=[jax.ShapeDtypeStruct((t, D_MODEL), F32), jax.ShapeDtypeStruct((t, D_MODEL), MXU_DTYPE),
                   jax.ShapeDtypeStruct((SUBLANES, D_MODEL), F32)],
        compiler_params=_params("arbitrary"),
    )(*parts, w, h, gain, d_res)


def _ret_bwd(proj, cos2, sin_signed, gain, o, states, dmixed):
    t = proj.shape[0]
    c, d, nh = RET_CHUNK, RET_HEAD_DIM, RET_HEADS
    n_chunks = t // c
    decay, xi, zeta, g_chunk = _ret_consts()

    def body(qk_ref, vg_ref, cos_ref, sin_ref, dec_ref, xi_ref, zeta_ref, gain_ref, o_ref, st_ref, dy_ref,
             dp_ref, dgain_ref, gstate):
        @pl.when(pl.program_id(0) == 0)
        def _():
            gstate[...] = jnp.zeros_like(gstate)
            dgain_ref[...] = jnp.zeros_like(dgain_ref)

        cos2, sin_s = cos_ref[...], sin_ref[...]
        for h in range(nh):
            lo = h * d
            q = _rope(qk_ref[:, lo:lo + d], cos2, sin_s)
            k = _rope(qk_ref[:, D_RET + lo:D_RET + lo + d], cos2, sin_s) * RET_SCALE
            v = vg_ref[:, lo:lo + d]
            g = vg_ref[:, D_RET + lo:D_RET + lo + d]
            gain_h = gain_ref[:, lo:lo + d]
            xi_h, zeta_h, dec = xi_ref[:, lo:lo + d], zeta_ref[:, lo:lo + d], dec_ref[h]
            dy = dy_ref[:, lo:lo + d]
            sg = _sigmoid(g)
            o_h = o_ref[:, lo:lo + d]
            oc = o_h - jnp.mean(o_h, axis=-1, keepdims=True)
            rstd = lax.rsqrt(jnp.mean(oc * oc, axis=-1, keepdims=True) + NORM_EPS)
            on = oc * rstd
            dp_ref[:, 3 * D_RET + lo:3 * D_RET + lo + d] = (
                dy * on * gain_h * (sg * (1.0 + g * (1.0 - sg)))).astype(dp_ref.dtype)
            don_g = dy * (g * sg)
            dgain_ref[:, lo:lo + d] += _colsum8(don_g * on)
            don = don_g * gain_h
            do = rstd * (don - jnp.mean(don, axis=-1, keepdims=True) - on * jnp.mean(don * on, axis=-1, keepdims=True))
            s_prev = st_ref[0, h]
            g_next = gstate[h]
            p = _dot_nt(q, k) * dec
            dpm = _dot_nt(do, v) * dec
            dq = _dot(dpm, k) + _dot_nt(do, s_prev) * xi_h
            dk = _dot_tn(dpm, q) + _dot_nt(v, g_next) * zeta_h
            dv = _dot_tn(p, do) + _dot(k * zeta_h, g_next)
            gstate[h] = g_next * g_chunk[h] + _dot_tn(q * xi_h, do)
            dp_ref[:, lo:lo + d] = _rope_bwd(dq, cos2, sin_s).astype(dp_ref.dtype)
            dp_ref[:, D_RET + lo:D_RET + lo + d] = _rope_bwd(dk * RET_SCALE, cos2, sin_s).astype(dp_ref.dtype)
            dp_ref[:, 2 * D_RET + lo:2 * D_RET + lo + d] = dv.astype(dp_ref.dtype)

    rev = lambda col: (lambda i: (n_chunks - 1 - i, col))
    full2 = lambda i: (0, 0)
    return pl.pallas_call(
        body, name="ret_bwd", grid=(n_chunks,),
        in_specs=[pl.BlockSpec((c, 2 * D_RET), rev(1)), pl.BlockSpec((c, 2 * D_RET), rev(2)),
                  pl.BlockSpec((c, d), rev(0)), pl.BlockSpec((c, d), rev(0)),
                  pl.BlockSpec((nh, c, c), lambda i: (0, 0, 0)), pl.BlockSpec((c, D_RET), full2),
                  pl.BlockSpec((c, D_RET), full2), pl.BlockSpec((1, D_RET), full2),
                  pl.BlockSpec((c, D_RET), rev(0)), pl.BlockSpec((1, nh, d, d), lambda i: (n_chunks - 1 - i, 0, 0, 0)),
                  pl.BlockSpec((c, D_RET), rev(1))],
        out_specs=[pl.BlockSpec((c, 4 * D_RET), rev(0)), pl.BlockSpec((SUBLANES, D_RET), full2)],
        out_shape=[jax.ShapeDtypeStruct((t, 4 * D_RET), MXU_DTYPE), jax.ShapeDtypeStruct((SUBLANES, D_RET), F32)],
        scratch_shapes=[pltpu.VMEM((nh, d, d), F32)],
        compiler_params=_params("arbitrary"),
    )(proj, proj, cos2, sin_signed, decay, xi, zeta, gain, o, states, dmixed)


LRU_ACC = {"conv_w": 0, "conv_b": LRU_CONV, "gate_a_b": LRU_CONV + 1, "gate_x_b": LRU_CONV + 2,
           "lambda": LRU_CONV + 3, "norm_gain": LRU_CONV + 4}
LRU_ACC_ROWS = SUBLANES * (LRU_CONV + 5)


def _lru_bwd(proj, xc_all, h_all, dmixed, conv_w, wa, ba, wx, bx, lam, gain):
    t = proj.shape[0]
    tm = min(256, t)
    c = D_LRU
    ni = t // tm

    def body(x_ref, xh_ref, g_ref, xc_ref, h_ref, hh_ref, dy_ref, cw_ref, wa_ref, ba_ref, wx_ref, bx_ref, lam_ref,
             gain_ref, dp_ref, acc_ref, dwa_ref, dwx_ref, a_scr, b_scr, mu_scr, carry_mu, carry_dxc):
        i = pl.program_id(0)
        r = ni - 1 - i

        @pl.when(i == 0)
        def _():
            acc_ref[...] = jnp.zeros_like(acc_ref)
            dwa_ref[...] = jnp.zeros_like(dwa_ref)
            dwx_ref[...] = jnp.zeros_like(dwx_ref)
            carry_mu[...] = jnp.zeros_like(carry_mu)
            carry_dxc[...] = jnp.zeros_like(carry_dxc)

        def add(name, val, k=0):
            lo = (LRU_ACC[name] + k) * SUBLANES
            acc_ref[lo:lo + SUBLANES, :] += _colsum8(val)

        xc, h = xc_ref[...], h_ref[...]
        lam_v = lam_ref[...]
        sp = _softplus(-lam_v)
        rg, ig, a, m = _lru_gates(xc, wa_ref[...], ba_ref[...], wx_ref[...], bx_ref[...], sp)
        gl, dgl = _gelu_parts(g_ref[...])
        zn, rstd, _ = _rms_fwd(h * gl, gain_ref[...])
        dy = dy_ref[...]
        dz, dgain = _rms_bwd(dy, zn, rstd, gain_ref[...])
        lo = LRU_ACC["norm_gain"] * SUBLANES
        acc_ref[lo:lo + SUBLANES, :] += dgain
        dp_ref[:, c:] = (dz * h * dgl).astype(dp_ref.dtype)
        dh = dz * gl
        ga, gb = _group_scan(a, a * dh, reverse=True)
        a_scr[...] = ga
        b_scr[...] = gb
        mu_next_tile = carry_mu[...]
        carry_mu[...] = _carry_scan(a_scr, b_scr, mu_scr, mu_next_tile, reverse=True)
        lam_t = dh + _shift_up(mu_scr[...], mu_next_tile, 1)
        h_prev = _shift_down(jnp.where(r == 0, 0.0, hh_ref[...]), h, 1)
        da = lam_t * h_prev
        dig = lam_t * m * xc
        dxc = lam_t * m * ig
        dlog_a = da * a - (lam_t * ig * xc) * (a * a) / m
        dpr = dlog_a * ((-LRU_C) * sp) * rg * (1.0 - rg)
        add("lambda", dlog_a * ((-LRU_C) * rg) * (-_sigmoid(-lam_v)))
        dpi = dig * ig * (1.0 - ig)
        add("gate_a_b", dpr)
        add("gate_x_b", dpi)
        dwa_ref[...] += _dot_tn(xc, dpr)
        dwx_ref[...] += _dot_tn(xc, dpi)
        dxc = dxc + _dot_nt(dpr, wa_ref[...]) + _dot_nt(dpi, wx_ref[...])
        add("conv_b", dxc)
        x = x_ref[...]
        prev = jnp.where(r == 0, 0.0, xh_ref[...])
        cw = cw_ref[...]
        nxt = carry_dxc[...]
        carry_dxc[...] = dxc[:SUBLANES, :]
        dx = cw[LRU_CONV - 1:LRU_CONV, :] * dxc
        for k in range(LRU_CONV - 1):
            dx = dx + cw[k:k + 1, :] * _shift_up(dxc, nxt, LRU_CONV - 1 - k)
        for k in range(LRU_CONV):
            add("conv_w", dxc * _shift_down(prev, x, LRU_CONV - 1 - k), k)
        dp_ref[:, :c] = dx.astype(dp_ref.dtype)

    per = tm // SUBLANES
    rev = lambda col: (lambda i: (ni - 1 - i, col))
    halo = lambda i: (jnp.maximum((ni - 1 - i) * per - 1, 0), 0)
    full = lambda i: (0, 0)
    vec = pl.BlockSpec((1, c), full)
    mat = pl.BlockSpec((c, c), full)
    return pl.pallas_call(
        body, name="lru_bwd", grid=(ni,),
        in_specs=[pl.BlockSpec((tm, c), rev(0)), pl.BlockSpec((SUBLANES, c), halo), pl.BlockSpec((tm, c), rev(1)),
                  pl.BlockSpec((tm, c), rev(0)), pl.BlockSpec((tm, c), rev(0)), pl.BlockSpec((SUBLANES, c), halo),
                  pl.BlockSpec((tm, c), rev(0)), pl.BlockSpec((LRU_CONV, c), full), mat, vec, mat, vec, vec, vec],
        out_specs=[pl.BlockSpec((tm, 2 * c), rev(0)), pl.BlockSpec((LRU_ACC_ROWS, c), full), mat, mat],
        out_shape=[jax.ShapeDtypeStruct((t, 2 * c), MXU_DTYPE), jax.ShapeDtypeStruct((LRU_ACC_ROWS, c), F32),
                   jax.ShapeDtypeStruct((c, c), F32), jax.ShapeDtypeStruct((c, c), F32)],
        scratch_shapes=[pltpu.VMEM((tm, c), F32), pltpu.VMEM((tm, c), F32), pltpu.VMEM((tm, c), F32),
                        pltpu.VMEM((SUBLANES, c), F32), pltpu.VMEM((SUBLANES, c), F32)],
        compiler_params=_params("arbitrary"),
    )(proj, proj, proj, xc_all, h_all, h_all, dmixed, conv_w, wa, ba, wx, bx, lam, gain)


def _block_diag(w):
    nh, d, _ = w.shape
    eye = jnp.eye(nh, dtype=w.dtype)
    return (w[:, :, None, :] * eye[:, None, :, None]).reshape(nh * d, nh * d)


def _diag_blocks(dense, nh):
    d = dense.shape[0] // nh
    blocks = dense.reshape(nh, d, nh, d)
    return jnp.stack([blocks[h, :, h, :] for h in range(nh)], axis=0)


def _local_step(x, target, w):
    t = x.shape[0]
    cos2, sin_signed = _rope_tables(t)
    wa = _block_diag(w["lru_gate_a_w"]).astype(MXU_DTYPE)
    wx = _block_diag(w["lru_gate_x_w"]).astype(MXU_DTYPE)

    u1, proj = _inproj_fwd(x, w["norm1_gain"], w["w_in"])
    xc, h_lru, y_lru = _lru_fwd(proj, w["lru_conv_w"], w["lru_conv_b"], wa, w["lru_gate_a_b"], wx, w["lru_gate_x_b"],
                                w["lru_lambda"], w["lru_norm_gain"])
    o_ret, y_ret, states = _ret_fwd(proj, cos2, sin_signed, w["ret_norm_gain"])
    h1, u2 = _outproj_fwd(x, y_lru, y_ret, w["w_out"], w["norm2_gain"])
    up = _mm(u2, w["ffn_up_w"], "ffn_up_fwd", F32, nt=False, tn=FFN_TN)
    act, dh2, dh2_b, dgf, loss = _ffn_fwd(up, w["ffn_conv_w"], w["ffn_conv_b"], w["ffn_down_w"], h1,
                                          w["final_norm_gain"], target)

    g = {"final_norm_gain": dgf[0]}
    g["ffn_down_w"] = _mm_tn(act, dh2_b, "ffn_down_wgrad")
    dup_a, dup_v, acc_a, acc_v = _ffn_bwd(dh2_b, w["ffn_down_w"], up, w["ffn_conv_w"], w["ffn_conv_b"])
    acc = jnp.concatenate([acc_a, acc_v], axis=1)[::SUBLANES]
    g["ffn_conv_w"], g["ffn_conv_b"] = acc[:FFN_CONV], acc[FFN_CONV:]
    g["ffn_up_w"] = jnp.concatenate([_mm_tn(u2, dup_a, "ffn_up_wgrad_a"), _mm_tn(u2, dup_v, "ffn_up_wgrad_v")], axis=1)
    dh1, dh1_b, dg2 = _norm_bwd_matmul([dup_a, dup_v], w["ffn_up_w"], h1, w["norm2_gain"], dh2, "ffn_up_bwd")
    g["norm2_gain"] = dg2[:1]
    g["w_out"] = jnp.concatenate([_mm_tn(y_lru, dh1_b, "w_out_wgrad_lru"), _mm_tn(y_ret, dh1_b, "w_out_wgrad_ret")],
                                 axis=0)
    dmixed = _mm(dh1_b, w["w_out"], "outproj_bwd", F32, nt=True)
    dp_ret, dgain_ret = _ret_bwd(proj, cos2, sin_signed, w["ret_norm_gain"], o_ret, states, dmixed)
    g["ret_norm_gain"] = dgain_ret[:1]
    dp_lru, lru_acc, dwa, dwx = _lru_bwd(proj, xc, h_lru, dmixed, w["lru_conv_w"], wa, w["lru_gate_a_b"], wx,
                                         w["lru_gate_x_b"], w["lru_lambda"], w["lru_norm_gain"])
    lru_acc = lru_acc[::SUBLANES]
    g["lru_conv_w"] = lru_acc[:LRU_CONV]
    for name in ("conv_b", "gate_a_b", "gate_x_b", "lambda", "norm_gain"):
        g["lru_" + name] = lru_acc[LRU_ACC[name]:LRU_ACC[name] + 1]
    g["lru_gate_a_w"] = _diag_blocks(dwa, LRU_HEADS)
    g["lru_gate_x_w"] = _diag_blocks(dwx, LRU_HEADS)
    g["w_in"] = jnp.concatenate([_mm_tn(u1, dp_lru, "w_in_wgrad_lru"), _mm_tn(u1, dp_ret, "w_in_wgrad_ret")], axis=1)
    grad_x, _, dg1 = _norm_bwd_matmul([dp_lru, dp_ret], w["w_in"], x, w["norm1_gain"], dh1, "inproj_bwd")
    g["norm1_gain"] = dg1[:1]
    return loss[0, 0], grad_x, g


MESH = pl.DeviceIdType.MESH
ANY = pl.BlockSpec(memory_space=pl.ANY)


def _mesh_place():
    x, y, c = lax.axis_index("x"), lax.axis_index("y"), lax.axis_index("c")
    return x, y, c


def _logical(px, py, pc):
    return 4 * px + 2 * py + pc


def _all_gather(shard, name):
    def body(x_ref, out_ref, send_sems, recv_sems, local_sem):
        x, y, c = _mesh_place()
        me, sibling = (x, y, c), (x, y, 1 - c)
        chips = [(1 - x, y), (x, 1 - y), (1 - x, 1 - y)]

        def slot(place):
            return out_ref.at[_logical(*place)]

        def copy(k, block, to, src=None):
            return pltpu.make_async_remote_copy(
                src_ref=slot(block) if src is None else src, dst_ref=slot(block),
                send_sem=send_sems.at[k], recv_sem=recv_sems.at[k], device_id=to, device_id_type=MESH)

        mine = pltpu.make_async_copy(x_ref, slot(me), local_sem)
        mine.start()
        first = [copy(0, me, sibling, src=x_ref)]
        first += [copy(1 + j, me, (*chip, c), src=x_ref) for j, chip in enumerate(chips)]
        for cp in first:
            cp.start()
        passed = [copy(4 + j, (*chip, c), sibling) for j, chip in enumerate(chips)]
        for j, chip in enumerate(chips):
            copy(1 + j, (*chip, c), me).wait_recv()
            passed[j].start()
        copy(0, sibling, me).wait_recv()
        for j, chip in enumerate(chips):
            copy(4 + j, (*chip, 1 - c), me).wait_recv()
        for cp in first + passed:
            cp.wait_send()
        mine.wait()

    return pl.pallas_call(
        body, name=name,
        out_shape=jax.ShapeDtypeStruct((N_DEV,) + shard.shape, shard.dtype),
        in_specs=[ANY], out_specs=ANY,
        scratch_shapes=[pltpu.SemaphoreType.DMA((N_DEV - 1,)), pltpu.SemaphoreType.DMA((N_DEV - 1,)),
                        pltpu.SemaphoreType.DMA],
    )(shard)


N_CHIPS = 4
CORES = 2


def _pair_exchange(blocks, name):
    n = len(blocks)

    def body(*refs):
        src, mine, theirs = refs[:n], refs[n:2 * n], refs[2 * n:3 * n]
        send_sems, recv_sems, local_sems = refs[3 * n:]
        x, y, c = _mesh_place()
        sibling = (x, y, 1 - c)
        local = [pltpu.make_async_copy(src[t].at[:, c], mine[t], local_sems.at[t]) for t in range(n)]
        remote = [pltpu.make_async_remote_copy(
            src_ref=src[t].at[:, 1 - c], dst_ref=theirs[t], send_sem=send_sems.at[t], recv_sem=recv_sems.at[t],
            device_id=sibling, device_id_type=MESH) for t in range(n)]
        for cp in remote + local:
            cp.start()
        for cp in remote:
            cp.wait_recv()
        for cp in remote:
            cp.wait_send()
        for cp in local:
            cp.wait()

    half = [jax.ShapeDtypeStruct((N_CHIPS,) + b.shape[2:], b.dtype) for b in blocks]
    return pl.pallas_call(
        body, name=name, out_shape=half + half,
        in_specs=[ANY] * n, out_specs=[ANY] * (2 * n),
        scratch_shapes=[pltpu.SemaphoreType.DMA((n,)), pltpu.SemaphoreType.DMA((n,)), pltpu.SemaphoreType.DMA((n,))],
    )(*blocks)


def _chip_exchange(blocks, name):
    n = len(blocks)
    n_peers = N_CHIPS - 1

    def body(*refs):
        src, dst = refs[:n], refs[n:2 * n]
        send_sems, recv_sems, local_sems = refs[2 * n:]
        x, y, c = _mesh_place()
        me = 2 * x + y
        local = [pltpu.make_async_copy(src[t].at[me], dst[t].at[me], local_sems.at[t]) for t in range(n)]
        for cp in local:
            cp.start()
        sends, recvs = [], []
        for k in range(1, N_CHIPS):
            px, py = (1 - x if k & 2 else x), (1 - y if k & 1 else y)
            q = 2 * px + py
            for t in range(n):
                s = t * n_peers + k - 1
                out = pltpu.make_async_remote_copy(
                    src_ref=src[t].at[q], dst_ref=dst[t].at[me], send_sem=send_sems.at[s], recv_sem=recv_sems.at[s],
                    device_id=(px, py, c), device_id_type=MESH)
                out.start()
                sends.append(out)
                recvs.append(pltpu.make_async_remote_copy(
                    src_ref=src[t].at[q], dst_ref=dst[t].at[q], send_sem=send_sems.at[s], recv_sem=recv_sems.at[s],
                    device_id=(px, py, c), device_id_type=MESH))
        for cp in recvs:
            cp.wait_recv()
        for cp in sends:
            cp.wait_send()
        for cp in local:
            cp.wait()

    return pl.pallas_call(
        body, name=name,
        out_shape=[jax.ShapeDtypeStruct(b.shape, b.dtype) for b in blocks],
        in_specs=[ANY] * n, out_specs=[ANY] * n,
        scratch_shapes=[pltpu.SemaphoreType.DMA((n * n_peers,)), pltpu.SemaphoreType.DMA((n * n_peers,)),
                        pltpu.SemaphoreType.DMA((n,))],
    )(*blocks)


def _pair_sum(a, b, name):
    n, r, c = a.shape
    spec = pl.BlockSpec((None, r, c), lambda q: (q, 0, 0))

    def body(a_ref, b_ref, o_ref):
        o_ref[...] = (a_ref[...].astype(F32) + b_ref[...].astype(F32)).astype(o_ref.dtype)

    return pl.pallas_call(
        body, name=name, grid=(n,), in_specs=[spec, spec], out_specs=spec,
        out_shape=jax.ShapeDtypeStruct(a.shape, a.dtype), compiler_params=_params("arbitrary"),
    )(a, b)


ADAMW_BLOCK_BYTES = 4 * 1024 * 1024


def _sum_adamw(parts, w, m, v, name):
    n_parts, r, c = parts.shape
    tr = r
    while n_parts * tr * c * parts.dtype.itemsize > ADAMW_BLOCK_BYTES and tr % (4 * SUBLANES) == 0:
        tr //= 2

    def body(p_ref, w_ref, m_ref, v_ref, g_ref, d_ref, nm_ref, nv_ref):
        g = p_ref[0].astype(F32)
        for s in range(1, n_parts):
            g = g + p_ref[s].astype(F32)
        nm = ADAM_B1 * m_ref[...] + (1.0 - ADAM_B1) * g
        nv = ADAM_B2 * v_ref[...] + (1.0 - ADAM_B2) * (g * g)
        m_hat = nm / (1.0 - ADAM_B1 ** ADAM_STEP)
        v_hat = nv / (1.0 - ADAM_B2 ** ADAM_STEP)
        g_ref[...] = g
        d_ref[...] = -ADAM_LR * (m_hat / (jnp.sqrt(v_hat) + ADAM_EPS) + ADAM_WD * w_ref[...])
        nm_ref[...] = nm
        nv_ref[...] = nv

    row = pl.BlockSpec((tr, c), lambda i: (i, 0))
    return pl.pallas_call(
        body, name=name, grid=(r // tr,),
        in_specs=[pl.BlockSpec((n_parts, tr, c), lambda i: (0, i, 0)), row, row, row],
        out_specs=[row, row, row, row],
        out_shape=[jax.ShapeDtypeStruct((r, c), F32)] * 4,
        compiler_params=_params("arbitrary"),
    )(parts, w, m, v)


LANES = 128
MATRICES = ("w_in", "w_out", "ffn_up_w", "ffn_down_w")
COLUMN_SHARDED = ("w_in", "lru_conv_w", "ffn_up_w", "ffn_conv_w")
REPLICATED = ("norm1_gain", "lru_conv_b", "lru_gate_a_w", "lru_gate_a_b", "lru_gate_x_w", "lru_gate_x_b", "lru_lambda",
              "lru_norm_gain", "ret_norm_gain", "norm2_gain", "ffn_conv_b", "final_norm_gain")
WEIGHTS = ("norm1_gain", "w_in", "lru_conv_w", "lru_conv_b", "lru_gate_a_w", "lru_gate_a_b", "lru_gate_x_w",
           "lru_gate_x_b", "lru_lambda", "lru_norm_gain", "ret_norm_gain", "w_out", "norm2_gain", "ffn_up_w",
           "ffn_conv_w", "ffn_conv_b", "ffn_down_w", "final_norm_gain")


def _rows(a, pad_to):
    a = a.reshape(-1, LANES)
    pad = (-a.shape[0]) % pad_to
    return jnp.pad(a, ((0, pad), (0, 0))) if pad else a


def _pack(arrays, pad_to):
    rows, layout, at = [], [], 0
    for a in arrays:
        r = _rows(a, pad_to)
        layout.append((at, a.size // LANES, a.shape))
        rows.append(r)
        at += r.shape[0]
    return jnp.concatenate(rows, axis=0), layout


def _unpack(packed, layout):
    lead = packed.shape[:-2]
    return [packed[..., at:at + n, :].reshape(lead + shape) for at, n, shape in layout]


def _to_blocks(full, name):
    r, c = full.shape
    if name in COLUMN_SHARDED:
        return full.reshape(r, N_DEV, c // N_DEV).transpose(1, 0, 2)
    return full.reshape(N_DEV, r // N_DEV, c)


def _from_blocks(blocks, name):
    _, r, c = blocks.shape
    if name in COLUMN_SHARDED:
        return blocks.transpose(1, 0, 2).reshape(r, N_DEV * c)
    return blocks.reshape(N_DEV * r, c)


def _f32_as_bf16(a):
    return lax.bitcast_convert_type(a, BF16).reshape(a.shape[:-1] + (2 * a.shape[-1],))


def _bf16_as_f32(a):
    return lax.bitcast_convert_type(a.reshape(a.shape[:-1] + (a.shape[-1] // 2, 2)), F32)


def kernel(x, norm1_gain, w_in, lru_conv_w, lru_conv_b, lru_gate_a_w, lru_gate_a_b, lru_gate_x_w, lru_gate_x_b, lru_lambda, lru_norm_gain, ret_norm_gain, w_out, norm2_gain, ffn_up_w, ffn_conv_w, ffn_conv_b, ffn_down_w, final_norm_gain, loss_target, m_norm1_gain, m_w_in, m_lru_conv_w, m_lru_conv_b, m_lru_gate_a_w, m_lru_gate_a_b, m_lru_gate_x_w, m_lru_gate_x_b, m_lru_lambda, m_lru_norm_gain, m_ret_norm_gain, m_w_out, m_norm2_gain, m_ffn_up_w, m_ffn_conv_w, m_ffn_conv_b, m_ffn_down_w, m_final_norm_gain, v_norm1_gain, v_w_in, v_lru_conv_w, v_lru_conv_b, v_lru_gate_a_w, v_lru_gate_a_b, v_lru_gate_x_w, v_lru_gate_x_b, v_lru_lambda, v_lru_norm_gain, v_ret_norm_gain, v_w_out, v_norm2_gain, v_ffn_up_w, v_ffn_conv_w, v_ffn_conv_b, v_ffn_down_w, v_final_norm_gain):
    args = dict(locals())
    given = {n: args[n] for n in WEIGHTS}
    mom_m = {n: args["m_" + n] for n in WEIGHTS}
    mom_v = {n: args["v_" + n] for n in WEIGHTS}
    out_shape = {n: given[n].shape for n in WEIGHTS}

    def plain(a, name):
        if a.ndim <= 2:
            return a.reshape(1, -1)
        return a[0]

    shard = {n: plain(given[n], n) for n in WEIGHTS}

    payload = [shard[n].astype(BF16) for n in MATRICES] + [_f32_as_bf16(shard[n].reshape(1, -1)) for n in
                                                             ("lru_conv_w", "ffn_conv_w")]
    packed, layout = _pack(payload, 2 * SUBLANES)
    gathered = _unpack(_all_gather(packed, "weights_all_gather"), layout)
    w = {n: shard[n] for n in REPLICATED}
    for n, blocks in zip(MATRICES, gathered[:len(MATRICES)]):
        w[n] = _from_blocks(blocks, n)
    for n, blocks in zip(("lru_conv_w", "ffn_conv_w"), gathered[len(MATRICES):]):
        w[n] = _from_blocks(_bf16_as_f32(blocks).reshape((N_DEV,) + shard[n].shape), n)

    loss, grad_x, g = _local_step(x[0], loss_target[0], w)
    loss = lax.psum(loss, ("x", "y", "c"))

    conv_blocks = jnp.concatenate([_to_blocks(g[n], n).reshape(N_DEV, -1) for n in ("lru_conv_w", "ffn_conv_w")], axis=1)
    conv_rows = conv_blocks.shape[1] // LANES
    conv_pad = (-conv_rows) % (2 * SUBLANES)
    conv_blocks = jnp.pad(conv_blocks.reshape(N_DEV, conv_rows, LANES), ((0, 0), (0, conv_pad), (0, 0)))
    blocks = [_to_blocks(g[n], n) for n in MATRICES] + [conv_blocks.astype(GRAD_DTYPE)]
    halves = _pair_exchange([b.reshape((N_CHIPS, CORES) + b.shape[1:]) for b in blocks], "grads_pair_exchange")
    names = MATRICES + ("conv",)
    sums = [_pair_sum(mine, theirs, "grads_pair_sum_" + n)
            for n, mine, theirs in zip(names, halves[:len(blocks)], halves[len(blocks):])]
    landed = _chip_exchange(sums, "grads_chip_exchange")
    rep_packed, rep_layout = _pack([g[n] for n in REPLICATED], SUBLANES)
    rep_parts = _all_gather(rep_packed, "small_grads_all_gather")

    res = {}
    for n, parts in zip(MATRICES, landed[:len(MATRICES)]):
        res[n] = _sum_adamw(parts, shard[n], plain(mom_m[n], n), plain(mom_v[n], n), "adamw_" + n)

    def conv_rows_of(d):
        flat = jnp.concatenate([plain(d[n], n).reshape(-1) for n in ("lru_conv_w", "ffn_conv_w")])
        return jnp.pad(flat.reshape(conv_rows, LANES), ((0, conv_pad), (0, 0)))

    conv_res = _sum_adamw(landed[-1], conv_rows_of(given), conv_rows_of(mom_m), conv_rows_of(mom_v), "adamw_conv")
    n_lru = shard["lru_conv_w"].size
    for n, lo, hi in (("lru_conv_w", 0, n_lru), ("ffn_conv_w", n_lru, n_lru + shard["ffn_conv_w"].size)):
        res[n] = [r.reshape(-1)[lo:hi].reshape(shard[n].shape) for r in conv_res]
    rep_res = _sum_adamw(rep_parts, _pack([shard[n] for n in REPLICATED], SUBLANES)[0],
                         _pack([plain(mom_m[n], n) for n in REPLICATED], SUBLANES)[0],
                         _pack([plain(mom_v[n], n) for n in REPLICATED], SUBLANES)[0], "adamw_replicated")
    for k in range(4):
        for n, a in zip(REPLICATED, _unpack(rep_res[k], rep_layout)):
            res.setdefault(n, [None] * 4)[k] = a

    outs = [loss, grad_x[None]]
    for k in range(4):
        outs += [res[n][k].reshape(out_shape[n]) for n in WEIGHTS]
    return tuple(outs)
```

```python
import functools
import math

import numpy as np
import jax
import jax.numpy as jnp
from jax import lax
from jax.experimental import pallas as pl
from jax.experimental.pallas import tpu as pltpu

F32 = jnp.float32
BF16 = jnp.bfloat16
MXU_DTYPE = jnp.bfloat16
GRAD_DTYPE = jnp.bfloat16

N_DEV = 8
D_MODEL = 1024
D_LRU = 512
LRU_HEADS = 8
LRU_HEAD_DIM = 64
LRU_CONV = 4
LRU_C = 8.0
D_RET = 512
RET_HEADS = 4
RET_HEAD_DIM = 128
RET_CHUNK = 128
ROPE_BASE = 10000.0
D_IN = 3072
D_FF = 3072
FFN_CONV = 3
NORM_EPS = 1e-6

ADAM_LR = 0.001
ADAM_B1 = 0.9
ADAM_B2 = 0.999
ADAM_EPS = 1e-08
ADAM_WD = 0.01
ADAM_STEP = 10

SUBLANES = 8
VMEM_LIMIT = 48 * 1024 * 1024


def _params(*sem):
    return pltpu.CompilerParams(dimension_semantics=sem, vmem_limit_bytes=VMEM_LIMIT)


def _dot(a, b):
    return jnp.dot(a.astype(MXU_DTYPE), b.astype(MXU_DTYPE), preferred_element_type=F32)


def _dot_nt(a, b):
    return lax.dot_general(a.astype(MXU_DTYPE), b.astype(MXU_DTYPE), (((1,), (1,)), ((), ())),
                           preferred_element_type=F32)


def _dot_tn(a, b):
    return lax.dot_general(a.astype(MXU_DTYPE), b.astype(MXU_DTYPE), (((0,), (0,)), ((), ())),
                           preferred_element_type=F32)


def _sigmoid(x):
    return 1.0 / (1.0 + jnp.exp(-x))


_GELU_C = math.sqrt(2.0 / math.pi)


def _gelu_parts(x):
    x2 = x * x
    t = jnp.tanh(_GELU_C * (x + 0.044715 * (x2 * x)))
    cdf = 0.5 * (1.0 + t)
    g = x * cdf
    dg = cdf + 0.5 * x * (1.0 - t * t) * (_GELU_C * (1.0 + 3.0 * 0.044715 * x2))
    return g, dg


def _gelu(x):
    t = jnp.tanh(_GELU_C * (x + 0.044715 * (x * x * x)))
    return x * (0.5 * (1.0 + t))


def _neg_expm1(x):
    series = x * (1.0 + x * (1.0 / 2.0) * (1.0 + x * (1.0 / 3.0) * (1.0 + x * (1.0 / 4.0) * (
        1.0 + x * (1.0 / 5.0) * (1.0 + x * (1.0 / 6.0) * (1.0 + x * (1.0 / 7.0)))))))
    return jnp.where(x > -0.25, -series, 1.0 - jnp.exp(x))


def _softplus(x):
    return jnp.maximum(x, 0.0) + jnp.log1p(jnp.exp(-jnp.abs(x)))


def _bcast_row(x, r, rows=SUBLANES):
    return jnp.broadcast_to(x[r:r + 1, :], (rows, x.shape[1]))


def _colsum8(x):
    return jnp.broadcast_to(jnp.sum(x, axis=0, keepdims=True), (SUBLANES, x.shape[1]))


def _shift_down(prev8, tile, s):
    if s == 0:
        return tile
    ext = jnp.concatenate([prev8, tile], axis=0)
    return pltpu.roll(ext, s, 0)[SUBLANES:, :]


def _shift_up(tile, next8, s):
    if s == 0:
        return tile
    ext = jnp.concatenate([tile, next8], axis=0)
    n = ext.shape[0]
    return pltpu.roll(ext, n - s, 0)[:tile.shape[0], :]


def _group_scan(a, b, reverse):
    n = a.shape[0]
    row = lax.broadcasted_iota(jnp.int32, a.shape, 0) & (SUBLANES - 1)
    for s in (1, 2, 4):
        shift = (n - s) if reverse else s
        a_sh = pltpu.roll(a, shift, 0)
        b_sh = pltpu.roll(b, shift, 0)
        m = (row <= SUBLANES - 1 - s) if reverse else (row >= s)
        b = jnp.where(m, a * b_sh + b, b)
        a = jnp.where(m, a * a_sh, a)
    return a, b


def _carry_scan(a_ref, b_ref, out_ref, carry0, reverse):
    n_groups = a_ref.shape[0] // SUBLANES

    def body(i, carry):
        g = (n_groups - 1 - i) if reverse else i
        r0 = pl.multiple_of(g * SUBLANES, SUBLANES)
        hg = a_ref[pl.ds(r0, SUBLANES), :] * carry + b_ref[pl.ds(r0, SUBLANES), :]
        out_ref[pl.ds(r0, SUBLANES), :] = hg
        return _bcast_row(hg, 0 if reverse else SUBLANES - 1)

    return lax.fori_loop(0, n_groups, body, carry0)


def _rms_fwd(h, gain):
    rstd = lax.rsqrt(jnp.mean(h * h, axis=-1, keepdims=True) + NORM_EPS)
    n = h * rstd
    return n, rstd, n * gain


def _rms_bwd(dy, n, rstd, gain):
    dn = dy * gain
    dh = rstd * (dn - n * jnp.mean(dn * n, axis=-1, keepdims=True))
    return dh, _colsum8(dy * n)


def _halo_map(tile_rows, col):
    per = tile_rows // SUBLANES
    return lambda i: (jnp.maximum(i * per - 1, 0), col)


def _mm(a, b, name, out_dtype, nt, tm=512, tn=512):
    m, k = a.shape
    n = b.shape[0] if nt else b.shape[1]
    tm, tn = min(tm, m), min(tn, n)

    def body(a_ref, b_ref, o_ref):
        f = _dot_nt if nt else _dot
        o_ref[...] = f(a_ref[...], b_ref[...]).astype(o_ref.dtype)

    b_spec = pl.BlockSpec((tn, k), lambda j, i: (j, 0)) if nt else pl.BlockSpec((k, tn), lambda j, i: (0, j))
    return pl.pallas_call(
        body, name=name, grid=(n // tn, m // tm),
        in_specs=[pl.BlockSpec((tm, k), lambda j, i: (i, 0)), b_spec],
        out_specs=pl.BlockSpec((tm, tn), lambda j, i: (i, j)),
        out_shape=jax.ShapeDtypeStruct((m, n), out_dtype),
        compiler_params=_params("arbitrary", "arbitrary"),
    )(a, b)


def _mm_tn(a, b, name, tn=1024, tk=2048):
    t, m = a.shape
    n = b.shape[1]
    tm, tn, tk = min(1024, m), min(tn, n), min(tk, t)
    nk = t // tk

    def body(a_ref, b_ref, o_ref, acc):
        k = pl.program_id(2)

        @pl.when(k == 0)
        def _():
            acc[...] = jnp.zeros_like(acc)
        acc[...] += _dot_tn(a_ref[...], b_ref[...])

        @pl.when(k == nk - 1)
        def _():
            o_ref[...] = acc[...].astype(o_ref.dtype)

    return pl.pallas_call(
        body, name=name, grid=(m // tm, n // tn, nk),
        in_specs=[pl.BlockSpec((tk, tm), lambda i, j, k: (k, i)), pl.BlockSpec((tk, tn), lambda i, j, k: (k, j))],
        out_specs=pl.BlockSpec((tm, tn), lambda i, j, k: (i, j)),
        out_shape=jax.ShapeDtypeStruct((m, n), GRAD_DTYPE),
        scratch_shapes=[pltpu.VMEM((tm, tn), F32)],
        compiler_params=_params("arbitrary", "arbitrary", "arbitrary"),
    )(a, b)


def _inproj_fwd(x, g1, w_in):
    t = x.shape[0]
    tm = min(512, t)

    def body(x_ref, g_ref, w_ref, u_ref, p_ref):
        _, _, u = _rms_fwd(x_ref[...], g_ref[...])
        u = u.astype(MXU_DTYPE)
        u_ref[...] = u
        p_ref[...] = _dot(u, w_ref[...])

    return pl.pallas_call(
        body, name="inproj_fwd", grid=(t // tm,),
        in_specs=[pl.BlockSpec((tm, D_MODEL), lambda i: (i, 0)), pl.BlockSpec((1, D_MODEL), lambda i: (0, 0)),
                  _resident((D_MODEL, D_IN))],
        out_specs=[pl.BlockSpec((tm, D_MODEL), lambda i: (i, 0)), pl.BlockSpec((tm, D_IN), lambda i: (i, 0))],
        out_shape=[jax.ShapeDtypeStruct((t, D_MODEL), MXU_DTYPE), jax.ShapeDtypeStruct((t, D_IN), F32)],
        compiler_params=_params("arbitrary"),
    )(x, g1, w_in)


def _lru_gates(xc, wa, ba, wx, bx, sp):
    r = _sigmoid(_dot(xc, wa) + ba)
    ig = _sigmoid(_dot(xc, wx) + bx)
    log_a = (-LRU_C) * r * sp
    a = jnp.exp(log_a)
    m = jnp.sqrt(_neg_expm1(2.0 * log_a))
    return r, ig, a, m


def _lru_fwd(proj, conv_w, conv_b, wa, ba, wx, bx, lam, gain):
    t = proj.shape[0]
    tm = min(256, t)
    c = D_LRU

    def body(x_ref, xh_ref, g_ref, cw_ref, cb_ref, wa_ref, ba_ref, wx_ref, bx_ref, lam_ref, gain_ref,
             xc_ref, h_ref, y_ref, a_scr, b_scr, carry):
        i = pl.program_id(0)

        @pl.when(i == 0)
        def _():
            carry[...] = jnp.zeros_like(carry)

        x = x_ref[...]
        prev = jnp.where(i == 0, 0.0, xh_ref[...])
        cw = cw_ref[...]
        xc = cb_ref[...] + cw[LRU_CONV - 1:LRU_CONV, :] * x
        for k in range(LRU_CONV - 1):
            xc = xc + cw[k:k + 1, :] * _shift_down(prev, x, LRU_CONV - 1 - k)
        xc_ref[...] = xc
        sp = _softplus(-lam_ref[...])
        _, ig, a, m = _lru_gates(xc, wa_ref[...], ba_ref[...], wx_ref[...], bx_ref[...], sp)
        ga, gb = _group_scan(a, m * (ig * xc), reverse=False)
        a_scr[...] = ga
        b_scr[...] = gb
        carry[...] = _carry_scan(a_scr, b_scr, h_ref, carry[...], reverse=False)
        z = h_ref[...] * _gelu(g_ref[...])
        _, _, y = _rms_fwd(z, gain_ref[...])
        y_ref[...] = y.astype(y_ref.dtype)

    row = lambda i: (i, 0)
    full = lambda i: (0, 0)
    vec = pl.BlockSpec((1, c), full)
    return pl.pallas_call(
        body, name="lru_fwd", grid=(t // tm,),
        in_specs=[pl.BlockSpec((tm, c), row), pl.BlockSpec((SUBLANES, c), _halo_map(tm, 0)),
                  pl.BlockSpec((tm, c), lambda i: (i, 1)),
                  pl.BlockSpec((LRU_CONV, c), full), vec, pl.BlockSpec((c, c), full), vec,
                  pl.BlockSpec((c, c), full), vec, vec, vec],
        out_specs=[pl.BlockSpec((tm, c), row), pl.BlockSpec((tm, c), row), pl.BlockSpec((tm, c), row)],
        out_shape=[jax.ShapeDtypeStruct((t, c), F32), jax.ShapeDtypeStruct((t, c), F32),
                   jax.ShapeDtypeStruct((t, c), MXU_DTYPE)],
        scratch_shapes=[pltpu.VMEM((tm, c), F32), pltpu.VMEM((tm, c), F32), pltpu.VMEM((SUBLANES, c), F32)],
        compiler_params=_params("arbitrary"),
    )(proj, proj, proj, conv_w, conv_b, wa, ba, wx, bx, lam, gain)


def _ret_consts():
    c = RET_CHUNK
    log_g = jnp.log1p(-jnp.exp2(-5.0 - jnp.arange(RET_HEADS, dtype=F32)))
    idx = jnp.arange(c, dtype=F32)
    diff = idx[:, None] - idx[None, :]
    decay = jnp.where(diff[None] >= 0, jnp.exp(jnp.maximum(diff, 0.0)[None] * log_g[:, None, None]), 0.0)
    zeta = jnp.exp((c - 1 - idx)[None, :] * log_g[:, None])
    xi = jnp.exp((idx + 1.0)[None, :] * log_g[:, None])
    spread = lambda v: jnp.repeat(v.T, RET_HEAD_DIM, axis=1)
    log_g_np = np.log1p(-np.exp2(-5.0 - np.arange(RET_HEADS, dtype=np.float32))).astype(np.float32)
    g_chunk = [float(np.exp(np.float32(c) * lg)) for lg in log_g_np]
    return decay, spread(xi), spread(zeta), g_chunk


def _rope_tables(t):
    pos = jnp.arange(t, dtype=F32)
    inv_freq = ROPE_BASE ** (-jnp.arange(0, RET_HEAD_DIM, 2, dtype=F32) / RET_HEAD_DIM)
    ang = pos[:, None] * inv_freq[None, :]
    cos, sin = jnp.cos(ang), jnp.sin(ang)
    return jnp.concatenate([cos, cos], axis=-1), jnp.concatenate([-sin, sin], axis=-1)


def _rope(x, cos2, sin_signed):
    return x * cos2 + pltpu.roll(x, RET_HEAD_DIM // 2, 1) * sin_signed


def _rope_bwd(d, cos2, sin_signed):
    return d * cos2 + pltpu.roll(d * sin_signed, RET_HEAD_DIM // 2, 1)


RET_SCALE = RET_HEAD_DIM ** -0.5


def _ret_fwd(proj, cos2, sin_signed, gain):
    t = proj.shape[0]
    c, d, nh = RET_CHUNK, RET_HEAD_DIM, RET_HEADS
    n_chunks = t // c
    decay, xi, zeta, g_chunk = _ret_consts()

    def body(qk_ref, vg_ref, cos_ref, sin_ref, dec_ref, xi_ref, zeta_ref, gain_ref, o_ref, y_ref, st_ref, state):
        @pl.when(pl.program_id(0) == 0)
        def _():
            state[...] = jnp.zeros_like(state)

        cos2, sin_s = cos_ref[...], sin_ref[...]
        for h in range(nh):
            lo = h * d
            q = _rope(qk_ref[:, lo:lo + d], cos2, sin_s)
            k = _rope(qk_ref[:, D_RET + lo:D_RET + lo + d], cos2, sin_s) * RET_SCALE
            v = vg_ref[:, lo:lo + d]
            g = vg_ref[:, D_RET + lo:D_RET + lo + d]
            s_prev = state[h]
            st_ref[0, h] = s_prev
            scores = _dot_nt(q, k) * dec_ref[h]
            o = _dot(scores, v) + _dot(q * xi_ref[:, lo:lo + d], s_prev)
            state[h] = s_prev * g_chunk[h] + _dot_tn(k * zeta_ref[:, lo:lo + d], v)
            o_ref[:, lo:lo + d] = o
            mu = jnp.mean(o, axis=-1, keepdims=True)
            oc = o - mu
            on = oc * lax.rsqrt(jnp.mean(oc * oc, axis=-1, keepdims=True) + NORM_EPS)
            y_ref[:, lo:lo + d] = (on * gain_ref[:, lo:lo + d] * (g * _sigmoid(g))).astype(y_ref.dtype)

    full2 = lambda i: (0, 0)
    return pl.pallas_call(
        body, name="ret_fwd", grid=(n_chunks,),
        in_specs=[pl.BlockSpec((c, 2 * D_RET), lambda i: (i, 1)), pl.BlockSpec((c, 2 * D_RET), lambda i: (i, 2)),
                  pl.BlockSpec((c, d), lambda i: (i, 0)), pl.BlockSpec((c, d), lambda i: (i, 0)),
                  pl.BlockSpec((nh, c, c), lambda i: (0, 0, 0)), pl.BlockSpec((c, D_RET), full2),
                  pl.BlockSpec((c, D_RET), full2), pl.BlockSpec((1, D_RET), full2)],
        out_specs=[pl.BlockSpec((c, D_RET), lambda i: (i, 0)), pl.BlockSpec((c, D_RET), lambda i: (i, 0)),
                   pl.BlockSpec((1, nh, d, d), lambda i: (i, 0, 0, 0))],
        out_shape=[jax.ShapeDtypeStruct((t, D_RET), F32), jax.ShapeDtypeStruct((t, D_RET), MXU_DTYPE),
                   jax.ShapeDtypeStruct((n_chunks, nh, d, d), F32)],
        scratch_shapes=[pltpu.VMEM((nh, d, d), F32)],
        compiler_params=_params("arbitrary"),
    )(proj, proj, cos2, sin_signed, decay, xi, zeta, gain)


def _outproj_fwd(x, y_lru, y_ret, w_out, g2):
    t = x.shape[0]
    tm = min(512, t)

    def body(x_ref, yl_ref, yr_ref, w_ref, g_ref, h1_ref, u2_ref):
        h1 = x_ref[...] + _dot(yl_ref[...], w_ref[:D_LRU, :]) + _dot(yr_ref[...], w_ref[D_LRU:, :])
        h1_ref[...] = h1
        _, _, u = _rms_fwd(h1, g_ref[...])
        u2_ref[...] = u.astype(u2_ref.dtype)

    row = lambda i: (i, 0)
    return pl.pallas_call(
        body, name="outproj_fwd", grid=(t // tm,),
        in_specs=[pl.BlockSpec((tm, D_MODEL), row), pl.BlockSpec((tm, D_LRU), row), pl.BlockSpec((tm, D_RET), row),
                  pl.BlockSpec((D_MODEL, D_MODEL), lambda i: (0, 0)), pl.BlockSpec((1, D_MODEL), lambda i: (0, 0))],
        out_specs=[pl.BlockSpec((tm, D_MODEL), row), pl.BlockSpec((tm, D_MODEL), row)],
        out_shape=[jax.ShapeDtypeStruct((t, D_MODEL), F32), jax.ShapeDtypeStruct((t, D_MODEL), MXU_DTYPE)],
        compiler_params=_params("arbitrary"),
    )(x, y_lru, y_ret, w_out, g2)


FFN_TN = 768
FFN_NJ = D_FF // FFN_TN


def _halo_rows(dtype):
    return SUBLANES * (4 // jnp.dtype(dtype).itemsize)


def _ffn_conv(up_ref, halo_ref, cw_ref, cb_ref, first):
    x = up_ref[...].astype(F32)
    prev = jnp.where(first, 0.0, halo_ref[...].astype(F32)[-SUBLANES:, :])
    cw = cw_ref[...]
    taps = [_shift_down(prev, x, FFN_CONV - 1 - k) for k in range(FFN_CONV)]
    y = cb_ref[...] + cw[FFN_CONV - 1:FFN_CONV, :] * x
    for k in range(FFN_CONV - 1):
        y = y + cw[k:k + 1, :] * taps[k]
    return y, taps


def _ffn_fwd(up, conv_w, conv_b, w_down, h1, gf, target):
    t = up.shape[0]
    tm = min(256, t)
    tn, nj = FFN_TN, FFN_NJ

    def body(ua_ref, uah_ref, uv_ref, uvh_ref, cwa_ref, cwv_ref, cba_ref, cbv_ref, wd_ref, h1_ref, gf_ref, tg_ref,
             act_ref, dh_ref, dhb_ref, dgf_ref, loss_ref, acc):
        i, j = pl.program_id(0), pl.program_id(1)

        @pl.when((i == 0) & (j == 0))
        def _():
            dgf_ref[...] = jnp.zeros_like(dgf_ref)
            loss_ref[...] = jnp.zeros_like(loss_ref)

        @pl.when(j == 0)
        def _():
            acc[...] = jnp.zeros_like(acc)

        a, _ = _ffn_conv(ua_ref, uah_ref, cwa_ref, cba_ref, i == 0)
        v, _ = _ffn_conv(uv_ref, uvh_ref, cwv_ref, cbv_ref, i == 0)
        act = (_gelu(a) * v).astype(act_ref.dtype)
        act_ref[...] = act
        acc[...] += _dot(act, wd_ref[...])

        @pl.when(j == nj - 1)
        def _():
            n, rstd, y = _rms_fwd(h1_ref[...] + acc[...], gf_ref[...])
            err = y - tg_ref[...]
            loss_ref[...] += (0.5 / D_MODEL) * jnp.sum(err * err)
            dh, dgf = _rms_bwd(err * (1.0 / D_MODEL), n, rstd, gf_ref[...])
            dgf_ref[...] += dgf
            dh_ref[...] = dh
            dhb_ref[...] = dh.astype(dhb_ref.dtype)

    hb = _halo_rows(up.dtype)
    per = tm // hb
    halo = lambda off: (lambda i, j: (jnp.maximum(i * per - 1, 0), j + off))
    row = lambda i, j: (i, 0)
    const = lambda i, j: (0, 0)
    return pl.pallas_call(
        body, name="ffn_fwd", grid=(t // tm, nj),
        in_specs=[pl.BlockSpec((tm, tn), lambda i, j: (i, j)), pl.BlockSpec((hb, tn), halo(0)),
                  pl.BlockSpec((tm, tn), lambda i, j: (i, j + nj)), pl.BlockSpec((hb, tn), halo(nj)),
                  pl.BlockSpec((FFN_CONV, tn), lambda i, j: (0, j)), pl.BlockSpec((FFN_CONV, tn), lambda i, j: (0, j + nj)),
                  pl.BlockSpec((1, tn), lambda i, j: (0, j)), pl.BlockSpec((1, tn), lambda i, j: (0, j + nj)),
                  pl.BlockSpec((tn, D_MODEL), lambda i, j: (j, 0)),
                  pl.BlockSpec((tm, D_MODEL), row), pl.BlockSpec((1, D_MODEL), const), pl.BlockSpec((tm, D_MODEL), row)],
        out_specs=[pl.BlockSpec((tm, tn), lambda i, j: (i, j)), pl.BlockSpec((tm, D_MODEL), row),
                   pl.BlockSpec((tm, D_MODEL), row), pl.BlockSpec((SUBLANES, D_MODEL), const),
                   pl.BlockSpec((SUBLANES, 128), const)],
        out_shape=[jax.ShapeDtypeStruct((t, D_FF), MXU_DTYPE), jax.ShapeDtypeStruct((t, D_MODEL), F32),
                   jax.ShapeDtypeStruct((t, D_MODEL), MXU_DTYPE), jax.ShapeDtypeStruct((SUBLANES, D_MODEL), F32),
                   jax.ShapeDtypeStruct((SUBLANES, 128), F32)],
        scratch_shapes=[pltpu.VMEM((tm, D_MODEL), F32)],
        compiler_params=_params("arbitrary", "arbitrary"),
    )(up, up, up, up, conv_w, conv_w, conv_b, conv_b, w_down, h1, gf, target)


FFN_ACC_ROWS = SUBLANES * (FFN_CONV + 1)


def _ffn_bwd(dh2_b, w_down, up, conv_w, conv_b):
    t = up.shape[0]
    tm = min(256, t)
    tn, nj = FFN_TN, FFN_NJ
    ni = t // tm

    def conv_bwd(dy, taps, cw, acc_ref, carry_ref, dup_ref):
        nxt = carry_ref[...]
        carry_ref[...] = dy[:SUBLANES, :]
        dx = cw[FFN_CONV - 1:FFN_CONV, :] * dy
        for k in range(FFN_CONV - 1):
            dx = dx + cw[k:k + 1, :] * _shift_up(dy, nxt, FFN_CONV - 1 - k)
        dup_ref[...] = dx.astype(dup_ref.dtype)
        for k in range(FFN_CONV):
            acc_ref[k * SUBLANES:(k + 1) * SUBLANES, :] += _colsum8(dy * taps[k])
        acc_ref[FFN_CONV * SUBLANES:, :] += _colsum8(dy)

    def body(dh_ref, wd_ref, ua_ref, uah_ref, uv_ref, uvh_ref, cwa_ref, cwv_ref, cba_ref, cbv_ref,
             dua_ref, duv_ref, acca_ref, accv_ref, carry_a, carry_v):
        i = pl.program_id(1)
        r = ni - 1 - i

        @pl.when(i == 0)
        def _():
            acca_ref[...] = jnp.zeros_like(acca_ref)
            accv_ref[...] = jnp.zeros_like(accv_ref)
            carry_a[...] = jnp.zeros_like(carry_a)
            carry_v[...] = jnp.zeros_like(carry_v)

        a, taps_a = _ffn_conv(ua_ref, uah_ref, cwa_ref, cba_ref, r == 0)
        v, taps_v = _ffn_conv(uv_ref, uvh_ref, cwv_ref, cbv_ref, r == 0)
        g, dg = _gelu_parts(a)
        dact = _dot_nt(dh_ref[...], wd_ref[...])
        conv_bwd(dact * v * dg, taps_a, cwa_ref[...], acca_ref, carry_a, dua_ref)
        conv_bwd(dact * g, taps_v, cwv_ref[...], accv_ref, carry_v, duv_ref)

    hb = _halo_rows(up.dtype)
    per = tm // hb
    rev = lambda off: (lambda j, i: (ni - 1 - i, j + off))
    halo = lambda off: (lambda j, i: (jnp.maximum((ni - 1 - i) * per - 1, 0), j + off))
    colj = lambda off: (lambda j, i: (0, j + off))
    return pl.pallas_call(
        body, name="ffn_bwd", grid=(nj, ni),
        in_specs=[pl.BlockSpec((tm, D_MODEL), lambda j, i: (ni - 1 - i, 0)), pl.BlockSpec((tn, D_MODEL), lambda j, i: (j, 0)),
                  pl.BlockSpec((tm, tn), rev(0)), pl.BlockSpec((hb, tn), halo(0)),
                  pl.BlockSpec((tm, tn), rev(nj)), pl.BlockSpec((hb, tn), halo(nj)),
                  pl.BlockSpec((FFN_CONV, tn), colj(0)), pl.BlockSpec((FFN_CONV, tn), colj(nj)),
                  pl.BlockSpec((1, tn), colj(0)), pl.BlockSpec((1, tn), colj(nj))],
        out_specs=[pl.BlockSpec((tm, tn), rev(0)), pl.BlockSpec((tm, tn), rev(0)),
                   pl.BlockSpec((FFN_ACC_ROWS, tn), colj(0)), pl.BlockSpec((FFN_ACC_ROWS, tn), colj(0))],
        out_shape=[jax.ShapeDtypeStruct((t, D_FF), MXU_DTYPE), jax.ShapeDtypeStruct((t, D_FF), MXU_DTYPE),
                   jax.ShapeDtypeStruct((FFN_ACC_ROWS, D_FF), F32), jax.ShapeDtypeStruct((FFN_ACC_ROWS, D_FF), F32)],
        scratch_shapes=[pltpu.VMEM((SUBLANES, tn), F32), pltpu.VMEM((SUBLANES, tn), F32)],
        compiler_params=_params("arbitrary", "arbitrary"),
    )(dh2_b, w_down, up, up, up, up, conv_w, conv_w, conv_b, conv_b)


def _resident(shape):
    return pl.BlockSpec(shape, lambda *_: (0,) * len(shape), pipeline_mode=pl.Buffered(1))


def _norm_bwd_matmul(parts, w, h, gain, d_res, name):
    t = h.shape[0]
    tm = min(256, t)
    n_parts = len(parts)
    widths = [a.shape[1] for a in parts]
    assert sum(widths) == w.shape[1]

    def body(*refs):
        a_refs = refs[:n_parts]
        w_ref, h_ref, g_ref, dres_ref, dh_ref, dhb_ref, dg_ref = refs[n_parts:]

        @pl.when(pl.program_id(0) == 0)
        def _():
            dg_ref[...] = jnp.zeros_like(dg_ref)

        du, lo = None, 0
        for a_ref, width in zip(a_refs, widths):
            term = _dot_nt(a_ref[...], w_ref[:, lo:lo + width])
            du = term if du is None else du + term
            lo += width
        n, rstd, _ = _rms_fwd(h_ref[...], g_ref[...])
        dh, dg = _rms_bwd(du, n, rstd, g_ref[...])
        dh = dh + dres_ref[...]
        dg_ref[...] += dg
        dh_ref[...] = dh
        dhb_ref[...] = dh.astype(dhb_ref.dtype)

    row = lambda i: (i, 0)
    const = lambda i: (0, 0)
    return pl.pallas_call(
        body, name=name, grid=(t // tm,),
        in_specs=[pl.BlockSpec((tm, width), row) for width in widths] + [
            _resident(w.shape), pl.BlockSpec((tm, D_MODEL), row), pl.BlockSpec((1, D_MODEL), const),
            pl.BlockSpec((tm, D_MODEL), row)],
        out_specs=[pl.BlockSpec((tm, D_MODEL), row), pl.BlockSpec((tm, D_MODEL), row),
                   pl.BlockSpec((SUBLANES, D_MODEL), const)],
        out_shape=[jax.ShapeDtypeStruct((t, D_MODEL), F32), jax.ShapeDtypeStruct((t, D_MODEL), MXU_DTYPE),
                   jax.ShapeDtypeStruct((SUBLANES, D_MODEL), F32)],
        compiler_params=_params("arbitrary"),
    )(*parts, w, h, gain, d_res)


def _ret_bwd(proj, cos2, sin_signed, gain, o, states, dmixed):
    t = proj.shape[0]
    c, d, nh = RET_CHUNK, RET_HEAD_DIM, RET_HEADS
    n_chunks = t // c
    decay, xi, zeta, g_chunk = _ret_consts()

    def body(qk_ref, vg_ref, cos_ref, sin_ref, dec_ref, xi_ref, zeta_ref, gain_ref, o_ref, st_ref, dy_ref,
             dp_ref, dgain_ref, gstate):
        @pl.when(pl.program_id(0) == 0)
        def _():
            gstate[...] = jnp.zeros_like(gstate)
            dgain_ref[...] = jnp.zeros_like(dgain_ref)

        cos2, sin_s = cos_ref[...], sin_ref[...]
        for h in range(nh):
            lo = h * d
            q = _rope(qk_ref[:, lo:lo + d], cos2, sin_s)
            k = _rope(qk_ref[:, D_RET + lo:D_RET + lo + d], cos2, sin_s) * RET_SCALE
            v = vg_ref[:, lo:lo + d]
            g = vg_ref[:, D_RET + lo:D_RET + lo + d]
            gain_h = gain_ref[:, lo:lo + d]
            xi_h, zeta_h, dec = xi_ref[:, lo:lo + d], zeta_ref[:, lo:lo + d], dec_ref[h]
            dy = dy_ref[:, lo:lo + d]
            sg = _sigmoid(g)
            o_h = o_ref[:, lo:lo + d]
            oc = o_h - jnp.mean(o_h, axis=-1, keepdims=True)
            rstd = lax.rsqrt(jnp.mean(oc * oc, axis=-1, keepdims=True) + NORM_EPS)
            on = oc * rstd
            dp_ref[:, 3 * D_RET + lo:3 * D_RET + lo + d] = (
                dy * on * gain_h * (sg * (1.0 + g * (1.0 - sg)))).astype(dp_ref.dtype)
            don_g = dy * (g * sg)
            dgain_ref[:, lo:lo + d] += _colsum8(don_g * on)
            don = don_g * gain_h
            do = rstd * (don - jnp.mean(don, axis=-1, keepdims=True) - on * jnp.mean(don * on, axis=-1, keepdims=True))
            s_prev = st_ref[0, h]
            g_next = gstate[h]
            p = _dot_nt(q, k) * dec
            dpm = _dot_nt(do, v) * dec
            dq = _dot(dpm, k) + _dot_nt(do, s_prev) * xi_h
            dk = _dot_tn(dpm, q) + _dot_nt(v, g_next) * zeta_h
            dv = _dot_tn(p, do) + _dot(k * zeta_h, g_next)
            gstate[h] = g_next * g_chunk[h] + _dot_tn(q * xi_h, do)
            dp_ref[:, lo:lo + d] = _rope_bwd(dq, cos2, sin_s).astype(dp_ref.dtype)
            dp_ref[:, D_RET + lo:D_RET + lo + d] = _rope_bwd(dk * RET_SCALE, cos2, sin_s).astype(dp_ref.dtype)
            dp_ref[:, 2 * D_RET + lo:2 * D_RET + lo + d] = dv.astype(dp_ref.dtype)

    rev = lambda col: (lambda i: (n_chunks - 1 - i, col))
    full2 = lambda i: (0, 0)
    return pl.pallas_call(
        body, name="ret_bwd", grid=(n_chunks,),
        in_specs=[pl.BlockSpec((c, 2 * D_RET), rev(1)), pl.BlockSpec((c, 2 * D_RET), rev(2)),
                  pl.BlockSpec((c, d), rev(0)), pl.BlockSpec((c, d), rev(0)),
                  pl.BlockSpec((nh, c, c), lambda i: (0, 0, 0)), pl.BlockSpec((c, D_RET), full2),
                  pl.BlockSpec((c, D_RET), full2), pl.BlockSpec((1, D_RET), full2),
                  pl.BlockSpec((c, D_RET), rev(0)), pl.BlockSpec((1, nh, d, d), lambda i: (n_chunks - 1 - i, 0, 0, 0)),
                  pl.BlockSpec((c, D_RET), rev(1))],
        out_specs=[pl.BlockSpec((c, 4 * D_RET), rev(0)), pl.BlockSpec((SUBLANES, D_RET), full2)],
        out_shape=[jax.ShapeDtypeStruct((t, 4 * D_RET), MXU_DTYPE), jax.ShapeDtypeStruct((SUBLANES, D_RET), F32)],
        scratch_shapes=[pltpu.VMEM((nh, d, d), F32)],
        compiler_params=_params("arbitrary"),
    )(proj, proj, cos2, sin_signed, decay, xi, zeta, gain, o, states, dmixed)


LRU_ACC = {"conv_w": 0, "conv_b": LRU_CONV, "gate_a_b": LRU_CONV + 1, "gate_x_b": LRU_CONV + 2,
           "lambda": LRU_CONV + 3, "norm_gain": LRU_CONV + 4}
LRU_ACC_ROWS = SUBLANES * (LRU_CONV + 5)


def _lru_bwd(proj, xc_all, h_all, dmixed, conv_w, wa, ba, wx, bx, lam, gain):
    t = proj.shape[0]
    tm = min(256, t)
    c = D_LRU
    ni = t // tm

    def body(x_ref, xh_ref, g_ref, xc_ref, h_ref, hh_ref, dy_ref, cw_ref, wa_ref, ba_ref, wx_ref, bx_ref, lam_ref,
             gain_ref, dp_ref, acc_ref, dwa_ref, dwx_ref, a_scr, b_scr, mu_scr, carry_mu, carry_dxc):
        i = pl.program_id(0)
        r = ni - 1 - i

        @pl.when(i == 0)
        def _():
            acc_ref[...] = jnp.zeros_like(acc_ref)
            dwa_ref[...] = jnp.zeros_like(dwa_ref)
            dwx_ref[...] = jnp.zeros_like(dwx_ref)
            carry_mu[...] = jnp.zeros_like(carry_mu)
            carry_dxc[...] = jnp.zeros_like(carry_dxc)

        def add(name, val, k=0):
            lo = (LRU_ACC[name] + k) * SUBLANES
            acc_ref[lo:lo + SUBLANES, :] += _colsum8(val)

        xc, h = xc_ref[...], h_ref[...]
        lam_v = lam_ref[...]
        sp = _softplus(-lam_v)
        rg, ig, a, m = _lru_gates(xc, wa_ref[...], ba_ref[...], wx_ref[...], bx_ref[...], sp)
        gl, dgl = _gelu_parts(g_ref[...])
        zn, rstd, _ = _rms_fwd(h * gl, gain_ref[...])
        dy = dy_ref[...]
        dz, dgain = _rms_bwd(dy, zn, rstd, gain_ref[...])
        lo = LRU_ACC["norm_gain"] * SUBLANES
        acc_ref[lo:lo + SUBLANES, :] += dgain
        dp_ref[:, c:] = (dz * h * dgl).astype(dp_ref.dtype)
        dh = dz * gl
        ga, gb = _group_scan(a, a * dh, reverse=True)
        a_scr[...] = ga
        b_scr[...] = gb
        mu_next_tile = carry_mu[...]
        carry_mu[...] = _carry_scan(a_scr, b_scr, mu_scr, mu_next_tile, reverse=True)
        lam_t = dh + _shift_up(mu_scr[...], mu_next_tile, 1)
        h_prev = _shift_down(jnp.where(r == 0, 0.0, hh_ref[...]), h, 1)
        da = lam_t * h_prev
        dig = lam_t * m * xc
        dxc = lam_t * m * ig
        dlog_a = da * a - (lam_t * ig * xc) * (a * a) / m
        dpr = dlog_a * ((-LRU_C) * sp) * rg * (1.0 - rg)
        add("lambda", dlog_a * ((-LRU_C) * rg) * (-_sigmoid(-lam_v)))
        dpi = dig * ig * (1.0 - ig)
        add("gate_a_b", dpr)
        add("gate_x_b", dpi)
        dwa_ref[...] += _dot_tn(xc, dpr)
        dwx_ref[...] += _dot_tn(xc, dpi)
        dxc = dxc + _dot_nt(dpr, wa_ref[...]) + _dot_nt(dpi, wx_ref[...])
        add("conv_b", dxc)
        x = x_ref[...]
        prev = jnp.where(r == 0, 0.0, xh_ref[...])
        cw = cw_ref[...]
        nxt = carry_dxc[...]
        carry_dxc[...] = dxc[:SUBLANES, :]
        dx = cw[LRU_CONV - 1:LRU_CONV, :] * dxc
        for k in range(LRU_CONV - 1):
            dx = dx + cw[k:k + 1, :] * _shift_up(dxc, nxt, LRU_CONV - 1 - k)
        for k in range(LRU_CONV):
            add("conv_w", dxc * _shift_down(prev, x, LRU_CONV - 1 - k), k)
        dp_ref[:, :c] = dx.astype(dp_ref.dtype)

    per = tm // SUBLANES
    rev = lambda col: (lambda i: (ni - 1 - i, col))
    halo = lambda i: (jnp.maximum((ni - 1 - i) * per - 1, 0), 0)
    full = lambda i: (0, 0)
    vec = pl.BlockSpec((1, c), full)
    mat = pl.BlockSpec((c, c), full)
    return pl.pallas_call(
        body, name="lru_bwd", grid=(ni,),
        in_specs=[pl.BlockSpec((tm, c), rev(0)), pl.BlockSpec((SUBLANES, c), halo), pl.BlockSpec((tm, c), rev(1)),
                  pl.BlockSpec((tm, c), rev(0)), pl.BlockSpec((tm, c), rev(0)), pl.BlockSpec((SUBLANES, c), halo),
                  pl.BlockSpec((tm, c), rev(0)), pl.BlockSpec((LRU_CONV, c), full), mat, vec, mat, vec, vec, vec],
        out_specs=[pl.BlockSpec((tm, 2 * c), rev(0)), pl.BlockSpec((LRU_ACC_ROWS, c), full), mat, mat],
        out_shape=[jax.ShapeDtypeStruct((t, 2 * c), MXU_DTYPE), jax.ShapeDtypeStruct((LRU_ACC_ROWS, c), F32),
                   jax.ShapeDtypeStruct((c, c), F32), jax.ShapeDtypeStruct((c, c), F32)],
        scratch_shapes=[pltpu.VMEM((tm, c), F32), pltpu.VMEM((tm, c), F32), pltpu.VMEM((tm, c), F32),
                        pltpu.VMEM((SUBLANES, c), F32), pltpu.VMEM((SUBLANES, c), F32)],
        compiler_params=_params("arbitrary"),
    )(proj, proj, proj, xc_all, h_all, h_all, dmixed, conv_w, wa, ba, wx, bx, lam, gain)


def _block_diag(w):
    nh, d, _ = w.shape
    eye = jnp.eye(nh, dtype=w.dtype)
    return (w[:, :, None, :] * eye[:, None, :, None]).reshape(nh * d, nh * d)


def _diag_blocks(dense, nh):
    d = dense.shape[0] // nh
    blocks = dense.reshape(nh, d, nh, d)
    return jnp.stack([blocks[h, :, h, :] for h in range(nh)], axis=0)


def _local_step(x, target, w):
    t = x.shape[0]
    cos2, sin_signed = _rope_tables(t)
    wa = _block_diag(w["lru_gate_a_w"]).astype(MXU_DTYPE)
    wx = _block_diag(w["lru_gate_x_w"]).astype(MXU_DTYPE)

    u1, proj = _inproj_fwd(x, w["norm1_gain"], w["w_in"])
    xc, h_lru, y_lru = _lru_fwd(proj, w["lru_conv_w"], w["lru_conv_b"], wa, w["lru_gate_a_b"], wx, w["lru_gate_x_b"],
                                w["lru_lambda"], w["lru_norm_gain"])
    o_ret, y_ret, states = _ret_fwd(proj, cos2, sin_signed, w["ret_norm_gain"])
    h1, u2 = _outproj_fwd(x, y_lru, y_ret, w["w_out"], w["norm2_gain"])
    up = _mm(u2, w["ffn_up_w"], "ffn_up_fwd", MXU_DTYPE, nt=False, tn=FFN_TN)
    act, dh2, dh2_b, dgf, loss = _ffn_fwd(up, w["ffn_conv_w"], w["ffn_conv_b"], w["ffn_down_w"], h1,
                                          w["final_norm_gain"], target)

    g = {"final_norm_gain": dgf[0]}
    g["ffn_down_w"] = _mm_tn(act, dh2_b, "ffn_down_wgrad")
    dup_a, dup_v, acc_a, acc_v = _ffn_bwd(dh2_b, w["ffn_down_w"], up, w["ffn_conv_w"], w["ffn_conv_b"])
    acc = jnp.concatenate([acc_a, acc_v], axis=1)[::SUBLANES]
    g["ffn_conv_w"], g["ffn_conv_b"] = acc[:FFN_CONV], acc[FFN_CONV:]
    g["ffn_up_w"] = jnp.concatenate([_mm_tn(u2, dup_a, "ffn_up_wgrad_a"), _mm_tn(u2, dup_v, "ffn_up_wgrad_v")], axis=1)
    dh1, dh1_b, dg2 = _norm_bwd_matmul([dup_a, dup_v], w["ffn_up_w"], h1, w["norm2_gain"], dh2, "ffn_up_bwd")
    g["norm2_gain"] = dg2[:1]
    g["w_out"] = jnp.concatenate([_mm_tn(y_lru, dh1_b, "w_out_wgrad_lru"), _mm_tn(y_ret, dh1_b, "w_out_wgrad_ret")],
                                 axis=0)
    dmixed = _mm(dh1_b, w["w_out"], "outproj_bwd", F32, nt=True)
    dp_ret, dgain_ret = _ret_bwd(proj, cos2, sin_signed, w["ret_norm_gain"], o_ret, states, dmixed)
    g["ret_norm_gain"] = dgain_ret[:1]
    dp_lru, lru_acc, dwa, dwx = _lru_bwd(proj, xc, h_lru, dmixed, w["lru_conv_w"], wa, w["lru_gate_a_b"], wx,
                                         w["lru_gate_x_b"], w["lru_lambda"], w["lru_norm_gain"])
    lru_acc = lru_acc[::SUBLANES]
    g["lru_conv_w"] = lru_acc[:LRU_CONV]
    for name in ("conv_b", "gate_a_b", "gate_x_b", "lambda", "norm_gain"):
        g["lru_" + name] = lru_acc[LRU_ACC[name]:LRU_ACC[name] + 1]
    g["lru_gate_a_w"] = _diag_blocks(dwa, LRU_HEADS)
    g["lru_gate_x_w"] = _diag_blocks(dwx, LRU_HEADS)
    g["w_in"] = jnp.concatenate([_mm_tn(u1, dp_lru, "w_in_wgrad_lru"), _mm_tn(u1, dp_ret, "w_in_wgrad_ret")], axis=1)
    grad_x, _, dg1 = _norm_bwd_matmul([dp_lru, dp_ret], w["w_in"], x, w["norm1_gain"], dh1, "inproj_bwd")
    g["norm1_gain"] = dg1[:1]
    return loss[0, 0], grad_x, g


MESH = pl.DeviceIdType.MESH
ANY = pl.BlockSpec(memory_space=pl.ANY)


def _mesh_place():
    x, y, c = lax.axis_index("x"), lax.axis_index("y"), lax.axis_index("c")
    return x, y, c


def _logical(px, py, pc):
    return 4 * px + 2 * py + pc


def _all_gather(shard, name):
    def body(x_ref, out_ref, send_sems, recv_sems, local_sem):
        x, y, c = _mesh_place()
        me, sibling = (x, y, c), (x, y, 1 - c)
        chips = [(1 - x, y), (x, 1 - y), (1 - x, 1 - y)]

        def slot(place):
            return out_ref.at[_logical(*place)]

        def copy(k, block, to, src=None):
            return pltpu.make_async_remote_copy(
                src_ref=slot(block) if src is None else src, dst_ref=slot(block),
                send_sem=send_sems.at[k], recv_sem=recv_sems.at[k], device_id=to, device_id_type=MESH)

        mine = pltpu.make_async_copy(x_ref, slot(me), local_sem)
        mine.start()
        first = [copy(0, me, sibling, src=x_ref)]
        first += [copy(1 + j, me, (*chip, c), src=x_ref) for j, chip in enumerate(chips)]
        for cp in first:
            cp.start()
        passed = [copy(4 + j, (*chip, c), sibling) for j, chip in enumerate(chips)]
        for j, chip in enumerate(chips):
            copy(1 + j, (*chip, c), me).wait_recv()
            passed[j].start()
        copy(0, sibling, me).wait_recv()
        for j, chip in enumerate(chips):
            copy(4 + j, (*chip, 1 - c), me).wait_recv()
        for cp in first + passed:
            cp.wait_send()
        mine.wait()

    return pl.pallas_call(
        body, name=name,
        out_shape=jax.ShapeDtypeStruct((N_DEV,) + shard.shape, shard.dtype),
        in_specs=[ANY], out_specs=ANY,
        scratch_shapes=[pltpu.SemaphoreType.DMA((N_DEV - 1,)), pltpu.SemaphoreType.DMA((N_DEV - 1,)),
                        pltpu.SemaphoreType.DMA],
    )(shard)


N_CHIPS = 4
CORES = 2


def _pair_exchange(blocks, name):
    n = len(blocks)

    def body(*refs):
        src, theirs = refs[:n], refs[n:2 * n]
        send_sems, recv_sems = refs[2 * n:]
        x, y, c = _mesh_place()
        remote = [pltpu.make_async_remote_copy(
            src_ref=src[t].at[1 - c], dst_ref=theirs[t], send_sem=send_sems.at[t], recv_sem=recv_sems.at[t],
            device_id=(x, y, 1 - c), device_id_type=MESH) for t in range(n)]
        for cp in remote:
            cp.start()
        for cp in remote:
            cp.wait_recv()
        for cp in remote:
            cp.wait_send()

    return pl.pallas_call(
        body, name=name, out_shape=[jax.ShapeDtypeStruct(b.shape[1:], b.dtype) for b in blocks],
        in_specs=[ANY] * n, out_specs=[ANY] * n,
        scratch_shapes=[pltpu.SemaphoreType.DMA((n,)), pltpu.SemaphoreType.DMA((n,))],
    )(*blocks)


def _chip_exchange(blocks, name):
    n = len(blocks)
    n_peers = N_CHIPS - 1

    def body(*refs):
        src, dst = refs[:n], refs[n:2 * n]
        send_sems, recv_sems, local_sems = refs[2 * n:]
        x, y, c = _mesh_place()
        me = 2 * x + y
        local = [pltpu.make_async_copy(src[t].at[me], dst[t].at[me], local_sems.at[t]) for t in range(n)]
        for cp in local:
            cp.start()
        sends, recvs = [], []
        for k in range(1, N_CHIPS):
            px, py = (1 - x if k & 2 else x), (1 - y if k & 1 else y)
            q = 2 * px + py
            for t in range(n):
                s = t * n_peers + k - 1
                out = pltpu.make_async_remote_copy(
                    src_ref=src[t].at[q], dst_ref=dst[t].at[me], send_sem=send_sems.at[s], recv_sem=recv_sems.at[s],
                    device_id=(px, py, c), device_id_type=MESH)
                out.start()
                sends.append(out)
                recvs.append(pltpu.make_async_remote_copy(
                    src_ref=src[t].at[q], dst_ref=dst[t].at[q], send_sem=send_sems.at[s], recv_sem=recv_sems.at[s],
                    device_id=(px, py, c), device_id_type=MESH))
        for cp in recvs:
            cp.wait_recv()
        for cp in sends:
            cp.wait_send()
        for cp in local:
            cp.wait()

    return pl.pallas_call(
        body, name=name,
        out_shape=[jax.ShapeDtypeStruct(b.shape, b.dtype) for b in blocks],
        in_specs=[ANY] * n, out_specs=[ANY] * n,
        scratch_shapes=[pltpu.SemaphoreType.DMA((n * n_peers,)), pltpu.SemaphoreType.DMA((n * n_peers,)),
                        pltpu.SemaphoreType.DMA((n,))],
    )(*blocks)


def _pair_sum(core, a, b, name):
    n, r, c = b.shape
    spec = pl.BlockSpec((None, r, c), lambda q, core: (q, 0, 0))

    def body(core_ref, a_ref, b_ref, o_ref):
        o_ref[...] = (a_ref[...].astype(F32) + b_ref[...].astype(F32)).astype(o_ref.dtype)

    return pl.pallas_call(
        body, name=name,
        grid_spec=pltpu.PrefetchScalarGridSpec(
            num_scalar_prefetch=1, grid=(n,),
            in_specs=[pl.BlockSpec((None, None, r, c), lambda q, core: (core[0], q, 0, 0)), spec], out_specs=spec),
        out_shape=jax.ShapeDtypeStruct(b.shape, b.dtype), compiler_params=_params("arbitrary"),
    )(core, a, b)


ADAMW_BLOCK_BYTES = 4 * 1024 * 1024


def _sum_adamw(parts, w, m, v, name):
    n_parts, r, c = parts.shape
    tr = r
    while n_parts * tr * c * parts.dtype.itemsize > ADAMW_BLOCK_BYTES and tr % (4 * SUBLANES) == 0:
        tr //= 2

    def body(p_ref, w_ref, m_ref, v_ref, g_ref, d_ref, nm_ref, nv_ref):
        g = p_ref[0].astype(F32)
        for s in range(1, n_parts):
            g = g + p_ref[s].astype(F32)
        nm = ADAM_B1 * m_ref[...] + (1.0 - ADAM_B1) * g
        nv = ADAM_B2 * v_ref[...] + (1.0 - ADAM_B2) * (g * g)
        m_hat = nm / (1.0 - ADAM_B1 ** ADAM_STEP)
        v_hat = nv / (1.0 - ADAM_B2 ** ADAM_STEP)
        g_ref[...] = g
        d_ref[...] = -ADAM_LR * (m_hat / (jnp.sqrt(v_hat) + ADAM_EPS) + ADAM_WD * w_ref[...])
        nm_ref[...] = nm
        nv_ref[...] = nv

    row = pl.BlockSpec((tr, c), lambda i: (i, 0))
    return pl.pallas_call(
        body, name=name, grid=(r // tr,),
        in_specs=[pl.BlockSpec((n_parts, tr, c), lambda i: (0, i, 0)), row, row, row],
        out_specs=[row, row, row, row],
        out_shape=[jax.ShapeDtypeStruct((r, c), F32)] * 4,
        compiler_params=_params("arbitrary"),
    )(parts, w, m, v)


LANES = 128
MATRICES = ("w_in", "w_out", "ffn_up_w", "ffn_down_w")
COLUMN_SHARDED = ("w_in", "lru_conv_w", "ffn_up_w", "ffn_conv_w")
REPLICATED = ("norm1_gain", "lru_conv_b", "lru_gate_a_w", "lru_gate_a_b", "lru_gate_x_w", "lru_gate_x_b", "lru_lambda",
              "lru_norm_gain", "ret_norm_gain", "norm2_gain", "ffn_conv_b", "final_norm_gain")
WEIGHTS = ("norm1_gain", "w_in", "lru_conv_w", "lru_conv_b", "lru_gate_a_w", "lru_gate_a_b", "lru_gate_x_w",
           "lru_gate_x_b", "lru_lambda", "lru_norm_gain", "ret_norm_gain", "w_out", "norm2_gain", "ffn_up_w",
           "ffn_conv_w", "ffn_conv_b", "ffn_down_w", "final_norm_gain")


def _rows(a, pad_to):
    a = a.reshape(-1, LANES)
    pad = (-a.shape[0]) % pad_to
    return jnp.pad(a, ((0, pad), (0, 0))) if pad else a


def _pack(arrays, pad_to):
    rows, layout, at = [], [], 0
    for a in arrays:
        r = _rows(a, pad_to)
        layout.append((at, a.size // LANES, a.shape))
        rows.append(r)
        at += r.shape[0]
    return jnp.concatenate(rows, axis=0), layout


def _unpack(packed, layout):
    lead = packed.shape[:-2]
    return [packed[..., at:at + n, :].reshape(lead + shape) for at, n, shape in layout]


def _to_blocks(full, name):
    r, c = full.shape
    if name in COLUMN_SHARDED:
        return full.reshape(r, N_DEV, c // N_DEV).transpose(1, 0, 2)
    return full.reshape(N_DEV, r // N_DEV, c)


def _from_blocks(blocks, name):
    _, r, c = blocks.shape
    if name in COLUMN_SHARDED:
        return blocks.transpose(1, 0, 2).reshape(r, N_DEV * c)
    return blocks.reshape(N_DEV * r, c)


def _f32_as_bf16(a):
    return lax.bitcast_convert_type(a, BF16).reshape(a.shape[:-1] + (2 * a.shape[-1],))


def _bf16_as_f32(a):
    return lax.bitcast_convert_type(a.reshape(a.shape[:-1] + (a.shape[-1] // 2, 2)), F32)


def kernel(x, norm1_gain, w_in, lru_conv_w, lru_conv_b, lru_gate_a_w, lru_gate_a_b, lru_gate_x_w, lru_gate_x_b, lru_lambda, lru_norm_gain, ret_norm_gain, w_out, norm2_gain, ffn_up_w, ffn_conv_w, ffn_conv_b, ffn_down_w, final_norm_gain, loss_target, m_norm1_gain, m_w_in, m_lru_conv_w, m_lru_conv_b, m_lru_gate_a_w, m_lru_gate_a_b, m_lru_gate_x_w, m_lru_gate_x_b, m_lru_lambda, m_lru_norm_gain, m_ret_norm_gain, m_w_out, m_norm2_gain, m_ffn_up_w, m_ffn_conv_w, m_ffn_conv_b, m_ffn_down_w, m_final_norm_gain, v_norm1_gain, v_w_in, v_lru_conv_w, v_lru_conv_b, v_lru_gate_a_w, v_lru_gate_a_b, v_lru_gate_x_w, v_lru_gate_x_b, v_lru_lambda, v_lru_norm_gain, v_ret_norm_gain, v_w_out, v_norm2_gain, v_ffn_up_w, v_ffn_conv_w, v_ffn_conv_b, v_ffn_down_w, v_final_norm_gain):
    args = dict(locals())
    given = {n: args[n] for n in WEIGHTS}
    mom_m = {n: args["m_" + n] for n in WEIGHTS}
    mom_v = {n: args["v_" + n] for n in WEIGHTS}
    out_shape = {n: given[n].shape for n in WEIGHTS}

    def plain(a, name):
        if a.ndim <= 2:
            return a.reshape(1, -1)
        return a[0]

    shard = {n: plain(given[n], n) for n in WEIGHTS}

    payload = [shard[n].astype(BF16) for n in MATRICES] + [_f32_as_bf16(shard[n].reshape(1, -1)) for n in
                                                             ("lru_conv_w", "ffn_conv_w")]
    packed, layout = _pack(payload, 2 * SUBLANES)
    gathered = _unpack(_all_gather(packed, "weights_all_gather"), layout)
    w = {n: shard[n] for n in REPLICATED}
    for n, blocks in zip(MATRICES, gathered[:len(MATRICES)]):
        w[n] = _from_blocks(blocks, n)
    for n, blocks in zip(("lru_conv_w", "ffn_conv_w"), gathered[len(MATRICES):]):
        w[n] = _from_blocks(_bf16_as_f32(blocks).reshape((N_DEV,) + shard[n].shape), n)

    loss, grad_x, g = _local_step(x[0], loss_target[0], w)
    loss = lax.psum(loss, ("x", "y", "c"))

    conv_blocks = jnp.concatenate([_to_blocks(g[n], n).reshape(N_DEV, -1) for n in ("lru_conv_w", "ffn_conv_w")], axis=1)
    conv_rows = conv_blocks.shape[1] // LANES
    conv_pad = (-conv_rows) % (2 * SUBLANES)
    conv_blocks = jnp.pad(conv_blocks.reshape(N_DEV, conv_rows, LANES), ((0, 0), (0, conv_pad), (0, 0)))
    blocks = [_to_blocks(g[n], n) for n in MATRICES] + [conv_blocks.astype(GRAD_DTYPE)]
    by_core = [b.reshape((N_CHIPS, CORES) + b.shape[1:]).transpose(1, 0, 2, 3) for b in blocks]
    theirs = _pair_exchange(by_core, "grads_pair_exchange")
    core = lax.axis_index("c").astype(jnp.int32).reshape(1)
    sums = [_pair_sum(core, a, b, "grads_pair_sum_" + n) for n, a, b in zip(MATRICES + ("conv",), by_core, theirs)]
    landed = _chip_exchange(sums, "grads_chip_exchange")
    rep_packed, rep_layout = _pack([g[n] for n in REPLICATED], SUBLANES)
    rep_parts = _all_gather(rep_packed, "small_grads_all_gather")

    res = {}
    for n, parts in zip(MATRICES, landed[:len(MATRICES)]):
        res[n] = _sum_adamw(parts, shard[n], plain(mom_m[n], n), plain(mom_v[n], n), "adamw_" + n)

    def conv_rows_of(d):
        flat = jnp.concatenate([plain(d[n], n).reshape(-1) for n in ("lru_conv_w", "ffn_conv_w")])
        return jnp.pad(flat.reshape(conv_rows, LANES), ((0, conv_pad), (0, 0)))

    conv_res = _sum_adamw(landed[-1], conv_rows_of(given), conv_rows_of(mom_m), conv_rows_of(mom_v), "adamw_conv")
    n_lru = shard["lru_conv_w"].size
    for n, lo, hi in (("lru_conv_w", 0, n_lru), ("ffn_conv_w", n_lru, n_lru + shard["ffn_conv_w"].size)):
        res[n] = [r.reshape(-1)[lo:hi].reshape(shard[n].shape) for r in conv_res]
    rep_res = _sum_adamw(rep_parts, _pack([shard[n] for n in REPLICATED], SUBLANES)[0],
                         _pack([plain(mom_m[n], n) for n in REPLICATED], SUBLANES)[0],
                         _pack([plain(mom_v[n], n) for n in REPLICATED], SUBLANES)[0], "adamw_replicated")
    for k in range(4):
        for n, a in zip(REPLICATED, _unpack(rep_res[k], rep_layout)):
            res.setdefault(n, [None] * 4)[k] = a

    outs = [loss, grad_x[None]]
    for k in range(4):
        outs += [res[n][k].reshape(out_shape[n]) for n in WEIGHTS]
    return tuple(outs)
```

```python
import math

import numpy as np
import jax
import jax.numpy as jnp
from jax import lax
from jax.experimental import pallas as pl
from jax.experimental.pallas import tpu as pltpu

F32 = jnp.float32
BF16 = jnp.bfloat16
MXU_DTYPE = jnp.bfloat16
GRAD_DTYPE = jnp.bfloat16

N_DEV = 8
N_CHIPS = 4
D_MODEL = 1024
D_LRU = 512
LRU_HEADS = 8
LRU_CONV = 4
LRU_C = 8.0
D_RET = 512
RET_HEADS = 4
RET_HEAD_DIM = 128
RET_CHUNK = 128
ROPE_BASE = 10000.0
D_IN = 3072
D_FF = 3072
FFN_CONV = 3
NORM_EPS = 1e-6

ADAM_LR = 0.001
ADAM_B1 = 0.9
ADAM_B2 = 0.999
ADAM_EPS = 1e-08
ADAM_WD = 0.01
ADAM_STEP = 10

SUBLANES = 8
LANES = 128
VMEM_LIMIT = 48 * 1024 * 1024

MESH = pl.DeviceIdType.MESH
ANY = pl.BlockSpec(memory_space=pl.ANY)


def _dot(a, b):
    return jnp.dot(a.astype(MXU_DTYPE), b.astype(MXU_DTYPE), preferred_element_type=F32)


def _dot_nt(a, b):
    return lax.dot_general(a.astype(MXU_DTYPE), b.astype(MXU_DTYPE), (((1,), (1,)), ((), ())),
                           preferred_element_type=F32)


def _dot_tn(a, b):
    return lax.dot_general(a.astype(MXU_DTYPE), b.astype(MXU_DTYPE), (((0,), (0,)), ((), ())),
                           preferred_element_type=F32)


def _sigmoid(x):
    return 1.0 / (1.0 + jnp.exp(-x))


_GELU_C = math.sqrt(2.0 / math.pi)


def _gelu_parts(x):
    x2 = x * x
    t = jnp.tanh(_GELU_C * (x + 0.044715 * (x2 * x)))
    cdf = 0.5 * (1.0 + t)
    g = x * cdf
    dg = cdf + 0.5 * x * (1.0 - t * t) * (_GELU_C * (1.0 + 3.0 * 0.044715 * x2))
    return g, dg


def _gelu(x):
    t = jnp.tanh(_GELU_C * (x + 0.044715 * (x * x * x)))
    return x * (0.5 * (1.0 + t))


def _neg_expm1(x):
    series = x * (1.0 + x * (1.0 / 2.0) * (1.0 + x * (1.0 / 3.0) * (1.0 + x * (1.0 / 4.0) * (
        1.0 + x * (1.0 / 5.0) * (1.0 + x * (1.0 / 6.0) * (1.0 + x * (1.0 / 7.0)))))))
    return jnp.where(x > -0.25, -series, 1.0 - jnp.exp(x))


def _softplus(x):
    return jnp.maximum(x, 0.0) + jnp.log1p(jnp.exp(-jnp.abs(x)))


def _bcast_row(x, r, rows=SUBLANES):
    return jnp.broadcast_to(x[r:r + 1, :], (rows, x.shape[1]))


def _colsum8(x):
    return jnp.broadcast_to(jnp.sum(x, axis=0, keepdims=True), (SUBLANES, x.shape[1]))


def _shift_down(prev8, tile, s):
    if s == 0:
        return tile
    ext = jnp.concatenate([prev8, tile], axis=0)
    return pltpu.roll(ext, s, 0)[SUBLANES:, :]


def _shift_up(tile, next8, s):
    if s == 0:
        return tile
    ext = jnp.concatenate([tile, next8], axis=0)
    n = ext.shape[0]
    return pltpu.roll(ext, n - s, 0)[:tile.shape[0], :]


def _group_scan(a, b, reverse):
    n = a.shape[0]
    row = lax.broadcasted_iota(jnp.int32, a.shape, 0) & (SUBLANES - 1)
    for s in (1, 2, 4):
        shift = (n - s) if reverse else s
        a_sh = pltpu.roll(a, shift, 0)
        b_sh = pltpu.roll(b, shift, 0)
        m = (row <= SUBLANES - 1 - s) if reverse else (row >= s)
        b = jnp.where(m, a * b_sh + b, b)
        a = jnp.where(m, a * a_sh, a)
    return a, b


def _carry_scan(a_ref, b_ref, out_ref, carry0, reverse):
    n_groups = a_ref.shape[0] // SUBLANES

    def body(i, carry):
        g = (n_groups - 1 - i) if reverse else i
        r0 = pl.multiple_of(g * SUBLANES, SUBLANES)
        hg = a_ref[pl.ds(r0, SUBLANES), :] * carry + b_ref[pl.ds(r0, SUBLANES), :]
        out_ref[pl.ds(r0, SUBLANES), :] = hg
        return _bcast_row(hg, 0 if reverse else SUBLANES - 1)

    return lax.fori_loop(0, n_groups, body, carry0)


def _rms_fwd(h, gain):
    rstd = lax.rsqrt(jnp.mean(h * h, axis=-1, keepdims=True) + NORM_EPS)
    n = h * rstd
    return n, rstd, n * gain


def _rms_bwd(dy, n, rstd, gain):
    dn = dy * gain
    dh = rstd * (dn - n * jnp.mean(dn * n, axis=-1, keepdims=True))
    return dh, _colsum8(dy * n)


def _halo_rows(dtype):
    return SUBLANES * (4 // jnp.dtype(dtype).itemsize)


def _halo_map(tile_rows, col, halo_rows=SUBLANES):
    per = tile_rows // halo_rows
    return lambda i: (jnp.maximum(i * per - 1, 0), col)


def _resident(shape):
    return pl.BlockSpec(shape, lambda *_: (0,) * len(shape), pipeline_mode=pl.Buffered(1))


def _place():
    x, y, c = lax.axis_index("x"), lax.axis_index("y"), lax.axis_index("c")
    chips = [(1 - x, y), (x, 1 - y), (1 - x, 1 - y)]
    return x, y, c, chips


def _dev(x, y, c):
    return 4 * x + 2 * y + c


def _remote(src, dst, send_sem, recv_sem, to):
    return pltpu.make_async_remote_copy(src_ref=src, dst_ref=dst, send_sem=send_sem, recv_sem=recv_sem,
                                        device_id=to, device_id_type=MESH)


class _Comm:
    def __init__(self, operands, out_shape, sems, descs, aliases=()):
        self.operands, self.out_shape, self.sems, self.descs, self.aliases = operands, out_shape, sems, descs, aliases

    def start(self, ins, outs, sems):
        local, sends, _ = self.descs(ins, outs, sems)
        for cp in sends + local:
            cp.start()

    def wait(self, ins, outs, sems):
        local, sends, recvs = self.descs(ins, outs, sems)
        for cp in recvs:
            cp.wait_recv()
        for cp in sends:
            cp.wait_send()
        for cp in local:
            cp.wait()


def _gather_first(shards):
    n = len(shards)

    def descs(ins, outs, sems):
        send, recv, loc = sems
        x, y, c, chips = _place()
        me = _dev(x, y, c)
        targets = [(x, y, 1 - c)] + [(*chip, c) for chip in chips]
        local, sends, recvs = [], [], []
        for t in range(n):
            local.append(pltpu.make_async_copy(ins[t], outs[t].at[me], loc.at[t]))
            for k, to in enumerate(targets):
                i = 4 * t + k
                sends.append(_remote(ins[t], outs[t].at[me], send.at[i], recv.at[i], to))
                recvs.append(_remote(ins[t], outs[t].at[_dev(*to)], send.at[i], recv.at[i], to))
        return local, sends, recvs

    return _Comm(list(shards), [jax.ShapeDtypeStruct((N_DEV,) + s.shape, s.dtype) for s in shards],
                 [pltpu.SemaphoreType.DMA((4 * n,)), pltpu.SemaphoreType.DMA((4 * n,)), pltpu.SemaphoreType.DMA((n,))],
                 descs)


def _gather_second(gathered):
    n = len(gathered)

    def descs(ins, outs, sems):
        send, recv = sems
        x, y, c, chips = _place()
        sends, recvs = [], []
        for t in range(n):
            for j, chip in enumerate(chips):
                i = 3 * t + j
                have, get = _dev(*chip, c), _dev(*chip, 1 - c)
                sends.append(_remote(outs[t].at[have], outs[t].at[have], send.at[i], recv.at[i], (x, y, 1 - c)))
                recvs.append(_remote(outs[t].at[have], outs[t].at[get], send.at[i], recv.at[i], (x, y, 1 - c)))
        return [], sends, recvs

    return _Comm(list(gathered), [jax.ShapeDtypeStruct(g.shape, g.dtype) for g in gathered],
                 [pltpu.SemaphoreType.DMA((3 * n,)), pltpu.SemaphoreType.DMA((3 * n,))], descs,
                 aliases=[(t, t) for t in range(n)])


def _pair_exchange(blocks):
    n = len(blocks)

    def descs(ins, outs, sems):
        send, recv = sems
        x, y, c, _ = _place()
        sends, recvs = [], []
        for t in range(n):
            for q in range(N_CHIPS):
                i = N_CHIPS * t + q
                cp = _remote(ins[t].at[2 * q + 1 - c], outs[t].at[q], send.at[i], recv.at[i], (x, y, 1 - c))
                sends.append(cp)
                recvs.append(cp)
        return [], sends, recvs

    return _Comm(list(blocks), [jax.ShapeDtypeStruct((N_CHIPS,) + b.shape[1:], b.dtype) for b in blocks],
                 [pltpu.SemaphoreType.DMA((N_CHIPS * n,)), pltpu.SemaphoreType.DMA((N_CHIPS * n,))], descs)


def _chip_exchange(blocks):
    n = len(blocks)

    def descs(ins, outs, sems):
        send, recv, loc = sems
        x, y, c, chips = _place()
        me = 2 * x + y
        local, sends, recvs = [], [], []
        for t in range(n):
            local.append(pltpu.make_async_copy(ins[t].at[me], outs[t].at[me], loc.at[t]))
            for j, (px, py) in enumerate(chips):
                i = 3 * t + j
                q = 2 * px + py
                sends.append(_remote(ins[t].at[q], outs[t].at[me], send.at[i], recv.at[i], (px, py, c)))
                recvs.append(_remote(ins[t].at[q], outs[t].at[q], send.at[i], recv.at[i], (px, py, c)))
        return local, sends, recvs

    return _Comm(list(blocks), [jax.ShapeDtypeStruct(b.shape, b.dtype) for b in blocks],
                 [pltpu.SemaphoreType.DMA((3 * n,)), pltpu.SemaphoreType.DMA((3 * n,)), pltpu.SemaphoreType.DMA((n,))],
                 descs)


def _both(a, b):
    na, oa, sa = len(a.operands), len(a.out_shape), len(a.sems)

    def descs(ins, outs, sems):
        local_a, sends_a, recvs_a = a.descs(ins[:na], outs[:oa], sems[:sa])
        local_b, sends_b, recvs_b = b.descs(ins[na:], outs[oa:], sems[sa:])
        return local_a + local_b, sends_a + sends_b, recvs_a + recvs_b

    return _Comm(a.operands + b.operands, a.out_shape + b.out_shape, a.sems + b.sems, descs,
                 aliases=list(a.aliases) + [(na + i, oa + o) for i, o in b.aliases])


def _run_comms(comms, name):
    first = comms[0]
    n_in, n_out = len(first.operands), len(first.out_shape)

    def body(*refs):
        ins, outs, sems = refs[:n_in], refs[n_in:n_in + n_out], list(refs[n_in + n_out:])
        for k, comm in enumerate(comms):
            mine = [sems.pop(0) for _ in comm.sems]
            comm.start(ins if k == 0 else outs, outs, mine)
            comm.wait(ins if k == 0 else outs, outs, mine)

    outs = pl.pallas_call(
        body, name=name, out_shape=first.out_shape, in_specs=[ANY] * n_in, out_specs=[ANY] * n_out,
        scratch_shapes=[s for comm in comms for s in comm.sems],
    )(*first.operands)
    return list(outs)


def _call(body, *, name, grid, in_specs, out_specs, out_shape, operands, scratch_shapes=(), comm=None):
    sem = ("arbitrary",) * len(grid)
    params = pltpu.CompilerParams(dimension_semantics=sem, vmem_limit_bytes=VMEM_LIMIT)
    if comm is None:
        return pl.pallas_call(body, name=name, grid=grid, in_specs=in_specs, out_specs=out_specs, out_shape=out_shape,
                              scratch_shapes=list(scratch_shapes), compiler_params=params)(*operands)
    n_in, n_out, n_scr = len(in_specs), len(out_specs), len(scratch_shapes)
    c_in, c_out = len(comm.operands), len(comm.out_shape)

    def wrapped(*refs):
        refs = list(refs)
        ins, refs = refs[:n_in], refs[n_in:]
        cins, refs = refs[:c_in], refs[c_in:]
        outs, refs = refs[:n_out], refs[n_out:]
        couts, refs = refs[:c_out], refs[c_out:]
        scr, csems = refs[:n_scr], refs[n_scr:]
        first = last = None
        for axis, size in enumerate(grid):
            at_first, at_last = pl.program_id(axis) == 0, pl.program_id(axis) == size - 1
            first = at_first if first is None else first & at_first
            last = at_last if last is None else last & at_last

        @pl.when(first)
        def _():
            comm.start(cins, couts, csems)

        body(*ins, *outs, *scr)

        @pl.when(last)
        def _():
            comm.wait(cins, couts, csems)

    res = pl.pallas_call(
        wrapped, name=name, grid=grid, in_specs=list(in_specs) + [ANY] * c_in, out_specs=list(out_specs) + [ANY] * c_out,
        out_shape=list(out_shape) + list(comm.out_shape), scratch_shapes=list(scratch_shapes) + list(comm.sems),
        input_output_aliases={n_in + i: n_out + o for i, o in comm.aliases}, compiler_params=params,
    )(*operands, *comm.operands)
    return list(res[:n_out]), list(res[n_out:])


def _mm_nt(a, b, name, out_dtype, comm=None, tm=512, tn=512):
    m, k = a.shape
    n = b.shape[0]
    tm, tn = min(tm, m), min(tn, n)

    def body(a_ref, b_ref, o_ref):
        o_ref[...] = _dot_nt(a_ref[...], b_ref[...]).astype(o_ref.dtype)

    return _call(body, name=name, grid=(n // tn, m // tm), comm=comm,
                 in_specs=[pl.BlockSpec((tm, k), lambda j, i: (i, 0)), pl.BlockSpec((tn, k), lambda j, i: (j, 0))],
                 out_specs=[pl.BlockSpec((tm, tn), lambda j, i: (i, j))],
                 out_shape=[jax.ShapeDtypeStruct((m, n), out_dtype)], operands=[a, b])


def _mm_tn(a, b, name, blocks=1, tk=2048):
    t, m = a.shape
    n = b.shape[1]
    tk = min(tk, t)
    nk = t // tk
    cb = n // blocks
    per = max(1, 768 // cb) if blocks > 1 else 1
    tn = per * cb if blocks > 1 else min(1024, n)
    tm = min(1024, m)
    assert blocks == 1 or tm == m

    def body(a_ref, b_ref, o_ref, acc):
        k = pl.program_id(2)

        @pl.when(k == 0)
        def _():
            acc[...] = jnp.zeros_like(acc)
        acc[...] += _dot_tn(a_ref[...], b_ref[...])

        @pl.when(k == nk - 1)
        def _():
            if blocks == 1:
                o_ref[...] = acc[...].astype(o_ref.dtype)
            else:
                for s in range(per):
                    o_ref[s] = acc[:, s * cb:(s + 1) * cb].astype(o_ref.dtype)

    if blocks == 1:
        out_spec = pl.BlockSpec((tm, tn), lambda i, j, k: (i, j))
        out_shape = jax.ShapeDtypeStruct((m, n), GRAD_DTYPE)
    else:
        out_spec = pl.BlockSpec((per, m, cb), lambda i, j, k: (j, 0, 0))
        out_shape = jax.ShapeDtypeStruct((blocks, m, cb), GRAD_DTYPE)
    return _call(body, name=name, grid=(m // tm, n // tn, nk),
                 in_specs=[pl.BlockSpec((tk, tm), lambda i, j, k: (k, i)), pl.BlockSpec((tk, tn), lambda i, j, k: (k, j))],
                 out_specs=[out_spec], out_shape=[out_shape], operands=[a, b],
                 scratch_shapes=[pltpu.VMEM((tm, tn), F32)])[0]


def _inproj_fwd(x, g1, w_blocks, comm):
    t = x.shape[0]
    tm = min(512, t)
    nb, _, cb = w_blocks.shape

    def body(x_ref, g_ref, w_ref, u_ref, p_ref):
        _, _, u = _rms_fwd(x_ref[...], g_ref[...])
        u = u.astype(MXU_DTYPE)
        u_ref[...] = u
        for d in range(nb):
            p_ref[:, d * cb:(d + 1) * cb] = _dot(u, w_ref[d])

    return _call(body, name="inproj_fwd", grid=(t // tm,), comm=comm,
                 in_specs=[pl.BlockSpec((tm, D_MODEL), lambda i: (i, 0)), pl.BlockSpec((1, D_MODEL), lambda i: (0, 0)),
                           _resident(w_blocks.shape)],
                 out_specs=[pl.BlockSpec((tm, D_MODEL), lambda i: (i, 0)), pl.BlockSpec((tm, D_IN), lambda i: (i, 0))],
                 out_shape=[jax.ShapeDtypeStruct((t, D_MODEL), MXU_DTYPE), jax.ShapeDtypeStruct((t, D_IN), F32)],
                 operands=[x, g1, w_blocks])


def _lru_gates(xc, wa, ba, wx, bx, sp):
    r = _sigmoid(_dot(xc, wa) + ba)
    ig = _sigmoid(_dot(xc, wx) + bx)
    log_a = (-LRU_C) * r * sp
    a = jnp.exp(log_a)
    m = jnp.sqrt(_neg_expm1(2.0 * log_a))
    return r, ig, a, m


def _lru_fwd(proj, conv_w, conv_b, wa, ba, wx, bx, lam, gain, comm):
    t = proj.shape[0]
    tm = min(256, t)
    c = D_LRU

    def body(x_ref, xh_ref, g_ref, cw_ref, cb_ref, wa_ref, ba_ref, wx_ref, bx_ref, lam_ref, gain_ref,
             xc_ref, h_ref, y_ref, a_scr, b_scr, carry):
        i = pl.program_id(0)

        @pl.when(i == 0)
        def _():
            carry[...] = jnp.zeros_like(carry)

        x = x_ref[...]
        prev = jnp.where(i == 0, 0.0, xh_ref[...])
        cw = cw_ref[...]
        xc = cb_ref[...] + cw[LRU_CONV - 1:LRU_CONV, :] * x
        for k in range(LRU_CONV - 1):
            xc = xc + cw[k:k + 1, :] * _shift_down(prev, x, LRU_CONV - 1 - k)
        xc_ref[...] = xc
        sp = _softplus(-lam_ref[...])
        _, ig, a, m = _lru_gates(xc, wa_ref[...], ba_ref[...], wx_ref[...], bx_ref[...], sp)
        ga, gb = _group_scan(a, m * (ig * xc), reverse=False)
        a_scr[...] = ga
        b_scr[...] = gb
        carry[...] = _carry_scan(a_scr, b_scr, h_ref, carry[...], reverse=False)
        z = h_ref[...] * _gelu(g_ref[...])
        _, _, y = _rms_fwd(z, gain_ref[...])
        y_ref[...] = y.astype(y_ref.dtype)

    row = lambda i: (i, 0)
    full = lambda i: (0, 0)
    vec = pl.BlockSpec((1, c), full)
    return _call(body, name="lru_fwd", grid=(t // tm,), comm=comm,
                 in_specs=[pl.BlockSpec((tm, c), row), pl.BlockSpec((SUBLANES, c), _halo_map(tm, 0)),
                           pl.BlockSpec((tm, c), lambda i: (i, 1)),
                           pl.BlockSpec((LRU_CONV, c), full), vec, pl.BlockSpec((c, c), full), vec,
                           pl.BlockSpec((c, c), full), vec, vec, vec],
                 out_specs=[pl.BlockSpec((tm, c), row), pl.BlockSpec((tm, c), row), pl.BlockSpec((tm, c), row)],
                 out_shape=[jax.ShapeDtypeStruct((t, c), F32), jax.ShapeDtypeStruct((t, c), F32),
                            jax.ShapeDtypeStruct((t, c), MXU_DTYPE)],
                 scratch_shapes=[pltpu.VMEM((tm, c), F32), pltpu.VMEM((tm, c), F32), pltpu.VMEM((SUBLANES, c), F32)],
                 operands=[proj, proj, proj, conv_w, conv_b, wa, ba, wx, bx, lam, gain])


def _ret_consts():
    c = RET_CHUNK
    log_g = jnp.log1p(-jnp.exp2(-5.0 - jnp.arange(RET_HEADS, dtype=F32)))
    idx = jnp.arange(c, dtype=F32)
    diff = idx[:, None] - idx[None, :]
    decay = jnp.where(diff[None] >= 0, jnp.exp(jnp.maximum(diff, 0.0)[None] * log_g[:, None, None]), 0.0)
    zeta = jnp.exp((c - 1 - idx)[None, :] * log_g[:, None])
    xi = jnp.exp((idx + 1.0)[None, :] * log_g[:, None])
    spread = lambda v: jnp.repeat(v.T, RET_HEAD_DIM, axis=1)
    log_g_np = np.log1p(-np.exp2(-5.0 - np.arange(RET_HEADS, dtype=np.float32))).astype(np.float32)
    g_chunk = [float(np.exp(np.float32(c) * lg)) for lg in log_g_np]
    return decay, spread(xi), spread(zeta), g_chunk


def _rope_tables(t):
    pos = jnp.arange(t, dtype=F32)
    inv_freq = ROPE_BASE ** (-jnp.arange(0, RET_HEAD_DIM, 2, dtype=F32) / RET_HEAD_DIM)
    ang = pos[:, None] * inv_freq[None, :]
    cos, sin = jnp.cos(ang), jnp.sin(ang)
    return jnp.concatenate([cos, cos], axis=-1), jnp.concatenate([-sin, sin], axis=-1)


def _rope(x, cos2, sin_signed):
    return x * cos2 + pltpu.roll(x, RET_HEAD_DIM // 2, 1) * sin_signed


def _rope_bwd(d, cos2, sin_signed):
    return d * cos2 + pltpu.roll(d * sin_signed, RET_HEAD_DIM // 2, 1)


RET_SCALE = RET_HEAD_DIM ** -0.5


def _ret_fwd(proj, cos2, sin_signed, gain, comm):
    t = proj.shape[0]
    c, d, nh = RET_CHUNK, RET_HEAD_DIM, RET_HEADS
    n_chunks = t // c
    decay, xi, zeta, g_chunk = _ret_consts()

    def body(qk_ref, vg_ref, cos_ref, sin_ref, dec_ref, xi_ref, zeta_ref, gain_ref, o_ref, y_ref, st_ref, state):
        @pl.when(pl.program_id(0) == 0)
        def _():
            state[...] = jnp.zeros_like(state)

        cos2, sin_s = cos_ref[...], sin_ref[...]
        for h in range(nh):
            lo = h * d
            q = _rope(qk_ref[:, lo:lo + d], cos2, sin_s)
            k = _rope(qk_ref[:, D_RET + lo:D_RET + lo + d], cos2, sin_s) * RET_SCALE
            v = vg_ref[:, lo:lo + d]
            g = vg_ref[:, D_RET + lo:D_RET + lo + d]
            s_prev = state[h]
            st_ref[0, h] = s_prev
            scores = _dot_nt(q, k) * dec_ref[h]
            o = _dot(scores, v) + _dot(q * xi_ref[:, lo:lo + d], s_prev)
            state[h] = s_prev * g_chunk[h] + _dot_tn(k * zeta_ref[:, lo:lo + d], v)
            o_ref[:, lo:lo + d] = o
            mu = jnp.mean(o, axis=-1, keepdims=True)
            oc = o - mu
            on = oc * lax.rsqrt(jnp.mean(oc * oc, axis=-1, keepdims=True) + NORM_EPS)
            y_ref[:, lo:lo + d] = (on * gain_ref[:, lo:lo + d] * (g * _sigmoid(g))).astype(y_ref.dtype)

    full2 = lambda i: (0, 0)
    return _call(body, name="ret_fwd", grid=(n_chunks,), comm=comm,
                 in_specs=[pl.BlockSpec((c, 2 * D_RET), lambda i: (i, 1)), pl.BlockSpec((c, 2 * D_RET), lambda i: (i, 2)),
                           pl.BlockSpec((c, d), lambda i: (i, 0)), pl.BlockSpec((c, d), lambda i: (i, 0)),
                           pl.BlockSpec((nh, c, c), lambda i: (0, 0, 0)), pl.BlockSpec((c, D_RET), full2),
                           pl.BlockSpec((c, D_RET), full2), pl.BlockSpec((1, D_RET), full2)],
                 out_specs=[pl.BlockSpec((c, D_RET), lambda i: (i, 0)), pl.BlockSpec((c, D_RET), lambda i: (i, 0)),
                            pl.BlockSpec((1, nh, d, d), lambda i: (i, 0, 0, 0))],
                 out_shape=[jax.ShapeDtypeStruct((t, D_RET), F32), jax.ShapeDtypeStruct((t, D_RET), MXU_DTYPE),
                            jax.ShapeDtypeStruct((n_chunks, nh, d, d), F32)],
                 scratch_shapes=[pltpu.VMEM((nh, d, d), F32)],
                 operands=[proj, proj, cos2, sin_signed, decay, xi, zeta, gain])


def _outproj_fwd(x, y_lru, y_ret, w_out, g2):
    t = x.shape[0]
    tm = min(512, t)

    def body(x_ref, yl_ref, yr_ref, w_ref, g_ref, h1_ref, u2_ref):
        h1 = x_ref[...] + _dot(yl_ref[...], w_ref[:D_LRU, :]) + _dot(yr_ref[...], w_ref[D_LRU:, :])
        h1_ref[...] = h1
        _, _, u = _rms_fwd(h1, g_ref[...])
        u2_ref[...] = u.astype(u2_ref.dtype)

    row = lambda i: (i, 0)
    return _call(body, name="outproj_fwd", grid=(t // tm,),
                 in_specs=[pl.BlockSpec((tm, D_MODEL), row), pl.BlockSpec((tm, D_LRU), row), pl.BlockSpec((tm, D_RET), row),
                           _resident((D_MODEL, D_MODEL)), pl.BlockSpec((1, D_MODEL), lambda i: (0, 0))],
                 out_specs=[pl.BlockSpec((tm, D_MODEL), row), pl.BlockSpec((tm, D_MODEL), row)],
                 out_shape=[jax.ShapeDtypeStruct((t, D_MODEL), F32), jax.ShapeDtypeStruct((t, D_MODEL), MXU_DTYPE)],
                 operands=[x, y_lru, y_ret, w_out, g2])


def _ffn_up_fwd(u2, w_blocks):
    t = u2.shape[0]
    tm = min(512, t)
    nb, _, cb = w_blocks.shape

    def body(u_ref, w_ref, o_ref):
        u = u_ref[...]
        for d in range(nb):
            o_ref[:, d * cb:(d + 1) * cb] = _dot(u, w_ref[d]).astype(o_ref.dtype)

    return _call(body, name="ffn_up_fwd", grid=(t // tm,),
                 in_specs=[pl.BlockSpec((tm, D_MODEL), lambda i: (i, 0)), _resident(w_blocks.shape)],
                 out_specs=[pl.BlockSpec((tm, nb * cb), lambda i: (i, 0))],
                 out_shape=[jax.ShapeDtypeStruct((t, nb * cb), MXU_DTYPE)], operands=[u2, w_blocks])[0]


FFN_TN = 768
FFN_NJ = D_FF // FFN_TN


def _ffn_conv(up_ref, halo_ref, cw_ref, cb_ref, first):
    x = up_ref[...].astype(F32)
    prev = jnp.where(first, 0.0, halo_ref[...].astype(F32)[-SUBLANES:, :])
    cw = cw_ref[...]
    taps = [_shift_down(prev, x, FFN_CONV - 1 - k) for k in range(FFN_CONV)]
    y = cb_ref[...] + cw[FFN_CONV - 1:FFN_CONV, :] * x
    for k in range(FFN_CONV - 1):
        y = y + cw[k:k + 1, :] * taps[k]
    return y, taps


def _ffn_fwd(up, conv_w, conv_b, w_down, h1, gf, target):
    t = up.shape[0]
    tm = min(256, t)
    tn, nj = FFN_TN, FFN_NJ

    def body(ua_ref, uah_ref, uv_ref, uvh_ref, cwa_ref, cwv_ref, cba_ref, cbv_ref, wd_ref, h1_ref, gf_ref, tg_ref,
             act_ref, dh_ref, dhb_ref, dgf_ref, loss_ref, acc):
        i, j = pl.program_id(0), pl.program_id(1)

        @pl.when((i == 0) & (j == 0))
        def _():
            dgf_ref[...] = jnp.zeros_like(dgf_ref)
            loss_ref[...] = jnp.zeros_like(loss_ref)

        @pl.when(j == 0)
        def _():
            acc[...] = jnp.zeros_like(acc)

        a, _ = _ffn_conv(ua_ref, uah_ref, cwa_ref, cba_ref, i == 0)
        v, _ = _ffn_conv(uv_ref, uvh_ref, cwv_ref, cbv_ref, i == 0)
        act = (_gelu(a) * v).astype(act_ref.dtype)
        act_ref[...] = act
        acc[...] += _dot(act, wd_ref[pl.ds(pl.multiple_of(j * tn, tn), tn), :])

        @pl.when(j == nj - 1)
        def _():
            n, rstd, y = _rms_fwd(h1_ref[...] + acc[...], gf_ref[...])
            err = y - tg_ref[...]
            loss_ref[...] += (0.5 / D_MODEL) * jnp.sum(err * err)
            dh, dgf = _rms_bwd(err * (1.0 / D_MODEL), n, rstd, gf_ref[...])
            dgf_ref[...] += dgf
            dh_ref[...] = dh
            dhb_ref[...] = dh.astype(dhb_ref.dtype)

    hb = _halo_rows(up.dtype)
    per = tm // hb
    halo = lambda off: (lambda i, j: (jnp.maximum(i * per - 1, 0), j + off))
    row = lambda i, j: (i, 0)
    const = lambda i, j: (0, 0)
    return _call(body, name="ffn_fwd", grid=(t // tm, nj),
                 in_specs=[pl.BlockSpec((tm, tn), lambda i, j: (i, j)), pl.BlockSpec((hb, tn), halo(0)),
                           pl.BlockSpec((tm, tn), lambda i, j: (i, j + nj)), pl.BlockSpec((hb, tn), halo(nj)),
                           pl.BlockSpec((FFN_CONV, tn), lambda i, j: (0, j)),
                           pl.BlockSpec((FFN_CONV, tn), lambda i, j: (0, j + nj)),
                           pl.BlockSpec((1, tn), lambda i, j: (0, j)), pl.BlockSpec((1, tn), lambda i, j: (0, j + nj)),
                           _resident((D_FF, D_MODEL)),
                           pl.BlockSpec((tm, D_MODEL), row), pl.BlockSpec((1, D_MODEL), const),
                           pl.BlockSpec((tm, D_MODEL), row)],
                 out_specs=[pl.BlockSpec((tm, tn), lambda i, j: (i, j)), pl.BlockSpec((tm, D_MODEL), row),
                            pl.BlockSpec((tm, D_MODEL), row), pl.BlockSpec((SUBLANES, D_MODEL), const),
                            pl.BlockSpec((SUBLANES, LANES), const)],
                 out_shape=[jax.ShapeDtypeStruct((t, D_FF), MXU_DTYPE), jax.ShapeDtypeStruct((t, D_MODEL), F32),
                            jax.ShapeDtypeStruct((t, D_MODEL), MXU_DTYPE), jax.ShapeDtypeStruct((SUBLANES, D_MODEL), F32),
                            jax.ShapeDtypeStruct((SUBLANES, LANES), F32)],
                 scratch_shapes=[pltpu.VMEM((tm, D_MODEL), F32)],
                 operands=[up, up, up, up, conv_w, conv_w, conv_b, conv_b, w_down, h1, gf, target])


FFN_ACC_ROWS = SUBLANES * (FFN_CONV + 1)


def _ffn_bwd(dh2_b, w_down, up, conv_w, conv_b, comm):
    t = up.shape[0]
    tm = min(256, t)
    tn, nj = FFN_TN, FFN_NJ
    ni = t // tm

    def conv_bwd(dy, taps, cw, acc_ref, carry_ref, dup_ref):
        nxt = carry_ref[...]
        carry_ref[...] = dy[:SUBLANES, :]
        dx = cw[FFN_CONV - 1:FFN_CONV, :] * dy
        for k in range(FFN_CONV - 1):
            dx = dx + cw[k:k + 1, :] * _shift_up(dy, nxt, FFN_CONV - 1 - k)
        dup_ref[...] = dx.astype(dup_ref.dtype)
        for k in range(FFN_CONV):
            acc_ref[k * SUBLANES:(k + 1) * SUBLANES, :] += _colsum8(dy * taps[k])
        acc_ref[FFN_CONV * SUBLANES:, :] += _colsum8(dy)

    def body(dh_ref, wd_ref, ua_ref, uah_ref, uv_ref, uvh_ref, cwa_ref, cwv_ref, cba_ref, cbv_ref,
             dua_ref, duv_ref, acca_ref, accv_ref, carry_a, carry_v):
        i = pl.program_id(1)
        r = ni - 1 - i

        @pl.when(i == 0)
        def _():
            acca_ref[...] = jnp.zeros_like(acca_ref)
            accv_ref[...] = jnp.zeros_like(accv_ref)
            carry_a[...] = jnp.zeros_like(carry_a)
            carry_v[...] = jnp.zeros_like(carry_v)

        a, taps_a = _ffn_conv(ua_ref, uah_ref, cwa_ref, cba_ref, r == 0)
        v, taps_v = _ffn_conv(uv_ref, uvh_ref, cwv_ref, cbv_ref, r == 0)
        g, dg = _gelu_parts(a)
        dact = _dot_nt(dh_ref[...], wd_ref[...])
        conv_bwd(dact * v * dg, taps_a, cwa_ref[...], acca_ref, carry_a, dua_ref)
        conv_bwd(dact * g, taps_v, cwv_ref[...], accv_ref, carry_v, duv_ref)

    hb = _halo_rows(up.dtype)
    per = tm // hb
    rev = lambda off: (lambda j, i: (ni - 1 - i, j + off))
    halo = lambda off: (lambda j, i: (jnp.maximum((ni - 1 - i) * per - 1, 0), j + off))
    colj = lambda off: (lambda j, i: (0, j + off))
    return _call(body, name="ffn_bwd", grid=(nj, ni), comm=comm,
                 in_specs=[pl.BlockSpec((tm, D_MODEL), lambda j, i: (ni - 1 - i, 0)),
                           pl.BlockSpec((tn, D_MODEL), lambda j, i: (j, 0)),
                           pl.BlockSpec((tm, tn), rev(0)), pl.BlockSpec((hb, tn), halo(0)),
                           pl.BlockSpec((tm, tn), rev(nj)), pl.BlockSpec((hb, tn), halo(nj)),
                           pl.BlockSpec((FFN_CONV, tn), colj(0)), pl.BlockSpec((FFN_CONV, tn), colj(nj)),
                           pl.BlockSpec((1, tn), colj(0)), pl.BlockSpec((1, tn), colj(nj))],
                 out_specs=[pl.BlockSpec((tm, tn), rev(0)), pl.BlockSpec((tm, tn), rev(0)),
                            pl.BlockSpec((FFN_ACC_ROWS, tn), colj(0)), pl.BlockSpec((FFN_ACC_ROWS, tn), colj(0))],
                 out_shape=[jax.ShapeDtypeStruct((t, D_FF), MXU_DTYPE), jax.ShapeDtypeStruct((t, D_FF), MXU_DTYPE),
                            jax.ShapeDtypeStruct((FFN_ACC_ROWS, D_FF), F32), jax.ShapeDtypeStruct((FFN_ACC_ROWS, D_FF), F32)],
                 scratch_shapes=[pltpu.VMEM((SUBLANES, tn), F32), pltpu.VMEM((SUBLANES, tn), F32)],
                 operands=[dh2_b, w_down, up, up, up, up, conv_w, conv_w, conv_b, conv_b])


def _norm_bwd_matmul(parts, w_blocks, h, gain, d_res, name, comm):
    t = h.shape[0]
    tm = min(256, t)
    n_parts = len(parts)
    nb, _, cb = w_blocks.shape
    where = []
    for p, a in enumerate(parts):
        assert a.shape[1] % cb == 0
        where += [(p, lo) for lo in range(0, a.shape[1], cb)]
    assert len(where) == nb

    def body(*refs):
        a_refs = refs[:n_parts]
        w_ref, h_ref, g_ref, dres_ref, dh_ref, dhb_ref, dg_ref = refs[n_parts:]

        @pl.when(pl.program_id(0) == 0)
        def _():
            dg_ref[...] = jnp.zeros_like(dg_ref)

        du = None
        for d, (p, lo) in enumerate(where):
            term = _dot_nt(a_refs[p][:, lo:lo + cb], w_ref[d])
            du = term if du is None else du + term
        n, rstd, _ = _rms_fwd(h_ref[...], g_ref[...])
        dh, dg = _rms_bwd(du, n, rstd, g_ref[...])
        dh = dh + dres_ref[...]
        dg_ref[...] += dg
        dh_ref[...] = dh
        dhb_ref[...] = dh.astype(dhb_ref.dtype)

    row = lambda i: (i, 0)
    const = lambda i: (0, 0)
    return _call(body, name=name, grid=(t // tm,), comm=comm,
                 in_specs=[pl.BlockSpec((tm, a.shape[1]), row) for a in parts] + [
                     _resident(w_blocks.shape), pl.BlockSpec((tm, D_MODEL), row), pl.BlockSpec((1, D_MODEL), const),
                     pl.BlockSpec((tm, D_MODEL), row)],
                 out_specs=[pl.BlockSpec((tm, D_MODEL), row), pl.BlockSpec((tm, D_MODEL), row),
                            pl.BlockSpec((SUBLANES, D_MODEL), const)],
                 out_shape=[jax.ShapeDtypeStruct((t, D_MODEL), F32), jax.ShapeDtypeStruct((t, D_MODEL), MXU_DTYPE),
                            jax.ShapeDtypeStruct((SUBLANES, D_MODEL), F32)],
                 operands=[*parts, w_blocks, h, gain, d_res])


def _ret_bwd(proj, cos2, sin_signed, gain, o, states, dmixed, comm):
    t = proj.shape[0]
    c, d, nh = RET_CHUNK, RET_HEAD_DIM, RET_HEADS
    n_chunks = t // c
    decay, xi, zeta, g_chunk = _ret_consts()
    base = 2 * D_LRU

    def body(qk_ref, vg_ref, cos_ref, sin_ref, dec_ref, xi_ref, zeta_ref, gain_ref, o_ref, st_ref, dy_ref,
             dp_ref, dgain_ref, gstate):
        @pl.when(pl.program_id(0) == 0)
        def _():
            gstate[...] = jnp.zeros_like(gstate)
            dgain_ref[...] = jnp.zeros_like(dgain_ref)

        cos2, sin_s = cos_ref[...], sin_ref[...]
        for h in range(nh):
            lo = h * d
            q = _rope(qk_ref[:, lo:lo + d], cos2, sin_s)
            k = _rope(qk_ref[:, D_RET + lo:D_RET + lo + d], cos2, sin_s) * RET_SCALE
            v = vg_ref[:, lo:lo + d]
            g = vg_ref[:, D_RET + lo:D_RET + lo + d]
            gain_h = gain_ref[:, lo:lo + d]
            xi_h, zeta_h, dec = xi_ref[:, lo:lo + d], zeta_ref[:, lo:lo + d], dec_ref[h]
            dy = dy_ref[:, lo:lo + d]
            sg = _sigmoid(g)
            o_h = o_ref[:, lo:lo + d]
            oc = o_h - jnp.mean(o_h, axis=-1, keepdims=True)
            rstd = lax.rsqrt(jnp.mean(oc * oc, axis=-1, keepdims=True) + NORM_EPS)
            on = oc * rstd
            at = base + 3 * D_RET + lo
            dp_ref[:, at:at + d] = (dy * on * gain_h * (sg * (1.0 + g * (1.0 - sg)))).astype(dp_ref.dtype)
            don_g = dy * (g * sg)
            dgain_ref[:, lo:lo + d] += _colsum8(don_g * on)
            don = don_g * gain_h
            do = rstd * (don - jnp.mean(don, axis=-1, keepdims=True) - on * jnp.mean(don * on, axis=-1, keepdims=True))
            s_prev = st_ref[0, h]
            g_next = gstate[h]
            p = _dot_nt(q, k) * dec
            dpm = _dot_nt(do, v) * dec
            dq = _dot(dpm, k) + _dot_nt(do, s_prev) * xi_h
            dk = _dot_tn(dpm, q) + _dot_nt(v, g_next) * zeta_h
            dv = _dot_tn(p, do) + _dot(k * zeta_h, g_next)
            gstate[h] = g_next * g_chunk[h] + _dot_tn(q * xi_h, do)
            dp_ref[:, base + lo:base + lo + d] = _rope_bwd(dq, cos2, sin_s).astype(dp_ref.dtype)
            at = base + D_RET + lo
            dp_ref[:, at:at + d] = _rope_bwd(dk * RET_SCALE, cos2, sin_s).astype(dp_ref.dtype)
            at = base + 2 * D_RET + lo
            dp_ref[:, at:at + d] = dv.astype(dp_ref.dtype)

    rev = lambda col: (lambda i: (n_chunks - 1 - i, col))
    full2 = lambda i: (0, 0)
    return _call(body, name="ret_bwd", grid=(n_chunks,), comm=comm,
                 in_specs=[pl.BlockSpec((c, 2 * D_RET), rev(1)), pl.BlockSpec((c, 2 * D_RET), rev(2)),
                           pl.BlockSpec((c, d), rev(0)), pl.BlockSpec((c, d), rev(0)),
                           pl.BlockSpec((nh, c, c), lambda i: (0, 0, 0)), pl.BlockSpec((c, D_RET), full2),
                           pl.BlockSpec((c, D_RET), full2), pl.BlockSpec((1, D_RET), full2),
                           pl.BlockSpec((c, D_RET), rev(0)),
                           pl.BlockSpec((1, nh, d, d), lambda i: (n_chunks - 1 - i, 0, 0, 0)),
                           pl.BlockSpec((c, D_RET), rev(1))],
                 out_specs=[pl.BlockSpec((c, D_IN), rev(0)), pl.BlockSpec((SUBLANES, D_RET), full2)],
                 out_shape=[jax.ShapeDtypeStruct((t, D_IN), MXU_DTYPE), jax.ShapeDtypeStruct((SUBLANES, D_RET), F32)],
                 scratch_shapes=[pltpu.VMEM((nh, d, d), F32)],
                 operands=[proj, proj, cos2, sin_signed, decay, xi, zeta, gain, o, states, dmixed])


LRU_ACC = {"conv_w": 0, "conv_b": LRU_CONV, "gate_a_b": LRU_CONV + 1, "gate_x_b": LRU_CONV + 2,
           "lambda": LRU_CONV + 3, "norm_gain": LRU_CONV + 4}
LRU_ACC_ROWS = SUBLANES * (LRU_CONV + 5)


def _lru_bwd(proj, xc_all, h_all, dmixed, dproj, conv_w, wa, ba, wx, bx, lam, gain):
    t = proj.shape[0]
    tm = min(256, t)
    c = D_LRU
    ni = t // tm

    def body(x_ref, xh_ref, g_ref, xc_ref, h_ref, hh_ref, dy_ref, cw_ref, wa_ref, ba_ref, wx_ref, bx_ref, lam_ref,
             gain_ref, dproj_in, dp_ref, acc_ref, dwa_ref, dwx_ref, a_scr, b_scr, mu_scr, carry_mu, carry_dxc):
        del dproj_in
        i = pl.program_id(0)
        r = ni - 1 - i

        @pl.when(i == 0)
        def _():
            acc_ref[...] = jnp.zeros_like(acc_ref)
            dwa_ref[...] = jnp.zeros_like(dwa_ref)
            dwx_ref[...] = jnp.zeros_like(dwx_ref)
            carry_mu[...] = jnp.zeros_like(carry_mu)
            carry_dxc[...] = jnp.zeros_like(carry_dxc)

        def add(name, val, k=0):
            lo = (LRU_ACC[name] + k) * SUBLANES
            acc_ref[lo:lo + SUBLANES, :] += _colsum8(val)

        xc, h = xc_ref[...], h_ref[...]
        lam_v = lam_ref[...]
        sp = _softplus(-lam_v)
        rg, ig, a, m = _lru_gates(xc, wa_ref[...], ba_ref[...], wx_ref[...], bx_ref[...], sp)
        gl, dgl = _gelu_parts(g_ref[...])
        zn, rstd, _ = _rms_fwd(h * gl, gain_ref[...])
        dy = dy_ref[...]
        dz, dgain = _rms_bwd(dy, zn, rstd, gain_ref[...])
        lo = LRU_ACC["norm_gain"] * SUBLANES
        acc_ref[lo:lo + SUBLANES, :] += dgain
        dp_ref[:, c:] = (dz * h * dgl).astype(dp_ref.dtype)
        dh = dz * gl
        ga, gb = _group_scan(a, a * dh, reverse=True)
        a_scr[...] = ga
        b_scr[...] = gb
        mu_next_tile = carry_mu[...]
        carry_mu[...] = _carry_scan(a_scr, b_scr, mu_scr, mu_next_tile, reverse=True)
        lam_t = dh + _shift_up(mu_scr[...], mu_next_tile, 1)
        h_prev = _shift_down(jnp.where(r == 0, 0.0, hh_ref[...]), h, 1)
        da = lam_t * h_prev
        dig = lam_t * m * xc
        dxc = lam_t * m * ig
        dlog_a = da * a - (lam_t * ig * xc) * (a * a) / m
        dpr = dlog_a * ((-LRU_C) * sp) * rg * (1.0 - rg)
        add("lambda", dlog_a * ((-LRU_C) * rg) * (-_sigmoid(-lam_v)))
        dpi = dig * ig * (1.0 - ig)
        add("gate_a_b", dpr)
        add("gate_x_b", dpi)
        dwa_ref[...] += _dot_tn(xc, dpr)
        dwx_ref[...] += _dot_tn(xc, dpi)
        dxc = dxc + _dot_nt(dpr, wa_ref[...]) + _dot_nt(dpi, wx_ref[...])
        add("conv_b", dxc)
        x = x_ref[...]
        prev = jnp.where(r == 0, 0.0, xh_ref[...])
        cw = cw_ref[...]
        nxt = carry_dxc[...]
        carry_dxc[...] = dxc[:SUBLANES, :]
        dx = cw[LRU_CONV - 1:LRU_CONV, :] * dxc
        for k in range(LRU_CONV - 1):
            dx = dx + cw[k:k + 1, :] * _shift_up(dxc, nxt, LRU_CONV - 1 - k)
        for k in range(LRU_CONV):
            add("conv_w", dxc * _shift_down(prev, x, LRU_CONV - 1 - k), k)
        dp_ref[:, :c] = dx.astype(dp_ref.dtype)

    per = tm // SUBLANES
    rev = lambda col: (lambda i: (ni - 1 - i, col))
    halo = lambda i: (jnp.maximum((ni - 1 - i) * per - 1, 0), 0)
    full = lambda i: (0, 0)
    vec = pl.BlockSpec((1, c), full)
    mat = pl.BlockSpec((c, c), full)
    return pl.pallas_call(
        body, name="lru_bwd", grid=(ni,),
        in_specs=[pl.BlockSpec((tm, c), rev(0)), pl.BlockSpec((SUBLANES, c), halo), pl.BlockSpec((tm, c), rev(1)),
                  pl.BlockSpec((tm, c), rev(0)), pl.BlockSpec((tm, c), rev(0)), pl.BlockSpec((SUBLANES, c), halo),
                  pl.BlockSpec((tm, c), rev(0)), pl.BlockSpec((LRU_CONV, c), full), mat, vec, mat, vec, vec, vec, ANY],
        out_specs=[pl.BlockSpec((tm, 2 * c), rev(0)), pl.BlockSpec((LRU_ACC_ROWS, c), full), mat, mat],
        out_shape=[jax.ShapeDtypeStruct(dproj.shape, dproj.dtype), jax.ShapeDtypeStruct((LRU_ACC_ROWS, c), F32),
                   jax.ShapeDtypeStruct((c, c), F32), jax.ShapeDtypeStruct((c, c), F32)],
        scratch_shapes=[pltpu.VMEM((tm, c), F32), pltpu.VMEM((tm, c), F32), pltpu.VMEM((tm, c), F32),
                        pltpu.VMEM((SUBLANES, c), F32), pltpu.VMEM((SUBLANES, c), F32)],
        input_output_aliases={14: 0},
        compiler_params=pltpu.CompilerParams(dimension_semantics=("arbitrary",), vmem_limit_bytes=VMEM_LIMIT),
    )(proj, proj, proj, xc_all, h_all, h_all, dmixed, conv_w, wa, ba, wx, bx, lam, gain, dproj)


def _pair_sum(core, a, b, name):
    n, r, c = b.shape
    spec = pl.BlockSpec((None, r, c), lambda q, core: (q, 0, 0))

    def body(core_ref, a_ref, b_ref, o_ref):
        o_ref[...] = (a_ref[...].astype(F32) + b_ref[...].astype(F32)).astype(o_ref.dtype)

    return pl.pallas_call(
        body, name=name,
        grid_spec=pltpu.PrefetchScalarGridSpec(
            num_scalar_prefetch=1, grid=(n,),
            in_specs=[pl.BlockSpec((None, r, c), lambda q, core: (2 * q + core[0], 0, 0)), spec], out_specs=spec),
        out_shape=jax.ShapeDtypeStruct(b.shape, b.dtype),
        compiler_params=pltpu.CompilerParams(dimension_semantics=("arbitrary",), vmem_limit_bytes=VMEM_LIMIT),
    )(core, a, b)


ADAMW_BLOCK_BYTES = 4 * 1024 * 1024


def _sum_adamw(parts, w, m, v, name):
    n_parts, r, c = parts.shape
    tr = r
    while n_parts * tr * c * parts.dtype.itemsize > ADAMW_BLOCK_BYTES and tr % (4 * SUBLANES) == 0:
        tr //= 2

    def body(p_ref, w_ref, m_ref, v_ref, g_ref, d_ref, nm_ref, nv_ref):
        g = p_ref[0].astype(F32)
        for s in range(1, n_parts):
            g = g + p_ref[s].astype(F32)
        nm = ADAM_B1 * m_ref[...] + (1.0 - ADAM_B1) * g
        nv = ADAM_B2 * v_ref[...] + (1.0 - ADAM_B2) * (g * g)
        m_hat = nm / (1.0 - ADAM_B1 ** ADAM_STEP)
        v_hat = nv / (1.0 - ADAM_B2 ** ADAM_STEP)
        g_ref[...] = g
        d_ref[...] = -ADAM_LR * (m_hat / (jnp.sqrt(v_hat) + ADAM_EPS) + ADAM_WD * w_ref[...])
        nm_ref[...] = nm
        nv_ref[...] = nv

    row = pl.BlockSpec((tr, c), lambda i: (i, 0))
    return _call(body, name=name, grid=(r // tr,),
                 in_specs=[pl.BlockSpec((n_parts, tr, c), lambda i: (0, i, 0)), row, row, row],
                 out_specs=[row, row, row, row], out_shape=[jax.ShapeDtypeStruct((r, c), F32)] * 4,
                 operands=[parts, w, m, v])


MATRICES = ("w_in", "w_out", "ffn_up_w", "ffn_down_w")
CONVS = ("lru_conv_w", "ffn_conv_w")
REPLICATED = ("norm1_gain", "lru_conv_b", "lru_gate_a_w", "lru_gate_a_b", "lru_gate_x_w", "lru_gate_x_b", "lru_lambda",
              "lru_norm_gain", "ret_norm_gain", "norm2_gain", "ffn_conv_b", "final_norm_gain")
WEIGHTS = ("norm1_gain", "w_in", "lru_conv_w", "lru_conv_b", "lru_gate_a_w", "lru_gate_a_b", "lru_gate_x_w",
           "lru_gate_x_b", "lru_lambda", "lru_norm_gain", "ret_norm_gain", "w_out", "norm2_gain", "ffn_up_w",
           "ffn_conv_w", "ffn_conv_b", "ffn_down_w", "final_norm_gain")


def _rows(a, pad_to):
    a = a.reshape(-1, LANES)
    pad = (-a.shape[0]) % pad_to
    return jnp.pad(a, ((0, pad), (0, 0))) if pad else a


def _pack(arrays, pad_to):
    rows, layout, at = [], [], 0
    for a in arrays:
        r = _rows(a, pad_to)
        layout.append((at, a.size // LANES, a.shape))
        rows.append(r)
        at += r.shape[0]
    return jnp.concatenate(rows, axis=0), layout


def _unpack(packed, layout):
    lead = packed.shape[:-2]
    return [packed[..., at:at + n, :].reshape(lead + shape) for at, n, shape in layout]


def _conv_rows(lru, ffn, dtype, pad_to):
    lead = lru.shape[:-2]
    flat = jnp.concatenate([lru.reshape(lead + (-1,)), ffn.reshape(lead + (-1,))], axis=-1).astype(dtype)
    rows = flat.shape[-1] // LANES
    pad = (-rows) % pad_to
    return jnp.pad(flat.reshape(lead + (rows, LANES)), [(0, 0)] * len(lead) + [(0, pad), (0, 0)])


def _column_blocks(full):
    r, c = full.shape
    return full.reshape(r, N_DEV, c // N_DEV).transpose(1, 0, 2)


def _block_diag(w):
    nh, d, _ = w.shape
    eye = jnp.eye(nh, dtype=w.dtype)
    return (w[:, :, None, :] * eye[:, None, :, None]).reshape(nh * d, nh * d)


def _diag_blocks(dense, nh):
    d = dense.shape[0] // nh
    blocks = dense.reshape(nh, d, nh, d)
    return jnp.stack([blocks[h, :, h, :] for h in range(nh)], axis=0)


def kernel(x, norm1_gain, w_in, lru_conv_w, lru_conv_b, lru_gate_a_w, lru_gate_a_b, lru_gate_x_w, lru_gate_x_b, lru_lambda, lru_norm_gain, ret_norm_gain, w_out, norm2_gain, ffn_up_w, ffn_conv_w, ffn_conv_b, ffn_down_w, final_norm_gain, loss_target, m_norm1_gain, m_w_in, m_lru_conv_w, m_lru_conv_b, m_lru_gate_a_w, m_lru_gate_a_b, m_lru_gate_x_w, m_lru_gate_x_b, m_lru_lambda, m_lru_norm_gain, m_ret_norm_gain, m_w_out, m_norm2_gain, m_ffn_up_w, m_ffn_conv_w, m_ffn_conv_b, m_ffn_down_w, m_final_norm_gain, v_norm1_gain, v_w_in, v_lru_conv_w, v_lru_conv_b, v_lru_gate_a_w, v_lru_gate_a_b, v_lru_gate_x_w, v_lru_gate_x_b, v_lru_lambda, v_lru_norm_gain, v_ret_norm_gain, v_w_out, v_norm2_gain, v_ffn_up_w, v_ffn_conv_w, v_ffn_conv_b, v_ffn_down_w, v_final_norm_gain):
    args = dict(locals())
    given = {n: args[n] for n in WEIGHTS}
    out_shape = {n: given[n].shape for n in WEIGHTS}

    def plain(a):
        return a.reshape(1, -1) if a.ndim <= 2 else a[0]

    w = {n: plain(given[n]) for n in WEIGHTS}
    mom_m = {n: plain(args["m_" + n]) for n in WEIGHTS}
    mom_v = {n: plain(args["v_" + n]) for n in WEIGHTS}
    x2, target = x[0], loss_target[0]
    t = x2.shape[0]
    core = lax.axis_index("c").astype(jnp.int32).reshape(1)
    res = {}

    conv_pad = _conv_rows(w["lru_conv_w"], w["ffn_conv_w"], F32, SUBLANES)
    first = _gather_first([w["w_in"].astype(MXU_DTYPE), conv_pad])
    w_in_blocks, conv_all = _run_comms([first, _gather_second(first.out_shape)], "w_in_all_gather")
    n_lru = w["lru_conv_w"].size
    conv_flat = conv_all.reshape(N_DEV, -1)
    lru_cw = conv_flat[:, :n_lru].reshape((N_DEV,) + w["lru_conv_w"].shape).transpose(1, 0, 2).reshape(LRU_CONV, D_LRU)
    ffn_cw = conv_flat[:, n_lru:n_lru + w["ffn_conv_w"].size].reshape((N_DEV,) + w["ffn_conv_w"].shape)
    ffn_cw = ffn_cw.transpose(1, 0, 2).reshape(FFN_CONV, 2 * D_FF)

    cos2, sin_signed = _rope_tables(t)
    wa = _block_diag(w["lru_gate_a_w"]).astype(MXU_DTYPE)
    wx = _block_diag(w["lru_gate_x_w"]).astype(MXU_DTYPE)
    gf = w["final_norm_gain"]

    up_first = _gather_first([w["ffn_up_w"].astype(MXU_DTYPE)])
    (u1, proj), (up_part,) = _inproj_fwd(x2, w["norm1_gain"], w_in_blocks, up_first)
    rest_first = _gather_first([w["w_out"].astype(MXU_DTYPE), w["ffn_down_w"].astype(MXU_DTYPE)])

    (xc, h_lru, y_lru), lru_side = _lru_fwd(proj, lru_cw, w["lru_conv_b"], wa, w["lru_gate_a_b"], wx, w["lru_gate_x_b"],
                                            w["lru_lambda"], w["lru_norm_gain"],
                                            _both(rest_first, _gather_second([up_part])))
    w_out_part, down_part, up_blocks = lru_side
    (o_ret, y_ret, states), (w_out_blocks, down_blocks) = _ret_fwd(proj, cos2, sin_signed, w["ret_norm_gain"],
                                                                 _gather_second([w_out_part, down_part]))
    w_out_full = w_out_blocks.reshape(D_MODEL, D_MODEL)
    w_down_full = down_blocks.reshape(D_FF, D_MODEL)

    h1, u2 = _outproj_fwd(x2, y_lru, y_ret, w_out_full, w["norm2_gain"])
    up = _ffn_up_fwd(u2, up_blocks)
    act, dh2, dh2_b, dgf, loss = _ffn_fwd(up, ffn_cw, w["ffn_conv_b"], w_down_full, h1, gf, target)
    loss = lax.psum(loss[0, 0], ("x", "y", "c"))

    def to_owner_chips(blocks, names, tag):
        theirs = _run_comms([_pair_exchange(blocks)], "grads_pair_exchange_" + tag)
        return [_pair_sum(core, a, b, "grads_pair_sum_" + n) for n, a, b in zip(names, blocks, theirs)]

    def adamw(name, parts):
        res[name] = _sum_adamw(parts, w[name], mom_m[name], mom_v[name], "adamw_" + name)

    g = {"final_norm_gain": dgf[0]}
    g_down = _mm_tn(act, dh2_b, "ffn_down_wgrad").reshape(N_DEV, D_FF // N_DEV, D_MODEL)
    down_sums = to_owner_chips([g_down], ["ffn_down_w"], "down")
    (dup_a, dup_v, acc_a, acc_v), (down_parts,) = _ffn_bwd(dh2_b, w_down_full, up, ffn_cw, w["ffn_conv_b"],
                                                           _chip_exchange(down_sums))
    adamw("ffn_down_w", down_parts)
    acc = jnp.concatenate([acc_a, acc_v], axis=1)[::SUBLANES]
    g_ffn_cw, g["ffn_conv_b"] = acc[:FFN_CONV], acc[FFN_CONV:]
    g_up = jnp.concatenate([_mm_tn(u2, dup_a, "ffn_up_wgrad_a", blocks=N_DEV // 2),
                            _mm_tn(u2, dup_v, "ffn_up_wgrad_v", blocks=N_DEV // 2)], axis=0)
    up_sums = to_owner_chips([g_up], ["ffn_up_w"], "up")
    (dh1, dh1_b, dg2), (up_parts,) = _norm_bwd_matmul([dup_a, dup_v], up_blocks, h1, w["norm2_gain"], dh2, "ffn_up_bwd",
                                                      _chip_exchange(up_sums))
    adamw("ffn_up_w", up_parts)
    g["norm2_gain"] = dg2[:1]
    g_out = jnp.concatenate([_mm_tn(y_lru, dh1_b, "w_out_wgrad_lru"), _mm_tn(y_ret, dh1_b, "w_out_wgrad_ret")], axis=0)
    out_sums = to_owner_chips([g_out.reshape(N_DEV, D_MODEL // N_DEV, D_MODEL)], ["w_out"], "out")
    (dmixed,), (out_parts,) = _mm_nt(dh1_b, w_out_full, "outproj_bwd", F32, _chip_exchange(out_sums))
    adamw("w_out", out_parts)
    dproj, dgain_ret = _ret_bwd(proj, cos2, sin_signed, w["ret_norm_gain"], o_ret, states, dmixed, None)
    g["ret_norm_gain"] = dgain_ret[:1]
    dproj, lru_acc, dwa, dwx = _lru_bwd(proj, xc, h_lru, dmixed, dproj, lru_cw, wa, w["lru_gate_a_b"], wx,
                                        w["lru_gate_x_b"], w["lru_lambda"], w["lru_norm_gain"])
    lru_acc = lru_acc[::SUBLANES]
    g_lru_cw = lru_acc[:LRU_CONV]
    for name in ("conv_b", "gate_a_b", "gate_x_b", "lambda", "norm_gain"):
        g["lru_" + name] = lru_acc[LRU_ACC[name]:LRU_ACC[name] + 1]
    g["lru_gate_a_w"] = _diag_blocks(dwa, LRU_HEADS)
    g["lru_gate_x_w"] = _diag_blocks(dwx, LRU_HEADS)
    g_in = _mm_tn(u1, dproj, "w_in_wgrad", blocks=N_DEV)
    g_conv = _conv_rows(_column_blocks(g_lru_cw), _column_blocks(g_ffn_cw), GRAD_DTYPE, 2 * SUBLANES)
    in_sums = to_owner_chips([g_in, g_conv], ["w_in", "conv"], "in")
    (grad_x, _, dg1), (in_parts, conv_parts) = _norm_bwd_matmul([dproj], w_in_blocks, x2, w["norm1_gain"], dh1,
                                                                "inproj_bwd", _chip_exchange(in_sums))
    adamw("w_in", in_parts)
    g["norm1_gain"] = dg1[:1]

    pad16 = lambda d: _conv_rows(d["lru_conv_w"], d["ffn_conv_w"], F32, 2 * SUBLANES)
    conv_res = _sum_adamw(conv_parts, pad16(w), pad16(mom_m), pad16(mom_v), "adamw_conv")
    for n, lo, hi in (("lru_conv_w", 0, n_lru), ("ffn_conv_w", n_lru, n_lru + w["ffn_conv_w"].size)):
        res[n] = [r.reshape(-1)[lo:hi].reshape(w[n].shape) for r in conv_res]
    rep_packed, rep_layout = _pack([g[n] for n in REPLICATED], SUBLANES)
    rep_first = _gather_first([rep_packed])
    (rep_parts,) = _run_comms([rep_first, _gather_second(rep_first.out_shape)], "small_grads_all_gather")
    rep_res = _sum_adamw(rep_parts, *[_pack([d[n] for n in REPLICATED], SUBLANES)[0] for d in (w, mom_m, mom_v)],
                         "adamw_replicated")
    for k in range(4):
        for n, a in zip(REPLICATED, _unpack(rep_res[k], rep_layout)):
            res.setdefault(n, [None] * 4)[k] = a

    outs = [loss, grad_x[None]]
    for k in range(4):
        outs += [res[n][k].reshape(out_shape[n]) for n in WEIGHTS]
    return tuple(outs)
```

```python
import math

import numpy as np
import jax
import jax.numpy as jnp
from jax import lax
from jax.experimental import pallas as pl
from jax.experimental.pallas import tpu as pltpu

F32 = jnp.float32
BF16 = jnp.bfloat16
MXU_DTYPE = jnp.bfloat16
GRAD_DTYPE = jnp.bfloat16

N_DEV = 8
N_CHIPS = 4
D_MODEL = 1024
D_LRU = 512
LRU_HEADS = 8
LRU_CONV = 4
LRU_C = 8.0
D_RET = 512
RET_HEADS = 4
RET_HEAD_DIM = 128
RET_CHUNK = 128
ROPE_BASE = 10000.0
D_IN = 3072
D_FF = 3072
FFN_CONV = 3
NORM_EPS = 1e-6

ADAM_LR = 0.001
ADAM_B1 = 0.9
ADAM_B2 = 0.999
ADAM_EPS = 1e-08
ADAM_WD = 0.01
ADAM_STEP = 10

SUBLANES = 8
LANES = 128
VMEM_LIMIT = 48 * 1024 * 1024

MESH = pl.DeviceIdType.MESH
ANY = pl.BlockSpec(memory_space=pl.ANY)


def _dot(a, b):
    return jnp.dot(a.astype(MXU_DTYPE), b.astype(MXU_DTYPE), preferred_element_type=F32)


def _dot_nt(a, b):
    return lax.dot_general(a.astype(MXU_DTYPE), b.astype(MXU_DTYPE), (((1,), (1,)), ((), ())),
                           preferred_element_type=F32)


def _dot_tn(a, b):
    return lax.dot_general(a.astype(MXU_DTYPE), b.astype(MXU_DTYPE), (((0,), (0,)), ((), ())),
                           preferred_element_type=F32)


def _sigmoid(x):
    return 1.0 / (1.0 + jnp.exp(-x))


_GELU_C = math.sqrt(2.0 / math.pi)


def _gelu_parts(x):
    x2 = x * x
    t = jnp.tanh(_GELU_C * (x + 0.044715 * (x2 * x)))
    cdf = 0.5 * (1.0 + t)
    g = x * cdf
    dg = cdf + 0.5 * x * (1.0 - t * t) * (_GELU_C * (1.0 + 3.0 * 0.044715 * x2))
    return g, dg


def _gelu(x):
    t = jnp.tanh(_GELU_C * (x + 0.044715 * (x * x * x)))
    return x * (0.5 * (1.0 + t))


def _neg_expm1(x):
    series = x * (1.0 + x * (1.0 / 2.0) * (1.0 + x * (1.0 / 3.0) * (1.0 + x * (1.0 / 4.0) * (
        1.0 + x * (1.0 / 5.0) * (1.0 + x * (1.0 / 6.0) * (1.0 + x * (1.0 / 7.0)))))))
    return jnp.where(x > -0.25, -series, 1.0 - jnp.exp(x))


def _softplus(x):
    return jnp.maximum(x, 0.0) + jnp.log1p(jnp.exp(-jnp.abs(x)))


def _bcast_row(x, r, rows=SUBLANES):
    return jnp.broadcast_to(x[r:r + 1, :], (rows, x.shape[1]))


def _colsum8(x):
    return jnp.broadcast_to(jnp.sum(x, axis=0, keepdims=True), (SUBLANES, x.shape[1]))


def _shift_down(prev8, tile, s):
    if s == 0:
        return tile
    ext = jnp.concatenate([prev8, tile], axis=0)
    return pltpu.roll(ext, s, 0)[SUBLANES:, :]


def _shift_up(tile, next8, s):
    if s == 0:
        return tile
    ext = jnp.concatenate([tile, next8], axis=0)
    return pltpu.roll(ext, SUBLANES - s, 0)[SUBLANES:, :]


def _group_scan(a, b, reverse):
    n = a.shape[0]
    row = lax.broadcasted_iota(jnp.int32, a.shape, 0) & (SUBLANES - 1)
    for s in (1, 2, 4):
        shift = (n - s) if reverse else s
        a_sh = pltpu.roll(a, shift, 0)
        b_sh = pltpu.roll(b, shift, 0)
        m = (row <= SUBLANES - 1 - s) if reverse else (row >= s)
        b = jnp.where(m, a * b_sh + b, b)
        a = jnp.where(m, a * a_sh, a)
    return a, b


def _carry_scan(a_ref, b_ref, out_ref, carry0, reverse):
    n_groups = a_ref.shape[0] // SUBLANES

    def body(i, carry):
        g = (n_groups - 1 - i) if reverse else i
        r0 = pl.multiple_of(g * SUBLANES, SUBLANES)
        hg = a_ref[pl.ds(r0, SUBLANES), :] * carry + b_ref[pl.ds(r0, SUBLANES), :]
        out_ref[pl.ds(r0, SUBLANES), :] = hg
        return _bcast_row(hg, 0 if reverse else SUBLANES - 1)

    return lax.fori_loop(0, n_groups, body, carry0)


def _rms_fwd(h, gain):
    rstd = lax.rsqrt(jnp.mean(h * h, axis=-1, keepdims=True) + NORM_EPS)
    n = h * rstd
    return n, rstd, n * gain


def _rms_bwd(dy, n, rstd, gain):
    dn = dy * gain
    dh = rstd * (dn - n * jnp.mean(dn * n, axis=-1, keepdims=True))
    return dh, _colsum8(dy * n)


def _halo_rows(dtype):
    return SUBLANES * (4 // jnp.dtype(dtype).itemsize)


def _halo_map(tile_rows, col, halo_rows=SUBLANES):
    per = tile_rows // halo_rows
    return lambda i: (jnp.maximum(i * per - 1, 0), col)


def _resident(shape):
    return pl.BlockSpec(shape, lambda *_: (0,) * len(shape), pipeline_mode=pl.Buffered(1))


def _place():
    x, y, c = lax.axis_index("x"), lax.axis_index("y"), lax.axis_index("c")
    chips = [(1 - x, y), (x, 1 - y), (1 - x, 1 - y)]
    return x, y, c, chips


def _dev(x, y, c):
    return 4 * x + 2 * y + c


def _remote(src, dst, send_sem, recv_sem, to):
    return pltpu.make_async_remote_copy(src_ref=src, dst_ref=dst, send_sem=send_sem, recv_sem=recv_sem,
                                        device_id=to, device_id_type=MESH)


class _Comm:
    def __init__(self, operands, out_shape, sems, descs, aliases=()):
        self.operands, self.out_shape, self.sems, self.descs, self.aliases = operands, out_shape, sems, descs, aliases

    def start(self, ins, outs, sems):
        local, sends, _ = self.descs(ins, outs, sems)
        for cp in sends + local:
            cp.start()

    def wait(self, ins, outs, sems):
        local, sends, recvs = self.descs(ins, outs, sems)
        for cp in recvs:
            cp.wait_recv()
        for cp in sends:
            cp.wait_send()
        for cp in local:
            cp.wait()


def _gather_first(shards):
    n = len(shards)

    def descs(ins, outs, sems):
        send, recv, loc = sems
        x, y, c, chips = _place()
        me = _dev(x, y, c)
        targets = [(x, y, 1 - c)] + [(*chip, c) for chip in chips]
        local, sends, recvs = [], [], []
        for t in range(n):
            local.append(pltpu.make_async_copy(ins[t], outs[t].at[me], loc.at[t]))
            for k, to in enumerate(targets):
                i = 4 * t + k
                sends.append(_remote(ins[t], outs[t].at[me], send.at[i], recv.at[i], to))
                recvs.append(_remote(ins[t], outs[t].at[_dev(*to)], send.at[i], recv.at[i], to))
        return local, sends, recvs

    return _Comm(list(shards), [jax.ShapeDtypeStruct((N_DEV,) + s.shape, s.dtype) for s in shards],
                 [pltpu.SemaphoreType.DMA((4 * n,)), pltpu.SemaphoreType.DMA((4 * n,)), pltpu.SemaphoreType.DMA((n,))],
                 descs)


def _gather_second(gathered):
    n = len(gathered)

    def descs(ins, outs, sems):
        send, recv = sems
        x, y, c, chips = _place()
        sends, recvs = [], []
        for t in range(n):
            for j, chip in enumerate(chips):
                i = 3 * t + j
                have, get = _dev(*chip, c), _dev(*chip, 1 - c)
                sends.append(_remote(outs[t].at[have], outs[t].at[have], send.at[i], recv.at[i], (x, y, 1 - c)))
                recvs.append(_remote(outs[t].at[have], outs[t].at[get], send.at[i], recv.at[i], (x, y, 1 - c)))
        return [], sends, recvs

    return _Comm(list(gathered), [jax.ShapeDtypeStruct(g.shape, g.dtype) for g in gathered],
                 [pltpu.SemaphoreType.DMA((3 * n,)), pltpu.SemaphoreType.DMA((3 * n,))], descs,
                 aliases=[(t, t) for t in range(n)])


def _pair_exchange(blocks):
    n = len(blocks)

    def descs(ins, outs, sems):
        send, recv = sems
        x, y, c, _ = _place()
        sends, recvs = [], []
        for t in range(n):
            for q in range(N_CHIPS):
                i = N_CHIPS * t + q
                cp = _remote(ins[t].at[2 * q + 1 - c], outs[t].at[q], send.at[i], recv.at[i], (x, y, 1 - c))
                sends.append(cp)
                recvs.append(cp)
        return [], sends, recvs

    return _Comm(list(blocks), [jax.ShapeDtypeStruct((N_CHIPS,) + b.shape[1:], b.dtype) for b in blocks],
                 [pltpu.SemaphoreType.DMA((N_CHIPS * n,)), pltpu.SemaphoreType.DMA((N_CHIPS * n,))], descs)


def _chip_exchange(blocks):
    n = len(blocks)

    def descs(ins, outs, sems):
        send, recv, loc = sems
        x, y, c, chips = _place()
        me = 2 * x + y
        local, sends, recvs = [], [], []
        for t in range(n):
            local.append(pltpu.make_async_copy(ins[t].at[me], outs[t].at[me], loc.at[t]))
            for j, (px, py) in enumerate(chips):
                i = 3 * t + j
                q = 2 * px + py
                sends.append(_remote(ins[t].at[q], outs[t].at[me], send.at[i], recv.at[i], (px, py, c)))
                recvs.append(_remote(ins[t].at[q], outs[t].at[q], send.at[i], recv.at[i], (px, py, c)))
        return local, sends, recvs

    return _Comm(list(blocks), [jax.ShapeDtypeStruct(b.shape, b.dtype) for b in blocks],
                 [pltpu.SemaphoreType.DMA((3 * n,)), pltpu.SemaphoreType.DMA((3 * n,)), pltpu.SemaphoreType.DMA((n,))],
                 descs)


def _both(a, b):
    na, oa, sa = len(a.operands), len(a.out_shape), len(a.sems)

    def descs(ins, outs, sems):
        local_a, sends_a, recvs_a = a.descs(ins[:na], outs[:oa], sems[:sa])
        local_b, sends_b, recvs_b = b.descs(ins[na:], outs[oa:], sems[sa:])
        return local_a + local_b, sends_a + sends_b, recvs_a + recvs_b

    return _Comm(a.operands + b.operands, a.out_shape + b.out_shape, a.sems + b.sems, descs,
                 aliases=list(a.aliases) + [(na + i, oa + o) for i, o in b.aliases])


def _run_comms(comms, name):
    first = comms[0]
    n_in, n_out = len(first.operands), len(first.out_shape)

    def body(*refs):
        ins, outs, sems = refs[:n_in], refs[n_in:n_in + n_out], list(refs[n_in + n_out:])
        for k, comm in enumerate(comms):
            mine = [sems.pop(0) for _ in comm.sems]
            comm.start(ins if k == 0 else outs, outs, mine)
            comm.wait(ins if k == 0 else outs, outs, mine)

    outs = pl.pallas_call(
        body, name=name, out_shape=first.out_shape, in_specs=[ANY] * n_in, out_specs=[ANY] * n_out,
        scratch_shapes=[s for comm in comms for s in comm.sems],
    )(*first.operands)
    return list(outs)


def _call(body, *, name, grid, in_specs, out_specs, out_shape, operands, scratch_shapes=(), comm=None):
    sem = ("arbitrary",) * len(grid)
    params = pltpu.CompilerParams(dimension_semantics=sem, vmem_limit_bytes=VMEM_LIMIT)
    if comm is None:
        return pl.pallas_call(body, name=name, grid=grid, in_specs=in_specs, out_specs=out_specs, out_shape=out_shape,
                              scratch_shapes=list(scratch_shapes), compiler_params=params)(*operands)
    n_in, n_out, n_scr = len(in_specs), len(out_specs), len(scratch_shapes)
    c_in, c_out = len(comm.operands), len(comm.out_shape)

    def wrapped(*refs):
        refs = list(refs)
        ins, refs = refs[:n_in], refs[n_in:]
        cins, refs = refs[:c_in], refs[c_in:]
        outs, refs = refs[:n_out], refs[n_out:]
        couts, refs = refs[:c_out], refs[c_out:]
        scr, csems = refs[:n_scr], refs[n_scr:]
        first = last = None
        for axis, size in enumerate(grid):
            at_first, at_last = pl.program_id(axis) == 0, pl.program_id(axis) == size - 1
            first = at_first if first is None else first & at_first
            last = at_last if last is None else last & at_last

        @pl.when(first)
        def _():
            comm.start(cins, couts, csems)

        body(*ins, *outs, *scr)

        @pl.when(last)
        def _():
            comm.wait(cins, couts, csems)

    res = pl.pallas_call(
        wrapped, name=name, grid=grid, in_specs=list(in_specs) + [ANY] * c_in, out_specs=list(out_specs) + [ANY] * c_out,
        out_shape=list(out_shape) + list(comm.out_shape), scratch_shapes=list(scratch_shapes) + list(comm.sems),
        input_output_aliases={n_in + i: n_out + o for i, o in comm.aliases}, compiler_params=params,
    )(*operands, *comm.operands)
    return list(res[:n_out]), list(res[n_out:])


def _mm_nt(a, b, name, out_dtype, comm=None, tm=512, tn=512):
    m, k = a.shape
    n = b.shape[0]
    tm, tn = min(tm, m), min(tn, n)

    def body(a_ref, b_ref, o_ref):
        o_ref[...] = _dot_nt(a_ref[...], b_ref[...]).astype(o_ref.dtype)

    return _call(body, name=name, grid=(n // tn, m // tm), comm=comm,
                 in_specs=[pl.BlockSpec((tm, k), lambda j, i: (i, 0)), pl.BlockSpec((tn, k), lambda j, i: (j, 0))],
                 out_specs=[pl.BlockSpec((tm, tn), lambda j, i: (i, j))],
                 out_shape=[jax.ShapeDtypeStruct((m, n), out_dtype)], operands=[a, b])


def _mm_tn(a, b, name, blocks=1, tk=2048):
    t, m = a.shape
    n = b.shape[1]
    tk = min(tk, t)
    nk = t // tk
    cb = n // blocks
    per = max(1, 768 // cb) if blocks > 1 else 1
    tn = per * cb if blocks > 1 else min(1024, n)
    tm = min(1024, m)
    assert blocks == 1 or tm == m

    def body(a_ref, b_ref, o_ref, acc):
        k = pl.program_id(2)

        @pl.when(k == 0)
        def _():
            acc[...] = jnp.zeros_like(acc)
        acc[...] += _dot_tn(a_ref[...], b_ref[...])

        @pl.when(k == nk - 1)
        def _():
            if blocks == 1:
                o_ref[...] = acc[...].astype(o_ref.dtype)
            else:
                for s in range(per):
                    o_ref[s] = acc[:, s * cb:(s + 1) * cb].astype(o_ref.dtype)

    if blocks == 1:
        out_spec = pl.BlockSpec((tm, tn), lambda i, j, k: (i, j))
        out_shape = jax.ShapeDtypeStruct((m, n), GRAD_DTYPE)
    else:
        out_spec = pl.BlockSpec((per, m, cb), lambda i, j, k: (j, 0, 0))
        out_shape = jax.ShapeDtypeStruct((blocks, m, cb), GRAD_DTYPE)
    return _call(body, name=name, grid=(m // tm, n // tn, nk),
                 in_specs=[pl.BlockSpec((tk, tm), lambda i, j, k: (k, i)), pl.BlockSpec((tk, tn), lambda i, j, k: (k, j))],
                 out_specs=[out_spec], out_shape=[out_shape], operands=[a, b],
                 scratch_shapes=[pltpu.VMEM((tm, tn), F32)])[0]


def _inproj_fwd(x, g1, w_blocks, comm):
    t = x.shape[0]
    tm = min(512, t)
    nb, _, cb = w_blocks.shape

    def body(x_ref, g_ref, w_ref, u_ref, p_ref):
        _, _, u = _rms_fwd(x_ref[...], g_ref[...])
        u = u.astype(MXU_DTYPE)
        u_ref[...] = u
        for d in range(nb):
            p_ref[:, d * cb:(d + 1) * cb] = _dot(u, w_ref[d])

    return _call(body, name="inproj_fwd", grid=(t // tm,), comm=comm,
                 in_specs=[pl.BlockSpec((tm, D_MODEL), lambda i: (i, 0)), pl.BlockSpec((1, D_MODEL), lambda i: (0, 0)),
                           _resident(w_blocks.shape)],
                 out_specs=[pl.BlockSpec((tm, D_MODEL), lambda i: (i, 0)), pl.BlockSpec((tm, D_IN), lambda i: (i, 0))],
                 out_shape=[jax.ShapeDtypeStruct((t, D_MODEL), MXU_DTYPE), jax.ShapeDtypeStruct((t, D_IN), F32)],
                 operands=[x, g1, w_blocks])


def _lru_gates(xc, wa, ba, wx, bx, sp):
    r = _sigmoid(_dot(xc, wa) + ba)
    ig = _sigmoid(_dot(xc, wx) + bx)
    log_a = (-LRU_C) * r * sp
    a = jnp.exp(log_a)
    m = jnp.sqrt(_neg_expm1(2.0 * log_a))
    return r, ig, a, m


def _lru_fwd(proj, conv_w, conv_b, wa, ba, wx, bx, lam, gain, comm):
    t = proj.shape[0]
    tm = min(256, t)
    c = D_LRU

    def body(x_ref, xh_ref, g_ref, cw_ref, cb_ref, wa_ref, ba_ref, wx_ref, bx_ref, lam_ref, gain_ref,
             xc_ref, h_ref, y_ref, a_scr, b_scr, carry):
        i = pl.program_id(0)

        @pl.when(i == 0)
        def _():
            carry[...] = jnp.zeros_like(carry)

        x = x_ref[...]
        prev = jnp.where(i == 0, 0.0, xh_ref[...])
        cw = cw_ref[...]
        xc = cb_ref[...] + cw[LRU_CONV - 1:LRU_CONV, :] * x
        for k in range(LRU_CONV - 1):
            xc = xc + cw[k:k + 1, :] * _shift_down(prev, x, LRU_CONV - 1 - k)
        xc_ref[...] = xc
        sp = _softplus(-lam_ref[...])
        _, ig, a, m = _lru_gates(xc, wa_ref[...], ba_ref[...], wx_ref[...], bx_ref[...], sp)
        ga, gb = _group_scan(a, m * (ig * xc), reverse=False)
        a_scr[...] = ga
        b_scr[...] = gb
        carry[...] = _carry_scan(a_scr, b_scr, h_ref, carry[...], reverse=False)
        z = h_ref[...] * _gelu(g_ref[...])
        _, _, y = _rms_fwd(z, gain_ref[...])
        y_ref[...] = y.astype(y_ref.dtype)

    row = lambda i: (i, 0)
    full = lambda i: (0, 0)
    vec = pl.BlockSpec((1, c), full)
    return _call(body, name="lru_fwd", grid=(t // tm,), comm=comm,
                 in_specs=[pl.BlockSpec((tm, c), row), pl.BlockSpec((SUBLANES, c), _halo_map(tm, 0)),
                           pl.BlockSpec((tm, c), lambda i: (i, 1)),
                           pl.BlockSpec((LRU_CONV, c), full), vec, pl.BlockSpec((c, c), full), vec,
                           pl.BlockSpec((c, c), full), vec, vec, vec],
                 out_specs=[pl.BlockSpec((tm, c), row), pl.BlockSpec((tm, c), row), pl.BlockSpec((tm, c), row)],
                 out_shape=[jax.ShapeDtypeStruct((t, c), F32), jax.ShapeDtypeStruct((t, c), F32),
                            jax.ShapeDtypeStruct((t, c), MXU_DTYPE)],
                 scratch_shapes=[pltpu.VMEM((tm, c), F32), pltpu.VMEM((tm, c), F32), pltpu.VMEM((SUBLANES, c), F32)],
                 operands=[proj, proj, proj, conv_w, conv_b, wa, ba, wx, bx, lam, gain])


def _ret_consts():
    c = RET_CHUNK
    log_g = jnp.log1p(-jnp.exp2(-5.0 - jnp.arange(RET_HEADS, dtype=F32)))
    idx = jnp.arange(c, dtype=F32)
    diff = idx[:, None] - idx[None, :]
    decay = jnp.where(diff[None] >= 0, jnp.exp(jnp.maximum(diff, 0.0)[None] * log_g[:, None, None]), 0.0)
    zeta = jnp.exp((c - 1 - idx)[None, :] * log_g[:, None])
    xi = jnp.exp((idx + 1.0)[None, :] * log_g[:, None])
    spread = lambda v: jnp.repeat(v.T, RET_HEAD_DIM, axis=1)
    log_g_np = np.log1p(-np.exp2(-5.0 - np.arange(RET_HEADS, dtype=np.float32))).astype(np.float32)
    g_chunk = [float(np.exp(np.float32(c) * lg)) for lg in log_g_np]
    return decay, spread(xi), spread(zeta), g_chunk


def _rope_tables(t):
    pos = jnp.arange(t, dtype=F32)
    inv_freq = ROPE_BASE ** (-jnp.arange(0, RET_HEAD_DIM, 2, dtype=F32) / RET_HEAD_DIM)
    ang = pos[:, None] * inv_freq[None, :]
    cos, sin = jnp.cos(ang), jnp.sin(ang)
    return jnp.concatenate([cos, cos], axis=-1), jnp.concatenate([-sin, sin], axis=-1)


def _rope(x, cos2, sin_signed):
    return x * cos2 + pltpu.roll(x, RET_HEAD_DIM // 2, 1) * sin_signed


def _rope_bwd(d, cos2, sin_signed):
    return d * cos2 + pltpu.roll(d * sin_signed, RET_HEAD_DIM // 2, 1)


RET_SCALE = RET_HEAD_DIM ** -0.5


def _ret_fwd(proj, cos2, sin_signed, gain, comm):
    t = proj.shape[0]
    c, d, nh = RET_CHUNK, RET_HEAD_DIM, RET_HEADS
    n_chunks = t // c
    decay, xi, zeta, g_chunk = _ret_consts()

    def body(qk_ref, vg_ref, cos_ref, sin_ref, dec_ref, xi_ref, zeta_ref, gain_ref, o_ref, y_ref, st_ref, state):
        @pl.when(pl.program_id(0) == 0)
        def _():
            state[...] = jnp.zeros_like(state)

        cos2, sin_s = cos_ref[...], sin_ref[...]
        for h in range(nh):
            lo = h * d
            q = _rope(qk_ref[:, lo:lo + d], cos2, sin_s)
            k = _rope(qk_ref[:, D_RET + lo:D_RET + lo + d], cos2, sin_s) * RET_SCALE
            v = vg_ref[:, lo:lo + d]
            g = vg_ref[:, D_RET + lo:D_RET + lo + d]
            s_prev = state[h]
            st_ref[0, h] = s_prev
            scores = _dot_nt(q, k) * dec_ref[h]
            o = _dot(scores, v) + _dot(q * xi_ref[:, lo:lo + d], s_prev)
            state[h] = s_prev * g_chunk[h] + _dot_tn(k * zeta_ref[:, lo:lo + d], v)
            o_ref[:, lo:lo + d] = o
            mu = jnp.mean(o, axis=-1, keepdims=True)
            oc = o - mu
            on = oc * lax.rsqrt(jnp.mean(oc * oc, axis=-1, keepdims=True) + NORM_EPS)
            y_ref[:, lo:lo + d] = (on * gain_ref[:, lo:lo + d] * (g * _sigmoid(g))).astype(y_ref.dtype)

    full2 = lambda i: (0, 0)
    return _call(body, name="ret_fwd", grid=(n_chunks,), comm=comm,
                 in_specs=[pl.BlockSpec((c, 2 * D_RET), lambda i: (i, 1)), pl.BlockSpec((c, 2 * D_RET), lambda i: (i, 2)),
                           pl.BlockSpec((c, d), lambda i: (i, 0)), pl.BlockSpec((c, d), lambda i: (i, 0)),
                           pl.BlockSpec((nh, c, c), lambda i: (0, 0, 0)), pl.BlockSpec((c, D_RET), full2),
                           pl.BlockSpec((c, D_RET), full2), pl.BlockSpec((1, D_RET), full2)],
                 out_specs=[pl.BlockSpec((c, D_RET), lambda i: (i, 0)), pl.BlockSpec((c, D_RET), lambda i: (i, 0)),
                            pl.BlockSpec((1, nh, d, d), lambda i: (i, 0, 0, 0))],
                 out_shape=[jax.ShapeDtypeStruct((t, D_RET), F32), jax.ShapeDtypeStruct((t, D_RET), MXU_DTYPE),
                            jax.ShapeDtypeStruct((n_chunks, nh, d, d), F32)],
                 scratch_shapes=[pltpu.VMEM((nh, d, d), F32)],
                 operands=[proj, proj, cos2, sin_signed, decay, xi, zeta, gain])


def _outproj_fwd(x, y_lru, y_ret, w_out, g2):
    t = x.shape[0]
    tm = min(512, t)

    def body(x_ref, yl_ref, yr_ref, w_ref, g_ref, h1_ref, u2_ref):
        h1 = x_ref[...] + _dot(yl_ref[...], w_ref[:D_LRU, :]) + _dot(yr_ref[...], w_ref[D_LRU:, :])
        h1_ref[...] = h1
        _, _, u = _rms_fwd(h1, g_ref[...])
        u2_ref[...] = u.astype(u2_ref.dtype)

    row = lambda i: (i, 0)
    return _call(body, name="outproj_fwd", grid=(t // tm,),
                 in_specs=[pl.BlockSpec((tm, D_MODEL), row), pl.BlockSpec((tm, D_LRU), row), pl.BlockSpec((tm, D_RET), row),
                           _resident((D_MODEL, D_MODEL)), pl.BlockSpec((1, D_MODEL), lambda i: (0, 0))],
                 out_specs=[pl.BlockSpec((tm, D_MODEL), row), pl.BlockSpec((tm, D_MODEL), row)],
                 out_shape=[jax.ShapeDtypeStruct((t, D_MODEL), F32), jax.ShapeDtypeStruct((t, D_MODEL), MXU_DTYPE)],
                 operands=[x, y_lru, y_ret, w_out, g2])


FFN_TN = 768
FFN_NJ = D_FF // FFN_TN


def _ffn_fwd(u2, w_blocks, conv_w, conv_b, w_down, h1, gf, target):
    t = u2.shape[0]
    tm = min(256, t)
    tn, nj = FFN_TN, FFN_NJ
    hb = _halo_rows(u2.dtype)
    assert w_blocks.shape == (2 * nj, D_MODEL, tn)

    def project(u_ext, w, up_ref, conv_ref, cw_ref, cb_ref, first):
        ext = _dot(u_ext, w).astype(up_ref.dtype)
        up_ref[...] = ext[hb:, :]
        ext = ext.astype(F32)
        x = ext[hb:, :]
        prev = jnp.where(first, 0.0, ext[hb - SUBLANES:hb, :])
        cw = cw_ref[...]
        y = cb_ref[...] + cw[FFN_CONV - 1:FFN_CONV, :] * x
        for k in range(FFN_CONV - 1):
            y = y + cw[k:k + 1, :] * _shift_down(prev, x, FFN_CONV - 1 - k)
        conv_ref[...] = y.astype(conv_ref.dtype)
        return y

    def body(u_ref, uh_ref, w_ref, cwa_ref, cwv_ref, cba_ref, cbv_ref, wd_ref, h1_ref, gf_ref, tg_ref,
             upa_ref, upv_ref, ca_ref, cv_ref, act_ref, dh_ref, dhb_ref, dgf_ref, loss_ref, acc):
        i, j = pl.program_id(0), pl.program_id(1)

        @pl.when((i == 0) & (j == 0))
        def _():
            dgf_ref[...] = jnp.zeros_like(dgf_ref)
            loss_ref[...] = jnp.zeros_like(loss_ref)

        @pl.when(j == 0)
        def _():
            acc[...] = jnp.zeros_like(acc)

        u_ext = jnp.concatenate([uh_ref[...], u_ref[...]], axis=0)
        a = project(u_ext, w_ref[j], upa_ref, ca_ref, cwa_ref, cba_ref, i == 0)
        v = project(u_ext, w_ref[nj + j], upv_ref, cv_ref, cwv_ref, cbv_ref, i == 0)
        act = (_gelu(a) * v).astype(act_ref.dtype)
        act_ref[...] = act
        acc[...] += _dot(act, wd_ref[pl.ds(pl.multiple_of(j * tn, tn), tn), :])

        @pl.when(j == nj - 1)
        def _():
            n, rstd, y = _rms_fwd(h1_ref[...] + acc[...], gf_ref[...])
            err = y - tg_ref[...]
            loss_ref[...] += (0.5 / D_MODEL) * jnp.sum(err * err)
            dh, dgf = _rms_bwd(err * (1.0 / D_MODEL), n, rstd, gf_ref[...])
            dgf_ref[...] += dgf
            dh_ref[...] = dh
            dhb_ref[...] = dh.astype(dhb_ref.dtype)

    per = tm // hb
    row = lambda i, j: (i, 0)
    const = lambda i, j: (0, 0)
    tile = pl.BlockSpec((tm, tn), lambda i, j: (i, j))
    return _call(body, name="ffn_fwd", grid=(t // tm, nj),
                 in_specs=[pl.BlockSpec((tm, D_MODEL), row),
                           pl.BlockSpec((hb, D_MODEL), lambda i, j: (jnp.maximum(i * per - 1, 0), 0)),
                           _resident(w_blocks.shape),
                           pl.BlockSpec((FFN_CONV, tn), lambda i, j: (0, j)),
                           pl.BlockSpec((FFN_CONV, tn), lambda i, j: (0, j + nj)),
                           pl.BlockSpec((1, tn), lambda i, j: (0, j)), pl.BlockSpec((1, tn), lambda i, j: (0, j + nj)),
                           _resident((D_FF, D_MODEL)),
                           pl.BlockSpec((tm, D_MODEL), row), pl.BlockSpec((1, D_MODEL), const),
                           pl.BlockSpec((tm, D_MODEL), row)],
                 out_specs=[tile] * 5 + [pl.BlockSpec((tm, D_MODEL), row),
                            pl.BlockSpec((tm, D_MODEL), row), pl.BlockSpec((SUBLANES, D_MODEL), const),
                            pl.BlockSpec((SUBLANES, LANES), const)],
                 out_shape=[jax.ShapeDtypeStruct((t, D_FF), MXU_DTYPE)] * 5 + [
                            jax.ShapeDtypeStruct((t, D_MODEL), F32),
                            jax.ShapeDtypeStruct((t, D_MODEL), MXU_DTYPE), jax.ShapeDtypeStruct((SUBLANES, D_MODEL), F32),
                            jax.ShapeDtypeStruct((SUBLANES, LANES), F32)],
                 scratch_shapes=[pltpu.VMEM((tm, D_MODEL), F32)],
                 operands=[u2, u2, w_blocks, conv_w, conv_w, conv_b, conv_b, w_down, h1, gf, target])


FFN_ACC_ROWS = SUBLANES * (FFN_CONV + 1)


def _ffn_bwd(dh2_b, w_down, up_a, up_v, conv_a, conv_v, conv_w, comm):
    t = up_a.shape[0]
    tm = min(256, t)
    tn, nj = FFN_TN, FFN_NJ
    ni = t // tm

    def conv_bwd(dy, x, cw, acc_ref, carry_ref, dup_ref):
        nxt = carry_ref[...]
        carry_ref[...] = dy[:SUBLANES, :]
        ahead = [_shift_up(dy, nxt, FFN_CONV - 1 - k) for k in range(FFN_CONV)]
        dx = cw[FFN_CONV - 1:FFN_CONV, :] * dy
        for k in range(FFN_CONV - 1):
            dx = dx + cw[k:k + 1, :] * ahead[k]
        dup_ref[...] = dx.astype(dup_ref.dtype)
        for k in range(FFN_CONV):
            acc_ref[k * SUBLANES:(k + 1) * SUBLANES, :] += _colsum8(ahead[k] * x)
        acc_ref[FFN_CONV * SUBLANES:, :] += _colsum8(dy)

    def body(dh_ref, wd_ref, ua_ref, uv_ref, ca_ref, cv_ref, cwa_ref, cwv_ref,
             dua_ref, duv_ref, acca_ref, accv_ref, carry_a, carry_v):
        i = pl.program_id(1)

        @pl.when(i == 0)
        def _():
            for ref in (acca_ref, accv_ref, carry_a, carry_v):
                ref[...] = jnp.zeros_like(ref)

        v = cv_ref[...].astype(F32)
        g, dg = _gelu_parts(ca_ref[...].astype(F32))
        dact = _dot_nt(dh_ref[...], wd_ref[...])
        conv_bwd(dact * v * dg, ua_ref[...].astype(F32), cwa_ref[...], acca_ref, carry_a, dua_ref)
        conv_bwd(dact * g, uv_ref[...].astype(F32), cwv_ref[...], accv_ref, carry_v, duv_ref)

    rev = lambda j, i: (ni - 1 - i, j)
    colj = lambda off: (lambda j, i: (0, j + off))
    tile = pl.BlockSpec((tm, tn), rev)
    return _call(body, name="ffn_bwd", grid=(nj, ni), comm=comm,
                 in_specs=[pl.BlockSpec((tm, D_MODEL), lambda j, i: (ni - 1 - i, 0)),
                           pl.BlockSpec((tn, D_MODEL), lambda j, i: (j, 0)), tile, tile, tile, tile,
                           pl.BlockSpec((FFN_CONV, tn), colj(0)), pl.BlockSpec((FFN_CONV, tn), colj(nj))],
                 out_specs=[tile, tile,
                            pl.BlockSpec((FFN_ACC_ROWS, tn), colj(0)), pl.BlockSpec((FFN_ACC_ROWS, tn), colj(0))],
                 out_shape=[jax.ShapeDtypeStruct((t, D_FF), MXU_DTYPE), jax.ShapeDtypeStruct((t, D_FF), MXU_DTYPE),
                            jax.ShapeDtypeStruct((FFN_ACC_ROWS, D_FF), F32), jax.ShapeDtypeStruct((FFN_ACC_ROWS, D_FF), F32)],
                 scratch_shapes=[pltpu.VMEM((SUBLANES, tn), F32), pltpu.VMEM((SUBLANES, tn), F32)],
                 operands=[dh2_b, w_down, up_a, up_v, conv_a, conv_v, conv_w, conv_w])


def _norm_bwd_matmul(parts, w_blocks, h, gain, d_res, name, comm):
    t = h.shape[0]
    tm = min(256, t)
    n_parts = len(parts)
    nb, _, cb = w_blocks.shape
    where = []
    for p, a in enumerate(parts):
        assert a.shape[1] % cb == 0
        where += [(p, lo) for lo in range(0, a.shape[1], cb)]
    assert len(where) == nb

    def body(*refs):
        a_refs = refs[:n_parts]
        w_ref, h_ref, g_ref, dres_ref, dh_ref, dhb_ref, dg_ref = refs[n_parts:]

        @pl.when(pl.program_id(0) == 0)
        def _():
            dg_ref[...] = jnp.zeros_like(dg_ref)

        du = None
        for d, (p, lo) in enumerate(where):
            term = _dot_nt(a_refs[p][:, lo:lo + cb], w_ref[d])
            du = term if du is None else du + term
        n, rstd, _ = _rms_fwd(h_ref[...], g_ref[...])
        dh, dg = _rms_bwd(du, n, rstd, g_ref[...])
        dh = dh + dres_ref[...]
        dg_ref[...] += dg
        dh_ref[...] = dh
        dhb_ref[...] = dh.astype(dhb_ref.dtype)

    row = lambda i: (i, 0)
    const = lambda i: (0, 0)
    return _call(body, name=name, grid=(t // tm,), comm=comm,
                 in_specs=[pl.BlockSpec((tm, a.shape[1]), row) for a in parts] + [
                     _resident(w_blocks.shape), pl.BlockSpec((tm, D_MODEL), row), pl.BlockSpec((1, D_MODEL), const),
                     pl.BlockSpec((tm, D_MODEL), row)],
                 out_specs=[pl.BlockSpec((tm, D_MODEL), row), pl.BlockSpec((tm, D_MODEL), row),
                            pl.BlockSpec((SUBLANES, D_MODEL), const)],
                 out_shape=[jax.ShapeDtypeStruct((t, D_MODEL), F32), jax.ShapeDtypeStruct((t, D_MODEL), MXU_DTYPE),
                            jax.ShapeDtypeStruct((SUBLANES, D_MODEL), F32)],
                 operands=[*parts, w_blocks, h, gain, d_res])


def _ret_bwd(proj, cos2, sin_signed, gain, o, states, dmixed, comm):
    t = proj.shape[0]
    c, d, nh = RET_CHUNK, RET_HEAD_DIM, RET_HEADS
    n_chunks = t // c
    decay, xi, zeta, g_chunk = _ret_consts()
    base = 2 * D_LRU

    def body(qk_ref, vg_ref, cos_ref, sin_ref, dec_ref, xi_ref, zeta_ref, gain_ref, o_ref, st_ref, dy_ref,
             dp_ref, dgain_ref, gstate):
        @pl.when(pl.program_id(0) == 0)
        def _():
            gstate[...] = jnp.zeros_like(gstate)
            dgain_ref[...] = jnp.zeros_like(dgain_ref)

        cos2, sin_s = cos_ref[...], sin_ref[...]
        for h in range(nh):
            lo = h * d
            q = _rope(qk_ref[:, lo:lo + d], cos2, sin_s)
            k = _rope(qk_ref[:, D_RET + lo:D_RET + lo + d], cos2, sin_s) * RET_SCALE
            v = vg_ref[:, lo:lo + d]
            g = vg_ref[:, D_RET + lo:D_RET + lo + d]
            gain_h = gain_ref[:, lo:lo + d]
            xi_h, zeta_h, dec = xi_ref[:, lo:lo + d], zeta_ref[:, lo:lo + d], dec_ref[h]
            dy = dy_ref[:, lo:lo + d]
            sg = _sigmoid(g)
            o_h = o_ref[:, lo:lo + d]
            oc = o_h - jnp.mean(o_h, axis=-1, keepdims=True)
            rstd = lax.rsqrt(jnp.mean(oc * oc, axis=-1, keepdims=True) + NORM_EPS)
            on = oc * rstd
            at = base + 3 * D_RET + lo
            dp_ref[:, at:at + d] = (dy * on * gain_h * (sg * (1.0 + g * (1.0 - sg)))).astype(dp_ref.dtype)
            don_g = dy * (g * sg)
            dgain_ref[:, lo:lo + d] += _colsum8(don_g * on)
            don = don_g * gain_h
            do = rstd * (don - jnp.mean(don, axis=-1, keepdims=True) - on * jnp.mean(don * on, axis=-1, keepdims=True))
            s_prev = st_ref[0, h]
            g_next = gstate[h]
            p = _dot_nt(q, k) * dec
            dpm = _dot_nt(do, v) * dec
            dq = _dot(dpm, k) + _dot_nt(do, s_prev) * xi_h
            dk = _dot_tn(dpm, q) + _dot_nt(v, g_next) * zeta_h
            dv = _dot_tn(p, do) + _dot(k * zeta_h, g_next)
            gstate[h] = g_next * g_chunk[h] + _dot_tn(q * xi_h, do)
            dp_ref[:, base + lo:base + lo + d] = _rope_bwd(dq, cos2, sin_s).astype(dp_ref.dtype)
            at = base + D_RET + lo
            dp_ref[:, at:at + d] = _rope_bwd(dk * RET_SCALE, cos2, sin_s).astype(dp_ref.dtype)
            at = base + 2 * D_RET + lo
            dp_ref[:, at:at + d] = dv.astype(dp_ref.dtype)

    rev = lambda col: (lambda i: (n_chunks - 1 - i, col))
    full2 = lambda i: (0, 0)
    return _call(body, name="ret_bwd", grid=(n_chunks,), comm=comm,
                 in_specs=[pl.BlockSpec((c, 2 * D_RET), rev(1)), pl.BlockSpec((c, 2 * D_RET), rev(2)),
                           pl.BlockSpec((c, d), rev(0)), pl.BlockSpec((c, d), rev(0)),
                           pl.BlockSpec((nh, c, c), lambda i: (0, 0, 0)), pl.BlockSpec((c, D_RET), full2),
                           pl.BlockSpec((c, D_RET), full2), pl.BlockSpec((1, D_RET), full2),
                           pl.BlockSpec((c, D_RET), rev(0)),
                           pl.BlockSpec((1, nh, d, d), lambda i: (n_chunks - 1 - i, 0, 0, 0)),
                           pl.BlockSpec((c, D_RET), rev(1))],
                 out_specs=[pl.BlockSpec((c, D_IN), rev(0)), pl.BlockSpec((SUBLANES, D_RET), full2)],
                 out_shape=[jax.ShapeDtypeStruct((t, D_IN), MXU_DTYPE), jax.ShapeDtypeStruct((SUBLANES, D_RET), F32)],
                 scratch_shapes=[pltpu.VMEM((nh, d, d), F32)],
                 operands=[proj, proj, cos2, sin_signed, decay, xi, zeta, gain, o, states, dmixed])


LRU_ACC = {"conv_w": 0, "conv_b": LRU_CONV, "gate_a_b": LRU_CONV + 1, "gate_x_b": LRU_CONV + 2,
           "lambda": LRU_CONV + 3, "norm_gain": LRU_CONV + 4}
LRU_ACC_ROWS = SUBLANES * (LRU_CONV + 5)


def _lru_bwd(proj, xc_all, h_all, dmixed, dproj, conv_w, wa, ba, wx, bx, lam, gain):
    t = proj.shape[0]
    tm = min(256, t)
    c = D_LRU
    ni = t // tm

    def body(x_ref, xh_ref, g_ref, xc_ref, h_ref, hh_ref, dy_ref, cw_ref, wa_ref, ba_ref, wx_ref, bx_ref, lam_ref,
             gain_ref, dproj_in, dp_ref, acc_ref, dwa_ref, dwx_ref, a_scr, b_scr, mu_scr, carry_mu, carry_dxc):
        del dproj_in
        i = pl.program_id(0)
        r = ni - 1 - i

        @pl.when(i == 0)
        def _():
            acc_ref[...] = jnp.zeros_like(acc_ref)
            dwa_ref[...] = jnp.zeros_like(dwa_ref)
            dwx_ref[...] = jnp.zeros_like(dwx_ref)
            carry_mu[...] = jnp.zeros_like(carry_mu)
            carry_dxc[...] = jnp.zeros_like(carry_dxc)

        def add(name, val, k=0):
            lo = (LRU_ACC[name] + k) * SUBLANES
            acc_ref[lo:lo + SUBLANES, :] += _colsum8(val)

        xc, h = xc_ref[...], h_ref[...]
        lam_v = lam_ref[...]
        sp = _softplus(-lam_v)
        rg, ig, a, m = _lru_gates(xc, wa_ref[...], ba_ref[...], wx_ref[...], bx_ref[...], sp)
        gl, dgl = _gelu_parts(g_ref[...])
        zn, rstd, _ = _rms_fwd(h * gl, gain_ref[...])
        dy = dy_ref[...]
        dz, dgain = _rms_bwd(dy, zn, rstd, gain_ref[...])
        lo = LRU_ACC["norm_gain"] * SUBLANES
        acc_ref[lo:lo + SUBLANES, :] += dgain
        dp_ref[:, c:] = (dz * h * dgl).astype(dp_ref.dtype)
        dh = dz * gl
        ga, gb = _group_scan(a, a * dh, reverse=True)
        a_scr[...] = ga
        b_scr[...] = gb
        mu_next_tile = carry_mu[...]
        carry_mu[...] = _carry_scan(a_scr, b_scr, mu_scr, mu_next_tile, reverse=True)
        lam_t = dh + _shift_up(mu_scr[...], mu_next_tile, 1)
        h_prev = _shift_down(jnp.where(r == 0, 0.0, hh_ref[...]), h, 1)
        da = lam_t * h_prev
        dig = lam_t * m * xc
        dxc = lam_t * m * ig
        dlog_a = da * a - (lam_t * ig * xc) * (a * a) / m
        dpr = dlog_a * ((-LRU_C) * sp) * rg * (1.0 - rg)
        add("lambda", dlog_a * ((-LRU_C) * rg) * (-_sigmoid(-lam_v)))
        dpi = dig * ig * (1.0 - ig)
        add("gate_a_b", dpr)
        add("gate_x_b", dpi)
        dwa_ref[...] += _dot_tn(xc, dpr)
        dwx_ref[...] += _dot_tn(xc, dpi)
        dxc = dxc + _dot_nt(dpr, wa_ref[...]) + _dot_nt(dpi, wx_ref[...])
        add("conv_b", dxc)
        x = x_ref[...]
        prev = jnp.where(r == 0, 0.0, xh_ref[...])
        cw = cw_ref[...]
        nxt = carry_dxc[...]
        carry_dxc[...] = dxc[:SUBLANES, :]
        dx = cw[LRU_CONV - 1:LRU_CONV, :] * dxc
        for k in range(LRU_CONV - 1):
            dx = dx + cw[k:k + 1, :] * _shift_up(dxc, nxt, LRU_CONV - 1 - k)
        for k in range(LRU_CONV):
            add("conv_w", dxc * _shift_down(prev, x, LRU_CONV - 1 - k), k)
        dp_ref[:, :c] = dx.astype(dp_ref.dtype)

    per = tm // SUBLANES
    rev = lambda col: (lambda i: (ni - 1 - i, col))
    halo = lambda i: (jnp.maximum((ni - 1 - i) * per - 1, 0), 0)
    full = lambda i: (0, 0)
    vec = pl.BlockSpec((1, c), full)
    mat = pl.BlockSpec((c, c), full)
    return pl.pallas_call(
        body, name="lru_bwd", grid=(ni,),
        in_specs=[pl.BlockSpec((tm, c), rev(0)), pl.BlockSpec((SUBLANES, c), halo), pl.BlockSpec((tm, c), rev(1)),
                  pl.BlockSpec((tm, c), rev(0)), pl.BlockSpec((tm, c), rev(0)), pl.BlockSpec((SUBLANES, c), halo),
                  pl.BlockSpec((tm, c), rev(0)), pl.BlockSpec((LRU_CONV, c), full), mat, vec, mat, vec, vec, vec, ANY],
        out_specs=[pl.BlockSpec((tm, 2 * c), rev(0)), pl.BlockSpec((LRU_ACC_ROWS, c), full), mat, mat],
        out_shape=[jax.ShapeDtypeStruct(dproj.shape, dproj.dtype), jax.ShapeDtypeStruct((LRU_ACC_ROWS, c), F32),
                   jax.ShapeDtypeStruct((c, c), F32), jax.ShapeDtypeStruct((c, c), F32)],
        scratch_shapes=[pltpu.VMEM((tm, c), F32), pltpu.VMEM((tm, c), F32), pltpu.VMEM((tm, c), F32),
                        pltpu.VMEM((SUBLANES, c), F32), pltpu.VMEM((SUBLANES, c), F32)],
        input_output_aliases={14: 0},
        compiler_params=pltpu.CompilerParams(dimension_semantics=("arbitrary",), vmem_limit_bytes=VMEM_LIMIT),
    )(proj, proj, proj, xc_all, h_all, h_all, dmixed, conv_w, wa, ba, wx, bx, lam, gain, dproj)


def _pair_sum(core, a, b, name):
    n, r, c = b.shape
    spec = pl.BlockSpec((None, r, c), lambda q, core: (q, 0, 0))

    def body(core_ref, a_ref, b_ref, o_ref):
        o_ref[...] = (a_ref[...].astype(F32) + b_ref[...].astype(F32)).astype(o_ref.dtype)

    return pl.pallas_call(
        body, name=name,
        grid_spec=pltpu.PrefetchScalarGridSpec(
            num_scalar_prefetch=1, grid=(n,),
            in_specs=[pl.BlockSpec((None, r, c), lambda q, core: (2 * q + core[0], 0, 0)), spec], out_specs=spec),
        out_shape=jax.ShapeDtypeStruct(b.shape, b.dtype),
        compiler_params=pltpu.CompilerParams(dimension_semantics=("arbitrary",), vmem_limit_bytes=VMEM_LIMIT),
    )(core, a, b)


ADAMW_BLOCK_BYTES = 4 * 1024 * 1024


def _sum_adamw(parts, w, m, v, name):
    n_parts, r, c = parts.shape
    tr = r
    while n_parts * tr * c * parts.dtype.itemsize > ADAMW_BLOCK_BYTES and tr % (4 * SUBLANES) == 0:
        tr //= 2

    def body(p_ref, w_ref, m_ref, v_ref, g_ref, d_ref, nm_ref, nv_ref):
        g = p_ref[0].astype(F32)
        for s in range(1, n_parts):
            g = g + p_ref[s].astype(F32)
        nm = ADAM_B1 * m_ref[...] + (1.0 - ADAM_B1) * g
        nv = ADAM_B2 * v_ref[...] + (1.0 - ADAM_B2) * (g * g)
        m_hat = nm / (1.0 - ADAM_B1 ** ADAM_STEP)
        v_hat = nv / (1.0 - ADAM_B2 ** ADAM_STEP)
        g_ref[...] = g
        d_ref[...] = -ADAM_LR * (m_hat / (jnp.sqrt(v_hat) + ADAM_EPS) + ADAM_WD * w_ref[...])
        nm_ref[...] = nm
        nv_ref[...] = nv

    row = pl.BlockSpec((tr, c), lambda i: (i, 0))
    return _call(body, name=name, grid=(r // tr,),
                 in_specs=[pl.BlockSpec((n_parts, tr, c), lambda i: (0, i, 0)), row, row, row],
                 out_specs=[row, row, row, row], out_shape=[jax.ShapeDtypeStruct((r, c), F32)] * 4,
                 operands=[parts, w, m, v])


MATRICES = ("w_in", "w_out", "ffn_up_w", "ffn_down_w")
CONVS = ("lru_conv_w", "ffn_conv_w")
REPLICATED = ("norm1_gain", "lru_conv_b", "lru_gate_a_w", "lru_gate_a_b", "lru_gate_x_w", "lru_gate_x_b", "lru_lambda",
              "lru_norm_gain", "ret_norm_gain", "norm2_gain", "ffn_conv_b", "final_norm_gain")
WEIGHTS = ("norm1_gain", "w_in", "lru_conv_w", "lru_conv_b", "lru_gate_a_w", "lru_gate_a_b", "lru_gate_x_w",
           "lru_gate_x_b", "lru_lambda", "lru_norm_gain", "ret_norm_gain", "w_out", "norm2_gain", "ffn_up_w",
           "ffn_conv_w", "ffn_conv_b", "ffn_down_w", "final_norm_gain")


def _rows(a, pad_to):
    a = a.reshape(-1, LANES)
    pad = (-a.shape[0]) % pad_to
    return jnp.pad(a, ((0, pad), (0, 0))) if pad else a


def _pack(arrays, pad_to):
    rows, layout, at = [], [], 0
    for a in arrays:
        r = _rows(a, pad_to)
        layout.append((at, a.size // LANES, a.shape))
        rows.append(r)
        at += r.shape[0]
    return jnp.concatenate(rows, axis=0), layout


def _unpack(packed, layout):
    lead = packed.shape[:-2]
    return [packed[..., at:at + n, :].reshape(lead + shape) for at, n, shape in layout]


def _conv_rows(lru, ffn, dtype, pad_to):
    lead = lru.shape[:-2]
    flat = jnp.concatenate([lru.reshape(lead + (-1,)), ffn.reshape(lead + (-1,))], axis=-1).astype(dtype)
    rows = flat.shape[-1] // LANES
    pad = (-rows) % pad_to
    return jnp.pad(flat.reshape(lead + (rows, LANES)), [(0, 0)] * len(lead) + [(0, pad), (0, 0)])


def _column_blocks(full):
    r, c = full.shape
    return full.reshape(r, N_DEV, c // N_DEV).transpose(1, 0, 2)


def _block_diag(w):
    nh, d, _ = w.shape
    eye = jnp.eye(nh, dtype=w.dtype)
    return (w[:, :, None, :] * eye[:, None, :, None]).reshape(nh * d, nh * d)


def _diag_blocks(dense, nh):
    d = dense.shape[0] // nh
    blocks = dense.reshape(nh, d, nh, d)
    return jnp.stack([blocks[h, :, h, :] for h in range(nh)], axis=0)


def kernel(x, norm1_gain, w_in, lru_conv_w, lru_conv_b, lru_gate_a_w, lru_gate_a_b, lru_gate_x_w, lru_gate_x_b, lru_lambda, lru_norm_gain, ret_norm_gain, w_out, norm2_gain, ffn_up_w, ffn_conv_w, ffn_conv_b, ffn_down_w, final_norm_gain, loss_target, m_norm1_gain, m_w_in, m_lru_conv_w, m_lru_conv_b, m_lru_gate_a_w, m_lru_gate_a_b, m_lru_gate_x_w, m_lru_gate_x_b, m_lru_lambda, m_lru_norm_gain, m_ret_norm_gain, m_w_out, m_norm2_gain, m_ffn_up_w, m_ffn_conv_w, m_ffn_conv_b, m_ffn_down_w, m_final_norm_gain, v_norm1_gain, v_w_in, v_lru_conv_w, v_lru_conv_b, v_lru_gate_a_w, v_lru_gate_a_b, v_lru_gate_x_w, v_lru_gate_x_b, v_lru_lambda, v_lru_norm_gain, v_ret_norm_gain, v_w_out, v_norm2_gain, v_ffn_up_w, v_ffn_conv_w, v_ffn_conv_b, v_ffn_down_w, v_final_norm_gain):
    args = dict(locals())
    given = {n: args[n] for n in WEIGHTS}
    out_shape = {n: given[n].shape for n in WEIGHTS}

    def plain(a):
        return a.reshape(1, -1) if a.ndim <= 2 else a[0]

    w = {n: plain(given[n]) for n in WEIGHTS}
    mom_m = {n: plain(args["m_" + n]) for n in WEIGHTS}
    mom_v = {n: plain(args["v_" + n]) for n in WEIGHTS}
    x2, target = x[0], loss_target[0]
    t = x2.shape[0]
    core = lax.axis_index("c").astype(jnp.int32).reshape(1)
    res = {}

    conv_pad = _conv_rows(w["lru_conv_w"], w["ffn_conv_w"], F32, SUBLANES)
    first = _gather_first([w["w_in"].astype(MXU_DTYPE), conv_pad])
    w_in_blocks, conv_all = _run_comms([first, _gather_second(first.out_shape)], "w_in_all_gather")
    n_lru = w["lru_conv_w"].size
    conv_flat = conv_all.reshape(N_DEV, -1)
    lru_cw = conv_flat[:, :n_lru].reshape((N_DEV,) + w["lru_conv_w"].shape).transpose(1, 0, 2).reshape(LRU_CONV, D_LRU)
    ffn_cw = conv_flat[:, n_lru:n_lru + w["ffn_conv_w"].size].reshape((N_DEV,) + w["ffn_conv_w"].shape)
    ffn_cw = ffn_cw.transpose(1, 0, 2).reshape(FFN_CONV, 2 * D_FF)

    cos2, sin_signed = _rope_tables(t)
    wa = _block_diag(w["lru_gate_a_w"]).astype(MXU_DTYPE)
    wx = _block_diag(w["lru_gate_x_w"]).astype(MXU_DTYPE)
    gf = w["final_norm_gain"]

    up_first = _gather_first([w["ffn_up_w"].astype(MXU_DTYPE)])
    (u1, proj), (up_part,) = _inproj_fwd(x2, w["norm1_gain"], w_in_blocks, up_first)
    rest_first = _gather_first([w["w_out"].astype(MXU_DTYPE), w["ffn_down_w"].astype(MXU_DTYPE)])

    (xc, h_lru, y_lru), lru_side = _lru_fwd(proj, lru_cw, w["lru_conv_b"], wa, w["lru_gate_a_b"], wx, w["lru_gate_x_b"],
                                            w["lru_lambda"], w["lru_norm_gain"],
                                            _both(rest_first, _gather_second([up_part])))
    w_out_part, down_part, up_blocks = lru_side
    (o_ret, y_ret, states), (w_out_blocks, down_blocks) = _ret_fwd(proj, cos2, sin_signed, w["ret_norm_gain"],
                                                                 _gather_second([w_out_part, down_part]))
    w_out_full = w_out_blocks.reshape(D_MODEL, D_MODEL)
    w_down_full = down_blocks.reshape(D_FF, D_MODEL)

    h1, u2 = _outproj_fwd(x2, y_lru, y_ret, w_out_full, w["norm2_gain"])
    up_a, up_v, conv_a, conv_v, act, dh2, dh2_b, dgf, loss = _ffn_fwd(u2, up_blocks, ffn_cw, w["ffn_conv_b"], w_down_full,
                                                                      h1, gf, target)
    loss = lax.psum(loss[0, 0], ("x", "y", "c"))

    def to_owner_chips(blocks, names, tag):
        theirs = _run_comms([_pair_exchange(blocks)], "grads_pair_exchange_" + tag)
        return [_pair_sum(core, a, b, "grads_pair_sum_" + n) for n, a, b in zip(names, blocks, theirs)]

    def adamw(name, parts):
        res[name] = _sum_adamw(parts, w[name], mom_m[name], mom_v[name], "adamw_" + name)

    g = {"final_norm_gain": dgf[0]}
    g_down = _mm_tn(act, dh2_b, "ffn_down_wgrad").reshape(N_DEV, D_FF // N_DEV, D_MODEL)
    down_sums = to_owner_chips([g_down], ["ffn_down_w"], "down")
    (dup_a, dup_v, acc_a, acc_v), (down_parts,) = _ffn_bwd(dh2_b, w_down_full, up_a, up_v, conv_a, conv_v, ffn_cw,
                                                           _chip_exchange(down_sums))
    adamw("ffn_down_w", down_parts)
    acc = jnp.concatenate([acc_a, acc_v], axis=1)[::SUBLANES]
    g_ffn_cw, g["ffn_conv_b"] = acc[:FFN_CONV], acc[FFN_CONV:]
    g_up = jnp.concatenate([_mm_tn(u2, dup_a, "ffn_up_wgrad_a", blocks=N_DEV // 2),
                            _mm_tn(u2, dup_v, "ffn_up_wgrad_v", blocks=N_DEV // 2)], axis=0)
    up_sums = to_owner_chips([g_up], ["ffn_up_w"], "up")
    (dh1, dh1_b, dg2), (up_parts,) = _norm_bwd_matmul([dup_a, dup_v], up_blocks, h1, w["norm2_gain"], dh2, "ffn_up_bwd",
                                                      _chip_exchange(up_sums))
    adamw("ffn_up_w", up_parts)
    g["norm2_gain"] = dg2[:1]
    g_out = jnp.concatenate([_mm_tn(y_lru, dh1_b, "w_out_wgrad_lru"), _mm_tn(y_ret, dh1_b, "w_out_wgrad_ret")], axis=0)
    out_sums = to_owner_chips([g_out.reshape(N_DEV, D_MODEL // N_DEV, D_MODEL)], ["w_out"], "out")
    (dmixed,), (out_parts,) = _mm_nt(dh1_b, w_out_full, "outproj_bwd", F32, _chip_exchange(out_sums))
    adamw("w_out", out_parts)
    dproj, dgain_ret = _ret_bwd(proj, cos2, sin_signed, w["ret_norm_gain"], o_ret, states, dmixed, None)
    g["ret_norm_gain"] = dgain_ret[:1]
    dproj, lru_acc, dwa, dwx = _lru_bwd(proj, xc, h_lru, dmixed, dproj, lru_cw, wa, w["lru_gate_a_b"], wx,
                                        w["lru_gate_x_b"], w["lru_lambda"], w["lru_norm_gain"])
    lru_acc = lru_acc[::SUBLANES]
    g_lru_cw = lru_acc[:LRU_CONV]
    for name in ("conv_b", "gate_a_b", "gate_x_b", "lambda", "norm_gain"):
        g["lru_" + name] = lru_acc[LRU_ACC[name]:LRU_ACC[name] + 1]
    g["lru_gate_a_w"] = _diag_blocks(dwa, LRU_HEADS)
    g["lru_gate_x_w"] = _diag_blocks(dwx, LRU_HEADS)
    g_in = _mm_tn(u1, dproj, "w_in_wgrad", blocks=N_DEV)
    g_conv = _conv_rows(_column_blocks(g_lru_cw), _column_blocks(g_ffn_cw), GRAD_DTYPE, 2 * SUBLANES)
    in_sums = to_owner_chips([g_in, g_conv], ["w_in", "conv"], "in")
    (grad_x, _, dg1), (in_parts, conv_parts) = _norm_bwd_matmul([dproj], w_in_blocks, x2, w["norm1_gain"], dh1,
                                                                "inproj_bwd", _chip_exchange(in_sums))
    adamw("w_in", in_parts)
    g["norm1_gain"] = dg1[:1]

    pad16 = lambda d: _conv_rows(d["lru_conv_w"], d["ffn_conv_w"], F32, 2 * SUBLANES)
    conv_res = _sum_adamw(conv_parts, pad16(w), pad16(mom_m), pad16(mom_v), "adamw_conv")
    for n, lo, hi in (("lru_conv_w", 0, n_lru), ("ffn_conv_w", n_lru, n_lru + w["ffn_conv_w"].size)):
        res[n] = [r.reshape(-1)[lo:hi].reshape(w[n].shape) for r in conv_res]
    rep_packed, rep_layout = _pack([g[n] for n in REPLICATED], SUBLANES)
    rep_first = _gather_first([rep_packed])
    (rep_parts,) = _run_comms([rep_first, _gather_second(rep_first.out_shape)], "small_grads_all_gather")
    rep_res = _sum_adamw(rep_parts, *[_pack([d[n] for n in REPLICATED], SUBLANES)[0] for d in (w, mom_m, mom_v)],
                         "adamw_replicated")
    for k in range(4):
        for n, a in zip(REPLICATED, _unpack(rep_res[k], rep_layout)):
            res.setdefault(n, [None] * 4)[k] = a

    outs = [loss, grad_x[None]]
    for k in range(4):
        outs += [res[n][k].reshape(out_shape[n]) for n in WEIGHTS]
    return tuple(outs)
```

```python
import math

import numpy as np
import jax
import jax.numpy as jnp
from jax import lax
from jax.experimental import pallas as pl
from jax.experimental.pallas import tpu as pltpu

F32 = jnp.float32
BF16 = jnp.bfloat16
MXU_DTYPE = jnp.bfloat16
GRAD_DTYPE = jnp.bfloat16

N_DEV = 8
N_CHIPS = 4
D_MODEL = 1024
D_LRU = 512
LRU_HEADS = 8
LRU_CONV = 4
LRU_C = 8.0
D_RET = 512
RET_HEADS = 4
RET_HEAD_DIM = 128
RET_CHUNK = 128
ROPE_BASE = 10000.0
D_IN = 3072
D_FF = 3072
FFN_CONV = 3
NORM_EPS = 1e-6

ADAM_LR = 0.001
ADAM_B1 = 0.9
ADAM_B2 = 0.999
ADAM_EPS = 1e-08
ADAM_WD = 0.01
ADAM_STEP = 10

SUBLANES = 8
LANES = 128
VMEM_LIMIT = 48 * 1024 * 1024

MESH = pl.DeviceIdType.MESH
ANY = pl.BlockSpec(memory_space=pl.ANY)


def _dot(a, b):
    return jnp.dot(a.astype(MXU_DTYPE), b.astype(MXU_DTYPE), preferred_element_type=F32)


def _dot_nt(a, b):
    return lax.dot_general(a.astype(MXU_DTYPE), b.astype(MXU_DTYPE), (((1,), (1,)), ((), ())),
                           preferred_element_type=F32)


def _dot_tn(a, b):
    return lax.dot_general(a.astype(MXU_DTYPE), b.astype(MXU_DTYPE), (((0,), (0,)), ((), ())),
                           preferred_element_type=F32)


def _sigmoid(x):
    return 0.5 + 0.5 * jnp.tanh(0.5 * x)


_GELU_C = math.sqrt(2.0 / math.pi)


def _gelu_parts(x):
    x2 = x * x
    t = jnp.tanh(_GELU_C * (x + 0.044715 * (x2 * x)))
    cdf = 0.5 * (1.0 + t)
    g = x * cdf
    dg = cdf + 0.5 * x * (1.0 - t * t) * (_GELU_C * (1.0 + 3.0 * 0.044715 * x2))
    return g, dg


def _gelu(x):
    t = jnp.tanh(_GELU_C * (x + 0.044715 * (x * x * x)))
    return x * (0.5 * (1.0 + t))


def _softplus(x):
    return jnp.maximum(x, 0.0) + jnp.log1p(jnp.exp(-jnp.abs(x)))


def _bcast_row(x, r, rows=SUBLANES):
    return jnp.broadcast_to(x[r:r + 1, :], (rows, x.shape[1]))


def _colsum8(x):
    return jnp.broadcast_to(jnp.sum(x, axis=0, keepdims=True), (SUBLANES, x.shape[1]))


def _shift_down(prev8, tile, s):
    if s == 0:
        return tile
    ext = jnp.concatenate([prev8, tile], axis=0)
    return pltpu.roll(ext, s, 0)[SUBLANES:, :]


def _shift_up(tile, next8, s):
    if s == 0:
        return tile
    ext = jnp.concatenate([tile, next8], axis=0)
    return pltpu.roll(ext, SUBLANES - s, 0)[SUBLANES:, :]


def _group_scan(a, b, reverse):
    n = a.shape[0]
    row = lax.broadcasted_iota(jnp.int32, a.shape, 0) & (SUBLANES - 1)
    for s in (1, 2, 4):
        shift = (n - s) if reverse else s
        a_sh = pltpu.roll(a, shift, 0)
        b_sh = pltpu.roll(b, shift, 0)
        m = (row <= SUBLANES - 1 - s) if reverse else (row >= s)
        b = jnp.where(m, a * b_sh + b, b)
        a = jnp.where(m, a * a_sh, a)
    return a, b


def _carry_scan(a_ref, b_ref, out_ref, carry0, reverse):
    n_groups = a_ref.shape[0] // SUBLANES

    def body(i, carry):
        g = (n_groups - 1 - i) if reverse else i
        r0 = pl.multiple_of(g * SUBLANES, SUBLANES)
        hg = a_ref[pl.ds(r0, SUBLANES), :] * carry + b_ref[pl.ds(r0, SUBLANES), :]
        out_ref[pl.ds(r0, SUBLANES), :] = hg
        return _bcast_row(hg, 0 if reverse else SUBLANES - 1)

    return lax.fori_loop(0, n_groups, body, carry0)


def _rms_fwd(h, gain):
    rstd = lax.rsqrt(jnp.mean(h * h, axis=-1, keepdims=True) + NORM_EPS)
    n = h * rstd
    return n, rstd, n * gain


def _rms_bwd(dy, n, rstd, gain):
    dn = dy * gain
    dh = rstd * (dn - n * jnp.mean(dn * n, axis=-1, keepdims=True))
    return dh, _colsum8(dy * n)


def _halo_rows(dtype):
    return SUBLANES * (4 // jnp.dtype(dtype).itemsize)


def _halo_map(tile_rows, col, halo_rows=SUBLANES):
    per = tile_rows // halo_rows
    return lambda i: (jnp.maximum(i * per - 1, 0), col)


def _resident(shape):
    return pl.BlockSpec(shape, lambda *_: (0,) * len(shape), pipeline_mode=pl.Buffered(1))


def _place():
    x, y, c = lax.axis_index("x"), lax.axis_index("y"), lax.axis_index("c")
    chips = [(1 - x, y), (x, 1 - y), (1 - x, 1 - y)]
    return x, y, c, chips


def _dev(x, y, c):
    return 4 * x + 2 * y + c


def _remote(src, dst, send_sem, recv_sem, to):
    return pltpu.make_async_remote_copy(src_ref=src, dst_ref=dst, send_sem=send_sem, recv_sem=recv_sem,
                                        device_id=to, device_id_type=MESH)


class _Comm:
    def __init__(self, operands, out_shape, sems, descs, aliases=()):
        self.operands, self.out_shape, self.sems, self.descs, self.aliases = operands, out_shape, sems, descs, aliases

    def start(self, ins, outs, sems):
        local, sends, _ = self.descs(ins, outs, sems)
        for cp in sends + local:
            cp.start()

    def wait(self, ins, outs, sems):
        local, sends, recvs = self.descs(ins, outs, sems)
        for cp in recvs:
            cp.wait_recv()
        for cp in sends:
            cp.wait_send()
        for cp in local:
            cp.wait()


def _gather_first(shards):
    n = len(shards)

    def descs(ins, outs, sems):
        send, recv, loc = sems
        x, y, c, chips = _place()
        me = _dev(x, y, c)
        targets = [(x, y, 1 - c)] + [(*chip, c) for chip in chips]
        local, sends, recvs = [], [], []
        for t in range(n):
            local.append(pltpu.make_async_copy(ins[t], outs[t].at[me], loc.at[t]))
            for k, to in enumerate(targets):
                i = 4 * t + k
                sends.append(_remote(ins[t], outs[t].at[me], send.at[i], recv.at[i], to))
                recvs.append(_remote(ins[t], outs[t].at[_dev(*to)], send.at[i], recv.at[i], to))
        return local, sends, recvs

    return _Comm(list(shards), [jax.ShapeDtypeStruct((N_DEV,) + s.shape, s.dtype) for s in shards],
                 [pltpu.SemaphoreType.DMA((4 * n,)), pltpu.SemaphoreType.DMA((4 * n,)), pltpu.SemaphoreType.DMA((n,))],
                 descs)


def _gather_second(gathered):
    n = len(gathered)

    def descs(ins, outs, sems):
        send, recv = sems
        x, y, c, chips = _place()
        sends, recvs = [], []
        for t in range(n):
            for j, chip in enumerate(chips):
                i = 3 * t + j
                have, get = _dev(*chip, c), _dev(*chip, 1 - c)
                sends.append(_remote(outs[t].at[have], outs[t].at[have], send.at[i], recv.at[i], (x, y, 1 - c)))
                recvs.append(_remote(outs[t].at[have], outs[t].at[get], send.at[i], recv.at[i], (x, y, 1 - c)))
        return [], sends, recvs

    return _Comm(list(gathered), [jax.ShapeDtypeStruct(g.shape, g.dtype) for g in gathered],
                 [pltpu.SemaphoreType.DMA((3 * n,)), pltpu.SemaphoreType.DMA((3 * n,))], descs,
                 aliases=[(t, t) for t in range(n)])


def _pair_exchange(blocks):
    n = len(blocks)

    def descs(ins, outs, sems):
        send, recv = sems
        x, y, c, _ = _place()
        sends, recvs = [], []
        for t in range(n):
            for q in range(N_CHIPS):
                i = N_CHIPS * t + q
                cp = _remote(ins[t].at[2 * q + 1 - c], outs[t].at[q], send.at[i], recv.at[i], (x, y, 1 - c))
                sends.append(cp)
                recvs.append(cp)
        return [], sends, recvs

    return _Comm(list(blocks), [jax.ShapeDtypeStruct((N_CHIPS,) + b.shape[1:], b.dtype) for b in blocks],
                 [pltpu.SemaphoreType.DMA((N_CHIPS * n,)), pltpu.SemaphoreType.DMA((N_CHIPS * n,))], descs)


def _chip_exchange(blocks):
    n = len(blocks)

    def descs(ins, outs, sems):
        send, recv, loc = sems
        x, y, c, chips = _place()
        me = 2 * x + y
        local, sends, recvs = [], [], []
        for t in range(n):
            local.append(pltpu.make_async_copy(ins[t].at[me], outs[t].at[me], loc.at[t]))
            for j, (px, py) in enumerate(chips):
                i = 3 * t + j
                q = 2 * px + py
                sends.append(_remote(ins[t].at[q], outs[t].at[me], send.at[i], recv.at[i], (px, py, c)))
                recvs.append(_remote(ins[t].at[q], outs[t].at[q], send.at[i], recv.at[i], (px, py, c)))
        return local, sends, recvs

    return _Comm(list(blocks), [jax.ShapeDtypeStruct(b.shape, b.dtype) for b in blocks],
                 [pltpu.SemaphoreType.DMA((3 * n,)), pltpu.SemaphoreType.DMA((3 * n,)), pltpu.SemaphoreType.DMA((n,))],
                 descs)


def _both(a, b):
    na, oa, sa = len(a.operands), len(a.out_shape), len(a.sems)

    def descs(ins, outs, sems):
        local_a, sends_a, recvs_a = a.descs(ins[:na], outs[:oa], sems[:sa])
        local_b, sends_b, recvs_b = b.descs(ins[na:], outs[oa:], sems[sa:])
        return local_a + local_b, sends_a + sends_b, recvs_a + recvs_b

    return _Comm(a.operands + b.operands, a.out_shape + b.out_shape, a.sems + b.sems, descs,
                 aliases=list(a.aliases) + [(na + i, oa + o) for i, o in b.aliases])


def _run_comms(comms, name):
    first = comms[0]
    n_in, n_out = len(first.operands), len(first.out_shape)

    def body(*refs):
        ins, outs, sems = refs[:n_in], refs[n_in:n_in + n_out], list(refs[n_in + n_out:])
        for k, comm in enumerate(comms):
            mine = [sems.pop(0) for _ in comm.sems]
            comm.start(ins if k == 0 else outs, outs, mine)
            comm.wait(ins if k == 0 else outs, outs, mine)

    outs = pl.pallas_call(
        body, name=name, out_shape=first.out_shape, in_specs=[ANY] * n_in, out_specs=[ANY] * n_out,
        scratch_shapes=[s for comm in comms for s in comm.sems],
    )(*first.operands)
    return list(outs)


def _call(body, *, name, grid, in_specs, out_specs, out_shape, operands, scratch_shapes=(), comm=None, aliases=None):
    sem = ("arbitrary",) * len(grid)
    params = pltpu.CompilerParams(dimension_semantics=sem, vmem_limit_bytes=VMEM_LIMIT)
    aliases = dict(aliases or {})
    if comm is None:
        return pl.pallas_call(body, name=name, grid=grid, in_specs=in_specs, out_specs=out_specs, out_shape=out_shape,
                              scratch_shapes=list(scratch_shapes), input_output_aliases=aliases,
                              compiler_params=params)(*operands)
    n_in, n_out, n_scr = len(in_specs), len(out_specs), len(scratch_shapes)
    c_in, c_out = len(comm.operands), len(comm.out_shape)

    def wrapped(*refs):
        refs = list(refs)
        ins, refs = refs[:n_in], refs[n_in:]
        cins, refs = refs[:c_in], refs[c_in:]
        outs, refs = refs[:n_out], refs[n_out:]
        couts, refs = refs[:c_out], refs[c_out:]
        scr, csems = refs[:n_scr], refs[n_scr:]
        first = last = None
        for axis, size in enumerate(grid):
            at_first, at_last = pl.program_id(axis) == 0, pl.program_id(axis) == size - 1
            first = at_first if first is None else first & at_first
            last = at_last if last is None else last & at_last

        @pl.when(first)
        def _():
            comm.start(cins, couts, csems)

        body(*ins, *outs, *scr)

        @pl.when(last)
        def _():
            comm.wait(cins, couts, csems)

    res = pl.pallas_call(
        wrapped, name=name, grid=grid, in_specs=list(in_specs) + [ANY] * c_in, out_specs=list(out_specs) + [ANY] * c_out,
        out_shape=list(out_shape) + list(comm.out_shape), scratch_shapes=list(scratch_shapes) + list(comm.sems),
        input_output_aliases={**aliases, **{n_in + i: n_out + o for i, o in comm.aliases}}, compiler_params=params,
    )(*operands, *comm.operands)
    return list(res[:n_out]), list(res[n_out:])


def _mm_nt(a, b, name, out_dtype, comm=None, tm=512, tn=512):
    m, k = a.shape
    n = b.shape[0]
    tm, tn = min(tm, m), min(tn, n)

    def body(a_ref, b_ref, o_ref):
        o_ref[...] = _dot_nt(a_ref[...], b_ref[...]).astype(o_ref.dtype)

    return _call(body, name=name, grid=(n // tn, m // tm), comm=comm,
                 in_specs=[pl.BlockSpec((tm, k), lambda j, i: (i, 0)), pl.BlockSpec((tn, k), lambda j, i: (j, 0))],
                 out_specs=[pl.BlockSpec((tm, tn), lambda j, i: (i, j))],
                 out_shape=[jax.ShapeDtypeStruct((m, n), out_dtype)], operands=[a, b])


def _mm_tn(a, b, name, blocks=1, tk=2048):
    t, m = a.shape
    n = b.shape[1]
    tk = min(tk, t)
    nk = t // tk
    cb = n // blocks
    per = max(1, 768 // cb) if blocks > 1 else 1
    tn = per * cb if blocks > 1 else min(1024, n)
    tm = min(1024, m)
    assert blocks == 1 or tm == m

    def body(a_ref, b_ref, o_ref, acc):
        k = pl.program_id(2)

        @pl.when(k == 0)
        def _():
            acc[...] = jnp.zeros_like(acc)
        acc[...] += _dot_tn(a_ref[...], b_ref[...])

        @pl.when(k == nk - 1)
        def _():
            if blocks == 1:
                o_ref[...] = acc[...].astype(o_ref.dtype)
            else:
                for s in range(per):
                    o_ref[s] = acc[:, s * cb:(s + 1) * cb].astype(o_ref.dtype)

    if blocks == 1:
        out_spec = pl.BlockSpec((tm, tn), lambda i, j, k: (i, j))
        out_shape = jax.ShapeDtypeStruct((m, n), GRAD_DTYPE)
    else:
        out_spec = pl.BlockSpec((per, m, cb), lambda i, j, k: (j, 0, 0))
        out_shape = jax.ShapeDtypeStruct((blocks, m, cb), GRAD_DTYPE)
    return _call(body, name=name, grid=(m // tm, n // tn, nk),
                 in_specs=[pl.BlockSpec((tk, tm), lambda i, j, k: (k, i)), pl.BlockSpec((tk, tn), lambda i, j, k: (k, j))],
                 out_specs=[out_spec], out_shape=[out_shape], operands=[a, b],
                 scratch_shapes=[pltpu.VMEM((tm, tn), F32)])[0]


def _inproj_fwd(x, g1, w_blocks, comm):
    t = x.shape[0]
    tm = min(512, t)
    nb, _, cb = w_blocks.shape

    def body(x_ref, g_ref, w_ref, u_ref, p_ref):
        _, _, u = _rms_fwd(x_ref[...], g_ref[...])
        u = u.astype(MXU_DTYPE)
        u_ref[...] = u
        for d in range(nb):
            p_ref[:, d * cb:(d + 1) * cb] = _dot(u, w_ref[d])

    return _call(body, name="inproj_fwd", grid=(t // tm,), comm=comm,
                 in_specs=[pl.BlockSpec((tm, D_MODEL), lambda i: (i, 0)), pl.BlockSpec((1, D_MODEL), lambda i: (0, 0)),
                           _resident(w_blocks.shape)],
                 out_specs=[pl.BlockSpec((tm, D_MODEL), lambda i: (i, 0)), pl.BlockSpec((tm, D_IN), lambda i: (i, 0))],
                 out_shape=[jax.ShapeDtypeStruct((t, D_MODEL), MXU_DTYPE), jax.ShapeDtypeStruct((t, D_IN), F32)],
                 operands=[x, g1, w_blocks])


def _lru_gates(xc, wa, ba, wx, bx, sp):
    r = _sigmoid(_dot(xc, wa) + ba)
    ig = _sigmoid(_dot(xc, wx) + bx)
    log_a = (-LRU_C) * r * sp
    a = jnp.exp(log_a)
    m = jnp.sqrt(-jnp.tanh(log_a) * (a * a + 1.0))
    return r, ig, a, m


def _lru_fwd(proj, conv_w, conv_b, wa, ba, wx, bx, lam, gain, comm):
    t = proj.shape[0]
    tm = min(256, t)
    c = D_LRU

    def body(x_ref, xh_ref, g_ref, cw_ref, cb_ref, wa_ref, ba_ref, wx_ref, bx_ref, lam_ref, gain_ref,
             xc_ref, h_ref, y_ref, a_scr, b_scr, carry):
        i = pl.program_id(0)

        @pl.when(i == 0)
        def _():
            carry[...] = jnp.zeros_like(carry)

        x = x_ref[...]
        prev = jnp.where(i == 0, 0.0, xh_ref[...])
        cw = cw_ref[...]
        xc = cb_ref[...] + cw[LRU_CONV - 1:LRU_CONV, :] * x
        for k in range(LRU_CONV - 1):
            xc = xc + cw[k:k + 1, :] * _shift_down(prev, x, LRU_CONV - 1 - k)
        xc_ref[...] = xc
        sp = _softplus(-lam_ref[...])
        _, ig, a, m = _lru_gates(xc, wa_ref[...], ba_ref[...], wx_ref[...], bx_ref[...], sp)
        ga, gb = _group_scan(a, m * (ig * xc), reverse=False)
        a_scr[...] = ga
        b_scr[...] = gb
        carry[...] = _carry_scan(a_scr, b_scr, h_ref, carry[...], reverse=False)
        z = h_ref[...] * _gelu(g_ref[...])
        _, _, y = _rms_fwd(z, gain_ref[...])
        y_ref[...] = y.astype(y_ref.dtype)

    row = lambda i: (i, 0)
    full = lambda i: (0, 0)
    vec = pl.BlockSpec((1, c), full)
    return _call(body, name="lru_fwd", grid=(t // tm,), comm=comm,
                 in_specs=[pl.BlockSpec((tm, c), row), pl.BlockSpec((SUBLANES, c), _halo_map(tm, 0)),
                           pl.BlockSpec((tm, c), lambda i: (i, 1)),
                           pl.BlockSpec((LRU_CONV, c), full), vec, pl.BlockSpec((c, c), full), vec,
                           pl.BlockSpec((c, c), full), vec, vec, vec],
                 out_specs=[pl.BlockSpec((tm, c), row), pl.BlockSpec((tm, c), row), pl.BlockSpec((tm, c), row)],
                 out_shape=[jax.ShapeDtypeStruct((t, c), F32), jax.ShapeDtypeStruct((t, c), F32),
                            jax.ShapeDtypeStruct((t, c), MXU_DTYPE)],
                 scratch_shapes=[pltpu.VMEM((tm, c), F32), pltpu.VMEM((tm, c), F32), pltpu.VMEM((SUBLANES, c), F32)],
                 operands=[proj, proj, proj, conv_w, conv_b, wa, ba, wx, bx, lam, gain])


def _ret_consts():
    c = RET_CHUNK
    log_g = jnp.log1p(-jnp.exp2(-5.0 - jnp.arange(RET_HEADS, dtype=F32)))
    idx = jnp.arange(c, dtype=F32)
    diff = idx[:, None] - idx[None, :]
    decay = jnp.where(diff[None] >= 0, jnp.exp(jnp.maximum(diff, 0.0)[None] * log_g[:, None, None]), 0.0)
    zeta = jnp.exp((c - 1 - idx)[None, :] * log_g[:, None])
    xi = jnp.exp((idx + 1.0)[None, :] * log_g[:, None])
    spread = lambda v: jnp.repeat(v.T, RET_HEAD_DIM, axis=1)
    log_g_np = np.log1p(-np.exp2(-5.0 - np.arange(RET_HEADS, dtype=np.float32))).astype(np.float32)
    g_chunk = [float(np.exp(np.float32(c) * lg)) for lg in log_g_np]
    return decay, spread(xi), spread(zeta), g_chunk


def _rope_tables(t):
    pos = np.arange(t, dtype=np.float32)
    inv_freq = np.float32(ROPE_BASE) ** (-np.arange(0, RET_HEAD_DIM, 2, dtype=np.float32) / np.float32(RET_HEAD_DIM))
    ang = (pos[:, None] * inv_freq.astype(np.float32)[None, :]).astype(np.float32).astype(np.float64)
    cos, sin = np.cos(ang).astype(np.float32), np.sin(ang).astype(np.float32)
    return jnp.asarray(np.concatenate([cos, cos], axis=-1)), jnp.asarray(np.concatenate([-sin, sin], axis=-1))


def _rope(x, cos2, sin_signed):
    return x * cos2 + pltpu.roll(x, RET_HEAD_DIM // 2, 1) * sin_signed


def _rope_bwd(d, cos2, sin_signed):
    return d * cos2 + pltpu.roll(d * sin_signed, RET_HEAD_DIM // 2, 1)


RET_SCALE = RET_HEAD_DIM ** -0.5


RET_CHUNKS_PER_STEP = 2


def _ret_fwd(proj, cos2, sin_signed, gain, comm):
    t = proj.shape[0]
    c, d, nh = RET_CHUNK, RET_HEAD_DIM, RET_HEADS
    n_chunks = t // c
    per = RET_CHUNKS_PER_STEP if n_chunks % RET_CHUNKS_PER_STEP == 0 else 1
    rows = per * c
    decay, xi, zeta, g_chunk = _ret_consts()

    def body(qk_ref, vg_ref, cos_ref, sin_ref, dec_ref, xi_ref, zeta_ref, gain_ref, o_ref, y_ref, st_ref, state):
        @pl.when(pl.program_id(0) == 0)
        def _():
            state[...] = jnp.zeros_like(state)

        cur = [state[h] for h in range(nh)]
        for s in range(per):
            rs = slice(s * c, (s + 1) * c)
            cos2, sin_s = cos_ref[rs, :], sin_ref[rs, :]
            for h in range(nh):
                lo = h * d
                q = _rope(qk_ref[rs, lo:lo + d], cos2, sin_s)
                k = _rope(qk_ref[rs, D_RET + lo:D_RET + lo + d], cos2, sin_s) * RET_SCALE
                v = vg_ref[rs, lo:lo + d]
                g = vg_ref[rs, D_RET + lo:D_RET + lo + d]
                s_prev = cur[h]
                st_ref[s, h] = s_prev
                scores = _dot_nt(q, k) * dec_ref[h]
                o = _dot(scores, v) + _dot(q * xi_ref[:, lo:lo + d], s_prev)
                cur[h] = s_prev * g_chunk[h] + _dot_tn(k * zeta_ref[:, lo:lo + d], v)
                o_ref[rs, lo:lo + d] = o
                mu = jnp.mean(o, axis=-1, keepdims=True)
                oc = o - mu
                on = oc * lax.rsqrt(jnp.mean(oc * oc, axis=-1, keepdims=True) + NORM_EPS)
                y_ref[rs, lo:lo + d] = (on * gain_ref[:, lo:lo + d] * (g * _sigmoid(g))).astype(y_ref.dtype)
        for h in range(nh):
            state[h] = cur[h]

    full2 = lambda i: (0, 0)
    return _call(body, name="ret_fwd", grid=(n_chunks // per,), comm=comm,
                 in_specs=[pl.BlockSpec((rows, 2 * D_RET), lambda i: (i, 1)),
                           pl.BlockSpec((rows, 2 * D_RET), lambda i: (i, 2)),
                           pl.BlockSpec((rows, d), lambda i: (i, 0)), pl.BlockSpec((rows, d), lambda i: (i, 0)),
                           pl.BlockSpec((nh, c, c), lambda i: (0, 0, 0)), pl.BlockSpec((c, D_RET), full2),
                           pl.BlockSpec((c, D_RET), full2), pl.BlockSpec((1, D_RET), full2)],
                 out_specs=[pl.BlockSpec((rows, D_RET), lambda i: (i, 0)), pl.BlockSpec((rows, D_RET), lambda i: (i, 0)),
                            pl.BlockSpec((per, nh, d, d), lambda i: (i, 0, 0, 0))],
                 out_shape=[jax.ShapeDtypeStruct((t, D_RET), F32), jax.ShapeDtypeStruct((t, D_RET), MXU_DTYPE),
                            jax.ShapeDtypeStruct((n_chunks, nh, d, d), F32)],
                 scratch_shapes=[pltpu.VMEM((nh, d, d), F32)],
                 operands=[proj, proj, cos2, sin_signed, decay, xi, zeta, gain])


def _outproj_fwd(x, y_lru, y_ret, w_out, g2):
    t = x.shape[0]
    tm = min(512, t)

    def body(x_ref, yl_ref, yr_ref, w_ref, g_ref, h1_ref, u2_ref):
        h1 = x_ref[...] + _dot(yl_ref[...], w_ref[:D_LRU, :]) + _dot(yr_ref[...], w_ref[D_LRU:, :])
        h1_ref[...] = h1
        _, _, u = _rms_fwd(h1, g_ref[...])
        u2_ref[...] = u.astype(u2_ref.dtype)

    row = lambda i: (i, 0)
    return _call(body, name="outproj_fwd", grid=(t // tm,),
                 in_specs=[pl.BlockSpec((tm, D_MODEL), row), pl.BlockSpec((tm, D_LRU), row), pl.BlockSpec((tm, D_RET), row),
                           _resident((D_MODEL, D_MODEL)), pl.BlockSpec((1, D_MODEL), lambda i: (0, 0))],
                 out_specs=[pl.BlockSpec((tm, D_MODEL), row), pl.BlockSpec((tm, D_MODEL), row)],
                 out_shape=[jax.ShapeDtypeStruct((t, D_MODEL), F32), jax.ShapeDtypeStruct((t, D_MODEL), MXU_DTYPE)],
                 operands=[x, y_lru, y_ret, w_out, g2])


FFN_TN = 768
FFN_NJ = D_FF // FFN_TN


def _ffn_fwd(u2, w_blocks, conv_w, conv_b, w_down, h1, gf, target):
    t = u2.shape[0]
    tm = min(256, t)
    tn, nj = FFN_TN, FFN_NJ
    hb = _halo_rows(u2.dtype)
    assert w_blocks.shape == (2 * nj, D_MODEL, tn)

    def project(u_ext, w, up_ref, conv_ref, cw_ref, cb_ref, first):
        ext = _dot(u_ext, w)
        x = ext[hb:, :]
        up_ref[...] = x.astype(up_ref.dtype)
        prev = jnp.where(first, 0.0, ext[hb - SUBLANES:hb, :])
        cw = cw_ref[...]
        y = cb_ref[...] + cw[FFN_CONV - 1:FFN_CONV, :] * x
        for k in range(FFN_CONV - 1):
            y = y + cw[k:k + 1, :] * _shift_down(prev, x, FFN_CONV - 1 - k)
        conv_ref[...] = y.astype(conv_ref.dtype)
        return y

    def body(u_ref, uh_ref, w_ref, cwa_ref, cwv_ref, cba_ref, cbv_ref, wd_ref, h1_ref, gf_ref, tg_ref,
             upa_ref, upv_ref, ca_ref, cv_ref, act_ref, dh_ref, dhb_ref, dgf_ref, loss_ref, acc):
        i, j = pl.program_id(0), pl.program_id(1)

        @pl.when((i == 0) & (j == 0))
        def _():
            dgf_ref[...] = jnp.zeros_like(dgf_ref)
            loss_ref[...] = jnp.zeros_like(loss_ref)

        @pl.when(j == 0)
        def _():
            acc[...] = jnp.zeros_like(acc)

        u_ext = jnp.concatenate([uh_ref[...], u_ref[...]], axis=0)
        a = project(u_ext, w_ref[j], upa_ref, ca_ref, cwa_ref, cba_ref, i == 0)
        v = project(u_ext, w_ref[nj + j], upv_ref, cv_ref, cwv_ref, cbv_ref, i == 0)
        act = (_gelu(a) * v).astype(act_ref.dtype)
        act_ref[...] = act
        acc[...] += _dot(act, wd_ref[pl.ds(pl.multiple_of(j * tn, tn), tn), :])

        @pl.when(j == nj - 1)
        def _():
            n, rstd, y = _rms_fwd(h1_ref[...] + acc[...], gf_ref[...])
            err = y - tg_ref[...]
            loss_ref[...] += (0.5 / D_MODEL) * jnp.sum(err * err)
            dh, dgf = _rms_bwd(err * (1.0 / D_MODEL), n, rstd, gf_ref[...])
            dgf_ref[...] += dgf
            dh_ref[...] = dh
            dhb_ref[...] = dh.astype(dhb_ref.dtype)

    per = tm // hb
    row = lambda i, j: (i, 0)
    const = lambda i, j: (0, 0)
    tile = pl.BlockSpec((tm, tn), lambda i, j: (i, j))
    return _call(body, name="ffn_fwd", grid=(t // tm, nj),
                 in_specs=[pl.BlockSpec((tm, D_MODEL), row),
                           pl.BlockSpec((hb, D_MODEL), lambda i, j: (jnp.maximum(i * per - 1, 0), 0)),
                           _resident(w_blocks.shape),
                           pl.BlockSpec((FFN_CONV, tn), lambda i, j: (0, j)),
                           pl.BlockSpec((FFN_CONV, tn), lambda i, j: (0, j + nj)),
                           pl.BlockSpec((1, tn), lambda i, j: (0, j)), pl.BlockSpec((1, tn), lambda i, j: (0, j + nj)),
                           _resident((D_FF, D_MODEL)),
                           pl.BlockSpec((tm, D_MODEL), row), pl.BlockSpec((1, D_MODEL), const),
                           pl.BlockSpec((tm, D_MODEL), row)],
                 out_specs=[tile] * 5 + [pl.BlockSpec((tm, D_MODEL), row),
                            pl.BlockSpec((tm, D_MODEL), row), pl.BlockSpec((SUBLANES, D_MODEL), const),
                            pl.BlockSpec((SUBLANES, LANES), const)],
                 out_shape=[jax.ShapeDtypeStruct((t, D_FF), MXU_DTYPE)] * 5 + [
                            jax.ShapeDtypeStruct((t, D_MODEL), F32),
                            jax.ShapeDtypeStruct((t, D_MODEL), MXU_DTYPE), jax.ShapeDtypeStruct((SUBLANES, D_MODEL), F32),
                            jax.ShapeDtypeStruct((SUBLANES, LANES), F32)],
                 scratch_shapes=[pltpu.VMEM((tm, D_MODEL), F32)],
                 operands=[u2, u2, w_blocks, conv_w, conv_w, conv_b, conv_b, w_down, h1, gf, target])


FFN_ACC_ROWS = SUBLANES * (FFN_CONV + 1)


def _ffn_bwd(dh2_b, w_down, up_a, up_v, conv_a, conv_v, conv_w, comm):
    t = up_a.shape[0]
    tm = min(256, t)
    tn, nj = FFN_TN, FFN_NJ
    ni = t // tm

    def conv_bwd(dy, x, cw, acc_ref, carry_ref, dup_ref):
        nxt = carry_ref[...]
        carry_ref[...] = dy[:SUBLANES, :]
        ahead = [_shift_up(dy, nxt, FFN_CONV - 1 - k) for k in range(FFN_CONV)]
        dx = cw[FFN_CONV - 1:FFN_CONV, :] * dy
        for k in range(FFN_CONV - 1):
            dx = dx + cw[k:k + 1, :] * ahead[k]
        dup_ref[...] = dx.astype(dup_ref.dtype)
        for k in range(FFN_CONV):
            acc_ref[k * SUBLANES:(k + 1) * SUBLANES, :] += _colsum8(ahead[k] * x)
        acc_ref[FFN_CONV * SUBLANES:, :] += _colsum8(dy)

    def body(dh_ref, wd_ref, ua_ref, uv_ref, ca_ref, cv_ref, cwa_ref, cwv_ref,
             dua_ref, duv_ref, acca_ref, accv_ref, carry_a, carry_v):
        i = pl.program_id(1)

        @pl.when(i == 0)
        def _():
            for ref in (acca_ref, accv_ref, carry_a, carry_v):
                ref[...] = jnp.zeros_like(ref)

        v = cv_ref[...].astype(F32)
        g, dg = _gelu_parts(ca_ref[...].astype(F32))
        dact = _dot_nt(dh_ref[...], wd_ref[...])
        conv_bwd(dact * v * dg, ua_ref[...].astype(F32), cwa_ref[...], acca_ref, carry_a, dua_ref)
        conv_bwd(dact * g, uv_ref[...].astype(F32), cwv_ref[...], accv_ref, carry_v, duv_ref)

    rev = lambda j, i: (ni - 1 - i, j)
    colj = lambda off: (lambda j, i: (0, j + off))
    tile = pl.BlockSpec((tm, tn), rev)
    return _call(body, name="ffn_bwd", grid=(nj, ni), comm=comm,
                 in_specs=[pl.BlockSpec((tm, D_MODEL), lambda j, i: (ni - 1 - i, 0)),
                           pl.BlockSpec((tn, D_MODEL), lambda j, i: (j, 0)), tile, tile, tile, tile,
                           pl.BlockSpec((FFN_CONV, tn), colj(0)), pl.BlockSpec((FFN_CONV, tn), colj(nj))],
                 out_specs=[tile, tile,
                            pl.BlockSpec((FFN_ACC_ROWS, tn), colj(0)), pl.BlockSpec((FFN_ACC_ROWS, tn), colj(0))],
                 out_shape=[jax.ShapeDtypeStruct((t, D_FF), MXU_DTYPE), jax.ShapeDtypeStruct((t, D_FF), MXU_DTYPE),
                            jax.ShapeDtypeStruct((FFN_ACC_ROWS, D_FF), F32), jax.ShapeDtypeStruct((FFN_ACC_ROWS, D_FF), F32)],
                 scratch_shapes=[pltpu.VMEM((SUBLANES, tn), F32), pltpu.VMEM((SUBLANES, tn), F32)],
                 operands=[dh2_b, w_down, up_a, up_v, conv_a, conv_v, conv_w, conv_w])


def _norm_bwd_matmul(parts, w_blocks, h, gain, d_res, name, comm=None, tiles=None, into=None):
    t = h.shape[0]
    tm = min(256, t)
    first, count = tiles or (0, t // tm)
    n_parts = len(parts)
    nb, _, cb = w_blocks.shape
    where = []
    for p, a in enumerate(parts):
        assert a.shape[1] % cb == 0
        where += [(p, lo) for lo in range(0, a.shape[1], cb)]
    assert len(where) == nb
    n_in = n_parts + 4

    def body(*refs):
        a_refs = refs[:n_parts]
        w_ref, h_ref, g_ref, dres_ref = refs[n_parts:n_in]
        dh_ref, dhb_ref, dg_ref = refs[-3:]

        @pl.when(pl.program_id(0) == 0)
        def _():
            dg_ref[...] = jnp.zeros_like(dg_ref)

        du = None
        for d, (p, lo) in enumerate(where):
            term = _dot_nt(a_refs[p][:, lo:lo + cb], w_ref[d])
            du = term if du is None else du + term
        n, rstd, _ = _rms_fwd(h_ref[...], g_ref[...])
        dh, dg = _rms_bwd(du, n, rstd, g_ref[...])
        dh = dh + dres_ref[...]
        dg_ref[...] += dg
        dh_ref[...] = dh
        dhb_ref[...] = dh.astype(dhb_ref.dtype)

    row = lambda i: (i + first, 0)
    const = lambda i: (0, 0)
    in_specs = [pl.BlockSpec((tm, a.shape[1]), row) for a in parts] + [
        _resident(w_blocks.shape), pl.BlockSpec((tm, D_MODEL), row), pl.BlockSpec((1, D_MODEL), const),
        pl.BlockSpec((tm, D_MODEL), row)]
    operands = [*parts, w_blocks, h, gain, d_res]
    aliases = {}
    if into is not None:
        in_specs += [ANY, ANY]
        operands += list(into)
        aliases = {n_in: 0, n_in + 1: 1}
    return _call(body, name=name, grid=(count,), comm=comm, in_specs=in_specs,
                 out_specs=[pl.BlockSpec((tm, D_MODEL), row), pl.BlockSpec((tm, D_MODEL), row),
                            pl.BlockSpec((SUBLANES, D_MODEL), const)],
                 out_shape=[jax.ShapeDtypeStruct((t, D_MODEL), F32), jax.ShapeDtypeStruct((t, D_MODEL), MXU_DTYPE),
                            jax.ShapeDtypeStruct((SUBLANES, D_MODEL), F32)],
                 operands=operands, aliases=aliases)


def _ret_bwd(proj, cos2, sin_signed, gain, o, states, dmixed, comm):
    t = proj.shape[0]
    c, d, nh = RET_CHUNK, RET_HEAD_DIM, RET_HEADS
    n_chunks = t // c
    per = RET_CHUNKS_PER_STEP if n_chunks % RET_CHUNKS_PER_STEP == 0 else 1
    rows = per * c
    n_steps = n_chunks // per
    decay, xi, zeta, g_chunk = _ret_consts()
    base = 2 * D_LRU

    def body(qk_ref, vg_ref, cos_ref, sin_ref, dec_ref, xi_ref, zeta_ref, gain_ref, o_ref, st_ref, dy_ref,
             dp_ref, dgain_ref, gstate):
        @pl.when(pl.program_id(0) == 0)
        def _():
            gstate[...] = jnp.zeros_like(gstate)
            dgain_ref[...] = jnp.zeros_like(dgain_ref)

        cur = [gstate[h] for h in range(nh)]
        for s in reversed(range(per)):
            rs = slice(s * c, (s + 1) * c)
            cos2, sin_s = cos_ref[rs, :], sin_ref[rs, :]
            for h in range(nh):
                lo = h * d
                q = _rope(qk_ref[rs, lo:lo + d], cos2, sin_s)
                k = _rope(qk_ref[rs, D_RET + lo:D_RET + lo + d], cos2, sin_s) * RET_SCALE
                v = vg_ref[rs, lo:lo + d]
                g = vg_ref[rs, D_RET + lo:D_RET + lo + d]
                gain_h = gain_ref[:, lo:lo + d]
                xi_h, zeta_h, dec = xi_ref[:, lo:lo + d], zeta_ref[:, lo:lo + d], dec_ref[h]
                dy = dy_ref[rs, lo:lo + d]
                sg = _sigmoid(g)
                o_h = o_ref[rs, lo:lo + d]
                oc = o_h - jnp.mean(o_h, axis=-1, keepdims=True)
                rstd = lax.rsqrt(jnp.mean(oc * oc, axis=-1, keepdims=True) + NORM_EPS)
                on = oc * rstd
                at = base + 3 * D_RET + lo
                dp_ref[rs, at:at + d] = (dy * on * gain_h * (sg * (1.0 + g * (1.0 - sg)))).astype(dp_ref.dtype)
                don_g = dy * (g * sg)
                dgain_ref[:, lo:lo + d] += _colsum8(don_g * on)
                don = don_g * gain_h
                do = rstd * (don - jnp.mean(don, axis=-1, keepdims=True)
                             - on * jnp.mean(don * on, axis=-1, keepdims=True))
                s_prev = st_ref[s, h]
                g_next = cur[h]
                p = _dot_nt(q, k) * dec
                dpm = _dot_nt(do, v) * dec
                dq = _dot(dpm, k) + _dot_nt(do, s_prev) * xi_h
                dk = _dot_tn(dpm, q) + _dot_nt(v, g_next) * zeta_h
                dv = _dot_tn(p, do) + _dot(k * zeta_h, g_next)
                cur[h] = g_next * g_chunk[h] + _dot_tn(q * xi_h, do)
                dp_ref[rs, base + lo:base + lo + d] = _rope_bwd(dq, cos2, sin_s).astype(dp_ref.dtype)
                at = base + D_RET + lo
                dp_ref[rs, at:at + d] = _rope_bwd(dk * RET_SCALE, cos2, sin_s).astype(dp_ref.dtype)
                at = base + 2 * D_RET + lo
                dp_ref[rs, at:at + d] = dv.astype(dp_ref.dtype)
        for h in range(nh):
            gstate[h] = cur[h]

    rev = lambda col: (lambda i: (n_steps - 1 - i, col))
    full2 = lambda i: (0, 0)
    return _call(body, name="ret_bwd", grid=(n_steps,), comm=comm,
                 in_specs=[pl.BlockSpec((rows, 2 * D_RET), rev(1)), pl.BlockSpec((rows, 2 * D_RET), rev(2)),
                           pl.BlockSpec((rows, d), rev(0)), pl.BlockSpec((rows, d), rev(0)),
                           pl.BlockSpec((nh, c, c), lambda i: (0, 0, 0)), pl.BlockSpec((c, D_RET), full2),
                           pl.BlockSpec((c, D_RET), full2), pl.BlockSpec((1, D_RET), full2),
                           pl.BlockSpec((rows, D_RET), rev(0)),
                           pl.BlockSpec((per, nh, d, d), lambda i: (n_steps - 1 - i, 0, 0, 0)),
                           pl.BlockSpec((rows, D_RET), rev(1))],
                 out_specs=[pl.BlockSpec((rows, D_IN), rev(0)), pl.BlockSpec((SUBLANES, D_RET), full2)],
                 out_shape=[jax.ShapeDtypeStruct((t, D_IN), MXU_DTYPE), jax.ShapeDtypeStruct((SUBLANES, D_RET), F32)],
                 scratch_shapes=[pltpu.VMEM((nh, d, d), F32)],
                 operands=[proj, proj, cos2, sin_signed, decay, xi, zeta, gain, o, states, dmixed])


LRU_ACC = {"conv_w": 0, "conv_b": LRU_CONV, "gate_a_b": LRU_CONV + 1, "gate_x_b": LRU_CONV + 2,
           "lambda": LRU_CONV + 3, "norm_gain": LRU_CONV + 4}
LRU_ACC_ROWS = SUBLANES * (LRU_CONV + 5)


def _lru_bwd(proj, xc_all, h_all, dmixed, dproj, conv_w, wa, ba, wx, bx, lam, gain):
    t = proj.shape[0]
    tm = min(256, t)
    c = D_LRU
    ni = t // tm

    def body(x_ref, xh_ref, g_ref, xc_ref, h_ref, hh_ref, dy_ref, cw_ref, wa_ref, ba_ref, wx_ref, bx_ref, lam_ref,
             gain_ref, dproj_in, dp_ref, acc_ref, dwa_ref, dwx_ref, a_scr, b_scr, mu_scr, carry_mu, carry_dxc):
        del dproj_in
        i = pl.program_id(0)
        r = ni - 1 - i

        @pl.when(i == 0)
        def _():
            acc_ref[...] = jnp.zeros_like(acc_ref)
            dwa_ref[...] = jnp.zeros_like(dwa_ref)
            dwx_ref[...] = jnp.zeros_like(dwx_ref)
            carry_mu[...] = jnp.zeros_like(carry_mu)
            carry_dxc[...] = jnp.zeros_like(carry_dxc)

        def add(name, val, k=0):
            lo = (LRU_ACC[name] + k) * SUBLANES
            acc_ref[lo:lo + SUBLANES, :] += _colsum8(val)

        xc, h = xc_ref[...], h_ref[...]
        lam_v = lam_ref[...]
        sp = _softplus(-lam_v)
        rg, ig, a, m = _lru_gates(xc, wa_ref[...], ba_ref[...], wx_ref[...], bx_ref[...], sp)
        gl, dgl = _gelu_parts(g_ref[...])
        zn, rstd, _ = _rms_fwd(h * gl, gain_ref[...])
        dy = dy_ref[...]
        dz, dgain = _rms_bwd(dy, zn, rstd, gain_ref[...])
        lo = LRU_ACC["norm_gain"] * SUBLANES
        acc_ref[lo:lo + SUBLANES, :] += dgain
        dp_ref[:, c:] = (dz * h * dgl).astype(dp_ref.dtype)
        dh = dz * gl
        ga, gb = _group_scan(a, a * dh, reverse=True)
        a_scr[...] = ga
        b_scr[...] = gb
        mu_next_tile = carry_mu[...]
        carry_mu[...] = _carry_scan(a_scr, b_scr, mu_scr, mu_next_tile, reverse=True)
        lam_t = dh + _shift_up(mu_scr[...], mu_next_tile, 1)
        h_prev = _shift_down(jnp.where(r == 0, 0.0, hh_ref[...]), h, 1)
        da = lam_t * h_prev
        dig = lam_t * m * xc
        dxc = lam_t * m * ig
        dlog_a = da * a - (lam_t * ig * xc) * (a * a) / m
        dpr = dlog_a * ((-LRU_C) * sp) * rg * (1.0 - rg)
        add("lambda", dlog_a * ((-LRU_C) * rg) * (-_sigmoid(-lam_v)))
        dpi = dig * ig * (1.0 - ig)
        add("gate_a_b", dpr)
        add("gate_x_b", dpi)
        dwa_ref[...] += _dot_tn(xc, dpr)
        dwx_ref[...] += _dot_tn(xc, dpi)
        dxc = dxc + _dot_nt(dpr, wa_ref[...]) + _dot_nt(dpi, wx_ref[...])
        add("conv_b", dxc)
        x = x_ref[...]
        prev = jnp.where(r == 0, 0.0, xh_ref[...])
        cw = cw_ref[...]
        nxt = carry_dxc[...]
        carry_dxc[...] = dxc[:SUBLANES, :]
        dx = cw[LRU_CONV - 1:LRU_CONV, :] * dxc
        for k in range(LRU_CONV - 1):
            dx = dx + cw[k:k + 1, :] * _shift_up(dxc, nxt, LRU_CONV - 1 - k)
        for k in range(LRU_CONV):
            add("conv_w", dxc * _shift_down(prev, x, LRU_CONV - 1 - k), k)
        dp_ref[:, :c] = dx.astype(dp_ref.dtype)

    per = tm // SUBLANES
    rev = lambda col: (lambda i: (ni - 1 - i, col))
    halo = lambda i: (jnp.maximum((ni - 1 - i) * per - 1, 0), 0)
    full = lambda i: (0, 0)
    vec = pl.BlockSpec((1, c), full)
    mat = pl.BlockSpec((c, c), full)
    return pl.pallas_call(
        body, name="lru_bwd", grid=(ni,),
        in_specs=[pl.BlockSpec((tm, c), rev(0)), pl.BlockSpec((SUBLANES, c), halo), pl.BlockSpec((tm, c), rev(1)),
                  pl.BlockSpec((tm, c), rev(0)), pl.BlockSpec((tm, c), rev(0)), pl.BlockSpec((SUBLANES, c), halo),
                  pl.BlockSpec((tm, c), rev(0)), pl.BlockSpec((LRU_CONV, c), full), mat, vec, mat, vec, vec, vec, ANY],
        out_specs=[pl.BlockSpec((tm, 2 * c), rev(0)), pl.BlockSpec((LRU_ACC_ROWS, c), full), mat, mat],
        out_shape=[jax.ShapeDtypeStruct(dproj.shape, dproj.dtype), jax.ShapeDtypeStruct((LRU_ACC_ROWS, c), F32),
                   jax.ShapeDtypeStruct((c, c), F32), jax.ShapeDtypeStruct((c, c), F32)],
        scratch_shapes=[pltpu.VMEM((tm, c), F32), pltpu.VMEM((tm, c), F32), pltpu.VMEM((tm, c), F32),
                        pltpu.VMEM((SUBLANES, c), F32), pltpu.VMEM((SUBLANES, c), F32)],
        input_output_aliases={14: 0},
        compiler_params=pltpu.CompilerParams(dimension_semantics=("arbitrary",), vmem_limit_bytes=VMEM_LIMIT),
    )(proj, proj, proj, xc_all, h_all, h_all, dmixed, conv_w, wa, ba, wx, bx, lam, gain, dproj)


def _pair_sum(core, a, b, name):
    n, r, c = b.shape
    spec = pl.BlockSpec((None, r, c), lambda q, core: (q, 0, 0))

    def body(core_ref, a_ref, b_ref, o_ref):
        o_ref[...] = (a_ref[...].astype(F32) + b_ref[...].astype(F32)).astype(o_ref.dtype)

    return pl.pallas_call(
        body, name=name,
        grid_spec=pltpu.PrefetchScalarGridSpec(
            num_scalar_prefetch=1, grid=(n,),
            in_specs=[pl.BlockSpec((None, r, c), lambda q, core: (2 * q + core[0], 0, 0)), spec], out_specs=spec),
        out_shape=jax.ShapeDtypeStruct(b.shape, b.dtype),
        compiler_params=pltpu.CompilerParams(dimension_semantics=("arbitrary",), vmem_limit_bytes=VMEM_LIMIT),
    )(core, a, b)


ADAMW_BLOCK_BYTES = 4 * 1024 * 1024


def _sum_adamw(parts, w, m, v, name):
    n_parts, r, c = parts.shape
    tr = r
    while n_parts * tr * c * parts.dtype.itemsize > ADAMW_BLOCK_BYTES and tr % (4 * SUBLANES) == 0:
        tr //= 2

    def body(p_ref, w_ref, m_ref, v_ref, g_ref, d_ref, nm_ref, nv_ref):
        g = p_ref[0].astype(F32)
        for s in range(1, n_parts):
            g = g + p_ref[s].astype(F32)
        nm = ADAM_B1 * m_ref[...] + (1.0 - ADAM_B1) * g
        nv = ADAM_B2 * v_ref[...] + (1.0 - ADAM_B2) * (g * g)
        m_hat = nm / (1.0 - ADAM_B1 ** ADAM_STEP)
        v_hat = nv / (1.0 - ADAM_B2 ** ADAM_STEP)
        g_ref[...] = g
        d_ref[...] = -ADAM_LR * (m_hat / (jnp.sqrt(v_hat) + ADAM_EPS) + ADAM_WD * w_ref[...])
        nm_ref[...] = nm
        nv_ref[...] = nv

    row = pl.BlockSpec((tr, c), lambda i: (i, 0))
    return _call(body, name=name, grid=(r // tr,),
                 in_specs=[pl.BlockSpec((n_parts, tr, c), lambda i: (0, i, 0)), row, row, row],
                 out_specs=[row, row, row, row], out_shape=[jax.ShapeDtypeStruct((r, c), F32)] * 4,
                 operands=[parts, w, m, v])


MATRICES = ("w_in", "w_out", "ffn_up_w", "ffn_down_w")
CONVS = ("lru_conv_w", "ffn_conv_w")
REPLICATED = ("norm1_gain", "lru_conv_b", "lru_gate_a_w", "lru_gate_a_b", "lru_gate_x_w", "lru_gate_x_b", "lru_lambda",
              "lru_norm_gain", "ret_norm_gain", "norm2_gain", "ffn_conv_b", "final_norm_gain")
WEIGHTS = ("norm1_gain", "w_in", "lru_conv_w", "lru_conv_b", "lru_gate_a_w", "lru_gate_a_b", "lru_gate_x_w",
           "lru_gate_x_b", "lru_lambda", "lru_norm_gain", "ret_norm_gain", "w_out", "norm2_gain", "ffn_up_w",
           "ffn_conv_w", "ffn_conv_b", "ffn_down_w", "final_norm_gain")


def _rows(a, pad_to):
    a = a.reshape(-1, LANES)
    pad = (-a.shape[0]) % pad_to
    return jnp.pad(a, ((0, pad), (0, 0))) if pad else a


def _pack(arrays, pad_to):
    rows, layout, at = [], [], 0
    for a in arrays:
        r = _rows(a, pad_to)
        layout.append((at, a.size // LANES, a.shape))
        rows.append(r)
        at += r.shape[0]
    return jnp.concatenate(rows, axis=0), layout


def _unpack(packed, layout):
    lead = packed.shape[:-2]
    return [packed[..., at:at + n, :].reshape(lead + shape) for at, n, shape in layout]


def _conv_rows(lru, ffn, dtype, pad_to):
    lead = lru.shape[:-2]
    flat = jnp.concatenate([lru.reshape(lead + (-1,)), ffn.reshape(lead + (-1,))], axis=-1).astype(dtype)
    rows = flat.shape[-1] // LANES
    pad = (-rows) % pad_to
    return jnp.pad(flat.reshape(lead + (rows, LANES)), [(0, 0)] * len(lead) + [(0, pad), (0, 0)])


def _column_blocks(full):
    r, c = full.shape
    return full.reshape(r, N_DEV, c // N_DEV).transpose(1, 0, 2)


def _block_diag(w):
    nh, d, _ = w.shape
    eye = jnp.eye(nh, dtype=w.dtype)
    return (w[:, :, None, :] * eye[:, None, :, None]).reshape(nh * d, nh * d)


def _diag_blocks(dense, nh):
    d = dense.shape[0] // nh
    blocks = dense.reshape(nh, d, nh, d)
    return jnp.stack([blocks[h, :, h, :] for h in range(nh)], axis=0)


def kernel(x, norm1_gain, w_in, lru_conv_w, lru_conv_b, lru_gate_a_w, lru_gate_a_b, lru_gate_x_w, lru_gate_x_b, lru_lambda, lru_norm_gain, ret_norm_gain, w_out, norm2_gain, ffn_up_w, ffn_conv_w, ffn_conv_b, ffn_down_w, final_norm_gain, loss_target, m_norm1_gain, m_w_in, m_lru_conv_w, m_lru_conv_b, m_lru_gate_a_w, m_lru_gate_a_b, m_lru_gate_x_w, m_lru_gate_x_b, m_lru_lambda, m_lru_norm_gain, m_ret_norm_gain, m_w_out, m_norm2_gain, m_ffn_up_w, m_ffn_conv_w, m_ffn_conv_b, m_ffn_down_w, m_final_norm_gain, v_norm1_gain, v_w_in, v_lru_conv_w, v_lru_conv_b, v_lru_gate_a_w, v_lru_gate_a_b, v_lru_gate_x_w, v_lru_gate_x_b, v_lru_lambda, v_lru_norm_gain, v_ret_norm_gain, v_w_out, v_norm2_gain, v_ffn_up_w, v_ffn_conv_w, v_ffn_conv_b, v_ffn_down_w, v_final_norm_gain):
    args = dict(locals())
    given = {n: args[n] for n in WEIGHTS}
    out_shape = {n: given[n].shape for n in WEIGHTS}

    def plain(a):
        return a.reshape(1, -1) if a.ndim <= 2 else a[0]

    w = {n: plain(given[n]) for n in WEIGHTS}
    mom_m = {n: plain(args["m_" + n]) for n in WEIGHTS}
    mom_v = {n: plain(args["v_" + n]) for n in WEIGHTS}
    x2, target = x[0], loss_target[0]
    t = x2.shape[0]
    core = lax.axis_index("c").astype(jnp.int32).reshape(1)
    res = {}

    conv_pad = _conv_rows(w["lru_conv_w"], w["ffn_conv_w"], F32, SUBLANES)
    first = _gather_first([w["w_in"].astype(MXU_DTYPE), conv_pad])
    w_in_blocks, conv_all = _run_comms([first, _gather_second(first.out_shape)], "w_in_all_gather")
    n_lru = w["lru_conv_w"].size
    conv_flat = conv_all.reshape(N_DEV, -1)
    lru_cw = conv_flat[:, :n_lru].reshape((N_DEV,) + w["lru_conv_w"].shape).transpose(1, 0, 2).reshape(LRU_CONV, D_LRU)
    ffn_cw = conv_flat[:, n_lru:n_lru + w["ffn_conv_w"].size].reshape((N_DEV,) + w["ffn_conv_w"].shape)
    ffn_cw = ffn_cw.transpose(1, 0, 2).reshape(FFN_CONV, 2 * D_FF)

    cos2, sin_signed = _rope_tables(t)
    wa = _block_diag(w["lru_gate_a_w"]).astype(MXU_DTYPE)
    wx = _block_diag(w["lru_gate_x_w"]).astype(MXU_DTYPE)
    gf = w["final_norm_gain"]

    up_first = _gather_first([w["ffn_up_w"].astype(MXU_DTYPE)])
    (u1, proj), (up_part,) = _inproj_fwd(x2, w["norm1_gain"], w_in_blocks, up_first)
    rest_first = _gather_first([w["w_out"].astype(MXU_DTYPE), w["ffn_down_w"].astype(MXU_DTYPE)])

    (xc, h_lru, y_lru), lru_side = _lru_fwd(proj, lru_cw, w["lru_conv_b"], wa, w["lru_gate_a_b"], wx, w["lru_gate_x_b"],
                                            w["lru_lambda"], w["lru_norm_gain"],
                                            _both(rest_first, _gather_second([up_part])))
    w_out_part, down_part, up_blocks = lru_side
    (o_ret, y_ret, states), (w_out_blocks, down_blocks) = _ret_fwd(proj, cos2, sin_signed, w["ret_norm_gain"],
                                                                 _gather_second([w_out_part, down_part]))
    w_out_full = w_out_blocks.reshape(D_MODEL, D_MODEL)
    w_down_full = down_blocks.reshape(D_FF, D_MODEL)

    h1, u2 = _outproj_fwd(x2, y_lru, y_ret, w_out_full, w["norm2_gain"])
    up_a, up_v, conv_a, conv_v, act, dh2, dh2_b, dgf, loss = _ffn_fwd(u2, up_blocks, ffn_cw, w["ffn_conv_b"], w_down_full,
                                                                      h1, gf, target)
    loss = lax.psum(loss[0, 0], ("x", "y", "c"))

    def to_owner_chips(blocks, names, tag):
        theirs = _run_comms([_pair_exchange(blocks)], "grads_pair_exchange_" + tag)
        return [_pair_sum(core, a, b, "grads_pair_sum_" + n) for n, a, b in zip(names, blocks, theirs)]

    def adamw(name, parts):
        res[name] = _sum_adamw(parts, w[name], mom_m[name], mom_v[name], "adamw_" + name)

    g = {"final_norm_gain": dgf[0]}
    g_down = _mm_tn(act, dh2_b, "ffn_down_wgrad").reshape(N_DEV, D_FF // N_DEV, D_MODEL)
    down_sums = to_owner_chips([g_down], ["ffn_down_w"], "down")
    (dup_a, dup_v, acc_a, acc_v), (down_parts,) = _ffn_bwd(dh2_b, w_down_full, up_a, up_v, conv_a, conv_v, ffn_cw,
                                                           _chip_exchange(down_sums))
    adamw("ffn_down_w", down_parts)
    acc = jnp.concatenate([acc_a, acc_v], axis=1)[::SUBLANES]
    g_ffn_cw, g["ffn_conv_b"] = acc[:FFN_CONV], acc[FFN_CONV:]
    g_up = jnp.concatenate([_mm_tn(u2, dup_a, "ffn_up_wgrad_a", blocks=N_DEV // 2),
                            _mm_tn(u2, dup_v, "ffn_up_wgrad_v", blocks=N_DEV // 2)], axis=0)
    up_sums = to_owner_chips([g_up], ["ffn_up_w"], "up")
    (dh1, dh1_b, dg2), (up_parts,) = _norm_bwd_matmul([dup_a, dup_v], up_blocks, h1, w["norm2_gain"], dh2, "ffn_up_bwd",
                                                      _chip_exchange(up_sums))
    adamw("ffn_up_w", up_parts)
    g["norm2_gain"] = dg2[:1]
    g_out = jnp.concatenate([_mm_tn(y_lru, dh1_b, "w_out_wgrad_lru"), _mm_tn(y_ret, dh1_b, "w_out_wgrad_ret")], axis=0)
    out_sums = to_owner_chips([g_out.reshape(N_DEV, D_MODEL // N_DEV, D_MODEL)], ["w_out"], "out")
    (dmixed,), (out_parts,) = _mm_nt(dh1_b, w_out_full, "outproj_bwd", F32, _chip_exchange(out_sums))
    adamw("w_out", out_parts)
    dproj, dgain_ret = _ret_bwd(proj, cos2, sin_signed, w["ret_norm_gain"], o_ret, states, dmixed, None)
    g["ret_norm_gain"] = dgain_ret[:1]
    dproj, lru_acc, dwa, dwx = _lru_bwd(proj, xc, h_lru, dmixed, dproj, lru_cw, wa, w["lru_gate_a_b"], wx,
                                        w["lru_gate_x_b"], w["lru_lambda"], w["lru_norm_gain"])
    lru_acc = lru_acc[::SUBLANES]
    g_lru_cw = lru_acc[:LRU_CONV]
    for name in ("conv_b", "gate_a_b", "gate_x_b", "lambda", "norm_gain"):
        g["lru_" + name] = lru_acc[LRU_ACC[name]:LRU_ACC[name] + 1]
    g["lru_gate_a_w"] = _diag_blocks(dwa, LRU_HEADS)
    g["lru_gate_x_w"] = _diag_blocks(dwx, LRU_HEADS)
    g_in = _mm_tn(u1, dproj, "w_in_wgrad", blocks=N_DEV)
    g_conv = _conv_rows(_column_blocks(g_lru_cw), _column_blocks(g_ffn_cw), GRAD_DTYPE, 2 * SUBLANES)
    in_sums = to_owner_chips([g_in, g_conv], ["w_in", "conv"], "in")
    half = t // min(256, t) // 2
    (gx_half, gxb_half, dg1_lo), (in_parts, conv_parts) = _norm_bwd_matmul(
        [dproj], w_in_blocks, x2, w["norm1_gain"], dh1, "inproj_bwd_lo", _chip_exchange(in_sums), tiles=(0, half))
    grad_x, _, dg1_hi = _norm_bwd_matmul([dproj], w_in_blocks, x2, w["norm1_gain"], dh1, "inproj_bwd_hi",
                                         tiles=(half, t // min(256, t) - half), into=(gx_half, gxb_half))
    adamw("w_in", in_parts)
    g["norm1_gain"] = dg1_lo[:1] + dg1_hi[:1]

    pad16 = lambda d: _conv_rows(d["lru_conv_w"], d["ffn_conv_w"], F32, 2 * SUBLANES)
    conv_res = _sum_adamw(conv_parts, pad16(w), pad16(mom_m), pad16(mom_v), "adamw_conv")
    for n, lo, hi in (("lru_conv_w", 0, n_lru), ("ffn_conv_w", n_lru, n_lru + w["ffn_conv_w"].size)):
        res[n] = [r.reshape(-1)[lo:hi].reshape(w[n].shape) for r in conv_res]
    rep_packed, rep_layout = _pack([g[n] for n in REPLICATED], SUBLANES)
    rep_first = _gather_first([rep_packed])
    (rep_parts,) = _run_comms([rep_first, _gather_second(rep_first.out_shape)], "small_grads_all_gather")
    rep_res = _sum_adamw(rep_parts, *[_pack([d[n] for n in REPLICATED], SUBLANES)[0] for d in (w, mom_m, mom_v)],
                         "adamw_replicated")
    for k in range(4):
        for n, a in zip(REPLICATED, _unpack(rep_res[k], rep_layout)):
            res.setdefault(n, [None] * 4)[k] = a

    outs = [loss, grad_x[None]]
    for k in range(4):
        outs += [res[n][k].reshape(out_shape[n]) for n in WEIGHTS]
    return tuple(outs)
```

```python
import math

import numpy as np
import jax
import jax.numpy as jnp
from jax import lax
from jax.experimental import pallas as pl
from jax.experimental.pallas import tpu as pltpu

F32 = jnp.float32
BF16 = jnp.bfloat16
MXU_DTYPE = jnp.bfloat16
GRAD_DTYPE = jnp.bfloat16

N_DEV = 8
N_CHIPS = 4
D_MODEL = 1024
D_LRU = 512
LRU_HEADS = 8
LRU_CONV = 4
LRU_C = 8.0
D_RET = 512
RET_HEADS = 4
RET_HEAD_DIM = 128
RET_CHUNK = 128
ROPE_BASE = 10000.0
D_IN = 3072
D_FF = 3072
FFN_CONV = 3
NORM_EPS = 1e-6

ADAM_LR = 0.001
ADAM_B1 = 0.9
ADAM_B2 = 0.999
ADAM_EPS = 1e-08
ADAM_WD = 0.01
ADAM_STEP = 10

SUBLANES = 8
LANES = 128
VMEM_LIMIT = 48 * 1024 * 1024

MESH = pl.DeviceIdType.MESH
ANY = pl.BlockSpec(memory_space=pl.ANY)


def _dot(a, b):
    return jnp.dot(a.astype(MXU_DTYPE), b.astype(MXU_DTYPE), preferred_element_type=F32)


def _dot_nt(a, b):
    return lax.dot_general(a.astype(MXU_DTYPE), b.astype(MXU_DTYPE), (((1,), (1,)), ((), ())),
                           preferred_element_type=F32)


def _dot_tn(a, b):
    return lax.dot_general(a.astype(MXU_DTYPE), b.astype(MXU_DTYPE), (((0,), (0,)), ((), ())),
                           preferred_element_type=F32)


def _sigmoid(x):
    return 0.5 + 0.5 * jnp.tanh(0.5 * x)


_GELU_C = math.sqrt(2.0 / math.pi)


def _gelu_parts(x):
    x2 = x * x
    t = jnp.tanh(_GELU_C * (x + 0.044715 * (x2 * x)))
    cdf = 0.5 * (1.0 + t)
    g = x * cdf
    dg = cdf + 0.5 * x * (1.0 - t * t) * (_GELU_C * (1.0 + 3.0 * 0.044715 * x2))
    return g, dg


def _gelu(x):
    t = jnp.tanh(_GELU_C * (x + 0.044715 * (x * x * x)))
    return x * (0.5 * (1.0 + t))


def _softplus(x):
    return jnp.maximum(x, 0.0) + jnp.log1p(jnp.exp(-jnp.abs(x)))


def _bcast_row(x, r, rows=SUBLANES):
    return jnp.broadcast_to(x[r:r + 1, :], (rows, x.shape[1]))


def _colsum8(x):
    return jnp.broadcast_to(jnp.sum(x, axis=0, keepdims=True), (SUBLANES, x.shape[1]))


def _shift_down(prev8, tile, s):
    if s == 0:
        return tile
    ext = jnp.concatenate([prev8, tile], axis=0)
    return pltpu.roll(ext, s, 0)[SUBLANES:, :]


def _shift_up(tile, next8, s):
    if s == 0:
        return tile
    ext = jnp.concatenate([tile, next8], axis=0)
    return pltpu.roll(ext, SUBLANES - s, 0)[SUBLANES:, :]


def _group_scan(a, b, reverse):
    n = a.shape[0]
    row = lax.broadcasted_iota(jnp.int32, a.shape, 0) & (SUBLANES - 1)
    for s in (1, 2, 4):
        shift = (n - s) if reverse else s
        a_sh = pltpu.roll(a, shift, 0)
        b_sh = pltpu.roll(b, shift, 0)
        m = (row <= SUBLANES - 1 - s) if reverse else (row >= s)
        b = jnp.where(m, a * b_sh + b, b)
        a = jnp.where(m, a * a_sh, a)
    return a, b


def _carry_scan(a_ref, b_ref, out_ref, carry0, reverse):
    n_groups = a_ref.shape[0] // SUBLANES

    def body(i, carry):
        g = (n_groups - 1 - i) if reverse else i
        r0 = pl.multiple_of(g * SUBLANES, SUBLANES)
        hg = a_ref[pl.ds(r0, SUBLANES), :] * carry + b_ref[pl.ds(r0, SUBLANES), :]
        out_ref[pl.ds(r0, SUBLANES), :] = hg
        return _bcast_row(hg, 0 if reverse else SUBLANES - 1)

    return lax.fori_loop(0, n_groups, body, carry0)


def _rms_fwd(h, gain):
    rstd = lax.rsqrt(jnp.mean(h * h, axis=-1, keepdims=True) + NORM_EPS)
    n = h * rstd
    return n, rstd, n * gain


def _rms_bwd(dy, n, rstd, gain):
    dn = dy * gain
    dh = rstd * (dn - n * jnp.mean(dn * n, axis=-1, keepdims=True))
    return dh, _colsum8(dy * n)


def _halo_rows(dtype):
    return SUBLANES * (4 // jnp.dtype(dtype).itemsize)


def _halo_map(tile_rows, col, halo_rows=SUBLANES):
    per = tile_rows // halo_rows
    return lambda i: (jnp.maximum(i * per - 1, 0), col)


def _resident(shape):
    return pl.BlockSpec(shape, lambda *_: (0,) * len(shape), pipeline_mode=pl.Buffered(1))


def _place():
    x, y, c = lax.axis_index("x"), lax.axis_index("y"), lax.axis_index("c")
    chips = [(1 - x, y), (x, 1 - y), (1 - x, 1 - y)]
    return x, y, c, chips


def _dev(x, y, c):
    return 4 * x + 2 * y + c


def _remote(src, dst, send_sem, recv_sem, to):
    return pltpu.make_async_remote_copy(src_ref=src, dst_ref=dst, send_sem=send_sem, recv_sem=recv_sem,
                                        device_id=to, device_id_type=MESH)


class _Comm:
    def __init__(self, operands, out_shape, sems, descs, aliases=()):
        self.operands, self.out_shape, self.sems, self.descs, self.aliases = operands, out_shape, sems, descs, aliases

    def start(self, ins, outs, sems):
        local, sends, _ = self.descs(ins, outs, sems)
        for cp in sends + local:
            cp.start()

    def wait(self, ins, outs, sems):
        local, sends, recvs = self.descs(ins, outs, sems)
        for cp in recvs:
            cp.wait_recv()
        for cp in sends:
            cp.wait_send()
        for cp in local:
            cp.wait()


def _gather_first(shards):
    n = len(shards)

    def descs(ins, outs, sems):
        send, recv, loc = sems
        x, y, c, chips = _place()
        me = _dev(x, y, c)
        targets = [(x, y, 1 - c)] + [(*chip, c) for chip in chips]
        local, sends, recvs = [], [], []
        for t in range(n):
            local.append(pltpu.make_async_copy(ins[t], outs[t].at[me], loc.at[t]))
            for k, to in enumerate(targets):
                i = 4 * t + k
                sends.append(_remote(ins[t], outs[t].at[me], send.at[i], recv.at[i], to))
                recvs.append(_remote(ins[t], outs[t].at[_dev(*to)], send.at[i], recv.at[i], to))
        return local, sends, recvs

    return _Comm(list(shards), [jax.ShapeDtypeStruct((N_DEV,) + s.shape, s.dtype) for s in shards],
                 [pltpu.SemaphoreType.DMA((4 * n,)), pltpu.SemaphoreType.DMA((4 * n,)), pltpu.SemaphoreType.DMA((n,))],
                 descs)


def _gather_second(gathered):
    n = len(gathered)

    def descs(ins, outs, sems):
        send, recv = sems
        x, y, c, chips = _place()
        sends, recvs = [], []
        for t in range(n):
            for j, chip in enumerate(chips):
                i = 3 * t + j
                have, get = _dev(*chip, c), _dev(*chip, 1 - c)
                sends.append(_remote(outs[t].at[have], outs[t].at[have], send.at[i], recv.at[i], (x, y, 1 - c)))
                recvs.append(_remote(outs[t].at[have], outs[t].at[get], send.at[i], recv.at[i], (x, y, 1 - c)))
        return [], sends, recvs

    return _Comm(list(gathered), [jax.ShapeDtypeStruct(g.shape, g.dtype) for g in gathered],
                 [pltpu.SemaphoreType.DMA((3 * n,)), pltpu.SemaphoreType.DMA((3 * n,))], descs,
                 aliases=[(t, t) for t in range(n)])


def _pair_exchange(blocks):
    n = len(blocks)

    def descs(ins, outs, sems):
        send, recv = sems
        x, y, c, _ = _place()
        sends, recvs = [], []
        for t in range(n):
            for q in range(N_CHIPS):
                i = N_CHIPS * t + q
                cp = _remote(ins[t].at[2 * q + 1 - c], outs[t].at[q], send.at[i], recv.at[i], (x, y, 1 - c))
                sends.append(cp)
                recvs.append(cp)
        return [], sends, recvs

    return _Comm(list(blocks), [jax.ShapeDtypeStruct((N_CHIPS,) + b.shape[1:], b.dtype) for b in blocks],
                 [pltpu.SemaphoreType.DMA((N_CHIPS * n,)), pltpu.SemaphoreType.DMA((N_CHIPS * n,))], descs)


def _chip_exchange(blocks):
    n = len(blocks)

    def descs(ins, outs, sems):
        send, recv, loc = sems
        x, y, c, chips = _place()
        me = 2 * x + y
        local, sends, recvs = [], [], []
        for t in range(n):
            local.append(pltpu.make_async_copy(ins[t].at[me], outs[t].at[me], loc.at[t]))
            for j, (px, py) in enumerate(chips):
                i = 3 * t + j
                q = 2 * px + py
                sends.append(_remote(ins[t].at[q], outs[t].at[me], send.at[i], recv.at[i], (px, py, c)))
                recvs.append(_remote(ins[t].at[q], outs[t].at[q], send.at[i], recv.at[i], (px, py, c)))
        return local, sends, recvs

    return _Comm(list(blocks), [jax.ShapeDtypeStruct(b.shape, b.dtype) for b in blocks],
                 [pltpu.SemaphoreType.DMA((3 * n,)), pltpu.SemaphoreType.DMA((3 * n,)), pltpu.SemaphoreType.DMA((n,))],
                 descs)


def _both(a, b):
    na, oa, sa = len(a.operands), len(a.out_shape), len(a.sems)

    def descs(ins, outs, sems):
        local_a, sends_a, recvs_a = a.descs(ins[:na], outs[:oa], sems[:sa])
        local_b, sends_b, recvs_b = b.descs(ins[na:], outs[oa:], sems[sa:])
        return local_a + local_b, sends_a + sends_b, recvs_a + recvs_b

    return _Comm(a.operands + b.operands, a.out_shape + b.out_shape, a.sems + b.sems, descs,
                 aliases=list(a.aliases) + [(na + i, oa + o) for i, o in b.aliases])


def _run_comms(comms, name):
    first = comms[0]
    n_in, n_out = len(first.operands), len(first.out_shape)

    def body(*refs):
        ins, outs, sems = refs[:n_in], refs[n_in:n_in + n_out], list(refs[n_in + n_out:])
        for k, comm in enumerate(comms):
            mine = [sems.pop(0) for _ in comm.sems]
            comm.start(ins if k == 0 else outs, outs, mine)
            comm.wait(ins if k == 0 else outs, outs, mine)

    outs = pl.pallas_call(
        body, name=name, out_shape=first.out_shape, in_specs=[ANY] * n_in, out_specs=[ANY] * n_out,
        scratch_shapes=[s for comm in comms for s in comm.sems],
    )(*first.operands)
    return list(outs)


def _call(body, *, name, grid, in_specs, out_specs, out_shape, operands, scratch_shapes=(), comm=None, aliases=None):
    sem = ("arbitrary",) * len(grid)
    params = pltpu.CompilerParams(dimension_semantics=sem, vmem_limit_bytes=VMEM_LIMIT)
    aliases = dict(aliases or {})
    if comm is None:
        return pl.pallas_call(body, name=name, grid=grid, in_specs=in_specs, out_specs=out_specs, out_shape=out_shape,
                              scratch_shapes=list(scratch_shapes), input_output_aliases=aliases,
                              compiler_params=params)(*operands)
    n_in, n_out, n_scr = len(in_specs), len(out_specs), len(scratch_shapes)
    c_in, c_out = len(comm.operands), len(comm.out_shape)

    def wrapped(*refs):
        refs = list(refs)
        ins, refs = refs[:n_in], refs[n_in:]
        cins, refs = refs[:c_in], refs[c_in:]
        outs, refs = refs[:n_out], refs[n_out:]
        couts, refs = refs[:c_out], refs[c_out:]
        scr, csems = refs[:n_scr], refs[n_scr:]
        first = last = None
        for axis, size in enumerate(grid):
            at_first, at_last = pl.program_id(axis) == 0, pl.program_id(axis) == size - 1
            first = at_first if first is None else first & at_first
            last = at_last if last is None else last & at_last

        @pl.when(first)
        def _():
            comm.start(cins, couts, csems)

        body(*ins, *outs, *scr)

        @pl.when(last)
        def _():
            comm.wait(cins, couts, csems)

    res = pl.pallas_call(
        wrapped, name=name, grid=grid, in_specs=list(in_specs) + [ANY] * c_in, out_specs=list(out_specs) + [ANY] * c_out,
        out_shape=list(out_shape) + list(comm.out_shape), scratch_shapes=list(scratch_shapes) + list(comm.sems),
        input_output_aliases={**aliases, **{n_in + i: n_out + o for i, o in comm.aliases}}, compiler_params=params,
    )(*operands, *comm.operands)
    return list(res[:n_out]), list(res[n_out:])


def _mm_nt(a, b, name, out_dtype, comm=None, tm=512, tn=512):
    m, k = a.shape
    n = b.shape[0]
    tm, tn = min(tm, m), min(tn, n)

    def body(a_ref, b_ref, o_ref):
        o_ref[...] = _dot_nt(a_ref[...], b_ref[...]).astype(o_ref.dtype)

    return _call(body, name=name, grid=(n // tn, m // tm), comm=comm,
                 in_specs=[pl.BlockSpec((tm, k), lambda j, i: (i, 0)), pl.BlockSpec((tn, k), lambda j, i: (j, 0))],
                 out_specs=[pl.BlockSpec((tm, tn), lambda j, i: (i, j))],
                 out_shape=[jax.ShapeDtypeStruct((m, n), out_dtype)], operands=[a, b])


def _mm_tn(a, b, name, blocks=1, tk=2048):
    t, m = a.shape
    n = b.shape[1]
    tk = min(tk, t)
    nk = t // tk
    cb = n // blocks
    per = max(1, 768 // cb) if blocks > 1 else 1
    tn = per * cb if blocks > 1 else min(1024, n)
    tm = min(1024, m)
    assert blocks == 1 or tm == m

    def body(a_ref, b_ref, o_ref, acc):
        k = pl.program_id(2)

        @pl.when(k == 0)
        def _():
            acc[...] = jnp.zeros_like(acc)
        acc[...] += _dot_tn(a_ref[...], b_ref[...])

        @pl.when(k == nk - 1)
        def _():
            if blocks == 1:
                o_ref[...] = acc[...].astype(o_ref.dtype)
            else:
                for s in range(per):
                    o_ref[s] = acc[:, s * cb:(s + 1) * cb].astype(o_ref.dtype)

    if blocks == 1:
        out_spec = pl.BlockSpec((tm, tn), lambda i, j, k: (i, j))
        out_shape = jax.ShapeDtypeStruct((m, n), GRAD_DTYPE)
    else:
        out_spec = pl.BlockSpec((per, m, cb), lambda i, j, k: (j, 0, 0))
        out_shape = jax.ShapeDtypeStruct((blocks, m, cb), GRAD_DTYPE)
    return _call(body, name=name, grid=(m // tm, n // tn, nk),
                 in_specs=[pl.BlockSpec((tk, tm), lambda i, j, k: (k, i)), pl.BlockSpec((tk, tn), lambda i, j, k: (k, j))],
                 out_specs=[out_spec], out_shape=[out_shape], operands=[a, b],
                 scratch_shapes=[pltpu.VMEM((tm, tn), F32)])[0]


def _inproj_fwd(x, g1, w_blocks, comm):
    t = x.shape[0]
    tm = min(512, t)
    nb, _, cb = w_blocks.shape

    def body(x_ref, g_ref, w_ref, u_ref, p_ref):
        _, _, u = _rms_fwd(x_ref[...], g_ref[...])
        u = u.astype(MXU_DTYPE)
        u_ref[...] = u
        for d in range(nb):
            p_ref[:, d * cb:(d + 1) * cb] = _dot(u, w_ref[d])

    return _call(body, name="inproj_fwd", grid=(t // tm,), comm=comm,
                 in_specs=[pl.BlockSpec((tm, D_MODEL), lambda i: (i, 0)), pl.BlockSpec((1, D_MODEL), lambda i: (0, 0)),
                           _resident(w_blocks.shape)],
                 out_specs=[pl.BlockSpec((tm, D_MODEL), lambda i: (i, 0)), pl.BlockSpec((tm, D_IN), lambda i: (i, 0))],
                 out_shape=[jax.ShapeDtypeStruct((t, D_MODEL), MXU_DTYPE), jax.ShapeDtypeStruct((t, D_IN), F32)],
                 operands=[x, g1, w_blocks])


def _lru_gates(xc, wa, ba, wx, bx, sp):
    r = _sigmoid(_dot(xc, wa) + ba)
    ig = _sigmoid(_dot(xc, wx) + bx)
    log_a = (-LRU_C) * r * sp
    a = jnp.exp(log_a)
    m = jnp.sqrt(-jnp.tanh(log_a) * (a * a + 1.0))
    return r, ig, a, m


def _lru_fwd(proj, conv_w, conv_b, wa, ba, wx, bx, lam, gain, comm):
    t = proj.shape[0]
    tm = min(256, t)
    c = D_LRU

    def body(x_ref, xh_ref, g_ref, cw_ref, cb_ref, wa_ref, ba_ref, wx_ref, bx_ref, lam_ref, gain_ref,
             xc_ref, h_ref, y_ref, a_scr, b_scr, carry):
        i = pl.program_id(0)

        @pl.when(i == 0)
        def _():
            carry[...] = jnp.zeros_like(carry)

        x = x_ref[...]
        prev = jnp.where(i == 0, 0.0, xh_ref[...])
        cw = cw_ref[...]
        xc = cb_ref[...] + cw[LRU_CONV - 1:LRU_CONV, :] * x
        for k in range(LRU_CONV - 1):
            xc = xc + cw[k:k + 1, :] * _shift_down(prev, x, LRU_CONV - 1 - k)
        xc_ref[...] = xc
        sp = _softplus(-lam_ref[...])
        _, ig, a, m = _lru_gates(xc, wa_ref[...], ba_ref[...], wx_ref[...], bx_ref[...], sp)
        ga, gb = _group_scan(a, m * (ig * xc), reverse=False)
        a_scr[...] = ga
        b_scr[...] = gb
        carry[...] = _carry_scan(a_scr, b_scr, h_ref, carry[...], reverse=False)
        z = h_ref[...] * _gelu(g_ref[...])
        _, _, y = _rms_fwd(z, gain_ref[...])
        y_ref[...] = y.astype(y_ref.dtype)

    row = lambda i: (i, 0)
    full = lambda i: (0, 0)
    vec = pl.BlockSpec((1, c), full)
    return _call(body, name="lru_fwd", grid=(t // tm,), comm=comm,
                 in_specs=[pl.BlockSpec((tm, c), row), pl.BlockSpec((SUBLANES, c), _halo_map(tm, 0)),
                           pl.BlockSpec((tm, c), lambda i: (i, 1)),
                           pl.BlockSpec((LRU_CONV, c), full), vec, pl.BlockSpec((c, c), full), vec,
                           pl.BlockSpec((c, c), full), vec, vec, vec],
                 out_specs=[pl.BlockSpec((tm, c), row), pl.BlockSpec((tm, c), row), pl.BlockSpec((tm, c), row)],
                 out_shape=[jax.ShapeDtypeStruct((t, c), F32), jax.ShapeDtypeStruct((t, c), F32),
                            jax.ShapeDtypeStruct((t, c), MXU_DTYPE)],
                 scratch_shapes=[pltpu.VMEM((tm, c), F32), pltpu.VMEM((tm, c), F32), pltpu.VMEM((SUBLANES, c), F32)],
                 operands=[proj, proj, proj, conv_w, conv_b, wa, ba, wx, bx, lam, gain])


def _ret_consts():
    c = RET_CHUNK
    log_g = jnp.log1p(-jnp.exp2(-5.0 - jnp.arange(RET_HEADS, dtype=F32)))
    idx = jnp.arange(c, dtype=F32)
    diff = idx[:, None] - idx[None, :]
    decay = jnp.where(diff[None] >= 0, jnp.exp(jnp.maximum(diff, 0.0)[None] * log_g[:, None, None]), 0.0)
    zeta = jnp.exp((c - 1 - idx)[None, :] * log_g[:, None])
    xi = jnp.exp((idx + 1.0)[None, :] * log_g[:, None])
    spread = lambda v: jnp.repeat(v.T, RET_HEAD_DIM, axis=1)
    log_g_np = np.log1p(-np.exp2(-5.0 - np.arange(RET_HEADS, dtype=np.float32))).astype(np.float32)
    g_chunk = [float(np.exp(np.float32(c) * lg)) for lg in log_g_np]
    return decay, spread(xi), spread(zeta), g_chunk


def _rope_tables(t):
    pos = np.arange(t, dtype=np.float32)
    inv_freq = np.float32(ROPE_BASE) ** (-np.arange(0, RET_HEAD_DIM, 2, dtype=np.float32) / np.float32(RET_HEAD_DIM))
    ang = (pos[:, None] * inv_freq.astype(np.float32)[None, :]).astype(np.float32).astype(np.float64)
    cos, sin = np.cos(ang).astype(np.float32), np.sin(ang).astype(np.float32)
    return jnp.asarray(np.concatenate([cos, cos], axis=-1)), jnp.asarray(np.concatenate([-sin, sin], axis=-1))


def _rope(x, cos2, sin_signed):
    return x * cos2 + pltpu.roll(x, RET_HEAD_DIM // 2, 1) * sin_signed


def _rope_bwd(d, cos2, sin_signed):
    return d * cos2 + pltpu.roll(d * sin_signed, RET_HEAD_DIM // 2, 1)


RET_SCALE = RET_HEAD_DIM ** -0.5


RET_CHUNKS_PER_STEP = 2


def _ret_fwd(proj, cos2, sin_signed, gain, comm):
    t = proj.shape[0]
    c, d, nh = RET_CHUNK, RET_HEAD_DIM, RET_HEADS
    n_chunks = t // c
    per = RET_CHUNKS_PER_STEP if n_chunks % RET_CHUNKS_PER_STEP == 0 else 1
    rows = per * c
    decay, xi, zeta, g_chunk = _ret_consts()

    def body(qk_ref, vg_ref, cos_ref, sin_ref, dec_ref, xi_ref, zeta_ref, gain_ref, o_ref, y_ref, st_ref, state):
        @pl.when(pl.program_id(0) == 0)
        def _():
            state[...] = jnp.zeros_like(state)

        cur = [state[h] for h in range(nh)]
        for s in range(per):
            rs = slice(s * c, (s + 1) * c)
            cos2, sin_s = cos_ref[rs, :], sin_ref[rs, :]
            for h in range(nh):
                lo = h * d
                q = _rope(qk_ref[rs, lo:lo + d], cos2, sin_s)
                k = _rope(qk_ref[rs, D_RET + lo:D_RET + lo + d], cos2, sin_s) * RET_SCALE
                v = vg_ref[rs, lo:lo + d]
                g = vg_ref[rs, D_RET + lo:D_RET + lo + d]
                s_prev = cur[h]
                st_ref[s, h] = s_prev
                scores = _dot_nt(q, k) * dec_ref[h]
                o = _dot(scores, v) + _dot(q * xi_ref[:, lo:lo + d], s_prev)
                cur[h] = s_prev * g_chunk[h] + _dot_tn(k * zeta_ref[:, lo:lo + d], v)
                o_ref[rs, lo:lo + d] = o
                mu = jnp.mean(o, axis=-1, keepdims=True)
                oc = o - mu
                on = oc * lax.rsqrt(jnp.mean(oc * oc, axis=-1, keepdims=True) + NORM_EPS)
                y_ref[rs, lo:lo + d] = (on * gain_ref[:, lo:lo + d] * (g * _sigmoid(g))).astype(y_ref.dtype)
        for h in range(nh):
            state[h] = cur[h]

    full2 = lambda i: (0, 0)
    return _call(body, name="ret_fwd", grid=(n_chunks // per,), comm=comm,
                 in_specs=[pl.BlockSpec((rows, 2 * D_RET), lambda i: (i, 1)),
                           pl.BlockSpec((rows, 2 * D_RET), lambda i: (i, 2)),
                           pl.BlockSpec((rows, d), lambda i: (i, 0)), pl.BlockSpec((rows, d), lambda i: (i, 0)),
                           pl.BlockSpec((nh, c, c), lambda i: (0, 0, 0)), pl.BlockSpec((c, D_RET), full2),
                           pl.BlockSpec((c, D_RET), full2), pl.BlockSpec((1, D_RET), full2)],
                 out_specs=[pl.BlockSpec((rows, D_RET), lambda i: (i, 0)), pl.BlockSpec((rows, D_RET), lambda i: (i, 0)),
                            pl.BlockSpec((per, nh, d, d), lambda i: (i, 0, 0, 0))],
                 out_shape=[jax.ShapeDtypeStruct((t, D_RET), F32), jax.ShapeDtypeStruct((t, D_RET), MXU_DTYPE),
                            jax.ShapeDtypeStruct((n_chunks, nh, d, d), F32)],
                 scratch_shapes=[pltpu.VMEM((nh, d, d), F32)],
                 operands=[proj, proj, cos2, sin_signed, decay, xi, zeta, gain])


def _outproj_fwd(x, y_lru, y_ret, w_out, g2):
    t = x.shape[0]
    tm = min(512, t)

    def body(x_ref, yl_ref, yr_ref, w_ref, g_ref, h1_ref, u2_ref):
        h1 = x_ref[...] + _dot(yl_ref[...], w_ref[:D_LRU, :]) + _dot(yr_ref[...], w_ref[D_LRU:, :])
        h1_ref[...] = h1
        _, _, u = _rms_fwd(h1, g_ref[...])
        u2_ref[...] = u.astype(u2_ref.dtype)

    row = lambda i: (i, 0)
    return _call(body, name="outproj_fwd", grid=(t // tm,),
                 in_specs=[pl.BlockSpec((tm, D_MODEL), row), pl.BlockSpec((tm, D_LRU), row), pl.BlockSpec((tm, D_RET), row),
                           _resident((D_MODEL, D_MODEL)), pl.BlockSpec((1, D_MODEL), lambda i: (0, 0))],
                 out_specs=[pl.BlockSpec((tm, D_MODEL), row), pl.BlockSpec((tm, D_MODEL), row)],
                 out_shape=[jax.ShapeDtypeStruct((t, D_MODEL), F32), jax.ShapeDtypeStruct((t, D_MODEL), MXU_DTYPE)],
                 operands=[x, y_lru, y_ret, w_out, g2])


FFN_TN = 768
FFN_NJ = D_FF // FFN_TN


def _ffn_fwd(u2, w_blocks, conv_w, conv_b, w_down, h1, gf, target):
    t = u2.shape[0]
    tm = min(256, t)
    tn, nj = FFN_TN, FFN_NJ
    hb = _halo_rows(u2.dtype)
    assert w_blocks.shape == (2 * nj, D_MODEL, tn)

    def project(u_ext, w, up_ref, conv_ref, cw_ref, cb_ref, first):
        ext = _dot(u_ext, w)
        x = ext[hb:, :]
        up_ref[...] = x.astype(up_ref.dtype)
        prev = jnp.where(first, 0.0, ext[hb - SUBLANES:hb, :])
        cw = cw_ref[...]
        y = cb_ref[...] + cw[FFN_CONV - 1:FFN_CONV, :] * x
        for k in range(FFN_CONV - 1):
            y = y + cw[k:k + 1, :] * _shift_down(prev, x, FFN_CONV - 1 - k)
        conv_ref[...] = y.astype(conv_ref.dtype)
        return y

    def body(u_ref, uh_ref, w_ref, cwa_ref, cwv_ref, cba_ref, cbv_ref, wd_ref, h1_ref, gf_ref, tg_ref,
             upa_ref, upv_ref, ca_ref, cv_ref, act_ref, dh_ref, dhb_ref, dgf_ref, loss_ref, acc):
        i, j = pl.program_id(0), pl.program_id(1)

        @pl.when((i == 0) & (j == 0))
        def _():
            dgf_ref[...] = jnp.zeros_like(dgf_ref)
            loss_ref[...] = jnp.zeros_like(loss_ref)

        @pl.when(j == 0)
        def _():
            acc[...] = jnp.zeros_like(acc)

        u_ext = jnp.concatenate([uh_ref[...], u_ref[...]], axis=0)
        a = project(u_ext, w_ref[j], upa_ref, ca_ref, cwa_ref, cba_ref, i == 0)
        v = project(u_ext, w_ref[nj + j], upv_ref, cv_ref, cwv_ref, cbv_ref, i == 0)
        act = (_gelu(a) * v).astype(act_ref.dtype)
        act_ref[...] = act
        acc[...] += _dot(act, wd_ref[pl.ds(pl.multiple_of(j * tn, tn), tn), :])

        @pl.when(j == nj - 1)
        def _():
            n, rstd, y = _rms_fwd(h1_ref[...] + acc[...], gf_ref[...])
            err = y - tg_ref[...]
            loss_ref[...] += (0.5 / D_MODEL) * jnp.sum(err * err)
            dh, dgf = _rms_bwd(err * (1.0 / D_MODEL), n, rstd, gf_ref[...])
            dgf_ref[...] += dgf
            dh_ref[...] = dh
            dhb_ref[...] = dh.astype(dhb_ref.dtype)

    per = tm // hb
    row = lambda i, j: (i, 0)
    const = lambda i, j: (0, 0)
    tile = pl.BlockSpec((tm, tn), lambda i, j: (i, j))
    return _call(body, name="ffn_fwd", grid=(t // tm, nj),
                 in_specs=[pl.BlockSpec((tm, D_MODEL), row),
                           pl.BlockSpec((hb, D_MODEL), lambda i, j: (jnp.maximum(i * per - 1, 0), 0)),
                           _resident(w_blocks.shape),
                           pl.BlockSpec((FFN_CONV, tn), lambda i, j: (0, j)),
                           pl.BlockSpec((FFN_CONV, tn), lambda i, j: (0, j + nj)),
                           pl.BlockSpec((1, tn), lambda i, j: (0, j)), pl.BlockSpec((1, tn), lambda i, j: (0, j + nj)),
                           _resident((D_FF, D_MODEL)),
                           pl.BlockSpec((tm, D_MODEL), row), pl.BlockSpec((1, D_MODEL), const),
                           pl.BlockSpec((tm, D_MODEL), row)],
                 out_specs=[tile] * 5 + [pl.BlockSpec((tm, D_MODEL), row),
                            pl.BlockSpec((tm, D_MODEL), row), pl.BlockSpec((SUBLANES, D_MODEL), const),
                            pl.BlockSpec((SUBLANES, LANES), const)],
                 out_shape=[jax.ShapeDtypeStruct((t, D_FF), MXU_DTYPE)] * 5 + [
                            jax.ShapeDtypeStruct((t, D_MODEL), F32),
                            jax.ShapeDtypeStruct((t, D_MODEL), MXU_DTYPE), jax.ShapeDtypeStruct((SUBLANES, D_MODEL), F32),
                            jax.ShapeDtypeStruct((SUBLANES, LANES), F32)],
                 scratch_shapes=[pltpu.VMEM((tm, D_MODEL), F32)],
                 operands=[u2, u2, w_blocks, conv_w, conv_w, conv_b, conv_b, w_down, h1, gf, target])


FFN_ACC_ROWS = SUBLANES * (FFN_CONV + 1)


def _ffn_bwd(dh2, dh2_b, w_down, up_a, up_v, conv_a, conv_v, conv_w, w_up_blocks, h1, g2, comm):
    t = up_a.shape[0]
    tm = min(256, t)
    tn, nj = FFN_TN, FFN_NJ
    ni = t // tm
    assert w_up_blocks.shape == (2 * nj, D_MODEL, tn)

    def conv_bwd(dy, x, cw, acc_ref, carry_ref, dup_ref):
        nxt = carry_ref[...]
        carry_ref[...] = dy[:SUBLANES, :]
        ahead = [_shift_up(dy, nxt, FFN_CONV - 1 - k) for k in range(FFN_CONV)]
        dx = cw[FFN_CONV - 1:FFN_CONV, :] * dy
        for k in range(FFN_CONV - 1):
            dx = dx + cw[k:k + 1, :] * ahead[k]
        dx = dx.astype(dup_ref.dtype)
        dup_ref[...] = dx
        for k in range(FFN_CONV):
            acc_ref[k * SUBLANES:(k + 1) * SUBLANES, :] += _colsum8(ahead[k] * x)
        acc_ref[FFN_CONV * SUBLANES:, :] += _colsum8(dy)
        return dx

    def body(dh_ref, dhb_ref, wd_ref, ua_ref, uv_ref, ca_ref, cv_ref, cwa_ref, cwv_ref, wu_ref, h1_ref, g2_ref,
             dua_ref, duv_ref, acca_ref, accv_ref, dh1_ref, dh1b_ref, dg2_ref, carry_a, carry_v, du):
        i, j = pl.program_id(0), pl.program_id(1)

        @pl.when((i == 0) & (j == 0))
        def _():
            for ref in (acca_ref, accv_ref, carry_a, carry_v, dg2_ref):
                ref[...] = jnp.zeros_like(ref)

        v = cv_ref[...].astype(F32)
        g, dg = _gelu_parts(ca_ref[...].astype(F32))
        dact = _dot_nt(dhb_ref[...], wd_ref[pl.ds(pl.multiple_of(j * tn, tn), tn), :])
        da = conv_bwd(dact * v * dg, ua_ref[...].astype(F32), cwa_ref[...], acca_ref.at[j], carry_a.at[j], dua_ref)
        dv = conv_bwd(dact * g, uv_ref[...].astype(F32), cwv_ref[...], accv_ref.at[j], carry_v.at[j], duv_ref)
        part = _dot_nt(da, wu_ref[j]) + _dot_nt(dv, wu_ref[nj + j])

        @pl.when(j == 0)
        def _():
            du[...] = part

        @pl.when(j > 0)
        def _():
            du[...] += part

        @pl.when(j == nj - 1)
        def _():
            n, rstd, _ = _rms_fwd(h1_ref[...], g2_ref[...])
            dh1, dg2 = _rms_bwd(du[...], n, rstd, g2_ref[...])
            dh1 = dh1 + dh_ref[...]
            dg2_ref[...] += dg2
            dh1_ref[...] = dh1
            dh1b_ref[...] = dh1.astype(dh1b_ref.dtype)

    row = lambda i, j: (ni - 1 - i, 0)
    const = lambda i, j: (0, 0)
    tile = pl.BlockSpec((tm, tn), lambda i, j: (ni - 1 - i, j))
    acc = pl.BlockSpec((nj, FFN_ACC_ROWS, tn), lambda i, j: (0, 0, 0))
    return _call(body, name="ffn_bwd", grid=(ni, nj), comm=comm,
                 in_specs=[pl.BlockSpec((tm, D_MODEL), row), pl.BlockSpec((tm, D_MODEL), row),
                           _resident((D_FF, D_MODEL)), tile, tile, tile, tile,
                           pl.BlockSpec((FFN_CONV, tn), lambda i, j: (0, j)),
                           pl.BlockSpec((FFN_CONV, tn), lambda i, j: (0, j + nj)),
                           _resident(w_up_blocks.shape), pl.BlockSpec((tm, D_MODEL), row),
                           pl.BlockSpec((1, D_MODEL), const)],
                 out_specs=[tile, tile, acc, acc, pl.BlockSpec((tm, D_MODEL), row), pl.BlockSpec((tm, D_MODEL), row),
                            pl.BlockSpec((SUBLANES, D_MODEL), const)],
                 out_shape=[jax.ShapeDtypeStruct((t, D_FF), MXU_DTYPE), jax.ShapeDtypeStruct((t, D_FF), MXU_DTYPE),
                            jax.ShapeDtypeStruct((nj, FFN_ACC_ROWS, tn), F32),
                            jax.ShapeDtypeStruct((nj, FFN_ACC_ROWS, tn), F32),
                            jax.ShapeDtypeStruct((t, D_MODEL), F32), jax.ShapeDtypeStruct((t, D_MODEL), MXU_DTYPE),
                            jax.ShapeDtypeStruct((SUBLANES, D_MODEL), F32)],
                 scratch_shapes=[pltpu.VMEM((nj, SUBLANES, tn), F32), pltpu.VMEM((nj, SUBLANES, tn), F32),
                                 pltpu.VMEM((tm, D_MODEL), F32)],
                 operands=[dh2, dh2_b, w_down, up_a, up_v, conv_a, conv_v, conv_w, conv_w, w_up_blocks, h1, g2])


def _norm_bwd_matmul(parts, w_blocks, h, gain, d_res, name, comm=None, tiles=None, into=None):
    t = h.shape[0]
    tm = min(256, t)
    first, count = tiles or (0, t // tm)
    n_parts = len(parts)
    nb, _, cb = w_blocks.shape
    where = []
    for p, a in enumerate(parts):
        assert a.shape[1] % cb == 0
        where += [(p, lo) for lo in range(0, a.shape[1], cb)]
    assert len(where) == nb
    n_in = n_parts + 4

    def body(*refs):
        a_refs = refs[:n_parts]
        w_ref, h_ref, g_ref, dres_ref = refs[n_parts:n_in]
        dh_ref, dhb_ref, dg_ref = refs[-3:]

        @pl.when(pl.program_id(0) == 0)
        def _():
            dg_ref[...] = jnp.zeros_like(dg_ref)

        du = None
        for d, (p, lo) in enumerate(where):
            term = _dot_nt(a_refs[p][:, lo:lo + cb], w_ref[d])
            du = term if du is None else du + term
        n, rstd, _ = _rms_fwd(h_ref[...], g_ref[...])
        dh, dg = _rms_bwd(du, n, rstd, g_ref[...])
        dh = dh + dres_ref[...]
        dg_ref[...] += dg
        dh_ref[...] = dh
        dhb_ref[...] = dh.astype(dhb_ref.dtype)

    row = lambda i: (i + first, 0)
    const = lambda i: (0, 0)
    in_specs = [pl.BlockSpec((tm, a.shape[1]), row) for a in parts] + [
        _resident(w_blocks.shape), pl.BlockSpec((tm, D_MODEL), row), pl.BlockSpec((1, D_MODEL), const),
        pl.BlockSpec((tm, D_MODEL), row)]
    operands = [*parts, w_blocks, h, gain, d_res]
    aliases = {}
    if into is not None:
        in_specs += [ANY, ANY]
        operands += list(into)
        aliases = {n_in: 0, n_in + 1: 1}
    return _call(body, name=name, grid=(count,), comm=comm, in_specs=in_specs,
                 out_specs=[pl.BlockSpec((tm, D_MODEL), row), pl.BlockSpec((tm, D_MODEL), row),
                            pl.BlockSpec((SUBLANES, D_MODEL), const)],
                 out_shape=[jax.ShapeDtypeStruct((t, D_MODEL), F32), jax.ShapeDtypeStruct((t, D_MODEL), MXU_DTYPE),
                            jax.ShapeDtypeStruct((SUBLANES, D_MODEL), F32)],
                 operands=operands, aliases=aliases)


def _ret_bwd(proj, cos2, sin_signed, gain, o, states, dmixed, comm):
    t = proj.shape[0]
    c, d, nh = RET_CHUNK, RET_HEAD_DIM, RET_HEADS
    n_chunks = t // c
    per = RET_CHUNKS_PER_STEP if n_chunks % RET_CHUNKS_PER_STEP == 0 else 1
    rows = per * c
    n_steps = n_chunks // per
    decay, xi, zeta, g_chunk = _ret_consts()
    base = 2 * D_LRU

    def body(qk_ref, vg_ref, cos_ref, sin_ref, dec_ref, xi_ref, zeta_ref, gain_ref, o_ref, st_ref, dy_ref,
             dp_ref, dgain_ref, gstate):
        @pl.when(pl.program_id(0) == 0)
        def _():
            gstate[...] = jnp.zeros_like(gstate)
            dgain_ref[...] = jnp.zeros_like(dgain_ref)

        cur = [gstate[h] for h in range(nh)]
        for s in reversed(range(per)):
            rs = slice(s * c, (s + 1) * c)
            cos2, sin_s = cos_ref[rs, :], sin_ref[rs, :]
            for h in range(nh):
                lo = h * d
                q = _rope(qk_ref[rs, lo:lo + d], cos2, sin_s)
                k = _rope(qk_ref[rs, D_RET + lo:D_RET + lo + d], cos2, sin_s) * RET_SCALE
                v = vg_ref[rs, lo:lo + d]
                g = vg_ref[rs, D_RET + lo:D_RET + lo + d]
                gain_h = gain_ref[:, lo:lo + d]
                xi_h, zeta_h, dec = xi_ref[:, lo:lo + d], zeta_ref[:, lo:lo + d], dec_ref[h]
                dy = dy_ref[rs, lo:lo + d]
                sg = _sigmoid(g)
                o_h = o_ref[rs, lo:lo + d]
                oc = o_h - jnp.mean(o_h, axis=-1, keepdims=True)
                rstd = lax.rsqrt(jnp.mean(oc * oc, axis=-1, keepdims=True) + NORM_EPS)
                on = oc * rstd
                at = base + 3 * D_RET + lo
                dp_ref[rs, at:at + d] = (dy * on * gain_h * (sg * (1.0 + g * (1.0 - sg)))).astype(dp_ref.dtype)
                don_g = dy * (g * sg)
                dgain_ref[:, lo:lo + d] += _colsum8(don_g * on)
                don = don_g * gain_h
                do = rstd * (don - jnp.mean(don, axis=-1, keepdims=True)
                             - on * jnp.mean(don * on, axis=-1, keepdims=True))
                s_prev = st_ref[s, h]
                g_next = cur[h]
                p = _dot_nt(q, k) * dec
                dpm = _dot_nt(do, v) * dec
                dq = _dot(dpm, k) + _dot_nt(do, s_prev) * xi_h
                dk = _dot_tn(dpm, q) + _dot_nt(v, g_next) * zeta_h
                dv = _dot_tn(p, do) + _dot(k * zeta_h, g_next)
                cur[h] = g_next * g_chunk[h] + _dot_tn(q * xi_h, do)
                dp_ref[rs, base + lo:base + lo + d] = _rope_bwd(dq, cos2, sin_s).astype(dp_ref.dtype)
                at = base + D_RET + lo
                dp_ref[rs, at:at + d] = _rope_bwd(dk * RET_SCALE, cos2, sin_s).astype(dp_ref.dtype)
                at = base + 2 * D_RET + lo
                dp_ref[rs, at:at + d] = dv.astype(dp_ref.dtype)
        for h in range(nh):
            gstate[h] = cur[h]

    rev = lambda col: (lambda i: (n_steps - 1 - i, col))
    full2 = lambda i: (0, 0)
    return _call(body, name="ret_bwd", grid=(n_steps,), comm=comm,
                 in_specs=[pl.BlockSpec((rows, 2 * D_RET), rev(1)), pl.BlockSpec((rows, 2 * D_RET), rev(2)),
                           pl.BlockSpec((rows, d), rev(0)), pl.BlockSpec((rows, d), rev(0)),
                           pl.BlockSpec((nh, c, c), lambda i: (0, 0, 0)), pl.BlockSpec((c, D_RET), full2),
                           pl.BlockSpec((c, D_RET), full2), pl.BlockSpec((1, D_RET), full2),
                           pl.BlockSpec((rows, D_RET), rev(0)),
                           pl.BlockSpec((per, nh, d, d), lambda i: (n_steps - 1 - i, 0, 0, 0)),
                           pl.BlockSpec((rows, D_RET), rev(1))],
                 out_specs=[pl.BlockSpec((rows, D_IN), rev(0)), pl.BlockSpec((SUBLANES, D_RET), full2)],
                 out_shape=[jax.ShapeDtypeStruct((t, D_IN), MXU_DTYPE), jax.ShapeDtypeStruct((SUBLANES, D_RET), F32)],
                 scratch_shapes=[pltpu.VMEM((nh, d, d), F32)],
                 operands=[proj, proj, cos2, sin_signed, decay, xi, zeta, gain, o, states, dmixed])


LRU_ACC = {"conv_w": 0, "conv_b": LRU_CONV, "gate_a_b": LRU_CONV + 1, "gate_x_b": LRU_CONV + 2,
           "lambda": LRU_CONV + 3, "norm_gain": LRU_CONV + 4}
LRU_ACC_ROWS = SUBLANES * (LRU_CONV + 5)


def _lru_bwd(proj, xc_all, h_all, dmixed, dproj, conv_w, wa, ba, wx, bx, lam, gain):
    t = proj.shape[0]
    tm = min(256, t)
    c = D_LRU
    ni = t // tm

    def body(x_ref, xh_ref, g_ref, xc_ref, h_ref, hh_ref, dy_ref, cw_ref, wa_ref, ba_ref, wx_ref, bx_ref, lam_ref,
             gain_ref, dproj_in, dp_ref, acc_ref, dwa_ref, dwx_ref, a_scr, b_scr, mu_scr, carry_mu, carry_dxc):
        del dproj_in
        i = pl.program_id(0)
        r = ni - 1 - i

        @pl.when(i == 0)
        def _():
            acc_ref[...] = jnp.zeros_like(acc_ref)
            dwa_ref[...] = jnp.zeros_like(dwa_ref)
            dwx_ref[...] = jnp.zeros_like(dwx_ref)
            carry_mu[...] = jnp.zeros_like(carry_mu)
            carry_dxc[...] = jnp.zeros_like(carry_dxc)

        def add(name, val, k=0):
            lo = (LRU_ACC[name] + k) * SUBLANES
            acc_ref[lo:lo + SUBLANES, :] += _colsum8(val)

        xc, h = xc_ref[...], h_ref[...]
        lam_v = lam_ref[...]
        sp = _softplus(-lam_v)
        rg, ig, a, m = _lru_gates(xc, wa_ref[...], ba_ref[...], wx_ref[...], bx_ref[...], sp)
        gl, dgl = _gelu_parts(g_ref[...])
        zn, rstd, _ = _rms_fwd(h * gl, gain_ref[...])
        dy = dy_ref[...]
        dz, dgain = _rms_bwd(dy, zn, rstd, gain_ref[...])
        lo = LRU_ACC["norm_gain"] * SUBLANES
        acc_ref[lo:lo + SUBLANES, :] += dgain
        dp_ref[:, c:] = (dz * h * dgl).astype(dp_ref.dtype)
        dh = dz * gl
        ga, gb = _group_scan(a, a * dh, reverse=True)
        a_scr[...] = ga
        b_scr[...] = gb
        mu_next_tile = carry_mu[...]
        carry_mu[...] = _carry_scan(a_scr, b_scr, mu_scr, mu_next_tile, reverse=True)
        lam_t = dh + _shift_up(mu_scr[...], mu_next_tile, 1)
        h_prev = _shift_down(jnp.where(r == 0, 0.0, hh_ref[...]), h, 1)
        da = lam_t * h_prev
        dig = lam_t * m * xc
        dxc = lam_t * m * ig
        dlog_a = da * a - (lam_t * ig * xc) * (a * a) / m
        dpr = dlog_a * ((-LRU_C) * sp) * rg * (1.0 - rg)
        add("lambda", dlog_a * ((-LRU_C) * rg) * (-_sigmoid(-lam_v)))
        dpi = dig * ig * (1.0 - ig)
        add("gate_a_b", dpr)
        add("gate_x_b", dpi)
        dwa_ref[...] += _dot_tn(xc, dpr)
        dwx_ref[...] += _dot_tn(xc, dpi)
        dxc = dxc + _dot_nt(dpr, wa_ref[...]) + _dot_nt(dpi, wx_ref[...])
        add("conv_b", dxc)
        x = x_ref[...]
        prev = jnp.where(r == 0, 0.0, xh_ref[...])
        cw = cw_ref[...]
        nxt = carry_dxc[...]
        carry_dxc[...] = dxc[:SUBLANES, :]
        dx = cw[LRU_CONV - 1:LRU_CONV, :] * dxc
        for k in range(LRU_CONV - 1):
            dx = dx + cw[k:k + 1, :] * _shift_up(dxc, nxt, LRU_CONV - 1 - k)
        for k in range(LRU_CONV):
            add("conv_w", dxc * _shift_down(prev, x, LRU_CONV - 1 - k), k)
        dp_ref[:, :c] = dx.astype(dp_ref.dtype)

    per = tm // SUBLANES
    rev = lambda col: (lambda i: (ni - 1 - i, col))
    halo = lambda i: (jnp.maximum((ni - 1 - i) * per - 1, 0), 0)
    full = lambda i: (0, 0)
    vec = pl.BlockSpec((1, c), full)
    mat = pl.BlockSpec((c, c), full)
    return pl.pallas_call(
        body, name="lru_bwd", grid=(ni,),
        in_specs=[pl.BlockSpec((tm, c), rev(0)), pl.BlockSpec((SUBLANES, c), halo), pl.BlockSpec((tm, c), rev(1)),
                  pl.BlockSpec((tm, c), rev(0)), pl.BlockSpec((tm, c), rev(0)), pl.BlockSpec((SUBLANES, c), halo),
                  pl.BlockSpec((tm, c), rev(0)), pl.BlockSpec((LRU_CONV, c), full), mat, vec, mat, vec, vec, vec, ANY],
        out_specs=[pl.BlockSpec((tm, 2 * c), rev(0)), pl.BlockSpec((LRU_ACC_ROWS, c), full), mat, mat],
        out_shape=[jax.ShapeDtypeStruct(dproj.shape, dproj.dtype), jax.ShapeDtypeStruct((LRU_ACC_ROWS, c), F32),
                   jax.ShapeDtypeStruct((c, c), F32), jax.ShapeDtypeStruct((c, c), F32)],
        scratch_shapes=[pltpu.VMEM((tm, c), F32), pltpu.VMEM((tm, c), F32), pltpu.VMEM((tm, c), F32),
                        pltpu.VMEM((SUBLANES, c), F32), pltpu.VMEM((SUBLANES, c), F32)],
        input_output_aliases={14: 0},
        compiler_params=pltpu.CompilerParams(dimension_semantics=("arbitrary",), vmem_limit_bytes=VMEM_LIMIT),
    )(proj, proj, proj, xc_all, h_all, h_all, dmixed, conv_w, wa, ba, wx, bx, lam, gain, dproj)


def _pair_sum(core, a, b, name):
    n, r, c = b.shape
    spec = pl.BlockSpec((None, r, c), lambda q, core: (q, 0, 0))

    def body(core_ref, a_ref, b_ref, o_ref):
        o_ref[...] = (a_ref[...].astype(F32) + b_ref[...].astype(F32)).astype(o_ref.dtype)

    return pl.pallas_call(
        body, name=name,
        grid_spec=pltpu.PrefetchScalarGridSpec(
            num_scalar_prefetch=1, grid=(n,),
            in_specs=[pl.BlockSpec((None, r, c), lambda q, core: (2 * q + core[0], 0, 0)), spec], out_specs=spec),
        out_shape=jax.ShapeDtypeStruct(b.shape, b.dtype),
        compiler_params=pltpu.CompilerParams(dimension_semantics=("arbitrary",), vmem_limit_bytes=VMEM_LIMIT),
    )(core, a, b)


ADAMW_BLOCK_BYTES = 4 * 1024 * 1024


def _sum_adamw(parts, w, m, v, name):
    n_parts, r, c = parts.shape
    tr = r
    while n_parts * tr * c * parts.dtype.itemsize > ADAMW_BLOCK_BYTES and tr % (4 * SUBLANES) == 0:
        tr //= 2

    def body(p_ref, w_ref, m_ref, v_ref, g_ref, d_ref, nm_ref, nv_ref):
        g = p_ref[0].astype(F32)
        for s in range(1, n_parts):
            g = g + p_ref[s].astype(F32)
        nm = ADAM_B1 * m_ref[...] + (1.0 - ADAM_B1) * g
        nv = ADAM_B2 * v_ref[...] + (1.0 - ADAM_B2) * (g * g)
        m_hat = nm / (1.0 - ADAM_B1 ** ADAM_STEP)
        v_hat = nv / (1.0 - ADAM_B2 ** ADAM_STEP)
        g_ref[...] = g
        d_ref[...] = -ADAM_LR * (m_hat / (jnp.sqrt(v_hat) + ADAM_EPS) + ADAM_WD * w_ref[...])
        nm_ref[...] = nm
        nv_ref[...] = nv

    row = pl.BlockSpec((tr, c), lambda i: (i, 0))
    return _call(body, name=name, grid=(r // tr,),
                 in_specs=[pl.BlockSpec((n_parts, tr, c), lambda i: (0, i, 0)), row, row, row],
                 out_specs=[row, row, row, row], out_shape=[jax.ShapeDtypeStruct((r, c), F32)] * 4,
                 operands=[parts, w, m, v])


MATRICES = ("w_in", "w_out", "ffn_up_w", "ffn_down_w")
CONVS = ("lru_conv_w", "ffn_conv_w")
REPLICATED = ("norm1_gain", "lru_conv_b", "lru_gate_a_w", "lru_gate_a_b", "lru_gate_x_w", "lru_gate_x_b", "lru_lambda",
              "lru_norm_gain", "ret_norm_gain", "norm2_gain", "ffn_conv_b", "final_norm_gain")
WEIGHTS = ("norm1_gain", "w_in", "lru_conv_w", "lru_conv_b", "lru_gate_a_w", "lru_gate_a_b", "lru_gate_x_w",
           "lru_gate_x_b", "lru_lambda", "lru_norm_gain", "ret_norm_gain", "w_out", "norm2_gain", "ffn_up_w",
           "ffn_conv_w", "ffn_conv_b", "ffn_down_w", "final_norm_gain")


def _rows(a, pad_to):
    a = a.reshape(-1, LANES)
    pad = (-a.shape[0]) % pad_to
    return jnp.pad(a, ((0, pad), (0, 0))) if pad else a


def _pack(arrays, pad_to):
    rows, layout, at = [], [], 0
    for a in arrays:
        r = _rows(a, pad_to)
        layout.append((at, a.size // LANES, a.shape))
        rows.append(r)
        at += r.shape[0]
    return jnp.concatenate(rows, axis=0), layout


def _unpack(packed, layout):
    lead = packed.shape[:-2]
    return [packed[..., at:at + n, :].reshape(lead + shape) for at, n, shape in layout]


def _conv_rows(lru, ffn, dtype, pad_to):
    lead = lru.shape[:-2]
    flat = jnp.concatenate([lru.reshape(lead + (-1,)), ffn.reshape(lead + (-1,))], axis=-1).astype(dtype)
    rows = flat.shape[-1] // LANES
    pad = (-rows) % pad_to
    return jnp.pad(flat.reshape(lead + (rows, LANES)), [(0, 0)] * len(lead) + [(0, pad), (0, 0)])


def _column_blocks(full):
    r, c = full.shape
    return full.reshape(r, N_DEV, c // N_DEV).transpose(1, 0, 2)


def _block_diag(w):
    nh, d, _ = w.shape
    eye = jnp.eye(nh, dtype=w.dtype)
    return (w[:, :, None, :] * eye[:, None, :, None]).reshape(nh * d, nh * d)


def _diag_blocks(dense, nh):
    d = dense.shape[0] // nh
    blocks = dense.reshape(nh, d, nh, d)
    return jnp.stack([blocks[h, :, h, :] for h in range(nh)], axis=0)


def kernel(x, norm1_gain, w_in, lru_conv_w, lru_conv_b, lru_gate_a_w, lru_gate_a_b, lru_gate_x_w, lru_gate_x_b, lru_lambda, lru_norm_gain, ret_norm_gain, w_out, norm2_gain, ffn_up_w, ffn_conv_w, ffn_conv_b, ffn_down_w, final_norm_gain, loss_target, m_norm1_gain, m_w_in, m_lru_conv_w, m_lru_conv_b, m_lru_gate_a_w, m_lru_gate_a_b, m_lru_gate_x_w, m_lru_gate_x_b, m_lru_lambda, m_lru_norm_gain, m_ret_norm_gain, m_w_out, m_norm2_gain, m_ffn_up_w, m_ffn_conv_w, m_ffn_conv_b, m_ffn_down_w, m_final_norm_gain, v_norm1_gain, v_w_in, v_lru_conv_w, v_lru_conv_b, v_lru_gate_a_w, v_lru_gate_a_b, v_lru_gate_x_w, v_lru_gate_x_b, v_lru_lambda, v_lru_norm_gain, v_ret_norm_gain, v_w_out, v_norm2_gain, v_ffn_up_w, v_ffn_conv_w, v_ffn_conv_b, v_ffn_down_w, v_final_norm_gain):
    args = dict(locals())
    given = {n: args[n] for n in WEIGHTS}
    out_shape = {n: given[n].shape for n in WEIGHTS}

    def plain(a):
        return a.reshape(1, -1) if a.ndim <= 2 else a[0]

    w = {n: plain(given[n]) for n in WEIGHTS}
    mom_m = {n: plain(args["m_" + n]) for n in WEIGHTS}
    mom_v = {n: plain(args["v_" + n]) for n in WEIGHTS}
    x2, target = x[0], loss_target[0]
    t = x2.shape[0]
    core = lax.axis_index("c").astype(jnp.int32).reshape(1)
    res = {}

    conv_pad = _conv_rows(w["lru_conv_w"], w["ffn_conv_w"], F32, SUBLANES)
    first = _gather_first([w["w_in"].astype(MXU_DTYPE), conv_pad])
    w_in_blocks, conv_all = _run_comms([first, _gather_second(first.out_shape)], "w_in_all_gather")
    n_lru = w["lru_conv_w"].size
    conv_flat = conv_all.reshape(N_DEV, -1)
    lru_cw = conv_flat[:, :n_lru].reshape((N_DEV,) + w["lru_conv_w"].shape).transpose(1, 0, 2).reshape(LRU_CONV, D_LRU)
    ffn_cw = conv_flat[:, n_lru:n_lru + w["ffn_conv_w"].size].reshape((N_DEV,) + w["ffn_conv_w"].shape)
    ffn_cw = ffn_cw.transpose(1, 0, 2).reshape(FFN_CONV, 2 * D_FF)

    cos2, sin_signed = _rope_tables(t)
    wa = _block_diag(w["lru_gate_a_w"]).astype(MXU_DTYPE)
    wx = _block_diag(w["lru_gate_x_w"]).astype(MXU_DTYPE)
    gf = w["final_norm_gain"]

    up_first = _gather_first([w["ffn_up_w"].astype(MXU_DTYPE)])
    (u1, proj), (up_part,) = _inproj_fwd(x2, w["norm1_gain"], w_in_blocks, up_first)
    rest_first = _gather_first([w["w_out"].astype(MXU_DTYPE), w["ffn_down_w"].astype(MXU_DTYPE)])

    (xc, h_lru, y_lru), lru_side = _lru_fwd(proj, lru_cw, w["lru_conv_b"], wa, w["lru_gate_a_b"], wx, w["lru_gate_x_b"],
                                            w["lru_lambda"], w["lru_norm_gain"],
                                            _both(rest_first, _gather_second([up_part])))
    w_out_part, down_part, up_blocks = lru_side
    (o_ret, y_ret, states), (w_out_blocks, down_blocks) = _ret_fwd(proj, cos2, sin_signed, w["ret_norm_gain"],
                                                                 _gather_second([w_out_part, down_part]))
    w_out_full = w_out_blocks.reshape(D_MODEL, D_MODEL)
    w_down_full = down_blocks.reshape(D_FF, D_MODEL)

    h1, u2 = _outproj_fwd(x2, y_lru, y_ret, w_out_full, w["norm2_gain"])
    up_a, up_v, conv_a, conv_v, act, dh2, dh2_b, dgf, loss = _ffn_fwd(u2, up_blocks, ffn_cw, w["ffn_conv_b"], w_down_full,
                                                                      h1, gf, target)
    loss = lax.psum(loss[0, 0], ("x", "y", "c"))

    def to_owner_chips(blocks, names, tag):
        theirs = _run_comms([_pair_exchange(blocks)], "grads_pair_exchange_" + tag)
        return [_pair_sum(core, a, b, "grads_pair_sum_" + n) for n, a, b in zip(names, blocks, theirs)]

    def adamw(name, parts):
        res[name] = _sum_adamw(parts, w[name], mom_m[name], mom_v[name], "adamw_" + name)

    g = {"final_norm_gain": dgf[0]}
    g_down = _mm_tn(act, dh2_b, "ffn_down_wgrad").reshape(N_DEV, D_FF // N_DEV, D_MODEL)
    down_sums = to_owner_chips([g_down], ["ffn_down_w"], "down")
    (dup_a, dup_v, acc_a, acc_v, dh1, dh1_b, dg2), (down_parts,) = _ffn_bwd(
        dh2, dh2_b, w_down_full, up_a, up_v, conv_a, conv_v, ffn_cw, up_blocks, h1, w["norm2_gain"],
        _chip_exchange(down_sums))
    adamw("ffn_down_w", down_parts)
    per_col = lambda a: a[:, ::SUBLANES].transpose(1, 0, 2).reshape(FFN_CONV + 1, D_FF)
    acc = jnp.concatenate([per_col(acc_a), per_col(acc_v)], axis=1)
    g_ffn_cw, g["ffn_conv_b"] = acc[:FFN_CONV], acc[FFN_CONV:]
    g["norm2_gain"] = dg2[:1]
    g_up = jnp.concatenate([_mm_tn(u2, dup_a, "ffn_up_wgrad_a", blocks=N_DEV // 2),
                            _mm_tn(u2, dup_v, "ffn_up_wgrad_v", blocks=N_DEV // 2)], axis=0)
    g_out = jnp.concatenate([_mm_tn(y_lru, dh1_b, "w_out_wgrad_lru"), _mm_tn(y_ret, dh1_b, "w_out_wgrad_ret")], axis=0)
    mid_sums = to_owner_chips([g_up, g_out.reshape(N_DEV, D_MODEL // N_DEV, D_MODEL)], ["ffn_up_w", "w_out"], "mid")
    (dmixed,) = _mm_nt(dh1_b, w_out_full, "outproj_bwd", F32)
    (dproj, dgain_ret), (up_parts, out_parts) = _ret_bwd(proj, cos2, sin_signed, w["ret_norm_gain"], o_ret, states, dmixed,
                                                         _chip_exchange(mid_sums))
    adamw("ffn_up_w", up_parts)
    adamw("w_out", out_parts)
    g["ret_norm_gain"] = dgain_ret[:1]
    dproj, lru_acc, dwa, dwx = _lru_bwd(proj, xc, h_lru, dmixed, dproj, lru_cw, wa, w["lru_gate_a_b"], wx,
                                        w["lru_gate_x_b"], w["lru_lambda"], w["lru_norm_gain"])
    lru_acc = lru_acc[::SUBLANES]
    g_lru_cw = lru_acc[:LRU_CONV]
    for name in ("conv_b", "gate_a_b", "gate_x_b", "lambda", "norm_gain"):
        g["lru_" + name] = lru_acc[LRU_ACC[name]:LRU_ACC[name] + 1]
    g["lru_gate_a_w"] = _diag_blocks(dwa, LRU_HEADS)
    g["lru_gate_x_w"] = _diag_blocks(dwx, LRU_HEADS)
    g_in = _mm_tn(u1, dproj, "w_in_wgrad", blocks=N_DEV)
    g_conv = _conv_rows(_column_blocks(g_lru_cw), _column_blocks(g_ffn_cw), GRAD_DTYPE, 2 * SUBLANES)
    in_sums = to_owner_chips([g_in, g_conv], ["w_in", "conv"], "in")
    half = t // min(256, t) // 2
    (gx_half, gxb_half, dg1_lo), (in_parts, conv_parts) = _norm_bwd_matmul(
        [dproj], w_in_blocks, x2, w["norm1_gain"], dh1, "inproj_bwd_lo", _chip_exchange(in_sums), tiles=(0, half))
    grad_x, _, dg1_hi = _norm_bwd_matmul([dproj], w_in_blocks, x2, w["norm1_gain"], dh1, "inproj_bwd_hi",
                                         tiles=(half, t // min(256, t) - half), into=(gx_half, gxb_half))
    adamw("w_in", in_parts)
    g["norm1_gain"] = dg1_lo[:1] + dg1_hi[:1]

    pad16 = lambda d: _conv_rows(d["lru_conv_w"], d["ffn_conv_w"], F32, 2 * SUBLANES)
    conv_res = _sum_adamw(conv_parts, pad16(w), pad16(mom_m), pad16(mom_v), "adamw_conv")
    for n, lo, hi in (("lru_conv_w", 0, n_lru), ("ffn_conv_w", n_lru, n_lru + w["ffn_conv_w"].size)):
        res[n] = [r.reshape(-1)[lo:hi].reshape(w[n].shape) for r in conv_res]
    rep_packed, rep_layout = _pack([g[n] for n in REPLICATED], SUBLANES)
    rep_first = _gather_first([rep_packed])
    (rep_parts,) = _run_comms([rep_first, _gather_second(rep_first.out_shape)], "small_grads_all_gather")
    rep_res = _sum_adamw(rep_parts, *[_pack([d[n] for n in REPLICATED], SUBLANES)[0] for d in (w, mom_m, mom_v)],
                         "adamw_replicated")
    for k in range(4):
        for n, a in zip(REPLICATED, _unpack(rep_res[k], rep_layout)):
            res.setdefault(n, [None] * 4)[k] = a

    outs = [loss, grad_x[None]]
    for k in range(4):
        outs += [res[n][k].reshape(out_shape[n]) for n in WEIGHTS]
    return tuple(outs)
```

```python
import math

import numpy as np
import jax
import jax.numpy as jnp
from jax import lax
from jax.experimental import pallas as pl
from jax.experimental.pallas import tpu as pltpu

F32 = jnp.float32
BF16 = jnp.bfloat16
MXU_DTYPE = jnp.bfloat16
GRAD_DTYPE = jnp.bfloat16

N_DEV = 8
N_CHIPS = 4
D_MODEL = 1024
D_LRU = 512
LRU_HEADS = 8
LRU_CONV = 4
LRU_C = 8.0
D_RET = 512
RET_HEADS = 4
RET_HEAD_DIM = 128
RET_CHUNK = 128
ROPE_BASE = 10000.0
D_IN = 3072
D_FF = 3072
FFN_CONV = 3
NORM_EPS = 1e-6

ADAM_LR = 0.001
ADAM_B1 = 0.9
ADAM_B2 = 0.999
ADAM_EPS = 1e-08
ADAM_WD = 0.01
ADAM_STEP = 10

SUBLANES = 8
LANES = 128
VMEM_LIMIT = 48 * 1024 * 1024

MESH = pl.DeviceIdType.MESH
ANY = pl.BlockSpec(memory_space=pl.ANY)


def _dot(a, b):
    return jnp.dot(a.astype(MXU_DTYPE), b.astype(MXU_DTYPE), preferred_element_type=F32)


def _dot_nt(a, b):
    return lax.dot_general(a.astype(MXU_DTYPE), b.astype(MXU_DTYPE), (((1,), (1,)), ((), ())),
                           preferred_element_type=F32)


def _dot_tn(a, b):
    return lax.dot_general(a.astype(MXU_DTYPE), b.astype(MXU_DTYPE), (((0,), (0,)), ((), ())),
                           preferred_element_type=F32)


def _sigmoid(x):
    return 0.5 + 0.5 * jnp.tanh(0.5 * x)


_GELU_C = math.sqrt(2.0 / math.pi)


def _gelu_parts(x):
    x2 = x * x
    t = jnp.tanh(_GELU_C * (x + 0.044715 * (x2 * x)))
    cdf = 0.5 * (1.0 + t)
    g = x * cdf
    dg = cdf + 0.5 * x * (1.0 - t * t) * (_GELU_C * (1.0 + 3.0 * 0.044715 * x2))
    return g, dg


def _gelu(x):
    t = jnp.tanh(_GELU_C * (x + 0.044715 * (x * x * x)))
    return x * (0.5 * (1.0 + t))


def _softplus(x):
    return jnp.maximum(x, 0.0) + jnp.log1p(jnp.exp(-jnp.abs(x)))


def _bcast_row(x, r, rows=SUBLANES):
    return jnp.broadcast_to(x[r:r + 1, :], (rows, x.shape[1]))


def _colsum8(x):
    return jnp.broadcast_to(jnp.sum(x, axis=0, keepdims=True), (SUBLANES, x.shape[1]))


def _shift_down(prev8, tile, s):
    if s == 0:
        return tile
    ext = jnp.concatenate([prev8, tile], axis=0)
    return pltpu.roll(ext, s, 0)[SUBLANES:, :]


def _shift_up(tile, next8, s):
    if s == 0:
        return tile
    ext = jnp.concatenate([tile, next8], axis=0)
    return pltpu.roll(ext, SUBLANES - s, 0)[SUBLANES:, :]


def _group_scan(a, b, reverse):
    n = a.shape[0]
    row = lax.broadcasted_iota(jnp.int32, a.shape, 0) & (SUBLANES - 1)
    for s in (1, 2, 4):
        shift = (n - s) if reverse else s
        a_sh = pltpu.roll(a, shift, 0)
        b_sh = pltpu.roll(b, shift, 0)
        m = (row <= SUBLANES - 1 - s) if reverse else (row >= s)
        b = jnp.where(m, a * b_sh + b, b)
        a = jnp.where(m, a * a_sh, a)
    return a, b


def _carry_scan(a_ref, b_ref, out_ref, carry0, reverse):
    n_groups = a_ref.shape[0] // SUBLANES

    def body(i, carry):
        g = (n_groups - 1 - i) if reverse else i
        r0 = pl.multiple_of(g * SUBLANES, SUBLANES)
        hg = a_ref[pl.ds(r0, SUBLANES), :] * carry + b_ref[pl.ds(r0, SUBLANES), :]
        out_ref[pl.ds(r0, SUBLANES), :] = hg
        return _bcast_row(hg, 0 if reverse else SUBLANES - 1)

    return lax.fori_loop(0, n_groups, body, carry0)


def _rms_fwd(h, gain):
    rstd = lax.rsqrt(jnp.mean(h * h, axis=-1, keepdims=True) + NORM_EPS)
    n = h * rstd
    return n, rstd, n * gain


def _rms_bwd(dy, n, rstd, gain):
    dn = dy * gain
    dh = rstd * (dn - n * jnp.mean(dn * n, axis=-1, keepdims=True))
    return dh, _colsum8(dy * n)


def _halo_rows(dtype):
    return SUBLANES * (4 // jnp.dtype(dtype).itemsize)


def _halo_map(tile_rows, col, halo_rows=SUBLANES):
    per = tile_rows // halo_rows
    return lambda i: (jnp.maximum(i * per - 1, 0), col)


def _resident(shape):
    return pl.BlockSpec(shape, lambda *_: (0,) * len(shape), pipeline_mode=pl.Buffered(1))


def _place():
    x, y, c = lax.axis_index("x"), lax.axis_index("y"), lax.axis_index("c")
    chips = [(1 - x, y), (x, 1 - y), (1 - x, 1 - y)]
    return x, y, c, chips


def _dev(x, y, c):
    return 4 * x + 2 * y + c


def _remote(src, dst, send_sem, recv_sem, to):
    return pltpu.make_async_remote_copy(src_ref=src, dst_ref=dst, send_sem=send_sem, recv_sem=recv_sem,
                                        device_id=to, device_id_type=MESH)


class _Comm:
    def __init__(self, operands, out_shape, sems, descs, aliases=()):
        self.operands, self.out_shape, self.sems, self.descs, self.aliases = operands, out_shape, sems, descs, aliases

    def start(self, ins, outs, sems):
        local, sends, _ = self.descs(ins, outs, sems)
        for cp in sends + local:
            cp.start()

    def wait(self, ins, outs, sems):
        local, sends, recvs = self.descs(ins, outs, sems)
        for cp in recvs:
            cp.wait_recv()
        for cp in sends:
            cp.wait_send()
        for cp in local:
            cp.wait()


def _gather_first(shards):
    n = len(shards)

    def descs(ins, outs, sems):
        send, recv, loc = sems
        x, y, c, chips = _place()
        me = _dev(x, y, c)
        targets = [(x, y, 1 - c)] + [(*chip, c) for chip in chips]
        local, sends, recvs = [], [], []
        for t in range(n):
            local.append(pltpu.make_async_copy(ins[t], outs[t].at[me], loc.at[t]))
            for k, to in enumerate(targets):
                i = 4 * t + k
                sends.append(_remote(ins[t], outs[t].at[me], send.at[i], recv.at[i], to))
                recvs.append(_remote(ins[t], outs[t].at[_dev(*to)], send.at[i], recv.at[i], to))
        return local, sends, recvs

    return _Comm(list(shards), [jax.ShapeDtypeStruct((N_DEV,) + s.shape, s.dtype) for s in shards],
                 [pltpu.SemaphoreType.DMA((4 * n,)), pltpu.SemaphoreType.DMA((4 * n,)), pltpu.SemaphoreType.DMA((n,))],
                 descs)


def _gather_second(gathered):
    n = len(gathered)

    def descs(ins, outs, sems):
        send, recv = sems
        x, y, c, chips = _place()
        sends, recvs = [], []
        for t in range(n):
            for j, chip in enumerate(chips):
                i = 3 * t + j
                have, get = _dev(*chip, c), _dev(*chip, 1 - c)
                sends.append(_remote(outs[t].at[have], outs[t].at[have], send.at[i], recv.at[i], (x, y, 1 - c)))
                recvs.append(_remote(outs[t].at[have], outs[t].at[get], send.at[i], recv.at[i], (x, y, 1 - c)))
        return [], sends, recvs

    return _Comm(list(gathered), [jax.ShapeDtypeStruct(g.shape, g.dtype) for g in gathered],
                 [pltpu.SemaphoreType.DMA((3 * n,)), pltpu.SemaphoreType.DMA((3 * n,))], descs,
                 aliases=[(t, t) for t in range(n)])


def _pair_exchange(blocks):
    n = len(blocks)

    def descs(ins, outs, sems):
        send, recv = sems
        x, y, c, _ = _place()
        sends, recvs = [], []
        for t in range(n):
            for q in range(N_CHIPS):
                i = N_CHIPS * t + q
                cp = _remote(ins[t].at[2 * q + 1 - c], outs[t].at[q], send.at[i], recv.at[i], (x, y, 1 - c))
                sends.append(cp)
                recvs.append(cp)
        return [], sends, recvs

    return _Comm(list(blocks), [jax.ShapeDtypeStruct((N_CHIPS,) + b.shape[1:], b.dtype) for b in blocks],
                 [pltpu.SemaphoreType.DMA((N_CHIPS * n,)), pltpu.SemaphoreType.DMA((N_CHIPS * n,))], descs)


def _chip_exchange(blocks):
    n = len(blocks)

    def descs(ins, outs, sems):
        send, recv, loc = sems
        x, y, c, chips = _place()
        me = 2 * x + y
        local, sends, recvs = [], [], []
        for t in range(n):
            local.append(pltpu.make_async_copy(ins[t].at[me], outs[t].at[me], loc.at[t]))
            for j, (px, py) in enumerate(chips):
                i = 3 * t + j
                q = 2 * px + py
                sends.append(_remote(ins[t].at[q], outs[t].at[me], send.at[i], recv.at[i], (px, py, c)))
                recvs.append(_remote(ins[t].at[q], outs[t].at[q], send.at[i], recv.at[i], (px, py, c)))
        return local, sends, recvs

    return _Comm(list(blocks), [jax.ShapeDtypeStruct(b.shape, b.dtype) for b in blocks],
                 [pltpu.SemaphoreType.DMA((3 * n,)), pltpu.SemaphoreType.DMA((3 * n,)), pltpu.SemaphoreType.DMA((n,))],
                 descs)


def _both(a, b):
    na, oa, sa = len(a.operands), len(a.out_shape), len(a.sems)

    def descs(ins, outs, sems):
        local_a, sends_a, recvs_a = a.descs(ins[:na], outs[:oa], sems[:sa])
        local_b, sends_b, recvs_b = b.descs(ins[na:], outs[oa:], sems[sa:])
        return local_a + local_b, sends_a + sends_b, recvs_a + recvs_b

    return _Comm(a.operands + b.operands, a.out_shape + b.out_shape, a.sems + b.sems, descs,
                 aliases=list(a.aliases) + [(na + i, oa + o) for i, o in b.aliases])


def _run_comms(comms, name):
    first = comms[0]
    n_in, n_out = len(first.operands), len(first.out_shape)

    def body(*refs):
        ins, outs, sems = refs[:n_in], refs[n_in:n_in + n_out], list(refs[n_in + n_out:])
        for k, comm in enumerate(comms):
            mine = [sems.pop(0) for _ in comm.sems]
            comm.start(ins if k == 0 else outs, outs, mine)
            comm.wait(ins if k == 0 else outs, outs, mine)

    outs = pl.pallas_call(
        body, name=name, out_shape=first.out_shape, in_specs=[ANY] * n_in, out_specs=[ANY] * n_out,
        scratch_shapes=[s for comm in comms for s in comm.sems],
    )(*first.operands)
    return list(outs)


def _call(body, *, name, grid, in_specs, out_specs, out_shape, operands, scratch_shapes=(), comm=None, aliases=None):
    sem = ("arbitrary",) * len(grid)
    params = pltpu.CompilerParams(dimension_semantics=sem, vmem_limit_bytes=VMEM_LIMIT)
    aliases = dict(aliases or {})
    if comm is None:
        return pl.pallas_call(body, name=name, grid=grid, in_specs=in_specs, out_specs=out_specs, out_shape=out_shape,
                              scratch_shapes=list(scratch_shapes), input_output_aliases=aliases,
                              compiler_params=params)(*operands)
    n_in, n_out, n_scr = len(in_specs), len(out_specs), len(scratch_shapes)
    c_in, c_out = len(comm.operands), len(comm.out_shape)

    def wrapped(*refs):
        refs = list(refs)
        ins, refs = refs[:n_in], refs[n_in:]
        cins, refs = refs[:c_in], refs[c_in:]
        outs, refs = refs[:n_out], refs[n_out:]
        couts, refs = refs[:c_out], refs[c_out:]
        scr, csems = refs[:n_scr], refs[n_scr:]
        first = last = None
        for axis, size in enumerate(grid):
            at_first, at_last = pl.program_id(axis) == 0, pl.program_id(axis) == size - 1
            first = at_first if first is None else first & at_first
            last = at_last if last is None else last & at_last

        @pl.when(first)
        def _():
            comm.start(cins, couts, csems)

        body(*ins, *outs, *scr)

        @pl.when(last)
        def _():
            comm.wait(cins, couts, csems)

    res = pl.pallas_call(
        wrapped, name=name, grid=grid, in_specs=list(in_specs) + [ANY] * c_in, out_specs=list(out_specs) + [ANY] * c_out,
        out_shape=list(out_shape) + list(comm.out_shape), scratch_shapes=list(scratch_shapes) + list(comm.sems),
        input_output_aliases={**aliases, **{n_in + i: n_out + o for i, o in comm.aliases}}, compiler_params=params,
    )(*operands, *comm.operands)
    return list(res[:n_out]), list(res[n_out:])


def _mm_nt(a, b, name, out_dtype, comm=None, tm=512, tn=512):
    m, k = a.shape
    n = b.shape[0]
    tm, tn = min(tm, m), min(tn, n)

    def body(a_ref, b_ref, o_ref):
        o_ref[...] = _dot_nt(a_ref[...], b_ref[...]).astype(o_ref.dtype)

    return _call(body, name=name, grid=(n // tn, m // tm), comm=comm,
                 in_specs=[pl.BlockSpec((tm, k), lambda j, i: (i, 0)), pl.BlockSpec((tn, k), lambda j, i: (j, 0))],
                 out_specs=[pl.BlockSpec((tm, tn), lambda j, i: (i, j))],
                 out_shape=[jax.ShapeDtypeStruct((m, n), out_dtype)], operands=[a, b])


def _mm_tn(a, b, name, blocks=1, tk=2048):
    t, m = a.shape
    n = b.shape[1]
    tk = min(tk, t)
    nk = t // tk
    cb = n // blocks
    per = max(1, 768 // cb) if blocks > 1 else 1
    tn = per * cb if blocks > 1 else min(1024, n)
    tm = min(1024, m)
    assert blocks == 1 or tm == m

    def body(a_ref, b_ref, o_ref, acc):
        k = pl.program_id(2)

        @pl.when(k == 0)
        def _():
            acc[...] = jnp.zeros_like(acc)
        acc[...] += _dot_tn(a_ref[...], b_ref[...])

        @pl.when(k == nk - 1)
        def _():
            if blocks == 1:
                o_ref[...] = acc[...].astype(o_ref.dtype)
            else:
                for s in range(per):
                    o_ref[s] = acc[:, s * cb:(s + 1) * cb].astype(o_ref.dtype)

    if blocks == 1:
        out_spec = pl.BlockSpec((tm, tn), lambda i, j, k: (i, j))
        out_shape = jax.ShapeDtypeStruct((m, n), GRAD_DTYPE)
    else:
        out_spec = pl.BlockSpec((per, m, cb), lambda i, j, k: (j, 0, 0))
        out_shape = jax.ShapeDtypeStruct((blocks, m, cb), GRAD_DTYPE)
    return _call(body, name=name, grid=(m // tm, n // tn, nk),
                 in_specs=[pl.BlockSpec((tk, tm), lambda i, j, k: (k, i)), pl.BlockSpec((tk, tn), lambda i, j, k: (k, j))],
                 out_specs=[out_spec], out_shape=[out_shape], operands=[a, b],
                 scratch_shapes=[pltpu.VMEM((tm, tn), F32)])[0]


def _inproj_fwd(x, g1, w_blocks, comm):
    t = x.shape[0]
    tm = min(512, t)
    nb, _, cb = w_blocks.shape

    def body(x_ref, g_ref, w_ref, u_ref, p_ref):
        _, _, u = _rms_fwd(x_ref[...], g_ref[...])
        u = u.astype(MXU_DTYPE)
        u_ref[...] = u
        for d in range(nb):
            p_ref[:, d * cb:(d + 1) * cb] = _dot(u, w_ref[d])

    return _call(body, name="inproj_fwd", grid=(t // tm,), comm=comm,
                 in_specs=[pl.BlockSpec((tm, D_MODEL), lambda i: (i, 0)), pl.BlockSpec((1, D_MODEL), lambda i: (0, 0)),
                           _resident(w_blocks.shape)],
                 out_specs=[pl.BlockSpec((tm, D_MODEL), lambda i: (i, 0)), pl.BlockSpec((tm, D_IN), lambda i: (i, 0))],
                 out_shape=[jax.ShapeDtypeStruct((t, D_MODEL), MXU_DTYPE), jax.ShapeDtypeStruct((t, D_IN), F32)],
                 operands=[x, g1, w_blocks])


def _lru_gates(xc, wa, ba, wx, bx, sp):
    r = _sigmoid(_dot(xc, wa) + ba)
    ig = _sigmoid(_dot(xc, wx) + bx)
    log_a = (-LRU_C) * r * sp
    a = jnp.exp(log_a)
    m = jnp.sqrt(-jnp.tanh(log_a) * (a * a + 1.0))
    return r, ig, a, m


def _lru_fwd(proj, conv_w, conv_b, wa, ba, wx, bx, lam, gain, comm):
    t = proj.shape[0]
    tm = min(256, t)
    c = D_LRU

    def body(x_ref, xh_ref, g_ref, cw_ref, cb_ref, wa_ref, ba_ref, wx_ref, bx_ref, lam_ref, gain_ref,
             xc_ref, h_ref, y_ref, a_scr, b_scr, carry):
        i = pl.program_id(0)

        @pl.when(i == 0)
        def _():
            carry[...] = jnp.zeros_like(carry)

        x = x_ref[...]
        prev = jnp.where(i == 0, 0.0, xh_ref[...])
        cw = cw_ref[...]
        xc = cb_ref[...] + cw[LRU_CONV - 1:LRU_CONV, :] * x
        for k in range(LRU_CONV - 1):
            xc = xc + cw[k:k + 1, :] * _shift_down(prev, x, LRU_CONV - 1 - k)
        xc_ref[...] = xc
        sp = _softplus(-lam_ref[...])
        _, ig, a, m = _lru_gates(xc, wa_ref[...], ba_ref[...], wx_ref[...], bx_ref[...], sp)
        ga, gb = _group_scan(a, m * (ig * xc), reverse=False)
        a_scr[...] = ga
        b_scr[...] = gb
        carry[...] = _carry_scan(a_scr, b_scr, h_ref, carry[...], reverse=False)
        z = h_ref[...] * _gelu(g_ref[...])
        _, _, y = _rms_fwd(z, gain_ref[...])
        y_ref[...] = y.astype(y_ref.dtype)

    row = lambda i: (i, 0)
    full = lambda i: (0, 0)
    vec = pl.BlockSpec((1, c), full)
    return _call(body, name="lru_fwd", grid=(t // tm,), comm=comm,
                 in_specs=[pl.BlockSpec((tm, c), row), pl.BlockSpec((SUBLANES, c), _halo_map(tm, 0)),
                           pl.BlockSpec((tm, c), lambda i: (i, 1)),
                           pl.BlockSpec((LRU_CONV, c), full), vec, pl.BlockSpec((c, c), full), vec,
                           pl.BlockSpec((c, c), full), vec, vec, vec],
                 out_specs=[pl.BlockSpec((tm, c), row), pl.BlockSpec((tm, c), row), pl.BlockSpec((tm, c), row)],
                 out_shape=[jax.ShapeDtypeStruct((t, c), F32), jax.ShapeDtypeStruct((t, c), F32),
                            jax.ShapeDtypeStruct((t, c), MXU_DTYPE)],
                 scratch_shapes=[pltpu.VMEM((tm, c), F32), pltpu.VMEM((tm, c), F32), pltpu.VMEM((SUBLANES, c), F32)],
                 operands=[proj, proj, proj, conv_w, conv_b, wa, ba, wx, bx, lam, gain])


def _ret_consts():
    c = RET_CHUNK
    log_g = jnp.log1p(-jnp.exp2(-5.0 - jnp.arange(RET_HEADS, dtype=F32)))
    idx = jnp.arange(c, dtype=F32)
    diff = idx[:, None] - idx[None, :]
    decay = jnp.where(diff[None] >= 0, jnp.exp(jnp.maximum(diff, 0.0)[None] * log_g[:, None, None]), 0.0)
    zeta = jnp.exp((c - 1 - idx)[None, :] * log_g[:, None])
    xi = jnp.exp((idx + 1.0)[None, :] * log_g[:, None])
    spread = lambda v: jnp.repeat(v.T, RET_HEAD_DIM, axis=1)
    log_g_np = np.log1p(-np.exp2(-5.0 - np.arange(RET_HEADS, dtype=np.float32))).astype(np.float32)
    g_chunk = [float(np.exp(np.float32(c) * lg)) for lg in log_g_np]
    return decay, spread(xi), spread(zeta), g_chunk


def _rope_tables(t):
    pos = np.arange(t, dtype=np.float32)
    inv_freq = np.float32(ROPE_BASE) ** (-np.arange(0, RET_HEAD_DIM, 2, dtype=np.float32) / np.float32(RET_HEAD_DIM))
    ang = (pos[:, None] * inv_freq.astype(np.float32)[None, :]).astype(np.float32).astype(np.float64)
    cos, sin = np.cos(ang).astype(np.float32), np.sin(ang).astype(np.float32)
    return jnp.asarray(np.concatenate([cos, cos], axis=-1)), jnp.asarray(np.concatenate([-sin, sin], axis=-1))


def _rope(x, cos2, sin_signed):
    return x * cos2 + pltpu.roll(x, RET_HEAD_DIM // 2, 1) * sin_signed


def _rope_bwd(d, cos2, sin_signed):
    return d * cos2 + pltpu.roll(d * sin_signed, RET_HEAD_DIM // 2, 1)


RET_SCALE = RET_HEAD_DIM ** -0.5


RET_CHUNKS_PER_STEP = 2


def _ret_fwd(proj, cos2, sin_signed, gain, comm):
    t = proj.shape[0]
    c, d, nh = RET_CHUNK, RET_HEAD_DIM, RET_HEADS
    n_chunks = t // c
    per = RET_CHUNKS_PER_STEP if n_chunks % RET_CHUNKS_PER_STEP == 0 else 1
    rows = per * c
    decay, xi, zeta, g_chunk = _ret_consts()

    def body(qk_ref, vg_ref, cos_ref, sin_ref, dec_ref, xi_ref, zeta_ref, gain_ref, o_ref, y_ref, st_ref, state):
        @pl.when(pl.program_id(0) == 0)
        def _():
            state[...] = jnp.zeros_like(state)

        cur = [state[h] for h in range(nh)]
        for s in range(per):
            rs = slice(s * c, (s + 1) * c)
            cos2, sin_s = cos_ref[rs, :], sin_ref[rs, :]
            for h in range(nh):
                lo = h * d
                q = _rope(qk_ref[rs, lo:lo + d], cos2, sin_s)
                k = _rope(qk_ref[rs, D_RET + lo:D_RET + lo + d], cos2, sin_s) * RET_SCALE
                v = vg_ref[rs, lo:lo + d]
                g = vg_ref[rs, D_RET + lo:D_RET + lo + d]
                s_prev = cur[h]
                st_ref[s, h] = s_prev
                scores = _dot_nt(q, k) * dec_ref[h]
                o = _dot(scores, v) + _dot(q * xi_ref[:, lo:lo + d], s_prev)
                cur[h] = s_prev * g_chunk[h] + _dot_tn(k * zeta_ref[:, lo:lo + d], v)
                o_ref[rs, lo:lo + d] = o
                mu = jnp.mean(o, axis=-1, keepdims=True)
                oc = o - mu
                on = oc * lax.rsqrt(jnp.mean(oc * oc, axis=-1, keepdims=True) + NORM_EPS)
                y_ref[rs, lo:lo + d] = (on * gain_ref[:, lo:lo + d] * (g * _sigmoid(g))).astype(y_ref.dtype)
        for h in range(nh):
            state[h] = cur[h]

    full2 = lambda i: (0, 0)
    return _call(body, name="ret_fwd", grid=(n_chunks // per,), comm=comm,
                 in_specs=[pl.BlockSpec((rows, 2 * D_RET), lambda i: (i, 1)),
                           pl.BlockSpec((rows, 2 * D_RET), lambda i: (i, 2)),
                           pl.BlockSpec((rows, d), lambda i: (i, 0)), pl.BlockSpec((rows, d), lambda i: (i, 0)),
                           pl.BlockSpec((nh, c, c), lambda i: (0, 0, 0)), pl.BlockSpec((c, D_RET), full2),
                           pl.BlockSpec((c, D_RET), full2), pl.BlockSpec((1, D_RET), full2)],
                 out_specs=[pl.BlockSpec((rows, D_RET), lambda i: (i, 0)), pl.BlockSpec((rows, D_RET), lambda i: (i, 0)),
                            pl.BlockSpec((per, nh, d, d), lambda i: (i, 0, 0, 0))],
                 out_shape=[jax.ShapeDtypeStruct((t, D_RET), F32), jax.ShapeDtypeStruct((t, D_RET), MXU_DTYPE),
                            jax.ShapeDtypeStruct((n_chunks, nh, d, d), F32)],
                 scratch_shapes=[pltpu.VMEM((nh, d, d), F32)],
                 operands=[proj, proj, cos2, sin_signed, decay, xi, zeta, gain])


def _outproj_fwd(x, y_lru, y_ret, w_out, g2):
    t = x.shape[0]
    tm = min(512, t)

    def body(x_ref, yl_ref, yr_ref, w_ref, g_ref, h1_ref, u2_ref):
        h1 = x_ref[...] + _dot(yl_ref[...], w_ref[:D_LRU, :]) + _dot(yr_ref[...], w_ref[D_LRU:, :])
        h1_ref[...] = h1
        _, _, u = _rms_fwd(h1, g_ref[...])
        u2_ref[...] = u.astype(u2_ref.dtype)

    row = lambda i: (i, 0)
    return _call(body, name="outproj_fwd", grid=(t // tm,),
                 in_specs=[pl.BlockSpec((tm, D_MODEL), row), pl.BlockSpec((tm, D_LRU), row), pl.BlockSpec((tm, D_RET), row),
                           _resident((D_MODEL, D_MODEL)), pl.BlockSpec((1, D_MODEL), lambda i: (0, 0))],
                 out_specs=[pl.BlockSpec((tm, D_MODEL), row), pl.BlockSpec((tm, D_MODEL), row)],
                 out_shape=[jax.ShapeDtypeStruct((t, D_MODEL), F32), jax.ShapeDtypeStruct((t, D_MODEL), MXU_DTYPE)],
                 operands=[x, y_lru, y_ret, w_out, g2])


FFN_TN = 768
FFN_NJ = D_FF // FFN_TN
FFN_GROUP = 4


def _ffn_fwd(u2, w_blocks, conv_w, conv_b, w_down, h1, gf, target):
    t = u2.shape[0]
    tm = min(256, t)
    tn, nj, group = FFN_TN, FFN_NJ, FFN_GROUP
    ng, tw = nj // group, group * tn
    hb = _halo_rows(u2.dtype)
    assert w_blocks.shape == (2 * nj, D_MODEL, tn)

    def project(u_ext, w, col, up_ref, conv_ref, cw_ref, cb_ref, first):
        ext = _dot(u_ext, w)
        x = ext[hb:, :]
        up_ref[:, col] = x.astype(up_ref.dtype)
        prev = jnp.where(first, 0.0, ext[hb - SUBLANES:hb, :])
        cw = cw_ref[:, col]
        y = cb_ref[:, col] + cw[FFN_CONV - 1:FFN_CONV, :] * x
        for k in range(FFN_CONV - 1):
            y = y + cw[k:k + 1, :] * _shift_down(prev, x, FFN_CONV - 1 - k)
        conv_ref[:, col] = y.astype(conv_ref.dtype)
        return y

    def body(u_ref, uh_ref, w_ref, cwa_ref, cwv_ref, cba_ref, cbv_ref, wd_ref, h1_ref, gf_ref, tg_ref,
             upa_ref, upv_ref, ca_ref, cv_ref, act_ref, dh_ref, dhb_ref, dgf_ref, loss_ref, acc):
        i, jg = pl.program_id(0), pl.program_id(1)

        @pl.when((i == 0) & (jg == 0))
        def _():
            dgf_ref[...] = jnp.zeros_like(dgf_ref)
            loss_ref[...] = jnp.zeros_like(loss_ref)

        @pl.when(jg == 0)
        def _():
            acc[...] = jnp.zeros_like(acc)

        u_ext = jnp.concatenate([uh_ref[...], u_ref[...]], axis=0)
        down = None
        for jj in range(group):
            col = slice(jj * tn, (jj + 1) * tn)
            j = jg * group + jj
            a = project(u_ext, w_ref[j], col, upa_ref, ca_ref, cwa_ref, cba_ref, i == 0)
            v = project(u_ext, w_ref[nj + j], col, upv_ref, cv_ref, cwv_ref, cbv_ref, i == 0)
            act = (_gelu(a) * v).astype(act_ref.dtype)
            act_ref[:, col] = act
            part = _dot(act, wd_ref[pl.ds(pl.multiple_of(j * tn, tn), tn), :])
            down = part if down is None else down + part
        acc[...] += down

        @pl.when(jg == ng - 1)
        def _():
            n, rstd, y = _rms_fwd(h1_ref[...] + acc[...], gf_ref[...])
            err = y - tg_ref[...]
            loss_ref[...] += (0.5 / D_MODEL) * jnp.sum(err * err)
            dh, dgf = _rms_bwd(err * (1.0 / D_MODEL), n, rstd, gf_ref[...])
            dgf_ref[...] += dgf
            dh_ref[...] = dh
            dhb_ref[...] = dh.astype(dhb_ref.dtype)

    per = tm // hb
    row = lambda i, j: (i, 0)
    const = lambda i, j: (0, 0)
    tile = pl.BlockSpec((tm, tw), lambda i, j: (i, j))
    return _call(body, name="ffn_fwd", grid=(t // tm, ng),
                 in_specs=[pl.BlockSpec((tm, D_MODEL), row),
                           pl.BlockSpec((hb, D_MODEL), lambda i, j: (jnp.maximum(i * per - 1, 0), 0)),
                           _resident(w_blocks.shape),
                           pl.BlockSpec((FFN_CONV, tw), lambda i, j: (0, j)),
                           pl.BlockSpec((FFN_CONV, tw), lambda i, j: (0, j + ng)),
                           pl.BlockSpec((1, tw), lambda i, j: (0, j)), pl.BlockSpec((1, tw), lambda i, j: (0, j + ng)),
                           _resident((D_FF, D_MODEL)),
                           pl.BlockSpec((tm, D_MODEL), row), pl.BlockSpec((1, D_MODEL), const),
                           pl.BlockSpec((tm, D_MODEL), row)],
                 out_specs=[tile] * 5 + [pl.BlockSpec((tm, D_MODEL), row),
                            pl.BlockSpec((tm, D_MODEL), row), pl.BlockSpec((SUBLANES, D_MODEL), const),
                            pl.BlockSpec((SUBLANES, LANES), const)],
                 out_shape=[jax.ShapeDtypeStruct((t, D_FF), MXU_DTYPE)] * 5 + [
                            jax.ShapeDtypeStruct((t, D_MODEL), F32),
                            jax.ShapeDtypeStruct((t, D_MODEL), MXU_DTYPE), jax.ShapeDtypeStruct((SUBLANES, D_MODEL), F32),
                            jax.ShapeDtypeStruct((SUBLANES, LANES), F32)],
                 scratch_shapes=[pltpu.VMEM((tm, D_MODEL), F32)],
                 operands=[u2, u2, w_blocks, conv_w, conv_w, conv_b, conv_b, w_down, h1, gf, target])


FFN_ACC_ROWS = SUBLANES * (FFN_CONV + 1)


def _ffn_bwd(dh2, dh2_b, w_down, up_a, up_v, conv_a, conv_v, conv_w, w_up_blocks, h1, g2, comm):
    t = up_a.shape[0]
    tm = min(256, t)
    tn, nj, group = FFN_TN, FFN_NJ, FFN_GROUP
    ng, tw = nj // group, group * tn
    ni = t // tm
    assert w_up_blocks.shape == (2 * nj, D_MODEL, tn)

    def conv_bwd(dy, x, cw, acc_ref, carry_ref, dup_ref, col):
        nxt = carry_ref[...]
        carry_ref[...] = dy[:SUBLANES, :]
        ahead = [_shift_up(dy, nxt, FFN_CONV - 1 - k) for k in range(FFN_CONV)]
        dx = cw[FFN_CONV - 1:FFN_CONV, :] * dy
        for k in range(FFN_CONV - 1):
            dx = dx + cw[k:k + 1, :] * ahead[k]
        dx = dx.astype(dup_ref.dtype)
        dup_ref[:, col] = dx
        for k in range(FFN_CONV):
            acc_ref[k * SUBLANES:(k + 1) * SUBLANES, :] += _colsum8(ahead[k] * x)
        acc_ref[FFN_CONV * SUBLANES:, :] += _colsum8(dy)
        return dx

    def body(dh_ref, dhb_ref, wd_ref, ua_ref, uv_ref, ca_ref, cv_ref, cwa_ref, cwv_ref, wu_ref, h1_ref, g2_ref,
             dua_ref, duv_ref, acca_ref, accv_ref, dh1_ref, dh1b_ref, dg2_ref, carry_a, carry_v, du):
        i, jg = pl.program_id(0), pl.program_id(1)

        @pl.when((i == 0) & (jg == 0))
        def _():
            for ref in (acca_ref, accv_ref, carry_a, carry_v, dg2_ref):
                ref[...] = jnp.zeros_like(ref)

        dhb = dhb_ref[...]
        part = None
        for jj in range(group):
            col = slice(jj * tn, (jj + 1) * tn)
            j = jg * group + jj
            v = cv_ref[:, col].astype(F32)
            g, dg = _gelu_parts(ca_ref[:, col].astype(F32))
            dact = _dot_nt(dhb, wd_ref[pl.ds(pl.multiple_of(j * tn, tn), tn), :])
            da = conv_bwd(dact * v * dg, ua_ref[:, col].astype(F32), cwa_ref[:, col], acca_ref.at[j], carry_a.at[j],
                          dua_ref, col)
            dv = conv_bwd(dact * g, uv_ref[:, col].astype(F32), cwv_ref[:, col], accv_ref.at[j], carry_v.at[j],
                          duv_ref, col)
            term = _dot_nt(da, wu_ref[j]) + _dot_nt(dv, wu_ref[nj + j])
            part = term if part is None else part + term

        @pl.when(jg == 0)
        def _():
            du[...] = part

        @pl.when(jg > 0)
        def _():
            du[...] += part

        @pl.when(jg == ng - 1)
        def _():
            n, rstd, _ = _rms_fwd(h1_ref[...], g2_ref[...])
            dh1, dg2 = _rms_bwd(du[...], n, rstd, g2_ref[...])
            dh1 = dh1 + dh_ref[...]
            dg2_ref[...] += dg2
            dh1_ref[...] = dh1
            dh1b_ref[...] = dh1.astype(dh1b_ref.dtype)

    row = lambda i, j: (ni - 1 - i, 0)
    const = lambda i, j: (0, 0)
    tile = pl.BlockSpec((tm, tw), lambda i, j: (ni - 1 - i, j))
    acc = pl.BlockSpec((nj, FFN_ACC_ROWS, tn), lambda i, j: (0, 0, 0))
    return _call(body, name="ffn_bwd", grid=(ni, ng), comm=comm,
                 in_specs=[pl.BlockSpec((tm, D_MODEL), row), pl.BlockSpec((tm, D_MODEL), row),
                           _resident((D_FF, D_MODEL)), tile, tile, tile, tile,
                           pl.BlockSpec((FFN_CONV, tw), lambda i, j: (0, j)),
                           pl.BlockSpec((FFN_CONV, tw), lambda i, j: (0, j + ng)),
                           _resident(w_up_blocks.shape), pl.BlockSpec((tm, D_MODEL), row),
                           pl.BlockSpec((1, D_MODEL), const)],
                 out_specs=[tile, tile, acc, acc, pl.BlockSpec((tm, D_MODEL), row), pl.BlockSpec((tm, D_MODEL), row),
                            pl.BlockSpec((SUBLANES, D_MODEL), const)],
                 out_shape=[jax.ShapeDtypeStruct((t, D_FF), MXU_DTYPE), jax.ShapeDtypeStruct((t, D_FF), MXU_DTYPE),
                            jax.ShapeDtypeStruct((nj, FFN_ACC_ROWS, tn), F32),
                            jax.ShapeDtypeStruct((nj, FFN_ACC_ROWS, tn), F32),
                            jax.ShapeDtypeStruct((t, D_MODEL), F32), jax.ShapeDtypeStruct((t, D_MODEL), MXU_DTYPE),
                            jax.ShapeDtypeStruct((SUBLANES, D_MODEL), F32)],
                 scratch_shapes=[pltpu.VMEM((nj, SUBLANES, tn), F32), pltpu.VMEM((nj, SUBLANES, tn), F32),
                                 pltpu.VMEM((tm, D_MODEL), F32)],
                 operands=[dh2, dh2_b, w_down, up_a, up_v, conv_a, conv_v, conv_w, conv_w, w_up_blocks, h1, g2])


def _norm_bwd_matmul(parts, w_blocks, h, gain, d_res, name, comm=None, tiles=None, into=None):
    t = h.shape[0]
    tm = min(256, t)
    first, count = tiles or (0, t // tm)
    n_parts = len(parts)
    nb, _, cb = w_blocks.shape
    where = []
    for p, a in enumerate(parts):
        assert a.shape[1] % cb == 0
        where += [(p, lo) for lo in range(0, a.shape[1], cb)]
    assert len(where) == nb
    n_in = n_parts + 4

    def body(*refs):
        a_refs = refs[:n_parts]
        w_ref, h_ref, g_ref, dres_ref = refs[n_parts:n_in]
        dh_ref, dhb_ref, dg_ref = refs[-3:]

        @pl.when(pl.program_id(0) == 0)
        def _():
            dg_ref[...] = jnp.zeros_like(dg_ref)

        du = None
        for d, (p, lo) in enumerate(where):
            term = _dot_nt(a_refs[p][:, lo:lo + cb], w_ref[d])
            du = term if du is None else du + term
        n, rstd, _ = _rms_fwd(h_ref[...], g_ref[...])
        dh, dg = _rms_bwd(du, n, rstd, g_ref[...])
        dh = dh + dres_ref[...]
        dg_ref[...] += dg
        dh_ref[...] = dh
        dhb_ref[...] = dh.astype(dhb_ref.dtype)

    row = lambda i: (i + first, 0)
    const = lambda i: (0, 0)
    in_specs = [pl.BlockSpec((tm, a.shape[1]), row) for a in parts] + [
        _resident(w_blocks.shape), pl.BlockSpec((tm, D_MODEL), row), pl.BlockSpec((1, D_MODEL), const),
        pl.BlockSpec((tm, D_MODEL), row)]
    operands = [*parts, w_blocks, h, gain, d_res]
    aliases = {}
    if into is not None:
        in_specs += [ANY, ANY]
        operands += list(into)
        aliases = {n_in: 0, n_in + 1: 1}
    return _call(body, name=name, grid=(count,), comm=comm, in_specs=in_specs,
                 out_specs=[pl.BlockSpec((tm, D_MODEL), row), pl.BlockSpec((tm, D_MODEL), row),
                            pl.BlockSpec((SUBLANES, D_MODEL), const)],
                 out_shape=[jax.ShapeDtypeStruct((t, D_MODEL), F32), jax.ShapeDtypeStruct((t, D_MODEL), MXU_DTYPE),
                            jax.ShapeDtypeStruct((SUBLANES, D_MODEL), F32)],
                 operands=operands, aliases=aliases)


def _ret_bwd(proj, cos2, sin_signed, gain, o, states, dmixed, comm):
    t = proj.shape[0]
    c, d, nh = RET_CHUNK, RET_HEAD_DIM, RET_HEADS
    n_chunks = t // c
    per = RET_CHUNKS_PER_STEP if n_chunks % RET_CHUNKS_PER_STEP == 0 else 1
    rows = per * c
    n_steps = n_chunks // per
    decay, xi, zeta, g_chunk = _ret_consts()
    base = 2 * D_LRU

    def body(qk_ref, vg_ref, cos_ref, sin_ref, dec_ref, xi_ref, zeta_ref, gain_ref, o_ref, st_ref, dy_ref,
             dp_ref, dgain_ref, gstate):
        @pl.when(pl.program_id(0) == 0)
        def _():
            gstate[...] = jnp.zeros_like(gstate)
            dgain_ref[...] = jnp.zeros_like(dgain_ref)

        cur = [gstate[h] for h in range(nh)]
        for s in reversed(range(per)):
            rs = slice(s * c, (s + 1) * c)
            cos2, sin_s = cos_ref[rs, :], sin_ref[rs, :]
            for h in range(nh):
                lo = h * d
                q = _rope(qk_ref[rs, lo:lo + d], cos2, sin_s)
                k = _rope(qk_ref[rs, D_RET + lo:D_RET + lo + d], cos2, sin_s) * RET_SCALE
                v = vg_ref[rs, lo:lo + d]
                g = vg_ref[rs, D_RET + lo:D_RET + lo + d]
                gain_h = gain_ref[:, lo:lo + d]
                xi_h, zeta_h, dec = xi_ref[:, lo:lo + d], zeta_ref[:, lo:lo + d], dec_ref[h]
                dy = dy_ref[rs, lo:lo + d]
                sg = _sigmoid(g)
                o_h = o_ref[rs, lo:lo + d]
                oc = o_h - jnp.mean(o_h, axis=-1, keepdims=True)
                rstd = lax.rsqrt(jnp.mean(oc * oc, axis=-1, keepdims=True) + NORM_EPS)
                on = oc * rstd
                at = base + 3 * D_RET + lo
                dp_ref[rs, at:at + d] = (dy * on * gain_h * (sg * (1.0 + g * (1.0 - sg)))).astype(dp_ref.dtype)
                don_g = dy * (g * sg)
                dgain_ref[:, lo:lo + d] += _colsum8(don_g * on)
                don = don_g * gain_h
                do = rstd * (don - jnp.mean(don, axis=-1, keepdims=True)
                             - on * jnp.mean(don * on, axis=-1, keepdims=True))
                s_prev = st_ref[s, h]
                g_next = cur[h]
                p = _dot_nt(q, k) * dec
                dpm = _dot_nt(do, v) * dec
                dq = _dot(dpm, k) + _dot_nt(do, s_prev) * xi_h
                dk = _dot_tn(dpm, q) + _dot_nt(v, g_next) * zeta_h
                dv = _dot_tn(p, do) + _dot(k * zeta_h, g_next)
                cur[h] = g_next * g_chunk[h] + _dot_tn(q * xi_h, do)
                dp_ref[rs, base + lo:base + lo + d] = _rope_bwd(dq, cos2, sin_s).astype(dp_ref.dtype)
                at = base + D_RET + lo
                dp_ref[rs, at:at + d] = _rope_bwd(dk * RET_SCALE, cos2, sin_s).astype(dp_ref.dtype)
                at = base + 2 * D_RET + lo
                dp_ref[rs, at:at + d] = dv.astype(dp_ref.dtype)
        for h in range(nh):
            gstate[h] = cur[h]

    rev = lambda col: (lambda i: (n_steps - 1 - i, col))
    full2 = lambda i: (0, 0)
    return _call(body, name="ret_bwd", grid=(n_steps,), comm=comm,
                 in_specs=[pl.BlockSpec((rows, 2 * D_RET), rev(1)), pl.BlockSpec((rows, 2 * D_RET), rev(2)),
                           pl.BlockSpec((rows, d), rev(0)), pl.BlockSpec((rows, d), rev(0)),
                           pl.BlockSpec((nh, c, c), lambda i: (0, 0, 0)), pl.BlockSpec((c, D_RET), full2),
                           pl.BlockSpec((c, D_RET), full2), pl.BlockSpec((1, D_RET), full2),
                           pl.BlockSpec((rows, D_RET), rev(0)),
                           pl.BlockSpec((per, nh, d, d), lambda i: (n_steps - 1 - i, 0, 0, 0)),
                           pl.BlockSpec((rows, D_RET), rev(1))],
                 out_specs=[pl.BlockSpec((rows, D_IN), rev(0)), pl.BlockSpec((SUBLANES, D_RET), full2)],
                 out_shape=[jax.ShapeDtypeStruct((t, D_IN), MXU_DTYPE), jax.ShapeDtypeStruct((SUBLANES, D_RET), F32)],
                 scratch_shapes=[pltpu.VMEM((nh, d, d), F32)],
                 operands=[proj, proj, cos2, sin_signed, decay, xi, zeta, gain, o, states, dmixed])


LRU_ACC = {"conv_w": 0, "conv_b": LRU_CONV, "gate_a_b": LRU_CONV + 1, "gate_x_b": LRU_CONV + 2,
           "lambda": LRU_CONV + 3, "norm_gain": LRU_CONV + 4}
LRU_ACC_ROWS = SUBLANES * (LRU_CONV + 5)


def _lru_bwd(proj, xc_all, h_all, dmixed, dproj, conv_w, wa, ba, wx, bx, lam, gain):
    t = proj.shape[0]
    tm = min(256, t)
    c = D_LRU
    ni = t // tm

    def body(x_ref, xh_ref, g_ref, xc_ref, h_ref, hh_ref, dy_ref, cw_ref, wa_ref, ba_ref, wx_ref, bx_ref, lam_ref,
             gain_ref, dproj_in, dp_ref, acc_ref, dwa_ref, dwx_ref, a_scr, b_scr, mu_scr, carry_mu, carry_dxc):
        del dproj_in
        i = pl.program_id(0)
        r = ni - 1 - i

        @pl.when(i == 0)
        def _():
            acc_ref[...] = jnp.zeros_like(acc_ref)
            dwa_ref[...] = jnp.zeros_like(dwa_ref)
            dwx_ref[...] = jnp.zeros_like(dwx_ref)
            carry_mu[...] = jnp.zeros_like(carry_mu)
            carry_dxc[...] = jnp.zeros_like(carry_dxc)

        def add(name, val, k=0):
            lo = (LRU_ACC[name] + k) * SUBLANES
            acc_ref[lo:lo + SUBLANES, :] += _colsum8(val)

        xc, h = xc_ref[...], h_ref[...]
        lam_v = lam_ref[...]
        sp = _softplus(-lam_v)
        rg, ig, a, m = _lru_gates(xc, wa_ref[...], ba_ref[...], wx_ref[...], bx_ref[...], sp)
        gl, dgl = _gelu_parts(g_ref[...])
        zn, rstd, _ = _rms_fwd(h * gl, gain_ref[...])
        dy = dy_ref[...]
        dz, dgain = _rms_bwd(dy, zn, rstd, gain_ref[...])
        lo = LRU_ACC["norm_gain"] * SUBLANES
        acc_ref[lo:lo + SUBLANES, :] += dgain
        dp_ref[:, c:] = (dz * h * dgl).astype(dp_ref.dtype)
        dh = dz * gl
        ga, gb = _group_scan(a, a * dh, reverse=True)
        a_scr[...] = ga
        b_scr[...] = gb
        mu_next_tile = carry_mu[...]
        carry_mu[...] = _carry_scan(a_scr, b_scr, mu_scr, mu_next_tile, reverse=True)
        lam_t = dh + _shift_up(mu_scr[...], mu_next_tile, 1)
        h_prev = _shift_down(jnp.where(r == 0, 0.0, hh_ref[...]), h, 1)
        da = lam_t * h_prev
        dig = lam_t * m * xc
        dxc = lam_t * m * ig
        dlog_a = da * a - (lam_t * ig * xc) * (a * a) / m
        dpr = dlog_a * ((-LRU_C) * sp) * rg * (1.0 - rg)
        add("lambda", dlog_a * ((-LRU_C) * rg) * (-_sigmoid(-lam_v)))
        dpi = dig * ig * (1.0 - ig)
        add("gate_a_b", dpr)
        add("gate_x_b", dpi)
        dwa_ref[...] += _dot_tn(xc, dpr)
        dwx_ref[...] += _dot_tn(xc, dpi)
        dxc = dxc + _dot_nt(dpr, wa_ref[...]) + _dot_nt(dpi, wx_ref[...])
        add("conv_b", dxc)
        x = x_ref[...]
        prev = jnp.where(r == 0, 0.0, xh_ref[...])
        cw = cw_ref[...]
        nxt = carry_dxc[...]
        carry_dxc[...] = dxc[:SUBLANES, :]
        dx = cw[LRU_CONV - 1:LRU_CONV, :] * dxc
        for k in range(LRU_CONV - 1):
            dx = dx + cw[k:k + 1, :] * _shift_up(dxc, nxt, LRU_CONV - 1 - k)
        for k in range(LRU_CONV):
            add("conv_w", dxc * _shift_down(prev, x, LRU_CONV - 1 - k), k)
        dp_ref[:, :c] = dx.astype(dp_ref.dtype)

    per = tm // SUBLANES
    rev = lambda col: (lambda i: (ni - 1 - i, col))
    halo = lambda i: (jnp.maximum((ni - 1 - i) * per - 1, 0), 0)
    full = lambda i: (0, 0)
    vec = pl.BlockSpec((1, c), full)
    mat = pl.BlockSpec((c, c), full)
    return pl.pallas_call(
        body, name="lru_bwd", grid=(ni,),
        in_specs=[pl.BlockSpec((tm, c), rev(0)), pl.BlockSpec((SUBLANES, c), halo), pl.BlockSpec((tm, c), rev(1)),
                  pl.BlockSpec((tm, c), rev(0)), pl.BlockSpec((tm, c), rev(0)), pl.BlockSpec((SUBLANES, c), halo),
                  pl.BlockSpec((tm, c), rev(0)), pl.BlockSpec((LRU_CONV, c), full), mat, vec, mat, vec, vec, vec, ANY],
        out_specs=[pl.BlockSpec((tm, 2 * c), rev(0)), pl.BlockSpec((LRU_ACC_ROWS, c), full), mat, mat],
        out_shape=[jax.ShapeDtypeStruct(dproj.shape, dproj.dtype), jax.ShapeDtypeStruct((LRU_ACC_ROWS, c), F32),
                   jax.ShapeDtypeStruct((c, c), F32), jax.ShapeDtypeStruct((c, c), F32)],
        scratch_shapes=[pltpu.VMEM((tm, c), F32), pltpu.VMEM((tm, c), F32), pltpu.VMEM((tm, c), F32),
                        pltpu.VMEM((SUBLANES, c), F32), pltpu.VMEM((SUBLANES, c), F32)],
        input_output_aliases={14: 0},
        compiler_params=pltpu.CompilerParams(dimension_semantics=("arbitrary",), vmem_limit_bytes=VMEM_LIMIT),
    )(proj, proj, proj, xc_all, h_all, h_all, dmixed, conv_w, wa, ba, wx, bx, lam, gain, dproj)


def _pair_sum(core, a, b, name):
    n, r, c = b.shape
    spec = pl.BlockSpec((None, r, c), lambda q, core: (q, 0, 0))

    def body(core_ref, a_ref, b_ref, o_ref):
        o_ref[...] = (a_ref[...].astype(F32) + b_ref[...].astype(F32)).astype(o_ref.dtype)

    return pl.pallas_call(
        body, name=name,
        grid_spec=pltpu.PrefetchScalarGridSpec(
            num_scalar_prefetch=1, grid=(n,),
            in_specs=[pl.BlockSpec((None, r, c), lambda q, core: (2 * q + core[0], 0, 0)), spec], out_specs=spec),
        out_shape=jax.ShapeDtypeStruct(b.shape, b.dtype),
        compiler_params=pltpu.CompilerParams(dimension_semantics=("arbitrary",), vmem_limit_bytes=VMEM_LIMIT),
    )(core, a, b)


ADAMW_BLOCK_BYTES = 4 * 1024 * 1024


def _sum_adamw(parts, w, m, v, name):
    n_parts, r, c = parts.shape
    tr = r
    while n_parts * tr * c * parts.dtype.itemsize > ADAMW_BLOCK_BYTES and tr % (4 * SUBLANES) == 0:
        tr //= 2

    def body(p_ref, w_ref, m_ref, v_ref, g_ref, d_ref, nm_ref, nv_ref):
        g = p_ref[0].astype(F32)
        for s in range(1, n_parts):
            g = g + p_ref[s].astype(F32)
        nm = ADAM_B1 * m_ref[...] + (1.0 - ADAM_B1) * g
        nv = ADAM_B2 * v_ref[...] + (1.0 - ADAM_B2) * (g * g)
        m_hat = nm / (1.0 - ADAM_B1 ** ADAM_STEP)
        v_hat = nv / (1.0 - ADAM_B2 ** ADAM_STEP)
        g_ref[...] = g
        d_ref[...] = -ADAM_LR * (m_hat / (jnp.sqrt(v_hat) + ADAM_EPS) + ADAM_WD * w_ref[...])
        nm_ref[...] = nm
        nv_ref[...] = nv

    row = pl.BlockSpec((tr, c), lambda i: (i, 0))
    return _call(body, name=name, grid=(r // tr,),
                 in_specs=[pl.BlockSpec((n_parts, tr, c), lambda i: (0, i, 0)), row, row, row],
                 out_specs=[row, row, row, row], out_shape=[jax.ShapeDtypeStruct((r, c), F32)] * 4,
                 operands=[parts, w, m, v])


MATRICES = ("w_in", "w_out", "ffn_up_w", "ffn_down_w")
CONVS = ("lru_conv_w", "ffn_conv_w")
REPLICATED = ("norm1_gain", "lru_conv_b", "lru_gate_a_w", "lru_gate_a_b", "lru_gate_x_w", "lru_gate_x_b", "lru_lambda",
              "lru_norm_gain", "ret_norm_gain", "norm2_gain", "ffn_conv_b", "final_norm_gain")
WEIGHTS = ("norm1_gain", "w_in", "lru_conv_w", "lru_conv_b", "lru_gate_a_w", "lru_gate_a_b", "lru_gate_x_w",
           "lru_gate_x_b", "lru_lambda", "lru_norm_gain", "ret_norm_gain", "w_out", "norm2_gain", "ffn_up_w",
           "ffn_conv_w", "ffn_conv_b", "ffn_down_w", "final_norm_gain")


def _rows(a, pad_to):
    a = a.reshape(-1, LANES)
    pad = (-a.shape[0]) % pad_to
    return jnp.pad(a, ((0, pad), (0, 0))) if pad else a


def _pack(arrays, pad_to):
    rows, layout, at = [], [], 0
    for a in arrays:
        r = _rows(a, pad_to)
        layout.append((at, a.size // LANES, a.shape))
        rows.append(r)
        at += r.shape[0]
    return jnp.concatenate(rows, axis=0), layout


def _unpack(packed, layout):
    lead = packed.shape[:-2]
    return [packed[..., at:at + n, :].reshape(lead + shape) for at, n, shape in layout]


def _conv_rows(lru, ffn, dtype, pad_to):
    lead = lru.shape[:-2]
    flat = jnp.concatenate([lru.reshape(lead + (-1,)), ffn.reshape(lead + (-1,))], axis=-1).astype(dtype)
    rows = flat.shape[-1] // LANES
    pad = (-rows) % pad_to
    return jnp.pad(flat.reshape(lead + (rows, LANES)), [(0, 0)] * len(lead) + [(0, pad), (0, 0)])


def _column_blocks(full):
    r, c = full.shape
    return full.reshape(r, N_DEV, c // N_DEV).transpose(1, 0, 2)


def _block_diag(w):
    nh, d, _ = w.shape
    eye = jnp.eye(nh, dtype=w.dtype)
    return (w[:, :, None, :] * eye[:, None, :, None]).reshape(nh * d, nh * d)


def _diag_blocks(dense, nh):
    d = dense.shape[0] // nh
    blocks = dense.reshape(nh, d, nh, d)
    return jnp.stack([blocks[h, :, h, :] for h in range(nh)], axis=0)


def kernel(x, norm1_gain, w_in, lru_conv_w, lru_conv_b, lru_gate_a_w, lru_gate_a_b, lru_gate_x_w, lru_gate_x_b, lru_lambda, lru_norm_gain, ret_norm_gain, w_out, norm2_gain, ffn_up_w, ffn_conv_w, ffn_conv_b, ffn_down_w, final_norm_gain, loss_target, m_norm1_gain, m_w_in, m_lru_conv_w, m_lru_conv_b, m_lru_gate_a_w, m_lru_gate_a_b, m_lru_gate_x_w, m_lru_gate_x_b, m_lru_lambda, m_lru_norm_gain, m_ret_norm_gain, m_w_out, m_norm2_gain, m_ffn_up_w, m_ffn_conv_w, m_ffn_conv_b, m_ffn_down_w, m_final_norm_gain, v_norm1_gain, v_w_in, v_lru_conv_w, v_lru_conv_b, v_lru_gate_a_w, v_lru_gate_a_b, v_lru_gate_x_w, v_lru_gate_x_b, v_lru_lambda, v_lru_norm_gain, v_ret_norm_gain, v_w_out, v_norm2_gain, v_ffn_up_w, v_ffn_conv_w, v_ffn_conv_b, v_ffn_down_w, v_final_norm_gain):
    args = dict(locals())
    given = {n: args[n] for n in WEIGHTS}
    out_shape = {n: given[n].shape for n in WEIGHTS}

    def plain(a):
        return a.reshape(1, -1) if a.ndim <= 2 else a[0]

    w = {n: plain(given[n]) for n in WEIGHTS}
    mom_m = {n: plain(args["m_" + n]) for n in WEIGHTS}
    mom_v = {n: plain(args["v_" + n]) for n in WEIGHTS}
    x2, target = x[0], loss_target[0]
    t = x2.shape[0]
    core = lax.axis_index("c").astype(jnp.int32).reshape(1)
    res = {}

    conv_pad = _conv_rows(w["lru_conv_w"], w["ffn_conv_w"], F32, SUBLANES)
    first = _gather_first([w["w_in"].astype(MXU_DTYPE), conv_pad])
    w_in_blocks, conv_all = _run_comms([first, _gather_second(first.out_shape)], "w_in_all_gather")
    n_lru = w["lru_conv_w"].size
    conv_flat = conv_all.reshape(N_DEV, -1)
    lru_cw = conv_flat[:, :n_lru].reshape((N_DEV,) + w["lru_conv_w"].shape).transpose(1, 0, 2).reshape(LRU_CONV, D_LRU)
    ffn_cw = conv_flat[:, n_lru:n_lru + w["ffn_conv_w"].size].reshape((N_DEV,) + w["ffn_conv_w"].shape)
    ffn_cw = ffn_cw.transpose(1, 0, 2).reshape(FFN_CONV, 2 * D_FF)

    cos2, sin_signed = _rope_tables(t)
    wa = _block_diag(w["lru_gate_a_w"]).astype(MXU_DTYPE)
    wx = _block_diag(w["lru_gate_x_w"]).astype(MXU_DTYPE)
    gf = w["final_norm_gain"]

    up_first = _gather_first([w["ffn_up_w"].astype(MXU_DTYPE)])
    (u1, proj), (up_part,) = _inproj_fwd(x2, w["norm1_gain"], w_in_blocks, up_first)
    rest_first = _gather_first([w["w_out"].astype(MXU_DTYPE), w["ffn_down_w"].astype(MXU_DTYPE)])

    (xc, h_lru, y_lru), lru_side = _lru_fwd(proj, lru_cw, w["lru_conv_b"], wa, w["lru_gate_a_b"], wx, w["lru_gate_x_b"],
                                            w["lru_lambda"], w["lru_norm_gain"],
                                            _both(rest_first, _gather_second([up_part])))
    w_out_part, down_part, up_blocks = lru_side
    (o_ret, y_ret, states), (w_out_blocks, down_blocks) = _ret_fwd(proj, cos2, sin_signed, w["ret_norm_gain"],
                                                                 _gather_second([w_out_part, down_part]))
    w_out_full = w_out_blocks.reshape(D_MODEL, D_MODEL)
    w_down_full = down_blocks.reshape(D_FF, D_MODEL)

    h1, u2 = _outproj_fwd(x2, y_lru, y_ret, w_out_full, w["norm2_gain"])
    up_a, up_v, conv_a, conv_v, act, dh2, dh2_b, dgf, loss = _ffn_fwd(u2, up_blocks, ffn_cw, w["ffn_conv_b"], w_down_full,
                                                                      h1, gf, target)
    loss = lax.psum(loss[0, 0], ("x", "y", "c"))

    def to_owner_chips(blocks, names, tag):
        theirs = _run_comms([_pair_exchange(blocks)], "grads_pair_exchange_" + tag)
        return [_pair_sum(core, a, b, "grads_pair_sum_" + n) for n, a, b in zip(names, blocks, theirs)]

    def adamw(name, parts):
        res[name] = _sum_adamw(parts, w[name], mom_m[name], mom_v[name], "adamw_" + name)

    g = {"final_norm_gain": dgf[0]}
    g_down = _mm_tn(act, dh2_b, "ffn_down_wgrad").reshape(N_DEV, D_FF // N_DEV, D_MODEL)
    down_sums = to_owner_chips([g_down], ["ffn_down_w"], "down")
    (dup_a, dup_v, acc_a, acc_v, dh1, dh1_b, dg2), (down_parts,) = _ffn_bwd(
        dh2, dh2_b, w_down_full, up_a, up_v, conv_a, conv_v, ffn_cw, up_blocks, h1, w["norm2_gain"],
        _chip_exchange(down_sums))
    adamw("ffn_down_w", down_parts)
    per_col = lambda a: a[:, ::SUBLANES].transpose(1, 0, 2).reshape(FFN_CONV + 1, D_FF)
    acc = jnp.concatenate([per_col(acc_a), per_col(acc_v)], axis=1)
    g_ffn_cw, g["ffn_conv_b"] = acc[:FFN_CONV], acc[FFN_CONV:]
    g["norm2_gain"] = dg2[:1]
    g_up = jnp.concatenate([_mm_tn(u2, dup_a, "ffn_up_wgrad_a", blocks=N_DEV // 2),
                            _mm_tn(u2, dup_v, "ffn_up_wgrad_v", blocks=N_DEV // 2)], axis=0)
    g_out = jnp.concatenate([_mm_tn(y_lru, dh1_b, "w_out_wgrad_lru"), _mm_tn(y_ret, dh1_b, "w_out_wgrad_ret")], axis=0)
    mid_sums = to_owner_chips([g_up, g_out.reshape(N_DEV, D_MODEL // N_DEV, D_MODEL)], ["ffn_up_w", "w_out"], "mid")
    (dmixed,) = _mm_nt(dh1_b, w_out_full, "outproj_bwd", F32)
    (dproj, dgain_ret), (up_parts, out_parts) = _ret_bwd(proj, cos2, sin_signed, w["ret_norm_gain"], o_ret, states, dmixed,
                                                         _chip_exchange(mid_sums))
    adamw("ffn_up_w", up_parts)
    adamw("w_out", out_parts)
    g["ret_norm_gain"] = dgain_ret[:1]
    dproj, lru_acc, dwa, dwx = _lru_bwd(proj, xc, h_lru, dmixed, dproj, lru_cw, wa, w["lru_gate_a_b"], wx,
                                        w["lru_gate_x_b"], w["lru_lambda"], w["lru_norm_gain"])
    lru_acc = lru_acc[::SUBLANES]
    g_lru_cw = lru_acc[:LRU_CONV]
    for name in ("conv_b", "gate_a_b", "gate_x_b", "lambda", "norm_gain"):
        g["lru_" + name] = lru_acc[LRU_ACC[name]:LRU_ACC[name] + 1]
    g["lru_gate_a_w"] = _diag_blocks(dwa, LRU_HEADS)
    g["lru_gate_x_w"] = _diag_blocks(dwx, LRU_HEADS)
    g_in = _mm_tn(u1, dproj, "w_in_wgrad", blocks=N_DEV)
    g_conv = _conv_rows(_column_blocks(g_lru_cw), _column_blocks(g_ffn_cw), GRAD_DTYPE, 2 * SUBLANES)
    in_sums = to_owner_chips([g_in, g_conv], ["w_in", "conv"], "in")
    half = t // min(256, t) // 2
    (gx_half, gxb_half, dg1_lo), (in_parts, conv_parts) = _norm_bwd_matmul(
        [dproj], w_in_blocks, x2, w["norm1_gain"], dh1, "inproj_bwd_lo", _chip_exchange(in_sums), tiles=(0, half))
    grad_x, _, dg1_hi = _norm_bwd_matmul([dproj], w_in_blocks, x2, w["norm1_gain"], dh1, "inproj_bwd_hi",
                                         tiles=(half, t // min(256, t) - half), into=(gx_half, gxb_half))
    adamw("w_in", in_parts)
    g["norm1_gain"] = dg1_lo[:1] + dg1_hi[:1]

    pad16 = lambda d: _conv_rows(d["lru_conv_w"], d["ffn_conv_w"], F32, 2 * SUBLANES)
    conv_res = _sum_adamw(conv_parts, pad16(w), pad16(mom_m), pad16(mom_v), "adamw_conv")
    for n, lo, hi in (("lru_conv_w", 0, n_lru), ("ffn_conv_w", n_lru, n_lru + w["ffn_conv_w"].size)):
        res[n] = [r.reshape(-1)[lo:hi].reshape(w[n].shape) for r in conv_res]
    rep_packed, rep_layout = _pack([g[n] for n in REPLICATED], SUBLANES)
    rep_first = _gather_first([rep_packed])
    (rep_parts,) = _run_comms([rep_first, _gather_second(rep_first.out_shape)], "small_grads_all_gather")
    rep_res = _sum_adamw(rep_parts, *[_pack([d[n] for n in REPLICATED], SUBLANES)[0] for d in (w, mom_m, mom_v)],
                         "adamw_replicated")
    for k in range(4):
        for n, a in zip(REPLICATED, _unpack(rep_res[k], rep_layout)):
            res.setdefault(n, [None] * 4)[k] = a

    outs = [loss, grad_x[None]]
    for k in range(4):
        outs += [res[n][k].reshape(out_shape[n]) for n in WEIGHTS]
    return tuple(outs)
```

```python
import math

import numpy as np
import jax
import jax.numpy as jnp
from jax import lax
from jax.experimental import pallas as pl
from jax.experimental.pallas import tpu as pltpu

F32 = jnp.float32
BF16 = jnp.bfloat16
MXU_DTYPE = jnp.bfloat16
GRAD_DTYPE = jnp.bfloat16

N_DEV = 8
N_CHIPS = 4
D_MODEL = 1024
D_LRU = 512
LRU_HEADS = 8
LRU_CONV = 4
LRU_C = 8.0
D_RET = 512
RET_HEADS = 4
RET_HEAD_DIM = 128
RET_CHUNK = 128
ROPE_BASE = 10000.0
D_IN = 3072
D_FF = 3072
FFN_CONV = 3
NORM_EPS = 1e-6

ADAM_LR = 0.001
ADAM_B1 = 0.9
ADAM_B2 = 0.999
ADAM_EPS = 1e-08
ADAM_WD = 0.01
ADAM_STEP = 10

SUBLANES = 8
LANES = 128
VMEM_LIMIT = 48 * 1024 * 1024

MESH = pl.DeviceIdType.MESH
ANY = pl.BlockSpec(memory_space=pl.ANY)


def _dot(a, b):
    return jnp.dot(a.astype(MXU_DTYPE), b.astype(MXU_DTYPE), preferred_element_type=F32)


def _dot_nt(a, b):
    return lax.dot_general(a.astype(MXU_DTYPE), b.astype(MXU_DTYPE), (((1,), (1,)), ((), ())),
                           preferred_element_type=F32)


def _dot_tn(a, b):
    return lax.dot_general(a.astype(MXU_DTYPE), b.astype(MXU_DTYPE), (((0,), (0,)), ((), ())),
                           preferred_element_type=F32)


def _sigmoid(x):
    return 0.5 + 0.5 * jnp.tanh(0.5 * x)


_GELU_C = math.sqrt(2.0 / math.pi)


def _gelu_parts(x):
    x2 = x * x
    t = jnp.tanh(_GELU_C * (x + 0.044715 * (x2 * x)))
    cdf = 0.5 * (1.0 + t)
    g = x * cdf
    dg = cdf + 0.5 * x * (1.0 - t * t) * (_GELU_C * (1.0 + 3.0 * 0.044715 * x2))
    return g, dg


def _gelu(x):
    t = jnp.tanh(_GELU_C * (x + 0.044715 * (x * x * x)))
    return x * (0.5 * (1.0 + t))


def _softplus(x):
    return jnp.maximum(x, 0.0) + jnp.log1p(jnp.exp(-jnp.abs(x)))


def _bcast_row(x, r, rows=SUBLANES):
    return jnp.broadcast_to(x[r:r + 1, :], (rows, x.shape[1]))


def _colsum8(x):
    return jnp.broadcast_to(jnp.sum(x, axis=0, keepdims=True), (SUBLANES, x.shape[1]))


def _shift_down(prev8, tile, s):
    if s == 0:
        return tile
    ext = jnp.concatenate([prev8, tile], axis=0)
    return pltpu.roll(ext, s, 0)[SUBLANES:, :]


def _shift_up(tile, next8, s):
    if s == 0:
        return tile
    ext = jnp.concatenate([tile, next8], axis=0)
    return pltpu.roll(ext, SUBLANES - s, 0)[SUBLANES:, :]


def _group_scan(a, b, reverse, fill=lambda: None):
    n = a.shape[0]
    row = lax.broadcasted_iota(jnp.int32, a.shape, 0) & (SUBLANES - 1)
    for s in (1, 2, 4):
        if s > 1:
            fill()
        shift = (n - s) if reverse else s
        a_sh = pltpu.roll(a, shift, 0)
        b_sh = pltpu.roll(b, shift, 0)
        m = (row <= SUBLANES - 1 - s) if reverse else (row >= s)
        b = jnp.where(m, a * b_sh + b, b)
        a = jnp.where(m, a * a_sh, a)
    return a, b


def _carry_scan(a_ref, b_ref, out_ref, carry0, reverse):
    n_groups = a_ref.shape[0] // SUBLANES
    carry = carry0
    for i in range(n_groups):
        r0 = ((n_groups - 1 - i) if reverse else i) * SUBLANES
        hg = a_ref[r0:r0 + SUBLANES, :] * carry + b_ref[r0:r0 + SUBLANES, :]
        out_ref[r0:r0 + SUBLANES, :] = hg
        carry = _bcast_row(hg, 0 if reverse else SUBLANES - 1)
    return carry


def _rms_fwd(h, gain):
    rstd = lax.rsqrt(jnp.mean(h * h, axis=-1, keepdims=True) + NORM_EPS)
    n = h * rstd
    return n, rstd, n * gain


def _rms_bwd(dy, n, rstd, gain):
    dn = dy * gain
    dh = rstd * (dn - n * jnp.mean(dn * n, axis=-1, keepdims=True))
    return dh, _colsum8(dy * n)


def _halo_rows(dtype):
    return SUBLANES * (4 // jnp.dtype(dtype).itemsize)


def _halo_map(tile_rows, col, halo_rows=SUBLANES):
    per = tile_rows // halo_rows
    return lambda i: (jnp.maximum(i * per - 1, 0), col)


def _resident(shape):
    return pl.BlockSpec(shape, lambda *_: (0,) * len(shape), pipeline_mode=pl.Buffered(1))


def _place():
    x, y, c = lax.axis_index("x"), lax.axis_index("y"), lax.axis_index("c")
    chips = [(1 - x, y), (x, 1 - y), (1 - x, 1 - y)]
    return x, y, c, chips


def _dev(x, y, c):
    return 4 * x + 2 * y + c


def _remote(src, dst, send_sem, recv_sem, to):
    return pltpu.make_async_remote_copy(src_ref=src, dst_ref=dst, send_sem=send_sem, recv_sem=recv_sem,
                                        device_id=to, device_id_type=MESH)


class _Comm:
    def __init__(self, operands, out_shape, sems, descs, aliases=()):
        self.operands, self.out_shape, self.sems, self.descs, self.aliases = operands, out_shape, sems, descs, aliases

    def start(self, ins, outs, sems):
        local, sends, _ = self.descs(ins, outs, sems)
        for cp in sends + local:
            cp.start()

    def wait(self, ins, outs, sems):
        local, sends, recvs = self.descs(ins, outs, sems)
        for cp in recvs:
            cp.wait_recv()
        for cp in sends:
            cp.wait_send()
        for cp in local:
            cp.wait()


def _gather_first(shards):
    n = len(shards)

    def descs(ins, outs, sems):
        send, recv, loc = sems
        x, y, c, chips = _place()
        me = _dev(x, y, c)
        targets = [(x, y, 1 - c)] + [(*chip, c) for chip in chips]
        local, sends, recvs = [], [], []
        for t in range(n):
            local.append(pltpu.make_async_copy(ins[t], outs[t].at[me], loc.at[t]))
            for k, to in enumerate(targets):
                i = 4 * t + k
                sends.append(_remote(ins[t], outs[t].at[me], send.at[i], recv.at[i], to))
                recvs.append(_remote(ins[t], outs[t].at[_dev(*to)], send.at[i], recv.at[i], to))
        return local, sends, recvs

    return _Comm(list(shards), [jax.ShapeDtypeStruct((N_DEV,) + s.shape, s.dtype) for s in shards],
                 [pltpu.SemaphoreType.DMA((4 * n,)), pltpu.SemaphoreType.DMA((4 * n,)), pltpu.SemaphoreType.DMA((n,))],
                 descs)


def _gather_second(gathered):
    n = len(gathered)

    def descs(ins, outs, sems):
        send, recv = sems
        x, y, c, chips = _place()
        sends, recvs = [], []
        for t in range(n):
            for j, chip in enumerate(chips):
                i = 3 * t + j
                have, get = _dev(*chip, c), _dev(*chip, 1 - c)
                sends.append(_remote(outs[t].at[have], outs[t].at[have], send.at[i], recv.at[i], (x, y, 1 - c)))
                recvs.append(_remote(outs[t].at[have], outs[t].at[get], send.at[i], recv.at[i], (x, y, 1 - c)))
        return [], sends, recvs

    return _Comm(list(gathered), [jax.ShapeDtypeStruct(g.shape, g.dtype) for g in gathered],
                 [pltpu.SemaphoreType.DMA((3 * n,)), pltpu.SemaphoreType.DMA((3 * n,))], descs,
                 aliases=[(t, t) for t in range(n)])


def _pair_exchange(blocks):
    n = len(blocks)

    def descs(ins, outs, sems):
        send, recv = sems
        x, y, c, _ = _place()
        sends, recvs = [], []
        for t in range(n):
            for q in range(N_CHIPS):
                i = N_CHIPS * t + q
                cp = _remote(ins[t].at[2 * q + 1 - c], outs[t].at[q], send.at[i], recv.at[i], (x, y, 1 - c))
                sends.append(cp)
                recvs.append(cp)
        return [], sends, recvs

    return _Comm(list(blocks), [jax.ShapeDtypeStruct((N_CHIPS,) + b.shape[1:], b.dtype) for b in blocks],
                 [pltpu.SemaphoreType.DMA((N_CHIPS * n,)), pltpu.SemaphoreType.DMA((N_CHIPS * n,))], descs)


def _chip_exchange(blocks):
    n = len(blocks)

    def descs(ins, outs, sems):
        send, recv, loc = sems
        x, y, c, chips = _place()
        me = 2 * x + y
        local, sends, recvs = [], [], []
        for t in range(n):
            local.append(pltpu.make_async_copy(ins[t].at[me], outs[t].at[me], loc.at[t]))
            for j, (px, py) in enumerate(chips):
                i = 3 * t + j
                q = 2 * px + py
                sends.append(_remote(ins[t].at[q], outs[t].at[me], send.at[i], recv.at[i], (px, py, c)))
                recvs.append(_remote(ins[t].at[q], outs[t].at[q], send.at[i], recv.at[i], (px, py, c)))
        return local, sends, recvs

    return _Comm(list(blocks), [jax.ShapeDtypeStruct(b.shape, b.dtype) for b in blocks],
                 [pltpu.SemaphoreType.DMA((3 * n,)), pltpu.SemaphoreType.DMA((3 * n,)), pltpu.SemaphoreType.DMA((n,))],
                 descs)


def _both(a, b):
    na, oa, sa = len(a.operands), len(a.out_shape), len(a.sems)

    def descs(ins, outs, sems):
        local_a, sends_a, recvs_a = a.descs(ins[:na], outs[:oa], sems[:sa])
        local_b, sends_b, recvs_b = b.descs(ins[na:], outs[oa:], sems[sa:])
        return local_a + local_b, sends_a + sends_b, recvs_a + recvs_b

    return _Comm(a.operands + b.operands, a.out_shape + b.out_shape, a.sems + b.sems, descs,
                 aliases=list(a.aliases) + [(na + i, oa + o) for i, o in b.aliases])


def _run_comms(comms, name):
    first = comms[0]
    n_in, n_out = len(first.operands), len(first.out_shape)

    def body(*refs):
        ins, outs, sems = refs[:n_in], refs[n_in:n_in + n_out], list(refs[n_in + n_out:])
        for k, comm in enumerate(comms):
            mine = [sems.pop(0) for _ in comm.sems]
            comm.start(ins if k == 0 else outs, outs, mine)
            comm.wait(ins if k == 0 else outs, outs, mine)

    outs = pl.pallas_call(
        body, name=name, out_shape=first.out_shape, in_specs=[ANY] * n_in, out_specs=[ANY] * n_out,
        scratch_shapes=[s for comm in comms for s in comm.sems],
    )(*first.operands)
    return list(outs)


def _call(body, *, name, grid, in_specs, out_specs, out_shape, operands, scratch_shapes=(), comm=None, aliases=None):
    sem = ("arbitrary",) * len(grid)
    params = pltpu.CompilerParams(dimension_semantics=sem, vmem_limit_bytes=VMEM_LIMIT)
    aliases = dict(aliases or {})
    if comm is None:
        return pl.pallas_call(body, name=name, grid=grid, in_specs=in_specs, out_specs=out_specs, out_shape=out_shape,
                              scratch_shapes=list(scratch_shapes), input_output_aliases=aliases,
                              compiler_params=params)(*operands)
    n_in, n_out, n_scr = len(in_specs), len(out_specs), len(scratch_shapes)
    c_in, c_out = len(comm.operands), len(comm.out_shape)

    def wrapped(*refs):
        refs = list(refs)
        ins, refs = refs[:n_in], refs[n_in:]
        cins, refs = refs[:c_in], refs[c_in:]
        outs, refs = refs[:n_out], refs[n_out:]
        couts, refs = refs[:c_out], refs[c_out:]
        scr, csems = refs[:n_scr], refs[n_scr:]
        first = last = None
        for axis, size in enumerate(grid):
            at_first, at_last = pl.program_id(axis) == 0, pl.program_id(axis) == size - 1
            first = at_first if first is None else first & at_first
            last = at_last if last is None else last & at_last

        @pl.when(first)
        def _():
            comm.start(cins, couts, csems)

        body(*ins, *outs, *scr)

        @pl.when(last)
        def _():
            comm.wait(cins, couts, csems)

    res = pl.pallas_call(
        wrapped, name=name, grid=grid, in_specs=list(in_specs) + [ANY] * c_in, out_specs=list(out_specs) + [ANY] * c_out,
        out_shape=list(out_shape) + list(comm.out_shape), scratch_shapes=list(scratch_shapes) + list(comm.sems),
        input_output_aliases={**aliases, **{n_in + i: n_out + o for i, o in comm.aliases}}, compiler_params=params,
    )(*operands, *comm.operands)
    return list(res[:n_out]), list(res[n_out:])


def _mm_nt(a, b, name, out_dtype, comm=None, tm=512, tn=512):
    m, k = a.shape
    n = b.shape[0]
    tm, tn = min(tm, m), min(tn, n)

    def body(a_ref, b_ref, o_ref):
        o_ref[...] = _dot_nt(a_ref[...], b_ref[...]).astype(o_ref.dtype)

    return _call(body, name=name, grid=(n // tn, m // tm), comm=comm,
                 in_specs=[pl.BlockSpec((tm, k), lambda j, i: (i, 0)), pl.BlockSpec((tn, k), lambda j, i: (j, 0))],
                 out_specs=[pl.BlockSpec((tm, tn), lambda j, i: (i, j))],
                 out_shape=[jax.ShapeDtypeStruct((m, n), out_dtype)], operands=[a, b])


def _mm_tn(a, b, name, blocks=1, tk=2048):
    t, m = a.shape
    n = b.shape[1]
    tk = min(tk, t)
    nk = t // tk
    cb = n // blocks
    per = max(1, 768 // cb) if blocks > 1 else 1
    tn = per * cb if blocks > 1 else min(1024, n)
    tm = min(1024, m)
    assert blocks == 1 or tm == m

    def body(a_ref, b_ref, o_ref, acc):
        k = pl.program_id(2)

        @pl.when(k == 0)
        def _():
            acc[...] = jnp.zeros_like(acc)
        acc[...] += _dot_tn(a_ref[...], b_ref[...])

        @pl.when(k == nk - 1)
        def _():
            if blocks == 1:
                o_ref[...] = acc[...].astype(o_ref.dtype)
            else:
                for s in range(per):
                    o_ref[s] = acc[:, s * cb:(s + 1) * cb].astype(o_ref.dtype)

    if blocks == 1:
        out_spec = pl.BlockSpec((tm, tn), lambda i, j, k: (i, j))
        out_shape = jax.ShapeDtypeStruct((m, n), GRAD_DTYPE)
    else:
        out_spec = pl.BlockSpec((per, m, cb), lambda i, j, k: (j, 0, 0))
        out_shape = jax.ShapeDtypeStruct((blocks, m, cb), GRAD_DTYPE)
    return _call(body, name=name, grid=(m // tm, n // tn, nk),
                 in_specs=[pl.BlockSpec((tk, tm), lambda i, j, k: (k, i)), pl.BlockSpec((tk, tn), lambda i, j, k: (k, j))],
                 out_specs=[out_spec], out_shape=[out_shape], operands=[a, b],
                 scratch_shapes=[pltpu.VMEM((tm, tn), F32)])[0]


def _inproj_fwd(x, g1, w_blocks, comm):
    t = x.shape[0]
    tm = min(512, t)
    nb, _, cb = w_blocks.shape

    def body(x_ref, g_ref, w_ref, u_ref, p_ref):
        _, _, u = _rms_fwd(x_ref[...], g_ref[...])
        u = u.astype(MXU_DTYPE)
        u_ref[...] = u
        for d in range(nb):
            p_ref[:, d * cb:(d + 1) * cb] = _dot(u, w_ref[d])

    return _call(body, name="inproj_fwd", grid=(t // tm,), comm=comm,
                 in_specs=[pl.BlockSpec((tm, D_MODEL), lambda i: (i, 0)), pl.BlockSpec((1, D_MODEL), lambda i: (0, 0)),
                           _resident(w_blocks.shape)],
                 out_specs=[pl.BlockSpec((tm, D_MODEL), lambda i: (i, 0)), pl.BlockSpec((tm, D_IN), lambda i: (i, 0))],
                 out_shape=[jax.ShapeDtypeStruct((t, D_MODEL), MXU_DTYPE), jax.ShapeDtypeStruct((t, D_IN), F32)],
                 operands=[x, g1, w_blocks])


def _lru_gates(xc, wa, ba, wx, bx, sp, fill=lambda: None):
    r = _sigmoid(_dot(xc, wa) + ba)
    fill()
    ig = _sigmoid(_dot(xc, wx) + bx)
    fill()
    log_a = (-LRU_C) * r * sp
    a = jnp.exp(log_a)
    m = jnp.sqrt(-jnp.tanh(log_a) * (a * a + 1.0))
    return r, ig, a, m


def _fused(parts, name, comm=None):
    grid = parts[0]["grid"]
    assert all(p["grid"] == grid for p in parts)
    counts = [(len(p["in_specs"]), len(p["out_specs"]), len(p.get("scratch_shapes", ()))) for p in parts]

    def body(*refs):
        refs = list(refs)
        groups = []
        for kind in range(3):
            taken = []
            for c in counts:
                taken.append(refs[:c[kind]])
                refs = refs[c[kind]:]
            groups.append(taken)
        ins, outs, scr = groups
        pending = []
        for k, p in enumerate(parts):
            if "units" in p:
                pending += p["units"](*ins[k], *outs[k], *scr[k])

        points = sum(p.get("fill_points", 0) for p in parts)
        share = -(-len(pending) // max(points, 1))

        def fill(n=None):
            for _ in range(share if n is None else n):
                if pending:
                    pending.pop(0)()

        for k, p in enumerate(parts):
            if "units" not in p:
                p["body"](*ins[k], *outs[k], *scr[k], dict(peers=outs, fill=fill))
        fill(len(pending))

    cat = lambda key: [x for p in parts for x in p.get(key, ())]
    res = _call(body, name=name, grid=grid, comm=comm, in_specs=cat("in_specs"), out_specs=cat("out_specs"),
                out_shape=cat("out_shape"), scratch_shapes=cat("scratch_shapes"), operands=cat("operands"))
    outs, side = (res if comm is not None else (res, None))
    split, at = [], 0
    for _, n_out, _ in counts:
        split.append(list(outs[at:at + n_out]))
        at += n_out
    return split if comm is None else (split, side)


def _lru_fwd(proj, conv_w, conv_b, wa, ba, wx, bx, lam, gain):
    t = proj.shape[0]
    tm = min(256, t)
    c = D_LRU

    def body(x_ref, xh_ref, g_ref, cw_ref, cb_ref, wa_ref, ba_ref, wx_ref, bx_ref, lam_ref, gain_ref,
             xc_ref, h_ref, y_ref, a_scr, b_scr, carry, ctx):
        fill = ctx["fill"]
        i = pl.program_id(0)

        @pl.when(i == 0)
        def _():
            carry[...] = jnp.zeros_like(carry)

        fill()
        x = x_ref[...]
        prev = jnp.where(i == 0, 0.0, xh_ref[...])
        cw = cw_ref[...]
        xc = cb_ref[...] + cw[LRU_CONV - 1:LRU_CONV, :] * x
        for k in range(LRU_CONV - 1):
            xc = xc + cw[k:k + 1, :] * _shift_down(prev, x, LRU_CONV - 1 - k)
        xc_ref[...] = xc
        fill()
        sp = _softplus(-lam_ref[...])
        _, ig, a, m = _lru_gates(xc, wa_ref[...], ba_ref[...], wx_ref[...], bx_ref[...], sp, fill)
        fill()
        ga, gb = _group_scan(a, m * (ig * xc), reverse=False, fill=fill)
        a_scr[...] = ga
        b_scr[...] = gb
        fill()
        carry[...] = _carry_scan(a_scr, b_scr, h_ref, carry[...], reverse=False)
        fill()
        z = h_ref[...] * _gelu(g_ref[...])
        fill()
        _, _, y = _rms_fwd(z, gain_ref[...])
        y_ref[...] = y.astype(y_ref.dtype)

    row = lambda i: (i, 0)
    full = lambda i: (0, 0)
    vec = pl.BlockSpec((1, c), full)
    return dict(body=body, grid=(t // tm,), fill_points=10,
                in_specs=[pl.BlockSpec((tm, c), row), pl.BlockSpec((SUBLANES, c), _halo_map(tm, 0)),
                          pl.BlockSpec((tm, c), lambda i: (i, 1)),
                          pl.BlockSpec((LRU_CONV, c), full), vec, pl.BlockSpec((c, c), full), vec,
                          pl.BlockSpec((c, c), full), vec, vec, vec],
                out_specs=[pl.BlockSpec((tm, c), row), pl.BlockSpec((tm, c), row), pl.BlockSpec((tm, c), row)],
                out_shape=[jax.ShapeDtypeStruct((t, c), F32), jax.ShapeDtypeStruct((t, c), F32),
                           jax.ShapeDtypeStruct((t, c), MXU_DTYPE)],
                scratch_shapes=[pltpu.VMEM((tm, c), F32), pltpu.VMEM((tm, c), F32), pltpu.VMEM((SUBLANES, c), F32)],
                operands=[proj, proj, proj, conv_w, conv_b, wa, ba, wx, bx, lam, gain])


def _ret_consts():
    c = RET_CHUNK
    log_g = jnp.log1p(-jnp.exp2(-5.0 - jnp.arange(RET_HEADS, dtype=F32)))
    idx = jnp.arange(c, dtype=F32)
    diff = idx[:, None] - idx[None, :]
    decay = jnp.where(diff[None] >= 0, jnp.exp(jnp.maximum(diff, 0.0)[None] * log_g[:, None, None]), 0.0)
    zeta = jnp.exp((c - 1 - idx)[None, :] * log_g[:, None])
    xi = jnp.exp((idx + 1.0)[None, :] * log_g[:, None])
    spread = lambda v: jnp.repeat(v.T, RET_HEAD_DIM, axis=1)
    log_g_np = np.log1p(-np.exp2(-5.0 - np.arange(RET_HEADS, dtype=np.float32))).astype(np.float32)
    g_chunk = [float(np.exp(np.float32(c) * lg)) for lg in log_g_np]
    return decay, spread(xi), spread(zeta), g_chunk


def _rope_tables(t):
    pos = np.arange(t, dtype=np.float32)
    inv_freq = np.float32(ROPE_BASE) ** (-np.arange(0, RET_HEAD_DIM, 2, dtype=np.float32) / np.float32(RET_HEAD_DIM))
    ang = (pos[:, None] * inv_freq.astype(np.float32)[None, :]).astype(np.float32).astype(np.float64)
    cos, sin = np.cos(ang).astype(np.float32), np.sin(ang).astype(np.float32)
    return jnp.asarray(np.concatenate([cos, cos], axis=-1)), jnp.asarray(np.concatenate([-sin, sin], axis=-1))


def _rope(x, cos2, sin_signed):
    return x * cos2 + pltpu.roll(x, RET_HEAD_DIM // 2, 1) * sin_signed


def _rope_bwd(d, cos2, sin_signed):
    return d * cos2 + pltpu.roll(d * sin_signed, RET_HEAD_DIM // 2, 1)


RET_SCALE = RET_HEAD_DIM ** -0.5


RET_CHUNKS_PER_STEP = 2


def _ret_fwd(proj, cos2, sin_signed, gain):
    t = proj.shape[0]
    c, d, nh = RET_CHUNK, RET_HEAD_DIM, RET_HEADS
    n_chunks = t // c
    per = RET_CHUNKS_PER_STEP if n_chunks % RET_CHUNKS_PER_STEP == 0 else 1
    rows = per * c
    decay, xi, zeta, g_chunk = _ret_consts()

    def units(qk_ref, vg_ref, cos_ref, sin_ref, dec_ref, xi_ref, zeta_ref, gain_ref, o_ref, y_ref, st_ref, state):
        cur = [None] * nh

        def start():
            @pl.when(pl.program_id(0) == 0)
            def _():
                state[...] = jnp.zeros_like(state)
            for h in range(nh):
                cur[h] = state[h]

        def retain(s, h, keep):
            rs = slice(s * c, (s + 1) * c)
            cos2, sin_s = cos_ref[rs, :], sin_ref[rs, :]
            lo = h * d
            q = _rope(qk_ref[rs, lo:lo + d], cos2, sin_s)
            k = _rope(qk_ref[rs, D_RET + lo:D_RET + lo + d], cos2, sin_s) * RET_SCALE
            v = vg_ref[rs, lo:lo + d]
            s_prev = cur[h]
            st_ref[s, h] = s_prev
            scores = _dot_nt(q, k) * dec_ref[h]
            o = _dot(scores, v) + _dot(q * xi_ref[:, lo:lo + d], s_prev)
            cur[h] = s_prev * g_chunk[h] + _dot_tn(k * zeta_ref[:, lo:lo + d], v)
            o_ref[rs, lo:lo + d] = o
            keep["o"] = o

        def normalise(s, h, keep):
            rs = slice(s * c, (s + 1) * c)
            lo = h * d
            o = keep["o"]
            g = vg_ref[rs, D_RET + lo:D_RET + lo + d]
            mu = jnp.mean(o, axis=-1, keepdims=True)
            oc = o - mu
            on = oc * lax.rsqrt(jnp.mean(oc * oc, axis=-1, keepdims=True) + NORM_EPS)
            y_ref[rs, lo:lo + d] = (on * gain_ref[:, lo:lo + d] * (g * _sigmoid(g))).astype(y_ref.dtype)

        def end():
            for h in range(nh):
                state[h] = cur[h]

        pieces = [start]
        for s in range(per):
            for h in range(nh):
                keep = {}
                pieces += [lambda s=s, h=h, keep=keep: retain(s, h, keep),
                           lambda s=s, h=h, keep=keep: normalise(s, h, keep)]
        return pieces + [end]

    full2 = lambda i: (0, 0)
    return dict(units=units, grid=(n_chunks // per,),
                in_specs=[pl.BlockSpec((rows, 2 * D_RET), lambda i: (i, 1)),
                          pl.BlockSpec((rows, 2 * D_RET), lambda i: (i, 2)),
                          pl.BlockSpec((rows, d), lambda i: (i, 0)), pl.BlockSpec((rows, d), lambda i: (i, 0)),
                          pl.BlockSpec((nh, c, c), lambda i: (0, 0, 0)), pl.BlockSpec((c, D_RET), full2),
                          pl.BlockSpec((c, D_RET), full2), pl.BlockSpec((1, D_RET), full2)],
                out_specs=[pl.BlockSpec((rows, D_RET), lambda i: (i, 0)), pl.BlockSpec((rows, D_RET), lambda i: (i, 0)),
                           pl.BlockSpec((per, nh, d, d), lambda i: (i, 0, 0, 0))],
                out_shape=[jax.ShapeDtypeStruct((t, D_RET), F32), jax.ShapeDtypeStruct((t, D_RET), MXU_DTYPE),
                           jax.ShapeDtypeStruct((n_chunks, nh, d, d), F32)],
                scratch_shapes=[pltpu.VMEM((nh, d, d), F32)],
                operands=[proj, proj, cos2, sin_signed, decay, xi, zeta, gain])


def _outproj_fwd(x, y_lru, y_ret, w_out, g2, comm):
    t = x.shape[0]
    tm = min(512, t)

    def body(x_ref, yl_ref, yr_ref, w_ref, g_ref, h1_ref, u2_ref):
        h1 = x_ref[...] + _dot(yl_ref[...], w_ref[:D_LRU, :]) + _dot(yr_ref[...], w_ref[D_LRU:, :])
        h1_ref[...] = h1
        _, _, u = _rms_fwd(h1, g_ref[...])
        u2_ref[...] = u.astype(u2_ref.dtype)

    row = lambda i: (i, 0)
    return _call(body, name="outproj_fwd", grid=(t // tm,), comm=comm,
                 in_specs=[pl.BlockSpec((tm, D_MODEL), row), pl.BlockSpec((tm, D_LRU), row), pl.BlockSpec((tm, D_RET), row),
                           _resident((D_MODEL, D_MODEL)), pl.BlockSpec((1, D_MODEL), lambda i: (0, 0))],
                 out_specs=[pl.BlockSpec((tm, D_MODEL), row), pl.BlockSpec((tm, D_MODEL), row)],
                 out_shape=[jax.ShapeDtypeStruct((t, D_MODEL), F32), jax.ShapeDtypeStruct((t, D_MODEL), MXU_DTYPE)],
                 operands=[x, y_lru, y_ret, w_out, g2])


FFN_TN = 768
FFN_NJ = D_FF // FFN_TN
FFN_GROUP = 4


def _ffn_fwd(u2, w_blocks, conv_w, conv_b, w_down, h1, gf, target):
    t = u2.shape[0]
    tm = min(256, t)
    tn, nj, group = FFN_TN, FFN_NJ, FFN_GROUP
    ng, tw = nj // group, group * tn
    hb = _halo_rows(u2.dtype)
    assert w_blocks.shape == (2 * nj, D_MODEL, tn)

    def project(u_ext, w, col, up_ref, conv_ref, cw_ref, cb_ref, first):
        ext = _dot(u_ext, w)
        x = ext[hb:, :]
        up_ref[:, col] = x.astype(up_ref.dtype)
        prev = jnp.where(first, 0.0, ext[hb - SUBLANES:hb, :])
        cw = cw_ref[:, col]
        y = cb_ref[:, col] + cw[FFN_CONV - 1:FFN_CONV, :] * x
        for k in range(FFN_CONV - 1):
            y = y + cw[k:k + 1, :] * _shift_down(prev, x, FFN_CONV - 1 - k)
        conv_ref[:, col] = y.astype(conv_ref.dtype)
        return y

    def body(u_ref, uh_ref, w_ref, cwa_ref, cwv_ref, cba_ref, cbv_ref, wd_ref, h1_ref, gf_ref, tg_ref,
             upa_ref, upv_ref, ca_ref, cv_ref, act_ref, dh_ref, dhb_ref, dgf_ref, loss_ref, acc):
        i, jg = pl.program_id(0), pl.program_id(1)

        @pl.when((i == 0) & (jg == 0))
        def _():
            dgf_ref[...] = jnp.zeros_like(dgf_ref)
            loss_ref[...] = jnp.zeros_like(loss_ref)

        @pl.when(jg == 0)
        def _():
            acc[...] = jnp.zeros_like(acc)

        u_ext = jnp.concatenate([uh_ref[...], u_ref[...]], axis=0)
        down = None
        for jj in range(group):
            col = slice(jj * tn, (jj + 1) * tn)
            j = jg * group + jj
            a = project(u_ext, w_ref[j], col, upa_ref, ca_ref, cwa_ref, cba_ref, i == 0)
            v = project(u_ext, w_ref[nj + j], col, upv_ref, cv_ref, cwv_ref, cbv_ref, i == 0)
            act = (_gelu(a) * v).astype(act_ref.dtype)
            act_ref[:, col] = act
            part = _dot(act, wd_ref[pl.ds(pl.multiple_of(j * tn, tn), tn), :])
            down = part if down is None else down + part
        acc[...] += down

        @pl.when(jg == ng - 1)
        def _():
            n, rstd, y = _rms_fwd(h1_ref[...] + acc[...], gf_ref[...])
            err = y - tg_ref[...]
            loss_ref[...] += (0.5 / D_MODEL) * jnp.sum(err * err)
            dh, dgf = _rms_bwd(err * (1.0 / D_MODEL), n, rstd, gf_ref[...])
            dgf_ref[...] += dgf
            dh_ref[...] = dh
            dhb_ref[...] = dh.astype(dhb_ref.dtype)

    per = tm // hb
    row = lambda i, j: (i, 0)
    const = lambda i, j: (0, 0)
    tile = pl.BlockSpec((tm, tw), lambda i, j: (i, j))
    return _call(body, name="ffn_fwd", grid=(t // tm, ng),
                 in_specs=[pl.BlockSpec((tm, D_MODEL), row),
                           pl.BlockSpec((hb, D_MODEL), lambda i, j: (jnp.maximum(i * per - 1, 0), 0)),
                           _resident(w_blocks.shape),
                           pl.BlockSpec((FFN_CONV, tw), lambda i, j: (0, j)),
                           pl.BlockSpec((FFN_CONV, tw), lambda i, j: (0, j + ng)),
                           pl.BlockSpec((1, tw), lambda i, j: (0, j)), pl.BlockSpec((1, tw), lambda i, j: (0, j + ng)),
                           _resident((D_FF, D_MODEL)),
                           pl.BlockSpec((tm, D_MODEL), row), pl.BlockSpec((1, D_MODEL), const),
                           pl.BlockSpec((tm, D_MODEL), row)],
                 out_specs=[tile] * 5 + [pl.BlockSpec((tm, D_MODEL), row),
                            pl.BlockSpec((tm, D_MODEL), row), pl.BlockSpec((SUBLANES, D_MODEL), const),
                            pl.BlockSpec((SUBLANES, LANES), const)],
                 out_shape=[jax.ShapeDtypeStruct((t, D_FF), MXU_DTYPE)] * 5 + [
                            jax.ShapeDtypeStruct((t, D_MODEL), F32),
                            jax.ShapeDtypeStruct((t, D_MODEL), MXU_DTYPE), jax.ShapeDtypeStruct((SUBLANES, D_MODEL), F32),
                            jax.ShapeDtypeStruct((SUBLANES, LANES), F32)],
                 scratch_shapes=[pltpu.VMEM((tm, D_MODEL), F32)],
                 operands=[u2, u2, w_blocks, conv_w, conv_w, conv_b, conv_b, w_down, h1, gf, target])


FFN_ACC_ROWS = SUBLANES * (FFN_CONV + 1)


def _ffn_bwd(dh2, dh2_b, w_down, up_a, up_v, conv_a, conv_v, conv_w, w_up_blocks, h1, g2, comm):
    t = up_a.shape[0]
    tm = min(256, t)
    tn, nj, group = FFN_TN, FFN_NJ, FFN_GROUP
    ng, tw = nj // group, group * tn
    ni = t // tm
    assert w_up_blocks.shape == (2 * nj, D_MODEL, tn)

    def conv_bwd(dy, x, cw, acc_ref, carry_ref, dup_ref, col):
        nxt = carry_ref[...]
        carry_ref[...] = dy[:SUBLANES, :]
        ahead = [_shift_up(dy, nxt, FFN_CONV - 1 - k) for k in range(FFN_CONV)]
        dx = cw[FFN_CONV - 1:FFN_CONV, :] * dy
        for k in range(FFN_CONV - 1):
            dx = dx + cw[k:k + 1, :] * ahead[k]
        dx = dx.astype(dup_ref.dtype)
        dup_ref[:, col] = dx
        for k in range(FFN_CONV):
            acc_ref[k * SUBLANES:(k + 1) * SUBLANES, :] += _colsum8(ahead[k] * x)
        acc_ref[FFN_CONV * SUBLANES:, :] += _colsum8(dy)
        return dx

    def body(dh_ref, dhb_ref, wd_ref, ua_ref, uv_ref, ca_ref, cv_ref, cwa_ref, cwv_ref, wu_ref, h1_ref, g2_ref,
             dua_ref, duv_ref, acca_ref, accv_ref, dh1_ref, dh1b_ref, dg2_ref, carry_a, carry_v, du):
        i, jg = pl.program_id(0), pl.program_id(1)

        @pl.when((i == 0) & (jg == 0))
        def _():
            for ref in (acca_ref, accv_ref, carry_a, carry_v, dg2_ref):
                ref[...] = jnp.zeros_like(ref)

        dhb = dhb_ref[...]
        part = None
        for jj in range(group):
            col = slice(jj * tn, (jj + 1) * tn)
            j = jg * group + jj
            v = cv_ref[:, col].astype(F32)
            g, dg = _gelu_parts(ca_ref[:, col].astype(F32))
            dact = _dot_nt(dhb, wd_ref[pl.ds(pl.multiple_of(j * tn, tn), tn), :])
            da = conv_bwd(dact * v * dg, ua_ref[:, col].astype(F32), cwa_ref[:, col], acca_ref.at[j], carry_a.at[j],
                          dua_ref, col)
            dv = conv_bwd(dact * g, uv_ref[:, col].astype(F32), cwv_ref[:, col], accv_ref.at[j], carry_v.at[j],
                          duv_ref, col)
            term = _dot_nt(da, wu_ref[j]) + _dot_nt(dv, wu_ref[nj + j])
            part = term if part is None else part + term

        @pl.when(jg == 0)
        def _():
            du[...] = part

        @pl.when(jg > 0)
        def _():
            du[...] += part

        @pl.when(jg == ng - 1)
        def _():
            n, rstd, _ = _rms_fwd(h1_ref[...], g2_ref[...])
            dh1, dg2 = _rms_bwd(du[...], n, rstd, g2_ref[...])
            dh1 = dh1 + dh_ref[...]
            dg2_ref[...] += dg2
            dh1_ref[...] = dh1
            dh1b_ref[...] = dh1.astype(dh1b_ref.dtype)

    row = lambda i, j: (ni - 1 - i, 0)
    const = lambda i, j: (0, 0)
    tile = pl.BlockSpec((tm, tw), lambda i, j: (ni - 1 - i, j))
    acc = pl.BlockSpec((nj, FFN_ACC_ROWS, tn), lambda i, j: (0, 0, 0))
    return _call(body, name="ffn_bwd", grid=(ni, ng), comm=comm,
                 in_specs=[pl.BlockSpec((tm, D_MODEL), row), pl.BlockSpec((tm, D_MODEL), row),
                           _resident((D_FF, D_MODEL)), tile, tile, tile, tile,
                           pl.BlockSpec((FFN_CONV, tw), lambda i, j: (0, j)),
                           pl.BlockSpec((FFN_CONV, tw), lambda i, j: (0, j + ng)),
                           _resident(w_up_blocks.shape), pl.BlockSpec((tm, D_MODEL), row),
                           pl.BlockSpec((1, D_MODEL), const)],
                 out_specs=[tile, tile, acc, acc, pl.BlockSpec((tm, D_MODEL), row), pl.BlockSpec((tm, D_MODEL), row),
                            pl.BlockSpec((SUBLANES, D_MODEL), const)],
                 out_shape=[jax.ShapeDtypeStruct((t, D_FF), MXU_DTYPE), jax.ShapeDtypeStruct((t, D_FF), MXU_DTYPE),
                            jax.ShapeDtypeStruct((nj, FFN_ACC_ROWS, tn), F32),
                            jax.ShapeDtypeStruct((nj, FFN_ACC_ROWS, tn), F32),
                            jax.ShapeDtypeStruct((t, D_MODEL), F32), jax.ShapeDtypeStruct((t, D_MODEL), MXU_DTYPE),
                            jax.ShapeDtypeStruct((SUBLANES, D_MODEL), F32)],
                 scratch_shapes=[pltpu.VMEM((nj, SUBLANES, tn), F32), pltpu.VMEM((nj, SUBLANES, tn), F32),
                                 pltpu.VMEM((tm, D_MODEL), F32)],
                 operands=[dh2, dh2_b, w_down, up_a, up_v, conv_a, conv_v, conv_w, conv_w, w_up_blocks, h1, g2])


def _norm_bwd_matmul(parts, w_blocks, h, gain, d_res, name, comm=None, tiles=None, into=None):
    t = h.shape[0]
    tm = min(256, t)
    first, count = tiles or (0, t // tm)
    n_parts = len(parts)
    nb, _, cb = w_blocks.shape
    where = []
    for p, a in enumerate(parts):
        assert a.shape[1] % cb == 0
        where += [(p, lo) for lo in range(0, a.shape[1], cb)]
    assert len(where) == nb
    n_in = n_parts + 4

    def body(*refs):
        a_refs = refs[:n_parts]
        w_ref, h_ref, g_ref, dres_ref = refs[n_parts:n_in]
        dh_ref, dhb_ref, dg_ref = refs[-3:]

        @pl.when(pl.program_id(0) == 0)
        def _():
            dg_ref[...] = jnp.zeros_like(dg_ref)

        du = None
        for d, (p, lo) in enumerate(where):
            term = _dot_nt(a_refs[p][:, lo:lo + cb], w_ref[d])
            du = term if du is None else du + term
        n, rstd, _ = _rms_fwd(h_ref[...], g_ref[...])
        dh, dg = _rms_bwd(du, n, rstd, g_ref[...])
        dh = dh + dres_ref[...]
        dg_ref[...] += dg
        dh_ref[...] = dh
        dhb_ref[...] = dh.astype(dhb_ref.dtype)

    row = lambda i: (i + first, 0)
    const = lambda i: (0, 0)
    in_specs = [pl.BlockSpec((tm, a.shape[1]), row) for a in parts] + [
        _resident(w_blocks.shape), pl.BlockSpec((tm, D_MODEL), row), pl.BlockSpec((1, D_MODEL), const),
        pl.BlockSpec((tm, D_MODEL), row)]
    operands = [*parts, w_blocks, h, gain, d_res]
    aliases = {}
    if into is not None:
        in_specs += [ANY, ANY]
        operands += list(into)
        aliases = {n_in: 0, n_in + 1: 1}
    return _call(body, name=name, grid=(count,), comm=comm, in_specs=in_specs,
                 out_specs=[pl.BlockSpec((tm, D_MODEL), row), pl.BlockSpec((tm, D_MODEL), row),
                            pl.BlockSpec((SUBLANES, D_MODEL), const)],
                 out_shape=[jax.ShapeDtypeStruct((t, D_MODEL), F32), jax.ShapeDtypeStruct((t, D_MODEL), MXU_DTYPE),
                            jax.ShapeDtypeStruct((SUBLANES, D_MODEL), F32)],
                 operands=operands, aliases=aliases)


def _ret_bwd(proj, cos2, sin_signed, gain, o, states, dmixed):
    t = proj.shape[0]
    c, d, nh = RET_CHUNK, RET_HEAD_DIM, RET_HEADS
    n_chunks = t // c
    per = RET_CHUNKS_PER_STEP if n_chunks % RET_CHUNKS_PER_STEP == 0 else 1
    rows = per * c
    n_steps = n_chunks // per
    decay, xi, zeta, g_chunk = _ret_consts()
    base = 2 * D_LRU

    def units(qk_ref, vg_ref, cos_ref, sin_ref, dec_ref, xi_ref, zeta_ref, gain_ref, o_ref, st_ref, dy_ref,
              dp_ref, dgain_ref, gstate):
        cur = [None] * nh

        def start():
            @pl.when(pl.program_id(0) == 0)
            def _():
                gstate[...] = jnp.zeros_like(gstate)
                dgain_ref[...] = jnp.zeros_like(dgain_ref)
            for h in range(nh):
                cur[h] = gstate[h]

        def gate_and_norm(s, h, keep):
            rs = slice(s * c, (s + 1) * c)
            lo = h * d
            g = vg_ref[rs, D_RET + lo:D_RET + lo + d]
            gain_h = gain_ref[:, lo:lo + d]
            dy = dy_ref[rs, lo:lo + d]
            sg = _sigmoid(g)
            o_h = o_ref[rs, lo:lo + d]
            oc = o_h - jnp.mean(o_h, axis=-1, keepdims=True)
            rstd = lax.rsqrt(jnp.mean(oc * oc, axis=-1, keepdims=True) + NORM_EPS)
            on = oc * rstd
            at = base + 3 * D_RET + lo
            dp_ref[rs, at:at + d] = (dy * on * gain_h * (sg * (1.0 + g * (1.0 - sg)))).astype(dp_ref.dtype)
            don_g = dy * (g * sg)
            dgain_ref[:, lo:lo + d] += _colsum8(don_g * on)
            don = don_g * gain_h
            keep["do"] = rstd * (don - jnp.mean(don, axis=-1, keepdims=True)
                                 - on * jnp.mean(don * on, axis=-1, keepdims=True))

        def retain(s, h, keep):
            rs = slice(s * c, (s + 1) * c)
            cos2, sin_s = cos_ref[rs, :], sin_ref[rs, :]
            lo = h * d
            q = _rope(qk_ref[rs, lo:lo + d], cos2, sin_s)
            k = _rope(qk_ref[rs, D_RET + lo:D_RET + lo + d], cos2, sin_s) * RET_SCALE
            v = vg_ref[rs, lo:lo + d]
            xi_h, zeta_h, dec = xi_ref[:, lo:lo + d], zeta_ref[:, lo:lo + d], dec_ref[h]
            do = keep["do"]
            s_prev = st_ref[s, h]
            g_next = cur[h]
            p = _dot_nt(q, k) * dec
            dpm = _dot_nt(do, v) * dec
            keep["dq"] = _dot(dpm, k) + _dot_nt(do, s_prev) * xi_h
            keep["dk"] = _dot_tn(dpm, q) + _dot_nt(v, g_next) * zeta_h
            dv = _dot_tn(p, do) + _dot(k * zeta_h, g_next)
            cur[h] = g_next * g_chunk[h] + _dot_tn(q * xi_h, do)
            at = base + 2 * D_RET + lo
            dp_ref[rs, at:at + d] = dv.astype(dp_ref.dtype)

        def unrope(s, h, keep):
            rs = slice(s * c, (s + 1) * c)
            cos2, sin_s = cos_ref[rs, :], sin_ref[rs, :]
            lo = h * d
            dp_ref[rs, base + lo:base + lo + d] = _rope_bwd(keep["dq"], cos2, sin_s).astype(dp_ref.dtype)
            at = base + D_RET + lo
            dp_ref[rs, at:at + d] = _rope_bwd(keep["dk"] * RET_SCALE, cos2, sin_s).astype(dp_ref.dtype)

        def end():
            for h in range(nh):
                gstate[h] = cur[h]

        pieces = [start]
        for s in reversed(range(per)):
            for h in range(nh):
                keep = {}
                pieces += [lambda s=s, h=h, keep=keep, f=f: f(s, h, keep) for f in (gate_and_norm, retain, unrope)]
        return pieces + [end]

    rev = lambda col: (lambda i: (n_steps - 1 - i, col))
    full2 = lambda i: (0, 0)
    return dict(units=units, grid=(n_steps,),
                in_specs=[pl.BlockSpec((rows, 2 * D_RET), rev(1)), pl.BlockSpec((rows, 2 * D_RET), rev(2)),
                          pl.BlockSpec((rows, d), rev(0)), pl.BlockSpec((rows, d), rev(0)),
                          pl.BlockSpec((nh, c, c), lambda i: (0, 0, 0)), pl.BlockSpec((c, D_RET), full2),
                          pl.BlockSpec((c, D_RET), full2), pl.BlockSpec((1, D_RET), full2),
                          pl.BlockSpec((rows, D_RET), rev(0)),
                          pl.BlockSpec((per, nh, d, d), lambda i: (n_steps - 1 - i, 0, 0, 0)),
                          pl.BlockSpec((rows, D_RET), rev(1))],
                out_specs=[pl.BlockSpec((rows, D_IN), rev(0)), pl.BlockSpec((SUBLANES, D_RET), full2)],
                out_shape=[jax.ShapeDtypeStruct((t, D_IN), MXU_DTYPE), jax.ShapeDtypeStruct((SUBLANES, D_RET), F32)],
                scratch_shapes=[pltpu.VMEM((nh, d, d), F32)],
                operands=[proj, proj, cos2, sin_signed, decay, xi, zeta, gain, o, states, dmixed])


LRU_ACC = {"conv_w": 0, "conv_b": LRU_CONV, "gate_a_b": LRU_CONV + 1, "gate_x_b": LRU_CONV + 2,
           "lambda": LRU_CONV + 3, "norm_gain": LRU_CONV + 4}
LRU_ACC_ROWS = SUBLANES * (LRU_CONV + 5)


def _lru_bwd(proj, xc_all, h_all, dmixed, conv_w, wa, ba, wx, bx, lam, gain, dproj_part):
    t = proj.shape[0]
    tm = min(256, t)
    c = D_LRU
    ni = t // tm

    def body(x_ref, xh_ref, g_ref, xc_ref, h_ref, hh_ref, dy_ref, cw_ref, wa_ref, ba_ref, wx_ref, bx_ref, lam_ref,
             gain_ref, acc_ref, dwa_ref, dwx_ref, a_scr, b_scr, mu_scr, carry_mu, carry_dxc, ctx):
        dp_ref = ctx["peers"][dproj_part][0]
        fill = ctx["fill"]
        i = pl.program_id(0)
        r = ni - 1 - i

        @pl.when(i == 0)
        def _():
            acc_ref[...] = jnp.zeros_like(acc_ref)
            dwa_ref[...] = jnp.zeros_like(dwa_ref)
            dwx_ref[...] = jnp.zeros_like(dwx_ref)
            carry_mu[...] = jnp.zeros_like(carry_mu)
            carry_dxc[...] = jnp.zeros_like(carry_dxc)

        def add(name, val, k=0):
            lo = (LRU_ACC[name] + k) * SUBLANES
            acc_ref[lo:lo + SUBLANES, :] += _colsum8(val)

        fill()
        xc, h = xc_ref[...], h_ref[...]
        lam_v = lam_ref[...]
        sp = _softplus(-lam_v)
        rg, ig, a, m = _lru_gates(xc, wa_ref[...], ba_ref[...], wx_ref[...], bx_ref[...], sp, fill)
        gl, dgl = _gelu_parts(g_ref[...])
        fill()
        zn, rstd, _ = _rms_fwd(h * gl, gain_ref[...])
        dy = dy_ref[...]
        dz, dgain = _rms_bwd(dy, zn, rstd, gain_ref[...])
        lo = LRU_ACC["norm_gain"] * SUBLANES
        acc_ref[lo:lo + SUBLANES, :] += dgain
        dp_ref[:, c:2 * c] = (dz * h * dgl).astype(dp_ref.dtype)
        dh = dz * gl
        fill()
        ga, gb = _group_scan(a, a * dh, reverse=True, fill=fill)
        a_scr[...] = ga
        b_scr[...] = gb
        mu_next_tile = carry_mu[...]
        carry_mu[...] = _carry_scan(a_scr, b_scr, mu_scr, mu_next_tile, reverse=True)
        fill()
        lam_t = dh + _shift_up(mu_scr[...], mu_next_tile, 1)
        h_prev = _shift_down(jnp.where(r == 0, 0.0, hh_ref[...]), h, 1)
        da = lam_t * h_prev
        dig = lam_t * m * xc
        dxc = lam_t * m * ig
        dlog_a = da * a - (lam_t * ig * xc) * (a * a) / m
        fill()
        dpr = dlog_a * ((-LRU_C) * sp) * rg * (1.0 - rg)
        add("lambda", dlog_a * ((-LRU_C) * rg) * (-_sigmoid(-lam_v)))
        dpi = dig * ig * (1.0 - ig)
        add("gate_a_b", dpr)
        add("gate_x_b", dpi)
        fill()
        dwa_ref[...] += _dot_tn(xc, dpr)
        dwx_ref[...] += _dot_tn(xc, dpi)
        dxc = dxc + _dot_nt(dpr, wa_ref[...]) + _dot_nt(dpi, wx_ref[...])
        fill()
        add("conv_b", dxc)
        x = x_ref[...]
        prev = jnp.where(r == 0, 0.0, xh_ref[...])
        cw = cw_ref[...]
        nxt = carry_dxc[...]
        carry_dxc[...] = dxc[:SUBLANES, :]
        dx = cw[LRU_CONV - 1:LRU_CONV, :] * dxc
        for k in range(LRU_CONV - 1):
            dx = dx + cw[k:k + 1, :] * _shift_up(dxc, nxt, LRU_CONV - 1 - k)
        fill()
        for k in range(LRU_CONV):
            add("conv_w", dxc * _shift_down(prev, x, LRU_CONV - 1 - k), k)
        dp_ref[:, :c] = dx.astype(dp_ref.dtype)

    per = tm // SUBLANES
    rev = lambda col: (lambda i: (ni - 1 - i, col))
    halo = lambda i: (jnp.maximum((ni - 1 - i) * per - 1, 0), 0)
    full = lambda i: (0, 0)
    vec = pl.BlockSpec((1, c), full)
    mat = pl.BlockSpec((c, c), full)
    return dict(body=body, grid=(ni,), fill_points=12,
                in_specs=[pl.BlockSpec((tm, c), rev(0)), pl.BlockSpec((SUBLANES, c), halo), pl.BlockSpec((tm, c), rev(1)),
                          pl.BlockSpec((tm, c), rev(0)), pl.BlockSpec((tm, c), rev(0)), pl.BlockSpec((SUBLANES, c), halo),
                          pl.BlockSpec((tm, c), rev(0)), pl.BlockSpec((LRU_CONV, c), full), mat, vec, mat, vec, vec, vec],
                out_specs=[pl.BlockSpec((LRU_ACC_ROWS, c), full), mat, mat],
                out_shape=[jax.ShapeDtypeStruct((LRU_ACC_ROWS, c), F32), jax.ShapeDtypeStruct((c, c), F32),
                           jax.ShapeDtypeStruct((c, c), F32)],
                scratch_shapes=[pltpu.VMEM((tm, c), F32), pltpu.VMEM((tm, c), F32), pltpu.VMEM((tm, c), F32),
                                pltpu.VMEM((SUBLANES, c), F32), pltpu.VMEM((SUBLANES, c), F32)],
                operands=[proj, proj, proj, xc_all, h_all, h_all, dmixed, conv_w, wa, ba, wx, bx, lam, gain])


def _pair_sum(core, a, b, name):
    n, r, c = b.shape
    spec = pl.BlockSpec((None, r, c), lambda q, core: (q, 0, 0))

    def body(core_ref, a_ref, b_ref, o_ref):
        o_ref[...] = (a_ref[...].astype(F32) + b_ref[...].astype(F32)).astype(o_ref.dtype)

    return pl.pallas_call(
        body, name=name,
        grid_spec=pltpu.PrefetchScalarGridSpec(
            num_scalar_prefetch=1, grid=(n,),
            in_specs=[pl.BlockSpec((None, r, c), lambda q, core: (2 * q + core[0], 0, 0)), spec], out_specs=spec),
        out_shape=jax.ShapeDtypeStruct(b.shape, b.dtype),
        compiler_params=pltpu.CompilerParams(dimension_semantics=("arbitrary",), vmem_limit_bytes=VMEM_LIMIT),
    )(core, a, b)


ADAMW_BLOCK_BYTES = 4 * 1024 * 1024


def _sum_adamw(parts, w, m, v, name):
    n_parts, r, c = parts.shape
    tr = r
    while n_parts * tr * c * parts.dtype.itemsize > ADAMW_BLOCK_BYTES and tr % (4 * SUBLANES) == 0:
        tr //= 2

    def body(p_ref, w_ref, m_ref, v_ref, g_ref, d_ref, nm_ref, nv_ref):
        g = p_ref[0].astype(F32)
        for s in range(1, n_parts):
            g = g + p_ref[s].astype(F32)
        nm = ADAM_B1 * m_ref[...] + (1.0 - ADAM_B1) * g
        nv = ADAM_B2 * v_ref[...] + (1.0 - ADAM_B2) * (g * g)
        m_hat = nm / (1.0 - ADAM_B1 ** ADAM_STEP)
        v_hat = nv / (1.0 - ADAM_B2 ** ADAM_STEP)
        g_ref[...] = g
        d_ref[...] = -ADAM_LR * (m_hat / (jnp.sqrt(v_hat) + ADAM_EPS) + ADAM_WD * w_ref[...])
        nm_ref[...] = nm
        nv_ref[...] = nv

    row = pl.BlockSpec((tr, c), lambda i: (i, 0))
    return _call(body, name=name, grid=(r // tr,),
                 in_specs=[pl.BlockSpec((n_parts, tr, c), lambda i: (0, i, 0)), row, row, row],
                 out_specs=[row, row, row, row], out_shape=[jax.ShapeDtypeStruct((r, c), F32)] * 4,
                 operands=[parts, w, m, v])


MATRICES = ("w_in", "w_out", "ffn_up_w", "ffn_down_w")
CONVS = ("lru_conv_w", "ffn_conv_w")
REPLICATED = ("norm1_gain", "lru_conv_b", "lru_gate_a_w", "lru_gate_a_b", "lru_gate_x_w", "lru_gate_x_b", "lru_lambda",
              "lru_norm_gain", "ret_norm_gain", "norm2_gain", "ffn_conv_b", "final_norm_gain")
WEIGHTS = ("norm1_gain", "w_in", "lru_conv_w", "lru_conv_b", "lru_gate_a_w", "lru_gate_a_b", "lru_gate_x_w",
           "lru_gate_x_b", "lru_lambda", "lru_norm_gain", "ret_norm_gain", "w_out", "norm2_gain", "ffn_up_w",
           "ffn_conv_w", "ffn_conv_b", "ffn_down_w", "final_norm_gain")


def _rows(a, pad_to):
    a = a.reshape(-1, LANES)
    pad = (-a.shape[0]) % pad_to
    return jnp.pad(a, ((0, pad), (0, 0))) if pad else a


def _pack(arrays, pad_to):
    rows, layout, at = [], [], 0
    for a in arrays:
        r = _rows(a, pad_to)
        layout.append((at, a.size // LANES, a.shape))
        rows.append(r)
        at += r.shape[0]
    return jnp.concatenate(rows, axis=0), layout


def _unpack(packed, layout):
    lead = packed.shape[:-2]
    return [packed[..., at:at + n, :].reshape(lead + shape) for at, n, shape in layout]


def _conv_rows(lru, ffn, dtype, pad_to):
    lead = lru.shape[:-2]
    flat = jnp.concatenate([lru.reshape(lead + (-1,)), ffn.reshape(lead + (-1,))], axis=-1).astype(dtype)
    rows = flat.shape[-1] // LANES
    pad = (-rows) % pad_to
    return jnp.pad(flat.reshape(lead + (rows, LANES)), [(0, 0)] * len(lead) + [(0, pad), (0, 0)])


def _column_blocks(full):
    r, c = full.shape
    return full.reshape(r, N_DEV, c // N_DEV).transpose(1, 0, 2)


def _block_diag(w):
    nh, d, _ = w.shape
    eye = jnp.eye(nh, dtype=w.dtype)
    return (w[:, :, None, :] * eye[:, None, :, None]).reshape(nh * d, nh * d)


def _diag_blocks(dense, nh):
    d = dense.shape[0] // nh
    blocks = dense.reshape(nh, d, nh, d)
    return jnp.stack([blocks[h, :, h, :] for h in range(nh)], axis=0)


def kernel(x, norm1_gain, w_in, lru_conv_w, lru_conv_b, lru_gate_a_w, lru_gate_a_b, lru_gate_x_w, lru_gate_x_b, lru_lambda, lru_norm_gain, ret_norm_gain, w_out, norm2_gain, ffn_up_w, ffn_conv_w, ffn_conv_b, ffn_down_w, final_norm_gain, loss_target, m_norm1_gain, m_w_in, m_lru_conv_w, m_lru_conv_b, m_lru_gate_a_w, m_lru_gate_a_b, m_lru_gate_x_w, m_lru_gate_x_b, m_lru_lambda, m_lru_norm_gain, m_ret_norm_gain, m_w_out, m_norm2_gain, m_ffn_up_w, m_ffn_conv_w, m_ffn_conv_b, m_ffn_down_w, m_final_norm_gain, v_norm1_gain, v_w_in, v_lru_conv_w, v_lru_conv_b, v_lru_gate_a_w, v_lru_gate_a_b, v_lru_gate_x_w, v_lru_gate_x_b, v_lru_lambda, v_lru_norm_gain, v_ret_norm_gain, v_w_out, v_norm2_gain, v_ffn_up_w, v_ffn_conv_w, v_ffn_conv_b, v_ffn_down_w, v_final_norm_gain):
    args = dict(locals())
    given = {n: args[n] for n in WEIGHTS}
    out_shape = {n: given[n].shape for n in WEIGHTS}

    def plain(a):
        return a.reshape(1, -1) if a.ndim <= 2 else a[0]

    w = {n: plain(given[n]) for n in WEIGHTS}
    mom_m = {n: plain(args["m_" + n]) for n in WEIGHTS}
    mom_v = {n: plain(args["v_" + n]) for n in WEIGHTS}
    x2, target = x[0], loss_target[0]
    t = x2.shape[0]
    core = lax.axis_index("c").astype(jnp.int32).reshape(1)
    res = {}

    conv_pad = _conv_rows(w["lru_conv_w"], w["ffn_conv_w"], F32, SUBLANES)
    first = _gather_first([w["w_in"].astype(MXU_DTYPE), conv_pad])
    w_in_blocks, conv_all = _run_comms([first, _gather_second(first.out_shape)], "w_in_all_gather")
    n_lru = w["lru_conv_w"].size
    conv_flat = conv_all.reshape(N_DEV, -1)
    lru_cw = conv_flat[:, :n_lru].reshape((N_DEV,) + w["lru_conv_w"].shape).transpose(1, 0, 2).reshape(LRU_CONV, D_LRU)
    ffn_cw = conv_flat[:, n_lru:n_lru + w["ffn_conv_w"].size].reshape((N_DEV,) + w["ffn_conv_w"].shape)
    ffn_cw = ffn_cw.transpose(1, 0, 2).reshape(FFN_CONV, 2 * D_FF)

    cos2, sin_signed = _rope_tables(t)
    wa = _block_diag(w["lru_gate_a_w"]).astype(MXU_DTYPE)
    wx = _block_diag(w["lru_gate_x_w"]).astype(MXU_DTYPE)
    gf = w["final_norm_gain"]

    early = _gather_first([w["ffn_up_w"].astype(MXU_DTYPE), w["w_out"].astype(MXU_DTYPE)])
    (u1, proj), (up_part, w_out_part) = _inproj_fwd(x2, w["norm1_gain"], w_in_blocks, early)
    ((xc, h_lru, y_lru), (o_ret, y_ret, states)), (up_blocks, w_out_blocks, down_part) = _fused(
        [_lru_fwd(proj, lru_cw, w["lru_conv_b"], wa, w["lru_gate_a_b"], wx, w["lru_gate_x_b"], w["lru_lambda"],
                  w["lru_norm_gain"]),
         _ret_fwd(proj, cos2, sin_signed, w["ret_norm_gain"])],
        "mix_fwd", _both(_gather_second([up_part, w_out_part]), _gather_first([w["ffn_down_w"].astype(MXU_DTYPE)])))
    w_out_full = w_out_blocks.reshape(D_MODEL, D_MODEL)

    (h1, u2), (down_blocks,) = _outproj_fwd(x2, y_lru, y_ret, w_out_full, w["norm2_gain"], _gather_second([down_part]))
    w_down_full = down_blocks.reshape(D_FF, D_MODEL)
    up_a, up_v, conv_a, conv_v, act, dh2, dh2_b, dgf, loss = _ffn_fwd(u2, up_blocks, ffn_cw, w["ffn_conv_b"], w_down_full,
                                                                      h1, gf, target)
    loss = lax.psum(loss[0, 0], ("x", "y", "c"))

    def to_owner_chips(blocks, names, tag):
        theirs = _run_comms([_pair_exchange(blocks)], "grads_pair_exchange_" + tag)
        return [_pair_sum(core, a, b, "grads_pair_sum_" + n) for n, a, b in zip(names, blocks, theirs)]

    def adamw(name, parts):
        res[name] = _sum_adamw(parts, w[name], mom_m[name], mom_v[name], "adamw_" + name)

    g = {"final_norm_gain": dgf[0]}
    g_down = _mm_tn(act, dh2_b, "ffn_down_wgrad").reshape(N_DEV, D_FF // N_DEV, D_MODEL)
    down_sums = to_owner_chips([g_down], ["ffn_down_w"], "down")
    (dup_a, dup_v, acc_a, acc_v, dh1, dh1_b, dg2), (down_parts,) = _ffn_bwd(
        dh2, dh2_b, w_down_full, up_a, up_v, conv_a, conv_v, ffn_cw, up_blocks, h1, w["norm2_gain"],
        _chip_exchange(down_sums))
    adamw("ffn_down_w", down_parts)
    per_col = lambda a: a[:, ::SUBLANES].transpose(1, 0, 2).reshape(FFN_CONV + 1, D_FF)
    acc = jnp.concatenate([per_col(acc_a), per_col(acc_v)], axis=1)
    g_ffn_cw, g["ffn_conv_b"] = acc[:FFN_CONV], acc[FFN_CONV:]
    g["norm2_gain"] = dg2[:1]
    g_up = jnp.concatenate([_mm_tn(u2, dup_a, "ffn_up_wgrad_a", blocks=N_DEV // 2),
                            _mm_tn(u2, dup_v, "ffn_up_wgrad_v", blocks=N_DEV // 2)], axis=0)
    g_out = jnp.concatenate([_mm_tn(y_lru, dh1_b, "w_out_wgrad_lru"), _mm_tn(y_ret, dh1_b, "w_out_wgrad_ret")], axis=0)
    mid_sums = to_owner_chips([g_up, g_out.reshape(N_DEV, D_MODEL // N_DEV, D_MODEL)], ["ffn_up_w", "w_out"], "mid")
    (dmixed,) = _mm_nt(dh1_b, w_out_full, "outproj_bwd", F32)
    ((dproj, dgain_ret), (lru_acc, dwa, dwx)), (up_parts, out_parts) = _fused(
        [_ret_bwd(proj, cos2, sin_signed, w["ret_norm_gain"], o_ret, states, dmixed),
         _lru_bwd(proj, xc, h_lru, dmixed, lru_cw, wa, w["lru_gate_a_b"], wx, w["lru_gate_x_b"], w["lru_lambda"],
                  w["lru_norm_gain"], dproj_part=0)],
        "mix_bwd", _chip_exchange(mid_sums))
    adamw("ffn_up_w", up_parts)
    adamw("w_out", out_parts)
    g["ret_norm_gain"] = dgain_ret[:1]
    lru_acc = lru_acc[::SUBLANES]
    g_lru_cw = lru_acc[:LRU_CONV]
    for name in ("conv_b", "gate_a_b", "gate_x_b", "lambda", "norm_gain"):
        g["lru_" + name] = lru_acc[LRU_ACC[name]:LRU_ACC[name] + 1]
    g["lru_gate_a_w"] = _diag_blocks(dwa, LRU_HEADS)
    g["lru_gate_x_w"] = _diag_blocks(dwx, LRU_HEADS)
    g_in = _mm_tn(u1, dproj, "w_in_wgrad", blocks=N_DEV)
    g_conv = _conv_rows(_column_blocks(g_lru_cw), _column_blocks(g_ffn_cw), GRAD_DTYPE, 2 * SUBLANES)
    in_sums = to_owner_chips([g_in, g_conv], ["w_in", "conv"], "in")
    half = t // min(256, t) // 2
    (gx_half, gxb_half, dg1_lo), (in_parts, conv_parts) = _norm_bwd_matmul(
        [dproj], w_in_blocks, x2, w["norm1_gain"], dh1, "inproj_bwd_lo", _chip_exchange(in_sums), tiles=(0, half))
    grad_x, _, dg1_hi = _norm_bwd_matmul([dproj], w_in_blocks, x2, w["norm1_gain"], dh1, "inproj_bwd_hi",
                                         tiles=(half, t // min(256, t) - half), into=(gx_half, gxb_half))
    adamw("w_in", in_parts)
    g["norm1_gain"] = dg1_lo[:1] + dg1_hi[:1]

    pad16 = lambda d: _conv_rows(d["lru_conv_w"], d["ffn_conv_w"], F32, 2 * SUBLANES)
    conv_res = _sum_adamw(conv_parts, pad16(w), pad16(mom_m), pad16(mom_v), "adamw_conv")
    for n, lo, hi in (("lru_conv_w", 0, n_lru), ("ffn_conv_w", n_lru, n_lru + w["ffn_conv_w"].size)):
        res[n] = [r.reshape(-1)[lo:hi].reshape(w[n].shape) for r in conv_res]
    rep_packed, rep_layout = _pack([g[n] for n in REPLICATED], SUBLANES)
    rep_first = _gather_first([rep_packed])
    (rep_parts,) = _run_comms([rep_first, _gather_second(rep_first.out_shape)], "small_grads_all_gather")
    rep_res = _sum_adamw(rep_parts, *[_pack([d[n] for n in REPLICATED], SUBLANES)[0] for d in (w, mom_m, mom_v)],
                         "adamw_replicated")
    for k in range(4):
        for n, a in zip(REPLICATED, _unpack(rep_res[k], rep_layout)):
            res.setdefault(n, [None] * 4)[k] = a

    outs = [loss, grad_x[None]]
    for k in range(4):
        outs += [res[n][k].reshape(out_shape[n]) for n in WEIGHTS]
    return tuple(outs)
```

```python
import math

import numpy as np
import jax
import jax.numpy as jnp
from jax import lax
from jax.experimental import pallas as pl
from jax.experimental.pallas import tpu as pltpu

F32 = jnp.float32
BF16 = jnp.bfloat16
MXU_DTYPE = jnp.bfloat16
GRAD_DTYPE = jnp.bfloat16

N_DEV = 8
N_CHIPS = 4
D_MODEL = 1024
D_LRU = 512
LRU_HEADS = 8
LRU_CONV = 4
LRU_C = 8.0
D_RET = 512
RET_HEADS = 4
RET_HEAD_DIM = 128
RET_CHUNK = 128
ROPE_BASE = 10000.0
D_IN = 3072
D_FF = 3072
FFN_CONV = 3
NORM_EPS = 1e-6

ADAM_LR = 0.001
ADAM_B1 = 0.9
ADAM_B2 = 0.999
ADAM_EPS = 1e-08
ADAM_WD = 0.01
ADAM_STEP = 10

SUBLANES = 8
LANES = 128
VMEM_LIMIT = 48 * 1024 * 1024

MESH = pl.DeviceIdType.MESH
ANY = pl.BlockSpec(memory_space=pl.ANY)


def _dot(a, b):
    return jnp.dot(a.astype(MXU_DTYPE), b.astype(MXU_DTYPE), preferred_element_type=F32)


def _dot_nt(a, b):
    return lax.dot_general(a.astype(MXU_DTYPE), b.astype(MXU_DTYPE), (((1,), (1,)), ((), ())),
                           preferred_element_type=F32)


def _dot_tn(a, b):
    return lax.dot_general(a.astype(MXU_DTYPE), b.astype(MXU_DTYPE), (((0,), (0,)), ((), ())),
                           preferred_element_type=F32)


def _sigmoid(x):
    return 0.5 + 0.5 * jnp.tanh(0.5 * x)


_GELU_C = math.sqrt(2.0 / math.pi)


def _gelu_parts(x):
    x2 = x * x
    t = jnp.tanh(_GELU_C * (x + 0.044715 * (x2 * x)))
    cdf = 0.5 * (1.0 + t)
    g = x * cdf
    dg = cdf + 0.5 * x * (1.0 - t * t) * (_GELU_C * (1.0 + 3.0 * 0.044715 * x2))
    return g, dg


def _gelu(x):
    t = jnp.tanh(_GELU_C * (x + 0.044715 * (x * x * x)))
    return x * (0.5 * (1.0 + t))


def _softplus(x):
    return jnp.maximum(x, 0.0) + jnp.log1p(jnp.exp(-jnp.abs(x)))


def _bcast_row(x, r, rows=SUBLANES):
    return jnp.broadcast_to(x[r:r + 1, :], (rows, x.shape[1]))


def _colsum8(x):
    return jnp.broadcast_to(jnp.sum(x, axis=0, keepdims=True), (SUBLANES, x.shape[1]))


def _shift_down(prev8, tile, s):
    if s == 0:
        return tile
    ext = jnp.concatenate([prev8, tile], axis=0)
    return pltpu.roll(ext, s, 0)[SUBLANES:, :]


def _shift_up(tile, next8, s):
    if s == 0:
        return tile
    ext = jnp.concatenate([tile, next8], axis=0)
    return pltpu.roll(ext, SUBLANES - s, 0)[SUBLANES:, :]


def _group_scan(a, b, reverse, fill=lambda: None):
    n = a.shape[0]
    row = lax.broadcasted_iota(jnp.int32, a.shape, 0) & (SUBLANES - 1)
    for s in (1, 2, 4):
        if s > 1:
            fill()
        shift = (n - s) if reverse else s
        a_sh = pltpu.roll(a, shift, 0)
        b_sh = pltpu.roll(b, shift, 0)
        m = (row <= SUBLANES - 1 - s) if reverse else (row >= s)
        b = jnp.where(m, a * b_sh + b, b)
        a = jnp.where(m, a * a_sh, a)
    return a, b


def _carry_scan(a_ref, b_ref, out_ref, carry0, reverse):
    n_groups = a_ref.shape[0] // SUBLANES
    carry = carry0
    for i in range(n_groups):
        r0 = ((n_groups - 1 - i) if reverse else i) * SUBLANES
        hg = a_ref[r0:r0 + SUBLANES, :] * carry + b_ref[r0:r0 + SUBLANES, :]
        out_ref[r0:r0 + SUBLANES, :] = hg
        carry = _bcast_row(hg, 0 if reverse else SUBLANES - 1)
    return carry


def _rms_fwd(h, gain):
    rstd = lax.rsqrt(jnp.mean(h * h, axis=-1, keepdims=True) + NORM_EPS)
    n = h * rstd
    return n, rstd, n * gain


def _rms_bwd(dy, n, rstd, gain):
    dn = dy * gain
    dh = rstd * (dn - n * jnp.mean(dn * n, axis=-1, keepdims=True))
    return dh, _colsum8(dy * n)


def _halo_rows(dtype):
    return SUBLANES * (4 // jnp.dtype(dtype).itemsize)


def _halo_map(tile_rows, col, halo_rows=SUBLANES):
    per = tile_rows // halo_rows
    return lambda i: (jnp.maximum(i * per - 1, 0), col)


def _resident(shape):
    return pl.BlockSpec(shape, lambda *_: (0,) * len(shape), pipeline_mode=pl.Buffered(1))


def _place():
    x, y, c = lax.axis_index("x"), lax.axis_index("y"), lax.axis_index("c")
    chips = [(1 - x, y), (x, 1 - y), (1 - x, 1 - y)]
    return x, y, c, chips


def _dev(x, y, c):
    return 4 * x + 2 * y + c


def _remote(src, dst, send_sem, recv_sem, to):
    return pltpu.make_async_remote_copy(src_ref=src, dst_ref=dst, send_sem=send_sem, recv_sem=recv_sem,
                                        device_id=to, device_id_type=MESH)


class _Comm:
    def __init__(self, operands, out_shape, sems, descs, aliases=()):
        self.operands, self.out_shape, self.sems, self.descs, self.aliases = operands, out_shape, sems, descs, aliases

    def start(self, ins, outs, sems):
        local, sends, _ = self.descs(ins, outs, sems)
        for cp in sends + local:
            cp.start()

    def wait(self, ins, outs, sems):
        local, sends, recvs = self.descs(ins, outs, sems)
        for cp in recvs:
            cp.wait_recv()
        for cp in sends:
            cp.wait_send()
        for cp in local:
            cp.wait()


def _gather_first(shards):
    n = len(shards)

    def descs(ins, outs, sems):
        send, recv, loc = sems
        x, y, c, chips = _place()
        me = _dev(x, y, c)
        targets = [(x, y, 1 - c)] + [(*chip, c) for chip in chips]
        local, sends, recvs = [], [], []
        for t in range(n):
            local.append(pltpu.make_async_copy(ins[t], outs[t].at[me], loc.at[t]))
            for k, to in enumerate(targets):
                i = 4 * t + k
                sends.append(_remote(ins[t], outs[t].at[me], send.at[i], recv.at[i], to))
                recvs.append(_remote(ins[t], outs[t].at[_dev(*to)], send.at[i], recv.at[i], to))
        return local, sends, recvs

    return _Comm(list(shards), [jax.ShapeDtypeStruct((N_DEV,) + s.shape, s.dtype) for s in shards],
                 [pltpu.SemaphoreType.DMA((4 * n,)), pltpu.SemaphoreType.DMA((4 * n,)), pltpu.SemaphoreType.DMA((n,))],
                 descs)


def _gather_second(gathered):
    n = len(gathered)

    def descs(ins, outs, sems):
        send, recv = sems
        x, y, c, chips = _place()
        sends, recvs = [], []
        for t in range(n):
            for j, chip in enumerate(chips):
                i = 3 * t + j
                have, get = _dev(*chip, c), _dev(*chip, 1 - c)
                sends.append(_remote(outs[t].at[have], outs[t].at[have], send.at[i], recv.at[i], (x, y, 1 - c)))
                recvs.append(_remote(outs[t].at[have], outs[t].at[get], send.at[i], recv.at[i], (x, y, 1 - c)))
        return [], sends, recvs

    return _Comm(list(gathered), [jax.ShapeDtypeStruct(g.shape, g.dtype) for g in gathered],
                 [pltpu.SemaphoreType.DMA((3 * n,)), pltpu.SemaphoreType.DMA((3 * n,))], descs,
                 aliases=[(t, t) for t in range(n)])


def _pair_exchange(blocks):
    n = len(blocks)

    def descs(ins, outs, sems):
        send, recv = sems
        x, y, c, _ = _place()
        sends, recvs = [], []
        for t in range(n):
            for q in range(N_CHIPS):
                i = N_CHIPS * t + q
                cp = _remote(ins[t].at[2 * q + 1 - c], outs[t].at[q], send.at[i], recv.at[i], (x, y, 1 - c))
                sends.append(cp)
                recvs.append(cp)
        return [], sends, recvs

    return _Comm(list(blocks), [jax.ShapeDtypeStruct((N_CHIPS,) + b.shape[1:], b.dtype) for b in blocks],
                 [pltpu.SemaphoreType.DMA((N_CHIPS * n,)), pltpu.SemaphoreType.DMA((N_CHIPS * n,))], descs)


def _chip_exchange(blocks):
    n = len(blocks)

    def descs(ins, outs, sems):
        send, recv, loc = sems
        x, y, c, chips = _place()
        me = 2 * x + y
        local, sends, recvs = [], [], []
        for t in range(n):
            local.append(pltpu.make_async_copy(ins[t].at[me], outs[t].at[me], loc.at[t]))
            for j, (px, py) in enumerate(chips):
                i = 3 * t + j
                q = 2 * px + py
                sends.append(_remote(ins[t].at[q], outs[t].at[me], send.at[i], recv.at[i], (px, py, c)))
                recvs.append(_remote(ins[t].at[q], outs[t].at[q], send.at[i], recv.at[i], (px, py, c)))
        return local, sends, recvs

    return _Comm(list(blocks), [jax.ShapeDtypeStruct(b.shape, b.dtype) for b in blocks],
                 [pltpu.SemaphoreType.DMA((3 * n,)), pltpu.SemaphoreType.DMA((3 * n,)), pltpu.SemaphoreType.DMA((n,))],
                 descs)


def _both(a, b):
    na, oa, sa = len(a.operands), len(a.out_shape), len(a.sems)

    def descs(ins, outs, sems):
        local_a, sends_a, recvs_a = a.descs(ins[:na], outs[:oa], sems[:sa])
        local_b, sends_b, recvs_b = b.descs(ins[na:], outs[oa:], sems[sa:])
        return local_a + local_b, sends_a + sends_b, recvs_a + recvs_b

    return _Comm(a.operands + b.operands, a.out_shape + b.out_shape, a.sems + b.sems, descs,
                 aliases=list(a.aliases) + [(na + i, oa + o) for i, o in b.aliases])


def _offset(comm, lo):
    return _Comm(comm.operands, comm.out_shape, comm.sems,
                 lambda ins, outs, sems: comm.descs(ins[lo:], outs[lo:], sems), comm.aliases)


def _run_comms(comms, name):
    first = comms[0]
    n_in, n_out = len(first.operands), len(first.out_shape)

    def body(*refs):
        ins, outs, sems = refs[:n_in], refs[n_in:n_in + n_out], list(refs[n_in + n_out:])
        for k, comm in enumerate(comms):
            mine = [sems.pop(0) for _ in comm.sems]
            comm.start(ins if k == 0 else outs, outs, mine)
            comm.wait(ins if k == 0 else outs, outs, mine)

    outs = pl.pallas_call(
        body, name=name, out_shape=first.out_shape, in_specs=[ANY] * n_in, out_specs=[ANY] * n_out,
        scratch_shapes=[s for comm in comms for s in comm.sems],
    )(*first.operands)
    return list(outs)


def _call(body, *, name, grid, in_specs, out_specs, out_shape, operands, scratch_shapes=(), comm=None, aliases=None):
    sem = ("arbitrary",) * len(grid)
    params = pltpu.CompilerParams(dimension_semantics=sem, vmem_limit_bytes=VMEM_LIMIT)
    aliases = dict(aliases or {})
    if comm is None:
        return pl.pallas_call(body, name=name, grid=grid, in_specs=in_specs, out_specs=out_specs, out_shape=out_shape,
                              scratch_shapes=list(scratch_shapes), input_output_aliases=aliases,
                              compiler_params=params)(*operands)
    n_in, n_out, n_scr = len(in_specs), len(out_specs), len(scratch_shapes)
    c_in, c_out = len(comm.operands), len(comm.out_shape)

    def wrapped(*refs):
        refs = list(refs)
        ins, refs = refs[:n_in], refs[n_in:]
        cins, refs = refs[:c_in], refs[c_in:]
        outs, refs = refs[:n_out], refs[n_out:]
        couts, refs = refs[:c_out], refs[c_out:]
        scr, csems = refs[:n_scr], refs[n_scr:]
        first = last = None
        for axis, size in enumerate(grid):
            at_first, at_last = pl.program_id(axis) == 0, pl.program_id(axis) == size - 1
            first = at_first if first is None else first & at_first
            last = at_last if last is None else last & at_last

        @pl.when(first)
        def _():
            comm.start(cins, couts, csems)

        body(*ins, *outs, *scr)

        @pl.when(last)
        def _():
            comm.wait(cins, couts, csems)

    res = pl.pallas_call(
        wrapped, name=name, grid=grid, in_specs=list(in_specs) + [ANY] * c_in, out_specs=list(out_specs) + [ANY] * c_out,
        out_shape=list(out_shape) + list(comm.out_shape), scratch_shapes=list(scratch_shapes) + list(comm.sems),
        input_output_aliases={**aliases, **{n_in + i: n_out + o for i, o in comm.aliases}}, compiler_params=params,
    )(*operands, *comm.operands)
    return list(res[:n_out]), list(res[n_out:])


def _mm_nt(a, b, name, out_dtype, comm=None, tm=512, tn=512):
    m, k = a.shape
    n = b.shape[0]
    tm, tn = min(tm, m), min(tn, n)

    def body(a_ref, b_ref, o_ref):
        o_ref[...] = _dot_nt(a_ref[...], b_ref[...]).astype(o_ref.dtype)

    return _call(body, name=name, grid=(n // tn, m // tm), comm=comm,
                 in_specs=[pl.BlockSpec((tm, k), lambda j, i: (i, 0)), pl.BlockSpec((tn, k), lambda j, i: (j, 0))],
                 out_specs=[pl.BlockSpec((tm, tn), lambda j, i: (i, j))],
                 out_shape=[jax.ShapeDtypeStruct((m, n), out_dtype)], operands=[a, b])


def _mm_tn(a, b, name, blocks=1, tk=2048, comm=None):
    t, m = a.shape
    n = b.shape[1]
    tk = min(tk, t)
    nk = t // tk
    cb = n // blocks
    per = max(1, 768 // cb) if blocks > 1 else 1
    tn = per * cb if blocks > 1 else min(1024, n)
    tm = min(1024, m)
    assert blocks == 1 or tm == m

    def body(a_ref, b_ref, o_ref, acc):
        k = pl.program_id(2)

        @pl.when(k == 0)
        def _():
            acc[...] = jnp.zeros_like(acc)
        acc[...] += _dot_tn(a_ref[...], b_ref[...])

        @pl.when(k == nk - 1)
        def _():
            if blocks == 1:
                o_ref[...] = acc[...].astype(o_ref.dtype)
            else:
                for s in range(per):
                    o_ref[s] = acc[:, s * cb:(s + 1) * cb].astype(o_ref.dtype)

    if blocks == 1:
        out_spec = pl.BlockSpec((tm, tn), lambda i, j, k: (i, j))
        out_shape = jax.ShapeDtypeStruct((m, n), GRAD_DTYPE)
    else:
        out_spec = pl.BlockSpec((per, m, cb), lambda i, j, k: (j, 0, 0))
        out_shape = jax.ShapeDtypeStruct((blocks, m, cb), GRAD_DTYPE)
    res = _call(body, name=name, grid=(m // tm, n // tn, nk), comm=comm,
                in_specs=[pl.BlockSpec((tk, tm), lambda i, j, k: (k, i)), pl.BlockSpec((tk, tn), lambda i, j, k: (k, j))],
                out_specs=[out_spec], out_shape=[out_shape], operands=[a, b],
                scratch_shapes=[pltpu.VMEM((tm, tn), F32)])
    return res[0] if comm is None else (res[0][0], res[1])


def _inproj_fwd(x, g1, w_blocks, comm):
    t = x.shape[0]
    tm = min(512, t)
    nb, _, cb = w_blocks.shape

    def body(x_ref, g_ref, w_ref, u_ref, p_ref):
        _, _, u = _rms_fwd(x_ref[...], g_ref[...])
        u = u.astype(MXU_DTYPE)
        u_ref[...] = u
        for d in range(nb):
            p_ref[:, d * cb:(d + 1) * cb] = _dot(u, w_ref[d])

    return _call(body, name="inproj_fwd", grid=(t // tm,), comm=comm,
                 in_specs=[pl.BlockSpec((tm, D_MODEL), lambda i: (i, 0)), pl.BlockSpec((1, D_MODEL), lambda i: (0, 0)),
                           _resident(w_blocks.shape)],
                 out_specs=[pl.BlockSpec((tm, D_MODEL), lambda i: (i, 0)), pl.BlockSpec((tm, D_IN), lambda i: (i, 0))],
                 out_shape=[jax.ShapeDtypeStruct((t, D_MODEL), MXU_DTYPE), jax.ShapeDtypeStruct((t, D_IN), F32)],
                 operands=[x, g1, w_blocks])


def _lru_gates(xc, wa, ba, wx, bx, sp, fill=lambda: None):
    r = _sigmoid(_dot(xc, wa) + ba)
    fill()
    ig = _sigmoid(_dot(xc, wx) + bx)
    fill()
    log_a = (-LRU_C) * r * sp
    a = jnp.exp(log_a)
    m = jnp.sqrt(-jnp.tanh(log_a) * (a * a + 1.0))
    return r, ig, a, m


def _fused(parts, name, comm=None):
    grid = parts[0]["grid"]
    assert all(p["grid"] == grid for p in parts)
    counts = [(len(p["in_specs"]), len(p["out_specs"]), len(p.get("scratch_shapes", ()))) for p in parts]

    def body(*refs):
        refs = list(refs)
        groups = []
        for kind in range(3):
            taken = []
            for c in counts:
                taken.append(refs[:c[kind]])
                refs = refs[c[kind]:]
            groups.append(taken)
        ins, outs, scr = groups
        pending = []

        def fill(n=None):
            for _ in range(share if n is None else n):
                if pending:
                    pending.pop(0)()

        ctx = dict(outs=outs, scratch=scr, fill=fill)
        run = lambda key: [p[key](*ins[k], *outs[k], *scr[k], ctx) for k, p in enumerate(parts) if key in p]
        run("head")
        for pieces in run("units"):
            pending.extend(pieces)
        points = sum(p.get("fill_points", 0) for p in parts)
        share = -(-len(pending) // max(points, 1))
        run("body")
        fill(len(pending))
        run("tail")

    cat = lambda key: [x for p in parts for x in p.get(key, ())]
    res = _call(body, name=name, grid=grid, comm=comm, in_specs=cat("in_specs"), out_specs=cat("out_specs"),
                out_shape=cat("out_shape"), scratch_shapes=cat("scratch_shapes"), operands=cat("operands"))
    outs, side = (res if comm is not None else (res, None))
    split, at = [], 0
    for _, n_out, _ in counts:
        split.append(list(outs[at:at + n_out]))
        at += n_out
    return split if comm is None else (split, side)


def _lru_fwd(proj, conv_w, conv_b, wa, ba, wx, bx, lam, gain):
    t = proj.shape[0]
    tm = min(256, t)
    c = D_LRU

    def body(x_ref, xh_ref, g_ref, cw_ref, cb_ref, wa_ref, ba_ref, wx_ref, bx_ref, lam_ref, gain_ref,
             xc_ref, h_ref, y_ref, a_scr, b_scr, carry, ctx):
        fill = ctx["fill"]
        i = pl.program_id(0)

        @pl.when(i == 0)
        def _():
            carry[...] = jnp.zeros_like(carry)

        fill()
        x = x_ref[...]
        prev = jnp.where(i == 0, 0.0, xh_ref[...])
        cw = cw_ref[...]
        xc = cb_ref[...] + cw[LRU_CONV - 1:LRU_CONV, :] * x
        for k in range(LRU_CONV - 1):
            xc = xc + cw[k:k + 1, :] * _shift_down(prev, x, LRU_CONV - 1 - k)
        xc_ref[...] = xc
        fill()
        sp = _softplus(-lam_ref[...])
        _, ig, a, m = _lru_gates(xc, wa_ref[...], ba_ref[...], wx_ref[...], bx_ref[...], sp, fill)
        fill()
        ga, gb = _group_scan(a, m * (ig * xc), reverse=False, fill=fill)
        a_scr[...] = ga
        b_scr[...] = gb
        fill()
        carry[...] = _carry_scan(a_scr, b_scr, h_ref, carry[...], reverse=False)
        fill()
        z = h_ref[...] * _gelu(g_ref[...])
        fill()
        _, _, y = _rms_fwd(z, gain_ref[...])
        y_ref[...] = y.astype(y_ref.dtype)

    row = lambda i: (i, 0)
    full = lambda i: (0, 0)
    vec = pl.BlockSpec((1, c), full)
    return dict(body=body, grid=(t // tm,), fill_points=10,
                in_specs=[pl.BlockSpec((tm, c), row), pl.BlockSpec((SUBLANES, c), _halo_map(tm, 0)),
                          pl.BlockSpec((tm, c), lambda i: (i, 1)),
                          pl.BlockSpec((LRU_CONV, c), full), vec, pl.BlockSpec((c, c), full), vec,
                          pl.BlockSpec((c, c), full), vec, vec, vec],
                out_specs=[pl.BlockSpec((tm, c), row), pl.BlockSpec((tm, c), row), pl.BlockSpec((tm, c), row)],
                out_shape=[jax.ShapeDtypeStruct((t, c), F32), jax.ShapeDtypeStruct((t, c), F32),
                           jax.ShapeDtypeStruct((t, c), MXU_DTYPE)],
                scratch_shapes=[pltpu.VMEM((tm, c), F32), pltpu.VMEM((tm, c), F32), pltpu.VMEM((SUBLANES, c), F32)],
                operands=[proj, proj, proj, conv_w, conv_b, wa, ba, wx, bx, lam, gain])


def _ret_consts():
    c = RET_CHUNK
    log_g = jnp.log1p(-jnp.exp2(-5.0 - jnp.arange(RET_HEADS, dtype=F32)))
    idx = jnp.arange(c, dtype=F32)
    diff = idx[:, None] - idx[None, :]
    decay = jnp.where(diff[None] >= 0, jnp.exp(jnp.maximum(diff, 0.0)[None] * log_g[:, None, None]), 0.0)
    zeta = jnp.exp((c - 1 - idx)[None, :] * log_g[:, None])
    xi = jnp.exp((idx + 1.0)[None, :] * log_g[:, None])
    spread = lambda v: jnp.repeat(v.T, RET_HEAD_DIM, axis=1)
    log_g_np = np.log1p(-np.exp2(-5.0 - np.arange(RET_HEADS, dtype=np.float32))).astype(np.float32)
    g_chunk = [float(np.exp(np.float32(c) * lg)) for lg in log_g_np]
    return decay, spread(xi), spread(zeta), g_chunk


def _rope_tables(t):
    pos = np.arange(t, dtype=np.float32)
    inv_freq = np.float32(ROPE_BASE) ** (-np.arange(0, RET_HEAD_DIM, 2, dtype=np.float32) / np.float32(RET_HEAD_DIM))
    ang = (pos[:, None] * inv_freq.astype(np.float32)[None, :]).astype(np.float32).astype(np.float64)
    cos, sin = np.cos(ang).astype(np.float32), np.sin(ang).astype(np.float32)
    return jnp.asarray(np.concatenate([cos, cos], axis=-1)), jnp.asarray(np.concatenate([-sin, sin], axis=-1))


def _rope(x, cos2, sin_signed):
    return x * cos2 + pltpu.roll(x, RET_HEAD_DIM // 2, 1) * sin_signed


def _rope_bwd(d, cos2, sin_signed):
    return d * cos2 + pltpu.roll(d * sin_signed, RET_HEAD_DIM // 2, 1)


RET_SCALE = RET_HEAD_DIM ** -0.5


RET_CHUNKS_PER_STEP = 2


def _ret_fwd(proj, cos2, sin_signed, gain):
    t = proj.shape[0]
    c, d, nh = RET_CHUNK, RET_HEAD_DIM, RET_HEADS
    n_chunks = t // c
    per = RET_CHUNKS_PER_STEP if n_chunks % RET_CHUNKS_PER_STEP == 0 else 1
    rows = per * c
    decay, xi, zeta, g_chunk = _ret_consts()

    def units(qk_ref, vg_ref, cos_ref, sin_ref, dec_ref, xi_ref, zeta_ref, gain_ref, o_ref, y_ref, st_ref, state, ctx):
        cur = [None] * nh

        def start():
            @pl.when(pl.program_id(0) == 0)
            def _():
                state[...] = jnp.zeros_like(state)
            for h in range(nh):
                cur[h] = state[h]

        def retain(s, h, keep):
            rs = slice(s * c, (s + 1) * c)
            cos2, sin_s = cos_ref[rs, :], sin_ref[rs, :]
            lo = h * d
            q = _rope(qk_ref[rs, lo:lo + d], cos2, sin_s)
            k = _rope(qk_ref[rs, D_RET + lo:D_RET + lo + d], cos2, sin_s) * RET_SCALE
            v = vg_ref[rs, lo:lo + d]
            s_prev = cur[h]
            st_ref[s, h] = s_prev
            scores = _dot_nt(q, k) * dec_ref[h]
            o = _dot(scores, v) + _dot(q * xi_ref[:, lo:lo + d], s_prev)
            cur[h] = s_prev * g_chunk[h] + _dot_tn(k * zeta_ref[:, lo:lo + d], v)
            o_ref[rs, lo:lo + d] = o
            keep["o"] = o

        def normalise(s, h, keep):
            rs = slice(s * c, (s + 1) * c)
            lo = h * d
            o = keep["o"]
            g = vg_ref[rs, D_RET + lo:D_RET + lo + d]
            mu = jnp.mean(o, axis=-1, keepdims=True)
            oc = o - mu
            on = oc * lax.rsqrt(jnp.mean(oc * oc, axis=-1, keepdims=True) + NORM_EPS)
            y_ref[rs, lo:lo + d] = (on * gain_ref[:, lo:lo + d] * (g * _sigmoid(g))).astype(y_ref.dtype)

        def end():
            for h in range(nh):
                state[h] = cur[h]

        pieces = [start]
        for s in range(per):
            for h in range(nh):
                keep = {}
                pieces += [lambda s=s, h=h, keep=keep: retain(s, h, keep),
                           lambda s=s, h=h, keep=keep: normalise(s, h, keep)]
        return pieces + [end]

    full2 = lambda i: (0, 0)
    return dict(units=units, grid=(n_chunks // per,),
                in_specs=[pl.BlockSpec((rows, 2 * D_RET), lambda i: (i, 1)),
                          pl.BlockSpec((rows, 2 * D_RET), lambda i: (i, 2)),
                          pl.BlockSpec((rows, d), lambda i: (i, 0)), pl.BlockSpec((rows, d), lambda i: (i, 0)),
                          pl.BlockSpec((nh, c, c), lambda i: (0, 0, 0)), pl.BlockSpec((c, D_RET), full2),
                          pl.BlockSpec((c, D_RET), full2), pl.BlockSpec((1, D_RET), full2)],
                out_specs=[pl.BlockSpec((rows, D_RET), lambda i: (i, 0)), pl.BlockSpec((rows, D_RET), lambda i: (i, 0)),
                           pl.BlockSpec((per, nh, d, d), lambda i: (i, 0, 0, 0))],
                out_shape=[jax.ShapeDtypeStruct((t, D_RET), F32), jax.ShapeDtypeStruct((t, D_RET), MXU_DTYPE),
                           jax.ShapeDtypeStruct((n_chunks, nh, d, d), F32)],
                scratch_shapes=[pltpu.VMEM((nh, d, d), F32)],
                operands=[proj, proj, cos2, sin_signed, decay, xi, zeta, gain])


def _outproj_fwd(x, y_lru, y_ret, w_out, g2, comm):
    t = x.shape[0]
    tm = min(512, t)

    def body(x_ref, yl_ref, yr_ref, w_ref, g_ref, h1_ref, u2_ref):
        h1 = x_ref[...] + _dot(yl_ref[...], w_ref[:D_LRU, :]) + _dot(yr_ref[...], w_ref[D_LRU:, :])
        h1_ref[...] = h1
        _, _, u = _rms_fwd(h1, g_ref[...])
        u2_ref[...] = u.astype(u2_ref.dtype)

    row = lambda i: (i, 0)
    return _call(body, name="outproj_fwd", grid=(t // tm,), comm=comm,
                 in_specs=[pl.BlockSpec((tm, D_MODEL), row), pl.BlockSpec((tm, D_LRU), row), pl.BlockSpec((tm, D_RET), row),
                           _resident((D_MODEL, D_MODEL)), pl.BlockSpec((1, D_MODEL), lambda i: (0, 0))],
                 out_specs=[pl.BlockSpec((tm, D_MODEL), row), pl.BlockSpec((tm, D_MODEL), row)],
                 out_shape=[jax.ShapeDtypeStruct((t, D_MODEL), F32), jax.ShapeDtypeStruct((t, D_MODEL), MXU_DTYPE)],
                 operands=[x, y_lru, y_ret, w_out, g2])


FFN_TN = 768
FFN_NJ = D_FF // FFN_TN
FFN_GROUP = 4


def _ffn_fwd(u2, w_blocks, conv_w, conv_b, w_down, h1, gf, target):
    t = u2.shape[0]
    tm = min(256, t)
    tn, nj, group = FFN_TN, FFN_NJ, FFN_GROUP
    ng, tw = nj // group, group * tn
    hb = _halo_rows(u2.dtype)
    assert w_blocks.shape == (2 * nj, D_MODEL, tn)

    def project(u_ext, w, col, up_ref, conv_ref, cw_ref, cb_ref, first):
        ext = _dot(u_ext, w)
        x = ext[hb:, :]
        up_ref[:, col] = x.astype(up_ref.dtype)
        prev = jnp.where(first, 0.0, ext[hb - SUBLANES:hb, :])
        cw = cw_ref[:, col]
        y = cb_ref[:, col] + cw[FFN_CONV - 1:FFN_CONV, :] * x
        for k in range(FFN_CONV - 1):
            y = y + cw[k:k + 1, :] * _shift_down(prev, x, FFN_CONV - 1 - k)
        conv_ref[:, col] = y.astype(conv_ref.dtype)
        return y

    def body(u_ref, uh_ref, w_ref, cwa_ref, cwv_ref, cba_ref, cbv_ref, wd_ref, h1_ref, gf_ref, tg_ref,
             upa_ref, upv_ref, ca_ref, cv_ref, act_ref, dh_ref, dhb_ref, dgf_ref, loss_ref, acc):
        i, jg = pl.program_id(0), pl.program_id(1)

        @pl.when((i == 0) & (jg == 0))
        def _():
            dgf_ref[...] = jnp.zeros_like(dgf_ref)
            loss_ref[...] = jnp.zeros_like(loss_ref)

        @pl.when(jg == 0)
        def _():
            acc[...] = jnp.zeros_like(acc)

        u_ext = jnp.concatenate([uh_ref[...], u_ref[...]], axis=0)
        down = None
        for jj in range(group):
            col = slice(jj * tn, (jj + 1) * tn)
            j = jg * group + jj
            a = project(u_ext, w_ref[j], col, upa_ref, ca_ref, cwa_ref, cba_ref, i == 0)
            v = project(u_ext, w_ref[nj + j], col, upv_ref, cv_ref, cwv_ref, cbv_ref, i == 0)
            act = (_gelu(a) * v).astype(act_ref.dtype)
            act_ref[:, col] = act
            part = _dot(act, wd_ref[pl.ds(pl.multiple_of(j * tn, tn), tn), :])
            down = part if down is None else down + part
        acc[...] += down

        @pl.when(jg == ng - 1)
        def _():
            n, rstd, y = _rms_fwd(h1_ref[...] + acc[...], gf_ref[...])
            err = y - tg_ref[...]
            loss_ref[...] += (0.5 / D_MODEL) * jnp.sum(err * err)
            dh, dgf = _rms_bwd(err * (1.0 / D_MODEL), n, rstd, gf_ref[...])
            dgf_ref[...] += dgf
            dh_ref[...] = dh
            dhb_ref[...] = dh.astype(dhb_ref.dtype)

    per = tm // hb
    row = lambda i, j: (i, 0)
    const = lambda i, j: (0, 0)
    tile = pl.BlockSpec((tm, tw), lambda i, j: (i, j))
    return _call(body, name="ffn_fwd", grid=(t // tm, ng),
                 in_specs=[pl.BlockSpec((tm, D_MODEL), row),
                           pl.BlockSpec((hb, D_MODEL), lambda i, j: (jnp.maximum(i * per - 1, 0), 0)),
                           _resident(w_blocks.shape),
                           pl.BlockSpec((FFN_CONV, tw), lambda i, j: (0, j)),
                           pl.BlockSpec((FFN_CONV, tw), lambda i, j: (0, j + ng)),
                           pl.BlockSpec((1, tw), lambda i, j: (0, j)), pl.BlockSpec((1, tw), lambda i, j: (0, j + ng)),
                           _resident((D_FF, D_MODEL)),
                           pl.BlockSpec((tm, D_MODEL), row), pl.BlockSpec((1, D_MODEL), const),
                           pl.BlockSpec((tm, D_MODEL), row)],
                 out_specs=[tile] * 5 + [pl.BlockSpec((tm, D_MODEL), row),
                            pl.BlockSpec((tm, D_MODEL), row), pl.BlockSpec((SUBLANES, D_MODEL), const),
                            pl.BlockSpec((SUBLANES, LANES), const)],
                 out_shape=[jax.ShapeDtypeStruct((t, D_FF), MXU_DTYPE)] * 5 + [
                            jax.ShapeDtypeStruct((t, D_MODEL), F32),
                            jax.ShapeDtypeStruct((t, D_MODEL), MXU_DTYPE), jax.ShapeDtypeStruct((SUBLANES, D_MODEL), F32),
                            jax.ShapeDtypeStruct((SUBLANES, LANES), F32)],
                 scratch_shapes=[pltpu.VMEM((tm, D_MODEL), F32)],
                 operands=[u2, u2, w_blocks, conv_w, conv_w, conv_b, conv_b, w_down, h1, gf, target])


FFN_ACC_ROWS = SUBLANES * (FFN_CONV + 1)


def _ffn_bwd(dh2, dh2_b, w_down, up_a, up_v, conv_a, conv_v, conv_w, w_up_blocks, h1, g2, comm):
    t = up_a.shape[0]
    tm = min(256, t)
    tn, nj, group = FFN_TN, FFN_NJ, FFN_GROUP
    ng, tw = nj // group, group * tn
    ni = t // tm
    assert w_up_blocks.shape == (2 * nj, D_MODEL, tn)

    def conv_bwd(dy, x, cw, acc_ref, carry_ref, dup_ref, col):
        nxt = carry_ref[...]
        carry_ref[...] = dy[:SUBLANES, :]
        ahead = [_shift_up(dy, nxt, FFN_CONV - 1 - k) for k in range(FFN_CONV)]
        dx = cw[FFN_CONV - 1:FFN_CONV, :] * dy
        for k in range(FFN_CONV - 1):
            dx = dx + cw[k:k + 1, :] * ahead[k]
        dx = dx.astype(dup_ref.dtype)
        dup_ref[:, col] = dx
        for k in range(FFN_CONV):
            acc_ref[k * SUBLANES:(k + 1) * SUBLANES, :] += _colsum8(ahead[k] * x)
        acc_ref[FFN_CONV * SUBLANES:, :] += _colsum8(dy)
        return dx

    def body(dh_ref, dhb_ref, wd_ref, ua_ref, uv_ref, ca_ref, cv_ref, cwa_ref, cwv_ref, wu_ref, h1_ref, g2_ref,
             dua_ref, duv_ref, acca_ref, accv_ref, dh1_ref, dh1b_ref, dg2_ref, carry_a, carry_v, du):
        i, jg = pl.program_id(0), pl.program_id(1)

        @pl.when((i == 0) & (jg == 0))
        def _():
            for ref in (acca_ref, accv_ref, carry_a, carry_v, dg2_ref):
                ref[...] = jnp.zeros_like(ref)

        dhb = dhb_ref[...]
        part = None
        for jj in range(group):
            col = slice(jj * tn, (jj + 1) * tn)
            j = jg * group + jj
            v = cv_ref[:, col].astype(F32)
            g, dg = _gelu_parts(ca_ref[:, col].astype(F32))
            dact = _dot_nt(dhb, wd_ref[pl.ds(pl.multiple_of(j * tn, tn), tn), :])
            da = conv_bwd(dact * v * dg, ua_ref[:, col].astype(F32), cwa_ref[:, col], acca_ref.at[j], carry_a.at[j],
                          dua_ref, col)
            dv = conv_bwd(dact * g, uv_ref[:, col].astype(F32), cwv_ref[:, col], accv_ref.at[j], carry_v.at[j],
                          duv_ref, col)
            term = _dot_nt(da, wu_ref[j]) + _dot_nt(dv, wu_ref[nj + j])
            part = term if part is None else part + term

        @pl.when(jg == 0)
        def _():
            du[...] = part

        @pl.when(jg > 0)
        def _():
            du[...] += part

        @pl.when(jg == ng - 1)
        def _():
            n, rstd, _ = _rms_fwd(h1_ref[...], g2_ref[...])
            dh1, dg2 = _rms_bwd(du[...], n, rstd, g2_ref[...])
            dh1 = dh1 + dh_ref[...]
            dg2_ref[...] += dg2
            dh1_ref[...] = dh1
            dh1b_ref[...] = dh1.astype(dh1b_ref.dtype)

    row = lambda i, j: (ni - 1 - i, 0)
    const = lambda i, j: (0, 0)
    tile = pl.BlockSpec((tm, tw), lambda i, j: (ni - 1 - i, j))
    acc = pl.BlockSpec((nj, FFN_ACC_ROWS, tn), lambda i, j: (0, 0, 0))
    return _call(body, name="ffn_bwd", grid=(ni, ng), comm=comm,
                 in_specs=[pl.BlockSpec((tm, D_MODEL), row), pl.BlockSpec((tm, D_MODEL), row),
                           _resident((D_FF, D_MODEL)), tile, tile, tile, tile,
                           pl.BlockSpec((FFN_CONV, tw), lambda i, j: (0, j)),
                           pl.BlockSpec((FFN_CONV, tw), lambda i, j: (0, j + ng)),
                           _resident(w_up_blocks.shape), pl.BlockSpec((tm, D_MODEL), row),
                           pl.BlockSpec((1, D_MODEL), const)],
                 out_specs=[tile, tile, acc, acc, pl.BlockSpec((tm, D_MODEL), row), pl.BlockSpec((tm, D_MODEL), row),
                            pl.BlockSpec((SUBLANES, D_MODEL), const)],
                 out_shape=[jax.ShapeDtypeStruct((t, D_FF), MXU_DTYPE), jax.ShapeDtypeStruct((t, D_FF), MXU_DTYPE),
                            jax.ShapeDtypeStruct((nj, FFN_ACC_ROWS, tn), F32),
                            jax.ShapeDtypeStruct((nj, FFN_ACC_ROWS, tn), F32),
                            jax.ShapeDtypeStruct((t, D_MODEL), F32), jax.ShapeDtypeStruct((t, D_MODEL), MXU_DTYPE),
                            jax.ShapeDtypeStruct((SUBLANES, D_MODEL), F32)],
                 scratch_shapes=[pltpu.VMEM((nj, SUBLANES, tn), F32), pltpu.VMEM((nj, SUBLANES, tn), F32),
                                 pltpu.VMEM((tm, D_MODEL), F32)],
                 operands=[dh2, dh2_b, w_down, up_a, up_v, conv_a, conv_v, conv_w, conv_w, w_up_blocks, h1, g2])


def _norm_bwd_matmul(parts, w_blocks, h, gain, d_res, name, comm=None, tiles=None, into=None):
    t = h.shape[0]
    tm = min(256, t)
    first, count = tiles or (0, t // tm)
    n_parts = len(parts)
    nb, _, cb = w_blocks.shape
    where = []
    for p, a in enumerate(parts):
        assert a.shape[1] % cb == 0
        where += [(p, lo) for lo in range(0, a.shape[1], cb)]
    assert len(where) == nb
    n_in = n_parts + 4

    def body(*refs):
        a_refs = refs[:n_parts]
        w_ref, h_ref, g_ref, dres_ref = refs[n_parts:n_in]
        dh_ref, dhb_ref, dg_ref = refs[-3:]

        @pl.when(pl.program_id(0) == 0)
        def _():
            dg_ref[...] = jnp.zeros_like(dg_ref)

        du = None
        for d, (p, lo) in enumerate(where):
            term = _dot_nt(a_refs[p][:, lo:lo + cb], w_ref[d])
            du = term if du is None else du + term
        n, rstd, _ = _rms_fwd(h_ref[...], g_ref[...])
        dh, dg = _rms_bwd(du, n, rstd, g_ref[...])
        dh = dh + dres_ref[...]
        dg_ref[...] += dg
        dh_ref[...] = dh
        dhb_ref[...] = dh.astype(dhb_ref.dtype)

    row = lambda i: (i + first, 0)
    const = lambda i: (0, 0)
    in_specs = [pl.BlockSpec((tm, a.shape[1]), row) for a in parts] + [
        _resident(w_blocks.shape), pl.BlockSpec((tm, D_MODEL), row), pl.BlockSpec((1, D_MODEL), const),
        pl.BlockSpec((tm, D_MODEL), row)]
    operands = [*parts, w_blocks, h, gain, d_res]
    aliases = {}
    if into is not None:
        in_specs += [ANY, ANY]
        operands += list(into)
        aliases = {n_in: 0, n_in + 1: 1}
    return _call(body, name=name, grid=(count,), comm=comm, in_specs=in_specs,
                 out_specs=[pl.BlockSpec((tm, D_MODEL), row), pl.BlockSpec((tm, D_MODEL), row),
                            pl.BlockSpec((SUBLANES, D_MODEL), const)],
                 out_shape=[jax.ShapeDtypeStruct((t, D_MODEL), F32), jax.ShapeDtypeStruct((t, D_MODEL), MXU_DTYPE),
                            jax.ShapeDtypeStruct((SUBLANES, D_MODEL), F32)],
                 operands=operands, aliases=aliases)


def _ret_bwd(proj, cos2, sin_signed, gain, o, states, dmix_at):
    t = proj.shape[0]
    c, d, nh = RET_CHUNK, RET_HEAD_DIM, RET_HEADS
    n_chunks = t // c
    per = RET_CHUNKS_PER_STEP if n_chunks % RET_CHUNKS_PER_STEP == 0 else 1
    rows = per * c
    n_steps = n_chunks // per
    decay, xi, zeta, g_chunk = _ret_consts()
    base = 2 * D_LRU

    def units(qk_ref, vg_ref, cos_ref, sin_ref, dec_ref, xi_ref, zeta_ref, gain_ref, o_ref, st_ref,
              dp_ref, dgain_ref, gstate, ctx):
        cur = [None] * nh
        dmix = ctx["scratch"][dmix_at[0]][dmix_at[1]]

        def start():
            @pl.when(pl.program_id(0) == 0)
            def _():
                gstate[...] = jnp.zeros_like(gstate)
                dgain_ref[...] = jnp.zeros_like(dgain_ref)
            for h in range(nh):
                cur[h] = gstate[h]

        def gate_and_norm(s, h, keep):
            rs = slice(s * c, (s + 1) * c)
            lo = h * d
            g = vg_ref[rs, D_RET + lo:D_RET + lo + d]
            gain_h = gain_ref[:, lo:lo + d]
            dy = dmix[rs, D_LRU + lo:D_LRU + lo + d]
            sg = _sigmoid(g)
            o_h = o_ref[rs, lo:lo + d]
            oc = o_h - jnp.mean(o_h, axis=-1, keepdims=True)
            rstd = lax.rsqrt(jnp.mean(oc * oc, axis=-1, keepdims=True) + NORM_EPS)
            on = oc * rstd
            at = base + 3 * D_RET + lo
            dp_ref[rs, at:at + d] = (dy * on * gain_h * (sg * (1.0 + g * (1.0 - sg)))).astype(dp_ref.dtype)
            don_g = dy * (g * sg)
            dgain_ref[:, lo:lo + d] += _colsum8(don_g * on)
            don = don_g * gain_h
            keep["do"] = rstd * (don - jnp.mean(don, axis=-1, keepdims=True)
                                 - on * jnp.mean(don * on, axis=-1, keepdims=True))

        def retain(s, h, keep):
            rs = slice(s * c, (s + 1) * c)
            cos2, sin_s = cos_ref[rs, :], sin_ref[rs, :]
            lo = h * d
            q = _rope(qk_ref[rs, lo:lo + d], cos2, sin_s)
            k = _rope(qk_ref[rs, D_RET + lo:D_RET + lo + d], cos2, sin_s) * RET_SCALE
            v = vg_ref[rs, lo:lo + d]
            xi_h, zeta_h, dec = xi_ref[:, lo:lo + d], zeta_ref[:, lo:lo + d], dec_ref[h]
            do = keep["do"]
            s_prev = st_ref[s, h]
            g_next = cur[h]
            p = _dot_nt(q, k) * dec
            dpm = _dot_nt(do, v) * dec
            keep["dq"] = _dot(dpm, k) + _dot_nt(do, s_prev) * xi_h
            keep["dk"] = _dot_tn(dpm, q) + _dot_nt(v, g_next) * zeta_h
            dv = _dot_tn(p, do) + _dot(k * zeta_h, g_next)
            cur[h] = g_next * g_chunk[h] + _dot_tn(q * xi_h, do)
            at = base + 2 * D_RET + lo
            dp_ref[rs, at:at + d] = dv.astype(dp_ref.dtype)

        def unrope(s, h, keep):
            rs = slice(s * c, (s + 1) * c)
            cos2, sin_s = cos_ref[rs, :], sin_ref[rs, :]
            lo = h * d
            dp_ref[rs, base + lo:base + lo + d] = _rope_bwd(keep["dq"], cos2, sin_s).astype(dp_ref.dtype)
            at = base + D_RET + lo
            dp_ref[rs, at:at + d] = _rope_bwd(keep["dk"] * RET_SCALE, cos2, sin_s).astype(dp_ref.dtype)

        def end():
            for h in range(nh):
                gstate[h] = cur[h]

        pieces = [start]
        for s in reversed(range(per)):
            for h in range(nh):
                keep = {}
                pieces += [lambda s=s, h=h, keep=keep, f=f: f(s, h, keep) for f in (gate_and_norm, retain, unrope)]
        return pieces + [end]

    rev = lambda col: (lambda i: (n_steps - 1 - i, col))
    full2 = lambda i: (0, 0)
    return dict(units=units, grid=(n_steps,),
                in_specs=[pl.BlockSpec((rows, 2 * D_RET), rev(1)), pl.BlockSpec((rows, 2 * D_RET), rev(2)),
                          pl.BlockSpec((rows, d), rev(0)), pl.BlockSpec((rows, d), rev(0)),
                          pl.BlockSpec((nh, c, c), lambda i: (0, 0, 0)), pl.BlockSpec((c, D_RET), full2),
                          pl.BlockSpec((c, D_RET), full2), pl.BlockSpec((1, D_RET), full2),
                          pl.BlockSpec((rows, D_RET), rev(0)),
                          pl.BlockSpec((per, nh, d, d), lambda i: (n_steps - 1 - i, 0, 0, 0))],
                out_specs=[pl.BlockSpec((rows, D_IN), rev(0)), pl.BlockSpec((SUBLANES, D_RET), full2)],
                out_shape=[jax.ShapeDtypeStruct((t, D_IN), MXU_DTYPE), jax.ShapeDtypeStruct((SUBLANES, D_RET), F32)],
                scratch_shapes=[pltpu.VMEM((nh, d, d), F32)],
                operands=[proj, proj, cos2, sin_signed, decay, xi, zeta, gain, o, states])


LRU_ACC = {"conv_w": 0, "conv_b": LRU_CONV, "gate_a_b": LRU_CONV + 1, "gate_x_b": LRU_CONV + 2,
           "lambda": LRU_CONV + 3, "norm_gain": LRU_CONV + 4}
LRU_ACC_ROWS = SUBLANES * (LRU_CONV + 5)


def _lru_bwd(proj, xc_all, h_all, conv_w, wa, ba, wx, bx, lam, gain, dproj_part, dmix_at):
    t = proj.shape[0]
    tm = min(256, t)
    c = D_LRU
    ni = t // tm

    def body(x_ref, xh_ref, g_ref, xc_ref, h_ref, hh_ref, cw_ref, wa_ref, ba_ref, wx_ref, bx_ref, lam_ref,
             gain_ref, acc_ref, dwa_ref, dwx_ref, a_scr, b_scr, mu_scr, carry_mu, carry_dxc, ctx):
        dp_ref = ctx["outs"][dproj_part][0]
        dmix = ctx["scratch"][dmix_at[0]][dmix_at[1]]
        fill = ctx["fill"]
        i = pl.program_id(0)
        r = ni - 1 - i

        @pl.when(i == 0)
        def _():
            acc_ref[...] = jnp.zeros_like(acc_ref)
            dwa_ref[...] = jnp.zeros_like(dwa_ref)
            dwx_ref[...] = jnp.zeros_like(dwx_ref)
            carry_mu[...] = jnp.zeros_like(carry_mu)
            carry_dxc[...] = jnp.zeros_like(carry_dxc)

        def add(name, val, k=0):
            lo = (LRU_ACC[name] + k) * SUBLANES
            acc_ref[lo:lo + SUBLANES, :] += _colsum8(val)

        fill()
        xc, h = xc_ref[...], h_ref[...]
        lam_v = lam_ref[...]
        sp = _softplus(-lam_v)
        rg, ig, a, m = _lru_gates(xc, wa_ref[...], ba_ref[...], wx_ref[...], bx_ref[...], sp, fill)
        gl, dgl = _gelu_parts(g_ref[...])
        fill()
        zn, rstd, _ = _rms_fwd(h * gl, gain_ref[...])
        dy = dmix[:, :c]
        dz, dgain = _rms_bwd(dy, zn, rstd, gain_ref[...])
        lo = LRU_ACC["norm_gain"] * SUBLANES
        acc_ref[lo:lo + SUBLANES, :] += dgain
        dp_ref[:, c:2 * c] = (dz * h * dgl).astype(dp_ref.dtype)
        dh = dz * gl
        fill()
        ga, gb = _group_scan(a, a * dh, reverse=True, fill=fill)
        a_scr[...] = ga
        b_scr[...] = gb
        mu_next_tile = carry_mu[...]
        carry_mu[...] = _carry_scan(a_scr, b_scr, mu_scr, mu_next_tile, reverse=True)
        fill()
        lam_t = dh + _shift_up(mu_scr[...], mu_next_tile, 1)
        h_prev = _shift_down(jnp.where(r == 0, 0.0, hh_ref[...]), h, 1)
        da = lam_t * h_prev
        dig = lam_t * m * xc
        dxc = lam_t * m * ig
        dlog_a = da * a - (lam_t * ig * xc) * (a * a) / m
        fill()
        dpr = dlog_a * ((-LRU_C) * sp) * rg * (1.0 - rg)
        add("lambda", dlog_a * ((-LRU_C) * rg) * (-_sigmoid(-lam_v)))
        dpi = dig * ig * (1.0 - ig)
        add("gate_a_b", dpr)
        add("gate_x_b", dpi)
        fill()
        dwa_ref[...] += _dot_tn(xc, dpr)
        dwx_ref[...] += _dot_tn(xc, dpi)
        dxc = dxc + _dot_nt(dpr, wa_ref[...]) + _dot_nt(dpi, wx_ref[...])
        fill()
        add("conv_b", dxc)
        x = x_ref[...]
        prev = jnp.where(r == 0, 0.0, xh_ref[...])
        cw = cw_ref[...]
        nxt = carry_dxc[...]
        carry_dxc[...] = dxc[:SUBLANES, :]
        dx = cw[LRU_CONV - 1:LRU_CONV, :] * dxc
        for k in range(LRU_CONV - 1):
            dx = dx + cw[k:k + 1, :] * _shift_up(dxc, nxt, LRU_CONV - 1 - k)
        fill()
        for k in range(LRU_CONV):
            add("conv_w", dxc * _shift_down(prev, x, LRU_CONV - 1 - k), k)
        dp_ref[:, :c] = dx.astype(dp_ref.dtype)

    per = tm // SUBLANES
    rev = lambda col: (lambda i: (ni - 1 - i, col))
    halo = lambda i: (jnp.maximum((ni - 1 - i) * per - 1, 0), 0)
    full = lambda i: (0, 0)
    vec = pl.BlockSpec((1, c), full)
    mat = pl.BlockSpec((c, c), full)
    return dict(body=body, grid=(ni,), fill_points=12,
                in_specs=[pl.BlockSpec((tm, c), rev(0)), pl.BlockSpec((SUBLANES, c), halo), pl.BlockSpec((tm, c), rev(1)),
                          pl.BlockSpec((tm, c), rev(0)), pl.BlockSpec((tm, c), rev(0)), pl.BlockSpec((SUBLANES, c), halo),
                          pl.BlockSpec((LRU_CONV, c), full), mat, vec, mat, vec, vec, vec],
                out_specs=[pl.BlockSpec((LRU_ACC_ROWS, c), full), mat, mat],
                out_shape=[jax.ShapeDtypeStruct((LRU_ACC_ROWS, c), F32), jax.ShapeDtypeStruct((c, c), F32),
                           jax.ShapeDtypeStruct((c, c), F32)],
                scratch_shapes=[pltpu.VMEM((tm, c), F32), pltpu.VMEM((tm, c), F32), pltpu.VMEM((tm, c), F32),
                                pltpu.VMEM((SUBLANES, c), F32), pltpu.VMEM((SUBLANES, c), F32)],
                operands=[proj, proj, proj, xc_all, h_all, h_all, conv_w, wa, ba, wx, bx, lam, gain])


def _mix_proj_bwd(dh1, dh1_b, w_out, w_in_blocks, x, g1, dproj_part):
    t = x.shape[0]
    tm = min(256, t)
    ni = t // tm
    nb, _, cb = w_in_blocks.shape
    first_free = -(-2 * D_LRU // cb)
    du = [None]

    def term(dp_ref, w_ref, d):
        part = _dot_nt(dp_ref[:, d * cb:(d + 1) * cb], w_ref[d])
        du[0] = part if du[0] is None else du[0] + part

    def head(dh_ref, dhb_ref, wo_ref, wi_ref, x_ref, g_ref, gx_ref, dg_ref, dmix, ctx):
        @pl.when(pl.program_id(0) == 0)
        def _():
            dg_ref[...] = jnp.zeros_like(dg_ref)
        dmix[...] = _dot_nt(dhb_ref[...], wo_ref[...])
        du[0] = None

    def units(dh_ref, dhb_ref, wo_ref, wi_ref, x_ref, g_ref, gx_ref, dg_ref, dmix, ctx):
        dp_ref = ctx["outs"][dproj_part][0]
        return [lambda d=d: term(dp_ref, wi_ref, d) for d in range(first_free, nb)]

    def tail(dh_ref, dhb_ref, wo_ref, wi_ref, x_ref, g_ref, gx_ref, dg_ref, dmix, ctx):
        dp_ref = ctx["outs"][dproj_part][0]
        for d in range(first_free):
            term(dp_ref, wi_ref, d)
        n, rstd, _ = _rms_fwd(x_ref[...], g_ref[...])
        dx, dg = _rms_bwd(du[0], n, rstd, g_ref[...])
        dg_ref[...] += dg
        gx_ref[...] = dx + dh_ref[...]

    row = lambda i: (ni - 1 - i, 0)
    const = lambda i: (0, 0)
    tile = pl.BlockSpec((tm, D_MODEL), row)
    return dict(head=head, units=units, tail=tail, grid=(ni,),
                in_specs=[tile, tile, _resident(w_out.shape), _resident(w_in_blocks.shape), tile,
                          pl.BlockSpec((1, D_MODEL), const)],
                out_specs=[tile, pl.BlockSpec((SUBLANES, D_MODEL), const)],
                out_shape=[jax.ShapeDtypeStruct((t, D_MODEL), F32), jax.ShapeDtypeStruct((SUBLANES, D_MODEL), F32)],
                scratch_shapes=[pltpu.VMEM((tm, D_MODEL), F32)],
                operands=[dh1, dh1_b, w_out, w_in_blocks, x, g1])


def _pair_sum(core, a, b, name):
    n, r, c = b.shape
    spec = pl.BlockSpec((None, r, c), lambda q, core: (q, 0, 0))

    def body(core_ref, a_ref, b_ref, o_ref):
        o_ref[...] = (a_ref[...].astype(F32) + b_ref[...].astype(F32)).astype(o_ref.dtype)

    return pl.pallas_call(
        body, name=name,
        grid_spec=pltpu.PrefetchScalarGridSpec(
            num_scalar_prefetch=1, grid=(n,),
            in_specs=[pl.BlockSpec((None, r, c), lambda q, core: (2 * q + core[0], 0, 0)), spec], out_specs=spec),
        out_shape=jax.ShapeDtypeStruct(b.shape, b.dtype),
        compiler_params=pltpu.CompilerParams(dimension_semantics=("arbitrary",), vmem_limit_bytes=VMEM_LIMIT),
    )(core, a, b)


ADAMW_BLOCK_BYTES = 4 * 1024 * 1024


def _sum_adamw(parts, w, m, v, name):
    n_parts, r, c = parts.shape
    tr = r
    while n_parts * tr * c * parts.dtype.itemsize > ADAMW_BLOCK_BYTES and tr % (4 * SUBLANES) == 0:
        tr //= 2

    def body(p_ref, w_ref, m_ref, v_ref, g_ref, d_ref, nm_ref, nv_ref):
        g = p_ref[0].astype(F32)
        for s in range(1, n_parts):
            g = g + p_ref[s].astype(F32)
        nm = ADAM_B1 * m_ref[...] + (1.0 - ADAM_B1) * g
        nv = ADAM_B2 * v_ref[...] + (1.0 - ADAM_B2) * (g * g)
        m_hat = nm / (1.0 - ADAM_B1 ** ADAM_STEP)
        v_hat = nv / (1.0 - ADAM_B2 ** ADAM_STEP)
        g_ref[...] = g
        d_ref[...] = -ADAM_LR * (m_hat / (jnp.sqrt(v_hat) + ADAM_EPS) + ADAM_WD * w_ref[...])
        nm_ref[...] = nm
        nv_ref[...] = nv

    row = pl.BlockSpec((tr, c), lambda i: (i, 0))
    return _call(body, name=name, grid=(r // tr,),
                 in_specs=[pl.BlockSpec((n_parts, tr, c), lambda i: (0, i, 0)), row, row, row],
                 out_specs=[row, row, row, row], out_shape=[jax.ShapeDtypeStruct((r, c), F32)] * 4,
                 operands=[parts, w, m, v])


MATRICES = ("w_in", "w_out", "ffn_up_w", "ffn_down_w")
CONVS = ("lru_conv_w", "ffn_conv_w")
REPLICATED = ("norm1_gain", "lru_conv_b", "lru_gate_a_w", "lru_gate_a_b", "lru_gate_x_w", "lru_gate_x_b", "lru_lambda",
              "lru_norm_gain", "ret_norm_gain", "norm2_gain", "ffn_conv_b", "final_norm_gain")
WEIGHTS = ("norm1_gain", "w_in", "lru_conv_w", "lru_conv_b", "lru_gate_a_w", "lru_gate_a_b", "lru_gate_x_w",
           "lru_gate_x_b", "lru_lambda", "lru_norm_gain", "ret_norm_gain", "w_out", "norm2_gain", "ffn_up_w",
           "ffn_conv_w", "ffn_conv_b", "ffn_down_w", "final_norm_gain")


def _rows(a, pad_to):
    a = a.reshape(-1, LANES)
    pad = (-a.shape[0]) % pad_to
    return jnp.pad(a, ((0, pad), (0, 0))) if pad else a


def _pack(arrays, pad_to):
    rows, layout, at = [], [], 0
    for a in arrays:
        r = _rows(a, pad_to)
        layout.append((at, a.size // LANES, a.shape))
        rows.append(r)
        at += r.shape[0]
    return jnp.concatenate(rows, axis=0), layout


def _unpack(packed, layout):
    lead = packed.shape[:-2]
    return [packed[..., at:at + n, :].reshape(lead + shape) for at, n, shape in layout]


def _conv_rows(lru, ffn, dtype, pad_to):
    lead = lru.shape[:-2]
    flat = jnp.concatenate([lru.reshape(lead + (-1,)), ffn.reshape(lead + (-1,))], axis=-1).astype(dtype)
    rows = flat.shape[-1] // LANES
    pad = (-rows) % pad_to
    return jnp.pad(flat.reshape(lead + (rows, LANES)), [(0, 0)] * len(lead) + [(0, pad), (0, 0)])


def _column_blocks(full):
    r, c = full.shape
    return full.reshape(r, N_DEV, c // N_DEV).transpose(1, 0, 2)


def _block_diag(w):
    nh, d, _ = w.shape
    eye = jnp.eye(nh, dtype=w.dtype)
    return (w[:, :, None, :] * eye[:, None, :, None]).reshape(nh * d, nh * d)


def _diag_blocks(dense, nh):
    d = dense.shape[0] // nh
    blocks = dense.reshape(nh, d, nh, d)
    return jnp.stack([blocks[h, :, h, :] for h in range(nh)], axis=0)


def kernel(x, norm1_gain, w_in, lru_conv_w, lru_conv_b, lru_gate_a_w, lru_gate_a_b, lru_gate_x_w, lru_gate_x_b, lru_lambda, lru_norm_gain, ret_norm_gain, w_out, norm2_gain, ffn_up_w, ffn_conv_w, ffn_conv_b, ffn_down_w, final_norm_gain, loss_target, m_norm1_gain, m_w_in, m_lru_conv_w, m_lru_conv_b, m_lru_gate_a_w, m_lru_gate_a_b, m_lru_gate_x_w, m_lru_gate_x_b, m_lru_lambda, m_lru_norm_gain, m_ret_norm_gain, m_w_out, m_norm2_gain, m_ffn_up_w, m_ffn_conv_w, m_ffn_conv_b, m_ffn_down_w, m_final_norm_gain, v_norm1_gain, v_w_in, v_lru_conv_w, v_lru_conv_b, v_lru_gate_a_w, v_lru_gate_a_b, v_lru_gate_x_w, v_lru_gate_x_b, v_lru_lambda, v_lru_norm_gain, v_ret_norm_gain, v_w_out, v_norm2_gain, v_ffn_up_w, v_ffn_conv_w, v_ffn_conv_b, v_ffn_down_w, v_final_norm_gain):
    args = dict(locals())
    given = {n: args[n] for n in WEIGHTS}
    out_shape = {n: given[n].shape for n in WEIGHTS}

    def plain(a):
        return a.reshape(1, -1) if a.ndim <= 2 else a[0]

    w = {n: plain(given[n]) for n in WEIGHTS}
    mom_m = {n: plain(args["m_" + n]) for n in WEIGHTS}
    mom_v = {n: plain(args["v_" + n]) for n in WEIGHTS}
    x2, target = x[0], loss_target[0]
    t = x2.shape[0]
    core = lax.axis_index("c").astype(jnp.int32).reshape(1)
    res = {}

    conv_pad = _conv_rows(w["lru_conv_w"], w["ffn_conv_w"], F32, SUBLANES)
    first = _gather_first([w["w_in"].astype(MXU_DTYPE), conv_pad])
    w_in_blocks, conv_all = _run_comms([first, _gather_second(first.out_shape)], "w_in_all_gather")
    n_lru = w["lru_conv_w"].size
    conv_flat = conv_all.reshape(N_DEV, -1)
    lru_cw = conv_flat[:, :n_lru].reshape((N_DEV,) + w["lru_conv_w"].shape).transpose(1, 0, 2).reshape(LRU_CONV, D_LRU)
    ffn_cw = conv_flat[:, n_lru:n_lru + w["ffn_conv_w"].size].reshape((N_DEV,) + w["ffn_conv_w"].shape)
    ffn_cw = ffn_cw.transpose(1, 0, 2).reshape(FFN_CONV, 2 * D_FF)

    cos2, sin_signed = _rope_tables(t)
    wa = _block_diag(w["lru_gate_a_w"]).astype(MXU_DTYPE)
    wx = _block_diag(w["lru_gate_x_w"]).astype(MXU_DTYPE)
    gf = w["final_norm_gain"]

    early = _gather_first([w["ffn_up_w"].astype(MXU_DTYPE), w["w_out"].astype(MXU_DTYPE)])
    (u1, proj), (up_part, w_out_part) = _inproj_fwd(x2, w["norm1_gain"], w_in_blocks, early)
    ((xc, h_lru, y_lru), (o_ret, y_ret, states)), (up_blocks, w_out_blocks, down_part) = _fused(
        [_lru_fwd(proj, lru_cw, w["lru_conv_b"], wa, w["lru_gate_a_b"], wx, w["lru_gate_x_b"], w["lru_lambda"],
                  w["lru_norm_gain"]),
         _ret_fwd(proj, cos2, sin_signed, w["ret_norm_gain"])],
        "mix_fwd", _both(_gather_second([up_part, w_out_part]), _gather_first([w["ffn_down_w"].astype(MXU_DTYPE)])))
    w_out_full = w_out_blocks.reshape(D_MODEL, D_MODEL)

    (h1, u2), (down_blocks,) = _outproj_fwd(x2, y_lru, y_ret, w_out_full, w["norm2_gain"], _gather_second([down_part]))
    w_down_full = down_blocks.reshape(D_FF, D_MODEL)
    up_a, up_v, conv_a, conv_v, act, dh2, dh2_b, dgf, loss = _ffn_fwd(u2, up_blocks, ffn_cw, w["ffn_conv_b"], w_down_full,
                                                                      h1, gf, target)
    loss = lax.psum(loss[0, 0], ("x", "y", "c"))

    def to_owner_chips(blocks, names, tag):
        theirs = _run_comms([_pair_exchange(blocks)], "grads_pair_exchange_" + tag)
        return [_pair_sum(core, a, b, "grads_pair_sum_" + n) for n, a, b in zip(names, blocks, theirs)]

    def adamw(name, parts):
        res[name] = _sum_adamw(parts, w[name], mom_m[name], mom_v[name], "adamw_" + name)

    g = {"final_norm_gain": dgf[0]}
    g_down = _mm_tn(act, dh2_b, "ffn_down_wgrad").reshape(N_DEV, D_FF // N_DEV, D_MODEL)
    down_sums = to_owner_chips([g_down], ["ffn_down_w"], "down")
    (dup_a, dup_v, acc_a, acc_v, dh1, dh1_b, dg2), (down_parts,) = _ffn_bwd(
        dh2, dh2_b, w_down_full, up_a, up_v, conv_a, conv_v, ffn_cw, up_blocks, h1, w["norm2_gain"],
        _chip_exchange(down_sums))
    adamw("ffn_down_w", down_parts)
    per_col = lambda a: a[:, ::SUBLANES].transpose(1, 0, 2).reshape(FFN_CONV + 1, D_FF)
    acc = jnp.concatenate([per_col(acc_a), per_col(acc_v)], axis=1)
    g_ffn_cw, g["ffn_conv_b"] = acc[:FFN_CONV], acc[FFN_CONV:]
    g["norm2_gain"] = dg2[:1]
    g_up = jnp.concatenate([_mm_tn(u2, dup_a, "ffn_up_wgrad_a", blocks=N_DEV // 2),
                            _mm_tn(u2, dup_v, "ffn_up_wgrad_v", blocks=N_DEV // 2)], axis=0)
    g_out = jnp.concatenate([_mm_tn(y_lru, dh1_b, "w_out_wgrad_lru"), _mm_tn(y_ret, dh1_b, "w_out_wgrad_ret")], axis=0)
    mid_sums = to_owner_chips([g_up, g_out.reshape(N_DEV, D_MODEL // N_DEV, D_MODEL)], ["ffn_up_w", "w_out"], "mid")
    (dproj, dgain_ret), (grad_x, dg1), (lru_acc, dwa, dwx) = _fused(
        [_ret_bwd(proj, cos2, sin_signed, w["ret_norm_gain"], o_ret, states, dmix_at=(1, 0)),
         _mix_proj_bwd(dh1, dh1_b, w_out_full, w_in_blocks, x2, w["norm1_gain"], dproj_part=0),
         _lru_bwd(proj, xc, h_lru, lru_cw, wa, w["lru_gate_a_b"], wx, w["lru_gate_x_b"], w["lru_lambda"],
                  w["lru_norm_gain"], dproj_part=0, dmix_at=(1, 0))],
        "mix_bwd")
    g["norm1_gain"] = dg1[:1]
    g["ret_norm_gain"] = dgain_ret[:1]
    lru_acc = lru_acc[::SUBLANES]
    g_lru_cw = lru_acc[:LRU_CONV]
    for name in ("conv_b", "gate_a_b", "gate_x_b", "lambda", "norm_gain"):
        g["lru_" + name] = lru_acc[LRU_ACC[name]:LRU_ACC[name] + 1]
    g["lru_gate_a_w"] = _diag_blocks(dwa, LRU_HEADS)
    g["lru_gate_x_w"] = _diag_blocks(dwx, LRU_HEADS)
    g_in, (up_parts, out_parts) = _mm_tn(u1, dproj, "w_in_wgrad", blocks=N_DEV, comm=_chip_exchange(mid_sums))
    adamw("ffn_up_w", up_parts)
    adamw("w_out", out_parts)
    g_conv = _conv_rows(_column_blocks(g_lru_cw), _column_blocks(g_ffn_cw), GRAD_DTYPE, 2 * SUBLANES)
    in_sums = to_owner_chips([g_in, g_conv], ["w_in", "conv"], "in")

    rep_packed, rep_layout = _pack([g[n] for n in REPLICATED], SUBLANES)
    rep_first = _gather_first([rep_packed])
    in_parts, conv_parts, rep_parts = _run_comms(
        [_both(_chip_exchange(in_sums), rep_first), _offset(_gather_second(rep_first.out_shape), 2)],
        "last_grads_exchange")
    adamw("w_in", in_parts)
    pad16 = lambda d: _conv_rows(d["lru_conv_w"], d["ffn_conv_w"], F32, 2 * SUBLANES)
    conv_res = _sum_adamw(conv_parts, pad16(w), pad16(mom_m), pad16(mom_v), "adamw_conv")
    for n, lo, hi in (("lru_conv_w", 0, n_lru), ("ffn_conv_w", n_lru, n_lru + w["ffn_conv_w"].size)):
        res[n] = [r.reshape(-1)[lo:hi].reshape(w[n].shape) for r in conv_res]
    rep_res = _sum_adamw(rep_parts, *[_pack([d[n] for n in REPLICATED], SUBLANES)[0] for d in (w, mom_m, mom_v)],
                         "adamw_replicated")
    for k in range(4):
        for n, a in zip(REPLICATED, _unpack(rep_res[k], rep_layout)):
            res.setdefault(n, [None] * 4)[k] = a

    outs = [loss, grad_x[None]]
    for k in range(4):
        outs += [res[n][k].reshape(out_shape[n]) for n in WEIGHTS]
    return tuple(outs)
```

```python
import math

import numpy as np
import jax
import jax.numpy as jnp
from jax import lax
from jax.experimental import pallas as pl
from jax.experimental.pallas import tpu as pltpu

F32 = jnp.float32
BF16 = jnp.bfloat16
MXU_DTYPE = jnp.bfloat16
GRAD_DTYPE = jnp.bfloat16

N_DEV = 8
N_CHIPS = 4
D_MODEL = 1024
D_LRU = 512
LRU_HEADS = 8
LRU_CONV = 4
LRU_C = 8.0
D_RET = 512
RET_HEADS = 4
RET_HEAD_DIM = 128
RET_CHUNK = 128
ROPE_BASE = 10000.0
D_IN = 3072
D_FF = 3072
FFN_CONV = 3
NORM_EPS = 1e-6

ADAM_LR = 0.001
ADAM_B1 = 0.9
ADAM_B2 = 0.999
ADAM_EPS = 1e-08
ADAM_WD = 0.01
ADAM_STEP = 10

SUBLANES = 8
LANES = 128
VMEM_LIMIT = 48 * 1024 * 1024

MESH = pl.DeviceIdType.MESH
ANY = pl.BlockSpec(memory_space=pl.ANY)


def _dot(a, b):
    return jnp.dot(a.astype(MXU_DTYPE), b.astype(MXU_DTYPE), preferred_element_type=F32)


def _dot_nt(a, b):
    return lax.dot_general(a.astype(MXU_DTYPE), b.astype(MXU_DTYPE), (((1,), (1,)), ((), ())),
                           preferred_element_type=F32)


def _dot_tn(a, b):
    return lax.dot_general(a.astype(MXU_DTYPE), b.astype(MXU_DTYPE), (((0,), (0,)), ((), ())),
                           preferred_element_type=F32)


def _sigmoid(x):
    return 0.5 + 0.5 * jnp.tanh(0.5 * x)


_GELU_C = math.sqrt(2.0 / math.pi)


def _gelu_parts(x):
    x2 = x * x
    t = jnp.tanh(_GELU_C * (x + 0.044715 * (x2 * x)))
    cdf = 0.5 * (1.0 + t)
    g = x * cdf
    dg = cdf + 0.5 * x * (1.0 - t * t) * (_GELU_C * (1.0 + 3.0 * 0.044715 * x2))
    return g, dg


def _gelu(x):
    t = jnp.tanh(_GELU_C * (x + 0.044715 * (x * x * x)))
    return x * (0.5 * (1.0 + t))


def _softplus(x):
    return jnp.maximum(x, 0.0) + jnp.log1p(jnp.exp(-jnp.abs(x)))


def _bcast_row(x, r, rows=SUBLANES):
    return jnp.broadcast_to(x[r:r + 1, :], (rows, x.shape[1]))


def _colsum8(x):
    return jnp.broadcast_to(jnp.sum(x, axis=0, keepdims=True), (SUBLANES, x.shape[1]))


def _shift_down(prev8, tile, s):
    if s == 0:
        return tile
    ext = jnp.concatenate([prev8, tile], axis=0)
    return pltpu.roll(ext, s, 0)[SUBLANES:, :]


def _shift_up(tile, next8, s):
    if s == 0:
        return tile
    ext = jnp.concatenate([tile, next8], axis=0)
    return pltpu.roll(ext, SUBLANES - s, 0)[SUBLANES:, :]


def _group_scan(a, b, reverse, fill=lambda: None):
    n = a.shape[0]
    row = lax.broadcasted_iota(jnp.int32, a.shape, 0) & (SUBLANES - 1)
    for s in (1, 2, 4):
        if s > 1:
            fill()
        shift = (n - s) if reverse else s
        a_sh = pltpu.roll(a, shift, 0)
        b_sh = pltpu.roll(b, shift, 0)
        m = (row <= SUBLANES - 1 - s) if reverse else (row >= s)
        b = jnp.where(m, a * b_sh + b, b)
        a = jnp.where(m, a * a_sh, a)
    return a, b


def _carry_scan(a_ref, b_ref, out_ref, carry0, reverse):
    n_groups = a_ref.shape[0] // SUBLANES
    carry = carry0
    for i in range(n_groups):
        r0 = ((n_groups - 1 - i) if reverse else i) * SUBLANES
        hg = a_ref[r0:r0 + SUBLANES, :] * carry + b_ref[r0:r0 + SUBLANES, :]
        out_ref[r0:r0 + SUBLANES, :] = hg
        carry = _bcast_row(hg, 0 if reverse else SUBLANES - 1)
    return carry


def _rms_fwd(h, gain):
    rstd = lax.rsqrt(jnp.mean(h * h, axis=-1, keepdims=True) + NORM_EPS)
    n = h * rstd
    return n, rstd, n * gain


def _rms_bwd(dy, n, rstd, gain):
    dn = dy * gain
    dh = rstd * (dn - n * jnp.mean(dn * n, axis=-1, keepdims=True))
    return dh, _colsum8(dy * n)


def _halo_rows(dtype):
    return SUBLANES * (4 // jnp.dtype(dtype).itemsize)


def _halo_map(tile_rows, col, halo_rows=SUBLANES):
    per = tile_rows // halo_rows
    return lambda i: (jnp.maximum(i * per - 1, 0), col)


def _resident(shape):
    return pl.BlockSpec(shape, lambda *_: (0,) * len(shape), pipeline_mode=pl.Buffered(1))


def _place():
    x, y, c = lax.axis_index("x"), lax.axis_index("y"), lax.axis_index("c")
    chips = [(1 - x, y), (x, 1 - y), (1 - x, 1 - y)]
    return x, y, c, chips


def _dev(x, y, c):
    return 4 * x + 2 * y + c


class _Copy:
    def __init__(self, make):
        self.make = make

    def start(self):
        self.make().start()

    def wait(self):
        self.make().wait()

    def wait_send(self):
        self.make().wait_send()

    def wait_recv(self):
        self.make().wait_recv()


def _remote(src, dst, send_sem, recv_sem, to):
    return _Copy(lambda: pltpu.make_async_remote_copy(src_ref=src, dst_ref=dst, send_sem=send_sem, recv_sem=recv_sem,
                                                      device_id=to, device_id_type=MESH))


def _local(src, dst, sem):
    return _Copy(lambda: pltpu.make_async_copy(src, dst, sem))


class _Comm:
    def __init__(self, operands, out_shape, sems, descs, aliases=()):
        self.operands, self.out_shape, self.sems, self.descs, self.aliases = operands, out_shape, sems, descs, aliases

    def start(self, ins, outs, sems):
        local, sends, _ = self.descs(ins, outs, sems)
        for cp in sends + local:
            cp.start()

    def wait(self, ins, outs, sems):
        local, sends, recvs = self.descs(ins, outs, sems)
        for cp in recvs:
            cp.wait_recv()
        for cp in sends:
            cp.wait_send()
        for cp in local:
            cp.wait()


def _gather_first(shards):
    n = len(shards)

    def descs(ins, outs, sems):
        send, recv, loc = sems
        x, y, c, chips = _place()
        me = _dev(x, y, c)
        targets = [(x, y, 1 - c)] + [(*chip, c) for chip in chips]
        local, sends, recvs = [], [], []
        for t in range(n):
            local.append(_local(ins[t], outs[t].at[me], loc.at[t]))
            for k, to in enumerate(targets):
                i = 4 * t + k
                sends.append(_remote(ins[t], outs[t].at[me], send.at[i], recv.at[i], to))
                recvs.append(_remote(ins[t], outs[t].at[_dev(*to)], send.at[i], recv.at[i], to))
        return local, sends, recvs

    return _Comm(list(shards), [jax.ShapeDtypeStruct((N_DEV,) + s.shape, s.dtype) for s in shards],
                 [pltpu.SemaphoreType.DMA((4 * n,)), pltpu.SemaphoreType.DMA((4 * n,)), pltpu.SemaphoreType.DMA((n,))],
                 descs)


def _gather_second(gathered):
    n = len(gathered)

    def descs(ins, outs, sems):
        send, recv = sems
        x, y, c, chips = _place()
        sends, recvs = [], []
        for t in range(n):
            for j, chip in enumerate(chips):
                i = 3 * t + j
                have, get = _dev(*chip, c), _dev(*chip, 1 - c)
                sends.append(_remote(outs[t].at[have], outs[t].at[have], send.at[i], recv.at[i], (x, y, 1 - c)))
                recvs.append(_remote(outs[t].at[have], outs[t].at[get], send.at[i], recv.at[i], (x, y, 1 - c)))
        return [], sends, recvs

    return _Comm(list(gathered), [jax.ShapeDtypeStruct(g.shape, g.dtype) for g in gathered],
                 [pltpu.SemaphoreType.DMA((3 * n,)), pltpu.SemaphoreType.DMA((3 * n,))], descs,
                 aliases=[(t, t) for t in range(n)])


def _pair_exchange(blocks):
    n = len(blocks)

    def descs(ins, outs, sems):
        send, recv = sems
        x, y, c, _ = _place()
        sends, recvs = [], []
        for t in range(n):
            for q in range(N_CHIPS):
                i = N_CHIPS * t + q
                cp = _remote(ins[t].at[2 * q + 1 - c], outs[t].at[q], send.at[i], recv.at[i], (x, y, 1 - c))
                sends.append(cp)
                recvs.append(cp)
        return [], sends, recvs

    return _Comm(list(blocks), [jax.ShapeDtypeStruct((N_CHIPS,) + b.shape[1:], b.dtype) for b in blocks],
                 [pltpu.SemaphoreType.DMA((N_CHIPS * n,)), pltpu.SemaphoreType.DMA((N_CHIPS * n,))], descs)


def _chip_exchange(blocks):
    n = len(blocks)

    def descs(ins, outs, sems):
        send, recv, loc = sems
        x, y, c, chips = _place()
        me = 2 * x + y
        local, sends, recvs = [], [], []
        for t in range(n):
            local.append(_local(ins[t].at[me], outs[t].at[me], loc.at[t]))
            for j, (px, py) in enumerate(chips):
                i = 3 * t + j
                q = 2 * px + py
                sends.append(_remote(ins[t].at[q], outs[t].at[me], send.at[i], recv.at[i], (px, py, c)))
                recvs.append(_remote(ins[t].at[q], outs[t].at[q], send.at[i], recv.at[i], (px, py, c)))
        return local, sends, recvs

    return _Comm(list(blocks), [jax.ShapeDtypeStruct(b.shape, b.dtype) for b in blocks],
                 [pltpu.SemaphoreType.DMA((3 * n,)), pltpu.SemaphoreType.DMA((3 * n,)), pltpu.SemaphoreType.DMA((n,))],
                 descs)


def _both(a, b):
    na, oa, sa = len(a.operands), len(a.out_shape), len(a.sems)

    def descs(ins, outs, sems):
        local_a, sends_a, recvs_a = a.descs(ins[:na], outs[:oa], sems[:sa])
        local_b, sends_b, recvs_b = b.descs(ins[na:], outs[oa:], sems[sa:])
        return local_a + local_b, sends_a + sends_b, recvs_a + recvs_b

    return _Comm(a.operands + b.operands, a.out_shape + b.out_shape, a.sems + b.sems, descs,
                 aliases=list(a.aliases) + [(na + i, oa + o) for i, o in b.aliases])


def _offset(comm, lo):
    return _Comm(comm.operands, comm.out_shape, comm.sems,
                 lambda ins, outs, sems: comm.descs(ins[lo:], outs[lo:], sems), comm.aliases)


def _run_comms(comms, name):
    first = comms[0]
    n_in, n_out = len(first.operands), len(first.out_shape)

    def body(*refs):
        ins, outs, sems = refs[:n_in], refs[n_in:n_in + n_out], list(refs[n_in + n_out:])
        for k, comm in enumerate(comms):
            mine = [sems.pop(0) for _ in comm.sems]
            comm.start(ins if k == 0 else outs, outs, mine)
            comm.wait(ins if k == 0 else outs, outs, mine)

    outs = pl.pallas_call(
        body, name=name, out_shape=first.out_shape, in_specs=[ANY] * n_in, out_specs=[ANY] * n_out,
        scratch_shapes=[s for comm in comms for s in comm.sems],
    )(*first.operands)
    return list(outs)


def _call(body, *, name, grid, in_specs, out_specs, out_shape, operands, scratch_shapes=(), comm=None, aliases=None):
    sem = ("arbitrary",) * len(grid)
    params = pltpu.CompilerParams(dimension_semantics=sem, vmem_limit_bytes=VMEM_LIMIT)
    aliases = dict(aliases or {})
    if comm is None:
        return pl.pallas_call(body, name=name, grid=grid, in_specs=in_specs, out_specs=out_specs, out_shape=out_shape,
                              scratch_shapes=list(scratch_shapes), input_output_aliases=aliases,
                              compiler_params=params)(*operands)
    n_in, n_out, n_scr = len(in_specs), len(out_specs), len(scratch_shapes)
    c_in, c_out = len(comm.operands), len(comm.out_shape)

    def wrapped(*refs):
        refs = list(refs)
        ins, refs = refs[:n_in], refs[n_in:]
        cins, refs = refs[:c_in], refs[c_in:]
        outs, refs = refs[:n_out], refs[n_out:]
        couts, refs = refs[:c_out], refs[c_out:]
        scr, csems = refs[:n_scr], refs[n_scr:]
        first = last = None
        for axis, size in enumerate(grid):
            at_first, at_last = pl.program_id(axis) == 0, pl.program_id(axis) == size - 1
            first = at_first if first is None else first & at_first
            last = at_last if last is None else last & at_last

        @pl.when(first)
        def _():
            comm.start(cins, couts, csems)

        body(*ins, *outs, *scr)

        @pl.when(last)
        def _():
            comm.wait(cins, couts, csems)

    res = pl.pallas_call(
        wrapped, name=name, grid=grid, in_specs=list(in_specs) + [ANY] * c_in, out_specs=list(out_specs) + [ANY] * c_out,
        out_shape=list(out_shape) + list(comm.out_shape), scratch_shapes=list(scratch_shapes) + list(comm.sems),
        input_output_aliases={**aliases, **{n_in + i: n_out + o for i, o in comm.aliases}}, compiler_params=params,
    )(*operands, *comm.operands)
    return list(res[:n_out]), list(res[n_out:])


def _mm_tn(a, b, name, blocks=1, tk=2048, comm=None):
    t, m = a.shape
    n = b.shape[1]
    tk = min(tk, t)
    nk = t // tk
    cb = n // blocks
    per = max(1, 768 // cb) if blocks > 1 else 1
    tn = per * cb if blocks > 1 else min(1024, n)
    tm = min(1024, m)
    assert blocks == 1 or tm == m

    def body(a_ref, b_ref, o_ref, acc):
        k = pl.program_id(2)

        @pl.when(k == 0)
        def _():
            acc[...] = jnp.zeros_like(acc)
        acc[...] += _dot_tn(a_ref[...], b_ref[...])

        @pl.when(k == nk - 1)
        def _():
            if blocks == 1:
                o_ref[...] = acc[...].astype(o_ref.dtype)
            else:
                for s in range(per):
                    o_ref[s] = acc[:, s * cb:(s + 1) * cb].astype(o_ref.dtype)

    if blocks == 1:
        out_spec = pl.BlockSpec((tm, tn), lambda i, j, k: (i, j))
        out_shape = jax.ShapeDtypeStruct((m, n), GRAD_DTYPE)
    else:
        out_spec = pl.BlockSpec((per, m, cb), lambda i, j, k: (j, 0, 0))
        out_shape = jax.ShapeDtypeStruct((blocks, m, cb), GRAD_DTYPE)
    res = _call(body, name=name, grid=(m // tm, n // tn, nk), comm=comm,
                in_specs=[pl.BlockSpec((tk, tm), lambda i, j, k: (k, i)), pl.BlockSpec((tk, tn), lambda i, j, k: (k, j))],
                out_specs=[out_spec], out_shape=[out_shape], operands=[a, b],
                scratch_shapes=[pltpu.VMEM((tm, tn), F32)])
    return res[0] if comm is None else (res[0][0], res[1])


def _inproj_fwd(x, g1, w_blocks, comm):
    t = x.shape[0]
    tm = min(512, t)
    nb, _, cb = w_blocks.shape

    def body(x_ref, g_ref, w_ref, u_ref, p_ref):
        _, _, u = _rms_fwd(x_ref[...], g_ref[...])
        u = u.astype(MXU_DTYPE)
        u_ref[...] = u
        for d in range(nb):
            p_ref[:, d * cb:(d + 1) * cb] = _dot(u, w_ref[d]).astype(p_ref.dtype)

    return _call(body, name="inproj_fwd", grid=(t // tm,), comm=comm,
                 in_specs=[pl.BlockSpec((tm, D_MODEL), lambda i: (i, 0)), pl.BlockSpec((1, D_MODEL), lambda i: (0, 0)),
                           _resident(w_blocks.shape)],
                 out_specs=[pl.BlockSpec((tm, D_MODEL), lambda i: (i, 0)), pl.BlockSpec((tm, D_IN), lambda i: (i, 0))],
                 out_shape=[jax.ShapeDtypeStruct((t, D_MODEL), MXU_DTYPE), jax.ShapeDtypeStruct((t, D_IN), MXU_DTYPE)],
                 operands=[x, g1, w_blocks])


def _lru_gates(xc, wa, ba, wx, bx, sp, fill=lambda: None):
    r = _sigmoid(_dot(xc, wa) + ba)
    fill()
    ig = _sigmoid(_dot(xc, wx) + bx)
    fill()
    log_a = (-LRU_C) * r * sp
    a = jnp.exp(log_a)
    m = jnp.sqrt(-jnp.tanh(log_a) * (a * a + 1.0))
    return r, ig, a, m


def _fused(parts, name, comm=None):
    grid = parts[0]["grid"]
    assert all(p["grid"] == grid for p in parts)
    counts = [(len(p["in_specs"]), len(p["out_specs"]), len(p.get("scratch_shapes", ()))) for p in parts]

    def body(*refs):
        refs = list(refs)
        groups = []
        for kind in range(3):
            taken = []
            for c in counts:
                taken.append(refs[:c[kind]])
                refs = refs[c[kind]:]
            groups.append(taken)
        ins, outs, scr = groups
        pending = []

        def fill(n=None):
            for _ in range(share if n is None else n):
                if pending:
                    pending.pop(0)()

        ctx = dict(outs=outs, scratch=scr, fill=fill)
        run = lambda key: [p[key](*ins[k], *outs[k], *scr[k], ctx) for k, p in enumerate(parts) if key in p]
        run("head")
        for pieces in run("units"):
            pending.extend(pieces)
        points = sum(p.get("fill_points", 0) for p in parts)
        share = -(-len(pending) // max(points, 1))
        run("body")
        fill(len(pending))
        run("tail")

    cat = lambda key: [x for p in parts for x in p.get(key, ())]
    res = _call(body, name=name, grid=grid, comm=comm, in_specs=cat("in_specs"), out_specs=cat("out_specs"),
                out_shape=cat("out_shape"), scratch_shapes=cat("scratch_shapes"), operands=cat("operands"))
    outs, side = (res if comm is not None else (res, None))
    split, at = [], 0
    for _, n_out, _ in counts:
        split.append(list(outs[at:at + n_out]))
        at += n_out
    return split if comm is None else (split, side)


def _lru_fwd(proj, conv_w, conv_b, wa, ba, wx, bx, lam, gain):
    t = proj.shape[0]
    tm = min(256, t)
    c = D_LRU

    def body(x_ref, xh_ref, g_ref, cw_ref, cb_ref, wa_ref, ba_ref, wx_ref, bx_ref, lam_ref, gain_ref,
             xc_ref, h_ref, y_ref, a_scr, b_scr, carry, ctx):
        fill = ctx["fill"]
        i = pl.program_id(0)

        @pl.when(i == 0)
        def _():
            carry[...] = jnp.zeros_like(carry)

        fill()
        x = x_ref[...].astype(F32)
        prev = jnp.where(i == 0, 0.0, xh_ref[...].astype(F32)[-SUBLANES:, :])
        cw = cw_ref[...]
        xc = cb_ref[...] + cw[LRU_CONV - 1:LRU_CONV, :] * x
        for k in range(LRU_CONV - 1):
            xc = xc + cw[k:k + 1, :] * _shift_down(prev, x, LRU_CONV - 1 - k)
        xc_ref[...] = xc
        fill()
        sp = _softplus(-lam_ref[...])
        _, ig, a, m = _lru_gates(xc, wa_ref[...], ba_ref[...], wx_ref[...], bx_ref[...], sp, fill)
        fill()
        ga, gb = _group_scan(a, m * (ig * xc), reverse=False, fill=fill)
        a_scr[...] = ga
        b_scr[...] = gb
        fill()
        carry[...] = _carry_scan(a_scr, b_scr, h_ref, carry[...], reverse=False)
        fill()
        z = h_ref[...] * _gelu(g_ref[...].astype(F32))
        fill()
        _, _, y = _rms_fwd(z, gain_ref[...])
        y_ref[...] = y.astype(y_ref.dtype)

    row = lambda i: (i, 0)
    full = lambda i: (0, 0)
    vec = pl.BlockSpec((1, c), full)
    hb = _halo_rows(proj.dtype)
    return dict(body=body, grid=(t // tm,), fill_points=10,
                in_specs=[pl.BlockSpec((tm, c), row), pl.BlockSpec((hb, c), _halo_map(tm, 0, hb)),
                          pl.BlockSpec((tm, c), lambda i: (i, 1)),
                          pl.BlockSpec((LRU_CONV, c), full), vec, pl.BlockSpec((c, c), full), vec,
                          pl.BlockSpec((c, c), full), vec, vec, vec],
                out_specs=[pl.BlockSpec((tm, c), row), pl.BlockSpec((tm, c), row), pl.BlockSpec((tm, c), row)],
                out_shape=[jax.ShapeDtypeStruct((t, c), F32), jax.ShapeDtypeStruct((t, c), F32),
                           jax.ShapeDtypeStruct((t, c), MXU_DTYPE)],
                scratch_shapes=[pltpu.VMEM((tm, c), F32), pltpu.VMEM((tm, c), F32), pltpu.VMEM((SUBLANES, c), F32)],
                operands=[proj, proj, proj, conv_w, conv_b, wa, ba, wx, bx, lam, gain])


def _ret_consts():
    c = RET_CHUNK
    log_g = jnp.log1p(-jnp.exp2(-5.0 - jnp.arange(RET_HEADS, dtype=F32)))
    idx = jnp.arange(c, dtype=F32)
    diff = idx[:, None] - idx[None, :]
    decay = jnp.where(diff[None] >= 0, jnp.exp(jnp.maximum(diff, 0.0)[None] * log_g[:, None, None]), 0.0)
    zeta = jnp.exp((c - 1 - idx)[None, :] * log_g[:, None])
    xi = jnp.exp((idx + 1.0)[None, :] * log_g[:, None])
    spread = lambda v: jnp.repeat(v.T, RET_HEAD_DIM, axis=1)
    log_g_np = np.log1p(-np.exp2(-5.0 - np.arange(RET_HEADS, dtype=np.float32))).astype(np.float32)
    g_chunk = [float(np.exp(np.float32(c) * lg)) for lg in log_g_np]
    return decay, spread(xi), spread(zeta), g_chunk


def _rope_tables(t):
    pos = np.arange(t, dtype=np.float32)
    inv_freq = np.float32(ROPE_BASE) ** (-np.arange(0, RET_HEAD_DIM, 2, dtype=np.float32) / np.float32(RET_HEAD_DIM))
    ang = (pos[:, None] * inv_freq.astype(np.float32)[None, :]).astype(np.float32).astype(np.float64)
    cos, sin = np.cos(ang).astype(np.float32), np.sin(ang).astype(np.float32)
    return jnp.asarray(np.concatenate([cos, cos], axis=-1)), jnp.asarray(np.concatenate([-sin, sin], axis=-1))


def _rope(x, cos2, sin_signed):
    return x * cos2 + pltpu.roll(x, RET_HEAD_DIM // 2, 1) * sin_signed


def _rope_bwd(d, cos2, sin_signed):
    return d * cos2 + pltpu.roll(d * sin_signed, RET_HEAD_DIM // 2, 1)


RET_SCALE = RET_HEAD_DIM ** -0.5


RET_CHUNKS_PER_STEP = 2


def _ret_fwd(proj, cos2, sin_signed, gain):
    t = proj.shape[0]
    c, d, nh = RET_CHUNK, RET_HEAD_DIM, RET_HEADS
    n_chunks = t // c
    per = RET_CHUNKS_PER_STEP if n_chunks % RET_CHUNKS_PER_STEP == 0 else 1
    rows = per * c
    decay, xi, zeta, g_chunk = _ret_consts()

    def units(qk_ref, vg_ref, cos_ref, sin_ref, dec_ref, xi_ref, zeta_ref, gain_ref, o_ref, y_ref, st_ref, state, ctx):
        cur = [None] * nh

        def start():
            @pl.when(pl.program_id(0) == 0)
            def _():
                state[...] = jnp.zeros_like(state)
            for h in range(nh):
                cur[h] = state[h]

        def retain(s, h, keep):
            rs = slice(s * c, (s + 1) * c)
            cos2, sin_s = cos_ref[rs, :], sin_ref[rs, :]
            lo = h * d
            q = _rope(qk_ref[rs, lo:lo + d].astype(F32), cos2, sin_s)
            k = _rope(qk_ref[rs, D_RET + lo:D_RET + lo + d].astype(F32), cos2, sin_s) * RET_SCALE
            v = vg_ref[rs, lo:lo + d]
            s_prev = cur[h]
            st_ref[s, h] = s_prev
            scores = _dot_nt(q, k) * dec_ref[h]
            o = _dot(scores, v) + _dot(q * xi_ref[:, lo:lo + d], s_prev)
            cur[h] = s_prev * g_chunk[h] + _dot_tn(k * zeta_ref[:, lo:lo + d], v)
            o_ref[rs, lo:lo + d] = o
            keep["o"] = o

        def normalise(s, h, keep):
            rs = slice(s * c, (s + 1) * c)
            lo = h * d
            o = keep["o"]
            g = vg_ref[rs, D_RET + lo:D_RET + lo + d].astype(F32)
            mu = jnp.mean(o, axis=-1, keepdims=True)
            oc = o - mu
            on = oc * lax.rsqrt(jnp.mean(oc * oc, axis=-1, keepdims=True) + NORM_EPS)
            y_ref[rs, lo:lo + d] = (on * gain_ref[:, lo:lo + d] * (g * _sigmoid(g))).astype(y_ref.dtype)

        def end():
            for h in range(nh):
                state[h] = cur[h]

        pieces = [start]
        for s in range(per):
            for h in range(nh):
                keep = {}
                pieces += [lambda s=s, h=h, keep=keep: retain(s, h, keep),
                           lambda s=s, h=h, keep=keep: normalise(s, h, keep)]
        return pieces + [end]

    full2 = lambda i: (0, 0)
    return dict(units=units, grid=(n_chunks // per,),
                in_specs=[pl.BlockSpec((rows, 2 * D_RET), lambda i: (i, 1)),
                          pl.BlockSpec((rows, 2 * D_RET), lambda i: (i, 2)),
                          pl.BlockSpec((rows, d), lambda i: (i, 0)), pl.BlockSpec((rows, d), lambda i: (i, 0)),
                          pl.BlockSpec((nh, c, c), lambda i: (0, 0, 0)), pl.BlockSpec((c, D_RET), full2),
                          pl.BlockSpec((c, D_RET), full2), pl.BlockSpec((1, D_RET), full2)],
                out_specs=[pl.BlockSpec((rows, D_RET), lambda i: (i, 0)), pl.BlockSpec((rows, D_RET), lambda i: (i, 0)),
                           pl.BlockSpec((per, nh, d, d), lambda i: (i, 0, 0, 0))],
                out_shape=[jax.ShapeDtypeStruct((t, D_RET), F32), jax.ShapeDtypeStruct((t, D_RET), MXU_DTYPE),
                           jax.ShapeDtypeStruct((n_chunks, nh, d, d), F32)],
                scratch_shapes=[pltpu.VMEM((nh, d, d), F32)],
                operands=[proj, proj, cos2, sin_signed, decay, xi, zeta, gain])


def _outproj_fwd(x, y_lru, y_ret, w_out, g2, comm):
    t = x.shape[0]
    tm = min(512, t)

    def body(x_ref, yl_ref, yr_ref, w_ref, g_ref, h1_ref, u2_ref):
        h1 = x_ref[...] + _dot(yl_ref[...], w_ref[:D_LRU, :]) + _dot(yr_ref[...], w_ref[D_LRU:, :])
        h1_ref[...] = h1
        _, _, u = _rms_fwd(h1, g_ref[...])
        u2_ref[...] = u.astype(u2_ref.dtype)

    row = lambda i: (i, 0)
    return _call(body, name="outproj_fwd", grid=(t // tm,), comm=comm,
                 in_specs=[pl.BlockSpec((tm, D_MODEL), row), pl.BlockSpec((tm, D_LRU), row), pl.BlockSpec((tm, D_RET), row),
                           _resident((D_MODEL, D_MODEL)), pl.BlockSpec((1, D_MODEL), lambda i: (0, 0))],
                 out_specs=[pl.BlockSpec((tm, D_MODEL), row), pl.BlockSpec((tm, D_MODEL), row)],
                 out_shape=[jax.ShapeDtypeStruct((t, D_MODEL), F32), jax.ShapeDtypeStruct((t, D_MODEL), MXU_DTYPE)],
                 operands=[x, y_lru, y_ret, w_out, g2])


FFN_TN = 768
FFN_NJ = D_FF // FFN_TN
FFN_GROUP = 4


def _ffn_fwd(u2, w_blocks, conv_w, conv_b, w_down, h1, gf, target):
    t = u2.shape[0]
    tm = min(256, t)
    tn, nj, group = FFN_TN, FFN_NJ, FFN_GROUP
    ng, tw = nj // group, group * tn
    hb = _halo_rows(u2.dtype)
    assert w_blocks.shape == (2 * nj, D_MODEL, tn)

    def project(u_ext, w, col, up_ref, conv_ref, cw_ref, cb_ref, first):
        ext = _dot(u_ext, w)
        x = ext[hb:, :]
        up_ref[:, col] = x.astype(up_ref.dtype)
        prev = jnp.where(first, 0.0, ext[hb - SUBLANES:hb, :])
        cw = cw_ref[:, col]
        y = cb_ref[:, col] + cw[FFN_CONV - 1:FFN_CONV, :] * x
        for k in range(FFN_CONV - 1):
            y = y + cw[k:k + 1, :] * _shift_down(prev, x, FFN_CONV - 1 - k)
        conv_ref[:, col] = y.astype(conv_ref.dtype)
        return y

    def body(u_ref, uh_ref, w_ref, cwa_ref, cwv_ref, cba_ref, cbv_ref, wd_ref, h1_ref, gf_ref, tg_ref,
             upa_ref, upv_ref, ca_ref, cv_ref, act_ref, dh_ref, dhb_ref, dgf_ref, loss_ref, acc):
        i, jg = pl.program_id(0), pl.program_id(1)

        @pl.when((i == 0) & (jg == 0))
        def _():
            dgf_ref[...] = jnp.zeros_like(dgf_ref)
            loss_ref[...] = jnp.zeros_like(loss_ref)

        @pl.when(jg == 0)
        def _():
            acc[...] = jnp.zeros_like(acc)

        u_ext = jnp.concatenate([uh_ref[...], u_ref[...]], axis=0)
        down = None
        for jj in range(group):
            col = slice(jj * tn, (jj + 1) * tn)
            j = jg * group + jj
            a = project(u_ext, w_ref[j], col, upa_ref, ca_ref, cwa_ref, cba_ref, i == 0)
            v = project(u_ext, w_ref[nj + j], col, upv_ref, cv_ref, cwv_ref, cbv_ref, i == 0)
            act = (_gelu(a) * v).astype(act_ref.dtype)
            act_ref[:, col] = act
            part = _dot(act, wd_ref[pl.ds(pl.multiple_of(j * tn, tn), tn), :])
            down = part if down is None else down + part
        acc[...] += down

        @pl.when(jg == ng - 1)
        def _():
            n, rstd, y = _rms_fwd(h1_ref[...] + acc[...], gf_ref[...])
            err = y - tg_ref[...]
            loss_ref[...] += (0.5 / D_MODEL) * jnp.sum(err * err)
            dh, dgf = _rms_bwd(err * (1.0 / D_MODEL), n, rstd, gf_ref[...])
            dgf_ref[...] += dgf
            dh_ref[...] = dh
            dhb_ref[...] = dh.astype(dhb_ref.dtype)

    per = tm // hb
    row = lambda i, j: (i, 0)
    const = lambda i, j: (0, 0)
    tile = pl.BlockSpec((tm, tw), lambda i, j: (i, j))
    return _call(body, name="ffn_fwd", grid=(t // tm, ng),
                 in_specs=[pl.BlockSpec((tm, D_MODEL), row),
                           pl.BlockSpec((hb, D_MODEL), lambda i, j: (jnp.maximum(i * per - 1, 0), 0)),
                           _resident(w_blocks.shape),
                           pl.BlockSpec((FFN_CONV, tw), lambda i, j: (0, j)),
                           pl.BlockSpec((FFN_CONV, tw), lambda i, j: (0, j + ng)),
                           pl.BlockSpec((1, tw), lambda i, j: (0, j)), pl.BlockSpec((1, tw), lambda i, j: (0, j + ng)),
                           _resident((D_FF, D_MODEL)),
                           pl.BlockSpec((tm, D_MODEL), row), pl.BlockSpec((1, D_MODEL), const),
                           pl.BlockSpec((tm, D_MODEL), row)],
                 out_specs=[tile] * 5 + [pl.BlockSpec((tm, D_MODEL), row),
                            pl.BlockSpec((tm, D_MODEL), row), pl.BlockSpec((SUBLANES, D_MODEL), const),
                            pl.BlockSpec((SUBLANES, LANES), const)],
                 out_shape=[jax.ShapeDtypeStruct((t, D_FF), MXU_DTYPE)] * 5 + [
                            jax.ShapeDtypeStruct((t, D_MODEL), F32),
                            jax.ShapeDtypeStruct((t, D_MODEL), MXU_DTYPE), jax.ShapeDtypeStruct((SUBLANES, D_MODEL), F32),
                            jax.ShapeDtypeStruct((SUBLANES, LANES), F32)],
                 scratch_shapes=[pltpu.VMEM((tm, D_MODEL), F32)],
                 operands=[u2, u2, w_blocks, conv_w, conv_w, conv_b, conv_b, w_down, h1, gf, target])


FFN_ACC_ROWS = SUBLANES * (FFN_CONV + 1)


def _ffn_bwd(dh2, dh2_b, w_down, up_a, up_v, conv_a, conv_v, conv_w, w_up_blocks, h1, g2, comm):
    t = up_a.shape[0]
    tm = min(256, t)
    tn, nj, group = FFN_TN, FFN_NJ, FFN_GROUP
    ng, tw = nj // group, group * tn
    ni = t // tm
    assert w_up_blocks.shape == (2 * nj, D_MODEL, tn)

    def conv_bwd(dy, x, cw, acc_ref, carry_ref, dup_ref, col):
        nxt = carry_ref[...]
        carry_ref[...] = dy[:SUBLANES, :]
        ahead = [_shift_up(dy, nxt, FFN_CONV - 1 - k) for k in range(FFN_CONV)]
        dx = cw[FFN_CONV - 1:FFN_CONV, :] * dy
        for k in range(FFN_CONV - 1):
            dx = dx + cw[k:k + 1, :] * ahead[k]
        dx = dx.astype(dup_ref.dtype)
        dup_ref[:, col] = dx
        for k in range(FFN_CONV):
            acc_ref[k * SUBLANES:(k + 1) * SUBLANES, :] += _colsum8(ahead[k] * x)
        acc_ref[FFN_CONV * SUBLANES:, :] += _colsum8(dy)
        return dx

    def body(dh_ref, dhb_ref, wd_ref, ua_ref, uv_ref, ca_ref, cv_ref, cwa_ref, cwv_ref, wu_ref, h1_ref, g2_ref,
             dua_ref, duv_ref, acca_ref, accv_ref, dh1_ref, dh1b_ref, dg2_ref, carry_a, carry_v, du):
        i, jg = pl.program_id(0), pl.program_id(1)

        @pl.when((i == 0) & (jg == 0))
        def _():
            for ref in (acca_ref, accv_ref, carry_a, carry_v, dg2_ref):
                ref[...] = jnp.zeros_like(ref)

        dhb = dhb_ref[...]
        part = None
        for jj in range(group):
            col = slice(jj * tn, (jj + 1) * tn)
            j = jg * group + jj
            v = cv_ref[:, col].astype(F32)
            g, dg = _gelu_parts(ca_ref[:, col].astype(F32))
            dact = _dot_nt(dhb, wd_ref[pl.ds(pl.multiple_of(j * tn, tn), tn), :])
            da = conv_bwd(dact * v * dg, ua_ref[:, col].astype(F32), cwa_ref[:, col], acca_ref.at[j], carry_a.at[j],
                          dua_ref, col)
            dv = conv_bwd(dact * g, uv_ref[:, col].astype(F32), cwv_ref[:, col], accv_ref.at[j], carry_v.at[j],
                          duv_ref, col)
            term = _dot_nt(da, wu_ref[j]) + _dot_nt(dv, wu_ref[nj + j])
            part = term if part is None else part + term

        @pl.when(jg == 0)
        def _():
            du[...] = part

        @pl.when(jg > 0)
        def _():
            du[...] += part

        @pl.when(jg == ng - 1)
        def _():
            n, rstd, _ = _rms_fwd(h1_ref[...], g2_ref[...])
            dh1, dg2 = _rms_bwd(du[...], n, rstd, g2_ref[...])
            dh1 = dh1 + dh_ref[...]
            dg2_ref[...] += dg2
            dh1_ref[...] = dh1
            dh1b_ref[...] = dh1.astype(dh1b_ref.dtype)

    row = lambda i, j: (ni - 1 - i, 0)
    const = lambda i, j: (0, 0)
    tile = pl.BlockSpec((tm, tw), lambda i, j: (ni - 1 - i, j))
    acc = pl.BlockSpec((nj, FFN_ACC_ROWS, tn), lambda i, j: (0, 0, 0))
    return _call(body, name="ffn_bwd", grid=(ni, ng), comm=comm,
                 in_specs=[pl.BlockSpec((tm, D_MODEL), row), pl.BlockSpec((tm, D_MODEL), row),
                           _resident((D_FF, D_MODEL)), tile, tile, tile, tile,
                           pl.BlockSpec((FFN_CONV, tw), lambda i, j: (0, j)),
                           pl.BlockSpec((FFN_CONV, tw), lambda i, j: (0, j + ng)),
                           _resident(w_up_blocks.shape), pl.BlockSpec((tm, D_MODEL), row),
                           pl.BlockSpec((1, D_MODEL), const)],
                 out_specs=[tile, tile, acc, acc, pl.BlockSpec((tm, D_MODEL), row), pl.BlockSpec((tm, D_MODEL), row),
                            pl.BlockSpec((SUBLANES, D_MODEL), const)],
                 out_shape=[jax.ShapeDtypeStruct((t, D_FF), MXU_DTYPE), jax.ShapeDtypeStruct((t, D_FF), MXU_DTYPE),
                            jax.ShapeDtypeStruct((nj, FFN_ACC_ROWS, tn), F32),
                            jax.ShapeDtypeStruct((nj, FFN_ACC_ROWS, tn), F32),
                            jax.ShapeDtypeStruct((t, D_MODEL), F32), jax.ShapeDtypeStruct((t, D_MODEL), MXU_DTYPE),
                            jax.ShapeDtypeStruct((SUBLANES, D_MODEL), F32)],
                 scratch_shapes=[pltpu.VMEM((nj, SUBLANES, tn), F32), pltpu.VMEM((nj, SUBLANES, tn), F32),
                                 pltpu.VMEM((tm, D_MODEL), F32)],
                 operands=[dh2, dh2_b, w_down, up_a, up_v, conv_a, conv_v, conv_w, conv_w, w_up_blocks, h1, g2])


def _ret_bwd(proj, cos2, sin_signed, gain, o, states, dmix_at):
    t = proj.shape[0]
    c, d, nh = RET_CHUNK, RET_HEAD_DIM, RET_HEADS
    n_chunks = t // c
    per = RET_CHUNKS_PER_STEP if n_chunks % RET_CHUNKS_PER_STEP == 0 else 1
    rows = per * c
    n_steps = n_chunks // per
    decay, xi, zeta, g_chunk = _ret_consts()
    base = 2 * D_LRU

    def units(qk_ref, vg_ref, cos_ref, sin_ref, dec_ref, xi_ref, zeta_ref, gain_ref, o_ref, st_ref,
              dp_ref, dgain_ref, gstate, ctx):
        cur = [None] * nh
        dmix = ctx["scratch"][dmix_at[0]][dmix_at[1]]

        def start():
            @pl.when(pl.program_id(0) == 0)
            def _():
                gstate[...] = jnp.zeros_like(gstate)
                dgain_ref[...] = jnp.zeros_like(dgain_ref)
            for h in range(nh):
                cur[h] = gstate[h]

        def gate_and_norm(s, h, keep):
            rs = slice(s * c, (s + 1) * c)
            lo = h * d
            g = vg_ref[rs, D_RET + lo:D_RET + lo + d].astype(F32)
            gain_h = gain_ref[:, lo:lo + d]
            dy = dmix[rs, D_LRU + lo:D_LRU + lo + d]
            sg = _sigmoid(g)
            o_h = o_ref[rs, lo:lo + d]
            oc = o_h - jnp.mean(o_h, axis=-1, keepdims=True)
            rstd = lax.rsqrt(jnp.mean(oc * oc, axis=-1, keepdims=True) + NORM_EPS)
            on = oc * rstd
            at = base + 3 * D_RET + lo
            dp_ref[rs, at:at + d] = (dy * on * gain_h * (sg * (1.0 + g * (1.0 - sg)))).astype(dp_ref.dtype)
            don_g = dy * (g * sg)
            dgain_ref[:, lo:lo + d] += _colsum8(don_g * on)
            don = don_g * gain_h
            keep["do"] = rstd * (don - jnp.mean(don, axis=-1, keepdims=True)
                                 - on * jnp.mean(don * on, axis=-1, keepdims=True))

        def retain(s, h, keep):
            rs = slice(s * c, (s + 1) * c)
            cos2, sin_s = cos_ref[rs, :], sin_ref[rs, :]
            lo = h * d
            q = _rope(qk_ref[rs, lo:lo + d].astype(F32), cos2, sin_s)
            k = _rope(qk_ref[rs, D_RET + lo:D_RET + lo + d].astype(F32), cos2, sin_s) * RET_SCALE
            v = vg_ref[rs, lo:lo + d]
            xi_h, zeta_h, dec = xi_ref[:, lo:lo + d], zeta_ref[:, lo:lo + d], dec_ref[h]
            do = keep["do"]
            s_prev = st_ref[s, h]
            g_next = cur[h]
            p = _dot_nt(q, k) * dec
            dpm = _dot_nt(do, v) * dec
            keep["dq"] = _dot(dpm, k) + _dot_nt(do, s_prev) * xi_h
            keep["dk"] = _dot_tn(dpm, q) + _dot_nt(v, g_next) * zeta_h
            dv = _dot_tn(p, do) + _dot(k * zeta_h, g_next)
            cur[h] = g_next * g_chunk[h] + _dot_tn(q * xi_h, do)
            at = base + 2 * D_RET + lo
            dp_ref[rs, at:at + d] = dv.astype(dp_ref.dtype)

        def unrope(s, h, keep):
            rs = slice(s * c, (s + 1) * c)
            cos2, sin_s = cos_ref[rs, :], sin_ref[rs, :]
            lo = h * d
            dp_ref[rs, base + lo:base + lo + d] = _rope_bwd(keep["dq"], cos2, sin_s).astype(dp_ref.dtype)
            at = base + D_RET + lo
            dp_ref[rs, at:at + d] = _rope_bwd(keep["dk"] * RET_SCALE, cos2, sin_s).astype(dp_ref.dtype)

        def end():
            for h in range(nh):
                gstate[h] = cur[h]

        pieces = [start]
        for s in reversed(range(per)):
            for h in range(nh):
                keep = {}
                pieces += [lambda s=s, h=h, keep=keep, f=f: f(s, h, keep) for f in (gate_and_norm, retain, unrope)]
        return pieces + [end]

    rev = lambda col: (lambda i: (n_steps - 1 - i, col))
    full2 = lambda i: (0, 0)
    return dict(units=units, grid=(n_steps,),
                in_specs=[pl.BlockSpec((rows, 2 * D_RET), rev(1)), pl.BlockSpec((rows, 2 * D_RET), rev(2)),
                          pl.BlockSpec((rows, d), rev(0)), pl.BlockSpec((rows, d), rev(0)),
                          pl.BlockSpec((nh, c, c), lambda i: (0, 0, 0)), pl.BlockSpec((c, D_RET), full2),
                          pl.BlockSpec((c, D_RET), full2), pl.BlockSpec((1, D_RET), full2),
                          pl.BlockSpec((rows, D_RET), rev(0)),
                          pl.BlockSpec((per, nh, d, d), lambda i: (n_steps - 1 - i, 0, 0, 0))],
                out_specs=[pl.BlockSpec((rows, D_IN), rev(0)), pl.BlockSpec((SUBLANES, D_RET), full2)],
                out_shape=[jax.ShapeDtypeStruct((t, D_IN), MXU_DTYPE), jax.ShapeDtypeStruct((SUBLANES, D_RET), F32)],
                scratch_shapes=[pltpu.VMEM((nh, d, d), F32)],
                operands=[proj, proj, cos2, sin_signed, decay, xi, zeta, gain, o, states])


LRU_ACC = {"conv_w": 0, "conv_b": LRU_CONV, "gate_a_b": LRU_CONV + 1, "gate_x_b": LRU_CONV + 2,
           "lambda": LRU_CONV + 3, "norm_gain": LRU_CONV + 4}
LRU_ACC_ROWS = SUBLANES * (LRU_CONV + 5)


def _lru_bwd(proj, xc_all, h_all, conv_w, wa, ba, wx, bx, lam, gain, dproj_part, dmix_at):
    t = proj.shape[0]
    tm = min(256, t)
    c = D_LRU
    ni = t // tm

    def body(x_ref, xh_ref, g_ref, xc_ref, h_ref, hh_ref, cw_ref, wa_ref, ba_ref, wx_ref, bx_ref, lam_ref,
             gain_ref, acc_ref, dwa_ref, dwx_ref, a_scr, b_scr, mu_scr, carry_mu, carry_dxc, ctx):
        dp_ref = ctx["outs"][dproj_part][0]
        dmix = ctx["scratch"][dmix_at[0]][dmix_at[1]]
        fill = ctx["fill"]
        i = pl.program_id(0)
        r = ni - 1 - i

        @pl.when(i == 0)
        def _():
            acc_ref[...] = jnp.zeros_like(acc_ref)
            dwa_ref[...] = jnp.zeros_like(dwa_ref)
            dwx_ref[...] = jnp.zeros_like(dwx_ref)
            carry_mu[...] = jnp.zeros_like(carry_mu)
            carry_dxc[...] = jnp.zeros_like(carry_dxc)

        def add(name, val, k=0):
            lo = (LRU_ACC[name] + k) * SUBLANES
            acc_ref[lo:lo + SUBLANES, :] += _colsum8(val)

        fill()
        xc, h = xc_ref[...], h_ref[...]
        lam_v = lam_ref[...]
        sp = _softplus(-lam_v)
        rg, ig, a, m = _lru_gates(xc, wa_ref[...], ba_ref[...], wx_ref[...], bx_ref[...], sp, fill)
        gl, dgl = _gelu_parts(g_ref[...].astype(F32))
        fill()
        zn, rstd, _ = _rms_fwd(h * gl, gain_ref[...])
        dy = dmix[:, :c]
        dz, dgain = _rms_bwd(dy, zn, rstd, gain_ref[...])
        lo = LRU_ACC["norm_gain"] * SUBLANES
        acc_ref[lo:lo + SUBLANES, :] += dgain
        dp_ref[:, c:2 * c] = (dz * h * dgl).astype(dp_ref.dtype)
        dh = dz * gl
        fill()
        ga, gb = _group_scan(a, a * dh, reverse=True, fill=fill)
        a_scr[...] = ga
        b_scr[...] = gb
        mu_next_tile = carry_mu[...]
        carry_mu[...] = _carry_scan(a_scr, b_scr, mu_scr, mu_next_tile, reverse=True)
        fill()
        lam_t = dh + _shift_up(mu_scr[...], mu_next_tile, 1)
        h_prev = _shift_down(jnp.where(r == 0, 0.0, hh_ref[...]), h, 1)
        da = lam_t * h_prev
        dig = lam_t * m * xc
        dxc = lam_t * m * ig
        dlog_a = da * a - (lam_t * ig * xc) * (a * a) / m
        fill()
        dpr = dlog_a * ((-LRU_C) * sp) * rg * (1.0 - rg)
        add("lambda", dlog_a * ((-LRU_C) * rg) * (-_sigmoid(-lam_v)))
        dpi = dig * ig * (1.0 - ig)
        add("gate_a_b", dpr)
        add("gate_x_b", dpi)
        fill()
        dwa_ref[...] += _dot_tn(xc, dpr)
        dwx_ref[...] += _dot_tn(xc, dpi)
        dxc = dxc + _dot_nt(dpr, wa_ref[...]) + _dot_nt(dpi, wx_ref[...])
        fill()
        add("conv_b", dxc)
        x = x_ref[...].astype(F32)
        prev = jnp.where(r == 0, 0.0, xh_ref[...].astype(F32)[-SUBLANES:, :])
        cw = cw_ref[...]
        nxt = carry_dxc[...]
        carry_dxc[...] = dxc[:SUBLANES, :]
        dx = cw[LRU_CONV - 1:LRU_CONV, :] * dxc
        for k in range(LRU_CONV - 1):
            dx = dx + cw[k:k + 1, :] * _shift_up(dxc, nxt, LRU_CONV - 1 - k)
        fill()
        for k in range(LRU_CONV):
            add("conv_w", dxc * _shift_down(prev, x, LRU_CONV - 1 - k), k)
        dp_ref[:, :c] = dx.astype(dp_ref.dtype)

    hb = _halo_rows(proj.dtype)
    rev = lambda col: (lambda i: (ni - 1 - i, col))
    halo = lambda rows: (lambda i: (jnp.maximum((ni - 1 - i) * (tm // rows) - 1, 0), 0))
    full = lambda i: (0, 0)
    vec = pl.BlockSpec((1, c), full)
    mat = pl.BlockSpec((c, c), full)
    return dict(body=body, grid=(ni,), fill_points=12,
                in_specs=[pl.BlockSpec((tm, c), rev(0)), pl.BlockSpec((hb, c), halo(hb)), pl.BlockSpec((tm, c), rev(1)),
                          pl.BlockSpec((tm, c), rev(0)), pl.BlockSpec((tm, c), rev(0)),
                          pl.BlockSpec((SUBLANES, c), halo(SUBLANES)),
                          pl.BlockSpec((LRU_CONV, c), full), mat, vec, mat, vec, vec, vec],
                out_specs=[pl.BlockSpec((LRU_ACC_ROWS, c), full), mat, mat],
                out_shape=[jax.ShapeDtypeStruct((LRU_ACC_ROWS, c), F32), jax.ShapeDtypeStruct((c, c), F32),
                           jax.ShapeDtypeStruct((c, c), F32)],
                scratch_shapes=[pltpu.VMEM((tm, c), F32), pltpu.VMEM((tm, c), F32), pltpu.VMEM((tm, c), F32),
                                pltpu.VMEM((SUBLANES, c), F32), pltpu.VMEM((SUBLANES, c), F32)],
                operands=[proj, proj, proj, xc_all, h_all, h_all, conv_w, wa, ba, wx, bx, lam, gain])


def _mix_proj_bwd(dh1, dh1_b, w_out, w_in_blocks, x, g1, dproj_part):
    t = x.shape[0]
    tm = min(256, t)
    ni = t // tm
    nb, _, cb = w_in_blocks.shape
    first_free = -(-2 * D_LRU // cb)
    du = [None]

    def term(dp_ref, w_ref, d):
        part = _dot_nt(dp_ref[:, d * cb:(d + 1) * cb], w_ref[d])
        du[0] = part if du[0] is None else du[0] + part

    def head(dh_ref, dhb_ref, wo_ref, wi_ref, x_ref, g_ref, gx_ref, dg_ref, dmix, ctx):
        @pl.when(pl.program_id(0) == 0)
        def _():
            dg_ref[...] = jnp.zeros_like(dg_ref)
        dmix[...] = _dot_nt(dhb_ref[...], wo_ref[...])
        du[0] = None

    def units(dh_ref, dhb_ref, wo_ref, wi_ref, x_ref, g_ref, gx_ref, dg_ref, dmix, ctx):
        dp_ref = ctx["outs"][dproj_part][0]
        return [lambda d=d: term(dp_ref, wi_ref, d) for d in range(first_free, nb)]

    def tail(dh_ref, dhb_ref, wo_ref, wi_ref, x_ref, g_ref, gx_ref, dg_ref, dmix, ctx):
        dp_ref = ctx["outs"][dproj_part][0]
        for d in range(first_free):
            term(dp_ref, wi_ref, d)
        n, rstd, _ = _rms_fwd(x_ref[...], g_ref[...])
        dx, dg = _rms_bwd(du[0], n, rstd, g_ref[...])
        dg_ref[...] += dg
        gx_ref[...] = dx + dh_ref[...]

    row = lambda i: (ni - 1 - i, 0)
    const = lambda i: (0, 0)
    tile = pl.BlockSpec((tm, D_MODEL), row)
    return dict(head=head, units=units, tail=tail, grid=(ni,),
                in_specs=[tile, tile, _resident(w_out.shape), _resident(w_in_blocks.shape), tile,
                          pl.BlockSpec((1, D_MODEL), const)],
                out_specs=[tile, pl.BlockSpec((SUBLANES, D_MODEL), const)],
                out_shape=[jax.ShapeDtypeStruct((t, D_MODEL), F32), jax.ShapeDtypeStruct((SUBLANES, D_MODEL), F32)],
                scratch_shapes=[pltpu.VMEM((tm, D_MODEL), F32)],
                operands=[dh1, dh1_b, w_out, w_in_blocks, x, g1])


def _pair_sum(core, a, b, name):
    n, r, c = b.shape
    spec = pl.BlockSpec((None, r, c), lambda q, core: (q, 0, 0))

    def body(core_ref, a_ref, b_ref, o_ref):
        o_ref[...] = (a_ref[...].astype(F32) + b_ref[...].astype(F32)).astype(o_ref.dtype)

    return pl.pallas_call(
        body, name=name,
        grid_spec=pltpu.PrefetchScalarGridSpec(
            num_scalar_prefetch=1, grid=(n,),
            in_specs=[pl.BlockSpec((None, r, c), lambda q, core: (2 * q + core[0], 0, 0)), spec], out_specs=spec),
        out_shape=jax.ShapeDtypeStruct(b.shape, b.dtype),
        compiler_params=pltpu.CompilerParams(dimension_semantics=("arbitrary",), vmem_limit_bytes=VMEM_LIMIT),
    )(core, a, b)


ADAMW_BLOCK_BYTES = 4 * 1024 * 1024


def _sum_adamw(parts, w, m, v, name):
    n_parts, r, c = parts.shape
    tr = r
    while n_parts * tr * c * parts.dtype.itemsize > ADAMW_BLOCK_BYTES and tr % (4 * SUBLANES) == 0:
        tr //= 2

    def body(p_ref, w_ref, m_ref, v_ref, g_ref, d_ref, nm_ref, nv_ref):
        g = p_ref[0].astype(F32)
        for s in range(1, n_parts):
            g = g + p_ref[s].astype(F32)
        nm = ADAM_B1 * m_ref[...] + (1.0 - ADAM_B1) * g
        nv = ADAM_B2 * v_ref[...] + (1.0 - ADAM_B2) * (g * g)
        m_hat = nm / (1.0 - ADAM_B1 ** ADAM_STEP)
        v_hat = nv / (1.0 - ADAM_B2 ** ADAM_STEP)
        g_ref[...] = g
        d_ref[...] = -ADAM_LR * (m_hat / (jnp.sqrt(v_hat) + ADAM_EPS) + ADAM_WD * w_ref[...])
        nm_ref[...] = nm
        nv_ref[...] = nv

    row = pl.BlockSpec((tr, c), lambda i: (i, 0))
    return _call(body, name=name, grid=(r // tr,),
                 in_specs=[pl.BlockSpec((n_parts, tr, c), lambda i: (0, i, 0)), row, row, row],
                 out_specs=[row, row, row, row], out_shape=[jax.ShapeDtypeStruct((r, c), F32)] * 4,
                 operands=[parts, w, m, v])


MATRICES = ("w_in", "w_out", "ffn_up_w", "ffn_down_w")
CONVS = ("lru_conv_w", "ffn_conv_w")
REPLICATED = ("norm1_gain", "lru_conv_b", "lru_gate_a_w", "lru_gate_a_b", "lru_gate_x_w", "lru_gate_x_b", "lru_lambda",
              "lru_norm_gain", "ret_norm_gain", "norm2_gain", "ffn_conv_b", "final_norm_gain")
WEIGHTS = ("norm1_gain", "w_in", "lru_conv_w", "lru_conv_b", "lru_gate_a_w", "lru_gate_a_b", "lru_gate_x_w",
           "lru_gate_x_b", "lru_lambda", "lru_norm_gain", "ret_norm_gain", "w_out", "norm2_gain", "ffn_up_w",
           "ffn_conv_w", "ffn_conv_b", "ffn_down_w", "final_norm_gain")


def _rows(a, pad_to):
    a = a.reshape(-1, LANES)
    pad = (-a.shape[0]) % pad_to
    return jnp.pad(a, ((0, pad), (0, 0))) if pad else a


def _pack(arrays, pad_to):
    rows, layout, at = [], [], 0
    for a in arrays:
        r = _rows(a, pad_to)
        layout.append((at, a.size // LANES, a.shape))
        rows.append(r)
        at += r.shape[0]
    return jnp.concatenate(rows, axis=0), layout


def _unpack(packed, layout):
    lead = packed.shape[:-2]
    return [packed[..., at:at + n, :].reshape(lead + shape) for at, n, shape in layout]


def _conv_rows(lru, ffn, dtype, pad_to):
    lead = lru.shape[:-2]
    flat = jnp.concatenate([lru.reshape(lead + (-1,)), ffn.reshape(lead + (-1,))], axis=-1).astype(dtype)
    rows = flat.shape[-1] // LANES
    pad = (-rows) % pad_to
    return jnp.pad(flat.reshape(lead + (rows, LANES)), [(0, 0)] * len(lead) + [(0, pad), (0, 0)])


def _column_blocks(full):
    r, c = full.shape
    return full.reshape(r, N_DEV, c // N_DEV).transpose(1, 0, 2)


def _block_diag(w):
    nh, d, _ = w.shape
    eye = jnp.eye(nh, dtype=w.dtype)
    return (w[:, :, None, :] * eye[:, None, :, None]).reshape(nh * d, nh * d)


def _diag_blocks(dense, nh):
    d = dense.shape[0] // nh
    blocks = dense.reshape(nh, d, nh, d)
    return jnp.stack([blocks[h, :, h, :] for h in range(nh)], axis=0)


def kernel(x, norm1_gain, w_in, lru_conv_w, lru_conv_b, lru_gate_a_w, lru_gate_a_b, lru_gate_x_w, lru_gate_x_b, lru_lambda, lru_norm_gain, ret_norm_gain, w_out, norm2_gain, ffn_up_w, ffn_conv_w, ffn_conv_b, ffn_down_w, final_norm_gain, loss_target, m_norm1_gain, m_w_in, m_lru_conv_w, m_lru_conv_b, m_lru_gate_a_w, m_lru_gate_a_b, m_lru_gate_x_w, m_lru_gate_x_b, m_lru_lambda, m_lru_norm_gain, m_ret_norm_gain, m_w_out, m_norm2_gain, m_ffn_up_w, m_ffn_conv_w, m_ffn_conv_b, m_ffn_down_w, m_final_norm_gain, v_norm1_gain, v_w_in, v_lru_conv_w, v_lru_conv_b, v_lru_gate_a_w, v_lru_gate_a_b, v_lru_gate_x_w, v_lru_gate_x_b, v_lru_lambda, v_lru_norm_gain, v_ret_norm_gain, v_w_out, v_norm2_gain, v_ffn_up_w, v_ffn_conv_w, v_ffn_conv_b, v_ffn_down_w, v_final_norm_gain):
    args = dict(locals())
    given = {n: args[n] for n in WEIGHTS}
    out_shape = {n: given[n].shape for n in WEIGHTS}

    def plain(a):
        return a.reshape(1, -1) if a.ndim <= 2 else a[0]

    w = {n: plain(given[n]) for n in WEIGHTS}
    mom_m = {n: plain(args["m_" + n]) for n in WEIGHTS}
    mom_v = {n: plain(args["v_" + n]) for n in WEIGHTS}
    x2, target = x[0], loss_target[0]
    t = x2.shape[0]
    core = lax.axis_index("c").astype(jnp.int32).reshape(1)
    res = {}

    conv_pad = _conv_rows(w["lru_conv_w"], w["ffn_conv_w"], F32, SUBLANES)
    first = _gather_first([w["w_in"].astype(MXU_DTYPE), conv_pad])
    w_in_blocks, conv_all = _run_comms([first, _gather_second(first.out_shape)], "w_in_all_gather")
    n_lru = w["lru_conv_w"].size
    conv_flat = conv_all.reshape(N_DEV, -1)
    lru_cw = conv_flat[:, :n_lru].reshape((N_DEV,) + w["lru_conv_w"].shape).transpose(1, 0, 2).reshape(LRU_CONV, D_LRU)
    ffn_cw = conv_flat[:, n_lru:n_lru + w["ffn_conv_w"].size].reshape((N_DEV,) + w["ffn_conv_w"].shape)
    ffn_cw = ffn_cw.transpose(1, 0, 2).reshape(FFN_CONV, 2 * D_FF)

    cos2, sin_signed = _rope_tables(t)
    wa = _block_diag(w["lru_gate_a_w"]).astype(MXU_DTYPE)
    wx = _block_diag(w["lru_gate_x_w"]).astype(MXU_DTYPE)
    gf = w["final_norm_gain"]

    early = _gather_first([w["ffn_up_w"].astype(MXU_DTYPE), w["w_out"].astype(MXU_DTYPE)])
    (u1, proj), (up_part, w_out_part) = _inproj_fwd(x2, w["norm1_gain"], w_in_blocks, early)
    ((xc, h_lru, y_lru), (o_ret, y_ret, states)), (up_blocks, w_out_blocks, down_part) = _fused(
        [_lru_fwd(proj, lru_cw, w["lru_conv_b"], wa, w["lru_gate_a_b"], wx, w["lru_gate_x_b"], w["lru_lambda"],
                  w["lru_norm_gain"]),
         _ret_fwd(proj, cos2, sin_signed, w["ret_norm_gain"])],
        "mix_fwd", _both(_gather_second([up_part, w_out_part]), _gather_first([w["ffn_down_w"].astype(MXU_DTYPE)])))
    w_out_full = w_out_blocks.reshape(D_MODEL, D_MODEL)

    (h1, u2), (down_blocks,) = _outproj_fwd(x2, y_lru, y_ret, w_out_full, w["norm2_gain"], _gather_second([down_part]))
    w_down_full = down_blocks.reshape(D_FF, D_MODEL)
    up_a, up_v, conv_a, conv_v, act, dh2, dh2_b, dgf, loss = _ffn_fwd(u2, up_blocks, ffn_cw, w["ffn_conv_b"], w_down_full,
                                                                      h1, gf, target)
    loss = lax.psum(loss[0, 0], ("x", "y", "c"))

    def to_owner_chips(blocks, names, tag):
        theirs = _run_comms([_pair_exchange(blocks)], "grads_pair_exchange_" + tag)
        return [_pair_sum(core, a, b, "grads_pair_sum_" + n) for n, a, b in zip(names, blocks, theirs)]

    def adamw(name, parts):
        res[name] = _sum_adamw(parts, w[name], mom_m[name], mom_v[name], "adamw_" + name)

    g = {"final_norm_gain": dgf[0]}
    g_down = _mm_tn(act, dh2_b, "ffn_down_wgrad").reshape(N_DEV, D_FF // N_DEV, D_MODEL)
    down_sums = to_owner_chips([g_down], ["ffn_down_w"], "down")
    (dup_a, dup_v, acc_a, acc_v, dh1, dh1_b, dg2), (down_parts,) = _ffn_bwd(
        dh2, dh2_b, w_down_full, up_a, up_v, conv_a, conv_v, ffn_cw, up_blocks, h1, w["norm2_gain"],
        _chip_exchange(down_sums))
    adamw("ffn_down_w", down_parts)
    per_col = lambda a: a[:, ::SUBLANES].transpose(1, 0, 2).reshape(FFN_CONV + 1, D_FF)
    acc = jnp.concatenate([per_col(acc_a), per_col(acc_v)], axis=1)
    g_ffn_cw, g["ffn_conv_b"] = acc[:FFN_CONV], acc[FFN_CONV:]
    g["norm2_gain"] = dg2[:1]
    g_up = jnp.concatenate([_mm_tn(u2, dup_a, "ffn_up_wgrad_a", blocks=N_DEV // 2),
                            _mm_tn(u2, dup_v, "ffn_up_wgrad_v", blocks=N_DEV // 2)], axis=0)
    g_out = jnp.concatenate([_mm_tn(y_lru, dh1_b, "w_out_wgrad_lru"), _mm_tn(y_ret, dh1_b, "w_out_wgrad_ret")], axis=0)
    mid_sums = to_owner_chips([g_up, g_out.reshape(N_DEV, D_MODEL // N_DEV, D_MODEL)], ["ffn_up_w", "w_out"], "mid")
    (dproj, dgain_ret), (grad_x, dg1), (lru_acc, dwa, dwx) = _fused(
        [_ret_bwd(proj, cos2, sin_signed, w["ret_norm_gain"], o_ret, states, dmix_at=(1, 0)),
         _mix_proj_bwd(dh1, dh1_b, w_out_full, w_in_blocks, x2, w["norm1_gain"], dproj_part=0),
         _lru_bwd(proj, xc, h_lru, lru_cw, wa, w["lru_gate_a_b"], wx, w["lru_gate_x_b"], w["lru_lambda"],
                  w["lru_norm_gain"], dproj_part=0, dmix_at=(1, 0))],
        "mix_bwd")
    g["norm1_gain"] = dg1[:1]
    g["ret_norm_gain"] = dgain_ret[:1]
    lru_acc = lru_acc[::SUBLANES]
    g_lru_cw = lru_acc[:LRU_CONV]
    for name in ("conv_b", "gate_a_b", "gate_x_b", "lambda", "norm_gain"):
        g["lru_" + name] = lru_acc[LRU_ACC[name]:LRU_ACC[name] + 1]
    g["lru_gate_a_w"] = _diag_blocks(dwa, LRU_HEADS)
    g["lru_gate_x_w"] = _diag_blocks(dwx, LRU_HEADS)
    g_in, (up_parts, out_parts) = _mm_tn(u1, dproj, "w_in_wgrad", blocks=N_DEV, comm=_chip_exchange(mid_sums))
    adamw("ffn_up_w", up_parts)
    adamw("w_out", out_parts)
    g_conv = _conv_rows(_column_blocks(g_lru_cw), _column_blocks(g_ffn_cw), GRAD_DTYPE, 2 * SUBLANES)
    in_sums = to_owner_chips([g_in, g_conv], ["w_in", "conv"], "in")

    rep_packed, rep_layout = _pack([g[n] for n in REPLICATED], SUBLANES)
    rep_first = _gather_first([rep_packed])
    in_parts, conv_parts, rep_parts = _run_comms(
        [_both(_chip_exchange(in_sums), rep_first), _offset(_gather_second(rep_first.out_shape), 2)],
        "last_grads_exchange")
    adamw("w_in", in_parts)
    pad16 = lambda d: _conv_rows(d["lru_conv_w"], d["ffn_conv_w"], F32, 2 * SUBLANES)
    conv_res = _sum_adamw(conv_parts, pad16(w), pad16(mom_m), pad16(mom_v), "adamw_conv")
    for n, lo, hi in (("lru_conv_w", 0, n_lru), ("ffn_conv_w", n_lru, n_lru + w["ffn_conv_w"].size)):
        res[n] = [r.reshape(-1)[lo:hi].reshape(w[n].shape) for r in conv_res]
    rep_res = _sum_adamw(rep_parts, *[_pack([d[n] for n in REPLICATED], SUBLANES)[0] for d in (w, mom_m, mom_v)],
                         "adamw_replicated")
    for k in range(4):
        for n, a in zip(REPLICATED, _unpack(rep_res[k], rep_layout)):
            res.setdefault(n, [None] * 4)[k] = a

    outs = [loss, grad_x[None]]
    for k in range(4):
        outs += [res[n][k].reshape(out_shape[n]) for n in WEIGHTS]
    return tuple(outs)
```

```python
import math

import numpy as np
import jax
import jax.numpy as jnp
from jax import lax
from jax.experimental import pallas as pl
from jax.experimental.pallas import tpu as pltpu

F32 = jnp.float32
BF16 = jnp.bfloat16
MXU_DTYPE = jnp.bfloat16
GRAD_DTYPE = jnp.bfloat16

N_DEV = 8
N_CHIPS = 4
D_MODEL = 1024
D_LRU = 512
LRU_HEADS = 8
LRU_CONV = 4
LRU_C = 8.0
D_RET = 512
RET_HEADS = 4
RET_HEAD_DIM = 128
RET_CHUNK = 128
ROPE_BASE = 10000.0
D_IN = 3072
D_FF = 3072
FFN_CONV = 3
NORM_EPS = 1e-6

ADAM_LR = 0.001
ADAM_B1 = 0.9
ADAM_B2 = 0.999
ADAM_EPS = 1e-08
ADAM_WD = 0.01
ADAM_STEP = 10

SUBLANES = 8
LANES = 128
VMEM_LIMIT = 48 * 1024 * 1024

MESH = pl.DeviceIdType.MESH
ANY = pl.BlockSpec(memory_space=pl.ANY)


def _dot(a, b):
    return jnp.dot(a.astype(MXU_DTYPE), b.astype(MXU_DTYPE), preferred_element_type=F32)


def _dot_nt(a, b):
    return lax.dot_general(a.astype(MXU_DTYPE), b.astype(MXU_DTYPE), (((1,), (1,)), ((), ())),
                           preferred_element_type=F32)


def _dot_tn(a, b):
    return lax.dot_general(a.astype(MXU_DTYPE), b.astype(MXU_DTYPE), (((0,), (0,)), ((), ())),
                           preferred_element_type=F32)


def _sigmoid(x):
    return 0.5 + 0.5 * jnp.tanh(0.5 * x)


_GELU_C = math.sqrt(2.0 / math.pi)
_GELU_C3 = _GELU_C * 0.044715


def _gelu_parts(x):
    x2 = x * x
    t = jnp.tanh(x * (_GELU_C + _GELU_C3 * x2))
    cdf = 0.5 + 0.5 * t
    g = x * cdf
    dg = cdf + (0.5 * x) * (1.0 - t * t) * (_GELU_C + (3.0 * _GELU_C3) * x2)
    return g, dg


def _gelu(x):
    t = jnp.tanh(x * (_GELU_C + _GELU_C3 * (x * x)))
    return x * (0.5 + 0.5 * t)


def _softplus(x):
    return jnp.maximum(x, 0.0) + jnp.log1p(jnp.exp(-jnp.abs(x)))


def _bcast_row(x, r, rows=SUBLANES):
    return jnp.broadcast_to(x[r:r + 1, :], (rows, x.shape[1]))


def _colsum8(x):
    return jnp.broadcast_to(jnp.sum(x, axis=0, keepdims=True), (SUBLANES, x.shape[1]))


def _shift_down(prev8, tile, s):
    if s == 0:
        return tile
    ext = jnp.concatenate([prev8, tile], axis=0)
    return pltpu.roll(ext, s, 0)[SUBLANES:, :]


def _shift_up(tile, next8, s):
    if s == 0:
        return tile
    ext = jnp.concatenate([tile, next8], axis=0)
    return pltpu.roll(ext, SUBLANES - s, 0)[SUBLANES:, :]


def _group_scan(a, b, reverse, fill=lambda: None):
    n = a.shape[0]
    row = lax.broadcasted_iota(jnp.int32, a.shape, 0) & (SUBLANES - 1)
    for s in (1, 2, 4):
        if s > 1:
            fill()
        shift = (n - s) if reverse else s
        a_sh = pltpu.roll(a, shift, 0)
        b_sh = pltpu.roll(b, shift, 0)
        m = (row <= SUBLANES - 1 - s) if reverse else (row >= s)
        b = jnp.where(m, a * b_sh + b, b)
        a = jnp.where(m, a * a_sh, a)
    return a, b


def _carry_scan(a_ref, b_ref, out_ref, carry0, reverse):
    n_groups = a_ref.shape[0] // SUBLANES
    carry = carry0
    for i in range(n_groups):
        r0 = ((n_groups - 1 - i) if reverse else i) * SUBLANES
        hg = a_ref[r0:r0 + SUBLANES, :] * carry + b_ref[r0:r0 + SUBLANES, :]
        out_ref[r0:r0 + SUBLANES, :] = hg
        carry = _bcast_row(hg, 0 if reverse else SUBLANES - 1)
    return carry


def _rms_fwd(h, gain):
    rstd = lax.rsqrt(jnp.mean(h * h, axis=-1, keepdims=True) + NORM_EPS)
    n = h * rstd
    return n, rstd, n * gain


def _rms_bwd(dy, n, rstd, gain):
    dn = dy * gain
    dh = rstd * (dn - n * jnp.mean(dn * n, axis=-1, keepdims=True))
    return dh, _colsum8(dy * n)


def _halo_rows(dtype):
    return SUBLANES * (4 // jnp.dtype(dtype).itemsize)


def _halo_map(tile_rows, col, halo_rows=SUBLANES):
    per = tile_rows // halo_rows
    return lambda i: (jnp.maximum(i * per - 1, 0), col)


def _resident(shape):
    return pl.BlockSpec(shape, lambda *_: (0,) * len(shape), pipeline_mode=pl.Buffered(1))


def _place():
    x, y, c = lax.axis_index("x"), lax.axis_index("y"), lax.axis_index("c")
    chips = [(1 - x, y), (x, 1 - y), (1 - x, 1 - y)]
    return x, y, c, chips


def _dev(x, y, c):
    return 4 * x + 2 * y + c


class _Copy:
    def __init__(self, make):
        self.make = make

    def start(self):
        self.make().start()

    def wait(self):
        self.make().wait()

    def wait_send(self):
        self.make().wait_send()

    def wait_recv(self):
        self.make().wait_recv()


def _remote(src, dst, send_sem, recv_sem, to):
    return _Copy(lambda: pltpu.make_async_remote_copy(src_ref=src, dst_ref=dst, send_sem=send_sem, recv_sem=recv_sem,
                                                      device_id=to, device_id_type=MESH))


def _local(src, dst, sem):
    return _Copy(lambda: pltpu.make_async_copy(src, dst, sem))


class _Comm:
    def __init__(self, operands, out_shape, sems, descs, aliases=()):
        self.operands, self.out_shape, self.sems, self.descs, self.aliases = operands, out_shape, sems, descs, aliases

    def start(self, ins, outs, sems):
        local, sends, _ = self.descs(ins, outs, sems)
        for cp in sends + local:
            cp.start()

    def wait(self, ins, outs, sems):
        local, sends, recvs = self.descs(ins, outs, sems)
        for cp in recvs:
            cp.wait_recv()
        for cp in sends:
            cp.wait_send()
        for cp in local:
            cp.wait()


def _gather_first(shards):
    n = len(shards)

    def descs(ins, outs, sems):
        send, recv, loc = sems
        x, y, c, chips = _place()
        me = _dev(x, y, c)
        targets = [(x, y, 1 - c)] + [(*chip, c) for chip in chips]
        local, sends, recvs = [], [], []
        for t in range(n):
            local.append(_local(ins[t], outs[t].at[me], loc.at[t]))
            for k, to in enumerate(targets):
                i = 4 * t + k
                sends.append(_remote(ins[t], outs[t].at[me], send.at[i], recv.at[i], to))
                recvs.append(_remote(ins[t], outs[t].at[_dev(*to)], send.at[i], recv.at[i], to))
        return local, sends, recvs

    return _Comm(list(shards), [jax.ShapeDtypeStruct((N_DEV,) + s.shape, s.dtype) for s in shards],
                 [pltpu.SemaphoreType.DMA((4 * n,)), pltpu.SemaphoreType.DMA((4 * n,)), pltpu.SemaphoreType.DMA((n,))],
                 descs)


def _gather_second(gathered):
    n = len(gathered)

    def descs(ins, outs, sems):
        send, recv = sems
        x, y, c, chips = _place()
        sends, recvs = [], []
        for t in range(n):
            for j, chip in enumerate(chips):
                i = 3 * t + j
                have, get = _dev(*chip, c), _dev(*chip, 1 - c)
                sends.append(_remote(outs[t].at[have], outs[t].at[have], send.at[i], recv.at[i], (x, y, 1 - c)))
                recvs.append(_remote(outs[t].at[have], outs[t].at[get], send.at[i], recv.at[i], (x, y, 1 - c)))
        return [], sends, recvs

    return _Comm(list(gathered), [jax.ShapeDtypeStruct(g.shape, g.dtype) for g in gathered],
                 [pltpu.SemaphoreType.DMA((3 * n,)), pltpu.SemaphoreType.DMA((3 * n,))], descs,
                 aliases=[(t, t) for t in range(n)])


def _pair_exchange(blocks):
    n = len(blocks)

    def descs(ins, outs, sems):
        send, recv = sems
        x, y, c, _ = _place()
        sends, recvs = [], []
        for t in range(n):
            for q in range(N_CHIPS):
                i = N_CHIPS * t + q
                cp = _remote(ins[t].at[2 * q + 1 - c], outs[t].at[q], send.at[i], recv.at[i], (x, y, 1 - c))
                sends.append(cp)
                recvs.append(cp)
        return [], sends, recvs

    return _Comm(list(blocks), [jax.ShapeDtypeStruct((N_CHIPS,) + b.shape[1:], b.dtype) for b in blocks],
                 [pltpu.SemaphoreType.DMA((N_CHIPS * n,)), pltpu.SemaphoreType.DMA((N_CHIPS * n,))], descs)


def _chip_exchange(blocks):
    n = len(blocks)

    def descs(ins, outs, sems):
        send, recv, loc = sems
        x, y, c, chips = _place()
        me = 2 * x + y
        local, sends, recvs = [], [], []
        for t in range(n):
            local.append(_local(ins[t].at[me], outs[t].at[me], loc.at[t]))
            for j, (px, py) in enumerate(chips):
                i = 3 * t + j
                q = 2 * px + py
                sends.append(_remote(ins[t].at[q], outs[t].at[me], send.at[i], recv.at[i], (px, py, c)))
                recvs.append(_remote(ins[t].at[q], outs[t].at[q], send.at[i], recv.at[i], (px, py, c)))
        return local, sends, recvs

    return _Comm(list(blocks), [jax.ShapeDtypeStruct(b.shape, b.dtype) for b in blocks],
                 [pltpu.SemaphoreType.DMA((3 * n,)), pltpu.SemaphoreType.DMA((3 * n,)), pltpu.SemaphoreType.DMA((n,))],
                 descs)


def _both(a, b):
    na, oa, sa = len(a.operands), len(a.out_shape), len(a.sems)

    def descs(ins, outs, sems):
        local_a, sends_a, recvs_a = a.descs(ins[:na], outs[:oa], sems[:sa])
        local_b, sends_b, recvs_b = b.descs(ins[na:], outs[oa:], sems[sa:])
        return local_a + local_b, sends_a + sends_b, recvs_a + recvs_b

    return _Comm(a.operands + b.operands, a.out_shape + b.out_shape, a.sems + b.sems, descs,
                 aliases=list(a.aliases) + [(na + i, oa + o) for i, o in b.aliases])


def _offset(comm, lo):
    return _Comm(comm.operands, comm.out_shape, comm.sems,
                 lambda ins, outs, sems: comm.descs(ins[lo:], outs[lo:], sems), comm.aliases)


def _run_comms(comms, name):
    first = comms[0]
    n_in, n_out = len(first.operands), len(first.out_shape)

    def body(*refs):
        ins, outs, sems = refs[:n_in], refs[n_in:n_in + n_out], list(refs[n_in + n_out:])
        for k, comm in enumerate(comms):
            mine = [sems.pop(0) for _ in comm.sems]
            comm.start(ins if k == 0 else outs, outs, mine)
            comm.wait(ins if k == 0 else outs, outs, mine)

    outs = pl.pallas_call(
        body, name=name, out_shape=first.out_shape, in_specs=[ANY] * n_in, out_specs=[ANY] * n_out,
        scratch_shapes=[s for comm in comms for s in comm.sems],
    )(*first.operands)
    return list(outs)


def _call(body, *, name, grid, in_specs, out_specs, out_shape, operands, scratch_shapes=(), comm=None, aliases=None):
    sem = ("arbitrary",) * len(grid)
    params = pltpu.CompilerParams(dimension_semantics=sem, vmem_limit_bytes=VMEM_LIMIT)
    aliases = dict(aliases or {})
    if comm is None:
        return pl.pallas_call(body, name=name, grid=grid, in_specs=in_specs, out_specs=out_specs, out_shape=out_shape,
                              scratch_shapes=list(scratch_shapes), input_output_aliases=aliases,
                              compiler_params=params)(*operands)
    n_in, n_out, n_scr = len(in_specs), len(out_specs), len(scratch_shapes)
    c_in, c_out = len(comm.operands), len(comm.out_shape)

    def wrapped(*refs):
        refs = list(refs)
        ins, refs = refs[:n_in], refs[n_in:]
        cins, refs = refs[:c_in], refs[c_in:]
        outs, refs = refs[:n_out], refs[n_out:]
        couts, refs = refs[:c_out], refs[c_out:]
        scr, csems = refs[:n_scr], refs[n_scr:]
        first = last = None
        for axis, size in enumerate(grid):
            at_first, at_last = pl.program_id(axis) == 0, pl.program_id(axis) == size - 1
            first = at_first if first is None else first & at_first
            last = at_last if last is None else last & at_last

        @pl.when(first)
        def _():
            comm.start(cins, couts, csems)

        body(*ins, *outs, *scr)

        @pl.when(last)
        def _():
            comm.wait(cins, couts, csems)

    res = pl.pallas_call(
        wrapped, name=name, grid=grid, in_specs=list(in_specs) + [ANY] * c_in, out_specs=list(out_specs) + [ANY] * c_out,
        out_shape=list(out_shape) + list(comm.out_shape), scratch_shapes=list(scratch_shapes) + list(comm.sems),
        input_output_aliases={**aliases, **{n_in + i: n_out + o for i, o in comm.aliases}}, compiler_params=params,
    )(*operands, *comm.operands)
    return list(res[:n_out]), list(res[n_out:])


def _mm_tn(a, b, name, blocks=1, tk=2048, comm=None):
    t, m = a.shape
    n = b.shape[1]
    tk = min(tk, t)
    nk = t // tk
    cb = n // blocks
    per = max(1, 768 // cb) if blocks > 1 else 1
    tn = per * cb if blocks > 1 else min(1024, n)
    tm = min(1024, m)
    assert blocks == 1 or tm == m

    def body(a_ref, b_ref, o_ref, acc):
        k = pl.program_id(2)

        @pl.when(k == 0)
        def _():
            acc[...] = jnp.zeros_like(acc)
        acc[...] += _dot_tn(a_ref[...], b_ref[...])

        @pl.when(k == nk - 1)
        def _():
            if blocks == 1:
                o_ref[...] = acc[...].astype(o_ref.dtype)
            else:
                for s in range(per):
                    o_ref[s] = acc[:, s * cb:(s + 1) * cb].astype(o_ref.dtype)

    if blocks == 1:
        out_spec = pl.BlockSpec((tm, tn), lambda i, j, k: (i, j))
        out_shape = jax.ShapeDtypeStruct((m, n), GRAD_DTYPE)
    else:
        out_spec = pl.BlockSpec((per, m, cb), lambda i, j, k: (j, 0, 0))
        out_shape = jax.ShapeDtypeStruct((blocks, m, cb), GRAD_DTYPE)
    res = _call(body, name=name, grid=(m // tm, n // tn, nk), comm=comm,
                in_specs=[pl.BlockSpec((tk, tm), lambda i, j, k: (k, i)), pl.BlockSpec((tk, tn), lambda i, j, k: (k, j))],
                out_specs=[out_spec], out_shape=[out_shape], operands=[a, b],
                scratch_shapes=[pltpu.VMEM((tm, tn), F32)])
    return res[0] if comm is None else (res[0][0], res[1])


INPROJ_TN = 1024


def _inproj_fwd(x, g1, w_in, comm):
    t = x.shape[0]
    tm = min(512, t)

    def body(x_ref, g_ref, w_ref, u_ref, p_ref):
        _, _, u = _rms_fwd(x_ref[...], g_ref[...])
        u = u.astype(MXU_DTYPE)
        u_ref[...] = u
        for lo in range(0, D_IN, INPROJ_TN):
            p_ref[:, lo:lo + INPROJ_TN] = _dot(u, w_ref[:, lo:lo + INPROJ_TN]).astype(p_ref.dtype)

    return _call(body, name="inproj_fwd", grid=(t // tm,), comm=comm,
                 in_specs=[pl.BlockSpec((tm, D_MODEL), lambda i: (i, 0)), pl.BlockSpec((1, D_MODEL), lambda i: (0, 0)),
                           _resident(w_in.shape)],
                 out_specs=[pl.BlockSpec((tm, D_MODEL), lambda i: (i, 0)), pl.BlockSpec((tm, D_IN), lambda i: (i, 0))],
                 out_shape=[jax.ShapeDtypeStruct((t, D_MODEL), MXU_DTYPE), jax.ShapeDtypeStruct((t, D_IN), MXU_DTYPE)],
                 operands=[x, g1, w_in])


def _lru_gates(xc, wa, ba, wx, bx, sp, fill=lambda: None):
    r = _sigmoid(_dot(xc, wa) + ba)
    fill()
    ig = _sigmoid(_dot(xc, wx) + bx)
    fill()
    log_a = (-LRU_C) * r * sp
    a = jnp.exp(log_a)
    m = jnp.sqrt(-jnp.tanh(log_a) * (a * a + 1.0))
    return r, ig, a, m


def _fused(parts, name, comm=None):
    grid = parts[0]["grid"]
    assert all(p["grid"] == grid for p in parts)
    counts = [(len(p["in_specs"]), len(p["out_specs"]), len(p.get("scratch_shapes", ()))) for p in parts]

    def body(*refs):
        refs = list(refs)
        groups = []
        for kind in range(3):
            taken = []
            for c in counts:
                taken.append(refs[:c[kind]])
                refs = refs[c[kind]:]
            groups.append(taken)
        ins, outs, scr = groups
        pending = []

        def fill(n=None):
            for _ in range(share if n is None else n):
                if pending:
                    pending.pop(0)()

        ctx = dict(outs=outs, scratch=scr, fill=fill)
        run = lambda key: [p[key](*ins[k], *outs[k], *scr[k], ctx) for k, p in enumerate(parts) if key in p]
        run("head")
        for pieces in run("units"):
            pending.extend(pieces)
        points = sum(p.get("fill_points", 0) for p in parts)
        share = -(-len(pending) // max(points, 1))
        run("body")
        fill(len(pending))
        run("tail")

    cat = lambda key: [x for p in parts for x in p.get(key, ())]
    res = _call(body, name=name, grid=grid, comm=comm, in_specs=cat("in_specs"), out_specs=cat("out_specs"),
                out_shape=cat("out_shape"), scratch_shapes=cat("scratch_shapes"), operands=cat("operands"))
    outs, side = (res if comm is not None else (res, None))
    split, at = [], 0
    for _, n_out, _ in counts:
        split.append(list(outs[at:at + n_out]))
        at += n_out
    return split if comm is None else (split, side)


def _lru_fwd(proj, conv_w, conv_b, wa, ba, wx, bx, lam, gain):
    t = proj.shape[0]
    tm = min(256, t)
    c = D_LRU

    def body(x_ref, xh_ref, g_ref, cw_ref, cb_ref, wa_ref, ba_ref, wx_ref, bx_ref, lam_ref, gain_ref,
             xc_ref, h_ref, y_ref, a_scr, b_scr, carry, ctx):
        fill = ctx["fill"]
        i = pl.program_id(0)

        @pl.when(i == 0)
        def _():
            carry[...] = jnp.zeros_like(carry)

        fill()
        x = x_ref[...].astype(F32)
        prev = jnp.where(i == 0, 0.0, xh_ref[...].astype(F32)[-SUBLANES:, :])
        cw = cw_ref[...]
        xc = cb_ref[...] + cw[LRU_CONV - 1:LRU_CONV, :] * x
        for k in range(LRU_CONV - 1):
            xc = xc + cw[k:k + 1, :] * _shift_down(prev, x, LRU_CONV - 1 - k)
        xc_ref[...] = xc
        fill()
        sp = _softplus(-lam_ref[...])
        _, ig, a, m = _lru_gates(xc, wa_ref[...], ba_ref[...], wx_ref[...], bx_ref[...], sp, fill)
        fill()
        ga, gb = _group_scan(a, m * (ig * xc), reverse=False, fill=fill)
        a_scr[...] = ga
        b_scr[...] = gb
        fill()
        carry[...] = _carry_scan(a_scr, b_scr, h_ref, carry[...], reverse=False)
        fill()
        z = h_ref[...] * _gelu(g_ref[...].astype(F32))
        fill()
        _, _, y = _rms_fwd(z, gain_ref[...])
        y_ref[...] = y.astype(y_ref.dtype)

    row = lambda i: (i, 0)
    full = lambda i: (0, 0)
    vec = pl.BlockSpec((1, c), full)
    hb = _halo_rows(proj.dtype)
    return dict(body=body, grid=(t // tm,), fill_points=10,
                in_specs=[pl.BlockSpec((tm, c), row), pl.BlockSpec((hb, c), _halo_map(tm, 0, hb)),
                          pl.BlockSpec((tm, c), lambda i: (i, 1)),
                          pl.BlockSpec((LRU_CONV, c), full), vec, pl.BlockSpec((c, c), full), vec,
                          pl.BlockSpec((c, c), full), vec, vec, vec],
                out_specs=[pl.BlockSpec((tm, c), row), pl.BlockSpec((tm, c), row), pl.BlockSpec((tm, c), row)],
                out_shape=[jax.ShapeDtypeStruct((t, c), F32), jax.ShapeDtypeStruct((t, c), F32),
                           jax.ShapeDtypeStruct((t, c), MXU_DTYPE)],
                scratch_shapes=[pltpu.VMEM((tm, c), F32), pltpu.VMEM((tm, c), F32), pltpu.VMEM((SUBLANES, c), F32)],
                operands=[proj, proj, proj, conv_w, conv_b, wa, ba, wx, bx, lam, gain])


def _ret_consts():
    c = RET_CHUNK
    log_g = jnp.log1p(-jnp.exp2(-5.0 - jnp.arange(RET_HEADS, dtype=F32)))
    idx = jnp.arange(c, dtype=F32)
    diff = idx[:, None] - idx[None, :]
    decay = jnp.where(diff[None] >= 0, jnp.exp(jnp.maximum(diff, 0.0)[None] * log_g[:, None, None]), 0.0)
    zeta = jnp.exp((c - 1 - idx)[None, :] * log_g[:, None])
    xi = jnp.exp((idx + 1.0)[None, :] * log_g[:, None])
    spread = lambda v: jnp.repeat(v.T, RET_HEAD_DIM, axis=1)
    log_g_np = np.log1p(-np.exp2(-5.0 - np.arange(RET_HEADS, dtype=np.float32))).astype(np.float32)
    g_chunk = [float(np.exp(np.float32(c) * lg)) for lg in log_g_np]
    return decay, spread(xi), spread(zeta), g_chunk


def _rope_tables(t):
    pos = np.arange(t, dtype=np.float32)
    inv_freq = np.float32(ROPE_BASE) ** (-np.arange(0, RET_HEAD_DIM, 2, dtype=np.float32) / np.float32(RET_HEAD_DIM))
    ang = (pos[:, None] * inv_freq.astype(np.float32)[None, :]).astype(np.float32).astype(np.float64)
    cos, sin = np.cos(ang).astype(np.float32), np.sin(ang).astype(np.float32)
    return jnp.asarray(np.concatenate([cos, cos], axis=-1)), jnp.asarray(np.concatenate([-sin, sin], axis=-1))


def _rope(x, cos2, sin_signed):
    return x * cos2 + pltpu.roll(x, RET_HEAD_DIM // 2, 1) * sin_signed


def _rope_bwd(d, cos2, sin_signed):
    return d * cos2 + pltpu.roll(d * sin_signed, RET_HEAD_DIM // 2, 1)


RET_SCALE = RET_HEAD_DIM ** -0.5


RET_CHUNKS_PER_STEP = 2


def _ret_fwd(proj, cos2, sin_signed, gain):
    t = proj.shape[0]
    c, d, nh = RET_CHUNK, RET_HEAD_DIM, RET_HEADS
    n_chunks = t // c
    per = RET_CHUNKS_PER_STEP if n_chunks % RET_CHUNKS_PER_STEP == 0 else 1
    rows = per * c
    decay, xi, zeta, g_chunk = _ret_consts()

    def units(qk_ref, vg_ref, cos_ref, sin_ref, dec_ref, xi_ref, zeta_ref, gain_ref, o_ref, y_ref, st_ref, state, ctx):
        cur = [None] * nh

        def start():
            @pl.when(pl.program_id(0) == 0)
            def _():
                state[...] = jnp.zeros_like(state)
            for h in range(nh):
                cur[h] = state[h]

        def retain(s, h, keep):
            rs = slice(s * c, (s + 1) * c)
            cos2, sin_s = cos_ref[rs, :], sin_ref[rs, :]
            lo = h * d
            q = _rope(qk_ref[rs, lo:lo + d].astype(F32), cos2, sin_s)
            k = _rope(qk_ref[rs, D_RET + lo:D_RET + lo + d].astype(F32), cos2, sin_s) * RET_SCALE
            v = vg_ref[rs, lo:lo + d]
            s_prev = cur[h]
            st_ref[s, h] = s_prev
            scores = _dot_nt(q, k) * dec_ref[h]
            o = _dot(scores, v) + _dot(q * xi_ref[:, lo:lo + d], s_prev)
            cur[h] = s_prev * g_chunk[h] + _dot_tn(k * zeta_ref[:, lo:lo + d], v)
            o_ref[rs, lo:lo + d] = o
            keep["o"] = o

        def normalise(s, h, keep):
            rs = slice(s * c, (s + 1) * c)
            lo = h * d
            o = keep["o"]
            g = vg_ref[rs, D_RET + lo:D_RET + lo + d].astype(F32)
            mu = jnp.mean(o, axis=-1, keepdims=True)
            oc = o - mu
            on = oc * lax.rsqrt(jnp.mean(oc * oc, axis=-1, keepdims=True) + NORM_EPS)
            y_ref[rs, lo:lo + d] = (on * gain_ref[:, lo:lo + d] * (g * _sigmoid(g))).astype(y_ref.dtype)

        def end():
            for h in range(nh):
                state[h] = cur[h]

        pieces = [start]
        for s in range(per):
            for h in range(nh):
                keep = {}
                pieces += [lambda s=s, h=h, keep=keep: retain(s, h, keep),
                           lambda s=s, h=h, keep=keep: normalise(s, h, keep)]
        return pieces + [end]

    full2 = lambda i: (0, 0)
    return dict(units=units, grid=(n_chunks // per,),
                in_specs=[pl.BlockSpec((rows, 2 * D_RET), lambda i: (i, 1)),
                          pl.BlockSpec((rows, 2 * D_RET), lambda i: (i, 2)),
                          pl.BlockSpec((rows, d), lambda i: (i, 0)), pl.BlockSpec((rows, d), lambda i: (i, 0)),
                          pl.BlockSpec((nh, c, c), lambda i: (0, 0, 0)), pl.BlockSpec((c, D_RET), full2),
                          pl.BlockSpec((c, D_RET), full2), pl.BlockSpec((1, D_RET), full2)],
                out_specs=[pl.BlockSpec((rows, D_RET), lambda i: (i, 0)), pl.BlockSpec((rows, D_RET), lambda i: (i, 0)),
                           pl.BlockSpec((per, nh, d, d), lambda i: (i, 0, 0, 0))],
                out_shape=[jax.ShapeDtypeStruct((t, D_RET), F32), jax.ShapeDtypeStruct((t, D_RET), MXU_DTYPE),
                           jax.ShapeDtypeStruct((n_chunks, nh, d, d), F32)],
                scratch_shapes=[pltpu.VMEM((nh, d, d), F32)],
                operands=[proj, proj, cos2, sin_signed, decay, xi, zeta, gain])


def _outproj_fwd(x, y_lru, y_ret, w_out, g2, comm):
    t = x.shape[0]
    tm = min(512, t)

    def body(x_ref, yl_ref, yr_ref, w_ref, g_ref, h1_ref, u2_ref):
        h1 = x_ref[...] + _dot(yl_ref[...], w_ref[:D_LRU, :]) + _dot(yr_ref[...], w_ref[D_LRU:, :])
        h1_ref[...] = h1
        _, _, u = _rms_fwd(h1, g_ref[...])
        u2_ref[...] = u.astype(u2_ref.dtype)

    row = lambda i: (i, 0)
    return _call(body, name="outproj_fwd", grid=(t // tm,), comm=comm,
                 in_specs=[pl.BlockSpec((tm, D_MODEL), row), pl.BlockSpec((tm, D_LRU), row), pl.BlockSpec((tm, D_RET), row),
                           _resident((D_MODEL, D_MODEL)), pl.BlockSpec((1, D_MODEL), lambda i: (0, 0))],
                 out_specs=[pl.BlockSpec((tm, D_MODEL), row), pl.BlockSpec((tm, D_MODEL), row)],
                 out_shape=[jax.ShapeDtypeStruct((t, D_MODEL), F32), jax.ShapeDtypeStruct((t, D_MODEL), MXU_DTYPE)],
                 operands=[x, y_lru, y_ret, w_out, g2])


FFN_TN = 768
FFN_NJ = D_FF // FFN_TN
FFN_GROUP = 4


def _ffn_fwd(u2, w_blocks, conv_w, conv_b, w_down, h1, gf, target):
    t = u2.shape[0]
    tm = min(256, t)
    tn, nj, group = FFN_TN, FFN_NJ, FFN_GROUP
    ng, tw = nj // group, group * tn
    hb = _halo_rows(u2.dtype)
    assert w_blocks.shape == (2 * nj, D_MODEL, tn)

    def project(u_ext, w, col, up_ref, conv_ref, cw_ref, cb_ref, first):
        ext = _dot(u_ext, w)
        x = ext[hb:, :]
        up_ref[:, col] = x.astype(up_ref.dtype)
        prev = jnp.where(first, 0.0, ext[hb - SUBLANES:hb, :])
        cw = cw_ref[:, col]
        y = cb_ref[:, col] + cw[FFN_CONV - 1:FFN_CONV, :] * x
        for k in range(FFN_CONV - 1):
            y = y + cw[k:k + 1, :] * _shift_down(prev, x, FFN_CONV - 1 - k)
        conv_ref[:, col] = y.astype(conv_ref.dtype)
        return y

    def body(u_ref, uh_ref, w_ref, cwa_ref, cwv_ref, cba_ref, cbv_ref, wd_ref, h1_ref, gf_ref, tg_ref,
             upa_ref, upv_ref, ca_ref, cv_ref, act_ref, dh_ref, dhb_ref, dgf_ref, loss_ref, acc):
        i, jg = pl.program_id(0), pl.program_id(1)

        @pl.when((i == 0) & (jg == 0))
        def _():
            dgf_ref[...] = jnp.zeros_like(dgf_ref)
            loss_ref[...] = jnp.zeros_like(loss_ref)

        @pl.when(jg == 0)
        def _():
            acc[...] = jnp.zeros_like(acc)

        u_ext = jnp.concatenate([uh_ref[...], u_ref[...]], axis=0)
        down = None
        for jj in range(group):
            col = slice(jj * tn, (jj + 1) * tn)
            j = jg * group + jj
            a = project(u_ext, w_ref[j], col, upa_ref, ca_ref, cwa_ref, cba_ref, i == 0)
            v = project(u_ext, w_ref[nj + j], col, upv_ref, cv_ref, cwv_ref, cbv_ref, i == 0)
            act = (_gelu(a) * v).astype(act_ref.dtype)
            act_ref[:, col] = act
            part = _dot(act, wd_ref[pl.ds(pl.multiple_of(j * tn, tn), tn), :])
            down = part if down is None else down + part
        acc[...] += down

        @pl.when(jg == ng - 1)
        def _():
            n, rstd, y = _rms_fwd(h1_ref[...] + acc[...], gf_ref[...])
            err = y - tg_ref[...]
            loss_ref[...] += (0.5 / D_MODEL) * jnp.sum(err * err)
            dh, dgf = _rms_bwd(err * (1.0 / D_MODEL), n, rstd, gf_ref[...])
            dgf_ref[...] += dgf
            dh_ref[...] = dh
            dhb_ref[...] = dh.astype(dhb_ref.dtype)

    per = tm // hb
    row = lambda i, j: (i, 0)
    const = lambda i, j: (0, 0)
    tile = pl.BlockSpec((tm, tw), lambda i, j: (i, j))
    return _call(body, name="ffn_fwd", grid=(t // tm, ng),
                 in_specs=[pl.BlockSpec((tm, D_MODEL), row),
                           pl.BlockSpec((hb, D_MODEL), lambda i, j: (jnp.maximum(i * per - 1, 0), 0)),
                           _resident(w_blocks.shape),
                           pl.BlockSpec((FFN_CONV, tw), lambda i, j: (0, j)),
                           pl.BlockSpec((FFN_CONV, tw), lambda i, j: (0, j + ng)),
                           pl.BlockSpec((1, tw), lambda i, j: (0, j)), pl.BlockSpec((1, tw), lambda i, j: (0, j + ng)),
                           _resident((D_FF, D_MODEL)),
                           pl.BlockSpec((tm, D_MODEL), row), pl.BlockSpec((1, D_MODEL), const),
                           pl.BlockSpec((tm, D_MODEL), row)],
                 out_specs=[tile] * 5 + [pl.BlockSpec((tm, D_MODEL), row),
                            pl.BlockSpec((tm, D_MODEL), row), pl.BlockSpec((SUBLANES, D_MODEL), const),
                            pl.BlockSpec((SUBLANES, LANES), const)],
                 out_shape=[jax.ShapeDtypeStruct((t, D_FF), MXU_DTYPE)] * 5 + [
                            jax.ShapeDtypeStruct((t, D_MODEL), F32),
                            jax.ShapeDtypeStruct((t, D_MODEL), MXU_DTYPE), jax.ShapeDtypeStruct((SUBLANES, D_MODEL), F32),
                            jax.ShapeDtypeStruct((SUBLANES, LANES), F32)],
                 scratch_shapes=[pltpu.VMEM((tm, D_MODEL), F32)],
                 operands=[u2, u2, w_blocks, conv_w, conv_w, conv_b, conv_b, w_down, h1, gf, target])


FFN_ACC_ROWS = SUBLANES * (FFN_CONV + 1)


def _ffn_bwd(dh2, dh2_b, w_down, up_a, up_v, conv_a, conv_v, conv_w, w_up_blocks, h1, g2, comm):
    t = up_a.shape[0]
    tm = min(256, t)
    tn, nj, group = FFN_TN, FFN_NJ, FFN_GROUP
    ng, tw = nj // group, group * tn
    ni = t // tm
    assert w_up_blocks.shape == (2 * nj, D_MODEL, tn)

    def conv_bwd(dy, x, cw, acc_ref, carry_ref, dup_ref, col):
        nxt = carry_ref[...]
        carry_ref[...] = dy[:SUBLANES, :]
        ahead = [_shift_up(dy, nxt, FFN_CONV - 1 - k) for k in range(FFN_CONV)]
        dx = cw[FFN_CONV - 1:FFN_CONV, :] * dy
        for k in range(FFN_CONV - 1):
            dx = dx + cw[k:k + 1, :] * ahead[k]
        dx = dx.astype(dup_ref.dtype)
        dup_ref[:, col] = dx
        for k in range(FFN_CONV):
            acc_ref[k * SUBLANES:(k + 1) * SUBLANES, :] += _colsum8(ahead[k] * x)
        acc_ref[FFN_CONV * SUBLANES:, :] += _colsum8(dy)
        return dx

    def body(dh_ref, dhb_ref, wd_ref, ua_ref, uv_ref, ca_ref, cv_ref, cwa_ref, cwv_ref, wu_ref, h1_ref, g2_ref,
             dua_ref, duv_ref, acca_ref, accv_ref, dh1_ref, dh1b_ref, dg2_ref, carry_a, carry_v, du):
        i, jg = pl.program_id(0), pl.program_id(1)

        @pl.when((i == 0) & (jg == 0))
        def _():
            for ref in (acca_ref, accv_ref, carry_a, carry_v, dg2_ref):
                ref[...] = jnp.zeros_like(ref)

        dhb = dhb_ref[...]
        part = None
        for jj in range(group):
            col = slice(jj * tn, (jj + 1) * tn)
            j = jg * group + jj
            v = cv_ref[:, col].astype(F32)
            g, dg = _gelu_parts(ca_ref[:, col].astype(F32))
            dact = _dot_nt(dhb, wd_ref[pl.ds(pl.multiple_of(j * tn, tn), tn), :])
            da = conv_bwd(dact * v * dg, ua_ref[:, col].astype(F32), cwa_ref[:, col], acca_ref.at[j], carry_a.at[j],
                          dua_ref, col)
            dv = conv_bwd(dact * g, uv_ref[:, col].astype(F32), cwv_ref[:, col], accv_ref.at[j], carry_v.at[j],
                          duv_ref, col)
            term = _dot_nt(da, wu_ref[j]) + _dot_nt(dv, wu_ref[nj + j])
            part = term if part is None else part + term

        @pl.when(jg == 0)
        def _():
            du[...] = part

        @pl.when(jg > 0)
        def _():
            du[...] += part

        @pl.when(jg == ng - 1)
        def _():
            n, rstd, _ = _rms_fwd(h1_ref[...], g2_ref[...])
            dh1, dg2 = _rms_bwd(du[...], n, rstd, g2_ref[...])
            dh1 = dh1 + dh_ref[...]
            dg2_ref[...] += dg2
            dh1_ref[...] = dh1
            dh1b_ref[...] = dh1.astype(dh1b_ref.dtype)

    row = lambda i, j: (ni - 1 - i, 0)
    const = lambda i, j: (0, 0)
    tile = pl.BlockSpec((tm, tw), lambda i, j: (ni - 1 - i, j))
    acc = pl.BlockSpec((nj, FFN_ACC_ROWS, tn), lambda i, j: (0, 0, 0))
    return _call(body, name="ffn_bwd", grid=(ni, ng), comm=comm,
                 in_specs=[pl.BlockSpec((tm, D_MODEL), row), pl.BlockSpec((tm, D_MODEL), row),
                           _resident((D_FF, D_MODEL)), tile, tile, tile, tile,
                           pl.BlockSpec((FFN_CONV, tw), lambda i, j: (0, j)),
                           pl.BlockSpec((FFN_CONV, tw), lambda i, j: (0, j + ng)),
                           _resident(w_up_blocks.shape), pl.BlockSpec((tm, D_MODEL), row),
                           pl.BlockSpec((1, D_MODEL), const)],
                 out_specs=[tile, tile, acc, acc, pl.BlockSpec((tm, D_MODEL), row), pl.BlockSpec((tm, D_MODEL), row),
                            pl.BlockSpec((SUBLANES, D_MODEL), const)],
                 out_shape=[jax.ShapeDtypeStruct((t, D_FF), MXU_DTYPE), jax.ShapeDtypeStruct((t, D_FF), MXU_DTYPE),
                            jax.ShapeDtypeStruct((nj, FFN_ACC_ROWS, tn), F32),
                            jax.ShapeDtypeStruct((nj, FFN_ACC_ROWS, tn), F32),
                            jax.ShapeDtypeStruct((t, D_MODEL), F32), jax.ShapeDtypeStruct((t, D_MODEL), MXU_DTYPE),
                            jax.ShapeDtypeStruct((SUBLANES, D_MODEL), F32)],
                 scratch_shapes=[pltpu.VMEM((nj, SUBLANES, tn), F32), pltpu.VMEM((nj, SUBLANES, tn), F32),
                                 pltpu.VMEM((tm, D_MODEL), F32)],
                 operands=[dh2, dh2_b, w_down, up_a, up_v, conv_a, conv_v, conv_w, conv_w, w_up_blocks, h1, g2])


def _ret_bwd(proj, cos2, sin_signed, gain, o, states, dmix_at):
    t = proj.shape[0]
    c, d, nh = RET_CHUNK, RET_HEAD_DIM, RET_HEADS
    n_chunks = t // c
    per = RET_CHUNKS_PER_STEP if n_chunks % RET_CHUNKS_PER_STEP == 0 else 1
    rows = per * c
    n_steps = n_chunks // per
    decay, xi, zeta, g_chunk = _ret_consts()
    base = 2 * D_LRU

    def units(qk_ref, vg_ref, cos_ref, sin_ref, dec_ref, xi_ref, zeta_ref, gain_ref, o_ref, st_ref,
              dp_ref, dgain_ref, gstate, ctx):
        cur = [None] * nh
        dmix = ctx["scratch"][dmix_at[0]][dmix_at[1]]

        def start():
            @pl.when(pl.program_id(0) == 0)
            def _():
                gstate[...] = jnp.zeros_like(gstate)
                dgain_ref[...] = jnp.zeros_like(dgain_ref)
            for h in range(nh):
                cur[h] = gstate[h]

        def gate_and_norm(s, h, keep):
            rs = slice(s * c, (s + 1) * c)
            lo = h * d
            g = vg_ref[rs, D_RET + lo:D_RET + lo + d].astype(F32)
            gain_h = gain_ref[:, lo:lo + d]
            dy = dmix[rs, D_LRU + lo:D_LRU + lo + d]
            sg = _sigmoid(g)
            o_h = o_ref[rs, lo:lo + d]
            oc = o_h - jnp.mean(o_h, axis=-1, keepdims=True)
            rstd = lax.rsqrt(jnp.mean(oc * oc, axis=-1, keepdims=True) + NORM_EPS)
            on = oc * rstd
            at = base + 3 * D_RET + lo
            dp_ref[rs, at:at + d] = (dy * on * gain_h * (sg * (1.0 + g * (1.0 - sg)))).astype(dp_ref.dtype)
            don_g = dy * (g * sg)
            dgain_ref[:, lo:lo + d] += _colsum8(don_g * on)
            don = don_g * gain_h
            keep["do"] = rstd * (don - jnp.mean(don, axis=-1, keepdims=True)
                                 - on * jnp.mean(don * on, axis=-1, keepdims=True))

        def retain(s, h, keep):
            rs = slice(s * c, (s + 1) * c)
            cos2, sin_s = cos_ref[rs, :], sin_ref[rs, :]
            lo = h * d
            q = _rope(qk_ref[rs, lo:lo + d].astype(F32), cos2, sin_s)
            k = _rope(qk_ref[rs, D_RET + lo:D_RET + lo + d].astype(F32), cos2, sin_s) * RET_SCALE
            v = vg_ref[rs, lo:lo + d]
            xi_h, zeta_h, dec = xi_ref[:, lo:lo + d], zeta_ref[:, lo:lo + d], dec_ref[h]
            do = keep["do"]
            s_prev = st_ref[s, h]
            g_next = cur[h]
            p = _dot_nt(q, k) * dec
            dpm = _dot_nt(do, v) * dec
            keep["dq"] = _dot(dpm, k) + _dot_nt(do, s_prev) * xi_h
            keep["dk"] = _dot_tn(dpm, q) + _dot_nt(v, g_next) * zeta_h
            dv = _dot_tn(p, do) + _dot(k * zeta_h, g_next)
            cur[h] = g_next * g_chunk[h] + _dot_tn(q * xi_h, do)
            at = base + 2 * D_RET + lo
            dp_ref[rs, at:at + d] = dv.astype(dp_ref.dtype)

        def unrope(s, h, keep):
            rs = slice(s * c, (s + 1) * c)
            cos2, sin_s = cos_ref[rs, :], sin_ref[rs, :]
            lo = h * d
            dp_ref[rs, base + lo:base + lo + d] = _rope_bwd(keep["dq"], cos2, sin_s).astype(dp_ref.dtype)
            at = base + D_RET + lo
            dp_ref[rs, at:at + d] = _rope_bwd(keep["dk"] * RET_SCALE, cos2, sin_s).astype(dp_ref.dtype)

        def end():
            for h in range(nh):
                gstate[h] = cur[h]

        pieces = [start]
        for s in reversed(range(per)):
            for h in range(nh):
                keep = {}
                pieces += [lambda s=s, h=h, keep=keep, f=f: f(s, h, keep) for f in (gate_and_norm, retain, unrope)]
        return pieces + [end]

    rev = lambda col: (lambda i: (n_steps - 1 - i, col))
    full2 = lambda i: (0, 0)
    return dict(units=units, grid=(n_steps,),
                in_specs=[pl.BlockSpec((rows, 2 * D_RET), rev(1)), pl.BlockSpec((rows, 2 * D_RET), rev(2)),
                          pl.BlockSpec((rows, d), rev(0)), pl.BlockSpec((rows, d), rev(0)),
                          pl.BlockSpec((nh, c, c), lambda i: (0, 0, 0)), pl.BlockSpec((c, D_RET), full2),
                          pl.BlockSpec((c, D_RET), full2), pl.BlockSpec((1, D_RET), full2),
                          pl.BlockSpec((rows, D_RET), rev(0)),
                          pl.BlockSpec((per, nh, d, d), lambda i: (n_steps - 1 - i, 0, 0, 0))],
                out_specs=[pl.BlockSpec((rows, D_IN), rev(0)), pl.BlockSpec((SUBLANES, D_RET), full2)],
                out_shape=[jax.ShapeDtypeStruct((t, D_IN), MXU_DTYPE), jax.ShapeDtypeStruct((SUBLANES, D_RET), F32)],
                scratch_shapes=[pltpu.VMEM((nh, d, d), F32)],
                operands=[proj, proj, cos2, sin_signed, decay, xi, zeta, gain, o, states])


LRU_ACC = {"conv_w": 0, "conv_b": LRU_CONV, "gate_a_b": LRU_CONV + 1, "gate_x_b": LRU_CONV + 2,
           "lambda": LRU_CONV + 3, "norm_gain": LRU_CONV + 4}
LRU_ACC_ROWS = SUBLANES * (LRU_CONV + 5)


def _lru_bwd(proj, xc_all, h_all, conv_w, wa, ba, wx, bx, lam, gain, dproj_part, dmix_at):
    t = proj.shape[0]
    tm = min(256, t)
    c = D_LRU
    ni = t // tm

    def body(x_ref, xh_ref, g_ref, xc_ref, h_ref, hh_ref, cw_ref, wa_ref, ba_ref, wx_ref, bx_ref, lam_ref,
             gain_ref, acc_ref, dwa_ref, dwx_ref, a_scr, b_scr, mu_scr, carry_mu, carry_dxc, ctx):
        dp_ref = ctx["outs"][dproj_part][0]
        dmix = ctx["scratch"][dmix_at[0]][dmix_at[1]]
        fill = ctx["fill"]
        i = pl.program_id(0)
        r = ni - 1 - i

        @pl.when(i == 0)
        def _():
            acc_ref[...] = jnp.zeros_like(acc_ref)
            dwa_ref[...] = jnp.zeros_like(dwa_ref)
            dwx_ref[...] = jnp.zeros_like(dwx_ref)
            carry_mu[...] = jnp.zeros_like(carry_mu)
            carry_dxc[...] = jnp.zeros_like(carry_dxc)

        def add(name, val, k=0):
            lo = (LRU_ACC[name] + k) * SUBLANES
            acc_ref[lo:lo + SUBLANES, :] += _colsum8(val)

        fill()
        xc, h = xc_ref[...], h_ref[...]
        lam_v = lam_ref[...]
        sp = _softplus(-lam_v)
        rg, ig, a, m = _lru_gates(xc, wa_ref[...], ba_ref[...], wx_ref[...], bx_ref[...], sp, fill)
        gl, dgl = _gelu_parts(g_ref[...].astype(F32))
        fill()
        zn, rstd, _ = _rms_fwd(h * gl, gain_ref[...])
        dy = dmix[:, :c]
        dz, dgain = _rms_bwd(dy, zn, rstd, gain_ref[...])
        lo = LRU_ACC["norm_gain"] * SUBLANES
        acc_ref[lo:lo + SUBLANES, :] += dgain
        dp_ref[:, c:2 * c] = (dz * h * dgl).astype(dp_ref.dtype)
        dh = dz * gl
        fill()
        ga, gb = _group_scan(a, a * dh, reverse=True, fill=fill)
        a_scr[...] = ga
        b_scr[...] = gb
        mu_next_tile = carry_mu[...]
        carry_mu[...] = _carry_scan(a_scr, b_scr, mu_scr, mu_next_tile, reverse=True)
        fill()
        lam_t = dh + _shift_up(mu_scr[...], mu_next_tile, 1)
        h_prev = _shift_down(jnp.where(r == 0, 0.0, hh_ref[...]), h, 1)
        da = lam_t * h_prev
        dig = lam_t * m * xc
        dxc = lam_t * m * ig
        dlog_a = da * a - (lam_t * ig * xc) * (a * a) / m
        fill()
        dpr = dlog_a * ((-LRU_C) * sp) * rg * (1.0 - rg)
        add("lambda", dlog_a * ((-LRU_C) * rg) * (-_sigmoid(-lam_v)))
        dpi = dig * ig * (1.0 - ig)
        add("gate_a_b", dpr)
        add("gate_x_b", dpi)
        fill()
        dwa_ref[...] += _dot_tn(xc, dpr)
        dwx_ref[...] += _dot_tn(xc, dpi)
        dxc = dxc + _dot_nt(dpr, wa_ref[...]) + _dot_nt(dpi, wx_ref[...])
        fill()
        add("conv_b", dxc)
        x = x_ref[...].astype(F32)
        prev = jnp.where(r == 0, 0.0, xh_ref[...].astype(F32)[-SUBLANES:, :])
        cw = cw_ref[...]
        nxt = carry_dxc[...]
        carry_dxc[...] = dxc[:SUBLANES, :]
        dx = cw[LRU_CONV - 1:LRU_CONV, :] * dxc
        for k in range(LRU_CONV - 1):
            dx = dx + cw[k:k + 1, :] * _shift_up(dxc, nxt, LRU_CONV - 1 - k)
        fill()
        for k in range(LRU_CONV):
            add("conv_w", dxc * _shift_down(prev, x, LRU_CONV - 1 - k), k)
        dp_ref[:, :c] = dx.astype(dp_ref.dtype)

    hb = _halo_rows(proj.dtype)
    rev = lambda col: (lambda i: (ni - 1 - i, col))
    halo = lambda rows: (lambda i: (jnp.maximum((ni - 1 - i) * (tm // rows) - 1, 0), 0))
    full = lambda i: (0, 0)
    vec = pl.BlockSpec((1, c), full)
    mat = pl.BlockSpec((c, c), full)
    return dict(body=body, grid=(ni,), fill_points=12,
                in_specs=[pl.BlockSpec((tm, c), rev(0)), pl.BlockSpec((hb, c), halo(hb)), pl.BlockSpec((tm, c), rev(1)),
                          pl.BlockSpec((tm, c), rev(0)), pl.BlockSpec((tm, c), rev(0)),
                          pl.BlockSpec((SUBLANES, c), halo(SUBLANES)),
                          pl.BlockSpec((LRU_CONV, c), full), mat, vec, mat, vec, vec, vec],
                out_specs=[pl.BlockSpec((LRU_ACC_ROWS, c), full), mat, mat],
                out_shape=[jax.ShapeDtypeStruct((LRU_ACC_ROWS, c), F32), jax.ShapeDtypeStruct((c, c), F32),
                           jax.ShapeDtypeStruct((c, c), F32)],
                scratch_shapes=[pltpu.VMEM((tm, c), F32), pltpu.VMEM((tm, c), F32), pltpu.VMEM((tm, c), F32),
                                pltpu.VMEM((SUBLANES, c), F32), pltpu.VMEM((SUBLANES, c), F32)],
                operands=[proj, proj, proj, xc_all, h_all, h_all, conv_w, wa, ba, wx, bx, lam, gain])


def _mix_proj_bwd(dh1, dh1_b, w_out, w_in, x, g1, dproj_part):
    t = x.shape[0]
    tm = min(256, t)
    ni = t // tm
    cb = INPROJ_TN
    nb = D_IN // cb
    first_free = -(-2 * D_LRU // cb)
    du = [None]

    def term(dp_ref, w_ref, d):
        part = _dot_nt(dp_ref[:, d * cb:(d + 1) * cb], w_ref[:, d * cb:(d + 1) * cb])
        du[0] = part if du[0] is None else du[0] + part

    def head(dh_ref, dhb_ref, wo_ref, wi_ref, x_ref, g_ref, gx_ref, dg_ref, dmix, ctx):
        @pl.when(pl.program_id(0) == 0)
        def _():
            dg_ref[...] = jnp.zeros_like(dg_ref)
        dmix[...] = _dot_nt(dhb_ref[...], wo_ref[...])
        du[0] = None

    def units(dh_ref, dhb_ref, wo_ref, wi_ref, x_ref, g_ref, gx_ref, dg_ref, dmix, ctx):
        dp_ref = ctx["outs"][dproj_part][0]
        return [lambda d=d: term(dp_ref, wi_ref, d) for d in range(first_free, nb)]

    def tail(dh_ref, dhb_ref, wo_ref, wi_ref, x_ref, g_ref, gx_ref, dg_ref, dmix, ctx):
        dp_ref = ctx["outs"][dproj_part][0]
        for d in range(first_free):
            term(dp_ref, wi_ref, d)
        n, rstd, _ = _rms_fwd(x_ref[...], g_ref[...])
        dx, dg = _rms_bwd(du[0], n, rstd, g_ref[...])
        dg_ref[...] += dg
        gx_ref[...] = dx + dh_ref[...]

    row = lambda i: (ni - 1 - i, 0)
    const = lambda i: (0, 0)
    tile = pl.BlockSpec((tm, D_MODEL), row)
    return dict(head=head, units=units, tail=tail, grid=(ni,),
                in_specs=[tile, tile, _resident(w_out.shape), _resident(w_in.shape), tile,
                          pl.BlockSpec((1, D_MODEL), const)],
                out_specs=[tile, pl.BlockSpec((SUBLANES, D_MODEL), const)],
                out_shape=[jax.ShapeDtypeStruct((t, D_MODEL), F32), jax.ShapeDtypeStruct((SUBLANES, D_MODEL), F32)],
                scratch_shapes=[pltpu.VMEM((tm, D_MODEL), F32)],
                operands=[dh1, dh1_b, w_out, w_in, x, g1])


def _pair_sum(core, a, b, name):
    n, r, c = b.shape
    spec = pl.BlockSpec((None, r, c), lambda q, core: (q, 0, 0))

    def body(core_ref, a_ref, b_ref, o_ref):
        o_ref[...] = (a_ref[...].astype(F32) + b_ref[...].astype(F32)).astype(o_ref.dtype)

    return pl.pallas_call(
        body, name=name,
        grid_spec=pltpu.PrefetchScalarGridSpec(
            num_scalar_prefetch=1, grid=(n,),
            in_specs=[pl.BlockSpec((None, r, c), lambda q, core: (2 * q + core[0], 0, 0)), spec], out_specs=spec),
        out_shape=jax.ShapeDtypeStruct(b.shape, b.dtype),
        compiler_params=pltpu.CompilerParams(dimension_semantics=("arbitrary",), vmem_limit_bytes=VMEM_LIMIT),
    )(core, a, b)


ADAMW_BLOCK_BYTES = 4 * 1024 * 1024


def _sum_adamw(parts, w, m, v, name):
    n_parts, r, c = parts.shape
    tr = r
    while n_parts * tr * c * parts.dtype.itemsize > ADAMW_BLOCK_BYTES and tr % (4 * SUBLANES) == 0:
        tr //= 2

    def body(p_ref, w_ref, m_ref, v_ref, g_ref, d_ref, nm_ref, nv_ref):
        g = p_ref[0].astype(F32)
        for s in range(1, n_parts):
            g = g + p_ref[s].astype(F32)
        nm = ADAM_B1 * m_ref[...] + (1.0 - ADAM_B1) * g
        nv = ADAM_B2 * v_ref[...] + (1.0 - ADAM_B2) * (g * g)
        m_hat = nm / (1.0 - ADAM_B1 ** ADAM_STEP)
        v_hat = nv / (1.0 - ADAM_B2 ** ADAM_STEP)
        g_ref[...] = g
        d_ref[...] = -ADAM_LR * (m_hat / (jnp.sqrt(v_hat) + ADAM_EPS) + ADAM_WD * w_ref[...])
        nm_ref[...] = nm
        nv_ref[...] = nv

    row = pl.BlockSpec((tr, c), lambda i: (i, 0))
    return _call(body, name=name, grid=(r // tr,),
                 in_specs=[pl.BlockSpec((n_parts, tr, c), lambda i: (0, i, 0)), row, row, row],
                 out_specs=[row, row, row, row], out_shape=[jax.ShapeDtypeStruct((r, c), F32)] * 4,
                 operands=[parts, w, m, v])


MATRICES = ("w_in", "w_out", "ffn_up_w", "ffn_down_w")
CONVS = ("lru_conv_w", "ffn_conv_w")
REPLICATED = ("norm1_gain", "lru_conv_b", "lru_gate_a_w", "lru_gate_a_b", "lru_gate_x_w", "lru_gate_x_b", "lru_lambda",
              "lru_norm_gain", "ret_norm_gain", "norm2_gain", "ffn_conv_b", "final_norm_gain")
WEIGHTS = ("norm1_gain", "w_in", "lru_conv_w", "lru_conv_b", "lru_gate_a_w", "lru_gate_a_b", "lru_gate_x_w",
           "lru_gate_x_b", "lru_lambda", "lru_norm_gain", "ret_norm_gain", "w_out", "norm2_gain", "ffn_up_w",
           "ffn_conv_w", "ffn_conv_b", "ffn_down_w", "final_norm_gain")


def _rows(a, pad_to):
    a = a.reshape(-1, LANES)
    pad = (-a.shape[0]) % pad_to
    return jnp.pad(a, ((0, pad), (0, 0))) if pad else a


def _pack(arrays, pad_to):
    rows, layout, at = [], [], 0
    for a in arrays:
        r = _rows(a, pad_to)
        layout.append((at, a.size // LANES, a.shape))
        rows.append(r)
        at += r.shape[0]
    return jnp.concatenate(rows, axis=0), layout


def _unpack(packed, layout):
    lead = packed.shape[:-2]
    return [packed[..., at:at + n, :].reshape(lead + shape) for at, n, shape in layout]


def _conv_rows(lru, ffn, dtype, pad_to):
    lead = lru.shape[:-2]
    flat = jnp.concatenate([lru.reshape(lead + (-1,)), ffn.reshape(lead + (-1,))], axis=-1).astype(dtype)
    rows = flat.shape[-1] // LANES
    pad = (-rows) % pad_to
    return jnp.pad(flat.reshape(lead + (rows, LANES)), [(0, 0)] * len(lead) + [(0, pad), (0, 0)])


def _column_blocks(full):
    r, c = full.shape
    return full.reshape(r, N_DEV, c // N_DEV).transpose(1, 0, 2)


def _block_diag(w):
    nh, d, _ = w.shape
    eye = jnp.eye(nh, dtype=w.dtype)
    return (w[:, :, None, :] * eye[:, None, :, None]).reshape(nh * d, nh * d)


def _diag_blocks(dense, nh):
    d = dense.shape[0] // nh
    blocks = dense.reshape(nh, d, nh, d)
    return jnp.stack([blocks[h, :, h, :] for h in range(nh)], axis=0)


def kernel(x, norm1_gain, w_in, lru_conv_w, lru_conv_b, lru_gate_a_w, lru_gate_a_b, lru_gate_x_w, lru_gate_x_b, lru_lambda, lru_norm_gain, ret_norm_gain, w_out, norm2_gain, ffn_up_w, ffn_conv_w, ffn_conv_b, ffn_down_w, final_norm_gain, loss_target, m_norm1_gain, m_w_in, m_lru_conv_w, m_lru_conv_b, m_lru_gate_a_w, m_lru_gate_a_b, m_lru_gate_x_w, m_lru_gate_x_b, m_lru_lambda, m_lru_norm_gain, m_ret_norm_gain, m_w_out, m_norm2_gain, m_ffn_up_w, m_ffn_conv_w, m_ffn_conv_b, m_ffn_down_w, m_final_norm_gain, v_norm1_gain, v_w_in, v_lru_conv_w, v_lru_conv_b, v_lru_gate_a_w, v_lru_gate_a_b, v_lru_gate_x_w, v_lru_gate_x_b, v_lru_lambda, v_lru_norm_gain, v_ret_norm_gain, v_w_out, v_norm2_gain, v_ffn_up_w, v_ffn_conv_w, v_ffn_conv_b, v_ffn_down_w, v_final_norm_gain):
    args = dict(locals())
    given = {n: args[n] for n in WEIGHTS}
    out_shape = {n: given[n].shape for n in WEIGHTS}

    def plain(a):
        return a.reshape(1, -1) if a.ndim <= 2 else a[0]

    w = {n: plain(given[n]) for n in WEIGHTS}
    mom_m = {n: plain(args["m_" + n]) for n in WEIGHTS}
    mom_v = {n: plain(args["v_" + n]) for n in WEIGHTS}
    x2, target = x[0], loss_target[0]
    t = x2.shape[0]
    core = lax.axis_index("c").astype(jnp.int32).reshape(1)
    res = {}

    conv_pad = _conv_rows(w["lru_conv_w"], w["ffn_conv_w"], F32, SUBLANES)
    first = _gather_first([w["w_in"].astype(MXU_DTYPE), conv_pad])
    w_in_blocks, conv_all = _run_comms([first, _gather_second(first.out_shape)], "w_in_all_gather")
    w_in_full = w_in_blocks.transpose(1, 0, 2).reshape(D_MODEL, D_IN)
    n_lru = w["lru_conv_w"].size
    conv_flat = conv_all.reshape(N_DEV, -1)
    lru_cw = conv_flat[:, :n_lru].reshape((N_DEV,) + w["lru_conv_w"].shape).transpose(1, 0, 2).reshape(LRU_CONV, D_LRU)
    ffn_cw = conv_flat[:, n_lru:n_lru + w["ffn_conv_w"].size].reshape((N_DEV,) + w["ffn_conv_w"].shape)
    ffn_cw = ffn_cw.transpose(1, 0, 2).reshape(FFN_CONV, 2 * D_FF)

    cos2, sin_signed = _rope_tables(t)
    wa = _block_diag(w["lru_gate_a_w"]).astype(MXU_DTYPE)
    wx = _block_diag(w["lru_gate_x_w"]).astype(MXU_DTYPE)
    gf = w["final_norm_gain"]

    early = _gather_first([w["ffn_up_w"].astype(MXU_DTYPE), w["w_out"].astype(MXU_DTYPE)])
    (u1, proj), (up_part, w_out_part) = _inproj_fwd(x2, w["norm1_gain"], w_in_full, early)
    ((xc, h_lru, y_lru), (o_ret, y_ret, states)), (up_blocks, w_out_blocks, down_part) = _fused(
        [_lru_fwd(proj, lru_cw, w["lru_conv_b"], wa, w["lru_gate_a_b"], wx, w["lru_gate_x_b"], w["lru_lambda"],
                  w["lru_norm_gain"]),
         _ret_fwd(proj, cos2, sin_signed, w["ret_norm_gain"])],
        "mix_fwd", _both(_gather_second([up_part, w_out_part]), _gather_first([w["ffn_down_w"].astype(MXU_DTYPE)])))
    w_out_full = w_out_blocks.reshape(D_MODEL, D_MODEL)

    (h1, u2), (down_blocks,) = _outproj_fwd(x2, y_lru, y_ret, w_out_full, w["norm2_gain"], _gather_second([down_part]))
    w_down_full = down_blocks.reshape(D_FF, D_MODEL)
    up_a, up_v, conv_a, conv_v, act, dh2, dh2_b, dgf, loss = _ffn_fwd(u2, up_blocks, ffn_cw, w["ffn_conv_b"], w_down_full,
                                                                      h1, gf, target)
    loss = lax.psum(loss[0, 0], ("x", "y", "c"))

    def to_owner_chips(blocks, names, tag):
        theirs = _run_comms([_pair_exchange(blocks)], "grads_pair_exchange_" + tag)
        return [_pair_sum(core, a, b, "grads_pair_sum_" + n) for n, a, b in zip(names, blocks, theirs)]

    def adamw(name, parts):
        res[name] = _sum_adamw(parts, w[name], mom_m[name], mom_v[name], "adamw_" + name)

    g = {"final_norm_gain": dgf[0]}
    g_down = _mm_tn(act, dh2_b, "ffn_down_wgrad").reshape(N_DEV, D_FF // N_DEV, D_MODEL)
    down_sums = to_owner_chips([g_down], ["ffn_down_w"], "down")
    (dup_a, dup_v, acc_a, acc_v, dh1, dh1_b, dg2), (down_parts,) = _ffn_bwd(
        dh2, dh2_b, w_down_full, up_a, up_v, conv_a, conv_v, ffn_cw, up_blocks, h1, w["norm2_gain"],
        _chip_exchange(down_sums))
    adamw("ffn_down_w", down_parts)
    per_col = lambda a: a[:, ::SUBLANES].transpose(1, 0, 2).reshape(FFN_CONV + 1, D_FF)
    acc = jnp.concatenate([per_col(acc_a), per_col(acc_v)], axis=1)
    g_ffn_cw, g["ffn_conv_b"] = acc[:FFN_CONV], acc[FFN_CONV:]
    g["norm2_gain"] = dg2[:1]
    g_up = jnp.concatenate([_mm_tn(u2, dup_a, "ffn_up_wgrad_a", blocks=N_DEV // 2),
                            _mm_tn(u2, dup_v, "ffn_up_wgrad_v", blocks=N_DEV // 2)], axis=0)
    g_out = jnp.concatenate([_mm_tn(y_lru, dh1_b, "w_out_wgrad_lru"), _mm_tn(y_ret, dh1_b, "w_out_wgrad_ret")], axis=0)
    mid_sums = to_owner_chips([g_up, g_out.reshape(N_DEV, D_MODEL // N_DEV, D_MODEL)], ["ffn_up_w", "w_out"], "mid")
    (dproj, dgain_ret), (grad_x, dg1), (lru_acc, dwa, dwx) = _fused(
        [_ret_bwd(proj, cos2, sin_signed, w["ret_norm_gain"], o_ret, states, dmix_at=(1, 0)),
         _mix_proj_bwd(dh1, dh1_b, w_out_full, w_in_full, x2, w["norm1_gain"], dproj_part=0),
         _lru_bwd(proj, xc, h_lru, lru_cw, wa, w["lru_gate_a_b"], wx, w["lru_gate_x_b"], w["lru_lambda"],
                  w["lru_norm_gain"], dproj_part=0, dmix_at=(1, 0))],
        "mix_bwd")
    g["norm1_gain"] = dg1[:1]
    g["ret_norm_gain"] = dgain_ret[:1]
    lru_acc = lru_acc[::SUBLANES]
    g_lru_cw = lru_acc[:LRU_CONV]
    for name in ("conv_b", "gate_a_b", "gate_x_b", "lambda", "norm_gain"):
        g["lru_" + name] = lru_acc[LRU_ACC[name]:LRU_ACC[name] + 1]
    g["lru_gate_a_w"] = _diag_blocks(dwa, LRU_HEADS)
    g["lru_gate_x_w"] = _diag_blocks(dwx, LRU_HEADS)
    g_in, (up_parts, out_parts) = _mm_tn(u1, dproj, "w_in_wgrad", blocks=N_DEV, comm=_chip_exchange(mid_sums))
    adamw("ffn_up_w", up_parts)
    adamw("w_out", out_parts)
    g_conv = _conv_rows(_column_blocks(g_lru_cw), _column_blocks(g_ffn_cw), GRAD_DTYPE, 2 * SUBLANES)
    in_sums = to_owner_chips([g_in, g_conv], ["w_in", "conv"], "in")

    rep_packed, rep_layout = _pack([g[n] for n in REPLICATED], SUBLANES)
    rep_first = _gather_first([rep_packed])
    in_parts, conv_parts, rep_parts = _run_comms(
        [_both(_chip_exchange(in_sums), rep_first), _offset(_gather_second(rep_first.out_shape), 2)],
        "last_grads_exchange")
    adamw("w_in", in_parts)
    pad16 = lambda d: _conv_rows(d["lru_conv_w"], d["ffn_conv_w"], F32, 2 * SUBLANES)
    conv_res = _sum_adamw(conv_parts, pad16(w), pad16(mom_m), pad16(mom_v), "adamw_conv")
    for n, lo, hi in (("lru_conv_w", 0, n_lru), ("ffn_conv_w", n_lru, n_lru + w["ffn_conv_w"].size)):
        res[n] = [r.reshape(-1)[lo:hi].reshape(w[n].shape) for r in conv_res]
    rep_res = _sum_adamw(rep_parts, *[_pack([d[n] for n in REPLICATED], SUBLANES)[0] for d in (w, mom_m, mom_v)],
                         "adamw_replicated")
    for k in range(4):
        for n, a in zip(REPLICATED, _unpack(rep_res[k], rep_layout)):
            res.setdefault(n, [None] * 4)[k] = a

    outs = [loss, grad_x[None]]
    for k in range(4):
        outs += [res[n][k].reshape(out_shape[n]) for n in WEIGHTS]
    return tuple(outs)
```

```python
import math

import numpy as np
import jax
import jax.numpy as jnp
from jax import lax
from jax.experimental import pallas as pl
from jax.experimental.pallas import tpu as pltpu

F32 = jnp.float32
BF16 = jnp.bfloat16
MXU_DTYPE = jnp.bfloat16
GRAD_DTYPE = jnp.bfloat16

N_DEV = 8
N_CHIPS = 4
D_MODEL = 1024
D_LRU = 512
LRU_HEADS = 8
LRU_CONV = 4
LRU_C = 8.0
D_RET = 512
RET_HEADS = 4
RET_HEAD_DIM = 128
RET_CHUNK = 128
ROPE_BASE = 10000.0
D_IN = 3072
D_FF = 3072
FFN_CONV = 3
NORM_EPS = 1e-6

ADAM_LR = 0.001
ADAM_B1 = 0.9
ADAM_B2 = 0.999
ADAM_EPS = 1e-08
ADAM_WD = 0.01
ADAM_STEP = 10

SUBLANES = 8
LANES = 128
VMEM_LIMIT = 48 * 1024 * 1024

MESH = pl.DeviceIdType.MESH
ANY = pl.BlockSpec(memory_space=pl.ANY)


def _dot(a, b):
    return jnp.dot(a.astype(MXU_DTYPE), b.astype(MXU_DTYPE), preferred_element_type=F32)


def _dot_nt(a, b):
    return lax.dot_general(a.astype(MXU_DTYPE), b.astype(MXU_DTYPE), (((1,), (1,)), ((), ())),
                           preferred_element_type=F32)


def _dot_tn(a, b):
    return lax.dot_general(a.astype(MXU_DTYPE), b.astype(MXU_DTYPE), (((0,), (0,)), ((), ())),
                           preferred_element_type=F32)


def _sigmoid(x):
    return 0.5 + 0.5 * jnp.tanh(0.5 * x)


_GELU_C = math.sqrt(2.0 / math.pi)
_GELU_C3 = _GELU_C * 0.044715


def _gelu_parts(x):
    x2 = x * x
    t = jnp.tanh(x * (_GELU_C + _GELU_C3 * x2))
    cdf = 0.5 + 0.5 * t
    g = x * cdf
    dg = cdf + (0.5 * x) * (1.0 - t * t) * (_GELU_C + (3.0 * _GELU_C3) * x2)
    return g, dg


def _gelu(x):
    t = jnp.tanh(_GELU_C * (x + 0.044715 * (x * x * x)))
    return x * (0.5 * (1.0 + t))


def _softplus(x):
    return jnp.maximum(x, 0.0) + jnp.log1p(jnp.exp(-jnp.abs(x)))


def _bcast_row(x, r, rows=SUBLANES):
    return jnp.broadcast_to(x[r:r + 1, :], (rows, x.shape[1]))


def _colsum8(x):
    return jnp.broadcast_to(jnp.sum(x, axis=0, keepdims=True), (SUBLANES, x.shape[1]))


def _shift_down(prev8, tile, s):
    if s == 0:
        return tile
    ext = jnp.concatenate([prev8, tile], axis=0)
    return pltpu.roll(ext, s, 0)[SUBLANES:, :]


def _shift_up(tile, next8, s):
    if s == 0:
        return tile
    ext = jnp.concatenate([tile, next8], axis=0)
    return pltpu.roll(ext, SUBLANES - s, 0)[SUBLANES:, :]


def _group_scan(a, b, reverse, fill=lambda: None):
    n = a.shape[0]
    row = lax.broadcasted_iota(jnp.int32, a.shape, 0) & (SUBLANES - 1)
    for s in (1, 2, 4):
        if s > 1:
            fill()
        shift = (n - s) if reverse else s
        a_sh = pltpu.roll(a, shift, 0)
        b_sh = pltpu.roll(b, shift, 0)
        m = (row <= SUBLANES - 1 - s) if reverse else (row >= s)
        b = jnp.where(m, a * b_sh + b, b)
        a = jnp.where(m, a * a_sh, a)
    return a, b


def _carry_scan(a_ref, b_ref, out_ref, carry0, reverse):
    n_groups = a_ref.shape[0] // SUBLANES
    carry = carry0
    for i in range(n_groups):
        r0 = ((n_groups - 1 - i) if reverse else i) * SUBLANES
        hg = a_ref[r0:r0 + SUBLANES, :] * carry + b_ref[r0:r0 + SUBLANES, :]
        out_ref[r0:r0 + SUBLANES, :] = hg
        carry = _bcast_row(hg, 0 if reverse else SUBLANES - 1)
    return carry


def _rms_fwd(h, gain):
    rstd = lax.rsqrt(jnp.mean(h * h, axis=-1, keepdims=True) + NORM_EPS)
    n = h * rstd
    return n, rstd, n * gain


def _rms_bwd(dy, n, rstd, gain):
    dn = dy * gain
    dh = rstd * (dn - n * jnp.mean(dn * n, axis=-1, keepdims=True))
    return dh, _colsum8(dy * n)


def _halo_rows(dtype):
    return SUBLANES * (4 // jnp.dtype(dtype).itemsize)


def _halo_map(tile_rows, col, halo_rows=SUBLANES):
    per = tile_rows // halo_rows
    return lambda i: (jnp.maximum(i * per - 1, 0), col)


def _resident(shape):
    return pl.BlockSpec(shape, lambda *_: (0,) * len(shape), pipeline_mode=pl.Buffered(1))


def _place():
    x, y, c = lax.axis_index("x"), lax.axis_index("y"), lax.axis_index("c")
    chips = [(1 - x, y), (x, 1 - y), (1 - x, 1 - y)]
    return x, y, c, chips


def _dev(x, y, c):
    return 4 * x + 2 * y + c


class _Copy:
    def __init__(self, make):
        self.make = make

    def start(self):
        self.make().start()

    def wait(self):
        self.make().wait()

    def wait_send(self):
        self.make().wait_send()

    def wait_recv(self):
        self.make().wait_recv()


def _remote(src, dst, send_sem, recv_sem, to):
    return _Copy(lambda: pltpu.make_async_remote_copy(src_ref=src, dst_ref=dst, send_sem=send_sem, recv_sem=recv_sem,
                                                      device_id=to, device_id_type=MESH))


def _local(src, dst, sem):
    return _Copy(lambda: pltpu.make_async_copy(src, dst, sem))


class _Comm:
    def __init__(self, operands, out_shape, sems, descs, aliases=()):
        self.operands, self.out_shape, self.sems, self.descs, self.aliases = operands, out_shape, sems, descs, aliases

    def start(self, ins, outs, sems):
        local, sends, _ = self.descs(ins, outs, sems)
        for cp in sends + local:
            cp.start()

    def wait(self, ins, outs, sems):
        local, sends, recvs = self.descs(ins, outs, sems)
        for cp in recvs:
            cp.wait_recv()
        for cp in sends:
            cp.wait_send()
        for cp in local:
            cp.wait()


def _gather_first(shards):
    n = len(shards)

    def descs(ins, outs, sems):
        send, recv, loc = sems
        x, y, c, chips = _place()
        me = _dev(x, y, c)
        targets = [(x, y, 1 - c)] + [(*chip, c) for chip in chips]
        local, sends, recvs = [], [], []
        for t in range(n):
            local.append(_local(ins[t], outs[t].at[me], loc.at[t]))
            for k, to in enumerate(targets):
                i = 4 * t + k
                sends.append(_remote(ins[t], outs[t].at[me], send.at[i], recv.at[i], to))
                recvs.append(_remote(ins[t], outs[t].at[_dev(*to)], send.at[i], recv.at[i], to))
        return local, sends, recvs

    return _Comm(list(shards), [jax.ShapeDtypeStruct((N_DEV,) + s.shape, s.dtype) for s in shards],
                 [pltpu.SemaphoreType.DMA((4 * n,)), pltpu.SemaphoreType.DMA((4 * n,)), pltpu.SemaphoreType.DMA((n,))],
                 descs)


def _gather_second(gathered):
    n = len(gathered)

    def descs(ins, outs, sems):
        send, recv = sems
        x, y, c, chips = _place()
        sends, recvs = [], []
        for t in range(n):
            for j, chip in enumerate(chips):
                i = 3 * t + j
                have, get = _dev(*chip, c), _dev(*chip, 1 - c)
                sends.append(_remote(outs[t].at[have], outs[t].at[have], send.at[i], recv.at[i], (x, y, 1 - c)))
                recvs.append(_remote(outs[t].at[have], outs[t].at[get], send.at[i], recv.at[i], (x, y, 1 - c)))
        return [], sends, recvs

    return _Comm(list(gathered), [jax.ShapeDtypeStruct(g.shape, g.dtype) for g in gathered],
                 [pltpu.SemaphoreType.DMA((3 * n,)), pltpu.SemaphoreType.DMA((3 * n,))], descs,
                 aliases=[(t, t) for t in range(n)])


def _pair_exchange(blocks):
    n = len(blocks)

    def descs(ins, outs, sems):
        send, recv = sems
        x, y, c, _ = _place()
        sends, recvs = [], []
        for t in range(n):
            for q in range(N_CHIPS):
                i = N_CHIPS * t + q
                cp = _remote(ins[t].at[2 * q + 1 - c], outs[t].at[q], send.at[i], recv.at[i], (x, y, 1 - c))
                sends.append(cp)
                recvs.append(cp)
        return [], sends, recvs

    return _Comm(list(blocks), [jax.ShapeDtypeStruct((N_CHIPS,) + b.shape[1:], b.dtype) for b in blocks],
                 [pltpu.SemaphoreType.DMA((N_CHIPS * n,)), pltpu.SemaphoreType.DMA((N_CHIPS * n,))], descs)


def _chip_exchange(blocks):
    n = len(blocks)

    def descs(ins, outs, sems):
        send, recv, loc = sems
        x, y, c, chips = _place()
        me = 2 * x + y
        local, sends, recvs = [], [], []
        for t in range(n):
            local.append(_local(ins[t].at[me], outs[t].at[me], loc.at[t]))
            for j, (px, py) in enumerate(chips):
                i = 3 * t + j
                q = 2 * px + py
                sends.append(_remote(ins[t].at[q], outs[t].at[me], send.at[i], recv.at[i], (px, py, c)))
                recvs.append(_remote(ins[t].at[q], outs[t].at[q], send.at[i], recv.at[i], (px, py, c)))
        return local, sends, recvs

    return _Comm(list(blocks), [jax.ShapeDtypeStruct(b.shape, b.dtype) for b in blocks],
                 [pltpu.SemaphoreType.DMA((3 * n,)), pltpu.SemaphoreType.DMA((3 * n,)), pltpu.SemaphoreType.DMA((n,))],
                 descs)


def _both(a, b):
    na, oa, sa = len(a.operands), len(a.out_shape), len(a.sems)

    def descs(ins, outs, sems):
        local_a, sends_a, recvs_a = a.descs(ins[:na], outs[:oa], sems[:sa])
        local_b, sends_b, recvs_b = b.descs(ins[na:], outs[oa:], sems[sa:])
        return local_a + local_b, sends_a + sends_b, recvs_a + recvs_b

    return _Comm(a.operands + b.operands, a.out_shape + b.out_shape, a.sems + b.sems, descs,
                 aliases=list(a.aliases) + [(na + i, oa + o) for i, o in b.aliases])


def _run_comms(comms, name):
    first = comms[0]
    n_in, n_out = len(first.operands), len(first.out_shape)

    def body(*refs):
        ins, outs, sems = refs[:n_in], refs[n_in:n_in + n_out], list(refs[n_in + n_out:])
        for k, comm in enumerate(comms):
            mine = [sems.pop(0) for _ in comm.sems]
            comm.start(ins if k == 0 else outs, outs, mine)
            comm.wait(ins if k == 0 else outs, outs, mine)

    outs = pl.pallas_call(
        body, name=name, out_shape=first.out_shape, in_specs=[ANY] * n_in, out_specs=[ANY] * n_out,
        scratch_shapes=[s for comm in comms for s in comm.sems], input_output_aliases=dict(first.aliases),
    )(*first.operands)
    return list(outs)


def _call(body, *, name, grid, in_specs, out_specs, out_shape, operands, scratch_shapes=(), comm=None, aliases=None):
    sem = ("arbitrary",) * len(grid)
    params = pltpu.CompilerParams(dimension_semantics=sem, vmem_limit_bytes=VMEM_LIMIT)
    aliases = dict(aliases or {})
    if comm is None:
        return pl.pallas_call(body, name=name, grid=grid, in_specs=in_specs, out_specs=out_specs, out_shape=out_shape,
                              scratch_shapes=list(scratch_shapes), input_output_aliases=aliases,
                              compiler_params=params)(*operands)
    n_in, n_out, n_scr = len(in_specs), len(out_specs), len(scratch_shapes)
    c_in, c_out = len(comm.operands), len(comm.out_shape)

    def wrapped(*refs):
        refs = list(refs)
        ins, refs = refs[:n_in], refs[n_in:]
        cins, refs = refs[:c_in], refs[c_in:]
        outs, refs = refs[:n_out], refs[n_out:]
        couts, refs = refs[:c_out], refs[c_out:]
        scr, csems = refs[:n_scr], refs[n_scr:]
        first = last = None
        for axis, size in enumerate(grid):
            at_first, at_last = pl.program_id(axis) == 0, pl.program_id(axis) == size - 1
            first = at_first if first is None else first & at_first
            last = at_last if last is None else last & at_last

        @pl.when(first)
        def _():
            comm.start(cins, couts, csems)

        body(*ins, *outs, *scr)

        @pl.when(last)
        def _():
            comm.wait(cins, couts, csems)

    res = pl.pallas_call(
        wrapped, name=name, grid=grid, in_specs=list(in_specs) + [ANY] * c_in, out_specs=list(out_specs) + [ANY] * c_out,
        out_shape=list(out_shape) + list(comm.out_shape), scratch_shapes=list(scratch_shapes) + list(comm.sems),
        input_output_aliases={**aliases, **{n_in + i: n_out + o for i, o in comm.aliases}}, compiler_params=params,
    )(*operands, *comm.operands)
    return list(res[:n_out]), list(res[n_out:])


def _mm_tn(a, b, name, blocks=1, tk=2048, comm=None):
    t, m = a.shape
    n = b.shape[1]
    tk = min(tk, t)
    nk = t // tk
    cb = n // blocks
    per = max(1, 768 // cb) if blocks > 1 else 1
    tn = per * cb if blocks > 1 else min(1024, n)
    tm = min(1024, m)
    assert blocks == 1 or tm == m

    def body(a_ref, b_ref, o_ref, acc):
        k = pl.program_id(2)

        @pl.when(k == 0)
        def _():
            acc[...] = jnp.zeros_like(acc)
        acc[...] += _dot_tn(a_ref[...], b_ref[...])

        @pl.when(k == nk - 1)
        def _():
            if blocks == 1:
                o_ref[...] = acc[...].astype(o_ref.dtype)
            else:
                for s in range(per):
                    o_ref[s] = acc[:, s * cb:(s + 1) * cb].astype(o_ref.dtype)

    if blocks == 1:
        out_spec = pl.BlockSpec((tm, tn), lambda i, j, k: (i, j))
        out_shape = jax.ShapeDtypeStruct((m, n), GRAD_DTYPE)
    else:
        out_spec = pl.BlockSpec((per, m, cb), lambda i, j, k: (j, 0, 0))
        out_shape = jax.ShapeDtypeStruct((blocks, m, cb), GRAD_DTYPE)
    res = _call(body, name=name, grid=(m // tm, n // tn, nk), comm=comm,
                in_specs=[pl.BlockSpec((tk, tm), lambda i, j, k: (k, i)), pl.BlockSpec((tk, tn), lambda i, j, k: (k, j))],
                out_specs=[out_spec], out_shape=[out_shape], operands=[a, b],
                scratch_shapes=[pltpu.VMEM((tm, tn), F32)])
    return res[0] if comm is None else (res[0][0], res[1])


def _inproj_fwd(x, g1, w_blocks, comm):
    t = x.shape[0]
    tm = min(512, t)
    nb, _, cb = w_blocks.shape

    def body(x_ref, g_ref, w_ref, u_ref, p_ref):
        _, _, u = _rms_fwd(x_ref[...], g_ref[...])
        u = u.astype(MXU_DTYPE)
        u_ref[...] = u
        for d in range(nb):
            p_ref[:, d * cb:(d + 1) * cb] = _dot(u, w_ref[d]).astype(p_ref.dtype)

    return _call(body, name="inproj_fwd", grid=(t // tm,), comm=comm,
                 in_specs=[pl.BlockSpec((tm, D_MODEL), lambda i: (i, 0)), pl.BlockSpec((1, D_MODEL), lambda i: (0, 0)),
                           _resident(w_blocks.shape)],
                 out_specs=[pl.BlockSpec((tm, D_MODEL), lambda i: (i, 0)), pl.BlockSpec((tm, D_IN), lambda i: (i, 0))],
                 out_shape=[jax.ShapeDtypeStruct((t, D_MODEL), MXU_DTYPE), jax.ShapeDtypeStruct((t, D_IN), MXU_DTYPE)],
                 operands=[x, g1, w_blocks])


def _lru_gates(xc, wa, ba, wx, bx, sp, fill=lambda: None):
    r = _sigmoid(_dot(xc, wa) + ba)
    fill()
    ig = _sigmoid(_dot(xc, wx) + bx)
    fill()
    log_a = (-LRU_C) * r * sp
    a = jnp.exp(log_a)
    m = jnp.sqrt(-jnp.tanh(log_a) * (a * a + 1.0))
    return r, ig, a, m


def _fused(parts, name, comm=None):
    grid = parts[0]["grid"]
    assert all(p["grid"] == grid for p in parts)
    counts = [(len(p["in_specs"]), len(p["out_specs"]), len(p.get("scratch_shapes", ()))) for p in parts]

    def body(*refs):
        refs = list(refs)
        groups = []
        for kind in range(3):
            taken = []
            for c in counts:
                taken.append(refs[:c[kind]])
                refs = refs[c[kind]:]
            groups.append(taken)
        ins, outs, scr = groups
        pending = []

        def fill(n=None):
            for _ in range(share if n is None else n):
                if pending:
                    pending.pop(0)()

        ctx = dict(outs=outs, scratch=scr, fill=fill)
        run = lambda key: [p[key](*ins[k], *outs[k], *scr[k], ctx) for k, p in enumerate(parts) if key in p]
        run("head")
        for pieces in run("units"):
            pending.extend(pieces)
        points = sum(p.get("fill_points", 0) for p in parts)
        share = -(-len(pending) // max(points, 1))
        run("body")
        fill(len(pending))
        run("tail")

    cat = lambda key: [x for p in parts for x in p.get(key, ())]
    res = _call(body, name=name, grid=grid, comm=comm, in_specs=cat("in_specs"), out_specs=cat("out_specs"),
                out_shape=cat("out_shape"), scratch_shapes=cat("scratch_shapes"), operands=cat("operands"))
    outs, side = (res if comm is not None else (res, None))
    split, at = [], 0
    for _, n_out, _ in counts:
        split.append(list(outs[at:at + n_out]))
        at += n_out
    return split if comm is None else (split, side)


def _lru_fwd(proj, conv_w, conv_b, wa, ba, wx, bx, lam, gain):
    t = proj.shape[0]
    tm = min(256, t)
    c = D_LRU

    def body(x_ref, xh_ref, g_ref, cw_ref, cb_ref, wa_ref, ba_ref, wx_ref, bx_ref, lam_ref, gain_ref,
             xc_ref, h_ref, y_ref, a_scr, b_scr, carry, ctx):
        fill = ctx["fill"]
        i = pl.program_id(0)

        @pl.when(i == 0)
        def _():
            carry[...] = jnp.zeros_like(carry)

        fill()
        x = x_ref[...].astype(F32)
        prev = jnp.where(i == 0, 0.0, xh_ref[...].astype(F32)[-SUBLANES:, :])
        cw = cw_ref[...]
        xc = cb_ref[...] + cw[LRU_CONV - 1:LRU_CONV, :] * x
        for k in range(LRU_CONV - 1):
            xc = xc + cw[k:k + 1, :] * _shift_down(prev, x, LRU_CONV - 1 - k)
        xc_ref[...] = xc
        fill()
        sp = _softplus(-lam_ref[...])
        _, ig, a, m = _lru_gates(xc, wa_ref[...], ba_ref[...], wx_ref[...], bx_ref[...], sp, fill)
        fill()
        ga, gb = _group_scan(a, m * (ig * xc), reverse=False, fill=fill)
        a_scr[...] = ga
        b_scr[...] = gb
        fill()
        carry[...] = _carry_scan(a_scr, b_scr, h_ref, carry[...], reverse=False)
        fill()
        z = h_ref[...] * _gelu(g_ref[...].astype(F32))
        fill()
        _, _, y = _rms_fwd(z, gain_ref[...])
        y_ref[...] = y.astype(y_ref.dtype)

    row = lambda i: (i, 0)
    full = lambda i: (0, 0)
    vec = pl.BlockSpec((1, c), full)
    hb = _halo_rows(proj.dtype)
    return dict(body=body, grid=(t // tm,), fill_points=10,
                in_specs=[pl.BlockSpec((tm, c), row), pl.BlockSpec((hb, c), _halo_map(tm, 0, hb)),
                          pl.BlockSpec((tm, c), lambda i: (i, 1)),
                          pl.BlockSpec((LRU_CONV, c), full), vec, pl.BlockSpec((c, c), full), vec,
                          pl.BlockSpec((c, c), full), vec, vec, vec],
                out_specs=[pl.BlockSpec((tm, c), row), pl.BlockSpec((tm, c), row), pl.BlockSpec((tm, c), row)],
                out_shape=[jax.ShapeDtypeStruct((t, c), F32), jax.ShapeDtypeStruct((t, c), F32),
                           jax.ShapeDtypeStruct((t, c), MXU_DTYPE)],
                scratch_shapes=[pltpu.VMEM((tm, c), F32), pltpu.VMEM((tm, c), F32), pltpu.VMEM((SUBLANES, c), F32)],
                operands=[proj, proj, proj, conv_w, conv_b, wa, ba, wx, bx, lam, gain])


def _ret_consts():
    c = RET_CHUNK
    log_g = jnp.log1p(-jnp.exp2(-5.0 - jnp.arange(RET_HEADS, dtype=F32)))
    idx = jnp.arange(c, dtype=F32)
    diff = idx[:, None] - idx[None, :]
    decay = jnp.where(diff[None] >= 0, jnp.exp(jnp.maximum(diff, 0.0)[None] * log_g[:, None, None]), 0.0)
    zeta = jnp.exp((c - 1 - idx)[None, :] * log_g[:, None])
    xi = jnp.exp((idx + 1.0)[None, :] * log_g[:, None])
    spread = lambda v: jnp.repeat(v.T, RET_HEAD_DIM, axis=1)
    log_g_np = np.log1p(-np.exp2(-5.0 - np.arange(RET_HEADS, dtype=np.float32))).astype(np.float32)
    g_chunk = [float(np.exp(np.float32(c) * lg)) for lg in log_g_np]
    return decay, spread(xi), spread(zeta), g_chunk


def _rope_tables(t):
    pos = np.arange(t, dtype=np.float32)
    inv_freq = np.float32(ROPE_BASE) ** (-np.arange(0, RET_HEAD_DIM, 2, dtype=np.float32) / np.float32(RET_HEAD_DIM))
    ang = (pos[:, None] * inv_freq.astype(np.float32)[None, :]).astype(np.float32).astype(np.float64)
    cos, sin = np.cos(ang).astype(np.float32), np.sin(ang).astype(np.float32)
    return jnp.asarray(np.concatenate([cos, cos], axis=-1)), jnp.asarray(np.concatenate([-sin, sin], axis=-1))


def _rope(x, cos2, sin_signed):
    return x * cos2 + pltpu.roll(x, RET_HEAD_DIM // 2, 1) * sin_signed


def _rope_bwd(d, cos2, sin_signed):
    return d * cos2 + pltpu.roll(d * sin_signed, RET_HEAD_DIM // 2, 1)


RET_SCALE = RET_HEAD_DIM ** -0.5


RET_CHUNKS_PER_STEP = 2


def _ret_fwd(proj, cos2, sin_signed, gain):
    t = proj.shape[0]
    c, d, nh = RET_CHUNK, RET_HEAD_DIM, RET_HEADS
    n_chunks = t // c
    per = RET_CHUNKS_PER_STEP if n_chunks % RET_CHUNKS_PER_STEP == 0 else 1
    rows = per * c
    decay, xi, zeta, g_chunk = _ret_consts()

    def units(qk_ref, vg_ref, cos_ref, sin_ref, dec_ref, xi_ref, zeta_ref, gain_ref, o_ref, y_ref, st_ref, state, ctx):
        cur = [None] * nh

        def start():
            @pl.when(pl.program_id(0) == 0)
            def _():
                state[...] = jnp.zeros_like(state)
            for h in range(nh):
                cur[h] = state[h]

        def retain(s, h, keep):
            rs = slice(s * c, (s + 1) * c)
            cos2, sin_s = cos_ref[rs, :], sin_ref[rs, :]
            lo = h * d
            q = _rope(qk_ref[rs, lo:lo + d].astype(F32), cos2, sin_s)
            k = _rope(qk_ref[rs, D_RET + lo:D_RET + lo + d].astype(F32), cos2, sin_s) * RET_SCALE
            v = vg_ref[rs, lo:lo + d]
            s_prev = cur[h]
            st_ref[s, h] = s_prev
            scores = _dot_nt(q, k) * dec_ref[h]
            o = _dot(scores, v) + _dot(q * xi_ref[:, lo:lo + d], s_prev)
            cur[h] = s_prev * g_chunk[h] + _dot_tn(k * zeta_ref[:, lo:lo + d], v)
            o_ref[rs, lo:lo + d] = o
            keep["o"] = o

        def normalise(s, h, keep):
            rs = slice(s * c, (s + 1) * c)
            lo = h * d
            o = keep["o"]
            g = vg_ref[rs, D_RET + lo:D_RET + lo + d].astype(F32)
            mu = jnp.mean(o, axis=-1, keepdims=True)
            oc = o - mu
            on = oc * lax.rsqrt(jnp.mean(oc * oc, axis=-1, keepdims=True) + NORM_EPS)
            y_ref[rs, lo:lo + d] = (on * gain_ref[:, lo:lo + d] * (g * _sigmoid(g))).astype(y_ref.dtype)

        def end():
            for h in range(nh):
                state[h] = cur[h]

        pieces = [start]
        for s in range(per):
            for h in range(nh):
                keep = {}
                pieces += [lambda s=s, h=h, keep=keep: retain(s, h, keep),
                           lambda s=s, h=h, keep=keep: normalise(s, h, keep)]
        return pieces + [end]

    full2 = lambda i: (0, 0)
    return dict(units=units, grid=(n_chunks // per,),
                in_specs=[pl.BlockSpec((rows, 2 * D_RET), lambda i: (i, 1)),
                          pl.BlockSpec((rows, 2 * D_RET), lambda i: (i, 2)),
                          pl.BlockSpec((rows, d), lambda i: (i, 0)), pl.BlockSpec((rows, d), lambda i: (i, 0)),
                          pl.BlockSpec((nh, c, c), lambda i: (0, 0, 0)), pl.BlockSpec((c, D_RET), full2),
                          pl.BlockSpec((c, D_RET), full2), pl.BlockSpec((1, D_RET), full2)],
                out_specs=[pl.BlockSpec((rows, D_RET), lambda i: (i, 0)), pl.BlockSpec((rows, D_RET), lambda i: (i, 0)),
                           pl.BlockSpec((per, nh, d, d), lambda i: (i, 0, 0, 0))],
                out_shape=[jax.ShapeDtypeStruct((t, D_RET), F32), jax.ShapeDtypeStruct((t, D_RET), MXU_DTYPE),
                           jax.ShapeDtypeStruct((n_chunks, nh, d, d), F32)],
                scratch_shapes=[pltpu.VMEM((nh, d, d), F32)],
                operands=[proj, proj, cos2, sin_signed, decay, xi, zeta, gain])


def _outproj_fwd(x, y_lru, y_ret, w_out, g2, comm):
    t = x.shape[0]
    tm = min(512, t)

    def body(x_ref, yl_ref, yr_ref, w_ref, g_ref, h1_ref, u2_ref):
        h1 = x_ref[...] + _dot(yl_ref[...], w_ref[:D_LRU, :]) + _dot(yr_ref[...], w_ref[D_LRU:, :])
        h1_ref[...] = h1
        _, _, u = _rms_fwd(h1, g_ref[...])
        u2_ref[...] = u.astype(u2_ref.dtype)

    row = lambda i: (i, 0)
    return _call(body, name="outproj_fwd", grid=(t // tm,), comm=comm,
                 in_specs=[pl.BlockSpec((tm, D_MODEL), row), pl.BlockSpec((tm, D_LRU), row), pl.BlockSpec((tm, D_RET), row),
                           _resident((D_MODEL, D_MODEL)), pl.BlockSpec((1, D_MODEL), lambda i: (0, 0))],
                 out_specs=[pl.BlockSpec((tm, D_MODEL), row), pl.BlockSpec((tm, D_MODEL), row)],
                 out_shape=[jax.ShapeDtypeStruct((t, D_MODEL), F32), jax.ShapeDtypeStruct((t, D_MODEL), MXU_DTYPE)],
                 operands=[x, y_lru, y_ret, w_out, g2])


FFN_TN = 768
FFN_NJ = D_FF // FFN_TN
FFN_GROUP = 4


def _ffn_fwd(u2, w_blocks, conv_w, conv_b, w_down, h1, gf, target):
    t = u2.shape[0]
    tm = min(256, t)
    tn, nj, group = FFN_TN, FFN_NJ, FFN_GROUP
    ng, tw = nj // group, group * tn
    hb = _halo_rows(u2.dtype)
    assert w_blocks.shape == (2 * nj, D_MODEL, tn)

    def project(u_ext, w, col, up_ref, conv_ref, cw_ref, cb_ref, first):
        ext = _dot(u_ext, w)
        x = ext[hb:, :]
        up_ref[:, col] = x.astype(up_ref.dtype)
        prev = jnp.where(first, 0.0, ext[hb - SUBLANES:hb, :])
        cw = cw_ref[:, col]
        y = cb_ref[:, col] + cw[FFN_CONV - 1:FFN_CONV, :] * x
        for k in range(FFN_CONV - 1):
            y = y + cw[k:k + 1, :] * _shift_down(prev, x, FFN_CONV - 1 - k)
        conv_ref[:, col] = y.astype(conv_ref.dtype)
        return y

    def body(u_ref, uh_ref, w_ref, cwa_ref, cwv_ref, cba_ref, cbv_ref, wd_ref, h1_ref, gf_ref, tg_ref,
             upa_ref, upv_ref, ca_ref, cv_ref, act_ref, dh_ref, dhb_ref, dgf_ref, loss_ref, acc):
        i, jg = pl.program_id(0), pl.program_id(1)

        @pl.when((i == 0) & (jg == 0))
        def _():
            dgf_ref[...] = jnp.zeros_like(dgf_ref)
            loss_ref[...] = jnp.zeros_like(loss_ref)

        @pl.when(jg == 0)
        def _():
            acc[...] = jnp.zeros_like(acc)

        u_ext = jnp.concatenate([uh_ref[...], u_ref[...]], axis=0)
        down = None
        for jj in range(group):
            col = slice(jj * tn, (jj + 1) * tn)
            j = jg * group + jj
            a = project(u_ext, w_ref[j], col, upa_ref, ca_ref, cwa_ref, cba_ref, i == 0)
            v = project(u_ext, w_ref[nj + j], col, upv_ref, cv_ref, cwv_ref, cbv_ref, i == 0)
            act = (_gelu(a) * v).astype(act_ref.dtype)
            act_ref[:, col] = act
            part = _dot(act, wd_ref[pl.ds(pl.multiple_of(j * tn, tn), tn), :])
            down = part if down is None else down + part
        acc[...] += down

        @pl.when(jg == ng - 1)
        def _():
            n, rstd, y = _rms_fwd(h1_ref[...] + acc[...], gf_ref[...])
            err = y - tg_ref[...]
            loss_ref[...] += (0.5 / D_MODEL) * jnp.sum(err * err)
            dh, dgf = _rms_bwd(err * (1.0 / D_MODEL), n, rstd, gf_ref[...])
            dgf_ref[...] += dgf
            dh_ref[...] = dh
            dhb_ref[...] = dh.astype(dhb_ref.dtype)

    per = tm // hb
    row = lambda i, j: (i, 0)
    const = lambda i, j: (0, 0)
    tile = pl.BlockSpec((tm, tw), lambda i, j: (i, j))
    return _call(body, name="ffn_fwd", grid=(t // tm, ng),
                 in_specs=[pl.BlockSpec((tm, D_MODEL), row),
                           pl.BlockSpec((hb, D_MODEL), lambda i, j: (jnp.maximum(i * per - 1, 0), 0)),
                           _resident(w_blocks.shape),
                           pl.BlockSpec((FFN_CONV, tw), lambda i, j: (0, j)),
                           pl.BlockSpec((FFN_CONV, tw), lambda i, j: (0, j + ng)),
                           pl.BlockSpec((1, tw), lambda i, j: (0, j)), pl.BlockSpec((1, tw), lambda i, j: (0, j + ng)),
                           _resident((D_FF, D_MODEL)),
                           pl.BlockSpec((tm, D_MODEL), row), pl.BlockSpec((1, D_MODEL), const),
                           pl.BlockSpec((tm, D_MODEL), row)],
                 out_specs=[tile] * 5 + [pl.BlockSpec((tm, D_MODEL), row),
                            pl.BlockSpec((tm, D_MODEL), row), pl.BlockSpec((SUBLANES, D_MODEL), const),
                            pl.BlockSpec((SUBLANES, LANES), const)],
                 out_shape=[jax.ShapeDtypeStruct((t, D_FF), MXU_DTYPE)] * 5 + [
                            jax.ShapeDtypeStruct((t, D_MODEL), F32),
                            jax.ShapeDtypeStruct((t, D_MODEL), MXU_DTYPE), jax.ShapeDtypeStruct((SUBLANES, D_MODEL), F32),
                            jax.ShapeDtypeStruct((SUBLANES, LANES), F32)],
                 scratch_shapes=[pltpu.VMEM((tm, D_MODEL), F32)],
                 operands=[u2, u2, w_blocks, conv_w, conv_w, conv_b, conv_b, w_down, h1, gf, target])


FFN_ACC_ROWS = SUBLANES * (FFN_CONV + 1)


def _ffn_bwd(dh2, dh2_b, w_down, up_a, up_v, conv_a, conv_v, conv_w, w_up_blocks, h1, g2, comm):
    t = up_a.shape[0]
    tm = min(256, t)
    tn, nj, group = FFN_TN, FFN_NJ, FFN_GROUP
    ng, tw = nj // group, group * tn
    ni = t // tm
    assert w_up_blocks.shape == (2 * nj, D_MODEL, tn)

    def conv_bwd(dy, x, cw, acc_ref, carry_ref, dup_ref, col):
        nxt = carry_ref[...]
        carry_ref[...] = dy[:SUBLANES, :]
        ahead = [_shift_up(dy, nxt, FFN_CONV - 1 - k) for k in range(FFN_CONV)]
        dx = cw[FFN_CONV - 1:FFN_CONV, :] * dy
        for k in range(FFN_CONV - 1):
            dx = dx + cw[k:k + 1, :] * ahead[k]
        dx = dx.astype(dup_ref.dtype)
        dup_ref[:, col] = dx
        for k in range(FFN_CONV):
            acc_ref[k * SUBLANES:(k + 1) * SUBLANES, :] += _colsum8(ahead[k] * x)
        acc_ref[FFN_CONV * SUBLANES:, :] += _colsum8(dy)
        return dx

    def body(dh_ref, dhb_ref, wd_ref, ua_ref, uv_ref, ca_ref, cv_ref, cwa_ref, cwv_ref, wu_ref, h1_ref, g2_ref,
             dua_ref, duv_ref, acca_ref, accv_ref, dh1_ref, dh1b_ref, dg2_ref, carry_a, carry_v, du):
        i, jg = pl.program_id(0), pl.program_id(1)

        @pl.when((i == 0) & (jg == 0))
        def _():
            for ref in (acca_ref, accv_ref, carry_a, carry_v, dg2_ref):
                ref[...] = jnp.zeros_like(ref)

        dhb = dhb_ref[...]
        part = None
        for jj in range(group):
            col = slice(jj * tn, (jj + 1) * tn)
            j = jg * group + jj
            v = cv_ref[:, col].astype(F32)
            g, dg = _gelu_parts(ca_ref[:, col].astype(F32))
            dact = _dot_nt(dhb, wd_ref[pl.ds(pl.multiple_of(j * tn, tn), tn), :])
            da = conv_bwd(dact * v * dg, ua_ref[:, col].astype(F32), cwa_ref[:, col], acca_ref.at[j], carry_a.at[j],
                          dua_ref, col)
            dv = conv_bwd(dact * g, uv_ref[:, col].astype(F32), cwv_ref[:, col], accv_ref.at[j], carry_v.at[j],
                          duv_ref, col)
            term = _dot_nt(da, wu_ref[j]) + _dot_nt(dv, wu_ref[nj + j])
            part = term if part is None else part + term

        @pl.when(jg == 0)
        def _():
            du[...] = part

        @pl.when(jg > 0)
        def _():
            du[...] += part

        @pl.when(jg == ng - 1)
        def _():
            n, rstd, _ = _rms_fwd(h1_ref[...], g2_ref[...])
            dh1, dg2 = _rms_bwd(du[...], n, rstd, g2_ref[...])
            dh1 = dh1 + dh_ref[...]
            dg2_ref[...] += dg2
            dh1_ref[...] = dh1
            dh1b_ref[...] = dh1.astype(dh1b_ref.dtype)

    row = lambda i, j: (ni - 1 - i, 0)
    const = lambda i, j: (0, 0)
    tile = pl.BlockSpec((tm, tw), lambda i, j: (ni - 1 - i, j))
    acc = pl.BlockSpec((nj, FFN_ACC_ROWS, tn), lambda i, j: (0, 0, 0))
    return _call(body, name="ffn_bwd", grid=(ni, ng), comm=comm,
                 in_specs=[pl.BlockSpec((tm, D_MODEL), row), pl.BlockSpec((tm, D_MODEL), row),
                           _resident((D_FF, D_MODEL)), tile, tile, tile, tile,
                           pl.BlockSpec((FFN_CONV, tw), lambda i, j: (0, j)),
                           pl.BlockSpec((FFN_CONV, tw), lambda i, j: (0, j + ng)),
                           _resident(w_up_blocks.shape), pl.BlockSpec((tm, D_MODEL), row),
                           pl.BlockSpec((1, D_MODEL), const)],
                 out_specs=[tile, tile, acc, acc, pl.BlockSpec((tm, D_MODEL), row), pl.BlockSpec((tm, D_MODEL), row),
                            pl.BlockSpec((SUBLANES, D_MODEL), const)],
                 out_shape=[jax.ShapeDtypeStruct((t, D_FF), MXU_DTYPE), jax.ShapeDtypeStruct((t, D_FF), MXU_DTYPE),
                            jax.ShapeDtypeStruct((nj, FFN_ACC_ROWS, tn), F32),
                            jax.ShapeDtypeStruct((nj, FFN_ACC_ROWS, tn), F32),
                            jax.ShapeDtypeStruct((t, D_MODEL), F32), jax.ShapeDtypeStruct((t, D_MODEL), MXU_DTYPE),
                            jax.ShapeDtypeStruct((SUBLANES, D_MODEL), F32)],
                 scratch_shapes=[pltpu.VMEM((nj, SUBLANES, tn), F32), pltpu.VMEM((nj, SUBLANES, tn), F32),
                                 pltpu.VMEM((tm, D_MODEL), F32)],
                 operands=[dh2, dh2_b, w_down, up_a, up_v, conv_a, conv_v, conv_w, conv_w, w_up_blocks, h1, g2])


def _ret_bwd(proj, cos2, sin_signed, gain, o, states, dmix_at):
    t = proj.shape[0]
    c, d, nh = RET_CHUNK, RET_HEAD_DIM, RET_HEADS
    n_chunks = t // c
    per = RET_CHUNKS_PER_STEP if n_chunks % RET_CHUNKS_PER_STEP == 0 else 1
    rows = per * c
    n_steps = n_chunks // per
    decay, xi, zeta, g_chunk = _ret_consts()
    base = 2 * D_LRU

    def units(qk_ref, vg_ref, cos_ref, sin_ref, dec_ref, xi_ref, zeta_ref, gain_ref, o_ref, st_ref,
              dp_ref, dgain_ref, gstate, ctx):
        cur = [None] * nh
        dmix = ctx["scratch"][dmix_at[0]][dmix_at[1]]

        def start():
            @pl.when(pl.program_id(0) == 0)
            def _():
                gstate[...] = jnp.zeros_like(gstate)
                dgain_ref[...] = jnp.zeros_like(dgain_ref)
            for h in range(nh):
                cur[h] = gstate[h]

        def gate_and_norm(s, h, keep):
            rs = slice(s * c, (s + 1) * c)
            lo = h * d
            g = vg_ref[rs, D_RET + lo:D_RET + lo + d].astype(F32)
            gain_h = gain_ref[:, lo:lo + d]
            dy = dmix[rs, D_LRU + lo:D_LRU + lo + d]
            sg = _sigmoid(g)
            o_h = o_ref[rs, lo:lo + d]
            oc = o_h - jnp.mean(o_h, axis=-1, keepdims=True)
            rstd = lax.rsqrt(jnp.mean(oc * oc, axis=-1, keepdims=True) + NORM_EPS)
            on = oc * rstd
            at = base + 3 * D_RET + lo
            dp_ref[rs, at:at + d] = (dy * on * gain_h * (sg * (1.0 + g * (1.0 - sg)))).astype(dp_ref.dtype)
            don_g = dy * (g * sg)
            dgain_ref[:, lo:lo + d] += _colsum8(don_g * on)
            don = don_g * gain_h
            keep["do"] = rstd * (don - jnp.mean(don, axis=-1, keepdims=True)
                                 - on * jnp.mean(don * on, axis=-1, keepdims=True))

        def retain(s, h, keep):
            rs = slice(s * c, (s + 1) * c)
            cos2, sin_s = cos_ref[rs, :], sin_ref[rs, :]
            lo = h * d
            q = _rope(qk_ref[rs, lo:lo + d].astype(F32), cos2, sin_s)
            k = _rope(qk_ref[rs, D_RET + lo:D_RET + lo + d].astype(F32), cos2, sin_s) * RET_SCALE
            v = vg_ref[rs, lo:lo + d]
            xi_h, zeta_h, dec = xi_ref[:, lo:lo + d], zeta_ref[:, lo:lo + d], dec_ref[h]
            do = keep["do"]
            s_prev = st_ref[s, h]
            g_next = cur[h]
            p = _dot_nt(q, k) * dec
            dpm = _dot_nt(do, v) * dec
            keep["dq"] = _dot(dpm, k) + _dot_nt(do, s_prev) * xi_h
            keep["dk"] = _dot_tn(dpm, q) + _dot_nt(v, g_next) * zeta_h
            dv = _dot_tn(p, do) + _dot(k * zeta_h, g_next)
            cur[h] = g_next * g_chunk[h] + _dot_tn(q * xi_h, do)
            at = base + 2 * D_RET + lo
            dp_ref[rs, at:at + d] = dv.astype(dp_ref.dtype)

        def unrope(s, h, keep):
            rs = slice(s * c, (s + 1) * c)
            cos2, sin_s = cos_ref[rs, :], sin_ref[rs, :]
            lo = h * d
            dp_ref[rs, base + lo:base + lo + d] = _rope_bwd(keep["dq"], cos2, sin_s).astype(dp_ref.dtype)
            at = base + D_RET + lo
            dp_ref[rs, at:at + d] = _rope_bwd(keep["dk"] * RET_SCALE, cos2, sin_s).astype(dp_ref.dtype)

        def end():
            for h in range(nh):
                gstate[h] = cur[h]

        pieces = [start]
        for s in reversed(range(per)):
            for h in range(nh):
                keep = {}
                pieces += [lambda s=s, h=h, keep=keep, f=f: f(s, h, keep) for f in (gate_and_norm, retain, unrope)]
        return pieces + [end]

    rev = lambda col: (lambda i: (n_steps - 1 - i, col))
    full2 = lambda i: (0, 0)
    return dict(units=units, grid=(n_steps,),
                in_specs=[pl.BlockSpec((rows, 2 * D_RET), rev(1)), pl.BlockSpec((rows, 2 * D_RET), rev(2)),
                          pl.BlockSpec((rows, d), rev(0)), pl.BlockSpec((rows, d), rev(0)),
                          pl.BlockSpec((nh, c, c), lambda i: (0, 0, 0)), pl.BlockSpec((c, D_RET), full2),
                          pl.BlockSpec((c, D_RET), full2), pl.BlockSpec((1, D_RET), full2),
                          pl.BlockSpec((rows, D_RET), rev(0)),
                          pl.BlockSpec((per, nh, d, d), lambda i: (n_steps - 1 - i, 0, 0, 0))],
                out_specs=[pl.BlockSpec((rows, D_IN), rev(0)), pl.BlockSpec((SUBLANES, D_RET), full2)],
                out_shape=[jax.ShapeDtypeStruct((t, D_IN), MXU_DTYPE), jax.ShapeDtypeStruct((SUBLANES, D_RET), F32)],
                scratch_shapes=[pltpu.VMEM((nh, d, d), F32)],
                operands=[proj, proj, cos2, sin_signed, decay, xi, zeta, gain, o, states])


LRU_ACC = {"conv_w": 0, "conv_b": LRU_CONV, "gate_a_b": LRU_CONV + 1, "gate_x_b": LRU_CONV + 2,
           "lambda": LRU_CONV + 3, "norm_gain": LRU_CONV + 4}
LRU_ACC_ROWS = SUBLANES * (LRU_CONV + 5)


def _lru_bwd(proj, xc_all, h_all, conv_w, wa, ba, wx, bx, lam, gain, dproj_part, dmix_at):
    t = proj.shape[0]
    tm = min(256, t)
    c = D_LRU
    ni = t // tm

    def body(x_ref, xh_ref, g_ref, xc_ref, h_ref, hh_ref, cw_ref, wa_ref, ba_ref, wx_ref, bx_ref, lam_ref,
             gain_ref, acc_ref, dwa_ref, dwx_ref, a_scr, b_scr, mu_scr, carry_mu, carry_dxc, ctx):
        dp_ref = ctx["outs"][dproj_part][0]
        dmix = ctx["scratch"][dmix_at[0]][dmix_at[1]]
        fill = ctx["fill"]
        i = pl.program_id(0)
        r = ni - 1 - i

        @pl.when(i == 0)
        def _():
            acc_ref[...] = jnp.zeros_like(acc_ref)
            dwa_ref[...] = jnp.zeros_like(dwa_ref)
            dwx_ref[...] = jnp.zeros_like(dwx_ref)
            carry_mu[...] = jnp.zeros_like(carry_mu)
            carry_dxc[...] = jnp.zeros_like(carry_dxc)

        def add(name, val, k=0):
            lo = (LRU_ACC[name] + k) * SUBLANES
            acc_ref[lo:lo + SUBLANES, :] += _colsum8(val)

        fill()
        xc, h = xc_ref[...], h_ref[...]
        lam_v = lam_ref[...]
        sp = _softplus(-lam_v)
        rg, ig, a, m = _lru_gates(xc, wa_ref[...], ba_ref[...], wx_ref[...], bx_ref[...], sp, fill)
        gl, dgl = _gelu_parts(g_ref[...].astype(F32))
        fill()
        zn, rstd, _ = _rms_fwd(h * gl, gain_ref[...])
        dy = dmix[:, :c]
        dz, dgain = _rms_bwd(dy, zn, rstd, gain_ref[...])
        lo = LRU_ACC["norm_gain"] * SUBLANES
        acc_ref[lo:lo + SUBLANES, :] += dgain
        dp_ref[:, c:2 * c] = (dz * h * dgl).astype(dp_ref.dtype)
        dh = dz * gl
        fill()
        ga, gb = _group_scan(a, a * dh, reverse=True, fill=fill)
        a_scr[...] = ga
        b_scr[...] = gb
        mu_next_tile = carry_mu[...]
        carry_mu[...] = _carry_scan(a_scr, b_scr, mu_scr, mu_next_tile, reverse=True)
        fill()
        lam_t = dh + _shift_up(mu_scr[...], mu_next_tile, 1)
        h_prev = _shift_down(jnp.where(r == 0, 0.0, hh_ref[...]), h, 1)
        da = lam_t * h_prev
        dig = lam_t * m * xc
        dxc = lam_t * m * ig
        dlog_a = da * a - (lam_t * ig * xc) * (a * a) / m
        fill()
        dpr = dlog_a * ((-LRU_C) * sp) * rg * (1.0 - rg)
        add("lambda", dlog_a * ((-LRU_C) * rg) * (-_sigmoid(-lam_v)))
        dpi = dig * ig * (1.0 - ig)
        add("gate_a_b", dpr)
        add("gate_x_b", dpi)
        fill()
        dwa_ref[...] += _dot_tn(xc, dpr)
        dwx_ref[...] += _dot_tn(xc, dpi)
        dxc = dxc + _dot_nt(dpr, wa_ref[...]) + _dot_nt(dpi, wx_ref[...])
        fill()
        add("conv_b", dxc)
        x = x_ref[...].astype(F32)
        prev = jnp.where(r == 0, 0.0, xh_ref[...].astype(F32)[-SUBLANES:, :])
        cw = cw_ref[...]
        nxt = carry_dxc[...]
        carry_dxc[...] = dxc[:SUBLANES, :]
        dx = cw[LRU_CONV - 1:LRU_CONV, :] * dxc
        for k in range(LRU_CONV - 1):
            dx = dx + cw[k:k + 1, :] * _shift_up(dxc, nxt, LRU_CONV - 1 - k)
        fill()
        for k in range(LRU_CONV):
            add("conv_w", dxc * _shift_down(prev, x, LRU_CONV - 1 - k), k)
        dp_ref[:, :c] = dx.astype(dp_ref.dtype)

    hb = _halo_rows(proj.dtype)
    rev = lambda col: (lambda i: (ni - 1 - i, col))
    halo = lambda rows: (lambda i: (jnp.maximum((ni - 1 - i) * (tm // rows) - 1, 0), 0))
    full = lambda i: (0, 0)
    vec = pl.BlockSpec((1, c), full)
    mat = pl.BlockSpec((c, c), full)
    return dict(body=body, grid=(ni,), fill_points=12,
                in_specs=[pl.BlockSpec((tm, c), rev(0)), pl.BlockSpec((hb, c), halo(hb)), pl.BlockSpec((tm, c), rev(1)),
                          pl.BlockSpec((tm, c), rev(0)), pl.BlockSpec((tm, c), rev(0)),
                          pl.BlockSpec((SUBLANES, c), halo(SUBLANES)),
                          pl.BlockSpec((LRU_CONV, c), full), mat, vec, mat, vec, vec, vec],
                out_specs=[pl.BlockSpec((LRU_ACC_ROWS, c), full), mat, mat],
                out_shape=[jax.ShapeDtypeStruct((LRU_ACC_ROWS, c), F32), jax.ShapeDtypeStruct((c, c), F32),
                           jax.ShapeDtypeStruct((c, c), F32)],
                scratch_shapes=[pltpu.VMEM((tm, c), F32), pltpu.VMEM((tm, c), F32), pltpu.VMEM((tm, c), F32),
                                pltpu.VMEM((SUBLANES, c), F32), pltpu.VMEM((SUBLANES, c), F32)],
                operands=[proj, proj, proj, xc_all, h_all, h_all, conv_w, wa, ba, wx, bx, lam, gain])


def _mix_proj_bwd(dh1, dh1_b, w_out, w_in_blocks, x, g1, dproj_part):
    t = x.shape[0]
    tm = min(256, t)
    ni = t // tm
    nb, _, cb = w_in_blocks.shape
    first_free = -(-2 * D_LRU // cb)
    du = [None]

    def term(dp_ref, w_ref, d):
        part = _dot_nt(dp_ref[:, d * cb:(d + 1) * cb], w_ref[d])
        du[0] = part if du[0] is None else du[0] + part

    def head(dh_ref, dhb_ref, wo_ref, wi_ref, x_ref, g_ref, gx_ref, dg_ref, dmix, ctx):
        @pl.when(pl.program_id(0) == 0)
        def _():
            dg_ref[...] = jnp.zeros_like(dg_ref)
        dmix[...] = _dot_nt(dhb_ref[...], wo_ref[...])
        du[0] = None

    def units(dh_ref, dhb_ref, wo_ref, wi_ref, x_ref, g_ref, gx_ref, dg_ref, dmix, ctx):
        dp_ref = ctx["outs"][dproj_part][0]
        return [lambda d=d: term(dp_ref, wi_ref, d) for d in range(first_free, nb)]

    def tail(dh_ref, dhb_ref, wo_ref, wi_ref, x_ref, g_ref, gx_ref, dg_ref, dmix, ctx):
        dp_ref = ctx["outs"][dproj_part][0]
        for d in range(first_free):
            term(dp_ref, wi_ref, d)
        n, rstd, _ = _rms_fwd(x_ref[...], g_ref[...])
        dx, dg = _rms_bwd(du[0], n, rstd, g_ref[...])
        dg_ref[...] += dg
        gx_ref[...] = dx + dh_ref[...]

    row = lambda i: (ni - 1 - i, 0)
    const = lambda i: (0, 0)
    tile = pl.BlockSpec((tm, D_MODEL), row)
    return dict(head=head, units=units, tail=tail, grid=(ni,),
                in_specs=[tile, tile, _resident(w_out.shape), _resident(w_in_blocks.shape), tile,
                          pl.BlockSpec((1, D_MODEL), const)],
                out_specs=[tile, pl.BlockSpec((SUBLANES, D_MODEL), const)],
                out_shape=[jax.ShapeDtypeStruct((t, D_MODEL), F32), jax.ShapeDtypeStruct((SUBLANES, D_MODEL), F32)],
                scratch_shapes=[pltpu.VMEM((tm, D_MODEL), F32)],
                operands=[dh1, dh1_b, w_out, w_in_blocks, x, g1])


def _pair_sum(core, a, b, name):
    n, r, c = b.shape
    spec = pl.BlockSpec((None, r, c), lambda q, core: (q, 0, 0))

    def body(core_ref, a_ref, b_ref, o_ref):
        o_ref[...] = (a_ref[...].astype(F32) + b_ref[...].astype(F32)).astype(o_ref.dtype)

    return pl.pallas_call(
        body, name=name,
        grid_spec=pltpu.PrefetchScalarGridSpec(
            num_scalar_prefetch=1, grid=(n,),
            in_specs=[pl.BlockSpec((None, r, c), lambda q, core: (2 * q + core[0], 0, 0)), spec], out_specs=spec),
        out_shape=jax.ShapeDtypeStruct(b.shape, b.dtype),
        compiler_params=pltpu.CompilerParams(dimension_semantics=("arbitrary",), vmem_limit_bytes=VMEM_LIMIT),
    )(core, a, b)


ADAMW_BLOCK_BYTES = 4 * 1024 * 1024


def _sum_adamw(parts, w, m, v, name):
    n_parts, r, c = parts.shape
    tr = r
    while n_parts * tr * c * parts.dtype.itemsize > ADAMW_BLOCK_BYTES and tr % (4 * SUBLANES) == 0:
        tr //= 2

    def body(p_ref, w_ref, m_ref, v_ref, g_ref, d_ref, nm_ref, nv_ref):
        g = p_ref[0].astype(F32)
        for s in range(1, n_parts):
            g = g + p_ref[s].astype(F32)
        nm = ADAM_B1 * m_ref[...] + (1.0 - ADAM_B1) * g
        nv = ADAM_B2 * v_ref[...] + (1.0 - ADAM_B2) * (g * g)
        m_hat = nm / (1.0 - ADAM_B1 ** ADAM_STEP)
        v_hat = nv / (1.0 - ADAM_B2 ** ADAM_STEP)
        g_ref[...] = g
        d_ref[...] = -ADAM_LR * (m_hat / (jnp.sqrt(v_hat) + ADAM_EPS) + ADAM_WD * w_ref[...])
        nm_ref[...] = nm
        nv_ref[...] = nv

    row = pl.BlockSpec((tr, c), lambda i: (i, 0))
    return _call(body, name=name, grid=(r // tr,),
                 in_specs=[pl.BlockSpec((n_parts, tr, c), lambda i: (0, i, 0)), row, row, row],
                 out_specs=[row, row, row, row], out_shape=[jax.ShapeDtypeStruct((r, c), F32)] * 4,
                 operands=[parts, w, m, v])


MATRICES = ("w_in", "w_out", "ffn_up_w", "ffn_down_w")
CONVS = ("lru_conv_w", "ffn_conv_w")
REPLICATED = ("norm1_gain", "lru_conv_b", "lru_gate_a_w", "lru_gate_a_b", "lru_gate_x_w", "lru_gate_x_b", "lru_lambda",
              "lru_norm_gain", "ret_norm_gain", "norm2_gain", "ffn_conv_b", "final_norm_gain")
WEIGHTS = ("norm1_gain", "w_in", "lru_conv_w", "lru_conv_b", "lru_gate_a_w", "lru_gate_a_b", "lru_gate_x_w",
           "lru_gate_x_b", "lru_lambda", "lru_norm_gain", "ret_norm_gain", "w_out", "norm2_gain", "ffn_up_w",
           "ffn_conv_w", "ffn_conv_b", "ffn_down_w", "final_norm_gain")


def _rows(a, pad_to):
    a = a.reshape(-1, LANES)
    pad = (-a.shape[0]) % pad_to
    return jnp.pad(a, ((0, pad), (0, 0))) if pad else a


def _pack(arrays, pad_to):
    rows, layout, at = [], [], 0
    for a in arrays:
        r = _rows(a, pad_to)
        layout.append((at, a.size // LANES, a.shape))
        rows.append(r)
        at += r.shape[0]
    return jnp.concatenate(rows, axis=0), layout


def _unpack(packed, layout):
    lead = packed.shape[:-2]
    return [packed[..., at:at + n, :].reshape(lead + shape) for at, n, shape in layout]


def _conv_rows(lru, ffn, dtype, pad_to):
    lead = lru.shape[:-2]
    flat = jnp.concatenate([lru.reshape(lead + (-1,)), ffn.reshape(lead + (-1,))], axis=-1).astype(dtype)
    rows = flat.shape[-1] // LANES
    pad = (-rows) % pad_to
    return jnp.pad(flat.reshape(lead + (rows, LANES)), [(0, 0)] * len(lead) + [(0, pad), (0, 0)])


def _column_blocks(full):
    r, c = full.shape
    return full.reshape(r, N_DEV, c // N_DEV).transpose(1, 0, 2)


def _block_diag(w):
    nh, d, _ = w.shape
    eye = jnp.eye(nh, dtype=w.dtype)
    return (w[:, :, None, :] * eye[:, None, :, None]).reshape(nh * d, nh * d)


def _diag_blocks(dense, nh):
    d = dense.shape[0] // nh
    blocks = dense.reshape(nh, d, nh, d)
    return jnp.stack([blocks[h, :, h, :] for h in range(nh)], axis=0)


def kernel(x, norm1_gain, w_in, lru_conv_w, lru_conv_b, lru_gate_a_w, lru_gate_a_b, lru_gate_x_w, lru_gate_x_b, lru_lambda, lru_norm_gain, ret_norm_gain, w_out, norm2_gain, ffn_up_w, ffn_conv_w, ffn_conv_b, ffn_down_w, final_norm_gain, loss_target, m_norm1_gain, m_w_in, m_lru_conv_w, m_lru_conv_b, m_lru_gate_a_w, m_lru_gate_a_b, m_lru_gate_x_w, m_lru_gate_x_b, m_lru_lambda, m_lru_norm_gain, m_ret_norm_gain, m_w_out, m_norm2_gain, m_ffn_up_w, m_ffn_conv_w, m_ffn_conv_b, m_ffn_down_w, m_final_norm_gain, v_norm1_gain, v_w_in, v_lru_conv_w, v_lru_conv_b, v_lru_gate_a_w, v_lru_gate_a_b, v_lru_gate_x_w, v_lru_gate_x_b, v_lru_lambda, v_lru_norm_gain, v_ret_norm_gain, v_w_out, v_norm2_gain, v_ffn_up_w, v_ffn_conv_w, v_ffn_conv_b, v_ffn_down_w, v_final_norm_gain):
    args = dict(locals())
    given = {n: args[n] for n in WEIGHTS}
    out_shape = {n: given[n].shape for n in WEIGHTS}

    def plain(a):
        return a.reshape(1, -1) if a.ndim <= 2 else a[0]

    w = {n: plain(given[n]) for n in WEIGHTS}
    mom_m = {n: plain(args["m_" + n]) for n in WEIGHTS}
    mom_v = {n: plain(args["v_" + n]) for n in WEIGHTS}
    x2, target = x[0], loss_target[0]
    t = x2.shape[0]
    core = lax.axis_index("c").astype(jnp.int32).reshape(1)
    res = {}

    conv_pad = _conv_rows(w["lru_conv_w"], w["ffn_conv_w"], F32, SUBLANES)
    first = _gather_first([w["w_in"].astype(MXU_DTYPE), conv_pad])
    w_in_blocks, conv_all = _run_comms([first, _gather_second(first.out_shape)], "w_in_all_gather")
    n_lru = w["lru_conv_w"].size
    conv_flat = conv_all.reshape(N_DEV, -1)
    lru_cw = conv_flat[:, :n_lru].reshape((N_DEV,) + w["lru_conv_w"].shape).transpose(1, 0, 2).reshape(LRU_CONV, D_LRU)
    ffn_cw = conv_flat[:, n_lru:n_lru + w["ffn_conv_w"].size].reshape((N_DEV,) + w["ffn_conv_w"].shape)
    ffn_cw = ffn_cw.transpose(1, 0, 2).reshape(FFN_CONV, 2 * D_FF)

    cos2, sin_signed = _rope_tables(t)
    wa = _block_diag(w["lru_gate_a_w"]).astype(MXU_DTYPE)
    wx = _block_diag(w["lru_gate_x_w"]).astype(MXU_DTYPE)
    gf = w["final_norm_gain"]

    early = _gather_first([w["ffn_up_w"].astype(MXU_DTYPE), w["w_out"].astype(MXU_DTYPE)])
    (u1, proj), (up_part, w_out_part) = _inproj_fwd(x2, w["norm1_gain"], w_in_blocks, early)
    ((xc, h_lru, y_lru), (o_ret, y_ret, states)), (up_blocks, w_out_blocks, down_part) = _fused(
        [_lru_fwd(proj, lru_cw, w["lru_conv_b"], wa, w["lru_gate_a_b"], wx, w["lru_gate_x_b"], w["lru_lambda"],
                  w["lru_norm_gain"]),
         _ret_fwd(proj, cos2, sin_signed, w["ret_norm_gain"])],
        "mix_fwd", _both(_gather_second([up_part, w_out_part]), _gather_first([w["ffn_down_w"].astype(MXU_DTYPE)])))
    w_out_full = w_out_blocks.reshape(D_MODEL, D_MODEL)

    (h1, u2), (down_blocks,) = _outproj_fwd(x2, y_lru, y_ret, w_out_full, w["norm2_gain"], _gather_second([down_part]))
    w_down_full = down_blocks.reshape(D_FF, D_MODEL)
    up_a, up_v, conv_a, conv_v, act, dh2, dh2_b, dgf, loss = _ffn_fwd(u2, up_blocks, ffn_cw, w["ffn_conv_b"], w_down_full,
                                                                      h1, gf, target)
    loss = lax.psum(loss[0, 0], ("x", "y", "c"))

    def to_owner_chips(blocks, names, tag):
        theirs = _run_comms([_pair_exchange(blocks)], "grads_pair_exchange_" + tag)
        return [_pair_sum(core, a, b, "grads_pair_sum_" + n) for n, a, b in zip(names, blocks, theirs)]

    def adamw(name, parts):
        res[name] = _sum_adamw(parts, w[name], mom_m[name], mom_v[name], "adamw_" + name)

    g = {"final_norm_gain": dgf[0]}
    g_down = _mm_tn(act, dh2_b, "ffn_down_wgrad").reshape(N_DEV, D_FF // N_DEV, D_MODEL)
    down_sums = to_owner_chips([g_down], ["ffn_down_w"], "down")
    (dup_a, dup_v, acc_a, acc_v, dh1, dh1_b, dg2), (down_parts,) = _ffn_bwd(
        dh2, dh2_b, w_down_full, up_a, up_v, conv_a, conv_v, ffn_cw, up_blocks, h1, w["norm2_gain"],
        _chip_exchange(down_sums))
    adamw("ffn_down_w", down_parts)
    per_col = lambda a: a[:, ::SUBLANES].transpose(1, 0, 2).reshape(FFN_CONV + 1, D_FF)
    acc = jnp.concatenate([per_col(acc_a), per_col(acc_v)], axis=1)
    g_ffn_cw, g["ffn_conv_b"] = acc[:FFN_CONV], acc[FFN_CONV:]
    g["norm2_gain"] = dg2[:1]
    g_up = jnp.concatenate([_mm_tn(u2, dup_a, "ffn_up_wgrad_a", blocks=N_DEV // 2),
                            _mm_tn(u2, dup_v, "ffn_up_wgrad_v", blocks=N_DEV // 2)], axis=0)
    g_out = jnp.concatenate([_mm_tn(y_lru, dh1_b, "w_out_wgrad_lru"), _mm_tn(y_ret, dh1_b, "w_out_wgrad_ret")], axis=0)
    mid_sums = to_owner_chips([g_up, g_out.reshape(N_DEV, D_MODEL // N_DEV, D_MODEL)], ["ffn_up_w", "w_out"], "mid")
    (dproj, dgain_ret), (grad_x, dg1), (lru_acc, dwa, dwx) = _fused(
        [_ret_bwd(proj, cos2, sin_signed, w["ret_norm_gain"], o_ret, states, dmix_at=(1, 0)),
         _mix_proj_bwd(dh1, dh1_b, w_out_full, w_in_blocks, x2, w["norm1_gain"], dproj_part=0),
         _lru_bwd(proj, xc, h_lru, lru_cw, wa, w["lru_gate_a_b"], wx, w["lru_gate_x_b"], w["lru_lambda"],
                  w["lru_norm_gain"], dproj_part=0, dmix_at=(1, 0))],
        "mix_bwd")
    g["norm1_gain"] = dg1[:1]
    g["ret_norm_gain"] = dgain_ret[:1]
    lru_acc = lru_acc[::SUBLANES]
    g_lru_cw = lru_acc[:LRU_CONV]
    for name in ("conv_b", "gate_a_b", "gate_x_b", "lambda", "norm_gain"):
        g["lru_" + name] = lru_acc[LRU_ACC[name]:LRU_ACC[name] + 1]
    g["lru_gate_a_w"] = _diag_blocks(dwa, LRU_HEADS)
    g["lru_gate_x_w"] = _diag_blocks(dwx, LRU_HEADS)
    rep_packed, rep_layout = _pack([g[n] for n in REPLICATED], SUBLANES)
    g_in, (up_parts, out_parts, rep_part) = _mm_tn(u1, dproj, "w_in_wgrad", blocks=N_DEV,
                                                   comm=_both(_chip_exchange(mid_sums), _gather_first([rep_packed])))
    adamw("ffn_up_w", up_parts)
    adamw("w_out", out_parts)
    g_conv = _conv_rows(_column_blocks(g_lru_cw), _column_blocks(g_ffn_cw), GRAD_DTYPE, 2 * SUBLANES)
    in_sums = to_owner_chips([g_in, g_conv], ["w_in", "conv"], "in")
    in_parts, conv_parts, rep_parts = _run_comms([_both(_chip_exchange(in_sums), _gather_second([rep_part]))],
                                                 "last_grads_exchange")
    adamw("w_in", in_parts)
    pad16 = lambda d: _conv_rows(d["lru_conv_w"], d["ffn_conv_w"], F32, 2 * SUBLANES)
    conv_res = _sum_adamw(conv_parts, pad16(w), pad16(mom_m), pad16(mom_v), "adamw_conv")
    for n, lo, hi in (("lru_conv_w", 0, n_lru), ("ffn_conv_w", n_lru, n_lru + w["ffn_conv_w"].size)):
        res[n] = [r.reshape(-1)[lo:hi].reshape(w[n].shape) for r in conv_res]
    rep_res = _sum_adamw(rep_parts, *[_pack([d[n] for n in REPLICATED], SUBLANES)[0] for d in (w, mom_m, mom_v)],
                         "adamw_replicated")
    for k in range(4):
        for n, a in zip(REPLICATED, _unpack(rep_res[k], rep_layout)):
            res.setdefault(n, [None] * 4)[k] = a

    outs = [loss, grad_x[None]]
    for k in range(4):
        outs += [res[n][k].reshape(out_shape[n]) for n in WEIGHTS]
    return tuple(outs)
```

```python
import math

import numpy as np
import jax
import jax.numpy as jnp
from jax import lax
from jax.experimental import pallas as pl
from jax.experimental.pallas import tpu as pltpu

F32 = jnp.float32
BF16 = jnp.bfloat16
MXU_DTYPE = jnp.bfloat16
GRAD_DTYPE = jnp.bfloat16

N_DEV = 8
N_CHIPS = 4
D_MODEL = 1024
D_LRU = 512
LRU_HEADS = 8
LRU_CONV = 4
LRU_C = 8.0
D_RET = 512
RET_HEADS = 4
RET_HEAD_DIM = 128
RET_CHUNK = 128
ROPE_BASE = 10000.0
D_IN = 3072
D_FF = 3072
FFN_CONV = 3
NORM_EPS = 1e-6

ADAM_LR = 0.001
ADAM_B1 = 0.9
ADAM_B2 = 0.999
ADAM_EPS = 1e-08
ADAM_WD = 0.01
ADAM_STEP = 10

SUBLANES = 8
LANES = 128
VMEM_LIMIT = 48 * 1024 * 1024

MESH = pl.DeviceIdType.MESH
ANY = pl.BlockSpec(memory_space=pl.ANY)


def _dot(a, b):
    return jnp.dot(a.astype(MXU_DTYPE), b.astype(MXU_DTYPE), preferred_element_type=F32)


def _dot_nt(a, b):
    return lax.dot_general(a.astype(MXU_DTYPE), b.astype(MXU_DTYPE), (((1,), (1,)), ((), ())),
                           preferred_element_type=F32)


def _dot_tn(a, b):
    return lax.dot_general(a.astype(MXU_DTYPE), b.astype(MXU_DTYPE), (((0,), (0,)), ((), ())),
                           preferred_element_type=F32)


def _sigmoid(x):
    return 0.5 + 0.5 * jnp.tanh(0.5 * x)


_GELU_C = math.sqrt(2.0 / math.pi)
_GELU_C3 = _GELU_C * 0.044715


def _gelu_parts(x):
    x2 = x * x
    t = jnp.tanh(x * (_GELU_C + _GELU_C3 * x2))
    cdf = 0.5 + 0.5 * t
    g = x * cdf
    dg = cdf + (0.5 * x) * (1.0 - t * t) * (_GELU_C + (3.0 * _GELU_C3) * x2)
    return g, dg


def _gelu(x):
    t = jnp.tanh(_GELU_C * (x + 0.044715 * (x * x * x)))
    return x * (0.5 * (1.0 + t))


def _softplus(x):
    return jnp.maximum(x, 0.0) + jnp.log1p(jnp.exp(-jnp.abs(x)))


def _bcast_row(x, r, rows=SUBLANES):
    return jnp.broadcast_to(x[r:r + 1, :], (rows, x.shape[1]))


def _colsum8(x):
    return jnp.broadcast_to(jnp.sum(x, axis=0, keepdims=True), (SUBLANES, x.shape[1]))


def _shift_down(prev8, tile, s):
    if s == 0:
        return tile
    ext = jnp.concatenate([prev8, tile], axis=0)
    return pltpu.roll(ext, s, 0)[SUBLANES:, :]


def _shift_up(tile, next8, s):
    if s == 0:
        return tile
    ext = jnp.concatenate([tile, next8], axis=0)
    return pltpu.roll(ext, SUBLANES - s, 0)[SUBLANES:, :]


def _group_scan(a, b, reverse, fill=lambda: None):
    n = a.shape[0]
    row = lax.broadcasted_iota(jnp.int32, a.shape, 0) & (SUBLANES - 1)
    for s in (1, 2, 4):
        if s > 1:
            fill()
        shift = (n - s) if reverse else s
        a_sh = pltpu.roll(a, shift, 0)
        b_sh = pltpu.roll(b, shift, 0)
        m = (row <= SUBLANES - 1 - s) if reverse else (row >= s)
        b = jnp.where(m, a * b_sh + b, b)
        a = jnp.where(m, a * a_sh, a)
    return a, b


def _carry_scan(a_ref, b_ref, out_ref, carry0, reverse):
    n_groups = a_ref.shape[0] // SUBLANES
    carry = carry0
    for i in range(n_groups):
        r0 = ((n_groups - 1 - i) if reverse else i) * SUBLANES
        hg = a_ref[r0:r0 + SUBLANES, :] * carry + b_ref[r0:r0 + SUBLANES, :]
        out_ref[r0:r0 + SUBLANES, :] = hg
        carry = _bcast_row(hg, 0 if reverse else SUBLANES - 1)
    return carry


def _rms_fwd(h, gain):
    rstd = lax.rsqrt(jnp.mean(h * h, axis=-1, keepdims=True) + NORM_EPS)
    n = h * rstd
    return n, rstd, n * gain


def _rms_bwd(dy, n, rstd, gain):
    dn = dy * gain
    dh = rstd * (dn - n * jnp.mean(dn * n, axis=-1, keepdims=True))
    return dh, _colsum8(dy * n)


def _halo_rows(dtype):
    return SUBLANES * (4 // jnp.dtype(dtype).itemsize)


def _halo_map(tile_rows, col, halo_rows=SUBLANES):
    per = tile_rows // halo_rows
    return lambda i: (jnp.maximum(i * per - 1, 0), col)


def _resident(shape):
    return pl.BlockSpec(shape, lambda *_: (0,) * len(shape), pipeline_mode=pl.Buffered(1))


def _place():
    x, y, c = lax.axis_index("x"), lax.axis_index("y"), lax.axis_index("c")
    chips = [(1 - x, y), (x, 1 - y), (1 - x, 1 - y)]
    return x, y, c, chips


def _dev(x, y, c):
    return 4 * x + 2 * y + c


class _Copy:
    def __init__(self, make):
        self.make = make

    def start(self):
        self.make().start()

    def wait(self):
        self.make().wait()

    def wait_send(self):
        self.make().wait_send()

    def wait_recv(self):
        self.make().wait_recv()


def _remote(src, dst, send_sem, recv_sem, to):
    return _Copy(lambda: pltpu.make_async_remote_copy(src_ref=src, dst_ref=dst, send_sem=send_sem, recv_sem=recv_sem,
                                                      device_id=to, device_id_type=MESH))


def _local(src, dst, sem):
    return _Copy(lambda: pltpu.make_async_copy(src, dst, sem))


class _Comm:
    def __init__(self, operands, out_shape, sems, descs, aliases=()):
        self.operands, self.out_shape, self.sems, self.descs, self.aliases = operands, out_shape, sems, descs, aliases

    def start(self, ins, outs, sems):
        local, sends, _ = self.descs(ins, outs, sems)
        for cp in sends + local:
            cp.start()

    def wait(self, ins, outs, sems):
        local, sends, recvs = self.descs(ins, outs, sems)
        for cp in recvs:
            cp.wait_recv()
        for cp in sends:
            cp.wait_send()
        for cp in local:
            cp.wait()


def _gather_first(shards):
    n = len(shards)

    def descs(ins, outs, sems):
        send, recv, loc = sems
        x, y, c, chips = _place()
        me = _dev(x, y, c)
        targets = [(x, y, 1 - c)] + [(*chip, c) for chip in chips]
        local, sends, recvs = [], [], []
        for t in range(n):
            local.append(_local(ins[t], outs[t].at[me], loc.at[t]))
            for k, to in enumerate(targets):
                i = 4 * t + k
                sends.append(_remote(ins[t], outs[t].at[me], send.at[i], recv.at[i], to))
                recvs.append(_remote(ins[t], outs[t].at[_dev(*to)], send.at[i], recv.at[i], to))
        return local, sends, recvs

    return _Comm(list(shards), [jax.ShapeDtypeStruct((N_DEV,) + s.shape, s.dtype) for s in shards],
                 [pltpu.SemaphoreType.DMA((4 * n,)), pltpu.SemaphoreType.DMA((4 * n,)), pltpu.SemaphoreType.DMA((n,))],
                 descs)


def _gather_second(gathered):
    n = len(gathered)

    def descs(ins, outs, sems):
        send, recv = sems
        x, y, c, chips = _place()
        sends, recvs = [], []
        for t in range(n):
            for j, chip in enumerate(chips):
                i = 3 * t + j
                have, get = _dev(*chip, c), _dev(*chip, 1 - c)
                sends.append(_remote(outs[t].at[have], outs[t].at[have], send.at[i], recv.at[i], (x, y, 1 - c)))
                recvs.append(_remote(outs[t].at[have], outs[t].at[get], send.at[i], recv.at[i], (x, y, 1 - c)))
        return [], sends, recvs

    return _Comm(list(gathered), [jax.ShapeDtypeStruct(g.shape, g.dtype) for g in gathered],
                 [pltpu.SemaphoreType.DMA((3 * n,)), pltpu.SemaphoreType.DMA((3 * n,))], descs,
                 aliases=[(t, t) for t in range(n)])


def _pair_exchange(blocks):
    n = len(blocks)

    def descs(ins, outs, sems):
        send, recv = sems
        x, y, c, _ = _place()
        sends, recvs = [], []
        for t in range(n):
            for q in range(N_CHIPS):
                i = N_CHIPS * t + q
                cp = _remote(ins[t].at[2 * q + 1 - c], outs[t].at[q], send.at[i], recv.at[i], (x, y, 1 - c))
                sends.append(cp)
                recvs.append(cp)
        return [], sends, recvs

    return _Comm(list(blocks), [jax.ShapeDtypeStruct((N_CHIPS,) + b.shape[1:], b.dtype) for b in blocks],
                 [pltpu.SemaphoreType.DMA((N_CHIPS * n,)), pltpu.SemaphoreType.DMA((N_CHIPS * n,))], descs)


def _chip_exchange(blocks):
    n = len(blocks)

    def descs(ins, outs, sems):
        send, recv, loc = sems
        x, y, c, chips = _place()
        me = 2 * x + y
        local, sends, recvs = [], [], []
        for t in range(n):
            local.append(_local(ins[t].at[me], outs[t].at[me], loc.at[t]))
            for j, (px, py) in enumerate(chips):
                i = 3 * t + j
                q = 2 * px + py
                sends.append(_remote(ins[t].at[q], outs[t].at[me], send.at[i], recv.at[i], (px, py, c)))
                recvs.append(_remote(ins[t].at[q], outs[t].at[q], send.at[i], recv.at[i], (px, py, c)))
        return local, sends, recvs

    return _Comm(list(blocks), [jax.ShapeDtypeStruct(b.shape, b.dtype) for b in blocks],
                 [pltpu.SemaphoreType.DMA((3 * n,)), pltpu.SemaphoreType.DMA((3 * n,)), pltpu.SemaphoreType.DMA((n,))],
                 descs)


def _both(a, b):
    na, oa, sa = len(a.operands), len(a.out_shape), len(a.sems)

    def descs(ins, outs, sems):
        local_a, sends_a, recvs_a = a.descs(ins[:na], outs[:oa], sems[:sa])
        local_b, sends_b, recvs_b = b.descs(ins[na:], outs[oa:], sems[sa:])
        return local_a + local_b, sends_a + sends_b, recvs_a + recvs_b

    return _Comm(a.operands + b.operands, a.out_shape + b.out_shape, a.sems + b.sems, descs,
                 aliases=list(a.aliases) + [(na + i, oa + o) for i, o in b.aliases])


def _run_comms(comms, name):
    first = comms[0]
    n_in, n_out = len(first.operands), len(first.out_shape)

    def body(*refs):
        ins, outs, sems = refs[:n_in], refs[n_in:n_in + n_out], list(refs[n_in + n_out:])
        for k, comm in enumerate(comms):
            mine = [sems.pop(0) for _ in comm.sems]
            comm.start(ins if k == 0 else outs, outs, mine)
            comm.wait(ins if k == 0 else outs, outs, mine)

    outs = pl.pallas_call(
        body, name=name, out_shape=first.out_shape, in_specs=[ANY] * n_in, out_specs=[ANY] * n_out,
        scratch_shapes=[s for comm in comms for s in comm.sems], input_output_aliases=dict(first.aliases),
    )(*first.operands)
    return list(outs)


def _call(body, *, name, grid, in_specs, out_specs, out_shape, operands, scratch_shapes=(), comm=None, aliases=None):
    sem = ("arbitrary",) * len(grid)
    params = pltpu.CompilerParams(dimension_semantics=sem, vmem_limit_bytes=VMEM_LIMIT)
    aliases = dict(aliases or {})
    if comm is None:
        return pl.pallas_call(body, name=name, grid=grid, in_specs=in_specs, out_specs=out_specs, out_shape=out_shape,
                              scratch_shapes=list(scratch_shapes), input_output_aliases=aliases,
                              compiler_params=params)(*operands)
    n_in, n_out, n_scr = len(in_specs), len(out_specs), len(scratch_shapes)
    c_in, c_out = len(comm.operands), len(comm.out_shape)

    def wrapped(*refs):
        refs = list(refs)
        ins, refs = refs[:n_in], refs[n_in:]
        cins, refs = refs[:c_in], refs[c_in:]
        outs, refs = refs[:n_out], refs[n_out:]
        couts, refs = refs[:c_out], refs[c_out:]
        scr, csems = refs[:n_scr], refs[n_scr:]
        first = last = None
        for axis, size in enumerate(grid):
            at_first, at_last = pl.program_id(axis) == 0, pl.program_id(axis) == size - 1
            first = at_first if first is None else first & at_first
            last = at_last if last is None else last & at_last

        @pl.when(first)
        def _():
            comm.start(cins, couts, csems)

        body(*ins, *outs, *scr)

        @pl.when(last)
        def _():
            comm.wait(cins, couts, csems)

    res = pl.pallas_call(
        wrapped, name=name, grid=grid, in_specs=list(in_specs) + [ANY] * c_in, out_specs=list(out_specs) + [ANY] * c_out,
        out_shape=list(out_shape) + list(comm.out_shape), scratch_shapes=list(scratch_shapes) + list(comm.sems),
        input_output_aliases={**aliases, **{n_in + i: n_out + o for i, o in comm.aliases}}, compiler_params=params,
    )(*operands, *comm.operands)
    return list(res[:n_out]), list(res[n_out:])


def _mm_tn(a, b, name, blocks=1, tk=2048, comm=None):
    t, m = a.shape
    n = b.shape[1]
    tk = min(tk, t)
    nk = t // tk
    cb = n // blocks
    per = max(1, 768 // cb) if blocks > 1 else 1
    tn = per * cb if blocks > 1 else min(1024, n)
    tm = min(1024, m)
    assert blocks == 1 or tm == m

    def body(a_ref, b_ref, o_ref, acc):
        k = pl.program_id(2)

        @pl.when(k == 0)
        def _():
            acc[...] = jnp.zeros_like(acc)
        acc[...] += _dot_tn(a_ref[...], b_ref[...])

        @pl.when(k == nk - 1)
        def _():
            if blocks == 1:
                o_ref[...] = acc[...].astype(o_ref.dtype)
            else:
                for s in range(per):
                    o_ref[s] = acc[:, s * cb:(s + 1) * cb].astype(o_ref.dtype)

    if blocks == 1:
        out_spec = pl.BlockSpec((tm, tn), lambda i, j, k: (i, j))
        out_shape = jax.ShapeDtypeStruct((m, n), GRAD_DTYPE)
    else:
        out_spec = pl.BlockSpec((per, m, cb), lambda i, j, k: (j, 0, 0))
        out_shape = jax.ShapeDtypeStruct((blocks, m, cb), GRAD_DTYPE)
    res = _call(body, name=name, grid=(m // tm, n // tn, nk), comm=comm,
                in_specs=[pl.BlockSpec((tk, tm), lambda i, j, k: (k, i)), pl.BlockSpec((tk, tn), lambda i, j, k: (k, j))],
                out_specs=[out_spec], out_shape=[out_shape], operands=[a, b],
                scratch_shapes=[pltpu.VMEM((tm, tn), F32)])
    return res[0] if comm is None else (res[0][0], res[1])


def _inproj_fwd(x, g1, w_blocks, comm):
    t = x.shape[0]
    tm = min(512, t)
    nb, _, cb = w_blocks.shape

    def body(x_ref, g_ref, w_ref, u_ref, p_ref):
        _, _, u = _rms_fwd(x_ref[...], g_ref[...])
        u = u.astype(MXU_DTYPE)
        u_ref[...] = u
        for d in range(nb):
            p_ref[:, d * cb:(d + 1) * cb] = _dot(u, w_ref[d]).astype(p_ref.dtype)

    return _call(body, name="inproj_fwd", grid=(t // tm,), comm=comm,
                 in_specs=[pl.BlockSpec((tm, D_MODEL), lambda i: (i, 0)), pl.BlockSpec((1, D_MODEL), lambda i: (0, 0)),
                           _resident(w_blocks.shape)],
                 out_specs=[pl.BlockSpec((tm, D_MODEL), lambda i: (i, 0)), pl.BlockSpec((tm, D_IN), lambda i: (i, 0))],
                 out_shape=[jax.ShapeDtypeStruct((t, D_MODEL), MXU_DTYPE), jax.ShapeDtypeStruct((t, D_IN), MXU_DTYPE)],
                 operands=[x, g1, w_blocks])


def _lru_gates(xc, wa, ba, wx, bx, sp, fill=lambda: None):
    r = _sigmoid(_dot(xc, wa) + ba)
    fill()
    ig = _sigmoid(_dot(xc, wx) + bx)
    fill()
    log_a = (-LRU_C) * r * sp
    a = jnp.exp(log_a)
    m = jnp.sqrt(-jnp.tanh(log_a) * (a * a + 1.0))
    return r, ig, a, m


def _fused(parts, name, comm=None):
    grid = parts[0]["grid"]
    assert all(p["grid"] == grid for p in parts)
    counts = [(len(p["in_specs"]), len(p["out_specs"]), len(p.get("scratch_shapes", ()))) for p in parts]

    def body(*refs):
        refs = list(refs)
        groups = []
        for kind in range(3):
            taken = []
            for c in counts:
                taken.append(refs[:c[kind]])
                refs = refs[c[kind]:]
            groups.append(taken)
        ins, outs, scr = groups
        pending = []

        def fill(n=None):
            for _ in range(share if n is None else n):
                if pending:
                    pending.pop(0)()

        ctx = dict(outs=outs, scratch=scr, fill=fill)
        run = lambda key: [p[key](*ins[k], *outs[k], *scr[k], ctx) for k, p in enumerate(parts) if key in p]
        run("head")
        for pieces in run("units"):
            pending.extend(pieces)
        points = sum(p.get("fill_points", 0) for p in parts)
        share = -(-len(pending) // max(points, 1))
        run("body")
        fill(len(pending))
        run("tail")

    cat = lambda key: [x for p in parts for x in p.get(key, ())]
    res = _call(body, name=name, grid=grid, comm=comm, in_specs=cat("in_specs"), out_specs=cat("out_specs"),
                out_shape=cat("out_shape"), scratch_shapes=cat("scratch_shapes"), operands=cat("operands"))
    outs, side = (res if comm is not None else (res, None))
    split, at = [], 0
    for _, n_out, _ in counts:
        split.append(list(outs[at:at + n_out]))
        at += n_out
    return split if comm is None else (split, side)


def _lru_fwd(proj, conv_w, conv_b, wa, ba, wx, bx, lam, gain):
    t = proj.shape[0]
    tm = min(256, t)
    c = D_LRU

    def body(x_ref, xh_ref, g_ref, cw_ref, cb_ref, wa_ref, ba_ref, wx_ref, bx_ref, lam_ref, gain_ref,
             xc_ref, h_ref, y_ref, a_scr, b_scr, carry, ctx):
        fill = ctx["fill"]
        i = pl.program_id(0)

        @pl.when(i == 0)
        def _():
            carry[...] = jnp.zeros_like(carry)

        fill()
        x = x_ref[...].astype(F32)
        prev = jnp.where(i == 0, 0.0, xh_ref[...].astype(F32)[-SUBLANES:, :])
        cw = cw_ref[...]
        xc = cb_ref[...] + cw[LRU_CONV - 1:LRU_CONV, :] * x
        for k in range(LRU_CONV - 1):
            xc = xc + cw[k:k + 1, :] * _shift_down(prev, x, LRU_CONV - 1 - k)
        xc_ref[...] = xc
        fill()
        sp = _softplus(-lam_ref[...])
        _, ig, a, m = _lru_gates(xc, wa_ref[...], ba_ref[...], wx_ref[...], bx_ref[...], sp, fill)
        fill()
        ga, gb = _group_scan(a, m * (ig * xc), reverse=False, fill=fill)
        a_scr[...] = ga
        b_scr[...] = gb
        fill()
        carry[...] = _carry_scan(a_scr, b_scr, h_ref, carry[...], reverse=False)
        fill()
        z = h_ref[...] * _gelu(g_ref[...].astype(F32))
        fill()
        _, _, y = _rms_fwd(z, gain_ref[...])
        y_ref[...] = y.astype(y_ref.dtype)

    row = lambda i: (i, 0)
    full = lambda i: (0, 0)
    vec = pl.BlockSpec((1, c), full)
    hb = _halo_rows(proj.dtype)
    return dict(body=body, grid=(t // tm,), fill_points=10,
                in_specs=[pl.BlockSpec((tm, c), row), pl.BlockSpec((hb, c), _halo_map(tm, 0, hb)),
                          pl.BlockSpec((tm, c), lambda i: (i, 1)),
                          pl.BlockSpec((LRU_CONV, c), full), vec, pl.BlockSpec((c, c), full), vec,
                          pl.BlockSpec((c, c), full), vec, vec, vec],
                out_specs=[pl.BlockSpec((tm, c), row), pl.BlockSpec((tm, c), row), pl.BlockSpec((tm, c), row)],
                out_shape=[jax.ShapeDtypeStruct((t, c), F32), jax.ShapeDtypeStruct((t, c), F32),
                           jax.ShapeDtypeStruct((t, c), MXU_DTYPE)],
                scratch_shapes=[pltpu.VMEM((tm, c), F32), pltpu.VMEM((tm, c), F32), pltpu.VMEM((SUBLANES, c), F32)],
                operands=[proj, proj, proj, conv_w, conv_b, wa, ba, wx, bx, lam, gain])


def _ret_consts():
    c = RET_CHUNK
    log_g = jnp.log1p(-jnp.exp2(-5.0 - jnp.arange(RET_HEADS, dtype=F32)))
    idx = jnp.arange(c, dtype=F32)
    diff = idx[:, None] - idx[None, :]
    decay = jnp.where(diff[None] >= 0, jnp.exp(jnp.maximum(diff, 0.0)[None] * log_g[:, None, None]), 0.0)
    zeta = jnp.exp((c - 1 - idx)[None, :] * log_g[:, None])
    xi = jnp.exp((idx + 1.0)[None, :] * log_g[:, None])
    spread = lambda v: jnp.repeat(v.T, RET_HEAD_DIM, axis=1)
    log_g_np = np.log1p(-np.exp2(-5.0 - np.arange(RET_HEADS, dtype=np.float32))).astype(np.float32)
    g_chunk = [float(np.exp(np.float32(c) * lg)) for lg in log_g_np]
    return decay, spread(xi), spread(zeta), g_chunk


def _rope_tables(t):
    pos = np.arange(t, dtype=np.float32)
    inv_freq = np.float32(ROPE_BASE) ** (-np.arange(0, RET_HEAD_DIM, 2, dtype=np.float32) / np.float32(RET_HEAD_DIM))
    ang = (pos[:, None] * inv_freq.astype(np.float32)[None, :]).astype(np.float32).astype(np.float64)
    cos, sin = np.cos(ang).astype(np.float32), np.sin(ang).astype(np.float32)
    return jnp.asarray(np.concatenate([cos, cos], axis=-1)), jnp.asarray(np.concatenate([-sin, sin], axis=-1))


def _rope(x, cos2, sin_signed):
    return x * cos2 + pltpu.roll(x, RET_HEAD_DIM // 2, 1) * sin_signed


def _rope_bwd(d, cos2, sin_signed):
    return d * cos2 + pltpu.roll(d * sin_signed, RET_HEAD_DIM // 2, 1)


RET_SCALE = RET_HEAD_DIM ** -0.5


RET_CHUNKS_PER_STEP = 2


def _ret_fwd(proj, cos2, sin_signed, gain):
    t = proj.shape[0]
    c, d, nh = RET_CHUNK, RET_HEAD_DIM, RET_HEADS
    n_chunks = t // c
    per = RET_CHUNKS_PER_STEP if n_chunks % RET_CHUNKS_PER_STEP == 0 else 1
    rows = per * c
    decay, xi, zeta, g_chunk = _ret_consts()

    def units(qk_ref, vg_ref, cos_ref, sin_ref, dec_ref, xi_ref, zeta_ref, gain_ref, o_ref, y_ref, st_ref, state, ctx):
        cur = [None] * nh

        def start():
            @pl.when(pl.program_id(0) == 0)
            def _():
                state[...] = jnp.zeros_like(state)
            for h in range(nh):
                cur[h] = state[h]

        def retain(s, h, keep):
            rs = slice(s * c, (s + 1) * c)
            cos2, sin_s = cos_ref[rs, :], sin_ref[rs, :]
            lo = h * d
            q = _rope(qk_ref[rs, lo:lo + d].astype(F32), cos2, sin_s)
            k = _rope(qk_ref[rs, D_RET + lo:D_RET + lo + d].astype(F32), cos2, sin_s) * RET_SCALE
            v = vg_ref[rs, lo:lo + d]
            s_prev = cur[h]
            st_ref[s, h] = s_prev
            scores = _dot_nt(q, k) * dec_ref[h]
            o = _dot(scores, v) + _dot(q * xi_ref[:, lo:lo + d], s_prev)
            cur[h] = s_prev * g_chunk[h] + _dot_tn(k * zeta_ref[:, lo:lo + d], v)
            o_ref[rs, lo:lo + d] = o
            keep["o"] = o

        def normalise(s, h, keep):
            rs = slice(s * c, (s + 1) * c)
            lo = h * d
            o = keep["o"]
            g = vg_ref[rs, D_RET + lo:D_RET + lo + d].astype(F32)
            mu = jnp.mean(o, axis=-1, keepdims=True)
            oc = o - mu
            on = oc * lax.rsqrt(jnp.mean(oc * oc, axis=-1, keepdims=True) + NORM_EPS)
            y_ref[rs, lo:lo + d] = (on * gain_ref[:, lo:lo + d] * (g * _sigmoid(g))).astype(y_ref.dtype)

        def end():
            for h in range(nh):
                state[h] = cur[h]

        pieces = [start]
        for s in range(per):
            for h in range(nh):
                keep = {}
                pieces += [lambda s=s, h=h, keep=keep: retain(s, h, keep),
                           lambda s=s, h=h, keep=keep: normalise(s, h, keep)]
        return pieces + [end]

    full2 = lambda i: (0, 0)
    return dict(units=units, grid=(n_chunks // per,),
                in_specs=[pl.BlockSpec((rows, 2 * D_RET), lambda i: (i, 1)),
                          pl.BlockSpec((rows, 2 * D_RET), lambda i: (i, 2)),
                          pl.BlockSpec((rows, d), lambda i: (i, 0)), pl.BlockSpec((rows, d), lambda i: (i, 0)),
                          pl.BlockSpec((nh, c, c), lambda i: (0, 0, 0)), pl.BlockSpec((c, D_RET), full2),
                          pl.BlockSpec((c, D_RET), full2), pl.BlockSpec((1, D_RET), full2)],
                out_specs=[pl.BlockSpec((rows, D_RET), lambda i: (i, 0)), pl.BlockSpec((rows, D_RET), lambda i: (i, 0)),
                           pl.BlockSpec((per, nh, d, d), lambda i: (i, 0, 0, 0))],
                out_shape=[jax.ShapeDtypeStruct((t, D_RET), F32), jax.ShapeDtypeStruct((t, D_RET), MXU_DTYPE),
                           jax.ShapeDtypeStruct((n_chunks, nh, d, d), F32)],
                scratch_shapes=[pltpu.VMEM((nh, d, d), F32)],
                operands=[proj, proj, cos2, sin_signed, decay, xi, zeta, gain])


def _outproj_fwd(x, y_lru, y_ret, w_out, g2, comm):
    t = x.shape[0]
    tm = min(512, t)

    def body(x_ref, yl_ref, yr_ref, w_ref, g_ref, h1_ref, u2_ref):
        h1 = x_ref[...] + _dot(yl_ref[...], w_ref[:D_LRU, :]) + _dot(yr_ref[...], w_ref[D_LRU:, :])
        h1_ref[...] = h1
        _, _, u = _rms_fwd(h1, g_ref[...])
        u2_ref[...] = u.astype(u2_ref.dtype)

    row = lambda i: (i, 0)
    return _call(body, name="outproj_fwd", grid=(t // tm,), comm=comm,
                 in_specs=[pl.BlockSpec((tm, D_MODEL), row), pl.BlockSpec((tm, D_LRU), row), pl.BlockSpec((tm, D_RET), row),
                           _resident((D_MODEL, D_MODEL)), pl.BlockSpec((1, D_MODEL), lambda i: (0, 0))],
                 out_specs=[pl.BlockSpec((tm, D_MODEL), row), pl.BlockSpec((tm, D_MODEL), row)],
                 out_shape=[jax.ShapeDtypeStruct((t, D_MODEL), F32), jax.ShapeDtypeStruct((t, D_MODEL), MXU_DTYPE)],
                 operands=[x, y_lru, y_ret, w_out, g2])


FFN_TN = 768
FFN_NJ = D_FF // FFN_TN
FFN_GROUP = 4


def _ffn_fwd(u2, w_blocks, conv_w, conv_b, w_down, h1, gf, target):
    t = u2.shape[0]
    tm = min(256, t)
    tn, nj, group = FFN_TN, FFN_NJ, FFN_GROUP
    ng, tw = nj // group, group * tn
    hb = _halo_rows(u2.dtype)
    assert w_blocks.shape == (2 * nj, D_MODEL, tn)

    def project(u_ext, w, col, up_ref, conv_ref, cw_ref, cb_ref, first):
        ext = _dot(u_ext, w)
        x = ext[hb:, :]
        up_ref[:, col] = x.astype(up_ref.dtype)
        prev = jnp.where(first, 0.0, ext[hb - SUBLANES:hb, :])
        cw = cw_ref[:, col]
        y = cb_ref[:, col] + cw[FFN_CONV - 1:FFN_CONV, :] * x
        for k in range(FFN_CONV - 1):
            y = y + cw[k:k + 1, :] * _shift_down(prev, x, FFN_CONV - 1 - k)
        conv_ref[:, col] = y.astype(conv_ref.dtype)
        return y

    def body(u_ref, uh_ref, w_ref, cwa_ref, cwv_ref, cba_ref, cbv_ref, wd_ref, h1_ref, gf_ref, tg_ref,
             upa_ref, upv_ref, ca_ref, cv_ref, act_ref, dh_ref, dhb_ref, dgf_ref, loss_ref, acc):
        i, jg = pl.program_id(0), pl.program_id(1)

        @pl.when((i == 0) & (jg == 0))
        def _():
            dgf_ref[...] = jnp.zeros_like(dgf_ref)
            loss_ref[...] = jnp.zeros_like(loss_ref)

        @pl.when(jg == 0)
        def _():
            acc[...] = jnp.zeros_like(acc)

        u_ext = jnp.concatenate([uh_ref[...], u_ref[...]], axis=0)
        down = None
        for jj in range(group):
            col = slice(jj * tn, (jj + 1) * tn)
            j = jg * group + jj
            a = project(u_ext, w_ref[j], col, upa_ref, ca_ref, cwa_ref, cba_ref, i == 0)
            v = project(u_ext, w_ref[nj + j], col, upv_ref, cv_ref, cwv_ref, cbv_ref, i == 0)
            act = (_gelu(a) * v).astype(act_ref.dtype)
            act_ref[:, col] = act
            part = _dot(act, wd_ref[pl.ds(pl.multiple_of(j * tn, tn), tn), :])
            down = part if down is None else down + part
        acc[...] += down

        @pl.when(jg == ng - 1)
        def _():
            n, rstd, y = _rms_fwd(h1_ref[...] + acc[...], gf_ref[...])
            err = y - tg_ref[...]
            loss_ref[...] += (0.5 / D_MODEL) * jnp.sum(err * err)
            dh, dgf = _rms_bwd(err * (1.0 / D_MODEL), n, rstd, gf_ref[...])
            dgf_ref[...] += dgf
            dh_ref[...] = dh
            dhb_ref[...] = dh.astype(dhb_ref.dtype)

    per = tm // hb
    row = lambda i, j: (i, 0)
    const = lambda i, j: (0, 0)
    tile = pl.BlockSpec((tm, tw), lambda i, j: (i, j))
    return _call(body, name="ffn_fwd", grid=(t // tm, ng),
                 in_specs=[pl.BlockSpec((tm, D_MODEL), row),
                           pl.BlockSpec((hb, D_MODEL), lambda i, j: (jnp.maximum(i * per - 1, 0), 0)),
                           _resident(w_blocks.shape),
                           pl.BlockSpec((FFN_CONV, tw), lambda i, j: (0, j)),
                           pl.BlockSpec((FFN_CONV, tw), lambda i, j: (0, j + ng)),
                           pl.BlockSpec((1, tw), lambda i, j: (0, j)), pl.BlockSpec((1, tw), lambda i, j: (0, j + ng)),
                           _resident((D_FF, D_MODEL)),
                           pl.BlockSpec((tm, D_MODEL), row), pl.BlockSpec((1, D_MODEL), const),
                           pl.BlockSpec((tm, D_MODEL), row)],
                 out_specs=[tile] * 5 + [pl.BlockSpec((tm, D_MODEL), row),
                            pl.BlockSpec((tm, D_MODEL), row), pl.BlockSpec((SUBLANES, D_MODEL), const),
                            pl.BlockSpec((SUBLANES, LANES), const)],
                 out_shape=[jax.ShapeDtypeStruct((t, D_FF), MXU_DTYPE)] * 5 + [
                            jax.ShapeDtypeStruct((t, D_MODEL), F32),
                            jax.ShapeDtypeStruct((t, D_MODEL), MXU_DTYPE), jax.ShapeDtypeStruct((SUBLANES, D_MODEL), F32),
                            jax.ShapeDtypeStruct((SUBLANES, LANES), F32)],
                 scratch_shapes=[pltpu.VMEM((tm, D_MODEL), F32)],
                 operands=[u2, u2, w_blocks, conv_w, conv_w, conv_b, conv_b, w_down, h1, gf, target])


FFN_ACC_ROWS = SUBLANES * (FFN_CONV + 1)


def _ffn_bwd(dh2, dh2_b, w_down, up_a, up_v, conv_a, conv_v, conv_w, w_up_blocks, h1, g2, comm):
    t = up_a.shape[0]
    tm = min(256, t)
    tn, nj, group = FFN_TN, FFN_NJ, FFN_GROUP
    ng, tw = nj // group, group * tn
    ni = t // tm
    assert w_up_blocks.shape == (2 * nj, D_MODEL, tn)

    def conv_bwd(dy, x, cw, acc_ref, carry_ref, dup_ref, col):
        nxt = carry_ref[...]
        carry_ref[...] = dy[:SUBLANES, :]
        ahead = [_shift_up(dy, nxt, FFN_CONV - 1 - k) for k in range(FFN_CONV)]
        dx = cw[FFN_CONV - 1:FFN_CONV, :] * dy
        for k in range(FFN_CONV - 1):
            dx = dx + cw[k:k + 1, :] * ahead[k]
        dx = dx.astype(dup_ref.dtype)
        dup_ref[:, col] = dx
        for k in range(FFN_CONV):
            acc_ref[k * SUBLANES:(k + 1) * SUBLANES, :] += _colsum8(ahead[k] * x)
        acc_ref[FFN_CONV * SUBLANES:, :] += _colsum8(dy)
        return dx

    def body(dh_ref, dhb_ref, wd_ref, ua_ref, uv_ref, ca_ref, cv_ref, cwa_ref, cwv_ref, wu_ref, h1_ref, g2_ref,
             dua_ref, duv_ref, acca_ref, accv_ref, dh1_ref, dh1b_ref, dg2_ref, carry_a, carry_v, du):
        i, jg = pl.program_id(0), pl.program_id(1)

        @pl.when((i == 0) & (jg == 0))
        def _():
            for ref in (acca_ref, accv_ref, carry_a, carry_v, dg2_ref):
                ref[...] = jnp.zeros_like(ref)

        dhb = dhb_ref[...]
        part = None
        for jj in range(group):
            col = slice(jj * tn, (jj + 1) * tn)
            j = jg * group + jj
            v = cv_ref[:, col].astype(F32)
            g, dg = _gelu_parts(ca_ref[:, col].astype(F32))
            dact = _dot_nt(dhb, wd_ref[pl.ds(pl.multiple_of(j * tn, tn), tn), :])
            da = conv_bwd(dact * v * dg, ua_ref[:, col].astype(F32), cwa_ref[:, col], acca_ref.at[j], carry_a.at[j],
                          dua_ref, col)
            dv = conv_bwd(dact * g, uv_ref[:, col].astype(F32), cwv_ref[:, col], accv_ref.at[j], carry_v.at[j],
                          duv_ref, col)
            term = _dot_nt(da, wu_ref[j]) + _dot_nt(dv, wu_ref[nj + j])
            part = term if part is None else part + term

        @pl.when(jg == 0)
        def _():
            du[...] = part

        @pl.when(jg > 0)
        def _():
            du[...] += part

        @pl.when(jg == ng - 1)
        def _():
            n, rstd, _ = _rms_fwd(h1_ref[...], g2_ref[...])
            dh1, dg2 = _rms_bwd(du[...], n, rstd, g2_ref[...])
            dh1 = dh1 + dh_ref[...]
            dg2_ref[...] += dg2
            dh1_ref[...] = dh1
            dh1b_ref[...] = dh1.astype(dh1b_ref.dtype)

    row = lambda i, j: (ni - 1 - i, 0)
    const = lambda i, j: (0, 0)
    tile = pl.BlockSpec((tm, tw), lambda i, j: (ni - 1 - i, j))
    acc = pl.BlockSpec((nj, FFN_ACC_ROWS, tn), lambda i, j: (0, 0, 0))
    return _call(body, name="ffn_bwd", grid=(ni, ng), comm=comm,
                 in_specs=[pl.BlockSpec((tm, D_MODEL), row), pl.BlockSpec((tm, D_MODEL), row),
                           _resident((D_FF, D_MODEL)), tile, tile, tile, tile,
                           pl.BlockSpec((FFN_CONV, tw), lambda i, j: (0, j)),
                           pl.BlockSpec((FFN_CONV, tw), lambda i, j: (0, j + ng)),
                           _resident(w_up_blocks.shape), pl.BlockSpec((tm, D_MODEL), row),
                           pl.BlockSpec((1, D_MODEL), const)],
                 out_specs=[tile, tile, acc, acc, pl.BlockSpec((tm, D_MODEL), row), pl.BlockSpec((tm, D_MODEL), row),
                            pl.BlockSpec((SUBLANES, D_MODEL), const)],
                 out_shape=[jax.ShapeDtypeStruct((t, D_FF), MXU_DTYPE), jax.ShapeDtypeStruct((t, D_FF), MXU_DTYPE),
                            jax.ShapeDtypeStruct((nj, FFN_ACC_ROWS, tn), F32),
                            jax.ShapeDtypeStruct((nj, FFN_ACC_ROWS, tn), F32),
                            jax.ShapeDtypeStruct((t, D_MODEL), F32), jax.ShapeDtypeStruct((t, D_MODEL), MXU_DTYPE),
                            jax.ShapeDtypeStruct((SUBLANES, D_MODEL), F32)],
                 scratch_shapes=[pltpu.VMEM((nj, SUBLANES, tn), F32), pltpu.VMEM((nj, SUBLANES, tn), F32),
                                 pltpu.VMEM((tm, D_MODEL), F32)],
                 operands=[dh2, dh2_b, w_down, up_a, up_v, conv_a, conv_v, conv_w, conv_w, w_up_blocks, h1, g2])


def _ret_bwd(proj, cos2, sin_signed, gain, o, states, dmix_at):
    t = proj.shape[0]
    c, d, nh = RET_CHUNK, RET_HEAD_DIM, RET_HEADS
    n_chunks = t // c
    per = RET_CHUNKS_PER_STEP if n_chunks % RET_CHUNKS_PER_STEP == 0 else 1
    rows = per * c
    n_steps = n_chunks // per
    decay, xi, zeta, g_chunk = _ret_consts()
    base = 2 * D_LRU

    def units(qk_ref, vg_ref, cos_ref, sin_ref, dec_ref, xi_ref, zeta_ref, gain_ref, o_ref, st_ref,
              dp_ref, dgain_ref, gstate, ctx):
        cur = [None] * nh
        dmix = ctx["scratch"][dmix_at[0]][dmix_at[1]]

        def start():
            @pl.when(pl.program_id(0) == 0)
            def _():
                gstate[...] = jnp.zeros_like(gstate)
                dgain_ref[...] = jnp.zeros_like(dgain_ref)
            for h in range(nh):
                cur[h] = gstate[h]

        def gate_and_norm(s, h, keep):
            rs = slice(s * c, (s + 1) * c)
            lo = h * d
            g = vg_ref[rs, D_RET + lo:D_RET + lo + d].astype(F32)
            gain_h = gain_ref[:, lo:lo + d]
            dy = dmix[rs, D_LRU + lo:D_LRU + lo + d]
            sg = _sigmoid(g)
            o_h = o_ref[rs, lo:lo + d]
            oc = o_h - jnp.mean(o_h, axis=-1, keepdims=True)
            rstd = lax.rsqrt(jnp.mean(oc * oc, axis=-1, keepdims=True) + NORM_EPS)
            on = oc * rstd
            at = base + 3 * D_RET + lo
            dp_ref[rs, at:at + d] = (dy * on * gain_h * (sg * (1.0 + g * (1.0 - sg)))).astype(dp_ref.dtype)
            don_g = dy * (g * sg)
            dgain_ref[:, lo:lo + d] += _colsum8(don_g * on)
            don = don_g * gain_h
            keep["do"] = rstd * (don - jnp.mean(don, axis=-1, keepdims=True)
                                 - on * jnp.mean(don * on, axis=-1, keepdims=True))

        def retain(s, h, keep):
            rs = slice(s * c, (s + 1) * c)
            cos2, sin_s = cos_ref[rs, :], sin_ref[rs, :]
            lo = h * d
            q = _rope(qk_ref[rs, lo:lo + d].astype(F32), cos2, sin_s)
            k = _rope(qk_ref[rs, D_RET + lo:D_RET + lo + d].astype(F32), cos2, sin_s) * RET_SCALE
            v = vg_ref[rs, lo:lo + d]
            xi_h, zeta_h, dec = xi_ref[:, lo:lo + d], zeta_ref[:, lo:lo + d], dec_ref[h]
            do = keep["do"]
            s_prev = st_ref[s, h]
            g_next = cur[h]
            p = _dot_nt(q, k) * dec
            dpm = _dot_nt(do, v) * dec
            keep["dq"] = _dot(dpm, k) + _dot_nt(do, s_prev) * xi_h
            keep["dk"] = _dot_tn(dpm, q) + _dot_nt(v, g_next) * zeta_h
            dv = _dot_tn(p, do) + _dot(k * zeta_h, g_next)
            cur[h] = g_next * g_chunk[h] + _dot_tn(q * xi_h, do)
            at = base + 2 * D_RET + lo
            dp_ref[rs, at:at + d] = dv.astype(dp_ref.dtype)

        def unrope(s, h, keep):
            rs = slice(s * c, (s + 1) * c)
            cos2, sin_s = cos_ref[rs, :], sin_ref[rs, :]
            lo = h * d
            dp_ref[rs, base + lo:base + lo + d] = _rope_bwd(keep["dq"], cos2, sin_s).astype(dp_ref.dtype)
            at = base + D_RET + lo
            dp_ref[rs, at:at + d] = _rope_bwd(keep["dk"] * RET_SCALE, cos2, sin_s).astype(dp_ref.dtype)

        def end():
            for h in range(nh):
                gstate[h] = cur[h]

        pieces = [start]
        for s in reversed(range(per)):
            for h in range(nh):
                keep = {}
                pieces += [lambda s=s, h=h, keep=keep, f=f: f(s, h, keep) for f in (gate_and_norm, retain, unrope)]
        return pieces + [end]

    rev = lambda col: (lambda i: (n_steps - 1 - i, col))
    full2 = lambda i: (0, 0)
    return dict(units=units, grid=(n_steps,),
                in_specs=[pl.BlockSpec((rows, 2 * D_RET), rev(1)), pl.BlockSpec((rows, 2 * D_RET), rev(2)),
                          pl.BlockSpec((rows, d), rev(0)), pl.BlockSpec((rows, d), rev(0)),
                          pl.BlockSpec((nh, c, c), lambda i: (0, 0, 0)), pl.BlockSpec((c, D_RET), full2),
                          pl.BlockSpec((c, D_RET), full2), pl.BlockSpec((1, D_RET), full2),
                          pl.BlockSpec((rows, D_RET), rev(0)),
                          pl.BlockSpec((per, nh, d, d), lambda i: (n_steps - 1 - i, 0, 0, 0))],
                out_specs=[pl.BlockSpec((rows, D_IN), rev(0)), pl.BlockSpec((SUBLANES, D_RET), full2)],
                out_shape=[jax.ShapeDtypeStruct((t, D_IN), MXU_DTYPE), jax.ShapeDtypeStruct((SUBLANES, D_RET), F32)],
                scratch_shapes=[pltpu.VMEM((nh, d, d), F32)],
                operands=[proj, proj, cos2, sin_signed, decay, xi, zeta, gain, o, states])


LRU_ACC = {"conv_w": 0, "conv_b": LRU_CONV, "gate_a_b": LRU_CONV + 1, "gate_x_b": LRU_CONV + 2,
           "lambda": LRU_CONV + 3, "norm_gain": LRU_CONV + 4}
LRU_ACC_ROWS = SUBLANES * (LRU_CONV + 5)


def _lru_bwd(proj, xc_all, h_all, conv_w, wa, ba, wx, bx, lam, gain, dproj_part, dmix_at):
    t = proj.shape[0]
    tm = min(256, t)
    c = D_LRU
    ni = t // tm

    def body(x_ref, xh_ref, g_ref, xc_ref, h_ref, hh_ref, cw_ref, wa_ref, ba_ref, wx_ref, bx_ref, lam_ref,
             gain_ref, acc_ref, dwa_ref, dwx_ref, a_scr, b_scr, mu_scr, carry_mu, carry_dxc, ctx):
        dp_ref = ctx["outs"][dproj_part][0]
        dmix = ctx["scratch"][dmix_at[0]][dmix_at[1]]
        fill = ctx["fill"]
        i = pl.program_id(0)
        r = ni - 1 - i

        @pl.when(i == 0)
        def _():
            acc_ref[...] = jnp.zeros_like(acc_ref)
            dwa_ref[...] = jnp.zeros_like(dwa_ref)
            dwx_ref[...] = jnp.zeros_like(dwx_ref)
            carry_mu[...] = jnp.zeros_like(carry_mu)
            carry_dxc[...] = jnp.zeros_like(carry_dxc)

        def add(name, val, k=0):
            lo = (LRU_ACC[name] + k) * SUBLANES
            acc_ref[lo:lo + SUBLANES, :] += _colsum8(val)

        fill()
        xc, h = xc_ref[...], h_ref[...]
        lam_v = lam_ref[...]
        sp = _softplus(-lam_v)
        rg, ig, a, m = _lru_gates(xc, wa_ref[...], ba_ref[...], wx_ref[...], bx_ref[...], sp, fill)
        gl, dgl = _gelu_parts(g_ref[...].astype(F32))
        fill()
        zn, rstd, _ = _rms_fwd(h * gl, gain_ref[...])
        dy = dmix[:, :c]
        dz, dgain = _rms_bwd(dy, zn, rstd, gain_ref[...])
        lo = LRU_ACC["norm_gain"] * SUBLANES
        acc_ref[lo:lo + SUBLANES, :] += dgain
        dp_ref[:, c:2 * c] = (dz * h * dgl).astype(dp_ref.dtype)
        dh = dz * gl
        fill()
        ga, gb = _group_scan(a, a * dh, reverse=True, fill=fill)
        a_scr[...] = ga
        b_scr[...] = gb
        mu_next_tile = carry_mu[...]
        carry_mu[...] = _carry_scan(a_scr, b_scr, mu_scr, mu_next_tile, reverse=True)
        fill()
        lam_t = dh + _shift_up(mu_scr[...], mu_next_tile, 1)
        h_prev = _shift_down(jnp.where(r == 0, 0.0, hh_ref[...]), h, 1)
        da = lam_t * h_prev
        dig = lam_t * m * xc
        dxc = lam_t * m * ig
        dlog_a = da * a - (lam_t * ig * xc) * (a * a) / m
        fill()
        dpr = dlog_a * ((-LRU_C) * sp) * rg * (1.0 - rg)
        add("lambda", dlog_a * ((-LRU_C) * rg) * (-_sigmoid(-lam_v)))
        dpi = dig * ig * (1.0 - ig)
        add("gate_a_b", dpr)
        add("gate_x_b", dpi)
        fill()
        dwa_ref[...] += _dot_tn(xc, dpr)
        dwx_ref[...] += _dot_tn(xc, dpi)
        dxc = dxc + _dot_nt(dpr, wa_ref[...]) + _dot_nt(dpi, wx_ref[...])
        fill()
        add("conv_b", dxc)
        x = x_ref[...].astype(F32)
        prev = jnp.where(r == 0, 0.0, xh_ref[...].astype(F32)[-SUBLANES:, :])
        cw = cw_ref[...]
        nxt = carry_dxc[...]
        carry_dxc[...] = dxc[:SUBLANES, :]
        dx = cw[LRU_CONV - 1:LRU_CONV, :] * dxc
        for k in range(LRU_CONV - 1):
            dx = dx + cw[k:k + 1, :] * _shift_up(dxc, nxt, LRU_CONV - 1 - k)
        fill()
        for k in range(LRU_CONV):
            add("conv_w", dxc * _shift_down(prev, x, LRU_CONV - 1 - k), k)
        dp_ref[:, :c] = dx.astype(dp_ref.dtype)

    hb = _halo_rows(proj.dtype)
    rev = lambda col: (lambda i: (ni - 1 - i, col))
    halo = lambda rows: (lambda i: (jnp.maximum((ni - 1 - i) * (tm // rows) - 1, 0), 0))
    full = lambda i: (0, 0)
    vec = pl.BlockSpec((1, c), full)
    mat = pl.BlockSpec((c, c), full)
    return dict(body=body, grid=(ni,), fill_points=12,
                in_specs=[pl.BlockSpec((tm, c), rev(0)), pl.BlockSpec((hb, c), halo(hb)), pl.BlockSpec((tm, c), rev(1)),
                          pl.BlockSpec((tm, c), rev(0)), pl.BlockSpec((tm, c), rev(0)),
                          pl.BlockSpec((SUBLANES, c), halo(SUBLANES)),
                          pl.BlockSpec((LRU_CONV, c), full), mat, vec, mat, vec, vec, vec],
                out_specs=[pl.BlockSpec((LRU_ACC_ROWS, c), full), mat, mat],
                out_shape=[jax.ShapeDtypeStruct((LRU_ACC_ROWS, c), F32), jax.ShapeDtypeStruct((c, c), F32),
                           jax.ShapeDtypeStruct((c, c), F32)],
                scratch_shapes=[pltpu.VMEM((tm, c), F32), pltpu.VMEM((tm, c), F32), pltpu.VMEM((tm, c), F32),
                                pltpu.VMEM((SUBLANES, c), F32), pltpu.VMEM((SUBLANES, c), F32)],
                operands=[proj, proj, proj, xc_all, h_all, h_all, conv_w, wa, ba, wx, bx, lam, gain])


def _mix_proj_bwd(dh1, dh1_b, w_out, w_in_blocks, x, g1, dproj_part):
    t = x.shape[0]
    tm = min(256, t)
    ni = t // tm
    nb, _, cb = w_in_blocks.shape
    first_free = -(-2 * D_LRU // cb)
    du = [None]

    def term(dp_ref, w_ref, d):
        part = _dot_nt(dp_ref[:, d * cb:(d + 1) * cb], w_ref[d])
        du[0] = part if du[0] is None else du[0] + part

    def head(dh_ref, dhb_ref, wo_ref, wi_ref, x_ref, g_ref, gx_ref, dg_ref, dmix, ctx):
        @pl.when(pl.program_id(0) == 0)
        def _():
            dg_ref[...] = jnp.zeros_like(dg_ref)
        dmix[...] = _dot_nt(dhb_ref[...], wo_ref[...])
        du[0] = None

    def units(dh_ref, dhb_ref, wo_ref, wi_ref, x_ref, g_ref, gx_ref, dg_ref, dmix, ctx):
        dp_ref = ctx["outs"][dproj_part][0]
        return [lambda d=d: term(dp_ref, wi_ref, d) for d in range(first_free, nb)]

    def tail(dh_ref, dhb_ref, wo_ref, wi_ref, x_ref, g_ref, gx_ref, dg_ref, dmix, ctx):
        dp_ref = ctx["outs"][dproj_part][0]
        for d in range(first_free):
            term(dp_ref, wi_ref, d)
        n, rstd, _ = _rms_fwd(x_ref[...], g_ref[...])
        dx, dg = _rms_bwd(du[0], n, rstd, g_ref[...])
        dg_ref[...] += dg
        gx_ref[...] = dx + dh_ref[...]

    row = lambda i: (ni - 1 - i, 0)
    const = lambda i: (0, 0)
    tile = pl.BlockSpec((tm, D_MODEL), row)
    return dict(head=head, units=units, tail=tail, grid=(ni,),
                in_specs=[tile, tile, _resident(w_out.shape), _resident(w_in_blocks.shape), tile,
                          pl.BlockSpec((1, D_MODEL), const)],
                out_specs=[tile, pl.BlockSpec((SUBLANES, D_MODEL), const)],
                out_shape=[jax.ShapeDtypeStruct((t, D_MODEL), F32), jax.ShapeDtypeStruct((SUBLANES, D_MODEL), F32)],
                scratch_shapes=[pltpu.VMEM((tm, D_MODEL), F32)],
                operands=[dh1, dh1_b, w_out, w_in_blocks, x, g1])


def _pair_sum(core, a, b, name):
    n, r, c = b.shape
    spec = pl.BlockSpec((None, r, c), lambda q, core: (q, 0, 0))

    def body(core_ref, a_ref, b_ref, o_ref):
        o_ref[...] = (a_ref[...].astype(F32) + b_ref[...].astype(F32)).astype(o_ref.dtype)

    return pl.pallas_call(
        body, name=name,
        grid_spec=pltpu.PrefetchScalarGridSpec(
            num_scalar_prefetch=1, grid=(n,),
            in_specs=[pl.BlockSpec((None, r, c), lambda q, core: (2 * q + core[0], 0, 0)), spec], out_specs=spec),
        out_shape=jax.ShapeDtypeStruct(b.shape, b.dtype),
        compiler_params=pltpu.CompilerParams(dimension_semantics=("arbitrary",), vmem_limit_bytes=VMEM_LIMIT),
    )(core, a, b)


ADAMW_BLOCK_BYTES = 4 * 1024 * 1024


def _sum_adamw(parts, w, m, v, name):
    n_parts, r, c = parts.shape
    tr = r
    while n_parts * tr * c * parts.dtype.itemsize > ADAMW_BLOCK_BYTES and tr % (4 * SUBLANES) == 0:
        tr //= 2

    def body(p_ref, w_ref, m_ref, v_ref, g_ref, d_ref, nm_ref, nv_ref):
        g = p_ref[0].astype(F32)
        for s in range(1, n_parts):
            g = g + p_ref[s].astype(F32)
        nm = ADAM_B1 * m_ref[...] + (1.0 - ADAM_B1) * g
        nv = ADAM_B2 * v_ref[...] + (1.0 - ADAM_B2) * (g * g)
        m_hat = nm / (1.0 - ADAM_B1 ** ADAM_STEP)
        v_hat = nv / (1.0 - ADAM_B2 ** ADAM_STEP)
        g_ref[...] = g
        d_ref[...] = -ADAM_LR * (m_hat / (jnp.sqrt(v_hat) + ADAM_EPS) + ADAM_WD * w_ref[...])
        nm_ref[...] = nm
        nv_ref[...] = nv

    row = pl.BlockSpec((tr, c), lambda i: (i, 0))
    return _call(body, name=name, grid=(r // tr,),
                 in_specs=[pl.BlockSpec((n_parts, tr, c), lambda i: (0, i, 0)), row, row, row],
                 out_specs=[row, row, row, row], out_shape=[jax.ShapeDtypeStruct((r, c), F32)] * 4,
                 operands=[parts, w, m, v])


MATRICES = ("w_in", "w_out", "ffn_up_w", "ffn_down_w")
CONVS = ("lru_conv_w", "ffn_conv_w")
REPLICATED = ("norm1_gain", "lru_conv_b", "lru_gate_a_w", "lru_gate_a_b", "lru_gate_x_w", "lru_gate_x_b", "lru_lambda",
              "lru_norm_gain", "ret_norm_gain", "norm2_gain", "ffn_conv_b", "final_norm_gain")
WEIGHTS = ("norm1_gain", "w_in", "lru_conv_w", "lru_conv_b", "lru_gate_a_w", "lru_gate_a_b", "lru_gate_x_w",
           "lru_gate_x_b", "lru_lambda", "lru_norm_gain", "ret_norm_gain", "w_out", "norm2_gain", "ffn_up_w",
           "ffn_conv_w", "ffn_conv_b", "ffn_down_w", "final_norm_gain")


def _rows(a, pad_to):
    a = a.reshape(-1, LANES)
    pad = (-a.shape[0]) % pad_to
    return jnp.pad(a, ((0, pad), (0, 0))) if pad else a


def _pack(arrays, pad_to):
    rows, layout, at = [], [], 0
    for a in arrays:
        r = _rows(a, pad_to)
        layout.append((at, a.size // LANES, a.shape))
        rows.append(r)
        at += r.shape[0]
    return jnp.concatenate(rows, axis=0), layout


def _unpack(packed, layout):
    lead = packed.shape[:-2]
    return [packed[..., at:at + n, :].reshape(lead + shape) for at, n, shape in layout]


def _conv_rows(lru, ffn, dtype, pad_to):
    lead = lru.shape[:-2]
    flat = jnp.concatenate([lru.reshape(lead + (-1,)), ffn.reshape(lead + (-1,))], axis=-1).astype(dtype)
    rows = flat.shape[-1] // LANES
    pad = (-rows) % pad_to
    return jnp.pad(flat.reshape(lead + (rows, LANES)), [(0, 0)] * len(lead) + [(0, pad), (0, 0)])


def _column_blocks(full):
    r, c = full.shape
    return full.reshape(r, N_DEV, c // N_DEV).transpose(1, 0, 2)


def _block_diag(w):
    nh, d, _ = w.shape
    eye = jnp.eye(nh, dtype=w.dtype)
    return (w[:, :, None, :] * eye[:, None, :, None]).reshape(nh * d, nh * d)


def _diag_blocks(dense, nh):
    d = dense.shape[0] // nh
    blocks = dense.reshape(nh, d, nh, d)
    return jnp.stack([blocks[h, :, h, :] for h in range(nh)], axis=0)


def kernel(x, norm1_gain, w_in, lru_conv_w, lru_conv_b, lru_gate_a_w, lru_gate_a_b, lru_gate_x_w, lru_gate_x_b, lru_lambda, lru_norm_gain, ret_norm_gain, w_out, norm2_gain, ffn_up_w, ffn_conv_w, ffn_conv_b, ffn_down_w, final_norm_gain, loss_target, m_norm1_gain, m_w_in, m_lru_conv_w, m_lru_conv_b, m_lru_gate_a_w, m_lru_gate_a_b, m_lru_gate_x_w, m_lru_gate_x_b, m_lru_lambda, m_lru_norm_gain, m_ret_norm_gain, m_w_out, m_norm2_gain, m_ffn_up_w, m_ffn_conv_w, m_ffn_conv_b, m_ffn_down_w, m_final_norm_gain, v_norm1_gain, v_w_in, v_lru_conv_w, v_lru_conv_b, v_lru_gate_a_w, v_lru_gate_a_b, v_lru_gate_x_w, v_lru_gate_x_b, v_lru_lambda, v_lru_norm_gain, v_ret_norm_gain, v_w_out, v_norm2_gain, v_ffn_up_w, v_ffn_conv_w, v_ffn_conv_b, v_ffn_down_w, v_final_norm_gain):
    args = dict(locals())
    given = {n: args[n] for n in WEIGHTS}
    out_shape = {n: given[n].shape for n in WEIGHTS}

    def plain(a):
        return a.reshape(1, -1) if a.ndim <= 2 else a[0]

    w = {n: plain(given[n]) for n in WEIGHTS}
    mom_m = {n: plain(args["m_" + n]) for n in WEIGHTS}
    mom_v = {n: plain(args["v_" + n]) for n in WEIGHTS}
    x2, target = x[0], loss_target[0]
    t = x2.shape[0]
    core = lax.axis_index("c").astype(jnp.int32).reshape(1)
    res = {}

    conv_pad = _conv_rows(w["lru_conv_w"], w["ffn_conv_w"], F32, SUBLANES)
    first = _gather_first([w["w_in"].astype(MXU_DTYPE), conv_pad])
    w_in_blocks, conv_all = _run_comms([first, _gather_second(first.out_shape)], "w_in_all_gather")
    n_lru = w["lru_conv_w"].size
    conv_flat = conv_all.reshape(N_DEV, -1)
    lru_cw = conv_flat[:, :n_lru].reshape((N_DEV,) + w["lru_conv_w"].shape).transpose(1, 0, 2).reshape(LRU_CONV, D_LRU)
    ffn_cw = conv_flat[:, n_lru:n_lru + w["ffn_conv_w"].size].reshape((N_DEV,) + w["ffn_conv_w"].shape)
    ffn_cw = ffn_cw.transpose(1, 0, 2).reshape(FFN_CONV, 2 * D_FF)

    cos2, sin_signed = _rope_tables(t)
    wa = _block_diag(w["lru_gate_a_w"]).astype(MXU_DTYPE)
    wx = _block_diag(w["lru_gate_x_w"]).astype(MXU_DTYPE)
    gf = w["final_norm_gain"]

    early = _gather_first([w["ffn_up_w"].astype(MXU_DTYPE), w["w_out"].astype(MXU_DTYPE)])
    (u1, proj), (up_part, w_out_part) = _inproj_fwd(x2, w["norm1_gain"], w_in_blocks, early)
    ((xc, h_lru, y_lru), (o_ret, y_ret, states)), (up_blocks, w_out_blocks, down_part) = _fused(
        [_lru_fwd(proj, lru_cw, w["lru_conv_b"], wa, w["lru_gate_a_b"], wx, w["lru_gate_x_b"], w["lru_lambda"],
                  w["lru_norm_gain"]),
         _ret_fwd(proj, cos2, sin_signed, w["ret_norm_gain"])],
        "mix_fwd", _both(_gather_second([up_part, w_out_part]), _gather_first([w["ffn_down_w"].astype(MXU_DTYPE)])))
    w_out_full = w_out_blocks.reshape(D_MODEL, D_MODEL)

    (h1, u2), (down_blocks,) = _outproj_fwd(x2, y_lru, y_ret, w_out_full, w["norm2_gain"], _gather_second([down_part]))
    w_down_full = down_blocks.reshape(D_FF, D_MODEL)
    up_a, up_v, conv_a, conv_v, act, dh2, dh2_b, dgf, loss = _ffn_fwd(u2, up_blocks, ffn_cw, w["ffn_conv_b"], w_down_full,
                                                                      h1, gf, target)
    loss = lax.psum(loss[0, 0], ("x", "y", "c"))

    def to_owner_chips(blocks, names, tag):
        theirs = _run_comms([_pair_exchange(blocks)], "grads_pair_exchange_" + tag)
        return [_pair_sum(core, a, b, "grads_pair_sum_" + n) for n, a, b in zip(names, blocks, theirs)]

    def adamw(name, parts):
        res[name] = _sum_adamw(parts, w[name], mom_m[name], mom_v[name], "adamw_" + name)

    g = {"final_norm_gain": dgf[0]}
    dup_a, dup_v, acc_a, acc_v, dh1, dh1_b, dg2 = _ffn_bwd(
        dh2, dh2_b, w_down_full, up_a, up_v, conv_a, conv_v, ffn_cw, up_blocks, h1, w["norm2_gain"], None)
    per_col = lambda a: a[:, ::SUBLANES].transpose(1, 0, 2).reshape(FFN_CONV + 1, D_FF)
    acc = jnp.concatenate([per_col(acc_a), per_col(acc_v)], axis=1)
    g_ffn_cw, g["ffn_conv_b"] = acc[:FFN_CONV], acc[FFN_CONV:]
    g["norm2_gain"] = dg2[:1]
    g_up = jnp.concatenate([_mm_tn(u2, dup_a, "ffn_up_wgrad_a", blocks=N_DEV // 2),
                            _mm_tn(u2, dup_v, "ffn_up_wgrad_v", blocks=N_DEV // 2)], axis=0)
    up_sums = to_owner_chips([g_up], ["ffn_up_w"], "up")
    g_down, (up_parts,) = _mm_tn(act, dh2_b, "ffn_down_wgrad", comm=_chip_exchange(up_sums))
    adamw("ffn_up_w", up_parts)
    g_out = jnp.concatenate([_mm_tn(y_lru, dh1_b, "w_out_wgrad_lru"), _mm_tn(y_ret, dh1_b, "w_out_wgrad_ret")], axis=0)
    low_sums = to_owner_chips([g_down.reshape(N_DEV, D_FF // N_DEV, D_MODEL),
                               g_out.reshape(N_DEV, D_MODEL // N_DEV, D_MODEL)], ["ffn_down_w", "w_out"], "low")
    (dproj, dgain_ret), (grad_x, dg1), (lru_acc, dwa, dwx) = _fused(
        [_ret_bwd(proj, cos2, sin_signed, w["ret_norm_gain"], o_ret, states, dmix_at=(1, 0)),
         _mix_proj_bwd(dh1, dh1_b, w_out_full, w_in_blocks, x2, w["norm1_gain"], dproj_part=0),
         _lru_bwd(proj, xc, h_lru, lru_cw, wa, w["lru_gate_a_b"], wx, w["lru_gate_x_b"], w["lru_lambda"],
                  w["lru_norm_gain"], dproj_part=0, dmix_at=(1, 0))],
        "mix_bwd")
    g["norm1_gain"] = dg1[:1]
    g["ret_norm_gain"] = dgain_ret[:1]
    lru_acc = lru_acc[::SUBLANES]
    g_lru_cw = lru_acc[:LRU_CONV]
    for name in ("conv_b", "gate_a_b", "gate_x_b", "lambda", "norm_gain"):
        g["lru_" + name] = lru_acc[LRU_ACC[name]:LRU_ACC[name] + 1]
    g["lru_gate_a_w"] = _diag_blocks(dwa, LRU_HEADS)
    g["lru_gate_x_w"] = _diag_blocks(dwx, LRU_HEADS)
    rep_packed, rep_layout = _pack([g[n] for n in REPLICATED], SUBLANES)
    g_in, (down_parts, out_parts, rep_part) = _mm_tn(u1, dproj, "w_in_wgrad", blocks=N_DEV,
                                                     comm=_both(_chip_exchange(low_sums), _gather_first([rep_packed])))
    adamw("ffn_down_w", down_parts)
    adamw("w_out", out_parts)
    g_conv = _conv_rows(_column_blocks(g_lru_cw), _column_blocks(g_ffn_cw), GRAD_DTYPE, 2 * SUBLANES)
    in_sums = to_owner_chips([g_in, g_conv], ["w_in", "conv"], "in")
    in_parts, conv_parts, rep_parts = _run_comms([_both(_chip_exchange(in_sums), _gather_second([rep_part]))],
                                                 "last_grads_exchange")
    adamw("w_in", in_parts)
    pad16 = lambda d: _conv_rows(d["lru_conv_w"], d["ffn_conv_w"], F32, 2 * SUBLANES)
    conv_res = _sum_adamw(conv_parts, pad16(w), pad16(mom_m), pad16(mom_v), "adamw_conv")
    for n, lo, hi in (("lru_conv_w", 0, n_lru), ("ffn_conv_w", n_lru, n_lru + w["ffn_conv_w"].size)):
        res[n] = [r.reshape(-1)[lo:hi].reshape(w[n].shape) for r in conv_res]
    rep_res = _sum_adamw(rep_parts, *[_pack([d[n] for n in REPLICATED], SUBLANES)[0] for d in (w, mom_m, mom_v)],
                         "adamw_replicated")
    for k in range(4):
        for n, a in zip(REPLICATED, _unpack(rep_res[k], rep_layout)):
            res.setdefault(n, [None] * 4)[k] = a

    outs = [loss, grad_x[None]]
    for k in range(4):
        outs += [res[n][k].reshape(out_shape[n]) for n in WEIGHTS]
    return tuple(outs)
```

```python
import math

import numpy as np
import jax
import jax.numpy as jnp
from jax import lax
from jax.experimental import pallas as pl
from jax.experimental.pallas import tpu as pltpu

F32 = jnp.float32
BF16 = jnp.bfloat16
MXU_DTYPE = jnp.bfloat16
GRAD_DTYPE = jnp.bfloat16

N_DEV = 8
N_CHIPS = 4
D_MODEL = 1024
D_LRU = 512
LRU_HEADS = 8
LRU_CONV = 4
LRU_C = 8.0
D_RET = 512
RET_HEADS = 4
RET_HEAD_DIM = 128
RET_CHUNK = 128
ROPE_BASE = 10000.0
D_IN = 3072
D_FF = 3072
FFN_CONV = 3
NORM_EPS = 1e-6

ADAM_LR = 0.001
ADAM_B1 = 0.9
ADAM_B2 = 0.999
ADAM_EPS = 1e-08
ADAM_WD = 0.01
ADAM_STEP = 10

SUBLANES = 8
LANES = 128
VMEM_LIMIT = 48 * 1024 * 1024

MESH = pl.DeviceIdType.MESH
ANY = pl.BlockSpec(memory_space=pl.ANY)


def _dot(a, b):
    return jnp.dot(a.astype(MXU_DTYPE), b.astype(MXU_DTYPE), preferred_element_type=F32)


def _dot_nt(a, b):
    return lax.dot_general(a.astype(MXU_DTYPE), b.astype(MXU_DTYPE), (((1,), (1,)), ((), ())),
                           preferred_element_type=F32)


def _dot_tn(a, b):
    return lax.dot_general(a.astype(MXU_DTYPE), b.astype(MXU_DTYPE), (((0,), (0,)), ((), ())),
                           preferred_element_type=F32)


def _sigmoid(x):
    return 0.5 + 0.5 * jnp.tanh(0.5 * x)


_GELU_C = math.sqrt(2.0 / math.pi)
_GELU_C3 = _GELU_C * 0.044715


def _gelu_parts(x):
    x2 = x * x
    t = jnp.tanh(x * (_GELU_C + _GELU_C3 * x2))
    cdf = 0.5 + 0.5 * t
    g = x * cdf
    dg = cdf + (0.5 * x) * (1.0 - t * t) * (_GELU_C + (3.0 * _GELU_C3) * x2)
    return g, dg


def _gelu(x):
    t = jnp.tanh(_GELU_C * (x + 0.044715 * (x * x * x)))
    return x * (0.5 * (1.0 + t))


def _softplus(x):
    return jnp.maximum(x, 0.0) + jnp.log1p(jnp.exp(-jnp.abs(x)))


def _bcast_row(x, r, rows=SUBLANES):
    return jnp.broadcast_to(x[r:r + 1, :], (rows, x.shape[1]))


def _colsum8(x):
    return jnp.broadcast_to(jnp.sum(x, axis=0, keepdims=True), (SUBLANES, x.shape[1]))


def _shift_down(prev8, tile, s):
    if s == 0:
        return tile
    ext = jnp.concatenate([prev8, tile], axis=0)
    return pltpu.roll(ext, s, 0)[SUBLANES:, :]


def _shift_up(tile, next8, s):
    if s == 0:
        return tile
    ext = jnp.concatenate([tile, next8], axis=0)
    return pltpu.roll(ext, SUBLANES - s, 0)[SUBLANES:, :]


def _group_scan(a, b, reverse, fill=lambda: None):
    n = a.shape[0]
    row = lax.broadcasted_iota(jnp.int32, a.shape, 0) & (SUBLANES - 1)
    for s in (1, 2, 4):
        if s > 1:
            fill()
        shift = (n - s) if reverse else s
        a_sh = pltpu.roll(a, shift, 0)
        b_sh = pltpu.roll(b, shift, 0)
        m = (row <= SUBLANES - 1 - s) if reverse else (row >= s)
        b = jnp.where(m, a * b_sh + b, b)
        a = jnp.where(m, a * a_sh, a)
    return a, b


def _carry_scan(a_ref, b_ref, out_ref, carry0, reverse):
    n_groups = a_ref.shape[0] // SUBLANES
    carry = carry0
    for i in range(n_groups):
        r0 = ((n_groups - 1 - i) if reverse else i) * SUBLANES
        hg = a_ref[r0:r0 + SUBLANES, :] * carry + b_ref[r0:r0 + SUBLANES, :]
        out_ref[r0:r0 + SUBLANES, :] = hg
        carry = _bcast_row(hg, 0 if reverse else SUBLANES - 1)
    return carry


def _rms_fwd(h, gain):
    rstd = lax.rsqrt(jnp.mean(h * h, axis=-1, keepdims=True) + NORM_EPS)
    n = h * rstd
    return n, rstd, n * gain


def _rms_bwd(dy, n, rstd, gain):
    dn = dy * gain
    dh = rstd * (dn - n * jnp.mean(dn * n, axis=-1, keepdims=True))
    return dh, _colsum8(dy * n)


def _halo_rows(dtype):
    return SUBLANES * (4 // jnp.dtype(dtype).itemsize)


def _halo_map(tile_rows, col, halo_rows=SUBLANES):
    per = tile_rows // halo_rows
    return lambda i: (jnp.maximum(i * per - 1, 0), col)


def _resident(shape):
    return pl.BlockSpec(shape, lambda *_: (0,) * len(shape), pipeline_mode=pl.Buffered(1))


def _place():
    x, y, c = lax.axis_index("x"), lax.axis_index("y"), lax.axis_index("c")
    chips = [(1 - x, y), (x, 1 - y), (1 - x, 1 - y)]
    return x, y, c, chips


def _dev(x, y, c):
    return 4 * x + 2 * y + c


class _Copy:
    def __init__(self, make):
        self.make = make

    def start(self):
        self.make().start()

    def wait(self):
        self.make().wait()

    def wait_send(self):
        self.make().wait_send()

    def wait_recv(self):
        self.make().wait_recv()


def _remote(src, dst, send_sem, recv_sem, to):
    return _Copy(lambda: pltpu.make_async_remote_copy(src_ref=src, dst_ref=dst, send_sem=send_sem, recv_sem=recv_sem,
                                                      device_id=to, device_id_type=MESH))


def _local(src, dst, sem):
    return _Copy(lambda: pltpu.make_async_copy(src, dst, sem))


class _Comm:
    def __init__(self, operands, out_shape, sems, descs, aliases=()):
        self.operands, self.out_shape, self.sems, self.descs, self.aliases = operands, out_shape, sems, descs, aliases

    def start(self, ins, outs, sems):
        local, sends, _ = self.descs(ins, outs, sems)
        for cp in sends + local:
            cp.start()

    def wait(self, ins, outs, sems):
        local, sends, recvs = self.descs(ins, outs, sems)
        for cp in recvs:
            cp.wait_recv()
        for cp in sends:
            cp.wait_send()
        for cp in local:
            cp.wait()


def _gather_first(shards):
    n = len(shards)

    def descs(ins, outs, sems):
        send, recv, loc = sems
        x, y, c, chips = _place()
        me = _dev(x, y, c)
        targets = [(x, y, 1 - c)] + [(*chip, c) for chip in chips]
        local, sends, recvs = [], [], []
        for t in range(n):
            local.append(_local(ins[t], outs[t].at[me], loc.at[t]))
            for k, to in enumerate(targets):
                i = 4 * t + k
                sends.append(_remote(ins[t], outs[t].at[me], send.at[i], recv.at[i], to))
                recvs.append(_remote(ins[t], outs[t].at[_dev(*to)], send.at[i], recv.at[i], to))
        return local, sends, recvs

    return _Comm(list(shards), [jax.ShapeDtypeStruct((N_DEV,) + s.shape, s.dtype) for s in shards],
                 [pltpu.SemaphoreType.DMA((4 * n,)), pltpu.SemaphoreType.DMA((4 * n,)), pltpu.SemaphoreType.DMA((n,))],
                 descs)


def _gather_second(gathered):
    n = len(gathered)

    def descs(ins, outs, sems):
        send, recv = sems
        x, y, c, chips = _place()
        sends, recvs = [], []
        for t in range(n):
            for j, chip in enumerate(chips):
                i = 3 * t + j
                have, get = _dev(*chip, c), _dev(*chip, 1 - c)
                sends.append(_remote(outs[t].at[have], outs[t].at[have], send.at[i], recv.at[i], (x, y, 1 - c)))
                recvs.append(_remote(outs[t].at[have], outs[t].at[get], send.at[i], recv.at[i], (x, y, 1 - c)))
        return [], sends, recvs

    return _Comm(list(gathered), [jax.ShapeDtypeStruct(g.shape, g.dtype) for g in gathered],
                 [pltpu.SemaphoreType.DMA((3 * n,)), pltpu.SemaphoreType.DMA((3 * n,))], descs,
                 aliases=[(t, t) for t in range(n)])


def _pair_exchange(blocks):
    n = len(blocks)

    def descs(ins, outs, sems):
        send, recv = sems
        x, y, c, _ = _place()
        sends, recvs = [], []
        for t in range(n):
            for q in range(N_CHIPS):
                i = N_CHIPS * t + q
                cp = _remote(ins[t].at[2 * q + 1 - c], outs[t].at[q], send.at[i], recv.at[i], (x, y, 1 - c))
                sends.append(cp)
                recvs.append(cp)
        return [], sends, recvs

    return _Comm(list(blocks), [jax.ShapeDtypeStruct((N_CHIPS,) + b.shape[1:], b.dtype) for b in blocks],
                 [pltpu.SemaphoreType.DMA((N_CHIPS * n,)), pltpu.SemaphoreType.DMA((N_CHIPS * n,))], descs)


def _chip_exchange(blocks):
    n = len(blocks)

    def descs(ins, outs, sems):
        send, recv, loc = sems
        x, y, c, chips = _place()
        me = 2 * x + y
        local, sends, recvs = [], [], []
        for t in range(n):
            local.append(_local(ins[t].at[me], outs[t].at[me], loc.at[t]))
            for j, (px, py) in enumerate(chips):
                i = 3 * t + j
                q = 2 * px + py
                sends.append(_remote(ins[t].at[q], outs[t].at[me], send.at[i], recv.at[i], (px, py, c)))
                recvs.append(_remote(ins[t].at[q], outs[t].at[q], send.at[i], recv.at[i], (px, py, c)))
        return local, sends, recvs

    return _Comm(list(blocks), [jax.ShapeDtypeStruct(b.shape, b.dtype) for b in blocks],
                 [pltpu.SemaphoreType.DMA((3 * n,)), pltpu.SemaphoreType.DMA((3 * n,)), pltpu.SemaphoreType.DMA((n,))],
                 descs)


def _both(a, b):
    na, oa, sa = len(a.operands), len(a.out_shape), len(a.sems)

    def descs(ins, outs, sems):
        local_a, sends_a, recvs_a = a.descs(ins[:na], outs[:oa], sems[:sa])
        local_b, sends_b, recvs_b = b.descs(ins[na:], outs[oa:], sems[sa:])
        return local_a + local_b, sends_a + sends_b, recvs_a + recvs_b

    return _Comm(a.operands + b.operands, a.out_shape + b.out_shape, a.sems + b.sems, descs,
                 aliases=list(a.aliases) + [(na + i, oa + o) for i, o in b.aliases])


def _run_comms(comms, name):
    first = comms[0]
    n_in, n_out = len(first.operands), len(first.out_shape)

    def body(*refs):
        ins, outs, sems = refs[:n_in], refs[n_in:n_in + n_out], list(refs[n_in + n_out:])
        for k, comm in enumerate(comms):
            mine = [sems.pop(0) for _ in comm.sems]
            comm.start(ins if k == 0 else outs, outs, mine)
            comm.wait(ins if k == 0 else outs, outs, mine)

    outs = pl.pallas_call(
        body, name=name, out_shape=first.out_shape, in_specs=[ANY] * n_in, out_specs=[ANY] * n_out,
        scratch_shapes=[s for comm in comms for s in comm.sems], input_output_aliases=dict(first.aliases),
    )(*first.operands)
    return list(outs)


def _call(body, *, name, grid, in_specs, out_specs, out_shape, operands, scratch_shapes=(), comm=None, aliases=None):
    sem = ("arbitrary",) * len(grid)
    params = pltpu.CompilerParams(dimension_semantics=sem, vmem_limit_bytes=VMEM_LIMIT)
    aliases = dict(aliases or {})
    if comm is None:
        return pl.pallas_call(body, name=name, grid=grid, in_specs=in_specs, out_specs=out_specs, out_shape=out_shape,
                              scratch_shapes=list(scratch_shapes), input_output_aliases=aliases,
                              compiler_params=params)(*operands)
    n_in, n_out, n_scr = len(in_specs), len(out_specs), len(scratch_shapes)
    c_in, c_out = len(comm.operands), len(comm.out_shape)

    def wrapped(*refs):
        refs = list(refs)
        ins, refs = refs[:n_in], refs[n_in:]
        cins, refs = refs[:c_in], refs[c_in:]
        outs, refs = refs[:n_out], refs[n_out:]
        couts, refs = refs[:c_out], refs[c_out:]
        scr, csems = refs[:n_scr], refs[n_scr:]
        first = last = None
        for axis, size in enumerate(grid):
            at_first, at_last = pl.program_id(axis) == 0, pl.program_id(axis) == size - 1
            first = at_first if first is None else first & at_first
            last = at_last if last is None else last & at_last

        @pl.when(first)
        def _():
            comm.start(cins, couts, csems)

        body(*ins, *outs, *scr)

        @pl.when(last)
        def _():
            comm.wait(cins, couts, csems)

    res = pl.pallas_call(
        wrapped, name=name, grid=grid, in_specs=list(in_specs) + [ANY] * c_in, out_specs=list(out_specs) + [ANY] * c_out,
        out_shape=list(out_shape) + list(comm.out_shape), scratch_shapes=list(scratch_shapes) + list(comm.sems),
        input_output_aliases={**aliases, **{n_in + i: n_out + o for i, o in comm.aliases}}, compiler_params=params,
    )(*operands, *comm.operands)
    return list(res[:n_out]), list(res[n_out:])


def _mm_tn(a, b, name, blocks=1, tk=2048, comm=None):
    t, m = a.shape
    n = b.shape[1]
    tk = min(tk, t)
    nk = t // tk
    cb = n // blocks
    per = max(1, 768 // cb) if blocks > 1 else 1
    tn = per * cb if blocks > 1 else min(1024, n)
    tm = min(1024, m)
    assert blocks == 1 or tm == m

    def body(a_ref, b_ref, o_ref, acc):
        k = pl.program_id(2)

        @pl.when(k == 0)
        def _():
            acc[...] = jnp.zeros_like(acc)
        acc[...] += _dot_tn(a_ref[...], b_ref[...])

        @pl.when(k == nk - 1)
        def _():
            if blocks == 1:
                o_ref[...] = acc[...].astype(o_ref.dtype)
            else:
                for s in range(per):
                    o_ref[s] = acc[:, s * cb:(s + 1) * cb].astype(o_ref.dtype)

    if blocks == 1:
        out_spec = pl.BlockSpec((tm, tn), lambda i, j, k: (i, j))
        out_shape = jax.ShapeDtypeStruct((m, n), GRAD_DTYPE)
    else:
        out_spec = pl.BlockSpec((per, m, cb), lambda i, j, k: (j, 0, 0))
        out_shape = jax.ShapeDtypeStruct((blocks, m, cb), GRAD_DTYPE)
    res = _call(body, name=name, grid=(m // tm, n // tn, nk), comm=comm,
                in_specs=[pl.BlockSpec((tk, tm), lambda i, j, k: (k, i)), pl.BlockSpec((tk, tn), lambda i, j, k: (k, j))],
                out_specs=[out_spec], out_shape=[out_shape], operands=[a, b],
                scratch_shapes=[pltpu.VMEM((tm, tn), F32)])
    return res[0] if comm is None else (res[0][0], res[1])


def _inproj_fwd(x, g1, w_blocks, comm):
    t = x.shape[0]
    tm = min(512, t)
    nb, _, cb = w_blocks.shape

    def body(x_ref, g_ref, w_ref, u_ref, p_ref):
        _, _, u = _rms_fwd(x_ref[...], g_ref[...])
        u = u.astype(MXU_DTYPE)
        u_ref[...] = u
        for d in range(nb):
            p_ref[:, d * cb:(d + 1) * cb] = _dot(u, w_ref[d]).astype(p_ref.dtype)

    return _call(body, name="inproj_fwd", grid=(t // tm,), comm=comm,
                 in_specs=[pl.BlockSpec((tm, D_MODEL), lambda i: (i, 0)), pl.BlockSpec((1, D_MODEL), lambda i: (0, 0)),
                           _resident(w_blocks.shape)],
                 out_specs=[pl.BlockSpec((tm, D_MODEL), lambda i: (i, 0)), pl.BlockSpec((tm, D_IN), lambda i: (i, 0))],
                 out_shape=[jax.ShapeDtypeStruct((t, D_MODEL), MXU_DTYPE), jax.ShapeDtypeStruct((t, D_IN), MXU_DTYPE)],
                 operands=[x, g1, w_blocks])


def _lru_gates(xc, wa, ba, wx, bx, sp, fill=lambda: None):
    r = _sigmoid(_dot(xc, wa) + ba)
    fill()
    ig = _sigmoid(_dot(xc, wx) + bx)
    fill()
    log_a = (-LRU_C) * r * sp
    a = jnp.exp(log_a)
    m = jnp.sqrt(-jnp.tanh(log_a) * (a * a + 1.0))
    return r, ig, a, m


def _fused(parts, name, comm=None):
    grid = parts[0]["grid"]
    assert all(p["grid"] == grid for p in parts)
    counts = [(len(p["in_specs"]), len(p["out_specs"]), len(p.get("scratch_shapes", ()))) for p in parts]

    def body(*refs):
        refs = list(refs)
        groups = []
        for kind in range(3):
            taken = []
            for c in counts:
                taken.append(refs[:c[kind]])
                refs = refs[c[kind]:]
            groups.append(taken)
        ins, outs, scr = groups
        pending = []

        def fill(n=None):
            for _ in range(share if n is None else n):
                if pending:
                    pending.pop(0)()

        ctx = dict(outs=outs, scratch=scr, fill=fill)
        run = lambda key: [p[key](*ins[k], *outs[k], *scr[k], ctx) for k, p in enumerate(parts) if key in p]
        run("head")
        for pieces in run("units"):
            pending.extend(pieces)
        points = sum(p.get("fill_points", 0) for p in parts)
        share = -(-len(pending) // max(points, 1))
        run("body")
        fill(len(pending))
        run("tail")

    cat = lambda key: [x for p in parts for x in p.get(key, ())]
    res = _call(body, name=name, grid=grid, comm=comm, in_specs=cat("in_specs"), out_specs=cat("out_specs"),
                out_shape=cat("out_shape"), scratch_shapes=cat("scratch_shapes"), operands=cat("operands"))
    outs, side = (res if comm is not None else (res, None))
    split, at = [], 0
    for _, n_out, _ in counts:
        split.append(list(outs[at:at + n_out]))
        at += n_out
    return split if comm is None else (split, side)


def _lru_fwd(proj, conv_w, conv_b, wa, ba, wx, bx, lam, gain):
    t = proj.shape[0]
    tm = min(256, t)
    c = D_LRU

    def body(x_ref, xh_ref, g_ref, cw_ref, cb_ref, wa_ref, ba_ref, wx_ref, bx_ref, lam_ref, gain_ref,
             xc_ref, h_ref, y_ref, a_scr, b_scr, carry, ctx):
        fill = ctx["fill"]
        i = pl.program_id(0)

        @pl.when(i == 0)
        def _():
            carry[...] = jnp.zeros_like(carry)

        fill()
        x = x_ref[...].astype(F32)
        prev = jnp.where(i == 0, 0.0, xh_ref[...].astype(F32)[-SUBLANES:, :])
        cw = cw_ref[...]
        xc = cb_ref[...] + cw[LRU_CONV - 1:LRU_CONV, :] * x
        for k in range(LRU_CONV - 1):
            xc = xc + cw[k:k + 1, :] * _shift_down(prev, x, LRU_CONV - 1 - k)
        xc_ref[...] = xc
        fill()
        sp = _softplus(-lam_ref[...])
        _, ig, a, m = _lru_gates(xc, wa_ref[...], ba_ref[...], wx_ref[...], bx_ref[...], sp, fill)
        fill()
        ga, gb = _group_scan(a, m * (ig * xc), reverse=False, fill=fill)
        a_scr[...] = ga
        b_scr[...] = gb
        fill()
        carry[...] = _carry_scan(a_scr, b_scr, h_ref, carry[...], reverse=False)
        fill()
        z = h_ref[...] * _gelu(g_ref[...].astype(F32))
        fill()
        _, _, y = _rms_fwd(z, gain_ref[...])
        y_ref[...] = y.astype(y_ref.dtype)

    row = lambda i: (i, 0)
    full = lambda i: (0, 0)
    vec = pl.BlockSpec((1, c), full)
    hb = _halo_rows(proj.dtype)
    return dict(body=body, grid=(t // tm,), fill_points=10,
                in_specs=[pl.BlockSpec((tm, c), row), pl.BlockSpec((hb, c), _halo_map(tm, 0, hb)),
                          pl.BlockSpec((tm, c), lambda i: (i, 1)),
                          pl.BlockSpec((LRU_CONV, c), full), vec, pl.BlockSpec((c, c), full), vec,
                          pl.BlockSpec((c, c), full), vec, vec, vec],
                out_specs=[pl.BlockSpec((tm, c), row), pl.BlockSpec((tm, c), row), pl.BlockSpec((tm, c), row)],
                out_shape=[jax.ShapeDtypeStruct((t, c), F32), jax.ShapeDtypeStruct((t, c), F32),
                           jax.ShapeDtypeStruct((t, c), MXU_DTYPE)],
                scratch_shapes=[pltpu.VMEM((tm, c), F32), pltpu.VMEM((tm, c), F32), pltpu.VMEM((SUBLANES, c), F32)],
                operands=[proj, proj, proj, conv_w, conv_b, wa, ba, wx, bx, lam, gain])


def _ret_consts():
    c = RET_CHUNK
    log_g = jnp.log1p(-jnp.exp2(-5.0 - jnp.arange(RET_HEADS, dtype=F32)))
    idx = jnp.arange(c, dtype=F32)
    diff = idx[:, None] - idx[None, :]
    decay = jnp.where(diff[None] >= 0, jnp.exp(jnp.maximum(diff, 0.0)[None] * log_g[:, None, None]), 0.0)
    zeta = jnp.exp((c - 1 - idx)[None, :] * log_g[:, None])
    xi = jnp.exp((idx + 1.0)[None, :] * log_g[:, None])
    spread = lambda v: jnp.repeat(v.T, RET_HEAD_DIM, axis=1)
    log_g_np = np.log1p(-np.exp2(-5.0 - np.arange(RET_HEADS, dtype=np.float32))).astype(np.float32)
    g_chunk = [float(np.exp(np.float32(c) * lg)) for lg in log_g_np]
    return decay, spread(xi), spread(zeta), g_chunk


def _rope_tables(t):
    pos = np.arange(t, dtype=np.float32)
    inv_freq = np.float32(ROPE_BASE) ** (-np.arange(0, RET_HEAD_DIM, 2, dtype=np.float32) / np.float32(RET_HEAD_DIM))
    ang = (pos[:, None] * inv_freq.astype(np.float32)[None, :]).astype(np.float32).astype(np.float64)
    cos, sin = np.cos(ang).astype(np.float32), np.sin(ang).astype(np.float32)
    return jnp.asarray(np.concatenate([cos, cos], axis=-1)), jnp.asarray(np.concatenate([-sin, sin], axis=-1))


def _rope(x, cos2, sin_signed):
    return x * cos2 + pltpu.roll(x, RET_HEAD_DIM // 2, 1) * sin_signed


def _rope_bwd(d, cos2, sin_signed):
    return d * cos2 + pltpu.roll(d * sin_signed, RET_HEAD_DIM // 2, 1)


RET_SCALE = RET_HEAD_DIM ** -0.5


RET_CHUNKS_PER_STEP = 2


def _ret_fwd(proj, cos2, sin_signed, gain):
    t = proj.shape[0]
    c, d, nh = RET_CHUNK, RET_HEAD_DIM, RET_HEADS
    n_chunks = t // c
    per = RET_CHUNKS_PER_STEP if n_chunks % RET_CHUNKS_PER_STEP == 0 else 1
    rows = per * c
    decay, xi, zeta, g_chunk = _ret_consts()

    def units(qk_ref, vg_ref, cos_ref, sin_ref, dec_ref, xi_ref, zeta_ref, gain_ref, o_ref, y_ref, st_ref, state, ctx):
        cur = [None] * nh

        def start():
            @pl.when(pl.program_id(0) == 0)
            def _():
                state[...] = jnp.zeros_like(state)
            for h in range(nh):
                cur[h] = state[h]

        def retain(s, h, keep):
            rs = slice(s * c, (s + 1) * c)
            cos2, sin_s = cos_ref[rs, :], sin_ref[rs, :]
            lo = h * d
            q = _rope(qk_ref[rs, lo:lo + d].astype(F32), cos2, sin_s)
            k = _rope(qk_ref[rs, D_RET + lo:D_RET + lo + d].astype(F32), cos2, sin_s) * RET_SCALE
            v = vg_ref[rs, lo:lo + d]
            s_prev = cur[h]
            st_ref[s, h] = s_prev
            scores = _dot_nt(q, k) * dec_ref[h]
            o = _dot(scores, v) + _dot(q * xi_ref[:, lo:lo + d], s_prev)
            cur[h] = s_prev * g_chunk[h] + _dot_tn(k * zeta_ref[:, lo:lo + d], v)
            o_ref[rs, lo:lo + d] = o
            keep["o"] = o

        def normalise(s, h, keep):
            rs = slice(s * c, (s + 1) * c)
            lo = h * d
            o = keep["o"]
            g = vg_ref[rs, D_RET + lo:D_RET + lo + d].astype(F32)
            mu = jnp.mean(o, axis=-1, keepdims=True)
            oc = o - mu
            on = oc * lax.rsqrt(jnp.mean(oc * oc, axis=-1, keepdims=True) + NORM_EPS)
            y_ref[rs, lo:lo + d] = (on * gain_ref[:, lo:lo + d] * (g * _sigmoid(g))).astype(y_ref.dtype)

        def end():
            for h in range(nh):
                state[h] = cur[h]

        pieces = [start]
        for s in range(per):
            for h in range(nh):
                keep = {}
                pieces += [lambda s=s, h=h, keep=keep: retain(s, h, keep),
                           lambda s=s, h=h, keep=keep: normalise(s, h, keep)]
        return pieces + [end]

    full2 = lambda i: (0, 0)
    return dict(units=units, grid=(n_chunks // per,),
                in_specs=[pl.BlockSpec((rows, 2 * D_RET), lambda i: (i, 1)),
                          pl.BlockSpec((rows, 2 * D_RET), lambda i: (i, 2)),
                          pl.BlockSpec((rows, d), lambda i: (i, 0)), pl.BlockSpec((rows, d), lambda i: (i, 0)),
                          pl.BlockSpec((nh, c, c), lambda i: (0, 0, 0)), pl.BlockSpec((c, D_RET), full2),
                          pl.BlockSpec((c, D_RET), full2), pl.BlockSpec((1, D_RET), full2)],
                out_specs=[pl.BlockSpec((rows, D_RET), lambda i: (i, 0)), pl.BlockSpec((rows, D_RET), lambda i: (i, 0)),
                           pl.BlockSpec((per, nh, d, d), lambda i: (i, 0, 0, 0))],
                out_shape=[jax.ShapeDtypeStruct((t, D_RET), F32), jax.ShapeDtypeStruct((t, D_RET), MXU_DTYPE),
                           jax.ShapeDtypeStruct((n_chunks, nh, d, d), F32)],
                scratch_shapes=[pltpu.VMEM((nh, d, d), F32)],
                operands=[proj, proj, cos2, sin_signed, decay, xi, zeta, gain])


def _outproj_fwd(x, y_lru, y_ret, w_out, g2, comm):
    t = x.shape[0]
    tm = min(512, t)

    def body(x_ref, yl_ref, yr_ref, w_ref, g_ref, h1_ref, u2_ref):
        h1 = x_ref[...] + _dot(yl_ref[...], w_ref[:D_LRU, :]) + _dot(yr_ref[...], w_ref[D_LRU:, :])
        h1_ref[...] = h1
        _, _, u = _rms_fwd(h1, g_ref[...])
        u2_ref[...] = u.astype(u2_ref.dtype)

    row = lambda i: (i, 0)
    return _call(body, name="outproj_fwd", grid=(t // tm,), comm=comm,
                 in_specs=[pl.BlockSpec((tm, D_MODEL), row), pl.BlockSpec((tm, D_LRU), row), pl.BlockSpec((tm, D_RET), row),
                           _resident((D_MODEL, D_MODEL)), pl.BlockSpec((1, D_MODEL), lambda i: (0, 0))],
                 out_specs=[pl.BlockSpec((tm, D_MODEL), row), pl.BlockSpec((tm, D_MODEL), row)],
                 out_shape=[jax.ShapeDtypeStruct((t, D_MODEL), F32), jax.ShapeDtypeStruct((t, D_MODEL), MXU_DTYPE)],
                 operands=[x, y_lru, y_ret, w_out, g2])


FFN_TN = 768
FFN_NJ = D_FF // FFN_TN
FFN_GROUP = 4


def _ffn_fwd(u2, w_blocks, conv_w, conv_b, w_down, h1, gf, target):
    t = u2.shape[0]
    tm = min(256, t)
    tn, nj, group = FFN_TN, FFN_NJ, FFN_GROUP
    ng, tw = nj // group, group * tn
    hb = _halo_rows(u2.dtype)
    assert w_blocks.shape == (2 * nj, D_MODEL, tn)

    def project(u_ext, w, col, up_ref, conv_ref, cw_ref, cb_ref, first):
        ext = _dot(u_ext, w)
        x = ext[hb:, :]
        up_ref[:, col] = x.astype(up_ref.dtype)
        prev = jnp.where(first, 0.0, ext[hb - SUBLANES:hb, :])
        cw = cw_ref[:, col]
        y = cb_ref[:, col] + cw[FFN_CONV - 1:FFN_CONV, :] * x
        for k in range(FFN_CONV - 1):
            y = y + cw[k:k + 1, :] * _shift_down(prev, x, FFN_CONV - 1 - k)
        conv_ref[:, col] = y.astype(conv_ref.dtype)
        return y

    def body(u_ref, uh_ref, w_ref, cwa_ref, cwv_ref, cba_ref, cbv_ref, wd_ref, h1_ref, gf_ref, tg_ref,
             upa_ref, upv_ref, ca_ref, cv_ref, act_ref, dh_ref, dhb_ref, dgf_ref, loss_ref, acc):
        i, jg = pl.program_id(0), pl.program_id(1)

        @pl.when((i == 0) & (jg == 0))
        def _():
            dgf_ref[...] = jnp.zeros_like(dgf_ref)
            loss_ref[...] = jnp.zeros_like(loss_ref)

        @pl.when(jg == 0)
        def _():
            acc[...] = jnp.zeros_like(acc)

        u_ext = jnp.concatenate([uh_ref[...], u_ref[...]], axis=0)
        down = None
        for jj in range(group):
            col = slice(jj * tn, (jj + 1) * tn)
            j = jg * group + jj
            a = project(u_ext, w_ref[j], col, upa_ref, ca_ref, cwa_ref, cba_ref, i == 0)
            v = project(u_ext, w_ref[nj + j], col, upv_ref, cv_ref, cwv_ref, cbv_ref, i == 0)
            act = (_gelu(a) * v).astype(act_ref.dtype)
            act_ref[:, col] = act
            part = _dot(act, wd_ref[pl.ds(pl.multiple_of(j * tn, tn), tn), :])
            down = part if down is None else down + part
        acc[...] += down

        @pl.when(jg == ng - 1)
        def _():
            n, rstd, y = _rms_fwd(h1_ref[...] + acc[...], gf_ref[...])
            err = y - tg_ref[...]
            loss_ref[...] += (0.5 / D_MODEL) * jnp.sum(err * err)
            dh, dgf = _rms_bwd(err * (1.0 / D_MODEL), n, rstd, gf_ref[...])
            dgf_ref[...] += dgf
            dh_ref[...] = dh
            dhb_ref[...] = dh.astype(dhb_ref.dtype)

    per = tm // hb
    row = lambda i, j: (i, 0)
    const = lambda i, j: (0, 0)
    tile = pl.BlockSpec((tm, tw), lambda i, j: (i, j))
    return _call(body, name="ffn_fwd", grid=(t // tm, ng),
                 in_specs=[pl.BlockSpec((tm, D_MODEL), row),
                           pl.BlockSpec((hb, D_MODEL), lambda i, j: (jnp.maximum(i * per - 1, 0), 0)),
                           _resident(w_blocks.shape),
                           pl.BlockSpec((FFN_CONV, tw), lambda i, j: (0, j)),
                           pl.BlockSpec((FFN_CONV, tw), lambda i, j: (0, j + ng)),
                           pl.BlockSpec((1, tw), lambda i, j: (0, j)), pl.BlockSpec((1, tw), lambda i, j: (0, j + ng)),
                           _resident((D_FF, D_MODEL)),
                           pl.BlockSpec((tm, D_MODEL), row), pl.BlockSpec((1, D_MODEL), const),
                           pl.BlockSpec((tm, D_MODEL), row)],
                 out_specs=[tile] * 5 + [pl.BlockSpec((tm, D_MODEL), row),
                            pl.BlockSpec((tm, D_MODEL), row), pl.BlockSpec((SUBLANES, D_MODEL), const),
                            pl.BlockSpec((SUBLANES, LANES), const)],
                 out_shape=[jax.ShapeDtypeStruct((t, D_FF), MXU_DTYPE)] * 5 + [
                            jax.ShapeDtypeStruct((t, D_MODEL), F32),
                            jax.ShapeDtypeStruct((t, D_MODEL), MXU_DTYPE), jax.ShapeDtypeStruct((SUBLANES, D_MODEL), F32),
                            jax.ShapeDtypeStruct((SUBLANES, LANES), F32)],
                 scratch_shapes=[pltpu.VMEM((tm, D_MODEL), F32)],
                 operands=[u2, u2, w_blocks, conv_w, conv_w, conv_b, conv_b, w_down, h1, gf, target])


FFN_ACC_ROWS = SUBLANES * (FFN_CONV + 1)


def _ffn_bwd(dh2, dh2_b, w_down, up_a, up_v, conv_a, conv_v, conv_w, w_up_blocks, h1, g2, comm):
    t = up_a.shape[0]
    tm = min(256, t)
    tn, nj, group = FFN_TN, FFN_NJ, FFN_GROUP
    ng, tw = nj // group, group * tn
    ni = t // tm
    assert w_up_blocks.shape == (2 * nj, D_MODEL, tn)

    def conv_bwd(dy, x, cw, acc_ref, carry_ref, dup_ref, col):
        nxt = carry_ref[...]
        carry_ref[...] = dy[:SUBLANES, :]
        ahead = [_shift_up(dy, nxt, FFN_CONV - 1 - k) for k in range(FFN_CONV)]
        dx = cw[FFN_CONV - 1:FFN_CONV, :] * dy
        for k in range(FFN_CONV - 1):
            dx = dx + cw[k:k + 1, :] * ahead[k]
        dx = dx.astype(dup_ref.dtype)
        dup_ref[:, col] = dx
        for k in range(FFN_CONV):
            acc_ref[k * SUBLANES:(k + 1) * SUBLANES, :] += _colsum8(ahead[k] * x)
        acc_ref[FFN_CONV * SUBLANES:, :] += _colsum8(dy)
        return dx

    def body(dh_ref, dhb_ref, wd_ref, ua_ref, uv_ref, ca_ref, cv_ref, cwa_ref, cwv_ref, wu_ref, h1_ref, g2_ref,
             dua_ref, duv_ref, acca_ref, accv_ref, dh1_ref, dh1b_ref, dg2_ref, carry_a, carry_v, du):
        i, jg = pl.program_id(0), pl.program_id(1)

        @pl.when((i == 0) & (jg == 0))
        def _():
            for ref in (acca_ref, accv_ref, carry_a, carry_v, dg2_ref):
                ref[...] = jnp.zeros_like(ref)

        dhb = dhb_ref[...]
        part = None
        for jj in range(group):
            col = slice(jj * tn, (jj + 1) * tn)
            j = jg * group + jj
            v = cv_ref[:, col].astype(F32)
            g, dg = _gelu_parts(ca_ref[:, col].astype(F32))
            dact = _dot_nt(dhb, wd_ref[pl.ds(pl.multiple_of(j * tn, tn), tn), :])
            da = conv_bwd(dact * v * dg, ua_ref[:, col].astype(F32), cwa_ref[:, col], acca_ref.at[j], carry_a.at[j],
                          dua_ref, col)
            dv = conv_bwd(dact * g, uv_ref[:, col].astype(F32), cwv_ref[:, col], accv_ref.at[j], carry_v.at[j],
                          duv_ref, col)
            term = _dot_nt(da, wu_ref[j]) + _dot_nt(dv, wu_ref[nj + j])
            part = term if part is None else part + term

        @pl.when(jg == 0)
        def _():
            du[...] = part

        @pl.when(jg > 0)
        def _():
            du[...] += part

        @pl.when(jg == ng - 1)
        def _():
            n, rstd, _ = _rms_fwd(h1_ref[...], g2_ref[...])
            dh1, dg2 = _rms_bwd(du[...], n, rstd, g2_ref[...])
            dh1 = dh1 + dh_ref[...]
            dg2_ref[...] += dg2
            dh1_ref[...] = dh1
            dh1b_ref[...] = dh1.astype(dh1b_ref.dtype)

    row = lambda i, j: (ni - 1 - i, 0)
    const = lambda i, j: (0, 0)
    tile = pl.BlockSpec((tm, tw), lambda i, j: (ni - 1 - i, j))
    acc = pl.BlockSpec((nj, FFN_ACC_ROWS, tn), lambda i, j: (0, 0, 0))
    return _call(body, name="ffn_bwd", grid=(ni, ng), comm=comm,
                 in_specs=[pl.BlockSpec((tm, D_MODEL), row), pl.BlockSpec((tm, D_MODEL), row),
                           _resident((D_FF, D_MODEL)), tile, tile, tile, tile,
                           pl.BlockSpec((FFN_CONV, tw), lambda i, j: (0, j)),
                           pl.BlockSpec((FFN_CONV, tw), lambda i, j: (0, j + ng)),
                           _resident(w_up_blocks.shape), pl.BlockSpec((tm, D_MODEL), row),
                           pl.BlockSpec((1, D_MODEL), const)],
                 out_specs=[tile, tile, acc, acc, pl.BlockSpec((tm, D_MODEL), row), pl.BlockSpec((tm, D_MODEL), row),
                            pl.BlockSpec((SUBLANES, D_MODEL), const)],
                 out_shape=[jax.ShapeDtypeStruct((t, D_FF), MXU_DTYPE), jax.ShapeDtypeStruct((t, D_FF), MXU_DTYPE),
                            jax.ShapeDtypeStruct((nj, FFN_ACC_ROWS, tn), F32),
                            jax.ShapeDtypeStruct((nj, FFN_ACC_ROWS, tn), F32),
                            jax.ShapeDtypeStruct((t, D_MODEL), F32), jax.ShapeDtypeStruct((t, D_MODEL), MXU_DTYPE),
                            jax.ShapeDtypeStruct((SUBLANES, D_MODEL), F32)],
                 scratch_shapes=[pltpu.VMEM((nj, SUBLANES, tn), F32), pltpu.VMEM((nj, SUBLANES, tn), F32),
                                 pltpu.VMEM((tm, D_MODEL), F32)],
                 operands=[dh2, dh2_b, w_down, up_a, up_v, conv_a, conv_v, conv_w, conv_w, w_up_blocks, h1, g2])


def _ret_bwd(proj, cos2, sin_signed, gain, o, states, dmix_at):
    t = proj.shape[0]
    c, d, nh = RET_CHUNK, RET_HEAD_DIM, RET_HEADS
    n_chunks = t // c
    per = RET_CHUNKS_PER_STEP if n_chunks % RET_CHUNKS_PER_STEP == 0 else 1
    rows = per * c
    n_steps = n_chunks // per
    decay, xi, zeta, g_chunk = _ret_consts()
    base = 2 * D_LRU

    def units(qk_ref, vg_ref, cos_ref, sin_ref, dec_ref, xi_ref, zeta_ref, gain_ref, o_ref, st_ref,
              dp_ref, dgain_ref, gstate, ctx):
        cur = [None] * nh
        dmix = ctx["scratch"][dmix_at[0]][dmix_at[1]]

        def start():
            @pl.when(pl.program_id(0) == 0)
            def _():
                gstate[...] = jnp.zeros_like(gstate)
                dgain_ref[...] = jnp.zeros_like(dgain_ref)
            for h in range(nh):
                cur[h] = gstate[h]

        def gate_and_norm(s, h, keep):
            rs = slice(s * c, (s + 1) * c)
            lo = h * d
            g = vg_ref[rs, D_RET + lo:D_RET + lo + d].astype(F32)
            gain_h = gain_ref[:, lo:lo + d]
            dy = dmix[rs, D_LRU + lo:D_LRU + lo + d]
            sg = _sigmoid(g)
            o_h = o_ref[rs, lo:lo + d]
            oc = o_h - jnp.mean(o_h, axis=-1, keepdims=True)
            rstd = lax.rsqrt(jnp.mean(oc * oc, axis=-1, keepdims=True) + NORM_EPS)
            on = oc * rstd
            at = base + 3 * D_RET + lo
            dp_ref[rs, at:at + d] = (dy * on * gain_h * (sg * (1.0 + g * (1.0 - sg)))).astype(dp_ref.dtype)
            don_g = dy * (g * sg)
            dgain_ref[:, lo:lo + d] += _colsum8(don_g * on)
            don = don_g * gain_h
            keep["do"] = rstd * (don - jnp.mean(don, axis=-1, keepdims=True)
                                 - on * jnp.mean(don * on, axis=-1, keepdims=True))

        def retain(s, h, keep):
            rs = slice(s * c, (s + 1) * c)
            cos2, sin_s = cos_ref[rs, :], sin_ref[rs, :]
            lo = h * d
            q = _rope(qk_ref[rs, lo:lo + d].astype(F32), cos2, sin_s)
            k = _rope(qk_ref[rs, D_RET + lo:D_RET + lo + d].astype(F32), cos2, sin_s) * RET_SCALE
            v = vg_ref[rs, lo:lo + d]
            xi_h, zeta_h, dec = xi_ref[:, lo:lo + d], zeta_ref[:, lo:lo + d], dec_ref[h]
            do = keep["do"]
            s_prev = st_ref[s, h]
            g_next = cur[h]
            p = _dot_nt(q, k) * dec
            dpm = _dot_nt(do, v) * dec
            keep["dq"] = _dot(dpm, k) + _dot_nt(do, s_prev) * xi_h
            keep["dk"] = _dot_tn(dpm, q) + _dot_nt(v, g_next) * zeta_h
            dv = _dot_tn(p, do) + _dot(k * zeta_h, g_next)
            cur[h] = g_next * g_chunk[h] + _dot_tn(q * xi_h, do)
            at = base + 2 * D_RET + lo
            dp_ref[rs, at:at + d] = dv.astype(dp_ref.dtype)

        def unrope(s, h, keep):
            rs = slice(s * c, (s + 1) * c)
            cos2, sin_s = cos_ref[rs, :], sin_ref[rs, :]
            lo = h * d
            dp_ref[rs, base + lo:base + lo + d] = _rope_bwd(keep["dq"], cos2, sin_s).astype(dp_ref.dtype)
            at = base + D_RET + lo
            dp_ref[rs, at:at + d] = _rope_bwd(keep["dk"] * RET_SCALE, cos2, sin_s).astype(dp_ref.dtype)

        def end():
            for h in range(nh):
                gstate[h] = cur[h]

        pieces = [start]
        for s in reversed(range(per)):
            for h in range(nh):
                keep = {}
                pieces += [lambda s=s, h=h, keep=keep, f=f: f(s, h, keep) for f in (gate_and_norm, retain, unrope)]
        return pieces + [end]

    rev = lambda col: (lambda i: (n_steps - 1 - i, col))
    full2 = lambda i: (0, 0)
    return dict(units=units, grid=(n_steps,),
                in_specs=[pl.BlockSpec((rows, 2 * D_RET), rev(1)), pl.BlockSpec((rows, 2 * D_RET), rev(2)),
                          pl.BlockSpec((rows, d), rev(0)), pl.BlockSpec((rows, d), rev(0)),
                          pl.BlockSpec((nh, c, c), lambda i: (0, 0, 0)), pl.BlockSpec((c, D_RET), full2),
                          pl.BlockSpec((c, D_RET), full2), pl.BlockSpec((1, D_RET), full2),
                          pl.BlockSpec((rows, D_RET), rev(0)),
                          pl.BlockSpec((per, nh, d, d), lambda i: (n_steps - 1 - i, 0, 0, 0))],
                out_specs=[pl.BlockSpec((rows, D_IN), rev(0)), pl.BlockSpec((SUBLANES, D_RET), full2)],
                out_shape=[jax.ShapeDtypeStruct((t, D_IN), MXU_DTYPE), jax.ShapeDtypeStruct((SUBLANES, D_RET), F32)],
                scratch_shapes=[pltpu.VMEM((nh, d, d), F32)],
                operands=[proj, proj, cos2, sin_signed, decay, xi, zeta, gain, o, states])


LRU_ACC = {"conv_w": 0, "conv_b": LRU_CONV, "gate_a_b": LRU_CONV + 1, "gate_x_b": LRU_CONV + 2,
           "lambda": LRU_CONV + 3, "norm_gain": LRU_CONV + 4}
LRU_ACC_ROWS = SUBLANES * (LRU_CONV + 5)


def _lru_bwd(proj, xc_all, h_all, conv_w, wa, ba, wx, bx, lam, gain, dproj_part, dmix_at):
    t = proj.shape[0]
    tm = min(256, t)
    c = D_LRU
    ni = t // tm

    def body(x_ref, xh_ref, g_ref, xc_ref, h_ref, hh_ref, cw_ref, wa_ref, ba_ref, wx_ref, bx_ref, lam_ref,
             gain_ref, acc_ref, dwa_ref, dwx_ref, a_scr, b_scr, mu_scr, carry_mu, carry_dxc, ctx):
        dp_ref = ctx["outs"][dproj_part][0]
        dmix = ctx["scratch"][dmix_at[0]][dmix_at[1]]
        fill = ctx["fill"]
        i = pl.program_id(0)
        r = ni - 1 - i

        @pl.when(i == 0)
        def _():
            acc_ref[...] = jnp.zeros_like(acc_ref)
            dwa_ref[...] = jnp.zeros_like(dwa_ref)
            dwx_ref[...] = jnp.zeros_like(dwx_ref)
            carry_mu[...] = jnp.zeros_like(carry_mu)
            carry_dxc[...] = jnp.zeros_like(carry_dxc)

        def add(name, val, k=0):
            lo = (LRU_ACC[name] + k) * SUBLANES
            acc_ref[lo:lo + SUBLANES, :] += _colsum8(val)

        fill()
        xc, h = xc_ref[...], h_ref[...]
        lam_v = lam_ref[...]
        sp = _softplus(-lam_v)
        rg, ig, a, m = _lru_gates(xc, wa_ref[...], ba_ref[...], wx_ref[...], bx_ref[...], sp, fill)
        gl, dgl = _gelu_parts(g_ref[...].astype(F32))
        fill()
        zn, rstd, _ = _rms_fwd(h * gl, gain_ref[...])
        dy = dmix[:, :c]
        dz, dgain = _rms_bwd(dy, zn, rstd, gain_ref[...])
        lo = LRU_ACC["norm_gain"] * SUBLANES
        acc_ref[lo:lo + SUBLANES, :] += dgain
        dp_ref[:, c:2 * c] = (dz * h * dgl).astype(dp_ref.dtype)
        dh = dz * gl
        fill()
        ga, gb = _group_scan(a, a * dh, reverse=True, fill=fill)
        a_scr[...] = ga
        b_scr[...] = gb
        mu_next_tile = carry_mu[...]
        carry_mu[...] = _carry_scan(a_scr, b_scr, mu_scr, mu_next_tile, reverse=True)
        fill()
        lam_t = dh + _shift_up(mu_scr[...], mu_next_tile, 1)
        h_prev = _shift_down(jnp.where(r == 0, 0.0, hh_ref[...]), h, 1)
        da = lam_t * h_prev
        dig = lam_t * m * xc
        dxc = lam_t * m * ig
        dlog_a = da * a - (lam_t * ig * xc) * (a * a) / m
        fill()
        dpr = dlog_a * ((-LRU_C) * sp) * rg * (1.0 - rg)
        add("lambda", dlog_a * ((-LRU_C) * rg) * (-_sigmoid(-lam_v)))
        dpi = dig * ig * (1.0 - ig)
        add("gate_a_b", dpr)
        add("gate_x_b", dpi)
        fill()
        dwa_ref[...] += _dot_tn(xc, dpr)
        dwx_ref[...] += _dot_tn(xc, dpi)
        dxc = dxc + _dot_nt(dpr, wa_ref[...]) + _dot_nt(dpi, wx_ref[...])
        fill()
        add("conv_b", dxc)
        x = x_ref[...].astype(F32)
        prev = jnp.where(r == 0, 0.0, xh_ref[...].astype(F32)[-SUBLANES:, :])
        cw = cw_ref[...]
        nxt = carry_dxc[...]
        carry_dxc[...] = dxc[:SUBLANES, :]
        dx = cw[LRU_CONV - 1:LRU_CONV, :] * dxc
        for k in range(LRU_CONV - 1):
            dx = dx + cw[k:k + 1, :] * _shift_up(dxc, nxt, LRU_CONV - 1 - k)
        fill()
        for k in range(LRU_CONV):
            add("conv_w", dxc * _shift_down(prev, x, LRU_CONV - 1 - k), k)
        dp_ref[:, :c] = dx.astype(dp_ref.dtype)

    hb = _halo_rows(proj.dtype)
    rev = lambda col: (lambda i: (ni - 1 - i, col))
    halo = lambda rows: (lambda i: (jnp.maximum((ni - 1 - i) * (tm // rows) - 1, 0), 0))
    full = lambda i: (0, 0)
    vec = pl.BlockSpec((1, c), full)
    mat = pl.BlockSpec((c, c), full)
    return dict(body=body, grid=(ni,), fill_points=12,
                in_specs=[pl.BlockSpec((tm, c), rev(0)), pl.BlockSpec((hb, c), halo(hb)), pl.BlockSpec((tm, c), rev(1)),
                          pl.BlockSpec((tm, c), rev(0)), pl.BlockSpec((tm, c), rev(0)),
                          pl.BlockSpec((SUBLANES, c), halo(SUBLANES)),
                          pl.BlockSpec((LRU_CONV, c), full), mat, vec, mat, vec, vec, vec],
                out_specs=[pl.BlockSpec((LRU_ACC_ROWS, c), full), mat, mat],
                out_shape=[jax.ShapeDtypeStruct((LRU_ACC_ROWS, c), F32), jax.ShapeDtypeStruct((c, c), F32),
                           jax.ShapeDtypeStruct((c, c), F32)],
                scratch_shapes=[pltpu.VMEM((tm, c), F32), pltpu.VMEM((tm, c), F32), pltpu.VMEM((tm, c), F32),
                                pltpu.VMEM((SUBLANES, c), F32), pltpu.VMEM((SUBLANES, c), F32)],
                operands=[proj, proj, proj, xc_all, h_all, h_all, conv_w, wa, ba, wx, bx, lam, gain])


def _mix_proj_bwd(dh1, dh1_b, w_out, w_in_blocks, x, g1, dproj_part):
    t = x.shape[0]
    tm = min(256, t)
    ni = t // tm
    nb, _, cb = w_in_blocks.shape
    first_free = -(-2 * D_LRU // cb)
    du = [None]

    def term(dp_ref, w_ref, d):
        part = _dot_nt(dp_ref[:, d * cb:(d + 1) * cb], w_ref[d])
        du[0] = part if du[0] is None else du[0] + part

    def head(dh_ref, dhb_ref, wo_ref, wi_ref, x_ref, g_ref, gx_ref, dg_ref, dmix, ctx):
        @pl.when(pl.program_id(0) == 0)
        def _():
            dg_ref[...] = jnp.zeros_like(dg_ref)
        dmix[...] = _dot_nt(dhb_ref[...], wo_ref[...])
        du[0] = None

    def units(dh_ref, dhb_ref, wo_ref, wi_ref, x_ref, g_ref, gx_ref, dg_ref, dmix, ctx):
        dp_ref = ctx["outs"][dproj_part][0]
        return [lambda d=d: term(dp_ref, wi_ref, d) for d in range(first_free, nb)]

    def tail(dh_ref, dhb_ref, wo_ref, wi_ref, x_ref, g_ref, gx_ref, dg_ref, dmix, ctx):
        dp_ref = ctx["outs"][dproj_part][0]
        for d in range(first_free):
            term(dp_ref, wi_ref, d)
        n, rstd, _ = _rms_fwd(x_ref[...], g_ref[...])
        dx, dg = _rms_bwd(du[0], n, rstd, g_ref[...])
        dg_ref[...] += dg
        gx_ref[...] = dx + dh_ref[...]

    row = lambda i: (ni - 1 - i, 0)
    const = lambda i: (0, 0)
    tile = pl.BlockSpec((tm, D_MODEL), row)
    return dict(head=head, units=units, tail=tail, grid=(ni,),
                in_specs=[tile, tile, _resident(w_out.shape), _resident(w_in_blocks.shape), tile,
                          pl.BlockSpec((1, D_MODEL), const)],
                out_specs=[tile, pl.BlockSpec((SUBLANES, D_MODEL), const)],
                out_shape=[jax.ShapeDtypeStruct((t, D_MODEL), F32), jax.ShapeDtypeStruct((SUBLANES, D_MODEL), F32)],
                scratch_shapes=[pltpu.VMEM((tm, D_MODEL), F32)],
                operands=[dh1, dh1_b, w_out, w_in_blocks, x, g1])


def _pair_sum(core, a, b, name):
    n, r, c = b.shape
    spec = pl.BlockSpec((None, r, c), lambda q, core: (q, 0, 0))

    def body(core_ref, a_ref, b_ref, o_ref):
        o_ref[...] = (a_ref[...].astype(F32) + b_ref[...].astype(F32)).astype(o_ref.dtype)

    return pl.pallas_call(
        body, name=name,
        grid_spec=pltpu.PrefetchScalarGridSpec(
            num_scalar_prefetch=1, grid=(n,),
            in_specs=[pl.BlockSpec((None, r, c), lambda q, core: (2 * q + core[0], 0, 0)), spec], out_specs=spec),
        out_shape=jax.ShapeDtypeStruct(b.shape, b.dtype),
        compiler_params=pltpu.CompilerParams(dimension_semantics=("arbitrary",), vmem_limit_bytes=VMEM_LIMIT),
    )(core, a, b)


ADAMW_BLOCK_BYTES = 4 * 1024 * 1024


def _sum_adamw(parts, w, m, v, name):
    n_parts, r, c = parts.shape
    tr = r
    while n_parts * tr * c * parts.dtype.itemsize > ADAMW_BLOCK_BYTES and tr % (4 * SUBLANES) == 0:
        tr //= 2

    def body(p_ref, w_ref, m_ref, v_ref, g_ref, d_ref, nm_ref, nv_ref):
        g = p_ref[0].astype(F32)
        for s in range(1, n_parts):
            g = g + p_ref[s].astype(F32)
        nm = ADAM_B1 * m_ref[...] + (1.0 - ADAM_B1) * g
        nv = ADAM_B2 * v_ref[...] + (1.0 - ADAM_B2) * (g * g)
        m_hat = nm / (1.0 - ADAM_B1 ** ADAM_STEP)
        v_hat = nv / (1.0 - ADAM_B2 ** ADAM_STEP)
        g_ref[...] = g
        d_ref[...] = -ADAM_LR * (m_hat / (jnp.sqrt(v_hat) + ADAM_EPS) + ADAM_WD * w_ref[...])
        nm_ref[...] = nm
        nv_ref[...] = nv

    row = pl.BlockSpec((tr, c), lambda i: (i, 0))
    return _call(body, name=name, grid=(r // tr,),
                 in_specs=[pl.BlockSpec((n_parts, tr, c), lambda i: (0, i, 0)), row, row, row],
                 out_specs=[row, row, row, row], out_shape=[jax.ShapeDtypeStruct((r, c), F32)] * 4,
                 operands=[parts, w, m, v])


MATRICES = ("w_in", "w_out", "ffn_up_w", "ffn_down_w")
CONVS = ("lru_conv_w", "ffn_conv_w")
REPLICATED = ("norm1_gain", "lru_conv_b", "lru_gate_a_w", "lru_gate_a_b", "lru_gate_x_w", "lru_gate_x_b", "lru_lambda",
              "lru_norm_gain", "ret_norm_gain", "norm2_gain", "ffn_conv_b", "final_norm_gain")
WEIGHTS = ("norm1_gain", "w_in", "lru_conv_w", "lru_conv_b", "lru_gate_a_w", "lru_gate_a_b", "lru_gate_x_w",
           "lru_gate_x_b", "lru_lambda", "lru_norm_gain", "ret_norm_gain", "w_out", "norm2_gain", "ffn_up_w",
           "ffn_conv_w", "ffn_conv_b", "ffn_down_w", "final_norm_gain")


def _rows(a, pad_to):
    a = a.reshape(-1, LANES)
    pad = (-a.shape[0]) % pad_to
    return jnp.pad(a, ((0, pad), (0, 0))) if pad else a


def _pack(arrays, pad_to):
    rows, layout, at = [], [], 0
    for a in arrays:
        r = _rows(a, pad_to)
        layout.append((at, a.size // LANES, a.shape))
        rows.append(r)
        at += r.shape[0]
    return jnp.concatenate(rows, axis=0), layout


def _unpack(packed, layout):
    lead = packed.shape[:-2]
    return [packed[..., at:at + n, :].reshape(lead + shape) for at, n, shape in layout]


def _conv_rows(lru, ffn, dtype, pad_to):
    lead = lru.shape[:-2]
    flat = jnp.concatenate([lru.reshape(lead + (-1,)), ffn.reshape(lead + (-1,))], axis=-1).astype(dtype)
    rows = flat.shape[-1] // LANES
    pad = (-rows) % pad_to
    return jnp.pad(flat.reshape(lead + (rows, LANES)), [(0, 0)] * len(lead) + [(0, pad), (0, 0)])


def _column_blocks(full):
    r, c = full.shape
    return full.reshape(r, N_DEV, c // N_DEV).transpose(1, 0, 2)


def _block_diag(w):
    nh, d, _ = w.shape
    eye = jnp.eye(nh, dtype=w.dtype)
    return (w[:, :, None, :] * eye[:, None, :, None]).reshape(nh * d, nh * d)


def _diag_blocks(dense, nh):
    d = dense.shape[0] // nh
    blocks = dense.reshape(nh, d, nh, d)
    return jnp.stack([blocks[h, :, h, :] for h in range(nh)], axis=0)


def kernel(x, norm1_gain, w_in, lru_conv_w, lru_conv_b, lru_gate_a_w, lru_gate_a_b, lru_gate_x_w, lru_gate_x_b, lru_lambda, lru_norm_gain, ret_norm_gain, w_out, norm2_gain, ffn_up_w, ffn_conv_w, ffn_conv_b, ffn_down_w, final_norm_gain, loss_target, m_norm1_gain, m_w_in, m_lru_conv_w, m_lru_conv_b, m_lru_gate_a_w, m_lru_gate_a_b, m_lru_gate_x_w, m_lru_gate_x_b, m_lru_lambda, m_lru_norm_gain, m_ret_norm_gain, m_w_out, m_norm2_gain, m_ffn_up_w, m_ffn_conv_w, m_ffn_conv_b, m_ffn_down_w, m_final_norm_gain, v_norm1_gain, v_w_in, v_lru_conv_w, v_lru_conv_b, v_lru_gate_a_w, v_lru_gate_a_b, v_lru_gate_x_w, v_lru_gate_x_b, v_lru_lambda, v_lru_norm_gain, v_ret_norm_gain, v_w_out, v_norm2_gain, v_ffn_up_w, v_ffn_conv_w, v_ffn_conv_b, v_ffn_down_w, v_final_norm_gain):
    args = dict(locals())
    given = {n: args[n] for n in WEIGHTS}
    out_shape = {n: given[n].shape for n in WEIGHTS}

    def plain(a):
        return a.reshape(1, -1) if a.ndim <= 2 else a[0]

    w = {n: plain(given[n]) for n in WEIGHTS}
    mom_m = {n: plain(args["m_" + n]) for n in WEIGHTS}
    mom_v = {n: plain(args["v_" + n]) for n in WEIGHTS}
    x2, target = x[0], loss_target[0]
    t = x2.shape[0]
    core = lax.axis_index("c").astype(jnp.int32).reshape(1)
    res = {}

    conv_pad = _conv_rows(w["lru_conv_w"], w["ffn_conv_w"], F32, SUBLANES)
    first = _gather_first([w["w_in"].astype(MXU_DTYPE), conv_pad])
    w_in_blocks, conv_all = _run_comms([first, _gather_second(first.out_shape)], "w_in_all_gather")
    n_lru = w["lru_conv_w"].size
    conv_flat = conv_all.reshape(N_DEV, -1)
    lru_cw = conv_flat[:, :n_lru].reshape((N_DEV,) + w["lru_conv_w"].shape).transpose(1, 0, 2).reshape(LRU_CONV, D_LRU)
    ffn_cw = conv_flat[:, n_lru:n_lru + w["ffn_conv_w"].size].reshape((N_DEV,) + w["ffn_conv_w"].shape)
    ffn_cw = ffn_cw.transpose(1, 0, 2).reshape(FFN_CONV, 2 * D_FF)

    cos2, sin_signed = _rope_tables(t)
    wa = _block_diag(w["lru_gate_a_w"]).astype(MXU_DTYPE)
    wx = _block_diag(w["lru_gate_x_w"]).astype(MXU_DTYPE)
    gf = w["final_norm_gain"]

    early = _gather_first([w["w_out"].astype(MXU_DTYPE), w["ffn_down_w"].astype(MXU_DTYPE)])
    (u1, proj), (w_out_part, down_part) = _inproj_fwd(x2, w["norm1_gain"], w_in_blocks, early)
    ((xc, h_lru, y_lru), (o_ret, y_ret, states)), (w_out_blocks, down_blocks, up_part) = _fused(
        [_lru_fwd(proj, lru_cw, w["lru_conv_b"], wa, w["lru_gate_a_b"], wx, w["lru_gate_x_b"], w["lru_lambda"],
                  w["lru_norm_gain"]),
         _ret_fwd(proj, cos2, sin_signed, w["ret_norm_gain"])],
        "mix_fwd", _both(_gather_second([w_out_part, down_part]), _gather_first([w["ffn_up_w"].astype(MXU_DTYPE)])))
    w_out_full = w_out_blocks.reshape(D_MODEL, D_MODEL)
    w_down_full = down_blocks.reshape(D_FF, D_MODEL)

    (h1, u2), (up_blocks,) = _outproj_fwd(x2, y_lru, y_ret, w_out_full, w["norm2_gain"], _gather_second([up_part]))
    up_a, up_v, conv_a, conv_v, act, dh2, dh2_b, dgf, loss_local = _ffn_fwd(u2, up_blocks, ffn_cw, w["ffn_conv_b"],
                                                                            w_down_full, h1, gf, target)

    def to_owner_chips(blocks, names, tag):
        theirs = _run_comms([_pair_exchange(blocks)], "grads_pair_exchange_" + tag)
        return [_pair_sum(core, a, b, "grads_pair_sum_" + n) for n, a, b in zip(names, blocks, theirs)]

    def adamw(name, parts):
        res[name] = _sum_adamw(parts, w[name], mom_m[name], mom_v[name], "adamw_" + name)

    g = {"final_norm_gain": dgf[0]}
    dup_a, dup_v, acc_a, acc_v, dh1, dh1_b, dg2 = _ffn_bwd(
        dh2, dh2_b, w_down_full, up_a, up_v, conv_a, conv_v, ffn_cw, up_blocks, h1, w["norm2_gain"], None)
    per_col = lambda a: a[:, ::SUBLANES].transpose(1, 0, 2).reshape(FFN_CONV + 1, D_FF)
    acc = jnp.concatenate([per_col(acc_a), per_col(acc_v)], axis=1)
    g_ffn_cw, g["ffn_conv_b"] = acc[:FFN_CONV], acc[FFN_CONV:]
    g["norm2_gain"] = dg2[:1]
    g_up = jnp.concatenate([_mm_tn(u2, dup_a, "ffn_up_wgrad_a", blocks=N_DEV // 2),
                            _mm_tn(u2, dup_v, "ffn_up_wgrad_v", blocks=N_DEV // 2)], axis=0)
    up_sums = to_owner_chips([g_up], ["ffn_up_w"], "up")
    g_down, (up_parts,) = _mm_tn(act, dh2_b, "ffn_down_wgrad", comm=_chip_exchange(up_sums))
    adamw("ffn_up_w", up_parts)
    g_out = jnp.concatenate([_mm_tn(y_lru, dh1_b, "w_out_wgrad_lru"), _mm_tn(y_ret, dh1_b, "w_out_wgrad_ret")], axis=0)
    low_sums = to_owner_chips([g_down.reshape(N_DEV, D_FF // N_DEV, D_MODEL),
                               g_out.reshape(N_DEV, D_MODEL // N_DEV, D_MODEL)], ["ffn_down_w", "w_out"], "low")
    (dproj, dgain_ret), (grad_x, dg1), (lru_acc, dwa, dwx) = _fused(
        [_ret_bwd(proj, cos2, sin_signed, w["ret_norm_gain"], o_ret, states, dmix_at=(1, 0)),
         _mix_proj_bwd(dh1, dh1_b, w_out_full, w_in_blocks, x2, w["norm1_gain"], dproj_part=0),
         _lru_bwd(proj, xc, h_lru, lru_cw, wa, w["lru_gate_a_b"], wx, w["lru_gate_x_b"], w["lru_lambda"],
                  w["lru_norm_gain"], dproj_part=0, dmix_at=(1, 0))],
        "mix_bwd")
    g["norm1_gain"] = dg1[:1]
    g["ret_norm_gain"] = dgain_ret[:1]
    lru_acc = lru_acc[::SUBLANES]
    g_lru_cw = lru_acc[:LRU_CONV]
    for name in ("conv_b", "gate_a_b", "gate_x_b", "lambda", "norm_gain"):
        g["lru_" + name] = lru_acc[LRU_ACC[name]:LRU_ACC[name] + 1]
    g["lru_gate_a_w"] = _diag_blocks(dwa, LRU_HEADS)
    g["lru_gate_x_w"] = _diag_blocks(dwx, LRU_HEADS)
    rep_packed, rep_layout = _pack([g[n] for n in REPLICATED] + [loss_local], SUBLANES)
    g_in, (down_parts, out_parts, rep_part) = _mm_tn(u1, dproj, "w_in_wgrad", blocks=N_DEV,
                                                     comm=_both(_chip_exchange(low_sums), _gather_first([rep_packed])))
    adamw("ffn_down_w", down_parts)
    adamw("w_out", out_parts)
    g_conv = _conv_rows(_column_blocks(g_lru_cw), _column_blocks(g_ffn_cw), GRAD_DTYPE, 2 * SUBLANES)
    in_sums = to_owner_chips([g_in, g_conv], ["w_in", "conv"], "in")
    in_parts, conv_parts, rep_parts = _run_comms([_both(_chip_exchange(in_sums), _gather_second([rep_part]))],
                                                 "last_grads_exchange")
    adamw("w_in", in_parts)
    pad16 = lambda d: _conv_rows(d["lru_conv_w"], d["ffn_conv_w"], F32, 2 * SUBLANES)
    conv_res = _sum_adamw(conv_parts, pad16(w), pad16(mom_m), pad16(mom_v), "adamw_conv")
    for n, lo, hi in (("lru_conv_w", 0, n_lru), ("ffn_conv_w", n_lru, n_lru + w["ffn_conv_w"].size)):
        res[n] = [r.reshape(-1)[lo:hi].reshape(w[n].shape) for r in conv_res]
    no_state = jnp.zeros_like(loss_local)
    rep_res = _sum_adamw(rep_parts, *[_pack([d[n] for n in REPLICATED] + [no_state], SUBLANES)[0]
                                      for d in (w, mom_m, mom_v)], "adamw_replicated")
    for k in range(4):
        for n, a in zip(REPLICATED, _unpack(rep_res[k], rep_layout)):
            res.setdefault(n, [None] * 4)[k] = a
    loss = _unpack(rep_res[0], rep_layout)[-1][0, 0]

    outs = [loss, grad_x[None]]
    for k in range(4):
        outs += [res[n][k].reshape(out_shape[n]) for n in WEIGHTS]
    return tuple(outs)
```

```python
import math

import numpy as np
import jax
import jax.numpy as jnp
from jax import lax
from jax.experimental import pallas as pl
from jax.experimental.pallas import tpu as pltpu

F32 = jnp.float32
BF16 = jnp.bfloat16
MXU_DTYPE = jnp.bfloat16
GRAD_DTYPE = jnp.bfloat16

N_DEV = 8
N_CHIPS = 4
D_MODEL = 1024
D_LRU = 512
LRU_HEADS = 8
LRU_CONV = 4
LRU_C = 8.0
D_RET = 512
RET_HEADS = 4
RET_HEAD_DIM = 128
RET_CHUNK = 128
ROPE_BASE = 10000.0
D_IN = 3072
D_FF = 3072
FFN_CONV = 3
NORM_EPS = 1e-6

ADAM_LR = 0.001
ADAM_B1 = 0.9
ADAM_B2 = 0.999
ADAM_EPS = 1e-08
ADAM_WD = 0.01
ADAM_STEP = 10

SUBLANES = 8
LANES = 128
VMEM_LIMIT = 56 * 1024 * 1024

ROW_TILE = 256
PROJ_ROW_TILE = 512
WGRAD_ROWS = 2048
WGRAD_TILE = 1024
WGRAD_BLOCK_COLUMNS = 768

MESH = pl.DeviceIdType.MESH
ANY = pl.BlockSpec(memory_space=pl.ANY)


def _dot(a, b):
    return jnp.dot(a.astype(MXU_DTYPE), b.astype(MXU_DTYPE), preferred_element_type=F32)


def _dot_nt(a, b):
    return lax.dot_general(a.astype(MXU_DTYPE), b.astype(MXU_DTYPE), (((1,), (1,)), ((), ())),
                           preferred_element_type=F32)


def _dot_tn(a, b):
    return lax.dot_general(a.astype(MXU_DTYPE), b.astype(MXU_DTYPE), (((0,), (0,)), ((), ())),
                           preferred_element_type=F32)


def _sigmoid(x):
    return 0.5 + 0.5 * jnp.tanh(0.5 * x)


_GELU_C = math.sqrt(2.0 / math.pi)
_GELU_C3 = _GELU_C * 0.044715


def _gelu_parts(x):
    x2 = x * x
    t = jnp.tanh(x * (_GELU_C + _GELU_C3 * x2))
    cdf = 0.5 + 0.5 * t
    g = x * cdf
    dg = cdf + (0.5 * x) * (1.0 - t * t) * (_GELU_C + (3.0 * _GELU_C3) * x2)
    return g, dg


def _gelu(x):
    t = jnp.tanh(_GELU_C * (x + 0.044715 * (x * x * x)))
    return x * (0.5 * (1.0 + t))


def _softplus(x):
    return jnp.maximum(x, 0.0) + jnp.log1p(jnp.exp(-jnp.abs(x)))


def _bcast_row(x, r, rows=SUBLANES):
    return jnp.broadcast_to(x[r:r + 1, :], (rows, x.shape[1]))


def _colsum8(x):
    return jnp.broadcast_to(jnp.sum(x, axis=0, keepdims=True), (SUBLANES, x.shape[1]))


def _shift_down(prev8, tile, s):
    if s == 0:
        return tile
    ext = jnp.concatenate([prev8, tile], axis=0)
    return pltpu.roll(ext, s, 0)[SUBLANES:, :]


def _shift_up(tile, next8, s):
    if s == 0:
        return tile
    ext = jnp.concatenate([tile, next8], axis=0)
    return pltpu.roll(ext, SUBLANES - s, 0)[SUBLANES:, :]


def _group_scan(a, b, reverse, fill=lambda: None):
    n = a.shape[0]
    row = lax.broadcasted_iota(jnp.int32, a.shape, 0) & (SUBLANES - 1)
    for s in (1, 2, 4):
        if s > 1:
            fill()
        shift = (n - s) if reverse else s
        a_sh = pltpu.roll(a, shift, 0)
        b_sh = pltpu.roll(b, shift, 0)
        m = (row <= SUBLANES - 1 - s) if reverse else (row >= s)
        b = jnp.where(m, a * b_sh + b, b)
        a = jnp.where(m, a * a_sh, a)
    return a, b


def _carry_scan(a_ref, b_ref, out_ref, carry0, reverse):
    n_groups = a_ref.shape[0] // SUBLANES
    carry = carry0
    for i in range(n_groups):
        r0 = ((n_groups - 1 - i) if reverse else i) * SUBLANES
        hg = a_ref[r0:r0 + SUBLANES, :] * carry + b_ref[r0:r0 + SUBLANES, :]
        out_ref[r0:r0 + SUBLANES, :] = hg
        carry = _bcast_row(hg, 0 if reverse else SUBLANES - 1)
    return carry


def _rms_fwd(h, gain):
    rstd = lax.rsqrt(jnp.mean(h * h, axis=-1, keepdims=True) + NORM_EPS)
    n = h * rstd
    return n, rstd, n * gain


def _rms_bwd(dy, n, rstd, gain):
    dn = dy * gain
    dh = rstd * (dn - n * jnp.mean(dn * n, axis=-1, keepdims=True))
    return dh, _colsum8(dy * n)


def _halo_rows(dtype):
    return SUBLANES * (4 // jnp.dtype(dtype).itemsize)


def _halo_map(tile_rows, col, halo_rows=SUBLANES):
    per = tile_rows // halo_rows
    return lambda i: (jnp.maximum(i * per - 1, 0), col)


def _resident(shape):
    return pl.BlockSpec(shape, lambda *_: (0,) * len(shape), pipeline_mode=pl.Buffered(1))


def _place():
    x, y, c = lax.axis_index("x"), lax.axis_index("y"), lax.axis_index("c")
    chips = [(1 - x, y), (x, 1 - y), (1 - x, 1 - y)]
    return x, y, c, chips


def _dev(x, y, c):
    return 4 * x + 2 * y + c


class _Copy:
    def __init__(self, make):
        self.make = make

    def start(self):
        self.make().start()

    def wait(self):
        self.make().wait()

    def wait_send(self):
        self.make().wait_send()

    def wait_recv(self):
        self.make().wait_recv()


def _remote(src, dst, send_sem, recv_sem, to):
    return _Copy(lambda: pltpu.make_async_remote_copy(src_ref=src, dst_ref=dst, send_sem=send_sem, recv_sem=recv_sem,
                                                      device_id=to, device_id_type=MESH))


def _local(src, dst, sem):
    return _Copy(lambda: pltpu.make_async_copy(src, dst, sem))


class _Comm:
    def __init__(self, operands, out_shape, sems, descs, aliases=()):
        self.operands, self.out_shape, self.sems, self.descs, self.aliases = operands, out_shape, sems, descs, aliases

    def start(self, ins, outs, sems):
        local, sends, _ = self.descs(ins, outs, sems)
        for cp in sends + local:
            cp.start()

    def wait(self, ins, outs, sems):
        local, sends, recvs = self.descs(ins, outs, sems)
        for cp in recvs:
            cp.wait_recv()
        for cp in sends:
            cp.wait_send()
        for cp in local:
            cp.wait()


def _gather_first(shards):
    n = len(shards)

    def descs(ins, outs, sems):
        send, recv, loc = sems
        x, y, c, chips = _place()
        me = _dev(x, y, c)
        targets = [(x, y, 1 - c)] + [(*chip, c) for chip in chips]
        local, sends, recvs = [], [], []
        for t in range(n):
            local.append(_local(ins[t], outs[t].at[me], loc.at[t]))
            for k, to in enumerate(targets):
                i = 4 * t + k
                sends.append(_remote(ins[t], outs[t].at[me], send.at[i], recv.at[i], to))
                recvs.append(_remote(ins[t], outs[t].at[_dev(*to)], send.at[i], recv.at[i], to))
        return local, sends, recvs

    return _Comm(list(shards), [jax.ShapeDtypeStruct((N_DEV,) + s.shape, s.dtype) for s in shards],
                 [pltpu.SemaphoreType.DMA((4 * n,)), pltpu.SemaphoreType.DMA((4 * n,)), pltpu.SemaphoreType.DMA((n,))],
                 descs)


def _gather_second(gathered):
    n = len(gathered)

    def descs(ins, outs, sems):
        send, recv = sems
        x, y, c, chips = _place()
        sends, recvs = [], []
        for t in range(n):
            for j, chip in enumerate(chips):
                i = 3 * t + j
                have, get = _dev(*chip, c), _dev(*chip, 1 - c)
                sends.append(_remote(outs[t].at[have], outs[t].at[have], send.at[i], recv.at[i], (x, y, 1 - c)))
                recvs.append(_remote(outs[t].at[have], outs[t].at[get], send.at[i], recv.at[i], (x, y, 1 - c)))
        return [], sends, recvs

    return _Comm(list(gathered), [jax.ShapeDtypeStruct(g.shape, g.dtype) for g in gathered],
                 [pltpu.SemaphoreType.DMA((3 * n,)), pltpu.SemaphoreType.DMA((3 * n,))], descs,
                 aliases=[(t, t) for t in range(n)])


def _pair_exchange(blocks):
    n = len(blocks)

    def descs(ins, outs, sems):
        send, recv = sems
        x, y, c, _ = _place()
        sends, recvs = [], []
        for t in range(n):
            for q in range(N_CHIPS):
                i = N_CHIPS * t + q
                cp = _remote(ins[t].at[2 * q + 1 - c], outs[t].at[q], send.at[i], recv.at[i], (x, y, 1 - c))
                sends.append(cp)
                recvs.append(cp)
        return [], sends, recvs

    return _Comm(list(blocks), [jax.ShapeDtypeStruct((N_CHIPS,) + b.shape[1:], b.dtype) for b in blocks],
                 [pltpu.SemaphoreType.DMA((N_CHIPS * n,)), pltpu.SemaphoreType.DMA((N_CHIPS * n,))], descs)


def _chip_exchange(blocks):
    n = len(blocks)

    def descs(ins, outs, sems):
        send, recv, loc = sems
        x, y, c, chips = _place()
        me = 2 * x + y
        local, sends, recvs = [], [], []
        for t in range(n):
            local.append(_local(ins[t].at[me], outs[t].at[me], loc.at[t]))
            for j, (px, py) in enumerate(chips):
                i = 3 * t + j
                q = 2 * px + py
                sends.append(_remote(ins[t].at[q], outs[t].at[me], send.at[i], recv.at[i], (px, py, c)))
                recvs.append(_remote(ins[t].at[q], outs[t].at[q], send.at[i], recv.at[i], (px, py, c)))
        return local, sends, recvs

    return _Comm(list(blocks), [jax.ShapeDtypeStruct(b.shape, b.dtype) for b in blocks],
                 [pltpu.SemaphoreType.DMA((3 * n,)), pltpu.SemaphoreType.DMA((3 * n,)), pltpu.SemaphoreType.DMA((n,))],
                 descs)


def _both(a, b):
    na, oa, sa = len(a.operands), len(a.out_shape), len(a.sems)

    def descs(ins, outs, sems):
        local_a, sends_a, recvs_a = a.descs(ins[:na], outs[:oa], sems[:sa])
        local_b, sends_b, recvs_b = b.descs(ins[na:], outs[oa:], sems[sa:])
        return local_a + local_b, sends_a + sends_b, recvs_a + recvs_b

    return _Comm(a.operands + b.operands, a.out_shape + b.out_shape, a.sems + b.sems, descs,
                 aliases=list(a.aliases) + [(na + i, oa + o) for i, o in b.aliases])


def _run_comms(comms, name):
    first = comms[0]
    n_in, n_out = len(first.operands), len(first.out_shape)

    def body(*refs):
        ins, outs, sems = refs[:n_in], refs[n_in:n_in + n_out], list(refs[n_in + n_out:])
        for k, comm in enumerate(comms):
            mine = [sems.pop(0) for _ in comm.sems]
            comm.start(ins if k == 0 else outs, outs, mine)
            comm.wait(ins if k == 0 else outs, outs, mine)

    outs = pl.pallas_call(
        body, name=name, out_shape=first.out_shape, in_specs=[ANY] * n_in, out_specs=[ANY] * n_out,
        scratch_shapes=[s for comm in comms for s in comm.sems], input_output_aliases=dict(first.aliases),
    )(*first.operands)
    return list(outs)


def _call(body, *, name, grid, in_specs, out_specs, out_shape, operands, scratch_shapes=(), comm=None, aliases=None):
    sem = ("arbitrary",) * len(grid)
    params = pltpu.CompilerParams(dimension_semantics=sem, vmem_limit_bytes=VMEM_LIMIT)
    aliases = dict(aliases or {})
    if comm is None:
        return pl.pallas_call(body, name=name, grid=grid, in_specs=in_specs, out_specs=out_specs, out_shape=out_shape,
                              scratch_shapes=list(scratch_shapes), input_output_aliases=aliases,
                              compiler_params=params)(*operands)
    n_in, n_out, n_scr = len(in_specs), len(out_specs), len(scratch_shapes)
    c_in, c_out = len(comm.operands), len(comm.out_shape)

    def wrapped(*refs):
        refs = list(refs)
        ins, refs = refs[:n_in], refs[n_in:]
        cins, refs = refs[:c_in], refs[c_in:]
        outs, refs = refs[:n_out], refs[n_out:]
        couts, refs = refs[:c_out], refs[c_out:]
        scr, csems = refs[:n_scr], refs[n_scr:]
        first = last = None
        for axis, size in enumerate(grid):
            at_first, at_last = pl.program_id(axis) == 0, pl.program_id(axis) == size - 1
            first = at_first if first is None else first & at_first
            last = at_last if last is None else last & at_last

        @pl.when(first)
        def _():
            comm.start(cins, couts, csems)

        body(*ins, *outs, *scr)

        @pl.when(last)
        def _():
            comm.wait(cins, couts, csems)

    res = pl.pallas_call(
        wrapped, name=name, grid=grid, in_specs=list(in_specs) + [ANY] * c_in, out_specs=list(out_specs) + [ANY] * c_out,
        out_shape=list(out_shape) + list(comm.out_shape), scratch_shapes=list(scratch_shapes) + list(comm.sems),
        input_output_aliases={**aliases, **{n_in + i: n_out + o for i, o in comm.aliases}}, compiler_params=params,
    )(*operands, *comm.operands)
    return list(res[:n_out]), list(res[n_out:])


def _mm_tn(a, b, name, blocks=1, comm=None):
    t, m = a.shape
    n = b.shape[1]
    tk = min(WGRAD_ROWS, t)
    nk = t // tk
    cb = n // blocks
    per = max(1, WGRAD_BLOCK_COLUMNS // cb) if blocks > 1 else 1
    tn = per * cb if blocks > 1 else min(WGRAD_TILE, n)
    tm = min(WGRAD_TILE, m)
    assert blocks == 1 or tm == m

    def body(a_ref, b_ref, o_ref, acc):
        k = pl.program_id(2)

        @pl.when(k == 0)
        def _():
            acc[...] = jnp.zeros_like(acc)
        acc[...] += _dot_tn(a_ref[...], b_ref[...])

        @pl.when(k == nk - 1)
        def _():
            if blocks == 1:
                o_ref[...] = acc[...].astype(o_ref.dtype)
            else:
                for s in range(per):
                    o_ref[s] = acc[:, s * cb:(s + 1) * cb].astype(o_ref.dtype)

    if blocks == 1:
        out_spec = pl.BlockSpec((tm, tn), lambda i, j, k: (i, j))
        out_shape = jax.ShapeDtypeStruct((m, n), GRAD_DTYPE)
    else:
        out_spec = pl.BlockSpec((per, m, cb), lambda i, j, k: (j, 0, 0))
        out_shape = jax.ShapeDtypeStruct((blocks, m, cb), GRAD_DTYPE)
    res = _call(body, name=name, grid=(m // tm, n // tn, nk), comm=comm,
                in_specs=[pl.BlockSpec((tk, tm), lambda i, j, k: (k, i)), pl.BlockSpec((tk, tn), lambda i, j, k: (k, j))],
                out_specs=[out_spec], out_shape=[out_shape], operands=[a, b],
                scratch_shapes=[pltpu.VMEM((tm, tn), F32)])
    return res[0] if comm is None else (res[0][0], res[1])


def _inproj_fwd(x, g1, w_blocks, comm):
    t = x.shape[0]
    tm = min(PROJ_ROW_TILE, t)
    nb, _, cb = w_blocks.shape

    def body(x_ref, g_ref, w_ref, u_ref, p_ref):
        _, _, u = _rms_fwd(x_ref[...], g_ref[...])
        u = u.astype(MXU_DTYPE)
        u_ref[...] = u
        for d in range(nb):
            p_ref[:, d * cb:(d + 1) * cb] = _dot(u, w_ref[d]).astype(p_ref.dtype)

    return _call(body, name="inproj_fwd", grid=(t // tm,), comm=comm,
                 in_specs=[pl.BlockSpec((tm, D_MODEL), lambda i: (i, 0)), pl.BlockSpec((1, D_MODEL), lambda i: (0, 0)),
                           _resident(w_blocks.shape)],
                 out_specs=[pl.BlockSpec((tm, D_MODEL), lambda i: (i, 0)), pl.BlockSpec((tm, D_IN), lambda i: (i, 0))],
                 out_shape=[jax.ShapeDtypeStruct((t, D_MODEL), MXU_DTYPE), jax.ShapeDtypeStruct((t, D_IN), MXU_DTYPE)],
                 operands=[x, g1, w_blocks])


def _lru_gates(xc, wa, ba, wx, bx, sp, fill=lambda: None):
    r = _sigmoid(_dot(xc, wa) + ba)
    fill()
    ig = _sigmoid(_dot(xc, wx) + bx)
    fill()
    log_a = (-LRU_C) * r * sp
    a = jnp.exp(log_a)
    m = jnp.sqrt(-jnp.tanh(log_a) * (a * a + 1.0))
    return r, ig, a, m


def _fused(parts, name, comm=None):
    grid = parts[0]["grid"]
    assert all(p["grid"] == grid for p in parts)
    counts = [(len(p["in_specs"]), len(p["out_specs"]), len(p.get("scratch_shapes", ()))) for p in parts]

    def body(*refs):
        refs = list(refs)
        groups = []
        for kind in range(3):
            taken = []
            for c in counts:
                taken.append(refs[:c[kind]])
                refs = refs[c[kind]:]
            groups.append(taken)
        ins, outs, scr = groups
        pending = []

        def fill(n=None):
            for _ in range(share if n is None else n):
                if pending:
                    pending.pop(0)()

        ctx = dict(outs=outs, scratch=scr, fill=fill)
        run = lambda key: [p[key](*ins[k], *outs[k], *scr[k], ctx) for k, p in enumerate(parts) if key in p]
        run("head")
        for pieces in run("units"):
            pending.extend(pieces)
        points = sum(p.get("fill_points", 0) for p in parts)
        share = -(-len(pending) // max(points, 1))
        run("body")
        fill(len(pending))
        run("tail")

    cat = lambda key: [x for p in parts for x in p.get(key, ())]
    res = _call(body, name=name, grid=grid, comm=comm, in_specs=cat("in_specs"), out_specs=cat("out_specs"),
                out_shape=cat("out_shape"), scratch_shapes=cat("scratch_shapes"), operands=cat("operands"))
    outs, side = (res if comm is not None else (res, None))
    split, at = [], 0
    for _, n_out, _ in counts:
        split.append(list(outs[at:at + n_out]))
        at += n_out
    return split if comm is None else (split, side)


def _lru_fwd(proj, conv_w, conv_b, wa, ba, wx, bx, lam, gain):
    t = proj.shape[0]
    tm = min(ROW_TILE, t)
    c = D_LRU

    def body(x_ref, xh_ref, g_ref, cw_ref, cb_ref, wa_ref, ba_ref, wx_ref, bx_ref, lam_ref, gain_ref,
             xc_ref, h_ref, y_ref, a_scr, b_scr, carry, ctx):
        fill = ctx["fill"]
        i = pl.program_id(0)

        @pl.when(i == 0)
        def _():
            carry[...] = jnp.zeros_like(carry)

        fill()
        x = x_ref[...].astype(F32)
        prev = jnp.where(i == 0, 0.0, xh_ref[...].astype(F32)[-SUBLANES:, :])
        cw = cw_ref[...]
        xc = cb_ref[...] + cw[LRU_CONV - 1:LRU_CONV, :] * x
        for k in range(LRU_CONV - 1):
            xc = xc + cw[k:k + 1, :] * _shift_down(prev, x, LRU_CONV - 1 - k)
        xc_ref[...] = xc
        fill()
        sp = _softplus(-lam_ref[...])
        _, ig, a, m = _lru_gates(xc, wa_ref[...], ba_ref[...], wx_ref[...], bx_ref[...], sp, fill)
        fill()
        ga, gb = _group_scan(a, m * (ig * xc), reverse=False, fill=fill)
        a_scr[...] = ga
        b_scr[...] = gb
        fill()
        carry[...] = _carry_scan(a_scr, b_scr, h_ref, carry[...], reverse=False)
        fill()
        z = h_ref[...] * _gelu(g_ref[...].astype(F32))
        fill()
        _, _, y = _rms_fwd(z, gain_ref[...])
        y_ref[...] = y.astype(y_ref.dtype)

    row = lambda i: (i, 0)
    full = lambda i: (0, 0)
    vec = pl.BlockSpec((1, c), full)
    hb = _halo_rows(proj.dtype)
    return dict(body=body, grid=(t // tm,), fill_points=10,
                in_specs=[pl.BlockSpec((tm, c), row), pl.BlockSpec((hb, c), _halo_map(tm, 0, hb)),
                          pl.BlockSpec((tm, c), lambda i: (i, 1)),
                          pl.BlockSpec((LRU_CONV, c), full), vec, pl.BlockSpec((c, c), full), vec,
                          pl.BlockSpec((c, c), full), vec, vec, vec],
                out_specs=[pl.BlockSpec((tm, c), row), pl.BlockSpec((tm, c), row), pl.BlockSpec((tm, c), row)],
                out_shape=[jax.ShapeDtypeStruct((t, c), F32), jax.ShapeDtypeStruct((t, c), F32),
                           jax.ShapeDtypeStruct((t, c), MXU_DTYPE)],
                scratch_shapes=[pltpu.VMEM((tm, c), F32), pltpu.VMEM((tm, c), F32), pltpu.VMEM((SUBLANES, c), F32)],
                operands=[proj, proj, proj, conv_w, conv_b, wa, ba, wx, bx, lam, gain])


def _ret_consts():
    c = RET_CHUNK
    log_g = jnp.log1p(-jnp.exp2(-5.0 - jnp.arange(RET_HEADS, dtype=F32)))
    idx = jnp.arange(c, dtype=F32)
    diff = idx[:, None] - idx[None, :]
    decay = jnp.where(diff[None] >= 0, jnp.exp(jnp.maximum(diff, 0.0)[None] * log_g[:, None, None]), 0.0)
    zeta = jnp.exp((c - 1 - idx)[None, :] * log_g[:, None])
    xi = jnp.exp((idx + 1.0)[None, :] * log_g[:, None])
    spread = lambda v: jnp.repeat(v.T, RET_HEAD_DIM, axis=1)
    log_g_np = np.log1p(-np.exp2(-5.0 - np.arange(RET_HEADS, dtype=np.float32))).astype(np.float32)
    g_chunk = [float(np.exp(np.float32(c) * lg)) for lg in log_g_np]
    return decay, spread(xi), spread(zeta), g_chunk


def _rope_tables(t):
    pos = np.arange(t, dtype=np.float32)
    inv_freq = np.float32(ROPE_BASE) ** (-np.arange(0, RET_HEAD_DIM, 2, dtype=np.float32) / np.float32(RET_HEAD_DIM))
    ang = (pos[:, None] * inv_freq.astype(np.float32)[None, :]).astype(np.float32).astype(np.float64)
    cos, sin = np.cos(ang).astype(np.float32), np.sin(ang).astype(np.float32)
    return jnp.asarray(np.concatenate([cos, cos], axis=-1)), jnp.asarray(np.concatenate([-sin, sin], axis=-1))


def _rope(x, cos2, sin_signed):
    return x * cos2 + pltpu.roll(x, RET_HEAD_DIM // 2, 1) * sin_signed


def _rope_bwd(d, cos2, sin_signed):
    return d * cos2 + pltpu.roll(d * sin_signed, RET_HEAD_DIM // 2, 1)


RET_SCALE = RET_HEAD_DIM ** -0.5


RET_CHUNKS_PER_STEP = 2


def _ret_fwd(proj, cos2, sin_signed, gain):
    t = proj.shape[0]
    c, d, nh = RET_CHUNK, RET_HEAD_DIM, RET_HEADS
    n_chunks = t // c
    per = RET_CHUNKS_PER_STEP if n_chunks % RET_CHUNKS_PER_STEP == 0 else 1
    rows = per * c
    decay, xi, zeta, g_chunk = _ret_consts()

    def units(qk_ref, vg_ref, cos_ref, sin_ref, dec_ref, xi_ref, zeta_ref, gain_ref, o_ref, y_ref, st_ref, state, ctx):
        cur = [None] * nh

        def start():
            @pl.when(pl.program_id(0) == 0)
            def _():
                state[...] = jnp.zeros_like(state)
            for h in range(nh):
                cur[h] = state[h]

        def retain(s, h, keep):
            rs = slice(s * c, (s + 1) * c)
            cos2, sin_s = cos_ref[rs, :], sin_ref[rs, :]
            lo = h * d
            q = _rope(qk_ref[rs, lo:lo + d].astype(F32), cos2, sin_s)
            k = _rope(qk_ref[rs, D_RET + lo:D_RET + lo + d].astype(F32), cos2, sin_s) * RET_SCALE
            v = vg_ref[rs, lo:lo + d]
            s_prev = cur[h]
            st_ref[s, h] = s_prev
            scores = _dot_nt(q, k) * dec_ref[h]
            o = _dot(scores, v) + _dot(q * xi_ref[:, lo:lo + d], s_prev)
            cur[h] = s_prev * g_chunk[h] + _dot_tn(k * zeta_ref[:, lo:lo + d], v)
            o_ref[rs, lo:lo + d] = o
            keep["o"] = o

        def normalise(s, h, keep):
            rs = slice(s * c, (s + 1) * c)
            lo = h * d
            o = keep["o"]
            g = vg_ref[rs, D_RET + lo:D_RET + lo + d].astype(F32)
            mu = jnp.mean(o, axis=-1, keepdims=True)
            oc = o - mu
            on = oc * lax.rsqrt(jnp.mean(oc * oc, axis=-1, keepdims=True) + NORM_EPS)
            y_ref[rs, lo:lo + d] = (on * gain_ref[:, lo:lo + d] * (g * _sigmoid(g))).astype(y_ref.dtype)

        def end():
            for h in range(nh):
                state[h] = cur[h]

        pieces = [start]
        for s in range(per):
            for h in range(nh):
                keep = {}
                pieces += [lambda s=s, h=h, keep=keep: retain(s, h, keep),
                           lambda s=s, h=h, keep=keep: normalise(s, h, keep)]
        return pieces + [end]

    full2 = lambda i: (0, 0)
    return dict(units=units, grid=(n_chunks // per,),
                in_specs=[pl.BlockSpec((rows, 2 * D_RET), lambda i: (i, 1)),
                          pl.BlockSpec((rows, 2 * D_RET), lambda i: (i, 2)),
                          pl.BlockSpec((rows, d), lambda i: (i, 0)), pl.BlockSpec((rows, d), lambda i: (i, 0)),
                          pl.BlockSpec((nh, c, c), lambda i: (0, 0, 0)), pl.BlockSpec((c, D_RET), full2),
                          pl.BlockSpec((c, D_RET), full2), pl.BlockSpec((1, D_RET), full2)],
                out_specs=[pl.BlockSpec((rows, D_RET), lambda i: (i, 0)), pl.BlockSpec((rows, D_RET), lambda i: (i, 0)),
                           pl.BlockSpec((per, nh, d, d), lambda i: (i, 0, 0, 0))],
                out_shape=[jax.ShapeDtypeStruct((t, D_RET), F32), jax.ShapeDtypeStruct((t, D_RET), MXU_DTYPE),
                           jax.ShapeDtypeStruct((n_chunks, nh, d, d), F32)],
                scratch_shapes=[pltpu.VMEM((nh, d, d), F32)],
                operands=[proj, proj, cos2, sin_signed, decay, xi, zeta, gain])


def _outproj_fwd(x, y_lru, y_ret, w_out, g2, comm):
    t = x.shape[0]
    tm = min(PROJ_ROW_TILE, t)

    def body(x_ref, yl_ref, yr_ref, w_ref, g_ref, h1_ref, u2_ref):
        h1 = x_ref[...] + _dot(yl_ref[...], w_ref[:D_LRU, :]) + _dot(yr_ref[...], w_ref[D_LRU:, :])
        h1_ref[...] = h1
        _, _, u = _rms_fwd(h1, g_ref[...])
        u2_ref[...] = u.astype(u2_ref.dtype)

    row = lambda i: (i, 0)
    return _call(body, name="outproj_fwd", grid=(t // tm,), comm=comm,
                 in_specs=[pl.BlockSpec((tm, D_MODEL), row), pl.BlockSpec((tm, D_LRU), row), pl.BlockSpec((tm, D_RET), row),
                           _resident((D_MODEL, D_MODEL)), pl.BlockSpec((1, D_MODEL), lambda i: (0, 0))],
                 out_specs=[pl.BlockSpec((tm, D_MODEL), row), pl.BlockSpec((tm, D_MODEL), row)],
                 out_shape=[jax.ShapeDtypeStruct((t, D_MODEL), F32), jax.ShapeDtypeStruct((t, D_MODEL), MXU_DTYPE)],
                 operands=[x, y_lru, y_ret, w_out, g2])


FFN_TN = 768
FFN_NJ = D_FF // FFN_TN
FFN_GROUP = 4


def _ffn_fwd(u2, w_blocks, conv_w, conv_b, w_down, h1, gf, target):
    t = u2.shape[0]
    tm = min(ROW_TILE, t)
    tn, nj, group = FFN_TN, FFN_NJ, FFN_GROUP
    ng, tw = nj // group, group * tn
    hb = _halo_rows(u2.dtype)
    assert w_blocks.shape == (2 * nj, D_MODEL, tn)

    def conv(ext, col, up_ref, conv_ref, cw_ref, cb_ref, first):
        x = ext[hb:, :]
        up_ref[:, col] = x.astype(up_ref.dtype)
        prev = jnp.where(first, 0.0, ext[hb - SUBLANES:hb, :])
        cw = cw_ref[:, col]
        y = cb_ref[:, col] + cw[FFN_CONV - 1:FFN_CONV, :] * x
        for k in range(FFN_CONV - 1):
            y = y + cw[k:k + 1, :] * _shift_down(prev, x, FFN_CONV - 1 - k)
        conv_ref[:, col] = y.astype(conv_ref.dtype)
        return y

    def body(u_ref, uh_ref, w_ref, cwa_ref, cwv_ref, cba_ref, cbv_ref, wd_ref, h1_ref, gf_ref, tg_ref,
             upa_ref, upv_ref, ca_ref, cv_ref, act_ref, dh_ref, dhb_ref, dgf_ref, loss_ref, acc):
        i, jg = pl.program_id(0), pl.program_id(1)

        @pl.when((i == 0) & (jg == 0))
        def _():
            dgf_ref[...] = jnp.zeros_like(dgf_ref)
            loss_ref[...] = jnp.zeros_like(loss_ref)

        @pl.when(jg == 0)
        def _():
            acc[...] = jnp.zeros_like(acc)

        u_ext = jnp.concatenate([uh_ref[...], u_ref[...]], axis=0)

        def project(jj):
            j = jg * group + jj
            return _dot(u_ext, w_ref[j]), _dot(u_ext, w_ref[nj + j])

        down, ahead = None, project(0)
        for jj in range(group):
            col = slice(jj * tn, (jj + 1) * tn)
            j = jg * group + jj
            ext_a, ext_v = ahead
            if jj + 1 < group:
                ahead = project(jj + 1)
            a = conv(ext_a, col, upa_ref, ca_ref, cwa_ref, cba_ref, i == 0)
            v = conv(ext_v, col, upv_ref, cv_ref, cwv_ref, cbv_ref, i == 0)
            act = (_gelu(a) * v).astype(act_ref.dtype)
            act_ref[:, col] = act
            part = _dot(act, wd_ref[pl.ds(pl.multiple_of(j * tn, tn), tn), :])
            down = part if down is None else down + part
        acc[...] += down

        @pl.when(jg == ng - 1)
        def _():
            n, rstd, y = _rms_fwd(h1_ref[...] + acc[...], gf_ref[...])
            err = y - tg_ref[...]
            loss_ref[...] += (0.5 / D_MODEL) * jnp.sum(err * err)
            dh, dgf = _rms_bwd(err * (1.0 / D_MODEL), n, rstd, gf_ref[...])
            dgf_ref[...] += dgf
            dh_ref[...] = dh
            dhb_ref[...] = dh.astype(dhb_ref.dtype)

    per = tm // hb
    row = lambda i, j: (i, 0)
    const = lambda i, j: (0, 0)
    tile = pl.BlockSpec((tm, tw), lambda i, j: (i, j))
    return _call(body, name="ffn_fwd", grid=(t // tm, ng),
                 in_specs=[pl.BlockSpec((tm, D_MODEL), row),
                           pl.BlockSpec((hb, D_MODEL), lambda i, j: (jnp.maximum(i * per - 1, 0), 0)),
                           _resident(w_blocks.shape),
                           pl.BlockSpec((FFN_CONV, tw), lambda i, j: (0, j)),
                           pl.BlockSpec((FFN_CONV, tw), lambda i, j: (0, j + ng)),
                           pl.BlockSpec((1, tw), lambda i, j: (0, j)), pl.BlockSpec((1, tw), lambda i, j: (0, j + ng)),
                           _resident((D_FF, D_MODEL)),
                           pl.BlockSpec((tm, D_MODEL), row), pl.BlockSpec((1, D_MODEL), const),
                           pl.BlockSpec((tm, D_MODEL), row)],
                 out_specs=[tile] * 5 + [pl.BlockSpec((tm, D_MODEL), row),
                            pl.BlockSpec((tm, D_MODEL), row), pl.BlockSpec((SUBLANES, D_MODEL), const),
                            pl.BlockSpec((SUBLANES, LANES), const)],
                 out_shape=[jax.ShapeDtypeStruct((t, D_FF), MXU_DTYPE)] * 5 + [
                            jax.ShapeDtypeStruct((t, D_MODEL), F32),
                            jax.ShapeDtypeStruct((t, D_MODEL), MXU_DTYPE), jax.ShapeDtypeStruct((SUBLANES, D_MODEL), F32),
                            jax.ShapeDtypeStruct((SUBLANES, LANES), F32)],
                 scratch_shapes=[pltpu.VMEM((tm, D_MODEL), F32)],
                 operands=[u2, u2, w_blocks, conv_w, conv_w, conv_b, conv_b, w_down, h1, gf, target])


FFN_ACC_ROWS = SUBLANES * (FFN_CONV + 1)


def _ffn_bwd(dh2, dh2_b, w_down, up_a, up_v, conv_a, conv_v, conv_w, w_up_blocks, h1, g2, comm):
    t = up_a.shape[0]
    tm = min(ROW_TILE, t)
    tn, nj, group = FFN_TN, FFN_NJ, FFN_GROUP
    ng, tw = nj // group, group * tn
    ni = t // tm
    assert w_up_blocks.shape == (2 * nj, D_MODEL, tn)

    def conv_bwd(dy, x, cw, acc_ref, carry_ref, dup_ref, col):
        nxt = carry_ref[...]
        carry_ref[...] = dy[:SUBLANES, :]
        ahead = [_shift_up(dy, nxt, FFN_CONV - 1 - k) for k in range(FFN_CONV)]
        dx = cw[FFN_CONV - 1:FFN_CONV, :] * dy
        for k in range(FFN_CONV - 1):
            dx = dx + cw[k:k + 1, :] * ahead[k]
        dx = dx.astype(dup_ref.dtype)
        dup_ref[:, col] = dx
        for k in range(FFN_CONV):
            acc_ref[k * SUBLANES:(k + 1) * SUBLANES, :] += _colsum8(ahead[k] * x)
        acc_ref[FFN_CONV * SUBLANES:, :] += _colsum8(dy)
        return dx

    def body(dh_ref, dhb_ref, wd_ref, ua_ref, uv_ref, ca_ref, cv_ref, cwa_ref, cwv_ref, wu_ref, h1_ref, g2_ref,
             dua_ref, duv_ref, acca_ref, accv_ref, dh1_ref, dh1b_ref, dg2_ref, carry_a, carry_v, du):
        i, jg = pl.program_id(0), pl.program_id(1)

        @pl.when((i == 0) & (jg == 0))
        def _():
            for ref in (acca_ref, accv_ref, carry_a, carry_v, dg2_ref):
                ref[...] = jnp.zeros_like(ref)

        dhb = dhb_ref[...]

        def through_down(jj):
            j = jg * group + jj
            return _dot_nt(dhb, wd_ref[pl.ds(pl.multiple_of(j * tn, tn), tn), :])

        part, ahead = None, through_down(0)
        for jj in range(group):
            col = slice(jj * tn, (jj + 1) * tn)
            j = jg * group + jj
            dact = ahead
            if jj + 1 < group:
                ahead = through_down(jj + 1)
            v = cv_ref[:, col].astype(F32)
            g, dg = _gelu_parts(ca_ref[:, col].astype(F32))
            da = conv_bwd(dact * v * dg, ua_ref[:, col].astype(F32), cwa_ref[:, col], acca_ref.at[j], carry_a.at[j],
                          dua_ref, col)
            dv = conv_bwd(dact * g, uv_ref[:, col].astype(F32), cwv_ref[:, col], accv_ref.at[j], carry_v.at[j],
                          duv_ref, col)
            term = _dot_nt(da, wu_ref[j]) + _dot_nt(dv, wu_ref[nj + j])
            part = term if part is None else part + term

        @pl.when(jg == 0)
        def _():
            du[...] = part

        @pl.when(jg > 0)
        def _():
            du[...] += part

        @pl.when(jg == ng - 1)
        def _():
            n, rstd, _ = _rms_fwd(h1_ref[...], g2_ref[...])
            dh1, dg2 = _rms_bwd(du[...], n, rstd, g2_ref[...])
            dh1 = dh1 + dh_ref[...]
            dg2_ref[...] += dg2
            dh1_ref[...] = dh1
            dh1b_ref[...] = dh1.astype(dh1b_ref.dtype)

    row = lambda i, j: (ni - 1 - i, 0)
    const = lambda i, j: (0, 0)
    tile = pl.BlockSpec((tm, tw), lambda i, j: (ni - 1 - i, j))
    acc = pl.BlockSpec((nj, FFN_ACC_ROWS, tn), lambda i, j: (0, 0, 0))
    return _call(body, name="ffn_bwd", grid=(ni, ng), comm=comm,
                 in_specs=[pl.BlockSpec((tm, D_MODEL), row), pl.BlockSpec((tm, D_MODEL), row),
                           _resident((D_FF, D_MODEL)), tile, tile, tile, tile,
                           pl.BlockSpec((FFN_CONV, tw), lambda i, j: (0, j)),
                           pl.BlockSpec((FFN_CONV, tw), lambda i, j: (0, j + ng)),
                           _resident(w_up_blocks.shape), pl.BlockSpec((tm, D_MODEL), row),
                           pl.BlockSpec((1, D_MODEL), const)],
                 out_specs=[tile, tile, acc, acc, pl.BlockSpec((tm, D_MODEL), row), pl.BlockSpec((tm, D_MODEL), row),
                            pl.BlockSpec((SUBLANES, D_MODEL), const)],
                 out_shape=[jax.ShapeDtypeStruct((t, D_FF), MXU_DTYPE), jax.ShapeDtypeStruct((t, D_FF), MXU_DTYPE),
                            jax.ShapeDtypeStruct((nj, FFN_ACC_ROWS, tn), F32),
                            jax.ShapeDtypeStruct((nj, FFN_ACC_ROWS, tn), F32),
                            jax.ShapeDtypeStruct((t, D_MODEL), F32), jax.ShapeDtypeStruct((t, D_MODEL), MXU_DTYPE),
                            jax.ShapeDtypeStruct((SUBLANES, D_MODEL), F32)],
                 scratch_shapes=[pltpu.VMEM((nj, SUBLANES, tn), F32), pltpu.VMEM((nj, SUBLANES, tn), F32),
                                 pltpu.VMEM((tm, D_MODEL), F32)],
                 operands=[dh2, dh2_b, w_down, up_a, up_v, conv_a, conv_v, conv_w, conv_w, w_up_blocks, h1, g2])


def _ret_bwd(proj, cos2, sin_signed, gain, o, states, dmix_at):
    t = proj.shape[0]
    c, d, nh = RET_CHUNK, RET_HEAD_DIM, RET_HEADS
    n_chunks = t // c
    per = RET_CHUNKS_PER_STEP if n_chunks % RET_CHUNKS_PER_STEP == 0 else 1
    rows = per * c
    n_steps = n_chunks // per
    decay, xi, zeta, g_chunk = _ret_consts()
    base = 2 * D_LRU

    def units(qk_ref, vg_ref, cos_ref, sin_ref, dec_ref, xi_ref, zeta_ref, gain_ref, o_ref, st_ref,
              dp_ref, dgain_ref, gstate, ctx):
        cur = [None] * nh
        dmix = ctx["scratch"][dmix_at[0]][dmix_at[1]]

        def start():
            @pl.when(pl.program_id(0) == 0)
            def _():
                gstate[...] = jnp.zeros_like(gstate)
                dgain_ref[...] = jnp.zeros_like(dgain_ref)
            for h in range(nh):
                cur[h] = gstate[h]

        def gate_and_norm(s, h, keep):
            rs = slice(s * c, (s + 1) * c)
            lo = h * d
            g = vg_ref[rs, D_RET + lo:D_RET + lo + d].astype(F32)
            gain_h = gain_ref[:, lo:lo + d]
            dy = dmix[rs, D_LRU + lo:D_LRU + lo + d]
            sg = _sigmoid(g)
            o_h = o_ref[rs, lo:lo + d]
            oc = o_h - jnp.mean(o_h, axis=-1, keepdims=True)
            rstd = lax.rsqrt(jnp.mean(oc * oc, axis=-1, keepdims=True) + NORM_EPS)
            on = oc * rstd
            at = base + 3 * D_RET + lo
            dp_ref[rs, at:at + d] = (dy * on * gain_h * (sg * (1.0 + g * (1.0 - sg)))).astype(dp_ref.dtype)
            don_g = dy * (g * sg)
            dgain_ref[:, lo:lo + d] += _colsum8(don_g * on)
            don = don_g * gain_h
            keep["do"] = rstd * (don - jnp.mean(don, axis=-1, keepdims=True)
                                 - on * jnp.mean(don * on, axis=-1, keepdims=True))

        def retain(s, h, keep):
            rs = slice(s * c, (s + 1) * c)
            cos2, sin_s = cos_ref[rs, :], sin_ref[rs, :]
            lo = h * d
            q = _rope(qk_ref[rs, lo:lo + d].astype(F32), cos2, sin_s)
            k = _rope(qk_ref[rs, D_RET + lo:D_RET + lo + d].astype(F32), cos2, sin_s) * RET_SCALE
            v = vg_ref[rs, lo:lo + d]
            xi_h, zeta_h, dec = xi_ref[:, lo:lo + d], zeta_ref[:, lo:lo + d], dec_ref[h]
            do = keep["do"]
            s_prev = st_ref[s, h]
            g_next = cur[h]
            p = _dot_nt(q, k) * dec
            dpm = _dot_nt(do, v) * dec
            keep["dq"] = _dot(dpm, k) + _dot_nt(do, s_prev) * xi_h
            keep["dk"] = _dot_tn(dpm, q) + _dot_nt(v, g_next) * zeta_h
            dv = _dot_tn(p, do) + _dot(k * zeta_h, g_next)
            cur[h] = g_next * g_chunk[h] + _dot_tn(q * xi_h, do)
            at = base + 2 * D_RET + lo
            dp_ref[rs, at:at + d] = dv.astype(dp_ref.dtype)

        def unrope(s, h, keep):
            rs = slice(s * c, (s + 1) * c)
            cos2, sin_s = cos_ref[rs, :], sin_ref[rs, :]
            lo = h * d
            dp_ref[rs, base + lo:base + lo + d] = _rope_bwd(keep["dq"], cos2, sin_s).astype(dp_ref.dtype)
            at = base + D_RET + lo
            dp_ref[rs, at:at + d] = _rope_bwd(keep["dk"] * RET_SCALE, cos2, sin_s).astype(dp_ref.dtype)

        def end():
            for h in range(nh):
                gstate[h] = cur[h]

        pieces = [start]
        for s in reversed(range(per)):
            for h in range(nh):
                keep = {}
                pieces += [lambda s=s, h=h, keep=keep, f=f: f(s, h, keep) for f in (gate_and_norm, retain, unrope)]
        return pieces + [end]

    rev = lambda col: (lambda i: (n_steps - 1 - i, col))
    full2 = lambda i: (0, 0)
    return dict(units=units, grid=(n_steps,),
                in_specs=[pl.BlockSpec((rows, 2 * D_RET), rev(1)), pl.BlockSpec((rows, 2 * D_RET), rev(2)),
                          pl.BlockSpec((rows, d), rev(0)), pl.BlockSpec((rows, d), rev(0)),
                          pl.BlockSpec((nh, c, c), lambda i: (0, 0, 0)), pl.BlockSpec((c, D_RET), full2),
                          pl.BlockSpec((c, D_RET), full2), pl.BlockSpec((1, D_RET), full2),
                          pl.BlockSpec((rows, D_RET), rev(0)),
                          pl.BlockSpec((per, nh, d, d), lambda i: (n_steps - 1 - i, 0, 0, 0))],
                out_specs=[pl.BlockSpec((rows, D_IN), rev(0)), pl.BlockSpec((SUBLANES, D_RET), full2)],
                out_shape=[jax.ShapeDtypeStruct((t, D_IN), MXU_DTYPE), jax.ShapeDtypeStruct((SUBLANES, D_RET), F32)],
                scratch_shapes=[pltpu.VMEM((nh, d, d), F32)],
                operands=[proj, proj, cos2, sin_signed, decay, xi, zeta, gain, o, states])


LRU_ACC = {"conv_w": 0, "conv_b": LRU_CONV, "gate_a_b": LRU_CONV + 1, "gate_x_b": LRU_CONV + 2,
           "lambda": LRU_CONV + 3, "norm_gain": LRU_CONV + 4}
LRU_ACC_ROWS = SUBLANES * (LRU_CONV + 5)


def _lru_bwd(proj, xc_all, h_all, conv_w, wa, ba, wx, bx, lam, gain, dproj_part, dmix_at):
    t = proj.shape[0]
    tm = min(ROW_TILE, t)
    c = D_LRU
    ni = t // tm

    def body(x_ref, xh_ref, g_ref, xc_ref, h_ref, hh_ref, cw_ref, wa_ref, ba_ref, wx_ref, bx_ref, lam_ref,
             gain_ref, acc_ref, dwa_ref, dwx_ref, a_scr, b_scr, mu_scr, carry_mu, carry_dxc, ctx):
        dp_ref = ctx["outs"][dproj_part][0]
        dmix = ctx["scratch"][dmix_at[0]][dmix_at[1]]
        fill = ctx["fill"]
        i = pl.program_id(0)
        r = ni - 1 - i

        @pl.when(i == 0)
        def _():
            acc_ref[...] = jnp.zeros_like(acc_ref)
            dwa_ref[...] = jnp.zeros_like(dwa_ref)
            dwx_ref[...] = jnp.zeros_like(dwx_ref)
            carry_mu[...] = jnp.zeros_like(carry_mu)
            carry_dxc[...] = jnp.zeros_like(carry_dxc)

        def add(name, val, k=0):
            lo = (LRU_ACC[name] + k) * SUBLANES
            acc_ref[lo:lo + SUBLANES, :] += _colsum8(val)

        fill()
        xc, h = xc_ref[...], h_ref[...]
        lam_v = lam_ref[...]
        sp = _softplus(-lam_v)
        rg, ig, a, m = _lru_gates(xc, wa_ref[...], ba_ref[...], wx_ref[...], bx_ref[...], sp, fill)
        gl, dgl = _gelu_parts(g_ref[...].astype(F32))
        fill()
        zn, rstd, _ = _rms_fwd(h * gl, gain_ref[...])
        dy = dmix[:, :c]
        dz, dgain = _rms_bwd(dy, zn, rstd, gain_ref[...])
        lo = LRU_ACC["norm_gain"] * SUBLANES
        acc_ref[lo:lo + SUBLANES, :] += dgain
        dp_ref[:, c:2 * c] = (dz * h * dgl).astype(dp_ref.dtype)
        dh = dz * gl
        fill()
        ga, gb = _group_scan(a, a * dh, reverse=True, fill=fill)
        a_scr[...] = ga
        b_scr[...] = gb
        mu_next_tile = carry_mu[...]
        carry_mu[...] = _carry_scan(a_scr, b_scr, mu_scr, mu_next_tile, reverse=True)
        fill()
        lam_t = dh + _shift_up(mu_scr[...], mu_next_tile, 1)
        h_prev = _shift_down(jnp.where(r == 0, 0.0, hh_ref[...]), h, 1)
        da = lam_t * h_prev
        dig = lam_t * m * xc
        dxc = lam_t * m * ig
        dlog_a = da * a - (lam_t * ig * xc) * (a * a) / m
        fill()
        dpr = dlog_a * ((-LRU_C) * sp) * rg * (1.0 - rg)
        add("lambda", dlog_a * ((-LRU_C) * rg) * (-_sigmoid(-lam_v)))
        dpi = dig * ig * (1.0 - ig)
        add("gate_a_b", dpr)
        add("gate_x_b", dpi)
        fill()
        dwa_ref[...] += _dot_tn(xc, dpr)
        dwx_ref[...] += _dot_tn(xc, dpi)
        dxc = dxc + _dot_nt(dpr, wa_ref[...]) + _dot_nt(dpi, wx_ref[...])
        fill()
        add("conv_b", dxc)
        x = x_ref[...].astype(F32)
        prev = jnp.where(r == 0, 0.0, xh_ref[...].astype(F32)[-SUBLANES:, :])
        cw = cw_ref[...]
        nxt = carry_dxc[...]
        carry_dxc[...] = dxc[:SUBLANES, :]
        dx = cw[LRU_CONV - 1:LRU_CONV, :] * dxc
        for k in range(LRU_CONV - 1):
            dx = dx + cw[k:k + 1, :] * _shift_up(dxc, nxt, LRU_CONV - 1 - k)
        fill()
        for k in range(LRU_CONV):
            add("conv_w", dxc * _shift_down(prev, x, LRU_CONV - 1 - k), k)
        dp_ref[:, :c] = dx.astype(dp_ref.dtype)

    hb = _halo_rows(proj.dtype)
    rev = lambda col: (lambda i: (ni - 1 - i, col))
    halo = lambda rows: (lambda i: (jnp.maximum((ni - 1 - i) * (tm // rows) - 1, 0), 0))
    full = lambda i: (0, 0)
    vec = pl.BlockSpec((1, c), full)
    mat = pl.BlockSpec((c, c), full)
    return dict(body=body, grid=(ni,), fill_points=12,
                in_specs=[pl.BlockSpec((tm, c), rev(0)), pl.BlockSpec((hb, c), halo(hb)), pl.BlockSpec((tm, c), rev(1)),
                          pl.BlockSpec((tm, c), rev(0)), pl.BlockSpec((tm, c), rev(0)),
                          pl.BlockSpec((SUBLANES, c), halo(SUBLANES)),
                          pl.BlockSpec((LRU_CONV, c), full), mat, vec, mat, vec, vec, vec],
                out_specs=[pl.BlockSpec((LRU_ACC_ROWS, c), full), mat, mat],
                out_shape=[jax.ShapeDtypeStruct((LRU_ACC_ROWS, c), F32), jax.ShapeDtypeStruct((c, c), F32),
                           jax.ShapeDtypeStruct((c, c), F32)],
                scratch_shapes=[pltpu.VMEM((tm, c), F32), pltpu.VMEM((tm, c), F32), pltpu.VMEM((tm, c), F32),
                                pltpu.VMEM((SUBLANES, c), F32), pltpu.VMEM((SUBLANES, c), F32)],
                operands=[proj, proj, proj, xc_all, h_all, h_all, conv_w, wa, ba, wx, bx, lam, gain])


def _mix_proj_bwd(dh1, dh1_b, w_out, w_in_blocks, x, g1, dproj_part):
    t = x.shape[0]
    tm = min(ROW_TILE, t)
    ni = t // tm
    nb, _, cb = w_in_blocks.shape
    first_free = -(-2 * D_LRU // cb)
    du = [None]

    def term(dp_ref, w_ref, d):
        part = _dot_nt(dp_ref[:, d * cb:(d + 1) * cb], w_ref[d])
        du[0] = part if du[0] is None else du[0] + part

    def head(dh_ref, dhb_ref, wo_ref, wi_ref, x_ref, g_ref, gx_ref, dg_ref, dmix, ctx):
        @pl.when(pl.program_id(0) == 0)
        def _():
            dg_ref[...] = jnp.zeros_like(dg_ref)
        dmix[...] = _dot_nt(dhb_ref[...], wo_ref[...])
        du[0] = None

    def units(dh_ref, dhb_ref, wo_ref, wi_ref, x_ref, g_ref, gx_ref, dg_ref, dmix, ctx):
        dp_ref = ctx["outs"][dproj_part][0]
        return [lambda d=d: term(dp_ref, wi_ref, d) for d in range(first_free, nb)]

    def tail(dh_ref, dhb_ref, wo_ref, wi_ref, x_ref, g_ref, gx_ref, dg_ref, dmix, ctx):
        dp_ref = ctx["outs"][dproj_part][0]
        for d in range(first_free):
            term(dp_ref, wi_ref, d)
        n, rstd, _ = _rms_fwd(x_ref[...], g_ref[...])
        dx, dg = _rms_bwd(du[0], n, rstd, g_ref[...])
        dg_ref[...] += dg
        gx_ref[...] = dx + dh_ref[...]

    row = lambda i: (ni - 1 - i, 0)
    const = lambda i: (0, 0)
    tile = pl.BlockSpec((tm, D_MODEL), row)
    return dict(head=head, units=units, tail=tail, grid=(ni,),
                in_specs=[tile, tile, _resident(w_out.shape), _resident(w_in_blocks.shape), tile,
                          pl.BlockSpec((1, D_MODEL), const)],
                out_specs=[tile, pl.BlockSpec((SUBLANES, D_MODEL), const)],
                out_shape=[jax.ShapeDtypeStruct((t, D_MODEL), F32), jax.ShapeDtypeStruct((SUBLANES, D_MODEL), F32)],
                scratch_shapes=[pltpu.VMEM((tm, D_MODEL), F32)],
                operands=[dh1, dh1_b, w_out, w_in_blocks, x, g1])


def _pair_sum(core, a, b, name):
    n, r, c = b.shape
    spec = pl.BlockSpec((None, r, c), lambda q, core: (q, 0, 0))

    def body(core_ref, a_ref, b_ref, o_ref):
        o_ref[...] = (a_ref[...].astype(F32) + b_ref[...].astype(F32)).astype(o_ref.dtype)

    return pl.pallas_call(
        body, name=name,
        grid_spec=pltpu.PrefetchScalarGridSpec(
            num_scalar_prefetch=1, grid=(n,),
            in_specs=[pl.BlockSpec((None, r, c), lambda q, core: (2 * q + core[0], 0, 0)), spec], out_specs=spec),
        out_shape=jax.ShapeDtypeStruct(b.shape, b.dtype),
        compiler_params=pltpu.CompilerParams(dimension_semantics=("arbitrary",), vmem_limit_bytes=VMEM_LIMIT),
    )(core, a, b)


ADAMW_BLOCK_BYTES = 4 * 1024 * 1024


def _sum_adamw(parts, w, m, v, name):
    n_parts, r, c = parts.shape
    tr = r
    while n_parts * tr * c * parts.dtype.itemsize > ADAMW_BLOCK_BYTES and tr % (4 * SUBLANES) == 0:
        tr //= 2

    def body(p_ref, w_ref, m_ref, v_ref, g_ref, d_ref, nm_ref, nv_ref):
        g = p_ref[0].astype(F32)
        for s in range(1, n_parts):
            g = g + p_ref[s].astype(F32)
        nm = ADAM_B1 * m_ref[...] + (1.0 - ADAM_B1) * g
        nv = ADAM_B2 * v_ref[...] + (1.0 - ADAM_B2) * (g * g)
        m_hat = nm / (1.0 - ADAM_B1 ** ADAM_STEP)
        v_hat = nv / (1.0 - ADAM_B2 ** ADAM_STEP)
        g_ref[...] = g
        d_ref[...] = -ADAM_LR * (m_hat / (jnp.sqrt(v_hat) + ADAM_EPS) + ADAM_WD * w_ref[...])
        nm_ref[...] = nm
        nv_ref[...] = nv

    row = pl.BlockSpec((tr, c), lambda i: (i, 0))
    return _call(body, name=name, grid=(r // tr,),
                 in_specs=[pl.BlockSpec((n_parts, tr, c), lambda i: (0, i, 0)), row, row, row],
                 out_specs=[row, row, row, row], out_shape=[jax.ShapeDtypeStruct((r, c), F32)] * 4,
                 operands=[parts, w, m, v])


MATRICES = ("w_in", "w_out", "ffn_up_w", "ffn_down_w")
CONVS = ("lru_conv_w", "ffn_conv_w")
REPLICATED = ("norm1_gain", "lru_conv_b", "lru_gate_a_w", "lru_gate_a_b", "lru_gate_x_w", "lru_gate_x_b", "lru_lambda",
              "lru_norm_gain", "ret_norm_gain", "norm2_gain", "ffn_conv_b", "final_norm_gain")
WEIGHTS = ("norm1_gain", "w_in", "lru_conv_w", "lru_conv_b", "lru_gate_a_w", "lru_gate_a_b", "lru_gate_x_w",
           "lru_gate_x_b", "lru_lambda", "lru_norm_gain", "ret_norm_gain", "w_out", "norm2_gain", "ffn_up_w",
           "ffn_conv_w", "ffn_conv_b", "ffn_down_w", "final_norm_gain")


def _rows(a, pad_to):
    a = a.reshape(-1, LANES)
    pad = (-a.shape[0]) % pad_to
    return jnp.pad(a, ((0, pad), (0, 0))) if pad else a


def _pack(arrays, pad_to):
    rows, layout, at = [], [], 0
    for a in arrays:
        r = _rows(a, pad_to)
        layout.append((at, a.size // LANES, a.shape))
        rows.append(r)
        at += r.shape[0]
    return jnp.concatenate(rows, axis=0), layout


def _unpack(packed, layout):
    lead = packed.shape[:-2]
    return [packed[..., at:at + n, :].reshape(lead + shape) for at, n, shape in layout]


def _conv_rows(lru, ffn, dtype, pad_to):
    lead = lru.shape[:-2]
    flat = jnp.concatenate([lru.reshape(lead + (-1,)), ffn.reshape(lead + (-1,))], axis=-1).astype(dtype)
    rows = flat.shape[-1] // LANES
    pad = (-rows) % pad_to
    return jnp.pad(flat.reshape(lead + (rows, LANES)), [(0, 0)] * len(lead) + [(0, pad), (0, 0)])


def _column_blocks(full):
    r, c = full.shape
    return full.reshape(r, N_DEV, c // N_DEV).transpose(1, 0, 2)


def _block_diag(w):
    nh, d, _ = w.shape
    eye = jnp.eye(nh, dtype=w.dtype)
    return (w[:, :, None, :] * eye[:, None, :, None]).reshape(nh * d, nh * d)


def _diag_blocks(dense, nh):
    d = dense.shape[0] // nh
    blocks = dense.reshape(nh, d, nh, d)
    return jnp.stack([blocks[h, :, h, :] for h in range(nh)], axis=0)


def kernel(x, norm1_gain, w_in, lru_conv_w, lru_conv_b, lru_gate_a_w, lru_gate_a_b, lru_gate_x_w, lru_gate_x_b, lru_lambda, lru_norm_gain, ret_norm_gain, w_out, norm2_gain, ffn_up_w, ffn_conv_w, ffn_conv_b, ffn_down_w, final_norm_gain, loss_target, m_norm1_gain, m_w_in, m_lru_conv_w, m_lru_conv_b, m_lru_gate_a_w, m_lru_gate_a_b, m_lru_gate_x_w, m_lru_gate_x_b, m_lru_lambda, m_lru_norm_gain, m_ret_norm_gain, m_w_out, m_norm2_gain, m_ffn_up_w, m_ffn_conv_w, m_ffn_conv_b, m_ffn_down_w, m_final_norm_gain, v_norm1_gain, v_w_in, v_lru_conv_w, v_lru_conv_b, v_lru_gate_a_w, v_lru_gate_a_b, v_lru_gate_x_w, v_lru_gate_x_b, v_lru_lambda, v_lru_norm_gain, v_ret_norm_gain, v_w_out, v_norm2_gain, v_ffn_up_w, v_ffn_conv_w, v_ffn_conv_b, v_ffn_down_w, v_final_norm_gain):
    args = dict(locals())
    given = {n: args[n] for n in WEIGHTS}
    out_shape = {n: given[n].shape for n in WEIGHTS}

    def plain(a):
        return a.reshape(1, -1) if a.ndim <= 2 else a[0]

    w = {n: plain(given[n]) for n in WEIGHTS}
    mom_m = {n: plain(args["m_" + n]) for n in WEIGHTS}
    mom_v = {n: plain(args["v_" + n]) for n in WEIGHTS}
    x2, target = x[0], loss_target[0]
    t = x2.shape[0]
    core = lax.axis_index("c").astype(jnp.int32).reshape(1)
    res = {}

    conv_pad = _conv_rows(w["lru_conv_w"], w["ffn_conv_w"], F32, SUBLANES)
    first = _gather_first([w["w_in"].astype(MXU_DTYPE), conv_pad])
    w_in_blocks, conv_all = _run_comms([first, _gather_second(first.out_shape)], "w_in_all_gather")
    n_lru = w["lru_conv_w"].size
    conv_flat = conv_all.reshape(N_DEV, -1)
    lru_cw = conv_flat[:, :n_lru].reshape((N_DEV,) + w["lru_conv_w"].shape).transpose(1, 0, 2).reshape(LRU_CONV, D_LRU)
    ffn_cw = conv_flat[:, n_lru:n_lru + w["ffn_conv_w"].size].reshape((N_DEV,) + w["ffn_conv_w"].shape)
    ffn_cw = ffn_cw.transpose(1, 0, 2).reshape(FFN_CONV, 2 * D_FF)

    cos2, sin_signed = _rope_tables(t)
    wa = _block_diag(w["lru_gate_a_w"]).astype(MXU_DTYPE)
    wx = _block_diag(w["lru_gate_x_w"]).astype(MXU_DTYPE)
    gf = w["final_norm_gain"]

    early = _gather_first([w["w_out"].astype(MXU_DTYPE), w["ffn_down_w"].astype(MXU_DTYPE)])
    (u1, proj), (w_out_part, down_part) = _inproj_fwd(x2, w["norm1_gain"], w_in_blocks, early)
    ((xc, h_lru, y_lru), (o_ret, y_ret, states)), (w_out_blocks, down_blocks, up_part) = _fused(
        [_lru_fwd(proj, lru_cw, w["lru_conv_b"], wa, w["lru_gate_a_b"], wx, w["lru_gate_x_b"], w["lru_lambda"],
                  w["lru_norm_gain"]),
         _ret_fwd(proj, cos2, sin_signed, w["ret_norm_gain"])],
        "mix_fwd", _both(_gather_second([w_out_part, down_part]), _gather_first([w["ffn_up_w"].astype(MXU_DTYPE)])))
    w_out_full = w_out_blocks.reshape(D_MODEL, D_MODEL)
    w_down_full = down_blocks.reshape(D_FF, D_MODEL)

    (h1, u2), (up_blocks,) = _outproj_fwd(x2, y_lru, y_ret, w_out_full, w["norm2_gain"], _gather_second([up_part]))
    up_a, up_v, conv_a, conv_v, act, dh2, dh2_b, dgf, loss_local = _ffn_fwd(u2, up_blocks, ffn_cw, w["ffn_conv_b"],
                                                                            w_down_full, h1, gf, target)

    def to_owner_chips(blocks, names, tag):
        theirs = _run_comms([_pair_exchange(blocks)], "grads_pair_exchange_" + tag)
        return [_pair_sum(core, a, b, "grads_pair_sum_" + n) for n, a, b in zip(names, blocks, theirs)]

    def adamw(name, parts):
        res[name] = _sum_adamw(parts, w[name], mom_m[name], mom_v[name], "adamw_" + name)

    g = {"final_norm_gain": dgf[0]}
    dup_a, dup_v, acc_a, acc_v, dh1, dh1_b, dg2 = _ffn_bwd(
        dh2, dh2_b, w_down_full, up_a, up_v, conv_a, conv_v, ffn_cw, up_blocks, h1, w["norm2_gain"], None)
    per_col = lambda a: a[:, ::SUBLANES].transpose(1, 0, 2).reshape(FFN_CONV + 1, D_FF)
    acc = jnp.concatenate([per_col(acc_a), per_col(acc_v)], axis=1)
    g_ffn_cw, g["ffn_conv_b"] = acc[:FFN_CONV], acc[FFN_CONV:]
    g["norm2_gain"] = dg2[:1]
    g_up = jnp.concatenate([_mm_tn(u2, dup_a, "ffn_up_wgrad_a", blocks=N_DEV // 2),
                            _mm_tn(u2, dup_v, "ffn_up_wgrad_v", blocks=N_DEV // 2)], axis=0)
    up_sums = to_owner_chips([g_up], ["ffn_up_w"], "up")
    g_down, (up_parts,) = _mm_tn(act, dh2_b, "ffn_down_wgrad", comm=_chip_exchange(up_sums))
    adamw("ffn_up_w", up_parts)
    g_out = jnp.concatenate([_mm_tn(y_lru, dh1_b, "w_out_wgrad_lru"), _mm_tn(y_ret, dh1_b, "w_out_wgrad_ret")], axis=0)
    low_sums = to_owner_chips([g_down.reshape(N_DEV, D_FF // N_DEV, D_MODEL),
                               g_out.reshape(N_DEV, D_MODEL // N_DEV, D_MODEL)], ["ffn_down_w", "w_out"], "low")
    (dproj, dgain_ret), (grad_x, dg1), (lru_acc, dwa, dwx) = _fused(
        [_ret_bwd(proj, cos2, sin_signed, w["ret_norm_gain"], o_ret, states, dmix_at=(1, 0)),
         _mix_proj_bwd(dh1, dh1_b, w_out_full, w_in_blocks, x2, w["norm1_gain"], dproj_part=0),
         _lru_bwd(proj, xc, h_lru, lru_cw, wa, w["lru_gate_a_b"], wx, w["lru_gate_x_b"], w["lru_lambda"],
                  w["lru_norm_gain"], dproj_part=0, dmix_at=(1, 0))],
        "mix_bwd")
    g["norm1_gain"] = dg1[:1]
    g["ret_norm_gain"] = dgain_ret[:1]
    lru_acc = lru_acc[::SUBLANES]
    g_lru_cw = lru_acc[:LRU_CONV]
    for name in ("conv_b", "gate_a_b", "gate_x_b", "lambda", "norm_gain"):
        g["lru_" + name] = lru_acc[LRU_ACC[name]:LRU_ACC[name] + 1]
    g["lru_gate_a_w"] = _diag_blocks(dwa, LRU_HEADS)
    g["lru_gate_x_w"] = _diag_blocks(dwx, LRU_HEADS)
    rep_packed, rep_layout = _pack([g[n] for n in REPLICATED] + [loss_local], SUBLANES)
    g_in, (down_parts, out_parts, rep_part) = _mm_tn(u1, dproj, "w_in_wgrad", blocks=N_DEV,
                                                     comm=_both(_chip_exchange(low_sums), _gather_first([rep_packed])))
    adamw("ffn_down_w", down_parts)
    adamw("w_out", out_parts)
    g_conv = _conv_rows(_column_blocks(g_lru_cw), _column_blocks(g_ffn_cw), GRAD_DTYPE, 2 * SUBLANES)
    in_sums = to_owner_chips([g_in, g_conv], ["w_in", "conv"], "in")
    in_parts, conv_parts, rep_parts = _run_comms([_both(_chip_exchange(in_sums), _gather_second([rep_part]))],
                                                 "last_grads_exchange")
    adamw("w_in", in_parts)
    pad16 = lambda d: _conv_rows(d["lru_conv_w"], d["ffn_conv_w"], F32, 2 * SUBLANES)
    conv_res = _sum_adamw(conv_parts, pad16(w), pad16(mom_m), pad16(mom_v), "adamw_conv")
    for n, lo, hi in (("lru_conv_w", 0, n_lru), ("ffn_conv_w", n_lru, n_lru + w["ffn_conv_w"].size)):
        res[n] = [r.reshape(-1)[lo:hi].reshape(w[n].shape) for r in conv_res]
    no_state = jnp.zeros_like(loss_local)
    rep_res = _sum_adamw(rep_parts, *[_pack([d[n] for n in REPLICATED] + [no_state], SUBLANES)[0]
                                      for d in (w, mom_m, mom_v)], "adamw_replicated")
    for k in range(4):
        for n, a in zip(REPLICATED, _unpack(rep_res[k], rep_layout)):
            res.setdefault(n, [None] * 4)[k] = a
    loss = _unpack(rep_res[0], rep_layout)[-1][0, 0]

    outs = [loss, grad_x[None]]
    for k in range(4):
        outs += [res[n][k].reshape(out_shape[n]) for n in WEIGHTS]
    return tuple(outs)
```

```python
import math

import numpy as np
import jax
import jax.numpy as jnp
from jax import lax
from jax.experimental import pallas as pl
from jax.experimental.pallas import tpu as pltpu

F32 = jnp.float32
BF16 = jnp.bfloat16
MXU_DTYPE = jnp.bfloat16
GRAD_DTYPE = jnp.bfloat16

N_DEV = 8
N_CHIPS = 4
D_MODEL = 1024
D_LRU = 512
LRU_HEADS = 8
LRU_CONV = 4
LRU_C = 8.0
D_RET = 512
RET_HEADS = 4
RET_HEAD_DIM = 128
RET_CHUNK = 128
ROPE_BASE = 10000.0
D_IN = 3072
D_FF = 3072
FFN_CONV = 3
NORM_EPS = 1e-6

ADAM_LR = 0.001
ADAM_B1 = 0.9
ADAM_B2 = 0.999
ADAM_EPS = 1e-08
ADAM_WD = 0.01
ADAM_STEP = 10

SUBLANES = 8
LANES = 128
VMEM_LIMIT = 48 * 1024 * 1024
FFN_BWD_VMEM_LIMIT = 56 * 1024 * 1024

ROW_TILE = 256
PROJ_ROW_TILE = 512
WGRAD_ROWS = 2048
WGRAD_TILE = 1024
WGRAD_BLOCK_COLUMNS = 768

MESH = pl.DeviceIdType.MESH
ANY = pl.BlockSpec(memory_space=pl.ANY)


def _dot(a, b):
    return jnp.dot(a.astype(MXU_DTYPE), b.astype(MXU_DTYPE), preferred_element_type=F32)


def _dot_nt(a, b):
    return lax.dot_general(a.astype(MXU_DTYPE), b.astype(MXU_DTYPE), (((1,), (1,)), ((), ())),
                           preferred_element_type=F32)


def _dot_tn(a, b):
    return lax.dot_general(a.astype(MXU_DTYPE), b.astype(MXU_DTYPE), (((0,), (0,)), ((), ())),
                           preferred_element_type=F32)


def _sigmoid(x):
    return 0.5 + 0.5 * jnp.tanh(0.5 * x)


_GELU_C = math.sqrt(2.0 / math.pi)
_GELU_C3 = _GELU_C * 0.044715


def _gelu_parts(x):
    x2 = x * x
    t = jnp.tanh(x * (_GELU_C + _GELU_C3 * x2))
    cdf = 0.5 + 0.5 * t
    g = x * cdf
    dg = cdf + (0.5 * x) * (1.0 - t * t) * (_GELU_C + (3.0 * _GELU_C3) * x2)
    return g, dg


def _gelu(x):
    t = jnp.tanh(_GELU_C * (x + 0.044715 * (x * x * x)))
    return x * (0.5 * (1.0 + t))


def _softplus(x):
    return jnp.maximum(x, 0.0) + jnp.log1p(jnp.exp(-jnp.abs(x)))


def _bcast_row(x, r, rows=SUBLANES):
    return jnp.broadcast_to(x[r:r + 1, :], (rows, x.shape[1]))


def _colsum8(x):
    return jnp.broadcast_to(jnp.sum(x, axis=0, keepdims=True), (SUBLANES, x.shape[1]))


def _shift_down(prev8, tile, s):
    if s == 0:
        return tile
    ext = jnp.concatenate([prev8, tile], axis=0)
    return pltpu.roll(ext, s, 0)[SUBLANES:, :]


def _shift_up(tile, next8, s):
    if s == 0:
        return tile
    ext = jnp.concatenate([tile, next8], axis=0)
    return pltpu.roll(ext, SUBLANES - s, 0)[SUBLANES:, :]


def _group_scan(a, b, reverse, fill=lambda: None):
    n = a.shape[0]
    row = lax.broadcasted_iota(jnp.int32, a.shape, 0) & (SUBLANES - 1)
    for s in (1, 2, 4):
        if s > 1:
            fill()
        shift = (n - s) if reverse else s
        a_sh = pltpu.roll(a, shift, 0)
        b_sh = pltpu.roll(b, shift, 0)
        m = (row <= SUBLANES - 1 - s) if reverse else (row >= s)
        b = jnp.where(m, a * b_sh + b, b)
        a = jnp.where(m, a * a_sh, a)
    return a, b


def _carry_scan(a_ref, b_ref, out_ref, carry0, reverse):
    n_groups = a_ref.shape[0] // SUBLANES
    carry = carry0
    for i in range(n_groups):
        r0 = ((n_groups - 1 - i) if reverse else i) * SUBLANES
        hg = a_ref[r0:r0 + SUBLANES, :] * carry + b_ref[r0:r0 + SUBLANES, :]
        out_ref[r0:r0 + SUBLANES, :] = hg
        carry = _bcast_row(hg, 0 if reverse else SUBLANES - 1)
    return carry


def _rms_fwd(h, gain):
    rstd = lax.rsqrt(jnp.mean(h * h, axis=-1, keepdims=True) + NORM_EPS)
    n = h * rstd
    return n, rstd, n * gain


def _rms_bwd(dy, n, rstd, gain):
    dn = dy * gain
    dh = rstd * (dn - n * jnp.mean(dn * n, axis=-1, keepdims=True))
    return dh, _colsum8(dy * n)


def _halo_rows(dtype):
    return SUBLANES * (4 // jnp.dtype(dtype).itemsize)


def _halo_map(tile_rows, col, halo_rows=SUBLANES):
    per = tile_rows // halo_rows
    return lambda i: (jnp.maximum(i * per - 1, 0), col)


def _resident(shape):
    return pl.BlockSpec(shape, lambda *_: (0,) * len(shape), pipeline_mode=pl.Buffered(1))


def _place():
    x, y, c = lax.axis_index("x"), lax.axis_index("y"), lax.axis_index("c")
    chips = [(1 - x, y), (x, 1 - y), (1 - x, 1 - y)]
    return x, y, c, chips


def _dev(x, y, c):
    return 4 * x + 2 * y + c


class _Copy:
    def __init__(self, make):
        self.make = make

    def start(self):
        self.make().start()

    def wait(self):
        self.make().wait()

    def wait_send(self):
        self.make().wait_send()

    def wait_recv(self):
        self.make().wait_recv()


def _remote(src, dst, send_sem, recv_sem, to):
    return _Copy(lambda: pltpu.make_async_remote_copy(src_ref=src, dst_ref=dst, send_sem=send_sem, recv_sem=recv_sem,
                                                      device_id=to, device_id_type=MESH))


def _local(src, dst, sem):
    return _Copy(lambda: pltpu.make_async_copy(src, dst, sem))


class _Comm:
    def __init__(self, operands, out_shape, sems, descs, aliases=()):
        self.operands, self.out_shape, self.sems, self.descs, self.aliases = operands, out_shape, sems, descs, aliases

    def start(self, ins, outs, sems):
        local, sends, _ = self.descs(ins, outs, sems)
        for cp in sends + local:
            cp.start()

    def wait(self, ins, outs, sems):
        local, sends, recvs = self.descs(ins, outs, sems)
        for cp in recvs:
            cp.wait_recv()
        for cp in sends:
            cp.wait_send()
        for cp in local:
            cp.wait()


def _gather_first(shards):
    n = len(shards)

    def descs(ins, outs, sems):
        send, recv, loc = sems
        x, y, c, chips = _place()
        me = _dev(x, y, c)
        targets = [(x, y, 1 - c)] + [(*chip, c) for chip in chips]
        local, sends, recvs = [], [], []
        for t in range(n):
            local.append(_local(ins[t], outs[t].at[me], loc.at[t]))
            for k, to in enumerate(targets):
                i = 4 * t + k
                sends.append(_remote(ins[t], outs[t].at[me], send.at[i], recv.at[i], to))
                recvs.append(_remote(ins[t], outs[t].at[_dev(*to)], send.at[i], recv.at[i], to))
        return local, sends, recvs

    return _Comm(list(shards), [jax.ShapeDtypeStruct((N_DEV,) + s.shape, s.dtype) for s in shards],
                 [pltpu.SemaphoreType.DMA((4 * n,)), pltpu.SemaphoreType.DMA((4 * n,)), pltpu.SemaphoreType.DMA((n,))],
                 descs)


def _gather_second(gathered):
    n = len(gathered)

    def descs(ins, outs, sems):
        send, recv = sems
        x, y, c, chips = _place()
        sends, recvs = [], []
        for t in range(n):
            for j, chip in enumerate(chips):
                i = 3 * t + j
                have, get = _dev(*chip, c), _dev(*chip, 1 - c)
                sends.append(_remote(outs[t].at[have], outs[t].at[have], send.at[i], recv.at[i], (x, y, 1 - c)))
                recvs.append(_remote(outs[t].at[have], outs[t].at[get], send.at[i], recv.at[i], (x, y, 1 - c)))
        return [], sends, recvs

    return _Comm(list(gathered), [jax.ShapeDtypeStruct(g.shape, g.dtype) for g in gathered],
                 [pltpu.SemaphoreType.DMA((3 * n,)), pltpu.SemaphoreType.DMA((3 * n,))], descs,
                 aliases=[(t, t) for t in range(n)])


def _pair_exchange(blocks):
    n = len(blocks)

    def descs(ins, outs, sems):
        send, recv = sems
        x, y, c, _ = _place()
        sends, recvs = [], []
        for t in range(n):
            for q in range(N_CHIPS):
                i = N_CHIPS * t + q
                cp = _remote(ins[t].at[2 * q + 1 - c], outs[t].at[q], send.at[i], recv.at[i], (x, y, 1 - c))
                sends.append(cp)
                recvs.append(cp)
        return [], sends, recvs

    return _Comm(list(blocks), [jax.ShapeDtypeStruct((N_CHIPS,) + b.shape[1:], b.dtype) for b in blocks],
                 [pltpu.SemaphoreType.DMA((N_CHIPS * n,)), pltpu.SemaphoreType.DMA((N_CHIPS * n,))], descs)


def _chip_exchange(blocks):
    n = len(blocks)

    def descs(ins, outs, sems):
        send, recv, loc = sems
        x, y, c, chips = _place()
        me = 2 * x + y
        local, sends, recvs = [], [], []
        for t in range(n):
            local.append(_local(ins[t].at[me], outs[t].at[me], loc.at[t]))
            for j, (px, py) in enumerate(chips):
                i = 3 * t + j
                q = 2 * px + py
                sends.append(_remote(ins[t].at[q], outs[t].at[me], send.at[i], recv.at[i], (px, py, c)))
                recvs.append(_remote(ins[t].at[q], outs[t].at[q], send.at[i], recv.at[i], (px, py, c)))
        return local, sends, recvs

    return _Comm(list(blocks), [jax.ShapeDtypeStruct(b.shape, b.dtype) for b in blocks],
                 [pltpu.SemaphoreType.DMA((3 * n,)), pltpu.SemaphoreType.DMA((3 * n,)), pltpu.SemaphoreType.DMA((n,))],
                 descs)


def _both(a, b):
    na, oa, sa = len(a.operands), len(a.out_shape), len(a.sems)

    def descs(ins, outs, sems):
        local_a, sends_a, recvs_a = a.descs(ins[:na], outs[:oa], sems[:sa])
        local_b, sends_b, recvs_b = b.descs(ins[na:], outs[oa:], sems[sa:])
        return local_a + local_b, sends_a + sends_b, recvs_a + recvs_b

    return _Comm(a.operands + b.operands, a.out_shape + b.out_shape, a.sems + b.sems, descs,
                 aliases=list(a.aliases) + [(na + i, oa + o) for i, o in b.aliases])


def _run_comms(comms, name):
    first = comms[0]
    n_in, n_out = len(first.operands), len(first.out_shape)

    def body(*refs):
        ins, outs, sems = refs[:n_in], refs[n_in:n_in + n_out], list(refs[n_in + n_out:])
        for k, comm in enumerate(comms):
            mine = [sems.pop(0) for _ in comm.sems]
            comm.start(ins if k == 0 else outs, outs, mine)
            comm.wait(ins if k == 0 else outs, outs, mine)

    outs = pl.pallas_call(
        body, name=name, out_shape=first.out_shape, in_specs=[ANY] * n_in, out_specs=[ANY] * n_out,
        scratch_shapes=[s for comm in comms for s in comm.sems], input_output_aliases=dict(first.aliases),
    )(*first.operands)
    return list(outs)


def _call(body, *, name, grid, in_specs, out_specs, out_shape, operands, scratch_shapes=(), comm=None, aliases=None,
          vmem_limit=VMEM_LIMIT):
    sem = ("arbitrary",) * len(grid)
    params = pltpu.CompilerParams(dimension_semantics=sem, vmem_limit_bytes=vmem_limit)
    aliases = dict(aliases or {})
    if comm is None:
        return pl.pallas_call(body, name=name, grid=grid, in_specs=in_specs, out_specs=out_specs, out_shape=out_shape,
                              scratch_shapes=list(scratch_shapes), input_output_aliases=aliases,
                              compiler_params=params)(*operands)
    n_in, n_out, n_scr = len(in_specs), len(out_specs), len(scratch_shapes)
    c_in, c_out = len(comm.operands), len(comm.out_shape)

    def wrapped(*refs):
        refs = list(refs)
        ins, refs = refs[:n_in], refs[n_in:]
        cins, refs = refs[:c_in], refs[c_in:]
        outs, refs = refs[:n_out], refs[n_out:]
        couts, refs = refs[:c_out], refs[c_out:]
        scr, csems = refs[:n_scr], refs[n_scr:]
        first = last = None
        for axis, size in enumerate(grid):
            at_first, at_last = pl.program_id(axis) == 0, pl.program_id(axis) == size - 1
            first = at_first if first is None else first & at_first
            last = at_last if last is None else last & at_last

        @pl.when(first)
        def _():
            comm.start(cins, couts, csems)

        body(*ins, *outs, *scr)

        @pl.when(last)
        def _():
            comm.wait(cins, couts, csems)

    res = pl.pallas_call(
        wrapped, name=name, grid=grid, in_specs=list(in_specs) + [ANY] * c_in, out_specs=list(out_specs) + [ANY] * c_out,
        out_shape=list(out_shape) + list(comm.out_shape), scratch_shapes=list(scratch_shapes) + list(comm.sems),
        input_output_aliases={**aliases, **{n_in + i: n_out + o for i, o in comm.aliases}}, compiler_params=params,
    )(*operands, *comm.operands)
    return list(res[:n_out]), list(res[n_out:])


def _mm_tn(a, b, name, blocks=1, comm=None):
    t, m = a.shape
    n = b.shape[1]
    tk = min(WGRAD_ROWS, t)
    nk = t // tk
    cb = n // blocks
    per = max(1, WGRAD_BLOCK_COLUMNS // cb) if blocks > 1 else 1
    tn = per * cb if blocks > 1 else min(WGRAD_TILE, n)
    tm = min(WGRAD_TILE, m)
    assert blocks == 1 or tm == m

    def body(a_ref, b_ref, o_ref, acc):
        k = pl.program_id(2)

        @pl.when(k == 0)
        def _():
            acc[...] = jnp.zeros_like(acc)
        acc[...] += _dot_tn(a_ref[...], b_ref[...])

        @pl.when(k == nk - 1)
        def _():
            if blocks == 1:
                o_ref[...] = acc[...].astype(o_ref.dtype)
            else:
                for s in range(per):
                    o_ref[s] = acc[:, s * cb:(s + 1) * cb].astype(o_ref.dtype)

    if blocks == 1:
        out_spec = pl.BlockSpec((tm, tn), lambda i, j, k: (i, j))
        out_shape = jax.ShapeDtypeStruct((m, n), GRAD_DTYPE)
    else:
        out_spec = pl.BlockSpec((per, m, cb), lambda i, j, k: (j, 0, 0))
        out_shape = jax.ShapeDtypeStruct((blocks, m, cb), GRAD_DTYPE)
    res = _call(body, name=name, grid=(m // tm, n // tn, nk), comm=comm,
                in_specs=[pl.BlockSpec((tk, tm), lambda i, j, k: (k, i)), pl.BlockSpec((tk, tn), lambda i, j, k: (k, j))],
                out_specs=[out_spec], out_shape=[out_shape], operands=[a, b],
                scratch_shapes=[pltpu.VMEM((tm, tn), F32)])
    return res[0] if comm is None else (res[0][0], res[1])


def _inproj_fwd(x, g1, w_blocks, comm):
    t = x.shape[0]
    tm = min(PROJ_ROW_TILE, t)
    nb, _, cb = w_blocks.shape

    def body(x_ref, g_ref, w_ref, u_ref, p_ref):
        _, _, u = _rms_fwd(x_ref[...], g_ref[...])
        u = u.astype(MXU_DTYPE)
        u_ref[...] = u
        for d in range(nb):
            p_ref[:, d * cb:(d + 1) * cb] = _dot(u, w_ref[d]).astype(p_ref.dtype)

    return _call(body, name="inproj_fwd", grid=(t // tm,), comm=comm,
                 in_specs=[pl.BlockSpec((tm, D_MODEL), lambda i: (i, 0)), pl.BlockSpec((1, D_MODEL), lambda i: (0, 0)),
                           _resident(w_blocks.shape)],
                 out_specs=[pl.BlockSpec((tm, D_MODEL), lambda i: (i, 0)), pl.BlockSpec((tm, D_IN), lambda i: (i, 0))],
                 out_shape=[jax.ShapeDtypeStruct((t, D_MODEL), MXU_DTYPE), jax.ShapeDtypeStruct((t, D_IN), MXU_DTYPE)],
                 operands=[x, g1, w_blocks])


def _lru_gates(xc, wa, ba, wx, bx, sp, fill=lambda: None):
    r = _sigmoid(_dot(xc, wa) + ba)
    fill()
    ig = _sigmoid(_dot(xc, wx) + bx)
    fill()
    log_a = (-LRU_C) * r * sp
    a = jnp.exp(log_a)
    m = jnp.sqrt(-jnp.tanh(log_a) * (a * a + 1.0))
    return r, ig, a, m


def _fused(parts, name, comm=None):
    grid = parts[0]["grid"]
    assert all(p["grid"] == grid for p in parts)
    counts = [(len(p["in_specs"]), len(p["out_specs"]), len(p.get("scratch_shapes", ()))) for p in parts]

    def body(*refs):
        refs = list(refs)
        groups = []
        for kind in range(3):
            taken = []
            for c in counts:
                taken.append(refs[:c[kind]])
                refs = refs[c[kind]:]
            groups.append(taken)
        ins, outs, scr = groups
        pending = []

        def fill(n=None):
            for _ in range(share if n is None else n):
                if pending:
                    pending.pop(0)()

        ctx = dict(outs=outs, scratch=scr, fill=fill)
        run = lambda key: [p[key](*ins[k], *outs[k], *scr[k], ctx) for k, p in enumerate(parts) if key in p]
        run("head")
        for pieces in run("units"):
            pending.extend(pieces)
        points = sum(p.get("fill_points", 0) for p in parts)
        share = -(-len(pending) // max(points, 1))
        run("body")
        fill(len(pending))
        run("tail")

    cat = lambda key: [x for p in parts for x in p.get(key, ())]
    res = _call(body, name=name, grid=grid, comm=comm, in_specs=cat("in_specs"), out_specs=cat("out_specs"),
                out_shape=cat("out_shape"), scratch_shapes=cat("scratch_shapes"), operands=cat("operands"))
    outs, side = (res if comm is not None else (res, None))
    split, at = [], 0
    for _, n_out, _ in counts:
        split.append(list(outs[at:at + n_out]))
        at += n_out
    return split if comm is None else (split, side)


def _lru_fwd(proj, conv_w, conv_b, wa, ba, wx, bx, lam, gain):
    t = proj.shape[0]
    tm = min(ROW_TILE, t)
    c = D_LRU

    def body(x_ref, xh_ref, g_ref, cw_ref, cb_ref, wa_ref, ba_ref, wx_ref, bx_ref, lam_ref, gain_ref,
             xc_ref, h_ref, y_ref, a_scr, b_scr, carry, ctx):
        fill = ctx["fill"]
        i = pl.program_id(0)

        @pl.when(i == 0)
        def _():
            carry[...] = jnp.zeros_like(carry)

        fill()
        x = x_ref[...].astype(F32)
        prev = jnp.where(i == 0, 0.0, xh_ref[...].astype(F32)[-SUBLANES:, :])
        cw = cw_ref[...]
        xc = cb_ref[...] + cw[LRU_CONV - 1:LRU_CONV, :] * x
        for k in range(LRU_CONV - 1):
            xc = xc + cw[k:k + 1, :] * _shift_down(prev, x, LRU_CONV - 1 - k)
        xc_ref[...] = xc
        fill()
        sp = _softplus(-lam_ref[...])
        _, ig, a, m = _lru_gates(xc, wa_ref[...], ba_ref[...], wx_ref[...], bx_ref[...], sp, fill)
        fill()
        ga, gb = _group_scan(a, m * (ig * xc), reverse=False, fill=fill)
        a_scr[...] = ga
        b_scr[...] = gb
        fill()
        carry[...] = _carry_scan(a_scr, b_scr, h_ref, carry[...], reverse=False)
        fill()
        z = h_ref[...] * _gelu(g_ref[...].astype(F32))
        fill()
        _, _, y = _rms_fwd(z, gain_ref[...])
        y_ref[...] = y.astype(y_ref.dtype)

    row = lambda i: (i, 0)
    full = lambda i: (0, 0)
    vec = pl.BlockSpec((1, c), full)
    hb = _halo_rows(proj.dtype)
    return dict(body=body, grid=(t // tm,), fill_points=10,
                in_specs=[pl.BlockSpec((tm, c), row), pl.BlockSpec((hb, c), _halo_map(tm, 0, hb)),
                          pl.BlockSpec((tm, c), lambda i: (i, 1)),
                          pl.BlockSpec((LRU_CONV, c), full), vec, pl.BlockSpec((c, c), full), vec,
                          pl.BlockSpec((c, c), full), vec, vec, vec],
                out_specs=[pl.BlockSpec((tm, c), row), pl.BlockSpec((tm, c), row), pl.BlockSpec((tm, c), row)],
                out_shape=[jax.ShapeDtypeStruct((t, c), F32), jax.ShapeDtypeStruct((t, c), F32),
                           jax.ShapeDtypeStruct((t, c), MXU_DTYPE)],
                scratch_shapes=[pltpu.VMEM((tm, c), F32), pltpu.VMEM((tm, c), F32), pltpu.VMEM((SUBLANES, c), F32)],
                operands=[proj, proj, proj, conv_w, conv_b, wa, ba, wx, bx, lam, gain])


def _ret_consts():
    c = RET_CHUNK
    log_g = jnp.log1p(-jnp.exp2(-5.0 - jnp.arange(RET_HEADS, dtype=F32)))
    idx = jnp.arange(c, dtype=F32)
    diff = idx[:, None] - idx[None, :]
    decay = jnp.where(diff[None] >= 0, jnp.exp(jnp.maximum(diff, 0.0)[None] * log_g[:, None, None]), 0.0)
    zeta = jnp.exp((c - 1 - idx)[None, :] * log_g[:, None])
    xi = jnp.exp((idx + 1.0)[None, :] * log_g[:, None])
    spread = lambda v: jnp.repeat(v.T, RET_HEAD_DIM, axis=1)
    log_g_np = np.log1p(-np.exp2(-5.0 - np.arange(RET_HEADS, dtype=np.float32))).astype(np.float32)
    g_chunk = [float(np.exp(np.float32(c) * lg)) for lg in log_g_np]
    return decay, spread(xi), spread(zeta), g_chunk


def _rope_tables(t):
    pos = np.arange(t, dtype=np.float32)
    inv_freq = np.float32(ROPE_BASE) ** (-np.arange(0, RET_HEAD_DIM, 2, dtype=np.float32) / np.float32(RET_HEAD_DIM))
    ang = (pos[:, None] * inv_freq.astype(np.float32)[None, :]).astype(np.float32).astype(np.float64)
    cos, sin = np.cos(ang).astype(np.float32), np.sin(ang).astype(np.float32)
    return jnp.asarray(np.concatenate([cos, cos], axis=-1)), jnp.asarray(np.concatenate([-sin, sin], axis=-1))


def _rope(x, cos2, sin_signed):
    return x * cos2 + pltpu.roll(x, RET_HEAD_DIM // 2, 1) * sin_signed


def _rope_bwd(d, cos2, sin_signed):
    return d * cos2 + pltpu.roll(d * sin_signed, RET_HEAD_DIM // 2, 1)


RET_SCALE = RET_HEAD_DIM ** -0.5


RET_CHUNKS_PER_STEP = 2


def _ret_fwd(proj, cos2, sin_signed, gain):
    t = proj.shape[0]
    c, d, nh = RET_CHUNK, RET_HEAD_DIM, RET_HEADS
    n_chunks = t // c
    per = RET_CHUNKS_PER_STEP if n_chunks % RET_CHUNKS_PER_STEP == 0 else 1
    rows = per * c
    decay, xi, zeta, g_chunk = _ret_consts()

    def units(qk_ref, vg_ref, cos_ref, sin_ref, dec_ref, xi_ref, zeta_ref, gain_ref, o_ref, y_ref, st_ref, state, ctx):
        cur = [None] * nh

        def start():
            @pl.when(pl.program_id(0) == 0)
            def _():
                state[...] = jnp.zeros_like(state)
            for h in range(nh):
                cur[h] = state[h]

        def retain(s, h, keep):
            rs = slice(s * c, (s + 1) * c)
            cos2, sin_s = cos_ref[rs, :], sin_ref[rs, :]
            lo = h * d
            q = _rope(qk_ref[rs, lo:lo + d].astype(F32), cos2, sin_s)
            k = _rope(qk_ref[rs, D_RET + lo:D_RET + lo + d].astype(F32), cos2, sin_s) * RET_SCALE
            v = vg_ref[rs, lo:lo + d]
            s_prev = cur[h]
            st_ref[s, h] = s_prev
            scores = _dot_nt(q, k) * dec_ref[h]
            o = _dot(scores, v) + _dot(q * xi_ref[:, lo:lo + d], s_prev)
            cur[h] = s_prev * g_chunk[h] + _dot_tn(k * zeta_ref[:, lo:lo + d], v)
            o_ref[rs, lo:lo + d] = o
            keep["o"] = o

        def normalise(s, h, keep):
            rs = slice(s * c, (s + 1) * c)
            lo = h * d
            o = keep["o"]
            g = vg_ref[rs, D_RET + lo:D_RET + lo + d].astype(F32)
            mu = jnp.mean(o, axis=-1, keepdims=True)
            oc = o - mu
            on = oc * lax.rsqrt(jnp.mean(oc * oc, axis=-1, keepdims=True) + NORM_EPS)
            y_ref[rs, lo:lo + d] = (on * gain_ref[:, lo:lo + d] * (g * _sigmoid(g))).astype(y_ref.dtype)

        def end():
            for h in range(nh):
                state[h] = cur[h]

        pieces = [start]
        for s in range(per):
            for h in range(nh):
                keep = {}
                pieces += [lambda s=s, h=h, keep=keep: retain(s, h, keep),
                           lambda s=s, h=h, keep=keep: normalise(s, h, keep)]
        return pieces + [end]

    full2 = lambda i: (0, 0)
    return dict(units=units, grid=(n_chunks // per,),
                in_specs=[pl.BlockSpec((rows, 2 * D_RET), lambda i: (i, 1)),
                          pl.BlockSpec((rows, 2 * D_RET), lambda i: (i, 2)),
                          pl.BlockSpec((rows, d), lambda i: (i, 0)), pl.BlockSpec((rows, d), lambda i: (i, 0)),
                          pl.BlockSpec((nh, c, c), lambda i: (0, 0, 0)), pl.BlockSpec((c, D_RET), full2),
                          pl.BlockSpec((c, D_RET), full2), pl.BlockSpec((1, D_RET), full2)],
                out_specs=[pl.BlockSpec((rows, D_RET), lambda i: (i, 0)), pl.BlockSpec((rows, D_RET), lambda i: (i, 0)),
                           pl.BlockSpec((per, nh, d, d), lambda i: (i, 0, 0, 0))],
                out_shape=[jax.ShapeDtypeStruct((t, D_RET), F32), jax.ShapeDtypeStruct((t, D_RET), MXU_DTYPE),
                           jax.ShapeDtypeStruct((n_chunks, nh, d, d), F32)],
                scratch_shapes=[pltpu.VMEM((nh, d, d), F32)],
                operands=[proj, proj, cos2, sin_signed, decay, xi, zeta, gain])


def _outproj_fwd(x, y_lru, y_ret, w_out, g2, comm):
    t = x.shape[0]
    tm = min(PROJ_ROW_TILE, t)

    def body(x_ref, yl_ref, yr_ref, w_ref, g_ref, h1_ref, u2_ref):
        h1 = x_ref[...] + _dot(yl_ref[...], w_ref[:D_LRU, :]) + _dot(yr_ref[...], w_ref[D_LRU:, :])
        h1_ref[...] = h1
        _, _, u = _rms_fwd(h1, g_ref[...])
        u2_ref[...] = u.astype(u2_ref.dtype)

    row = lambda i: (i, 0)
    return _call(body, name="outproj_fwd", grid=(t // tm,), comm=comm,
                 in_specs=[pl.BlockSpec((tm, D_MODEL), row), pl.BlockSpec((tm, D_LRU), row), pl.BlockSpec((tm, D_RET), row),
                           _resident((D_MODEL, D_MODEL)), pl.BlockSpec((1, D_MODEL), lambda i: (0, 0))],
                 out_specs=[pl.BlockSpec((tm, D_MODEL), row), pl.BlockSpec((tm, D_MODEL), row)],
                 out_shape=[jax.ShapeDtypeStruct((t, D_MODEL), F32), jax.ShapeDtypeStruct((t, D_MODEL), MXU_DTYPE)],
                 operands=[x, y_lru, y_ret, w_out, g2])


FFN_TN = 768
FFN_NJ = D_FF // FFN_TN
FFN_GROUP = 4


def _ffn_fwd(u2, w_blocks, conv_w, conv_b, w_down, h1, gf, target):
    t = u2.shape[0]
    tm = min(ROW_TILE, t)
    tn, nj, group = FFN_TN, FFN_NJ, FFN_GROUP
    ng, tw = nj // group, group * tn
    hb = _halo_rows(u2.dtype)
    assert w_blocks.shape == (2 * nj, D_MODEL, tn)

    def conv(ext, col, up_ref, conv_ref, cw_ref, cb_ref, first):
        x = ext[hb:, :]
        up_ref[:, col] = x.astype(up_ref.dtype)
        prev = jnp.where(first, 0.0, ext[hb - SUBLANES:hb, :])
        cw = cw_ref[:, col]
        y = cb_ref[:, col] + cw[FFN_CONV - 1:FFN_CONV, :] * x
        for k in range(FFN_CONV - 1):
            y = y + cw[k:k + 1, :] * _shift_down(prev, x, FFN_CONV - 1 - k)
        conv_ref[:, col] = y.astype(conv_ref.dtype)
        return y

    def body(u_ref, uh_ref, w_ref, cwa_ref, cwv_ref, cba_ref, cbv_ref, wd_ref, h1_ref, gf_ref, tg_ref,
             upa_ref, upv_ref, ca_ref, cv_ref, act_ref, dh_ref, dhb_ref, dgf_ref, loss_ref, acc):
        i, jg = pl.program_id(0), pl.program_id(1)

        @pl.when((i == 0) & (jg == 0))
        def _():
            dgf_ref[...] = jnp.zeros_like(dgf_ref)
            loss_ref[...] = jnp.zeros_like(loss_ref)

        @pl.when(jg == 0)
        def _():
            acc[...] = jnp.zeros_like(acc)

        u_ext = jnp.concatenate([uh_ref[...], u_ref[...]], axis=0)

        def project(jj):
            j = jg * group + jj
            return _dot(u_ext, w_ref[j]), _dot(u_ext, w_ref[nj + j])

        down, ahead = None, project(0)
        for jj in range(group):
            col = slice(jj * tn, (jj + 1) * tn)
            j = jg * group + jj
            ext_a, ext_v = ahead
            if jj + 1 < group:
                ahead = project(jj + 1)
            a = conv(ext_a, col, upa_ref, ca_ref, cwa_ref, cba_ref, i == 0)
            v = conv(ext_v, col, upv_ref, cv_ref, cwv_ref, cbv_ref, i == 0)
            act = (_gelu(a) * v).astype(act_ref.dtype)
            act_ref[:, col] = act
            part = _dot(act, wd_ref[pl.ds(pl.multiple_of(j * tn, tn), tn), :])
            down = part if down is None else down + part
        acc[...] += down

        @pl.when(jg == ng - 1)
        def _():
            n, rstd, y = _rms_fwd(h1_ref[...] + acc[...], gf_ref[...])
            err = y - tg_ref[...]
            loss_ref[...] += (0.5 / D_MODEL) * jnp.sum(err * err)
            dh, dgf = _rms_bwd(err * (1.0 / D_MODEL), n, rstd, gf_ref[...])
            dgf_ref[...] += dgf
            dh_ref[...] = dh
            dhb_ref[...] = dh.astype(dhb_ref.dtype)

    per = tm // hb
    row = lambda i, j: (i, 0)
    const = lambda i, j: (0, 0)
    tile = pl.BlockSpec((tm, tw), lambda i, j: (i, j))
    return _call(body, name="ffn_fwd", grid=(t // tm, ng),
                 in_specs=[pl.BlockSpec((tm, D_MODEL), row),
                           pl.BlockSpec((hb, D_MODEL), lambda i, j: (jnp.maximum(i * per - 1, 0), 0)),
                           _resident(w_blocks.shape),
                           pl.BlockSpec((FFN_CONV, tw), lambda i, j: (0, j)),
                           pl.BlockSpec((FFN_CONV, tw), lambda i, j: (0, j + ng)),
                           pl.BlockSpec((1, tw), lambda i, j: (0, j)), pl.BlockSpec((1, tw), lambda i, j: (0, j + ng)),
                           _resident((D_FF, D_MODEL)),
                           pl.BlockSpec((tm, D_MODEL), row), pl.BlockSpec((1, D_MODEL), const),
                           pl.BlockSpec((tm, D_MODEL), row)],
                 out_specs=[tile] * 5 + [pl.BlockSpec((tm, D_MODEL), row),
                            pl.BlockSpec((tm, D_MODEL), row), pl.BlockSpec((SUBLANES, D_MODEL), const),
                            pl.BlockSpec((SUBLANES, LANES), const)],
                 out_shape=[jax.ShapeDtypeStruct((t, D_FF), MXU_DTYPE)] * 5 + [
                            jax.ShapeDtypeStruct((t, D_MODEL), F32),
                            jax.ShapeDtypeStruct((t, D_MODEL), MXU_DTYPE), jax.ShapeDtypeStruct((SUBLANES, D_MODEL), F32),
                            jax.ShapeDtypeStruct((SUBLANES, LANES), F32)],
                 scratch_shapes=[pltpu.VMEM((tm, D_MODEL), F32)],
                 operands=[u2, u2, w_blocks, conv_w, conv_w, conv_b, conv_b, w_down, h1, gf, target])


FFN_ACC_ROWS = SUBLANES * (FFN_CONV + 1)


def _ffn_bwd(dh2, dh2_b, w_down, up_a, up_v, conv_a, conv_v, conv_w, w_up_blocks, h1, g2, comm):
    t = up_a.shape[0]
    tm = min(ROW_TILE, t)
    tn, nj, group = FFN_TN, FFN_NJ, FFN_GROUP
    ng, tw = nj // group, group * tn
    ni = t // tm
    assert w_up_blocks.shape == (2 * nj, D_MODEL, tn)

    def conv_bwd(dy, x, cw, acc_ref, carry_ref, dup_ref, col):
        nxt = carry_ref[...]
        carry_ref[...] = dy[:SUBLANES, :]
        ahead = [_shift_up(dy, nxt, FFN_CONV - 1 - k) for k in range(FFN_CONV)]
        dx = cw[FFN_CONV - 1:FFN_CONV, :] * dy
        for k in range(FFN_CONV - 1):
            dx = dx + cw[k:k + 1, :] * ahead[k]
        dx = dx.astype(dup_ref.dtype)
        dup_ref[:, col] = dx
        for k in range(FFN_CONV):
            acc_ref[k * SUBLANES:(k + 1) * SUBLANES, :] += _colsum8(ahead[k] * x)
        acc_ref[FFN_CONV * SUBLANES:, :] += _colsum8(dy)
        return dx

    def body(dh_ref, dhb_ref, wd_ref, ua_ref, uv_ref, ca_ref, cv_ref, cwa_ref, cwv_ref, wu_ref, h1_ref, g2_ref,
             dua_ref, duv_ref, acca_ref, accv_ref, dh1_ref, dh1b_ref, dg2_ref, carry_a, carry_v, du):
        i, jg = pl.program_id(0), pl.program_id(1)

        @pl.when((i == 0) & (jg == 0))
        def _():
            for ref in (acca_ref, accv_ref, carry_a, carry_v, dg2_ref):
                ref[...] = jnp.zeros_like(ref)

        dhb = dhb_ref[...]

        def through_down(jj):
            j = jg * group + jj
            return _dot_nt(dhb, wd_ref[pl.ds(pl.multiple_of(j * tn, tn), tn), :])

        part, ahead = None, through_down(0)
        for jj in range(group):
            col = slice(jj * tn, (jj + 1) * tn)
            j = jg * group + jj
            dact = ahead
            if jj + 1 < group:
                ahead = through_down(jj + 1)
            v = cv_ref[:, col].astype(F32)
            g, dg = _gelu_parts(ca_ref[:, col].astype(F32))
            da = conv_bwd(dact * v * dg, ua_ref[:, col].astype(F32), cwa_ref[:, col], acca_ref.at[j], carry_a.at[j],
                          dua_ref, col)
            dv = conv_bwd(dact * g, uv_ref[:, col].astype(F32), cwv_ref[:, col], accv_ref.at[j], carry_v.at[j],
                          duv_ref, col)
            term = _dot_nt(da, wu_ref[j]) + _dot_nt(dv, wu_ref[nj + j])
            part = term if part is None else part + term

        @pl.when(jg == 0)
        def _():
            du[...] = part

        @pl.when(jg > 0)
        def _():
            du[...] += part

        @pl.when(jg == ng - 1)
        def _():
            n, rstd, _ = _rms_fwd(h1_ref[...], g2_ref[...])
            dh1, dg2 = _rms_bwd(du[...], n, rstd, g2_ref[...])
            dh1 = dh1 + dh_ref[...]
            dg2_ref[...] += dg2
            dh1_ref[...] = dh1
            dh1b_ref[...] = dh1.astype(dh1b_ref.dtype)

    row = lambda i, j: (ni - 1 - i, 0)
    const = lambda i, j: (0, 0)
    tile = pl.BlockSpec((tm, tw), lambda i, j: (ni - 1 - i, j))
    acc = pl.BlockSpec((nj, FFN_ACC_ROWS, tn), lambda i, j: (0, 0, 0))
    return _call(body, name="ffn_bwd", grid=(ni, ng), comm=comm, vmem_limit=FFN_BWD_VMEM_LIMIT,
                 in_specs=[pl.BlockSpec((tm, D_MODEL), row), pl.BlockSpec((tm, D_MODEL), row),
                           _resident((D_FF, D_MODEL)), tile, tile, tile, tile,
                           pl.BlockSpec((FFN_CONV, tw), lambda i, j: (0, j)),
                           pl.BlockSpec((FFN_CONV, tw), lambda i, j: (0, j + ng)),
                           _resident(w_up_blocks.shape), pl.BlockSpec((tm, D_MODEL), row),
                           pl.BlockSpec((1, D_MODEL), const)],
                 out_specs=[tile, tile, acc, acc, pl.BlockSpec((tm, D_MODEL), row), pl.BlockSpec((tm, D_MODEL), row),
                            pl.BlockSpec((SUBLANES, D_MODEL), const)],
                 out_shape=[jax.ShapeDtypeStruct((t, D_FF), MXU_DTYPE), jax.ShapeDtypeStruct((t, D_FF), MXU_DTYPE),
                            jax.ShapeDtypeStruct((nj, FFN_ACC_ROWS, tn), F32),
                            jax.ShapeDtypeStruct((nj, FFN_ACC_ROWS, tn), F32),
                            jax.ShapeDtypeStruct((t, D_MODEL), F32), jax.ShapeDtypeStruct((t, D_MODEL), MXU_DTYPE),
                            jax.ShapeDtypeStruct((SUBLANES, D_MODEL), F32)],
                 scratch_shapes=[pltpu.VMEM((nj, SUBLANES, tn), F32), pltpu.VMEM((nj, SUBLANES, tn), F32),
                                 pltpu.VMEM((tm, D_MODEL), F32)],
                 operands=[dh2, dh2_b, w_down, up_a, up_v, conv_a, conv_v, conv_w, conv_w, w_up_blocks, h1, g2])


def _ret_bwd(proj, cos2, sin_signed, gain, o, states, dmix_at):
    t = proj.shape[0]
    c, d, nh = RET_CHUNK, RET_HEAD_DIM, RET_HEADS
    n_chunks = t // c
    per = RET_CHUNKS_PER_STEP if n_chunks % RET_CHUNKS_PER_STEP == 0 else 1
    rows = per * c
    n_steps = n_chunks // per
    decay, xi, zeta, g_chunk = _ret_consts()
    base = 2 * D_LRU

    def units(qk_ref, vg_ref, cos_ref, sin_ref, dec_ref, xi_ref, zeta_ref, gain_ref, o_ref, st_ref,
              dp_ref, dgain_ref, gstate, ctx):
        cur = [None] * nh
        dmix = ctx["scratch"][dmix_at[0]][dmix_at[1]]

        def start():
            @pl.when(pl.program_id(0) == 0)
            def _():
                gstate[...] = jnp.zeros_like(gstate)
                dgain_ref[...] = jnp.zeros_like(dgain_ref)
            for h in range(nh):
                cur[h] = gstate[h]

        def gate_and_norm(s, h, keep):
            rs = slice(s * c, (s + 1) * c)
            lo = h * d
            g = vg_ref[rs, D_RET + lo:D_RET + lo + d].astype(F32)
            gain_h = gain_ref[:, lo:lo + d]
            dy = dmix[rs, D_LRU + lo:D_LRU + lo + d]
            sg = _sigmoid(g)
            o_h = o_ref[rs, lo:lo + d]
            oc = o_h - jnp.mean(o_h, axis=-1, keepdims=True)
            rstd = lax.rsqrt(jnp.mean(oc * oc, axis=-1, keepdims=True) + NORM_EPS)
            on = oc * rstd
            at = base + 3 * D_RET + lo
            dp_ref[rs, at:at + d] = (dy * on * gain_h * (sg * (1.0 + g * (1.0 - sg)))).astype(dp_ref.dtype)
            don_g = dy * (g * sg)
            dgain_ref[:, lo:lo + d] += _colsum8(don_g * on)
            don = don_g * gain_h
            keep["do"] = rstd * (don - jnp.mean(don, axis=-1, keepdims=True)
                                 - on * jnp.mean(don * on, axis=-1, keepdims=True))

        def retain(s, h, keep):
            rs = slice(s * c, (s + 1) * c)
            cos2, sin_s = cos_ref[rs, :], sin_ref[rs, :]
            lo = h * d
            q = _rope(qk_ref[rs, lo:lo + d].astype(F32), cos2, sin_s)
            k = _rope(qk_ref[rs, D_RET + lo:D_RET + lo + d].astype(F32), cos2, sin_s) * RET_SCALE
            v = vg_ref[rs, lo:lo + d]
            xi_h, zeta_h, dec = xi_ref[:, lo:lo + d], zeta_ref[:, lo:lo + d], dec_ref[h]
            do = keep["do"]
            s_prev = st_ref[s, h]
            g_next = cur[h]
            p = _dot_nt(q, k) * dec
            dpm = _dot_nt(do, v) * dec
            keep["dq"] = _dot(dpm, k) + _dot_nt(do, s_prev) * xi_h
            keep["dk"] = _dot_tn(dpm, q) + _dot_nt(v, g_next) * zeta_h
            dv = _dot_tn(p, do) + _dot(k * zeta_h, g_next)
            cur[h] = g_next * g_chunk[h] + _dot_tn(q * xi_h, do)
            at = base + 2 * D_RET + lo
            dp_ref[rs, at:at + d] = dv.astype(dp_ref.dtype)

        def unrope(s, h, keep):
            rs = slice(s * c, (s + 1) * c)
            cos2, sin_s = cos_ref[rs, :], sin_ref[rs, :]
            lo = h * d
            dp_ref[rs, base + lo:base + lo + d] = _rope_bwd(keep["dq"], cos2, sin_s).astype(dp_ref.dtype)
            at = base + D_RET + lo
            dp_ref[rs, at:at + d] = _rope_bwd(keep["dk"] * RET_SCALE, cos2, sin_s).astype(dp_ref.dtype)

        def end():
            for h in range(nh):
                gstate[h] = cur[h]

        pieces = [start]
        for s in reversed(range(per)):
            for h in range(nh):
                keep = {}
                pieces += [lambda s=s, h=h, keep=keep, f=f: f(s, h, keep) for f in (gate_and_norm, retain, unrope)]
        return pieces + [end]

    rev = lambda col: (lambda i: (n_steps - 1 - i, col))
    full2 = lambda i: (0, 0)
    return dict(units=units, grid=(n_steps,),
                in_specs=[pl.BlockSpec((rows, 2 * D_RET), rev(1)), pl.BlockSpec((rows, 2 * D_RET), rev(2)),
                          pl.BlockSpec((rows, d), rev(0)), pl.BlockSpec((rows, d), rev(0)),
                          pl.BlockSpec((nh, c, c), lambda i: (0, 0, 0)), pl.BlockSpec((c, D_RET), full2),
                          pl.BlockSpec((c, D_RET), full2), pl.BlockSpec((1, D_RET), full2),
                          pl.BlockSpec((rows, D_RET), rev(0)),
                          pl.BlockSpec((per, nh, d, d), lambda i: (n_steps - 1 - i, 0, 0, 0))],
                out_specs=[pl.BlockSpec((rows, D_IN), rev(0)), pl.BlockSpec((SUBLANES, D_RET), full2)],
                out_shape=[jax.ShapeDtypeStruct((t, D_IN), MXU_DTYPE), jax.ShapeDtypeStruct((SUBLANES, D_RET), F32)],
                scratch_shapes=[pltpu.VMEM((nh, d, d), F32)],
                operands=[proj, proj, cos2, sin_signed, decay, xi, zeta, gain, o, states])


LRU_ACC = {"conv_w": 0, "conv_b": LRU_CONV, "gate_a_b": LRU_CONV + 1, "gate_x_b": LRU_CONV + 2,
           "lambda": LRU_CONV + 3, "norm_gain": LRU_CONV + 4}
LRU_ACC_ROWS = SUBLANES * (LRU_CONV + 5)


def _lru_bwd(proj, xc_all, h_all, conv_w, wa, ba, wx, bx, lam, gain, dproj_part, dmix_at):
    t = proj.shape[0]
    tm = min(ROW_TILE, t)
    c = D_LRU
    ni = t // tm

    def body(x_ref, xh_ref, g_ref, xc_ref, h_ref, hh_ref, cw_ref, wa_ref, ba_ref, wx_ref, bx_ref, lam_ref,
             gain_ref, acc_ref, dwa_ref, dwx_ref, a_scr, b_scr, mu_scr, carry_mu, carry_dxc, ctx):
        dp_ref = ctx["outs"][dproj_part][0]
        dmix = ctx["scratch"][dmix_at[0]][dmix_at[1]]
        fill = ctx["fill"]
        i = pl.program_id(0)
        r = ni - 1 - i

        @pl.when(i == 0)
        def _():
            acc_ref[...] = jnp.zeros_like(acc_ref)
            dwa_ref[...] = jnp.zeros_like(dwa_ref)
            dwx_ref[...] = jnp.zeros_like(dwx_ref)
            carry_mu[...] = jnp.zeros_like(carry_mu)
            carry_dxc[...] = jnp.zeros_like(carry_dxc)

        def add(name, val, k=0):
            lo = (LRU_ACC[name] + k) * SUBLANES
            acc_ref[lo:lo + SUBLANES, :] += _colsum8(val)

        fill()
        xc, h = xc_ref[...], h_ref[...]
        lam_v = lam_ref[...]
        sp = _softplus(-lam_v)
        rg, ig, a, m = _lru_gates(xc, wa_ref[...], ba_ref[...], wx_ref[...], bx_ref[...], sp, fill)
        gl, dgl = _gelu_parts(g_ref[...].astype(F32))
        fill()
        zn, rstd, _ = _rms_fwd(h * gl, gain_ref[...])
        dy = dmix[:, :c]
        dz, dgain = _rms_bwd(dy, zn, rstd, gain_ref[...])
        lo = LRU_ACC["norm_gain"] * SUBLANES
        acc_ref[lo:lo + SUBLANES, :] += dgain
        dp_ref[:, c:2 * c] = (dz * h * dgl).astype(dp_ref.dtype)
        dh = dz * gl
        fill()
        ga, gb = _group_scan(a, a * dh, reverse=True, fill=fill)
        a_scr[...] = ga
        b_scr[...] = gb
        mu_next_tile = carry_mu[...]
        carry_mu[...] = _carry_scan(a_scr, b_scr, mu_scr, mu_next_tile, reverse=True)
        fill()
        lam_t = dh + _shift_up(mu_scr[...], mu_next_tile, 1)
        h_prev = _shift_down(jnp.where(r == 0, 0.0, hh_ref[...]), h, 1)
        da = lam_t * h_prev
        dig = lam_t * m * xc
        dxc = lam_t * m * ig
        dlog_a = da * a - (lam_t * ig * xc) * (a * a) / m
        fill()
        dpr = dlog_a * ((-LRU_C) * sp) * rg * (1.0 - rg)
        add("lambda", dlog_a * ((-LRU_C) * rg) * (-_sigmoid(-lam_v)))
        dpi = dig * ig * (1.0 - ig)
        add("gate_a_b", dpr)
        add("gate_x_b", dpi)
        fill()
        dwa_ref[...] += _dot_tn(xc, dpr)
        dwx_ref[...] += _dot_tn(xc, dpi)
        dxc = dxc + _dot_nt(dpr, wa_ref[...]) + _dot_nt(dpi, wx_ref[...])
        fill()
        add("conv_b", dxc)
        x = x_ref[...].astype(F32)
        prev = jnp.where(r == 0, 0.0, xh_ref[...].astype(F32)[-SUBLANES:, :])
        cw = cw_ref[...]
        nxt = carry_dxc[...]
        carry_dxc[...] = dxc[:SUBLANES, :]
        dx = cw[LRU_CONV - 1:LRU_CONV, :] * dxc
        for k in range(LRU_CONV - 1):
            dx = dx + cw[k:k + 1, :] * _shift_up(dxc, nxt, LRU_CONV - 1 - k)
        fill()
        for k in range(LRU_CONV):
            add("conv_w", dxc * _shift_down(prev, x, LRU_CONV - 1 - k), k)
        dp_ref[:, :c] = dx.astype(dp_ref.dtype)

    hb = _halo_rows(proj.dtype)
    rev = lambda col: (lambda i: (ni - 1 - i, col))
    halo = lambda rows: (lambda i: (jnp.maximum((ni - 1 - i) * (tm // rows) - 1, 0), 0))
    full = lambda i: (0, 0)
    vec = pl.BlockSpec((1, c), full)
    mat = pl.BlockSpec((c, c), full)
    return dict(body=body, grid=(ni,), fill_points=12,
                in_specs=[pl.BlockSpec((tm, c), rev(0)), pl.BlockSpec((hb, c), halo(hb)), pl.BlockSpec((tm, c), rev(1)),
                          pl.BlockSpec((tm, c), rev(0)), pl.BlockSpec((tm, c), rev(0)),
                          pl.BlockSpec((SUBLANES, c), halo(SUBLANES)),
                          pl.BlockSpec((LRU_CONV, c), full), mat, vec, mat, vec, vec, vec],
                out_specs=[pl.BlockSpec((LRU_ACC_ROWS, c), full), mat, mat],
                out_shape=[jax.ShapeDtypeStruct((LRU_ACC_ROWS, c), F32), jax.ShapeDtypeStruct((c, c), F32),
                           jax.ShapeDtypeStruct((c, c), F32)],
                scratch_shapes=[pltpu.VMEM((tm, c), F32), pltpu.VMEM((tm, c), F32), pltpu.VMEM((tm, c), F32),
                                pltpu.VMEM((SUBLANES, c), F32), pltpu.VMEM((SUBLANES, c), F32)],
                operands=[proj, proj, proj, xc_all, h_all, h_all, conv_w, wa, ba, wx, bx, lam, gain])


def _mix_proj_bwd(dh1, dh1_b, w_out, w_in_blocks, x, g1, dproj_part):
    t = x.shape[0]
    tm = min(ROW_TILE, t)
    ni = t // tm
    nb, _, cb = w_in_blocks.shape
    first_free = -(-2 * D_LRU // cb)
    du = [None]

    def term(dp_ref, w_ref, d):
        part = _dot_nt(dp_ref[:, d * cb:(d + 1) * cb], w_ref[d])
        du[0] = part if du[0] is None else du[0] + part

    def head(dh_ref, dhb_ref, wo_ref, wi_ref, x_ref, g_ref, gx_ref, dg_ref, dmix, ctx):
        @pl.when(pl.program_id(0) == 0)
        def _():
            dg_ref[...] = jnp.zeros_like(dg_ref)
        dmix[...] = _dot_nt(dhb_ref[...], wo_ref[...])
        du[0] = None

    def units(dh_ref, dhb_ref, wo_ref, wi_ref, x_ref, g_ref, gx_ref, dg_ref, dmix, ctx):
        dp_ref = ctx["outs"][dproj_part][0]
        return [lambda d=d: term(dp_ref, wi_ref, d) for d in range(first_free, nb)]

    def tail(dh_ref, dhb_ref, wo_ref, wi_ref, x_ref, g_ref, gx_ref, dg_ref, dmix, ctx):
        dp_ref = ctx["outs"][dproj_part][0]
        for d in range(first_free):
            term(dp_ref, wi_ref, d)
        n, rstd, _ = _rms_fwd(x_ref[...], g_ref[...])
        dx, dg = _rms_bwd(du[0], n, rstd, g_ref[...])
        dg_ref[...] += dg
        gx_ref[...] = dx + dh_ref[...]

    row = lambda i: (ni - 1 - i, 0)
    const = lambda i: (0, 0)
    tile = pl.BlockSpec((tm, D_MODEL), row)
    return dict(head=head, units=units, tail=tail, grid=(ni,),
                in_specs=[tile, tile, _resident(w_out.shape), _resident(w_in_blocks.shape), tile,
                          pl.BlockSpec((1, D_MODEL), const)],
                out_specs=[tile, pl.BlockSpec((SUBLANES, D_MODEL), const)],
                out_shape=[jax.ShapeDtypeStruct((t, D_MODEL), F32), jax.ShapeDtypeStruct((SUBLANES, D_MODEL), F32)],
                scratch_shapes=[pltpu.VMEM((tm, D_MODEL), F32)],
                operands=[dh1, dh1_b, w_out, w_in_blocks, x, g1])


def _pair_sum(core, a, b, name):
    n, r, c = b.shape
    spec = pl.BlockSpec((None, r, c), lambda q, core: (q, 0, 0))

    def body(core_ref, a_ref, b_ref, o_ref):
        o_ref[...] = (a_ref[...].astype(F32) + b_ref[...].astype(F32)).astype(o_ref.dtype)

    return pl.pallas_call(
        body, name=name,
        grid_spec=pltpu.PrefetchScalarGridSpec(
            num_scalar_prefetch=1, grid=(n,),
            in_specs=[pl.BlockSpec((None, r, c), lambda q, core: (2 * q + core[0], 0, 0)), spec], out_specs=spec),
        out_shape=jax.ShapeDtypeStruct(b.shape, b.dtype),
        compiler_params=pltpu.CompilerParams(dimension_semantics=("arbitrary",), vmem_limit_bytes=VMEM_LIMIT),
    )(core, a, b)


ADAMW_BLOCK_BYTES = 4 * 1024 * 1024


def _sum_adamw(parts, w, m, v, name):
    n_parts, r, c = parts.shape
    tr = r
    while n_parts * tr * c * parts.dtype.itemsize > ADAMW_BLOCK_BYTES and tr % (4 * SUBLANES) == 0:
        tr //= 2

    def body(p_ref, w_ref, m_ref, v_ref, g_ref, d_ref, nm_ref, nv_ref):
        g = p_ref[0].astype(F32)
        for s in range(1, n_parts):
            g = g + p_ref[s].astype(F32)
        nm = ADAM_B1 * m_ref[...] + (1.0 - ADAM_B1) * g
        nv = ADAM_B2 * v_ref[...] + (1.0 - ADAM_B2) * (g * g)
        m_hat = nm / (1.0 - ADAM_B1 ** ADAM_STEP)
        v_hat = nv / (1.0 - ADAM_B2 ** ADAM_STEP)
        g_ref[...] = g
        d_ref[...] = -ADAM_LR * (m_hat / (jnp.sqrt(v_hat) + ADAM_EPS) + ADAM_WD * w_ref[...])
        nm_ref[...] = nm
        nv_ref[...] = nv

    row = pl.BlockSpec((tr, c), lambda i: (i, 0))
    return _call(body, name=name, grid=(r // tr,),
                 in_specs=[pl.BlockSpec((n_parts, tr, c), lambda i: (0, i, 0)), row, row, row],
                 out_specs=[row, row, row, row], out_shape=[jax.ShapeDtypeStruct((r, c), F32)] * 4,
                 operands=[parts, w, m, v])


MATRICES = ("w_in", "w_out", "ffn_up_w", "ffn_down_w")
CONVS = ("lru_conv_w", "ffn_conv_w")
REPLICATED = ("norm1_gain", "lru_conv_b", "lru_gate_a_w", "lru_gate_a_b", "lru_gate_x_w", "lru_gate_x_b", "lru_lambda",
              "lru_norm_gain", "ret_norm_gain", "norm2_gain", "ffn_conv_b", "final_norm_gain")
WEIGHTS = ("norm1_gain", "w_in", "lru_conv_w", "lru_conv_b", "lru_gate_a_w", "lru_gate_a_b", "lru_gate_x_w",
           "lru_gate_x_b", "lru_lambda", "lru_norm_gain", "ret_norm_gain", "w_out", "norm2_gain", "ffn_up_w",
           "ffn_conv_w", "ffn_conv_b", "ffn_down_w", "final_norm_gain")


def _rows(a, pad_to):
    a = a.reshape(-1, LANES)
    pad = (-a.shape[0]) % pad_to
    return jnp.pad(a, ((0, pad), (0, 0))) if pad else a


def _pack(arrays, pad_to):
    rows, layout, at = [], [], 0
    for a in arrays:
        r = _rows(a, pad_to)
        layout.append((at, a.size // LANES, a.shape))
        rows.append(r)
        at += r.shape[0]
    return jnp.concatenate(rows, axis=0), layout


def _unpack(packed, layout):
    lead = packed.shape[:-2]
    return [packed[..., at:at + n, :].reshape(lead + shape) for at, n, shape in layout]


def _conv_rows(lru, ffn, dtype, pad_to):
    lead = lru.shape[:-2]
    flat = jnp.concatenate([lru.reshape(lead + (-1,)), ffn.reshape(lead + (-1,))], axis=-1).astype(dtype)
    rows = flat.shape[-1] // LANES
    pad = (-rows) % pad_to
    return jnp.pad(flat.reshape(lead + (rows, LANES)), [(0, 0)] * len(lead) + [(0, pad), (0, 0)])


def _column_blocks(full):
    r, c = full.shape
    return full.reshape(r, N_DEV, c // N_DEV).transpose(1, 0, 2)


def _block_diag(w):
    nh, d, _ = w.shape
    eye = jnp.eye(nh, dtype=w.dtype)
    return (w[:, :, None, :] * eye[:, None, :, None]).reshape(nh * d, nh * d)


def _diag_blocks(dense, nh):
    d = dense.shape[0] // nh
    blocks = dense.reshape(nh, d, nh, d)
    return jnp.stack([blocks[h, :, h, :] for h in range(nh)], axis=0)


def kernel(x, norm1_gain, w_in, lru_conv_w, lru_conv_b, lru_gate_a_w, lru_gate_a_b, lru_gate_x_w, lru_gate_x_b, lru_lambda, lru_norm_gain, ret_norm_gain, w_out, norm2_gain, ffn_up_w, ffn_conv_w, ffn_conv_b, ffn_down_w, final_norm_gain, loss_target, m_norm1_gain, m_w_in, m_lru_conv_w, m_lru_conv_b, m_lru_gate_a_w, m_lru_gate_a_b, m_lru_gate_x_w, m_lru_gate_x_b, m_lru_lambda, m_lru_norm_gain, m_ret_norm_gain, m_w_out, m_norm2_gain, m_ffn_up_w, m_ffn_conv_w, m_ffn_conv_b, m_ffn_down_w, m_final_norm_gain, v_norm1_gain, v_w_in, v_lru_conv_w, v_lru_conv_b, v_lru_gate_a_w, v_lru_gate_a_b, v_lru_gate_x_w, v_lru_gate_x_b, v_lru_lambda, v_lru_norm_gain, v_ret_norm_gain, v_w_out, v_norm2_gain, v_ffn_up_w, v_ffn_conv_w, v_ffn_conv_b, v_ffn_down_w, v_final_norm_gain):
    args = dict(locals())
    given = {n: args[n] for n in WEIGHTS}
    out_shape = {n: given[n].shape for n in WEIGHTS}

    def plain(a):
        return a.reshape(1, -1) if a.ndim <= 2 else a[0]

    w = {n: plain(given[n]) for n in WEIGHTS}
    mom_m = {n: plain(args["m_" + n]) for n in WEIGHTS}
    mom_v = {n: plain(args["v_" + n]) for n in WEIGHTS}
    x2, target = x[0], loss_target[0]
    t = x2.shape[0]
    core = lax.axis_index("c").astype(jnp.int32).reshape(1)
    res = {}

    conv_pad = _conv_rows(w["lru_conv_w"], w["ffn_conv_w"], F32, SUBLANES)
    first = _gather_first([w["w_in"].astype(MXU_DTYPE), conv_pad])
    w_in_blocks, conv_all = _run_comms([first, _gather_second(first.out_shape)], "w_in_all_gather")
    n_lru = w["lru_conv_w"].size
    conv_flat = conv_all.reshape(N_DEV, -1)
    lru_cw = conv_flat[:, :n_lru].reshape((N_DEV,) + w["lru_conv_w"].shape).transpose(1, 0, 2).reshape(LRU_CONV, D_LRU)
    ffn_cw = conv_flat[:, n_lru:n_lru + w["ffn_conv_w"].size].reshape((N_DEV,) + w["ffn_conv_w"].shape)
    ffn_cw = ffn_cw.transpose(1, 0, 2).reshape(FFN_CONV, 2 * D_FF)

    cos2, sin_signed = _rope_tables(t)
    wa = _block_diag(w["lru_gate_a_w"]).astype(MXU_DTYPE)
    wx = _block_diag(w["lru_gate_x_w"]).astype(MXU_DTYPE)
    gf = w["final_norm_gain"]

    early = _gather_first([w["w_out"].astype(MXU_DTYPE), w["ffn_down_w"].astype(MXU_DTYPE)])
    (u1, proj), (w_out_part, down_part) = _inproj_fwd(x2, w["norm1_gain"], w_in_blocks, early)
    ((xc, h_lru, y_lru), (o_ret, y_ret, states)), (w_out_blocks, down_blocks, up_part) = _fused(
        [_lru_fwd(proj, lru_cw, w["lru_conv_b"], wa, w["lru_gate_a_b"], wx, w["lru_gate_x_b"], w["lru_lambda"],
                  w["lru_norm_gain"]),
         _ret_fwd(proj, cos2, sin_signed, w["ret_norm_gain"])],
        "mix_fwd", _both(_gather_second([w_out_part, down_part]), _gather_first([w["ffn_up_w"].astype(MXU_DTYPE)])))
    w_out_full = w_out_blocks.reshape(D_MODEL, D_MODEL)
    w_down_full = down_blocks.reshape(D_FF, D_MODEL)

    (h1, u2), (up_blocks,) = _outproj_fwd(x2, y_lru, y_ret, w_out_full, w["norm2_gain"], _gather_second([up_part]))
    up_a, up_v, conv_a, conv_v, act, dh2, dh2_b, dgf, loss_local = _ffn_fwd(u2, up_blocks, ffn_cw, w["ffn_conv_b"],
                                                                            w_down_full, h1, gf, target)

    def to_owner_chips(blocks, names, tag):
        theirs = _run_comms([_pair_exchange(blocks)], "grads_pair_exchange_" + tag)
        return [_pair_sum(core, a, b, "grads_pair_sum_" + n) for n, a, b in zip(names, blocks, theirs)]

    def adamw(name, parts):
        res[name] = _sum_adamw(parts, w[name], mom_m[name], mom_v[name], "adamw_" + name)

    g = {"final_norm_gain": dgf[0]}
    dup_a, dup_v, acc_a, acc_v, dh1, dh1_b, dg2 = _ffn_bwd(
        dh2, dh2_b, w_down_full, up_a, up_v, conv_a, conv_v, ffn_cw, up_blocks, h1, w["norm2_gain"], None)
    per_col = lambda a: a[:, ::SUBLANES].transpose(1, 0, 2).reshape(FFN_CONV + 1, D_FF)
    acc = jnp.concatenate([per_col(acc_a), per_col(acc_v)], axis=1)
    g_ffn_cw, g["ffn_conv_b"] = acc[:FFN_CONV], acc[FFN_CONV:]
    g["norm2_gain"] = dg2[:1]
    g_up = jnp.concatenate([_mm_tn(u2, dup_a, "ffn_up_wgrad_a", blocks=N_DEV // 2),
                            _mm_tn(u2, dup_v, "ffn_up_wgrad_v", blocks=N_DEV // 2)], axis=0)
    up_sums = to_owner_chips([g_up], ["ffn_up_w"], "up")
    g_down, (up_parts,) = _mm_tn(act, dh2_b, "ffn_down_wgrad", comm=_chip_exchange(up_sums))
    adamw("ffn_up_w", up_parts)
    g_out = jnp.concatenate([_mm_tn(y_lru, dh1_b, "w_out_wgrad_lru"), _mm_tn(y_ret, dh1_b, "w_out_wgrad_ret")], axis=0)
    low_sums = to_owner_chips([g_down.reshape(N_DEV, D_FF // N_DEV, D_MODEL),
                               g_out.reshape(N_DEV, D_MODEL // N_DEV, D_MODEL)], ["ffn_down_w", "w_out"], "low")
    (dproj, dgain_ret), (grad_x, dg1), (lru_acc, dwa, dwx) = _fused(
        [_ret_bwd(proj, cos2, sin_signed, w["ret_norm_gain"], o_ret, states, dmix_at=(1, 0)),
         _mix_proj_bwd(dh1, dh1_b, w_out_full, w_in_blocks, x2, w["norm1_gain"], dproj_part=0),
         _lru_bwd(proj, xc, h_lru, lru_cw, wa, w["lru_gate_a_b"], wx, w["lru_gate_x_b"], w["lru_lambda"],
                  w["lru_norm_gain"], dproj_part=0, dmix_at=(1, 0))],
        "mix_bwd")
    g["norm1_gain"] = dg1[:1]
    g["ret_norm_gain"] = dgain_ret[:1]
    lru_acc = lru_acc[::SUBLANES]
    g_lru_cw = lru_acc[:LRU_CONV]
    for name in ("conv_b", "gate_a_b", "gate_x_b", "lambda", "norm_gain"):
        g["lru_" + name] = lru_acc[LRU_ACC[name]:LRU_ACC[name] + 1]
    g["lru_gate_a_w"] = _diag_blocks(dwa, LRU_HEADS)
    g["lru_gate_x_w"] = _diag_blocks(dwx, LRU_HEADS)
    rep_packed, rep_layout = _pack([g[n] for n in REPLICATED] + [loss_local], SUBLANES)
    g_in, (down_parts, out_parts, rep_part) = _mm_tn(u1, dproj, "w_in_wgrad", blocks=N_DEV,
                                                     comm=_both(_chip_exchange(low_sums), _gather_first([rep_packed])))
    adamw("ffn_down_w", down_parts)
    adamw("w_out", out_parts)
    g_conv = _conv_rows(_column_blocks(g_lru_cw), _column_blocks(g_ffn_cw), GRAD_DTYPE, 2 * SUBLANES)
    in_sums = to_owner_chips([g_in, g_conv], ["w_in", "conv"], "in")
    in_parts, conv_parts, rep_parts = _run_comms([_both(_chip_exchange(in_sums), _gather_second([rep_part]))],
                                                 "last_grads_exchange")
    adamw("w_in", in_parts)
    pad16 = lambda d: _conv_rows(d["lru_conv_w"], d["ffn_conv_w"], F32, 2 * SUBLANES)
    conv_res = _sum_adamw(conv_parts, pad16(w), pad16(mom_m), pad16(mom_v), "adamw_conv")
    for n, lo, hi in (("lru_conv_w", 0, n_lru), ("ffn_conv_w", n_lru, n_lru + w["ffn_conv_w"].size)):
        res[n] = [r.reshape(-1)[lo:hi].reshape(w[n].shape) for r in conv_res]
    no_state = jnp.zeros_like(loss_local)
    rep_res = _sum_adamw(rep_parts, *[_pack([d[n] for n in REPLICATED] + [no_state], SUBLANES)[0]
                                      for d in (w, mom_m, mom_v)], "adamw_replicated")
    for k in range(4):
        for n, a in zip(REPLICATED, _unpack(rep_res[k], rep_layout)):
            res.setdefault(n, [None] * 4)[k] = a
    loss = _unpack(rep_res[0], rep_layout)[-1][0, 0]

    outs = [loss, grad_x[None]]
    for k in range(4):
        outs += [res[n][k].reshape(out_shape[n]) for n in WEIGHTS]
    return tuple(outs)
```

```python
import math

import numpy as np
import jax
import jax.numpy as jnp
from jax import lax
from jax.experimental import pallas as pl
from jax.experimental.pallas import tpu as pltpu

F32 = jnp.float32
BF16 = jnp.bfloat16
MXU_DTYPE = jnp.bfloat16
GRAD_DTYPE = jnp.bfloat16

N_DEV = 8
N_CHIPS = 4
D_MODEL = 1024
D_LRU = 512
LRU_HEADS = 8
LRU_CONV = 4
LRU_C = 8.0
D_RET = 512
RET_HEADS = 4
RET_HEAD_DIM = 128
RET_CHUNK = 128
ROPE_BASE = 10000.0
D_IN = 3072
D_FF = 3072
FFN_CONV = 3
NORM_EPS = 1e-6

ADAM_LR = 0.001
ADAM_B1 = 0.9
ADAM_B2 = 0.999
ADAM_EPS = 1e-08
ADAM_WD = 0.01
ADAM_STEP = 10

SUBLANES = 8
LANES = 128
VMEM_LIMIT = 32 * 1024 * 1024
FUSED_VMEM_LIMIT = 48 * 1024 * 1024
FFN_BWD_VMEM_LIMIT = 56 * 1024 * 1024

ROW_TILE = 256
PROJ_ROW_TILE = 512
WGRAD_ROWS = 2048
WGRAD_TILE = 1024
WGRAD_BLOCK_COLUMNS = 768

MESH = pl.DeviceIdType.MESH
ANY = pl.BlockSpec(memory_space=pl.ANY)


def _dot(a, b):
    return jnp.dot(a.astype(MXU_DTYPE), b.astype(MXU_DTYPE), preferred_element_type=F32)


def _dot_nt(a, b):
    return lax.dot_general(a.astype(MXU_DTYPE), b.astype(MXU_DTYPE), (((1,), (1,)), ((), ())),
                           preferred_element_type=F32)


def _dot_tn(a, b):
    return lax.dot_general(a.astype(MXU_DTYPE), b.astype(MXU_DTYPE), (((0,), (0,)), ((), ())),
                           preferred_element_type=F32)


def _sigmoid(x):
    return 0.5 + 0.5 * jnp.tanh(0.5 * x)


_GELU_C = math.sqrt(2.0 / math.pi)
_GELU_C3 = _GELU_C * 0.044715


def _gelu_parts(x):
    x2 = x * x
    t = jnp.tanh(x * (_GELU_C + _GELU_C3 * x2))
    cdf = 0.5 + 0.5 * t
    g = x * cdf
    dg = cdf + (0.5 * x) * (1.0 - t * t) * (_GELU_C + (3.0 * _GELU_C3) * x2)
    return g, dg


def _gelu(x):
    t = jnp.tanh(_GELU_C * (x + 0.044715 * (x * x * x)))
    return x * (0.5 * (1.0 + t))


def _softplus(x):
    return jnp.maximum(x, 0.0) + jnp.log1p(jnp.exp(-jnp.abs(x)))


def _bcast_row(x, r, rows=SUBLANES):
    return jnp.broadcast_to(x[r:r + 1, :], (rows, x.shape[1]))


def _colsum8(x):
    return jnp.broadcast_to(jnp.sum(x, axis=0, keepdims=True), (SUBLANES, x.shape[1]))


def _shift_down(prev8, tile, s):
    if s == 0:
        return tile
    ext = jnp.concatenate([prev8, tile], axis=0)
    return pltpu.roll(ext, s, 0)[SUBLANES:, :]


def _shift_up(tile, next8, s):
    if s == 0:
        return tile
    ext = jnp.concatenate([tile, next8], axis=0)
    return pltpu.roll(ext, SUBLANES - s, 0)[SUBLANES:, :]


def _group_scan(a, b, reverse, fill=lambda: None):
    n = a.shape[0]
    row = lax.broadcasted_iota(jnp.int32, a.shape, 0) & (SUBLANES - 1)
    for s in (1, 2, 4):
        if s > 1:
            fill()
        shift = (n - s) if reverse else s
        a_sh = pltpu.roll(a, shift, 0)
        b_sh = pltpu.roll(b, shift, 0)
        m = (row <= SUBLANES - 1 - s) if reverse else (row >= s)
        b = jnp.where(m, a * b_sh + b, b)
        a = jnp.where(m, a * a_sh, a)
    return a, b


def _carry_scan(a_ref, b_ref, out_ref, carry0, reverse):
    n_groups = a_ref.shape[0] // SUBLANES
    carry = carry0
    for i in range(n_groups):
        r0 = ((n_groups - 1 - i) if reverse else i) * SUBLANES
        hg = a_ref[r0:r0 + SUBLANES, :] * carry + b_ref[r0:r0 + SUBLANES, :]
        out_ref[r0:r0 + SUBLANES, :] = hg
        carry = _bcast_row(hg, 0 if reverse else SUBLANES - 1)
    return carry


def _rms_fwd(h, gain):
    rstd = lax.rsqrt(jnp.mean(h * h, axis=-1, keepdims=True) + NORM_EPS)
    n = h * rstd
    return n, rstd, n * gain


def _rms_bwd(dy, n, rstd, gain):
    dn = dy * gain
    dh = rstd * (dn - n * jnp.mean(dn * n, axis=-1, keepdims=True))
    return dh, _colsum8(dy * n)


def _halo_rows(dtype):
    return SUBLANES * (4 // jnp.dtype(dtype).itemsize)


def _halo_map(tile_rows, col, halo_rows=SUBLANES):
    per = tile_rows // halo_rows
    return lambda i: (jnp.maximum(i * per - 1, 0), col)


def _resident(shape):
    return pl.BlockSpec(shape, lambda *_: (0,) * len(shape), pipeline_mode=pl.Buffered(1))


def _place():
    x, y, c = lax.axis_index("x"), lax.axis_index("y"), lax.axis_index("c")
    chips = [(1 - x, y), (x, 1 - y), (1 - x, 1 - y)]
    return x, y, c, chips


def _dev(x, y, c):
    return 4 * x + 2 * y + c


class _Copy:
    def __init__(self, make):
        self.make = make

    def start(self):
        self.make().start()

    def wait(self):
        self.make().wait()

    def wait_send(self):
        self.make().wait_send()

    def wait_recv(self):
        self.make().wait_recv()


def _remote(src, dst, send_sem, recv_sem, to):
    return _Copy(lambda: pltpu.make_async_remote_copy(src_ref=src, dst_ref=dst, send_sem=send_sem, recv_sem=recv_sem,
                                                      device_id=to, device_id_type=MESH))


def _local(src, dst, sem):
    return _Copy(lambda: pltpu.make_async_copy(src, dst, sem))


class _Comm:
    def __init__(self, operands, out_shape, sems, descs, aliases=()):
        self.operands, self.out_shape, self.sems, self.descs, self.aliases = operands, out_shape, sems, descs, aliases

    def start(self, ins, outs, sems):
        local, sends, _ = self.descs(ins, outs, sems)
        for cp in sends + local:
            cp.start()

    def wait(self, ins, outs, sems):
        local, sends, recvs = self.descs(ins, outs, sems)
        for cp in recvs:
            cp.wait_recv()
        for cp in sends:
            cp.wait_send()
        for cp in local:
            cp.wait()


def _gather_first(shards):
    n = len(shards)

    def descs(ins, outs, sems):
        send, recv, loc = sems
        x, y, c, chips = _place()
        me = _dev(x, y, c)
        targets = [(x, y, 1 - c)] + [(*chip, c) for chip in chips]
        local, sends, recvs = [], [], []
        for t in range(n):
            local.append(_local(ins[t], outs[t].at[me], loc.at[t]))
            for k, to in enumerate(targets):
                i = 4 * t + k
                sends.append(_remote(ins[t], outs[t].at[me], send.at[i], recv.at[i], to))
                recvs.append(_remote(ins[t], outs[t].at[_dev(*to)], send.at[i], recv.at[i], to))
        return local, sends, recvs

    return _Comm(list(shards), [jax.ShapeDtypeStruct((N_DEV,) + s.shape, s.dtype) for s in shards],
                 [pltpu.SemaphoreType.DMA((4 * n,)), pltpu.SemaphoreType.DMA((4 * n,)), pltpu.SemaphoreType.DMA((n,))],
                 descs)


def _gather_second(gathered):
    n = len(gathered)

    def descs(ins, outs, sems):
        send, recv = sems
        x, y, c, chips = _place()
        sends, recvs = [], []
        for t in range(n):
            for j, chip in enumerate(chips):
                i = 3 * t + j
                have, get = _dev(*chip, c), _dev(*chip, 1 - c)
                sends.append(_remote(outs[t].at[have], outs[t].at[have], send.at[i], recv.at[i], (x, y, 1 - c)))
                recvs.append(_remote(outs[t].at[have], outs[t].at[get], send.at[i], recv.at[i], (x, y, 1 - c)))
        return [], sends, recvs

    return _Comm(list(gathered), [jax.ShapeDtypeStruct(g.shape, g.dtype) for g in gathered],
                 [pltpu.SemaphoreType.DMA((3 * n,)), pltpu.SemaphoreType.DMA((3 * n,))], descs,
                 aliases=[(t, t) for t in range(n)])


def _pair_exchange(blocks):
    n = len(blocks)

    def descs(ins, outs, sems):
        send, recv = sems
        x, y, c, _ = _place()
        sends, recvs = [], []
        for t in range(n):
            for q in range(N_CHIPS):
                i = N_CHIPS * t + q
                cp = _remote(ins[t].at[2 * q + 1 - c], outs[t].at[q], send.at[i], recv.at[i], (x, y, 1 - c))
                sends.append(cp)
                recvs.append(cp)
        return [], sends, recvs

    return _Comm(list(blocks), [jax.ShapeDtypeStruct((N_CHIPS,) + b.shape[1:], b.dtype) for b in blocks],
                 [pltpu.SemaphoreType.DMA((N_CHIPS * n,)), pltpu.SemaphoreType.DMA((N_CHIPS * n,))], descs)


def _chip_exchange(blocks):
    n = len(blocks)

    def descs(ins, outs, sems):
        send, recv, loc = sems
        x, y, c, chips = _place()
        me = 2 * x + y
        local, sends, recvs = [], [], []
        for t in range(n):
            local.append(_local(ins[t].at[me], outs[t].at[me], loc.at[t]))
            for j, (px, py) in enumerate(chips):
                i = 3 * t + j
                q = 2 * px + py
                sends.append(_remote(ins[t].at[q], outs[t].at[me], send.at[i], recv.at[i], (px, py, c)))
                recvs.append(_remote(ins[t].at[q], outs[t].at[q], send.at[i], recv.at[i], (px, py, c)))
        return local, sends, recvs

    return _Comm(list(blocks), [jax.ShapeDtypeStruct(b.shape, b.dtype) for b in blocks],
                 [pltpu.SemaphoreType.DMA((3 * n,)), pltpu.SemaphoreType.DMA((3 * n,)), pltpu.SemaphoreType.DMA((n,))],
                 descs)


def _both(a, b):
    na, oa, sa = len(a.operands), len(a.out_shape), len(a.sems)

    def descs(ins, outs, sems):
        local_a, sends_a, recvs_a = a.descs(ins[:na], outs[:oa], sems[:sa])
        local_b, sends_b, recvs_b = b.descs(ins[na:], outs[oa:], sems[sa:])
        return local_a + local_b, sends_a + sends_b, recvs_a + recvs_b

    return _Comm(a.operands + b.operands, a.out_shape + b.out_shape, a.sems + b.sems, descs,
                 aliases=list(a.aliases) + [(na + i, oa + o) for i, o in b.aliases])


def _run_comms(comms, name):
    first = comms[0]
    n_in, n_out = len(first.operands), len(first.out_shape)

    def body(*refs):
        ins, outs, sems = refs[:n_in], refs[n_in:n_in + n_out], list(refs[n_in + n_out:])
        for k, comm in enumerate(comms):
            mine = [sems.pop(0) for _ in comm.sems]
            comm.start(ins if k == 0 else outs, outs, mine)
            comm.wait(ins if k == 0 else outs, outs, mine)

    outs = pl.pallas_call(
        body, name=name, out_shape=first.out_shape, in_specs=[ANY] * n_in, out_specs=[ANY] * n_out,
        scratch_shapes=[s for comm in comms for s in comm.sems], input_output_aliases=dict(first.aliases),
    )(*first.operands)
    return list(outs)


def _call(body, *, name, grid, in_specs, out_specs, out_shape, operands, scratch_shapes=(), comm=None, aliases=None,
          vmem_limit=VMEM_LIMIT):
    sem = ("arbitrary",) * len(grid)
    params = pltpu.CompilerParams(dimension_semantics=sem, vmem_limit_bytes=vmem_limit)
    aliases = dict(aliases or {})
    if comm is None:
        return pl.pallas_call(body, name=name, grid=grid, in_specs=in_specs, out_specs=out_specs, out_shape=out_shape,
                              scratch_shapes=list(scratch_shapes), input_output_aliases=aliases,
                              compiler_params=params)(*operands)
    n_in, n_out, n_scr = len(in_specs), len(out_specs), len(scratch_shapes)
    c_in, c_out = len(comm.operands), len(comm.out_shape)

    def wrapped(*refs):
        refs = list(refs)
        ins, refs = refs[:n_in], refs[n_in:]
        cins, refs = refs[:c_in], refs[c_in:]
        outs, refs = refs[:n_out], refs[n_out:]
        couts, refs = refs[:c_out], refs[c_out:]
        scr, csems = refs[:n_scr], refs[n_scr:]
        first = last = None
        for axis, size in enumerate(grid):
            at_first, at_last = pl.program_id(axis) == 0, pl.program_id(axis) == size - 1
            first = at_first if first is None else first & at_first
            last = at_last if last is None else last & at_last

        @pl.when(first)
        def _():
            comm.start(cins, couts, csems)

        body(*ins, *outs, *scr)

        @pl.when(last)
        def _():
            comm.wait(cins, couts, csems)

    res = pl.pallas_call(
        wrapped, name=name, grid=grid, in_specs=list(in_specs) + [ANY] * c_in, out_specs=list(out_specs) + [ANY] * c_out,
        out_shape=list(out_shape) + list(comm.out_shape), scratch_shapes=list(scratch_shapes) + list(comm.sems),
        input_output_aliases={**aliases, **{n_in + i: n_out + o for i, o in comm.aliases}}, compiler_params=params,
    )(*operands, *comm.operands)
    return list(res[:n_out]), list(res[n_out:])


def _mm_tn(a, b, name, blocks=1, comm=None):
    t, m = a.shape
    n = b.shape[1]
    tk = min(WGRAD_ROWS, t)
    nk = t // tk
    cb = n // blocks
    per = max(1, WGRAD_BLOCK_COLUMNS // cb) if blocks > 1 else 1
    tn = per * cb if blocks > 1 else min(WGRAD_TILE, n)
    tm = min(WGRAD_TILE, m)
    assert blocks == 1 or tm == m

    def body(a_ref, b_ref, o_ref, acc):
        k = pl.program_id(2)

        @pl.when(k == 0)
        def _():
            acc[...] = jnp.zeros_like(acc)
        acc[...] += _dot_tn(a_ref[...], b_ref[...])

        @pl.when(k == nk - 1)
        def _():
            if blocks == 1:
                o_ref[...] = acc[...].astype(o_ref.dtype)
            else:
                for s in range(per):
                    o_ref[s] = acc[:, s * cb:(s + 1) * cb].astype(o_ref.dtype)

    if blocks == 1:
        out_spec = pl.BlockSpec((tm, tn), lambda i, j, k: (i, j))
        out_shape = jax.ShapeDtypeStruct((m, n), GRAD_DTYPE)
    else:
        out_spec = pl.BlockSpec((per, m, cb), lambda i, j, k: (j, 0, 0))
        out_shape = jax.ShapeDtypeStruct((blocks, m, cb), GRAD_DTYPE)
    res = _call(body, name=name, grid=(m // tm, n // tn, nk), comm=comm,
                in_specs=[pl.BlockSpec((tk, tm), lambda i, j, k: (k, i)), pl.BlockSpec((tk, tn), lambda i, j, k: (k, j))],
                out_specs=[out_spec], out_shape=[out_shape], operands=[a, b],
                scratch_shapes=[pltpu.VMEM((tm, tn), F32)])
    return res[0] if comm is None else (res[0][0], res[1])


def _inproj_fwd(x, g1, w_blocks, comm):
    t = x.shape[0]
    tm = min(PROJ_ROW_TILE, t)
    nb, _, cb = w_blocks.shape

    def body(x_ref, g_ref, w_ref, u_ref, p_ref):
        _, _, u = _rms_fwd(x_ref[...], g_ref[...])
        u = u.astype(MXU_DTYPE)
        u_ref[...] = u
        for d in range(nb):
            p_ref[:, d * cb:(d + 1) * cb] = _dot(u, w_ref[d]).astype(p_ref.dtype)

    return _call(body, name="inproj_fwd", grid=(t // tm,), comm=comm,
                 in_specs=[pl.BlockSpec((tm, D_MODEL), lambda i: (i, 0)), pl.BlockSpec((1, D_MODEL), lambda i: (0, 0)),
                           _resident(w_blocks.shape)],
                 out_specs=[pl.BlockSpec((tm, D_MODEL), lambda i: (i, 0)), pl.BlockSpec((tm, D_IN), lambda i: (i, 0))],
                 out_shape=[jax.ShapeDtypeStruct((t, D_MODEL), MXU_DTYPE), jax.ShapeDtypeStruct((t, D_IN), MXU_DTYPE)],
                 operands=[x, g1, w_blocks])


def _lru_gates(xc, wa, ba, wx, bx, sp, fill=lambda: None):
    r = _sigmoid(_dot(xc, wa) + ba)
    fill()
    ig = _sigmoid(_dot(xc, wx) + bx)
    fill()
    log_a = (-LRU_C) * r * sp
    a = jnp.exp(log_a)
    m = jnp.sqrt(-jnp.tanh(log_a) * (a * a + 1.0))
    return r, ig, a, m


def _fused(parts, name, comm=None):
    grid = parts[0]["grid"]
    assert all(p["grid"] == grid for p in parts)
    counts = [(len(p["in_specs"]), len(p["out_specs"]), len(p.get("scratch_shapes", ()))) for p in parts]

    def body(*refs):
        refs = list(refs)
        groups = []
        for kind in range(3):
            taken = []
            for c in counts:
                taken.append(refs[:c[kind]])
                refs = refs[c[kind]:]
            groups.append(taken)
        ins, outs, scr = groups
        pending = []

        def fill(n=None):
            for _ in range(share if n is None else n):
                if pending:
                    pending.pop(0)()

        ctx = dict(outs=outs, scratch=scr, fill=fill)
        run = lambda key: [p[key](*ins[k], *outs[k], *scr[k], ctx) for k, p in enumerate(parts) if key in p]
        run("head")
        for pieces in run("units"):
            pending.extend(pieces)
        points = sum(p.get("fill_points", 0) for p in parts)
        share = -(-len(pending) // max(points, 1))
        run("body")
        fill(len(pending))
        run("tail")

    cat = lambda key: [x for p in parts for x in p.get(key, ())]
    res = _call(body, name=name, grid=grid, comm=comm, vmem_limit=FUSED_VMEM_LIMIT,
                in_specs=cat("in_specs"), out_specs=cat("out_specs"),
                out_shape=cat("out_shape"), scratch_shapes=cat("scratch_shapes"), operands=cat("operands"))
    outs, side = (res if comm is not None else (res, None))
    split, at = [], 0
    for _, n_out, _ in counts:
        split.append(list(outs[at:at + n_out]))
        at += n_out
    return split if comm is None else (split, side)


def _lru_fwd(proj, conv_w, conv_b, wa, ba, wx, bx, lam, gain):
    t = proj.shape[0]
    tm = min(ROW_TILE, t)
    c = D_LRU

    def body(x_ref, xh_ref, g_ref, cw_ref, cb_ref, wa_ref, ba_ref, wx_ref, bx_ref, lam_ref, gain_ref,
             xc_ref, h_ref, y_ref, a_scr, b_scr, carry, ctx):
        fill = ctx["fill"]
        i = pl.program_id(0)

        @pl.when(i == 0)
        def _():
            carry[...] = jnp.zeros_like(carry)

        fill()
        x = x_ref[...].astype(F32)
        prev = jnp.where(i == 0, 0.0, xh_ref[...].astype(F32)[-SUBLANES:, :])
        cw = cw_ref[...]
        xc = cb_ref[...] + cw[LRU_CONV - 1:LRU_CONV, :] * x
        for k in range(LRU_CONV - 1):
            xc = xc + cw[k:k + 1, :] * _shift_down(prev, x, LRU_CONV - 1 - k)
        xc_ref[...] = xc
        fill()
        sp = _softplus(-lam_ref[...])
        _, ig, a, m = _lru_gates(xc, wa_ref[...], ba_ref[...], wx_ref[...], bx_ref[...], sp, fill)
        fill()
        ga, gb = _group_scan(a, m * (ig * xc), reverse=False, fill=fill)
        a_scr[...] = ga
        b_scr[...] = gb
        fill()
        carry[...] = _carry_scan(a_scr, b_scr, h_ref, carry[...], reverse=False)
        fill()
        z = h_ref[...] * _gelu(g_ref[...].astype(F32))
        fill()
        _, _, y = _rms_fwd(z, gain_ref[...])
        y_ref[...] = y.astype(y_ref.dtype)

    row = lambda i: (i, 0)
    full = lambda i: (0, 0)
    vec = pl.BlockSpec((1, c), full)
    hb = _halo_rows(proj.dtype)
    return dict(body=body, grid=(t // tm,), fill_points=10,
                in_specs=[pl.BlockSpec((tm, c), row), pl.BlockSpec((hb, c), _halo_map(tm, 0, hb)),
                          pl.BlockSpec((tm, c), lambda i: (i, 1)),
                          pl.BlockSpec((LRU_CONV, c), full), vec, pl.BlockSpec((c, c), full), vec,
                          pl.BlockSpec((c, c), full), vec, vec, vec],
                out_specs=[pl.BlockSpec((tm, c), row), pl.BlockSpec((tm, c), row), pl.BlockSpec((tm, c), row)],
                out_shape=[jax.ShapeDtypeStruct((t, c), F32), jax.ShapeDtypeStruct((t, c), F32),
                           jax.ShapeDtypeStruct((t, c), MXU_DTYPE)],
                scratch_shapes=[pltpu.VMEM((tm, c), F32), pltpu.VMEM((tm, c), F32), pltpu.VMEM((SUBLANES, c), F32)],
                operands=[proj, proj, proj, conv_w, conv_b, wa, ba, wx, bx, lam, gain])


def _ret_consts():
    c = RET_CHUNK
    log_g = jnp.log1p(-jnp.exp2(-5.0 - jnp.arange(RET_HEADS, dtype=F32)))
    idx = jnp.arange(c, dtype=F32)
    diff = idx[:, None] - idx[None, :]
    decay = jnp.where(diff[None] >= 0, jnp.exp(jnp.maximum(diff, 0.0)[None] * log_g[:, None, None]), 0.0)
    zeta = jnp.exp((c - 1 - idx)[None, :] * log_g[:, None])
    xi = jnp.exp((idx + 1.0)[None, :] * log_g[:, None])
    spread = lambda v: jnp.repeat(v.T, RET_HEAD_DIM, axis=1)
    log_g_np = np.log1p(-np.exp2(-5.0 - np.arange(RET_HEADS, dtype=np.float32))).astype(np.float32)
    g_chunk = [float(np.exp(np.float32(c) * lg)) for lg in log_g_np]
    return decay, spread(xi), spread(zeta), g_chunk


def _rope_tables(t):
    pos = np.arange(t, dtype=np.float32)
    inv_freq = np.float32(ROPE_BASE) ** (-np.arange(0, RET_HEAD_DIM, 2, dtype=np.float32) / np.float32(RET_HEAD_DIM))
    ang = (pos[:, None] * inv_freq.astype(np.float32)[None, :]).astype(np.float32).astype(np.float64)
    cos, sin = np.cos(ang).astype(np.float32), np.sin(ang).astype(np.float32)
    return jnp.asarray(np.concatenate([cos, cos], axis=-1)), jnp.asarray(np.concatenate([-sin, sin], axis=-1))


def _rope(x, cos2, sin_signed):
    return x * cos2 + pltpu.roll(x, RET_HEAD_DIM // 2, 1) * sin_signed


def _rope_bwd(d, cos2, sin_signed):
    return d * cos2 + pltpu.roll(d * sin_signed, RET_HEAD_DIM // 2, 1)


RET_SCALE = RET_HEAD_DIM ** -0.5


RET_CHUNKS_PER_STEP = 2


def _ret_fwd(proj, cos2, sin_signed, gain):
    t = proj.shape[0]
    c, d, nh = RET_CHUNK, RET_HEAD_DIM, RET_HEADS
    n_chunks = t // c
    per = RET_CHUNKS_PER_STEP if n_chunks % RET_CHUNKS_PER_STEP == 0 else 1
    rows = per * c
    decay, xi, zeta, g_chunk = _ret_consts()

    def units(qk_ref, vg_ref, cos_ref, sin_ref, dec_ref, xi_ref, zeta_ref, gain_ref, o_ref, y_ref, st_ref, state, ctx):
        cur = [None] * nh

        def start():
            @pl.when(pl.program_id(0) == 0)
            def _():
                state[...] = jnp.zeros_like(state)
            for h in range(nh):
                cur[h] = state[h]

        def retain(s, h, keep):
            rs = slice(s * c, (s + 1) * c)
            cos2, sin_s = cos_ref[rs, :], sin_ref[rs, :]
            lo = h * d
            q = _rope(qk_ref[rs, lo:lo + d].astype(F32), cos2, sin_s)
            k = _rope(qk_ref[rs, D_RET + lo:D_RET + lo + d].astype(F32), cos2, sin_s) * RET_SCALE
            v = vg_ref[rs, lo:lo + d]
            s_prev = cur[h]
            st_ref[s, h] = s_prev
            scores = _dot_nt(q, k) * dec_ref[h]
            o = _dot(scores, v) + _dot(q * xi_ref[:, lo:lo + d], s_prev)
            cur[h] = s_prev * g_chunk[h] + _dot_tn(k * zeta_ref[:, lo:lo + d], v)
            o_ref[rs, lo:lo + d] = o
            keep["o"] = o

        def normalise(s, h, keep):
            rs = slice(s * c, (s + 1) * c)
            lo = h * d
            o = keep["o"]
            g = vg_ref[rs, D_RET + lo:D_RET + lo + d].astype(F32)
            mu = jnp.mean(o, axis=-1, keepdims=True)
            oc = o - mu
            on = oc * lax.rsqrt(jnp.mean(oc * oc, axis=-1, keepdims=True) + NORM_EPS)
            y_ref[rs, lo:lo + d] = (on * gain_ref[:, lo:lo + d] * (g * _sigmoid(g))).astype(y_ref.dtype)

        def end():
            for h in range(nh):
                state[h] = cur[h]

        pieces = [start]
        for s in range(per):
            for h in range(nh):
                keep = {}
                pieces += [lambda s=s, h=h, keep=keep: retain(s, h, keep),
                           lambda s=s, h=h, keep=keep: normalise(s, h, keep)]
        return pieces + [end]

    full2 = lambda i: (0, 0)
    return dict(units=units, grid=(n_chunks // per,),
                in_specs=[pl.BlockSpec((rows, 2 * D_RET), lambda i: (i, 1)),
                          pl.BlockSpec((rows, 2 * D_RET), lambda i: (i, 2)),
                          pl.BlockSpec((rows, d), lambda i: (i, 0)), pl.BlockSpec((rows, d), lambda i: (i, 0)),
                          pl.BlockSpec((nh, c, c), lambda i: (0, 0, 0)), pl.BlockSpec((c, D_RET), full2),
                          pl.BlockSpec((c, D_RET), full2), pl.BlockSpec((1, D_RET), full2)],
                out_specs=[pl.BlockSpec((rows, D_RET), lambda i: (i, 0)), pl.BlockSpec((rows, D_RET), lambda i: (i, 0)),
                           pl.BlockSpec((per, nh, d, d), lambda i: (i, 0, 0, 0))],
                out_shape=[jax.ShapeDtypeStruct((t, D_RET), F32), jax.ShapeDtypeStruct((t, D_RET), MXU_DTYPE),
                           jax.ShapeDtypeStruct((n_chunks, nh, d, d), F32)],
                scratch_shapes=[pltpu.VMEM((nh, d, d), F32)],
                operands=[proj, proj, cos2, sin_signed, decay, xi, zeta, gain])


def _outproj_fwd(x, y_lru, y_ret, w_out, g2, comm):
    t = x.shape[0]
    tm = min(PROJ_ROW_TILE, t)

    def body(x_ref, yl_ref, yr_ref, w_ref, g_ref, h1_ref, u2_ref):
        h1 = x_ref[...] + _dot(yl_ref[...], w_ref[:D_LRU, :]) + _dot(yr_ref[...], w_ref[D_LRU:, :])
        h1_ref[...] = h1
        _, _, u = _rms_fwd(h1, g_ref[...])
        u2_ref[...] = u.astype(u2_ref.dtype)

    row = lambda i: (i, 0)
    return _call(body, name="outproj_fwd", grid=(t // tm,), comm=comm,
                 in_specs=[pl.BlockSpec((tm, D_MODEL), row), pl.BlockSpec((tm, D_LRU), row), pl.BlockSpec((tm, D_RET), row),
                           _resident((D_MODEL, D_MODEL)), pl.BlockSpec((1, D_MODEL), lambda i: (0, 0))],
                 out_specs=[pl.BlockSpec((tm, D_MODEL), row), pl.BlockSpec((tm, D_MODEL), row)],
                 out_shape=[jax.ShapeDtypeStruct((t, D_MODEL), F32), jax.ShapeDtypeStruct((t, D_MODEL), MXU_DTYPE)],
                 operands=[x, y_lru, y_ret, w_out, g2])


FFN_TN = 768
FFN_NJ = D_FF // FFN_TN
FFN_GROUP = 4


def _ffn_fwd(u2, w_blocks, conv_w, conv_b, w_down, h1, gf, target):
    t = u2.shape[0]
    tm = min(ROW_TILE, t)
    tn, nj, group = FFN_TN, FFN_NJ, FFN_GROUP
    ng, tw = nj // group, group * tn
    hb = _halo_rows(u2.dtype)
    assert w_blocks.shape == (2 * nj, D_MODEL, tn)

    def conv(ext, col, up_ref, conv_ref, cw_ref, cb_ref, first):
        x = ext[hb:, :]
        up_ref[:, col] = x.astype(up_ref.dtype)
        prev = jnp.where(first, 0.0, ext[hb - SUBLANES:hb, :])
        cw = cw_ref[:, col]
        y = cb_ref[:, col] + cw[FFN_CONV - 1:FFN_CONV, :] * x
        for k in range(FFN_CONV - 1):
            y = y + cw[k:k + 1, :] * _shift_down(prev, x, FFN_CONV - 1 - k)
        conv_ref[:, col] = y.astype(conv_ref.dtype)
        return y

    def body(u_ref, uh_ref, w_ref, cwa_ref, cwv_ref, cba_ref, cbv_ref, wd_ref, h1_ref, gf_ref, tg_ref,
             upa_ref, upv_ref, ca_ref, cv_ref, act_ref, dh_ref, dhb_ref, dgf_ref, loss_ref, acc):
        i, jg = pl.program_id(0), pl.program_id(1)

        @pl.when((i == 0) & (jg == 0))
        def _():
            dgf_ref[...] = jnp.zeros_like(dgf_ref)
            loss_ref[...] = jnp.zeros_like(loss_ref)

        @pl.when(jg == 0)
        def _():
            acc[...] = jnp.zeros_like(acc)

        u_ext = jnp.concatenate([uh_ref[...], u_ref[...]], axis=0)

        def project(jj):
            j = jg * group + jj
            return _dot(u_ext, w_ref[j]), _dot(u_ext, w_ref[nj + j])

        down, ahead = None, project(0)
        for jj in range(group):
            col = slice(jj * tn, (jj + 1) * tn)
            j = jg * group + jj
            ext_a, ext_v = ahead
            if jj + 1 < group:
                ahead = project(jj + 1)
            a = conv(ext_a, col, upa_ref, ca_ref, cwa_ref, cba_ref, i == 0)
            v = conv(ext_v, col, upv_ref, cv_ref, cwv_ref, cbv_ref, i == 0)
            act = (_gelu(a) * v).astype(act_ref.dtype)
            act_ref[:, col] = act
            part = _dot(act, wd_ref[pl.ds(pl.multiple_of(j * tn, tn), tn), :])
            down = part if down is None else down + part
        acc[...] += down

        @pl.when(jg == ng - 1)
        def _():
            n, rstd, y = _rms_fwd(h1_ref[...] + acc[...], gf_ref[...])
            err = y - tg_ref[...]
            loss_ref[...] += (0.5 / D_MODEL) * jnp.sum(err * err)
            dh, dgf = _rms_bwd(err * (1.0 / D_MODEL), n, rstd, gf_ref[...])
            dgf_ref[...] += dgf
            dh_ref[...] = dh
            dhb_ref[...] = dh.astype(dhb_ref.dtype)

    per = tm // hb
    row = lambda i, j: (i, 0)
    const = lambda i, j: (0, 0)
    tile = pl.BlockSpec((tm, tw), lambda i, j: (i, j))
    return _call(body, name="ffn_fwd", grid=(t // tm, ng), vmem_limit=FUSED_VMEM_LIMIT,
                 in_specs=[pl.BlockSpec((tm, D_MODEL), row),
                           pl.BlockSpec((hb, D_MODEL), lambda i, j: (jnp.maximum(i * per - 1, 0), 0)),
                           _resident(w_blocks.shape),
                           pl.BlockSpec((FFN_CONV, tw), lambda i, j: (0, j)),
                           pl.BlockSpec((FFN_CONV, tw), lambda i, j: (0, j + ng)),
                           pl.BlockSpec((1, tw), lambda i, j: (0, j)), pl.BlockSpec((1, tw), lambda i, j: (0, j + ng)),
                           _resident((D_FF, D_MODEL)),
                           pl.BlockSpec((tm, D_MODEL), row), pl.BlockSpec((1, D_MODEL), const),
                           pl.BlockSpec((tm, D_MODEL), row)],
                 out_specs=[tile] * 5 + [pl.BlockSpec((tm, D_MODEL), row),
                            pl.BlockSpec((tm, D_MODEL), row), pl.BlockSpec((SUBLANES, D_MODEL), const),
                            pl.BlockSpec((SUBLANES, LANES), const)],
                 out_shape=[jax.ShapeDtypeStruct((t, D_FF), MXU_DTYPE)] * 5 + [
                            jax.ShapeDtypeStruct((t, D_MODEL), F32),
                            jax.ShapeDtypeStruct((t, D_MODEL), MXU_DTYPE), jax.ShapeDtypeStruct((SUBLANES, D_MODEL), F32),
                            jax.ShapeDtypeStruct((SUBLANES, LANES), F32)],
                 scratch_shapes=[pltpu.VMEM((tm, D_MODEL), F32)],
                 operands=[u2, u2, w_blocks, conv_w, conv_w, conv_b, conv_b, w_down, h1, gf, target])


FFN_ACC_ROWS = SUBLANES * (FFN_CONV + 1)


def _ffn_bwd(dh2, dh2_b, w_down, up_a, up_v, conv_a, conv_v, conv_w, w_up_blocks, h1, g2, comm):
    t = up_a.shape[0]
    tm = min(ROW_TILE, t)
    tn, nj, group = FFN_TN, FFN_NJ, FFN_GROUP
    ng, tw = nj // group, group * tn
    ni = t // tm
    assert w_up_blocks.shape == (2 * nj, D_MODEL, tn)

    def conv_bwd(dy, x, cw, acc_ref, carry_ref, dup_ref, col):
        nxt = carry_ref[...]
        carry_ref[...] = dy[:SUBLANES, :]
        ahead = [_shift_up(dy, nxt, FFN_CONV - 1 - k) for k in range(FFN_CONV)]
        dx = cw[FFN_CONV - 1:FFN_CONV, :] * dy
        for k in range(FFN_CONV - 1):
            dx = dx + cw[k:k + 1, :] * ahead[k]
        dx = dx.astype(dup_ref.dtype)
        dup_ref[:, col] = dx
        for k in range(FFN_CONV):
            acc_ref[k * SUBLANES:(k + 1) * SUBLANES, :] += _colsum8(ahead[k] * x)
        acc_ref[FFN_CONV * SUBLANES:, :] += _colsum8(dy)
        return dx

    def body(dh_ref, dhb_ref, wd_ref, ua_ref, uv_ref, ca_ref, cv_ref, cwa_ref, cwv_ref, wu_ref, h1_ref, g2_ref,
             dua_ref, duv_ref, acca_ref, accv_ref, dh1_ref, dh1b_ref, dg2_ref, carry_a, carry_v, du):
        i, jg = pl.program_id(0), pl.program_id(1)

        @pl.when((i == 0) & (jg == 0))
        def _():
            for ref in (acca_ref, accv_ref, carry_a, carry_v, dg2_ref):
                ref[...] = jnp.zeros_like(ref)

        dhb = dhb_ref[...]

        def through_down(jj):
            j = jg * group + jj
            return _dot_nt(dhb, wd_ref[pl.ds(pl.multiple_of(j * tn, tn), tn), :])

        part, ahead = None, through_down(0)
        for jj in range(group):
            col = slice(jj * tn, (jj + 1) * tn)
            j = jg * group + jj
            dact = ahead
            if jj + 1 < group:
                ahead = through_down(jj + 1)
            v = cv_ref[:, col].astype(F32)
            g, dg = _gelu_parts(ca_ref[:, col].astype(F32))
            da = conv_bwd(dact * v * dg, ua_ref[:, col].astype(F32), cwa_ref[:, col], acca_ref.at[j], carry_a.at[j],
                          dua_ref, col)
            dv = conv_bwd(dact * g, uv_ref[:, col].astype(F32), cwv_ref[:, col], accv_ref.at[j], carry_v.at[j],
                          duv_ref, col)
            term = _dot_nt(da, wu_ref[j]) + _dot_nt(dv, wu_ref[nj + j])
            part = term if part is None else part + term

        @pl.when(jg == 0)
        def _():
            du[...] = part

        @pl.when(jg > 0)
        def _():
            du[...] += part

        @pl.when(jg == ng - 1)
        def _():
            n, rstd, _ = _rms_fwd(h1_ref[...], g2_ref[...])
            dh1, dg2 = _rms_bwd(du[...], n, rstd, g2_ref[...])
            dh1 = dh1 + dh_ref[...]
            dg2_ref[...] += dg2
            dh1_ref[...] = dh1
            dh1b_ref[...] = dh1.astype(dh1b_ref.dtype)

    row = lambda i, j: (ni - 1 - i, 0)
    const = lambda i, j: (0, 0)
    tile = pl.BlockSpec((tm, tw), lambda i, j: (ni - 1 - i, j))
    acc = pl.BlockSpec((nj, FFN_ACC_ROWS, tn), lambda i, j: (0, 0, 0))
    return _call(body, name="ffn_bwd", grid=(ni, ng), comm=comm, vmem_limit=FFN_BWD_VMEM_LIMIT,
                 in_specs=[pl.BlockSpec((tm, D_MODEL), row), pl.BlockSpec((tm, D_MODEL), row),
                           _resident((D_FF, D_MODEL)), tile, tile, tile, tile,
                           pl.BlockSpec((FFN_CONV, tw), lambda i, j: (0, j)),
                           pl.BlockSpec((FFN_CONV, tw), lambda i, j: (0, j + ng)),
                           _resident(w_up_blocks.shape), pl.BlockSpec((tm, D_MODEL), row),
                           pl.BlockSpec((1, D_MODEL), const)],
                 out_specs=[tile, tile, acc, acc, pl.BlockSpec((tm, D_MODEL), row), pl.BlockSpec((tm, D_MODEL), row),
                            pl.BlockSpec((SUBLANES, D_MODEL), const)],
                 out_shape=[jax.ShapeDtypeStruct((t, D_FF), MXU_DTYPE), jax.ShapeDtypeStruct((t, D_FF), MXU_DTYPE),
                            jax.ShapeDtypeStruct((nj, FFN_ACC_ROWS, tn), F32),
                            jax.ShapeDtypeStruct((nj, FFN_ACC_ROWS, tn), F32),
                            jax.ShapeDtypeStruct((t, D_MODEL), F32), jax.ShapeDtypeStruct((t, D_MODEL), MXU_DTYPE),
                            jax.ShapeDtypeStruct((SUBLANES, D_MODEL), F32)],
                 scratch_shapes=[pltpu.VMEM((nj, SUBLANES, tn), F32), pltpu.VMEM((nj, SUBLANES, tn), F32),
                                 pltpu.VMEM((tm, D_MODEL), F32)],
                 operands=[dh2, dh2_b, w_down, up_a, up_v, conv_a, conv_v, conv_w, conv_w, w_up_blocks, h1, g2])


def _ret_bwd(proj, cos2, sin_signed, gain, o, states, dmix_at):
    t = proj.shape[0]
    c, d, nh = RET_CHUNK, RET_HEAD_DIM, RET_HEADS
    n_chunks = t // c
    per = RET_CHUNKS_PER_STEP if n_chunks % RET_CHUNKS_PER_STEP == 0 else 1
    rows = per * c
    n_steps = n_chunks // per
    decay, xi, zeta, g_chunk = _ret_consts()
    base = 2 * D_LRU

    def units(qk_ref, vg_ref, cos_ref, sin_ref, dec_ref, xi_ref, zeta_ref, gain_ref, o_ref, st_ref,
              dp_ref, dgain_ref, gstate, ctx):
        cur = [None] * nh
        dmix = ctx["scratch"][dmix_at[0]][dmix_at[1]]

        def start():
            @pl.when(pl.program_id(0) == 0)
            def _():
                gstate[...] = jnp.zeros_like(gstate)
                dgain_ref[...] = jnp.zeros_like(dgain_ref)
            for h in range(nh):
                cur[h] = gstate[h]

        def gate_and_norm(s, h, keep):
            rs = slice(s * c, (s + 1) * c)
            lo = h * d
            g = vg_ref[rs, D_RET + lo:D_RET + lo + d].astype(F32)
            gain_h = gain_ref[:, lo:lo + d]
            dy = dmix[rs, D_LRU + lo:D_LRU + lo + d]
            sg = _sigmoid(g)
            o_h = o_ref[rs, lo:lo + d]
            oc = o_h - jnp.mean(o_h, axis=-1, keepdims=True)
            rstd = lax.rsqrt(jnp.mean(oc * oc, axis=-1, keepdims=True) + NORM_EPS)
            on = oc * rstd
            at = base + 3 * D_RET + lo
            dp_ref[rs, at:at + d] = (dy * on * gain_h * (sg * (1.0 + g * (1.0 - sg)))).astype(dp_ref.dtype)
            don_g = dy * (g * sg)
            dgain_ref[:, lo:lo + d] += _colsum8(don_g * on)
            don = don_g * gain_h
            keep["do"] = rstd * (don - jnp.mean(don, axis=-1, keepdims=True)
                                 - on * jnp.mean(don * on, axis=-1, keepdims=True))

        def retain(s, h, keep):
            rs = slice(s * c, (s + 1) * c)
            cos2, sin_s = cos_ref[rs, :], sin_ref[rs, :]
            lo = h * d
            q = _rope(qk_ref[rs, lo:lo + d].astype(F32), cos2, sin_s)
            k = _rope(qk_ref[rs, D_RET + lo:D_RET + lo + d].astype(F32), cos2, sin_s) * RET_SCALE
            v = vg_ref[rs, lo:lo + d]
            xi_h, zeta_h, dec = xi_ref[:, lo:lo + d], zeta_ref[:, lo:lo + d], dec_ref[h]
            do = keep["do"]
            s_prev = st_ref[s, h]
            g_next = cur[h]
            p = _dot_nt(q, k) * dec
            dpm = _dot_nt(do, v) * dec
            keep["dq"] = _dot(dpm, k) + _dot_nt(do, s_prev) * xi_h
            keep["dk"] = _dot_tn(dpm, q) + _dot_nt(v, g_next) * zeta_h
            dv = _dot_tn(p, do) + _dot(k * zeta_h, g_next)
            cur[h] = g_next * g_chunk[h] + _dot_tn(q * xi_h, do)
            at = base + 2 * D_RET + lo
            dp_ref[rs, at:at + d] = dv.astype(dp_ref.dtype)

        def unrope(s, h, keep):
            rs = slice(s * c, (s + 1) * c)
            cos2, sin_s = cos_ref[rs, :], sin_ref[rs, :]
            lo = h * d
            dp_ref[rs, base + lo:base + lo + d] = _rope_bwd(keep["dq"], cos2, sin_s).astype(dp_ref.dtype)
            at = base + D_RET + lo
            dp_ref[rs, at:at + d] = _rope_bwd(keep["dk"] * RET_SCALE, cos2, sin_s).astype(dp_ref.dtype)

        def end():
            for h in range(nh):
                gstate[h] = cur[h]

        pieces = [start]
        for s in reversed(range(per)):
            for h in range(nh):
                keep = {}
                pieces += [lambda s=s, h=h, keep=keep, f=f: f(s, h, keep) for f in (gate_and_norm, retain, unrope)]
        return pieces + [end]

    rev = lambda col: (lambda i: (n_steps - 1 - i, col))
    full2 = lambda i: (0, 0)
    return dict(units=units, grid=(n_steps,),
                in_specs=[pl.BlockSpec((rows, 2 * D_RET), rev(1)), pl.BlockSpec((rows, 2 * D_RET), rev(2)),
                          pl.BlockSpec((rows, d), rev(0)), pl.BlockSpec((rows, d), rev(0)),
                          pl.BlockSpec((nh, c, c), lambda i: (0, 0, 0)), pl.BlockSpec((c, D_RET), full2),
                          pl.BlockSpec((c, D_RET), full2), pl.BlockSpec((1, D_RET), full2),
                          pl.BlockSpec((rows, D_RET), rev(0)),
                          pl.BlockSpec((per, nh, d, d), lambda i: (n_steps - 1 - i, 0, 0, 0))],
                out_specs=[pl.BlockSpec((rows, D_IN), rev(0)), pl.BlockSpec((SUBLANES, D_RET), full2)],
                out_shape=[jax.ShapeDtypeStruct((t, D_IN), MXU_DTYPE), jax.ShapeDtypeStruct((SUBLANES, D_RET), F32)],
                scratch_shapes=[pltpu.VMEM((nh, d, d), F32)],
                operands=[proj, proj, cos2, sin_signed, decay, xi, zeta, gain, o, states])


LRU_ACC = {"conv_w": 0, "conv_b": LRU_CONV, "gate_a_b": LRU_CONV + 1, "gate_x_b": LRU_CONV + 2,
           "lambda": LRU_CONV + 3, "norm_gain": LRU_CONV + 4}
LRU_ACC_ROWS = SUBLANES * (LRU_CONV + 5)


def _lru_bwd(proj, xc_all, h_all, conv_w, wa, ba, wx, bx, lam, gain, dproj_part, dmix_at):
    t = proj.shape[0]
    tm = min(ROW_TILE, t)
    c = D_LRU
    ni = t // tm

    def body(x_ref, xh_ref, g_ref, xc_ref, h_ref, hh_ref, cw_ref, wa_ref, ba_ref, wx_ref, bx_ref, lam_ref,
             gain_ref, acc_ref, dwa_ref, dwx_ref, a_scr, b_scr, mu_scr, carry_mu, carry_dxc, ctx):
        dp_ref = ctx["outs"][dproj_part][0]
        dmix = ctx["scratch"][dmix_at[0]][dmix_at[1]]
        fill = ctx["fill"]
        i = pl.program_id(0)
        r = ni - 1 - i

        @pl.when(i == 0)
        def _():
            acc_ref[...] = jnp.zeros_like(acc_ref)
            dwa_ref[...] = jnp.zeros_like(dwa_ref)
            dwx_ref[...] = jnp.zeros_like(dwx_ref)
            carry_mu[...] = jnp.zeros_like(carry_mu)
            carry_dxc[...] = jnp.zeros_like(carry_dxc)

        def add(name, val, k=0):
            lo = (LRU_ACC[name] + k) * SUBLANES
            acc_ref[lo:lo + SUBLANES, :] += _colsum8(val)

        fill()
        xc, h = xc_ref[...], h_ref[...]
        lam_v = lam_ref[...]
        sp = _softplus(-lam_v)
        rg, ig, a, m = _lru_gates(xc, wa_ref[...], ba_ref[...], wx_ref[...], bx_ref[...], sp, fill)
        gl, dgl = _gelu_parts(g_ref[...].astype(F32))
        fill()
        zn, rstd, _ = _rms_fwd(h * gl, gain_ref[...])
        dy = dmix[:, :c]
        dz, dgain = _rms_bwd(dy, zn, rstd, gain_ref[...])
        lo = LRU_ACC["norm_gain"] * SUBLANES
        acc_ref[lo:lo + SUBLANES, :] += dgain
        dp_ref[:, c:2 * c] = (dz * h * dgl).astype(dp_ref.dtype)
        dh = dz * gl
        fill()
        ga, gb = _group_scan(a, a * dh, reverse=True, fill=fill)
        a_scr[...] = ga
        b_scr[...] = gb
        mu_next_tile = carry_mu[...]
        carry_mu[...] = _carry_scan(a_scr, b_scr, mu_scr, mu_next_tile, reverse=True)
        fill()
        lam_t = dh + _shift_up(mu_scr[...], mu_next_tile, 1)
        h_prev = _shift_down(jnp.where(r == 0, 0.0, hh_ref[...]), h, 1)
        da = lam_t * h_prev
        dig = lam_t * m * xc
        dxc = lam_t * m * ig
        dlog_a = da * a - (lam_t * ig * xc) * (a * a) / m
        fill()
        dpr = dlog_a * ((-LRU_C) * sp) * rg * (1.0 - rg)
        add("lambda", dlog_a * ((-LRU_C) * rg) * (-_sigmoid(-lam_v)))
        dpi = dig * ig * (1.0 - ig)
        add("gate_a_b", dpr)
        add("gate_x_b", dpi)
        fill()
        dwa_ref[...] += _dot_tn(xc, dpr)
        dwx_ref[...] += _dot_tn(xc, dpi)
        dxc = dxc + _dot_nt(dpr, wa_ref[...]) + _dot_nt(dpi, wx_ref[...])
        fill()
        add("conv_b", dxc)
        x = x_ref[...].astype(F32)
        prev = jnp.where(r == 0, 0.0, xh_ref[...].astype(F32)[-SUBLANES:, :])
        cw = cw_ref[...]
        nxt = carry_dxc[...]
        carry_dxc[...] = dxc[:SUBLANES, :]
        dx = cw[LRU_CONV - 1:LRU_CONV, :] * dxc
        for k in range(LRU_CONV - 1):
            dx = dx + cw[k:k + 1, :] * _shift_up(dxc, nxt, LRU_CONV - 1 - k)
        fill()
        for k in range(LRU_CONV):
            add("conv_w", dxc * _shift_down(prev, x, LRU_CONV - 1 - k), k)
        dp_ref[:, :c] = dx.astype(dp_ref.dtype)

    hb = _halo_rows(proj.dtype)
    rev = lambda col: (lambda i: (ni - 1 - i, col))
    halo = lambda rows: (lambda i: (jnp.maximum((ni - 1 - i) * (tm // rows) - 1, 0), 0))
    full = lambda i: (0, 0)
    vec = pl.BlockSpec((1, c), full)
    mat = pl.BlockSpec((c, c), full)
    return dict(body=body, grid=(ni,), fill_points=12,
                in_specs=[pl.BlockSpec((tm, c), rev(0)), pl.BlockSpec((hb, c), halo(hb)), pl.BlockSpec((tm, c), rev(1)),
                          pl.BlockSpec((tm, c), rev(0)), pl.BlockSpec((tm, c), rev(0)),
                          pl.BlockSpec((SUBLANES, c), halo(SUBLANES)),
                          pl.BlockSpec((LRU_CONV, c), full), mat, vec, mat, vec, vec, vec],
                out_specs=[pl.BlockSpec((LRU_ACC_ROWS, c), full), mat, mat],
                out_shape=[jax.ShapeDtypeStruct((LRU_ACC_ROWS, c), F32), jax.ShapeDtypeStruct((c, c), F32),
                           jax.ShapeDtypeStruct((c, c), F32)],
                scratch_shapes=[pltpu.VMEM((tm, c), F32), pltpu.VMEM((tm, c), F32), pltpu.VMEM((tm, c), F32),
                                pltpu.VMEM((SUBLANES, c), F32), pltpu.VMEM((SUBLANES, c), F32)],
                operands=[proj, proj, proj, xc_all, h_all, h_all, conv_w, wa, ba, wx, bx, lam, gain])


def _mix_proj_bwd(dh1, dh1_b, w_out, w_in_blocks, x, g1, dproj_part):
    t = x.shape[0]
    tm = min(ROW_TILE, t)
    ni = t // tm
    nb, _, cb = w_in_blocks.shape
    first_free = -(-2 * D_LRU // cb)
    du = [None]

    def term(dp_ref, w_ref, d):
        part = _dot_nt(dp_ref[:, d * cb:(d + 1) * cb], w_ref[d])
        du[0] = part if du[0] is None else du[0] + part

    def head(dh_ref, dhb_ref, wo_ref, wi_ref, x_ref, g_ref, gx_ref, dg_ref, dmix, ctx):
        @pl.when(pl.program_id(0) == 0)
        def _():
            dg_ref[...] = jnp.zeros_like(dg_ref)
        dmix[...] = _dot_nt(dhb_ref[...], wo_ref[...])
        du[0] = None

    def units(dh_ref, dhb_ref, wo_ref, wi_ref, x_ref, g_ref, gx_ref, dg_ref, dmix, ctx):
        dp_ref = ctx["outs"][dproj_part][0]
        return [lambda d=d: term(dp_ref, wi_ref, d) for d in range(first_free, nb)]

    def tail(dh_ref, dhb_ref, wo_ref, wi_ref, x_ref, g_ref, gx_ref, dg_ref, dmix, ctx):
        dp_ref = ctx["outs"][dproj_part][0]
        for d in range(first_free):
            term(dp_ref, wi_ref, d)
        n, rstd, _ = _rms_fwd(x_ref[...], g_ref[...])
        dx, dg = _rms_bwd(du[0], n, rstd, g_ref[...])
        dg_ref[...] += dg
        gx_ref[...] = dx + dh_ref[...]

    row = lambda i: (ni - 1 - i, 0)
    const = lambda i: (0, 0)
    tile = pl.BlockSpec((tm, D_MODEL), row)
    return dict(head=head, units=units, tail=tail, grid=(ni,),
                in_specs=[tile, tile, _resident(w_out.shape), _resident(w_in_blocks.shape), tile,
                          pl.BlockSpec((1, D_MODEL), const)],
                out_specs=[tile, pl.BlockSpec((SUBLANES, D_MODEL), const)],
                out_shape=[jax.ShapeDtypeStruct((t, D_MODEL), F32), jax.ShapeDtypeStruct((SUBLANES, D_MODEL), F32)],
                scratch_shapes=[pltpu.VMEM((tm, D_MODEL), F32)],
                operands=[dh1, dh1_b, w_out, w_in_blocks, x, g1])


def _pair_sum(core, a, b, name):
    n, r, c = b.shape
    spec = pl.BlockSpec((None, r, c), lambda q, core: (q, 0, 0))

    def body(core_ref, a_ref, b_ref, o_ref):
        o_ref[...] = (a_ref[...].astype(F32) + b_ref[...].astype(F32)).astype(o_ref.dtype)

    return pl.pallas_call(
        body, name=name,
        grid_spec=pltpu.PrefetchScalarGridSpec(
            num_scalar_prefetch=1, grid=(n,),
            in_specs=[pl.BlockSpec((None, r, c), lambda q, core: (2 * q + core[0], 0, 0)), spec], out_specs=spec),
        out_shape=jax.ShapeDtypeStruct(b.shape, b.dtype),
        compiler_params=pltpu.CompilerParams(dimension_semantics=("arbitrary",), vmem_limit_bytes=VMEM_LIMIT),
    )(core, a, b)


ADAMW_BLOCK_BYTES = 4 * 1024 * 1024


def _sum_adamw(parts, w, m, v, name):
    n_parts, r, c = parts.shape
    tr = r
    while n_parts * tr * c * parts.dtype.itemsize > ADAMW_BLOCK_BYTES and tr % (4 * SUBLANES) == 0:
        tr //= 2

    def body(p_ref, w_ref, m_ref, v_ref, g_ref, d_ref, nm_ref, nv_ref):
        g = p_ref[0].astype(F32)
        for s in range(1, n_parts):
            g = g + p_ref[s].astype(F32)
        nm = ADAM_B1 * m_ref[...] + (1.0 - ADAM_B1) * g
        nv = ADAM_B2 * v_ref[...] + (1.0 - ADAM_B2) * (g * g)
        m_hat = nm / (1.0 - ADAM_B1 ** ADAM_STEP)
        v_hat = nv / (1.0 - ADAM_B2 ** ADAM_STEP)
        g_ref[...] = g
        d_ref[...] = -ADAM_LR * (m_hat / (jnp.sqrt(v_hat) + ADAM_EPS) + ADAM_WD * w_ref[...])
        nm_ref[...] = nm
        nv_ref[...] = nv

    row = pl.BlockSpec((tr, c), lambda i: (i, 0))
    return _call(body, name=name, grid=(r // tr,),
                 in_specs=[pl.BlockSpec((n_parts, tr, c), lambda i: (0, i, 0)), row, row, row],
                 out_specs=[row, row, row, row], out_shape=[jax.ShapeDtypeStruct((r, c), F32)] * 4,
                 operands=[parts, w, m, v])


MATRICES = ("w_in", "w_out", "ffn_up_w", "ffn_down_w")
CONVS = ("lru_conv_w", "ffn_conv_w")
REPLICATED = ("norm1_gain", "lru_conv_b", "lru_gate_a_w", "lru_gate_a_b", "lru_gate_x_w", "lru_gate_x_b", "lru_lambda",
              "lru_norm_gain", "ret_norm_gain", "norm2_gain", "ffn_conv_b", "final_norm_gain")
WEIGHTS = ("norm1_gain", "w_in", "lru_conv_w", "lru_conv_b", "lru_gate_a_w", "lru_gate_a_b", "lru_gate_x_w",
           "lru_gate_x_b", "lru_lambda", "lru_norm_gain", "ret_norm_gain", "w_out", "norm2_gain", "ffn_up_w",
           "ffn_conv_w", "ffn_conv_b", "ffn_down_w", "final_norm_gain")


def _rows(a, pad_to):
    a = a.reshape(-1, LANES)
    pad = (-a.shape[0]) % pad_to
    return jnp.pad(a, ((0, pad), (0, 0))) if pad else a


def _pack(arrays, pad_to):
    rows, layout, at = [], [], 0
    for a in arrays:
        r = _rows(a, pad_to)
        layout.append((at, a.size // LANES, a.shape))
        rows.append(r)
        at += r.shape[0]
    return jnp.concatenate(rows, axis=0), layout


def _unpack(packed, layout):
    lead = packed.shape[:-2]
    return [packed[..., at:at + n, :].reshape(lead + shape) for at, n, shape in layout]


def _conv_rows(lru, ffn, dtype, pad_to):
    lead = lru.shape[:-2]
    flat = jnp.concatenate([lru.reshape(lead + (-1,)), ffn.reshape(lead + (-1,))], axis=-1).astype(dtype)
    rows = flat.shape[-1] // LANES
    pad = (-rows) % pad_to
    return jnp.pad(flat.reshape(lead + (rows, LANES)), [(0, 0)] * len(lead) + [(0, pad), (0, 0)])


def _column_blocks(full):
    r, c = full.shape
    return full.reshape(r, N_DEV, c // N_DEV).transpose(1, 0, 2)


def _block_diag(w):
    nh, d, _ = w.shape
    eye = jnp.eye(nh, dtype=w.dtype)
    return (w[:, :, None, :] * eye[:, None, :, None]).reshape(nh * d, nh * d)


def _diag_blocks(dense, nh):
    d = dense.shape[0] // nh
    blocks = dense.reshape(nh, d, nh, d)
    return jnp.stack([blocks[h, :, h, :] for h in range(nh)], axis=0)


def kernel(x, norm1_gain, w_in, lru_conv_w, lru_conv_b, lru_gate_a_w, lru_gate_a_b, lru_gate_x_w, lru_gate_x_b, lru_lambda, lru_norm_gain, ret_norm_gain, w_out, norm2_gain, ffn_up_w, ffn_conv_w, ffn_conv_b, ffn_down_w, final_norm_gain, loss_target, m_norm1_gain, m_w_in, m_lru_conv_w, m_lru_conv_b, m_lru_gate_a_w, m_lru_gate_a_b, m_lru_gate_x_w, m_lru_gate_x_b, m_lru_lambda, m_lru_norm_gain, m_ret_norm_gain, m_w_out, m_norm2_gain, m_ffn_up_w, m_ffn_conv_w, m_ffn_conv_b, m_ffn_down_w, m_final_norm_gain, v_norm1_gain, v_w_in, v_lru_conv_w, v_lru_conv_b, v_lru_gate_a_w, v_lru_gate_a_b, v_lru_gate_x_w, v_lru_gate_x_b, v_lru_lambda, v_lru_norm_gain, v_ret_norm_gain, v_w_out, v_norm2_gain, v_ffn_up_w, v_ffn_conv_w, v_ffn_conv_b, v_ffn_down_w, v_final_norm_gain):
    args = dict(locals())
    given = {n: args[n] for n in WEIGHTS}
    out_shape = {n: given[n].shape for n in WEIGHTS}

    def plain(a):
        return a.reshape(1, -1) if a.ndim <= 2 else a[0]

    w = {n: plain(given[n]) for n in WEIGHTS}
    mom_m = {n: plain(args["m_" + n]) for n in WEIGHTS}
    mom_v = {n: plain(args["v_" + n]) for n in WEIGHTS}
    x2, target = x[0], loss_target[0]
    t = x2.shape[0]
    core = lax.axis_index("c").astype(jnp.int32).reshape(1)
    res = {}

    conv_pad = _conv_rows(w["lru_conv_w"], w["ffn_conv_w"], F32, SUBLANES)
    first = _gather_first([w["w_in"].astype(MXU_DTYPE), conv_pad])
    w_in_blocks, conv_all = _run_comms([first, _gather_second(first.out_shape)], "w_in_all_gather")
    n_lru = w["lru_conv_w"].size
    conv_flat = conv_all.reshape(N_DEV, -1)
    lru_cw = conv_flat[:, :n_lru].reshape((N_DEV,) + w["lru_conv_w"].shape).transpose(1, 0, 2).reshape(LRU_CONV, D_LRU)
    ffn_cw = conv_flat[:, n_lru:n_lru + w["ffn_conv_w"].size].reshape((N_DEV,) + w["ffn_conv_w"].shape)
    ffn_cw = ffn_cw.transpose(1, 0, 2).reshape(FFN_CONV, 2 * D_FF)

    cos2, sin_signed = _rope_tables(t)
    wa = _block_diag(w["lru_gate_a_w"]).astype(MXU_DTYPE)
    wx = _block_diag(w["lru_gate_x_w"]).astype(MXU_DTYPE)
    gf = w["final_norm_gain"]

    early = _gather_first([w["w_out"].astype(MXU_DTYPE), w["ffn_down_w"].astype(MXU_DTYPE)])
    (u1, proj), (w_out_part, down_part) = _inproj_fwd(x2, w["norm1_gain"], w_in_blocks, early)
    ((xc, h_lru, y_lru), (o_ret, y_ret, states)), (w_out_blocks, down_blocks, up_part) = _fused(
        [_lru_fwd(proj, lru_cw, w["lru_conv_b"], wa, w["lru_gate_a_b"], wx, w["lru_gate_x_b"], w["lru_lambda"],
                  w["lru_norm_gain"]),
         _ret_fwd(proj, cos2, sin_signed, w["ret_norm_gain"])],
        "mix_fwd", _both(_gather_second([w_out_part, down_part]), _gather_first([w["ffn_up_w"].astype(MXU_DTYPE)])))
    w_out_full = w_out_blocks.reshape(D_MODEL, D_MODEL)
    w_down_full = down_blocks.reshape(D_FF, D_MODEL)

    (h1, u2), (up_blocks,) = _outproj_fwd(x2, y_lru, y_ret, w_out_full, w["norm2_gain"], _gather_second([up_part]))
    up_a, up_v, conv_a, conv_v, act, dh2, dh2_b, dgf, loss_local = _ffn_fwd(u2, up_blocks, ffn_cw, w["ffn_conv_b"],
                                                                            w_down_full, h1, gf, target)

    def to_owner_chips(blocks, names, tag):
        theirs = _run_comms([_pair_exchange(blocks)], "grads_pair_exchange_" + tag)
        return [_pair_sum(core, a, b, "grads_pair_sum_" + n) for n, a, b in zip(names, blocks, theirs)]

    def adamw(name, parts):
        res[name] = _sum_adamw(parts, w[name], mom_m[name], mom_v[name], "adamw_" + name)

    g = {"final_norm_gain": dgf[0]}
    dup_a, dup_v, acc_a, acc_v, dh1, dh1_b, dg2 = _ffn_bwd(
        dh2, dh2_b, w_down_full, up_a, up_v, conv_a, conv_v, ffn_cw, up_blocks, h1, w["norm2_gain"], None)
    per_col = lambda a: a[:, ::SUBLANES].transpose(1, 0, 2).reshape(FFN_CONV + 1, D_FF)
    acc = jnp.concatenate([per_col(acc_a), per_col(acc_v)], axis=1)
    g_ffn_cw, g["ffn_conv_b"] = acc[:FFN_CONV], acc[FFN_CONV:]
    g["norm2_gain"] = dg2[:1]
    g_up = jnp.concatenate([_mm_tn(u2, dup_a, "ffn_up_wgrad_a", blocks=N_DEV // 2),
                            _mm_tn(u2, dup_v, "ffn_up_wgrad_v", blocks=N_DEV // 2)], axis=0)
    up_sums = to_owner_chips([g_up], ["ffn_up_w"], "up")
    g_down, (up_parts,) = _mm_tn(act, dh2_b, "ffn_down_wgrad", comm=_chip_exchange(up_sums))
    adamw("ffn_up_w", up_parts)
    g_out = jnp.concatenate([_mm_tn(y_lru, dh1_b, "w_out_wgrad_lru"), _mm_tn(y_ret, dh1_b, "w_out_wgrad_ret")], axis=0)
    low_sums = to_owner_chips([g_down.reshape(N_DEV, D_FF // N_DEV, D_MODEL),
                               g_out.reshape(N_DEV, D_MODEL // N_DEV, D_MODEL)], ["ffn_down_w", "w_out"], "low")
    (dproj, dgain_ret), (grad_x, dg1), (lru_acc, dwa, dwx) = _fused(
        [_ret_bwd(proj, cos2, sin_signed, w["ret_norm_gain"], o_ret, states, dmix_at=(1, 0)),
         _mix_proj_bwd(dh1, dh1_b, w_out_full, w_in_blocks, x2, w["norm1_gain"], dproj_part=0),
         _lru_bwd(proj, xc, h_lru, lru_cw, wa, w["lru_gate_a_b"], wx, w["lru_gate_x_b"], w["lru_lambda"],
                  w["lru_norm_gain"], dproj_part=0, dmix_at=(1, 0))],
        "mix_bwd")
    g["norm1_gain"] = dg1[:1]
    g["ret_norm_gain"] = dgain_ret[:1]
    lru_acc = lru_acc[::SUBLANES]
    g_lru_cw = lru_acc[:LRU_CONV]
    for name in ("conv_b", "gate_a_b", "gate_x_b", "lambda", "norm_gain"):
        g["lru_" + name] = lru_acc[LRU_ACC[name]:LRU_ACC[name] + 1]
    g["lru_gate_a_w"] = _diag_blocks(dwa, LRU_HEADS)
    g["lru_gate_x_w"] = _diag_blocks(dwx, LRU_HEADS)
    rep_packed, rep_layout = _pack([g[n] for n in REPLICATED] + [loss_local], SUBLANES)
    g_in, (down_parts, out_parts, rep_part) = _mm_tn(u1, dproj, "w_in_wgrad", blocks=N_DEV,
                                                     comm=_both(_chip_exchange(low_sums), _gather_first([rep_packed])))
    adamw("ffn_down_w", down_parts)
    adamw("w_out", out_parts)
    g_conv = _conv_rows(_column_blocks(g_lru_cw), _column_blocks(g_ffn_cw), GRAD_DTYPE, 2 * SUBLANES)
    in_sums = to_owner_chips([g_in, g_conv], ["w_in", "conv"], "in")
    in_parts, conv_parts, rep_parts = _run_comms([_both(_chip_exchange(in_sums), _gather_second([rep_part]))],
                                                 "last_grads_exchange")
    adamw("w_in", in_parts)
    pad16 = lambda d: _conv_rows(d["lru_conv_w"], d["ffn_conv_w"], F32, 2 * SUBLANES)
    conv_res = _sum_adamw(conv_parts, pad16(w), pad16(mom_m), pad16(mom_v), "adamw_conv")
    for n, lo, hi in (("lru_conv_w", 0, n_lru), ("ffn_conv_w", n_lru, n_lru + w["ffn_conv_w"].size)):
        res[n] = [r.reshape(-1)[lo:hi].reshape(w[n].shape) for r in conv_res]
    no_state = jnp.zeros_like(loss_local)
    rep_res = _sum_adamw(rep_parts, *[_pack([d[n] for n in REPLICATED] + [no_state], SUBLANES)[0]
                                      for d in (w, mom_m, mom_v)], "adamw_replicated")
    for k in range(4):
        for n, a in zip(REPLICATED, _unpack(rep_res[k], rep_layout)):
            res.setdefault(n, [None] * 4)[k] = a
    loss = _unpack(rep_res[0], rep_layout)[-1][0, 0]

    outs = [loss, grad_x[None]]
    for k in range(4):
        outs += [res[n][k].reshape(out_shape[n]) for n in WEIGHTS]
    return tuple(outs)
```

```python
import math

import numpy as np
import jax
import jax.numpy as jnp
from jax import lax
from jax.experimental import pallas as pl
from jax.experimental.pallas import tpu as pltpu

F32 = jnp.float32
BF16 = jnp.bfloat16
MXU_DTYPE = jnp.bfloat16
GRAD_DTYPE = jnp.bfloat16

N_DEV = 8
N_CHIPS = 4
D_MODEL = 1024
D_LRU = 512
LRU_HEADS = 8
LRU_CONV = 4
LRU_C = 8.0
D_RET = 512
RET_HEADS = 4
RET_HEAD_DIM = 128
RET_CHUNK = 128
ROPE_BASE = 10000.0
D_IN = 3072
D_FF = 3072
FFN_CONV = 3
NORM_EPS = 1e-6

ADAM_LR = 0.001
ADAM_B1 = 0.9
ADAM_B2 = 0.999
ADAM_EPS = 1e-08
ADAM_WD = 0.01
ADAM_STEP = 10

SUBLANES = 8
LANES = 128
VMEM_LIMIT = 48 * 1024 * 1024
FUSED_VMEM_LIMIT = VMEM_LIMIT
FFN_BWD_VMEM_LIMIT = 56 * 1024 * 1024

ROW_TILE = 256
MIX_ROW_TILE = 256
PROJ_ROW_TILE = 512
WGRAD_ROWS = 2048
WGRAD_TILE = 1024
WGRAD_BLOCK_COLUMNS = 768

MESH = pl.DeviceIdType.MESH
ANY = pl.BlockSpec(memory_space=pl.ANY)


def _dot(a, b):
    return jnp.dot(a.astype(MXU_DTYPE), b.astype(MXU_DTYPE), preferred_element_type=F32)


def _dot_nt(a, b):
    return lax.dot_general(a.astype(MXU_DTYPE), b.astype(MXU_DTYPE), (((1,), (1,)), ((), ())),
                           preferred_element_type=F32)


def _dot_tn(a, b):
    return lax.dot_general(a.astype(MXU_DTYPE), b.astype(MXU_DTYPE), (((0,), (0,)), ((), ())),
                           preferred_element_type=F32)


def _sigmoid(x):
    return 0.5 + 0.5 * jnp.tanh(0.5 * x)


_GELU_C = math.sqrt(2.0 / math.pi)
_GELU_C3 = _GELU_C * 0.044715


def _gelu_parts(x):
    x2 = x * x
    t = jnp.tanh(x * (_GELU_C + _GELU_C3 * x2))
    cdf = 0.5 + 0.5 * t
    g = x * cdf
    dg = cdf + (0.5 * x) * (1.0 - t * t) * (_GELU_C + (3.0 * _GELU_C3) * x2)
    return g, dg


def _gelu(x):
    t = jnp.tanh(_GELU_C * (x + 0.044715 * (x * x * x)))
    return x * (0.5 * (1.0 + t))


def _softplus(x):
    return jnp.maximum(x, 0.0) + jnp.log1p(jnp.exp(-jnp.abs(x)))


def _bcast_row(x, r, rows=SUBLANES):
    return jnp.broadcast_to(x[r:r + 1, :], (rows, x.shape[1]))


def _colsum8(x):
    return jnp.broadcast_to(jnp.sum(x, axis=0, keepdims=True), (SUBLANES, x.shape[1]))


def _shift_down(prev8, tile, s):
    if s == 0:
        return tile
    ext = jnp.concatenate([prev8, tile], axis=0)
    return pltpu.roll(ext, s, 0)[SUBLANES:, :]


def _shift_up(tile, next8, s):
    if s == 0:
        return tile
    ext = jnp.concatenate([tile, next8], axis=0)
    return pltpu.roll(ext, SUBLANES - s, 0)[SUBLANES:, :]


def _group_scan(a, b, reverse, fill=lambda: None):
    n, c = a.shape
    row = lax.broadcasted_iota(jnp.int32, a.shape, 0) & (SUBLANES - 1)

    def within_group(x, shift):
        return pltpu.roll(x.reshape(n // SUBLANES, SUBLANES, c), shift, 1).reshape(n, c)

    for s in (1, 2, 4):
        if s > 1:
            fill()
        shift = (SUBLANES - s) if reverse else s
        a_sh = within_group(a, shift)
        b_sh = within_group(b, shift)
        m = (row <= SUBLANES - 1 - s) if reverse else (row >= s)
        b = jnp.where(m, a * b_sh + b, b)
        a = jnp.where(m, a * a_sh, a)
    return a, b


def _carry_scan(a_ref, b_ref, out_ref, carry0, reverse):
    n_groups = a_ref.shape[0] // SUBLANES
    carry = carry0
    for i in range(n_groups):
        r0 = ((n_groups - 1 - i) if reverse else i) * SUBLANES
        hg = a_ref[r0:r0 + SUBLANES, :] * carry + b_ref[r0:r0 + SUBLANES, :]
        out_ref[r0:r0 + SUBLANES, :] = hg
        carry = _bcast_row(hg, 0 if reverse else SUBLANES - 1)
    return carry


def _rms_fwd(h, gain):
    rstd = lax.rsqrt(jnp.mean(h * h, axis=-1, keepdims=True) + NORM_EPS)
    n = h * rstd
    return n, rstd, n * gain


def _rms_bwd(dy, n, rstd, gain):
    dn = dy * gain
    dh = rstd * (dn - n * jnp.mean(dn * n, axis=-1, keepdims=True))
    return dh, _colsum8(dy * n)


def _halo_rows(dtype):
    return SUBLANES * (4 // jnp.dtype(dtype).itemsize)


def _halo_map(tile_rows, col, halo_rows=SUBLANES):
    per = tile_rows // halo_rows
    return lambda i: (jnp.maximum(i * per - 1, 0), col)


def _resident(shape):
    return pl.BlockSpec(shape, lambda *_: (0,) * len(shape), pipeline_mode=pl.Buffered(1))


def _place():
    x, y, c = lax.axis_index("x"), lax.axis_index("y"), lax.axis_index("c")
    chips = [(1 - x, y), (x, 1 - y), (1 - x, 1 - y)]
    return x, y, c, chips


def _dev(x, y, c):
    return 4 * x + 2 * y + c


class _Copy:
    def __init__(self, make):
        self.make = make

    def start(self):
        self.make().start()

    def wait(self):
        self.make().wait()

    def wait_send(self):
        self.make().wait_send()

    def wait_recv(self):
        self.make().wait_recv()


def _remote(src, dst, send_sem, recv_sem, to):
    return _Copy(lambda: pltpu.make_async_remote_copy(src_ref=src, dst_ref=dst, send_sem=send_sem, recv_sem=recv_sem,
                                                      device_id=to, device_id_type=MESH))


def _local(src, dst, sem):
    return _Copy(lambda: pltpu.make_async_copy(src, dst, sem))


class _Comm:
    def __init__(self, operands, out_shape, sems, descs, aliases=()):
        self.operands, self.out_shape, self.sems, self.descs, self.aliases = operands, out_shape, sems, descs, aliases

    def start(self, ins, outs, sems):
        local, sends, _ = self.descs(ins, outs, sems)
        for cp in sends + local:
            cp.start()

    def wait(self, ins, outs, sems):
        local, sends, recvs = self.descs(ins, outs, sems)
        for cp in recvs:
            cp.wait_recv()
        for cp in sends:
            cp.wait_send()
        for cp in local:
            cp.wait()


def _gather_first(shards):
    n = len(shards)

    def descs(ins, outs, sems):
        send, recv, loc = sems
        x, y, c, chips = _place()
        me = _dev(x, y, c)
        targets = [(x, y, 1 - c)] + [(*chip, c) for chip in chips]
        local, sends, recvs = [], [], []
        for t in range(n):
            local.append(_local(ins[t], outs[t].at[me], loc.at[t]))
            for k, to in enumerate(targets):
                i = 4 * t + k
                sends.append(_remote(ins[t], outs[t].at[me], send.at[i], recv.at[i], to))
                recvs.append(_remote(ins[t], outs[t].at[_dev(*to)], send.at[i], recv.at[i], to))
        return local, sends, recvs

    return _Comm(list(shards), [jax.ShapeDtypeStruct((N_DEV,) + s.shape, s.dtype) for s in shards],
                 [pltpu.SemaphoreType.DMA((4 * n,)), pltpu.SemaphoreType.DMA((4 * n,)), pltpu.SemaphoreType.DMA((n,))],
                 descs)


def _gather_second(gathered):
    n = len(gathered)

    def descs(ins, outs, sems):
        send, recv = sems
        x, y, c, chips = _place()
        sends, recvs = [], []
        for t in range(n):
            for j, chip in enumerate(chips):
                i = 3 * t + j
                have, get = _dev(*chip, c), _dev(*chip, 1 - c)
                sends.append(_remote(outs[t].at[have], outs[t].at[have], send.at[i], recv.at[i], (x, y, 1 - c)))
                recvs.append(_remote(outs[t].at[have], outs[t].at[get], send.at[i], recv.at[i], (x, y, 1 - c)))
        return [], sends, recvs

    return _Comm(list(gathered), [jax.ShapeDtypeStruct(g.shape, g.dtype) for g in gathered],
                 [pltpu.SemaphoreType.DMA((3 * n,)), pltpu.SemaphoreType.DMA((3 * n,))], descs,
                 aliases=[(t, t) for t in range(n)])


def _pair_exchange(blocks):
    n = len(blocks)

    def descs(ins, outs, sems):
        send, recv = sems
        x, y, c, _ = _place()
        sends, recvs = [], []
        for t in range(n):
            for q in range(N_CHIPS):
                i = N_CHIPS * t + q
                cp = _remote(ins[t].at[2 * q + 1 - c], outs[t].at[q], send.at[i], recv.at[i], (x, y, 1 - c))
                sends.append(cp)
                recvs.append(cp)
        return [], sends, recvs

    return _Comm(list(blocks), [jax.ShapeDtypeStruct((N_CHIPS,) + b.shape[1:], b.dtype) for b in blocks],
                 [pltpu.SemaphoreType.DMA((N_CHIPS * n,)), pltpu.SemaphoreType.DMA((N_CHIPS * n,))], descs)


def _chip_exchange(blocks):
    n = len(blocks)

    def descs(ins, outs, sems):
        send, recv, loc = sems
        x, y, c, chips = _place()
        me = 2 * x + y
        local, sends, recvs = [], [], []
        for t in range(n):
            local.append(_local(ins[t].at[me], outs[t].at[me], loc.at[t]))
            for j, (px, py) in enumerate(chips):
                i = 3 * t + j
                q = 2 * px + py
                sends.append(_remote(ins[t].at[q], outs[t].at[me], send.at[i], recv.at[i], (px, py, c)))
                recvs.append(_remote(ins[t].at[q], outs[t].at[q], send.at[i], recv.at[i], (px, py, c)))
        return local, sends, recvs

    return _Comm(list(blocks), [jax.ShapeDtypeStruct(b.shape, b.dtype) for b in blocks],
                 [pltpu.SemaphoreType.DMA((3 * n,)), pltpu.SemaphoreType.DMA((3 * n,)), pltpu.SemaphoreType.DMA((n,))],
                 descs)


def _both(a, b):
    na, oa, sa = len(a.operands), len(a.out_shape), len(a.sems)

    def descs(ins, outs, sems):
        local_a, sends_a, recvs_a = a.descs(ins[:na], outs[:oa], sems[:sa])
        local_b, sends_b, recvs_b = b.descs(ins[na:], outs[oa:], sems[sa:])
        return local_a + local_b, sends_a + sends_b, recvs_a + recvs_b

    return _Comm(a.operands + b.operands, a.out_shape + b.out_shape, a.sems + b.sems, descs,
                 aliases=list(a.aliases) + [(na + i, oa + o) for i, o in b.aliases])


def _run_comms(comms, name):
    first = comms[0]
    n_in, n_out = len(first.operands), len(first.out_shape)

    def body(*refs):
        ins, outs, sems = refs[:n_in], refs[n_in:n_in + n_out], list(refs[n_in + n_out:])
        for k, comm in enumerate(comms):
            mine = [sems.pop(0) for _ in comm.sems]
            comm.start(ins if k == 0 else outs, outs, mine)
            comm.wait(ins if k == 0 else outs, outs, mine)

    outs = pl.pallas_call(
        body, name=name, out_shape=first.out_shape, in_specs=[ANY] * n_in, out_specs=[ANY] * n_out,
        scratch_shapes=[s for comm in comms for s in comm.sems], input_output_aliases=dict(first.aliases),
    )(*first.operands)
    return list(outs)


def _call(body, *, name, grid, in_specs, out_specs, out_shape, operands, scratch_shapes=(), comm=None, aliases=None,
          vmem_limit=VMEM_LIMIT):
    sem = ("arbitrary",) * len(grid)
    params = pltpu.CompilerParams(dimension_semantics=sem, vmem_limit_bytes=vmem_limit)
    aliases = dict(aliases or {})
    if comm is None:
        return pl.pallas_call(body, name=name, grid=grid, in_specs=in_specs, out_specs=out_specs, out_shape=out_shape,
                              scratch_shapes=list(scratch_shapes), input_output_aliases=aliases,
                              compiler_params=params)(*operands)
    n_in, n_out, n_scr = len(in_specs), len(out_specs), len(scratch_shapes)
    c_in, c_out = len(comm.operands), len(comm.out_shape)

    def wrapped(*refs):
        refs = list(refs)
        ins, refs = refs[:n_in], refs[n_in:]
        cins, refs = refs[:c_in], refs[c_in:]
        outs, refs = refs[:n_out], refs[n_out:]
        couts, refs = refs[:c_out], refs[c_out:]
        scr, csems = refs[:n_scr], refs[n_scr:]
        first = last = None
        for axis, size in enumerate(grid):
            at_first, at_last = pl.program_id(axis) == 0, pl.program_id(axis) == size - 1
            first = at_first if first is None else first & at_first
            last = at_last if last is None else last & at_last

        @pl.when(first)
        def _():
            comm.start(cins, couts, csems)

        body(*ins, *outs, *scr)

        @pl.when(last)
        def _():
            comm.wait(cins, couts, csems)

    res = pl.pallas_call(
        wrapped, name=name, grid=grid, in_specs=list(in_specs) + [ANY] * c_in, out_specs=list(out_specs) + [ANY] * c_out,
        out_shape=list(out_shape) + list(comm.out_shape), scratch_shapes=list(scratch_shapes) + list(comm.sems),
        input_output_aliases={**aliases, **{n_in + i: n_out + o for i, o in comm.aliases}}, compiler_params=params,
    )(*operands, *comm.operands)
    return list(res[:n_out]), list(res[n_out:])


def _mm_tn(a, b, name, blocks=1, comm=None):
    t, m = a.shape
    n = b.shape[1]
    tk = min(WGRAD_ROWS, t)
    nk = t // tk
    cb = n // blocks
    per = max(1, WGRAD_BLOCK_COLUMNS // cb) if blocks > 1 else 1
    tn = per * cb if blocks > 1 else min(WGRAD_TILE, n)
    tm = min(WGRAD_TILE, m)
    assert blocks == 1 or tm == m

    def body(a_ref, b_ref, o_ref, acc):
        k = pl.program_id(2)

        @pl.when(k == 0)
        def _():
            acc[...] = jnp.zeros_like(acc)
        acc[...] += _dot_tn(a_ref[...], b_ref[...])

        @pl.when(k == nk - 1)
        def _():
            if blocks == 1:
                o_ref[...] = acc[...].astype(o_ref.dtype)
            else:
                for s in range(per):
                    o_ref[s] = acc[:, s * cb:(s + 1) * cb].astype(o_ref.dtype)

    if blocks == 1:
        out_spec = pl.BlockSpec((tm, tn), lambda i, j, k: (i, j))
        out_shape = jax.ShapeDtypeStruct((m, n), GRAD_DTYPE)
    else:
        out_spec = pl.BlockSpec((per, m, cb), lambda i, j, k: (j, 0, 0))
        out_shape = jax.ShapeDtypeStruct((blocks, m, cb), GRAD_DTYPE)
    res = _call(body, name=name, grid=(m // tm, n // tn, nk), comm=comm,
                in_specs=[pl.BlockSpec((tk, tm), lambda i, j, k: (k, i)), pl.BlockSpec((tk, tn), lambda i, j, k: (k, j))],
                out_specs=[out_spec], out_shape=[out_shape], operands=[a, b],
                scratch_shapes=[pltpu.VMEM((tm, tn), F32)])
    return res[0] if comm is None else (res[0][0], res[1])


def _inproj_fwd(x, g1, w_blocks, comm):
    t = x.shape[0]
    tm = min(PROJ_ROW_TILE, t)
    nb, _, cb = w_blocks.shape

    def body(x_ref, g_ref, w_ref, u_ref, p_ref):
        _, _, u = _rms_fwd(x_ref[...], g_ref[...])
        u = u.astype(MXU_DTYPE)
        u_ref[...] = u
        for d in range(nb):
            p_ref[:, d * cb:(d + 1) * cb] = _dot(u, w_ref[d]).astype(p_ref.dtype)

    return _call(body, name="inproj_fwd", grid=(t // tm,), comm=comm,
                 in_specs=[pl.BlockSpec((tm, D_MODEL), lambda i: (i, 0)), pl.BlockSpec((1, D_MODEL), lambda i: (0, 0)),
                           _resident(w_blocks.shape)],
                 out_specs=[pl.BlockSpec((tm, D_MODEL), lambda i: (i, 0)), pl.BlockSpec((tm, D_IN), lambda i: (i, 0))],
                 out_shape=[jax.ShapeDtypeStruct((t, D_MODEL), MXU_DTYPE), jax.ShapeDtypeStruct((t, D_IN), MXU_DTYPE)],
                 operands=[x, g1, w_blocks])


def _lru_gates(xc, wa, ba, wx, bx, sp, fill=lambda: None):
    r = _sigmoid(_dot(xc, wa) + ba)
    fill()
    ig = _sigmoid(_dot(xc, wx) + bx)
    fill()
    log_a = (-LRU_C) * r * sp
    a = jnp.exp(log_a)
    m = jnp.sqrt(-jnp.tanh(log_a) * (a * a + 1.0))
    return r, ig, a, m


def _fused(parts, name, comm=None):
    grid = parts[0]["grid"]
    assert all(p["grid"] == grid for p in parts)
    counts = [(len(p["in_specs"]), len(p["out_specs"]), len(p.get("scratch_shapes", ()))) for p in parts]

    def body(*refs):
        refs = list(refs)
        groups = []
        for kind in range(3):
            taken = []
            for c in counts:
                taken.append(refs[:c[kind]])
                refs = refs[c[kind]:]
            groups.append(taken)
        ins, outs, scr = groups
        pending = []

        def fill(n=None):
            for _ in range(share if n is None else n):
                if pending:
                    pending.pop(0)()

        ctx = dict(outs=outs, scratch=scr, fill=fill)
        run = lambda key: [p[key](*ins[k], *outs[k], *scr[k], ctx) for k, p in enumerate(parts) if key in p]
        run("head")
        for pieces in run("units"):
            pending.extend(pieces)
        points = sum(p.get("fill_points", 0) for p in parts)
        share = -(-len(pending) // max(points, 1))
        run("body")
        fill(len(pending))
        run("tail")

    cat = lambda key: [x for p in parts for x in p.get(key, ())]
    res = _call(body, name=name, grid=grid, comm=comm, vmem_limit=FUSED_VMEM_LIMIT,
                in_specs=cat("in_specs"), out_specs=cat("out_specs"),
                out_shape=cat("out_shape"), scratch_shapes=cat("scratch_shapes"), operands=cat("operands"))
    outs, side = (res if comm is not None else (res, None))
    split, at = [], 0
    for _, n_out, _ in counts:
        split.append(list(outs[at:at + n_out]))
        at += n_out
    return split if comm is None else (split, side)


def _lru_fwd(proj, conv_w, conv_b, wa, ba, wx, bx, lam, gain):
    t = proj.shape[0]
    tm = min(MIX_ROW_TILE, t)
    c = D_LRU

    def body(x_ref, xh_ref, g_ref, cw_ref, cb_ref, wa_ref, ba_ref, wx_ref, bx_ref, lam_ref, gain_ref,
             xc_ref, h_ref, y_ref, a_scr, b_scr, carry, ctx):
        fill = ctx["fill"]
        i = pl.program_id(0)

        @pl.when(i == 0)
        def _():
            carry[...] = jnp.zeros_like(carry)

        fill()
        x = x_ref[...].astype(F32)
        prev = jnp.where(i == 0, 0.0, xh_ref[...].astype(F32)[-SUBLANES:, :])
        cw = cw_ref[...]
        xc = cb_ref[...] + cw[LRU_CONV - 1:LRU_CONV, :] * x
        for k in range(LRU_CONV - 1):
            xc = xc + cw[k:k + 1, :] * _shift_down(prev, x, LRU_CONV - 1 - k)
        xc_ref[...] = xc
        fill()
        sp = _softplus(-lam_ref[...])
        _, ig, a, m = _lru_gates(xc, wa_ref[...], ba_ref[...], wx_ref[...], bx_ref[...], sp, fill)
        fill()
        ga, gb = _group_scan(a, m * (ig * xc), reverse=False, fill=fill)
        a_scr[...] = ga
        b_scr[...] = gb
        fill()
        carry[...] = _carry_scan(a_scr, b_scr, h_ref, carry[...], reverse=False)
        fill()
        z = h_ref[...] * _gelu(g_ref[...].astype(F32))
        fill()
        _, _, y = _rms_fwd(z, gain_ref[...])
        y_ref[...] = y.astype(y_ref.dtype)

    row = lambda i: (i, 0)
    full = lambda i: (0, 0)
    vec = pl.BlockSpec((1, c), full)
    hb = _halo_rows(proj.dtype)
    return dict(body=body, grid=(t // tm,), fill_points=10,
                in_specs=[pl.BlockSpec((tm, c), row), pl.BlockSpec((hb, c), _halo_map(tm, 0, hb)),
                          pl.BlockSpec((tm, c), lambda i: (i, 1)),
                          pl.BlockSpec((LRU_CONV, c), full), vec, pl.BlockSpec((c, c), full), vec,
                          pl.BlockSpec((c, c), full), vec, vec, vec],
                out_specs=[pl.BlockSpec((tm, c), row), pl.BlockSpec((tm, c), row), pl.BlockSpec((tm, c), row)],
                out_shape=[jax.ShapeDtypeStruct((t, c), F32), jax.ShapeDtypeStruct((t, c), F32),
                           jax.ShapeDtypeStruct((t, c), MXU_DTYPE)],
                scratch_shapes=[pltpu.VMEM((tm, c), F32), pltpu.VMEM((tm, c), F32), pltpu.VMEM((SUBLANES, c), F32)],
                operands=[proj, proj, proj, conv_w, conv_b, wa, ba, wx, bx, lam, gain])


def _ret_consts():
    c = RET_CHUNK
    log_g = jnp.log1p(-jnp.exp2(-5.0 - jnp.arange(RET_HEADS, dtype=F32)))
    idx = jnp.arange(c, dtype=F32)
    diff = idx[:, None] - idx[None, :]
    decay = jnp.where(diff[None] >= 0, jnp.exp(jnp.maximum(diff, 0.0)[None] * log_g[:, None, None]), 0.0)
    zeta = jnp.exp((c - 1 - idx)[None, :] * log_g[:, None])
    xi = jnp.exp((idx + 1.0)[None, :] * log_g[:, None])
    spread = lambda v: jnp.repeat(v.T, RET_HEAD_DIM, axis=1)
    log_g_np = np.log1p(-np.exp2(-5.0 - np.arange(RET_HEADS, dtype=np.float32))).astype(np.float32)
    g_chunk = [float(np.exp(np.float32(c) * lg)) for lg in log_g_np]
    return decay, spread(xi), spread(zeta), g_chunk


def _rope_tables(t):
    pos = np.arange(t, dtype=np.float32)
    inv_freq = np.float32(ROPE_BASE) ** (-np.arange(0, RET_HEAD_DIM, 2, dtype=np.float32) / np.float32(RET_HEAD_DIM))
    ang = (pos[:, None] * inv_freq.astype(np.float32)[None, :]).astype(np.float32).astype(np.float64)
    cos, sin = np.cos(ang).astype(np.float32), np.sin(ang).astype(np.float32)
    return jnp.asarray(np.concatenate([cos, cos], axis=-1)), jnp.asarray(np.concatenate([-sin, sin], axis=-1))


def _rope(x, cos2, sin_signed):
    return x * cos2 + pltpu.roll(x, RET_HEAD_DIM // 2, 1) * sin_signed


def _rope_bwd(d, cos2, sin_signed):
    return d * cos2 + pltpu.roll(d * sin_signed, RET_HEAD_DIM // 2, 1)


RET_SCALE = RET_HEAD_DIM ** -0.5


RET_CHUNKS_PER_STEP = MIX_ROW_TILE // RET_CHUNK


def _ret_fwd(proj, cos2, sin_signed, gain):
    t = proj.shape[0]
    c, d, nh = RET_CHUNK, RET_HEAD_DIM, RET_HEADS
    n_chunks = t // c
    per = RET_CHUNKS_PER_STEP if n_chunks % RET_CHUNKS_PER_STEP == 0 else 1
    rows = per * c
    decay, xi, zeta, g_chunk = _ret_consts()

    def units(qk_ref, vg_ref, cos_ref, sin_ref, dec_ref, xi_ref, zeta_ref, gain_ref, o_ref, y_ref, st_ref, state, ctx):
        cur = [None] * nh

        def start():
            @pl.when(pl.program_id(0) == 0)
            def _():
                state[...] = jnp.zeros_like(state)
            for h in range(nh):
                cur[h] = state[h]

        def retain(s, h, keep):
            rs = slice(s * c, (s + 1) * c)
            cos2, sin_s = cos_ref[rs, :], sin_ref[rs, :]
            lo = h * d
            q = _rope(qk_ref[rs, lo:lo + d].astype(F32), cos2, sin_s)
            k = _rope(qk_ref[rs, D_RET + lo:D_RET + lo + d].astype(F32), cos2, sin_s) * RET_SCALE
            v = vg_ref[rs, lo:lo + d]
            s_prev = cur[h]
            st_ref[s, h] = s_prev
            scores = _dot_nt(q, k) * dec_ref[h]
            o = _dot(scores, v) + _dot(q * xi_ref[:, lo:lo + d], s_prev)
            cur[h] = s_prev * g_chunk[h] + _dot_tn(k * zeta_ref[:, lo:lo + d], v)
            o_ref[rs, lo:lo + d] = o
            keep["o"] = o

        def normalise(s, h, keep):
            rs = slice(s * c, (s + 1) * c)
            lo = h * d
            o = keep["o"]
            g = vg_ref[rs, D_RET + lo:D_RET + lo + d].astype(F32)
            mu = jnp.mean(o, axis=-1, keepdims=True)
            oc = o - mu
            on = oc * lax.rsqrt(jnp.mean(oc * oc, axis=-1, keepdims=True) + NORM_EPS)
            y_ref[rs, lo:lo + d] = (on * gain_ref[:, lo:lo + d] * (g * _sigmoid(g))).astype(y_ref.dtype)

        def end():
            for h in range(nh):
                state[h] = cur[h]

        pieces = [start]
        for s in range(per):
            for h in range(nh):
                keep = {}
                pieces += [lambda s=s, h=h, keep=keep: retain(s, h, keep),
                           lambda s=s, h=h, keep=keep: normalise(s, h, keep)]
        return pieces + [end]

    full2 = lambda i: (0, 0)
    return dict(units=units, grid=(n_chunks // per,),
                in_specs=[pl.BlockSpec((rows, 2 * D_RET), lambda i: (i, 1)),
                          pl.BlockSpec((rows, 2 * D_RET), lambda i: (i, 2)),
                          pl.BlockSpec((rows, d), lambda i: (i, 0)), pl.BlockSpec((rows, d), lambda i: (i, 0)),
                          pl.BlockSpec((nh, c, c), lambda i: (0, 0, 0)), pl.BlockSpec((c, D_RET), full2),
                          pl.BlockSpec((c, D_RET), full2), pl.BlockSpec((1, D_RET), full2)],
                out_specs=[pl.BlockSpec((rows, D_RET), lambda i: (i, 0)), pl.BlockSpec((rows, D_RET), lambda i: (i, 0)),
                           pl.BlockSpec((per, nh, d, d), lambda i: (i, 0, 0, 0))],
                out_shape=[jax.ShapeDtypeStruct((t, D_RET), F32), jax.ShapeDtypeStruct((t, D_RET), MXU_DTYPE),
                           jax.ShapeDtypeStruct((n_chunks, nh, d, d), F32)],
                scratch_shapes=[pltpu.VMEM((nh, d, d), F32)],
                operands=[proj, proj, cos2, sin_signed, decay, xi, zeta, gain])


def _outproj_fwd(x, y_lru, y_ret, w_out, g2, comm):
    t = x.shape[0]
    tm = min(PROJ_ROW_TILE, t)

    def body(x_ref, yl_ref, yr_ref, w_ref, g_ref, h1_ref, u2_ref):
        h1 = x_ref[...] + _dot(yl_ref[...], w_ref[:D_LRU, :]) + _dot(yr_ref[...], w_ref[D_LRU:, :])
        h1_ref[...] = h1
        _, _, u = _rms_fwd(h1, g_ref[...])
        u2_ref[...] = u.astype(u2_ref.dtype)

    row = lambda i: (i, 0)
    return _call(body, name="outproj_fwd", grid=(t // tm,), comm=comm,
                 in_specs=[pl.BlockSpec((tm, D_MODEL), row), pl.BlockSpec((tm, D_LRU), row), pl.BlockSpec((tm, D_RET), row),
                           _resident((D_MODEL, D_MODEL)), pl.BlockSpec((1, D_MODEL), lambda i: (0, 0))],
                 out_specs=[pl.BlockSpec((tm, D_MODEL), row), pl.BlockSpec((tm, D_MODEL), row)],
                 out_shape=[jax.ShapeDtypeStruct((t, D_MODEL), F32), jax.ShapeDtypeStruct((t, D_MODEL), MXU_DTYPE)],
                 operands=[x, y_lru, y_ret, w_out, g2])


FFN_TN = 768
FFN_NJ = D_FF // FFN_TN
FFN_GROUP = 4


def _ffn_fwd(u2, w_blocks, conv_w, conv_b, w_down, h1, gf, target):
    t = u2.shape[0]
    tm = min(ROW_TILE, t)
    tn, nj, group = FFN_TN, FFN_NJ, FFN_GROUP
    ng, tw = nj // group, group * tn
    hb = _halo_rows(u2.dtype)
    assert w_blocks.shape == (2 * nj, D_MODEL, tn)

    def conv(ext, col, up_ref, conv_ref, cw_ref, cb_ref, first):
        x = ext[hb:, :]
        up_ref[:, col] = x.astype(up_ref.dtype)
        prev = jnp.where(first, 0.0, ext[hb - SUBLANES:hb, :])
        cw = cw_ref[:, col]
        y = cb_ref[:, col] + cw[FFN_CONV - 1:FFN_CONV, :] * x
        for k in range(FFN_CONV - 1):
            y = y + cw[k:k + 1, :] * _shift_down(prev, x, FFN_CONV - 1 - k)
        conv_ref[:, col] = y.astype(conv_ref.dtype)
        return y

    def body(u_ref, uh_ref, w_ref, cwa_ref, cwv_ref, cba_ref, cbv_ref, wd_ref, h1_ref, gf_ref, tg_ref,
             upa_ref, upv_ref, ca_ref, cv_ref, act_ref, dh_ref, dhb_ref, dgf_ref, loss_ref, acc):
        i, jg = pl.program_id(0), pl.program_id(1)

        @pl.when((i == 0) & (jg == 0))
        def _():
            dgf_ref[...] = jnp.zeros_like(dgf_ref)
            loss_ref[...] = jnp.zeros_like(loss_ref)

        @pl.when(jg == 0)
        def _():
            acc[...] = jnp.zeros_like(acc)

        u_ext = jnp.concatenate([uh_ref[...], u_ref[...]], axis=0)

        def project(jj):
            j = jg * group + jj
            return _dot(u_ext, w_ref[j]), _dot(u_ext, w_ref[nj + j])

        down, ahead = None, project(0)
        for jj in range(group):
            col = slice(jj * tn, (jj + 1) * tn)
            j = jg * group + jj
            ext_a, ext_v = ahead
            if jj + 1 < group:
                ahead = project(jj + 1)
            a = conv(ext_a, col, upa_ref, ca_ref, cwa_ref, cba_ref, i == 0)
            v = conv(ext_v, col, upv_ref, cv_ref, cwv_ref, cbv_ref, i == 0)
            act = (_gelu(a) * v).astype(act_ref.dtype)
            act_ref[:, col] = act
            part = _dot(act, wd_ref[pl.ds(pl.multiple_of(j * tn, tn), tn), :])
            down = part if down is None else down + part
        acc[...] += down

        @pl.when(jg == ng - 1)
        def _():
            n, rstd, y = _rms_fwd(h1_ref[...] + acc[...], gf_ref[...])
            err = y - tg_ref[...]
            loss_ref[...] += (0.5 / D_MODEL) * jnp.sum(err * err)
            dh, dgf = _rms_bwd(err * (1.0 / D_MODEL), n, rstd, gf_ref[...])
            dgf_ref[...] += dgf
            dh_ref[...] = dh
            dhb_ref[...] = dh.astype(dhb_ref.dtype)

    per = tm // hb
    row = lambda i, j: (i, 0)
    const = lambda i, j: (0, 0)
    tile = pl.BlockSpec((tm, tw), lambda i, j: (i, j))
    return _call(body, name="ffn_fwd", grid=(t // tm, ng), vmem_limit=FUSED_VMEM_LIMIT,
                 in_specs=[pl.BlockSpec((tm, D_MODEL), row),
                           pl.BlockSpec((hb, D_MODEL), lambda i, j: (jnp.maximum(i * per - 1, 0), 0)),
                           _resident(w_blocks.shape),
                           pl.BlockSpec((FFN_CONV, tw), lambda i, j: (0, j)),
                           pl.BlockSpec((FFN_CONV, tw), lambda i, j: (0, j + ng)),
                           pl.BlockSpec((1, tw), lambda i, j: (0, j)), pl.BlockSpec((1, tw), lambda i, j: (0, j + ng)),
                           _resident((D_FF, D_MODEL)),
                           pl.BlockSpec((tm, D_MODEL), row), pl.BlockSpec((1, D_MODEL), const),
                           pl.BlockSpec((tm, D_MODEL), row)],
                 out_specs=[tile] * 5 + [pl.BlockSpec((tm, D_MODEL), row),
                            pl.BlockSpec((tm, D_MODEL), row), pl.BlockSpec((SUBLANES, D_MODEL), const),
                            pl.BlockSpec((SUBLANES, LANES), const)],
                 out_shape=[jax.ShapeDtypeStruct((t, D_FF), MXU_DTYPE)] * 5 + [
                            jax.ShapeDtypeStruct((t, D_MODEL), F32),
                            jax.ShapeDtypeStruct((t, D_MODEL), MXU_DTYPE), jax.ShapeDtypeStruct((SUBLANES, D_MODEL), F32),
                            jax.ShapeDtypeStruct((SUBLANES, LANES), F32)],
                 scratch_shapes=[pltpu.VMEM((tm, D_MODEL), F32)],
                 operands=[u2, u2, w_blocks, conv_w, conv_w, conv_b, conv_b, w_down, h1, gf, target])


FFN_ACC_ROWS = SUBLANES * (FFN_CONV + 1)


def _ffn_bwd(dh2, dh2_b, w_down, up_a, up_v, conv_a, conv_v, conv_w, w_up_blocks, h1, g2, comm):
    t = up_a.shape[0]
    tm = min(ROW_TILE, t)
    tn, nj, group = FFN_TN, FFN_NJ, FFN_GROUP
    ng, tw = nj // group, group * tn
    ni = t // tm
    assert w_up_blocks.shape == (2 * nj, D_MODEL, tn)

    def conv_bwd(dy, x, cw, acc_ref, carry_ref, dup_ref, col):
        nxt = carry_ref[...]
        carry_ref[...] = dy[:SUBLANES, :]
        ahead = [_shift_up(dy, nxt, FFN_CONV - 1 - k) for k in range(FFN_CONV)]
        dx = cw[FFN_CONV - 1:FFN_CONV, :] * dy
        for k in range(FFN_CONV - 1):
            dx = dx + cw[k:k + 1, :] * ahead[k]
        dx = dx.astype(dup_ref.dtype)
        dup_ref[:, col] = dx
        for k in range(FFN_CONV):
            acc_ref[k * SUBLANES:(k + 1) * SUBLANES, :] += _colsum8(ahead[k] * x)
        acc_ref[FFN_CONV * SUBLANES:, :] += _colsum8(dy)
        return dx

    def body(dh_ref, dhb_ref, wd_ref, ua_ref, uv_ref, ca_ref, cv_ref, cwa_ref, cwv_ref, wu_ref, h1_ref, g2_ref,
             dua_ref, duv_ref, acca_ref, accv_ref, dh1_ref, dh1b_ref, dg2_ref, carry_a, carry_v, du):
        i, jg = pl.program_id(0), pl.program_id(1)

        @pl.when((i == 0) & (jg == 0))
        def _():
            for ref in (acca_ref, accv_ref, carry_a, carry_v, dg2_ref):
                ref[...] = jnp.zeros_like(ref)

        dhb = dhb_ref[...]

        def through_down(jj):
            j = jg * group + jj
            return _dot_nt(dhb, wd_ref[pl.ds(pl.multiple_of(j * tn, tn), tn), :])

        part, ahead = None, through_down(0)
        for jj in range(group):
            col = slice(jj * tn, (jj + 1) * tn)
            j = jg * group + jj
            dact = ahead
            if jj + 1 < group:
                ahead = through_down(jj + 1)
            v = cv_ref[:, col].astype(F32)
            g, dg = _gelu_parts(ca_ref[:, col].astype(F32))
            da = conv_bwd(dact * v * dg, ua_ref[:, col].astype(F32), cwa_ref[:, col], acca_ref.at[j], carry_a.at[j],
                          dua_ref, col)
            dv = conv_bwd(dact * g, uv_ref[:, col].astype(F32), cwv_ref[:, col], accv_ref.at[j], carry_v.at[j],
                          duv_ref, col)
            term = _dot_nt(da, wu_ref[j]) + _dot_nt(dv, wu_ref[nj + j])
            part = term if part is None else part + term

        @pl.when(jg == 0)
        def _():
            du[...] = part

        @pl.when(jg > 0)
        def _():
            du[...] += part

        @pl.when(jg == ng - 1)
        def _():
            n, rstd, _ = _rms_fwd(h1_ref[...], g2_ref[...])
            dh1, dg2 = _rms_bwd(du[...], n, rstd, g2_ref[...])
            dh1 = dh1 + dh_ref[...]
            dg2_ref[...] += dg2
            dh1_ref[...] = dh1
            dh1b_ref[...] = dh1.astype(dh1b_ref.dtype)

    row = lambda i, j: (ni - 1 - i, 0)
    const = lambda i, j: (0, 0)
    tile = pl.BlockSpec((tm, tw), lambda i, j: (ni - 1 - i, j))
    acc = pl.BlockSpec((nj, FFN_ACC_ROWS, tn), lambda i, j: (0, 0, 0))
    return _call(body, name="ffn_bwd", grid=(ni, ng), comm=comm, vmem_limit=FFN_BWD_VMEM_LIMIT,
                 in_specs=[pl.BlockSpec((tm, D_MODEL), row), pl.BlockSpec((tm, D_MODEL), row),
                           _resident((D_FF, D_MODEL)), tile, tile, tile, tile,
                           pl.BlockSpec((FFN_CONV, tw), lambda i, j: (0, j)),
                           pl.BlockSpec((FFN_CONV, tw), lambda i, j: (0, j + ng)),
                           _resident(w_up_blocks.shape), pl.BlockSpec((tm, D_MODEL), row),
                           pl.BlockSpec((1, D_MODEL), const)],
                 out_specs=[tile, tile, acc, acc, pl.BlockSpec((tm, D_MODEL), row), pl.BlockSpec((tm, D_MODEL), row),
                            pl.BlockSpec((SUBLANES, D_MODEL), const)],
                 out_shape=[jax.ShapeDtypeStruct((t, D_FF), MXU_DTYPE), jax.ShapeDtypeStruct((t, D_FF), MXU_DTYPE),
                            jax.ShapeDtypeStruct((nj, FFN_ACC_ROWS, tn), F32),
                            jax.ShapeDtypeStruct((nj, FFN_ACC_ROWS, tn), F32),
                            jax.ShapeDtypeStruct((t, D_MODEL), F32), jax.ShapeDtypeStruct((t, D_MODEL), MXU_DTYPE),
                            jax.ShapeDtypeStruct((SUBLANES, D_MODEL), F32)],
                 scratch_shapes=[pltpu.VMEM((nj, SUBLANES, tn), F32), pltpu.VMEM((nj, SUBLANES, tn), F32),
                                 pltpu.VMEM((tm, D_MODEL), F32)],
                 operands=[dh2, dh2_b, w_down, up_a, up_v, conv_a, conv_v, conv_w, conv_w, w_up_blocks, h1, g2])


def _ret_bwd(proj, cos2, sin_signed, gain, o, states, dmix_at):
    t = proj.shape[0]
    c, d, nh = RET_CHUNK, RET_HEAD_DIM, RET_HEADS
    n_chunks = t // c
    per = RET_CHUNKS_PER_STEP if n_chunks % RET_CHUNKS_PER_STEP == 0 else 1
    rows = per * c
    n_steps = n_chunks // per
    decay, xi, zeta, g_chunk = _ret_consts()
    base = 2 * D_LRU

    def units(qk_ref, vg_ref, cos_ref, sin_ref, dec_ref, xi_ref, zeta_ref, gain_ref, o_ref, st_ref,
              dp_ref, dgain_ref, gstate, ctx):
        cur = [None] * nh
        dmix = ctx["scratch"][dmix_at[0]][dmix_at[1]]

        def start():
            @pl.when(pl.program_id(0) == 0)
            def _():
                gstate[...] = jnp.zeros_like(gstate)
                dgain_ref[...] = jnp.zeros_like(dgain_ref)
            for h in range(nh):
                cur[h] = gstate[h]

        def gate_and_norm(s, h, keep):
            rs = slice(s * c, (s + 1) * c)
            lo = h * d
            g = vg_ref[rs, D_RET + lo:D_RET + lo + d].astype(F32)
            gain_h = gain_ref[:, lo:lo + d]
            dy = dmix[rs, D_LRU + lo:D_LRU + lo + d]
            sg = _sigmoid(g)
            o_h = o_ref[rs, lo:lo + d]
            oc = o_h - jnp.mean(o_h, axis=-1, keepdims=True)
            rstd = lax.rsqrt(jnp.mean(oc * oc, axis=-1, keepdims=True) + NORM_EPS)
            on = oc * rstd
            at = base + 3 * D_RET + lo
            dp_ref[rs, at:at + d] = (dy * on * gain_h * (sg * (1.0 + g * (1.0 - sg)))).astype(dp_ref.dtype)
            don_g = dy * (g * sg)
            dgain_ref[:, lo:lo + d] += _colsum8(don_g * on)
            don = don_g * gain_h
            keep["do"] = rstd * (don - jnp.mean(don, axis=-1, keepdims=True)
                                 - on * jnp.mean(don * on, axis=-1, keepdims=True))

        def retain(s, h, keep):
            rs = slice(s * c, (s + 1) * c)
            cos2, sin_s = cos_ref[rs, :], sin_ref[rs, :]
            lo = h * d
            q = _rope(qk_ref[rs, lo:lo + d].astype(F32), cos2, sin_s)
            k = _rope(qk_ref[rs, D_RET + lo:D_RET + lo + d].astype(F32), cos2, sin_s) * RET_SCALE
            v = vg_ref[rs, lo:lo + d]
            xi_h, zeta_h, dec = xi_ref[:, lo:lo + d], zeta_ref[:, lo:lo + d], dec_ref[h]
            do = keep["do"]
            s_prev = st_ref[s, h]
            g_next = cur[h]
            p = _dot_nt(q, k) * dec
            dpm = _dot_nt(do, v) * dec
            keep["dq"] = _dot(dpm, k) + _dot_nt(do, s_prev) * xi_h
            keep["dk"] = _dot_tn(dpm, q) + _dot_nt(v, g_next) * zeta_h
            dv = _dot_tn(p, do) + _dot(k * zeta_h, g_next)
            cur[h] = g_next * g_chunk[h] + _dot_tn(q * xi_h, do)
            at = base + 2 * D_RET + lo
            dp_ref[rs, at:at + d] = dv.astype(dp_ref.dtype)

        def unrope(s, h, keep):
            rs = slice(s * c, (s + 1) * c)
            cos2, sin_s = cos_ref[rs, :], sin_ref[rs, :]
            lo = h * d
            dp_ref[rs, base + lo:base + lo + d] = _rope_bwd(keep["dq"], cos2, sin_s).astype(dp_ref.dtype)
            at = base + D_RET + lo
            dp_ref[rs, at:at + d] = _rope_bwd(keep["dk"] * RET_SCALE, cos2, sin_s).astype(dp_ref.dtype)

        def end():
            for h in range(nh):
                gstate[h] = cur[h]

        pieces = [start]
        for s in reversed(range(per)):
            for h in range(nh):
                keep = {}
                pieces += [lambda s=s, h=h, keep=keep, f=f: f(s, h, keep) for f in (gate_and_norm, retain, unrope)]
        return pieces + [end]

    rev = lambda col: (lambda i: (n_steps - 1 - i, col))
    full2 = lambda i: (0, 0)
    return dict(units=units, grid=(n_steps,),
                in_specs=[pl.BlockSpec((rows, 2 * D_RET), rev(1)), pl.BlockSpec((rows, 2 * D_RET), rev(2)),
                          pl.BlockSpec((rows, d), rev(0)), pl.BlockSpec((rows, d), rev(0)),
                          pl.BlockSpec((nh, c, c), lambda i: (0, 0, 0)), pl.BlockSpec((c, D_RET), full2),
                          pl.BlockSpec((c, D_RET), full2), pl.BlockSpec((1, D_RET), full2),
                          pl.BlockSpec((rows, D_RET), rev(0)),
                          pl.BlockSpec((per, nh, d, d), lambda i: (n_steps - 1 - i, 0, 0, 0))],
                out_specs=[pl.BlockSpec((rows, D_IN), rev(0)), pl.BlockSpec((SUBLANES, D_RET), full2)],
                out_shape=[jax.ShapeDtypeStruct((t, D_IN), MXU_DTYPE), jax.ShapeDtypeStruct((SUBLANES, D_RET), F32)],
                scratch_shapes=[pltpu.VMEM((nh, d, d), F32)],
                operands=[proj, proj, cos2, sin_signed, decay, xi, zeta, gain, o, states])


LRU_ACC = {"conv_w": 0, "conv_b": LRU_CONV, "gate_a_b": LRU_CONV + 1, "gate_x_b": LRU_CONV + 2,
           "lambda": LRU_CONV + 3, "norm_gain": LRU_CONV + 4}
LRU_ACC_ROWS = SUBLANES * (LRU_CONV + 5)


def _lru_bwd(proj, xc_all, h_all, conv_w, wa, ba, wx, bx, lam, gain, dproj_part, dmix_at):
    t = proj.shape[0]
    tm = min(MIX_ROW_TILE, t)
    c = D_LRU
    ni = t // tm

    def body(x_ref, xh_ref, g_ref, xc_ref, h_ref, hh_ref, cw_ref, wa_ref, ba_ref, wx_ref, bx_ref, lam_ref,
             gain_ref, acc_ref, dwa_ref, dwx_ref, a_scr, b_scr, mu_scr, carry_mu, carry_dxc, ctx):
        dp_ref = ctx["outs"][dproj_part][0]
        dmix = ctx["scratch"][dmix_at[0]][dmix_at[1]]
        fill = ctx["fill"]
        i = pl.program_id(0)
        r = ni - 1 - i

        @pl.when(i == 0)
        def _():
            acc_ref[...] = jnp.zeros_like(acc_ref)
            dwa_ref[...] = jnp.zeros_like(dwa_ref)
            dwx_ref[...] = jnp.zeros_like(dwx_ref)
            carry_mu[...] = jnp.zeros_like(carry_mu)
            carry_dxc[...] = jnp.zeros_like(carry_dxc)

        def add(name, val, k=0):
            lo = (LRU_ACC[name] + k) * SUBLANES
            acc_ref[lo:lo + SUBLANES, :] += _colsum8(val)

        fill()
        xc, h = xc_ref[...], h_ref[...]
        lam_v = lam_ref[...]
        sp = _softplus(-lam_v)
        rg, ig, a, m = _lru_gates(xc, wa_ref[...], ba_ref[...], wx_ref[...], bx_ref[...], sp, fill)
        gl, dgl = _gelu_parts(g_ref[...].astype(F32))
        fill()
        zn, rstd, _ = _rms_fwd(h * gl, gain_ref[...])
        dy = dmix[:, :c]
        dz, dgain = _rms_bwd(dy, zn, rstd, gain_ref[...])
        lo = LRU_ACC["norm_gain"] * SUBLANES
        acc_ref[lo:lo + SUBLANES, :] += dgain
        dp_ref[:, c:2 * c] = (dz * h * dgl).astype(dp_ref.dtype)
        dh = dz * gl
        fill()
        ga, gb = _group_scan(a, a * dh, reverse=True, fill=fill)
        a_scr[...] = ga
        b_scr[...] = gb
        mu_next_tile = carry_mu[...]
        carry_mu[...] = _carry_scan(a_scr, b_scr, mu_scr, mu_next_tile, reverse=True)
        fill()
        lam_t = dh + _shift_up(mu_scr[...], mu_next_tile, 1)
        h_prev = _shift_down(jnp.where(r == 0, 0.0, hh_ref[...]), h, 1)
        da = lam_t * h_prev
        dig = lam_t * m * xc
        dxc = lam_t * m * ig
        dlog_a = da * a - (lam_t * ig * xc) * (a * a) / m
        fill()
        dpr = dlog_a * ((-LRU_C) * sp) * rg * (1.0 - rg)
        add("lambda", dlog_a * ((-LRU_C) * rg) * (-_sigmoid(-lam_v)))
        dpi = dig * ig * (1.0 - ig)
        add("gate_a_b", dpr)
        add("gate_x_b", dpi)
        fill()
        dwa_ref[...] += _dot_tn(xc, dpr)
        dwx_ref[...] += _dot_tn(xc, dpi)
        dxc = dxc + _dot_nt(dpr, wa_ref[...]) + _dot_nt(dpi, wx_ref[...])
        fill()
        add("conv_b", dxc)
        x = x_ref[...].astype(F32)
        prev = jnp.where(r == 0, 0.0, xh_ref[...].astype(F32)[-SUBLANES:, :])
        cw = cw_ref[...]
        nxt = carry_dxc[...]
        carry_dxc[...] = dxc[:SUBLANES, :]
        dx = cw[LRU_CONV - 1:LRU_CONV, :] * dxc
        for k in range(LRU_CONV - 1):
            dx = dx + cw[k:k + 1, :] * _shift_up(dxc, nxt, LRU_CONV - 1 - k)
        fill()
        for k in range(LRU_CONV):
            add("conv_w", dxc * _shift_down(prev, x, LRU_CONV - 1 - k), k)
        dp_ref[:, :c] = dx.astype(dp_ref.dtype)

    hb = _halo_rows(proj.dtype)
    rev = lambda col: (lambda i: (ni - 1 - i, col))
    halo = lambda rows: (lambda i: (jnp.maximum((ni - 1 - i) * (tm // rows) - 1, 0), 0))
    full = lambda i: (0, 0)
    vec = pl.BlockSpec((1, c), full)
    mat = pl.BlockSpec((c, c), full)
    return dict(body=body, grid=(ni,), fill_points=12,
                in_specs=[pl.BlockSpec((tm, c), rev(0)), pl.BlockSpec((hb, c), halo(hb)), pl.BlockSpec((tm, c), rev(1)),
                          pl.BlockSpec((tm, c), rev(0)), pl.BlockSpec((tm, c), rev(0)),
                          pl.BlockSpec((SUBLANES, c), halo(SUBLANES)),
                          pl.BlockSpec((LRU_CONV, c), full), mat, vec, mat, vec, vec, vec],
                out_specs=[pl.BlockSpec((LRU_ACC_ROWS, c), full), mat, mat],
                out_shape=[jax.ShapeDtypeStruct((LRU_ACC_ROWS, c), F32), jax.ShapeDtypeStruct((c, c), F32),
                           jax.ShapeDtypeStruct((c, c), F32)],
                scratch_shapes=[pltpu.VMEM((tm, c), F32), pltpu.VMEM((tm, c), F32), pltpu.VMEM((tm, c), F32),
                                pltpu.VMEM((SUBLANES, c), F32), pltpu.VMEM((SUBLANES, c), F32)],
                operands=[proj, proj, proj, xc_all, h_all, h_all, conv_w, wa, ba, wx, bx, lam, gain])


def _mix_proj_bwd(dh1, dh1_b, w_out, w_in_blocks, x, g1, dproj_part):
    t = x.shape[0]
    tm = min(MIX_ROW_TILE, t)
    ni = t // tm
    nb, _, cb = w_in_blocks.shape
    first_free = -(-2 * D_LRU // cb)
    du = [None]

    def term(dp_ref, w_ref, d):
        part = _dot_nt(dp_ref[:, d * cb:(d + 1) * cb], w_ref[d])
        du[0] = part if du[0] is None else du[0] + part

    def head(dh_ref, dhb_ref, wo_ref, wi_ref, x_ref, g_ref, gx_ref, dg_ref, dmix, ctx):
        @pl.when(pl.program_id(0) == 0)
        def _():
            dg_ref[...] = jnp.zeros_like(dg_ref)
        dmix[...] = _dot_nt(dhb_ref[...], wo_ref[...])
        du[0] = None

    def units(dh_ref, dhb_ref, wo_ref, wi_ref, x_ref, g_ref, gx_ref, dg_ref, dmix, ctx):
        dp_ref = ctx["outs"][dproj_part][0]
        return [lambda d=d: term(dp_ref, wi_ref, d) for d in range(first_free, nb)]

    def tail(dh_ref, dhb_ref, wo_ref, wi_ref, x_ref, g_ref, gx_ref, dg_ref, dmix, ctx):
        dp_ref = ctx["outs"][dproj_part][0]
        for d in range(first_free):
            term(dp_ref, wi_ref, d)
        n, rstd, _ = _rms_fwd(x_ref[...], g_ref[...])
        dx, dg = _rms_bwd(du[0], n, rstd, g_ref[...])
        dg_ref[...] += dg
        gx_ref[...] = dx + dh_ref[...]

    row = lambda i: (ni - 1 - i, 0)
    const = lambda i: (0, 0)
    tile = pl.BlockSpec((tm, D_MODEL), row)
    return dict(head=head, units=units, tail=tail, grid=(ni,),
                in_specs=[tile, tile, _resident(w_out.shape), _resident(w_in_blocks.shape), tile,
                          pl.BlockSpec((1, D_MODEL), const)],
                out_specs=[tile, pl.BlockSpec((SUBLANES, D_MODEL), const)],
                out_shape=[jax.ShapeDtypeStruct((t, D_MODEL), F32), jax.ShapeDtypeStruct((SUBLANES, D_MODEL), F32)],
                scratch_shapes=[pltpu.VMEM((tm, D_MODEL), F32)],
                operands=[dh1, dh1_b, w_out, w_in_blocks, x, g1])


def _pair_sum(core, a, b, name):
    n, r, c = b.shape
    spec = pl.BlockSpec((None, r, c), lambda q, core: (q, 0, 0))

    def body(core_ref, a_ref, b_ref, o_ref):
        o_ref[...] = (a_ref[...].astype(F32) + b_ref[...].astype(F32)).astype(o_ref.dtype)

    return pl.pallas_call(
        body, name=name,
        grid_spec=pltpu.PrefetchScalarGridSpec(
            num_scalar_prefetch=1, grid=(n,),
            in_specs=[pl.BlockSpec((None, r, c), lambda q, core: (2 * q + core[0], 0, 0)), spec], out_specs=spec),
        out_shape=jax.ShapeDtypeStruct(b.shape, b.dtype),
        compiler_params=pltpu.CompilerParams(dimension_semantics=("arbitrary",), vmem_limit_bytes=VMEM_LIMIT),
    )(core, a, b)


ADAMW_BLOCK_BYTES = 4 * 1024 * 1024


def _sum_adamw(parts, w, m, v, name):
    n_parts, r, c = parts.shape
    tr = r
    while n_parts * tr * c * parts.dtype.itemsize > ADAMW_BLOCK_BYTES and tr % (4 * SUBLANES) == 0:
        tr //= 2

    def body(p_ref, w_ref, m_ref, v_ref, g_ref, d_ref, nm_ref, nv_ref):
        g = p_ref[0].astype(F32)
        for s in range(1, n_parts):
            g = g + p_ref[s].astype(F32)
        nm = ADAM_B1 * m_ref[...] + (1.0 - ADAM_B1) * g
        nv = ADAM_B2 * v_ref[...] + (1.0 - ADAM_B2) * (g * g)
        m_hat = nm / (1.0 - ADAM_B1 ** ADAM_STEP)
        v_hat = nv / (1.0 - ADAM_B2 ** ADAM_STEP)
        g_ref[...] = g
        d_ref[...] = -ADAM_LR * (m_hat / (jnp.sqrt(v_hat) + ADAM_EPS) + ADAM_WD * w_ref[...])
        nm_ref[...] = nm
        nv_ref[...] = nv

    row = pl.BlockSpec((tr, c), lambda i: (i, 0))
    return _call(body, name=name, grid=(r // tr,),
                 in_specs=[pl.BlockSpec((n_parts, tr, c), lambda i: (0, i, 0)), row, row, row],
                 out_specs=[row, row, row, row], out_shape=[jax.ShapeDtypeStruct((r, c), F32)] * 4,
                 operands=[parts, w, m, v])


MATRICES = ("w_in", "w_out", "ffn_up_w", "ffn_down_w")
CONVS = ("lru_conv_w", "ffn_conv_w")
REPLICATED = ("norm1_gain", "lru_conv_b", "lru_gate_a_w", "lru_gate_a_b", "lru_gate_x_w", "lru_gate_x_b", "lru_lambda",
              "lru_norm_gain", "ret_norm_gain", "norm2_gain", "ffn_conv_b", "final_norm_gain")
WEIGHTS = ("norm1_gain", "w_in", "lru_conv_w", "lru_conv_b", "lru_gate_a_w", "lru_gate_a_b", "lru_gate_x_w",
           "lru_gate_x_b", "lru_lambda", "lru_norm_gain", "ret_norm_gain", "w_out", "norm2_gain", "ffn_up_w",
           "ffn_conv_w", "ffn_conv_b", "ffn_down_w", "final_norm_gain")


def _rows(a, pad_to):
    a = a.reshape(-1, LANES)
    pad = (-a.shape[0]) % pad_to
    return jnp.pad(a, ((0, pad), (0, 0))) if pad else a


def _pack(arrays, pad_to):
    rows, layout, at = [], [], 0
    for a in arrays:
        r = _rows(a, pad_to)
        layout.append((at, a.size // LANES, a.shape))
        rows.append(r)
        at += r.shape[0]
    return jnp.concatenate(rows, axis=0), layout


def _unpack(packed, layout):
    lead = packed.shape[:-2]
    return [packed[..., at:at + n, :].reshape(lead + shape) for at, n, shape in layout]


def _conv_rows(lru, ffn, dtype, pad_to):
    lead = lru.shape[:-2]
    flat = jnp.concatenate([lru.reshape(lead + (-1,)), ffn.reshape(lead + (-1,))], axis=-1).astype(dtype)
    rows = flat.shape[-1] // LANES
    pad = (-rows) % pad_to
    return jnp.pad(flat.reshape(lead + (rows, LANES)), [(0, 0)] * len(lead) + [(0, pad), (0, 0)])


def _column_blocks(full):
    r, c = full.shape
    return full.reshape(r, N_DEV, c // N_DEV).transpose(1, 0, 2)


def _block_diag(w):
    nh, d, _ = w.shape
    eye = jnp.eye(nh, dtype=w.dtype)
    return (w[:, :, None, :] * eye[:, None, :, None]).reshape(nh * d, nh * d)


def _diag_blocks(dense, nh):
    d = dense.shape[0] // nh
    blocks = dense.reshape(nh, d, nh, d)
    return jnp.stack([blocks[h, :, h, :] for h in range(nh)], axis=0)


def kernel(x, norm1_gain, w_in, lru_conv_w, lru_conv_b, lru_gate_a_w, lru_gate_a_b, lru_gate_x_w, lru_gate_x_b, lru_lambda, lru_norm_gain, ret_norm_gain, w_out, norm2_gain, ffn_up_w, ffn_conv_w, ffn_conv_b, ffn_down_w, final_norm_gain, loss_target, m_norm1_gain, m_w_in, m_lru_conv_w, m_lru_conv_b, m_lru_gate_a_w, m_lru_gate_a_b, m_lru_gate_x_w, m_lru_gate_x_b, m_lru_lambda, m_lru_norm_gain, m_ret_norm_gain, m_w_out, m_norm2_gain, m_ffn_up_w, m_ffn_conv_w, m_ffn_conv_b, m_ffn_down_w, m_final_norm_gain, v_norm1_gain, v_w_in, v_lru_conv_w, v_lru_conv_b, v_lru_gate_a_w, v_lru_gate_a_b, v_lru_gate_x_w, v_lru_gate_x_b, v_lru_lambda, v_lru_norm_gain, v_ret_norm_gain, v_w_out, v_norm2_gain, v_ffn_up_w, v_ffn_conv_w, v_ffn_conv_b, v_ffn_down_w, v_final_norm_gain):
    args = dict(locals())
    given = {n: args[n] for n in WEIGHTS}
    out_shape = {n: given[n].shape for n in WEIGHTS}

    def plain(a):
        return a.reshape(1, -1) if a.ndim <= 2 else a[0]

    w = {n: plain(given[n]) for n in WEIGHTS}
    mom_m = {n: plain(args["m_" + n]) for n in WEIGHTS}
    mom_v = {n: plain(args["v_" + n]) for n in WEIGHTS}
    x2, target = x[0], loss_target[0]
    t = x2.shape[0]
    core = lax.axis_index("c").astype(jnp.int32).reshape(1)
    res = {}

    conv_pad = _conv_rows(w["lru_conv_w"], w["ffn_conv_w"], F32, SUBLANES)
    first = _gather_first([w["w_in"].astype(MXU_DTYPE), conv_pad])
    w_in_blocks, conv_all = _run_comms([first, _gather_second(first.out_shape)], "w_in_all_gather")
    n_lru = w["lru_conv_w"].size
    conv_flat = conv_all.reshape(N_DEV, -1)
    lru_cw = conv_flat[:, :n_lru].reshape((N_DEV,) + w["lru_conv_w"].shape).transpose(1, 0, 2).reshape(LRU_CONV, D_LRU)
    ffn_cw = conv_flat[:, n_lru:n_lru + w["ffn_conv_w"].size].reshape((N_DEV,) + w["ffn_conv_w"].shape)
    ffn_cw = ffn_cw.transpose(1, 0, 2).reshape(FFN_CONV, 2 * D_FF)

    cos2, sin_signed = _rope_tables(t)
    wa = _block_diag(w["lru_gate_a_w"]).astype(MXU_DTYPE)
    wx = _block_diag(w["lru_gate_x_w"]).astype(MXU_DTYPE)
    gf = w["final_norm_gain"]

    early = _gather_first([w["w_out"].astype(MXU_DTYPE), w["ffn_down_w"].astype(MXU_DTYPE)])
    (u1, proj), (w_out_part, down_part) = _inproj_fwd(x2, w["norm1_gain"], w_in_blocks, early)
    ((xc, h_lru, y_lru), (o_ret, y_ret, states)), (w_out_blocks, down_blocks, up_part) = _fused(
        [_lru_fwd(proj, lru_cw, w["lru_conv_b"], wa, w["lru_gate_a_b"], wx, w["lru_gate_x_b"], w["lru_lambda"],
                  w["lru_norm_gain"]),
         _ret_fwd(proj, cos2, sin_signed, w["ret_norm_gain"])],
        "mix_fwd", _both(_gather_second([w_out_part, down_part]), _gather_first([w["ffn_up_w"].astype(MXU_DTYPE)])))
    w_out_full = w_out_blocks.reshape(D_MODEL, D_MODEL)
    w_down_full = down_blocks.reshape(D_FF, D_MODEL)

    (h1, u2), (up_blocks,) = _outproj_fwd(x2, y_lru, y_ret, w_out_full, w["norm2_gain"], _gather_second([up_part]))
    up_a, up_v, conv_a, conv_v, act, dh2, dh2_b, dgf, loss_local = _ffn_fwd(u2, up_blocks, ffn_cw, w["ffn_conv_b"],
                                                                            w_down_full, h1, gf, target)

    def to_owner_chips(blocks, names, tag):
        theirs = _run_comms([_pair_exchange(blocks)], "grads_pair_exchange_" + tag)
        return [_pair_sum(core, a, b, "grads_pair_sum_" + n) for n, a, b in zip(names, blocks, theirs)]

    def adamw(name, parts):
        res[name] = _sum_adamw(parts, w[name], mom_m[name], mom_v[name], "adamw_" + name)

    g = {"final_norm_gain": dgf[0]}
    dup_a, dup_v, acc_a, acc_v, dh1, dh1_b, dg2 = _ffn_bwd(
        dh2, dh2_b, w_down_full, up_a, up_v, conv_a, conv_v, ffn_cw, up_blocks, h1, w["norm2_gain"], None)
    per_col = lambda a: a[:, ::SUBLANES].transpose(1, 0, 2).reshape(FFN_CONV + 1, D_FF)
    acc = jnp.concatenate([per_col(acc_a), per_col(acc_v)], axis=1)
    g_ffn_cw, g["ffn_conv_b"] = acc[:FFN_CONV], acc[FFN_CONV:]
    g["norm2_gain"] = dg2[:1]
    g_up = jnp.concatenate([_mm_tn(u2, dup_a, "ffn_up_wgrad_a", blocks=N_DEV // 2),
                            _mm_tn(u2, dup_v, "ffn_up_wgrad_v", blocks=N_DEV // 2)], axis=0)
    up_sums = to_owner_chips([g_up], ["ffn_up_w"], "up")
    g_down, (up_parts,) = _mm_tn(act, dh2_b, "ffn_down_wgrad", comm=_chip_exchange(up_sums))
    adamw("ffn_up_w", up_parts)
    g_out = jnp.concatenate([_mm_tn(y_lru, dh1_b, "w_out_wgrad_lru"), _mm_tn(y_ret, dh1_b, "w_out_wgrad_ret")], axis=0)
    low_sums = to_owner_chips([g_down.reshape(N_DEV, D_FF // N_DEV, D_MODEL),
                               g_out.reshape(N_DEV, D_MODEL // N_DEV, D_MODEL)], ["ffn_down_w", "w_out"], "low")
    (dproj, dgain_ret), (grad_x, dg1), (lru_acc, dwa, dwx) = _fused(
        [_ret_bwd(proj, cos2, sin_signed, w["ret_norm_gain"], o_ret, states, dmix_at=(1, 0)),
         _mix_proj_bwd(dh1, dh1_b, w_out_full, w_in_blocks, x2, w["norm1_gain"], dproj_part=0),
         _lru_bwd(proj, xc, h_lru, lru_cw, wa, w["lru_gate_a_b"], wx, w["lru_gate_x_b"], w["lru_lambda"],
                  w["lru_norm_gain"], dproj_part=0, dmix_at=(1, 0))],
        "mix_bwd")
    g["norm1_gain"] = dg1[:1]
    g["ret_norm_gain"] = dgain_ret[:1]
    lru_acc = lru_acc[::SUBLANES]
    g_lru_cw = lru_acc[:LRU_CONV]
    for name in ("conv_b", "gate_a_b", "gate_x_b", "lambda", "norm_gain"):
        g["lru_" + name] = lru_acc[LRU_ACC[name]:LRU_ACC[name] + 1]
    g["lru_gate_a_w"] = _diag_blocks(dwa, LRU_HEADS)
    g["lru_gate_x_w"] = _diag_blocks(dwx, LRU_HEADS)
    rep_packed, rep_layout = _pack([g[n] for n in REPLICATED] + [loss_local], SUBLANES)
    g_in, (down_parts, out_parts, rep_part) = _mm_tn(u1, dproj, "w_in_wgrad", blocks=N_DEV,
                                                     comm=_both(_chip_exchange(low_sums), _gather_first([rep_packed])))
    adamw("ffn_down_w", down_parts)
    adamw("w_out", out_parts)
    g_conv = _conv_rows(_column_blocks(g_lru_cw), _column_blocks(g_ffn_cw), GRAD_DTYPE, 2 * SUBLANES)
    in_sums = to_owner_chips([g_in, g_conv], ["w_in", "conv"], "in")
    in_parts, conv_parts, rep_parts = _run_comms([_both(_chip_exchange(in_sums), _gather_second([rep_part]))],
                                                 "last_grads_exchange")
    adamw("w_in", in_parts)
    pad16 = lambda d: _conv_rows(d["lru_conv_w"], d["ffn_conv_w"], F32, 2 * SUBLANES)
    conv_res = _sum_adamw(conv_parts, pad16(w), pad16(mom_m), pad16(mom_v), "adamw_conv")
    for n, lo, hi in (("lru_conv_w", 0, n_lru), ("ffn_conv_w", n_lru, n_lru + w["ffn_conv_w"].size)):
        res[n] = [r.reshape(-1)[lo:hi].reshape(w[n].shape) for r in conv_res]
    no_state = jnp.zeros_like(loss_local)
    rep_res = _sum_adamw(rep_parts, *[_pack([d[n] for n in REPLICATED] + [no_state], SUBLANES)[0]
                                      for d in (w, mom_m, mom_v)], "adamw_replicated")
    for k in range(4):
        for n, a in zip(REPLICATED, _unpack(rep_res[k], rep_layout)):
            res.setdefault(n, [None] * 4)[k] = a
    loss = _unpack(rep_res[0], rep_layout)[-1][0, 0]

    outs = [loss, grad_x[None]]
    for k in range(4):
        outs += [res[n][k].reshape(out_shape[n]) for n in WEIGHTS]
    return tuple(outs)
```

```python
import math

import numpy as np
import jax
import jax.numpy as jnp
from jax import lax
from jax.experimental import pallas as pl
from jax.experimental.pallas import tpu as pltpu

F32 = jnp.float32
BF16 = jnp.bfloat16
MXU_DTYPE = jnp.bfloat16
GRAD_DTYPE = jnp.bfloat16

N_DEV = 8
N_CHIPS = 4
D_MODEL = 1024
D_LRU = 512
LRU_HEADS = 8
LRU_CONV = 4
LRU_C = 8.0
D_RET = 512
RET_HEADS = 4
RET_HEAD_DIM = 128
RET_CHUNK = 128
ROPE_BASE = 10000.0
D_IN = 3072
D_FF = 3072
FFN_CONV = 3
NORM_EPS = 1e-6

ADAM_LR = 0.001
ADAM_B1 = 0.9
ADAM_B2 = 0.999
ADAM_EPS = 1e-08
ADAM_WD = 0.01
ADAM_STEP = 10

SUBLANES = 8
LANES = 128
VMEM_LIMIT = 48 * 1024 * 1024
FUSED_VMEM_LIMIT = VMEM_LIMIT
FFN_BWD_VMEM_LIMIT = 56 * 1024 * 1024

ROW_TILE = 256
MIX_ROW_TILE = 256
PROJ_ROW_TILE = 512
WGRAD_ROWS = 2048
WGRAD_TILE = 1024
WGRAD_BLOCK_COLUMNS = 768

MESH = pl.DeviceIdType.MESH
ANY = pl.BlockSpec(memory_space=pl.ANY)


def _dot(a, b):
    return jnp.dot(a.astype(MXU_DTYPE), b.astype(MXU_DTYPE), preferred_element_type=F32)


def _dot_nt(a, b):
    return lax.dot_general(a.astype(MXU_DTYPE), b.astype(MXU_DTYPE), (((1,), (1,)), ((), ())),
                           preferred_element_type=F32)


def _dot_tn(a, b):
    return lax.dot_general(a.astype(MXU_DTYPE), b.astype(MXU_DTYPE), (((0,), (0,)), ((), ())),
                           preferred_element_type=F32)


def _sigmoid(x):
    return 0.5 + 0.5 * jnp.tanh(0.5 * x)


_GELU_C = math.sqrt(2.0 / math.pi)
_GELU_C3 = _GELU_C * 0.044715


def _gelu_parts(x):
    x2 = x * x
    t = jnp.tanh(x * (_GELU_C + _GELU_C3 * x2))
    cdf = 0.5 + 0.5 * t
    g = x * cdf
    dg = cdf + (0.5 * x) * (1.0 - t * t) * (_GELU_C + (3.0 * _GELU_C3) * x2)
    return g, dg


def _gelu(x):
    t = jnp.tanh(_GELU_C * (x + 0.044715 * (x * x * x)))
    return x * (0.5 * (1.0 + t))


def _softplus(x):
    return jnp.maximum(x, 0.0) + jnp.log1p(jnp.exp(-jnp.abs(x)))


def _bcast_row(x, r, rows=SUBLANES):
    return jnp.broadcast_to(x[r:r + 1, :], (rows, x.shape[1]))


def _colsum8(x):
    return jnp.broadcast_to(jnp.sum(x, axis=0, keepdims=True), (SUBLANES, x.shape[1]))


def _groups(x):
    return x.reshape(x.shape[0] // SUBLANES, SUBLANES, x.shape[1])


def _shift_down(prev8, tile, s):
    if s == 0:
        return tile
    own = pltpu.roll(_groups(tile), s, 1)
    before = jnp.concatenate([pltpu.roll(_groups(prev8), s, 1), own[:-1]], axis=0)
    row = lax.broadcasted_iota(jnp.int32, own.shape, 1)
    return jnp.where(row >= s, own, before).reshape(tile.shape)


def _shift_up(tile, next8, s):
    if s == 0:
        return tile
    own = pltpu.roll(_groups(tile), SUBLANES - s, 1)
    after = jnp.concatenate([own[1:], pltpu.roll(_groups(next8), SUBLANES - s, 1)], axis=0)
    row = lax.broadcasted_iota(jnp.int32, own.shape, 1)
    return jnp.where(row < SUBLANES - s, own, after).reshape(tile.shape)


def _group_scan(a, b, reverse, fill=lambda: None):
    n, c = a.shape
    row = lax.broadcasted_iota(jnp.int32, a.shape, 0) & (SUBLANES - 1)

    def within_group(x, shift):
        return pltpu.roll(x.reshape(n // SUBLANES, SUBLANES, c), shift, 1).reshape(n, c)

    for s in (1, 2, 4):
        if s > 1:
            fill()
        shift = (SUBLANES - s) if reverse else s
        a_sh = within_group(a, shift)
        b_sh = within_group(b, shift)
        m = (row <= SUBLANES - 1 - s) if reverse else (row >= s)
        b = jnp.where(m, a * b_sh + b, b)
        a = jnp.where(m, a * a_sh, a)
    return a, b


def _carry_scan(a_ref, b_ref, out_ref, carry0, reverse):
    n_groups = a_ref.shape[0] // SUBLANES
    carry = carry0
    for i in range(n_groups):
        r0 = ((n_groups - 1 - i) if reverse else i) * SUBLANES
        hg = a_ref[r0:r0 + SUBLANES, :] * carry + b_ref[r0:r0 + SUBLANES, :]
        out_ref[r0:r0 + SUBLANES, :] = hg
        carry = _bcast_row(hg, 0 if reverse else SUBLANES - 1)
    return carry


def _rms_fwd(h, gain):
    rstd = lax.rsqrt(jnp.mean(h * h, axis=-1, keepdims=True) + NORM_EPS)
    n = h * rstd
    return n, rstd, n * gain


def _rms_bwd(dy, n, rstd, gain):
    dn = dy * gain
    dh = rstd * (dn - n * jnp.mean(dn * n, axis=-1, keepdims=True))
    return dh, _colsum8(dy * n)


def _halo_rows(dtype):
    return SUBLANES * (4 // jnp.dtype(dtype).itemsize)


def _halo_map(tile_rows, col, halo_rows=SUBLANES):
    per = tile_rows // halo_rows
    return lambda i: (jnp.maximum(i * per - 1, 0), col)


def _resident(shape):
    return pl.BlockSpec(shape, lambda *_: (0,) * len(shape), pipeline_mode=pl.Buffered(1))


def _place():
    x, y, c = lax.axis_index("x"), lax.axis_index("y"), lax.axis_index("c")
    chips = [(1 - x, y), (x, 1 - y), (1 - x, 1 - y)]
    return x, y, c, chips


def _dev(x, y, c):
    return 4 * x + 2 * y + c


class _Copy:
    def __init__(self, make):
        self.make = make

    def start(self):
        self.make().start()

    def wait(self):
        self.make().wait()

    def wait_send(self):
        self.make().wait_send()

    def wait_recv(self):
        self.make().wait_recv()


def _remote(src, dst, send_sem, recv_sem, to):
    return _Copy(lambda: pltpu.make_async_remote_copy(src_ref=src, dst_ref=dst, send_sem=send_sem, recv_sem=recv_sem,
                                                      device_id=to, device_id_type=MESH))


def _local(src, dst, sem):
    return _Copy(lambda: pltpu.make_async_copy(src, dst, sem))


class _Comm:
    def __init__(self, operands, out_shape, sems, descs, aliases=()):
        self.operands, self.out_shape, self.sems, self.descs, self.aliases = operands, out_shape, sems, descs, aliases

    def start(self, ins, outs, sems):
        local, sends, _ = self.descs(ins, outs, sems)
        for cp in sends + local:
            cp.start()

    def wait(self, ins, outs, sems):
        local, sends, recvs = self.descs(ins, outs, sems)
        for cp in recvs:
            cp.wait_recv()
        for cp in sends:
            cp.wait_send()
        for cp in local:
            cp.wait()


def _gather_first(shards):
    n = len(shards)

    def descs(ins, outs, sems):
        send, recv, loc = sems
        x, y, c, chips = _place()
        me = _dev(x, y, c)
        targets = [(x, y, 1 - c)] + [(*chip, c) for chip in chips]
        local, sends, recvs = [], [], []
        for t in range(n):
            local.append(_local(ins[t], outs[t].at[me], loc.at[t]))
            for k, to in enumerate(targets):
                i = 4 * t + k
                sends.append(_remote(ins[t], outs[t].at[me], send.at[i], recv.at[i], to))
                recvs.append(_remote(ins[t], outs[t].at[_dev(*to)], send.at[i], recv.at[i], to))
        return local, sends, recvs

    return _Comm(list(shards), [jax.ShapeDtypeStruct((N_DEV,) + s.shape, s.dtype) for s in shards],
                 [pltpu.SemaphoreType.DMA((4 * n,)), pltpu.SemaphoreType.DMA((4 * n,)), pltpu.SemaphoreType.DMA((n,))],
                 descs)


def _gather_second(gathered):
    n = len(gathered)

    def descs(ins, outs, sems):
        send, recv = sems
        x, y, c, chips = _place()
        sends, recvs = [], []
        for t in range(n):
            for j, chip in enumerate(chips):
                i = 3 * t + j
                have, get = _dev(*chip, c), _dev(*chip, 1 - c)
                sends.append(_remote(outs[t].at[have], outs[t].at[have], send.at[i], recv.at[i], (x, y, 1 - c)))
                recvs.append(_remote(outs[t].at[have], outs[t].at[get], send.at[i], recv.at[i], (x, y, 1 - c)))
        return [], sends, recvs

    return _Comm(list(gathered), [jax.ShapeDtypeStruct(g.shape, g.dtype) for g in gathered],
                 [pltpu.SemaphoreType.DMA((3 * n,)), pltpu.SemaphoreType.DMA((3 * n,))], descs,
                 aliases=[(t, t) for t in range(n)])


def _pair_exchange(blocks):
    n = len(blocks)

    def descs(ins, outs, sems):
        send, recv = sems
        x, y, c, _ = _place()
        sends, recvs = [], []
        for t in range(n):
            for q in range(N_CHIPS):
                i = N_CHIPS * t + q
                cp = _remote(ins[t].at[2 * q + 1 - c], outs[t].at[q], send.at[i], recv.at[i], (x, y, 1 - c))
                sends.append(cp)
                recvs.append(cp)
        return [], sends, recvs

    return _Comm(list(blocks), [jax.ShapeDtypeStruct((N_CHIPS,) + b.shape[1:], b.dtype) for b in blocks],
                 [pltpu.SemaphoreType.DMA((N_CHIPS * n,)), pltpu.SemaphoreType.DMA((N_CHIPS * n,))], descs)


def _chip_exchange(blocks):
    n = len(blocks)

    def descs(ins, outs, sems):
        send, recv, loc = sems
        x, y, c, chips = _place()
        me = 2 * x + y
        local, sends, recvs = [], [], []
        for t in range(n):
            local.append(_local(ins[t].at[me], outs[t].at[me], loc.at[t]))
            for j, (px, py) in enumerate(chips):
                i = 3 * t + j
                q = 2 * px + py
                sends.append(_remote(ins[t].at[q], outs[t].at[me], send.at[i], recv.at[i], (px, py, c)))
                recvs.append(_remote(ins[t].at[q], outs[t].at[q], send.at[i], recv.at[i], (px, py, c)))
        return local, sends, recvs

    return _Comm(list(blocks), [jax.ShapeDtypeStruct(b.shape, b.dtype) for b in blocks],
                 [pltpu.SemaphoreType.DMA((3 * n,)), pltpu.SemaphoreType.DMA((3 * n,)), pltpu.SemaphoreType.DMA((n,))],
                 descs)


def _both(a, b):
    na, oa, sa = len(a.operands), len(a.out_shape), len(a.sems)

    def descs(ins, outs, sems):
        local_a, sends_a, recvs_a = a.descs(ins[:na], outs[:oa], sems[:sa])
        local_b, sends_b, recvs_b = b.descs(ins[na:], outs[oa:], sems[sa:])
        return local_a + local_b, sends_a + sends_b, recvs_a + recvs_b

    return _Comm(a.operands + b.operands, a.out_shape + b.out_shape, a.sems + b.sems, descs,
                 aliases=list(a.aliases) + [(na + i, oa + o) for i, o in b.aliases])


def _run_comms(comms, name):
    first = comms[0]
    n_in, n_out = len(first.operands), len(first.out_shape)

    def body(*refs):
        ins, outs, sems = refs[:n_in], refs[n_in:n_in + n_out], list(refs[n_in + n_out:])
        for k, comm in enumerate(comms):
            mine = [sems.pop(0) for _ in comm.sems]
            comm.start(ins if k == 0 else outs, outs, mine)
            comm.wait(ins if k == 0 else outs, outs, mine)

    outs = pl.pallas_call(
        body, name=name, out_shape=first.out_shape, in_specs=[ANY] * n_in, out_specs=[ANY] * n_out,
        scratch_shapes=[s for comm in comms for s in comm.sems], input_output_aliases=dict(first.aliases),
    )(*first.operands)
    return list(outs)


def _call(body, *, name, grid, in_specs, out_specs, out_shape, operands, scratch_shapes=(), comm=None, aliases=None,
          vmem_limit=VMEM_LIMIT):
    sem = ("arbitrary",) * len(grid)
    params = pltpu.CompilerParams(dimension_semantics=sem, vmem_limit_bytes=vmem_limit)
    aliases = dict(aliases or {})
    if comm is None:
        return pl.pallas_call(body, name=name, grid=grid, in_specs=in_specs, out_specs=out_specs, out_shape=out_shape,
                              scratch_shapes=list(scratch_shapes), input_output_aliases=aliases,
                              compiler_params=params)(*operands)
    n_in, n_out, n_scr = len(in_specs), len(out_specs), len(scratch_shapes)
    c_in, c_out = len(comm.operands), len(comm.out_shape)

    def wrapped(*refs):
        refs = list(refs)
        ins, refs = refs[:n_in], refs[n_in:]
        cins, refs = refs[:c_in], refs[c_in:]
        outs, refs = refs[:n_out], refs[n_out:]
        couts, refs = refs[:c_out], refs[c_out:]
        scr, csems = refs[:n_scr], refs[n_scr:]
        first = last = None
        for axis, size in enumerate(grid):
            at_first, at_last = pl.program_id(axis) == 0, pl.program_id(axis) == size - 1
            first = at_first if first is None else first & at_first
            last = at_last if last is None else last & at_last

        @pl.when(first)
        def _():
            comm.start(cins, couts, csems)

        body(*ins, *outs, *scr)

        @pl.when(last)
        def _():
            comm.wait(cins, couts, csems)

    res = pl.pallas_call(
        wrapped, name=name, grid=grid, in_specs=list(in_specs) + [ANY] * c_in, out_specs=list(out_specs) + [ANY] * c_out,
        out_shape=list(out_shape) + list(comm.out_shape), scratch_shapes=list(scratch_shapes) + list(comm.sems),
        input_output_aliases={**aliases, **{n_in + i: n_out + o for i, o in comm.aliases}}, compiler_params=params,
    )(*operands, *comm.operands)
    return list(res[:n_out]), list(res[n_out:])


def _mm_tn(a, b, name, blocks=1, comm=None):
    t, m = a.shape
    n = b.shape[1]
    tk = min(WGRAD_ROWS, t)
    nk = t // tk
    cb = n // blocks
    per = max(1, WGRAD_BLOCK_COLUMNS // cb) if blocks > 1 else 1
    tn = per * cb if blocks > 1 else min(WGRAD_TILE, n)
    tm = min(WGRAD_TILE, m)
    assert blocks == 1 or tm == m

    def body(a_ref, b_ref, o_ref, acc):
        k = pl.program_id(2)

        @pl.when(k == 0)
        def _():
            acc[...] = jnp.zeros_like(acc)
        acc[...] += _dot_tn(a_ref[...], b_ref[...])

        @pl.when(k == nk - 1)
        def _():
            if blocks == 1:
                o_ref[...] = acc[...].astype(o_ref.dtype)
            else:
                for s in range(per):
                    o_ref[s] = acc[:, s * cb:(s + 1) * cb].astype(o_ref.dtype)

    if blocks == 1:
        out_spec = pl.BlockSpec((tm, tn), lambda i, j, k: (i, j))
        out_shape = jax.ShapeDtypeStruct((m, n), GRAD_DTYPE)
    else:
        out_spec = pl.BlockSpec((per, m, cb), lambda i, j, k: (j, 0, 0))
        out_shape = jax.ShapeDtypeStruct((blocks, m, cb), GRAD_DTYPE)
    res = _call(body, name=name, grid=(m // tm, n // tn, nk), comm=comm,
                in_specs=[pl.BlockSpec((tk, tm), lambda i, j, k: (k, i)), pl.BlockSpec((tk, tn), lambda i, j, k: (k, j))],
                out_specs=[out_spec], out_shape=[out_shape], operands=[a, b],
                scratch_shapes=[pltpu.VMEM((tm, tn), F32)])
    return res[0] if comm is None else (res[0][0], res[1])


def _inproj_fwd(x, g1, w_blocks, comm):
    t = x.shape[0]
    tm = min(PROJ_ROW_TILE, t)
    nb, _, cb = w_blocks.shape

    def body(x_ref, g_ref, w_ref, u_ref, p_ref):
        _, _, u = _rms_fwd(x_ref[...], g_ref[...])
        u = u.astype(MXU_DTYPE)
        u_ref[...] = u
        for d in range(nb):
            p_ref[:, d * cb:(d + 1) * cb] = _dot(u, w_ref[d]).astype(p_ref.dtype)

    return _call(body, name="inproj_fwd", grid=(t // tm,), comm=comm,
                 in_specs=[pl.BlockSpec((tm, D_MODEL), lambda i: (i, 0)), pl.BlockSpec((1, D_MODEL), lambda i: (0, 0)),
                           _resident(w_blocks.shape)],
                 out_specs=[pl.BlockSpec((tm, D_MODEL), lambda i: (i, 0)), pl.BlockSpec((tm, D_IN), lambda i: (i, 0))],
                 out_shape=[jax.ShapeDtypeStruct((t, D_MODEL), MXU_DTYPE), jax.ShapeDtypeStruct((t, D_IN), MXU_DTYPE)],
                 operands=[x, g1, w_blocks])


def _lru_gates(xc, wa, ba, wx, bx, sp, fill=lambda: None):
    r = _sigmoid(_dot(xc, wa) + ba)
    fill()
    ig = _sigmoid(_dot(xc, wx) + bx)
    fill()
    log_a = (-LRU_C) * r * sp
    a = jnp.exp(log_a)
    m = jnp.sqrt(-jnp.tanh(log_a) * (a * a + 1.0))
    return r, ig, a, m


def _fused(parts, name, comm=None):
    grid = parts[0]["grid"]
    assert all(p["grid"] == grid for p in parts)
    counts = [(len(p["in_specs"]), len(p["out_specs"]), len(p.get("scratch_shapes", ()))) for p in parts]

    def body(*refs):
        refs = list(refs)
        groups = []
        for kind in range(3):
            taken = []
            for c in counts:
                taken.append(refs[:c[kind]])
                refs = refs[c[kind]:]
            groups.append(taken)
        ins, outs, scr = groups
        pending = []

        def fill(n=None):
            for _ in range(share if n is None else n):
                if pending:
                    pending.pop(0)()

        ctx = dict(outs=outs, scratch=scr, fill=fill)
        run = lambda key: [p[key](*ins[k], *outs[k], *scr[k], ctx) for k, p in enumerate(parts) if key in p]
        run("head")
        for pieces in run("units"):
            pending.extend(pieces)
        points = sum(p.get("fill_points", 0) for p in parts)
        share = -(-len(pending) // max(points, 1))
        run("body")
        fill(len(pending))
        run("tail")

    cat = lambda key: [x for p in parts for x in p.get(key, ())]
    res = _call(body, name=name, grid=grid, comm=comm, vmem_limit=FUSED_VMEM_LIMIT,
                in_specs=cat("in_specs"), out_specs=cat("out_specs"),
                out_shape=cat("out_shape"), scratch_shapes=cat("scratch_shapes"), operands=cat("operands"))
    outs, side = (res if comm is not None else (res, None))
    split, at = [], 0
    for _, n_out, _ in counts:
        split.append(list(outs[at:at + n_out]))
        at += n_out
    return split if comm is None else (split, side)


def _lru_fwd(proj, conv_w, conv_b, wa, ba, wx, bx, lam, gain):
    t = proj.shape[0]
    tm = min(MIX_ROW_TILE, t)
    c = D_LRU

    def body(x_ref, xh_ref, g_ref, cw_ref, cb_ref, wa_ref, ba_ref, wx_ref, bx_ref, lam_ref, gain_ref,
             xc_ref, h_ref, y_ref, a_scr, b_scr, carry, ctx):
        fill = ctx["fill"]
        i = pl.program_id(0)

        @pl.when(i == 0)
        def _():
            carry[...] = jnp.zeros_like(carry)

        fill()
        x = x_ref[...].astype(F32)
        prev = jnp.where(i == 0, 0.0, xh_ref[...].astype(F32)[-SUBLANES:, :])
        cw = cw_ref[...]
        xc = cb_ref[...] + cw[LRU_CONV - 1:LRU_CONV, :] * x
        for k in range(LRU_CONV - 1):
            xc = xc + cw[k:k + 1, :] * _shift_down(prev, x, LRU_CONV - 1 - k)
        xc_ref[...] = xc
        fill()
        sp = _softplus(-lam_ref[...])
        _, ig, a, m = _lru_gates(xc, wa_ref[...], ba_ref[...], wx_ref[...], bx_ref[...], sp, fill)
        fill()
        ga, gb = _group_scan(a, m * (ig * xc), reverse=False, fill=fill)
        a_scr[...] = ga
        b_scr[...] = gb
        fill()
        carry[...] = _carry_scan(a_scr, b_scr, h_ref, carry[...], reverse=False)
        fill()
        z = h_ref[...] * _gelu(g_ref[...].astype(F32))
        fill()
        _, _, y = _rms_fwd(z, gain_ref[...])
        y_ref[...] = y.astype(y_ref.dtype)

    row = lambda i: (i, 0)
    full = lambda i: (0, 0)
    vec = pl.BlockSpec((1, c), full)
    hb = _halo_rows(proj.dtype)
    return dict(body=body, grid=(t // tm,), fill_points=10,
                in_specs=[pl.BlockSpec((tm, c), row), pl.BlockSpec((hb, c), _halo_map(tm, 0, hb)),
                          pl.BlockSpec((tm, c), lambda i: (i, 1)),
                          pl.BlockSpec((LRU_CONV, c), full), vec, pl.BlockSpec((c, c), full), vec,
                          pl.BlockSpec((c, c), full), vec, vec, vec],
                out_specs=[pl.BlockSpec((tm, c), row), pl.BlockSpec((tm, c), row), pl.BlockSpec((tm, c), row)],
                out_shape=[jax.ShapeDtypeStruct((t, c), F32), jax.ShapeDtypeStruct((t, c), F32),
                           jax.ShapeDtypeStruct((t, c), MXU_DTYPE)],
                scratch_shapes=[pltpu.VMEM((tm, c), F32), pltpu.VMEM((tm, c), F32), pltpu.VMEM((SUBLANES, c), F32)],
                operands=[proj, proj, proj, conv_w, conv_b, wa, ba, wx, bx, lam, gain])


def _ret_consts():
    c = RET_CHUNK
    log_g = jnp.log1p(-jnp.exp2(-5.0 - jnp.arange(RET_HEADS, dtype=F32)))
    idx = jnp.arange(c, dtype=F32)
    diff = idx[:, None] - idx[None, :]
    decay = jnp.where(diff[None] >= 0, jnp.exp(jnp.maximum(diff, 0.0)[None] * log_g[:, None, None]), 0.0)
    zeta = jnp.exp((c - 1 - idx)[None, :] * log_g[:, None])
    xi = jnp.exp((idx + 1.0)[None, :] * log_g[:, None])
    spread = lambda v: jnp.repeat(v.T, RET_HEAD_DIM, axis=1)
    log_g_np = np.log1p(-np.exp2(-5.0 - np.arange(RET_HEADS, dtype=np.float32))).astype(np.float32)
    g_chunk = [float(np.exp(np.float32(c) * lg)) for lg in log_g_np]
    return decay, spread(xi), spread(zeta), g_chunk


def _rope_tables(t):
    pos = np.arange(t, dtype=np.float32)
    inv_freq = np.float32(ROPE_BASE) ** (-np.arange(0, RET_HEAD_DIM, 2, dtype=np.float32) / np.float32(RET_HEAD_DIM))
    ang = (pos[:, None] * inv_freq.astype(np.float32)[None, :]).astype(np.float32).astype(np.float64)
    cos, sin = np.cos(ang).astype(np.float32), np.sin(ang).astype(np.float32)
    return jnp.asarray(np.concatenate([cos, cos], axis=-1)), jnp.asarray(np.concatenate([-sin, sin], axis=-1))


def _rope(x, cos2, sin_signed):
    return x * cos2 + pltpu.roll(x, RET_HEAD_DIM // 2, 1) * sin_signed


def _rope_bwd(d, cos2, sin_signed):
    return d * cos2 + pltpu.roll(d * sin_signed, RET_HEAD_DIM // 2, 1)


RET_SCALE = RET_HEAD_DIM ** -0.5


RET_CHUNKS_PER_STEP = MIX_ROW_TILE // RET_CHUNK


def _ret_fwd(proj, cos2, sin_signed, gain):
    t = proj.shape[0]
    c, d, nh = RET_CHUNK, RET_HEAD_DIM, RET_HEADS
    n_chunks = t // c
    per = RET_CHUNKS_PER_STEP if n_chunks % RET_CHUNKS_PER_STEP == 0 else 1
    rows = per * c
    decay, xi, zeta, g_chunk = _ret_consts()

    def units(qk_ref, vg_ref, cos_ref, sin_ref, dec_ref, xi_ref, zeta_ref, gain_ref, o_ref, y_ref, st_ref, state, ctx):
        cur = [None] * nh

        def start():
            @pl.when(pl.program_id(0) == 0)
            def _():
                state[...] = jnp.zeros_like(state)
            for h in range(nh):
                cur[h] = state[h]

        def retain(s, h, keep):
            rs = slice(s * c, (s + 1) * c)
            cos2, sin_s = cos_ref[rs, :], sin_ref[rs, :]
            lo = h * d
            q = _rope(qk_ref[rs, lo:lo + d].astype(F32), cos2, sin_s)
            k = _rope(qk_ref[rs, D_RET + lo:D_RET + lo + d].astype(F32), cos2, sin_s) * RET_SCALE
            v = vg_ref[rs, lo:lo + d]
            s_prev = cur[h]
            st_ref[s, h] = s_prev
            scores = _dot_nt(q, k) * dec_ref[h]
            o = _dot(scores, v) + _dot(q * xi_ref[:, lo:lo + d], s_prev)
            cur[h] = s_prev * g_chunk[h] + _dot_tn(k * zeta_ref[:, lo:lo + d], v)
            o_ref[rs, lo:lo + d] = o
            keep["o"] = o

        def normalise(s, h, keep):
            rs = slice(s * c, (s + 1) * c)
            lo = h * d
            o = keep["o"]
            g = vg_ref[rs, D_RET + lo:D_RET + lo + d].astype(F32)
            mu = jnp.mean(o, axis=-1, keepdims=True)
            oc = o - mu
            on = oc * lax.rsqrt(jnp.mean(oc * oc, axis=-1, keepdims=True) + NORM_EPS)
            y_ref[rs, lo:lo + d] = (on * gain_ref[:, lo:lo + d] * (g * _sigmoid(g))).astype(y_ref.dtype)

        def end():
            for h in range(nh):
                state[h] = cur[h]

        pieces = [start]
        for s in range(per):
            for h in range(nh):
                keep = {}
                pieces += [lambda s=s, h=h, keep=keep: retain(s, h, keep),
                           lambda s=s, h=h, keep=keep: normalise(s, h, keep)]
        return pieces + [end]

    full2 = lambda i: (0, 0)
    return dict(units=units, grid=(n_chunks // per,),
                in_specs=[pl.BlockSpec((rows, 2 * D_RET), lambda i: (i, 1)),
                          pl.BlockSpec((rows, 2 * D_RET), lambda i: (i, 2)),
                          pl.BlockSpec((rows, d), lambda i: (i, 0)), pl.BlockSpec((rows, d), lambda i: (i, 0)),
                          pl.BlockSpec((nh, c, c), lambda i: (0, 0, 0)), pl.BlockSpec((c, D_RET), full2),
                          pl.BlockSpec((c, D_RET), full2), pl.BlockSpec((1, D_RET), full2)],
                out_specs=[pl.BlockSpec((rows, D_RET), lambda i: (i, 0)), pl.BlockSpec((rows, D_RET), lambda i: (i, 0)),
                           pl.BlockSpec((per, nh, d, d), lambda i: (i, 0, 0, 0))],
                out_shape=[jax.ShapeDtypeStruct((t, D_RET), F32), jax.ShapeDtypeStruct((t, D_RET), MXU_DTYPE),
                           jax.ShapeDtypeStruct((n_chunks, nh, d, d), F32)],
                scratch_shapes=[pltpu.VMEM((nh, d, d), F32)],
                operands=[proj, proj, cos2, sin_signed, decay, xi, zeta, gain])


def _outproj_fwd(x, y_lru, y_ret, w_out, g2, comm):
    t = x.shape[0]
    tm = min(PROJ_ROW_TILE, t)

    def body(x_ref, yl_ref, yr_ref, w_ref, g_ref, h1_ref, u2_ref):
        h1 = x_ref[...] + _dot(yl_ref[...], w_ref[:D_LRU, :]) + _dot(yr_ref[...], w_ref[D_LRU:, :])
        h1_ref[...] = h1
        _, _, u = _rms_fwd(h1, g_ref[...])
        u2_ref[...] = u.astype(u2_ref.dtype)

    row = lambda i: (i, 0)
    return _call(body, name="outproj_fwd", grid=(t // tm,), comm=comm,
                 in_specs=[pl.BlockSpec((tm, D_MODEL), row), pl.BlockSpec((tm, D_LRU), row), pl.BlockSpec((tm, D_RET), row),
                           _resident((D_MODEL, D_MODEL)), pl.BlockSpec((1, D_MODEL), lambda i: (0, 0))],
                 out_specs=[pl.BlockSpec((tm, D_MODEL), row), pl.BlockSpec((tm, D_MODEL), row)],
                 out_shape=[jax.ShapeDtypeStruct((t, D_MODEL), F32), jax.ShapeDtypeStruct((t, D_MODEL), MXU_DTYPE)],
                 operands=[x, y_lru, y_ret, w_out, g2])


FFN_TN = 768
FFN_NJ = D_FF // FFN_TN
FFN_GROUP = 4


def _ffn_fwd(u2, w_blocks, conv_w, conv_b, w_down, h1, gf, target):
    t = u2.shape[0]
    tm = min(ROW_TILE, t)
    tn, nj, group = FFN_TN, FFN_NJ, FFN_GROUP
    ng, tw = nj // group, group * tn
    hb = _halo_rows(u2.dtype)
    assert w_blocks.shape == (2 * nj, D_MODEL, tn)

    def conv(ext, col, up_ref, conv_ref, cw_ref, cb_ref, first):
        x = ext[hb:, :]
        up_ref[:, col] = x.astype(up_ref.dtype)
        prev = jnp.where(first, 0.0, ext[hb - SUBLANES:hb, :])
        cw = cw_ref[:, col]
        y = cb_ref[:, col] + cw[FFN_CONV - 1:FFN_CONV, :] * x
        for k in range(FFN_CONV - 1):
            y = y + cw[k:k + 1, :] * _shift_down(prev, x, FFN_CONV - 1 - k)
        conv_ref[:, col] = y.astype(conv_ref.dtype)
        return y

    def body(u_ref, uh_ref, w_ref, cwa_ref, cwv_ref, cba_ref, cbv_ref, wd_ref, h1_ref, gf_ref, tg_ref,
             upa_ref, upv_ref, ca_ref, cv_ref, act_ref, dh_ref, dhb_ref, dgf_ref, loss_ref, acc):
        i, jg = pl.program_id(0), pl.program_id(1)

        @pl.when((i == 0) & (jg == 0))
        def _():
            dgf_ref[...] = jnp.zeros_like(dgf_ref)
            loss_ref[...] = jnp.zeros_like(loss_ref)

        @pl.when(jg == 0)
        def _():
            acc[...] = jnp.zeros_like(acc)

        u_ext = jnp.concatenate([uh_ref[...], u_ref[...]], axis=0)

        def project(jj):
            j = jg * group + jj
            return _dot(u_ext, w_ref[j]), _dot(u_ext, w_ref[nj + j])

        down, ahead = None, project(0)
        for jj in range(group):
            col = slice(jj * tn, (jj + 1) * tn)
            j = jg * group + jj
            ext_a, ext_v = ahead
            if jj + 1 < group:
                ahead = project(jj + 1)
            a = conv(ext_a, col, upa_ref, ca_ref, cwa_ref, cba_ref, i == 0)
            v = conv(ext_v, col, upv_ref, cv_ref, cwv_ref, cbv_ref, i == 0)
            act = (_gelu(a) * v).astype(act_ref.dtype)
            act_ref[:, col] = act
            part = _dot(act, wd_ref[pl.ds(pl.multiple_of(j * tn, tn), tn), :])
            down = part if down is None else down + part
        acc[...] += down

        @pl.when(jg == ng - 1)
        def _():
            n, rstd, y = _rms_fwd(h1_ref[...] + acc[...], gf_ref[...])
            err = y - tg_ref[...]
            loss_ref[...] += (0.5 / D_MODEL) * jnp.sum(err * err)
            dh, dgf = _rms_bwd(err * (1.0 / D_MODEL), n, rstd, gf_ref[...])
            dgf_ref[...] += dgf
            dh_ref[...] = dh
            dhb_ref[...] = dh.astype(dhb_ref.dtype)

    per = tm // hb
    row = lambda i, j: (i, 0)
    const = lambda i, j: (0, 0)
    tile = pl.BlockSpec((tm, tw), lambda i, j: (i, j))
    return _call(body, name="ffn_fwd", grid=(t // tm, ng), vmem_limit=FUSED_VMEM_LIMIT,
                 in_specs=[pl.BlockSpec((tm, D_MODEL), row),
                           pl.BlockSpec((hb, D_MODEL), lambda i, j: (jnp.maximum(i * per - 1, 0), 0)),
                           _resident(w_blocks.shape),
                           pl.BlockSpec((FFN_CONV, tw), lambda i, j: (0, j)),
                           pl.BlockSpec((FFN_CONV, tw), lambda i, j: (0, j + ng)),
                           pl.BlockSpec((1, tw), lambda i, j: (0, j)), pl.BlockSpec((1, tw), lambda i, j: (0, j + ng)),
                           _resident((D_FF, D_MODEL)),
                           pl.BlockSpec((tm, D_MODEL), row), pl.BlockSpec((1, D_MODEL), const),
                           pl.BlockSpec((tm, D_MODEL), row)],
                 out_specs=[tile] * 5 + [pl.BlockSpec((tm, D_MODEL), row),
                            pl.BlockSpec((tm, D_MODEL), row), pl.BlockSpec((SUBLANES, D_MODEL), const),
                            pl.BlockSpec((SUBLANES, LANES), const)],
                 out_shape=[jax.ShapeDtypeStruct((t, D_FF), MXU_DTYPE)] * 5 + [
                            jax.ShapeDtypeStruct((t, D_MODEL), F32),
                            jax.ShapeDtypeStruct((t, D_MODEL), MXU_DTYPE), jax.ShapeDtypeStruct((SUBLANES, D_MODEL), F32),
                            jax.ShapeDtypeStruct((SUBLANES, LANES), F32)],
                 scratch_shapes=[pltpu.VMEM((tm, D_MODEL), F32)],
                 operands=[u2, u2, w_blocks, conv_w, conv_w, conv_b, conv_b, w_down, h1, gf, target])


FFN_ACC_ROWS = SUBLANES * (FFN_CONV + 1)


def _ffn_bwd(dh2, dh2_b, w_down, up_a, up_v, conv_a, conv_v, conv_w, w_up_blocks, h1, g2, comm):
    t = up_a.shape[0]
    tm = min(ROW_TILE, t)
    tn, nj, group = FFN_TN, FFN_NJ, FFN_GROUP
    ng, tw = nj // group, group * tn
    ni = t // tm
    assert w_up_blocks.shape == (2 * nj, D_MODEL, tn)

    def conv_bwd(dy, x, cw, acc_ref, carry_ref, dup_ref, col):
        nxt = carry_ref[...]
        carry_ref[...] = dy[:SUBLANES, :]
        ahead = [_shift_up(dy, nxt, FFN_CONV - 1 - k) for k in range(FFN_CONV)]
        dx = cw[FFN_CONV - 1:FFN_CONV, :] * dy
        for k in range(FFN_CONV - 1):
            dx = dx + cw[k:k + 1, :] * ahead[k]
        dx = dx.astype(dup_ref.dtype)
        dup_ref[:, col] = dx
        for k in range(FFN_CONV):
            acc_ref[k * SUBLANES:(k + 1) * SUBLANES, :] += _colsum8(ahead[k] * x)
        acc_ref[FFN_CONV * SUBLANES:, :] += _colsum8(dy)
        return dx

    def body(dh_ref, dhb_ref, wd_ref, ua_ref, uv_ref, ca_ref, cv_ref, cwa_ref, cwv_ref, wu_ref, h1_ref, g2_ref,
             dua_ref, duv_ref, acca_ref, accv_ref, dh1_ref, dh1b_ref, dg2_ref, carry_a, carry_v, du):
        i, jg = pl.program_id(0), pl.program_id(1)

        @pl.when((i == 0) & (jg == 0))
        def _():
            for ref in (acca_ref, accv_ref, carry_a, carry_v, dg2_ref):
                ref[...] = jnp.zeros_like(ref)

        dhb = dhb_ref[...]

        def through_down(jj):
            j = jg * group + jj
            return _dot_nt(dhb, wd_ref[pl.ds(pl.multiple_of(j * tn, tn), tn), :])

        part, ahead = None, through_down(0)
        for jj in range(group):
            col = slice(jj * tn, (jj + 1) * tn)
            j = jg * group + jj
            dact = ahead
            if jj + 1 < group:
                ahead = through_down(jj + 1)
            v = cv_ref[:, col].astype(F32)
            g, dg = _gelu_parts(ca_ref[:, col].astype(F32))
            da = conv_bwd(dact * v * dg, ua_ref[:, col].astype(F32), cwa_ref[:, col], acca_ref.at[j], carry_a.at[j],
                          dua_ref, col)
            dv = conv_bwd(dact * g, uv_ref[:, col].astype(F32), cwv_ref[:, col], accv_ref.at[j], carry_v.at[j],
                          duv_ref, col)
            term = _dot_nt(da, wu_ref[j]) + _dot_nt(dv, wu_ref[nj + j])
            part = term if part is None else part + term

        @pl.when(jg == 0)
        def _():
            du[...] = part

        @pl.when(jg > 0)
        def _():
            du[...] += part

        @pl.when(jg == ng - 1)
        def _():
            n, rstd, _ = _rms_fwd(h1_ref[...], g2_ref[...])
            dh1, dg2 = _rms_bwd(du[...], n, rstd, g2_ref[...])
            dh1 = dh1 + dh_ref[...]
            dg2_ref[...] += dg2
            dh1_ref[...] = dh1
            dh1b_ref[...] = dh1.astype(dh1b_ref.dtype)

    row = lambda i, j: (ni - 1 - i, 0)
    const = lambda i, j: (0, 0)
    tile = pl.BlockSpec((tm, tw), lambda i, j: (ni - 1 - i, j))
    acc = pl.BlockSpec((nj, FFN_ACC_ROWS, tn), lambda i, j: (0, 0, 0))
    return _call(body, name="ffn_bwd", grid=(ni, ng), comm=comm, vmem_limit=FFN_BWD_VMEM_LIMIT,
                 in_specs=[pl.BlockSpec((tm, D_MODEL), row), pl.BlockSpec((tm, D_MODEL), row),
                           _resident((D_FF, D_MODEL)), tile, tile, tile, tile,
                           pl.BlockSpec((FFN_CONV, tw), lambda i, j: (0, j)),
                           pl.BlockSpec((FFN_CONV, tw), lambda i, j: (0, j + ng)),
                           _resident(w_up_blocks.shape), pl.BlockSpec((tm, D_MODEL), row),
                           pl.BlockSpec((1, D_MODEL), const)],
                 out_specs=[tile, tile, acc, acc, pl.BlockSpec((tm, D_MODEL), row), pl.BlockSpec((tm, D_MODEL), row),
                            pl.BlockSpec((SUBLANES, D_MODEL), const)],
                 out_shape=[jax.ShapeDtypeStruct((t, D_FF), MXU_DTYPE), jax.ShapeDtypeStruct((t, D_FF), MXU_DTYPE),
                            jax.ShapeDtypeStruct((nj, FFN_ACC_ROWS, tn), F32),
                            jax.ShapeDtypeStruct((nj, FFN_ACC_ROWS, tn), F32),
                            jax.ShapeDtypeStruct((t, D_MODEL), F32), jax.ShapeDtypeStruct((t, D_MODEL), MXU_DTYPE),
                            jax.ShapeDtypeStruct((SUBLANES, D_MODEL), F32)],
                 scratch_shapes=[pltpu.VMEM((nj, SUBLANES, tn), F32), pltpu.VMEM((nj, SUBLANES, tn), F32),
                                 pltpu.VMEM((tm, D_MODEL), F32)],
                 operands=[dh2, dh2_b, w_down, up_a, up_v, conv_a, conv_v, conv_w, conv_w, w_up_blocks, h1, g2])


def _ret_bwd(proj, cos2, sin_signed, gain, o, states, dmix_at):
    t = proj.shape[0]
    c, d, nh = RET_CHUNK, RET_HEAD_DIM, RET_HEADS
    n_chunks = t // c
    per = RET_CHUNKS_PER_STEP if n_chunks % RET_CHUNKS_PER_STEP == 0 else 1
    rows = per * c
    n_steps = n_chunks // per
    decay, xi, zeta, g_chunk = _ret_consts()
    base = 2 * D_LRU

    def units(qk_ref, vg_ref, cos_ref, sin_ref, dec_ref, xi_ref, zeta_ref, gain_ref, o_ref, st_ref,
              dp_ref, dgain_ref, gstate, ctx):
        cur = [None] * nh
        dmix = ctx["scratch"][dmix_at[0]][dmix_at[1]]

        def start():
            @pl.when(pl.program_id(0) == 0)
            def _():
                gstate[...] = jnp.zeros_like(gstate)
                dgain_ref[...] = jnp.zeros_like(dgain_ref)
            for h in range(nh):
                cur[h] = gstate[h]

        def gate_and_norm(s, h, keep):
            rs = slice(s * c, (s + 1) * c)
            lo = h * d
            g = vg_ref[rs, D_RET + lo:D_RET + lo + d].astype(F32)
            gain_h = gain_ref[:, lo:lo + d]
            dy = dmix[rs, D_LRU + lo:D_LRU + lo + d]
            sg = _sigmoid(g)
            o_h = o_ref[rs, lo:lo + d]
            oc = o_h - jnp.mean(o_h, axis=-1, keepdims=True)
            rstd = lax.rsqrt(jnp.mean(oc * oc, axis=-1, keepdims=True) + NORM_EPS)
            on = oc * rstd
            at = base + 3 * D_RET + lo
            dp_ref[rs, at:at + d] = (dy * on * gain_h * (sg * (1.0 + g * (1.0 - sg)))).astype(dp_ref.dtype)
            don_g = dy * (g * sg)
            dgain_ref[:, lo:lo + d] += _colsum8(don_g * on)
            don = don_g * gain_h
            keep["do"] = rstd * (don - jnp.mean(don, axis=-1, keepdims=True)
                                 - on * jnp.mean(don * on, axis=-1, keepdims=True))

        def retain(s, h, keep):
            rs = slice(s * c, (s + 1) * c)
            cos2, sin_s = cos_ref[rs, :], sin_ref[rs, :]
            lo = h * d
            q = _rope(qk_ref[rs, lo:lo + d].astype(F32), cos2, sin_s)
            k = _rope(qk_ref[rs, D_RET + lo:D_RET + lo + d].astype(F32), cos2, sin_s) * RET_SCALE
            v = vg_ref[rs, lo:lo + d]
            xi_h, zeta_h, dec = xi_ref[:, lo:lo + d], zeta_ref[:, lo:lo + d], dec_ref[h]
            do = keep["do"]
            s_prev = st_ref[s, h]
            g_next = cur[h]
            p = _dot_nt(q, k) * dec
            dpm = _dot_nt(do, v) * dec
            keep["dq"] = _dot(dpm, k) + _dot_nt(do, s_prev) * xi_h
            keep["dk"] = _dot_tn(dpm, q) + _dot_nt(v, g_next) * zeta_h
            dv = _dot_tn(p, do) + _dot(k * zeta_h, g_next)
            cur[h] = g_next * g_chunk[h] + _dot_tn(q * xi_h, do)
            at = base + 2 * D_RET + lo
            dp_ref[rs, at:at + d] = dv.astype(dp_ref.dtype)

        def unrope(s, h, keep):
            rs = slice(s * c, (s + 1) * c)
            cos2, sin_s = cos_ref[rs, :], sin_ref[rs, :]
            lo = h * d
            dp_ref[rs, base + lo:base + lo + d] = _rope_bwd(keep["dq"], cos2, sin_s).astype(dp_ref.dtype)
            at = base + D_RET + lo
            dp_ref[rs, at:at + d] = _rope_bwd(keep["dk"] * RET_SCALE, cos2, sin_s).astype(dp_ref.dtype)

        def end():
            for h in range(nh):
                gstate[h] = cur[h]

        pieces = [start]
        for s in reversed(range(per)):
            for h in range(nh):
                keep = {}
                pieces += [lambda s=s, h=h, keep=keep, f=f: f(s, h, keep) for f in (gate_and_norm, retain, unrope)]
        return pieces + [end]

    rev = lambda col: (lambda i: (n_steps - 1 - i, col))
    full2 = lambda i: (0, 0)
    return dict(units=units, grid=(n_steps,),
                in_specs=[pl.BlockSpec((rows, 2 * D_RET), rev(1)), pl.BlockSpec((rows, 2 * D_RET), rev(2)),
                          pl.BlockSpec((rows, d), rev(0)), pl.BlockSpec((rows, d), rev(0)),
                          pl.BlockSpec((nh, c, c), lambda i: (0, 0, 0)), pl.BlockSpec((c, D_RET), full2),
                          pl.BlockSpec((c, D_RET), full2), pl.BlockSpec((1, D_RET), full2),
                          pl.BlockSpec((rows, D_RET), rev(0)),
                          pl.BlockSpec((per, nh, d, d), lambda i: (n_steps - 1 - i, 0, 0, 0))],
                out_specs=[pl.BlockSpec((rows, D_IN), rev(0)), pl.BlockSpec((SUBLANES, D_RET), full2)],
                out_shape=[jax.ShapeDtypeStruct((t, D_IN), MXU_DTYPE), jax.ShapeDtypeStruct((SUBLANES, D_RET), F32)],
                scratch_shapes=[pltpu.VMEM((nh, d, d), F32)],
                operands=[proj, proj, cos2, sin_signed, decay, xi, zeta, gain, o, states])


LRU_ACC = {"conv_w": 0, "conv_b": LRU_CONV, "gate_a_b": LRU_CONV + 1, "gate_x_b": LRU_CONV + 2,
           "lambda": LRU_CONV + 3, "norm_gain": LRU_CONV + 4}
LRU_ACC_ROWS = SUBLANES * (LRU_CONV + 5)


def _lru_bwd(proj, xc_all, h_all, conv_w, wa, ba, wx, bx, lam, gain, dproj_part, dmix_at):
    t = proj.shape[0]
    tm = min(MIX_ROW_TILE, t)
    c = D_LRU
    ni = t // tm

    def body(x_ref, xh_ref, g_ref, xc_ref, h_ref, hh_ref, cw_ref, wa_ref, ba_ref, wx_ref, bx_ref, lam_ref,
             gain_ref, acc_ref, dwa_ref, dwx_ref, a_scr, b_scr, mu_scr, carry_mu, carry_dxc, ctx):
        dp_ref = ctx["outs"][dproj_part][0]
        dmix = ctx["scratch"][dmix_at[0]][dmix_at[1]]
        fill = ctx["fill"]
        i = pl.program_id(0)
        r = ni - 1 - i

        @pl.when(i == 0)
        def _():
            acc_ref[...] = jnp.zeros_like(acc_ref)
            dwa_ref[...] = jnp.zeros_like(dwa_ref)
            dwx_ref[...] = jnp.zeros_like(dwx_ref)
            carry_mu[...] = jnp.zeros_like(carry_mu)
            carry_dxc[...] = jnp.zeros_like(carry_dxc)

        def add(name, val, k=0):
            lo = (LRU_ACC[name] + k) * SUBLANES
            acc_ref[lo:lo + SUBLANES, :] += _colsum8(val)

        fill()
        xc, h = xc_ref[...], h_ref[...]
        lam_v = lam_ref[...]
        sp = _softplus(-lam_v)
        rg, ig, a, m = _lru_gates(xc, wa_ref[...], ba_ref[...], wx_ref[...], bx_ref[...], sp, fill)
        gl, dgl = _gelu_parts(g_ref[...].astype(F32))
        fill()
        zn, rstd, _ = _rms_fwd(h * gl, gain_ref[...])
        dy = dmix[:, :c]
        dz, dgain = _rms_bwd(dy, zn, rstd, gain_ref[...])
        lo = LRU_ACC["norm_gain"] * SUBLANES
        acc_ref[lo:lo + SUBLANES, :] += dgain
        dp_ref[:, c:2 * c] = (dz * h * dgl).astype(dp_ref.dtype)
        dh = dz * gl
        fill()
        ga, gb = _group_scan(a, a * dh, reverse=True, fill=fill)
        a_scr[...] = ga
        b_scr[...] = gb
        mu_next_tile = carry_mu[...]
        carry_mu[...] = _carry_scan(a_scr, b_scr, mu_scr, mu_next_tile, reverse=True)
        fill()
        lam_t = dh + _shift_up(mu_scr[...], mu_next_tile, 1)
        h_prev = _shift_down(jnp.where(r == 0, 0.0, hh_ref[...]), h, 1)
        da = lam_t * h_prev
        dig = lam_t * m * xc
        dxc = lam_t * m * ig
        dlog_a = da * a - (lam_t * ig * xc) * (a * a) / m
        fill()
        dpr = dlog_a * ((-LRU_C) * sp) * rg * (1.0 - rg)
        add("lambda", dlog_a * ((-LRU_C) * rg) * (-_sigmoid(-lam_v)))
        dpi = dig * ig * (1.0 - ig)
        add("gate_a_b", dpr)
        add("gate_x_b", dpi)
        fill()
        dwa_ref[...] += _dot_tn(xc, dpr)
        dwx_ref[...] += _dot_tn(xc, dpi)
        dxc = dxc + _dot_nt(dpr, wa_ref[...]) + _dot_nt(dpi, wx_ref[...])
        fill()
        add("conv_b", dxc)
        x = x_ref[...].astype(F32)
        prev = jnp.where(r == 0, 0.0, xh_ref[...].astype(F32)[-SUBLANES:, :])
        cw = cw_ref[...]
        nxt = carry_dxc[...]
        carry_dxc[...] = dxc[:SUBLANES, :]
        dx = cw[LRU_CONV - 1:LRU_CONV, :] * dxc
        for k in range(LRU_CONV - 1):
            dx = dx + cw[k:k + 1, :] * _shift_up(dxc, nxt, LRU_CONV - 1 - k)
        fill()
        for k in range(LRU_CONV):
            add("conv_w", dxc * _shift_down(prev, x, LRU_CONV - 1 - k), k)
        dp_ref[:, :c] = dx.astype(dp_ref.dtype)

    hb = _halo_rows(proj.dtype)
    rev = lambda col: (lambda i: (ni - 1 - i, col))
    halo = lambda rows: (lambda i: (jnp.maximum((ni - 1 - i) * (tm // rows) - 1, 0), 0))
    full = lambda i: (0, 0)
    vec = pl.BlockSpec((1, c), full)
    mat = pl.BlockSpec((c, c), full)
    return dict(body=body, grid=(ni,), fill_points=12,
                in_specs=[pl.BlockSpec((tm, c), rev(0)), pl.BlockSpec((hb, c), halo(hb)), pl.BlockSpec((tm, c), rev(1)),
                          pl.BlockSpec((tm, c), rev(0)), pl.BlockSpec((tm, c), rev(0)),
                          pl.BlockSpec((SUBLANES, c), halo(SUBLANES)),
                          pl.BlockSpec((LRU_CONV, c), full), mat, vec, mat, vec, vec, vec],
                out_specs=[pl.BlockSpec((LRU_ACC_ROWS, c), full), mat, mat],
                out_shape=[jax.ShapeDtypeStruct((LRU_ACC_ROWS, c), F32), jax.ShapeDtypeStruct((c, c), F32),
                           jax.ShapeDtypeStruct((c, c), F32)],
                scratch_shapes=[pltpu.VMEM((tm, c), F32), pltpu.VMEM((tm, c), F32), pltpu.VMEM((tm, c), F32),
                                pltpu.VMEM((SUBLANES, c), F32), pltpu.VMEM((SUBLANES, c), F32)],
                operands=[proj, proj, proj, xc_all, h_all, h_all, conv_w, wa, ba, wx, bx, lam, gain])


def _mix_proj_bwd(dh1, dh1_b, w_out, w_in_blocks, x, g1, dproj_part):
    t = x.shape[0]
    tm = min(MIX_ROW_TILE, t)
    ni = t // tm
    nb, _, cb = w_in_blocks.shape
    first_free = -(-2 * D_LRU // cb)
    du = [None]

    def term(dp_ref, w_ref, d):
        part = _dot_nt(dp_ref[:, d * cb:(d + 1) * cb], w_ref[d])
        du[0] = part if du[0] is None else du[0] + part

    def head(dh_ref, dhb_ref, wo_ref, wi_ref, x_ref, g_ref, gx_ref, dg_ref, dmix, ctx):
        @pl.when(pl.program_id(0) == 0)
        def _():
            dg_ref[...] = jnp.zeros_like(dg_ref)
        dmix[...] = _dot_nt(dhb_ref[...], wo_ref[...])
        du[0] = None

    def units(dh_ref, dhb_ref, wo_ref, wi_ref, x_ref, g_ref, gx_ref, dg_ref, dmix, ctx):
        dp_ref = ctx["outs"][dproj_part][0]
        return [lambda d=d: term(dp_ref, wi_ref, d) for d in range(first_free, nb)]

    def tail(dh_ref, dhb_ref, wo_ref, wi_ref, x_ref, g_ref, gx_ref, dg_ref, dmix, ctx):
        dp_ref = ctx["outs"][dproj_part][0]
        for d in range(first_free):
            term(dp_ref, wi_ref, d)
        n, rstd, _ = _rms_fwd(x_ref[...], g_ref[...])
        dx, dg = _rms_bwd(du[0], n, rstd, g_ref[...])
        dg_ref[...] += dg
        gx_ref[...] = dx + dh_ref[...]

    row = lambda i: (ni - 1 - i, 0)
    const = lambda i: (0, 0)
    tile = pl.BlockSpec((tm, D_MODEL), row)
    return dict(head=head, units=units, tail=tail, grid=(ni,),
                in_specs=[tile, tile, _resident(w_out.shape), _resident(w_in_blocks.shape), tile,
                          pl.BlockSpec((1, D_MODEL), const)],
                out_specs=[tile, pl.BlockSpec((SUBLANES, D_MODEL), const)],
                out_shape=[jax.ShapeDtypeStruct((t, D_MODEL), F32), jax.ShapeDtypeStruct((SUBLANES, D_MODEL), F32)],
                scratch_shapes=[pltpu.VMEM((tm, D_MODEL), F32)],
                operands=[dh1, dh1_b, w_out, w_in_blocks, x, g1])


def _pair_sum(core, a, b, name):
    n, r, c = b.shape
    spec = pl.BlockSpec((None, r, c), lambda q, core: (q, 0, 0))

    def body(core_ref, a_ref, b_ref, o_ref):
        o_ref[...] = (a_ref[...].astype(F32) + b_ref[...].astype(F32)).astype(o_ref.dtype)

    return pl.pallas_call(
        body, name=name,
        grid_spec=pltpu.PrefetchScalarGridSpec(
            num_scalar_prefetch=1, grid=(n,),
            in_specs=[pl.BlockSpec((None, r, c), lambda q, core: (2 * q + core[0], 0, 0)), spec], out_specs=spec),
        out_shape=jax.ShapeDtypeStruct(b.shape, b.dtype),
        compiler_params=pltpu.CompilerParams(dimension_semantics=("arbitrary",), vmem_limit_bytes=VMEM_LIMIT),
    )(core, a, b)


ADAMW_BLOCK_BYTES = 4 * 1024 * 1024


def _sum_adamw(parts, w, m, v, name):
    n_parts, r, c = parts.shape
    tr = r
    while n_parts * tr * c * parts.dtype.itemsize > ADAMW_BLOCK_BYTES and tr % (4 * SUBLANES) == 0:
        tr //= 2

    def body(p_ref, w_ref, m_ref, v_ref, g_ref, d_ref, nm_ref, nv_ref):
        g = p_ref[0].astype(F32)
        for s in range(1, n_parts):
            g = g + p_ref[s].astype(F32)
        nm = ADAM_B1 * m_ref[...] + (1.0 - ADAM_B1) * g
        nv = ADAM_B2 * v_ref[...] + (1.0 - ADAM_B2) * (g * g)
        m_hat = nm / (1.0 - ADAM_B1 ** ADAM_STEP)
        v_hat = nv / (1.0 - ADAM_B2 ** ADAM_STEP)
        g_ref[...] = g
        d_ref[...] = -ADAM_LR * (m_hat / (jnp.sqrt(v_hat) + ADAM_EPS) + ADAM_WD * w_ref[...])
        nm_ref[...] = nm
        nv_ref[...] = nv

    row = pl.BlockSpec((tr, c), lambda i: (i, 0))
    return _call(body, name=name, grid=(r // tr,),
                 in_specs=[pl.BlockSpec((n_parts, tr, c), lambda i: (0, i, 0)), row, row, row],
                 out_specs=[row, row, row, row], out_shape=[jax.ShapeDtypeStruct((r, c), F32)] * 4,
                 operands=[parts, w, m, v])


MATRICES = ("w_in", "w_out", "ffn_up_w", "ffn_down_w")
CONVS = ("lru_conv_w", "ffn_conv_w")
REPLICATED = ("norm1_gain", "lru_conv_b", "lru_gate_a_w", "lru_gate_a_b", "lru_gate_x_w", "lru_gate_x_b", "lru_lambda",
              "lru_norm_gain", "ret_norm_gain", "norm2_gain", "ffn_conv_b", "final_norm_gain")
WEIGHTS = ("norm1_gain", "w_in", "lru_conv_w", "lru_conv_b", "lru_gate_a_w", "lru_gate_a_b", "lru_gate_x_w",
           "lru_gate_x_b", "lru_lambda", "lru_norm_gain", "ret_norm_gain", "w_out", "norm2_gain", "ffn_up_w",
           "ffn_conv_w", "ffn_conv_b", "ffn_down_w", "final_norm_gain")


def _rows(a, pad_to):
    a = a.reshape(-1, LANES)
    pad = (-a.shape[0]) % pad_to
    return jnp.pad(a, ((0, pad), (0, 0))) if pad else a


def _pack(arrays, pad_to):
    rows, layout, at = [], [], 0
    for a in arrays:
        r = _rows(a, pad_to)
        layout.append((at, a.size // LANES, a.shape))
        rows.append(r)
        at += r.shape[0]
    return jnp.concatenate(rows, axis=0), layout


def _unpack(packed, layout):
    lead = packed.shape[:-2]
    return [packed[..., at:at + n, :].reshape(lead + shape) for at, n, shape in layout]


def _conv_rows(lru, ffn, dtype, pad_to):
    lead = lru.shape[:-2]
    flat = jnp.concatenate([lru.reshape(lead + (-1,)), ffn.reshape(lead + (-1,))], axis=-1).astype(dtype)
    rows = flat.shape[-1] // LANES
    pad = (-rows) % pad_to
    return jnp.pad(flat.reshape(lead + (rows, LANES)), [(0, 0)] * len(lead) + [(0, pad), (0, 0)])


def _column_blocks(full):
    r, c = full.shape
    return full.reshape(r, N_DEV, c // N_DEV).transpose(1, 0, 2)


def _block_diag(w):
    nh, d, _ = w.shape
    eye = jnp.eye(nh, dtype=w.dtype)
    return (w[:, :, None, :] * eye[:, None, :, None]).reshape(nh * d, nh * d)


def _diag_blocks(dense, nh):
    d = dense.shape[0] // nh
    blocks = dense.reshape(nh, d, nh, d)
    return jnp.stack([blocks[h, :, h, :] for h in range(nh)], axis=0)


def kernel(x, norm1_gain, w_in, lru_conv_w, lru_conv_b, lru_gate_a_w, lru_gate_a_b, lru_gate_x_w, lru_gate_x_b, lru_lambda, lru_norm_gain, ret_norm_gain, w_out, norm2_gain, ffn_up_w, ffn_conv_w, ffn_conv_b, ffn_down_w, final_norm_gain, loss_target, m_norm1_gain, m_w_in, m_lru_conv_w, m_lru_conv_b, m_lru_gate_a_w, m_lru_gate_a_b, m_lru_gate_x_w, m_lru_gate_x_b, m_lru_lambda, m_lru_norm_gain, m_ret_norm_gain, m_w_out, m_norm2_gain, m_ffn_up_w, m_ffn_conv_w, m_ffn_conv_b, m_ffn_down_w, m_final_norm_gain, v_norm1_gain, v_w_in, v_lru_conv_w, v_lru_conv_b, v_lru_gate_a_w, v_lru_gate_a_b, v_lru_gate_x_w, v_lru_gate_x_b, v_lru_lambda, v_lru_norm_gain, v_ret_norm_gain, v_w_out, v_norm2_gain, v_ffn_up_w, v_ffn_conv_w, v_ffn_conv_b, v_ffn_down_w, v_final_norm_gain):
    args = dict(locals())
    given = {n: args[n] for n in WEIGHTS}
    out_shape = {n: given[n].shape for n in WEIGHTS}

    def plain(a):
        return a.reshape(1, -1) if a.ndim <= 2 else a[0]

    w = {n: plain(given[n]) for n in WEIGHTS}
    mom_m = {n: plain(args["m_" + n]) for n in WEIGHTS}
    mom_v = {n: plain(args["v_" + n]) for n in WEIGHTS}
    x2, target = x[0], loss_target[0]
    t = x2.shape[0]
    core = lax.axis_index("c").astype(jnp.int32).reshape(1)
    res = {}

    conv_pad = _conv_rows(w["lru_conv_w"], w["ffn_conv_w"], F32, SUBLANES)
    first = _gather_first([w["w_in"].astype(MXU_DTYPE), conv_pad])
    w_in_blocks, conv_all = _run_comms([first, _gather_second(first.out_shape)], "w_in_all_gather")
    n_lru = w["lru_conv_w"].size
    conv_flat = conv_all.reshape(N_DEV, -1)
    lru_cw = conv_flat[:, :n_lru].reshape((N_DEV,) + w["lru_conv_w"].shape).transpose(1, 0, 2).reshape(LRU_CONV, D_LRU)
    ffn_cw = conv_flat[:, n_lru:n_lru + w["ffn_conv_w"].size].reshape((N_DEV,) + w["ffn_conv_w"].shape)
    ffn_cw = ffn_cw.transpose(1, 0, 2).reshape(FFN_CONV, 2 * D_FF)

    cos2, sin_signed = _rope_tables(t)
    wa = _block_diag(w["lru_gate_a_w"]).astype(MXU_DTYPE)
    wx = _block_diag(w["lru_gate_x_w"]).astype(MXU_DTYPE)
    gf = w["final_norm_gain"]

    early = _gather_first([w["w_out"].astype(MXU_DTYPE), w["ffn_down_w"].astype(MXU_DTYPE)])
    (u1, proj), (w_out_part, down_part) = _inproj_fwd(x2, w["norm1_gain"], w_in_blocks, early)
    ((xc, h_lru, y_lru), (o_ret, y_ret, states)), (w_out_blocks, down_blocks, up_part) = _fused(
        [_lru_fwd(proj, lru_cw, w["lru_conv_b"], wa, w["lru_gate_a_b"], wx, w["lru_gate_x_b"], w["lru_lambda"],
                  w["lru_norm_gain"]),
         _ret_fwd(proj, cos2, sin_signed, w["ret_norm_gain"])],
        "mix_fwd", _both(_gather_second([w_out_part, down_part]), _gather_first([w["ffn_up_w"].astype(MXU_DTYPE)])))
    w_out_full = w_out_blocks.reshape(D_MODEL, D_MODEL)
    w_down_full = down_blocks.reshape(D_FF, D_MODEL)

    (h1, u2), (up_blocks,) = _outproj_fwd(x2, y_lru, y_ret, w_out_full, w["norm2_gain"], _gather_second([up_part]))
    up_a, up_v, conv_a, conv_v, act, dh2, dh2_b, dgf, loss_local = _ffn_fwd(u2, up_blocks, ffn_cw, w["ffn_conv_b"],
                                                                            w_down_full, h1, gf, target)

    def to_owner_chips(blocks, names, tag):
        theirs = _run_comms([_pair_exchange(blocks)], "grads_pair_exchange_" + tag)
        return [_pair_sum(core, a, b, "grads_pair_sum_" + n) for n, a, b in zip(names, blocks, theirs)]

    def adamw(name, parts):
        res[name] = _sum_adamw(parts, w[name], mom_m[name], mom_v[name], "adamw_" + name)

    g = {"final_norm_gain": dgf[0]}
    dup_a, dup_v, acc_a, acc_v, dh1, dh1_b, dg2 = _ffn_bwd(
        dh2, dh2_b, w_down_full, up_a, up_v, conv_a, conv_v, ffn_cw, up_blocks, h1, w["norm2_gain"], None)
    per_col = lambda a: a[:, ::SUBLANES].transpose(1, 0, 2).reshape(FFN_CONV + 1, D_FF)
    acc = jnp.concatenate([per_col(acc_a), per_col(acc_v)], axis=1)
    g_ffn_cw, g["ffn_conv_b"] = acc[:FFN_CONV], acc[FFN_CONV:]
    g["norm2_gain"] = dg2[:1]
    g_up = jnp.concatenate([_mm_tn(u2, dup_a, "ffn_up_wgrad_a", blocks=N_DEV // 2),
                            _mm_tn(u2, dup_v, "ffn_up_wgrad_v", blocks=N_DEV // 2)], axis=0)
    up_sums = to_owner_chips([g_up], ["ffn_up_w"], "up")
    g_down, (up_parts,) = _mm_tn(act, dh2_b, "ffn_down_wgrad", comm=_chip_exchange(up_sums))
    adamw("ffn_up_w", up_parts)
    g_out = jnp.concatenate([_mm_tn(y_lru, dh1_b, "w_out_wgrad_lru"), _mm_tn(y_ret, dh1_b, "w_out_wgrad_ret")], axis=0)
    low_sums = to_owner_chips([g_down.reshape(N_DEV, D_FF // N_DEV, D_MODEL),
                               g_out.reshape(N_DEV, D_MODEL // N_DEV, D_MODEL)], ["ffn_down_w", "w_out"], "low")
    (dproj, dgain_ret), (grad_x, dg1), (lru_acc, dwa, dwx) = _fused(
        [_ret_bwd(proj, cos2, sin_signed, w["ret_norm_gain"], o_ret, states, dmix_at=(1, 0)),
         _mix_proj_bwd(dh1, dh1_b, w_out_full, w_in_blocks, x2, w["norm1_gain"], dproj_part=0),
         _lru_bwd(proj, xc, h_lru, lru_cw, wa, w["lru_gate_a_b"], wx, w["lru_gate_x_b"], w["lru_lambda"],
                  w["lru_norm_gain"], dproj_part=0, dmix_at=(1, 0))],
        "mix_bwd")
    g["norm1_gain"] = dg1[:1]
    g["ret_norm_gain"] = dgain_ret[:1]
    lru_acc = lru_acc[::SUBLANES]
    g_lru_cw = lru_acc[:LRU_CONV]
    for name in ("conv_b", "gate_a_b", "gate_x_b", "lambda", "norm_gain"):
        g["lru_" + name] = lru_acc[LRU_ACC[name]:LRU_ACC[name] + 1]
    g["lru_gate_a_w"] = _diag_blocks(dwa, LRU_HEADS)
    g["lru_gate_x_w"] = _diag_blocks(dwx, LRU_HEADS)
    rep_packed, rep_layout = _pack([g[n] for n in REPLICATED] + [loss_local], SUBLANES)
    g_in, (down_parts, out_parts, rep_part) = _mm_tn(u1, dproj, "w_in_wgrad", blocks=N_DEV,
                                                     comm=_both(_chip_exchange(low_sums), _gather_first([rep_packed])))
    adamw("ffn_down_w", down_parts)
    adamw("w_out", out_parts)
    g_conv = _conv_rows(_column_blocks(g_lru_cw), _column_blocks(g_ffn_cw), GRAD_DTYPE, 2 * SUBLANES)
    in_sums = to_owner_chips([g_in, g_conv], ["w_in", "conv"], "in")
    in_parts, conv_parts, rep_parts = _run_comms([_both(_chip_exchange(in_sums), _gather_second([rep_part]))],
                                                 "last_grads_exchange")
    adamw("w_in", in_parts)
    pad16 = lambda d: _conv_rows(d["lru_conv_w"], d["ffn_conv_w"], F32, 2 * SUBLANES)
    conv_res = _sum_adamw(conv_parts, pad16(w), pad16(mom_m), pad16(mom_v), "adamw_conv")
    for n, lo, hi in (("lru_conv_w", 0, n_lru), ("ffn_conv_w", n_lru, n_lru + w["ffn_conv_w"].size)):
        res[n] = [r.reshape(-1)[lo:hi].reshape(w[n].shape) for r in conv_res]
    no_state = jnp.zeros_like(loss_local)
    rep_res = _sum_adamw(rep_parts, *[_pack([d[n] for n in REPLICATED] + [no_state], SUBLANES)[0]
                                      for d in (w, mom_m, mom_v)], "adamw_replicated")
    for k in range(4):
        for n, a in zip(REPLICATED, _unpack(rep_res[k], rep_layout)):
            res.setdefault(n, [None] * 4)[k] = a
    loss = _unpack(rep_res[0], rep_layout)[-1][0, 0]

    outs = [loss, grad_x[None]]
    for k in range(4):
        outs += [res[n][k].reshape(out_shape[n]) for n in WEIGHTS]
    return tuple(outs)
```

```python
import math

import numpy as np
import jax
import jax.numpy as jnp
from jax import lax
from jax.experimental import pallas as pl
from jax.experimental.pallas import tpu as pltpu

F32 = jnp.float32
BF16 = jnp.bfloat16
MXU_DTYPE = jnp.bfloat16
GRAD_DTYPE = jnp.bfloat16

N_DEV = 8
N_CHIPS = 4
D_MODEL = 1024
D_LRU = 512
LRU_HEADS = 8
LRU_CONV = 4
LRU_C = 8.0
D_RET = 512
RET_HEADS = 4
RET_HEAD_DIM = 128
RET_CHUNK = 128
ROPE_BASE = 10000.0
D_IN = 3072
D_FF = 3072
FFN_CONV = 3
NORM_EPS = 1e-6

ADAM_LR = 0.001
ADAM_B1 = 0.9
ADAM_B2 = 0.999
ADAM_EPS = 1e-08
ADAM_WD = 0.01
ADAM_STEP = 10

SUBLANES = 8
LANES = 128
VMEM_LIMIT = 48 * 1024 * 1024
FUSED_VMEM_LIMIT = VMEM_LIMIT
FFN_BWD_VMEM_LIMIT = 56 * 1024 * 1024

ROW_TILE = 256
MIX_ROW_TILE = 256
PROJ_ROW_TILE = 512
WGRAD_ROWS = 2048
WGRAD_TILE = 1024
WGRAD_BLOCK_COLUMNS = 768

MESH = pl.DeviceIdType.MESH
ANY = pl.BlockSpec(memory_space=pl.ANY)


def _dot(a, b):
    return jnp.dot(a.astype(MXU_DTYPE), b.astype(MXU_DTYPE), preferred_element_type=F32)


def _dot_nt(a, b):
    return lax.dot_general(a.astype(MXU_DTYPE), b.astype(MXU_DTYPE), (((1,), (1,)), ((), ())),
                           preferred_element_type=F32)


def _dot_tn(a, b):
    return lax.dot_general(a.astype(MXU_DTYPE), b.astype(MXU_DTYPE), (((0,), (0,)), ((), ())),
                           preferred_element_type=F32)


def _sigmoid(x):
    return 0.5 + 0.5 * jnp.tanh(0.5 * x)


_GELU_C = math.sqrt(2.0 / math.pi)
_GELU_C3 = _GELU_C * 0.044715


def _gelu_parts(x):
    x2 = x * x
    t = jnp.tanh(x * (_GELU_C + _GELU_C3 * x2))
    cdf = 0.5 + 0.5 * t
    g = x * cdf
    dg = cdf + (0.5 * x) * (1.0 - t * t) * (_GELU_C + (3.0 * _GELU_C3) * x2)
    return g, dg


def _gelu(x):
    t = jnp.tanh(_GELU_C * (x + 0.044715 * (x * x * x)))
    return x * (0.5 * (1.0 + t))


def _softplus(x):
    return jnp.maximum(x, 0.0) + jnp.log1p(jnp.exp(-jnp.abs(x)))


def _bcast_row(x, r, rows=SUBLANES):
    return jnp.broadcast_to(x[r:r + 1, :], (rows, x.shape[1]))


def _colsum8(x):
    return jnp.broadcast_to(jnp.sum(x, axis=0, keepdims=True), (SUBLANES, x.shape[1]))


def _groups(x):
    return x.reshape(x.shape[0] // SUBLANES, SUBLANES, x.shape[1])


def _shift_down(prev8, tile, s):
    if s == 0:
        return tile
    own = pltpu.roll(_groups(tile), s, 1)
    before = jnp.concatenate([pltpu.roll(_groups(prev8), s, 1), own[:-1]], axis=0)
    row = lax.broadcasted_iota(jnp.int32, own.shape, 1)
    return jnp.where(row >= s, own, before).reshape(tile.shape)


def _shift_up(tile, next8, s):
    if s == 0:
        return tile
    own = pltpu.roll(_groups(tile), SUBLANES - s, 1)
    after = jnp.concatenate([own[1:], pltpu.roll(_groups(next8), SUBLANES - s, 1)], axis=0)
    row = lax.broadcasted_iota(jnp.int32, own.shape, 1)
    return jnp.where(row < SUBLANES - s, own, after).reshape(tile.shape)


def _group_scan(a, b, reverse, fill=lambda: None):
    n, c = a.shape
    row = lax.broadcasted_iota(jnp.int32, a.shape, 0) & (SUBLANES - 1)

    def within_group(x, shift):
        return pltpu.roll(x.reshape(n // SUBLANES, SUBLANES, c), shift, 1).reshape(n, c)

    for s in (1, 2, 4):
        if s > 1:
            fill()
        shift = (SUBLANES - s) if reverse else s
        a_sh = within_group(a, shift)
        b_sh = within_group(b, shift)
        m = (row <= SUBLANES - 1 - s) if reverse else (row >= s)
        b = jnp.where(m, a * b_sh + b, b)
        a = jnp.where(m, a * a_sh, a)
    return a, b


def _carry_scan(a_ref, b_ref, out_ref, carry0, reverse):
    n_groups = a_ref.shape[0] // SUBLANES
    carry = carry0
    for i in range(n_groups):
        r0 = ((n_groups - 1 - i) if reverse else i) * SUBLANES
        hg = a_ref[r0:r0 + SUBLANES, :] * carry + b_ref[r0:r0 + SUBLANES, :]
        out_ref[r0:r0 + SUBLANES, :] = hg
        carry = _bcast_row(hg, 0 if reverse else SUBLANES - 1)
    return carry


def _rms_fwd(h, gain):
    rstd = lax.rsqrt(jnp.mean(h * h, axis=-1, keepdims=True) + NORM_EPS)
    n = h * rstd
    return n, rstd, n * gain


def _rms_bwd(dy, n, rstd, gain):
    dn = dy * gain
    dh = rstd * (dn - n * jnp.mean(dn * n, axis=-1, keepdims=True))
    return dh, _colsum8(dy * n)


def _halo_rows(dtype):
    return SUBLANES * (4 // jnp.dtype(dtype).itemsize)


def _halo_map(tile_rows, col, halo_rows=SUBLANES):
    per = tile_rows // halo_rows
    return lambda i: (jnp.maximum(i * per - 1, 0), col)


def _resident(shape):
    return pl.BlockSpec(shape, lambda *_: (0,) * len(shape), pipeline_mode=pl.Buffered(1))


def _place():
    x, y, c = lax.axis_index("x"), lax.axis_index("y"), lax.axis_index("c")
    chips = [(1 - x, y), (x, 1 - y), (1 - x, 1 - y)]
    return x, y, c, chips


def _dev(x, y, c):
    return 4 * x + 2 * y + c


class _Copy:
    def __init__(self, make):
        self.make = make

    def start(self):
        self.make().start()

    def wait(self):
        self.make().wait()

    def wait_send(self):
        self.make().wait_send()

    def wait_recv(self):
        self.make().wait_recv()


def _remote(src, dst, send_sem, recv_sem, to):
    return _Copy(lambda: pltpu.make_async_remote_copy(src_ref=src, dst_ref=dst, send_sem=send_sem, recv_sem=recv_sem,
                                                      device_id=to, device_id_type=MESH))


def _local(src, dst, sem):
    return _Copy(lambda: pltpu.make_async_copy(src, dst, sem))


class _Comm:
    def __init__(self, operands, out_shape, sems, descs, aliases=()):
        self.operands, self.out_shape, self.sems, self.descs, self.aliases = operands, out_shape, sems, descs, aliases

    def start(self, ins, outs, sems):
        local, sends, _ = self.descs(ins, outs, sems)
        for cp in sends + local:
            cp.start()

    def wait(self, ins, outs, sems):
        local, sends, recvs = self.descs(ins, outs, sems)
        for cp in recvs:
            cp.wait_recv()
        for cp in sends:
            cp.wait_send()
        for cp in local:
            cp.wait()


def _gather_first(shards):
    n = len(shards)

    def descs(ins, outs, sems):
        send, recv, loc = sems
        x, y, c, chips = _place()
        me = _dev(x, y, c)
        targets = [(x, y, 1 - c)] + [(*chip, c) for chip in chips]
        local, sends, recvs = [], [], []
        for t in range(n):
            local.append(_local(ins[t], outs[t].at[me], loc.at[t]))
            for k, to in enumerate(targets):
                i = 4 * t + k
                sends.append(_remote(ins[t], outs[t].at[me], send.at[i], recv.at[i], to))
                recvs.append(_remote(ins[t], outs[t].at[_dev(*to)], send.at[i], recv.at[i], to))
        return local, sends, recvs

    return _Comm(list(shards), [jax.ShapeDtypeStruct((N_DEV,) + s.shape, s.dtype) for s in shards],
                 [pltpu.SemaphoreType.DMA((4 * n,)), pltpu.SemaphoreType.DMA((4 * n,)), pltpu.SemaphoreType.DMA((n,))],
                 descs)


def _gather_second(gathered):
    n = len(gathered)

    def descs(ins, outs, sems):
        send, recv = sems
        x, y, c, chips = _place()
        sends, recvs = [], []
        for t in range(n):
            for j, chip in enumerate(chips):
                i = 3 * t + j
                have, get = _dev(*chip, c), _dev(*chip, 1 - c)
                sends.append(_remote(outs[t].at[have], outs[t].at[have], send.at[i], recv.at[i], (x, y, 1 - c)))
                recvs.append(_remote(outs[t].at[have], outs[t].at[get], send.at[i], recv.at[i], (x, y, 1 - c)))
        return [], sends, recvs

    return _Comm(list(gathered), [jax.ShapeDtypeStruct(g.shape, g.dtype) for g in gathered],
                 [pltpu.SemaphoreType.DMA((3 * n,)), pltpu.SemaphoreType.DMA((3 * n,))], descs,
                 aliases=[(t, t) for t in range(n)])


def _pair_exchange(blocks):
    n = len(blocks)

    def descs(ins, outs, sems):
        send, recv = sems
        x, y, c, _ = _place()
        sends, recvs = [], []
        for t in range(n):
            for q in range(N_CHIPS):
                i = N_CHIPS * t + q
                cp = _remote(ins[t].at[2 * q + 1 - c], outs[t].at[q], send.at[i], recv.at[i], (x, y, 1 - c))
                sends.append(cp)
                recvs.append(cp)
        return [], sends, recvs

    return _Comm(list(blocks), [jax.ShapeDtypeStruct((N_CHIPS,) + b.shape[1:], b.dtype) for b in blocks],
                 [pltpu.SemaphoreType.DMA((N_CHIPS * n,)), pltpu.SemaphoreType.DMA((N_CHIPS * n,))], descs)


def _chip_exchange(blocks):
    n = len(blocks)

    def descs(ins, outs, sems):
        send, recv, loc = sems
        x, y, c, chips = _place()
        me = 2 * x + y
        local, sends, recvs = [], [], []
        for t in range(n):
            local.append(_local(ins[t].at[me], outs[t].at[me], loc.at[t]))
            for j, (px, py) in enumerate(chips):
                i = 3 * t + j
                q = 2 * px + py
                sends.append(_remote(ins[t].at[q], outs[t].at[me], send.at[i], recv.at[i], (px, py, c)))
                recvs.append(_remote(ins[t].at[q], outs[t].at[q], send.at[i], recv.at[i], (px, py, c)))
        return local, sends, recvs

    return _Comm(list(blocks), [jax.ShapeDtypeStruct(b.shape, b.dtype) for b in blocks],
                 [pltpu.SemaphoreType.DMA((3 * n,)), pltpu.SemaphoreType.DMA((3 * n,)), pltpu.SemaphoreType.DMA((n,))],
                 descs)


def _both(a, b):
    na, oa, sa = len(a.operands), len(a.out_shape), len(a.sems)

    def descs(ins, outs, sems):
        local_a, sends_a, recvs_a = a.descs(ins[:na], outs[:oa], sems[:sa])
        local_b, sends_b, recvs_b = b.descs(ins[na:], outs[oa:], sems[sa:])
        return local_a + local_b, sends_a + sends_b, recvs_a + recvs_b

    return _Comm(a.operands + b.operands, a.out_shape + b.out_shape, a.sems + b.sems, descs,
                 aliases=list(a.aliases) + [(na + i, oa + o) for i, o in b.aliases])


def _run_comms(comms, name):
    first = comms[0]
    n_in, n_out = len(first.operands), len(first.out_shape)

    def body(*refs):
        ins, outs, sems = refs[:n_in], refs[n_in:n_in + n_out], list(refs[n_in + n_out:])
        for k, comm in enumerate(comms):
            mine = [sems.pop(0) for _ in comm.sems]
            comm.start(ins if k == 0 else outs, outs, mine)
            comm.wait(ins if k == 0 else outs, outs, mine)

    outs = pl.pallas_call(
        body, name=name, out_shape=first.out_shape, in_specs=[ANY] * n_in, out_specs=[ANY] * n_out,
        scratch_shapes=[s for comm in comms for s in comm.sems], input_output_aliases=dict(first.aliases),
    )(*first.operands)
    return list(outs)


def _call(body, *, name, grid, in_specs, out_specs, out_shape, operands, scratch_shapes=(), comm=None, aliases=None,
          vmem_limit=VMEM_LIMIT):
    sem = ("arbitrary",) * len(grid)
    params = pltpu.CompilerParams(dimension_semantics=sem, vmem_limit_bytes=vmem_limit)
    aliases = dict(aliases or {})
    if comm is None:
        return pl.pallas_call(body, name=name, grid=grid, in_specs=in_specs, out_specs=out_specs, out_shape=out_shape,
                              scratch_shapes=list(scratch_shapes), input_output_aliases=aliases,
                              compiler_params=params)(*operands)
    n_in, n_out, n_scr = len(in_specs), len(out_specs), len(scratch_shapes)
    c_in, c_out = len(comm.operands), len(comm.out_shape)

    def wrapped(*refs):
        refs = list(refs)
        ins, refs = refs[:n_in], refs[n_in:]
        cins, refs = refs[:c_in], refs[c_in:]
        outs, refs = refs[:n_out], refs[n_out:]
        couts, refs = refs[:c_out], refs[c_out:]
        scr, csems = refs[:n_scr], refs[n_scr:]
        first = last = None
        for axis, size in enumerate(grid):
            at_first, at_last = pl.program_id(axis) == 0, pl.program_id(axis) == size - 1
            first = at_first if first is None else first & at_first
            last = at_last if last is None else last & at_last

        @pl.when(first)
        def _():
            comm.start(cins, couts, csems)

        body(*ins, *outs, *scr)

        @pl.when(last)
        def _():
            comm.wait(cins, couts, csems)

    res = pl.pallas_call(
        wrapped, name=name, grid=grid, in_specs=list(in_specs) + [ANY] * c_in, out_specs=list(out_specs) + [ANY] * c_out,
        out_shape=list(out_shape) + list(comm.out_shape), scratch_shapes=list(scratch_shapes) + list(comm.sems),
        input_output_aliases={**aliases, **{n_in + i: n_out + o for i, o in comm.aliases}}, compiler_params=params,
    )(*operands, *comm.operands)
    return list(res[:n_out]), list(res[n_out:])


def _mm_tn(a, b, name, blocks=1, comm=None):
    t, m = a.shape
    n = b.shape[1]
    tk = min(WGRAD_ROWS, t)
    nk = t // tk
    cb = n // blocks
    per = max(1, WGRAD_BLOCK_COLUMNS // cb) if blocks > 1 else 1
    tn = per * cb if blocks > 1 else min(WGRAD_TILE, n)
    tm = min(WGRAD_TILE, m)
    assert blocks == 1 or tm == m

    def body(a_ref, b_ref, o_ref, acc):
        k = pl.program_id(2)

        @pl.when(k == 0)
        def _():
            acc[...] = jnp.zeros_like(acc)
        acc[...] += _dot_tn(a_ref[...], b_ref[...])

        @pl.when(k == nk - 1)
        def _():
            if blocks == 1:
                o_ref[...] = acc[...].astype(o_ref.dtype)
            else:
                for s in range(per):
                    o_ref[s] = acc[:, s * cb:(s + 1) * cb].astype(o_ref.dtype)

    if blocks == 1:
        out_spec = pl.BlockSpec((tm, tn), lambda i, j, k: (i, j))
        out_shape = jax.ShapeDtypeStruct((m, n), GRAD_DTYPE)
    else:
        out_spec = pl.BlockSpec((per, m, cb), lambda i, j, k: (j, 0, 0))
        out_shape = jax.ShapeDtypeStruct((blocks, m, cb), GRAD_DTYPE)
    res = _call(body, name=name, grid=(m // tm, n // tn, nk), comm=comm,
                in_specs=[pl.BlockSpec((tk, tm), lambda i, j, k: (k, i)), pl.BlockSpec((tk, tn), lambda i, j, k: (k, j))],
                out_specs=[out_spec], out_shape=[out_shape], operands=[a, b],
                scratch_shapes=[pltpu.VMEM((tm, tn), F32)])
    return res[0] if comm is None else (res[0][0], res[1])


INPROJ_TN = 1024


def _inproj_fwd(x, g1, w_blocks, comm):
    t = x.shape[0]
    tm = min(PROJ_ROW_TILE, t)
    nb, _, cb = w_blocks.shape

    def body(x_ref, g_ref, w_hbm, u_ref, p_ref, w_all, sems):
        @pl.when(pl.program_id(0) == 0)
        def _():
            copies = [pltpu.make_async_copy(w_hbm.at[d], w_all.at[:, pl.ds(d * cb, cb)], sems.at[d]) for d in range(nb)]
            for cp in copies:
                cp.start()
            for cp in copies:
                cp.wait()

        _, _, u = _rms_fwd(x_ref[...], g_ref[...])
        u = u.astype(MXU_DTYPE)
        u_ref[...] = u
        for lo in range(0, D_IN, INPROJ_TN):
            p_ref[:, lo:lo + INPROJ_TN] = _dot(u, w_all[:, lo:lo + INPROJ_TN]).astype(p_ref.dtype)

    return _call(body, name="inproj_fwd", grid=(t // tm,), comm=comm,
                 in_specs=[pl.BlockSpec((tm, D_MODEL), lambda i: (i, 0)), pl.BlockSpec((1, D_MODEL), lambda i: (0, 0)), ANY],
                 out_specs=[pl.BlockSpec((tm, D_MODEL), lambda i: (i, 0)), pl.BlockSpec((tm, D_IN), lambda i: (i, 0))],
                 out_shape=[jax.ShapeDtypeStruct((t, D_MODEL), MXU_DTYPE), jax.ShapeDtypeStruct((t, D_IN), MXU_DTYPE)],
                 scratch_shapes=[pltpu.VMEM((D_MODEL, nb * cb), w_blocks.dtype), pltpu.SemaphoreType.DMA((nb,))],
                 operands=[x, g1, w_blocks])


def _lru_gates(xc, wa, ba, wx, bx, sp, fill=lambda: None):
    r = _sigmoid(_dot(xc, wa) + ba)
    fill()
    ig = _sigmoid(_dot(xc, wx) + bx)
    fill()
    log_a = (-LRU_C) * r * sp
    a = jnp.exp(log_a)
    m = jnp.sqrt(-jnp.tanh(log_a) * (a * a + 1.0))
    return r, ig, a, m


def _fused(parts, name, comm=None):
    grid = parts[0]["grid"]
    assert all(p["grid"] == grid for p in parts)
    counts = [(len(p["in_specs"]), len(p["out_specs"]), len(p.get("scratch_shapes", ()))) for p in parts]

    def body(*refs):
        refs = list(refs)
        groups = []
        for kind in range(3):
            taken = []
            for c in counts:
                taken.append(refs[:c[kind]])
                refs = refs[c[kind]:]
            groups.append(taken)
        ins, outs, scr = groups
        pending = []

        def fill(n=None):
            for _ in range(share if n is None else n):
                if pending:
                    pending.pop(0)()

        ctx = dict(outs=outs, scratch=scr, fill=fill)
        run = lambda key: [p[key](*ins[k], *outs[k], *scr[k], ctx) for k, p in enumerate(parts) if key in p]
        run("head")
        for pieces in run("units"):
            pending.extend(pieces)
        points = sum(p.get("fill_points", 0) for p in parts)
        share = -(-len(pending) // max(points, 1))
        run("body")
        fill(len(pending))
        run("tail")

    cat = lambda key: [x for p in parts for x in p.get(key, ())]
    res = _call(body, name=name, grid=grid, comm=comm, vmem_limit=FUSED_VMEM_LIMIT,
                in_specs=cat("in_specs"), out_specs=cat("out_specs"),
                out_shape=cat("out_shape"), scratch_shapes=cat("scratch_shapes"), operands=cat("operands"))
    outs, side = (res if comm is not None else (res, None))
    split, at = [], 0
    for _, n_out, _ in counts:
        split.append(list(outs[at:at + n_out]))
        at += n_out
    return split if comm is None else (split, side)


def _lru_fwd(proj, conv_w, conv_b, wa, ba, wx, bx, lam, gain):
    t = proj.shape[0]
    tm = min(MIX_ROW_TILE, t)
    c = D_LRU

    def body(x_ref, xh_ref, g_ref, cw_ref, cb_ref, wa_ref, ba_ref, wx_ref, bx_ref, lam_ref, gain_ref,
             xc_ref, h_ref, y_ref, a_scr, b_scr, carry, ctx):
        fill = ctx["fill"]
        i = pl.program_id(0)

        @pl.when(i == 0)
        def _():
            carry[...] = jnp.zeros_like(carry)

        fill()
        x = x_ref[...].astype(F32)
        prev = jnp.where(i == 0, 0.0, xh_ref[...].astype(F32)[-SUBLANES:, :])
        cw = cw_ref[...]
        xc = cb_ref[...] + cw[LRU_CONV - 1:LRU_CONV, :] * x
        for k in range(LRU_CONV - 1):
            xc = xc + cw[k:k + 1, :] * _shift_down(prev, x, LRU_CONV - 1 - k)
        xc_ref[...] = xc
        fill()
        sp = _softplus(-lam_ref[...])
        _, ig, a, m = _lru_gates(xc, wa_ref[...], ba_ref[...], wx_ref[...], bx_ref[...], sp, fill)
        fill()
        ga, gb = _group_scan(a, m * (ig * xc), reverse=False, fill=fill)
        a_scr[...] = ga
        b_scr[...] = gb
        fill()
        carry[...] = _carry_scan(a_scr, b_scr, h_ref, carry[...], reverse=False)
        fill()
        z = h_ref[...] * _gelu(g_ref[...].astype(F32))
        fill()
        _, _, y = _rms_fwd(z, gain_ref[...])
        y_ref[...] = y.astype(y_ref.dtype)

    row = lambda i: (i, 0)
    full = lambda i: (0, 0)
    vec = pl.BlockSpec((1, c), full)
    hb = _halo_rows(proj.dtype)
    return dict(body=body, grid=(t // tm,), fill_points=10,
                in_specs=[pl.BlockSpec((tm, c), row), pl.BlockSpec((hb, c), _halo_map(tm, 0, hb)),
                          pl.BlockSpec((tm, c), lambda i: (i, 1)),
                          pl.BlockSpec((LRU_CONV, c), full), vec, pl.BlockSpec((c, c), full), vec,
                          pl.BlockSpec((c, c), full), vec, vec, vec],
                out_specs=[pl.BlockSpec((tm, c), row), pl.BlockSpec((tm, c), row), pl.BlockSpec((tm, c), row)],
                out_shape=[jax.ShapeDtypeStruct((t, c), F32), jax.ShapeDtypeStruct((t, c), F32),
                           jax.ShapeDtypeStruct((t, c), MXU_DTYPE)],
                scratch_shapes=[pltpu.VMEM((tm, c), F32), pltpu.VMEM((tm, c), F32), pltpu.VMEM((SUBLANES, c), F32)],
                operands=[proj, proj, proj, conv_w, conv_b, wa, ba, wx, bx, lam, gain])


def _ret_consts():
    c = RET_CHUNK
    log_g = jnp.log1p(-jnp.exp2(-5.0 - jnp.arange(RET_HEADS, dtype=F32)))
    idx = jnp.arange(c, dtype=F32)
    diff = idx[:, None] - idx[None, :]
    decay = jnp.where(diff[None] >= 0, jnp.exp(jnp.maximum(diff, 0.0)[None] * log_g[:, None, None]), 0.0)
    zeta = jnp.exp((c - 1 - idx)[None, :] * log_g[:, None])
    xi = jnp.exp((idx + 1.0)[None, :] * log_g[:, None])
    spread = lambda v: jnp.repeat(v.T, RET_HEAD_DIM, axis=1)
    log_g_np = np.log1p(-np.exp2(-5.0 - np.arange(RET_HEADS, dtype=np.float32))).astype(np.float32)
    g_chunk = [float(np.exp(np.float32(c) * lg)) for lg in log_g_np]
    return decay, spread(xi), spread(zeta), g_chunk


def _rope_tables(t):
    pos = np.arange(t, dtype=np.float32)
    inv_freq = np.float32(ROPE_BASE) ** (-np.arange(0, RET_HEAD_DIM, 2, dtype=np.float32) / np.float32(RET_HEAD_DIM))
    ang = (pos[:, None] * inv_freq.astype(np.float32)[None, :]).astype(np.float32).astype(np.float64)
    cos, sin = np.cos(ang).astype(np.float32), np.sin(ang).astype(np.float32)
    return jnp.asarray(np.concatenate([cos, cos], axis=-1)), jnp.asarray(np.concatenate([-sin, sin], axis=-1))


def _rope(x, cos2, sin_signed):
    return x * cos2 + pltpu.roll(x, RET_HEAD_DIM // 2, 1) * sin_signed


def _rope_bwd(d, cos2, sin_signed):
    return d * cos2 + pltpu.roll(d * sin_signed, RET_HEAD_DIM // 2, 1)


RET_SCALE = RET_HEAD_DIM ** -0.5


RET_CHUNKS_PER_STEP = MIX_ROW_TILE // RET_CHUNK


def _ret_fwd(proj, cos2, sin_signed, gain):
    t = proj.shape[0]
    c, d, nh = RET_CHUNK, RET_HEAD_DIM, RET_HEADS
    n_chunks = t // c
    per = RET_CHUNKS_PER_STEP if n_chunks % RET_CHUNKS_PER_STEP == 0 else 1
    rows = per * c
    decay, xi, zeta, g_chunk = _ret_consts()

    def units(qk_ref, vg_ref, cos_ref, sin_ref, dec_ref, xi_ref, zeta_ref, gain_ref, o_ref, y_ref, st_ref, state, ctx):
        cur = [None] * nh

        def start():
            @pl.when(pl.program_id(0) == 0)
            def _():
                state[...] = jnp.zeros_like(state)
            for h in range(nh):
                cur[h] = state[h]

        def retain(s, h, keep):
            rs = slice(s * c, (s + 1) * c)
            cos2, sin_s = cos_ref[rs, :], sin_ref[rs, :]
            lo = h * d
            q = _rope(qk_ref[rs, lo:lo + d].astype(F32), cos2, sin_s)
            k = _rope(qk_ref[rs, D_RET + lo:D_RET + lo + d].astype(F32), cos2, sin_s) * RET_SCALE
            v = vg_ref[rs, lo:lo + d]
            s_prev = cur[h]
            st_ref[s, h] = s_prev
            scores = _dot_nt(q, k) * dec_ref[h]
            o = _dot(scores, v) + _dot(q * xi_ref[:, lo:lo + d], s_prev)
            cur[h] = s_prev * g_chunk[h] + _dot_tn(k * zeta_ref[:, lo:lo + d], v)
            o_ref[rs, lo:lo + d] = o
            keep["o"] = o

        def normalise(s, h, keep):
            rs = slice(s * c, (s + 1) * c)
            lo = h * d
            o = keep["o"]
            g = vg_ref[rs, D_RET + lo:D_RET + lo + d].astype(F32)
            mu = jnp.mean(o, axis=-1, keepdims=True)
            oc = o - mu
            on = oc * lax.rsqrt(jnp.mean(oc * oc, axis=-1, keepdims=True) + NORM_EPS)
            y_ref[rs, lo:lo + d] = (on * gain_ref[:, lo:lo + d] * (g * _sigmoid(g))).astype(y_ref.dtype)

        def end():
            for h in range(nh):
                state[h] = cur[h]

        pieces = [start]
        for s in range(per):
            for h in range(nh):
                keep = {}
                pieces += [lambda s=s, h=h, keep=keep: retain(s, h, keep),
                           lambda s=s, h=h, keep=keep: normalise(s, h, keep)]
        return pieces + [end]

    full2 = lambda i: (0, 0)
    return dict(units=units, grid=(n_chunks // per,),
                in_specs=[pl.BlockSpec((rows, 2 * D_RET), lambda i: (i, 1)),
                          pl.BlockSpec((rows, 2 * D_RET), lambda i: (i, 2)),
                          pl.BlockSpec((rows, d), lambda i: (i, 0)), pl.BlockSpec((rows, d), lambda i: (i, 0)),
                          pl.BlockSpec((nh, c, c), lambda i: (0, 0, 0)), pl.BlockSpec((c, D_RET), full2),
                          pl.BlockSpec((c, D_RET), full2), pl.BlockSpec((1, D_RET), full2)],
                out_specs=[pl.BlockSpec((rows, D_RET), lambda i: (i, 0)), pl.BlockSpec((rows, D_RET), lambda i: (i, 0)),
                           pl.BlockSpec((per, nh, d, d), lambda i: (i, 0, 0, 0))],
                out_shape=[jax.ShapeDtypeStruct((t, D_RET), F32), jax.ShapeDtypeStruct((t, D_RET), MXU_DTYPE),
                           jax.ShapeDtypeStruct((n_chunks, nh, d, d), F32)],
                scratch_shapes=[pltpu.VMEM((nh, d, d), F32)],
                operands=[proj, proj, cos2, sin_signed, decay, xi, zeta, gain])


def _outproj_fwd(x, y_lru, y_ret, w_out, g2, comm):
    t = x.shape[0]
    tm = min(PROJ_ROW_TILE, t)

    def body(x_ref, yl_ref, yr_ref, w_ref, g_ref, h1_ref, u2_ref):
        h1 = x_ref[...] + _dot(yl_ref[...], w_ref[:D_LRU, :]) + _dot(yr_ref[...], w_ref[D_LRU:, :])
        h1_ref[...] = h1
        _, _, u = _rms_fwd(h1, g_ref[...])
        u2_ref[...] = u.astype(u2_ref.dtype)

    row = lambda i: (i, 0)
    return _call(body, name="outproj_fwd", grid=(t // tm,), comm=comm,
                 in_specs=[pl.BlockSpec((tm, D_MODEL), row), pl.BlockSpec((tm, D_LRU), row), pl.BlockSpec((tm, D_RET), row),
                           _resident((D_MODEL, D_MODEL)), pl.BlockSpec((1, D_MODEL), lambda i: (0, 0))],
                 out_specs=[pl.BlockSpec((tm, D_MODEL), row), pl.BlockSpec((tm, D_MODEL), row)],
                 out_shape=[jax.ShapeDtypeStruct((t, D_MODEL), F32), jax.ShapeDtypeStruct((t, D_MODEL), MXU_DTYPE)],
                 operands=[x, y_lru, y_ret, w_out, g2])


FFN_TN = 768
FFN_NJ = D_FF // FFN_TN
FFN_GROUP = 4


def _ffn_fwd(u2, w_blocks, conv_w, conv_b, w_down, h1, gf, target):
    t = u2.shape[0]
    tm = min(ROW_TILE, t)
    tn, nj, group = FFN_TN, FFN_NJ, FFN_GROUP
    ng, tw = nj // group, group * tn
    hb = _halo_rows(u2.dtype)
    assert w_blocks.shape == (2 * nj, D_MODEL, tn)

    def conv(ext, col, up_ref, conv_ref, cw_ref, cb_ref, first):
        x = ext[hb:, :]
        up_ref[:, col] = x.astype(up_ref.dtype)
        prev = jnp.where(first, 0.0, ext[hb - SUBLANES:hb, :])
        cw = cw_ref[:, col]
        y = cb_ref[:, col] + cw[FFN_CONV - 1:FFN_CONV, :] * x
        for k in range(FFN_CONV - 1):
            y = y + cw[k:k + 1, :] * _shift_down(prev, x, FFN_CONV - 1 - k)
        conv_ref[:, col] = y.astype(conv_ref.dtype)
        return y

    def body(u_ref, uh_ref, w_ref, cwa_ref, cwv_ref, cba_ref, cbv_ref, wd_ref, h1_ref, gf_ref, tg_ref,
             upa_ref, upv_ref, ca_ref, cv_ref, act_ref, dh_ref, dhb_ref, dgf_ref, loss_ref, acc):
        i, jg = pl.program_id(0), pl.program_id(1)

        @pl.when((i == 0) & (jg == 0))
        def _():
            dgf_ref[...] = jnp.zeros_like(dgf_ref)
            loss_ref[...] = jnp.zeros_like(loss_ref)

        @pl.when(jg == 0)
        def _():
            acc[...] = jnp.zeros_like(acc)

        u_ext = jnp.concatenate([uh_ref[...], u_ref[...]], axis=0)

        def project(jj):
            j = jg * group + jj
            return _dot(u_ext, w_ref[j]), _dot(u_ext, w_ref[nj + j])

        down, ahead = None, project(0)
        for jj in range(group):
            col = slice(jj * tn, (jj + 1) * tn)
            j = jg * group + jj
            ext_a, ext_v = ahead
            if jj + 1 < group:
                ahead = project(jj + 1)
            a = conv(ext_a, col, upa_ref, ca_ref, cwa_ref, cba_ref, i == 0)
            v = conv(ext_v, col, upv_ref, cv_ref, cwv_ref, cbv_ref, i == 0)
            act = (_gelu(a) * v).astype(act_ref.dtype)
            act_ref[:, col] = act
            part = _dot(act, wd_ref[pl.ds(pl.multiple_of(j * tn, tn), tn), :])
            down = part if down is None else down + part
        acc[...] += down

        @pl.when(jg == ng - 1)
        def _():
            n, rstd, y = _rms_fwd(h1_ref[...] + acc[...], gf_ref[...])
            err = y - tg_ref[...]
            loss_ref[...] += (0.5 / D_MODEL) * jnp.sum(err * err)
            dh, dgf = _rms_bwd(err * (1.0 / D_MODEL), n, rstd, gf_ref[...])
            dgf_ref[...] += dgf
            dh_ref[...] = dh
            dhb_ref[...] = dh.astype(dhb_ref.dtype)

    per = tm // hb
    row = lambda i, j: (i, 0)
    const = lambda i, j: (0, 0)
    tile = pl.BlockSpec((tm, tw), lambda i, j: (i, j))
    return _call(body, name="ffn_fwd", grid=(t // tm, ng), vmem_limit=FUSED_VMEM_LIMIT,
                 in_specs=[pl.BlockSpec((tm, D_MODEL), row),
                           pl.BlockSpec((hb, D_MODEL), lambda i, j: (jnp.maximum(i * per - 1, 0), 0)),
                           _resident(w_blocks.shape),
                           pl.BlockSpec((FFN_CONV, tw), lambda i, j: (0, j)),
                           pl.BlockSpec((FFN_CONV, tw), lambda i, j: (0, j + ng)),
                           pl.BlockSpec((1, tw), lambda i, j: (0, j)), pl.BlockSpec((1, tw), lambda i, j: (0, j + ng)),
                           _resident((D_FF, D_MODEL)),
                           pl.BlockSpec((tm, D_MODEL), row), pl.BlockSpec((1, D_MODEL), const),
                           pl.BlockSpec((tm, D_MODEL), row)],
                 out_specs=[tile] * 5 + [pl.BlockSpec((tm, D_MODEL), row),
                            pl.BlockSpec((tm, D_MODEL), row), pl.BlockSpec((SUBLANES, D_MODEL), const),
                            pl.BlockSpec((SUBLANES, LANES), const)],
                 out_shape=[jax.ShapeDtypeStruct((t, D_FF), MXU_DTYPE)] * 5 + [
                            jax.ShapeDtypeStruct((t, D_MODEL), F32),
                            jax.ShapeDtypeStruct((t, D_MODEL), MXU_DTYPE), jax.ShapeDtypeStruct((SUBLANES, D_MODEL), F32),
                            jax.ShapeDtypeStruct((SUBLANES, LANES), F32)],
                 scratch_shapes=[pltpu.VMEM((tm, D_MODEL), F32)],
                 operands=[u2, u2, w_blocks, conv_w, conv_w, conv_b, conv_b, w_down, h1, gf, target])


FFN_ACC_ROWS = SUBLANES * (FFN_CONV + 1)


def _ffn_bwd(dh2, dh2_b, w_down, up_a, up_v, conv_a, conv_v, conv_w, w_up_blocks, h1, g2, comm):
    t = up_a.shape[0]
    tm = min(ROW_TILE, t)
    tn, nj, group = FFN_TN, FFN_NJ, FFN_GROUP
    ng, tw = nj // group, group * tn
    ni = t // tm
    assert w_up_blocks.shape == (2 * nj, D_MODEL, tn)

    def conv_bwd(dy, x, cw, acc_ref, carry_ref, dup_ref, col):
        nxt = carry_ref[...]
        carry_ref[...] = dy[:SUBLANES, :]
        ahead = [_shift_up(dy, nxt, FFN_CONV - 1 - k) for k in range(FFN_CONV)]
        dx = cw[FFN_CONV - 1:FFN_CONV, :] * dy
        for k in range(FFN_CONV - 1):
            dx = dx + cw[k:k + 1, :] * ahead[k]
        dx = dx.astype(dup_ref.dtype)
        dup_ref[:, col] = dx
        for k in range(FFN_CONV):
            acc_ref[k * SUBLANES:(k + 1) * SUBLANES, :] += _colsum8(ahead[k] * x)
        acc_ref[FFN_CONV * SUBLANES:, :] += _colsum8(dy)
        return dx

    def body(dh_ref, dhb_ref, wd_ref, ua_ref, uv_ref, ca_ref, cv_ref, cwa_ref, cwv_ref, wu_ref, h1_ref, g2_ref,
             dua_ref, duv_ref, acca_ref, accv_ref, dh1_ref, dh1b_ref, dg2_ref, carry_a, carry_v, du):
        i, jg = pl.program_id(0), pl.program_id(1)

        @pl.when((i == 0) & (jg == 0))
        def _():
            for ref in (acca_ref, accv_ref, carry_a, carry_v, dg2_ref):
                ref[...] = jnp.zeros_like(ref)

        dhb = dhb_ref[...]

        def through_down(jj):
            j = jg * group + jj
            return _dot_nt(dhb, wd_ref[pl.ds(pl.multiple_of(j * tn, tn), tn), :])

        part, ahead = None, through_down(0)
        for jj in range(group):
            col = slice(jj * tn, (jj + 1) * tn)
            j = jg * group + jj
            dact = ahead
            if jj + 1 < group:
                ahead = through_down(jj + 1)
            v = cv_ref[:, col].astype(F32)
            g, dg = _gelu_parts(ca_ref[:, col].astype(F32))
            da = conv_bwd(dact * v * dg, ua_ref[:, col].astype(F32), cwa_ref[:, col], acca_ref.at[j], carry_a.at[j],
                          dua_ref, col)
            dv = conv_bwd(dact * g, uv_ref[:, col].astype(F32), cwv_ref[:, col], accv_ref.at[j], carry_v.at[j],
                          duv_ref, col)
            term = _dot_nt(da, wu_ref[j]) + _dot_nt(dv, wu_ref[nj + j])
            part = term if part is None else part + term

        @pl.when(jg == 0)
        def _():
            du[...] = part

        @pl.when(jg > 0)
        def _():
            du[...] += part

        @pl.when(jg == ng - 1)
        def _():
            n, rstd, _ = _rms_fwd(h1_ref[...], g2_ref[...])
            dh1, dg2 = _rms_bwd(du[...], n, rstd, g2_ref[...])
            dh1 = dh1 + dh_ref[...]
            dg2_ref[...] += dg2
            dh1_ref[...] = dh1
            dh1b_ref[...] = dh1.astype(dh1b_ref.dtype)

    row = lambda i, j: (ni - 1 - i, 0)
    const = lambda i, j: (0, 0)
    tile = pl.BlockSpec((tm, tw), lambda i, j: (ni - 1 - i, j))
    acc = pl.BlockSpec((nj, FFN_ACC_ROWS, tn), lambda i, j: (0, 0, 0))
    return _call(body, name="ffn_bwd", grid=(ni, ng), comm=comm, vmem_limit=FFN_BWD_VMEM_LIMIT,
                 in_specs=[pl.BlockSpec((tm, D_MODEL), row), pl.BlockSpec((tm, D_MODEL), row),
                           _resident((D_FF, D_MODEL)), tile, tile, tile, tile,
                           pl.BlockSpec((FFN_CONV, tw), lambda i, j: (0, j)),
                           pl.BlockSpec((FFN_CONV, tw), lambda i, j: (0, j + ng)),
                           _resident(w_up_blocks.shape), pl.BlockSpec((tm, D_MODEL), row),
                           pl.BlockSpec((1, D_MODEL), const)],
                 out_specs=[tile, tile, acc, acc, pl.BlockSpec((tm, D_MODEL), row), pl.BlockSpec((tm, D_MODEL), row),
                            pl.BlockSpec((SUBLANES, D_MODEL), const)],
                 out_shape=[jax.ShapeDtypeStruct((t, D_FF), MXU_DTYPE), jax.ShapeDtypeStruct((t, D_FF), MXU_DTYPE),
                            jax.ShapeDtypeStruct((nj, FFN_ACC_ROWS, tn), F32),
                            jax.ShapeDtypeStruct((nj, FFN_ACC_ROWS, tn), F32),
                            jax.ShapeDtypeStruct((t, D_MODEL), F32), jax.ShapeDtypeStruct((t, D_MODEL), MXU_DTYPE),
                            jax.ShapeDtypeStruct((SUBLANES, D_MODEL), F32)],
                 scratch_shapes=[pltpu.VMEM((nj, SUBLANES, tn), F32), pltpu.VMEM((nj, SUBLANES, tn), F32),
                                 pltpu.VMEM((tm, D_MODEL), F32)],
                 operands=[dh2, dh2_b, w_down, up_a, up_v, conv_a, conv_v, conv_w, conv_w, w_up_blocks, h1, g2])


def _ret_bwd(proj, cos2, sin_signed, gain, o, states, dmix_at):
    t = proj.shape[0]
    c, d, nh = RET_CHUNK, RET_HEAD_DIM, RET_HEADS
    n_chunks = t // c
    per = RET_CHUNKS_PER_STEP if n_chunks % RET_CHUNKS_PER_STEP == 0 else 1
    rows = per * c
    n_steps = n_chunks // per
    decay, xi, zeta, g_chunk = _ret_consts()
    base = 2 * D_LRU

    def units(qk_ref, vg_ref, cos_ref, sin_ref, dec_ref, xi_ref, zeta_ref, gain_ref, o_ref, st_ref,
              dp_ref, dgain_ref, gstate, ctx):
        cur = [None] * nh
        dmix = ctx["scratch"][dmix_at[0]][dmix_at[1]]

        def start():
            @pl.when(pl.program_id(0) == 0)
            def _():
                gstate[...] = jnp.zeros_like(gstate)
                dgain_ref[...] = jnp.zeros_like(dgain_ref)
            for h in range(nh):
                cur[h] = gstate[h]

        def gate_and_norm(s, h, keep):
            rs = slice(s * c, (s + 1) * c)
            lo = h * d
            g = vg_ref[rs, D_RET + lo:D_RET + lo + d].astype(F32)
            gain_h = gain_ref[:, lo:lo + d]
            dy = dmix[rs, D_LRU + lo:D_LRU + lo + d]
            sg = _sigmoid(g)
            o_h = o_ref[rs, lo:lo + d]
            oc = o_h - jnp.mean(o_h, axis=-1, keepdims=True)
            rstd = lax.rsqrt(jnp.mean(oc * oc, axis=-1, keepdims=True) + NORM_EPS)
            on = oc * rstd
            at = base + 3 * D_RET + lo
            dp_ref[rs, at:at + d] = (dy * on * gain_h * (sg * (1.0 + g * (1.0 - sg)))).astype(dp_ref.dtype)
            don_g = dy * (g * sg)
            dgain_ref[:, lo:lo + d] += _colsum8(don_g * on)
            don = don_g * gain_h
            keep["do"] = rstd * (don - jnp.mean(don, axis=-1, keepdims=True)
                                 - on * jnp.mean(don * on, axis=-1, keepdims=True))

        def retain(s, h, keep):
            rs = slice(s * c, (s + 1) * c)
            cos2, sin_s = cos_ref[rs, :], sin_ref[rs, :]
            lo = h * d
            q = _rope(qk_ref[rs, lo:lo + d].astype(F32), cos2, sin_s)
            k = _rope(qk_ref[rs, D_RET + lo:D_RET + lo + d].astype(F32), cos2, sin_s) * RET_SCALE
            v = vg_ref[rs, lo:lo + d]
            xi_h, zeta_h, dec = xi_ref[:, lo:lo + d], zeta_ref[:, lo:lo + d], dec_ref[h]
            do = keep["do"]
            s_prev = st_ref[s, h]
            g_next = cur[h]
            p = _dot_nt(q, k) * dec
            dpm = _dot_nt(do, v) * dec
            keep["dq"] = _dot(dpm, k) + _dot_nt(do, s_prev) * xi_h
            keep["dk"] = _dot_tn(dpm, q) + _dot_nt(v, g_next) * zeta_h
            dv = _dot_tn(p, do) + _dot(k * zeta_h, g_next)
            cur[h] = g_next * g_chunk[h] + _dot_tn(q * xi_h, do)
            at = base + 2 * D_RET + lo
            dp_ref[rs, at:at + d] = dv.astype(dp_ref.dtype)

        def unrope(s, h, keep):
            rs = slice(s * c, (s + 1) * c)
            cos2, sin_s = cos_ref[rs, :], sin_ref[rs, :]
            lo = h * d
            dp_ref[rs, base + lo:base + lo + d] = _rope_bwd(keep["dq"], cos2, sin_s).astype(dp_ref.dtype)
            at = base + D_RET + lo
            dp_ref[rs, at:at + d] = _rope_bwd(keep["dk"] * RET_SCALE, cos2, sin_s).astype(dp_ref.dtype)

        def end():
            for h in range(nh):
                gstate[h] = cur[h]

        pieces = [start]
        for s in reversed(range(per)):
            for h in range(nh):
                keep = {}
                pieces += [lambda s=s, h=h, keep=keep, f=f: f(s, h, keep) for f in (gate_and_norm, retain, unrope)]
        return pieces + [end]

    rev = lambda col: (lambda i: (n_steps - 1 - i, col))
    full2 = lambda i: (0, 0)
    return dict(units=units, grid=(n_steps,),
                in_specs=[pl.BlockSpec((rows, 2 * D_RET), rev(1)), pl.BlockSpec((rows, 2 * D_RET), rev(2)),
                          pl.BlockSpec((rows, d), rev(0)), pl.BlockSpec((rows, d), rev(0)),
                          pl.BlockSpec((nh, c, c), lambda i: (0, 0, 0)), pl.BlockSpec((c, D_RET), full2),
                          pl.BlockSpec((c, D_RET), full2), pl.BlockSpec((1, D_RET), full2),
                          pl.BlockSpec((rows, D_RET), rev(0)),
                          pl.BlockSpec((per, nh, d, d), lambda i: (n_steps - 1 - i, 0, 0, 0))],
                out_specs=[pl.BlockSpec((rows, D_IN), rev(0)), pl.BlockSpec((SUBLANES, D_RET), full2)],
                out_shape=[jax.ShapeDtypeStruct((t, D_IN), MXU_DTYPE), jax.ShapeDtypeStruct((SUBLANES, D_RET), F32)],
                scratch_shapes=[pltpu.VMEM((nh, d, d), F32)],
                operands=[proj, proj, cos2, sin_signed, decay, xi, zeta, gain, o, states])


LRU_ACC = {"conv_w": 0, "conv_b": LRU_CONV, "gate_a_b": LRU_CONV + 1, "gate_x_b": LRU_CONV + 2,
           "lambda": LRU_CONV + 3, "norm_gain": LRU_CONV + 4}
LRU_ACC_ROWS = SUBLANES * (LRU_CONV + 5)


def _lru_bwd(proj, xc_all, h_all, conv_w, wa, ba, wx, bx, lam, gain, dproj_part, dmix_at):
    t = proj.shape[0]
    tm = min(MIX_ROW_TILE, t)
    c = D_LRU
    ni = t // tm

    def body(x_ref, xh_ref, g_ref, xc_ref, h_ref, hh_ref, cw_ref, wa_ref, ba_ref, wx_ref, bx_ref, lam_ref,
             gain_ref, acc_ref, dwa_ref, dwx_ref, a_scr, b_scr, mu_scr, carry_mu, carry_dxc, ctx):
        dp_ref = ctx["outs"][dproj_part][0]
        dmix = ctx["scratch"][dmix_at[0]][dmix_at[1]]
        fill = ctx["fill"]
        i = pl.program_id(0)
        r = ni - 1 - i

        @pl.when(i == 0)
        def _():
            acc_ref[...] = jnp.zeros_like(acc_ref)
            dwa_ref[...] = jnp.zeros_like(dwa_ref)
            dwx_ref[...] = jnp.zeros_like(dwx_ref)
            carry_mu[...] = jnp.zeros_like(carry_mu)
            carry_dxc[...] = jnp.zeros_like(carry_dxc)

        def add(name, val, k=0):
            lo = (LRU_ACC[name] + k) * SUBLANES
            acc_ref[lo:lo + SUBLANES, :] += _colsum8(val)

        fill()
        xc, h = xc_ref[...], h_ref[...]
        lam_v = lam_ref[...]
        sp = _softplus(-lam_v)
        rg, ig, a, m = _lru_gates(xc, wa_ref[...], ba_ref[...], wx_ref[...], bx_ref[...], sp, fill)
        gl, dgl = _gelu_parts(g_ref[...].astype(F32))
        fill()
        zn, rstd, _ = _rms_fwd(h * gl, gain_ref[...])
        dy = dmix[:, :c]
        dz, dgain = _rms_bwd(dy, zn, rstd, gain_ref[...])
        lo = LRU_ACC["norm_gain"] * SUBLANES
        acc_ref[lo:lo + SUBLANES, :] += dgain
        dp_ref[:, c:2 * c] = (dz * h * dgl).astype(dp_ref.dtype)
        dh = dz * gl
        fill()
        ga, gb = _group_scan(a, a * dh, reverse=True, fill=fill)
        a_scr[...] = ga
        b_scr[...] = gb
        mu_next_tile = carry_mu[...]
        carry_mu[...] = _carry_scan(a_scr, b_scr, mu_scr, mu_next_tile, reverse=True)
        fill()
        lam_t = dh + _shift_up(mu_scr[...], mu_next_tile, 1)
        h_prev = _shift_down(jnp.where(r == 0, 0.0, hh_ref[...]), h, 1)
        da = lam_t * h_prev
        dig = lam_t * m * xc
        dxc = lam_t * m * ig
        dlog_a = da * a - (lam_t * ig * xc) * (a * a) / m
        fill()
        dpr = dlog_a * ((-LRU_C) * sp) * rg * (1.0 - rg)
        add("lambda", dlog_a * ((-LRU_C) * rg) * (-_sigmoid(-lam_v)))
        dpi = dig * ig * (1.0 - ig)
        add("gate_a_b", dpr)
        add("gate_x_b", dpi)
        fill()
        dwa_ref[...] += _dot_tn(xc, dpr)
        dwx_ref[...] += _dot_tn(xc, dpi)
        dxc = dxc + _dot_nt(dpr, wa_ref[...]) + _dot_nt(dpi, wx_ref[...])
        fill()
        add("conv_b", dxc)
        x = x_ref[...].astype(F32)
        prev = jnp.where(r == 0, 0.0, xh_ref[...].astype(F32)[-SUBLANES:, :])
        cw = cw_ref[...]
        nxt = carry_dxc[...]
        carry_dxc[...] = dxc[:SUBLANES, :]
        dx = cw[LRU_CONV - 1:LRU_CONV, :] * dxc
        for k in range(LRU_CONV - 1):
            dx = dx + cw[k:k + 1, :] * _shift_up(dxc, nxt, LRU_CONV - 1 - k)
        fill()
        for k in range(LRU_CONV):
            add("conv_w", dxc * _shift_down(prev, x, LRU_CONV - 1 - k), k)
        dp_ref[:, :c] = dx.astype(dp_ref.dtype)

    hb = _halo_rows(proj.dtype)
    rev = lambda col: (lambda i: (ni - 1 - i, col))
    halo = lambda rows: (lambda i: (jnp.maximum((ni - 1 - i) * (tm // rows) - 1, 0), 0))
    full = lambda i: (0, 0)
    vec = pl.BlockSpec((1, c), full)
    mat = pl.BlockSpec((c, c), full)
    return dict(body=body, grid=(ni,), fill_points=12,
                in_specs=[pl.BlockSpec((tm, c), rev(0)), pl.BlockSpec((hb, c), halo(hb)), pl.BlockSpec((tm, c), rev(1)),
                          pl.BlockSpec((tm, c), rev(0)), pl.BlockSpec((tm, c), rev(0)),
                          pl.BlockSpec((SUBLANES, c), halo(SUBLANES)),
                          pl.BlockSpec((LRU_CONV, c), full), mat, vec, mat, vec, vec, vec],
                out_specs=[pl.BlockSpec((LRU_ACC_ROWS, c), full), mat, mat],
                out_shape=[jax.ShapeDtypeStruct((LRU_ACC_ROWS, c), F32), jax.ShapeDtypeStruct((c, c), F32),
                           jax.ShapeDtypeStruct((c, c), F32)],
                scratch_shapes=[pltpu.VMEM((tm, c), F32), pltpu.VMEM((tm, c), F32), pltpu.VMEM((tm, c), F32),
                                pltpu.VMEM((SUBLANES, c), F32), pltpu.VMEM((SUBLANES, c), F32)],
                operands=[proj, proj, proj, xc_all, h_all, h_all, conv_w, wa, ba, wx, bx, lam, gain])


def _mix_proj_bwd(dh1, dh1_b, w_out, w_in_blocks, x, g1, dproj_part):
    t = x.shape[0]
    tm = min(MIX_ROW_TILE, t)
    ni = t // tm
    nb, _, cb = w_in_blocks.shape
    first_free = -(-2 * D_LRU // cb)
    du = [None]

    def term(dp_ref, w_ref, d):
        part = _dot_nt(dp_ref[:, d * cb:(d + 1) * cb], w_ref[d])
        du[0] = part if du[0] is None else du[0] + part

    def head(dh_ref, dhb_ref, wo_ref, wi_ref, x_ref, g_ref, gx_ref, dg_ref, dmix, ctx):
        @pl.when(pl.program_id(0) == 0)
        def _():
            dg_ref[...] = jnp.zeros_like(dg_ref)
        dmix[...] = _dot_nt(dhb_ref[...], wo_ref[...])
        du[0] = None

    def units(dh_ref, dhb_ref, wo_ref, wi_ref, x_ref, g_ref, gx_ref, dg_ref, dmix, ctx):
        dp_ref = ctx["outs"][dproj_part][0]
        return [lambda d=d: term(dp_ref, wi_ref, d) for d in range(first_free, nb)]

    def tail(dh_ref, dhb_ref, wo_ref, wi_ref, x_ref, g_ref, gx_ref, dg_ref, dmix, ctx):
        dp_ref = ctx["outs"][dproj_part][0]
        for d in range(first_free):
            term(dp_ref, wi_ref, d)
        n, rstd, _ = _rms_fwd(x_ref[...], g_ref[...])
        dx, dg = _rms_bwd(du[0], n, rstd, g_ref[...])
        dg_ref[...] += dg
        gx_ref[...] = dx + dh_ref[...]

    row = lambda i: (ni - 1 - i, 0)
    const = lambda i: (0, 0)
    tile = pl.BlockSpec((tm, D_MODEL), row)
    return dict(head=head, units=units, tail=tail, grid=(ni,),
                in_specs=[tile, tile, _resident(w_out.shape), _resident(w_in_blocks.shape), tile,
                          pl.BlockSpec((1, D_MODEL), const)],
                out_specs=[tile, pl.BlockSpec((SUBLANES, D_MODEL), const)],
                out_shape=[jax.ShapeDtypeStruct((t, D_MODEL), F32), jax.ShapeDtypeStruct((SUBLANES, D_MODEL), F32)],
                scratch_shapes=[pltpu.VMEM((tm, D_MODEL), F32)],
                operands=[dh1, dh1_b, w_out, w_in_blocks, x, g1])


def _pair_sum(core, a, b, name):
    n, r, c = b.shape
    spec = pl.BlockSpec((None, r, c), lambda q, core: (q, 0, 0))

    def body(core_ref, a_ref, b_ref, o_ref):
        o_ref[...] = (a_ref[...].astype(F32) + b_ref[...].astype(F32)).astype(o_ref.dtype)

    return pl.pallas_call(
        body, name=name,
        grid_spec=pltpu.PrefetchScalarGridSpec(
            num_scalar_prefetch=1, grid=(n,),
            in_specs=[pl.BlockSpec((None, r, c), lambda q, core: (2 * q + core[0], 0, 0)), spec], out_specs=spec),
        out_shape=jax.ShapeDtypeStruct(b.shape, b.dtype),
        compiler_params=pltpu.CompilerParams(dimension_semantics=("arbitrary",), vmem_limit_bytes=VMEM_LIMIT),
    )(core, a, b)


ADAMW_BLOCK_BYTES = 4 * 1024 * 1024


def _sum_adamw(parts, w, m, v, name):
    n_parts, r, c = parts.shape
    tr = r
    while n_parts * tr * c * parts.dtype.itemsize > ADAMW_BLOCK_BYTES and tr % (4 * SUBLANES) == 0:
        tr //= 2

    def body(p_ref, w_ref, m_ref, v_ref, g_ref, d_ref, nm_ref, nv_ref):
        g = p_ref[0].astype(F32)
        for s in range(1, n_parts):
            g = g + p_ref[s].astype(F32)
        nm = ADAM_B1 * m_ref[...] + (1.0 - ADAM_B1) * g
        nv = ADAM_B2 * v_ref[...] + (1.0 - ADAM_B2) * (g * g)
        m_hat = nm / (1.0 - ADAM_B1 ** ADAM_STEP)
        v_hat = nv / (1.0 - ADAM_B2 ** ADAM_STEP)
        g_ref[...] = g
        d_ref[...] = -ADAM_LR * (m_hat / (jnp.sqrt(v_hat) + ADAM_EPS) + ADAM_WD * w_ref[...])
        nm_ref[...] = nm
        nv_ref[...] = nv

    row = pl.BlockSpec((tr, c), lambda i: (i, 0))
    return _call(body, name=name, grid=(r // tr,),
                 in_specs=[pl.BlockSpec((n_parts, tr, c), lambda i: (0, i, 0)), row, row, row],
                 out_specs=[row, row, row, row], out_shape=[jax.ShapeDtypeStruct((r, c), F32)] * 4,
                 operands=[parts, w, m, v])


MATRICES = ("w_in", "w_out", "ffn_up_w", "ffn_down_w")
CONVS = ("lru_conv_w", "ffn_conv_w")
REPLICATED = ("norm1_gain", "lru_conv_b", "lru_gate_a_w", "lru_gate_a_b", "lru_gate_x_w", "lru_gate_x_b", "lru_lambda",
              "lru_norm_gain", "ret_norm_gain", "norm2_gain", "ffn_conv_b", "final_norm_gain")
WEIGHTS = ("norm1_gain", "w_in", "lru_conv_w", "lru_conv_b", "lru_gate_a_w", "lru_gate_a_b", "lru_gate_x_w",
           "lru_gate_x_b", "lru_lambda", "lru_norm_gain", "ret_norm_gain", "w_out", "norm2_gain", "ffn_up_w",
           "ffn_conv_w", "ffn_conv_b", "ffn_down_w", "final_norm_gain")


def _rows(a, pad_to):
    a = a.reshape(-1, LANES)
    pad = (-a.shape[0]) % pad_to
    return jnp.pad(a, ((0, pad), (0, 0))) if pad else a


def _pack(arrays, pad_to):
    rows, layout, at = [], [], 0
    for a in arrays:
        r = _rows(a, pad_to)
        layout.append((at, a.size // LANES, a.shape))
        rows.append(r)
        at += r.shape[0]
    return jnp.concatenate(rows, axis=0), layout


def _unpack(packed, layout):
    lead = packed.shape[:-2]
    return [packed[..., at:at + n, :].reshape(lead + shape) for at, n, shape in layout]


def _conv_rows(lru, ffn, dtype, pad_to):
    lead = lru.shape[:-2]
    flat = jnp.concatenate([lru.reshape(lead + (-1,)), ffn.reshape(lead + (-1,))], axis=-1).astype(dtype)
    rows = flat.shape[-1] // LANES
    pad = (-rows) % pad_to
    return jnp.pad(flat.reshape(lead + (rows, LANES)), [(0, 0)] * len(lead) + [(0, pad), (0, 0)])


def _column_blocks(full):
    r, c = full.shape
    return full.reshape(r, N_DEV, c // N_DEV).transpose(1, 0, 2)


def _block_diag(w):
    nh, d, _ = w.shape
    eye = jnp.eye(nh, dtype=w.dtype)
    return (w[:, :, None, :] * eye[:, None, :, None]).reshape(nh * d, nh * d)


def _diag_blocks(dense, nh):
    d = dense.shape[0] // nh
    blocks = dense.reshape(nh, d, nh, d)
    return jnp.stack([blocks[h, :, h, :] for h in range(nh)], axis=0)


def kernel(x, norm1_gain, w_in, lru_conv_w, lru_conv_b, lru_gate_a_w, lru_gate_a_b, lru_gate_x_w, lru_gate_x_b, lru_lambda, lru_norm_gain, ret_norm_gain, w_out, norm2_gain, ffn_up_w, ffn_conv_w, ffn_conv_b, ffn_down_w, final_norm_gain, loss_target, m_norm1_gain, m_w_in, m_lru_conv_w, m_lru_conv_b, m_lru_gate_a_w, m_lru_gate_a_b, m_lru_gate_x_w, m_lru_gate_x_b, m_lru_lambda, m_lru_norm_gain, m_ret_norm_gain, m_w_out, m_norm2_gain, m_ffn_up_w, m_ffn_conv_w, m_ffn_conv_b, m_ffn_down_w, m_final_norm_gain, v_norm1_gain, v_w_in, v_lru_conv_w, v_lru_conv_b, v_lru_gate_a_w, v_lru_gate_a_b, v_lru_gate_x_w, v_lru_gate_x_b, v_lru_lambda, v_lru_norm_gain, v_ret_norm_gain, v_w_out, v_norm2_gain, v_ffn_up_w, v_ffn_conv_w, v_ffn_conv_b, v_ffn_down_w, v_final_norm_gain):
    args = dict(locals())
    given = {n: args[n] for n in WEIGHTS}
    out_shape = {n: given[n].shape for n in WEIGHTS}

    def plain(a):
        return a.reshape(1, -1) if a.ndim <= 2 else a[0]

    w = {n: plain(given[n]) for n in WEIGHTS}
    mom_m = {n: plain(args["m_" + n]) for n in WEIGHTS}
    mom_v = {n: plain(args["v_" + n]) for n in WEIGHTS}
    x2, target = x[0], loss_target[0]
    t = x2.shape[0]
    core = lax.axis_index("c").astype(jnp.int32).reshape(1)
    res = {}

    conv_pad = _conv_rows(w["lru_conv_w"], w["ffn_conv_w"], F32, SUBLANES)
    first = _gather_first([w["w_in"].astype(MXU_DTYPE), conv_pad])
    w_in_blocks, conv_all = _run_comms([first, _gather_second(first.out_shape)], "w_in_all_gather")
    n_lru = w["lru_conv_w"].size
    conv_flat = conv_all.reshape(N_DEV, -1)
    lru_cw = conv_flat[:, :n_lru].reshape((N_DEV,) + w["lru_conv_w"].shape).transpose(1, 0, 2).reshape(LRU_CONV, D_LRU)
    ffn_cw = conv_flat[:, n_lru:n_lru + w["ffn_conv_w"].size].reshape((N_DEV,) + w["ffn_conv_w"].shape)
    ffn_cw = ffn_cw.transpose(1, 0, 2).reshape(FFN_CONV, 2 * D_FF)

    cos2, sin_signed = _rope_tables(t)
    wa = _block_diag(w["lru_gate_a_w"]).astype(MXU_DTYPE)
    wx = _block_diag(w["lru_gate_x_w"]).astype(MXU_DTYPE)
    gf = w["final_norm_gain"]

    early = _gather_first([w["w_out"].astype(MXU_DTYPE), w["ffn_down_w"].astype(MXU_DTYPE)])
    (u1, proj), (w_out_part, down_part) = _inproj_fwd(x2, w["norm1_gain"], w_in_blocks, early)
    ((xc, h_lru, y_lru), (o_ret, y_ret, states)), (w_out_blocks, down_blocks, up_part) = _fused(
        [_lru_fwd(proj, lru_cw, w["lru_conv_b"], wa, w["lru_gate_a_b"], wx, w["lru_gate_x_b"], w["lru_lambda"],
                  w["lru_norm_gain"]),
         _ret_fwd(proj, cos2, sin_signed, w["ret_norm_gain"])],
        "mix_fwd", _both(_gather_second([w_out_part, down_part]), _gather_first([w["ffn_up_w"].astype(MXU_DTYPE)])))
    w_out_full = w_out_blocks.reshape(D_MODEL, D_MODEL)
    w_down_full = down_blocks.reshape(D_FF, D_MODEL)

    (h1, u2), (up_blocks,) = _outproj_fwd(x2, y_lru, y_ret, w_out_full, w["norm2_gain"], _gather_second([up_part]))
    up_a, up_v, conv_a, conv_v, act, dh2, dh2_b, dgf, loss_local = _ffn_fwd(u2, up_blocks, ffn_cw, w["ffn_conv_b"],
                                                                            w_down_full, h1, gf, target)

    def to_owner_chips(blocks, names, tag):
        theirs = _run_comms([_pair_exchange(blocks)], "grads_pair_exchange_" + tag)
        return [_pair_sum(core, a, b, "grads_pair_sum_" + n) for n, a, b in zip(names, blocks, theirs)]

    def adamw(name, parts):
        res[name] = _sum_adamw(parts, w[name], mom_m[name], mom_v[name], "adamw_" + name)

    g = {"final_norm_gain": dgf[0]}
    dup_a, dup_v, acc_a, acc_v, dh1, dh1_b, dg2 = _ffn_bwd(
        dh2, dh2_b, w_down_full, up_a, up_v, conv_a, conv_v, ffn_cw, up_blocks, h1, w["norm2_gain"], None)
    per_col = lambda a: a[:, ::SUBLANES].transpose(1, 0, 2).reshape(FFN_CONV + 1, D_FF)
    acc = jnp.concatenate([per_col(acc_a), per_col(acc_v)], axis=1)
    g_ffn_cw, g["ffn_conv_b"] = acc[:FFN_CONV], acc[FFN_CONV:]
    g["norm2_gain"] = dg2[:1]
    g_up = jnp.concatenate([_mm_tn(u2, dup_a, "ffn_up_wgrad_a", blocks=N_DEV // 2),
                            _mm_tn(u2, dup_v, "ffn_up_wgrad_v", blocks=N_DEV // 2)], axis=0)
    up_sums = to_owner_chips([g_up], ["ffn_up_w"], "up")
    g_down, (up_parts,) = _mm_tn(act, dh2_b, "ffn_down_wgrad", comm=_chip_exchange(up_sums))
    adamw("ffn_up_w", up_parts)
    g_out = jnp.concatenate([_mm_tn(y_lru, dh1_b, "w_out_wgrad_lru"), _mm_tn(y_ret, dh1_b, "w_out_wgrad_ret")], axis=0)
    low_sums = to_owner_chips([g_down.reshape(N_DEV, D_FF // N_DEV, D_MODEL),
                               g_out.reshape(N_DEV, D_MODEL // N_DEV, D_MODEL)], ["ffn_down_w", "w_out"], "low")
    (dproj, dgain_ret), (grad_x, dg1), (lru_acc, dwa, dwx) = _fused(
        [_ret_bwd(proj, cos2, sin_signed, w["ret_norm_gain"], o_ret, states, dmix_at=(1, 0)),
         _mix_proj_bwd(dh1, dh1_b, w_out_full, w_in_blocks, x2, w["norm1_gain"], dproj_part=0),
         _lru_bwd(proj, xc, h_lru, lru_cw, wa, w["lru_gate_a_b"], wx, w["lru_gate_x_b"], w["lru_lambda"],
                  w["lru_norm_gain"], dproj_part=0, dmix_at=(1, 0))],
        "mix_bwd")
    g["norm1_gain"] = dg1[:1]
    g["ret_norm_gain"] = dgain_ret[:1]
    lru_acc = lru_acc[::SUBLANES]
    g_lru_cw = lru_acc[:LRU_CONV]
    for name in ("conv_b", "gate_a_b", "gate_x_b", "lambda", "norm_gain"):
        g["lru_" + name] = lru_acc[LRU_ACC[name]:LRU_ACC[name] + 1]
    g["lru_gate_a_w"] = _diag_blocks(dwa, LRU_HEADS)
    g["lru_gate_x_w"] = _diag_blocks(dwx, LRU_HEADS)
    rep_packed, rep_layout = _pack([g[n] for n in REPLICATED] + [loss_local], SUBLANES)
    g_in, (down_parts, out_parts, rep_part) = _mm_tn(u1, dproj, "w_in_wgrad", blocks=N_DEV,
                                                     comm=_both(_chip_exchange(low_sums), _gather_first([rep_packed])))
    adamw("ffn_down_w", down_parts)
    adamw("w_out", out_parts)
    g_conv = _conv_rows(_column_blocks(g_lru_cw), _column_blocks(g_ffn_cw), GRAD_DTYPE, 2 * SUBLANES)
    in_sums = to_owner_chips([g_in, g_conv], ["w_in", "conv"], "in")
    in_parts, conv_parts, rep_parts = _run_comms([_both(_chip_exchange(in_sums), _gather_second([rep_part]))],
                                                 "last_grads_exchange")
    adamw("w_in", in_parts)
    pad16 = lambda d: _conv_rows(d["lru_conv_w"], d["ffn_conv_w"], F32, 2 * SUBLANES)
    conv_res = _sum_adamw(conv_parts, pad16(w), pad16(mom_m), pad16(mom_v), "adamw_conv")
    for n, lo, hi in (("lru_conv_w", 0, n_lru), ("ffn_conv_w", n_lru, n_lru + w["ffn_conv_w"].size)):
        res[n] = [r.reshape(-1)[lo:hi].reshape(w[n].shape) for r in conv_res]
    no_state = jnp.zeros_like(loss_local)
    rep_res = _sum_adamw(rep_parts, *[_pack([d[n] for n in REPLICATED] + [no_state], SUBLANES)[0]
                                      for d in (w, mom_m, mom_v)], "adamw_replicated")
    for k in range(4):
        for n, a in zip(REPLICATED, _unpack(rep_res[k], rep_layout)):
            res.setdefault(n, [None] * 4)[k] = a
    loss = _unpack(rep_res[0], rep_layout)[-1][0, 0]

    outs = [loss, grad_x[None]]
    for k in range(4):
        outs += [res[n][k].reshape(out_shape[n]) for n in WEIGHTS]
    return tuple(outs)
```

```python
import math

import numpy as np
import jax
import jax.numpy as jnp
from jax import lax
from jax.experimental import pallas as pl
from jax.experimental.pallas import tpu as pltpu

F32 = jnp.float32
BF16 = jnp.bfloat16
MXU_DTYPE = jnp.bfloat16
GRAD_DTYPE = jnp.bfloat16

N_DEV = 8
N_CHIPS = 4
D_MODEL = 1024
D_LRU = 512
LRU_HEADS = 8
LRU_CONV = 4
LRU_C = 8.0
D_RET = 512
RET_HEADS = 4
RET_HEAD_DIM = 128
RET_CHUNK = 128
ROPE_BASE = 10000.0
D_IN = 3072
D_FF = 3072
FFN_CONV = 3
NORM_EPS = 1e-6

ADAM_LR = 0.001
ADAM_B1 = 0.9
ADAM_B2 = 0.999
ADAM_EPS = 1e-08
ADAM_WD = 0.01
ADAM_STEP = 10

SUBLANES = 8
LANES = 128
VMEM_LIMIT = 48 * 1024 * 1024
FUSED_VMEM_LIMIT = VMEM_LIMIT
FFN_BWD_VMEM_LIMIT = 56 * 1024 * 1024

ROW_TILE = 256
MIX_ROW_TILE = 256
PROJ_ROW_TILE = 512
WGRAD_ROWS = 2048
WGRAD_TILE = 1024
WGRAD_BLOCK_COLUMNS = 768

MESH = pl.DeviceIdType.MESH
ANY = pl.BlockSpec(memory_space=pl.ANY)


def _dot(a, b):
    return jnp.dot(a.astype(MXU_DTYPE), b.astype(MXU_DTYPE), preferred_element_type=F32)


def _dot_nt(a, b):
    return lax.dot_general(a.astype(MXU_DTYPE), b.astype(MXU_DTYPE), (((1,), (1,)), ((), ())),
                           preferred_element_type=F32)


def _dot_tn(a, b):
    return lax.dot_general(a.astype(MXU_DTYPE), b.astype(MXU_DTYPE), (((0,), (0,)), ((), ())),
                           preferred_element_type=F32)


def _sigmoid(x):
    return 0.5 + 0.5 * jnp.tanh(0.5 * x)


_GELU_C = math.sqrt(2.0 / math.pi)
_GELU_C3 = _GELU_C * 0.044715


def _gelu_parts(x):
    x2 = x * x
    t = jnp.tanh(x * (_GELU_C + _GELU_C3 * x2))
    cdf = 0.5 + 0.5 * t
    g = x * cdf
    dg = cdf + (0.5 * x) * (1.0 - t * t) * (_GELU_C + (3.0 * _GELU_C3) * x2)
    return g, dg


def _gelu(x):
    t = jnp.tanh(_GELU_C * (x + 0.044715 * (x * x * x)))
    return x * (0.5 * (1.0 + t))


def _softplus(x):
    return jnp.maximum(x, 0.0) + jnp.log1p(jnp.exp(-jnp.abs(x)))


def _bcast_row(x, r, rows=SUBLANES):
    return jnp.broadcast_to(x[r:r + 1, :], (rows, x.shape[1]))


def _colsum8(x):
    return jnp.broadcast_to(jnp.sum(x, axis=0, keepdims=True), (SUBLANES, x.shape[1]))


def _groups(x):
    return x.reshape(x.shape[0] // SUBLANES, SUBLANES, x.shape[1])


def _shift_down(prev8, tile, s):
    if s == 0:
        return tile
    own = pltpu.roll(_groups(tile), s, 1)
    before = jnp.concatenate([pltpu.roll(_groups(prev8), s, 1), own[:-1]], axis=0)
    row = lax.broadcasted_iota(jnp.int32, own.shape, 1)
    return jnp.where(row >= s, own, before).reshape(tile.shape)


def _shift_up(tile, next8, s):
    if s == 0:
        return tile
    own = pltpu.roll(_groups(tile), SUBLANES - s, 1)
    after = jnp.concatenate([own[1:], pltpu.roll(_groups(next8), SUBLANES - s, 1)], axis=0)
    row = lax.broadcasted_iota(jnp.int32, own.shape, 1)
    return jnp.where(row < SUBLANES - s, own, after).reshape(tile.shape)


def _group_scan(a, b, reverse, fill=lambda: None):
    n, c = a.shape
    row = lax.broadcasted_iota(jnp.int32, a.shape, 0) & (SUBLANES - 1)

    def within_group(x, shift):
        return pltpu.roll(x.reshape(n // SUBLANES, SUBLANES, c), shift, 1).reshape(n, c)

    for s in (1, 2, 4):
        if s > 1:
            fill()
        shift = (SUBLANES - s) if reverse else s
        a_sh = within_group(a, shift)
        b_sh = within_group(b, shift)
        m = (row <= SUBLANES - 1 - s) if reverse else (row >= s)
        b = jnp.where(m, a * b_sh + b, b)
        a = jnp.where(m, a * a_sh, a)
    return a, b


def _carry_scan(a_ref, b_ref, out_ref, carry0, reverse):
    n_groups = a_ref.shape[0] // SUBLANES
    carry = carry0
    for i in range(n_groups):
        r0 = ((n_groups - 1 - i) if reverse else i) * SUBLANES
        hg = a_ref[r0:r0 + SUBLANES, :] * carry + b_ref[r0:r0 + SUBLANES, :]
        out_ref[r0:r0 + SUBLANES, :] = hg
        carry = _bcast_row(hg, 0 if reverse else SUBLANES - 1)
    return carry


def _rms_fwd(h, gain):
    rstd = lax.rsqrt(jnp.mean(h * h, axis=-1, keepdims=True) + NORM_EPS)
    n = h * rstd
    return n, rstd, n * gain


def _rms_bwd(dy, n, rstd, gain):
    dn = dy * gain
    dh = rstd * (dn - n * jnp.mean(dn * n, axis=-1, keepdims=True))
    return dh, _colsum8(dy * n)


def _halo_rows(dtype):
    return SUBLANES * (4 // jnp.dtype(dtype).itemsize)


def _halo_map(tile_rows, col, halo_rows=SUBLANES):
    per = tile_rows // halo_rows
    return lambda i: (jnp.maximum(i * per - 1, 0), col)


def _resident(shape):
    return pl.BlockSpec(shape, lambda *_: (0,) * len(shape), pipeline_mode=pl.Buffered(1))


def _place():
    x, y, c = lax.axis_index("x"), lax.axis_index("y"), lax.axis_index("c")
    chips = [(1 - x, y), (x, 1 - y), (1 - x, 1 - y)]
    return x, y, c, chips


def _dev(x, y, c):
    return 4 * x + 2 * y + c


class _Copy:
    def __init__(self, make):
        self.make = make

    def start(self):
        self.make().start()

    def wait(self):
        self.make().wait()

    def wait_send(self):
        self.make().wait_send()

    def wait_recv(self):
        self.make().wait_recv()


def _remote(src, dst, send_sem, recv_sem, to):
    return _Copy(lambda: pltpu.make_async_remote_copy(src_ref=src, dst_ref=dst, send_sem=send_sem, recv_sem=recv_sem,
                                                      device_id=to, device_id_type=MESH))


def _local(src, dst, sem):
    return _Copy(lambda: pltpu.make_async_copy(src, dst, sem))


class _Comm:
    def __init__(self, operands, out_shape, sems, descs, aliases=()):
        self.operands, self.out_shape, self.sems, self.descs, self.aliases = operands, out_shape, sems, descs, aliases

    def start(self, ins, outs, sems):
        local, sends, _ = self.descs(ins, outs, sems)
        for cp in sends + local:
            cp.start()

    def wait(self, ins, outs, sems):
        local, sends, recvs = self.descs(ins, outs, sems)
        for cp in recvs:
            cp.wait_recv()
        for cp in sends:
            cp.wait_send()
        for cp in local:
            cp.wait()


def _gather_first(shards):
    n = len(shards)

    def descs(ins, outs, sems):
        send, recv, loc = sems
        x, y, c, chips = _place()
        me = _dev(x, y, c)
        targets = [(x, y, 1 - c)] + [(*chip, c) for chip in chips]
        local, sends, recvs = [], [], []
        for t in range(n):
            local.append(_local(ins[t], outs[t].at[me], loc.at[t]))
            for k, to in enumerate(targets):
                i = 4 * t + k
                sends.append(_remote(ins[t], outs[t].at[me], send.at[i], recv.at[i], to))
                recvs.append(_remote(ins[t], outs[t].at[_dev(*to)], send.at[i], recv.at[i], to))
        return local, sends, recvs

    return _Comm(list(shards), [jax.ShapeDtypeStruct((N_DEV,) + s.shape, s.dtype) for s in shards],
                 [pltpu.SemaphoreType.DMA((4 * n,)), pltpu.SemaphoreType.DMA((4 * n,)), pltpu.SemaphoreType.DMA((n,))],
                 descs)


def _gather_second(gathered):
    n = len(gathered)

    def descs(ins, outs, sems):
        send, recv = sems
        x, y, c, chips = _place()
        sends, recvs = [], []
        for t in range(n):
            for j, chip in enumerate(chips):
                i = 3 * t + j
                have, get = _dev(*chip, c), _dev(*chip, 1 - c)
                sends.append(_remote(outs[t].at[have], outs[t].at[have], send.at[i], recv.at[i], (x, y, 1 - c)))
                recvs.append(_remote(outs[t].at[have], outs[t].at[get], send.at[i], recv.at[i], (x, y, 1 - c)))
        return [], sends, recvs

    return _Comm(list(gathered), [jax.ShapeDtypeStruct(g.shape, g.dtype) for g in gathered],
                 [pltpu.SemaphoreType.DMA((3 * n,)), pltpu.SemaphoreType.DMA((3 * n,))], descs,
                 aliases=[(t, t) for t in range(n)])


def _pair_exchange(blocks):
    n = len(blocks)

    def descs(ins, outs, sems):
        send, recv = sems
        x, y, c, _ = _place()
        sends, recvs = [], []
        for t in range(n):
            for q in range(N_CHIPS):
                i = N_CHIPS * t + q
                cp = _remote(ins[t].at[2 * q + 1 - c], outs[t].at[q], send.at[i], recv.at[i], (x, y, 1 - c))
                sends.append(cp)
                recvs.append(cp)
        return [], sends, recvs

    return _Comm(list(blocks), [jax.ShapeDtypeStruct((N_CHIPS,) + b.shape[1:], b.dtype) for b in blocks],
                 [pltpu.SemaphoreType.DMA((N_CHIPS * n,)), pltpu.SemaphoreType.DMA((N_CHIPS * n,))], descs)


def _chip_exchange(blocks):
    n = len(blocks)

    def descs(ins, outs, sems):
        send, recv, loc = sems
        x, y, c, chips = _place()
        me = 2 * x + y
        local, sends, recvs = [], [], []
        for t in range(n):
            local.append(_local(ins[t].at[me], outs[t].at[me], loc.at[t]))
            for j, (px, py) in enumerate(chips):
                i = 3 * t + j
                q = 2 * px + py
                sends.append(_remote(ins[t].at[q], outs[t].at[me], send.at[i], recv.at[i], (px, py, c)))
                recvs.append(_remote(ins[t].at[q], outs[t].at[q], send.at[i], recv.at[i], (px, py, c)))
        return local, sends, recvs

    return _Comm(list(blocks), [jax.ShapeDtypeStruct(b.shape, b.dtype) for b in blocks],
                 [pltpu.SemaphoreType.DMA((3 * n,)), pltpu.SemaphoreType.DMA((3 * n,)), pltpu.SemaphoreType.DMA((n,))],
                 descs)


def _both(a, b):
    na, oa, sa = len(a.operands), len(a.out_shape), len(a.sems)

    def descs(ins, outs, sems):
        local_a, sends_a, recvs_a = a.descs(ins[:na], outs[:oa], sems[:sa])
        local_b, sends_b, recvs_b = b.descs(ins[na:], outs[oa:], sems[sa:])
        return local_a + local_b, sends_a + sends_b, recvs_a + recvs_b

    return _Comm(a.operands + b.operands, a.out_shape + b.out_shape, a.sems + b.sems, descs,
                 aliases=list(a.aliases) + [(na + i, oa + o) for i, o in b.aliases])


def _run_comms(comms, name):
    first = comms[0]
    n_in, n_out = len(first.operands), len(first.out_shape)

    def body(*refs):
        ins, outs, sems = refs[:n_in], refs[n_in:n_in + n_out], list(refs[n_in + n_out:])
        for k, comm in enumerate(comms):
            mine = [sems.pop(0) for _ in comm.sems]
            comm.start(ins if k == 0 else outs, outs, mine)
            comm.wait(ins if k == 0 else outs, outs, mine)

    outs = pl.pallas_call(
        body, name=name, out_shape=first.out_shape, in_specs=[ANY] * n_in, out_specs=[ANY] * n_out,
        scratch_shapes=[s for comm in comms for s in comm.sems], input_output_aliases=dict(first.aliases),
    )(*first.operands)
    return list(outs)


def _call(body, *, name, grid, in_specs, out_specs, out_shape, operands, scratch_shapes=(), comm=None, aliases=None,
          vmem_limit=VMEM_LIMIT):
    sem = ("arbitrary",) * len(grid)
    params = pltpu.CompilerParams(dimension_semantics=sem, vmem_limit_bytes=vmem_limit)
    aliases = dict(aliases or {})
    if comm is None:
        return pl.pallas_call(body, name=name, grid=grid, in_specs=in_specs, out_specs=out_specs, out_shape=out_shape,
                              scratch_shapes=list(scratch_shapes), input_output_aliases=aliases,
                              compiler_params=params)(*operands)
    n_in, n_out, n_scr = len(in_specs), len(out_specs), len(scratch_shapes)
    c_in, c_out = len(comm.operands), len(comm.out_shape)

    def wrapped(*refs):
        refs = list(refs)
        ins, refs = refs[:n_in], refs[n_in:]
        cins, refs = refs[:c_in], refs[c_in:]
        outs, refs = refs[:n_out], refs[n_out:]
        couts, refs = refs[:c_out], refs[c_out:]
        scr, csems = refs[:n_scr], refs[n_scr:]
        first = last = None
        for axis, size in enumerate(grid):
            at_first, at_last = pl.program_id(axis) == 0, pl.program_id(axis) == size - 1
            first = at_first if first is None else first & at_first
            last = at_last if last is None else last & at_last

        @pl.when(first)
        def _():
            comm.start(cins, couts, csems)

        body(*ins, *outs, *scr)

        @pl.when(last)
        def _():
            comm.wait(cins, couts, csems)

    res = pl.pallas_call(
        wrapped, name=name, grid=grid, in_specs=list(in_specs) + [ANY] * c_in, out_specs=list(out_specs) + [ANY] * c_out,
        out_shape=list(out_shape) + list(comm.out_shape), scratch_shapes=list(scratch_shapes) + list(comm.sems),
        input_output_aliases={**aliases, **{n_in + i: n_out + o for i, o in comm.aliases}}, compiler_params=params,
    )(*operands, *comm.operands)
    return list(res[:n_out]), list(res[n_out:])


def _mm_tn(a, b, name, blocks=1, comm=None):
    t, m = a.shape
    n = b.shape[1]
    tk = min(WGRAD_ROWS, t)
    nk = t // tk
    cb = n // blocks
    per = max(1, WGRAD_BLOCK_COLUMNS // cb) if blocks > 1 else 1
    tn = per * cb if blocks > 1 else min(WGRAD_TILE, n)
    tm = min(WGRAD_TILE, m)
    assert blocks == 1 or tm == m

    def body(a_ref, b_ref, o_ref, acc):
        k = pl.program_id(2)

        @pl.when(k == 0)
        def _():
            acc[...] = jnp.zeros_like(acc)
        acc[...] += _dot_tn(a_ref[...], b_ref[...])

        @pl.when(k == nk - 1)
        def _():
            if blocks == 1:
                o_ref[...] = acc[...].astype(o_ref.dtype)
            else:
                for s in range(per):
                    o_ref[s] = acc[:, s * cb:(s + 1) * cb].astype(o_ref.dtype)

    if blocks == 1:
        out_spec = pl.BlockSpec((tm, tn), lambda i, j, k: (i, j))
        out_shape = jax.ShapeDtypeStruct((m, n), GRAD_DTYPE)
    else:
        out_spec = pl.BlockSpec((per, m, cb), lambda i, j, k: (j, 0, 0))
        out_shape = jax.ShapeDtypeStruct((blocks, m, cb), GRAD_DTYPE)
    res = _call(body, name=name, grid=(m // tm, n // tn, nk), comm=comm,
                in_specs=[pl.BlockSpec((tk, tm), lambda i, j, k: (k, i)), pl.BlockSpec((tk, tn), lambda i, j, k: (k, j))],
                out_specs=[out_spec], out_shape=[out_shape], operands=[a, b],
                scratch_shapes=[pltpu.VMEM((tm, tn), F32)])
    return res[0] if comm is None else (res[0][0], res[1])


INPROJ_TN = 1024


def _inproj_fwd(x, g1, w_blocks, comm):
    t = x.shape[0]
    tm = min(PROJ_ROW_TILE, t)
    nb, _, cb = w_blocks.shape

    def body(x_ref, g_ref, w_hbm, u_ref, p_ref, w_all, sems):
        @pl.when(pl.program_id(0) == 0)
        def _():
            copies = [pltpu.make_async_copy(w_hbm.at[d], w_all.at[:, pl.ds(d * cb, cb)], sems.at[d]) for d in range(nb)]
            for cp in copies:
                cp.start()
            for cp in copies:
                cp.wait()

        _, _, u = _rms_fwd(x_ref[...], g_ref[...])
        u = u.astype(MXU_DTYPE)
        u_ref[...] = u
        for lo in range(0, D_IN, INPROJ_TN):
            p_ref[:, lo:lo + INPROJ_TN] = _dot(u, w_all[:, lo:lo + INPROJ_TN]).astype(p_ref.dtype)

    return _call(body, name="inproj_fwd", grid=(t // tm,), comm=comm,
                 in_specs=[pl.BlockSpec((tm, D_MODEL), lambda i: (i, 0)), pl.BlockSpec((1, D_MODEL), lambda i: (0, 0)), ANY],
                 out_specs=[pl.BlockSpec((tm, D_MODEL), lambda i: (i, 0)), pl.BlockSpec((tm, D_IN), lambda i: (i, 0))],
                 out_shape=[jax.ShapeDtypeStruct((t, D_MODEL), MXU_DTYPE), jax.ShapeDtypeStruct((t, D_IN), MXU_DTYPE)],
                 scratch_shapes=[pltpu.VMEM((D_MODEL, nb * cb), w_blocks.dtype), pltpu.SemaphoreType.DMA((nb,))],
                 operands=[x, g1, w_blocks])


def _lru_gates(xc, wa, ba, wx, bx, sp, fill=lambda: None):
    r = _sigmoid(_dot(xc, wa) + ba)
    fill()
    ig = _sigmoid(_dot(xc, wx) + bx)
    fill()
    log_a = (-LRU_C) * r * sp
    a = jnp.exp(log_a)
    m = jnp.sqrt(-jnp.tanh(log_a) * (a * a + 1.0))
    return r, ig, a, m


def _fused(parts, name, comm=None):
    grid = parts[0]["grid"]
    assert all(p["grid"] == grid for p in parts)
    counts = [(len(p["in_specs"]), len(p["out_specs"]), len(p.get("scratch_shapes", ()))) for p in parts]

    def body(*refs):
        refs = list(refs)
        groups = []
        for kind in range(3):
            taken = []
            for c in counts:
                taken.append(refs[:c[kind]])
                refs = refs[c[kind]:]
            groups.append(taken)
        ins, outs, scr = groups
        pending = []

        def fill(n=None):
            for _ in range(share if n is None else n):
                if pending:
                    pending.pop(0)()

        ctx = dict(outs=outs, scratch=scr, fill=fill)
        run = lambda key: [p[key](*ins[k], *outs[k], *scr[k], ctx) for k, p in enumerate(parts) if key in p]
        run("head")
        for pieces in run("units"):
            pending.extend(pieces)
        points = sum(p.get("fill_points", 0) for p in parts)
        share = -(-len(pending) // max(points, 1))
        run("body")
        fill(len(pending))
        run("tail")

    cat = lambda key: [x for p in parts for x in p.get(key, ())]
    res = _call(body, name=name, grid=grid, comm=comm, vmem_limit=FUSED_VMEM_LIMIT,
                in_specs=cat("in_specs"), out_specs=cat("out_specs"),
                out_shape=cat("out_shape"), scratch_shapes=cat("scratch_shapes"), operands=cat("operands"))
    outs, side = (res if comm is not None else (res, None))
    split, at = [], 0
    for _, n_out, _ in counts:
        split.append(list(outs[at:at + n_out]))
        at += n_out
    return split if comm is None else (split, side)


def _lru_fwd(proj, conv_w, conv_b, wa, ba, wx, bx, lam, gain):
    t = proj.shape[0]
    tm = min(MIX_ROW_TILE, t)
    c = D_LRU

    def body(x_ref, xh_ref, g_ref, cw_ref, cb_ref, wa_ref, ba_ref, wx_ref, bx_ref, lam_ref, gain_ref,
             xc_ref, h_ref, y_ref, a_scr, b_scr, carry, ctx):
        fill = ctx["fill"]
        i = pl.program_id(0)

        @pl.when(i == 0)
        def _():
            carry[...] = jnp.zeros_like(carry)

        fill()
        x = x_ref[...].astype(F32)
        prev = jnp.where(i == 0, 0.0, xh_ref[...].astype(F32)[-SUBLANES:, :])
        cw = cw_ref[...]
        xc = cb_ref[...] + cw[LRU_CONV - 1:LRU_CONV, :] * x
        for k in range(LRU_CONV - 1):
            xc = xc + cw[k:k + 1, :] * _shift_down(prev, x, LRU_CONV - 1 - k)
        xc_ref[...] = xc
        fill()
        sp = _softplus(-lam_ref[...])
        _, ig, a, m = _lru_gates(xc, wa_ref[...], ba_ref[...], wx_ref[...], bx_ref[...], sp, fill)
        fill()
        ga, gb = _group_scan(a, m * (ig * xc), reverse=False, fill=fill)
        a_scr[...] = ga
        b_scr[...] = gb
        fill()
        carry[...] = _carry_scan(a_scr, b_scr, h_ref, carry[...], reverse=False)
        fill()
        z = h_ref[...] * _gelu(g_ref[...].astype(F32))
        fill()
        _, _, y = _rms_fwd(z, gain_ref[...])
        y_ref[...] = y.astype(y_ref.dtype)

    row = lambda i: (i, 0)
    full = lambda i: (0, 0)
    vec = pl.BlockSpec((1, c), full)
    hb = _halo_rows(proj.dtype)
    return dict(body=body, grid=(t // tm,), fill_points=6,
                in_specs=[pl.BlockSpec((tm, c), row), pl.BlockSpec((hb, c), _halo_map(tm, 0, hb)),
                          pl.BlockSpec((tm, c), lambda i: (i, 1)),
                          pl.BlockSpec((LRU_CONV, c), full), vec, pl.BlockSpec((c, c), full), vec,
                          pl.BlockSpec((c, c), full), vec, vec, vec],
                out_specs=[pl.BlockSpec((tm, c), row), pl.BlockSpec((tm, c), row), pl.BlockSpec((tm, c), row)],
                out_shape=[jax.ShapeDtypeStruct((t, c), F32), jax.ShapeDtypeStruct((t, c), F32),
                           jax.ShapeDtypeStruct((t, c), MXU_DTYPE)],
                scratch_shapes=[pltpu.VMEM((tm, c), F32), pltpu.VMEM((tm, c), F32), pltpu.VMEM((SUBLANES, c), F32)],
                operands=[proj, proj, proj, conv_w, conv_b, wa, ba, wx, bx, lam, gain])


def _ret_consts():
    c = RET_CHUNK
    log_g = jnp.log1p(-jnp.exp2(-5.0 - jnp.arange(RET_HEADS, dtype=F32)))
    idx = jnp.arange(c, dtype=F32)
    diff = idx[:, None] - idx[None, :]
    decay = jnp.where(diff[None] >= 0, jnp.exp(jnp.maximum(diff, 0.0)[None] * log_g[:, None, None]), 0.0)
    zeta = jnp.exp((c - 1 - idx)[None, :] * log_g[:, None])
    xi = jnp.exp((idx + 1.0)[None, :] * log_g[:, None])
    spread = lambda v: jnp.repeat(v.T, RET_HEAD_DIM, axis=1)
    log_g_np = np.log1p(-np.exp2(-5.0 - np.arange(RET_HEADS, dtype=np.float32))).astype(np.float32)
    g_chunk = [float(np.exp(np.float32(c) * lg)) for lg in log_g_np]
    return decay, spread(xi), spread(zeta), g_chunk


def _rope_tables(t):
    pos = np.arange(t, dtype=np.float32)
    inv_freq = np.float32(ROPE_BASE) ** (-np.arange(0, RET_HEAD_DIM, 2, dtype=np.float32) / np.float32(RET_HEAD_DIM))
    ang = (pos[:, None] * inv_freq.astype(np.float32)[None, :]).astype(np.float32).astype(np.float64)
    cos, sin = np.cos(ang).astype(np.float32), np.sin(ang).astype(np.float32)
    return jnp.asarray(np.concatenate([cos, cos], axis=-1)), jnp.asarray(np.concatenate([-sin, sin], axis=-1))


def _rope(x, cos2, sin_signed):
    return x * cos2 + pltpu.roll(x, RET_HEAD_DIM // 2, 1) * sin_signed


def _rope_bwd(d, cos2, sin_signed):
    return d * cos2 + pltpu.roll(d * sin_signed, RET_HEAD_DIM // 2, 1)


RET_SCALE = RET_HEAD_DIM ** -0.5


RET_CHUNKS_PER_STEP = MIX_ROW_TILE // RET_CHUNK


def _ret_fwd(proj, cos2, sin_signed, gain):
    t = proj.shape[0]
    c, d, nh = RET_CHUNK, RET_HEAD_DIM, RET_HEADS
    n_chunks = t // c
    per = RET_CHUNKS_PER_STEP if n_chunks % RET_CHUNKS_PER_STEP == 0 else 1
    rows = per * c
    decay, xi, zeta, g_chunk = _ret_consts()

    def units(qk_ref, vg_ref, cos_ref, sin_ref, dec_ref, xi_ref, zeta_ref, gain_ref, o_ref, y_ref, st_ref, state, ctx):
        cur = [None] * nh

        def start():
            @pl.when(pl.program_id(0) == 0)
            def _():
                state[...] = jnp.zeros_like(state)
            for h in range(nh):
                cur[h] = state[h]

        def retain(s, h, keep):
            rs = slice(s * c, (s + 1) * c)
            cos2, sin_s = cos_ref[rs, :], sin_ref[rs, :]
            lo = h * d
            q = _rope(qk_ref[rs, lo:lo + d].astype(F32), cos2, sin_s)
            k = _rope(qk_ref[rs, D_RET + lo:D_RET + lo + d].astype(F32), cos2, sin_s) * RET_SCALE
            v = vg_ref[rs, lo:lo + d]
            s_prev = cur[h]
            st_ref[s, h] = s_prev
            scores = _dot_nt(q, k) * dec_ref[h]
            o = _dot(scores, v) + _dot(q * xi_ref[:, lo:lo + d], s_prev)
            cur[h] = s_prev * g_chunk[h] + _dot_tn(k * zeta_ref[:, lo:lo + d], v)
            o_ref[rs, lo:lo + d] = o
            keep["o"] = o

        def normalise(s, h, keep):
            rs = slice(s * c, (s + 1) * c)
            lo = h * d
            o = keep["o"]
            g = vg_ref[rs, D_RET + lo:D_RET + lo + d].astype(F32)
            mu = jnp.mean(o, axis=-1, keepdims=True)
            oc = o - mu
            on = oc * lax.rsqrt(jnp.mean(oc * oc, axis=-1, keepdims=True) + NORM_EPS)
            y_ref[rs, lo:lo + d] = (on * gain_ref[:, lo:lo + d] * (g * _sigmoid(g))).astype(y_ref.dtype)

        def end():
            for h in range(nh):
                state[h] = cur[h]

        pieces = [start]
        for s in range(per):
            for h in range(nh):
                keep = {}
                pieces += [lambda s=s, h=h, keep=keep: retain(s, h, keep),
                           lambda s=s, h=h, keep=keep: normalise(s, h, keep)]
        return pieces + [end]

    full2 = lambda i: (0, 0)
    return dict(units=units, grid=(n_chunks // per,),
                in_specs=[pl.BlockSpec((rows, 2 * D_RET), lambda i: (i, 1)),
                          pl.BlockSpec((rows, 2 * D_RET), lambda i: (i, 2)),
                          pl.BlockSpec((rows, d), lambda i: (i, 0)), pl.BlockSpec((rows, d), lambda i: (i, 0)),
                          pl.BlockSpec((nh, c, c), lambda i: (0, 0, 0)), pl.BlockSpec((c, D_RET), full2),
                          pl.BlockSpec((c, D_RET), full2), pl.BlockSpec((1, D_RET), full2)],
                out_specs=[pl.BlockSpec((rows, D_RET), lambda i: (i, 0)), pl.BlockSpec((rows, D_RET), lambda i: (i, 0)),
                           pl.BlockSpec((per, nh, d, d), lambda i: (i, 0, 0, 0))],
                out_shape=[jax.ShapeDtypeStruct((t, D_RET), F32), jax.ShapeDtypeStruct((t, D_RET), MXU_DTYPE),
                           jax.ShapeDtypeStruct((n_chunks, nh, d, d), F32)],
                scratch_shapes=[pltpu.VMEM((nh, d, d), F32)],
                operands=[proj, proj, cos2, sin_signed, decay, xi, zeta, gain])


def _outproj_fwd(x, y_lru, y_ret, w_out, g2, comm):
    t = x.shape[0]
    tm = min(PROJ_ROW_TILE, t)

    def body(x_ref, yl_ref, yr_ref, w_ref, g_ref, h1_ref, u2_ref):
        h1 = x_ref[...] + _dot(yl_ref[...], w_ref[:D_LRU, :]) + _dot(yr_ref[...], w_ref[D_LRU:, :])
        h1_ref[...] = h1
        _, _, u = _rms_fwd(h1, g_ref[...])
        u2_ref[...] = u.astype(u2_ref.dtype)

    row = lambda i: (i, 0)
    return _call(body, name="outproj_fwd", grid=(t // tm,), comm=comm,
                 in_specs=[pl.BlockSpec((tm, D_MODEL), row), pl.BlockSpec((tm, D_LRU), row), pl.BlockSpec((tm, D_RET), row),
                           _resident((D_MODEL, D_MODEL)), pl.BlockSpec((1, D_MODEL), lambda i: (0, 0))],
                 out_specs=[pl.BlockSpec((tm, D_MODEL), row), pl.BlockSpec((tm, D_MODEL), row)],
                 out_shape=[jax.ShapeDtypeStruct((t, D_MODEL), F32), jax.ShapeDtypeStruct((t, D_MODEL), MXU_DTYPE)],
                 operands=[x, y_lru, y_ret, w_out, g2])


FFN_TN = 768
FFN_NJ = D_FF // FFN_TN
FFN_GROUP = 4


def _ffn_fwd(u2, w_blocks, conv_w, conv_b, w_down, h1, gf, target):
    t = u2.shape[0]
    tm = min(ROW_TILE, t)
    tn, nj, group = FFN_TN, FFN_NJ, FFN_GROUP
    ng, tw = nj // group, group * tn
    hb = _halo_rows(u2.dtype)
    assert w_blocks.shape == (2 * nj, D_MODEL, tn)

    def conv(ext, col, up_ref, conv_ref, cw_ref, cb_ref, first):
        x = ext[hb:, :]
        up_ref[:, col] = x.astype(up_ref.dtype)
        prev = jnp.where(first, 0.0, ext[hb - SUBLANES:hb, :])
        cw = cw_ref[:, col]
        y = cb_ref[:, col] + cw[FFN_CONV - 1:FFN_CONV, :] * x
        for k in range(FFN_CONV - 1):
            y = y + cw[k:k + 1, :] * _shift_down(prev, x, FFN_CONV - 1 - k)
        conv_ref[:, col] = y.astype(conv_ref.dtype)
        return y

    def body(u_ref, uh_ref, w_ref, cwa_ref, cwv_ref, cba_ref, cbv_ref, wd_ref, h1_ref, gf_ref, tg_ref,
             upa_ref, upv_ref, ca_ref, cv_ref, act_ref, dh_ref, dhb_ref, dgf_ref, loss_ref, acc):
        i, jg = pl.program_id(0), pl.program_id(1)

        @pl.when((i == 0) & (jg == 0))
        def _():
            dgf_ref[...] = jnp.zeros_like(dgf_ref)
            loss_ref[...] = jnp.zeros_like(loss_ref)

        @pl.when(jg == 0)
        def _():
            acc[...] = jnp.zeros_like(acc)

        u_ext = jnp.concatenate([uh_ref[...], u_ref[...]], axis=0)

        def project(jj):
            j = jg * group + jj
            return _dot(u_ext, w_ref[j]), _dot(u_ext, w_ref[nj + j])

        down, ahead = None, project(0)
        for jj in range(group):
            col = slice(jj * tn, (jj + 1) * tn)
            j = jg * group + jj
            ext_a, ext_v = ahead
            if jj + 1 < group:
                ahead = project(jj + 1)
            a = conv(ext_a, col, upa_ref, ca_ref, cwa_ref, cba_ref, i == 0)
            v = conv(ext_v, col, upv_ref, cv_ref, cwv_ref, cbv_ref, i == 0)
            act = (_gelu(a) * v).astype(act_ref.dtype)
            act_ref[:, col] = act
            part = _dot(act, wd_ref[pl.ds(pl.multiple_of(j * tn, tn), tn), :])
            down = part if down is None else down + part
        acc[...] += down

        @pl.when(jg == ng - 1)
        def _():
            n, rstd, y = _rms_fwd(h1_ref[...] + acc[...], gf_ref[...])
            err = y - tg_ref[...]
            loss_ref[...] += (0.5 / D_MODEL) * jnp.sum(err * err)
            dh, dgf = _rms_bwd(err * (1.0 / D_MODEL), n, rstd, gf_ref[...])
            dgf_ref[...] += dgf
            dh_ref[...] = dh
            dhb_ref[...] = dh.astype(dhb_ref.dtype)

    per = tm // hb
    row = lambda i, j: (i, 0)
    const = lambda i, j: (0, 0)
    tile = pl.BlockSpec((tm, tw), lambda i, j: (i, j))
    return _call(body, name="ffn_fwd", grid=(t // tm, ng), vmem_limit=FUSED_VMEM_LIMIT,
                 in_specs=[pl.BlockSpec((tm, D_MODEL), row),
                           pl.BlockSpec((hb, D_MODEL), lambda i, j: (jnp.maximum(i * per - 1, 0), 0)),
                           _resident(w_blocks.shape),
                           pl.BlockSpec((FFN_CONV, tw), lambda i, j: (0, j)),
                           pl.BlockSpec((FFN_CONV, tw), lambda i, j: (0, j + ng)),
                           pl.BlockSpec((1, tw), lambda i, j: (0, j)), pl.BlockSpec((1, tw), lambda i, j: (0, j + ng)),
                           _resident((D_FF, D_MODEL)),
                           pl.BlockSpec((tm, D_MODEL), row), pl.BlockSpec((1, D_MODEL), const),
                           pl.BlockSpec((tm, D_MODEL), row)],
                 out_specs=[tile] * 5 + [pl.BlockSpec((tm, D_MODEL), row),
                            pl.BlockSpec((tm, D_MODEL), row), pl.BlockSpec((SUBLANES, D_MODEL), const),
                            pl.BlockSpec((SUBLANES, LANES), const)],
                 out_shape=[jax.ShapeDtypeStruct((t, D_FF), MXU_DTYPE)] * 5 + [
                            jax.ShapeDtypeStruct((t, D_MODEL), F32),
                            jax.ShapeDtypeStruct((t, D_MODEL), MXU_DTYPE), jax.ShapeDtypeStruct((SUBLANES, D_MODEL), F32),
                            jax.ShapeDtypeStruct((SUBLANES, LANES), F32)],
                 scratch_shapes=[pltpu.VMEM((tm, D_MODEL), F32)],
                 operands=[u2, u2, w_blocks, conv_w, conv_w, conv_b, conv_b, w_down, h1, gf, target])


FFN_ACC_ROWS = SUBLANES * (FFN_CONV + 1)


def _ffn_bwd(dh2, dh2_b, w_down, up_a, up_v, conv_a, conv_v, conv_w, w_up_blocks, h1, g2, comm):
    t = up_a.shape[0]
    tm = min(ROW_TILE, t)
    tn, nj, group = FFN_TN, FFN_NJ, FFN_GROUP
    ng, tw = nj // group, group * tn
    ni = t // tm
    assert w_up_blocks.shape == (2 * nj, D_MODEL, tn)

    def conv_bwd(dy, x, cw, acc_ref, carry_ref, dup_ref, col):
        nxt = carry_ref[...]
        carry_ref[...] = dy[:SUBLANES, :]
        ahead = [_shift_up(dy, nxt, FFN_CONV - 1 - k) for k in range(FFN_CONV)]
        dx = cw[FFN_CONV - 1:FFN_CONV, :] * dy
        for k in range(FFN_CONV - 1):
            dx = dx + cw[k:k + 1, :] * ahead[k]
        dx = dx.astype(dup_ref.dtype)
        dup_ref[:, col] = dx
        for k in range(FFN_CONV):
            acc_ref[k * SUBLANES:(k + 1) * SUBLANES, :] += _colsum8(ahead[k] * x)
        acc_ref[FFN_CONV * SUBLANES:, :] += _colsum8(dy)
        return dx

    def body(dh_ref, dhb_ref, wd_ref, ua_ref, uv_ref, ca_ref, cv_ref, cwa_ref, cwv_ref, wu_ref, h1_ref, g2_ref,
             dua_ref, duv_ref, acca_ref, accv_ref, dh1_ref, dh1b_ref, dg2_ref, carry_a, carry_v, du):
        i, jg = pl.program_id(0), pl.program_id(1)

        @pl.when((i == 0) & (jg == 0))
        def _():
            for ref in (acca_ref, accv_ref, carry_a, carry_v, dg2_ref):
                ref[...] = jnp.zeros_like(ref)

        dhb = dhb_ref[...]

        def through_down(jj):
            j = jg * group + jj
            return _dot_nt(dhb, wd_ref[pl.ds(pl.multiple_of(j * tn, tn), tn), :])

        part, ahead = None, through_down(0)
        for jj in range(group):
            col = slice(jj * tn, (jj + 1) * tn)
            j = jg * group + jj
            dact = ahead
            if jj + 1 < group:
                ahead = through_down(jj + 1)
            v = cv_ref[:, col].astype(F32)
            g, dg = _gelu_parts(ca_ref[:, col].astype(F32))
            da = conv_bwd(dact * v * dg, ua_ref[:, col].astype(F32), cwa_ref[:, col], acca_ref.at[j], carry_a.at[j],
                          dua_ref, col)
            dv = conv_bwd(dact * g, uv_ref[:, col].astype(F32), cwv_ref[:, col], accv_ref.at[j], carry_v.at[j],
                          duv_ref, col)
            term = _dot_nt(da, wu_ref[j]) + _dot_nt(dv, wu_ref[nj + j])
            part = term if part is None else part + term

        @pl.when(jg == 0)
        def _():
            du[...] = part

        @pl.when(jg > 0)
        def _():
            du[...] += part

        @pl.when(jg == ng - 1)
        def _():
            n, rstd, _ = _rms_fwd(h1_ref[...], g2_ref[...])
            dh1, dg2 = _rms_bwd(du[...], n, rstd, g2_ref[...])
            dh1 = dh1 + dh_ref[...]
            dg2_ref[...] += dg2
            dh1_ref[...] = dh1
            dh1b_ref[...] = dh1.astype(dh1b_ref.dtype)

    row = lambda i, j: (ni - 1 - i, 0)
    const = lambda i, j: (0, 0)
    tile = pl.BlockSpec((tm, tw), lambda i, j: (ni - 1 - i, j))
    acc = pl.BlockSpec((nj, FFN_ACC_ROWS, tn), lambda i, j: (0, 0, 0))
    return _call(body, name="ffn_bwd", grid=(ni, ng), comm=comm, vmem_limit=FFN_BWD_VMEM_LIMIT,
                 in_specs=[pl.BlockSpec((tm, D_MODEL), row), pl.BlockSpec((tm, D_MODEL), row),
                           _resident((D_FF, D_MODEL)), tile, tile, tile, tile,
                           pl.BlockSpec((FFN_CONV, tw), lambda i, j: (0, j)),
                           pl.BlockSpec((FFN_CONV, tw), lambda i, j: (0, j + ng)),
                           _resident(w_up_blocks.shape), pl.BlockSpec((tm, D_MODEL), row),
                           pl.BlockSpec((1, D_MODEL), const)],
                 out_specs=[tile, tile, acc, acc, pl.BlockSpec((tm, D_MODEL), row), pl.BlockSpec((tm, D_MODEL), row),
                            pl.BlockSpec((SUBLANES, D_MODEL), const)],
                 out_shape=[jax.ShapeDtypeStruct((t, D_FF), MXU_DTYPE), jax.ShapeDtypeStruct((t, D_FF), MXU_DTYPE),
                            jax.ShapeDtypeStruct((nj, FFN_ACC_ROWS, tn), F32),
                            jax.ShapeDtypeStruct((nj, FFN_ACC_ROWS, tn), F32),
                            jax.ShapeDtypeStruct((t, D_MODEL), F32), jax.ShapeDtypeStruct((t, D_MODEL), MXU_DTYPE),
                            jax.ShapeDtypeStruct((SUBLANES, D_MODEL), F32)],
                 scratch_shapes=[pltpu.VMEM((nj, SUBLANES, tn), F32), pltpu.VMEM((nj, SUBLANES, tn), F32),
                                 pltpu.VMEM((tm, D_MODEL), F32)],
                 operands=[dh2, dh2_b, w_down, up_a, up_v, conv_a, conv_v, conv_w, conv_w, w_up_blocks, h1, g2])


def _ret_bwd(proj, cos2, sin_signed, gain, o, states, dmix_at):
    t = proj.shape[0]
    c, d, nh = RET_CHUNK, RET_HEAD_DIM, RET_HEADS
    n_chunks = t // c
    per = RET_CHUNKS_PER_STEP if n_chunks % RET_CHUNKS_PER_STEP == 0 else 1
    rows = per * c
    n_steps = n_chunks // per
    decay, xi, zeta, g_chunk = _ret_consts()
    base = 2 * D_LRU

    def units(qk_ref, vg_ref, cos_ref, sin_ref, dec_ref, xi_ref, zeta_ref, gain_ref, o_ref, st_ref,
              dp_ref, dgain_ref, gstate, ctx):
        cur = [None] * nh
        dmix = ctx["scratch"][dmix_at[0]][dmix_at[1]]

        def start():
            @pl.when(pl.program_id(0) == 0)
            def _():
                gstate[...] = jnp.zeros_like(gstate)
                dgain_ref[...] = jnp.zeros_like(dgain_ref)
            for h in range(nh):
                cur[h] = gstate[h]

        def gate_and_norm(s, h, keep):
            rs = slice(s * c, (s + 1) * c)
            lo = h * d
            g = vg_ref[rs, D_RET + lo:D_RET + lo + d].astype(F32)
            gain_h = gain_ref[:, lo:lo + d]
            dy = dmix[rs, D_LRU + lo:D_LRU + lo + d]
            sg = _sigmoid(g)
            o_h = o_ref[rs, lo:lo + d]
            oc = o_h - jnp.mean(o_h, axis=-1, keepdims=True)
            rstd = lax.rsqrt(jnp.mean(oc * oc, axis=-1, keepdims=True) + NORM_EPS)
            on = oc * rstd
            at = base + 3 * D_RET + lo
            dp_ref[rs, at:at + d] = (dy * on * gain_h * (sg * (1.0 + g * (1.0 - sg)))).astype(dp_ref.dtype)
            don_g = dy * (g * sg)
            dgain_ref[:, lo:lo + d] += _colsum8(don_g * on)
            don = don_g * gain_h
            keep["do"] = rstd * (don - jnp.mean(don, axis=-1, keepdims=True)
                                 - on * jnp.mean(don * on, axis=-1, keepdims=True))

        def retain(s, h, keep):
            rs = slice(s * c, (s + 1) * c)
            cos2, sin_s = cos_ref[rs, :], sin_ref[rs, :]
            lo = h * d
            q = _rope(qk_ref[rs, lo:lo + d].astype(F32), cos2, sin_s)
            k = _rope(qk_ref[rs, D_RET + lo:D_RET + lo + d].astype(F32), cos2, sin_s) * RET_SCALE
            v = vg_ref[rs, lo:lo + d]
            xi_h, zeta_h, dec = xi_ref[:, lo:lo + d], zeta_ref[:, lo:lo + d], dec_ref[h]
            do = keep["do"]
            s_prev = st_ref[s, h]
            g_next = cur[h]
            p = _dot_nt(q, k) * dec
            dpm = _dot_nt(do, v) * dec
            keep["dq"] = _dot(dpm, k) + _dot_nt(do, s_prev) * xi_h
            keep["dk"] = _dot_tn(dpm, q) + _dot_nt(v, g_next) * zeta_h
            dv = _dot_tn(p, do) + _dot(k * zeta_h, g_next)
            cur[h] = g_next * g_chunk[h] + _dot_tn(q * xi_h, do)
            at = base + 2 * D_RET + lo
            dp_ref[rs, at:at + d] = dv.astype(dp_ref.dtype)

        def unrope(s, h, keep):
            rs = slice(s * c, (s + 1) * c)
            cos2, sin_s = cos_ref[rs, :], sin_ref[rs, :]
            lo = h * d
            dp_ref[rs, base + lo:base + lo + d] = _rope_bwd(keep["dq"], cos2, sin_s).astype(dp_ref.dtype)
            at = base + D_RET + lo
            dp_ref[rs, at:at + d] = _rope_bwd(keep["dk"] * RET_SCALE, cos2, sin_s).astype(dp_ref.dtype)

        def end():
            for h in range(nh):
                gstate[h] = cur[h]

        pieces = [start]
        for s in reversed(range(per)):
            for h in range(nh):
                keep = {}
                pieces += [lambda s=s, h=h, keep=keep, f=f: f(s, h, keep) for f in (gate_and_norm, retain, unrope)]
        return pieces + [end]

    rev = lambda col: (lambda i: (n_steps - 1 - i, col))
    full2 = lambda i: (0, 0)
    return dict(units=units, grid=(n_steps,),
                in_specs=[pl.BlockSpec((rows, 2 * D_RET), rev(1)), pl.BlockSpec((rows, 2 * D_RET), rev(2)),
                          pl.BlockSpec((rows, d), rev(0)), pl.BlockSpec((rows, d), rev(0)),
                          pl.BlockSpec((nh, c, c), lambda i: (0, 0, 0)), pl.BlockSpec((c, D_RET), full2),
                          pl.BlockSpec((c, D_RET), full2), pl.BlockSpec((1, D_RET), full2),
                          pl.BlockSpec((rows, D_RET), rev(0)),
                          pl.BlockSpec((per, nh, d, d), lambda i: (n_steps - 1 - i, 0, 0, 0))],
                out_specs=[pl.BlockSpec((rows, D_IN), rev(0)), pl.BlockSpec((SUBLANES, D_RET), full2)],
                out_shape=[jax.ShapeDtypeStruct((t, D_IN), MXU_DTYPE), jax.ShapeDtypeStruct((SUBLANES, D_RET), F32)],
                scratch_shapes=[pltpu.VMEM((nh, d, d), F32)],
                operands=[proj, proj, cos2, sin_signed, decay, xi, zeta, gain, o, states])


LRU_ACC = {"conv_w": 0, "conv_b": LRU_CONV, "gate_a_b": LRU_CONV + 1, "gate_x_b": LRU_CONV + 2,
           "lambda": LRU_CONV + 3, "norm_gain": LRU_CONV + 4}
LRU_ACC_ROWS = SUBLANES * (LRU_CONV + 5)


def _lru_bwd(proj, xc_all, h_all, conv_w, wa, ba, wx, bx, lam, gain, dproj_part, dmix_at):
    t = proj.shape[0]
    tm = min(MIX_ROW_TILE, t)
    c = D_LRU
    ni = t // tm

    def body(x_ref, xh_ref, g_ref, xc_ref, h_ref, hh_ref, cw_ref, wa_ref, ba_ref, wx_ref, bx_ref, lam_ref,
             gain_ref, acc_ref, dwa_ref, dwx_ref, a_scr, b_scr, mu_scr, carry_mu, carry_dxc, ctx):
        dp_ref = ctx["outs"][dproj_part][0]
        dmix = ctx["scratch"][dmix_at[0]][dmix_at[1]]
        fill = ctx["fill"]
        i = pl.program_id(0)
        r = ni - 1 - i

        @pl.when(i == 0)
        def _():
            acc_ref[...] = jnp.zeros_like(acc_ref)
            dwa_ref[...] = jnp.zeros_like(dwa_ref)
            dwx_ref[...] = jnp.zeros_like(dwx_ref)
            carry_mu[...] = jnp.zeros_like(carry_mu)
            carry_dxc[...] = jnp.zeros_like(carry_dxc)

        def add(name, val, k=0):
            lo = (LRU_ACC[name] + k) * SUBLANES
            acc_ref[lo:lo + SUBLANES, :] += _colsum8(val)

        fill()
        xc, h = xc_ref[...], h_ref[...]
        lam_v = lam_ref[...]
        sp = _softplus(-lam_v)
        rg, ig, a, m = _lru_gates(xc, wa_ref[...], ba_ref[...], wx_ref[...], bx_ref[...], sp, fill)
        gl, dgl = _gelu_parts(g_ref[...].astype(F32))
        fill()
        zn, rstd, _ = _rms_fwd(h * gl, gain_ref[...])
        dy = dmix[:, :c]
        dz, dgain = _rms_bwd(dy, zn, rstd, gain_ref[...])
        lo = LRU_ACC["norm_gain"] * SUBLANES
        acc_ref[lo:lo + SUBLANES, :] += dgain
        dp_ref[:, c:2 * c] = (dz * h * dgl).astype(dp_ref.dtype)
        dh = dz * gl
        fill()
        ga, gb = _group_scan(a, a * dh, reverse=True, fill=fill)
        a_scr[...] = ga
        b_scr[...] = gb
        mu_next_tile = carry_mu[...]
        carry_mu[...] = _carry_scan(a_scr, b_scr, mu_scr, mu_next_tile, reverse=True)
        fill()
        lam_t = dh + _shift_up(mu_scr[...], mu_next_tile, 1)
        h_prev = _shift_down(jnp.where(r == 0, 0.0, hh_ref[...]), h, 1)
        da = lam_t * h_prev
        dig = lam_t * m * xc
        dxc = lam_t * m * ig
        dlog_a = da * a - (lam_t * ig * xc) * (a * a) / m
        fill()
        dpr = dlog_a * ((-LRU_C) * sp) * rg * (1.0 - rg)
        add("lambda", dlog_a * ((-LRU_C) * rg) * (-_sigmoid(-lam_v)))
        dpi = dig * ig * (1.0 - ig)
        add("gate_a_b", dpr)
        add("gate_x_b", dpi)
        fill()
        dwa_ref[...] += _dot_tn(xc, dpr)
        dwx_ref[...] += _dot_tn(xc, dpi)
        dxc = dxc + _dot_nt(dpr, wa_ref[...]) + _dot_nt(dpi, wx_ref[...])
        fill()
        add("conv_b", dxc)
        x = x_ref[...].astype(F32)
        prev = jnp.where(r == 0, 0.0, xh_ref[...].astype(F32)[-SUBLANES:, :])
        cw = cw_ref[...]
        nxt = carry_dxc[...]
        carry_dxc[...] = dxc[:SUBLANES, :]
        dx = cw[LRU_CONV - 1:LRU_CONV, :] * dxc
        for k in range(LRU_CONV - 1):
            dx = dx + cw[k:k + 1, :] * _shift_up(dxc, nxt, LRU_CONV - 1 - k)
        fill()
        for k in range(LRU_CONV):
            add("conv_w", dxc * _shift_down(prev, x, LRU_CONV - 1 - k), k)
        dp_ref[:, :c] = dx.astype(dp_ref.dtype)

    hb = _halo_rows(proj.dtype)
    rev = lambda col: (lambda i: (ni - 1 - i, col))
    halo = lambda rows: (lambda i: (jnp.maximum((ni - 1 - i) * (tm // rows) - 1, 0), 0))
    full = lambda i: (0, 0)
    vec = pl.BlockSpec((1, c), full)
    mat = pl.BlockSpec((c, c), full)
    return dict(body=body, grid=(ni,), fill_points=16,
                in_specs=[pl.BlockSpec((tm, c), rev(0)), pl.BlockSpec((hb, c), halo(hb)), pl.BlockSpec((tm, c), rev(1)),
                          pl.BlockSpec((tm, c), rev(0)), pl.BlockSpec((tm, c), rev(0)),
                          pl.BlockSpec((SUBLANES, c), halo(SUBLANES)),
                          pl.BlockSpec((LRU_CONV, c), full), mat, vec, mat, vec, vec, vec],
                out_specs=[pl.BlockSpec((LRU_ACC_ROWS, c), full), mat, mat],
                out_shape=[jax.ShapeDtypeStruct((LRU_ACC_ROWS, c), F32), jax.ShapeDtypeStruct((c, c), F32),
                           jax.ShapeDtypeStruct((c, c), F32)],
                scratch_shapes=[pltpu.VMEM((tm, c), F32), pltpu.VMEM((tm, c), F32), pltpu.VMEM((tm, c), F32),
                                pltpu.VMEM((SUBLANES, c), F32), pltpu.VMEM((SUBLANES, c), F32)],
                operands=[proj, proj, proj, xc_all, h_all, h_all, conv_w, wa, ba, wx, bx, lam, gain])


def _mix_proj_bwd(dh1, dh1_b, w_out, w_in_blocks, x, g1, dproj_part):
    t = x.shape[0]
    tm = min(MIX_ROW_TILE, t)
    ni = t // tm
    nb, _, cb = w_in_blocks.shape
    first_free = -(-2 * D_LRU // cb)
    du = [None]

    def term(dp_ref, w_ref, d):
        part = _dot_nt(dp_ref[:, d * cb:(d + 1) * cb], w_ref[d])
        du[0] = part if du[0] is None else du[0] + part

    def head(dh_ref, dhb_ref, wo_ref, wi_ref, x_ref, g_ref, gx_ref, dg_ref, dmix, ctx):
        @pl.when(pl.program_id(0) == 0)
        def _():
            dg_ref[...] = jnp.zeros_like(dg_ref)
        dmix[...] = _dot_nt(dhb_ref[...], wo_ref[...])
        du[0] = None

    def units(dh_ref, dhb_ref, wo_ref, wi_ref, x_ref, g_ref, gx_ref, dg_ref, dmix, ctx):
        dp_ref = ctx["outs"][dproj_part][0]
        return [lambda d=d: term(dp_ref, wi_ref, d) for d in range(first_free, nb)]

    def tail(dh_ref, dhb_ref, wo_ref, wi_ref, x_ref, g_ref, gx_ref, dg_ref, dmix, ctx):
        dp_ref = ctx["outs"][dproj_part][0]
        for d in range(first_free):
            term(dp_ref, wi_ref, d)
        n, rstd, _ = _rms_fwd(x_ref[...], g_ref[...])
        dx, dg = _rms_bwd(du[0], n, rstd, g_ref[...])
        dg_ref[...] += dg
        gx_ref[...] = dx + dh_ref[...]

    row = lambda i: (ni - 1 - i, 0)
    const = lambda i: (0, 0)
    tile = pl.BlockSpec((tm, D_MODEL), row)
    return dict(head=head, units=units, tail=tail, grid=(ni,),
                in_specs=[tile, tile, _resident(w_out.shape), _resident(w_in_blocks.shape), tile,
                          pl.BlockSpec((1, D_MODEL), const)],
                out_specs=[tile, pl.BlockSpec((SUBLANES, D_MODEL), const)],
                out_shape=[jax.ShapeDtypeStruct((t, D_MODEL), F32), jax.ShapeDtypeStruct((SUBLANES, D_MODEL), F32)],
                scratch_shapes=[pltpu.VMEM((tm, D_MODEL), F32)],
                operands=[dh1, dh1_b, w_out, w_in_blocks, x, g1])


def _pair_sum(core, a, b, name):
    n, r, c = b.shape
    spec = pl.BlockSpec((None, r, c), lambda q, core: (q, 0, 0))

    def body(core_ref, a_ref, b_ref, o_ref):
        o_ref[...] = (a_ref[...].astype(F32) + b_ref[...].astype(F32)).astype(o_ref.dtype)

    return pl.pallas_call(
        body, name=name,
        grid_spec=pltpu.PrefetchScalarGridSpec(
            num_scalar_prefetch=1, grid=(n,),
            in_specs=[pl.BlockSpec((None, r, c), lambda q, core: (2 * q + core[0], 0, 0)), spec], out_specs=spec),
        out_shape=jax.ShapeDtypeStruct(b.shape, b.dtype),
        compiler_params=pltpu.CompilerParams(dimension_semantics=("arbitrary",), vmem_limit_bytes=VMEM_LIMIT),
    )(core, a, b)


ADAMW_BLOCK_BYTES = 4 * 1024 * 1024


def _sum_adamw(parts, w, m, v, name):
    n_parts, r, c = parts.shape
    tr = r
    while n_parts * tr * c * parts.dtype.itemsize > ADAMW_BLOCK_BYTES and tr % (4 * SUBLANES) == 0:
        tr //= 2

    def body(p_ref, w_ref, m_ref, v_ref, g_ref, d_ref, nm_ref, nv_ref):
        g = p_ref[0].astype(F32)
        for s in range(1, n_parts):
            g = g + p_ref[s].astype(F32)
        nm = ADAM_B1 * m_ref[...] + (1.0 - ADAM_B1) * g
        nv = ADAM_B2 * v_ref[...] + (1.0 - ADAM_B2) * (g * g)
        m_hat = nm / (1.0 - ADAM_B1 ** ADAM_STEP)
        v_hat = nv / (1.0 - ADAM_B2 ** ADAM_STEP)
        g_ref[...] = g
        d_ref[...] = -ADAM_LR * (m_hat / (jnp.sqrt(v_hat) + ADAM_EPS) + ADAM_WD * w_ref[...])
        nm_ref[...] = nm
        nv_ref[...] = nv

    row = pl.BlockSpec((tr, c), lambda i: (i, 0))
    return _call(body, name=name, grid=(r // tr,),
                 in_specs=[pl.BlockSpec((n_parts, tr, c), lambda i: (0, i, 0)), row, row, row],
                 out_specs=[row, row, row, row], out_shape=[jax.ShapeDtypeStruct((r, c), F32)] * 4,
                 operands=[parts, w, m, v])


MATRICES = ("w_in", "w_out", "ffn_up_w", "ffn_down_w")
CONVS = ("lru_conv_w", "ffn_conv_w")
REPLICATED = ("norm1_gain", "lru_conv_b", "lru_gate_a_w", "lru_gate_a_b", "lru_gate_x_w", "lru_gate_x_b", "lru_lambda",
              "lru_norm_gain", "ret_norm_gain", "norm2_gain", "ffn_conv_b", "final_norm_gain")
WEIGHTS = ("norm1_gain", "w_in", "lru_conv_w", "lru_conv_b", "lru_gate_a_w", "lru_gate_a_b", "lru_gate_x_w",
           "lru_gate_x_b", "lru_lambda", "lru_norm_gain", "ret_norm_gain", "w_out", "norm2_gain", "ffn_up_w",
           "ffn_conv_w", "ffn_conv_b", "ffn_down_w", "final_norm_gain")


def _rows(a, pad_to):
    a = a.reshape(-1, LANES)
    pad = (-a.shape[0]) % pad_to
    return jnp.pad(a, ((0, pad), (0, 0))) if pad else a


def _pack(arrays, pad_to):
    rows, layout, at = [], [], 0
    for a in arrays:
        r = _rows(a, pad_to)
        layout.append((at, a.size // LANES, a.shape))
        rows.append(r)
        at += r.shape[0]
    return jnp.concatenate(rows, axis=0), layout


def _unpack(packed, layout):
    lead = packed.shape[:-2]
    return [packed[..., at:at + n, :].reshape(lead + shape) for at, n, shape in layout]


def _conv_rows(lru, ffn, dtype, pad_to):
    lead = lru.shape[:-2]
    flat = jnp.concatenate([lru.reshape(lead + (-1,)), ffn.reshape(lead + (-1,))], axis=-1).astype(dtype)
    rows = flat.shape[-1] // LANES
    pad = (-rows) % pad_to
    return jnp.pad(flat.reshape(lead + (rows, LANES)), [(0, 0)] * len(lead) + [(0, pad), (0, 0)])


def _column_blocks(full):
    r, c = full.shape
    return full.reshape(r, N_DEV, c // N_DEV).transpose(1, 0, 2)


def _block_diag(w):
    nh, d, _ = w.shape
    eye = jnp.eye(nh, dtype=w.dtype)
    return (w[:, :, None, :] * eye[:, None, :, None]).reshape(nh * d, nh * d)


def _diag_blocks(dense, nh):
    d = dense.shape[0] // nh
    blocks = dense.reshape(nh, d, nh, d)
    return jnp.stack([blocks[h, :, h, :] for h in range(nh)], axis=0)


def kernel(x, norm1_gain, w_in, lru_conv_w, lru_conv_b, lru_gate_a_w, lru_gate_a_b, lru_gate_x_w, lru_gate_x_b, lru_lambda, lru_norm_gain, ret_norm_gain, w_out, norm2_gain, ffn_up_w, ffn_conv_w, ffn_conv_b, ffn_down_w, final_norm_gain, loss_target, m_norm1_gain, m_w_in, m_lru_conv_w, m_lru_conv_b, m_lru_gate_a_w, m_lru_gate_a_b, m_lru_gate_x_w, m_lru_gate_x_b, m_lru_lambda, m_lru_norm_gain, m_ret_norm_gain, m_w_out, m_norm2_gain, m_ffn_up_w, m_ffn_conv_w, m_ffn_conv_b, m_ffn_down_w, m_final_norm_gain, v_norm1_gain, v_w_in, v_lru_conv_w, v_lru_conv_b, v_lru_gate_a_w, v_lru_gate_a_b, v_lru_gate_x_w, v_lru_gate_x_b, v_lru_lambda, v_lru_norm_gain, v_ret_norm_gain, v_w_out, v_norm2_gain, v_ffn_up_w, v_ffn_conv_w, v_ffn_conv_b, v_ffn_down_w, v_final_norm_gain):
    args = dict(locals())
    given = {n: args[n] for n in WEIGHTS}
    out_shape = {n: given[n].shape for n in WEIGHTS}

    def plain(a):
        return a.reshape(1, -1) if a.ndim <= 2 else a[0]

    w = {n: plain(given[n]) for n in WEIGHTS}
    mom_m = {n: plain(args["m_" + n]) for n in WEIGHTS}
    mom_v = {n: plain(args["v_" + n]) for n in WEIGHTS}
    x2, target = x[0], loss_target[0]
    t = x2.shape[0]
    core = lax.axis_index("c").astype(jnp.int32).reshape(1)
    res = {}

    conv_pad = _conv_rows(w["lru_conv_w"], w["ffn_conv_w"], F32, SUBLANES)
    first = _gather_first([w["w_in"].astype(MXU_DTYPE), conv_pad])
    w_in_blocks, conv_all = _run_comms([first, _gather_second(first.out_shape)], "w_in_all_gather")
    n_lru = w["lru_conv_w"].size
    conv_flat = conv_all.reshape(N_DEV, -1)
    lru_cw = conv_flat[:, :n_lru].reshape((N_DEV,) + w["lru_conv_w"].shape).transpose(1, 0, 2).reshape(LRU_CONV, D_LRU)
    ffn_cw = conv_flat[:, n_lru:n_lru + w["ffn_conv_w"].size].reshape((N_DEV,) + w["ffn_conv_w"].shape)
    ffn_cw = ffn_cw.transpose(1, 0, 2).reshape(FFN_CONV, 2 * D_FF)

    cos2, sin_signed = _rope_tables(t)
    wa = _block_diag(w["lru_gate_a_w"]).astype(MXU_DTYPE)
    wx = _block_diag(w["lru_gate_x_w"]).astype(MXU_DTYPE)
    gf = w["final_norm_gain"]

    early = _gather_first([w["w_out"].astype(MXU_DTYPE), w["ffn_down_w"].astype(MXU_DTYPE)])
    (u1, proj), (w_out_part, down_part) = _inproj_fwd(x2, w["norm1_gain"], w_in_blocks, early)
    ((xc, h_lru, y_lru), (o_ret, y_ret, states)), (w_out_blocks, down_blocks, up_part) = _fused(
        [_lru_fwd(proj, lru_cw, w["lru_conv_b"], wa, w["lru_gate_a_b"], wx, w["lru_gate_x_b"], w["lru_lambda"],
                  w["lru_norm_gain"]),
         _ret_fwd(proj, cos2, sin_signed, w["ret_norm_gain"])],
        "mix_fwd", _both(_gather_second([w_out_part, down_part]), _gather_first([w["ffn_up_w"].astype(MXU_DTYPE)])))
    w_out_full = w_out_blocks.reshape(D_MODEL, D_MODEL)
    w_down_full = down_blocks.reshape(D_FF, D_MODEL)

    (h1, u2), (up_blocks,) = _outproj_fwd(x2, y_lru, y_ret, w_out_full, w["norm2_gain"], _gather_second([up_part]))
    up_a, up_v, conv_a, conv_v, act, dh2, dh2_b, dgf, loss_local = _ffn_fwd(u2, up_blocks, ffn_cw, w["ffn_conv_b"],
                                                                            w_down_full, h1, gf, target)

    def to_owner_chips(blocks, names, tag):
        theirs = _run_comms([_pair_exchange(blocks)], "grads_pair_exchange_" + tag)
        return [_pair_sum(core, a, b, "grads_pair_sum_" + n) for n, a, b in zip(names, blocks, theirs)]

    def adamw(name, parts):
        res[name] = _sum_adamw(parts, w[name], mom_m[name], mom_v[name], "adamw_" + name)

    g = {"final_norm_gain": dgf[0]}
    dup_a, dup_v, acc_a, acc_v, dh1, dh1_b, dg2 = _ffn_bwd(
        dh2, dh2_b, w_down_full, up_a, up_v, conv_a, conv_v, ffn_cw, up_blocks, h1, w["norm2_gain"], None)
    per_col = lambda a: a[:, ::SUBLANES].transpose(1, 0, 2).reshape(FFN_CONV + 1, D_FF)
    acc = jnp.concatenate([per_col(acc_a), per_col(acc_v)], axis=1)
    g_ffn_cw, g["ffn_conv_b"] = acc[:FFN_CONV], acc[FFN_CONV:]
    g["norm2_gain"] = dg2[:1]
    g_up = jnp.concatenate([_mm_tn(u2, dup_a, "ffn_up_wgrad_a", blocks=N_DEV // 2),
                            _mm_tn(u2, dup_v, "ffn_up_wgrad_v", blocks=N_DEV // 2)], axis=0)
    up_sums = to_owner_chips([g_up], ["ffn_up_w"], "up")
    g_down, (up_parts,) = _mm_tn(act, dh2_b, "ffn_down_wgrad", comm=_chip_exchange(up_sums))
    adamw("ffn_up_w", up_parts)
    g_out = jnp.concatenate([_mm_tn(y_lru, dh1_b, "w_out_wgrad_lru"), _mm_tn(y_ret, dh1_b, "w_out_wgrad_ret")], axis=0)
    low_sums = to_owner_chips([g_down.reshape(N_DEV, D_FF // N_DEV, D_MODEL),
                               g_out.reshape(N_DEV, D_MODEL // N_DEV, D_MODEL)], ["ffn_down_w", "w_out"], "low")
    (dproj, dgain_ret), (grad_x, dg1), (lru_acc, dwa, dwx) = _fused(
        [_ret_bwd(proj, cos2, sin_signed, w["ret_norm_gain"], o_ret, states, dmix_at=(1, 0)),
         _mix_proj_bwd(dh1, dh1_b, w_out_full, w_in_blocks, x2, w["norm1_gain"], dproj_part=0),
         _lru_bwd(proj, xc, h_lru, lru_cw, wa, w["lru_gate_a_b"], wx, w["lru_gate_x_b"], w["lru_lambda"],
                  w["lru_norm_gain"], dproj_part=0, dmix_at=(1, 0))],
        "mix_bwd")
    g["norm1_gain"] = dg1[:1]
    g["ret_norm_gain"] = dgain_ret[:1]
    lru_acc = lru_acc[::SUBLANES]
    g_lru_cw = lru_acc[:LRU_CONV]
    for name in ("conv_b", "gate_a_b", "gate_x_b", "lambda", "norm_gain"):
        g["lru_" + name] = lru_acc[LRU_ACC[name]:LRU_ACC[name] + 1]
    g["lru_gate_a_w"] = _diag_blocks(dwa, LRU_HEADS)
    g["lru_gate_x_w"] = _diag_blocks(dwx, LRU_HEADS)
    rep_packed, rep_layout = _pack([g[n] for n in REPLICATED] + [loss_local], SUBLANES)
    g_in, (down_parts, out_parts, rep_part) = _mm_tn(u1, dproj, "w_in_wgrad", blocks=N_DEV,
                                                     comm=_both(_chip_exchange(low_sums), _gather_first([rep_packed])))
    adamw("ffn_down_w", down_parts)
    adamw("w_out", out_parts)
    g_conv = _conv_rows(_column_blocks(g_lru_cw), _column_blocks(g_ffn_cw), GRAD_DTYPE, 2 * SUBLANES)
    in_sums = to_owner_chips([g_in, g_conv], ["w_in", "conv"], "in")
    in_parts, conv_parts, rep_parts = _run_comms([_both(_chip_exchange(in_sums), _gather_second([rep_part]))],
                                                 "last_grads_exchange")
    adamw("w_in", in_parts)
    pad16 = lambda d: _conv_rows(d["lru_conv_w"], d["ffn_conv_w"], F32, 2 * SUBLANES)
    conv_res = _sum_adamw(conv_parts, pad16(w), pad16(mom_m), pad16(mom_v), "adamw_conv")
    for n, lo, hi in (("lru_conv_w", 0, n_lru), ("ffn_conv_w", n_lru, n_lru + w["ffn_conv_w"].size)):
        res[n] = [r.reshape(-1)[lo:hi].reshape(w[n].shape) for r in conv_res]
    no_state = jnp.zeros_like(loss_local)
    rep_res = _sum_adamw(rep_parts, *[_pack([d[n] for n in REPLICATED] + [no_state], SUBLANES)[0]
                                      for d in (w, mom_m, mom_v)], "adamw_replicated")
    for k in range(4):
        for n, a in zip(REPLICATED, _unpack(rep_res[k], rep_layout)):
            res.setdefault(n, [None] * 4)[k] = a
    loss = _unpack(rep_res[0], rep_layout)[-1][0, 0]

    outs = [loss, grad_x[None]]
    for k in range(4):
        outs += [res[n][k].reshape(out_shape[n]) for n in WEIGHTS]
    return tuple(outs)
```

```python
import math

import numpy as np
import jax
import jax.numpy as jnp
from jax import lax
from jax.experimental import pallas as pl
from jax.experimental.pallas import tpu as pltpu

F32 = jnp.float32
BF16 = jnp.bfloat16
MXU_DTYPE = jnp.bfloat16
GRAD_DTYPE = jnp.bfloat16

N_DEV = 8
N_CHIPS = 4
D_MODEL = 1024
D_LRU = 512
LRU_HEADS = 8
LRU_CONV = 4
LRU_C = 8.0
D_RET = 512
RET_HEADS = 4
RET_HEAD_DIM = 128
RET_CHUNK = 128
ROPE_BASE = 10000.0
D_IN = 3072
D_FF = 3072
FFN_CONV = 3
NORM_EPS = 1e-6

ADAM_LR = 0.001
ADAM_B1 = 0.9
ADAM_B2 = 0.999
ADAM_EPS = 1e-08
ADAM_WD = 0.01
ADAM_STEP = 10

SUBLANES = 8
LANES = 128
VMEM_LIMIT = 48 * 1024 * 1024
FUSED_VMEM_LIMIT = VMEM_LIMIT
FFN_BWD_VMEM_LIMIT = 56 * 1024 * 1024

ROW_TILE = 256
MIX_ROW_TILE = 256
PROJ_ROW_TILE = 512
WGRAD_ROWS = 4096
WGRAD_TILE = 1024
WGRAD_BLOCK_COLUMNS = 768

MESH = pl.DeviceIdType.MESH
ANY = pl.BlockSpec(memory_space=pl.ANY)


def _dot(a, b):
    return jnp.dot(a.astype(MXU_DTYPE), b.astype(MXU_DTYPE), preferred_element_type=F32)


def _dot_nt(a, b):
    return lax.dot_general(a.astype(MXU_DTYPE), b.astype(MXU_DTYPE), (((1,), (1,)), ((), ())),
                           preferred_element_type=F32)


def _dot_tn(a, b):
    return lax.dot_general(a.astype(MXU_DTYPE), b.astype(MXU_DTYPE), (((0,), (0,)), ((), ())),
                           preferred_element_type=F32)


def _sigmoid(x):
    return 0.5 + 0.5 * jnp.tanh(0.5 * x)


_GELU_C = math.sqrt(2.0 / math.pi)
_GELU_C3 = _GELU_C * 0.044715


def _gelu_parts(x):
    x2 = x * x
    t = jnp.tanh(x * (_GELU_C + _GELU_C3 * x2))
    cdf = 0.5 + 0.5 * t
    g = x * cdf
    dg = cdf + (0.5 * x) * (1.0 - t * t) * (_GELU_C + (3.0 * _GELU_C3) * x2)
    return g, dg


def _gelu(x):
    t = jnp.tanh(_GELU_C * (x + 0.044715 * (x * x * x)))
    return x * (0.5 * (1.0 + t))


def _softplus(x):
    return jnp.maximum(x, 0.0) + jnp.log1p(jnp.exp(-jnp.abs(x)))


def _bcast_row(x, r, rows=SUBLANES):
    return jnp.broadcast_to(x[r:r + 1, :], (rows, x.shape[1]))


def _colsum8(x):
    return jnp.broadcast_to(jnp.sum(x, axis=0, keepdims=True), (SUBLANES, x.shape[1]))


def _groups(x):
    return x.reshape(x.shape[0] // SUBLANES, SUBLANES, x.shape[1])


def _shift_down(prev8, tile, s):
    if s == 0:
        return tile
    own = pltpu.roll(_groups(tile), s, 1)
    before = jnp.concatenate([pltpu.roll(_groups(prev8), s, 1), own[:-1]], axis=0)
    row = lax.broadcasted_iota(jnp.int32, own.shape, 1)
    return jnp.where(row >= s, own, before).reshape(tile.shape)


def _shift_up(tile, next8, s):
    if s == 0:
        return tile
    own = pltpu.roll(_groups(tile), SUBLANES - s, 1)
    after = jnp.concatenate([own[1:], pltpu.roll(_groups(next8), SUBLANES - s, 1)], axis=0)
    row = lax.broadcasted_iota(jnp.int32, own.shape, 1)
    return jnp.where(row < SUBLANES - s, own, after).reshape(tile.shape)


def _group_scan(a, b, reverse, fill=lambda: None):
    n, c = a.shape
    row = lax.broadcasted_iota(jnp.int32, a.shape, 0) & (SUBLANES - 1)

    def within_group(x, shift):
        return pltpu.roll(x.reshape(n // SUBLANES, SUBLANES, c), shift, 1).reshape(n, c)

    for s in (1, 2, 4):
        if s > 1:
            fill()
        shift = (SUBLANES - s) if reverse else s
        a_sh = within_group(a, shift)
        b_sh = within_group(b, shift)
        m = (row <= SUBLANES - 1 - s) if reverse else (row >= s)
        b = jnp.where(m, a * b_sh + b, b)
        a = jnp.where(m, a * a_sh, a)
    return a, b


def _carry_scan(a_ref, b_ref, out_ref, carry0, reverse):
    n_groups = a_ref.shape[0] // SUBLANES
    carry = carry0
    for i in range(n_groups):
        r0 = ((n_groups - 1 - i) if reverse else i) * SUBLANES
        hg = a_ref[r0:r0 + SUBLANES, :] * carry + b_ref[r0:r0 + SUBLANES, :]
        out_ref[r0:r0 + SUBLANES, :] = hg
        carry = _bcast_row(hg, 0 if reverse else SUBLANES - 1)
    return carry


def _rms_fwd(h, gain):
    rstd = lax.rsqrt(jnp.mean(h * h, axis=-1, keepdims=True) + NORM_EPS)
    n = h * rstd
    return n, rstd, n * gain


def _rms_bwd(dy, n, rstd, gain):
    dn = dy * gain
    dh = rstd * (dn - n * jnp.mean(dn * n, axis=-1, keepdims=True))
    return dh, _colsum8(dy * n)


def _halo_rows(dtype):
    return SUBLANES * (4 // jnp.dtype(dtype).itemsize)


def _halo_map(tile_rows, col, halo_rows=SUBLANES):
    per = tile_rows // halo_rows
    return lambda i: (jnp.maximum(i * per - 1, 0), col)


def _resident(shape):
    return pl.BlockSpec(shape, lambda *_: (0,) * len(shape), pipeline_mode=pl.Buffered(1))


def _place():
    x, y, c = lax.axis_index("x"), lax.axis_index("y"), lax.axis_index("c")
    chips = [(1 - x, y), (x, 1 - y), (1 - x, 1 - y)]
    return x, y, c, chips


def _dev(x, y, c):
    return 4 * x + 2 * y + c


class _Copy:
    def __init__(self, make):
        self.make = make

    def start(self):
        self.make().start()

    def wait(self):
        self.make().wait()

    def wait_send(self):
        self.make().wait_send()

    def wait_recv(self):
        self.make().wait_recv()


def _remote(src, dst, send_sem, recv_sem, to):
    return _Copy(lambda: pltpu.make_async_remote_copy(src_ref=src, dst_ref=dst, send_sem=send_sem, recv_sem=recv_sem,
                                                      device_id=to, device_id_type=MESH))


def _local(src, dst, sem):
    return _Copy(lambda: pltpu.make_async_copy(src, dst, sem))


class _Comm:
    def __init__(self, operands, out_shape, sems, descs, aliases=()):
        self.operands, self.out_shape, self.sems, self.descs, self.aliases = operands, out_shape, sems, descs, aliases

    def start(self, ins, outs, sems):
        local, sends, _ = self.descs(ins, outs, sems)
        for cp in sends + local:
            cp.start()

    def wait(self, ins, outs, sems):
        local, sends, recvs = self.descs(ins, outs, sems)
        for cp in recvs:
            cp.wait_recv()
        for cp in sends:
            cp.wait_send()
        for cp in local:
            cp.wait()


def _gather_first(shards):
    n = len(shards)

    def descs(ins, outs, sems):
        send, recv, loc = sems
        x, y, c, chips = _place()
        me = _dev(x, y, c)
        targets = [(x, y, 1 - c)] + [(*chip, c) for chip in chips]
        local, sends, recvs = [], [], []
        for t in range(n):
            local.append(_local(ins[t], outs[t].at[me], loc.at[t]))
            for k, to in enumerate(targets):
                i = 4 * t + k
                sends.append(_remote(ins[t], outs[t].at[me], send.at[i], recv.at[i], to))
                recvs.append(_remote(ins[t], outs[t].at[_dev(*to)], send.at[i], recv.at[i], to))
        return local, sends, recvs

    return _Comm(list(shards), [jax.ShapeDtypeStruct((N_DEV,) + s.shape, s.dtype) for s in shards],
                 [pltpu.SemaphoreType.DMA((4 * n,)), pltpu.SemaphoreType.DMA((4 * n,)), pltpu.SemaphoreType.DMA((n,))],
                 descs)


def _gather_second(gathered):
    n = len(gathered)

    def descs(ins, outs, sems):
        send, recv = sems
        x, y, c, chips = _place()
        sends, recvs = [], []
        for t in range(n):
            for j, chip in enumerate(chips):
                i = 3 * t + j
                have, get = _dev(*chip, c), _dev(*chip, 1 - c)
                sends.append(_remote(outs[t].at[have], outs[t].at[have], send.at[i], recv.at[i], (x, y, 1 - c)))
                recvs.append(_remote(outs[t].at[have], outs[t].at[get], send.at[i], recv.at[i], (x, y, 1 - c)))
        return [], sends, recvs

    return _Comm(list(gathered), [jax.ShapeDtypeStruct(g.shape, g.dtype) for g in gathered],
                 [pltpu.SemaphoreType.DMA((3 * n,)), pltpu.SemaphoreType.DMA((3 * n,))], descs,
                 aliases=[(t, t) for t in range(n)])


def _pair_exchange(blocks):
    n = len(blocks)

    def descs(ins, outs, sems):
        send, recv = sems
        x, y, c, _ = _place()
        sends, recvs = [], []
        for t in range(n):
            for q in range(N_CHIPS):
                i = N_CHIPS * t + q
                cp = _remote(ins[t].at[2 * q + 1 - c], outs[t].at[q], send.at[i], recv.at[i], (x, y, 1 - c))
                sends.append(cp)
                recvs.append(cp)
        return [], sends, recvs

    return _Comm(list(blocks), [jax.ShapeDtypeStruct((N_CHIPS,) + b.shape[1:], b.dtype) for b in blocks],
                 [pltpu.SemaphoreType.DMA((N_CHIPS * n,)), pltpu.SemaphoreType.DMA((N_CHIPS * n,))], descs)


def _chip_exchange(blocks):
    n = len(blocks)

    def descs(ins, outs, sems):
        send, recv, loc = sems
        x, y, c, chips = _place()
        me = 2 * x + y
        local, sends, recvs = [], [], []
        for t in range(n):
            local.append(_local(ins[t].at[me], outs[t].at[me], loc.at[t]))
            for j, (px, py) in enumerate(chips):
                i = 3 * t + j
                q = 2 * px + py
                sends.append(_remote(ins[t].at[q], outs[t].at[me], send.at[i], recv.at[i], (px, py, c)))
                recvs.append(_remote(ins[t].at[q], outs[t].at[q], send.at[i], recv.at[i], (px, py, c)))
        return local, sends, recvs

    return _Comm(list(blocks), [jax.ShapeDtypeStruct(b.shape, b.dtype) for b in blocks],
                 [pltpu.SemaphoreType.DMA((3 * n,)), pltpu.SemaphoreType.DMA((3 * n,)), pltpu.SemaphoreType.DMA((n,))],
                 descs)


def _both(a, b):
    na, oa, sa = len(a.operands), len(a.out_shape), len(a.sems)

    def descs(ins, outs, sems):
        local_a, sends_a, recvs_a = a.descs(ins[:na], outs[:oa], sems[:sa])
        local_b, sends_b, recvs_b = b.descs(ins[na:], outs[oa:], sems[sa:])
        return local_a + local_b, sends_a + sends_b, recvs_a + recvs_b

    return _Comm(a.operands + b.operands, a.out_shape + b.out_shape, a.sems + b.sems, descs,
                 aliases=list(a.aliases) + [(na + i, oa + o) for i, o in b.aliases])


def _run_comms(comms, name):
    first = comms[0]
    n_in, n_out = len(first.operands), len(first.out_shape)

    def body(*refs):
        ins, outs, sems = refs[:n_in], refs[n_in:n_in + n_out], list(refs[n_in + n_out:])
        for k, comm in enumerate(comms):
            mine = [sems.pop(0) for _ in comm.sems]
            comm.start(ins if k == 0 else outs, outs, mine)
            comm.wait(ins if k == 0 else outs, outs, mine)

    outs = pl.pallas_call(
        body, name=name, out_shape=first.out_shape, in_specs=[ANY] * n_in, out_specs=[ANY] * n_out,
        scratch_shapes=[s for comm in comms for s in comm.sems], input_output_aliases=dict(first.aliases),
    )(*first.operands)
    return list(outs)


def _call(body, *, name, grid, in_specs, out_specs, out_shape, operands, scratch_shapes=(), comm=None, aliases=None,
          vmem_limit=VMEM_LIMIT):
    sem = ("arbitrary",) * len(grid)
    params = pltpu.CompilerParams(dimension_semantics=sem, vmem_limit_bytes=vmem_limit)
    aliases = dict(aliases or {})
    if comm is None:
        return pl.pallas_call(body, name=name, grid=grid, in_specs=in_specs, out_specs=out_specs, out_shape=out_shape,
                              scratch_shapes=list(scratch_shapes), input_output_aliases=aliases,
                              compiler_params=params)(*operands)
    n_in, n_out, n_scr = len(in_specs), len(out_specs), len(scratch_shapes)
    c_in, c_out = len(comm.operands), len(comm.out_shape)

    def wrapped(*refs):
        refs = list(refs)
        ins, refs = refs[:n_in], refs[n_in:]
        cins, refs = refs[:c_in], refs[c_in:]
        outs, refs = refs[:n_out], refs[n_out:]
        couts, refs = refs[:c_out], refs[c_out:]
        scr, csems = refs[:n_scr], refs[n_scr:]
        first = last = None
        for axis, size in enumerate(grid):
            at_first, at_last = pl.program_id(axis) == 0, pl.program_id(axis) == size - 1
            first = at_first if first is None else first & at_first
            last = at_last if last is None else last & at_last

        @pl.when(first)
        def _():
            comm.start(cins, couts, csems)

        body(*ins, *outs, *scr)

        @pl.when(last)
        def _():
            comm.wait(cins, couts, csems)

    res = pl.pallas_call(
        wrapped, name=name, grid=grid, in_specs=list(in_specs) + [ANY] * c_in, out_specs=list(out_specs) + [ANY] * c_out,
        out_shape=list(out_shape) + list(comm.out_shape), scratch_shapes=list(scratch_shapes) + list(comm.sems),
        input_output_aliases={**aliases, **{n_in + i: n_out + o for i, o in comm.aliases}}, compiler_params=params,
    )(*operands, *comm.operands)
    return list(res[:n_out]), list(res[n_out:])


def _mm_tn(a, b, name, blocks=1, comm=None):
    t, m = a.shape
    n = b.shape[1]
    tk = min(WGRAD_ROWS, t)
    nk = t // tk
    cb = n // blocks
    per = max(1, WGRAD_BLOCK_COLUMNS // cb) if blocks > 1 else 1
    tn = per * cb if blocks > 1 else min(WGRAD_TILE, n)
    tm = min(WGRAD_TILE, m)
    assert blocks == 1 or tm == m

    def body(a_ref, b_ref, o_ref, acc):
        k = pl.program_id(2)

        @pl.when(k == 0)
        def _():
            acc[...] = jnp.zeros_like(acc)
        acc[...] += _dot_tn(a_ref[...], b_ref[...])

        @pl.when(k == nk - 1)
        def _():
            if blocks == 1:
                o_ref[...] = acc[...].astype(o_ref.dtype)
            else:
                for s in range(per):
                    o_ref[s] = acc[:, s * cb:(s + 1) * cb].astype(o_ref.dtype)

    if blocks == 1:
        out_spec = pl.BlockSpec((tm, tn), lambda i, j, k: (i, j))
        out_shape = jax.ShapeDtypeStruct((m, n), GRAD_DTYPE)
    else:
        out_spec = pl.BlockSpec((per, m, cb), lambda i, j, k: (j, 0, 0))
        out_shape = jax.ShapeDtypeStruct((blocks, m, cb), GRAD_DTYPE)
    res = _call(body, name=name, grid=(m // tm, n // tn, nk), comm=comm,
                in_specs=[pl.BlockSpec((tk, tm), lambda i, j, k: (k, i)), pl.BlockSpec((tk, tn), lambda i, j, k: (k, j))],
                out_specs=[out_spec], out_shape=[out_shape], operands=[a, b],
                scratch_shapes=[pltpu.VMEM((tm, tn), F32)])
    return res[0] if comm is None else (res[0][0], res[1])


INPROJ_TN = 1024


def _inproj_fwd(x, g1, w_blocks, comm):
    t = x.shape[0]
    tm = min(PROJ_ROW_TILE, t)
    nb, _, cb = w_blocks.shape

    def body(x_ref, g_ref, w_hbm, u_ref, p_ref, w_all, sems):
        @pl.when(pl.program_id(0) == 0)
        def _():
            copies = [pltpu.make_async_copy(w_hbm.at[d], w_all.at[:, pl.ds(d * cb, cb)], sems.at[d]) for d in range(nb)]
            for cp in copies:
                cp.start()
            for cp in copies:
                cp.wait()

        _, _, u = _rms_fwd(x_ref[...], g_ref[...])
        u = u.astype(MXU_DTYPE)
        u_ref[...] = u
        for lo in range(0, D_IN, INPROJ_TN):
            p_ref[:, lo:lo + INPROJ_TN] = _dot(u, w_all[:, lo:lo + INPROJ_TN]).astype(p_ref.dtype)

    return _call(body, name="inproj_fwd", grid=(t // tm,), comm=comm,
                 in_specs=[pl.BlockSpec((tm, D_MODEL), lambda i: (i, 0)), pl.BlockSpec((1, D_MODEL), lambda i: (0, 0)), ANY],
                 out_specs=[pl.BlockSpec((tm, D_MODEL), lambda i: (i, 0)), pl.BlockSpec((tm, D_IN), lambda i: (i, 0))],
                 out_shape=[jax.ShapeDtypeStruct((t, D_MODEL), MXU_DTYPE), jax.ShapeDtypeStruct((t, D_IN), MXU_DTYPE)],
                 scratch_shapes=[pltpu.VMEM((D_MODEL, nb * cb), w_blocks.dtype), pltpu.SemaphoreType.DMA((nb,))],
                 operands=[x, g1, w_blocks])


def _lru_gates(xc, wa, ba, wx, bx, sp, fill=lambda: None):
    r = _sigmoid(_dot(xc, wa) + ba)
    fill()
    ig = _sigmoid(_dot(xc, wx) + bx)
    fill()
    log_a = (-LRU_C) * r * sp
    a = jnp.exp(log_a)
    m = jnp.sqrt(-jnp.tanh(log_a) * (a * a + 1.0))
    return r, ig, a, m


def _fused(parts, name, comm=None):
    grid = parts[0]["grid"]
    assert all(p["grid"] == grid for p in parts)
    counts = [(len(p["in_specs"]), len(p["out_specs"]), len(p.get("scratch_shapes", ()))) for p in parts]

    def body(*refs):
        refs = list(refs)
        groups = []
        for kind in range(3):
            taken = []
            for c in counts:
                taken.append(refs[:c[kind]])
                refs = refs[c[kind]:]
            groups.append(taken)
        ins, outs, scr = groups
        pending = []

        def fill(n=None):
            for _ in range(share if n is None else n):
                if pending:
                    pending.pop(0)()

        ctx = dict(outs=outs, scratch=scr, fill=fill)
        run = lambda key: [p[key](*ins[k], *outs[k], *scr[k], ctx) for k, p in enumerate(parts) if key in p]
        run("head")
        for pieces in run("units"):
            pending.extend(pieces)
        points = sum(p.get("fill_points", 0) for p in parts)
        share = -(-len(pending) // max(points, 1))
        run("body")
        fill(len(pending))
        run("tail")

    cat = lambda key: [x for p in parts for x in p.get(key, ())]
    res = _call(body, name=name, grid=grid, comm=comm, vmem_limit=FUSED_VMEM_LIMIT,
                in_specs=cat("in_specs"), out_specs=cat("out_specs"),
                out_shape=cat("out_shape"), scratch_shapes=cat("scratch_shapes"), operands=cat("operands"))
    outs, side = (res if comm is not None else (res, None))
    split, at = [], 0
    for _, n_out, _ in counts:
        split.append(list(outs[at:at + n_out]))
        at += n_out
    return split if comm is None else (split, side)


def _lru_fwd(proj, conv_w, conv_b, wa, ba, wx, bx, lam, gain):
    t = proj.shape[0]
    tm = min(MIX_ROW_TILE, t)
    c = D_LRU

    def body(x_ref, xh_ref, g_ref, cw_ref, cb_ref, wa_ref, ba_ref, wx_ref, bx_ref, lam_ref, gain_ref,
             xc_ref, h_ref, y_ref, a_scr, b_scr, carry, ctx):
        fill = ctx["fill"]
        i = pl.program_id(0)

        @pl.when(i == 0)
        def _():
            carry[...] = jnp.zeros_like(carry)

        fill()
        x = x_ref[...].astype(F32)
        prev = jnp.where(i == 0, 0.0, xh_ref[...].astype(F32)[-SUBLANES:, :])
        cw = cw_ref[...]
        xc = cb_ref[...] + cw[LRU_CONV - 1:LRU_CONV, :] * x
        for k in range(LRU_CONV - 1):
            xc = xc + cw[k:k + 1, :] * _shift_down(prev, x, LRU_CONV - 1 - k)
        xc_ref[...] = xc
        fill()
        sp = _softplus(-lam_ref[...])
        _, ig, a, m = _lru_gates(xc, wa_ref[...], ba_ref[...], wx_ref[...], bx_ref[...], sp, fill)
        fill()
        ga, gb = _group_scan(a, m * (ig * xc), reverse=False, fill=fill)
        a_scr[...] = ga
        b_scr[...] = gb
        fill()
        carry[...] = _carry_scan(a_scr, b_scr, h_ref, carry[...], reverse=False)
        fill()
        z = h_ref[...] * _gelu(g_ref[...].astype(F32))
        fill()
        _, _, y = _rms_fwd(z, gain_ref[...])
        y_ref[...] = y.astype(y_ref.dtype)

    row = lambda i: (i, 0)
    full = lambda i: (0, 0)
    vec = pl.BlockSpec((1, c), full)
    hb = _halo_rows(proj.dtype)
    return dict(body=body, grid=(t // tm,), fill_points=6,
                in_specs=[pl.BlockSpec((tm, c), row), pl.BlockSpec((hb, c), _halo_map(tm, 0, hb)),
                          pl.BlockSpec((tm, c), lambda i: (i, 1)),
                          pl.BlockSpec((LRU_CONV, c), full), vec, pl.BlockSpec((c, c), full), vec,
                          pl.BlockSpec((c, c), full), vec, vec, vec],
                out_specs=[pl.BlockSpec((tm, c), row), pl.BlockSpec((tm, c), row), pl.BlockSpec((tm, c), row)],
                out_shape=[jax.ShapeDtypeStruct((t, c), F32), jax.ShapeDtypeStruct((t, c), F32),
                           jax.ShapeDtypeStruct((t, c), MXU_DTYPE)],
                scratch_shapes=[pltpu.VMEM((tm, c), F32), pltpu.VMEM((tm, c), F32), pltpu.VMEM((SUBLANES, c), F32)],
                operands=[proj, proj, proj, conv_w, conv_b, wa, ba, wx, bx, lam, gain])


def _ret_consts():
    c = RET_CHUNK
    log_g = jnp.log1p(-jnp.exp2(-5.0 - jnp.arange(RET_HEADS, dtype=F32)))
    idx = jnp.arange(c, dtype=F32)
    diff = idx[:, None] - idx[None, :]
    decay = jnp.where(diff[None] >= 0, jnp.exp(jnp.maximum(diff, 0.0)[None] * log_g[:, None, None]), 0.0)
    zeta = jnp.exp((c - 1 - idx)[None, :] * log_g[:, None])
    xi = jnp.exp((idx + 1.0)[None, :] * log_g[:, None])
    spread = lambda v: jnp.repeat(v.T, RET_HEAD_DIM, axis=1)
    log_g_np = np.log1p(-np.exp2(-5.0 - np.arange(RET_HEADS, dtype=np.float32))).astype(np.float32)
    g_chunk = [float(np.exp(np.float32(c) * lg)) for lg in log_g_np]
    return decay, spread(xi), spread(zeta), g_chunk


def _rope_tables(t):
    pos = np.arange(t, dtype=np.float32)
    inv_freq = np.float32(ROPE_BASE) ** (-np.arange(0, RET_HEAD_DIM, 2, dtype=np.float32) / np.float32(RET_HEAD_DIM))
    ang = (pos[:, None] * inv_freq.astype(np.float32)[None, :]).astype(np.float32).astype(np.float64)
    cos, sin = np.cos(ang).astype(np.float32), np.sin(ang).astype(np.float32)
    return jnp.asarray(np.concatenate([cos, cos], axis=-1)), jnp.asarray(np.concatenate([-sin, sin], axis=-1))


def _rope(x, cos2, sin_signed):
    return x * cos2 + pltpu.roll(x, RET_HEAD_DIM // 2, 1) * sin_signed


def _rope_bwd(d, cos2, sin_signed):
    return d * cos2 + pltpu.roll(d * sin_signed, RET_HEAD_DIM // 2, 1)


RET_SCALE = RET_HEAD_DIM ** -0.5


RET_CHUNKS_PER_STEP = MIX_ROW_TILE // RET_CHUNK


def _ret_fwd(proj, cos2, sin_signed, gain):
    t = proj.shape[0]
    c, d, nh = RET_CHUNK, RET_HEAD_DIM, RET_HEADS
    n_chunks = t // c
    per = RET_CHUNKS_PER_STEP if n_chunks % RET_CHUNKS_PER_STEP == 0 else 1
    rows = per * c
    decay, xi, zeta, g_chunk = _ret_consts()

    def units(qk_ref, vg_ref, cos_ref, sin_ref, dec_ref, xi_ref, zeta_ref, gain_ref, o_ref, y_ref, st_ref, state, ctx):
        cur = [None] * nh

        def start():
            @pl.when(pl.program_id(0) == 0)
            def _():
                state[...] = jnp.zeros_like(state)
            for h in range(nh):
                cur[h] = state[h]

        def retain(s, h, keep):
            rs = slice(s * c, (s + 1) * c)
            cos2, sin_s = cos_ref[rs, :], sin_ref[rs, :]
            lo = h * d
            q = _rope(qk_ref[rs, lo:lo + d].astype(F32), cos2, sin_s)
            k = _rope(qk_ref[rs, D_RET + lo:D_RET + lo + d].astype(F32), cos2, sin_s) * RET_SCALE
            v = vg_ref[rs, lo:lo + d]
            s_prev = cur[h]
            st_ref[s, h] = s_prev
            scores = _dot_nt(q, k) * dec_ref[h]
            o = _dot(scores, v) + _dot(q * xi_ref[:, lo:lo + d], s_prev)
            cur[h] = s_prev * g_chunk[h] + _dot_tn(k * zeta_ref[:, lo:lo + d], v)
            o_ref[rs, lo:lo + d] = o
            keep["o"] = o

        def normalise(s, h, keep):
            rs = slice(s * c, (s + 1) * c)
            lo = h * d
            o = keep["o"]
            g = vg_ref[rs, D_RET + lo:D_RET + lo + d].astype(F32)
            mu = jnp.mean(o, axis=-1, keepdims=True)
            oc = o - mu
            on = oc * lax.rsqrt(jnp.mean(oc * oc, axis=-1, keepdims=True) + NORM_EPS)
            y_ref[rs, lo:lo + d] = (on * gain_ref[:, lo:lo + d] * (g * _sigmoid(g))).astype(y_ref.dtype)

        def end():
            for h in range(nh):
                state[h] = cur[h]

        pieces = [start]
        for s in range(per):
            for h in range(nh):
                keep = {}
                pieces += [lambda s=s, h=h, keep=keep: retain(s, h, keep),
                           lambda s=s, h=h, keep=keep: normalise(s, h, keep)]
        return pieces + [end]

    full2 = lambda i: (0, 0)
    return dict(units=units, grid=(n_chunks // per,),
                in_specs=[pl.BlockSpec((rows, 2 * D_RET), lambda i: (i, 1)),
                          pl.BlockSpec((rows, 2 * D_RET), lambda i: (i, 2)),
                          pl.BlockSpec((rows, d), lambda i: (i, 0)), pl.BlockSpec((rows, d), lambda i: (i, 0)),
                          pl.BlockSpec((nh, c, c), lambda i: (0, 0, 0)), pl.BlockSpec((c, D_RET), full2),
                          pl.BlockSpec((c, D_RET), full2), pl.BlockSpec((1, D_RET), full2)],
                out_specs=[pl.BlockSpec((rows, D_RET), lambda i: (i, 0)), pl.BlockSpec((rows, D_RET), lambda i: (i, 0)),
                           pl.BlockSpec((per, nh, d, d), lambda i: (i, 0, 0, 0))],
                out_shape=[jax.ShapeDtypeStruct((t, D_RET), F32), jax.ShapeDtypeStruct((t, D_RET), MXU_DTYPE),
                           jax.ShapeDtypeStruct((n_chunks, nh, d, d), F32)],
                scratch_shapes=[pltpu.VMEM((nh, d, d), F32)],
                operands=[proj, proj, cos2, sin_signed, decay, xi, zeta, gain])


def _outproj_fwd(x, y_lru, y_ret, w_out, g2, comm):
    t = x.shape[0]
    tm = min(PROJ_ROW_TILE, t)

    def body(x_ref, yl_ref, yr_ref, w_ref, g_ref, h1_ref, u2_ref):
        h1 = x_ref[...] + _dot(yl_ref[...], w_ref[:D_LRU, :]) + _dot(yr_ref[...], w_ref[D_LRU:, :])
        h1_ref[...] = h1
        _, _, u = _rms_fwd(h1, g_ref[...])
        u2_ref[...] = u.astype(u2_ref.dtype)

    row = lambda i: (i, 0)
    return _call(body, name="outproj_fwd", grid=(t // tm,), comm=comm,
                 in_specs=[pl.BlockSpec((tm, D_MODEL), row), pl.BlockSpec((tm, D_LRU), row), pl.BlockSpec((tm, D_RET), row),
                           _resident((D_MODEL, D_MODEL)), pl.BlockSpec((1, D_MODEL), lambda i: (0, 0))],
                 out_specs=[pl.BlockSpec((tm, D_MODEL), row), pl.BlockSpec((tm, D_MODEL), row)],
                 out_shape=[jax.ShapeDtypeStruct((t, D_MODEL), F32), jax.ShapeDtypeStruct((t, D_MODEL), MXU_DTYPE)],
                 operands=[x, y_lru, y_ret, w_out, g2])


FFN_TN = 768
FFN_NJ = D_FF // FFN_TN
FFN_GROUP = 4


def _ffn_fwd(u2, w_blocks, conv_w, conv_b, w_down, h1, gf, target):
    t = u2.shape[0]
    tm = min(ROW_TILE, t)
    tn, nj, group = FFN_TN, FFN_NJ, FFN_GROUP
    ng, tw = nj // group, group * tn
    hb = _halo_rows(u2.dtype)
    assert w_blocks.shape == (2 * nj, D_MODEL, tn)

    def conv(ext, col, up_ref, conv_ref, cw_ref, cb_ref, first):
        x = ext[hb:, :]
        up_ref[:, col] = x.astype(up_ref.dtype)
        prev = jnp.where(first, 0.0, ext[hb - SUBLANES:hb, :])
        cw = cw_ref[:, col]
        y = cb_ref[:, col] + cw[FFN_CONV - 1:FFN_CONV, :] * x
        for k in range(FFN_CONV - 1):
            y = y + cw[k:k + 1, :] * _shift_down(prev, x, FFN_CONV - 1 - k)
        conv_ref[:, col] = y.astype(conv_ref.dtype)
        return y

    def body(u_ref, uh_ref, w_ref, cwa_ref, cwv_ref, cba_ref, cbv_ref, wd_ref, h1_ref, gf_ref, tg_ref,
             upa_ref, upv_ref, ca_ref, cv_ref, act_ref, dh_ref, dhb_ref, dgf_ref, loss_ref, acc):
        i, jg = pl.program_id(0), pl.program_id(1)

        @pl.when((i == 0) & (jg == 0))
        def _():
            dgf_ref[...] = jnp.zeros_like(dgf_ref)
            loss_ref[...] = jnp.zeros_like(loss_ref)

        @pl.when(jg == 0)
        def _():
            acc[...] = jnp.zeros_like(acc)

        u_ext = jnp.concatenate([uh_ref[...], u_ref[...]], axis=0)

        def project(jj):
            j = jg * group + jj
            return _dot(u_ext, w_ref[j]), _dot(u_ext, w_ref[nj + j])

        down, ahead = None, project(0)
        for jj in range(group):
            col = slice(jj * tn, (jj + 1) * tn)
            j = jg * group + jj
            ext_a, ext_v = ahead
            if jj + 1 < group:
                ahead = project(jj + 1)
            a = conv(ext_a, col, upa_ref, ca_ref, cwa_ref, cba_ref, i == 0)
            v = conv(ext_v, col, upv_ref, cv_ref, cwv_ref, cbv_ref, i == 0)
            act = (_gelu(a) * v).astype(act_ref.dtype)
            act_ref[:, col] = act
            part = _dot(act, wd_ref[pl.ds(pl.multiple_of(j * tn, tn), tn), :])
            down = part if down is None else down + part
        acc[...] += down

        @pl.when(jg == ng - 1)
        def _():
            n, rstd, y = _rms_fwd(h1_ref[...] + acc[...], gf_ref[...])
            err = y - tg_ref[...]
            loss_ref[...] += (0.5 / D_MODEL) * jnp.sum(err * err)
            dh, dgf = _rms_bwd(err * (1.0 / D_MODEL), n, rstd, gf_ref[...])
            dgf_ref[...] += dgf
            dh_ref[...] = dh
            dhb_ref[...] = dh.astype(dhb_ref.dtype)

    per = tm // hb
    row = lambda i, j: (i, 0)
    const = lambda i, j: (0, 0)
    tile = pl.BlockSpec((tm, tw), lambda i, j: (i, j))
    return _call(body, name="ffn_fwd", grid=(t // tm, ng), vmem_limit=FUSED_VMEM_LIMIT,
                 in_specs=[pl.BlockSpec((tm, D_MODEL), row),
                           pl.BlockSpec((hb, D_MODEL), lambda i, j: (jnp.maximum(i * per - 1, 0), 0)),
                           _resident(w_blocks.shape),
                           pl.BlockSpec((FFN_CONV, tw), lambda i, j: (0, j)),
                           pl.BlockSpec((FFN_CONV, tw), lambda i, j: (0, j + ng)),
                           pl.BlockSpec((1, tw), lambda i, j: (0, j)), pl.BlockSpec((1, tw), lambda i, j: (0, j + ng)),
                           _resident((D_FF, D_MODEL)),
                           pl.BlockSpec((tm, D_MODEL), row), pl.BlockSpec((1, D_MODEL), const),
                           pl.BlockSpec((tm, D_MODEL), row)],
                 out_specs=[tile] * 5 + [pl.BlockSpec((tm, D_MODEL), row),
                            pl.BlockSpec((tm, D_MODEL), row), pl.BlockSpec((SUBLANES, D_MODEL), const),
                            pl.BlockSpec((SUBLANES, LANES), const)],
                 out_shape=[jax.ShapeDtypeStruct((t, D_FF), MXU_DTYPE)] * 5 + [
                            jax.ShapeDtypeStruct((t, D_MODEL), F32),
                            jax.ShapeDtypeStruct((t, D_MODEL), MXU_DTYPE), jax.ShapeDtypeStruct((SUBLANES, D_MODEL), F32),
                            jax.ShapeDtypeStruct((SUBLANES, LANES), F32)],
                 scratch_shapes=[pltpu.VMEM((tm, D_MODEL), F32)],
                 operands=[u2, u2, w_blocks, conv_w, conv_w, conv_b, conv_b, w_down, h1, gf, target])


FFN_ACC_ROWS = SUBLANES * (FFN_CONV + 1)


def _ffn_bwd(dh2, dh2_b, w_down, up_a, up_v, conv_a, conv_v, conv_w, w_up_blocks, h1, g2, comm):
    t = up_a.shape[0]
    tm = min(ROW_TILE, t)
    tn, nj, group = FFN_TN, FFN_NJ, FFN_GROUP
    ng, tw = nj // group, group * tn
    ni = t // tm
    assert w_up_blocks.shape == (2 * nj, D_MODEL, tn)

    def conv_bwd(dy, x, cw, acc_ref, carry_ref, dup_ref, col):
        nxt = carry_ref[...]
        carry_ref[...] = dy[:SUBLANES, :]
        ahead = [_shift_up(dy, nxt, FFN_CONV - 1 - k) for k in range(FFN_CONV)]
        dx = cw[FFN_CONV - 1:FFN_CONV, :] * dy
        for k in range(FFN_CONV - 1):
            dx = dx + cw[k:k + 1, :] * ahead[k]
        dx = dx.astype(dup_ref.dtype)
        dup_ref[:, col] = dx
        for k in range(FFN_CONV):
            acc_ref[k * SUBLANES:(k + 1) * SUBLANES, :] += _colsum8(ahead[k] * x)
        acc_ref[FFN_CONV * SUBLANES:, :] += _colsum8(dy)
        return dx

    def body(dh_ref, dhb_ref, wd_ref, ua_ref, uv_ref, ca_ref, cv_ref, cwa_ref, cwv_ref, wu_ref, h1_ref, g2_ref,
             dua_ref, duv_ref, acca_ref, accv_ref, dh1_ref, dh1b_ref, dg2_ref, carry_a, carry_v, du):
        i, jg = pl.program_id(0), pl.program_id(1)

        @pl.when((i == 0) & (jg == 0))
        def _():
            for ref in (acca_ref, accv_ref, carry_a, carry_v, dg2_ref):
                ref[...] = jnp.zeros_like(ref)

        dhb = dhb_ref[...]

        def through_down(jj):
            j = jg * group + jj
            return _dot_nt(dhb, wd_ref[pl.ds(pl.multiple_of(j * tn, tn), tn), :])

        part, ahead = None, through_down(0)
        for jj in range(group):
            col = slice(jj * tn, (jj + 1) * tn)
            j = jg * group + jj
            dact = ahead
            if jj + 1 < group:
                ahead = through_down(jj + 1)
            v = cv_ref[:, col].astype(F32)
            g, dg = _gelu_parts(ca_ref[:, col].astype(F32))
            da = conv_bwd(dact * v * dg, ua_ref[:, col].astype(F32), cwa_ref[:, col], acca_ref.at[j], carry_a.at[j],
                          dua_ref, col)
            dv = conv_bwd(dact * g, uv_ref[:, col].astype(F32), cwv_ref[:, col], accv_ref.at[j], carry_v.at[j],
                          duv_ref, col)
            term = _dot_nt(da, wu_ref[j]) + _dot_nt(dv, wu_ref[nj + j])
            part = term if part is None else part + term

        @pl.when(jg == 0)
        def _():
            du[...] = part

        @pl.when(jg > 0)
        def _():
            du[...] += part

        @pl.when(jg == ng - 1)
        def _():
            n, rstd, _ = _rms_fwd(h1_ref[...], g2_ref[...])
            dh1, dg2 = _rms_bwd(du[...], n, rstd, g2_ref[...])
            dh1 = dh1 + dh_ref[...]
            dg2_ref[...] += dg2
            dh1_ref[...] = dh1
            dh1b_ref[...] = dh1.astype(dh1b_ref.dtype)

    row = lambda i, j: (ni - 1 - i, 0)
    const = lambda i, j: (0, 0)
    tile = pl.BlockSpec((tm, tw), lambda i, j: (ni - 1 - i, j))
    acc = pl.BlockSpec((nj, FFN_ACC_ROWS, tn), lambda i, j: (0, 0, 0))
    return _call(body, name="ffn_bwd", grid=(ni, ng), comm=comm, vmem_limit=FFN_BWD_VMEM_LIMIT,
                 in_specs=[pl.BlockSpec((tm, D_MODEL), row), pl.BlockSpec((tm, D_MODEL), row),
                           _resident((D_FF, D_MODEL)), tile, tile, tile, tile,
                           pl.BlockSpec((FFN_CONV, tw), lambda i, j: (0, j)),
                           pl.BlockSpec((FFN_CONV, tw), lambda i, j: (0, j + ng)),
                           _resident(w_up_blocks.shape), pl.BlockSpec((tm, D_MODEL), row),
                           pl.BlockSpec((1, D_MODEL), const)],
                 out_specs=[tile, tile, acc, acc, pl.BlockSpec((tm, D_MODEL), row), pl.BlockSpec((tm, D_MODEL), row),
                            pl.BlockSpec((SUBLANES, D_MODEL), const)],
                 out_shape=[jax.ShapeDtypeStruct((t, D_FF), MXU_DTYPE), jax.ShapeDtypeStruct((t, D_FF), MXU_DTYPE),
                            jax.ShapeDtypeStruct((nj, FFN_ACC_ROWS, tn), F32),
                            jax.ShapeDtypeStruct((nj, FFN_ACC_ROWS, tn), F32),
                            jax.ShapeDtypeStruct((t, D_MODEL), F32), jax.ShapeDtypeStruct((t, D_MODEL), MXU_DTYPE),
                            jax.ShapeDtypeStruct((SUBLANES, D_MODEL), F32)],
                 scratch_shapes=[pltpu.VMEM((nj, SUBLANES, tn), F32), pltpu.VMEM((nj, SUBLANES, tn), F32),
                                 pltpu.VMEM((tm, D_MODEL), F32)],
                 operands=[dh2, dh2_b, w_down, up_a, up_v, conv_a, conv_v, conv_w, conv_w, w_up_blocks, h1, g2])


def _ret_bwd(proj, cos2, sin_signed, gain, o, states, dmix_at):
    t = proj.shape[0]
    c, d, nh = RET_CHUNK, RET_HEAD_DIM, RET_HEADS
    n_chunks = t // c
    per = RET_CHUNKS_PER_STEP if n_chunks % RET_CHUNKS_PER_STEP == 0 else 1
    rows = per * c
    n_steps = n_chunks // per
    decay, xi, zeta, g_chunk = _ret_consts()
    base = 2 * D_LRU

    def units(qk_ref, vg_ref, cos_ref, sin_ref, dec_ref, xi_ref, zeta_ref, gain_ref, o_ref, st_ref,
              dp_ref, dgain_ref, gstate, ctx):
        cur = [None] * nh
        dmix = ctx["scratch"][dmix_at[0]][dmix_at[1]]

        def start():
            @pl.when(pl.program_id(0) == 0)
            def _():
                gstate[...] = jnp.zeros_like(gstate)
                dgain_ref[...] = jnp.zeros_like(dgain_ref)
            for h in range(nh):
                cur[h] = gstate[h]

        def gate_and_norm(s, h, keep):
            rs = slice(s * c, (s + 1) * c)
            lo = h * d
            g = vg_ref[rs, D_RET + lo:D_RET + lo + d].astype(F32)
            gain_h = gain_ref[:, lo:lo + d]
            dy = dmix[rs, D_LRU + lo:D_LRU + lo + d]
            sg = _sigmoid(g)
            o_h = o_ref[rs, lo:lo + d]
            oc = o_h - jnp.mean(o_h, axis=-1, keepdims=True)
            rstd = lax.rsqrt(jnp.mean(oc * oc, axis=-1, keepdims=True) + NORM_EPS)
            on = oc * rstd
            at = base + 3 * D_RET + lo
            dp_ref[rs, at:at + d] = (dy * on * gain_h * (sg * (1.0 + g * (1.0 - sg)))).astype(dp_ref.dtype)
            don_g = dy * (g * sg)
            dgain_ref[:, lo:lo + d] += _colsum8(don_g * on)
            don = don_g * gain_h
            keep["do"] = rstd * (don - jnp.mean(don, axis=-1, keepdims=True)
                                 - on * jnp.mean(don * on, axis=-1, keepdims=True))

        def retain(s, h, keep):
            rs = slice(s * c, (s + 1) * c)
            cos2, sin_s = cos_ref[rs, :], sin_ref[rs, :]
            lo = h * d
            q = _rope(qk_ref[rs, lo:lo + d].astype(F32), cos2, sin_s)
            k = _rope(qk_ref[rs, D_RET + lo:D_RET + lo + d].astype(F32), cos2, sin_s) * RET_SCALE
            v = vg_ref[rs, lo:lo + d]
            xi_h, zeta_h, dec = xi_ref[:, lo:lo + d], zeta_ref[:, lo:lo + d], dec_ref[h]
            do = keep["do"]
            s_prev = st_ref[s, h]
            g_next = cur[h]
            p = _dot_nt(q, k) * dec
            dpm = _dot_nt(do, v) * dec
            keep["dq"] = _dot(dpm, k) + _dot_nt(do, s_prev) * xi_h
            keep["dk"] = _dot_tn(dpm, q) + _dot_nt(v, g_next) * zeta_h
            dv = _dot_tn(p, do) + _dot(k * zeta_h, g_next)
            cur[h] = g_next * g_chunk[h] + _dot_tn(q * xi_h, do)
            at = base + 2 * D_RET + lo
            dp_ref[rs, at:at + d] = dv.astype(dp_ref.dtype)

        def unrope(s, h, keep):
            rs = slice(s * c, (s + 1) * c)
            cos2, sin_s = cos_ref[rs, :], sin_ref[rs, :]
            lo = h * d
            dp_ref[rs, base + lo:base + lo + d] = _rope_bwd(keep["dq"], cos2, sin_s).astype(dp_ref.dtype)
            at = base + D_RET + lo
            dp_ref[rs, at:at + d] = _rope_bwd(keep["dk"] * RET_SCALE, cos2, sin_s).astype(dp_ref.dtype)

        def end():
            for h in range(nh):
                gstate[h] = cur[h]

        pieces = [start]
        for s in reversed(range(per)):
            for h in range(nh):
                keep = {}
                pieces += [lambda s=s, h=h, keep=keep, f=f: f(s, h, keep) for f in (gate_and_norm, retain, unrope)]
        return pieces + [end]

    rev = lambda col: (lambda i: (n_steps - 1 - i, col))
    full2 = lambda i: (0, 0)
    return dict(units=units, grid=(n_steps,),
                in_specs=[pl.BlockSpec((rows, 2 * D_RET), rev(1)), pl.BlockSpec((rows, 2 * D_RET), rev(2)),
                          pl.BlockSpec((rows, d), rev(0)), pl.BlockSpec((rows, d), rev(0)),
                          pl.BlockSpec((nh, c, c), lambda i: (0, 0, 0)), pl.BlockSpec((c, D_RET), full2),
                          pl.BlockSpec((c, D_RET), full2), pl.BlockSpec((1, D_RET), full2),
                          pl.BlockSpec((rows, D_RET), rev(0)),
                          pl.BlockSpec((per, nh, d, d), lambda i: (n_steps - 1 - i, 0, 0, 0))],
                out_specs=[pl.BlockSpec((rows, D_IN), rev(0)), pl.BlockSpec((SUBLANES, D_RET), full2)],
                out_shape=[jax.ShapeDtypeStruct((t, D_IN), MXU_DTYPE), jax.ShapeDtypeStruct((SUBLANES, D_RET), F32)],
                scratch_shapes=[pltpu.VMEM((nh, d, d), F32)],
                operands=[proj, proj, cos2, sin_signed, decay, xi, zeta, gain, o, states])


LRU_ACC = {"conv_w": 0, "conv_b": LRU_CONV, "gate_a_b": LRU_CONV + 1, "gate_x_b": LRU_CONV + 2,
           "lambda": LRU_CONV + 3, "norm_gain": LRU_CONV + 4}
LRU_ACC_ROWS = SUBLANES * (LRU_CONV + 5)


def _lru_bwd(proj, xc_all, h_all, conv_w, wa, ba, wx, bx, lam, gain, dproj_part, dmix_at):
    t = proj.shape[0]
    tm = min(MIX_ROW_TILE, t)
    c = D_LRU
    ni = t // tm

    def body(x_ref, xh_ref, g_ref, xc_ref, h_ref, hh_ref, cw_ref, wa_ref, ba_ref, wx_ref, bx_ref, lam_ref,
             gain_ref, acc_ref, dwa_ref, dwx_ref, a_scr, b_scr, mu_scr, carry_mu, carry_dxc, ctx):
        dp_ref = ctx["outs"][dproj_part][0]
        dmix = ctx["scratch"][dmix_at[0]][dmix_at[1]]
        fill = ctx["fill"]
        i = pl.program_id(0)
        r = ni - 1 - i

        @pl.when(i == 0)
        def _():
            acc_ref[...] = jnp.zeros_like(acc_ref)
            dwa_ref[...] = jnp.zeros_like(dwa_ref)
            dwx_ref[...] = jnp.zeros_like(dwx_ref)
            carry_mu[...] = jnp.zeros_like(carry_mu)
            carry_dxc[...] = jnp.zeros_like(carry_dxc)

        def add(name, val, k=0):
            lo = (LRU_ACC[name] + k) * SUBLANES
            acc_ref[lo:lo + SUBLANES, :] += _colsum8(val)

        fill()
        xc, h = xc_ref[...], h_ref[...]
        lam_v = lam_ref[...]
        sp = _softplus(-lam_v)
        rg, ig, a, m = _lru_gates(xc, wa_ref[...], ba_ref[...], wx_ref[...], bx_ref[...], sp, fill)
        gl, dgl = _gelu_parts(g_ref[...].astype(F32))
        fill()
        zn, rstd, _ = _rms_fwd(h * gl, gain_ref[...])
        dy = dmix[:, :c]
        dz, dgain = _rms_bwd(dy, zn, rstd, gain_ref[...])
        lo = LRU_ACC["norm_gain"] * SUBLANES
        acc_ref[lo:lo + SUBLANES, :] += dgain
        dp_ref[:, c:2 * c] = (dz * h * dgl).astype(dp_ref.dtype)
        dh = dz * gl
        fill()
        ga, gb = _group_scan(a, a * dh, reverse=True, fill=fill)
        a_scr[...] = ga
        b_scr[...] = gb
        mu_next_tile = carry_mu[...]
        carry_mu[...] = _carry_scan(a_scr, b_scr, mu_scr, mu_next_tile, reverse=True)
        fill()
        lam_t = dh + _shift_up(mu_scr[...], mu_next_tile, 1)
        h_prev = _shift_down(jnp.where(r == 0, 0.0, hh_ref[...]), h, 1)
        da = lam_t * h_prev
        dig = lam_t * m * xc
        dxc = lam_t * m * ig
        dlog_a = da * a - (lam_t * ig * xc) * (a * a) / m
        fill()
        dpr = dlog_a * ((-LRU_C) * sp) * rg * (1.0 - rg)
        add("lambda", dlog_a * ((-LRU_C) * rg) * (-_sigmoid(-lam_v)))
        dpi = dig * ig * (1.0 - ig)
        add("gate_a_b", dpr)
        add("gate_x_b", dpi)
        fill()
        dwa_ref[...] += _dot_tn(xc, dpr)
        dwx_ref[...] += _dot_tn(xc, dpi)
        dxc = dxc + _dot_nt(dpr, wa_ref[...]) + _dot_nt(dpi, wx_ref[...])
        fill()
        add("conv_b", dxc)
        x = x_ref[...].astype(F32)
        prev = jnp.where(r == 0, 0.0, xh_ref[...].astype(F32)[-SUBLANES:, :])
        cw = cw_ref[...]
        nxt = carry_dxc[...]
        carry_dxc[...] = dxc[:SUBLANES, :]
        dx = cw[LRU_CONV - 1:LRU_CONV, :] * dxc
        for k in range(LRU_CONV - 1):
            dx = dx + cw[k:k + 1, :] * _shift_up(dxc, nxt, LRU_CONV - 1 - k)
        fill()
        for k in range(LRU_CONV):
            add("conv_w", dxc * _shift_down(prev, x, LRU_CONV - 1 - k), k)
        dp_ref[:, :c] = dx.astype(dp_ref.dtype)

    hb = _halo_rows(proj.dtype)
    rev = lambda col: (lambda i: (ni - 1 - i, col))
    halo = lambda rows: (lambda i: (jnp.maximum((ni - 1 - i) * (tm // rows) - 1, 0), 0))
    full = lambda i: (0, 0)
    vec = pl.BlockSpec((1, c), full)
    mat = pl.BlockSpec((c, c), full)
    return dict(body=body, grid=(ni,), fill_points=16,
                in_specs=[pl.BlockSpec((tm, c), rev(0)), pl.BlockSpec((hb, c), halo(hb)), pl.BlockSpec((tm, c), rev(1)),
                          pl.BlockSpec((tm, c), rev(0)), pl.BlockSpec((tm, c), rev(0)),
                          pl.BlockSpec((SUBLANES, c), halo(SUBLANES)),
                          pl.BlockSpec((LRU_CONV, c), full), mat, vec, mat, vec, vec, vec],
                out_specs=[pl.BlockSpec((LRU_ACC_ROWS, c), full), mat, mat],
                out_shape=[jax.ShapeDtypeStruct((LRU_ACC_ROWS, c), F32), jax.ShapeDtypeStruct((c, c), F32),
                           jax.ShapeDtypeStruct((c, c), F32)],
                scratch_shapes=[pltpu.VMEM((tm, c), F32), pltpu.VMEM((tm, c), F32), pltpu.VMEM((tm, c), F32),
                                pltpu.VMEM((SUBLANES, c), F32), pltpu.VMEM((SUBLANES, c), F32)],
                operands=[proj, proj, proj, xc_all, h_all, h_all, conv_w, wa, ba, wx, bx, lam, gain])


def _mix_proj_bwd(dh1, dh1_b, w_out, w_in_blocks, x, g1, dproj_part):
    t = x.shape[0]
    tm = min(MIX_ROW_TILE, t)
    ni = t // tm
    nb, _, cb = w_in_blocks.shape
    first_free = -(-2 * D_LRU // cb)
    du = [None]

    def term(dp_ref, w_ref, d):
        part = _dot_nt(dp_ref[:, d * cb:(d + 1) * cb], w_ref[d])
        du[0] = part if du[0] is None else du[0] + part

    def head(dh_ref, dhb_ref, wo_ref, wi_ref, x_ref, g_ref, gx_ref, dg_ref, dmix, ctx):
        @pl.when(pl.program_id(0) == 0)
        def _():
            dg_ref[...] = jnp.zeros_like(dg_ref)
        dmix[...] = _dot_nt(dhb_ref[...], wo_ref[...])
        du[0] = None

    def units(dh_ref, dhb_ref, wo_ref, wi_ref, x_ref, g_ref, gx_ref, dg_ref, dmix, ctx):
        dp_ref = ctx["outs"][dproj_part][0]
        return [lambda d=d: term(dp_ref, wi_ref, d) for d in range(first_free, nb)]

    def tail(dh_ref, dhb_ref, wo_ref, wi_ref, x_ref, g_ref, gx_ref, dg_ref, dmix, ctx):
        dp_ref = ctx["outs"][dproj_part][0]
        for d in range(first_free):
            term(dp_ref, wi_ref, d)
        n, rstd, _ = _rms_fwd(x_ref[...], g_ref[...])
        dx, dg = _rms_bwd(du[0], n, rstd, g_ref[...])
        dg_ref[...] += dg
        gx_ref[...] = dx + dh_ref[...]

    row = lambda i: (ni - 1 - i, 0)
    const = lambda i: (0, 0)
    tile = pl.BlockSpec((tm, D_MODEL), row)
    return dict(head=head, units=units, tail=tail, grid=(ni,),
                in_specs=[tile, tile, _resident(w_out.shape), _resident(w_in_blocks.shape), tile,
                          pl.BlockSpec((1, D_MODEL), const)],
                out_specs=[tile, pl.BlockSpec((SUBLANES, D_MODEL), const)],
                out_shape=[jax.ShapeDtypeStruct((t, D_MODEL), F32), jax.ShapeDtypeStruct((SUBLANES, D_MODEL), F32)],
                scratch_shapes=[pltpu.VMEM((tm, D_MODEL), F32)],
                operands=[dh1, dh1_b, w_out, w_in_blocks, x, g1])


def _pair_sum(core, a, b, name):
    n, r, c = b.shape
    spec = pl.BlockSpec((None, r, c), lambda q, core: (q, 0, 0))

    def body(core_ref, a_ref, b_ref, o_ref):
        o_ref[...] = (a_ref[...].astype(F32) + b_ref[...].astype(F32)).astype(o_ref.dtype)

    return pl.pallas_call(
        body, name=name,
        grid_spec=pltpu.PrefetchScalarGridSpec(
            num_scalar_prefetch=1, grid=(n,),
            in_specs=[pl.BlockSpec((None, r, c), lambda q, core: (2 * q + core[0], 0, 0)), spec], out_specs=spec),
        out_shape=jax.ShapeDtypeStruct(b.shape, b.dtype),
        compiler_params=pltpu.CompilerParams(dimension_semantics=("arbitrary",), vmem_limit_bytes=VMEM_LIMIT),
    )(core, a, b)


ADAMW_BLOCK_BYTES = 4 * 1024 * 1024


def _sum_adamw(parts, w, m, v, name):
    n_parts, r, c = parts.shape
    tr = r
    while n_parts * tr * c * parts.dtype.itemsize > ADAMW_BLOCK_BYTES and tr % (4 * SUBLANES) == 0:
        tr //= 2

    def body(p_ref, w_ref, m_ref, v_ref, g_ref, d_ref, nm_ref, nv_ref):
        g = p_ref[0].astype(F32)
        for s in range(1, n_parts):
            g = g + p_ref[s].astype(F32)
        nm = ADAM_B1 * m_ref[...] + (1.0 - ADAM_B1) * g
        nv = ADAM_B2 * v_ref[...] + (1.0 - ADAM_B2) * (g * g)
        m_hat = nm / (1.0 - ADAM_B1 ** ADAM_STEP)
        v_hat = nv / (1.0 - ADAM_B2 ** ADAM_STEP)
        g_ref[...] = g
        d_ref[...] = -ADAM_LR * (m_hat / (jnp.sqrt(v_hat) + ADAM_EPS) + ADAM_WD * w_ref[...])
        nm_ref[...] = nm
        nv_ref[...] = nv

    row = pl.BlockSpec((tr, c), lambda i: (i, 0))
    return _call(body, name=name, grid=(r // tr,),
                 in_specs=[pl.BlockSpec((n_parts, tr, c), lambda i: (0, i, 0)), row, row, row],
                 out_specs=[row, row, row, row], out_shape=[jax.ShapeDtypeStruct((r, c), F32)] * 4,
                 operands=[parts, w, m, v])


MATRICES = ("w_in", "w_out", "ffn_up_w", "ffn_down_w")
CONVS = ("lru_conv_w", "ffn_conv_w")
REPLICATED = ("norm1_gain", "lru_conv_b", "lru_gate_a_w", "lru_gate_a_b", "lru_gate_x_w", "lru_gate_x_b", "lru_lambda",
              "lru_norm_gain", "ret_norm_gain", "norm2_gain", "ffn_conv_b", "final_norm_gain")
WEIGHTS = ("norm1_gain", "w_in", "lru_conv_w", "lru_conv_b", "lru_gate_a_w", "lru_gate_a_b", "lru_gate_x_w",
           "lru_gate_x_b", "lru_lambda", "lru_norm_gain", "ret_norm_gain", "w_out", "norm2_gain", "ffn_up_w",
           "ffn_conv_w", "ffn_conv_b", "ffn_down_w", "final_norm_gain")


def _rows(a, pad_to):
    a = a.reshape(-1, LANES)
    pad = (-a.shape[0]) % pad_to
    return jnp.pad(a, ((0, pad), (0, 0))) if pad else a


def _pack(arrays, pad_to):
    rows, layout, at = [], [], 0
    for a in arrays:
        r = _rows(a, pad_to)
        layout.append((at, a.size // LANES, a.shape))
        rows.append(r)
        at += r.shape[0]
    return jnp.concatenate(rows, axis=0), layout


def _unpack(packed, layout):
    lead = packed.shape[:-2]
    return [packed[..., at:at + n, :].reshape(lead + shape) for at, n, shape in layout]


def _conv_rows(lru, ffn, dtype, pad_to):
    lead = lru.shape[:-2]
    flat = jnp.concatenate([lru.reshape(lead + (-1,)), ffn.reshape(lead + (-1,))], axis=-1).astype(dtype)
    rows = flat.shape[-1] // LANES
    pad = (-rows) % pad_to
    return jnp.pad(flat.reshape(lead + (rows, LANES)), [(0, 0)] * len(lead) + [(0, pad), (0, 0)])


def _column_blocks(full):
    r, c = full.shape
    return full.reshape(r, N_DEV, c // N_DEV).transpose(1, 0, 2)


def _block_diag(w):
    nh, d, _ = w.shape
    eye = jnp.eye(nh, dtype=w.dtype)
    return (w[:, :, None, :] * eye[:, None, :, None]).reshape(nh * d, nh * d)


def _diag_blocks(dense, nh):
    d = dense.shape[0] // nh
    blocks = dense.reshape(nh, d, nh, d)
    return jnp.stack([blocks[h, :, h, :] for h in range(nh)], axis=0)


def kernel(x, norm1_gain, w_in, lru_conv_w, lru_conv_b, lru_gate_a_w, lru_gate_a_b, lru_gate_x_w, lru_gate_x_b, lru_lambda, lru_norm_gain, ret_norm_gain, w_out, norm2_gain, ffn_up_w, ffn_conv_w, ffn_conv_b, ffn_down_w, final_norm_gain, loss_target, m_norm1_gain, m_w_in, m_lru_conv_w, m_lru_conv_b, m_lru_gate_a_w, m_lru_gate_a_b, m_lru_gate_x_w, m_lru_gate_x_b, m_lru_lambda, m_lru_norm_gain, m_ret_norm_gain, m_w_out, m_norm2_gain, m_ffn_up_w, m_ffn_conv_w, m_ffn_conv_b, m_ffn_down_w, m_final_norm_gain, v_norm1_gain, v_w_in, v_lru_conv_w, v_lru_conv_b, v_lru_gate_a_w, v_lru_gate_a_b, v_lru_gate_x_w, v_lru_gate_x_b, v_lru_lambda, v_lru_norm_gain, v_ret_norm_gain, v_w_out, v_norm2_gain, v_ffn_up_w, v_ffn_conv_w, v_ffn_conv_b, v_ffn_down_w, v_final_norm_gain):
    args = dict(locals())
    given = {n: args[n] for n in WEIGHTS}
    out_shape = {n: given[n].shape for n in WEIGHTS}

    def plain(a):
        return a.reshape(1, -1) if a.ndim <= 2 else a[0]

    w = {n: plain(given[n]) for n in WEIGHTS}
    mom_m = {n: plain(args["m_" + n]) for n in WEIGHTS}
    mom_v = {n: plain(args["v_" + n]) for n in WEIGHTS}
    x2, target = x[0], loss_target[0]
    t = x2.shape[0]
    core = lax.axis_index("c").astype(jnp.int32).reshape(1)
    res = {}

    conv_pad = _conv_rows(w["lru_conv_w"], w["ffn_conv_w"], F32, SUBLANES)
    first = _gather_first([w["w_in"].astype(MXU_DTYPE), conv_pad])
    w_in_blocks, conv_all = _run_comms([first, _gather_second(first.out_shape)], "w_in_all_gather")
    n_lru = w["lru_conv_w"].size
    conv_flat = conv_all.reshape(N_DEV, -1)
    lru_cw = conv_flat[:, :n_lru].reshape((N_DEV,) + w["lru_conv_w"].shape).transpose(1, 0, 2).reshape(LRU_CONV, D_LRU)
    ffn_cw = conv_flat[:, n_lru:n_lru + w["ffn_conv_w"].size].reshape((N_DEV,) + w["ffn_conv_w"].shape)
    ffn_cw = ffn_cw.transpose(1, 0, 2).reshape(FFN_CONV, 2 * D_FF)

    cos2, sin_signed = _rope_tables(t)
    wa = _block_diag(w["lru_gate_a_w"]).astype(MXU_DTYPE)
    wx = _block_diag(w["lru_gate_x_w"]).astype(MXU_DTYPE)
    gf = w["final_norm_gain"]

    early = _gather_first([w["w_out"].astype(MXU_DTYPE), w["ffn_down_w"].astype(MXU_DTYPE)])
    (u1, proj), (w_out_part, down_part) = _inproj_fwd(x2, w["norm1_gain"], w_in_blocks, early)
    ((xc, h_lru, y_lru), (o_ret, y_ret, states)), (w_out_blocks, down_blocks, up_part) = _fused(
        [_lru_fwd(proj, lru_cw, w["lru_conv_b"], wa, w["lru_gate_a_b"], wx, w["lru_gate_x_b"], w["lru_lambda"],
                  w["lru_norm_gain"]),
         _ret_fwd(proj, cos2, sin_signed, w["ret_norm_gain"])],
        "mix_fwd", _both(_gather_second([w_out_part, down_part]), _gather_first([w["ffn_up_w"].astype(MXU_DTYPE)])))
    w_out_full = w_out_blocks.reshape(D_MODEL, D_MODEL)
    w_down_full = down_blocks.reshape(D_FF, D_MODEL)

    (h1, u2), (up_blocks,) = _outproj_fwd(x2, y_lru, y_ret, w_out_full, w["norm2_gain"], _gather_second([up_part]))
    up_a, up_v, conv_a, conv_v, act, dh2, dh2_b, dgf, loss_local = _ffn_fwd(u2, up_blocks, ffn_cw, w["ffn_conv_b"],
                                                                            w_down_full, h1, gf, target)

    def to_owner_chips(blocks, names, tag):
        theirs = _run_comms([_pair_exchange(blocks)], "grads_pair_exchange_" + tag)
        return [_pair_sum(core, a, b, "grads_pair_sum_" + n) for n, a, b in zip(names, blocks, theirs)]

    def adamw(name, parts):
        res[name] = _sum_adamw(parts, w[name], mom_m[name], mom_v[name], "adamw_" + name)

    g = {"final_norm_gain": dgf[0]}
    dup_a, dup_v, acc_a, acc_v, dh1, dh1_b, dg2 = _ffn_bwd(
        dh2, dh2_b, w_down_full, up_a, up_v, conv_a, conv_v, ffn_cw, up_blocks, h1, w["norm2_gain"], None)
    per_col = lambda a: a[:, ::SUBLANES].transpose(1, 0, 2).reshape(FFN_CONV + 1, D_FF)
    acc = jnp.concatenate([per_col(acc_a), per_col(acc_v)], axis=1)
    g_ffn_cw, g["ffn_conv_b"] = acc[:FFN_CONV], acc[FFN_CONV:]
    g["norm2_gain"] = dg2[:1]
    g_up = jnp.concatenate([_mm_tn(u2, dup_a, "ffn_up_wgrad_a", blocks=N_DEV // 2),
                            _mm_tn(u2, dup_v, "ffn_up_wgrad_v", blocks=N_DEV // 2)], axis=0)
    up_sums = to_owner_chips([g_up], ["ffn_up_w"], "up")
    g_down, (up_parts,) = _mm_tn(act, dh2_b, "ffn_down_wgrad", comm=_chip_exchange(up_sums))
    adamw("ffn_up_w", up_parts)
    g_out = jnp.concatenate([_mm_tn(y_lru, dh1_b, "w_out_wgrad_lru"), _mm_tn(y_ret, dh1_b, "w_out_wgrad_ret")], axis=0)
    low_sums = to_owner_chips([g_down.reshape(N_DEV, D_FF // N_DEV, D_MODEL),
                               g_out.reshape(N_DEV, D_MODEL // N_DEV, D_MODEL)], ["ffn_down_w", "w_out"], "low")
    (dproj, dgain_ret), (grad_x, dg1), (lru_acc, dwa, dwx) = _fused(
        [_ret_bwd(proj, cos2, sin_signed, w["ret_norm_gain"], o_ret, states, dmix_at=(1, 0)),
         _mix_proj_bwd(dh1, dh1_b, w_out_full, w_in_blocks, x2, w["norm1_gain"], dproj_part=0),
         _lru_bwd(proj, xc, h_lru, lru_cw, wa, w["lru_gate_a_b"], wx, w["lru_gate_x_b"], w["lru_lambda"],
                  w["lru_norm_gain"], dproj_part=0, dmix_at=(1, 0))],
        "mix_bwd")
    g["norm1_gain"] = dg1[:1]
    g["ret_norm_gain"] = dgain_ret[:1]
    lru_acc = lru_acc[::SUBLANES]
    g_lru_cw = lru_acc[:LRU_CONV]
    for name in ("conv_b", "gate_a_b", "gate_x_b", "lambda", "norm_gain"):
        g["lru_" + name] = lru_acc[LRU_ACC[name]:LRU_ACC[name] + 1]
    g["lru_gate_a_w"] = _diag_blocks(dwa, LRU_HEADS)
    g["lru_gate_x_w"] = _diag_blocks(dwx, LRU_HEADS)
    rep_packed, rep_layout = _pack([g[n] for n in REPLICATED] + [loss_local], SUBLANES)
    g_in, (down_parts, out_parts, rep_part) = _mm_tn(u1, dproj, "w_in_wgrad", blocks=N_DEV,
                                                     comm=_both(_chip_exchange(low_sums), _gather_first([rep_packed])))
    adamw("ffn_down_w", down_parts)
    adamw("w_out", out_parts)
    g_conv = _conv_rows(_column_blocks(g_lru_cw), _column_blocks(g_ffn_cw), GRAD_DTYPE, 2 * SUBLANES)
    in_sums = to_owner_chips([g_in, g_conv], ["w_in", "conv"], "in")
    in_parts, conv_parts, rep_parts = _run_comms([_both(_chip_exchange(in_sums), _gather_second([rep_part]))],
                                                 "last_grads_exchange")
    adamw("w_in", in_parts)
    pad16 = lambda d: _conv_rows(d["lru_conv_w"], d["ffn_conv_w"], F32, 2 * SUBLANES)
    conv_res = _sum_adamw(conv_parts, pad16(w), pad16(mom_m), pad16(mom_v), "adamw_conv")
    for n, lo, hi in (("lru_conv_w", 0, n_lru), ("ffn_conv_w", n_lru, n_lru + w["ffn_conv_w"].size)):
        res[n] = [r.reshape(-1)[lo:hi].reshape(w[n].shape) for r in conv_res]
    no_state = jnp.zeros_like(loss_local)
    rep_res = _sum_adamw(rep_parts, *[_pack([d[n] for n in REPLICATED] + [no_state], SUBLANES)[0]
                                      for d in (w, mom_m, mom_v)], "adamw_replicated")
    for k in range(4):
        for n, a in zip(REPLICATED, _unpack(rep_res[k], rep_layout)):
            res.setdefault(n, [None] * 4)[k] = a
    loss = _unpack(rep_res[0], rep_layout)[-1][0, 0]

    outs = [loss, grad_x[None]]
    for k in range(4):
        outs += [res[n][k].reshape(out_shape[n]) for n in WEIGHTS]
    return tuple(outs)
```

```python
import math

import numpy as np
import jax
import jax.numpy as jnp
from jax import lax
from jax.experimental import pallas as pl
from jax.experimental.pallas import tpu as pltpu

F32 = jnp.float32
BF16 = jnp.bfloat16
MXU_DTYPE = jnp.bfloat16
GRAD_DTYPE = jnp.bfloat16

N_DEV = 8
N_CHIPS = 4
D_MODEL = 1024
D_LRU = 512
LRU_HEADS = 8
LRU_CONV = 4
LRU_C = 8.0
D_RET = 512
RET_HEADS = 4
RET_HEAD_DIM = 128
RET_CHUNK = 128
ROPE_BASE = 10000.0
D_IN = 3072
D_FF = 3072
FFN_CONV = 3
NORM_EPS = 1e-6

ADAM_LR = 0.001
ADAM_B1 = 0.9
ADAM_B2 = 0.999
ADAM_EPS = 1e-08
ADAM_WD = 0.01
ADAM_STEP = 10

SUBLANES = 8
LANES = 128
VMEM_LIMIT = 48 * 1024 * 1024
FUSED_VMEM_LIMIT = VMEM_LIMIT
FFN_BWD_VMEM_LIMIT = 56 * 1024 * 1024

ROW_TILE = 256
MIX_ROW_TILE = 256
PROJ_ROW_TILE = 512
WGRAD_ROWS = 2048
WGRAD_TILE = 1024
WGRAD_BLOCK_COLUMNS = 768

MESH = pl.DeviceIdType.MESH
ANY = pl.BlockSpec(memory_space=pl.ANY)


def _dot(a, b):
    return jnp.dot(a.astype(MXU_DTYPE), b.astype(MXU_DTYPE), preferred_element_type=F32)


def _dot_nt(a, b):
    return lax.dot_general(a.astype(MXU_DTYPE), b.astype(MXU_DTYPE), (((1,), (1,)), ((), ())),
                           preferred_element_type=F32)


def _dot_tn(a, b):
    return lax.dot_general(a.astype(MXU_DTYPE), b.astype(MXU_DTYPE), (((0,), (0,)), ((), ())),
                           preferred_element_type=F32)


def _sigmoid(x):
    return 0.5 + 0.5 * jnp.tanh(0.5 * x)


_GELU_C = math.sqrt(2.0 / math.pi)
_GELU_C3 = _GELU_C * 0.044715


def _gelu_parts(x):
    x2 = x * x
    t = jnp.tanh(x * (_GELU_C + _GELU_C3 * x2))
    cdf = 0.5 + 0.5 * t
    g = x * cdf
    dg = cdf + (0.5 * x) * (1.0 - t * t) * (_GELU_C + (3.0 * _GELU_C3) * x2)
    return g, dg


def _gelu(x):
    t = jnp.tanh(_GELU_C * (x + 0.044715 * (x * x * x)))
    return x * (0.5 * (1.0 + t))


def _softplus(x):
    return jnp.maximum(x, 0.0) + jnp.log1p(jnp.exp(-jnp.abs(x)))


def _bcast_row(x, r, rows=SUBLANES):
    return jnp.broadcast_to(x[r:r + 1, :], (rows, x.shape[1]))


def _colsum8(x):
    return jnp.broadcast_to(jnp.sum(x, axis=0, keepdims=True), (SUBLANES, x.shape[1]))


def _groups(x):
    return x.reshape(x.shape[0] // SUBLANES, SUBLANES, x.shape[1])


def _shift_down(prev8, tile, s):
    if s == 0:
        return tile
    own = pltpu.roll(_groups(tile), s, 1)
    before = jnp.concatenate([pltpu.roll(_groups(prev8), s, 1), own[:-1]], axis=0)
    row = lax.broadcasted_iota(jnp.int32, own.shape, 1)
    return jnp.where(row >= s, own, before).reshape(tile.shape)


def _shift_up(tile, next8, s):
    if s == 0:
        return tile
    own = pltpu.roll(_groups(tile), SUBLANES - s, 1)
    after = jnp.concatenate([own[1:], pltpu.roll(_groups(next8), SUBLANES - s, 1)], axis=0)
    row = lax.broadcasted_iota(jnp.int32, own.shape, 1)
    return jnp.where(row < SUBLANES - s, own, after).reshape(tile.shape)


def _group_scan(a, b, reverse, fill=lambda: None):
    n, c = a.shape
    row = lax.broadcasted_iota(jnp.int32, a.shape, 0) & (SUBLANES - 1)

    def within_group(x, shift):
        return pltpu.roll(x.reshape(n // SUBLANES, SUBLANES, c), shift, 1).reshape(n, c)

    for s in (1, 2, 4):
        if s > 1:
            fill()
        shift = (SUBLANES - s) if reverse else s
        a_sh = within_group(a, shift)
        b_sh = within_group(b, shift)
        m = (row <= SUBLANES - 1 - s) if reverse else (row >= s)
        b = jnp.where(m, a * b_sh + b, b)
        a = jnp.where(m, a * a_sh, a)
    return a, b


def _carry_scan(a_ref, b_ref, out_ref, carry0, reverse):
    n_groups = a_ref.shape[0] // SUBLANES
    carry = carry0
    for i in range(n_groups):
        r0 = ((n_groups - 1 - i) if reverse else i) * SUBLANES
        hg = a_ref[r0:r0 + SUBLANES, :] * carry + b_ref[r0:r0 + SUBLANES, :]
        out_ref[r0:r0 + SUBLANES, :] = hg
        carry = _bcast_row(hg, 0 if reverse else SUBLANES - 1)
    return carry


def _rms_fwd(h, gain):
    rstd = lax.rsqrt(jnp.mean(h * h, axis=-1, keepdims=True) + NORM_EPS)
    n = h * rstd
    return n, rstd, n * gain


def _rms_bwd(dy, n, rstd, gain):
    dn = dy * gain
    dh = rstd * (dn - n * jnp.mean(dn * n, axis=-1, keepdims=True))
    return dh, _colsum8(dy * n)


def _halo_rows(dtype):
    return SUBLANES * (4 // jnp.dtype(dtype).itemsize)


def _halo_map(tile_rows, col, halo_rows=SUBLANES):
    per = tile_rows // halo_rows
    return lambda i: (jnp.maximum(i * per - 1, 0), col)


def _resident(shape):
    return pl.BlockSpec(shape, lambda *_: (0,) * len(shape), pipeline_mode=pl.Buffered(1))


def _place():
    x, y, c = lax.axis_index("x"), lax.axis_index("y"), lax.axis_index("c")
    chips = [(1 - x, y), (x, 1 - y), (1 - x, 1 - y)]
    return x, y, c, chips


def _dev(x, y, c):
    return 4 * x + 2 * y + c


class _Copy:
    def __init__(self, make):
        self.make = make

    def start(self):
        self.make().start()

    def wait(self):
        self.make().wait()

    def wait_send(self):
        self.make().wait_send()

    def wait_recv(self):
        self.make().wait_recv()


def _remote(src, dst, send_sem, recv_sem, to):
    return _Copy(lambda: pltpu.make_async_remote_copy(src_ref=src, dst_ref=dst, send_sem=send_sem, recv_sem=recv_sem,
                                                      device_id=to, device_id_type=MESH))


def _local(src, dst, sem):
    return _Copy(lambda: pltpu.make_async_copy(src, dst, sem))


class _Comm:
    def __init__(self, operands, out_shape, sems, descs, aliases=()):
        self.operands, self.out_shape, self.sems, self.descs, self.aliases = operands, out_shape, sems, descs, aliases

    def start(self, ins, outs, sems):
        local, sends, _ = self.descs(ins, outs, sems)
        for cp in sends + local:
            cp.start()

    def wait(self, ins, outs, sems):
        local, sends, recvs = self.descs(ins, outs, sems)
        for cp in recvs:
            cp.wait_recv()
        for cp in sends:
            cp.wait_send()
        for cp in local:
            cp.wait()


def _gather_first(shards):
    n = len(shards)

    def descs(ins, outs, sems):
        send, recv, loc = sems
        x, y, c, chips = _place()
        me = _dev(x, y, c)
        targets = [(x, y, 1 - c)] + [(*chip, c) for chip in chips]
        local, sends, recvs = [], [], []
        for t in range(n):
            local.append(_local(ins[t], outs[t].at[me], loc.at[t]))
            for k, to in enumerate(targets):
                i = 4 * t + k
                sends.append(_remote(ins[t], outs[t].at[me], send.at[i], recv.at[i], to))
                recvs.append(_remote(ins[t], outs[t].at[_dev(*to)], send.at[i], recv.at[i], to))
        return local, sends, recvs

    return _Comm(list(shards), [jax.ShapeDtypeStruct((N_DEV,) + s.shape, s.dtype) for s in shards],
                 [pltpu.SemaphoreType.DMA((4 * n,)), pltpu.SemaphoreType.DMA((4 * n,)), pltpu.SemaphoreType.DMA((n,))],
                 descs)


def _gather_second(gathered):
    n = len(gathered)

    def descs(ins, outs, sems):
        send, recv = sems
        x, y, c, chips = _place()
        sends, recvs = [], []
        for t in range(n):
            for j, chip in enumerate(chips):
                i = 3 * t + j
                have, get = _dev(*chip, c), _dev(*chip, 1 - c)
                sends.append(_remote(outs[t].at[have], outs[t].at[have], send.at[i], recv.at[i], (x, y, 1 - c)))
                recvs.append(_remote(outs[t].at[have], outs[t].at[get], send.at[i], recv.at[i], (x, y, 1 - c)))
        return [], sends, recvs

    return _Comm(list(gathered), [jax.ShapeDtypeStruct(g.shape, g.dtype) for g in gathered],
                 [pltpu.SemaphoreType.DMA((3 * n,)), pltpu.SemaphoreType.DMA((3 * n,))], descs,
                 aliases=[(t, t) for t in range(n)])


def _pair_exchange(blocks):
    n = len(blocks)

    def descs(ins, outs, sems):
        send, recv = sems
        x, y, c, _ = _place()
        sends, recvs = [], []
        for t in range(n):
            for q in range(N_CHIPS):
                i = N_CHIPS * t + q
                cp = _remote(ins[t].at[2 * q + 1 - c], outs[t].at[q], send.at[i], recv.at[i], (x, y, 1 - c))
                sends.append(cp)
                recvs.append(cp)
        return [], sends, recvs

    return _Comm(list(blocks), [jax.ShapeDtypeStruct((N_CHIPS,) + b.shape[1:], b.dtype) for b in blocks],
                 [pltpu.SemaphoreType.DMA((N_CHIPS * n,)), pltpu.SemaphoreType.DMA((N_CHIPS * n,))], descs)


def _chip_exchange(blocks):
    n = len(blocks)

    def descs(ins, outs, sems):
        send, recv, loc = sems
        x, y, c, chips = _place()
        me = 2 * x + y
        local, sends, recvs = [], [], []
        for t in range(n):
            local.append(_local(ins[t].at[me], outs[t].at[me], loc.at[t]))
            for j, (px, py) in enumerate(chips):
                i = 3 * t + j
                q = 2 * px + py
                sends.append(_remote(ins[t].at[q], outs[t].at[me], send.at[i], recv.at[i], (px, py, c)))
                recvs.append(_remote(ins[t].at[q], outs[t].at[q], send.at[i], recv.at[i], (px, py, c)))
        return local, sends, recvs

    return _Comm(list(blocks), [jax.ShapeDtypeStruct(b.shape, b.dtype) for b in blocks],
                 [pltpu.SemaphoreType.DMA((3 * n,)), pltpu.SemaphoreType.DMA((3 * n,)), pltpu.SemaphoreType.DMA((n,))],
                 descs)


def _both(a, b):
    na, oa, sa = len(a.operands), len(a.out_shape), len(a.sems)

    def descs(ins, outs, sems):
        local_a, sends_a, recvs_a = a.descs(ins[:na], outs[:oa], sems[:sa])
        local_b, sends_b, recvs_b = b.descs(ins[na:], outs[oa:], sems[sa:])
        return local_a + local_b, sends_a + sends_b, recvs_a + recvs_b

    return _Comm(a.operands + b.operands, a.out_shape + b.out_shape, a.sems + b.sems, descs,
                 aliases=list(a.aliases) + [(na + i, oa + o) for i, o in b.aliases])


def _run_comms(comms, name):
    first = comms[0]
    n_in, n_out = len(first.operands), len(first.out_shape)

    def body(*refs):
        ins, outs, sems = refs[:n_in], refs[n_in:n_in + n_out], list(refs[n_in + n_out:])
        for k, comm in enumerate(comms):
            mine = [sems.pop(0) for _ in comm.sems]
            comm.start(ins if k == 0 else outs, outs, mine)
            comm.wait(ins if k == 0 else outs, outs, mine)

    outs = pl.pallas_call(
        body, name=name, out_shape=first.out_shape, in_specs=[ANY] * n_in, out_specs=[ANY] * n_out,
        scratch_shapes=[s for comm in comms for s in comm.sems], input_output_aliases=dict(first.aliases),
    )(*first.operands)
    return list(outs)


def _call(body, *, name, grid, in_specs, out_specs, out_shape, operands, scratch_shapes=(), comm=None, aliases=None,
          vmem_limit=VMEM_LIMIT):
    sem = ("arbitrary",) * len(grid)
    params = pltpu.CompilerParams(dimension_semantics=sem, vmem_limit_bytes=vmem_limit)
    aliases = dict(aliases or {})
    if comm is None:
        return pl.pallas_call(body, name=name, grid=grid, in_specs=in_specs, out_specs=out_specs, out_shape=out_shape,
                              scratch_shapes=list(scratch_shapes), input_output_aliases=aliases,
                              compiler_params=params)(*operands)
    n_in, n_out, n_scr = len(in_specs), len(out_specs), len(scratch_shapes)
    c_in, c_out = len(comm.operands), len(comm.out_shape)

    def wrapped(*refs):
        refs = list(refs)
        ins, refs = refs[:n_in], refs[n_in:]
        cins, refs = refs[:c_in], refs[c_in:]
        outs, refs = refs[:n_out], refs[n_out:]
        couts, refs = refs[:c_out], refs[c_out:]
        scr, csems = refs[:n_scr], refs[n_scr:]
        first = last = None
        for axis, size in enumerate(grid):
            at_first, at_last = pl.program_id(axis) == 0, pl.program_id(axis) == size - 1
            first = at_first if first is None else first & at_first
            last = at_last if last is None else last & at_last

        @pl.when(first)
        def _():
            comm.start(cins, couts, csems)

        body(*ins, *outs, *scr)

        @pl.when(last)
        def _():
            comm.wait(cins, couts, csems)

    res = pl.pallas_call(
        wrapped, name=name, grid=grid, in_specs=list(in_specs) + [ANY] * c_in, out_specs=list(out_specs) + [ANY] * c_out,
        out_shape=list(out_shape) + list(comm.out_shape), scratch_shapes=list(scratch_shapes) + list(comm.sems),
        input_output_aliases={**aliases, **{n_in + i: n_out + o for i, o in comm.aliases}}, compiler_params=params,
    )(*operands, *comm.operands)
    return list(res[:n_out]), list(res[n_out:])


def _mm_tn(a, b, name, blocks=1, comm=None, room=None, before=None):
    t, m = a.shape
    n = b.shape[1]
    tk = min(WGRAD_ROWS, t)
    nk = t // tk
    cb = n // blocks
    per = max(1, WGRAD_BLOCK_COLUMNS // cb) if blocks > 1 else 1
    tn = per * cb if blocks > 1 else min(WGRAD_TILE, n)
    tm = min(WGRAD_TILE, m)
    assert blocks == 1 or tm == m
    total = blocks if before is None and room is None else (room if before is None else before.shape[0])
    first = (total - blocks) // per if before is not None else 0
    assert blocks > 1 or total == 1

    def body(*refs):
        (a_ref, b_ref), (o_ref, acc) = refs[:2], refs[-2:]
        k = pl.program_id(2)

        @pl.when(k == 0)
        def _():
            acc[...] = jnp.zeros_like(acc)
        acc[...] += _dot_tn(a_ref[...], b_ref[...])

        @pl.when(k == nk - 1)
        def _():
            if blocks == 1:
                o_ref[...] = acc[...].astype(o_ref.dtype)
            else:
                for s in range(per):
                    o_ref[s] = acc[:, s * cb:(s + 1) * cb].astype(o_ref.dtype)

    if blocks == 1:
        out_spec = pl.BlockSpec((tm, tn), lambda i, j, k: (i, j))
        out_shape = jax.ShapeDtypeStruct((m, n), GRAD_DTYPE)
    else:
        out_spec = pl.BlockSpec((per, m, cb), lambda i, j, k: (first + j, 0, 0))
        out_shape = jax.ShapeDtypeStruct((total, m, cb), GRAD_DTYPE)
    in_specs = [pl.BlockSpec((tk, tm), lambda i, j, k: (k, i)), pl.BlockSpec((tk, tn), lambda i, j, k: (k, j))]
    given = [] if before is None else [before]
    res = _call(body, name=name, grid=(m // tm, n // tn, nk), comm=comm, in_specs=in_specs + [ANY] * len(given),
                out_specs=[out_spec], out_shape=[out_shape], operands=[a, b] + given,
                aliases={len(in_specs): 0} if given else None, scratch_shapes=[pltpu.VMEM((tm, tn), F32)])
    return res[0] if comm is None else (res[0][0], res[1])


INPROJ_TN = 1024


def _inproj_fwd(x, g1, w_blocks, comm):
    t = x.shape[0]
    tm = min(PROJ_ROW_TILE, t)
    nb, _, cb = w_blocks.shape

    def body(x_ref, g_ref, w_hbm, u_ref, p_ref, w_all, sems):
        @pl.when(pl.program_id(0) == 0)
        def _():
            copies = [pltpu.make_async_copy(w_hbm.at[d], w_all.at[:, pl.ds(d * cb, cb)], sems.at[d]) for d in range(nb)]
            for cp in copies:
                cp.start()
            for cp in copies:
                cp.wait()

        _, _, u = _rms_fwd(x_ref[...], g_ref[...])
        u = u.astype(MXU_DTYPE)
        u_ref[...] = u
        for lo in range(0, D_IN, INPROJ_TN):
            p_ref[:, lo:lo + INPROJ_TN] = _dot(u, w_all[:, lo:lo + INPROJ_TN]).astype(p_ref.dtype)

    return _call(body, name="inproj_fwd", grid=(t // tm,), comm=comm,
                 in_specs=[pl.BlockSpec((tm, D_MODEL), lambda i: (i, 0)), pl.BlockSpec((1, D_MODEL), lambda i: (0, 0)), ANY],
                 out_specs=[pl.BlockSpec((tm, D_MODEL), lambda i: (i, 0)), pl.BlockSpec((tm, D_IN), lambda i: (i, 0))],
                 out_shape=[jax.ShapeDtypeStruct((t, D_MODEL), MXU_DTYPE), jax.ShapeDtypeStruct((t, D_IN), MXU_DTYPE)],
                 scratch_shapes=[pltpu.VMEM((D_MODEL, nb * cb), w_blocks.dtype), pltpu.SemaphoreType.DMA((nb,))],
                 operands=[x, g1, w_blocks])


def _lru_gates(xc, wa, ba, wx, bx, sp, fill=lambda: None):
    r = _sigmoid(_dot(xc, wa) + ba)
    fill()
    ig = _sigmoid(_dot(xc, wx) + bx)
    fill()
    log_a = (-LRU_C) * r * sp
    a = jnp.exp(log_a)
    m = jnp.sqrt(-jnp.tanh(log_a) * (a * a + 1.0))
    return r, ig, a, m


def _fused(parts, name, comm=None):
    grid = parts[0]["grid"]
    assert all(p["grid"] == grid for p in parts)
    counts = [(len(p["in_specs"]), len(p["out_specs"]), len(p.get("scratch_shapes", ()))) for p in parts]

    def body(*refs):
        refs = list(refs)
        groups = []
        for kind in range(3):
            taken = []
            for c in counts:
                taken.append(refs[:c[kind]])
                refs = refs[c[kind]:]
            groups.append(taken)
        ins, outs, scr = groups
        pending = []

        def fill(n=None):
            for _ in range(share if n is None else n):
                if pending:
                    pending.pop(0)()

        ctx = dict(outs=outs, scratch=scr, fill=fill)
        run = lambda key: [p[key](*ins[k], *outs[k], *scr[k], ctx) for k, p in enumerate(parts) if key in p]
        run("head")
        for pieces in run("units"):
            pending.extend(pieces)
        points = sum(p.get("fill_points", 0) for p in parts)
        share = -(-len(pending) // max(points, 1))
        run("body")
        fill(len(pending))
        run("tail")

    cat = lambda key: [x for p in parts for x in p.get(key, ())]
    res = _call(body, name=name, grid=grid, comm=comm, vmem_limit=FUSED_VMEM_LIMIT,
                in_specs=cat("in_specs"), out_specs=cat("out_specs"),
                out_shape=cat("out_shape"), scratch_shapes=cat("scratch_shapes"), operands=cat("operands"))
    outs, side = (res if comm is not None else (res, None))
    split, at = [], 0
    for _, n_out, _ in counts:
        split.append(list(outs[at:at + n_out]))
        at += n_out
    return split if comm is None else (split, side)


def _lru_fwd(proj, conv_w, conv_b, wa, ba, wx, bx, lam, gain):
    t = proj.shape[0]
    tm = min(MIX_ROW_TILE, t)
    c = D_LRU

    def body(x_ref, xh_ref, g_ref, cw_ref, cb_ref, wa_ref, ba_ref, wx_ref, bx_ref, lam_ref, gain_ref,
             xc_ref, h_ref, y_ref, a_scr, b_scr, carry, ctx):
        fill = ctx["fill"]
        i = pl.program_id(0)

        @pl.when(i == 0)
        def _():
            carry[...] = jnp.zeros_like(carry)

        fill()
        x = x_ref[...].astype(F32)
        prev = jnp.where(i == 0, 0.0, xh_ref[...].astype(F32)[-SUBLANES:, :])
        cw = cw_ref[...]
        xc = cb_ref[...] + cw[LRU_CONV - 1:LRU_CONV, :] * x
        for k in range(LRU_CONV - 1):
            xc = xc + cw[k:k + 1, :] * _shift_down(prev, x, LRU_CONV - 1 - k)
        xc_ref[...] = xc
        fill()
        sp = _softplus(-lam_ref[...])
        _, ig, a, m = _lru_gates(xc, wa_ref[...], ba_ref[...], wx_ref[...], bx_ref[...], sp, fill)
        fill()
        ga, gb = _group_scan(a, m * (ig * xc), reverse=False, fill=fill)
        a_scr[...] = ga
        b_scr[...] = gb
        fill()
        carry[...] = _carry_scan(a_scr, b_scr, h_ref, carry[...], reverse=False)
        fill()
        z = h_ref[...] * _gelu(g_ref[...].astype(F32))
        fill()
        _, _, y = _rms_fwd(z, gain_ref[...])
        y_ref[...] = y.astype(y_ref.dtype)

    row = lambda i: (i, 0)
    full = lambda i: (0, 0)
    vec = pl.BlockSpec((1, c), full)
    hb = _halo_rows(proj.dtype)
    return dict(body=body, grid=(t // tm,), fill_points=6,
                in_specs=[pl.BlockSpec((tm, c), row), pl.BlockSpec((hb, c), _halo_map(tm, 0, hb)),
                          pl.BlockSpec((tm, c), lambda i: (i, 1)),
                          pl.BlockSpec((LRU_CONV, c), full), vec, pl.BlockSpec((c, c), full), vec,
                          pl.BlockSpec((c, c), full), vec, vec, vec],
                out_specs=[pl.BlockSpec((tm, c), row), pl.BlockSpec((tm, c), row), pl.BlockSpec((tm, c), row)],
                out_shape=[jax.ShapeDtypeStruct((t, c), F32), jax.ShapeDtypeStruct((t, c), F32),
                           jax.ShapeDtypeStruct((t, c), MXU_DTYPE)],
                scratch_shapes=[pltpu.VMEM((tm, c), F32), pltpu.VMEM((tm, c), F32), pltpu.VMEM((SUBLANES, c), F32)],
                operands=[proj, proj, proj, conv_w, conv_b, wa, ba, wx, bx, lam, gain])


def _ret_consts():
    c = RET_CHUNK
    log_g = jnp.log1p(-jnp.exp2(-5.0 - jnp.arange(RET_HEADS, dtype=F32)))
    idx = jnp.arange(c, dtype=F32)
    diff = idx[:, None] - idx[None, :]
    decay = jnp.where(diff[None] >= 0, jnp.exp(jnp.maximum(diff, 0.0)[None] * log_g[:, None, None]), 0.0)
    zeta = jnp.exp((c - 1 - idx)[None, :] * log_g[:, None])
    xi = jnp.exp((idx + 1.0)[None, :] * log_g[:, None])
    spread = lambda v: jnp.repeat(v.T, RET_HEAD_DIM, axis=1)
    log_g_np = np.log1p(-np.exp2(-5.0 - np.arange(RET_HEADS, dtype=np.float32))).astype(np.float32)
    g_chunk = [float(np.exp(np.float32(c) * lg)) for lg in log_g_np]
    return decay, spread(xi), spread(zeta), g_chunk


def _rope_tables(t):
    pos = np.arange(t, dtype=np.float32)
    inv_freq = np.float32(ROPE_BASE) ** (-np.arange(0, RET_HEAD_DIM, 2, dtype=np.float32) / np.float32(RET_HEAD_DIM))
    ang = (pos[:, None] * inv_freq.astype(np.float32)[None, :]).astype(np.float32).astype(np.float64)
    cos, sin = np.cos(ang).astype(np.float32), np.sin(ang).astype(np.float32)
    return jnp.asarray(np.concatenate([cos, cos], axis=-1)), jnp.asarray(np.concatenate([-sin, sin], axis=-1))


def _rope(x, cos2, sin_signed):
    return x * cos2 + pltpu.roll(x, RET_HEAD_DIM // 2, 1) * sin_signed


def _rope_bwd(d, cos2, sin_signed):
    return d * cos2 + pltpu.roll(d * sin_signed, RET_HEAD_DIM // 2, 1)


RET_SCALE = RET_HEAD_DIM ** -0.5


RET_CHUNKS_PER_STEP = MIX_ROW_TILE // RET_CHUNK


def _ret_fwd(proj, cos2, sin_signed, gain):
    t = proj.shape[0]
    c, d, nh = RET_CHUNK, RET_HEAD_DIM, RET_HEADS
    n_chunks = t // c
    per = RET_CHUNKS_PER_STEP if n_chunks % RET_CHUNKS_PER_STEP == 0 else 1
    rows = per * c
    decay, xi, zeta, g_chunk = _ret_consts()

    def units(qk_ref, vg_ref, cos_ref, sin_ref, dec_ref, xi_ref, zeta_ref, gain_ref, o_ref, y_ref, st_ref, state, ctx):
        cur = [None] * nh

        def start():
            @pl.when(pl.program_id(0) == 0)
            def _():
                state[...] = jnp.zeros_like(state)
            for h in range(nh):
                cur[h] = state[h]

        def retain(s, h, keep):
            rs = slice(s * c, (s + 1) * c)
            cos2, sin_s = cos_ref[rs, :], sin_ref[rs, :]
            lo = h * d
            q = _rope(qk_ref[rs, lo:lo + d].astype(F32), cos2, sin_s)
            k = _rope(qk_ref[rs, D_RET + lo:D_RET + lo + d].astype(F32), cos2, sin_s) * RET_SCALE
            v = vg_ref[rs, lo:lo + d]
            s_prev = cur[h]
            st_ref[s, h] = s_prev
            scores = _dot_nt(q, k) * dec_ref[h]
            o = _dot(scores, v) + _dot(q * xi_ref[:, lo:lo + d], s_prev)
            cur[h] = s_prev * g_chunk[h] + _dot_tn(k * zeta_ref[:, lo:lo + d], v)
            o_ref[rs, lo:lo + d] = o
            keep["o"] = o

        def normalise(s, h, keep):
            rs = slice(s * c, (s + 1) * c)
            lo = h * d
            o = keep["o"]
            g = vg_ref[rs, D_RET + lo:D_RET + lo + d].astype(F32)
            mu = jnp.mean(o, axis=-1, keepdims=True)
            oc = o - mu
            on = oc * lax.rsqrt(jnp.mean(oc * oc, axis=-1, keepdims=True) + NORM_EPS)
            y_ref[rs, lo:lo + d] = (on * gain_ref[:, lo:lo + d] * (g * _sigmoid(g))).astype(y_ref.dtype)

        def end():
            for h in range(nh):
                state[h] = cur[h]

        pieces = [start]
        for s in range(per):
            for h in range(nh):
                keep = {}
                pieces += [lambda s=s, h=h, keep=keep: retain(s, h, keep),
                           lambda s=s, h=h, keep=keep: normalise(s, h, keep)]
        return pieces + [end]

    full2 = lambda i: (0, 0)
    return dict(units=units, grid=(n_chunks // per,),
                in_specs=[pl.BlockSpec((rows, 2 * D_RET), lambda i: (i, 1)),
                          pl.BlockSpec((rows, 2 * D_RET), lambda i: (i, 2)),
                          pl.BlockSpec((rows, d), lambda i: (i, 0)), pl.BlockSpec((rows, d), lambda i: (i, 0)),
                          pl.BlockSpec((nh, c, c), lambda i: (0, 0, 0)), pl.BlockSpec((c, D_RET), full2),
                          pl.BlockSpec((c, D_RET), full2), pl.BlockSpec((1, D_RET), full2)],
                out_specs=[pl.BlockSpec((rows, D_RET), lambda i: (i, 0)), pl.BlockSpec((rows, D_RET), lambda i: (i, 0)),
                           pl.BlockSpec((per, nh, d, d), lambda i: (i, 0, 0, 0))],
                out_shape=[jax.ShapeDtypeStruct((t, D_RET), F32), jax.ShapeDtypeStruct((t, D_RET), MXU_DTYPE),
                           jax.ShapeDtypeStruct((n_chunks, nh, d, d), F32)],
                scratch_shapes=[pltpu.VMEM((nh, d, d), F32)],
                operands=[proj, proj, cos2, sin_signed, decay, xi, zeta, gain])


def _outproj_fwd(x, y_lru, y_ret, w_out, g2, comm):
    t = x.shape[0]
    tm = min(PROJ_ROW_TILE, t)

    def body(x_ref, yl_ref, yr_ref, w_ref, g_ref, h1_ref, u2_ref):
        h1 = x_ref[...] + _dot(yl_ref[...], w_ref[:D_LRU, :]) + _dot(yr_ref[...], w_ref[D_LRU:, :])
        h1_ref[...] = h1
        _, _, u = _rms_fwd(h1, g_ref[...])
        u2_ref[...] = u.astype(u2_ref.dtype)

    row = lambda i: (i, 0)
    return _call(body, name="outproj_fwd", grid=(t // tm,), comm=comm,
                 in_specs=[pl.BlockSpec((tm, D_MODEL), row), pl.BlockSpec((tm, D_LRU), row), pl.BlockSpec((tm, D_RET), row),
                           _resident((D_MODEL, D_MODEL)), pl.BlockSpec((1, D_MODEL), lambda i: (0, 0))],
                 out_specs=[pl.BlockSpec((tm, D_MODEL), row), pl.BlockSpec((tm, D_MODEL), row)],
                 out_shape=[jax.ShapeDtypeStruct((t, D_MODEL), F32), jax.ShapeDtypeStruct((t, D_MODEL), MXU_DTYPE)],
                 operands=[x, y_lru, y_ret, w_out, g2])


FFN_TN = 768
FFN_NJ = D_FF // FFN_TN
FFN_GROUP = 4


def _ffn_fwd(u2, w_blocks, conv_w, conv_b, w_down, h1, gf, target):
    t = u2.shape[0]
    tm = min(ROW_TILE, t)
    tn, nj, group = FFN_TN, FFN_NJ, FFN_GROUP
    ng, tw = nj // group, group * tn
    hb = _halo_rows(u2.dtype)
    assert w_blocks.shape == (2 * nj, D_MODEL, tn)

    def conv(ext, col, up_ref, conv_ref, cw_ref, cb_ref, first):
        x = ext[hb:, :]
        up_ref[:, col] = x.astype(up_ref.dtype)
        prev = jnp.where(first, 0.0, ext[hb - SUBLANES:hb, :])
        cw = cw_ref[:, col]
        y = cb_ref[:, col] + cw[FFN_CONV - 1:FFN_CONV, :] * x
        for k in range(FFN_CONV - 1):
            y = y + cw[k:k + 1, :] * _shift_down(prev, x, FFN_CONV - 1 - k)
        conv_ref[:, col] = y.astype(conv_ref.dtype)
        return y

    def body(u_ref, uh_ref, w_ref, cwa_ref, cwv_ref, cba_ref, cbv_ref, wd_ref, h1_ref, gf_ref, tg_ref,
             upa_ref, upv_ref, ca_ref, cv_ref, act_ref, dh_ref, dhb_ref, dgf_ref, loss_ref, acc):
        i, jg = pl.program_id(0), pl.program_id(1)

        @pl.when((i == 0) & (jg == 0))
        def _():
            dgf_ref[...] = jnp.zeros_like(dgf_ref)
            loss_ref[...] = jnp.zeros_like(loss_ref)

        @pl.when(jg == 0)
        def _():
            acc[...] = jnp.zeros_like(acc)

        u_ext = jnp.concatenate([uh_ref[...], u_ref[...]], axis=0)

        def project(jj):
            j = jg * group + jj
            return _dot(u_ext, w_ref[j]), _dot(u_ext, w_ref[nj + j])

        down, ahead = None, project(0)
        for jj in range(group):
            col = slice(jj * tn, (jj + 1) * tn)
            j = jg * group + jj
            ext_a, ext_v = ahead
            if jj + 1 < group:
                ahead = project(jj + 1)
            a = conv(ext_a, col, upa_ref, ca_ref, cwa_ref, cba_ref, i == 0)
            v = conv(ext_v, col, upv_ref, cv_ref, cwv_ref, cbv_ref, i == 0)
            act = (_gelu(a) * v).astype(act_ref.dtype)
            act_ref[:, col] = act
            part = _dot(act, wd_ref[pl.ds(pl.multiple_of(j * tn, tn), tn), :])
            down = part if down is None else down + part
        acc[...] += down

        @pl.when(jg == ng - 1)
        def _():
            n, rstd, y = _rms_fwd(h1_ref[...] + acc[...], gf_ref[...])
            err = y - tg_ref[...]
            loss_ref[...] += (0.5 / D_MODEL) * jnp.sum(err * err)
            dh, dgf = _rms_bwd(err * (1.0 / D_MODEL), n, rstd, gf_ref[...])
            dgf_ref[...] += dgf
            dh_ref[...] = dh
            dhb_ref[...] = dh.astype(dhb_ref.dtype)

    per = tm // hb
    row = lambda i, j: (i, 0)
    const = lambda i, j: (0, 0)
    tile = pl.BlockSpec((tm, tw), lambda i, j: (i, j))
    return _call(body, name="ffn_fwd", grid=(t // tm, ng), vmem_limit=FUSED_VMEM_LIMIT,
                 in_specs=[pl.BlockSpec((tm, D_MODEL), row),
                           pl.BlockSpec((hb, D_MODEL), lambda i, j: (jnp.maximum(i * per - 1, 0), 0)),
                           _resident(w_blocks.shape),
                           pl.BlockSpec((FFN_CONV, tw), lambda i, j: (0, j)),
                           pl.BlockSpec((FFN_CONV, tw), lambda i, j: (0, j + ng)),
                           pl.BlockSpec((1, tw), lambda i, j: (0, j)), pl.BlockSpec((1, tw), lambda i, j: (0, j + ng)),
                           _resident((D_FF, D_MODEL)),
                           pl.BlockSpec((tm, D_MODEL), row), pl.BlockSpec((1, D_MODEL), const),
                           pl.BlockSpec((tm, D_MODEL), row)],
                 out_specs=[tile] * 5 + [pl.BlockSpec((tm, D_MODEL), row),
                            pl.BlockSpec((tm, D_MODEL), row), pl.BlockSpec((SUBLANES, D_MODEL), const),
                            pl.BlockSpec((SUBLANES, LANES), const)],
                 out_shape=[jax.ShapeDtypeStruct((t, D_FF), MXU_DTYPE)] * 5 + [
                            jax.ShapeDtypeStruct((t, D_MODEL), F32),
                            jax.ShapeDtypeStruct((t, D_MODEL), MXU_DTYPE), jax.ShapeDtypeStruct((SUBLANES, D_MODEL), F32),
                            jax.ShapeDtypeStruct((SUBLANES, LANES), F32)],
                 scratch_shapes=[pltpu.VMEM((tm, D_MODEL), F32)],
                 operands=[u2, u2, w_blocks, conv_w, conv_w, conv_b, conv_b, w_down, h1, gf, target])


FFN_ACC_ROWS = SUBLANES * (FFN_CONV + 1)


def _ffn_bwd(dh2, dh2_b, w_down, up_a, up_v, conv_a, conv_v, conv_w, w_up_blocks, h1, g2, comm):
    t = up_a.shape[0]
    tm = min(ROW_TILE, t)
    tn, nj, group = FFN_TN, FFN_NJ, FFN_GROUP
    ng, tw = nj // group, group * tn
    ni = t // tm
    assert w_up_blocks.shape == (2 * nj, D_MODEL, tn)

    def conv_bwd(dy, x, cw, acc_ref, carry_ref, dup_ref, col):
        nxt = carry_ref[...]
        carry_ref[...] = dy[:SUBLANES, :]
        ahead = [_shift_up(dy, nxt, FFN_CONV - 1 - k) for k in range(FFN_CONV)]
        dx = cw[FFN_CONV - 1:FFN_CONV, :] * dy
        for k in range(FFN_CONV - 1):
            dx = dx + cw[k:k + 1, :] * ahead[k]
        dx = dx.astype(dup_ref.dtype)
        dup_ref[:, col] = dx
        for k in range(FFN_CONV):
            acc_ref[k * SUBLANES:(k + 1) * SUBLANES, :] += _colsum8(ahead[k] * x)
        acc_ref[FFN_CONV * SUBLANES:, :] += _colsum8(dy)
        return dx

    def body(dh_ref, dhb_ref, wd_ref, ua_ref, uv_ref, ca_ref, cv_ref, cwa_ref, cwv_ref, wu_ref, h1_ref, g2_ref,
             dua_ref, duv_ref, acca_ref, accv_ref, dh1_ref, dh1b_ref, dg2_ref, carry_a, carry_v, du):
        i, jg = pl.program_id(0), pl.program_id(1)

        @pl.when((i == 0) & (jg == 0))
        def _():
            for ref in (acca_ref, accv_ref, carry_a, carry_v, dg2_ref):
                ref[...] = jnp.zeros_like(ref)

        dhb = dhb_ref[...]

        def through_down(jj):
            j = jg * group + jj
            return _dot_nt(dhb, wd_ref[pl.ds(pl.multiple_of(j * tn, tn), tn), :])

        part, ahead = None, through_down(0)
        for jj in range(group):
            col = slice(jj * tn, (jj + 1) * tn)
            j = jg * group + jj
            dact = ahead
            if jj + 1 < group:
                ahead = through_down(jj + 1)
            v = cv_ref[:, col].astype(F32)
            g, dg = _gelu_parts(ca_ref[:, col].astype(F32))
            da = conv_bwd(dact * v * dg, ua_ref[:, col].astype(F32), cwa_ref[:, col], acca_ref.at[j], carry_a.at[j],
                          dua_ref, col)
            dv = conv_bwd(dact * g, uv_ref[:, col].astype(F32), cwv_ref[:, col], accv_ref.at[j], carry_v.at[j],
                          duv_ref, col)
            term = _dot_nt(da, wu_ref[j]) + _dot_nt(dv, wu_ref[nj + j])
            part = term if part is None else part + term

        @pl.when(jg == 0)
        def _():
            du[...] = part

        @pl.when(jg > 0)
        def _():
            du[...] += part

        @pl.when(jg == ng - 1)
        def _():
            n, rstd, _ = _rms_fwd(h1_ref[...], g2_ref[...])
            dh1, dg2 = _rms_bwd(du[...], n, rstd, g2_ref[...])
            dh1 = dh1 + dh_ref[...]
            dg2_ref[...] += dg2
            dh1_ref[...] = dh1
            dh1b_ref[...] = dh1.astype(dh1b_ref.dtype)

    row = lambda i, j: (ni - 1 - i, 0)
    const = lambda i, j: (0, 0)
    tile = pl.BlockSpec((tm, tw), lambda i, j: (ni - 1 - i, j))
    acc = pl.BlockSpec((nj, FFN_ACC_ROWS, tn), lambda i, j: (0, 0, 0))
    return _call(body, name="ffn_bwd", grid=(ni, ng), comm=comm, vmem_limit=FFN_BWD_VMEM_LIMIT,
                 in_specs=[pl.BlockSpec((tm, D_MODEL), row), pl.BlockSpec((tm, D_MODEL), row),
                           _resident((D_FF, D_MODEL)), tile, tile, tile, tile,
                           pl.BlockSpec((FFN_CONV, tw), lambda i, j: (0, j)),
                           pl.BlockSpec((FFN_CONV, tw), lambda i, j: (0, j + ng)),
                           _resident(w_up_blocks.shape), pl.BlockSpec((tm, D_MODEL), row),
                           pl.BlockSpec((1, D_MODEL), const)],
                 out_specs=[tile, tile, acc, acc, pl.BlockSpec((tm, D_MODEL), row), pl.BlockSpec((tm, D_MODEL), row),
                            pl.BlockSpec((SUBLANES, D_MODEL), const)],
                 out_shape=[jax.ShapeDtypeStruct((t, D_FF), MXU_DTYPE), jax.ShapeDtypeStruct((t, D_FF), MXU_DTYPE),
                            jax.ShapeDtypeStruct((nj, FFN_ACC_ROWS, tn), F32),
                            jax.ShapeDtypeStruct((nj, FFN_ACC_ROWS, tn), F32),
                            jax.ShapeDtypeStruct((t, D_MODEL), F32), jax.ShapeDtypeStruct((t, D_MODEL), MXU_DTYPE),
                            jax.ShapeDtypeStruct((SUBLANES, D_MODEL), F32)],
                 scratch_shapes=[pltpu.VMEM((nj, SUBLANES, tn), F32), pltpu.VMEM((nj, SUBLANES, tn), F32),
                                 pltpu.VMEM((tm, D_MODEL), F32)],
                 operands=[dh2, dh2_b, w_down, up_a, up_v, conv_a, conv_v, conv_w, conv_w, w_up_blocks, h1, g2])


def _ret_bwd(proj, cos2, sin_signed, gain, o, states, dmix_at):
    t = proj.shape[0]
    c, d, nh = RET_CHUNK, RET_HEAD_DIM, RET_HEADS
    n_chunks = t // c
    per = RET_CHUNKS_PER_STEP if n_chunks % RET_CHUNKS_PER_STEP == 0 else 1
    rows = per * c
    n_steps = n_chunks // per
    decay, xi, zeta, g_chunk = _ret_consts()
    base = 2 * D_LRU

    def units(qk_ref, vg_ref, cos_ref, sin_ref, dec_ref, xi_ref, zeta_ref, gain_ref, o_ref, st_ref,
              dp_ref, dgain_ref, gstate, ctx):
        cur = [None] * nh
        dmix = ctx["scratch"][dmix_at[0]][dmix_at[1]]

        def start():
            @pl.when(pl.program_id(0) == 0)
            def _():
                gstate[...] = jnp.zeros_like(gstate)
                dgain_ref[...] = jnp.zeros_like(dgain_ref)
            for h in range(nh):
                cur[h] = gstate[h]

        def gate_and_norm(s, h, keep):
            rs = slice(s * c, (s + 1) * c)
            lo = h * d
            g = vg_ref[rs, D_RET + lo:D_RET + lo + d].astype(F32)
            gain_h = gain_ref[:, lo:lo + d]
            dy = dmix[rs, D_LRU + lo:D_LRU + lo + d]
            sg = _sigmoid(g)
            o_h = o_ref[rs, lo:lo + d]
            oc = o_h - jnp.mean(o_h, axis=-1, keepdims=True)
            rstd = lax.rsqrt(jnp.mean(oc * oc, axis=-1, keepdims=True) + NORM_EPS)
            on = oc * rstd
            at = base + 3 * D_RET + lo
            dp_ref[rs, at:at + d] = (dy * on * gain_h * (sg * (1.0 + g * (1.0 - sg)))).astype(dp_ref.dtype)
            don_g = dy * (g * sg)
            dgain_ref[:, lo:lo + d] += _colsum8(don_g * on)
            don = don_g * gain_h
            keep["do"] = rstd * (don - jnp.mean(don, axis=-1, keepdims=True)
                                 - on * jnp.mean(don * on, axis=-1, keepdims=True))

        def retain(s, h, keep):
            rs = slice(s * c, (s + 1) * c)
            cos2, sin_s = cos_ref[rs, :], sin_ref[rs, :]
            lo = h * d
            q = _rope(qk_ref[rs, lo:lo + d].astype(F32), cos2, sin_s)
            k = _rope(qk_ref[rs, D_RET + lo:D_RET + lo + d].astype(F32), cos2, sin_s) * RET_SCALE
            v = vg_ref[rs, lo:lo + d]
            xi_h, zeta_h, dec = xi_ref[:, lo:lo + d], zeta_ref[:, lo:lo + d], dec_ref[h]
            do = keep["do"]
            s_prev = st_ref[s, h]
            g_next = cur[h]
            p = _dot_nt(q, k) * dec
            dpm = _dot_nt(do, v) * dec
            keep["dq"] = _dot(dpm, k) + _dot_nt(do, s_prev) * xi_h
            keep["dk"] = _dot_tn(dpm, q) + _dot_nt(v, g_next) * zeta_h
            dv = _dot_tn(p, do) + _dot(k * zeta_h, g_next)
            cur[h] = g_next * g_chunk[h] + _dot_tn(q * xi_h, do)
            at = base + 2 * D_RET + lo
            dp_ref[rs, at:at + d] = dv.astype(dp_ref.dtype)

        def unrope(s, h, keep):
            rs = slice(s * c, (s + 1) * c)
            cos2, sin_s = cos_ref[rs, :], sin_ref[rs, :]
            lo = h * d
            dp_ref[rs, base + lo:base + lo + d] = _rope_bwd(keep["dq"], cos2, sin_s).astype(dp_ref.dtype)
            at = base + D_RET + lo
            dp_ref[rs, at:at + d] = _rope_bwd(keep["dk"] * RET_SCALE, cos2, sin_s).astype(dp_ref.dtype)

        def end():
            for h in range(nh):
                gstate[h] = cur[h]

        pieces = [start]
        for s in reversed(range(per)):
            for h in range(nh):
                keep = {}
                pieces += [lambda s=s, h=h, keep=keep, f=f: f(s, h, keep) for f in (gate_and_norm, retain, unrope)]
        return pieces + [end]

    rev = lambda col: (lambda i: (n_steps - 1 - i, col))
    full2 = lambda i: (0, 0)
    return dict(units=units, grid=(n_steps,),
                in_specs=[pl.BlockSpec((rows, 2 * D_RET), rev(1)), pl.BlockSpec((rows, 2 * D_RET), rev(2)),
                          pl.BlockSpec((rows, d), rev(0)), pl.BlockSpec((rows, d), rev(0)),
                          pl.BlockSpec((nh, c, c), lambda i: (0, 0, 0)), pl.BlockSpec((c, D_RET), full2),
                          pl.BlockSpec((c, D_RET), full2), pl.BlockSpec((1, D_RET), full2),
                          pl.BlockSpec((rows, D_RET), rev(0)),
                          pl.BlockSpec((per, nh, d, d), lambda i: (n_steps - 1 - i, 0, 0, 0))],
                out_specs=[pl.BlockSpec((rows, D_IN), rev(0)), pl.BlockSpec((SUBLANES, D_RET), full2)],
                out_shape=[jax.ShapeDtypeStruct((t, D_IN), MXU_DTYPE), jax.ShapeDtypeStruct((SUBLANES, D_RET), F32)],
                scratch_shapes=[pltpu.VMEM((nh, d, d), F32)],
                operands=[proj, proj, cos2, sin_signed, decay, xi, zeta, gain, o, states])


LRU_ACC = {"conv_w": 0, "conv_b": LRU_CONV, "gate_a_b": LRU_CONV + 1, "gate_x_b": LRU_CONV + 2,
           "lambda": LRU_CONV + 3, "norm_gain": LRU_CONV + 4}
LRU_ACC_ROWS = SUBLANES * (LRU_CONV + 5)


def _lru_bwd(proj, xc_all, h_all, conv_w, wa, ba, wx, bx, lam, gain, dproj_part, dmix_at):
    t = proj.shape[0]
    tm = min(MIX_ROW_TILE, t)
    c = D_LRU
    ni = t // tm

    def body(x_ref, xh_ref, g_ref, xc_ref, h_ref, hh_ref, cw_ref, wa_ref, ba_ref, wx_ref, bx_ref, lam_ref,
             gain_ref, acc_ref, dwa_ref, dwx_ref, a_scr, b_scr, mu_scr, carry_mu, carry_dxc, ctx):
        dp_ref = ctx["outs"][dproj_part][0]
        dmix = ctx["scratch"][dmix_at[0]][dmix_at[1]]
        fill = ctx["fill"]
        i = pl.program_id(0)
        r = ni - 1 - i

        @pl.when(i == 0)
        def _():
            acc_ref[...] = jnp.zeros_like(acc_ref)
            dwa_ref[...] = jnp.zeros_like(dwa_ref)
            dwx_ref[...] = jnp.zeros_like(dwx_ref)
            carry_mu[...] = jnp.zeros_like(carry_mu)
            carry_dxc[...] = jnp.zeros_like(carry_dxc)

        def add(name, val, k=0):
            lo = (LRU_ACC[name] + k) * SUBLANES
            acc_ref[lo:lo + SUBLANES, :] += _colsum8(val)

        fill()
        xc, h = xc_ref[...], h_ref[...]
        lam_v = lam_ref[...]
        sp = _softplus(-lam_v)
        rg, ig, a, m = _lru_gates(xc, wa_ref[...], ba_ref[...], wx_ref[...], bx_ref[...], sp, fill)
        gl, dgl = _gelu_parts(g_ref[...].astype(F32))
        fill()
        zn, rstd, _ = _rms_fwd(h * gl, gain_ref[...])
        dy = dmix[:, :c]
        dz, dgain = _rms_bwd(dy, zn, rstd, gain_ref[...])
        lo = LRU_ACC["norm_gain"] * SUBLANES
        acc_ref[lo:lo + SUBLANES, :] += dgain
        dp_ref[:, c:2 * c] = (dz * h * dgl).astype(dp_ref.dtype)
        dh = dz * gl
        fill()
        ga, gb = _group_scan(a, a * dh, reverse=True, fill=fill)
        a_scr[...] = ga
        b_scr[...] = gb
        mu_next_tile = carry_mu[...]
        carry_mu[...] = _carry_scan(a_scr, b_scr, mu_scr, mu_next_tile, reverse=True)
        fill()
        lam_t = dh + _shift_up(mu_scr[...], mu_next_tile, 1)
        h_prev = _shift_down(jnp.where(r == 0, 0.0, hh_ref[...]), h, 1)
        da = lam_t * h_prev
        dig = lam_t * m * xc
        dxc = lam_t * m * ig
        dlog_a = da * a - (lam_t * ig * xc) * (a * a) / m
        fill()
        dpr = dlog_a * ((-LRU_C) * sp) * rg * (1.0 - rg)
        add("lambda", dlog_a * ((-LRU_C) * rg) * (-_sigmoid(-lam_v)))
        dpi = dig * ig * (1.0 - ig)
        add("gate_a_b", dpr)
        add("gate_x_b", dpi)
        fill()
        dwa_ref[...] += _dot_tn(xc, dpr)
        dwx_ref[...] += _dot_tn(xc, dpi)
        dxc = dxc + _dot_nt(dpr, wa_ref[...]) + _dot_nt(dpi, wx_ref[...])
        fill()
        add("conv_b", dxc)
        x = x_ref[...].astype(F32)
        prev = jnp.where(r == 0, 0.0, xh_ref[...].astype(F32)[-SUBLANES:, :])
        cw = cw_ref[...]
        nxt = carry_dxc[...]
        carry_dxc[...] = dxc[:SUBLANES, :]
        dx = cw[LRU_CONV - 1:LRU_CONV, :] * dxc
        for k in range(LRU_CONV - 1):
            dx = dx + cw[k:k + 1, :] * _shift_up(dxc, nxt, LRU_CONV - 1 - k)
        fill()
        for k in range(LRU_CONV):
            add("conv_w", dxc * _shift_down(prev, x, LRU_CONV - 1 - k), k)
        dp_ref[:, :c] = dx.astype(dp_ref.dtype)

    hb = _halo_rows(proj.dtype)
    rev = lambda col: (lambda i: (ni - 1 - i, col))
    halo = lambda rows: (lambda i: (jnp.maximum((ni - 1 - i) * (tm // rows) - 1, 0), 0))
    full = lambda i: (0, 0)
    vec = pl.BlockSpec((1, c), full)
    mat = pl.BlockSpec((c, c), full)
    return dict(body=body, grid=(ni,), fill_points=16,
                in_specs=[pl.BlockSpec((tm, c), rev(0)), pl.BlockSpec((hb, c), halo(hb)), pl.BlockSpec((tm, c), rev(1)),
                          pl.BlockSpec((tm, c), rev(0)), pl.BlockSpec((tm, c), rev(0)),
                          pl.BlockSpec((SUBLANES, c), halo(SUBLANES)),
                          pl.BlockSpec((LRU_CONV, c), full), mat, vec, mat, vec, vec, vec],
                out_specs=[pl.BlockSpec((LRU_ACC_ROWS, c), full), mat, mat],
                out_shape=[jax.ShapeDtypeStruct((LRU_ACC_ROWS, c), F32), jax.ShapeDtypeStruct((c, c), F32),
                           jax.ShapeDtypeStruct((c, c), F32)],
                scratch_shapes=[pltpu.VMEM((tm, c), F32), pltpu.VMEM((tm, c), F32), pltpu.VMEM((tm, c), F32),
                                pltpu.VMEM((SUBLANES, c), F32), pltpu.VMEM((SUBLANES, c), F32)],
                operands=[proj, proj, proj, xc_all, h_all, h_all, conv_w, wa, ba, wx, bx, lam, gain])


def _mix_proj_bwd(dh1, dh1_b, w_out, w_in_blocks, x, g1, dproj_part):
    t = x.shape[0]
    tm = min(MIX_ROW_TILE, t)
    ni = t // tm
    nb, _, cb = w_in_blocks.shape
    first_free = -(-2 * D_LRU // cb)
    du = [None]

    def term(dp_ref, w_ref, d):
        part = _dot_nt(dp_ref[:, d * cb:(d + 1) * cb], w_ref[d])
        du[0] = part if du[0] is None else du[0] + part

    def head(dh_ref, dhb_ref, wo_ref, wi_ref, x_ref, g_ref, gx_ref, dg_ref, dmix, ctx):
        @pl.when(pl.program_id(0) == 0)
        def _():
            dg_ref[...] = jnp.zeros_like(dg_ref)
        dmix[...] = _dot_nt(dhb_ref[...], wo_ref[...])
        du[0] = None

    def units(dh_ref, dhb_ref, wo_ref, wi_ref, x_ref, g_ref, gx_ref, dg_ref, dmix, ctx):
        dp_ref = ctx["outs"][dproj_part][0]
        return [lambda d=d: term(dp_ref, wi_ref, d) for d in range(first_free, nb)]

    def tail(dh_ref, dhb_ref, wo_ref, wi_ref, x_ref, g_ref, gx_ref, dg_ref, dmix, ctx):
        dp_ref = ctx["outs"][dproj_part][0]
        for d in range(first_free):
            term(dp_ref, wi_ref, d)
        n, rstd, _ = _rms_fwd(x_ref[...], g_ref[...])
        dx, dg = _rms_bwd(du[0], n, rstd, g_ref[...])
        dg_ref[...] += dg
        gx_ref[...] = dx + dh_ref[...]

    row = lambda i: (ni - 1 - i, 0)
    const = lambda i: (0, 0)
    tile = pl.BlockSpec((tm, D_MODEL), row)
    return dict(head=head, units=units, tail=tail, grid=(ni,),
                in_specs=[tile, tile, _resident(w_out.shape), _resident(w_in_blocks.shape), tile,
                          pl.BlockSpec((1, D_MODEL), const)],
                out_specs=[tile, pl.BlockSpec((SUBLANES, D_MODEL), const)],
                out_shape=[jax.ShapeDtypeStruct((t, D_MODEL), F32), jax.ShapeDtypeStruct((SUBLANES, D_MODEL), F32)],
                scratch_shapes=[pltpu.VMEM((tm, D_MODEL), F32)],
                operands=[dh1, dh1_b, w_out, w_in_blocks, x, g1])


def _pair_sum(core, a, b, name):
    n, r, c = b.shape
    spec = pl.BlockSpec((None, r, c), lambda q, core: (q, 0, 0))

    def body(core_ref, a_ref, b_ref, o_ref):
        o_ref[...] = (a_ref[...].astype(F32) + b_ref[...].astype(F32)).astype(o_ref.dtype)

    return pl.pallas_call(
        body, name=name,
        grid_spec=pltpu.PrefetchScalarGridSpec(
            num_scalar_prefetch=1, grid=(n,),
            in_specs=[pl.BlockSpec((None, r, c), lambda q, core: (2 * q + core[0], 0, 0)), spec], out_specs=spec),
        out_shape=jax.ShapeDtypeStruct(b.shape, b.dtype),
        compiler_params=pltpu.CompilerParams(dimension_semantics=("arbitrary",), vmem_limit_bytes=VMEM_LIMIT),
    )(core, a, b)


ADAMW_BLOCK_BYTES = 4 * 1024 * 1024


def _sum_adamw(parts, w, m, v, name):
    n_parts, r, c = parts.shape
    tr = r
    while n_parts * tr * c * parts.dtype.itemsize > ADAMW_BLOCK_BYTES and tr % (4 * SUBLANES) == 0:
        tr //= 2

    def body(p_ref, w_ref, m_ref, v_ref, g_ref, d_ref, nm_ref, nv_ref):
        g = p_ref[0].astype(F32)
        for s in range(1, n_parts):
            g = g + p_ref[s].astype(F32)
        nm = ADAM_B1 * m_ref[...] + (1.0 - ADAM_B1) * g
        nv = ADAM_B2 * v_ref[...] + (1.0 - ADAM_B2) * (g * g)
        m_hat = nm / (1.0 - ADAM_B1 ** ADAM_STEP)
        v_hat = nv / (1.0 - ADAM_B2 ** ADAM_STEP)
        g_ref[...] = g
        d_ref[...] = -ADAM_LR * (m_hat / (jnp.sqrt(v_hat) + ADAM_EPS) + ADAM_WD * w_ref[...])
        nm_ref[...] = nm
        nv_ref[...] = nv

    row = pl.BlockSpec((tr, c), lambda i: (i, 0))
    return _call(body, name=name, grid=(r // tr,),
                 in_specs=[pl.BlockSpec((n_parts, tr, c), lambda i: (0, i, 0)), row, row, row],
                 out_specs=[row, row, row, row], out_shape=[jax.ShapeDtypeStruct((r, c), F32)] * 4,
                 operands=[parts, w, m, v])


MATRICES = ("w_in", "w_out", "ffn_up_w", "ffn_down_w")
CONVS = ("lru_conv_w", "ffn_conv_w")
REPLICATED = ("norm1_gain", "lru_conv_b", "lru_gate_a_w", "lru_gate_a_b", "lru_gate_x_w", "lru_gate_x_b", "lru_lambda",
              "lru_norm_gain", "ret_norm_gain", "norm2_gain", "ffn_conv_b", "final_norm_gain")
WEIGHTS = ("norm1_gain", "w_in", "lru_conv_w", "lru_conv_b", "lru_gate_a_w", "lru_gate_a_b", "lru_gate_x_w",
           "lru_gate_x_b", "lru_lambda", "lru_norm_gain", "ret_norm_gain", "w_out", "norm2_gain", "ffn_up_w",
           "ffn_conv_w", "ffn_conv_b", "ffn_down_w", "final_norm_gain")


def _rows(a, pad_to):
    a = a.reshape(-1, LANES)
    pad = (-a.shape[0]) % pad_to
    return jnp.pad(a, ((0, pad), (0, 0))) if pad else a


def _pack(arrays, pad_to):
    rows, layout, at = [], [], 0
    for a in arrays:
        r = _rows(a, pad_to)
        layout.append((at, a.size // LANES, a.shape))
        rows.append(r)
        at += r.shape[0]
    return jnp.concatenate(rows, axis=0), layout


def _unpack(packed, layout):
    lead = packed.shape[:-2]
    return [packed[..., at:at + n, :].reshape(lead + shape) for at, n, shape in layout]


def _conv_rows(lru, ffn, dtype, pad_to):
    lead = lru.shape[:-2]
    flat = jnp.concatenate([lru.reshape(lead + (-1,)), ffn.reshape(lead + (-1,))], axis=-1).astype(dtype)
    rows = flat.shape[-1] // LANES
    pad = (-rows) % pad_to
    return jnp.pad(flat.reshape(lead + (rows, LANES)), [(0, 0)] * len(lead) + [(0, pad), (0, 0)])


def _column_blocks(full):
    r, c = full.shape
    return full.reshape(r, N_DEV, c // N_DEV).transpose(1, 0, 2)


def _block_diag(w):
    nh, d, _ = w.shape
    eye = jnp.eye(nh, dtype=w.dtype)
    return (w[:, :, None, :] * eye[:, None, :, None]).reshape(nh * d, nh * d)


def _diag_blocks(dense, nh):
    d = dense.shape[0] // nh
    blocks = dense.reshape(nh, d, nh, d)
    return jnp.stack([blocks[h, :, h, :] for h in range(nh)], axis=0)


def kernel(x, norm1_gain, w_in, lru_conv_w, lru_conv_b, lru_gate_a_w, lru_gate_a_b, lru_gate_x_w, lru_gate_x_b, lru_lambda, lru_norm_gain, ret_norm_gain, w_out, norm2_gain, ffn_up_w, ffn_conv_w, ffn_conv_b, ffn_down_w, final_norm_gain, loss_target, m_norm1_gain, m_w_in, m_lru_conv_w, m_lru_conv_b, m_lru_gate_a_w, m_lru_gate_a_b, m_lru_gate_x_w, m_lru_gate_x_b, m_lru_lambda, m_lru_norm_gain, m_ret_norm_gain, m_w_out, m_norm2_gain, m_ffn_up_w, m_ffn_conv_w, m_ffn_conv_b, m_ffn_down_w, m_final_norm_gain, v_norm1_gain, v_w_in, v_lru_conv_w, v_lru_conv_b, v_lru_gate_a_w, v_lru_gate_a_b, v_lru_gate_x_w, v_lru_gate_x_b, v_lru_lambda, v_lru_norm_gain, v_ret_norm_gain, v_w_out, v_norm2_gain, v_ffn_up_w, v_ffn_conv_w, v_ffn_conv_b, v_ffn_down_w, v_final_norm_gain):
    args = dict(locals())
    given = {n: args[n] for n in WEIGHTS}
    out_shape = {n: given[n].shape for n in WEIGHTS}

    def plain(a):
        return a.reshape(1, -1) if a.ndim <= 2 else a[0]

    w = {n: plain(given[n]) for n in WEIGHTS}
    mom_m = {n: plain(args["m_" + n]) for n in WEIGHTS}
    mom_v = {n: plain(args["v_" + n]) for n in WEIGHTS}
    x2, target = x[0], loss_target[0]
    t = x2.shape[0]
    core = lax.axis_index("c").astype(jnp.int32).reshape(1)
    res = {}

    conv_pad = _conv_rows(w["lru_conv_w"], w["ffn_conv_w"], F32, SUBLANES)
    first = _gather_first([w["w_in"].astype(MXU_DTYPE), conv_pad])
    w_in_blocks, conv_all = _run_comms([first, _gather_second(first.out_shape)], "w_in_all_gather")
    n_lru = w["lru_conv_w"].size
    conv_flat = conv_all.reshape(N_DEV, -1)
    lru_cw = conv_flat[:, :n_lru].reshape((N_DEV,) + w["lru_conv_w"].shape).transpose(1, 0, 2).reshape(LRU_CONV, D_LRU)
    ffn_cw = conv_flat[:, n_lru:n_lru + w["ffn_conv_w"].size].reshape((N_DEV,) + w["ffn_conv_w"].shape)
    ffn_cw = ffn_cw.transpose(1, 0, 2).reshape(FFN_CONV, 2 * D_FF)

    cos2, sin_signed = _rope_tables(t)
    wa = _block_diag(w["lru_gate_a_w"]).astype(MXU_DTYPE)
    wx = _block_diag(w["lru_gate_x_w"]).astype(MXU_DTYPE)
    gf = w["final_norm_gain"]

    early = _gather_first([w["w_out"].astype(MXU_DTYPE), w["ffn_down_w"].astype(MXU_DTYPE)])
    (u1, proj), (w_out_part, down_part) = _inproj_fwd(x2, w["norm1_gain"], w_in_blocks, early)
    ((xc, h_lru, y_lru), (o_ret, y_ret, states)), (w_out_blocks, down_blocks, up_part) = _fused(
        [_lru_fwd(proj, lru_cw, w["lru_conv_b"], wa, w["lru_gate_a_b"], wx, w["lru_gate_x_b"], w["lru_lambda"],
                  w["lru_norm_gain"]),
         _ret_fwd(proj, cos2, sin_signed, w["ret_norm_gain"])],
        "mix_fwd", _both(_gather_second([w_out_part, down_part]), _gather_first([w["ffn_up_w"].astype(MXU_DTYPE)])))
    w_out_full = w_out_blocks.reshape(D_MODEL, D_MODEL)
    w_down_full = down_blocks.reshape(D_FF, D_MODEL)

    (h1, u2), (up_blocks,) = _outproj_fwd(x2, y_lru, y_ret, w_out_full, w["norm2_gain"], _gather_second([up_part]))
    up_a, up_v, conv_a, conv_v, act, dh2, dh2_b, dgf, loss_local = _ffn_fwd(u2, up_blocks, ffn_cw, w["ffn_conv_b"],
                                                                            w_down_full, h1, gf, target)

    def to_owner_chips(blocks, names, tag):
        theirs = _run_comms([_pair_exchange(blocks)], "grads_pair_exchange_" + tag)
        return [_pair_sum(core, a, b, "grads_pair_sum_" + n) for n, a, b in zip(names, blocks, theirs)]

    def adamw(name, parts):
        res[name] = _sum_adamw(parts, w[name], mom_m[name], mom_v[name], "adamw_" + name)

    g = {"final_norm_gain": dgf[0]}
    dup_a, dup_v, acc_a, acc_v, dh1, dh1_b, dg2 = _ffn_bwd(
        dh2, dh2_b, w_down_full, up_a, up_v, conv_a, conv_v, ffn_cw, up_blocks, h1, w["norm2_gain"], None)
    per_col = lambda a: a[:, ::SUBLANES].transpose(1, 0, 2).reshape(FFN_CONV + 1, D_FF)
    acc = jnp.concatenate([per_col(acc_a), per_col(acc_v)], axis=1)
    g_ffn_cw, g["ffn_conv_b"] = acc[:FFN_CONV], acc[FFN_CONV:]
    g["norm2_gain"] = dg2[:1]
    g_up = _mm_tn(u2, dup_a, "ffn_up_wgrad_a", blocks=N_DEV // 2, room=N_DEV)
    g_up = _mm_tn(u2, dup_v, "ffn_up_wgrad_v", blocks=N_DEV // 2, before=g_up)
    up_sums = to_owner_chips([g_up], ["ffn_up_w"], "up")
    g_down, (up_parts,) = _mm_tn(act, dh2_b, "ffn_down_wgrad", comm=_chip_exchange(up_sums))
    adamw("ffn_up_w", up_parts)
    g_out = jnp.concatenate([_mm_tn(y_lru, dh1_b, "w_out_wgrad_lru"), _mm_tn(y_ret, dh1_b, "w_out_wgrad_ret")], axis=0)
    low_sums = to_owner_chips([g_down.reshape(N_DEV, D_FF // N_DEV, D_MODEL),
                               g_out.reshape(N_DEV, D_MODEL // N_DEV, D_MODEL)], ["ffn_down_w", "w_out"], "low")
    (dproj, dgain_ret), (grad_x, dg1), (lru_acc, dwa, dwx) = _fused(
        [_ret_bwd(proj, cos2, sin_signed, w["ret_norm_gain"], o_ret, states, dmix_at=(1, 0)),
         _mix_proj_bwd(dh1, dh1_b, w_out_full, w_in_blocks, x2, w["norm1_gain"], dproj_part=0),
         _lru_bwd(proj, xc, h_lru, lru_cw, wa, w["lru_gate_a_b"], wx, w["lru_gate_x_b"], w["lru_lambda"],
                  w["lru_norm_gain"], dproj_part=0, dmix_at=(1, 0))],
        "mix_bwd")
    g["norm1_gain"] = dg1[:1]
    g["ret_norm_gain"] = dgain_ret[:1]
    lru_acc = lru_acc[::SUBLANES]
    g_lru_cw = lru_acc[:LRU_CONV]
    for name in ("conv_b", "gate_a_b", "gate_x_b", "lambda", "norm_gain"):
        g["lru_" + name] = lru_acc[LRU_ACC[name]:LRU_ACC[name] + 1]
    g["lru_gate_a_w"] = _diag_blocks(dwa, LRU_HEADS)
    g["lru_gate_x_w"] = _diag_blocks(dwx, LRU_HEADS)
    rep_packed, rep_layout = _pack([g[n] for n in REPLICATED] + [loss_local], SUBLANES)
    g_in, (down_parts, out_parts, rep_part) = _mm_tn(u1, dproj, "w_in_wgrad", blocks=N_DEV,
                                                     comm=_both(_chip_exchange(low_sums), _gather_first([rep_packed])))
    adamw("ffn_down_w", down_parts)
    adamw("w_out", out_parts)
    g_conv = _conv_rows(_column_blocks(g_lru_cw), _column_blocks(g_ffn_cw), GRAD_DTYPE, 2 * SUBLANES)
    in_sums = to_owner_chips([g_in, g_conv], ["w_in", "conv"], "in")
    in_parts, conv_parts, rep_parts = _run_comms([_both(_chip_exchange(in_sums), _gather_second([rep_part]))],
                                                 "last_grads_exchange")
    adamw("w_in", in_parts)
    pad16 = lambda d: _conv_rows(d["lru_conv_w"], d["ffn_conv_w"], F32, 2 * SUBLANES)
    conv_res = _sum_adamw(conv_parts, pad16(w), pad16(mom_m), pad16(mom_v), "adamw_conv")
    for n, lo, hi in (("lru_conv_w", 0, n_lru), ("ffn_conv_w", n_lru, n_lru + w["ffn_conv_w"].size)):
        res[n] = [r.reshape(-1)[lo:hi].reshape(w[n].shape) for r in conv_res]
    no_state = jnp.zeros_like(loss_local)
    rep_res = _sum_adamw(rep_parts, *[_pack([d[n] for n in REPLICATED] + [no_state], SUBLANES)[0]
                                      for d in (w, mom_m, mom_v)], "adamw_replicated")
    for k in range(4):
        for n, a in zip(REPLICATED, _unpack(rep_res[k], rep_layout)):
            res.setdefault(n, [None] * 4)[k] = a
    loss = _unpack(rep_res[0], rep_layout)[-1][0, 0]

    outs = [loss, grad_x[None]]
    for k in range(4):
        outs += [res[n][k].reshape(out_shape[n]) for n in WEIGHTS]
    return tuple(outs)
```

```python
import math

import numpy as np
import jax
import jax.numpy as jnp
from jax import lax
from jax.experimental import pallas as pl
from jax.experimental.pallas import tpu as pltpu

F32 = jnp.float32
BF16 = jnp.bfloat16
MXU_DTYPE = jnp.bfloat16
GRAD_DTYPE = jnp.bfloat16

N_DEV = 8
N_CHIPS = 4
D_MODEL = 1024
D_LRU = 512
LRU_HEADS = 8
LRU_CONV = 4
LRU_C = 8.0
D_RET = 512
RET_HEADS = 4
RET_HEAD_DIM = 128
RET_CHUNK = 128
ROPE_BASE = 10000.0
D_IN = 3072
D_FF = 3072
FFN_CONV = 3
NORM_EPS = 1e-6

ADAM_LR = 0.001
ADAM_B1 = 0.9
ADAM_B2 = 0.999
ADAM_EPS = 1e-08
ADAM_WD = 0.01
ADAM_STEP = 10

SUBLANES = 8
LANES = 128
VMEM_LIMIT = 48 * 1024 * 1024
FUSED_VMEM_LIMIT = VMEM_LIMIT
FFN_BWD_VMEM_LIMIT = 56 * 1024 * 1024

ROW_TILE = 256
MIX_ROW_TILE = 256
PROJ_ROW_TILE = 512
WGRAD_ROWS = 2048
WGRAD_TILE = 1024
WGRAD_BLOCK_COLUMNS = 768

MESH = pl.DeviceIdType.MESH
ANY = pl.BlockSpec(memory_space=pl.ANY)


def _dot(a, b):
    return jnp.dot(a.astype(MXU_DTYPE), b.astype(MXU_DTYPE), preferred_element_type=F32)


def _dot_nt(a, b):
    return lax.dot_general(a.astype(MXU_DTYPE), b.astype(MXU_DTYPE), (((1,), (1,)), ((), ())),
                           preferred_element_type=F32)


def _dot_tn(a, b):
    return lax.dot_general(a.astype(MXU_DTYPE), b.astype(MXU_DTYPE), (((0,), (0,)), ((), ())),
                           preferred_element_type=F32)


def _sigmoid(x):
    return 0.5 + 0.5 * jnp.tanh(0.5 * x)


_GELU_C = math.sqrt(2.0 / math.pi)
_GELU_C3 = _GELU_C * 0.044715


def _gelu_parts(x):
    x2 = x * x
    t = jnp.tanh(x * (_GELU_C + _GELU_C3 * x2))
    cdf = 0.5 + 0.5 * t
    g = x * cdf
    dg = cdf + (0.5 * x) * (1.0 - t * t) * (_GELU_C + (3.0 * _GELU_C3) * x2)
    return g, dg


def _gelu(x):
    t = jnp.tanh(_GELU_C * (x + 0.044715 * (x * x * x)))
    return x * (0.5 * (1.0 + t))


def _softplus(x):
    return jnp.maximum(x, 0.0) + jnp.log1p(jnp.exp(-jnp.abs(x)))


def _bcast_row(x, r, rows=SUBLANES):
    return jnp.broadcast_to(x[r:r + 1, :], (rows, x.shape[1]))


def _colsum8(x):
    return jnp.broadcast_to(jnp.sum(x, axis=0, keepdims=True), (SUBLANES, x.shape[1]))


def _groups(x):
    return x.reshape(x.shape[0] // SUBLANES, SUBLANES, x.shape[1])


def _shift_down(prev8, tile, s):
    if s == 0:
        return tile
    own = pltpu.roll(_groups(tile), s, 1)
    before = jnp.concatenate([pltpu.roll(_groups(prev8), s, 1), own[:-1]], axis=0)
    row = lax.broadcasted_iota(jnp.int32, own.shape, 1)
    return jnp.where(row >= s, own, before).reshape(tile.shape)


def _shift_up(tile, next8, s):
    if s == 0:
        return tile
    own = pltpu.roll(_groups(tile), SUBLANES - s, 1)
    after = jnp.concatenate([own[1:], pltpu.roll(_groups(next8), SUBLANES - s, 1)], axis=0)
    row = lax.broadcasted_iota(jnp.int32, own.shape, 1)
    return jnp.where(row < SUBLANES - s, own, after).reshape(tile.shape)


def _group_scan(a, b, reverse, fill=lambda: None):
    n, c = a.shape
    row = lax.broadcasted_iota(jnp.int32, a.shape, 0) & (SUBLANES - 1)

    def within_group(x, shift):
        return pltpu.roll(x.reshape(n // SUBLANES, SUBLANES, c), shift, 1).reshape(n, c)

    for s in (1, 2, 4):
        if s > 1:
            fill()
        shift = (SUBLANES - s) if reverse else s
        a_sh = within_group(a, shift)
        b_sh = within_group(b, shift)
        m = (row <= SUBLANES - 1 - s) if reverse else (row >= s)
        b = jnp.where(m, a * b_sh + b, b)
        a = jnp.where(m, a * a_sh, a)
    return a, b


def _carry_scan(a_ref, b_ref, out_ref, carry0, reverse):
    n_groups = a_ref.shape[0] // SUBLANES
    carry = carry0
    for i in range(n_groups):
        r0 = ((n_groups - 1 - i) if reverse else i) * SUBLANES
        hg = a_ref[r0:r0 + SUBLANES, :] * carry + b_ref[r0:r0 + SUBLANES, :]
        out_ref[r0:r0 + SUBLANES, :] = hg
        carry = _bcast_row(hg, 0 if reverse else SUBLANES - 1)
    return carry


def _rms_fwd(h, gain):
    rstd = lax.rsqrt(jnp.mean(h * h, axis=-1, keepdims=True) + NORM_EPS)
    n = h * rstd
    return n, rstd, n * gain


def _rms_bwd(dy, n, rstd, gain):
    dn = dy * gain
    dh = rstd * (dn - n * jnp.mean(dn * n, axis=-1, keepdims=True))
    return dh, _colsum8(dy * n)


def _halo_rows(dtype):
    return SUBLANES * (4 // jnp.dtype(dtype).itemsize)


def _halo_map(tile_rows, col, halo_rows=SUBLANES):
    per = tile_rows // halo_rows
    return lambda i: (jnp.maximum(i * per - 1, 0), col)


def _resident(shape):
    return pl.BlockSpec(shape, lambda *_: (0,) * len(shape), pipeline_mode=pl.Buffered(1))


def _place():
    x, y, c = lax.axis_index("x"), lax.axis_index("y"), lax.axis_index("c")
    chips = [(1 - x, y), (x, 1 - y), (1 - x, 1 - y)]
    return x, y, c, chips


def _dev(x, y, c):
    return 4 * x + 2 * y + c


class _Copy:
    def __init__(self, make):
        self.make = make

    def start(self):
        self.make().start()

    def wait(self):
        self.make().wait()

    def wait_send(self):
        self.make().wait_send()

    def wait_recv(self):
        self.make().wait_recv()


def _remote(src, dst, send_sem, recv_sem, to):
    return _Copy(lambda: pltpu.make_async_remote_copy(src_ref=src, dst_ref=dst, send_sem=send_sem, recv_sem=recv_sem,
                                                      device_id=to, device_id_type=MESH))


def _local(src, dst, sem):
    return _Copy(lambda: pltpu.make_async_copy(src, dst, sem))


class _Comm:
    def __init__(self, operands, out_shape, sems, descs, aliases=()):
        self.operands, self.out_shape, self.sems, self.descs, self.aliases = operands, out_shape, sems, descs, aliases

    def start(self, ins, outs, sems):
        local, sends, _ = self.descs(ins, outs, sems)
        for cp in sends + local:
            cp.start()

    def wait(self, ins, outs, sems):
        local, sends, recvs = self.descs(ins, outs, sems)
        for cp in recvs:
            cp.wait_recv()
        for cp in sends:
            cp.wait_send()
        for cp in local:
            cp.wait()


def _gather_first(shards):
    n = len(shards)

    def descs(ins, outs, sems):
        send, recv, loc = sems
        x, y, c, chips = _place()
        me = _dev(x, y, c)
        targets = [(x, y, 1 - c)] + [(*chip, c) for chip in chips]
        local, sends, recvs = [], [], []
        for t in range(n):
            local.append(_local(ins[t], outs[t].at[me], loc.at[t]))
            for k, to in enumerate(targets):
                i = 4 * t + k
                sends.append(_remote(ins[t], outs[t].at[me], send.at[i], recv.at[i], to))
                recvs.append(_remote(ins[t], outs[t].at[_dev(*to)], send.at[i], recv.at[i], to))
        return local, sends, recvs

    return _Comm(list(shards), [jax.ShapeDtypeStruct((N_DEV,) + s.shape, s.dtype) for s in shards],
                 [pltpu.SemaphoreType.DMA((4 * n,)), pltpu.SemaphoreType.DMA((4 * n,)), pltpu.SemaphoreType.DMA((n,))],
                 descs)


def _gather_second(gathered):
    n = len(gathered)

    def descs(ins, outs, sems):
        send, recv = sems
        x, y, c, chips = _place()
        sends, recvs = [], []
        for t in range(n):
            for j, chip in enumerate(chips):
                i = 3 * t + j
                have, get = _dev(*chip, c), _dev(*chip, 1 - c)
                sends.append(_remote(outs[t].at[have], outs[t].at[have], send.at[i], recv.at[i], (x, y, 1 - c)))
                recvs.append(_remote(outs[t].at[have], outs[t].at[get], send.at[i], recv.at[i], (x, y, 1 - c)))
        return [], sends, recvs

    return _Comm(list(gathered), [jax.ShapeDtypeStruct(g.shape, g.dtype) for g in gathered],
                 [pltpu.SemaphoreType.DMA((3 * n,)), pltpu.SemaphoreType.DMA((3 * n,))], descs,
                 aliases=[(t, t) for t in range(n)])


def _pair_exchange(blocks):
    n = len(blocks)

    def descs(ins, outs, sems):
        send, recv = sems
        x, y, c, _ = _place()
        sends, recvs = [], []
        for t in range(n):
            for q in range(N_CHIPS):
                i = N_CHIPS * t + q
                cp = _remote(ins[t].at[2 * q + 1 - c], outs[t].at[q], send.at[i], recv.at[i], (x, y, 1 - c))
                sends.append(cp)
                recvs.append(cp)
        return [], sends, recvs

    return _Comm(list(blocks), [jax.ShapeDtypeStruct((N_CHIPS,) + b.shape[1:], b.dtype) for b in blocks],
                 [pltpu.SemaphoreType.DMA((N_CHIPS * n,)), pltpu.SemaphoreType.DMA((N_CHIPS * n,))], descs)


def _chip_exchange(blocks):
    n = len(blocks)

    def descs(ins, outs, sems):
        send, recv, loc = sems
        x, y, c, chips = _place()
        me = 2 * x + y
        local, sends, recvs = [], [], []
        for t in range(n):
            local.append(_local(ins[t].at[me], outs[t].at[me], loc.at[t]))
            for j, (px, py) in enumerate(chips):
                i = 3 * t + j
                q = 2 * px + py
                sends.append(_remote(ins[t].at[q], outs[t].at[me], send.at[i], recv.at[i], (px, py, c)))
                recvs.append(_remote(ins[t].at[q], outs[t].at[q], send.at[i], recv.at[i], (px, py, c)))
        return local, sends, recvs

    return _Comm(list(blocks), [jax.ShapeDtypeStruct(b.shape, b.dtype) for b in blocks],
                 [pltpu.SemaphoreType.DMA((3 * n,)), pltpu.SemaphoreType.DMA((3 * n,)), pltpu.SemaphoreType.DMA((n,))],
                 descs)


def _both(a, b):
    na, oa, sa = len(a.operands), len(a.out_shape), len(a.sems)

    def descs(ins, outs, sems):
        local_a, sends_a, recvs_a = a.descs(ins[:na], outs[:oa], sems[:sa])
        local_b, sends_b, recvs_b = b.descs(ins[na:], outs[oa:], sems[sa:])
        return local_a + local_b, sends_a + sends_b, recvs_a + recvs_b

    return _Comm(a.operands + b.operands, a.out_shape + b.out_shape, a.sems + b.sems, descs,
                 aliases=list(a.aliases) + [(na + i, oa + o) for i, o in b.aliases])


def _run_comms(comms, name):
    first = comms[0]
    n_in, n_out = len(first.operands), len(first.out_shape)

    def body(*refs):
        ins, outs, sems = refs[:n_in], refs[n_in:n_in + n_out], list(refs[n_in + n_out:])
        for k, comm in enumerate(comms):
            mine = [sems.pop(0) for _ in comm.sems]
            comm.start(ins if k == 0 else outs, outs, mine)
            comm.wait(ins if k == 0 else outs, outs, mine)

    outs = pl.pallas_call(
        body, name=name, out_shape=first.out_shape, in_specs=[ANY] * n_in, out_specs=[ANY] * n_out,
        scratch_shapes=[s for comm in comms for s in comm.sems], input_output_aliases=dict(first.aliases),
    )(*first.operands)
    return list(outs)


def _call(body, *, name, grid, in_specs, out_specs, out_shape, operands, scratch_shapes=(), comm=None, aliases=None,
          vmem_limit=VMEM_LIMIT):
    sem = ("arbitrary",) * len(grid)
    params = pltpu.CompilerParams(dimension_semantics=sem, vmem_limit_bytes=vmem_limit)
    aliases = dict(aliases or {})
    if comm is None:
        return pl.pallas_call(body, name=name, grid=grid, in_specs=in_specs, out_specs=out_specs, out_shape=out_shape,
                              scratch_shapes=list(scratch_shapes), input_output_aliases=aliases,
                              compiler_params=params)(*operands)
    n_in, n_out, n_scr = len(in_specs), len(out_specs), len(scratch_shapes)
    c_in, c_out = len(comm.operands), len(comm.out_shape)

    def wrapped(*refs):
        refs = list(refs)
        ins, refs = refs[:n_in], refs[n_in:]
        cins, refs = refs[:c_in], refs[c_in:]
        outs, refs = refs[:n_out], refs[n_out:]
        couts, refs = refs[:c_out], refs[c_out:]
        scr, csems = refs[:n_scr], refs[n_scr:]
        first = last = None
        for axis, size in enumerate(grid):
            at_first, at_last = pl.program_id(axis) == 0, pl.program_id(axis) == size - 1
            first = at_first if first is None else first & at_first
            last = at_last if last is None else last & at_last

        @pl.when(first)
        def _():
            comm.start(cins, couts, csems)

        body(*ins, *outs, *scr)

        @pl.when(last)
        def _():
            comm.wait(cins, couts, csems)

    res = pl.pallas_call(
        wrapped, name=name, grid=grid, in_specs=list(in_specs) + [ANY] * c_in, out_specs=list(out_specs) + [ANY] * c_out,
        out_shape=list(out_shape) + list(comm.out_shape), scratch_shapes=list(scratch_shapes) + list(comm.sems),
        input_output_aliases={**aliases, **{n_in + i: n_out + o for i, o in comm.aliases}}, compiler_params=params,
    )(*operands, *comm.operands)
    return list(res[:n_out]), list(res[n_out:])


def _mm_tn(a, b, name, blocks=1, comm=None, room=None, before=None):
    t, m = a.shape
    n = b.shape[1]
    tk = min(WGRAD_ROWS, t)
    nk = t // tk
    cb = n // blocks
    per = max(1, WGRAD_BLOCK_COLUMNS // cb) if blocks > 1 else 1
    tn = per * cb if blocks > 1 else min(WGRAD_TILE, n)
    tm = min(WGRAD_TILE, m)
    assert blocks == 1 or tm == m
    total = blocks if before is None and room is None else (room if before is None else before.shape[0])
    first = (total - blocks) // per if before is not None else 0
    assert blocks > 1 or total == 1

    def body(*refs):
        (a_ref, b_ref), (o_ref, acc) = refs[:2], refs[-2:]
        k = pl.program_id(2)

        @pl.when(k == 0)
        def _():
            acc[...] = jnp.zeros_like(acc)
        acc[...] += _dot_tn(a_ref[...], b_ref[...])

        @pl.when(k == nk - 1)
        def _():
            if blocks == 1:
                o_ref[...] = acc[...].astype(o_ref.dtype)
            else:
                for s in range(per):
                    o_ref[s] = acc[:, s * cb:(s + 1) * cb].astype(o_ref.dtype)

    if blocks == 1:
        out_spec = pl.BlockSpec((tm, tn), lambda i, j, k: (i, j))
        out_shape = jax.ShapeDtypeStruct((m, n), GRAD_DTYPE)
    else:
        out_spec = pl.BlockSpec((per, m, cb), lambda i, j, k: (first + j, 0, 0))
        out_shape = jax.ShapeDtypeStruct((total, m, cb), GRAD_DTYPE)
    in_specs = [pl.BlockSpec((tk, tm), lambda i, j, k: (k, i)), pl.BlockSpec((tk, tn), lambda i, j, k: (k, j))]
    given = [] if before is None else [before]
    res = _call(body, name=name, grid=(m // tm, n // tn, nk), comm=comm, in_specs=in_specs + [ANY] * len(given),
                out_specs=[out_spec], out_shape=[out_shape], operands=[a, b] + given,
                aliases={len(in_specs): 0} if given else None, scratch_shapes=[pltpu.VMEM((tm, tn), F32)])
    return res[0] if comm is None else (res[0][0], res[1])


INPROJ_TN = 1024


def _inproj_fwd(x, g1, w_blocks, comm):
    t = x.shape[0]
    tm = min(PROJ_ROW_TILE, t)
    nb, _, cb = w_blocks.shape

    def body(x_ref, g_ref, w_hbm, u_ref, p_ref, w_all, sems):
        @pl.when(pl.program_id(0) == 0)
        def _():
            copies = [pltpu.make_async_copy(w_hbm.at[d], w_all.at[:, pl.ds(d * cb, cb)], sems.at[d]) for d in range(nb)]
            for cp in copies:
                cp.start()
            for cp in copies:
                cp.wait()

        _, _, u = _rms_fwd(x_ref[...], g_ref[...])
        u = u.astype(MXU_DTYPE)
        u_ref[...] = u
        for lo in range(0, D_IN, INPROJ_TN):
            p_ref[:, lo:lo + INPROJ_TN] = _dot(u, w_all[:, lo:lo + INPROJ_TN]).astype(p_ref.dtype)

    return _call(body, name="inproj_fwd", grid=(t // tm,), comm=comm,
                 in_specs=[pl.BlockSpec((tm, D_MODEL), lambda i: (i, 0)), pl.BlockSpec((1, D_MODEL), lambda i: (0, 0)), ANY],
                 out_specs=[pl.BlockSpec((tm, D_MODEL), lambda i: (i, 0)), pl.BlockSpec((tm, D_IN), lambda i: (i, 0))],
                 out_shape=[jax.ShapeDtypeStruct((t, D_MODEL), MXU_DTYPE), jax.ShapeDtypeStruct((t, D_IN), MXU_DTYPE)],
                 scratch_shapes=[pltpu.VMEM((D_MODEL, nb * cb), w_blocks.dtype), pltpu.SemaphoreType.DMA((nb,))],
                 operands=[x, g1, w_blocks])


def _lru_gates(xc, wa, ba, wx, bx, sp, fill=lambda: None):
    r = _sigmoid(_dot(xc, wa) + ba)
    fill()
    ig = _sigmoid(_dot(xc, wx) + bx)
    fill()
    log_a = (-LRU_C) * r * sp
    a = jnp.exp(log_a)
    m = jnp.sqrt(-jnp.tanh(log_a) * (a * a + 1.0))
    return r, ig, a, m


def _fused(parts, name, comm=None):
    grid = parts[0]["grid"]
    assert all(p["grid"] == grid for p in parts)
    counts = [(len(p["in_specs"]), len(p["out_specs"]), len(p.get("scratch_shapes", ()))) for p in parts]

    def body(*refs):
        refs = list(refs)
        groups = []
        for kind in range(3):
            taken = []
            for c in counts:
                taken.append(refs[:c[kind]])
                refs = refs[c[kind]:]
            groups.append(taken)
        ins, outs, scr = groups
        pending = []

        def fill(n=None):
            for _ in range(share if n is None else n):
                if pending:
                    pending.pop(0)()

        ctx = dict(outs=outs, scratch=scr, fill=fill)
        run = lambda key: [p[key](*ins[k], *outs[k], *scr[k], ctx) for k, p in enumerate(parts) if key in p]
        run("head")
        for pieces in run("units"):
            pending.extend(pieces)
        points = sum(p.get("fill_points", 0) for p in parts)
        share = -(-len(pending) // max(points, 1))
        run("body")
        fill(len(pending))
        run("tail")

    cat = lambda key: [x for p in parts for x in p.get(key, ())]
    res = _call(body, name=name, grid=grid, comm=comm, vmem_limit=FUSED_VMEM_LIMIT,
                in_specs=cat("in_specs"), out_specs=cat("out_specs"),
                out_shape=cat("out_shape"), scratch_shapes=cat("scratch_shapes"), operands=cat("operands"))
    outs, side = (res if comm is not None else (res, None))
    split, at = [], 0
    for _, n_out, _ in counts:
        split.append(list(outs[at:at + n_out]))
        at += n_out
    return split if comm is None else (split, side)


def _lru_fwd(proj, conv_w, conv_b, wa, ba, wx, bx, lam, gain):
    t = proj.shape[0]
    tm = min(MIX_ROW_TILE, t)
    c = D_LRU

    def body(x_ref, xh_ref, g_ref, cw_ref, cb_ref, wa_ref, ba_ref, wx_ref, bx_ref, lam_ref, gain_ref,
             xc_ref, h_ref, y_ref, a_scr, b_scr, carry, ctx):
        fill = ctx["fill"]
        i = pl.program_id(0)

        @pl.when(i == 0)
        def _():
            carry[...] = jnp.zeros_like(carry)

        fill()
        x = x_ref[...].astype(F32)
        prev = jnp.where(i == 0, 0.0, xh_ref[...].astype(F32)[-SUBLANES:, :])
        cw = cw_ref[...]
        xc = cb_ref[...] + cw[LRU_CONV - 1:LRU_CONV, :] * x
        for k in range(LRU_CONV - 1):
            xc = xc + cw[k:k + 1, :] * _shift_down(prev, x, LRU_CONV - 1 - k)
        xc_ref[...] = xc
        fill()
        sp = _softplus(-lam_ref[...])
        _, ig, a, m = _lru_gates(xc, wa_ref[...], ba_ref[...], wx_ref[...], bx_ref[...], sp, fill)
        fill()
        ga, gb = _group_scan(a, m * (ig * xc), reverse=False, fill=fill)
        a_scr[...] = ga
        b_scr[...] = gb
        fill()
        carry[...] = _carry_scan(a_scr, b_scr, h_ref, carry[...], reverse=False)
        fill()
        z = h_ref[...] * _gelu(g_ref[...].astype(F32))
        fill()
        _, _, y = _rms_fwd(z, gain_ref[...])
        y_ref[...] = y.astype(y_ref.dtype)

    row = lambda i: (i, 0)
    full = lambda i: (0, 0)
    vec = pl.BlockSpec((1, c), full)
    hb = _halo_rows(proj.dtype)
    return dict(body=body, grid=(t // tm,), fill_points=6,
                in_specs=[pl.BlockSpec((tm, c), row), pl.BlockSpec((hb, c), _halo_map(tm, 0, hb)),
                          pl.BlockSpec((tm, c), lambda i: (i, 1)),
                          pl.BlockSpec((LRU_CONV, c), full), vec, pl.BlockSpec((c, c), full), vec,
                          pl.BlockSpec((c, c), full), vec, vec, vec],
                out_specs=[pl.BlockSpec((tm, c), row), pl.BlockSpec((tm, c), row), pl.BlockSpec((tm, c), row)],
                out_shape=[jax.ShapeDtypeStruct((t, c), F32), jax.ShapeDtypeStruct((t, c), F32),
                           jax.ShapeDtypeStruct((t, c), MXU_DTYPE)],
                scratch_shapes=[pltpu.VMEM((tm, c), F32), pltpu.VMEM((tm, c), F32), pltpu.VMEM((SUBLANES, c), F32)],
                operands=[proj, proj, proj, conv_w, conv_b, wa, ba, wx, bx, lam, gain])


def _ret_consts():
    c = RET_CHUNK
    log_g = jnp.log1p(-jnp.exp2(-5.0 - jnp.arange(RET_HEADS, dtype=F32)))
    idx = jnp.arange(c, dtype=F32)
    diff = idx[:, None] - idx[None, :]
    decay = jnp.where(diff[None] >= 0, jnp.exp(jnp.maximum(diff, 0.0)[None] * log_g[:, None, None]), 0.0)
    zeta = jnp.exp((c - 1 - idx)[None, :] * log_g[:, None])
    xi = jnp.exp((idx + 1.0)[None, :] * log_g[:, None])
    spread = lambda v: jnp.repeat(v.T, RET_HEAD_DIM, axis=1)
    log_g_np = np.log1p(-np.exp2(-5.0 - np.arange(RET_HEADS, dtype=np.float32))).astype(np.float32)
    g_chunk = [float(np.exp(np.float32(c) * lg)) for lg in log_g_np]
    return decay, spread(xi), spread(zeta), g_chunk


def _rope_tables(t):
    pos = np.arange(t, dtype=np.float32)
    inv_freq = np.float32(ROPE_BASE) ** (-np.arange(0, RET_HEAD_DIM, 2, dtype=np.float32) / np.float32(RET_HEAD_DIM))
    ang = (pos[:, None] * inv_freq.astype(np.float32)[None, :]).astype(np.float32).astype(np.float64)
    cos, sin = np.cos(ang).astype(np.float32), np.sin(ang).astype(np.float32)
    return jnp.asarray(np.concatenate([cos, cos], axis=-1)), jnp.asarray(np.concatenate([-sin, sin], axis=-1))


def _rope(x, cos2, sin_signed):
    return x * cos2 + pltpu.roll(x, RET_HEAD_DIM // 2, 1) * sin_signed


def _rope_bwd(d, cos2, sin_signed):
    return d * cos2 + pltpu.roll(d * sin_signed, RET_HEAD_DIM // 2, 1)


RET_SCALE = RET_HEAD_DIM ** -0.5


RET_CHUNKS_PER_STEP = MIX_ROW_TILE // RET_CHUNK


def _ret_fwd(proj, cos2, sin_signed, gain):
    t = proj.shape[0]
    c, d, nh = RET_CHUNK, RET_HEAD_DIM, RET_HEADS
    n_chunks = t // c
    per = RET_CHUNKS_PER_STEP if n_chunks % RET_CHUNKS_PER_STEP == 0 else 1
    rows = per * c
    decay, xi, zeta, g_chunk = _ret_consts()

    def units(qk_ref, vg_ref, cos_ref, sin_ref, dec_ref, xi_ref, zeta_ref, gain_ref, o_ref, y_ref, st_ref, state, ctx):
        cur = [None] * nh

        def start():
            @pl.when(pl.program_id(0) == 0)
            def _():
                state[...] = jnp.zeros_like(state)
            for h in range(nh):
                cur[h] = state[h]

        def retain(s, h, keep):
            rs = slice(s * c, (s + 1) * c)
            cos2, sin_s = cos_ref[rs, :], sin_ref[rs, :]
            lo = h * d
            q = _rope(qk_ref[rs, lo:lo + d].astype(F32), cos2, sin_s)
            k = _rope(qk_ref[rs, D_RET + lo:D_RET + lo + d].astype(F32), cos2, sin_s) * RET_SCALE
            v = vg_ref[rs, lo:lo + d]
            s_prev = cur[h]
            st_ref[s, h] = s_prev
            scores = _dot_nt(q, k) * dec_ref[h]
            o = _dot(scores, v) + _dot(q * xi_ref[:, lo:lo + d], s_prev)
            cur[h] = s_prev * g_chunk[h] + _dot_tn(k * zeta_ref[:, lo:lo + d], v)
            o_ref[rs, lo:lo + d] = o
            keep["o"] = o

        def normalise(s, h, keep):
            rs = slice(s * c, (s + 1) * c)
            lo = h * d
            o = keep["o"]
            g = vg_ref[rs, D_RET + lo:D_RET + lo + d].astype(F32)
            mu = jnp.mean(o, axis=-1, keepdims=True)
            oc = o - mu
            on = oc * lax.rsqrt(jnp.mean(oc * oc, axis=-1, keepdims=True) + NORM_EPS)
            y_ref[rs, lo:lo + d] = (on * gain_ref[:, lo:lo + d] * (g * _sigmoid(g))).astype(y_ref.dtype)

        def end():
            for h in range(nh):
                state[h] = cur[h]

        pieces = [start]
        for s in range(per):
            for h in range(nh):
                keep = {}
                pieces += [lambda s=s, h=h, keep=keep: retain(s, h, keep),
                           lambda s=s, h=h, keep=keep: normalise(s, h, keep)]
        return pieces + [end]

    full2 = lambda i: (0, 0)
    return dict(units=units, grid=(n_chunks // per,),
                in_specs=[pl.BlockSpec((rows, 2 * D_RET), lambda i: (i, 1)),
                          pl.BlockSpec((rows, 2 * D_RET), lambda i: (i, 2)),
                          pl.BlockSpec((rows, d), lambda i: (i, 0)), pl.BlockSpec((rows, d), lambda i: (i, 0)),
                          pl.BlockSpec((nh, c, c), lambda i: (0, 0, 0)), pl.BlockSpec((c, D_RET), full2),
                          pl.BlockSpec((c, D_RET), full2), pl.BlockSpec((1, D_RET), full2)],
                out_specs=[pl.BlockSpec((rows, D_RET), lambda i: (i, 0)), pl.BlockSpec((rows, D_RET), lambda i: (i, 0)),
                           pl.BlockSpec((per, nh, d, d), lambda i: (i, 0, 0, 0))],
                out_shape=[jax.ShapeDtypeStruct((t, D_RET), F32), jax.ShapeDtypeStruct((t, D_RET), MXU_DTYPE),
                           jax.ShapeDtypeStruct((n_chunks, nh, d, d), F32)],
                scratch_shapes=[pltpu.VMEM((nh, d, d), F32)],
                operands=[proj, proj, cos2, sin_signed, decay, xi, zeta, gain])


def _outproj_fwd(x, y_lru, y_ret, w_out, g2, comm):
    t = x.shape[0]
    tm = min(PROJ_ROW_TILE, t)

    def body(x_ref, yl_ref, yr_ref, w_ref, g_ref, h1_ref, u2_ref):
        h1 = x_ref[...] + _dot(yl_ref[...], w_ref[:D_LRU, :]) + _dot(yr_ref[...], w_ref[D_LRU:, :])
        h1_ref[...] = h1
        _, _, u = _rms_fwd(h1, g_ref[...])
        u2_ref[...] = u.astype(u2_ref.dtype)

    row = lambda i: (i, 0)
    return _call(body, name="outproj_fwd", grid=(t // tm,), comm=comm,
                 in_specs=[pl.BlockSpec((tm, D_MODEL), row), pl.BlockSpec((tm, D_LRU), row), pl.BlockSpec((tm, D_RET), row),
                           _resident((D_MODEL, D_MODEL)), pl.BlockSpec((1, D_MODEL), lambda i: (0, 0))],
                 out_specs=[pl.BlockSpec((tm, D_MODEL), row), pl.BlockSpec((tm, D_MODEL), row)],
                 out_shape=[jax.ShapeDtypeStruct((t, D_MODEL), F32), jax.ShapeDtypeStruct((t, D_MODEL), MXU_DTYPE)],
                 operands=[x, y_lru, y_ret, w_out, g2])


FFN_TN = 768
FFN_NJ = D_FF // FFN_TN
FFN_GROUP = 4


def _ffn_fwd(u2, w_blocks, conv_w, conv_b, w_down, h1, gf, target):
    t = u2.shape[0]
    tm = min(ROW_TILE, t)
    tn, nj, group = FFN_TN, FFN_NJ, FFN_GROUP
    ng, tw = nj // group, group * tn
    hb = _halo_rows(u2.dtype)
    assert w_blocks.shape == (2 * nj, D_MODEL, tn)

    def conv(ext, col, up_ref, conv_ref, cw_ref, cb_ref, first):
        x = ext[hb:, :]
        up_ref[:, col] = x.astype(up_ref.dtype)
        prev = jnp.where(first, 0.0, ext[hb - SUBLANES:hb, :])
        cw = cw_ref[:, col]
        y = cb_ref[:, col] + cw[FFN_CONV - 1:FFN_CONV, :] * x
        for k in range(FFN_CONV - 1):
            y = y + cw[k:k + 1, :] * _shift_down(prev, x, FFN_CONV - 1 - k)
        conv_ref[:, col] = y.astype(conv_ref.dtype)
        return y

    def body(u_ref, uh_ref, w_ref, cwa_ref, cwv_ref, cba_ref, cbv_ref, wd_ref, h1_ref, gf_ref, tg_ref,
             upa_ref, upv_ref, ca_ref, cv_ref, act_ref, dh_ref, dhb_ref, dgf_ref, loss_ref, acc):
        i, jg = pl.program_id(0), pl.program_id(1)

        @pl.when((i == 0) & (jg == 0))
        def _():
            dgf_ref[...] = jnp.zeros_like(dgf_ref)
            loss_ref[...] = jnp.zeros_like(loss_ref)

        @pl.when(jg == 0)
        def _():
            acc[...] = jnp.zeros_like(acc)

        u_ext = jnp.concatenate([uh_ref[...], u_ref[...]], axis=0)

        def project(jj):
            j = jg * group + jj
            return _dot(u_ext, w_ref[j]), _dot(u_ext, w_ref[nj + j])

        down, ahead = None, project(0)
        for jj in range(group):
            col = slice(jj * tn, (jj + 1) * tn)
            j = jg * group + jj
            ext_a, ext_v = ahead
            if jj + 1 < group:
                ahead = project(jj + 1)
            a = conv(ext_a, col, upa_ref, ca_ref, cwa_ref, cba_ref, i == 0)
            v = conv(ext_v, col, upv_ref, cv_ref, cwv_ref, cbv_ref, i == 0)
            act = (_gelu(a) * v).astype(act_ref.dtype)
            act_ref[:, col] = act
            part = _dot(act, wd_ref[pl.ds(pl.multiple_of(j * tn, tn), tn), :])
            down = part if down is None else down + part
        acc[...] += down

        @pl.when(jg == ng - 1)
        def _():
            n, rstd, y = _rms_fwd(h1_ref[...] + acc[...], gf_ref[...])
            err = y - tg_ref[...]
            loss_ref[...] += (0.5 / D_MODEL) * jnp.sum(err * err)
            dh, dgf = _rms_bwd(err * (1.0 / D_MODEL), n, rstd, gf_ref[...])
            dgf_ref[...] += dgf
            dh_ref[...] = dh
            dhb_ref[...] = dh.astype(dhb_ref.dtype)

    per = tm // hb
    row = lambda i, j: (i, 0)
    const = lambda i, j: (0, 0)
    tile = pl.BlockSpec((tm, tw), lambda i, j: (i, j))
    return _call(body, name="ffn_fwd", grid=(t // tm, ng), vmem_limit=FUSED_VMEM_LIMIT,
                 in_specs=[pl.BlockSpec((tm, D_MODEL), row),
                           pl.BlockSpec((hb, D_MODEL), lambda i, j: (jnp.maximum(i * per - 1, 0), 0)),
                           _resident(w_blocks.shape),
                           pl.BlockSpec((FFN_CONV, tw), lambda i, j: (0, j)),
                           pl.BlockSpec((FFN_CONV, tw), lambda i, j: (0, j + ng)),
                           pl.BlockSpec((1, tw), lambda i, j: (0, j)), pl.BlockSpec((1, tw), lambda i, j: (0, j + ng)),
                           _resident((D_FF, D_MODEL)),
                           pl.BlockSpec((tm, D_MODEL), row), pl.BlockSpec((1, D_MODEL), const),
                           pl.BlockSpec((tm, D_MODEL), row)],
                 out_specs=[tile] * 5 + [pl.BlockSpec((tm, D_MODEL), row),
                            pl.BlockSpec((tm, D_MODEL), row), pl.BlockSpec((SUBLANES, D_MODEL), const),
                            pl.BlockSpec((SUBLANES, LANES), const)],
                 out_shape=[jax.ShapeDtypeStruct((t, D_FF), MXU_DTYPE)] * 5 + [
                            jax.ShapeDtypeStruct((t, D_MODEL), F32),
                            jax.ShapeDtypeStruct((t, D_MODEL), MXU_DTYPE), jax.ShapeDtypeStruct((SUBLANES, D_MODEL), F32),
                            jax.ShapeDtypeStruct((SUBLANES, LANES), F32)],
                 scratch_shapes=[pltpu.VMEM((tm, D_MODEL), F32)],
                 operands=[u2, u2, w_blocks, conv_w, conv_w, conv_b, conv_b, w_down, h1, gf, target])


FFN_ACC_ROWS = SUBLANES * (FFN_CONV + 1)


def _ffn_bwd(dh2, dh2_b, w_down, up_a, up_v, conv_a, conv_v, conv_w, w_up_blocks, h1, g2, comm):
    t = up_a.shape[0]
    tm = min(ROW_TILE, t)
    tn, nj, group = FFN_TN, FFN_NJ, FFN_GROUP
    ng, tw = nj // group, group * tn
    ni = t // tm
    assert w_up_blocks.shape == (2 * nj, D_MODEL, tn)

    def conv_bwd(dy, x, cw, acc_ref, carry_ref, dup_ref, col):
        nxt = carry_ref[...]
        carry_ref[...] = dy[:SUBLANES, :]
        ahead = [_shift_up(dy, nxt, FFN_CONV - 1 - k) for k in range(FFN_CONV)]
        dx = cw[FFN_CONV - 1:FFN_CONV, :] * dy
        for k in range(FFN_CONV - 1):
            dx = dx + cw[k:k + 1, :] * ahead[k]
        dx = dx.astype(dup_ref.dtype)
        dup_ref[:, col] = dx
        for k in range(FFN_CONV):
            acc_ref[k * SUBLANES:(k + 1) * SUBLANES, :] += _colsum8(ahead[k] * x)
        acc_ref[FFN_CONV * SUBLANES:, :] += _colsum8(dy)
        return dx

    def body(dh_ref, dhb_ref, wd_ref, ua_ref, uv_ref, ca_ref, cv_ref, cwa_ref, cwv_ref, wu_ref, h1_ref, g2_ref,
             dua_ref, duv_ref, acca_ref, accv_ref, dh1_ref, dh1b_ref, dg2_ref, carry_a, carry_v, du):
        i, jg = pl.program_id(0), pl.program_id(1)

        @pl.when((i == 0) & (jg == 0))
        def _():
            for ref in (acca_ref, accv_ref, carry_a, carry_v, dg2_ref):
                ref[...] = jnp.zeros_like(ref)

        dhb = dhb_ref[...]

        def through_down(jj):
            j = jg * group + jj
            return _dot_nt(dhb, wd_ref[pl.ds(pl.multiple_of(j * tn, tn), tn), :])

        part, ahead = None, through_down(0)
        for jj in range(group):
            col = slice(jj * tn, (jj + 1) * tn)
            j = jg * group + jj
            dact = ahead
            if jj + 1 < group:
                ahead = through_down(jj + 1)
            v = cv_ref[:, col].astype(F32)
            g, dg = _gelu_parts(ca_ref[:, col].astype(F32))
            da = conv_bwd(dact * v * dg, ua_ref[:, col].astype(F32), cwa_ref[:, col], acca_ref.at[j], carry_a.at[j],
                          dua_ref, col)
            dv = conv_bwd(dact * g, uv_ref[:, col].astype(F32), cwv_ref[:, col], accv_ref.at[j], carry_v.at[j],
                          duv_ref, col)
            term = _dot_nt(da, wu_ref[j]) + _dot_nt(dv, wu_ref[nj + j])
            part = term if part is None else part + term

        @pl.when(jg == 0)
        def _():
            du[...] = part

        @pl.when(jg > 0)
        def _():
            du[...] += part

        @pl.when(jg == ng - 1)
        def _():
            n, rstd, _ = _rms_fwd(h1_ref[...], g2_ref[...])
            dh1, dg2 = _rms_bwd(du[...], n, rstd, g2_ref[...])
            dh1 = dh1 + dh_ref[...]
            dg2_ref[...] += dg2
            dh1_ref[...] = dh1
            dh1b_ref[...] = dh1.astype(dh1b_ref.dtype)

    row = lambda i, j: (ni - 1 - i, 0)
    const = lambda i, j: (0, 0)
    tile = pl.BlockSpec((tm, tw), lambda i, j: (ni - 1 - i, j))
    acc = pl.BlockSpec((nj, FFN_ACC_ROWS, tn), lambda i, j: (0, 0, 0))
    return _call(body, name="ffn_bwd", grid=(ni, ng), comm=comm, vmem_limit=FFN_BWD_VMEM_LIMIT,
                 in_specs=[pl.BlockSpec((tm, D_MODEL), row), pl.BlockSpec((tm, D_MODEL), row),
                           _resident((D_FF, D_MODEL)), tile, tile, tile, tile,
                           pl.BlockSpec((FFN_CONV, tw), lambda i, j: (0, j)),
                           pl.BlockSpec((FFN_CONV, tw), lambda i, j: (0, j + ng)),
                           _resident(w_up_blocks.shape), pl.BlockSpec((tm, D_MODEL), row),
                           pl.BlockSpec((1, D_MODEL), const)],
                 out_specs=[tile, tile, acc, acc, pl.BlockSpec((tm, D_MODEL), row), pl.BlockSpec((tm, D_MODEL), row),
                            pl.BlockSpec((SUBLANES, D_MODEL), const)],
                 out_shape=[jax.ShapeDtypeStruct((t, D_FF), MXU_DTYPE), jax.ShapeDtypeStruct((t, D_FF), MXU_DTYPE),
                            jax.ShapeDtypeStruct((nj, FFN_ACC_ROWS, tn), F32),
                            jax.ShapeDtypeStruct((nj, FFN_ACC_ROWS, tn), F32),
                            jax.ShapeDtypeStruct((t, D_MODEL), F32), jax.ShapeDtypeStruct((t, D_MODEL), MXU_DTYPE),
                            jax.ShapeDtypeStruct((SUBLANES, D_MODEL), F32)],
                 scratch_shapes=[pltpu.VMEM((nj, SUBLANES, tn), F32), pltpu.VMEM((nj, SUBLANES, tn), F32),
                                 pltpu.VMEM((tm, D_MODEL), F32)],
                 operands=[dh2, dh2_b, w_down, up_a, up_v, conv_a, conv_v, conv_w, conv_w, w_up_blocks, h1, g2])


def _ret_bwd(proj, cos2, sin_signed, gain, o, states, dmix_at):
    t = proj.shape[0]
    c, d, nh = RET_CHUNK, RET_HEAD_DIM, RET_HEADS
    n_chunks = t // c
    per = RET_CHUNKS_PER_STEP if n_chunks % RET_CHUNKS_PER_STEP == 0 else 1
    rows = per * c
    n_steps = n_chunks // per
    decay, xi, zeta, g_chunk = _ret_consts()
    base = 2 * D_LRU

    def units(qk_ref, vg_ref, cos_ref, sin_ref, dec_ref, xi_ref, zeta_ref, gain_ref, o_ref, st_ref,
              dp_ref, dgain_ref, gstate, ctx):
        cur = [None] * nh
        dmix = ctx["scratch"][dmix_at[0]][dmix_at[1]]

        def start():
            @pl.when(pl.program_id(0) == 0)
            def _():
                gstate[...] = jnp.zeros_like(gstate)
                dgain_ref[...] = jnp.zeros_like(dgain_ref)
            for h in range(nh):
                cur[h] = gstate[h]

        def gate_and_norm(s, h, keep):
            rs = slice(s * c, (s + 1) * c)
            lo = h * d
            g = vg_ref[rs, D_RET + lo:D_RET + lo + d].astype(F32)
            gain_h = gain_ref[:, lo:lo + d]
            dy = dmix[rs, D_LRU + lo:D_LRU + lo + d]
            sg = _sigmoid(g)
            o_h = o_ref[rs, lo:lo + d]
            oc = o_h - jnp.mean(o_h, axis=-1, keepdims=True)
            rstd = lax.rsqrt(jnp.mean(oc * oc, axis=-1, keepdims=True) + NORM_EPS)
            on = oc * rstd
            at = base + 3 * D_RET + lo
            dp_ref[rs, at:at + d] = (dy * on * gain_h * (sg * (1.0 + g * (1.0 - sg)))).astype(dp_ref.dtype)
            don_g = dy * (g * sg)
            dgain_ref[:, lo:lo + d] += _colsum8(don_g * on)
            don = don_g * gain_h
            keep["do"] = rstd * (don - jnp.mean(don, axis=-1, keepdims=True)
                                 - on * jnp.mean(don * on, axis=-1, keepdims=True))

        def retain(s, h, keep):
            rs = slice(s * c, (s + 1) * c)
            cos2, sin_s = cos_ref[rs, :], sin_ref[rs, :]
            lo = h * d
            q = _rope(qk_ref[rs, lo:lo + d].astype(F32), cos2, sin_s)
            k = _rope(qk_ref[rs, D_RET + lo:D_RET + lo + d].astype(F32), cos2, sin_s) * RET_SCALE
            v = vg_ref[rs, lo:lo + d]
            xi_h, zeta_h, dec = xi_ref[:, lo:lo + d], zeta_ref[:, lo:lo + d], dec_ref[h]
            do = keep["do"]
            s_prev = st_ref[s, h]
            g_next = cur[h]
            p = _dot_nt(q, k) * dec
            dpm = _dot_nt(do, v) * dec
            keep["dq"] = _dot(dpm, k) + _dot_nt(do, s_prev) * xi_h
            keep["dk"] = _dot_tn(dpm, q) + _dot_nt(v, g_next) * zeta_h
            dv = _dot_tn(p, do) + _dot(k * zeta_h, g_next)
            cur[h] = g_next * g_chunk[h] + _dot_tn(q * xi_h, do)
            at = base + 2 * D_RET + lo
            dp_ref[rs, at:at + d] = dv.astype(dp_ref.dtype)

        def unrope(s, h, keep):
            rs = slice(s * c, (s + 1) * c)
            cos2, sin_s = cos_ref[rs, :], sin_ref[rs, :]
            lo = h * d
            dp_ref[rs, base + lo:base + lo + d] = _rope_bwd(keep["dq"], cos2, sin_s).astype(dp_ref.dtype)
            at = base + D_RET + lo
            dp_ref[rs, at:at + d] = _rope_bwd(keep["dk"] * RET_SCALE, cos2, sin_s).astype(dp_ref.dtype)

        def end():
            for h in range(nh):
                gstate[h] = cur[h]

        pieces = [start]
        for s in reversed(range(per)):
            for h in range(nh):
                keep = {}
                pieces += [lambda s=s, h=h, keep=keep, f=f: f(s, h, keep) for f in (gate_and_norm, retain, unrope)]
        return pieces + [end]

    rev = lambda col: (lambda i: (n_steps - 1 - i, col))
    full2 = lambda i: (0, 0)
    return dict(units=units, grid=(n_steps,),
                in_specs=[pl.BlockSpec((rows, 2 * D_RET), rev(1)), pl.BlockSpec((rows, 2 * D_RET), rev(2)),
                          pl.BlockSpec((rows, d), rev(0)), pl.BlockSpec((rows, d), rev(0)),
                          pl.BlockSpec((nh, c, c), lambda i: (0, 0, 0)), pl.BlockSpec((c, D_RET), full2),
                          pl.BlockSpec((c, D_RET), full2), pl.BlockSpec((1, D_RET), full2),
                          pl.BlockSpec((rows, D_RET), rev(0)),
                          pl.BlockSpec((per, nh, d, d), lambda i: (n_steps - 1 - i, 0, 0, 0))],
                out_specs=[pl.BlockSpec((rows, D_IN), rev(0)), pl.BlockSpec((SUBLANES, D_RET), full2)],
                out_shape=[jax.ShapeDtypeStruct((t, D_IN), MXU_DTYPE), jax.ShapeDtypeStruct((SUBLANES, D_RET), F32)],
                scratch_shapes=[pltpu.VMEM((nh, d, d), F32)],
                operands=[proj, proj, cos2, sin_signed, decay, xi, zeta, gain, o, states])


LRU_ACC = {"conv_w": 0, "conv_b": LRU_CONV, "gate_a_b": LRU_CONV + 1, "gate_x_b": LRU_CONV + 2,
           "lambda": LRU_CONV + 3, "norm_gain": LRU_CONV + 4}
LRU_ACC_ROWS = SUBLANES * (LRU_CONV + 5)


def _lru_bwd(proj, xc_all, h_all, conv_w, wa, ba, wx, bx, lam, gain, dproj_part, dmix_at):
    t = proj.shape[0]
    tm = min(MIX_ROW_TILE, t)
    c = D_LRU
    ni = t // tm

    def body(x_ref, xh_ref, g_ref, xc_ref, h_ref, hh_ref, cw_ref, wa_ref, ba_ref, wx_ref, bx_ref, lam_ref,
             gain_ref, acc_ref, dwa_ref, dwx_ref, a_scr, b_scr, mu_scr, carry_mu, carry_dxc, ctx):
        dp_ref = ctx["outs"][dproj_part][0]
        dmix = ctx["scratch"][dmix_at[0]][dmix_at[1]]
        fill = ctx["fill"]
        i = pl.program_id(0)
        r = ni - 1 - i

        @pl.when(i == 0)
        def _():
            acc_ref[...] = jnp.zeros_like(acc_ref)
            dwa_ref[...] = jnp.zeros_like(dwa_ref)
            dwx_ref[...] = jnp.zeros_like(dwx_ref)
            carry_mu[...] = jnp.zeros_like(carry_mu)
            carry_dxc[...] = jnp.zeros_like(carry_dxc)

        def add(name, val, k=0):
            lo = (LRU_ACC[name] + k) * SUBLANES
            acc_ref[lo:lo + SUBLANES, :] += _colsum8(val)

        fill()
        xc, h = xc_ref[...], h_ref[...]
        lam_v = lam_ref[...]
        sp = _softplus(-lam_v)
        rg, ig, a, m = _lru_gates(xc, wa_ref[...], ba_ref[...], wx_ref[...], bx_ref[...], sp, fill)
        gl, dgl = _gelu_parts(g_ref[...].astype(F32))
        fill()
        zn, rstd, _ = _rms_fwd(h * gl, gain_ref[...])
        dy = dmix[:, :c]
        dz, dgain = _rms_bwd(dy, zn, rstd, gain_ref[...])
        lo = LRU_ACC["norm_gain"] * SUBLANES
        acc_ref[lo:lo + SUBLANES, :] += dgain
        dp_ref[:, c:2 * c] = (dz * h * dgl).astype(dp_ref.dtype)
        dh = dz * gl
        fill()
        ga, gb = _group_scan(a, a * dh, reverse=True, fill=fill)
        a_scr[...] = ga
        b_scr[...] = gb
        mu_next_tile = carry_mu[...]
        carry_mu[...] = _carry_scan(a_scr, b_scr, mu_scr, mu_next_tile, reverse=True)
        fill()
        lam_t = dh + _shift_up(mu_scr[...], mu_next_tile, 1)
        h_prev = _shift_down(jnp.where(r == 0, 0.0, hh_ref[...]), h, 1)
        da = lam_t * h_prev
        dig = lam_t * m * xc
        dxc = lam_t * m * ig
        dlog_a = da * a - (lam_t * ig * xc) * (a * a) / m
        fill()
        dpr = dlog_a * ((-LRU_C) * sp) * rg * (1.0 - rg)
        add("lambda", dlog_a * ((-LRU_C) * rg) * (-_sigmoid(-lam_v)))
        dpi = dig * ig * (1.0 - ig)
        add("gate_a_b", dpr)
        add("gate_x_b", dpi)
        fill()
        dwa_ref[...] += _dot_tn(xc, dpr)
        dwx_ref[...] += _dot_tn(xc, dpi)
        dxc = dxc + _dot_nt(dpr, wa_ref[...]) + _dot_nt(dpi, wx_ref[...])
        fill()
        add("conv_b", dxc)
        x = x_ref[...].astype(F32)
        prev = jnp.where(r == 0, 0.0, xh_ref[...].astype(F32)[-SUBLANES:, :])
        cw = cw_ref[...]
        nxt = carry_dxc[...]
        carry_dxc[...] = dxc[:SUBLANES, :]
        dx = cw[LRU_CONV - 1:LRU_CONV, :] * dxc
        for k in range(LRU_CONV - 1):
            dx = dx + cw[k:k + 1, :] * _shift_up(dxc, nxt, LRU_CONV - 1 - k)
        fill()
        for k in range(LRU_CONV):
            add("conv_w", dxc * _shift_down(prev, x, LRU_CONV - 1 - k), k)
        dp_ref[:, :c] = dx.astype(dp_ref.dtype)

    hb = _halo_rows(proj.dtype)
    rev = lambda col: (lambda i: (ni - 1 - i, col))
    halo = lambda rows: (lambda i: (jnp.maximum((ni - 1 - i) * (tm // rows) - 1, 0), 0))
    full = lambda i: (0, 0)
    vec = pl.BlockSpec((1, c), full)
    mat = pl.BlockSpec((c, c), full)
    return dict(body=body, grid=(ni,), fill_points=16,
                in_specs=[pl.BlockSpec((tm, c), rev(0)), pl.BlockSpec((hb, c), halo(hb)), pl.BlockSpec((tm, c), rev(1)),
                          pl.BlockSpec((tm, c), rev(0)), pl.BlockSpec((tm, c), rev(0)),
                          pl.BlockSpec((SUBLANES, c), halo(SUBLANES)),
                          pl.BlockSpec((LRU_CONV, c), full), mat, vec, mat, vec, vec, vec],
                out_specs=[pl.BlockSpec((LRU_ACC_ROWS, c), full), mat, mat],
                out_shape=[jax.ShapeDtypeStruct((LRU_ACC_ROWS, c), F32), jax.ShapeDtypeStruct((c, c), F32),
                           jax.ShapeDtypeStruct((c, c), F32)],
                scratch_shapes=[pltpu.VMEM((tm, c), F32), pltpu.VMEM((tm, c), F32), pltpu.VMEM((tm, c), F32),
                                pltpu.VMEM((SUBLANES, c), F32), pltpu.VMEM((SUBLANES, c), F32)],
                operands=[proj, proj, proj, xc_all, h_all, h_all, conv_w, wa, ba, wx, bx, lam, gain])


def _mix_proj_bwd(dh1, dh1_b, w_out, w_in_blocks, x, g1, dproj_part):
    t = x.shape[0]
    tm = min(MIX_ROW_TILE, t)
    ni = t // tm
    nb, _, cb = w_in_blocks.shape
    first_free = -(-2 * D_LRU // cb)
    du = [None]

    def term(dp_ref, w_ref, d):
        part = _dot_nt(dp_ref[:, d * cb:(d + 1) * cb], w_ref[d])
        du[0] = part if du[0] is None else du[0] + part

    def head(dh_ref, dhb_ref, wo_ref, wi_ref, x_ref, g_ref, gx_ref, dg_ref, dmix, ctx):
        @pl.when(pl.program_id(0) == 0)
        def _():
            dg_ref[...] = jnp.zeros_like(dg_ref)
        dmix[...] = _dot_nt(dhb_ref[...], wo_ref[...])
        du[0] = None

    def units(dh_ref, dhb_ref, wo_ref, wi_ref, x_ref, g_ref, gx_ref, dg_ref, dmix, ctx):
        dp_ref = ctx["outs"][dproj_part][0]
        return [lambda d=d: term(dp_ref, wi_ref, d) for d in range(first_free, nb)]

    def tail(dh_ref, dhb_ref, wo_ref, wi_ref, x_ref, g_ref, gx_ref, dg_ref, dmix, ctx):
        dp_ref = ctx["outs"][dproj_part][0]
        for d in range(first_free):
            term(dp_ref, wi_ref, d)
        n, rstd, _ = _rms_fwd(x_ref[...], g_ref[...])
        dx, dg = _rms_bwd(du[0], n, rstd, g_ref[...])
        dg_ref[...] += dg
        gx_ref[...] = dx + dh_ref[...]

    row = lambda i: (ni - 1 - i, 0)
    const = lambda i: (0, 0)
    tile = pl.BlockSpec((tm, D_MODEL), row)
    return dict(head=head, units=units, tail=tail, grid=(ni,),
                in_specs=[tile, tile, _resident(w_out.shape), _resident(w_in_blocks.shape), tile,
                          pl.BlockSpec((1, D_MODEL), const)],
                out_specs=[tile, pl.BlockSpec((SUBLANES, D_MODEL), const)],
                out_shape=[jax.ShapeDtypeStruct((t, D_MODEL), F32), jax.ShapeDtypeStruct((SUBLANES, D_MODEL), F32)],
                scratch_shapes=[pltpu.VMEM((tm, D_MODEL), F32)],
                operands=[dh1, dh1_b, w_out, w_in_blocks, x, g1])


def _pair_sum(core, a, b, name):
    n, r, c = b.shape
    spec = pl.BlockSpec((None, r, c), lambda q, core: (q, 0, 0))

    def body(core_ref, a_ref, b_ref, o_ref):
        o_ref[...] = (a_ref[...].astype(F32) + b_ref[...].astype(F32)).astype(o_ref.dtype)

    return pl.pallas_call(
        body, name=name,
        grid_spec=pltpu.PrefetchScalarGridSpec(
            num_scalar_prefetch=1, grid=(n,),
            in_specs=[pl.BlockSpec((None, r, c), lambda q, core: (2 * q + core[0], 0, 0)), spec], out_specs=spec),
        out_shape=jax.ShapeDtypeStruct(b.shape, b.dtype),
        compiler_params=pltpu.CompilerParams(dimension_semantics=("arbitrary",), vmem_limit_bytes=VMEM_LIMIT),
    )(core, a, b)


ADAMW_BLOCK_BYTES = 4 * 1024 * 1024


def _sum_adamw(parts, w, m, v, name):
    n_parts, r, c = parts.shape
    tr = r
    while n_parts * tr * c * parts.dtype.itemsize > ADAMW_BLOCK_BYTES and tr % (4 * SUBLANES) == 0:
        tr //= 2

    def body(p_ref, w_ref, m_ref, v_ref, g_ref, d_ref, nm_ref, nv_ref):
        g = p_ref[0].astype(F32)
        for s in range(1, n_parts):
            g = g + p_ref[s].astype(F32)
        nm = ADAM_B1 * m_ref[...] + (1.0 - ADAM_B1) * g
        nv = ADAM_B2 * v_ref[...] + (1.0 - ADAM_B2) * (g * g)
        m_hat = nm / (1.0 - ADAM_B1 ** ADAM_STEP)
        v_hat = nv / (1.0 - ADAM_B2 ** ADAM_STEP)
        g_ref[...] = g
        d_ref[...] = -ADAM_LR * (m_hat / (jnp.sqrt(v_hat) + ADAM_EPS) + ADAM_WD * w_ref[...])
        nm_ref[...] = nm
        nv_ref[...] = nv

    row = pl.BlockSpec((tr, c), lambda i: (i, 0))
    return _call(body, name=name, grid=(r // tr,),
                 in_specs=[pl.BlockSpec((n_parts, tr, c), lambda i: (0, i, 0)), row, row, row],
                 out_specs=[row, row, row, row], out_shape=[jax.ShapeDtypeStruct((r, c), F32)] * 4,
                 operands=[parts, w, m, v])


MATRICES = ("w_in", "w_out", "ffn_up_w", "ffn_down_w")
CONVS = ("lru_conv_w", "ffn_conv_w")
REPLICATED = ("norm1_gain", "lru_conv_b", "lru_gate_a_w", "lru_gate_a_b", "lru_gate_x_w", "lru_gate_x_b", "lru_lambda",
              "lru_norm_gain", "ret_norm_gain", "norm2_gain", "ffn_conv_b", "final_norm_gain")
WEIGHTS = ("norm1_gain", "w_in", "lru_conv_w", "lru_conv_b", "lru_gate_a_w", "lru_gate_a_b", "lru_gate_x_w",
           "lru_gate_x_b", "lru_lambda", "lru_norm_gain", "ret_norm_gain", "w_out", "norm2_gain", "ffn_up_w",
           "ffn_conv_w", "ffn_conv_b", "ffn_down_w", "final_norm_gain")


def _rows(a, pad_to):
    a = a.reshape(-1, LANES)
    pad = (-a.shape[0]) % pad_to
    return jnp.pad(a, ((0, pad), (0, 0))) if pad else a


def _pack(arrays, pad_to):
    rows, layout, at = [], [], 0
    for a in arrays:
        r = _rows(a, pad_to)
        layout.append((at, a.size // LANES, a.shape))
        rows.append(r)
        at += r.shape[0]
    return jnp.concatenate(rows, axis=0), layout


def _unpack(packed, layout):
    lead = packed.shape[:-2]
    return [packed[..., at:at + n, :].reshape(lead + shape) for at, n, shape in layout]


def _conv_rows(lru, ffn, dtype, pad_to):
    lead = lru.shape[:-2]
    flat = jnp.concatenate([lru.reshape(lead + (-1,)), ffn.reshape(lead + (-1,))], axis=-1).astype(dtype)
    rows = flat.shape[-1] // LANES
    pad = (-rows) % pad_to
    return jnp.pad(flat.reshape(lead + (rows, LANES)), [(0, 0)] * len(lead) + [(0, pad), (0, 0)])


def _column_blocks(full):
    r, c = full.shape
    return full.reshape(r, N_DEV, c // N_DEV).transpose(1, 0, 2)


def _block_diag(w):
    nh, d, _ = w.shape
    eye = jnp.eye(nh, dtype=w.dtype)
    return (w[:, :, None, :] * eye[:, None, :, None]).reshape(nh * d, nh * d)


def _diag_blocks(dense, nh):
    d = dense.shape[0] // nh
    blocks = dense.reshape(nh, d, nh, d)
    return jnp.stack([blocks[h, :, h, :] for h in range(nh)], axis=0)


def kernel(x, norm1_gain, w_in, lru_conv_w, lru_conv_b, lru_gate_a_w, lru_gate_a_b, lru_gate_x_w, lru_gate_x_b, lru_lambda, lru_norm_gain, ret_norm_gain, w_out, norm2_gain, ffn_up_w, ffn_conv_w, ffn_conv_b, ffn_down_w, final_norm_gain, loss_target, m_norm1_gain, m_w_in, m_lru_conv_w, m_lru_conv_b, m_lru_gate_a_w, m_lru_gate_a_b, m_lru_gate_x_w, m_lru_gate_x_b, m_lru_lambda, m_lru_norm_gain, m_ret_norm_gain, m_w_out, m_norm2_gain, m_ffn_up_w, m_ffn_conv_w, m_ffn_conv_b, m_ffn_down_w, m_final_norm_gain, v_norm1_gain, v_w_in, v_lru_conv_w, v_lru_conv_b, v_lru_gate_a_w, v_lru_gate_a_b, v_lru_gate_x_w, v_lru_gate_x_b, v_lru_lambda, v_lru_norm_gain, v_ret_norm_gain, v_w_out, v_norm2_gain, v_ffn_up_w, v_ffn_conv_w, v_ffn_conv_b, v_ffn_down_w, v_final_norm_gain):
    args = dict(locals())
    given = {n: args[n] for n in WEIGHTS}
    out_shape = {n: given[n].shape for n in WEIGHTS}

    def plain(a):
        return a.reshape(1, -1) if a.ndim <= 2 else a[0]

    w = {n: plain(given[n]) for n in WEIGHTS}
    mom_m = {n: plain(args["m_" + n]) for n in WEIGHTS}
    mom_v = {n: plain(args["v_" + n]) for n in WEIGHTS}
    x2, target = x[0], loss_target[0]
    t = x2.shape[0]
    core = lax.axis_index("c").astype(jnp.int32).reshape(1)
    res = {}

    conv_pad = _conv_rows(w["lru_conv_w"], w["ffn_conv_w"], F32, SUBLANES)
    first = _gather_first([w["w_in"].astype(MXU_DTYPE), conv_pad])
    w_in_blocks, conv_all = _run_comms([first, _gather_second(first.out_shape)], "w_in_all_gather")
    n_lru = w["lru_conv_w"].size
    conv_flat = conv_all.reshape(N_DEV, -1)
    lru_cw = conv_flat[:, :n_lru].reshape((N_DEV,) + w["lru_conv_w"].shape).transpose(1, 0, 2).reshape(LRU_CONV, D_LRU)
    ffn_cw = conv_flat[:, n_lru:n_lru + w["ffn_conv_w"].size].reshape((N_DEV,) + w["ffn_conv_w"].shape)
    ffn_cw = ffn_cw.transpose(1, 0, 2).reshape(FFN_CONV, 2 * D_FF)

    cos2, sin_signed = _rope_tables(t)
    wa = _block_diag(w["lru_gate_a_w"]).astype(MXU_DTYPE)
    wx = _block_diag(w["lru_gate_x_w"]).astype(MXU_DTYPE)
    gf = w["final_norm_gain"]

    early = _gather_first([w["w_out"].astype(MXU_DTYPE), w["ffn_down_w"].astype(MXU_DTYPE)])
    (u1, proj), (w_out_part, down_part) = _inproj_fwd(x2, w["norm1_gain"], w_in_blocks, early)
    ((xc, h_lru, y_lru), (o_ret, y_ret, states)), (w_out_blocks, down_blocks, up_part) = _fused(
        [_lru_fwd(proj, lru_cw, w["lru_conv_b"], wa, w["lru_gate_a_b"], wx, w["lru_gate_x_b"], w["lru_lambda"],
                  w["lru_norm_gain"]),
         _ret_fwd(proj, cos2, sin_signed, w["ret_norm_gain"])],
        "mix_fwd", _both(_gather_second([w_out_part, down_part]), _gather_first([w["ffn_up_w"].astype(MXU_DTYPE)])))
    w_out_full = w_out_blocks.reshape(D_MODEL, D_MODEL)
    w_down_full = down_blocks.reshape(D_FF, D_MODEL)

    (h1, u2), (up_blocks,) = _outproj_fwd(x2, y_lru, y_ret, w_out_full, w["norm2_gain"], _gather_second([up_part]))
    up_a, up_v, conv_a, conv_v, act, dh2, dh2_b, dgf, loss_local = _ffn_fwd(u2, up_blocks, ffn_cw, w["ffn_conv_b"],
                                                                            w_down_full, h1, gf, target)

    def to_owner_chips(blocks, names, tag):
        theirs = _run_comms([_pair_exchange(blocks)], "grads_pair_exchange_" + tag)
        return [_pair_sum(core, a, b, "grads_pair_sum_" + n) for n, a, b in zip(names, blocks, theirs)]

    def adamw(name, parts):
        res[name] = _sum_adamw(parts, w[name], mom_m[name], mom_v[name], "adamw_" + name)

    g = {"final_norm_gain": dgf[0]}
    dup_a, dup_v, acc_a, acc_v, dh1, dh1_b, dg2 = _ffn_bwd(
        dh2, dh2_b, w_down_full, up_a, up_v, conv_a, conv_v, ffn_cw, up_blocks, h1, w["norm2_gain"], None)
    per_col = lambda a: a[:, ::SUBLANES].transpose(1, 0, 2).reshape(FFN_CONV + 1, D_FF)
    acc = jnp.concatenate([per_col(acc_a), per_col(acc_v)], axis=1)
    g_ffn_cw, g["ffn_conv_b"] = acc[:FFN_CONV], acc[FFN_CONV:]
    g["norm2_gain"] = dg2[:1]
    g_up = _mm_tn(u2, dup_a, "ffn_up_wgrad_a", blocks=N_DEV // 2, room=N_DEV)
    g_up = _mm_tn(u2, dup_v, "ffn_up_wgrad_v", blocks=N_DEV // 2, before=g_up)
    g_out_lru, (up_theirs,) = _mm_tn(y_lru, dh1_b, "w_out_wgrad_lru", comm=_pair_exchange([g_up]))
    up_sums = [_pair_sum(core, g_up, up_theirs, "grads_pair_sum_ffn_up_w")]
    g_down, (up_parts,) = _mm_tn(act, dh2_b, "ffn_down_wgrad", comm=_chip_exchange(up_sums))
    adamw("ffn_up_w", up_parts)
    g_out = jnp.concatenate([g_out_lru, _mm_tn(y_ret, dh1_b, "w_out_wgrad_ret")], axis=0)
    low_sums = to_owner_chips([g_down.reshape(N_DEV, D_FF // N_DEV, D_MODEL),
                               g_out.reshape(N_DEV, D_MODEL // N_DEV, D_MODEL)], ["ffn_down_w", "w_out"], "low")
    (dproj, dgain_ret), (grad_x, dg1), (lru_acc, dwa, dwx) = _fused(
        [_ret_bwd(proj, cos2, sin_signed, w["ret_norm_gain"], o_ret, states, dmix_at=(1, 0)),
         _mix_proj_bwd(dh1, dh1_b, w_out_full, w_in_blocks, x2, w["norm1_gain"], dproj_part=0),
         _lru_bwd(proj, xc, h_lru, lru_cw, wa, w["lru_gate_a_b"], wx, w["lru_gate_x_b"], w["lru_lambda"],
                  w["lru_norm_gain"], dproj_part=0, dmix_at=(1, 0))],
        "mix_bwd")
    g["norm1_gain"] = dg1[:1]
    g["ret_norm_gain"] = dgain_ret[:1]
    lru_acc = lru_acc[::SUBLANES]
    g_lru_cw = lru_acc[:LRU_CONV]
    for name in ("conv_b", "gate_a_b", "gate_x_b", "lambda", "norm_gain"):
        g["lru_" + name] = lru_acc[LRU_ACC[name]:LRU_ACC[name] + 1]
    g["lru_gate_a_w"] = _diag_blocks(dwa, LRU_HEADS)
    g["lru_gate_x_w"] = _diag_blocks(dwx, LRU_HEADS)
    rep_packed, rep_layout = _pack([g[n] for n in REPLICATED] + [loss_local], SUBLANES)
    g_in, (down_parts, out_parts, rep_part) = _mm_tn(u1, dproj, "w_in_wgrad", blocks=N_DEV,
                                                     comm=_both(_chip_exchange(low_sums), _gather_first([rep_packed])))
    adamw("ffn_down_w", down_parts)
    adamw("w_out", out_parts)
    g_conv = _conv_rows(_column_blocks(g_lru_cw), _column_blocks(g_ffn_cw), GRAD_DTYPE, 2 * SUBLANES)
    in_sums = to_owner_chips([g_in, g_conv], ["w_in", "conv"], "in")
    in_parts, conv_parts, rep_parts = _run_comms([_both(_chip_exchange(in_sums), _gather_second([rep_part]))],
                                                 "last_grads_exchange")
    adamw("w_in", in_parts)
    pad16 = lambda d: _conv_rows(d["lru_conv_w"], d["ffn_conv_w"], F32, 2 * SUBLANES)
    conv_res = _sum_adamw(conv_parts, pad16(w), pad16(mom_m), pad16(mom_v), "adamw_conv")
    for n, lo, hi in (("lru_conv_w", 0, n_lru), ("ffn_conv_w", n_lru, n_lru + w["ffn_conv_w"].size)):
        res[n] = [r.reshape(-1)[lo:hi].reshape(w[n].shape) for r in conv_res]
    no_state = jnp.zeros_like(loss_local)
    rep_res = _sum_adamw(rep_parts, *[_pack([d[n] for n in REPLICATED] + [no_state], SUBLANES)[0]
                                      for d in (w, mom_m, mom_v)], "adamw_replicated")
    for k in range(4):
        for n, a in zip(REPLICATED, _unpack(rep_res[k], rep_layout)):
            res.setdefault(n, [None] * 4)[k] = a
    loss = _unpack(rep_res[0], rep_layout)[-1][0, 0]

    outs = [loss, grad_x[None]]
    for k in range(4):
        outs += [res[n][k].reshape(out_shape[n]) for n in WEIGHTS]
    return tuple(outs)
```

```python
import math

import numpy as np
import jax
import jax.numpy as jnp
from jax import lax
from jax.experimental import pallas as pl
from jax.experimental.pallas import tpu as pltpu

F32 = jnp.float32
BF16 = jnp.bfloat16
MXU_DTYPE = jnp.bfloat16
GRAD_DTYPE = jnp.bfloat16

N_DEV = 8
N_CHIPS = 4
D_MODEL = 1024
D_LRU = 512
LRU_HEADS = 8
LRU_CONV = 4
LRU_C = 8.0
D_RET = 512
RET_HEADS = 4
RET_HEAD_DIM = 128
RET_CHUNK = 128
ROPE_BASE = 10000.0
D_IN = 3072
D_FF = 3072
FFN_CONV = 3
NORM_EPS = 1e-6

ADAM_LR = 0.001
ADAM_B1 = 0.9
ADAM_B2 = 0.999
ADAM_EPS = 1e-08
ADAM_WD = 0.01
ADAM_STEP = 10

SUBLANES = 8
LANES = 128
VMEM_LIMIT = 48 * 1024 * 1024
FUSED_VMEM_LIMIT = VMEM_LIMIT
FFN_BWD_VMEM_LIMIT = 56 * 1024 * 1024

ROW_TILE = 256
MIX_ROW_TILE = 256
PROJ_ROW_TILE = 512
WGRAD_ROWS = 2048
WGRAD_TILE = 1024
WGRAD_BLOCK_COLUMNS = 768
WGRAD_SLOTS = 3

MESH = pl.DeviceIdType.MESH
ANY = pl.BlockSpec(memory_space=pl.ANY)


def _dot(a, b):
    return jnp.dot(a.astype(MXU_DTYPE), b.astype(MXU_DTYPE), preferred_element_type=F32)


def _dot_nt(a, b):
    return lax.dot_general(a.astype(MXU_DTYPE), b.astype(MXU_DTYPE), (((1,), (1,)), ((), ())),
                           preferred_element_type=F32)


def _dot_tn(a, b):
    return lax.dot_general(a.astype(MXU_DTYPE), b.astype(MXU_DTYPE), (((0,), (0,)), ((), ())),
                           preferred_element_type=F32)


def _sigmoid(x):
    return 0.5 + 0.5 * jnp.tanh(0.5 * x)


_GELU_C = math.sqrt(2.0 / math.pi)
_GELU_C3 = _GELU_C * 0.044715


def _gelu_parts(x):
    x2 = x * x
    t = jnp.tanh(x * (_GELU_C + _GELU_C3 * x2))
    cdf = 0.5 + 0.5 * t
    g = x * cdf
    dg = cdf + (0.5 * x) * (1.0 - t * t) * (_GELU_C + (3.0 * _GELU_C3) * x2)
    return g, dg


def _gelu(x):
    t = jnp.tanh(_GELU_C * (x + 0.044715 * (x * x * x)))
    return x * (0.5 * (1.0 + t))


def _softplus(x):
    return jnp.maximum(x, 0.0) + jnp.log1p(jnp.exp(-jnp.abs(x)))


def _bcast_row(x, r, rows=SUBLANES):
    return jnp.broadcast_to(x[r:r + 1, :], (rows, x.shape[1]))


def _colsum8(x):
    return jnp.broadcast_to(jnp.sum(x, axis=0, keepdims=True), (SUBLANES, x.shape[1]))


def _groups(x):
    return x.reshape(x.shape[0] // SUBLANES, SUBLANES, x.shape[1])


def _shift_down(prev8, tile, s):
    if s == 0:
        return tile
    own = pltpu.roll(_groups(tile), s, 1)
    before = jnp.concatenate([pltpu.roll(_groups(prev8), s, 1), own[:-1]], axis=0)
    row = lax.broadcasted_iota(jnp.int32, own.shape, 1)
    return jnp.where(row >= s, own, before).reshape(tile.shape)


def _shift_up(tile, next8, s):
    if s == 0:
        return tile
    own = pltpu.roll(_groups(tile), SUBLANES - s, 1)
    after = jnp.concatenate([own[1:], pltpu.roll(_groups(next8), SUBLANES - s, 1)], axis=0)
    row = lax.broadcasted_iota(jnp.int32, own.shape, 1)
    return jnp.where(row < SUBLANES - s, own, after).reshape(tile.shape)


def _group_scan(a, b, reverse, fill=lambda: None):
    n, c = a.shape
    row = lax.broadcasted_iota(jnp.int32, a.shape, 0) & (SUBLANES - 1)

    def within_group(x, shift):
        return pltpu.roll(x.reshape(n // SUBLANES, SUBLANES, c), shift, 1).reshape(n, c)

    for s in (1, 2, 4):
        if s > 1:
            fill()
        shift = (SUBLANES - s) if reverse else s
        a_sh = within_group(a, shift)
        b_sh = within_group(b, shift)
        m = (row <= SUBLANES - 1 - s) if reverse else (row >= s)
        b = jnp.where(m, a * b_sh + b, b)
        a = jnp.where(m, a * a_sh, a)
    return a, b


def _carry_scan(a_ref, b_ref, out_ref, carry0, reverse):
    n_groups = a_ref.shape[0] // SUBLANES
    carry = carry0
    for i in range(n_groups):
        r0 = ((n_groups - 1 - i) if reverse else i) * SUBLANES
        hg = a_ref[r0:r0 + SUBLANES, :] * carry + b_ref[r0:r0 + SUBLANES, :]
        out_ref[r0:r0 + SUBLANES, :] = hg
        carry = _bcast_row(hg, 0 if reverse else SUBLANES - 1)
    return carry


def _rms_fwd(h, gain):
    rstd = lax.rsqrt(jnp.mean(h * h, axis=-1, keepdims=True) + NORM_EPS)
    n = h * rstd
    return n, rstd, n * gain


def _rms_bwd(dy, n, rstd, gain):
    dn = dy * gain
    dh = rstd * (dn - n * jnp.mean(dn * n, axis=-1, keepdims=True))
    return dh, _colsum8(dy * n)


def _halo_rows(dtype):
    return SUBLANES * (4 // jnp.dtype(dtype).itemsize)


def _halo_map(tile_rows, col, halo_rows=SUBLANES):
    per = tile_rows // halo_rows
    return lambda i: (jnp.maximum(i * per - 1, 0), col)


def _resident(shape):
    return pl.BlockSpec(shape, lambda *_: (0,) * len(shape), pipeline_mode=pl.Buffered(1))


def _place():
    x, y, c = lax.axis_index("x"), lax.axis_index("y"), lax.axis_index("c")
    chips = [(1 - x, y), (x, 1 - y), (1 - x, 1 - y)]
    return x, y, c, chips


def _dev(x, y, c):
    return 4 * x + 2 * y + c


class _Copy:
    def __init__(self, make):
        self.make = make

    def start(self):
        self.make().start()

    def wait(self):
        self.make().wait()

    def wait_send(self):
        self.make().wait_send()

    def wait_recv(self):
        self.make().wait_recv()


def _remote(src, dst, send_sem, recv_sem, to):
    return _Copy(lambda: pltpu.make_async_remote_copy(src_ref=src, dst_ref=dst, send_sem=send_sem, recv_sem=recv_sem,
                                                      device_id=to, device_id_type=MESH))


def _local(src, dst, sem):
    return _Copy(lambda: pltpu.make_async_copy(src, dst, sem))


class _Comm:
    def __init__(self, operands, out_shape, sems, descs, aliases=()):
        self.operands, self.out_shape, self.sems, self.descs, self.aliases = operands, out_shape, sems, descs, aliases

    def start(self, ins, outs, sems):
        local, sends, _ = self.descs(ins, outs, sems)
        for cp in sends + local:
            cp.start()

    def wait(self, ins, outs, sems):
        local, sends, recvs = self.descs(ins, outs, sems)
        for cp in recvs:
            cp.wait_recv()
        for cp in sends:
            cp.wait_send()
        for cp in local:
            cp.wait()


def _gather_first(shards):
    n = len(shards)

    def descs(ins, outs, sems):
        send, recv, loc = sems
        x, y, c, chips = _place()
        me = _dev(x, y, c)
        targets = [(x, y, 1 - c)] + [(*chip, c) for chip in chips]
        local, sends, recvs = [], [], []
        for t in range(n):
            local.append(_local(ins[t], outs[t].at[me], loc.at[t]))
            for k, to in enumerate(targets):
                i = 4 * t + k
                sends.append(_remote(ins[t], outs[t].at[me], send.at[i], recv.at[i], to))
                recvs.append(_remote(ins[t], outs[t].at[_dev(*to)], send.at[i], recv.at[i], to))
        return local, sends, recvs

    return _Comm(list(shards), [jax.ShapeDtypeStruct((N_DEV,) + s.shape, s.dtype) for s in shards],
                 [pltpu.SemaphoreType.DMA((4 * n,)), pltpu.SemaphoreType.DMA((4 * n,)), pltpu.SemaphoreType.DMA((n,))],
                 descs)


def _gather_second(gathered):
    n = len(gathered)

    def descs(ins, outs, sems):
        send, recv = sems
        x, y, c, chips = _place()
        sends, recvs = [], []
        for t in range(n):
            for j, chip in enumerate(chips):
                i = 3 * t + j
                have, get = _dev(*chip, c), _dev(*chip, 1 - c)
                sends.append(_remote(outs[t].at[have], outs[t].at[have], send.at[i], recv.at[i], (x, y, 1 - c)))
                recvs.append(_remote(outs[t].at[have], outs[t].at[get], send.at[i], recv.at[i], (x, y, 1 - c)))
        return [], sends, recvs

    return _Comm(list(gathered), [jax.ShapeDtypeStruct(g.shape, g.dtype) for g in gathered],
                 [pltpu.SemaphoreType.DMA((3 * n,)), pltpu.SemaphoreType.DMA((3 * n,))], descs,
                 aliases=[(t, t) for t in range(n)])


def _pair_exchange(blocks):
    n = len(blocks)

    def descs(ins, outs, sems):
        send, recv = sems
        x, y, c, _ = _place()
        sends, recvs = [], []
        for t in range(n):
            for q in range(N_CHIPS):
                i = N_CHIPS * t + q
                cp = _remote(ins[t].at[2 * q + 1 - c], outs[t].at[q], send.at[i], recv.at[i], (x, y, 1 - c))
                sends.append(cp)
                recvs.append(cp)
        return [], sends, recvs

    return _Comm(list(blocks), [jax.ShapeDtypeStruct((N_CHIPS,) + b.shape[1:], b.dtype) for b in blocks],
                 [pltpu.SemaphoreType.DMA((N_CHIPS * n,)), pltpu.SemaphoreType.DMA((N_CHIPS * n,))], descs)


def _chip_exchange(blocks):
    n = len(blocks)

    def descs(ins, outs, sems):
        send, recv, loc = sems
        x, y, c, chips = _place()
        me = 2 * x + y
        local, sends, recvs = [], [], []
        for t in range(n):
            local.append(_local(ins[t].at[me], outs[t].at[me], loc.at[t]))
            for j, (px, py) in enumerate(chips):
                i = 3 * t + j
                q = 2 * px + py
                sends.append(_remote(ins[t].at[q], outs[t].at[me], send.at[i], recv.at[i], (px, py, c)))
                recvs.append(_remote(ins[t].at[q], outs[t].at[q], send.at[i], recv.at[i], (px, py, c)))
        return local, sends, recvs

    return _Comm(list(blocks), [jax.ShapeDtypeStruct(b.shape, b.dtype) for b in blocks],
                 [pltpu.SemaphoreType.DMA((3 * n,)), pltpu.SemaphoreType.DMA((3 * n,)), pltpu.SemaphoreType.DMA((n,))],
                 descs)


def _both(a, b):
    na, oa, sa = len(a.operands), len(a.out_shape), len(a.sems)

    def descs(ins, outs, sems):
        local_a, sends_a, recvs_a = a.descs(ins[:na], outs[:oa], sems[:sa])
        local_b, sends_b, recvs_b = b.descs(ins[na:], outs[oa:], sems[sa:])
        return local_a + local_b, sends_a + sends_b, recvs_a + recvs_b

    return _Comm(a.operands + b.operands, a.out_shape + b.out_shape, a.sems + b.sems, descs,
                 aliases=list(a.aliases) + [(na + i, oa + o) for i, o in b.aliases])


def _run_comms(comms, name):
    first = comms[0]
    n_in, n_out = len(first.operands), len(first.out_shape)

    def body(*refs):
        ins, outs, sems = refs[:n_in], refs[n_in:n_in + n_out], list(refs[n_in + n_out:])
        for k, comm in enumerate(comms):
            mine = [sems.pop(0) for _ in comm.sems]
            comm.start(ins if k == 0 else outs, outs, mine)
            comm.wait(ins if k == 0 else outs, outs, mine)

    outs = pl.pallas_call(
        body, name=name, out_shape=first.out_shape, in_specs=[ANY] * n_in, out_specs=[ANY] * n_out,
        scratch_shapes=[s for comm in comms for s in comm.sems], input_output_aliases=dict(first.aliases),
    )(*first.operands)
    return list(outs)


def _call(body, *, name, grid, in_specs, out_specs, out_shape, operands, scratch_shapes=(), comm=None, aliases=None,
          vmem_limit=VMEM_LIMIT):
    sem = ("arbitrary",) * len(grid)
    params = pltpu.CompilerParams(dimension_semantics=sem, vmem_limit_bytes=vmem_limit)
    aliases = dict(aliases or {})
    if comm is None:
        return pl.pallas_call(body, name=name, grid=grid, in_specs=in_specs, out_specs=out_specs, out_shape=out_shape,
                              scratch_shapes=list(scratch_shapes), input_output_aliases=aliases,
                              compiler_params=params)(*operands)
    n_in, n_out, n_scr = len(in_specs), len(out_specs), len(scratch_shapes)
    c_in, c_out = len(comm.operands), len(comm.out_shape)

    def wrapped(*refs):
        refs = list(refs)
        ins, refs = refs[:n_in], refs[n_in:]
        cins, refs = refs[:c_in], refs[c_in:]
        outs, refs = refs[:n_out], refs[n_out:]
        couts, refs = refs[:c_out], refs[c_out:]
        scr, csems = refs[:n_scr], refs[n_scr:]
        first = last = None
        for axis, size in enumerate(grid):
            at_first, at_last = pl.program_id(axis) == 0, pl.program_id(axis) == size - 1
            first = at_first if first is None else first & at_first
            last = at_last if last is None else last & at_last

        @pl.when(first)
        def _():
            comm.start(cins, couts, csems)

        body(*ins, *outs, *scr)

        @pl.when(last)
        def _():
            comm.wait(cins, couts, csems)

    res = pl.pallas_call(
        wrapped, name=name, grid=grid, in_specs=list(in_specs) + [ANY] * c_in, out_specs=list(out_specs) + [ANY] * c_out,
        out_shape=list(out_shape) + list(comm.out_shape), scratch_shapes=list(scratch_shapes) + list(comm.sems),
        input_output_aliases={**aliases, **{n_in + i: n_out + o for i, o in comm.aliases}}, compiler_params=params,
    )(*operands, *comm.operands)
    return list(res[:n_out]), list(res[n_out:])


def _mm_tn(a, b, name, blocks=1, comm=None, room=None, before=None):
    t, m = a.shape
    n = b.shape[1]
    tk = min(WGRAD_ROWS, t)
    nk = t // tk
    cb = n // blocks
    per = max(1, WGRAD_BLOCK_COLUMNS // cb) if blocks > 1 else 1
    tn = per * cb if blocks > 1 else min(WGRAD_TILE, n)
    tm = min(WGRAD_TILE, m)
    assert blocks == 1 or tm == m
    total = blocks if before is None and room is None else (room if before is None else before.shape[0])
    first = (total - blocks) // per if before is not None else 0
    assert blocks > 1 or total == 1

    ni, nj = m // tm, n // tn
    steps = ni * nj * nk

    def body(*refs):
        (a_hbm, b_hbm), (o_ref, acc, a_buf, b_buf, sems) = refs[:2], refs[-5:]
        k = pl.program_id(2)
        step = (pl.program_id(0) * nj + pl.program_id(1)) * nk + k

        def fetch(s):
            slot = s % WGRAD_SLOTS
            rows = pl.ds((s % nk) * tk, tk)
            return (pltpu.make_async_copy(a_hbm.at[rows, pl.ds((s // (nk * nj)) * tm, tm)], a_buf.at[slot], sems.at[0, slot]),
                    pltpu.make_async_copy(b_hbm.at[rows, pl.ds(((s // nk) % nj) * tn, tn)], b_buf.at[slot], sems.at[1, slot]))

        def start(s):
            for copy in fetch(s):
                copy.start()

        @pl.when(step == 0)
        def _():
            for s in range(min(WGRAD_SLOTS - 1, steps)):
                start(s)

        @pl.when(step + WGRAD_SLOTS - 1 < steps)
        def _():
            start(step + WGRAD_SLOTS - 1)

        for copy in fetch(step):
            copy.wait()

        @pl.when(k == 0)
        def _():
            acc[...] = jnp.zeros_like(acc)
        slot = step % WGRAD_SLOTS
        acc[...] += _dot_tn(a_buf[slot], b_buf[slot])

        @pl.when(k == nk - 1)
        def _():
            if blocks == 1:
                o_ref[...] = acc[...].astype(o_ref.dtype)
            else:
                for s in range(per):
                    o_ref[s] = acc[:, s * cb:(s + 1) * cb].astype(o_ref.dtype)

    if blocks == 1:
        out_spec = pl.BlockSpec((tm, tn), lambda i, j, k: (i, j))
        out_shape = jax.ShapeDtypeStruct((m, n), GRAD_DTYPE)
    else:
        out_spec = pl.BlockSpec((per, m, cb), lambda i, j, k: (first + j, 0, 0))
        out_shape = jax.ShapeDtypeStruct((total, m, cb), GRAD_DTYPE)
    given = [] if before is None else [before]
    res = _call(body, name=name, grid=(ni, nj, nk), comm=comm, in_specs=[ANY] * (2 + len(given)),
                out_specs=[out_spec], out_shape=[out_shape], operands=[a, b] + given,
                aliases={2: 0} if given else None,
                scratch_shapes=[pltpu.VMEM((tm, tn), F32), pltpu.VMEM((WGRAD_SLOTS, tk, tm), a.dtype),
                                pltpu.VMEM((WGRAD_SLOTS, tk, tn), b.dtype), pltpu.SemaphoreType.DMA((2, WGRAD_SLOTS))])
    return res[0] if comm is None else (res[0][0], res[1])


INPROJ_TN = 1024


def _inproj_fwd(x, g1, w_blocks, comm):
    t = x.shape[0]
    tm = min(PROJ_ROW_TILE, t)
    nb, _, cb = w_blocks.shape

    def body(x_ref, g_ref, w_hbm, u_ref, p_ref, w_all, sems):
        @pl.when(pl.program_id(0) == 0)
        def _():
            copies = [pltpu.make_async_copy(w_hbm.at[d], w_all.at[:, pl.ds(d * cb, cb)], sems.at[d]) for d in range(nb)]
            for cp in copies:
                cp.start()
            for cp in copies:
                cp.wait()

        _, _, u = _rms_fwd(x_ref[...], g_ref[...])
        u = u.astype(MXU_DTYPE)
        u_ref[...] = u
        for lo in range(0, D_IN, INPROJ_TN):
            p_ref[:, lo:lo + INPROJ_TN] = _dot(u, w_all[:, lo:lo + INPROJ_TN]).astype(p_ref.dtype)

    return _call(body, name="inproj_fwd", grid=(t // tm,), comm=comm,
                 in_specs=[pl.BlockSpec((tm, D_MODEL), lambda i: (i, 0)), pl.BlockSpec((1, D_MODEL), lambda i: (0, 0)), ANY],
                 out_specs=[pl.BlockSpec((tm, D_MODEL), lambda i: (i, 0)), pl.BlockSpec((tm, D_IN), lambda i: (i, 0))],
                 out_shape=[jax.ShapeDtypeStruct((t, D_MODEL), MXU_DTYPE), jax.ShapeDtypeStruct((t, D_IN), MXU_DTYPE)],
                 scratch_shapes=[pltpu.VMEM((D_MODEL, nb * cb), w_blocks.dtype), pltpu.SemaphoreType.DMA((nb,))],
                 operands=[x, g1, w_blocks])


def _lru_gates(xc, wa, ba, wx, bx, sp, fill=lambda: None):
    r = _sigmoid(_dot(xc, wa) + ba)
    fill()
    ig = _sigmoid(_dot(xc, wx) + bx)
    fill()
    log_a = (-LRU_C) * r * sp
    a = jnp.exp(log_a)
    m = jnp.sqrt(-jnp.tanh(log_a) * (a * a + 1.0))
    return r, ig, a, m


def _fused(parts, name, comm=None):
    grid = parts[0]["grid"]
    assert all(p["grid"] == grid for p in parts)
    counts = [(len(p["in_specs"]), len(p["out_specs"]), len(p.get("scratch_shapes", ()))) for p in parts]

    def body(*refs):
        refs = list(refs)
        groups = []
        for kind in range(3):
            taken = []
            for c in counts:
                taken.append(refs[:c[kind]])
                refs = refs[c[kind]:]
            groups.append(taken)
        ins, outs, scr = groups
        pending = []

        def fill(n=None):
            for _ in range(share if n is None else n):
                if pending:
                    pending.pop(0)()

        ctx = dict(outs=outs, scratch=scr, fill=fill)
        run = lambda key: [p[key](*ins[k], *outs[k], *scr[k], ctx) for k, p in enumerate(parts) if key in p]
        run("head")
        for pieces in run("units"):
            pending.extend(pieces)
        points = sum(p.get("fill_points", 0) for p in parts)
        share = -(-len(pending) // max(points, 1))
        run("body")
        fill(len(pending))
        run("tail")

    cat = lambda key: [x for p in parts for x in p.get(key, ())]
    res = _call(body, name=name, grid=grid, comm=comm, vmem_limit=FUSED_VMEM_LIMIT,
                in_specs=cat("in_specs"), out_specs=cat("out_specs"),
                out_shape=cat("out_shape"), scratch_shapes=cat("scratch_shapes"), operands=cat("operands"))
    outs, side = (res if comm is not None else (res, None))
    split, at = [], 0
    for _, n_out, _ in counts:
        split.append(list(outs[at:at + n_out]))
        at += n_out
    return split if comm is None else (split, side)


def _lru_fwd(proj, conv_w, conv_b, wa, ba, wx, bx, lam, gain):
    t = proj.shape[0]
    tm = min(MIX_ROW_TILE, t)
    c = D_LRU

    def body(x_ref, xh_ref, g_ref, cw_ref, cb_ref, wa_ref, ba_ref, wx_ref, bx_ref, lam_ref, gain_ref,
             xc_ref, h_ref, y_ref, a_scr, b_scr, carry, ctx):
        fill = ctx["fill"]
        i = pl.program_id(0)

        @pl.when(i == 0)
        def _():
            carry[...] = jnp.zeros_like(carry)

        fill()
        x = x_ref[...].astype(F32)
        prev = jnp.where(i == 0, 0.0, xh_ref[...].astype(F32)[-SUBLANES:, :])
        cw = cw_ref[...]
        xc = cb_ref[...] + cw[LRU_CONV - 1:LRU_CONV, :] * x
        for k in range(LRU_CONV - 1):
            xc = xc + cw[k:k + 1, :] * _shift_down(prev, x, LRU_CONV - 1 - k)
        xc_ref[...] = xc
        fill()
        sp = _softplus(-lam_ref[...])
        _, ig, a, m = _lru_gates(xc, wa_ref[...], ba_ref[...], wx_ref[...], bx_ref[...], sp, fill)
        fill()
        ga, gb = _group_scan(a, m * (ig * xc), reverse=False, fill=fill)
        a_scr[...] = ga
        b_scr[...] = gb
        fill()
        carry[...] = _carry_scan(a_scr, b_scr, h_ref, carry[...], reverse=False)
        fill()
        z = h_ref[...] * _gelu(g_ref[...].astype(F32))
        fill()
        _, _, y = _rms_fwd(z, gain_ref[...])
        y_ref[...] = y.astype(y_ref.dtype)

    row = lambda i: (i, 0)
    full = lambda i: (0, 0)
    vec = pl.BlockSpec((1, c), full)
    hb = _halo_rows(proj.dtype)
    return dict(body=body, grid=(t // tm,), fill_points=6,
                in_specs=[pl.BlockSpec((tm, c), row), pl.BlockSpec((hb, c), _halo_map(tm, 0, hb)),
                          pl.BlockSpec((tm, c), lambda i: (i, 1)),
                          pl.BlockSpec((LRU_CONV, c), full), vec, pl.BlockSpec((c, c), full), vec,
                          pl.BlockSpec((c, c), full), vec, vec, vec],
                out_specs=[pl.BlockSpec((tm, c), row), pl.BlockSpec((tm, c), row), pl.BlockSpec((tm, c), row)],
                out_shape=[jax.ShapeDtypeStruct((t, c), F32), jax.ShapeDtypeStruct((t, c), F32),
                           jax.ShapeDtypeStruct((t, c), MXU_DTYPE)],
                scratch_shapes=[pltpu.VMEM((tm, c), F32), pltpu.VMEM((tm, c), F32), pltpu.VMEM((SUBLANES, c), F32)],
                operands=[proj, proj, proj, conv_w, conv_b, wa, ba, wx, bx, lam, gain])


def _ret_consts():
    c = RET_CHUNK
    log_g = jnp.log1p(-jnp.exp2(-5.0 - jnp.arange(RET_HEADS, dtype=F32)))
    idx = jnp.arange(c, dtype=F32)
    diff = idx[:, None] - idx[None, :]
    decay = jnp.where(diff[None] >= 0, jnp.exp(jnp.maximum(diff, 0.0)[None] * log_g[:, None, None]), 0.0)
    zeta = jnp.exp((c - 1 - idx)[None, :] * log_g[:, None])
    xi = jnp.exp((idx + 1.0)[None, :] * log_g[:, None])
    spread = lambda v: jnp.repeat(v.T, RET_HEAD_DIM, axis=1)
    log_g_np = np.log1p(-np.exp2(-5.0 - np.arange(RET_HEADS, dtype=np.float32))).astype(np.float32)
    g_chunk = [float(np.exp(np.float32(c) * lg)) for lg in log_g_np]
    return decay, spread(xi), spread(zeta), g_chunk


def _rope_tables(t):
    pos = np.arange(t, dtype=np.float32)
    inv_freq = np.float32(ROPE_BASE) ** (-np.arange(0, RET_HEAD_DIM, 2, dtype=np.float32) / np.float32(RET_HEAD_DIM))
    ang = (pos[:, None] * inv_freq.astype(np.float32)[None, :]).astype(np.float32).astype(np.float64)
    cos, sin = np.cos(ang).astype(np.float32), np.sin(ang).astype(np.float32)
    return jnp.asarray(np.concatenate([cos, cos], axis=-1)), jnp.asarray(np.concatenate([-sin, sin], axis=-1))


def _rope(x, cos2, sin_signed):
    return x * cos2 + pltpu.roll(x, RET_HEAD_DIM // 2, 1) * sin_signed


def _rope_bwd(d, cos2, sin_signed):
    return d * cos2 + pltpu.roll(d * sin_signed, RET_HEAD_DIM // 2, 1)


RET_SCALE = RET_HEAD_DIM ** -0.5


RET_CHUNKS_PER_STEP = MIX_ROW_TILE // RET_CHUNK


def _ret_fwd(proj, cos2, sin_signed, gain):
    t = proj.shape[0]
    c, d, nh = RET_CHUNK, RET_HEAD_DIM, RET_HEADS
    n_chunks = t // c
    per = RET_CHUNKS_PER_STEP if n_chunks % RET_CHUNKS_PER_STEP == 0 else 1
    rows = per * c
    decay, xi, zeta, g_chunk = _ret_consts()

    def units(qk_ref, vg_ref, cos_ref, sin_ref, dec_ref, xi_ref, zeta_ref, gain_ref, o_ref, y_ref, st_ref, state, ctx):
        cur = [None] * nh

        def start():
            @pl.when(pl.program_id(0) == 0)
            def _():
                state[...] = jnp.zeros_like(state)
            for h in range(nh):
                cur[h] = state[h]

        def retain(s, h, keep):
            rs = slice(s * c, (s + 1) * c)
            cos2, sin_s = cos_ref[rs, :], sin_ref[rs, :]
            lo = h * d
            q = _rope(qk_ref[rs, lo:lo + d].astype(F32), cos2, sin_s)
            k = _rope(qk_ref[rs, D_RET + lo:D_RET + lo + d].astype(F32), cos2, sin_s) * RET_SCALE
            v = vg_ref[rs, lo:lo + d]
            s_prev = cur[h]
            st_ref[s, h] = s_prev
            scores = _dot_nt(q, k) * dec_ref[h]
            o = _dot(scores, v) + _dot(q * xi_ref[:, lo:lo + d], s_prev)
            cur[h] = s_prev * g_chunk[h] + _dot_tn(k * zeta_ref[:, lo:lo + d], v)
            o_ref[rs, lo:lo + d] = o
            keep["o"] = o

        def normalise(s, h, keep):
            rs = slice(s * c, (s + 1) * c)
            lo = h * d
            o = keep["o"]
            g = vg_ref[rs, D_RET + lo:D_RET + lo + d].astype(F32)
            mu = jnp.mean(o, axis=-1, keepdims=True)
            oc = o - mu
            on = oc * lax.rsqrt(jnp.mean(oc * oc, axis=-1, keepdims=True) + NORM_EPS)
            y_ref[rs, lo:lo + d] = (on * gain_ref[:, lo:lo + d] * (g * _sigmoid(g))).astype(y_ref.dtype)

        def end():
            for h in range(nh):
                state[h] = cur[h]

        pieces = [start]
        for s in range(per):
            for h in range(nh):
                keep = {}
                pieces += [lambda s=s, h=h, keep=keep: retain(s, h, keep),
                           lambda s=s, h=h, keep=keep: normalise(s, h, keep)]
        return pieces + [end]

    full2 = lambda i: (0, 0)
    return dict(units=units, grid=(n_chunks // per,),
                in_specs=[pl.BlockSpec((rows, 2 * D_RET), lambda i: (i, 1)),
                          pl.BlockSpec((rows, 2 * D_RET), lambda i: (i, 2)),
                          pl.BlockSpec((rows, d), lambda i: (i, 0)), pl.BlockSpec((rows, d), lambda i: (i, 0)),
                          pl.BlockSpec((nh, c, c), lambda i: (0, 0, 0)), pl.BlockSpec((c, D_RET), full2),
                          pl.BlockSpec((c, D_RET), full2), pl.BlockSpec((1, D_RET), full2)],
                out_specs=[pl.BlockSpec((rows, D_RET), lambda i: (i, 0)), pl.BlockSpec((rows, D_RET), lambda i: (i, 0)),
                           pl.BlockSpec((per, nh, d, d), lambda i: (i, 0, 0, 0))],
                out_shape=[jax.ShapeDtypeStruct((t, D_RET), F32), jax.ShapeDtypeStruct((t, D_RET), MXU_DTYPE),
                           jax.ShapeDtypeStruct((n_chunks, nh, d, d), F32)],
                scratch_shapes=[pltpu.VMEM((nh, d, d), F32)],
                operands=[proj, proj, cos2, sin_signed, decay, xi, zeta, gain])


def _outproj_fwd(x, y_lru, y_ret, w_out, g2, comm):
    t = x.shape[0]
    tm = min(PROJ_ROW_TILE, t)

    def body(x_ref, yl_ref, yr_ref, w_ref, g_ref, h1_ref, u2_ref):
        h1 = x_ref[...] + _dot(yl_ref[...], w_ref[:D_LRU, :]) + _dot(yr_ref[...], w_ref[D_LRU:, :])
        h1_ref[...] = h1
        _, _, u = _rms_fwd(h1, g_ref[...])
        u2_ref[...] = u.astype(u2_ref.dtype)

    row = lambda i: (i, 0)
    return _call(body, name="outproj_fwd", grid=(t // tm,), comm=comm,
                 in_specs=[pl.BlockSpec((tm, D_MODEL), row), pl.BlockSpec((tm, D_LRU), row), pl.BlockSpec((tm, D_RET), row),
                           _resident((D_MODEL, D_MODEL)), pl.BlockSpec((1, D_MODEL), lambda i: (0, 0))],
                 out_specs=[pl.BlockSpec((tm, D_MODEL), row), pl.BlockSpec((tm, D_MODEL), row)],
                 out_shape=[jax.ShapeDtypeStruct((t, D_MODEL), F32), jax.ShapeDtypeStruct((t, D_MODEL), MXU_DTYPE)],
                 operands=[x, y_lru, y_ret, w_out, g2])


FFN_TN = 768
FFN_NJ = D_FF // FFN_TN
FFN_GROUP = 4


def _ffn_fwd(u2, w_blocks, conv_w, conv_b, w_down, h1, gf, target):
    t = u2.shape[0]
    tm = min(ROW_TILE, t)
    tn, nj, group = FFN_TN, FFN_NJ, FFN_GROUP
    ng, tw = nj // group, group * tn
    hb = _halo_rows(u2.dtype)
    assert w_blocks.shape == (2 * nj, D_MODEL, tn)

    def conv(ext, col, up_ref, conv_ref, cw_ref, cb_ref, first):
        x = ext[hb:, :]
        up_ref[:, col] = x.astype(up_ref.dtype)
        prev = jnp.where(first, 0.0, ext[hb - SUBLANES:hb, :])
        cw = cw_ref[:, col]
        y = cb_ref[:, col] + cw[FFN_CONV - 1:FFN_CONV, :] * x
        for k in range(FFN_CONV - 1):
            y = y + cw[k:k + 1, :] * _shift_down(prev, x, FFN_CONV - 1 - k)
        conv_ref[:, col] = y.astype(conv_ref.dtype)
        return y

    def body(u_ref, uh_ref, w_ref, cwa_ref, cwv_ref, cba_ref, cbv_ref, wd_ref, h1_ref, gf_ref, tg_ref,
             upa_ref, upv_ref, ca_ref, cv_ref, act_ref, dh_ref, dhb_ref, dgf_ref, loss_ref, acc):
        i, jg = pl.program_id(0), pl.program_id(1)

        @pl.when((i == 0) & (jg == 0))
        def _():
            dgf_ref[...] = jnp.zeros_like(dgf_ref)
            loss_ref[...] = jnp.zeros_like(loss_ref)

        @pl.when(jg == 0)
        def _():
            acc[...] = jnp.zeros_like(acc)

        u_ext = jnp.concatenate([uh_ref[...], u_ref[...]], axis=0)

        def project(jj):
            j = jg * group + jj
            return _dot(u_ext, w_ref[j]), _dot(u_ext, w_ref[nj + j])

        down, ahead = None, project(0)
        for jj in range(group):
            col = slice(jj * tn, (jj + 1) * tn)
            j = jg * group + jj
            ext_a, ext_v = ahead
            if jj + 1 < group:
                ahead = project(jj + 1)
            a = conv(ext_a, col, upa_ref, ca_ref, cwa_ref, cba_ref, i == 0)
            v = conv(ext_v, col, upv_ref, cv_ref, cwv_ref, cbv_ref, i == 0)
            act = (_gelu(a) * v).astype(act_ref.dtype)
            act_ref[:, col] = act
            part = _dot(act, wd_ref[pl.ds(pl.multiple_of(j * tn, tn), tn), :])
            down = part if down is None else down + part
        acc[...] += down

        @pl.when(jg == ng - 1)
        def _():
            n, rstd, y = _rms_fwd(h1_ref[...] + acc[...], gf_ref[...])
            err = y - tg_ref[...]
            loss_ref[...] += (0.5 / D_MODEL) * jnp.sum(err * err)
            dh, dgf = _rms_bwd(err * (1.0 / D_MODEL), n, rstd, gf_ref[...])
            dgf_ref[...] += dgf
            dh_ref[...] = dh
            dhb_ref[...] = dh.astype(dhb_ref.dtype)

    per = tm // hb
    row = lambda i, j: (i, 0)
    const = lambda i, j: (0, 0)
    tile = pl.BlockSpec((tm, tw), lambda i, j: (i, j))
    return _call(body, name="ffn_fwd", grid=(t // tm, ng), vmem_limit=FUSED_VMEM_LIMIT,
                 in_specs=[pl.BlockSpec((tm, D_MODEL), row),
                           pl.BlockSpec((hb, D_MODEL), lambda i, j: (jnp.maximum(i * per - 1, 0), 0)),
                           _resident(w_blocks.shape),
                           pl.BlockSpec((FFN_CONV, tw), lambda i, j: (0, j)),
                           pl.BlockSpec((FFN_CONV, tw), lambda i, j: (0, j + ng)),
                           pl.BlockSpec((1, tw), lambda i, j: (0, j)), pl.BlockSpec((1, tw), lambda i, j: (0, j + ng)),
                           _resident((D_FF, D_MODEL)),
                           pl.BlockSpec((tm, D_MODEL), row), pl.BlockSpec((1, D_MODEL), const),
                           pl.BlockSpec((tm, D_MODEL), row)],
                 out_specs=[tile] * 5 + [pl.BlockSpec((tm, D_MODEL), row),
                            pl.BlockSpec((tm, D_MODEL), row), pl.BlockSpec((SUBLANES, D_MODEL), const),
                            pl.BlockSpec((SUBLANES, LANES), const)],
                 out_shape=[jax.ShapeDtypeStruct((t, D_FF), MXU_DTYPE)] * 5 + [
                            jax.ShapeDtypeStruct((t, D_MODEL), F32),
                            jax.ShapeDtypeStruct((t, D_MODEL), MXU_DTYPE), jax.ShapeDtypeStruct((SUBLANES, D_MODEL), F32),
                            jax.ShapeDtypeStruct((SUBLANES, LANES), F32)],
                 scratch_shapes=[pltpu.VMEM((tm, D_MODEL), F32)],
                 operands=[u2, u2, w_blocks, conv_w, conv_w, conv_b, conv_b, w_down, h1, gf, target])


FFN_ACC_ROWS = SUBLANES * (FFN_CONV + 1)


def _ffn_bwd(dh2, dh2_b, w_down, up_a, up_v, conv_a, conv_v, conv_w, w_up_blocks, h1, g2, comm):
    t = up_a.shape[0]
    tm = min(ROW_TILE, t)
    tn, nj, group = FFN_TN, FFN_NJ, FFN_GROUP
    ng, tw = nj // group, group * tn
    ni = t // tm
    assert w_up_blocks.shape == (2 * nj, D_MODEL, tn)

    def conv_bwd(dy, x, cw, acc_ref, carry_ref, dup_ref, col):
        nxt = carry_ref[...]
        carry_ref[...] = dy[:SUBLANES, :]
        ahead = [_shift_up(dy, nxt, FFN_CONV - 1 - k) for k in range(FFN_CONV)]
        dx = cw[FFN_CONV - 1:FFN_CONV, :] * dy
        for k in range(FFN_CONV - 1):
            dx = dx + cw[k:k + 1, :] * ahead[k]
        dx = dx.astype(dup_ref.dtype)
        dup_ref[:, col] = dx
        for k in range(FFN_CONV):
            acc_ref[k * SUBLANES:(k + 1) * SUBLANES, :] += _colsum8(ahead[k] * x)
        acc_ref[FFN_CONV * SUBLANES:, :] += _colsum8(dy)
        return dx

    def body(dh_ref, dhb_ref, wd_ref, ua_ref, uv_ref, ca_ref, cv_ref, cwa_ref, cwv_ref, wu_ref, h1_ref, g2_ref,
             dua_ref, duv_ref, acca_ref, accv_ref, dh1_ref, dh1b_ref, dg2_ref, carry_a, carry_v, du):
        i, jg = pl.program_id(0), pl.program_id(1)

        @pl.when((i == 0) & (jg == 0))
        def _():
            for ref in (acca_ref, accv_ref, carry_a, carry_v, dg2_ref):
                ref[...] = jnp.zeros_like(ref)

        dhb = dhb_ref[...]

        def through_down(jj):
            j = jg * group + jj
            return _dot_nt(dhb, wd_ref[pl.ds(pl.multiple_of(j * tn, tn), tn), :])

        part, ahead = None, through_down(0)
        for jj in range(group):
            col = slice(jj * tn, (jj + 1) * tn)
            j = jg * group + jj
            dact = ahead
            if jj + 1 < group:
                ahead = through_down(jj + 1)
            v = cv_ref[:, col].astype(F32)
            g, dg = _gelu_parts(ca_ref[:, col].astype(F32))
            da = conv_bwd(dact * v * dg, ua_ref[:, col].astype(F32), cwa_ref[:, col], acca_ref.at[j], carry_a.at[j],
                          dua_ref, col)
            dv = conv_bwd(dact * g, uv_ref[:, col].astype(F32), cwv_ref[:, col], accv_ref.at[j], carry_v.at[j],
                          duv_ref, col)
            term = _dot_nt(da, wu_ref[j]) + _dot_nt(dv, wu_ref[nj + j])
            part = term if part is None else part + term

        @pl.when(jg == 0)
        def _():
            du[...] = part

        @pl.when(jg > 0)
        def _():
            du[...] += part

        @pl.when(jg == ng - 1)
        def _():
            n, rstd, _ = _rms_fwd(h1_ref[...], g2_ref[...])
            dh1, dg2 = _rms_bwd(du[...], n, rstd, g2_ref[...])
            dh1 = dh1 + dh_ref[...]
            dg2_ref[...] += dg2
            dh1_ref[...] = dh1
            dh1b_ref[...] = dh1.astype(dh1b_ref.dtype)

    row = lambda i, j: (ni - 1 - i, 0)
    const = lambda i, j: (0, 0)
    tile = pl.BlockSpec((tm, tw), lambda i, j: (ni - 1 - i, j))
    acc = pl.BlockSpec((nj, FFN_ACC_ROWS, tn), lambda i, j: (0, 0, 0))
    return _call(body, name="ffn_bwd", grid=(ni, ng), comm=comm, vmem_limit=FFN_BWD_VMEM_LIMIT,
                 in_specs=[pl.BlockSpec((tm, D_MODEL), row), pl.BlockSpec((tm, D_MODEL), row),
                           _resident((D_FF, D_MODEL)), tile, tile, tile, tile,
                           pl.BlockSpec((FFN_CONV, tw), lambda i, j: (0, j)),
                           pl.BlockSpec((FFN_CONV, tw), lambda i, j: (0, j + ng)),
                           _resident(w_up_blocks.shape), pl.BlockSpec((tm, D_MODEL), row),
                           pl.BlockSpec((1, D_MODEL), const)],
                 out_specs=[tile, tile, acc, acc, pl.BlockSpec((tm, D_MODEL), row), pl.BlockSpec((tm, D_MODEL), row),
                            pl.BlockSpec((SUBLANES, D_MODEL), const)],
                 out_shape=[jax.ShapeDtypeStruct((t, D_FF), MXU_DTYPE), jax.ShapeDtypeStruct((t, D_FF), MXU_DTYPE),
                            jax.ShapeDtypeStruct((nj, FFN_ACC_ROWS, tn), F32),
                            jax.ShapeDtypeStruct((nj, FFN_ACC_ROWS, tn), F32),
                            jax.ShapeDtypeStruct((t, D_MODEL), F32), jax.ShapeDtypeStruct((t, D_MODEL), MXU_DTYPE),
                            jax.ShapeDtypeStruct((SUBLANES, D_MODEL), F32)],
                 scratch_shapes=[pltpu.VMEM((nj, SUBLANES, tn), F32), pltpu.VMEM((nj, SUBLANES, tn), F32),
                                 pltpu.VMEM((tm, D_MODEL), F32)],
                 operands=[dh2, dh2_b, w_down, up_a, up_v, conv_a, conv_v, conv_w, conv_w, w_up_blocks, h1, g2])


def _ret_bwd(proj, cos2, sin_signed, gain, o, states, dmix_at):
    t = proj.shape[0]
    c, d, nh = RET_CHUNK, RET_HEAD_DIM, RET_HEADS
    n_chunks = t // c
    per = RET_CHUNKS_PER_STEP if n_chunks % RET_CHUNKS_PER_STEP == 0 else 1
    rows = per * c
    n_steps = n_chunks // per
    decay, xi, zeta, g_chunk = _ret_consts()
    base = 2 * D_LRU

    def units(qk_ref, vg_ref, cos_ref, sin_ref, dec_ref, xi_ref, zeta_ref, gain_ref, o_ref, st_ref,
              dp_ref, dgain_ref, gstate, ctx):
        cur = [None] * nh
        dmix = ctx["scratch"][dmix_at[0]][dmix_at[1]]

        def start():
            @pl.when(pl.program_id(0) == 0)
            def _():
                gstate[...] = jnp.zeros_like(gstate)
                dgain_ref[...] = jnp.zeros_like(dgain_ref)
            for h in range(nh):
                cur[h] = gstate[h]

        def gate_and_norm(s, h, keep):
            rs = slice(s * c, (s + 1) * c)
            lo = h * d
            g = vg_ref[rs, D_RET + lo:D_RET + lo + d].astype(F32)
            gain_h = gain_ref[:, lo:lo + d]
            dy = dmix[rs, D_LRU + lo:D_LRU + lo + d]
            sg = _sigmoid(g)
            o_h = o_ref[rs, lo:lo + d]
            oc = o_h - jnp.mean(o_h, axis=-1, keepdims=True)
            rstd = lax.rsqrt(jnp.mean(oc * oc, axis=-1, keepdims=True) + NORM_EPS)
            on = oc * rstd
            at = base + 3 * D_RET + lo
            dp_ref[rs, at:at + d] = (dy * on * gain_h * (sg * (1.0 + g * (1.0 - sg)))).astype(dp_ref.dtype)
            don_g = dy * (g * sg)
            dgain_ref[:, lo:lo + d] += _colsum8(don_g * on)
            don = don_g * gain_h
            keep["do"] = rstd * (don - jnp.mean(don, axis=-1, keepdims=True)
                                 - on * jnp.mean(don * on, axis=-1, keepdims=True))

        def retain(s, h, keep):
            rs = slice(s * c, (s + 1) * c)
            cos2, sin_s = cos_ref[rs, :], sin_ref[rs, :]
            lo = h * d
            q = _rope(qk_ref[rs, lo:lo + d].astype(F32), cos2, sin_s)
            k = _rope(qk_ref[rs, D_RET + lo:D_RET + lo + d].astype(F32), cos2, sin_s) * RET_SCALE
            v = vg_ref[rs, lo:lo + d]
            xi_h, zeta_h, dec = xi_ref[:, lo:lo + d], zeta_ref[:, lo:lo + d], dec_ref[h]
            do = keep["do"]
            s_prev = st_ref[s, h]
            g_next = cur[h]
            p = _dot_nt(q, k) * dec
            dpm = _dot_nt(do, v) * dec
            keep["dq"] = _dot(dpm, k) + _dot_nt(do, s_prev) * xi_h
            keep["dk"] = _dot_tn(dpm, q) + _dot_nt(v, g_next) * zeta_h
            dv = _dot_tn(p, do) + _dot(k * zeta_h, g_next)
            cur[h] = g_next * g_chunk[h] + _dot_tn(q * xi_h, do)
            at = base + 2 * D_RET + lo
            dp_ref[rs, at:at + d] = dv.astype(dp_ref.dtype)

        def unrope(s, h, keep):
            rs = slice(s * c, (s + 1) * c)
            cos2, sin_s = cos_ref[rs, :], sin_ref[rs, :]
            lo = h * d
            dp_ref[rs, base + lo:base + lo + d] = _rope_bwd(keep["dq"], cos2, sin_s).astype(dp_ref.dtype)
            at = base + D_RET + lo
            dp_ref[rs, at:at + d] = _rope_bwd(keep["dk"] * RET_SCALE, cos2, sin_s).astype(dp_ref.dtype)

        def end():
            for h in range(nh):
                gstate[h] = cur[h]

        pieces = [start]
        for s in reversed(range(per)):
            for h in range(nh):
                keep = {}
                pieces += [lambda s=s, h=h, keep=keep, f=f: f(s, h, keep) for f in (gate_and_norm, retain, unrope)]
        return pieces + [end]

    rev = lambda col: (lambda i: (n_steps - 1 - i, col))
    full2 = lambda i: (0, 0)
    return dict(units=units, grid=(n_steps,),
                in_specs=[pl.BlockSpec((rows, 2 * D_RET), rev(1)), pl.BlockSpec((rows, 2 * D_RET), rev(2)),
                          pl.BlockSpec((rows, d), rev(0)), pl.BlockSpec((rows, d), rev(0)),
                          pl.BlockSpec((nh, c, c), lambda i: (0, 0, 0)), pl.BlockSpec((c, D_RET), full2),
                          pl.BlockSpec((c, D_RET), full2), pl.BlockSpec((1, D_RET), full2),
                          pl.BlockSpec((rows, D_RET), rev(0)),
                          pl.BlockSpec((per, nh, d, d), lambda i: (n_steps - 1 - i, 0, 0, 0))],
                out_specs=[pl.BlockSpec((rows, D_IN), rev(0)), pl.BlockSpec((SUBLANES, D_RET), full2)],
                out_shape=[jax.ShapeDtypeStruct((t, D_IN), MXU_DTYPE), jax.ShapeDtypeStruct((SUBLANES, D_RET), F32)],
                scratch_shapes=[pltpu.VMEM((nh, d, d), F32)],
                operands=[proj, proj, cos2, sin_signed, decay, xi, zeta, gain, o, states])


LRU_ACC = {"conv_w": 0, "conv_b": LRU_CONV, "gate_a_b": LRU_CONV + 1, "gate_x_b": LRU_CONV + 2,
           "lambda": LRU_CONV + 3, "norm_gain": LRU_CONV + 4}
LRU_ACC_ROWS = SUBLANES * (LRU_CONV + 5)


def _lru_bwd(proj, xc_all, h_all, conv_w, wa, ba, wx, bx, lam, gain, dproj_part, dmix_at):
    t = proj.shape[0]
    tm = min(MIX_ROW_TILE, t)
    c = D_LRU
    ni = t // tm

    def body(x_ref, xh_ref, g_ref, xc_ref, h_ref, hh_ref, cw_ref, wa_ref, ba_ref, wx_ref, bx_ref, lam_ref,
             gain_ref, acc_ref, dwa_ref, dwx_ref, a_scr, b_scr, mu_scr, carry_mu, carry_dxc, ctx):
        dp_ref = ctx["outs"][dproj_part][0]
        dmix = ctx["scratch"][dmix_at[0]][dmix_at[1]]
        fill = ctx["fill"]
        i = pl.program_id(0)
        r = ni - 1 - i

        @pl.when(i == 0)
        def _():
            acc_ref[...] = jnp.zeros_like(acc_ref)
            dwa_ref[...] = jnp.zeros_like(dwa_ref)
            dwx_ref[...] = jnp.zeros_like(dwx_ref)
            carry_mu[...] = jnp.zeros_like(carry_mu)
            carry_dxc[...] = jnp.zeros_like(carry_dxc)

        def add(name, val, k=0):
            lo = (LRU_ACC[name] + k) * SUBLANES
            acc_ref[lo:lo + SUBLANES, :] += _colsum8(val)

        fill()
        xc, h = xc_ref[...], h_ref[...]
        lam_v = lam_ref[...]
        sp = _softplus(-lam_v)
        rg, ig, a, m = _lru_gates(xc, wa_ref[...], ba_ref[...], wx_ref[...], bx_ref[...], sp, fill)
        gl, dgl = _gelu_parts(g_ref[...].astype(F32))
        fill()
        zn, rstd, _ = _rms_fwd(h * gl, gain_ref[...])
        dy = dmix[:, :c]
        dz, dgain = _rms_bwd(dy, zn, rstd, gain_ref[...])
        lo = LRU_ACC["norm_gain"] * SUBLANES
        acc_ref[lo:lo + SUBLANES, :] += dgain
        dp_ref[:, c:2 * c] = (dz * h * dgl).astype(dp_ref.dtype)
        dh = dz * gl
        fill()
        ga, gb = _group_scan(a, a * dh, reverse=True, fill=fill)
        a_scr[...] = ga
        b_scr[...] = gb
        mu_next_tile = carry_mu[...]
        carry_mu[...] = _carry_scan(a_scr, b_scr, mu_scr, mu_next_tile, reverse=True)
        fill()
        lam_t = dh + _shift_up(mu_scr[...], mu_next_tile, 1)
        h_prev = _shift_down(jnp.where(r == 0, 0.0, hh_ref[...]), h, 1)
        da = lam_t * h_prev
        dig = lam_t * m * xc
        dxc = lam_t * m * ig
        dlog_a = da * a - (lam_t * ig * xc) * (a * a) / m
        fill()
        dpr = dlog_a * ((-LRU_C) * sp) * rg * (1.0 - rg)
        add("lambda", dlog_a * ((-LRU_C) * rg) * (-_sigmoid(-lam_v)))
        dpi = dig * ig * (1.0 - ig)
        add("gate_a_b", dpr)
        add("gate_x_b", dpi)
        fill()
        dwa_ref[...] += _dot_tn(xc, dpr)
        dwx_ref[...] += _dot_tn(xc, dpi)
        dxc = dxc + _dot_nt(dpr, wa_ref[...]) + _dot_nt(dpi, wx_ref[...])
        fill()
        add("conv_b", dxc)
        x = x_ref[...].astype(F32)
        prev = jnp.where(r == 0, 0.0, xh_ref[...].astype(F32)[-SUBLANES:, :])
        cw = cw_ref[...]
        nxt = carry_dxc[...]
        carry_dxc[...] = dxc[:SUBLANES, :]
        dx = cw[LRU_CONV - 1:LRU_CONV, :] * dxc
        for k in range(LRU_CONV - 1):
            dx = dx + cw[k:k + 1, :] * _shift_up(dxc, nxt, LRU_CONV - 1 - k)
        fill()
        for k in range(LRU_CONV):
            add("conv_w", dxc * _shift_down(prev, x, LRU_CONV - 1 - k), k)
        dp_ref[:, :c] = dx.astype(dp_ref.dtype)

    hb = _halo_rows(proj.dtype)
    rev = lambda col: (lambda i: (ni - 1 - i, col))
    halo = lambda rows: (lambda i: (jnp.maximum((ni - 1 - i) * (tm // rows) - 1, 0), 0))
    full = lambda i: (0, 0)
    vec = pl.BlockSpec((1, c), full)
    mat = pl.BlockSpec((c, c), full)
    return dict(body=body, grid=(ni,), fill_points=16,
                in_specs=[pl.BlockSpec((tm, c), rev(0)), pl.BlockSpec((hb, c), halo(hb)), pl.BlockSpec((tm, c), rev(1)),
                          pl.BlockSpec((tm, c), rev(0)), pl.BlockSpec((tm, c), rev(0)),
                          pl.BlockSpec((SUBLANES, c), halo(SUBLANES)),
                          pl.BlockSpec((LRU_CONV, c), full), mat, vec, mat, vec, vec, vec],
                out_specs=[pl.BlockSpec((LRU_ACC_ROWS, c), full), mat, mat],
                out_shape=[jax.ShapeDtypeStruct((LRU_ACC_ROWS, c), F32), jax.ShapeDtypeStruct((c, c), F32),
                           jax.ShapeDtypeStruct((c, c), F32)],
                scratch_shapes=[pltpu.VMEM((tm, c), F32), pltpu.VMEM((tm, c), F32), pltpu.VMEM((tm, c), F32),
                                pltpu.VMEM((SUBLANES, c), F32), pltpu.VMEM((SUBLANES, c), F32)],
                operands=[proj, proj, proj, xc_all, h_all, h_all, conv_w, wa, ba, wx, bx, lam, gain])


def _mix_proj_bwd(dh1, dh1_b, w_out, w_in_blocks, x, g1, dproj_part):
    t = x.shape[0]
    tm = min(MIX_ROW_TILE, t)
    ni = t // tm
    nb, _, cb = w_in_blocks.shape
    first_free = -(-2 * D_LRU // cb)
    du = [None]

    def term(dp_ref, w_ref, d):
        part = _dot_nt(dp_ref[:, d * cb:(d + 1) * cb], w_ref[d])
        du[0] = part if du[0] is None else du[0] + part

    def head(dh_ref, dhb_ref, wo_ref, wi_ref, x_ref, g_ref, gx_ref, dg_ref, dmix, ctx):
        @pl.when(pl.program_id(0) == 0)
        def _():
            dg_ref[...] = jnp.zeros_like(dg_ref)
        dmix[...] = _dot_nt(dhb_ref[...], wo_ref[...])
        du[0] = None

    def units(dh_ref, dhb_ref, wo_ref, wi_ref, x_ref, g_ref, gx_ref, dg_ref, dmix, ctx):
        dp_ref = ctx["outs"][dproj_part][0]
        return [lambda d=d: term(dp_ref, wi_ref, d) for d in range(first_free, nb)]

    def tail(dh_ref, dhb_ref, wo_ref, wi_ref, x_ref, g_ref, gx_ref, dg_ref, dmix, ctx):
        dp_ref = ctx["outs"][dproj_part][0]
        for d in range(first_free):
            term(dp_ref, wi_ref, d)
        n, rstd, _ = _rms_fwd(x_ref[...], g_ref[...])
        dx, dg = _rms_bwd(du[0], n, rstd, g_ref[...])
        dg_ref[...] += dg
        gx_ref[...] = dx + dh_ref[...]

    row = lambda i: (ni - 1 - i, 0)
    const = lambda i: (0, 0)
    tile = pl.BlockSpec((tm, D_MODEL), row)
    return dict(head=head, units=units, tail=tail, grid=(ni,),
                in_specs=[tile, tile, _resident(w_out.shape), _resident(w_in_blocks.shape), tile,
                          pl.BlockSpec((1, D_MODEL), const)],
                out_specs=[tile, pl.BlockSpec((SUBLANES, D_MODEL), const)],
                out_shape=[jax.ShapeDtypeStruct((t, D_MODEL), F32), jax.ShapeDtypeStruct((SUBLANES, D_MODEL), F32)],
                scratch_shapes=[pltpu.VMEM((tm, D_MODEL), F32)],
                operands=[dh1, dh1_b, w_out, w_in_blocks, x, g1])


def _pair_sum(core, a, b, name):
    n, r, c = b.shape
    spec = pl.BlockSpec((None, r, c), lambda q, core: (q, 0, 0))

    def body(core_ref, a_ref, b_ref, o_ref):
        o_ref[...] = (a_ref[...].astype(F32) + b_ref[...].astype(F32)).astype(o_ref.dtype)

    return pl.pallas_call(
        body, name=name,
        grid_spec=pltpu.PrefetchScalarGridSpec(
            num_scalar_prefetch=1, grid=(n,),
            in_specs=[pl.BlockSpec((None, r, c), lambda q, core: (2 * q + core[0], 0, 0)), spec], out_specs=spec),
        out_shape=jax.ShapeDtypeStruct(b.shape, b.dtype),
        compiler_params=pltpu.CompilerParams(dimension_semantics=("arbitrary",), vmem_limit_bytes=VMEM_LIMIT),
    )(core, a, b)


ADAMW_BLOCK_BYTES = 4 * 1024 * 1024


def _sum_adamw(parts, w, m, v, name):
    n_parts, r, c = parts.shape
    tr = r
    while n_parts * tr * c * parts.dtype.itemsize > ADAMW_BLOCK_BYTES and tr % (4 * SUBLANES) == 0:
        tr //= 2

    def body(p_ref, w_ref, m_ref, v_ref, g_ref, d_ref, nm_ref, nv_ref):
        g = p_ref[0].astype(F32)
        for s in range(1, n_parts):
            g = g + p_ref[s].astype(F32)
        nm = ADAM_B1 * m_ref[...] + (1.0 - ADAM_B1) * g
        nv = ADAM_B2 * v_ref[...] + (1.0 - ADAM_B2) * (g * g)
        m_hat = nm / (1.0 - ADAM_B1 ** ADAM_STEP)
        v_hat = nv / (1.0 - ADAM_B2 ** ADAM_STEP)
        g_ref[...] = g
        d_ref[...] = -ADAM_LR * (m_hat / (jnp.sqrt(v_hat) + ADAM_EPS) + ADAM_WD * w_ref[...])
        nm_ref[...] = nm
        nv_ref[...] = nv

    row = pl.BlockSpec((tr, c), lambda i: (i, 0))
    return _call(body, name=name, grid=(r // tr,),
                 in_specs=[pl.BlockSpec((n_parts, tr, c), lambda i: (0, i, 0)), row, row, row],
                 out_specs=[row, row, row, row], out_shape=[jax.ShapeDtypeStruct((r, c), F32)] * 4,
                 operands=[parts, w, m, v])


MATRICES = ("w_in", "w_out", "ffn_up_w", "ffn_down_w")
CONVS = ("lru_conv_w", "ffn_conv_w")
REPLICATED = ("norm1_gain", "lru_conv_b", "lru_gate_a_w", "lru_gate_a_b", "lru_gate_x_w", "lru_gate_x_b", "lru_lambda",
              "lru_norm_gain", "ret_norm_gain", "norm2_gain", "ffn_conv_b", "final_norm_gain")
WEIGHTS = ("norm1_gain", "w_in", "lru_conv_w", "lru_conv_b", "lru_gate_a_w", "lru_gate_a_b", "lru_gate_x_w",
           "lru_gate_x_b", "lru_lambda", "lru_norm_gain", "ret_norm_gain", "w_out", "norm2_gain", "ffn_up_w",
           "ffn_conv_w", "ffn_conv_b", "ffn_down_w", "final_norm_gain")


def _rows(a, pad_to):
    a = a.reshape(-1, LANES)
    pad = (-a.shape[0]) % pad_to
    return jnp.pad(a, ((0, pad), (0, 0))) if pad else a


def _pack(arrays, pad_to):
    rows, layout, at = [], [], 0
    for a in arrays:
        r = _rows(a, pad_to)
        layout.append((at, a.size // LANES, a.shape))
        rows.append(r)
        at += r.shape[0]
    return jnp.concatenate(rows, axis=0), layout


def _unpack(packed, layout):
    lead = packed.shape[:-2]
    return [packed[..., at:at + n, :].reshape(lead + shape) for at, n, shape in layout]


def _conv_rows(lru, ffn, dtype, pad_to):
    lead = lru.shape[:-2]
    flat = jnp.concatenate([lru.reshape(lead + (-1,)), ffn.reshape(lead + (-1,))], axis=-1).astype(dtype)
    rows = flat.shape[-1] // LANES
    pad = (-rows) % pad_to
    return jnp.pad(flat.reshape(lead + (rows, LANES)), [(0, 0)] * len(lead) + [(0, pad), (0, 0)])


def _column_blocks(full):
    r, c = full.shape
    return full.reshape(r, N_DEV, c // N_DEV).transpose(1, 0, 2)


def _block_diag(w):
    nh, d, _ = w.shape
    eye = jnp.eye(nh, dtype=w.dtype)
    return (w[:, :, None, :] * eye[:, None, :, None]).reshape(nh * d, nh * d)


def _diag_blocks(dense, nh):
    d = dense.shape[0] // nh
    blocks = dense.reshape(nh, d, nh, d)
    return jnp.stack([blocks[h, :, h, :] for h in range(nh)], axis=0)


def kernel(x, norm1_gain, w_in, lru_conv_w, lru_conv_b, lru_gate_a_w, lru_gate_a_b, lru_gate_x_w, lru_gate_x_b, lru_lambda, lru_norm_gain, ret_norm_gain, w_out, norm2_gain, ffn_up_w, ffn_conv_w, ffn_conv_b, ffn_down_w, final_norm_gain, loss_target, m_norm1_gain, m_w_in, m_lru_conv_w, m_lru_conv_b, m_lru_gate_a_w, m_lru_gate_a_b, m_lru_gate_x_w, m_lru_gate_x_b, m_lru_lambda, m_lru_norm_gain, m_ret_norm_gain, m_w_out, m_norm2_gain, m_ffn_up_w, m_ffn_conv_w, m_ffn_conv_b, m_ffn_down_w, m_final_norm_gain, v_norm1_gain, v_w_in, v_lru_conv_w, v_lru_conv_b, v_lru_gate_a_w, v_lru_gate_a_b, v_lru_gate_x_w, v_lru_gate_x_b, v_lru_lambda, v_lru_norm_gain, v_ret_norm_gain, v_w_out, v_norm2_gain, v_ffn_up_w, v_ffn_conv_w, v_ffn_conv_b, v_ffn_down_w, v_final_norm_gain):
    args = dict(locals())
    given = {n: args[n] for n in WEIGHTS}
    out_shape = {n: given[n].shape for n in WEIGHTS}

    def plain(a):
        return a.reshape(1, -1) if a.ndim <= 2 else a[0]

    w = {n: plain(given[n]) for n in WEIGHTS}
    mom_m = {n: plain(args["m_" + n]) for n in WEIGHTS}
    mom_v = {n: plain(args["v_" + n]) for n in WEIGHTS}
    x2, target = x[0], loss_target[0]
    t = x2.shape[0]
    core = lax.axis_index("c").astype(jnp.int32).reshape(1)
    res = {}

    conv_pad = _conv_rows(w["lru_conv_w"], w["ffn_conv_w"], F32, SUBLANES)
    first = _gather_first([w["w_in"].astype(MXU_DTYPE), conv_pad])
    w_in_blocks, conv_all = _run_comms([first, _gather_second(first.out_shape)], "w_in_all_gather")
    n_lru = w["lru_conv_w"].size
    conv_flat = conv_all.reshape(N_DEV, -1)
    lru_cw = conv_flat[:, :n_lru].reshape((N_DEV,) + w["lru_conv_w"].shape).transpose(1, 0, 2).reshape(LRU_CONV, D_LRU)
    ffn_cw = conv_flat[:, n_lru:n_lru + w["ffn_conv_w"].size].reshape((N_DEV,) + w["ffn_conv_w"].shape)
    ffn_cw = ffn_cw.transpose(1, 0, 2).reshape(FFN_CONV, 2 * D_FF)

    cos2, sin_signed = _rope_tables(t)
    wa = _block_diag(w["lru_gate_a_w"]).astype(MXU_DTYPE)
    wx = _block_diag(w["lru_gate_x_w"]).astype(MXU_DTYPE)
    gf = w["final_norm_gain"]

    early = _gather_first([w["w_out"].astype(MXU_DTYPE), w["ffn_down_w"].astype(MXU_DTYPE)])
    (u1, proj), (w_out_part, down_part) = _inproj_fwd(x2, w["norm1_gain"], w_in_blocks, early)
    ((xc, h_lru, y_lru), (o_ret, y_ret, states)), (w_out_blocks, down_blocks, up_part) = _fused(
        [_lru_fwd(proj, lru_cw, w["lru_conv_b"], wa, w["lru_gate_a_b"], wx, w["lru_gate_x_b"], w["lru_lambda"],
                  w["lru_norm_gain"]),
         _ret_fwd(proj, cos2, sin_signed, w["ret_norm_gain"])],
        "mix_fwd", _both(_gather_second([w_out_part, down_part]), _gather_first([w["ffn_up_w"].astype(MXU_DTYPE)])))
    w_out_full = w_out_blocks.reshape(D_MODEL, D_MODEL)
    w_down_full = down_blocks.reshape(D_FF, D_MODEL)

    (h1, u2), (up_blocks,) = _outproj_fwd(x2, y_lru, y_ret, w_out_full, w["norm2_gain"], _gather_second([up_part]))
    up_a, up_v, conv_a, conv_v, act, dh2, dh2_b, dgf, loss_local = _ffn_fwd(u2, up_blocks, ffn_cw, w["ffn_conv_b"],
                                                                            w_down_full, h1, gf, target)

    def to_owner_chips(blocks, names, tag):
        theirs = _run_comms([_pair_exchange(blocks)], "grads_pair_exchange_" + tag)
        return [_pair_sum(core, a, b, "grads_pair_sum_" + n) for n, a, b in zip(names, blocks, theirs)]

    def adamw(name, parts):
        res[name] = _sum_adamw(parts, w[name], mom_m[name], mom_v[name], "adamw_" + name)

    g = {"final_norm_gain": dgf[0]}
    dup_a, dup_v, acc_a, acc_v, dh1, dh1_b, dg2 = _ffn_bwd(
        dh2, dh2_b, w_down_full, up_a, up_v, conv_a, conv_v, ffn_cw, up_blocks, h1, w["norm2_gain"], None)
    per_col = lambda a: a[:, ::SUBLANES].transpose(1, 0, 2).reshape(FFN_CONV + 1, D_FF)
    acc = jnp.concatenate([per_col(acc_a), per_col(acc_v)], axis=1)
    g_ffn_cw, g["ffn_conv_b"] = acc[:FFN_CONV], acc[FFN_CONV:]
    g["norm2_gain"] = dg2[:1]
    g_up = _mm_tn(u2, dup_a, "ffn_up_wgrad_a", blocks=N_DEV // 2, room=N_DEV)
    g_up = _mm_tn(u2, dup_v, "ffn_up_wgrad_v", blocks=N_DEV // 2, before=g_up)
    g_out_lru, (up_theirs,) = _mm_tn(y_lru, dh1_b, "w_out_wgrad_lru", comm=_pair_exchange([g_up]))
    up_sums = [_pair_sum(core, g_up, up_theirs, "grads_pair_sum_ffn_up_w")]
    g_down, (up_parts,) = _mm_tn(act, dh2_b, "ffn_down_wgrad", comm=_chip_exchange(up_sums))
    adamw("ffn_up_w", up_parts)
    g_out = jnp.concatenate([g_out_lru, _mm_tn(y_ret, dh1_b, "w_out_wgrad_ret")], axis=0)
    low_sums = to_owner_chips([g_down.reshape(N_DEV, D_FF // N_DEV, D_MODEL),
                               g_out.reshape(N_DEV, D_MODEL // N_DEV, D_MODEL)], ["ffn_down_w", "w_out"], "low")
    (dproj, dgain_ret), (grad_x, dg1), (lru_acc, dwa, dwx) = _fused(
        [_ret_bwd(proj, cos2, sin_signed, w["ret_norm_gain"], o_ret, states, dmix_at=(1, 0)),
         _mix_proj_bwd(dh1, dh1_b, w_out_full, w_in_blocks, x2, w["norm1_gain"], dproj_part=0),
         _lru_bwd(proj, xc, h_lru, lru_cw, wa, w["lru_gate_a_b"], wx, w["lru_gate_x_b"], w["lru_lambda"],
                  w["lru_norm_gain"], dproj_part=0, dmix_at=(1, 0))],
        "mix_bwd")
    g["norm1_gain"] = dg1[:1]
    g["ret_norm_gain"] = dgain_ret[:1]
    lru_acc = lru_acc[::SUBLANES]
    g_lru_cw = lru_acc[:LRU_CONV]
    for name in ("conv_b", "gate_a_b", "gate_x_b", "lambda", "norm_gain"):
        g["lru_" + name] = lru_acc[LRU_ACC[name]:LRU_ACC[name] + 1]
    g["lru_gate_a_w"] = _diag_blocks(dwa, LRU_HEADS)
    g["lru_gate_x_w"] = _diag_blocks(dwx, LRU_HEADS)
    rep_packed, rep_layout = _pack([g[n] for n in REPLICATED] + [loss_local], SUBLANES)
    g_in, (down_parts, out_parts, rep_part) = _mm_tn(u1, dproj, "w_in_wgrad", blocks=N_DEV,
                                                     comm=_both(_chip_exchange(low_sums), _gather_first([rep_packed])))
    adamw("ffn_down_w", down_parts)
    adamw("w_out", out_parts)
    g_conv = _conv_rows(_column_blocks(g_lru_cw), _column_blocks(g_ffn_cw), GRAD_DTYPE, 2 * SUBLANES)
    in_sums = to_owner_chips([g_in, g_conv], ["w_in", "conv"], "in")
    in_parts, conv_parts, rep_parts = _run_comms([_both(_chip_exchange(in_sums), _gather_second([rep_part]))],
                                                 "last_grads_exchange")
    adamw("w_in", in_parts)
    pad16 = lambda d: _conv_rows(d["lru_conv_w"], d["ffn_conv_w"], F32, 2 * SUBLANES)
    conv_res = _sum_adamw(conv_parts, pad16(w), pad16(mom_m), pad16(mom_v), "adamw_conv")
    for n, lo, hi in (("lru_conv_w", 0, n_lru), ("ffn_conv_w", n_lru, n_lru + w["ffn_conv_w"].size)):
        res[n] = [r.reshape(-1)[lo:hi].reshape(w[n].shape) for r in conv_res]
    no_state = jnp.zeros_like(loss_local)
    rep_res = _sum_adamw(rep_parts, *[_pack([d[n] for n in REPLICATED] + [no_state], SUBLANES)[0]
                                      for d in (w, mom_m, mom_v)], "adamw_replicated")
    for k in range(4):
        for n, a in zip(REPLICATED, _unpack(rep_res[k], rep_layout)):
            res.setdefault(n, [None] * 4)[k] = a
    loss = _unpack(rep_res[0], rep_layout)[-1][0, 0]

    outs = [loss, grad_x[None]]
    for k in range(4):
        outs += [res[n][k].reshape(out_shape[n]) for n in WEIGHTS]
    return tuple(outs)
```

```python
import math

import numpy as np
import jax
import jax.numpy as jnp
from jax import lax
from jax.experimental import pallas as pl
from jax.experimental.pallas import tpu as pltpu

F32 = jnp.float32
BF16 = jnp.bfloat16
MXU_DTYPE = jnp.bfloat16
GRAD_DTYPE = jnp.bfloat16

N_DEV = 8
N_CHIPS = 4
D_MODEL = 1024
D_LRU = 512
LRU_HEADS = 8
LRU_CONV = 4
LRU_C = 8.0
D_RET = 512
RET_HEADS = 4
RET_HEAD_DIM = 128
RET_CHUNK = 128
ROPE_BASE = 10000.0
D_IN = 3072
D_FF = 3072
FFN_CONV = 3
NORM_EPS = 1e-6

ADAM_LR = 0.001
ADAM_B1 = 0.9
ADAM_B2 = 0.999
ADAM_EPS = 1e-08
ADAM_WD = 0.01
ADAM_STEP = 10

SUBLANES = 8
LANES = 128
VMEM_LIMIT = 48 * 1024 * 1024
FUSED_VMEM_LIMIT = VMEM_LIMIT
FFN_BWD_VMEM_LIMIT = 56 * 1024 * 1024

ROW_TILE = 256
MIX_ROW_TILE = 256
PROJ_ROW_TILE = 512
WGRAD_ROWS = 2048
WGRAD_TILE = 1024
WGRAD_BLOCK_COLUMNS = 768
WGRAD_SLOTS = 3

MESH = pl.DeviceIdType.MESH
ANY = pl.BlockSpec(memory_space=pl.ANY)


def _dot(a, b):
    return jnp.dot(a.astype(MXU_DTYPE), b.astype(MXU_DTYPE), preferred_element_type=F32)


def _dot_nt(a, b):
    return lax.dot_general(a.astype(MXU_DTYPE), b.astype(MXU_DTYPE), (((1,), (1,)), ((), ())),
                           preferred_element_type=F32)


def _dot_tn(a, b):
    return lax.dot_general(a.astype(MXU_DTYPE), b.astype(MXU_DTYPE), (((0,), (0,)), ((), ())),
                           preferred_element_type=F32)


def _sigmoid(x):
    return 0.5 + 0.5 * jnp.tanh(0.5 * x)


_GELU_C = math.sqrt(2.0 / math.pi)
_GELU_C3 = _GELU_C * 0.044715


def _gelu_parts(x):
    x2 = x * x
    t = jnp.tanh(x * (_GELU_C + _GELU_C3 * x2))
    cdf = 0.5 + 0.5 * t
    g = x * cdf
    dg = cdf + (0.5 * x) * (1.0 - t * t) * (_GELU_C + (3.0 * _GELU_C3) * x2)
    return g, dg


def _gelu(x):
    t = jnp.tanh(_GELU_C * (x + 0.044715 * (x * x * x)))
    return x * (0.5 * (1.0 + t))


def _softplus(x):
    return jnp.maximum(x, 0.0) + jnp.log1p(jnp.exp(-jnp.abs(x)))


def _bcast_row(x, r, rows=SUBLANES):
    return jnp.broadcast_to(x[r:r + 1, :], (rows, x.shape[1]))


def _colsum8(x):
    return jnp.broadcast_to(jnp.sum(x, axis=0, keepdims=True), (SUBLANES, x.shape[1]))


def _groups(x):
    return x.reshape(x.shape[0] // SUBLANES, SUBLANES, x.shape[1])


def _shift_down(prev8, tile, s):
    if s == 0:
        return tile
    own = pltpu.roll(_groups(tile), s, 1)
    before = jnp.concatenate([pltpu.roll(_groups(prev8), s, 1), own[:-1]], axis=0)
    row = lax.broadcasted_iota(jnp.int32, own.shape, 1)
    return jnp.where(row >= s, own, before).reshape(tile.shape)


def _shift_up(tile, next8, s):
    if s == 0:
        return tile
    own = pltpu.roll(_groups(tile), SUBLANES - s, 1)
    after = jnp.concatenate([own[1:], pltpu.roll(_groups(next8), SUBLANES - s, 1)], axis=0)
    row = lax.broadcasted_iota(jnp.int32, own.shape, 1)
    return jnp.where(row < SUBLANES - s, own, after).reshape(tile.shape)


def _group_scan(a, b, reverse, fill=lambda: None):
    n, c = a.shape
    row = lax.broadcasted_iota(jnp.int32, a.shape, 0) & (SUBLANES - 1)

    def within_group(x, shift):
        return pltpu.roll(x.reshape(n // SUBLANES, SUBLANES, c), shift, 1).reshape(n, c)

    for s in (1, 2, 4):
        if s > 1:
            fill()
        shift = (SUBLANES - s) if reverse else s
        a_sh = within_group(a, shift)
        b_sh = within_group(b, shift)
        m = (row <= SUBLANES - 1 - s) if reverse else (row >= s)
        b = jnp.where(m, a * b_sh + b, b)
        a = jnp.where(m, a * a_sh, a)
    return a, b


def _carry_scan(a_ref, b_ref, out_ref, carry0, reverse):
    n_groups = a_ref.shape[0] // SUBLANES
    carry = carry0
    for i in range(n_groups):
        r0 = ((n_groups - 1 - i) if reverse else i) * SUBLANES
        hg = a_ref[r0:r0 + SUBLANES, :] * carry + b_ref[r0:r0 + SUBLANES, :]
        out_ref[r0:r0 + SUBLANES, :] = hg
        carry = _bcast_row(hg, 0 if reverse else SUBLANES - 1)
    return carry


def _rms_fwd(h, gain):
    rstd = lax.rsqrt(jnp.mean(h * h, axis=-1, keepdims=True) + NORM_EPS)
    n = h * rstd
    return n, rstd, n * gain


def _rms_bwd(dy, n, rstd, gain):
    dn = dy * gain
    dh = rstd * (dn - n * jnp.mean(dn * n, axis=-1, keepdims=True))
    return dh, _colsum8(dy * n)


def _halo_rows(dtype):
    return SUBLANES * (4 // jnp.dtype(dtype).itemsize)


def _halo_map(tile_rows, col, halo_rows=SUBLANES):
    per = tile_rows // halo_rows
    return lambda i: (jnp.maximum(i * per - 1, 0), col)


def _resident(shape):
    return pl.BlockSpec(shape, lambda *_: (0,) * len(shape), pipeline_mode=pl.Buffered(1))


def _place():
    x, y, c = lax.axis_index("x"), lax.axis_index("y"), lax.axis_index("c")
    chips = [(1 - x, y), (x, 1 - y), (1 - x, 1 - y)]
    return x, y, c, chips


def _dev(x, y, c):
    return 4 * x + 2 * y + c


class _Copy:
    def __init__(self, make):
        self.make = make

    def start(self):
        self.make().start()

    def wait(self):
        self.make().wait()

    def wait_send(self):
        self.make().wait_send()

    def wait_recv(self):
        self.make().wait_recv()


def _remote(src, dst, send_sem, recv_sem, to):
    return _Copy(lambda: pltpu.make_async_remote_copy(src_ref=src, dst_ref=dst, send_sem=send_sem, recv_sem=recv_sem,
                                                      device_id=to, device_id_type=MESH))


def _local(src, dst, sem):
    return _Copy(lambda: pltpu.make_async_copy(src, dst, sem))


class _Comm:
    def __init__(self, operands, out_shape, sems, descs, aliases=()):
        self.operands, self.out_shape, self.sems, self.descs, self.aliases = operands, out_shape, sems, descs, aliases

    def start(self, ins, outs, sems):
        local, sends, _ = self.descs(ins, outs, sems)
        for cp in sends + local:
            cp.start()

    def wait(self, ins, outs, sems):
        local, sends, recvs = self.descs(ins, outs, sems)
        for cp in recvs:
            cp.wait_recv()
        for cp in sends:
            cp.wait_send()
        for cp in local:
            cp.wait()


def _gather_first(shards):
    n = len(shards)

    def descs(ins, outs, sems):
        send, recv, loc = sems
        x, y, c, chips = _place()
        me = _dev(x, y, c)
        targets = [(x, y, 1 - c)] + [(*chip, c) for chip in chips]
        local, sends, recvs = [], [], []
        for t in range(n):
            local.append(_local(ins[t], outs[t].at[me], loc.at[t]))
            for k, to in enumerate(targets):
                i = 4 * t + k
                sends.append(_remote(ins[t], outs[t].at[me], send.at[i], recv.at[i], to))
                recvs.append(_remote(ins[t], outs[t].at[_dev(*to)], send.at[i], recv.at[i], to))
        return local, sends, recvs

    return _Comm(list(shards), [jax.ShapeDtypeStruct((N_DEV,) + s.shape, s.dtype) for s in shards],
                 [pltpu.SemaphoreType.DMA((4 * n,)), pltpu.SemaphoreType.DMA((4 * n,)), pltpu.SemaphoreType.DMA((n,))],
                 descs)


def _gather_second(gathered):
    n = len(gathered)

    def descs(ins, outs, sems):
        send, recv = sems
        x, y, c, chips = _place()
        sends, recvs = [], []
        for t in range(n):
            for j, chip in enumerate(chips):
                i = 3 * t + j
                have, get = _dev(*chip, c), _dev(*chip, 1 - c)
                sends.append(_remote(outs[t].at[have], outs[t].at[have], send.at[i], recv.at[i], (x, y, 1 - c)))
                recvs.append(_remote(outs[t].at[have], outs[t].at[get], send.at[i], recv.at[i], (x, y, 1 - c)))
        return [], sends, recvs

    return _Comm(list(gathered), [jax.ShapeDtypeStruct(g.shape, g.dtype) for g in gathered],
                 [pltpu.SemaphoreType.DMA((3 * n,)), pltpu.SemaphoreType.DMA((3 * n,))], descs,
                 aliases=[(t, t) for t in range(n)])


def _pair_exchange(blocks):
    n = len(blocks)

    def descs(ins, outs, sems):
        send, recv = sems
        x, y, c, _ = _place()
        sends, recvs = [], []
        for t in range(n):
            for q in range(N_CHIPS):
                i = N_CHIPS * t + q
                cp = _remote(ins[t].at[2 * q + 1 - c], outs[t].at[q], send.at[i], recv.at[i], (x, y, 1 - c))
                sends.append(cp)
                recvs.append(cp)
        return [], sends, recvs

    return _Comm(list(blocks), [jax.ShapeDtypeStruct((N_CHIPS,) + b.shape[1:], b.dtype) for b in blocks],
                 [pltpu.SemaphoreType.DMA((N_CHIPS * n,)), pltpu.SemaphoreType.DMA((N_CHIPS * n,))], descs)


def _chip_exchange(blocks):
    n = len(blocks)

    def descs(ins, outs, sems):
        send, recv, loc = sems
        x, y, c, chips = _place()
        me = 2 * x + y
        local, sends, recvs = [], [], []
        for t in range(n):
            local.append(_local(ins[t].at[me], outs[t].at[me], loc.at[t]))
            for j, (px, py) in enumerate(chips):
                i = 3 * t + j
                q = 2 * px + py
                sends.append(_remote(ins[t].at[q], outs[t].at[me], send.at[i], recv.at[i], (px, py, c)))
                recvs.append(_remote(ins[t].at[q], outs[t].at[q], send.at[i], recv.at[i], (px, py, c)))
        return local, sends, recvs

    return _Comm(list(blocks), [jax.ShapeDtypeStruct(b.shape, b.dtype) for b in blocks],
                 [pltpu.SemaphoreType.DMA((3 * n,)), pltpu.SemaphoreType.DMA((3 * n,)), pltpu.SemaphoreType.DMA((n,))],
                 descs)


def _both(a, b):
    na, oa, sa = len(a.operands), len(a.out_shape), len(a.sems)

    def descs(ins, outs, sems):
        local_a, sends_a, recvs_a = a.descs(ins[:na], outs[:oa], sems[:sa])
        local_b, sends_b, recvs_b = b.descs(ins[na:], outs[oa:], sems[sa:])
        return local_a + local_b, sends_a + sends_b, recvs_a + recvs_b

    return _Comm(a.operands + b.operands, a.out_shape + b.out_shape, a.sems + b.sems, descs,
                 aliases=list(a.aliases) + [(na + i, oa + o) for i, o in b.aliases])


def _run_comms(comms, name):
    first = comms[0]
    n_in, n_out = len(first.operands), len(first.out_shape)

    def body(*refs):
        ins, outs, sems = refs[:n_in], refs[n_in:n_in + n_out], list(refs[n_in + n_out:])
        for k, comm in enumerate(comms):
            mine = [sems.pop(0) for _ in comm.sems]
            comm.start(ins if k == 0 else outs, outs, mine)
            comm.wait(ins if k == 0 else outs, outs, mine)

    outs = pl.pallas_call(
        body, name=name, out_shape=first.out_shape, in_specs=[ANY] * n_in, out_specs=[ANY] * n_out,
        scratch_shapes=[s for comm in comms for s in comm.sems], input_output_aliases=dict(first.aliases),
    )(*first.operands)
    return list(outs)


def _call(body, *, name, grid, in_specs, out_specs, out_shape, operands, scratch_shapes=(), comm=None, aliases=None,
          vmem_limit=VMEM_LIMIT, body_first=False):
    sem = ("arbitrary",) * len(grid)
    params = pltpu.CompilerParams(dimension_semantics=sem, vmem_limit_bytes=vmem_limit)
    aliases = dict(aliases or {})
    if comm is None:
        return pl.pallas_call(body, name=name, grid=grid, in_specs=in_specs, out_specs=out_specs, out_shape=out_shape,
                              scratch_shapes=list(scratch_shapes), input_output_aliases=aliases,
                              compiler_params=params)(*operands)
    n_in, n_out, n_scr = len(in_specs), len(out_specs), len(scratch_shapes)
    c_in, c_out = len(comm.operands), len(comm.out_shape)

    def wrapped(*refs):
        refs = list(refs)
        ins, refs = refs[:n_in], refs[n_in:]
        cins, refs = refs[:c_in], refs[c_in:]
        outs, refs = refs[:n_out], refs[n_out:]
        couts, refs = refs[:c_out], refs[c_out:]
        scr, csems = refs[:n_scr], refs[n_scr:]
        first = last = None
        for axis, size in enumerate(grid):
            at_first, at_last = pl.program_id(axis) == 0, pl.program_id(axis) == size - 1
            first = at_first if first is None else first & at_first
            last = at_last if last is None else last & at_last

        def start_comm():
            pl.when(first)(lambda: comm.start(cins, couts, csems))

        if not body_first:
            start_comm()
        body(*ins, *outs, *scr)
        if body_first:
            start_comm()

        @pl.when(last)
        def _():
            comm.wait(cins, couts, csems)

    res = pl.pallas_call(
        wrapped, name=name, grid=grid, in_specs=list(in_specs) + [ANY] * c_in, out_specs=list(out_specs) + [ANY] * c_out,
        out_shape=list(out_shape) + list(comm.out_shape), scratch_shapes=list(scratch_shapes) + list(comm.sems),
        input_output_aliases={**aliases, **{n_in + i: n_out + o for i, o in comm.aliases}}, compiler_params=params,
    )(*operands, *comm.operands)
    return list(res[:n_out]), list(res[n_out:])


def _mm_tn(a, b, name, blocks=1, comm=None, room=None, before=None):
    t, m = a.shape
    n = b.shape[1]
    tk = min(WGRAD_ROWS, t)
    nk = t // tk
    cb = n // blocks
    per = max(1, WGRAD_BLOCK_COLUMNS // cb) if blocks > 1 else 1
    tn = per * cb if blocks > 1 else min(WGRAD_TILE, n)
    tm = min(WGRAD_TILE, m)
    assert blocks == 1 or tm == m
    total = blocks if before is None and room is None else (room if before is None else before.shape[0])
    first = (total - blocks) // per if before is not None else 0
    assert blocks > 1 or total == 1

    ni, nj = m // tm, n // tn
    steps = ni * nj * nk

    def body(*refs):
        (a_hbm, b_hbm), (o_ref, acc, a_buf, b_buf, sems) = refs[:2], refs[-5:]
        k = pl.program_id(2)
        step = (pl.program_id(0) * nj + pl.program_id(1)) * nk + k

        def fetch(s):
            slot = s % WGRAD_SLOTS
            rows = pl.ds((s % nk) * tk, tk)
            return (pltpu.make_async_copy(a_hbm.at[rows, pl.ds((s // (nk * nj)) * tm, tm)], a_buf.at[slot], sems.at[0, slot]),
                    pltpu.make_async_copy(b_hbm.at[rows, pl.ds(((s // nk) % nj) * tn, tn)], b_buf.at[slot], sems.at[1, slot]))

        def start(s):
            for copy in fetch(s):
                copy.start()

        @pl.when(step == 0)
        def _():
            for s in range(min(WGRAD_SLOTS - 1, steps)):
                start(s)

        @pl.when(step + WGRAD_SLOTS - 1 < steps)
        def _():
            start(step + WGRAD_SLOTS - 1)

        for copy in fetch(step):
            copy.wait()

        @pl.when(k == 0)
        def _():
            acc[...] = jnp.zeros_like(acc)
        slot = step % WGRAD_SLOTS
        acc[...] += _dot_tn(a_buf[slot], b_buf[slot])

        @pl.when(k == nk - 1)
        def _():
            if blocks == 1:
                o_ref[...] = acc[...].astype(o_ref.dtype)
            else:
                for s in range(per):
                    o_ref[s] = acc[:, s * cb:(s + 1) * cb].astype(o_ref.dtype)

    if blocks == 1:
        out_spec = pl.BlockSpec((tm, tn), lambda i, j, k: (i, j))
        out_shape = jax.ShapeDtypeStruct((m, n), GRAD_DTYPE)
    else:
        out_spec = pl.BlockSpec((per, m, cb), lambda i, j, k: (first + j, 0, 0))
        out_shape = jax.ShapeDtypeStruct((total, m, cb), GRAD_DTYPE)
    given = [] if before is None else [before]
    res = _call(body, name=name, grid=(ni, nj, nk), comm=comm, in_specs=[ANY] * (2 + len(given)),
                out_specs=[out_spec], out_shape=[out_shape], operands=[a, b] + given,
                aliases={2: 0} if given else None, body_first=True,
                scratch_shapes=[pltpu.VMEM((tm, tn), F32), pltpu.VMEM((WGRAD_SLOTS, tk, tm), a.dtype),
                                pltpu.VMEM((WGRAD_SLOTS, tk, tn), b.dtype), pltpu.SemaphoreType.DMA((2, WGRAD_SLOTS))])
    return res[0] if comm is None else (res[0][0], res[1])


INPROJ_TN = 1024


def _inproj_fwd(x, g1, w_blocks, comm):
    t = x.shape[0]
    tm = min(PROJ_ROW_TILE, t)
    nb, _, cb = w_blocks.shape

    def body(x_ref, g_ref, w_hbm, u_ref, p_ref, w_all, sems):
        @pl.when(pl.program_id(0) == 0)
        def _():
            copies = [pltpu.make_async_copy(w_hbm.at[d], w_all.at[:, pl.ds(d * cb, cb)], sems.at[d]) for d in range(nb)]
            for cp in copies:
                cp.start()
            for cp in copies:
                cp.wait()

        _, _, u = _rms_fwd(x_ref[...], g_ref[...])
        u = u.astype(MXU_DTYPE)
        u_ref[...] = u
        for lo in range(0, D_IN, INPROJ_TN):
            p_ref[:, lo:lo + INPROJ_TN] = _dot(u, w_all[:, lo:lo + INPROJ_TN]).astype(p_ref.dtype)

    return _call(body, name="inproj_fwd", grid=(t // tm,), comm=comm,
                 in_specs=[pl.BlockSpec((tm, D_MODEL), lambda i: (i, 0)), pl.BlockSpec((1, D_MODEL), lambda i: (0, 0)), ANY],
                 out_specs=[pl.BlockSpec((tm, D_MODEL), lambda i: (i, 0)), pl.BlockSpec((tm, D_IN), lambda i: (i, 0))],
                 out_shape=[jax.ShapeDtypeStruct((t, D_MODEL), MXU_DTYPE), jax.ShapeDtypeStruct((t, D_IN), MXU_DTYPE)],
                 scratch_shapes=[pltpu.VMEM((D_MODEL, nb * cb), w_blocks.dtype), pltpu.SemaphoreType.DMA((nb,))],
                 operands=[x, g1, w_blocks])


def _lru_gates(xc, wa, ba, wx, bx, sp, fill=lambda: None):
    r = _sigmoid(_dot(xc, wa) + ba)
    fill()
    ig = _sigmoid(_dot(xc, wx) + bx)
    fill()
    log_a = (-LRU_C) * r * sp
    a = jnp.exp(log_a)
    m = jnp.sqrt(-jnp.tanh(log_a) * (a * a + 1.0))
    return r, ig, a, m


def _fused(parts, name, comm=None):
    grid = parts[0]["grid"]
    assert all(p["grid"] == grid for p in parts)
    counts = [(len(p["in_specs"]), len(p["out_specs"]), len(p.get("scratch_shapes", ()))) for p in parts]

    def body(*refs):
        refs = list(refs)
        groups = []
        for kind in range(3):
            taken = []
            for c in counts:
                taken.append(refs[:c[kind]])
                refs = refs[c[kind]:]
            groups.append(taken)
        ins, outs, scr = groups
        pending = []

        def fill(n=None):
            for _ in range(share if n is None else n):
                if pending:
                    pending.pop(0)()

        ctx = dict(outs=outs, scratch=scr, fill=fill)
        run = lambda key: [p[key](*ins[k], *outs[k], *scr[k], ctx) for k, p in enumerate(parts) if key in p]
        run("head")
        for pieces in run("units"):
            pending.extend(pieces)
        points = sum(p.get("fill_points", 0) for p in parts)
        share = -(-len(pending) // max(points, 1))
        run("body")
        fill(len(pending))
        run("tail")

    cat = lambda key: [x for p in parts for x in p.get(key, ())]
    res = _call(body, name=name, grid=grid, comm=comm, vmem_limit=FUSED_VMEM_LIMIT,
                in_specs=cat("in_specs"), out_specs=cat("out_specs"),
                out_shape=cat("out_shape"), scratch_shapes=cat("scratch_shapes"), operands=cat("operands"))
    outs, side = (res if comm is not None else (res, None))
    split, at = [], 0
    for _, n_out, _ in counts:
        split.append(list(outs[at:at + n_out]))
        at += n_out
    return split if comm is None else (split, side)


def _lru_fwd(proj, conv_w, conv_b, wa, ba, wx, bx, lam, gain):
    t = proj.shape[0]
    tm = min(MIX_ROW_TILE, t)
    c = D_LRU

    def body(x_ref, xh_ref, g_ref, cw_ref, cb_ref, wa_ref, ba_ref, wx_ref, bx_ref, lam_ref, gain_ref,
             xc_ref, h_ref, y_ref, a_scr, b_scr, carry, ctx):
        fill = ctx["fill"]
        i = pl.program_id(0)

        @pl.when(i == 0)
        def _():
            carry[...] = jnp.zeros_like(carry)

        fill()
        x = x_ref[...].astype(F32)
        prev = jnp.where(i == 0, 0.0, xh_ref[...].astype(F32)[-SUBLANES:, :])
        cw = cw_ref[...]
        xc = cb_ref[...] + cw[LRU_CONV - 1:LRU_CONV, :] * x
        for k in range(LRU_CONV - 1):
            xc = xc + cw[k:k + 1, :] * _shift_down(prev, x, LRU_CONV - 1 - k)
        xc_ref[...] = xc
        fill()
        sp = _softplus(-lam_ref[...])
        _, ig, a, m = _lru_gates(xc, wa_ref[...], ba_ref[...], wx_ref[...], bx_ref[...], sp, fill)
        fill()
        ga, gb = _group_scan(a, m * (ig * xc), reverse=False, fill=fill)
        a_scr[...] = ga
        b_scr[...] = gb
        fill()
        carry[...] = _carry_scan(a_scr, b_scr, h_ref, carry[...], reverse=False)
        fill()
        z = h_ref[...] * _gelu(g_ref[...].astype(F32))
        fill()
        _, _, y = _rms_fwd(z, gain_ref[...])
        y_ref[...] = y.astype(y_ref.dtype)

    row = lambda i: (i, 0)
    full = lambda i: (0, 0)
    vec = pl.BlockSpec((1, c), full)
    hb = _halo_rows(proj.dtype)
    return dict(body=body, grid=(t // tm,), fill_points=6,
                in_specs=[pl.BlockSpec((tm, c), row), pl.BlockSpec((hb, c), _halo_map(tm, 0, hb)),
                          pl.BlockSpec((tm, c), lambda i: (i, 1)),
                          pl.BlockSpec((LRU_CONV, c), full), vec, pl.BlockSpec((c, c), full), vec,
                          pl.BlockSpec((c, c), full), vec, vec, vec],
                out_specs=[pl.BlockSpec((tm, c), row), pl.BlockSpec((tm, c), row), pl.BlockSpec((tm, c), row)],
                out_shape=[jax.ShapeDtypeStruct((t, c), F32), jax.ShapeDtypeStruct((t, c), F32),
                           jax.ShapeDtypeStruct((t, c), MXU_DTYPE)],
                scratch_shapes=[pltpu.VMEM((tm, c), F32), pltpu.VMEM((tm, c), F32), pltpu.VMEM((SUBLANES, c), F32)],
                operands=[proj, proj, proj, conv_w, conv_b, wa, ba, wx, bx, lam, gain])


def _ret_consts():
    c = RET_CHUNK
    log_g = jnp.log1p(-jnp.exp2(-5.0 - jnp.arange(RET_HEADS, dtype=F32)))
    idx = jnp.arange(c, dtype=F32)
    diff = idx[:, None] - idx[None, :]
    decay = jnp.where(diff[None] >= 0, jnp.exp(jnp.maximum(diff, 0.0)[None] * log_g[:, None, None]), 0.0)
    zeta = jnp.exp((c - 1 - idx)[None, :] * log_g[:, None])
    xi = jnp.exp((idx + 1.0)[None, :] * log_g[:, None])
    spread = lambda v: jnp.repeat(v.T, RET_HEAD_DIM, axis=1)
    log_g_np = np.log1p(-np.exp2(-5.0 - np.arange(RET_HEADS, dtype=np.float32))).astype(np.float32)
    g_chunk = [float(np.exp(np.float32(c) * lg)) for lg in log_g_np]
    return decay, spread(xi), spread(zeta), g_chunk


def _rope_tables(t):
    pos = np.arange(t, dtype=np.float32)
    inv_freq = np.float32(ROPE_BASE) ** (-np.arange(0, RET_HEAD_DIM, 2, dtype=np.float32) / np.float32(RET_HEAD_DIM))
    ang = (pos[:, None] * inv_freq.astype(np.float32)[None, :]).astype(np.float32).astype(np.float64)
    cos, sin = np.cos(ang).astype(np.float32), np.sin(ang).astype(np.float32)
    return jnp.asarray(np.concatenate([cos, cos], axis=-1)), jnp.asarray(np.concatenate([-sin, sin], axis=-1))


def _rope(x, cos2, sin_signed):
    return x * cos2 + pltpu.roll(x, RET_HEAD_DIM // 2, 1) * sin_signed


def _rope_bwd(d, cos2, sin_signed):
    return d * cos2 + pltpu.roll(d * sin_signed, RET_HEAD_DIM // 2, 1)


RET_SCALE = RET_HEAD_DIM ** -0.5


RET_CHUNKS_PER_STEP = MIX_ROW_TILE // RET_CHUNK


def _ret_fwd(proj, cos2, sin_signed, gain):
    t = proj.shape[0]
    c, d, nh = RET_CHUNK, RET_HEAD_DIM, RET_HEADS
    n_chunks = t // c
    per = RET_CHUNKS_PER_STEP if n_chunks % RET_CHUNKS_PER_STEP == 0 else 1
    rows = per * c
    decay, xi, zeta, g_chunk = _ret_consts()

    def units(qk_ref, vg_ref, cos_ref, sin_ref, dec_ref, xi_ref, zeta_ref, gain_ref, o_ref, y_ref, st_ref, state, ctx):
        cur = [None] * nh

        def start():
            @pl.when(pl.program_id(0) == 0)
            def _():
                state[...] = jnp.zeros_like(state)
            for h in range(nh):
                cur[h] = state[h]

        def retain(s, h, keep):
            rs = slice(s * c, (s + 1) * c)
            cos2, sin_s = cos_ref[rs, :], sin_ref[rs, :]
            lo = h * d
            q = _rope(qk_ref[rs, lo:lo + d].astype(F32), cos2, sin_s)
            k = _rope(qk_ref[rs, D_RET + lo:D_RET + lo + d].astype(F32), cos2, sin_s) * RET_SCALE
            v = vg_ref[rs, lo:lo + d]
            s_prev = cur[h]
            st_ref[s, h] = s_prev
            scores = _dot_nt(q, k) * dec_ref[h]
            o = _dot(scores, v) + _dot(q * xi_ref[:, lo:lo + d], s_prev)
            cur[h] = s_prev * g_chunk[h] + _dot_tn(k * zeta_ref[:, lo:lo + d], v)
            o_ref[rs, lo:lo + d] = o
            keep["o"] = o

        def normalise(s, h, keep):
            rs = slice(s * c, (s + 1) * c)
            lo = h * d
            o = keep["o"]
            g = vg_ref[rs, D_RET + lo:D_RET + lo + d].astype(F32)
            mu = jnp.mean(o, axis=-1, keepdims=True)
            oc = o - mu
            on = oc * lax.rsqrt(jnp.mean(oc * oc, axis=-1, keepdims=True) + NORM_EPS)
            y_ref[rs, lo:lo + d] = (on * gain_ref[:, lo:lo + d] * (g * _sigmoid(g))).astype(y_ref.dtype)

        def end():
            for h in range(nh):
                state[h] = cur[h]

        pieces = [start]
        for s in range(per):
            for h in range(nh):
                keep = {}
                pieces += [lambda s=s, h=h, keep=keep: retain(s, h, keep),
                           lambda s=s, h=h, keep=keep: normalise(s, h, keep)]
        return pieces + [end]

    full2 = lambda i: (0, 0)
    return dict(units=units, grid=(n_chunks // per,),
                in_specs=[pl.BlockSpec((rows, 2 * D_RET), lambda i: (i, 1)),
                          pl.BlockSpec((rows, 2 * D_RET), lambda i: (i, 2)),
                          pl.BlockSpec((rows, d), lambda i: (i, 0)), pl.BlockSpec((rows, d), lambda i: (i, 0)),
                          pl.BlockSpec((nh, c, c), lambda i: (0, 0, 0)), pl.BlockSpec((c, D_RET), full2),
                          pl.BlockSpec((c, D_RET), full2), pl.BlockSpec((1, D_RET), full2)],
                out_specs=[pl.BlockSpec((rows, D_RET), lambda i: (i, 0)), pl.BlockSpec((rows, D_RET), lambda i: (i, 0)),
                           pl.BlockSpec((per, nh, d, d), lambda i: (i, 0, 0, 0))],
                out_shape=[jax.ShapeDtypeStruct((t, D_RET), F32), jax.ShapeDtypeStruct((t, D_RET), MXU_DTYPE),
                           jax.ShapeDtypeStruct((n_chunks, nh, d, d), F32)],
                scratch_shapes=[pltpu.VMEM((nh, d, d), F32)],
                operands=[proj, proj, cos2, sin_signed, decay, xi, zeta, gain])


def _outproj_fwd(x, y_lru, y_ret, w_out, g2, comm):
    t = x.shape[0]
    tm = min(PROJ_ROW_TILE, t)

    def body(x_ref, yl_ref, yr_ref, w_ref, g_ref, h1_ref, u2_ref):
        h1 = x_ref[...] + _dot(yl_ref[...], w_ref[:D_LRU, :]) + _dot(yr_ref[...], w_ref[D_LRU:, :])
        h1_ref[...] = h1
        _, _, u = _rms_fwd(h1, g_ref[...])
        u2_ref[...] = u.astype(u2_ref.dtype)

    row = lambda i: (i, 0)
    return _call(body, name="outproj_fwd", grid=(t // tm,), comm=comm,
                 in_specs=[pl.BlockSpec((tm, D_MODEL), row), pl.BlockSpec((tm, D_LRU), row), pl.BlockSpec((tm, D_RET), row),
                           _resident((D_MODEL, D_MODEL)), pl.BlockSpec((1, D_MODEL), lambda i: (0, 0))],
                 out_specs=[pl.BlockSpec((tm, D_MODEL), row), pl.BlockSpec((tm, D_MODEL), row)],
                 out_shape=[jax.ShapeDtypeStruct((t, D_MODEL), F32), jax.ShapeDtypeStruct((t, D_MODEL), MXU_DTYPE)],
                 operands=[x, y_lru, y_ret, w_out, g2])


FFN_TN = 768
FFN_NJ = D_FF // FFN_TN
FFN_GROUP = 4


def _ffn_fwd(u2, w_blocks, conv_w, conv_b, w_down, h1, gf, target):
    t = u2.shape[0]
    tm = min(ROW_TILE, t)
    tn, nj, group = FFN_TN, FFN_NJ, FFN_GROUP
    ng, tw = nj // group, group * tn
    hb = _halo_rows(u2.dtype)
    assert w_blocks.shape == (2 * nj, D_MODEL, tn)

    def conv(ext, col, up_ref, conv_ref, cw_ref, cb_ref, first):
        x = ext[hb:, :]
        up_ref[:, col] = x.astype(up_ref.dtype)
        prev = jnp.where(first, 0.0, ext[hb - SUBLANES:hb, :])
        cw = cw_ref[:, col]
        y = cb_ref[:, col] + cw[FFN_CONV - 1:FFN_CONV, :] * x
        for k in range(FFN_CONV - 1):
            y = y + cw[k:k + 1, :] * _shift_down(prev, x, FFN_CONV - 1 - k)
        conv_ref[:, col] = y.astype(conv_ref.dtype)
        return y

    def body(u_ref, uh_ref, w_ref, cwa_ref, cwv_ref, cba_ref, cbv_ref, wd_ref, h1_ref, gf_ref, tg_ref,
             upa_ref, upv_ref, ca_ref, cv_ref, act_ref, dh_ref, dhb_ref, dgf_ref, loss_ref, acc):
        i, jg = pl.program_id(0), pl.program_id(1)

        @pl.when((i == 0) & (jg == 0))
        def _():
            dgf_ref[...] = jnp.zeros_like(dgf_ref)
            loss_ref[...] = jnp.zeros_like(loss_ref)

        @pl.when(jg == 0)
        def _():
            acc[...] = jnp.zeros_like(acc)

        u_ext = jnp.concatenate([uh_ref[...], u_ref[...]], axis=0)

        def project(jj):
            j = jg * group + jj
            return _dot(u_ext, w_ref[j]), _dot(u_ext, w_ref[nj + j])

        down, ahead = None, project(0)
        for jj in range(group):
            col = slice(jj * tn, (jj + 1) * tn)
            j = jg * group + jj
            ext_a, ext_v = ahead
            if jj + 1 < group:
                ahead = project(jj + 1)
            a = conv(ext_a, col, upa_ref, ca_ref, cwa_ref, cba_ref, i == 0)
            v = conv(ext_v, col, upv_ref, cv_ref, cwv_ref, cbv_ref, i == 0)
            act = (_gelu(a) * v).astype(act_ref.dtype)
            act_ref[:, col] = act
            part = _dot(act, wd_ref[pl.ds(pl.multiple_of(j * tn, tn), tn), :])
            down = part if down is None else down + part
        acc[...] += down

        @pl.when(jg == ng - 1)
        def _():
            n, rstd, y = _rms_fwd(h1_ref[...] + acc[...], gf_ref[...])
            err = y - tg_ref[...]
            loss_ref[...] += (0.5 / D_MODEL) * jnp.sum(err * err)
            dh, dgf = _rms_bwd(err * (1.0 / D_MODEL), n, rstd, gf_ref[...])
            dgf_ref[...] += dgf
            dh_ref[...] = dh
            dhb_ref[...] = dh.astype(dhb_ref.dtype)

    per = tm // hb
    row = lambda i, j: (i, 0)
    const = lambda i, j: (0, 0)
    tile = pl.BlockSpec((tm, tw), lambda i, j: (i, j))
    return _call(body, name="ffn_fwd", grid=(t // tm, ng), vmem_limit=FUSED_VMEM_LIMIT,
                 in_specs=[pl.BlockSpec((tm, D_MODEL), row),
                           pl.BlockSpec((hb, D_MODEL), lambda i, j: (jnp.maximum(i * per - 1, 0), 0)),
                           _resident(w_blocks.shape),
                           pl.BlockSpec((FFN_CONV, tw), lambda i, j: (0, j)),
                           pl.BlockSpec((FFN_CONV, tw), lambda i, j: (0, j + ng)),
                           pl.BlockSpec((1, tw), lambda i, j: (0, j)), pl.BlockSpec((1, tw), lambda i, j: (0, j + ng)),
                           _resident((D_FF, D_MODEL)),
                           pl.BlockSpec((tm, D_MODEL), row), pl.BlockSpec((1, D_MODEL), const),
                           pl.BlockSpec((tm, D_MODEL), row)],
                 out_specs=[tile] * 5 + [pl.BlockSpec((tm, D_MODEL), row),
                            pl.BlockSpec((tm, D_MODEL), row), pl.BlockSpec((SUBLANES, D_MODEL), const),
                            pl.BlockSpec((SUBLANES, LANES), const)],
                 out_shape=[jax.ShapeDtypeStruct((t, D_FF), MXU_DTYPE)] * 5 + [
                            jax.ShapeDtypeStruct((t, D_MODEL), F32),
                            jax.ShapeDtypeStruct((t, D_MODEL), MXU_DTYPE), jax.ShapeDtypeStruct((SUBLANES, D_MODEL), F32),
                            jax.ShapeDtypeStruct((SUBLANES, LANES), F32)],
                 scratch_shapes=[pltpu.VMEM((tm, D_MODEL), F32)],
                 operands=[u2, u2, w_blocks, conv_w, conv_w, conv_b, conv_b, w_down, h1, gf, target])


FFN_ACC_ROWS = SUBLANES * (FFN_CONV + 1)


def _ffn_bwd(dh2, dh2_b, w_down, up_a, up_v, conv_a, conv_v, conv_w, w_up_blocks, h1, g2, comm):
    t = up_a.shape[0]
    tm = min(ROW_TILE, t)
    tn, nj, group = FFN_TN, FFN_NJ, FFN_GROUP
    ng, tw = nj // group, group * tn
    ni = t // tm
    assert w_up_blocks.shape == (2 * nj, D_MODEL, tn)

    def conv_bwd(dy, x, cw, acc_ref, carry_ref, dup_ref, col):
        nxt = carry_ref[...]
        carry_ref[...] = dy[:SUBLANES, :]
        ahead = [_shift_up(dy, nxt, FFN_CONV - 1 - k) for k in range(FFN_CONV)]
        dx = cw[FFN_CONV - 1:FFN_CONV, :] * dy
        for k in range(FFN_CONV - 1):
            dx = dx + cw[k:k + 1, :] * ahead[k]
        dx = dx.astype(dup_ref.dtype)
        dup_ref[:, col] = dx
        for k in range(FFN_CONV):
            acc_ref[k * SUBLANES:(k + 1) * SUBLANES, :] += _colsum8(ahead[k] * x)
        acc_ref[FFN_CONV * SUBLANES:, :] += _colsum8(dy)
        return dx

    def body(dh_ref, dhb_ref, wd_ref, ua_ref, uv_ref, ca_ref, cv_ref, cwa_ref, cwv_ref, wu_ref, h1_ref, g2_ref,
             dua_ref, duv_ref, acca_ref, accv_ref, dh1_ref, dh1b_ref, dg2_ref, carry_a, carry_v, du):
        i, jg = pl.program_id(0), pl.program_id(1)

        @pl.when((i == 0) & (jg == 0))
        def _():
            for ref in (acca_ref, accv_ref, carry_a, carry_v, dg2_ref):
                ref[...] = jnp.zeros_like(ref)

        dhb = dhb_ref[...]

        def through_down(jj):
            j = jg * group + jj
            return _dot_nt(dhb, wd_ref[pl.ds(pl.multiple_of(j * tn, tn), tn), :])

        part, ahead = None, through_down(0)
        for jj in range(group):
            col = slice(jj * tn, (jj + 1) * tn)
            j = jg * group + jj
            dact = ahead
            if jj + 1 < group:
                ahead = through_down(jj + 1)
            v = cv_ref[:, col].astype(F32)
            g, dg = _gelu_parts(ca_ref[:, col].astype(F32))
            da = conv_bwd(dact * v * dg, ua_ref[:, col].astype(F32), cwa_ref[:, col], acca_ref.at[j], carry_a.at[j],
                          dua_ref, col)
            dv = conv_bwd(dact * g, uv_ref[:, col].astype(F32), cwv_ref[:, col], accv_ref.at[j], carry_v.at[j],
                          duv_ref, col)
            term = _dot_nt(da, wu_ref[j]) + _dot_nt(dv, wu_ref[nj + j])
            part = term if part is None else part + term

        @pl.when(jg == 0)
        def _():
            du[...] = part

        @pl.when(jg > 0)
        def _():
            du[...] += part

        @pl.when(jg == ng - 1)
        def _():
            n, rstd, _ = _rms_fwd(h1_ref[...], g2_ref[...])
            dh1, dg2 = _rms_bwd(du[...], n, rstd, g2_ref[...])
            dh1 = dh1 + dh_ref[...]
            dg2_ref[...] += dg2
            dh1_ref[...] = dh1
            dh1b_ref[...] = dh1.astype(dh1b_ref.dtype)

    row = lambda i, j: (ni - 1 - i, 0)
    const = lambda i, j: (0, 0)
    tile = pl.BlockSpec((tm, tw), lambda i, j: (ni - 1 - i, j))
    acc = pl.BlockSpec((nj, FFN_ACC_ROWS, tn), lambda i, j: (0, 0, 0))
    return _call(body, name="ffn_bwd", grid=(ni, ng), comm=comm, vmem_limit=FFN_BWD_VMEM_LIMIT,
                 in_specs=[pl.BlockSpec((tm, D_MODEL), row), pl.BlockSpec((tm, D_MODEL), row),
                           _resident((D_FF, D_MODEL)), tile, tile, tile, tile,
                           pl.BlockSpec((FFN_CONV, tw), lambda i, j: (0, j)),
                           pl.BlockSpec((FFN_CONV, tw), lambda i, j: (0, j + ng)),
                           _resident(w_up_blocks.shape), pl.BlockSpec((tm, D_MODEL), row),
                           pl.BlockSpec((1, D_MODEL), const)],
                 out_specs=[tile, tile, acc, acc, pl.BlockSpec((tm, D_MODEL), row), pl.BlockSpec((tm, D_MODEL), row),
                            pl.BlockSpec((SUBLANES, D_MODEL), const)],
                 out_shape=[jax.ShapeDtypeStruct((t, D_FF), MXU_DTYPE), jax.ShapeDtypeStruct((t, D_FF), MXU_DTYPE),
                            jax.ShapeDtypeStruct((nj, FFN_ACC_ROWS, tn), F32),
                            jax.ShapeDtypeStruct((nj, FFN_ACC_ROWS, tn), F32),
                            jax.ShapeDtypeStruct((t, D_MODEL), F32), jax.ShapeDtypeStruct((t, D_MODEL), MXU_DTYPE),
                            jax.ShapeDtypeStruct((SUBLANES, D_MODEL), F32)],
                 scratch_shapes=[pltpu.VMEM((nj, SUBLANES, tn), F32), pltpu.VMEM((nj, SUBLANES, tn), F32),
                                 pltpu.VMEM((tm, D_MODEL), F32)],
                 operands=[dh2, dh2_b, w_down, up_a, up_v, conv_a, conv_v, conv_w, conv_w, w_up_blocks, h1, g2])


def _ret_bwd(proj, cos2, sin_signed, gain, o, states, dmix_at):
    t = proj.shape[0]
    c, d, nh = RET_CHUNK, RET_HEAD_DIM, RET_HEADS
    n_chunks = t // c
    per = RET_CHUNKS_PER_STEP if n_chunks % RET_CHUNKS_PER_STEP == 0 else 1
    rows = per * c
    n_steps = n_chunks // per
    decay, xi, zeta, g_chunk = _ret_consts()
    base = 2 * D_LRU

    def units(qk_ref, vg_ref, cos_ref, sin_ref, dec_ref, xi_ref, zeta_ref, gain_ref, o_ref, st_ref,
              dp_ref, dgain_ref, gstate, ctx):
        cur = [None] * nh
        dmix = ctx["scratch"][dmix_at[0]][dmix_at[1]]

        def start():
            @pl.when(pl.program_id(0) == 0)
            def _():
                gstate[...] = jnp.zeros_like(gstate)
                dgain_ref[...] = jnp.zeros_like(dgain_ref)
            for h in range(nh):
                cur[h] = gstate[h]

        def gate_and_norm(s, h, keep):
            rs = slice(s * c, (s + 1) * c)
            lo = h * d
            g = vg_ref[rs, D_RET + lo:D_RET + lo + d].astype(F32)
            gain_h = gain_ref[:, lo:lo + d]
            dy = dmix[rs, D_LRU + lo:D_LRU + lo + d]
            sg = _sigmoid(g)
            o_h = o_ref[rs, lo:lo + d]
            oc = o_h - jnp.mean(o_h, axis=-1, keepdims=True)
            rstd = lax.rsqrt(jnp.mean(oc * oc, axis=-1, keepdims=True) + NORM_EPS)
            on = oc * rstd
            at = base + 3 * D_RET + lo
            dp_ref[rs, at:at + d] = (dy * on * gain_h * (sg * (1.0 + g * (1.0 - sg)))).astype(dp_ref.dtype)
            don_g = dy * (g * sg)
            dgain_ref[:, lo:lo + d] += _colsum8(don_g * on)
            don = don_g * gain_h
            keep["do"] = rstd * (don - jnp.mean(don, axis=-1, keepdims=True)
                                 - on * jnp.mean(don * on, axis=-1, keepdims=True))

        def retain(s, h, keep):
            rs = slice(s * c, (s + 1) * c)
            cos2, sin_s = cos_ref[rs, :], sin_ref[rs, :]
            lo = h * d
            q = _rope(qk_ref[rs, lo:lo + d].astype(F32), cos2, sin_s)
            k = _rope(qk_ref[rs, D_RET + lo:D_RET + lo + d].astype(F32), cos2, sin_s) * RET_SCALE
            v = vg_ref[rs, lo:lo + d]
            xi_h, zeta_h, dec = xi_ref[:, lo:lo + d], zeta_ref[:, lo:lo + d], dec_ref[h]
            do = keep["do"]
            s_prev = st_ref[s, h]
            g_next = cur[h]
            p = _dot_nt(q, k) * dec
            dpm = _dot_nt(do, v) * dec
            keep["dq"] = _dot(dpm, k) + _dot_nt(do, s_prev) * xi_h
            keep["dk"] = _dot_tn(dpm, q) + _dot_nt(v, g_next) * zeta_h
            dv = _dot_tn(p, do) + _dot(k * zeta_h, g_next)
            cur[h] = g_next * g_chunk[h] + _dot_tn(q * xi_h, do)
            at = base + 2 * D_RET + lo
            dp_ref[rs, at:at + d] = dv.astype(dp_ref.dtype)

        def unrope(s, h, keep):
            rs = slice(s * c, (s + 1) * c)
            cos2, sin_s = cos_ref[rs, :], sin_ref[rs, :]
            lo = h * d
            dp_ref[rs, base + lo:base + lo + d] = _rope_bwd(keep["dq"], cos2, sin_s).astype(dp_ref.dtype)
            at = base + D_RET + lo
            dp_ref[rs, at:at + d] = _rope_bwd(keep["dk"] * RET_SCALE, cos2, sin_s).astype(dp_ref.dtype)

        def end():
            for h in range(nh):
                gstate[h] = cur[h]

        pieces = [start]
        for s in reversed(range(per)):
            for h in range(nh):
                keep = {}
                pieces += [lambda s=s, h=h, keep=keep, f=f: f(s, h, keep) for f in (gate_and_norm, retain, unrope)]
        return pieces + [end]

    rev = lambda col: (lambda i: (n_steps - 1 - i, col))
    full2 = lambda i: (0, 0)
    return dict(units=units, grid=(n_steps,),
                in_specs=[pl.BlockSpec((rows, 2 * D_RET), rev(1)), pl.BlockSpec((rows, 2 * D_RET), rev(2)),
                          pl.BlockSpec((rows, d), rev(0)), pl.BlockSpec((rows, d), rev(0)),
                          pl.BlockSpec((nh, c, c), lambda i: (0, 0, 0)), pl.BlockSpec((c, D_RET), full2),
                          pl.BlockSpec((c, D_RET), full2), pl.BlockSpec((1, D_RET), full2),
                          pl.BlockSpec((rows, D_RET), rev(0)),
                          pl.BlockSpec((per, nh, d, d), lambda i: (n_steps - 1 - i, 0, 0, 0))],
                out_specs=[pl.BlockSpec((rows, D_IN), rev(0)), pl.BlockSpec((SUBLANES, D_RET), full2)],
                out_shape=[jax.ShapeDtypeStruct((t, D_IN), MXU_DTYPE), jax.ShapeDtypeStruct((SUBLANES, D_RET), F32)],
                scratch_shapes=[pltpu.VMEM((nh, d, d), F32)],
                operands=[proj, proj, cos2, sin_signed, decay, xi, zeta, gain, o, states])


LRU_ACC = {"conv_w": 0, "conv_b": LRU_CONV, "gate_a_b": LRU_CONV + 1, "gate_x_b": LRU_CONV + 2,
           "lambda": LRU_CONV + 3, "norm_gain": LRU_CONV + 4}
LRU_ACC_ROWS = SUBLANES * (LRU_CONV + 5)


def _lru_bwd(proj, xc_all, h_all, conv_w, wa, ba, wx, bx, lam, gain, dproj_part, dmix_at):
    t = proj.shape[0]
    tm = min(MIX_ROW_TILE, t)
    c = D_LRU
    ni = t // tm

    def body(x_ref, xh_ref, g_ref, xc_ref, h_ref, hh_ref, cw_ref, wa_ref, ba_ref, wx_ref, bx_ref, lam_ref,
             gain_ref, acc_ref, dwa_ref, dwx_ref, a_scr, b_scr, mu_scr, carry_mu, carry_dxc, ctx):
        dp_ref = ctx["outs"][dproj_part][0]
        dmix = ctx["scratch"][dmix_at[0]][dmix_at[1]]
        fill = ctx["fill"]
        i = pl.program_id(0)
        r = ni - 1 - i

        @pl.when(i == 0)
        def _():
            acc_ref[...] = jnp.zeros_like(acc_ref)
            dwa_ref[...] = jnp.zeros_like(dwa_ref)
            dwx_ref[...] = jnp.zeros_like(dwx_ref)
            carry_mu[...] = jnp.zeros_like(carry_mu)
            carry_dxc[...] = jnp.zeros_like(carry_dxc)

        def add(name, val, k=0):
            lo = (LRU_ACC[name] + k) * SUBLANES
            acc_ref[lo:lo + SUBLANES, :] += _colsum8(val)

        fill()
        xc, h = xc_ref[...], h_ref[...]
        lam_v = lam_ref[...]
        sp = _softplus(-lam_v)
        rg, ig, a, m = _lru_gates(xc, wa_ref[...], ba_ref[...], wx_ref[...], bx_ref[...], sp, fill)
        gl, dgl = _gelu_parts(g_ref[...].astype(F32))
        fill()
        zn, rstd, _ = _rms_fwd(h * gl, gain_ref[...])
        dy = dmix[:, :c]
        dz, dgain = _rms_bwd(dy, zn, rstd, gain_ref[...])
        lo = LRU_ACC["norm_gain"] * SUBLANES
        acc_ref[lo:lo + SUBLANES, :] += dgain
        dp_ref[:, c:2 * c] = (dz * h * dgl).astype(dp_ref.dtype)
        dh = dz * gl
        fill()
        ga, gb = _group_scan(a, a * dh, reverse=True, fill=fill)
        a_scr[...] = ga
        b_scr[...] = gb
        mu_next_tile = carry_mu[...]
        carry_mu[...] = _carry_scan(a_scr, b_scr, mu_scr, mu_next_tile, reverse=True)
        fill()
        lam_t = dh + _shift_up(mu_scr[...], mu_next_tile, 1)
        h_prev = _shift_down(jnp.where(r == 0, 0.0, hh_ref[...]), h, 1)
        da = lam_t * h_prev
        dig = lam_t * m * xc
        dxc = lam_t * m * ig
        dlog_a = da * a - (lam_t * ig * xc) * (a * a) / m
        fill()
        dpr = dlog_a * ((-LRU_C) * sp) * rg * (1.0 - rg)
        add("lambda", dlog_a * ((-LRU_C) * rg) * (-_sigmoid(-lam_v)))
        dpi = dig * ig * (1.0 - ig)
        add("gate_a_b", dpr)
        add("gate_x_b", dpi)
        fill()
        dwa_ref[...] += _dot_tn(xc, dpr)
        dwx_ref[...] += _dot_tn(xc, dpi)
        dxc = dxc + _dot_nt(dpr, wa_ref[...]) + _dot_nt(dpi, wx_ref[...])
        fill()
        add("conv_b", dxc)
        x = x_ref[...].astype(F32)
        prev = jnp.where(r == 0, 0.0, xh_ref[...].astype(F32)[-SUBLANES:, :])
        cw = cw_ref[...]
        nxt = carry_dxc[...]
        carry_dxc[...] = dxc[:SUBLANES, :]
        dx = cw[LRU_CONV - 1:LRU_CONV, :] * dxc
        for k in range(LRU_CONV - 1):
            dx = dx + cw[k:k + 1, :] * _shift_up(dxc, nxt, LRU_CONV - 1 - k)
        fill()
        for k in range(LRU_CONV):
            add("conv_w", dxc * _shift_down(prev, x, LRU_CONV - 1 - k), k)
        dp_ref[:, :c] = dx.astype(dp_ref.dtype)

    hb = _halo_rows(proj.dtype)
    rev = lambda col: (lambda i: (ni - 1 - i, col))
    halo = lambda rows: (lambda i: (jnp.maximum((ni - 1 - i) * (tm // rows) - 1, 0), 0))
    full = lambda i: (0, 0)
    vec = pl.BlockSpec((1, c), full)
    mat = pl.BlockSpec((c, c), full)
    return dict(body=body, grid=(ni,), fill_points=16,
                in_specs=[pl.BlockSpec((tm, c), rev(0)), pl.BlockSpec((hb, c), halo(hb)), pl.BlockSpec((tm, c), rev(1)),
                          pl.BlockSpec((tm, c), rev(0)), pl.BlockSpec((tm, c), rev(0)),
                          pl.BlockSpec((SUBLANES, c), halo(SUBLANES)),
                          pl.BlockSpec((LRU_CONV, c), full), mat, vec, mat, vec, vec, vec],
                out_specs=[pl.BlockSpec((LRU_ACC_ROWS, c), full), mat, mat],
                out_shape=[jax.ShapeDtypeStruct((LRU_ACC_ROWS, c), F32), jax.ShapeDtypeStruct((c, c), F32),
                           jax.ShapeDtypeStruct((c, c), F32)],
                scratch_shapes=[pltpu.VMEM((tm, c), F32), pltpu.VMEM((tm, c), F32), pltpu.VMEM((tm, c), F32),
                                pltpu.VMEM((SUBLANES, c), F32), pltpu.VMEM((SUBLANES, c), F32)],
                operands=[proj, proj, proj, xc_all, h_all, h_all, conv_w, wa, ba, wx, bx, lam, gain])


def _mix_proj_bwd(dh1, dh1_b, w_out, w_in_blocks, x, g1, dproj_part):
    t = x.shape[0]
    tm = min(MIX_ROW_TILE, t)
    ni = t // tm
    nb, _, cb = w_in_blocks.shape
    first_free = -(-2 * D_LRU // cb)
    du = [None]

    def term(dp_ref, w_ref, d):
        part = _dot_nt(dp_ref[:, d * cb:(d + 1) * cb], w_ref[d])
        du[0] = part if du[0] is None else du[0] + part

    def head(dh_ref, dhb_ref, wo_ref, wi_ref, x_ref, g_ref, gx_ref, dg_ref, dmix, ctx):
        @pl.when(pl.program_id(0) == 0)
        def _():
            dg_ref[...] = jnp.zeros_like(dg_ref)
        dmix[...] = _dot_nt(dhb_ref[...], wo_ref[...])
        du[0] = None

    def units(dh_ref, dhb_ref, wo_ref, wi_ref, x_ref, g_ref, gx_ref, dg_ref, dmix, ctx):
        dp_ref = ctx["outs"][dproj_part][0]
        return [lambda d=d: term(dp_ref, wi_ref, d) for d in range(first_free, nb)]

    def tail(dh_ref, dhb_ref, wo_ref, wi_ref, x_ref, g_ref, gx_ref, dg_ref, dmix, ctx):
        dp_ref = ctx["outs"][dproj_part][0]
        for d in range(first_free):
            term(dp_ref, wi_ref, d)
        n, rstd, _ = _rms_fwd(x_ref[...], g_ref[...])
        dx, dg = _rms_bwd(du[0], n, rstd, g_ref[...])
        dg_ref[...] += dg
        gx_ref[...] = dx + dh_ref[...]

    row = lambda i: (ni - 1 - i, 0)
    const = lambda i: (0, 0)
    tile = pl.BlockSpec((tm, D_MODEL), row)
    return dict(head=head, units=units, tail=tail, grid=(ni,),
                in_specs=[tile, tile, _resident(w_out.shape), _resident(w_in_blocks.shape), tile,
                          pl.BlockSpec((1, D_MODEL), const)],
                out_specs=[tile, pl.BlockSpec((SUBLANES, D_MODEL), const)],
                out_shape=[jax.ShapeDtypeStruct((t, D_MODEL), F32), jax.ShapeDtypeStruct((SUBLANES, D_MODEL), F32)],
                scratch_shapes=[pltpu.VMEM((tm, D_MODEL), F32)],
                operands=[dh1, dh1_b, w_out, w_in_blocks, x, g1])


def _pair_sum(core, a, b, name):
    n, r, c = b.shape
    spec = pl.BlockSpec((None, r, c), lambda q, core: (q, 0, 0))

    def body(core_ref, a_ref, b_ref, o_ref):
        o_ref[...] = (a_ref[...].astype(F32) + b_ref[...].astype(F32)).astype(o_ref.dtype)

    return pl.pallas_call(
        body, name=name,
        grid_spec=pltpu.PrefetchScalarGridSpec(
            num_scalar_prefetch=1, grid=(n,),
            in_specs=[pl.BlockSpec((None, r, c), lambda q, core: (2 * q + core[0], 0, 0)), spec], out_specs=spec),
        out_shape=jax.ShapeDtypeStruct(b.shape, b.dtype),
        compiler_params=pltpu.CompilerParams(dimension_semantics=("arbitrary",), vmem_limit_bytes=VMEM_LIMIT),
    )(core, a, b)


ADAMW_BLOCK_BYTES = 4 * 1024 * 1024


def _sum_adamw(parts, w, m, v, name):
    n_parts, r, c = parts.shape
    tr = r
    while n_parts * tr * c * parts.dtype.itemsize > ADAMW_BLOCK_BYTES and tr % (4 * SUBLANES) == 0:
        tr //= 2

    def body(p_ref, w_ref, m_ref, v_ref, g_ref, d_ref, nm_ref, nv_ref):
        g = p_ref[0].astype(F32)
        for s in range(1, n_parts):
            g = g + p_ref[s].astype(F32)
        nm = ADAM_B1 * m_ref[...] + (1.0 - ADAM_B1) * g
        nv = ADAM_B2 * v_ref[...] + (1.0 - ADAM_B2) * (g * g)
        m_hat = nm / (1.0 - ADAM_B1 ** ADAM_STEP)
        v_hat = nv / (1.0 - ADAM_B2 ** ADAM_STEP)
        g_ref[...] = g
        d_ref[...] = -ADAM_LR * (m_hat / (jnp.sqrt(v_hat) + ADAM_EPS) + ADAM_WD * w_ref[...])
        nm_ref[...] = nm
        nv_ref[...] = nv

    row = pl.BlockSpec((tr, c), lambda i: (i, 0))
    return _call(body, name=name, grid=(r // tr,),
                 in_specs=[pl.BlockSpec((n_parts, tr, c), lambda i: (0, i, 0)), row, row, row],
                 out_specs=[row, row, row, row], out_shape=[jax.ShapeDtypeStruct((r, c), F32)] * 4,
                 operands=[parts, w, m, v])


MATRICES = ("w_in", "w_out", "ffn_up_w", "ffn_down_w")
CONVS = ("lru_conv_w", "ffn_conv_w")
REPLICATED = ("norm1_gain", "lru_conv_b", "lru_gate_a_w", "lru_gate_a_b", "lru_gate_x_w", "lru_gate_x_b", "lru_lambda",
              "lru_norm_gain", "ret_norm_gain", "norm2_gain", "ffn_conv_b", "final_norm_gain")
WEIGHTS = ("norm1_gain", "w_in", "lru_conv_w", "lru_conv_b", "lru_gate_a_w", "lru_gate_a_b", "lru_gate_x_w",
           "lru_gate_x_b", "lru_lambda", "lru_norm_gain", "ret_norm_gain", "w_out", "norm2_gain", "ffn_up_w",
           "ffn_conv_w", "ffn_conv_b", "ffn_down_w", "final_norm_gain")


def _rows(a, pad_to):
    a = a.reshape(-1, LANES)
    pad = (-a.shape[0]) % pad_to
    return jnp.pad(a, ((0, pad), (0, 0))) if pad else a


def _pack(arrays, pad_to):
    rows, layout, at = [], [], 0
    for a in arrays:
        r = _rows(a, pad_to)
        layout.append((at, a.size // LANES, a.shape))
        rows.append(r)
        at += r.shape[0]
    return jnp.concatenate(rows, axis=0), layout


def _unpack(packed, layout):
    lead = packed.shape[:-2]
    return [packed[..., at:at + n, :].reshape(lead + shape) for at, n, shape in layout]


def _conv_rows(lru, ffn, dtype, pad_to):
    lead = lru.shape[:-2]
    flat = jnp.concatenate([lru.reshape(lead + (-1,)), ffn.reshape(lead + (-1,))], axis=-1).astype(dtype)
    rows = flat.shape[-1] // LANES
    pad = (-rows) % pad_to
    return jnp.pad(flat.reshape(lead + (rows, LANES)), [(0, 0)] * len(lead) + [(0, pad), (0, 0)])


def _column_blocks(full):
    r, c = full.shape
    return full.reshape(r, N_DEV, c // N_DEV).transpose(1, 0, 2)


def _block_diag(w):
    nh, d, _ = w.shape
    eye = jnp.eye(nh, dtype=w.dtype)
    return (w[:, :, None, :] * eye[:, None, :, None]).reshape(nh * d, nh * d)


def _diag_blocks(dense, nh):
    d = dense.shape[0] // nh
    blocks = dense.reshape(nh, d, nh, d)
    return jnp.stack([blocks[h, :, h, :] for h in range(nh)], axis=0)


def kernel(x, norm1_gain, w_in, lru_conv_w, lru_conv_b, lru_gate_a_w, lru_gate_a_b, lru_gate_x_w, lru_gate_x_b, lru_lambda, lru_norm_gain, ret_norm_gain, w_out, norm2_gain, ffn_up_w, ffn_conv_w, ffn_conv_b, ffn_down_w, final_norm_gain, loss_target, m_norm1_gain, m_w_in, m_lru_conv_w, m_lru_conv_b, m_lru_gate_a_w, m_lru_gate_a_b, m_lru_gate_x_w, m_lru_gate_x_b, m_lru_lambda, m_lru_norm_gain, m_ret_norm_gain, m_w_out, m_norm2_gain, m_ffn_up_w, m_ffn_conv_w, m_ffn_conv_b, m_ffn_down_w, m_final_norm_gain, v_norm1_gain, v_w_in, v_lru_conv_w, v_lru_conv_b, v_lru_gate_a_w, v_lru_gate_a_b, v_lru_gate_x_w, v_lru_gate_x_b, v_lru_lambda, v_lru_norm_gain, v_ret_norm_gain, v_w_out, v_norm2_gain, v_ffn_up_w, v_ffn_conv_w, v_ffn_conv_b, v_ffn_down_w, v_final_norm_gain):
    args = dict(locals())
    given = {n: args[n] for n in WEIGHTS}
    out_shape = {n: given[n].shape for n in WEIGHTS}

    def plain(a):
        return a.reshape(1, -1) if a.ndim <= 2 else a[0]

    w = {n: plain(given[n]) for n in WEIGHTS}
    mom_m = {n: plain(args["m_" + n]) for n in WEIGHTS}
    mom_v = {n: plain(args["v_" + n]) for n in WEIGHTS}
    x2, target = x[0], loss_target[0]
    t = x2.shape[0]
    core = lax.axis_index("c").astype(jnp.int32).reshape(1)
    res = {}

    conv_pad = _conv_rows(w["lru_conv_w"], w["ffn_conv_w"], F32, SUBLANES)
    first = _gather_first([w["w_in"].astype(MXU_DTYPE), conv_pad])
    w_in_blocks, conv_all = _run_comms([first, _gather_second(first.out_shape)], "w_in_all_gather")
    n_lru = w["lru_conv_w"].size
    conv_flat = conv_all.reshape(N_DEV, -1)
    lru_cw = conv_flat[:, :n_lru].reshape((N_DEV,) + w["lru_conv_w"].shape).transpose(1, 0, 2).reshape(LRU_CONV, D_LRU)
    ffn_cw = conv_flat[:, n_lru:n_lru + w["ffn_conv_w"].size].reshape((N_DEV,) + w["ffn_conv_w"].shape)
    ffn_cw = ffn_cw.transpose(1, 0, 2).reshape(FFN_CONV, 2 * D_FF)

    cos2, sin_signed = _rope_tables(t)
    wa = _block_diag(w["lru_gate_a_w"]).astype(MXU_DTYPE)
    wx = _block_diag(w["lru_gate_x_w"]).astype(MXU_DTYPE)
    gf = w["final_norm_gain"]

    early = _gather_first([w["w_out"].astype(MXU_DTYPE), w["ffn_down_w"].astype(MXU_DTYPE)])
    (u1, proj), (w_out_part, down_part) = _inproj_fwd(x2, w["norm1_gain"], w_in_blocks, early)
    ((xc, h_lru, y_lru), (o_ret, y_ret, states)), (w_out_blocks, down_blocks, up_part) = _fused(
        [_lru_fwd(proj, lru_cw, w["lru_conv_b"], wa, w["lru_gate_a_b"], wx, w["lru_gate_x_b"], w["lru_lambda"],
                  w["lru_norm_gain"]),
         _ret_fwd(proj, cos2, sin_signed, w["ret_norm_gain"])],
        "mix_fwd", _both(_gather_second([w_out_part, down_part]), _gather_first([w["ffn_up_w"].astype(MXU_DTYPE)])))
    w_out_full = w_out_blocks.reshape(D_MODEL, D_MODEL)
    w_down_full = down_blocks.reshape(D_FF, D_MODEL)

    (h1, u2), (up_blocks,) = _outproj_fwd(x2, y_lru, y_ret, w_out_full, w["norm2_gain"], _gather_second([up_part]))
    up_a, up_v, conv_a, conv_v, act, dh2, dh2_b, dgf, loss_local = _ffn_fwd(u2, up_blocks, ffn_cw, w["ffn_conv_b"],
                                                                            w_down_full, h1, gf, target)

    def to_owner_chips(blocks, names, tag):
        theirs = _run_comms([_pair_exchange(blocks)], "grads_pair_exchange_" + tag)
        return [_pair_sum(core, a, b, "grads_pair_sum_" + n) for n, a, b in zip(names, blocks, theirs)]

    def adamw(name, parts):
        res[name] = _sum_adamw(parts, w[name], mom_m[name], mom_v[name], "adamw_" + name)

    g = {"final_norm_gain": dgf[0]}
    dup_a, dup_v, acc_a, acc_v, dh1, dh1_b, dg2 = _ffn_bwd(
        dh2, dh2_b, w_down_full, up_a, up_v, conv_a, conv_v, ffn_cw, up_blocks, h1, w["norm2_gain"], None)
    per_col = lambda a: a[:, ::SUBLANES].transpose(1, 0, 2).reshape(FFN_CONV + 1, D_FF)
    acc = jnp.concatenate([per_col(acc_a), per_col(acc_v)], axis=1)
    g_ffn_cw, g["ffn_conv_b"] = acc[:FFN_CONV], acc[FFN_CONV:]
    g["norm2_gain"] = dg2[:1]
    g_up = _mm_tn(u2, dup_a, "ffn_up_wgrad_a", blocks=N_DEV // 2, room=N_DEV)
    g_up = _mm_tn(u2, dup_v, "ffn_up_wgrad_v", blocks=N_DEV // 2, before=g_up)
    g_out_lru, (up_theirs,) = _mm_tn(y_lru, dh1_b, "w_out_wgrad_lru", comm=_pair_exchange([g_up]))
    up_sums = [_pair_sum(core, g_up, up_theirs, "grads_pair_sum_ffn_up_w")]
    g_down, (up_parts,) = _mm_tn(act, dh2_b, "ffn_down_wgrad", comm=_chip_exchange(up_sums))
    adamw("ffn_up_w", up_parts)
    g_out = jnp.concatenate([g_out_lru, _mm_tn(y_ret, dh1_b, "w_out_wgrad_ret")], axis=0)
    low_sums = to_owner_chips([g_down.reshape(N_DEV, D_FF // N_DEV, D_MODEL),
                               g_out.reshape(N_DEV, D_MODEL // N_DEV, D_MODEL)], ["ffn_down_w", "w_out"], "low")
    (dproj, dgain_ret), (grad_x, dg1), (lru_acc, dwa, dwx) = _fused(
        [_ret_bwd(proj, cos2, sin_signed, w["ret_norm_gain"], o_ret, states, dmix_at=(1, 0)),
         _mix_proj_bwd(dh1, dh1_b, w_out_full, w_in_blocks, x2, w["norm1_gain"], dproj_part=0),
         _lru_bwd(proj, xc, h_lru, lru_cw, wa, w["lru_gate_a_b"], wx, w["lru_gate_x_b"], w["lru_lambda"],
                  w["lru_norm_gain"], dproj_part=0, dmix_at=(1, 0))],
        "mix_bwd")
    g["norm1_gain"] = dg1[:1]
    g["ret_norm_gain"] = dgain_ret[:1]
    lru_acc = lru_acc[::SUBLANES]
    g_lru_cw = lru_acc[:LRU_CONV]
    for name in ("conv_b", "gate_a_b", "gate_x_b", "lambda", "norm_gain"):
        g["lru_" + name] = lru_acc[LRU_ACC[name]:LRU_ACC[name] + 1]
    g["lru_gate_a_w"] = _diag_blocks(dwa, LRU_HEADS)
    g["lru_gate_x_w"] = _diag_blocks(dwx, LRU_HEADS)
    rep_packed, rep_layout = _pack([g[n] for n in REPLICATED] + [loss_local], SUBLANES)
    g_in, (down_parts, out_parts, rep_part) = _mm_tn(u1, dproj, "w_in_wgrad", blocks=N_DEV,
                                                     comm=_both(_chip_exchange(low_sums), _gather_first([rep_packed])))
    adamw("ffn_down_w", down_parts)
    adamw("w_out", out_parts)
    g_conv = _conv_rows(_column_blocks(g_lru_cw), _column_blocks(g_ffn_cw), GRAD_DTYPE, 2 * SUBLANES)
    in_sums = to_owner_chips([g_in, g_conv], ["w_in", "conv"], "in")
    in_parts, conv_parts, rep_parts = _run_comms([_both(_chip_exchange(in_sums), _gather_second([rep_part]))],
                                                 "last_grads_exchange")
    adamw("w_in", in_parts)
    pad16 = lambda d: _conv_rows(d["lru_conv_w"], d["ffn_conv_w"], F32, 2 * SUBLANES)
    conv_res = _sum_adamw(conv_parts, pad16(w), pad16(mom_m), pad16(mom_v), "adamw_conv")
    for n, lo, hi in (("lru_conv_w", 0, n_lru), ("ffn_conv_w", n_lru, n_lru + w["ffn_conv_w"].size)):
        res[n] = [r.reshape(-1)[lo:hi].reshape(w[n].shape) for r in conv_res]
    no_state = jnp.zeros_like(loss_local)
    rep_res = _sum_adamw(rep_parts, *[_pack([d[n] for n in REPLICATED] + [no_state], SUBLANES)[0]
                                      for d in (w, mom_m, mom_v)], "adamw_replicated")
    for k in range(4):
        for n, a in zip(REPLICATED, _unpack(rep_res[k], rep_layout)):
            res.setdefault(n, [None] * 4)[k] = a
    loss = _unpack(rep_res[0], rep_layout)[-1][0, 0]

    outs = [loss, grad_x[None]]
    for k in range(4):
        outs += [res[n][k].reshape(out_shape[n]) for n in WEIGHTS]
    return tuple(outs)
```

```python
import math

import numpy as np
import jax
import jax.numpy as jnp
from jax import lax
from jax.experimental import pallas as pl
from jax.experimental.pallas import tpu as pltpu

F32 = jnp.float32
BF16 = jnp.bfloat16
MXU_DTYPE = jnp.bfloat16
GRAD_DTYPE = jnp.bfloat16

N_DEV = 8
N_CHIPS = 4
D_MODEL = 1024
D_LRU = 512
LRU_HEADS = 8
LRU_CONV = 4
LRU_C = 8.0
D_RET = 512
RET_HEADS = 4
RET_HEAD_DIM = 128
RET_CHUNK = 128
ROPE_BASE = 10000.0
D_IN = 3072
D_FF = 3072
FFN_CONV = 3
NORM_EPS = 1e-6

ADAM_LR = 0.001
ADAM_B1 = 0.9
ADAM_B2 = 0.999
ADAM_EPS = 1e-08
ADAM_WD = 0.01
ADAM_STEP = 10

SUBLANES = 8
LANES = 128
VMEM_LIMIT = 48 * 1024 * 1024
FUSED_VMEM_LIMIT = VMEM_LIMIT
FFN_BWD_VMEM_LIMIT = 56 * 1024 * 1024

ROW_TILE = 256
MIX_ROW_TILE = 256
PROJ_ROW_TILE = 512
WGRAD_ROWS = 2048
WGRAD_TILE = 1024
WGRAD_BLOCK_COLUMNS = 768
WGRAD_SLOTS = 3

MESH = pl.DeviceIdType.MESH
ANY = pl.BlockSpec(memory_space=pl.ANY)


def _dot(a, b):
    return jnp.dot(a.astype(MXU_DTYPE), b.astype(MXU_DTYPE), preferred_element_type=F32)


def _dot_nt(a, b):
    return lax.dot_general(a.astype(MXU_DTYPE), b.astype(MXU_DTYPE), (((1,), (1,)), ((), ())),
                           preferred_element_type=F32)


def _dot_tn(a, b):
    return lax.dot_general(a.astype(MXU_DTYPE), b.astype(MXU_DTYPE), (((0,), (0,)), ((), ())),
                           preferred_element_type=F32)


def _sigmoid(x):
    return 0.5 + 0.5 * jnp.tanh(0.5 * x)


_GELU_C = math.sqrt(2.0 / math.pi)
_GELU_C3 = _GELU_C * 0.044715


def _gelu_parts(x):
    x2 = x * x
    t = jnp.tanh(x * (_GELU_C + _GELU_C3 * x2))
    cdf = 0.5 + 0.5 * t
    g = x * cdf
    dg = cdf + (0.5 * x) * (1.0 - t * t) * (_GELU_C + (3.0 * _GELU_C3) * x2)
    return g, dg


def _gelu(x):
    t = jnp.tanh(_GELU_C * (x + 0.044715 * (x * x * x)))
    return x * (0.5 * (1.0 + t))


def _softplus(x):
    return jnp.maximum(x, 0.0) + jnp.log1p(jnp.exp(-jnp.abs(x)))


def _bcast_row(x, r, rows=SUBLANES):
    return jnp.broadcast_to(x[r:r + 1, :], (rows, x.shape[1]))


def _colsum8(x):
    return jnp.broadcast_to(jnp.sum(x, axis=0, keepdims=True), (SUBLANES, x.shape[1]))


def _groups(x):
    return x.reshape(x.shape[0] // SUBLANES, SUBLANES, x.shape[1])


def _shift_down(prev8, tile, s):
    if s == 0:
        return tile
    own = pltpu.roll(_groups(tile), s, 1)
    before = jnp.concatenate([pltpu.roll(_groups(prev8), s, 1), own[:-1]], axis=0)
    row = lax.broadcasted_iota(jnp.int32, own.shape, 1)
    return jnp.where(row >= s, own, before).reshape(tile.shape)


def _shift_up(tile, next8, s):
    if s == 0:
        return tile
    own = pltpu.roll(_groups(tile), SUBLANES - s, 1)
    after = jnp.concatenate([own[1:], pltpu.roll(_groups(next8), SUBLANES - s, 1)], axis=0)
    row = lax.broadcasted_iota(jnp.int32, own.shape, 1)
    return jnp.where(row < SUBLANES - s, own, after).reshape(tile.shape)


def _group_scan(a, b, reverse, fill=lambda: None):
    n, c = a.shape
    row = lax.broadcasted_iota(jnp.int32, a.shape, 0) & (SUBLANES - 1)

    def within_group(x, shift):
        return pltpu.roll(x.reshape(n // SUBLANES, SUBLANES, c), shift, 1).reshape(n, c)

    for s in (1, 2, 4):
        if s > 1:
            fill()
        shift = (SUBLANES - s) if reverse else s
        a_sh = within_group(a, shift)
        b_sh = within_group(b, shift)
        m = (row <= SUBLANES - 1 - s) if reverse else (row >= s)
        b = jnp.where(m, a * b_sh + b, b)
        a = jnp.where(m, a * a_sh, a)
    return a, b


def _carry_scan(a_ref, b_ref, out_ref, carry0, reverse):
    n_groups = a_ref.shape[0] // SUBLANES
    carry = carry0
    for i in range(n_groups):
        r0 = ((n_groups - 1 - i) if reverse else i) * SUBLANES
        hg = a_ref[r0:r0 + SUBLANES, :] * carry + b_ref[r0:r0 + SUBLANES, :]
        out_ref[r0:r0 + SUBLANES, :] = hg
        carry = _bcast_row(hg, 0 if reverse else SUBLANES - 1)
    return carry


def _rms_fwd(h, gain):
    rstd = lax.rsqrt(jnp.mean(h * h, axis=-1, keepdims=True) + NORM_EPS)
    n = h * rstd
    return n, rstd, n * gain


def _rms_bwd(dy, n, rstd, gain):
    dn = dy * gain
    dh = rstd * (dn - n * jnp.mean(dn * n, axis=-1, keepdims=True))
    return dh, _colsum8(dy * n)


def _halo_rows(dtype):
    return SUBLANES * (4 // jnp.dtype(dtype).itemsize)


def _halo_map(tile_rows, col, halo_rows=SUBLANES):
    per = tile_rows // halo_rows
    return lambda i: (jnp.maximum(i * per - 1, 0), col)


def _resident(shape):
    return pl.BlockSpec(shape, lambda *_: (0,) * len(shape), pipeline_mode=pl.Buffered(1))


def _place():
    x, y, c = lax.axis_index("x"), lax.axis_index("y"), lax.axis_index("c")
    chips = [(1 - x, y), (x, 1 - y), (1 - x, 1 - y)]
    return x, y, c, chips


def _dev(x, y, c):
    return 4 * x + 2 * y + c


class _Copy:
    def __init__(self, make):
        self.make = make

    def start(self):
        self.make().start()

    def wait(self):
        self.make().wait()

    def wait_send(self):
        self.make().wait_send()

    def wait_recv(self):
        self.make().wait_recv()


def _remote(src, dst, send_sem, recv_sem, to):
    return _Copy(lambda: pltpu.make_async_remote_copy(src_ref=src, dst_ref=dst, send_sem=send_sem, recv_sem=recv_sem,
                                                      device_id=to, device_id_type=MESH))


def _local(src, dst, sem):
    return _Copy(lambda: pltpu.make_async_copy(src, dst, sem))


class _Comm:
    def __init__(self, operands, out_shape, sems, descs, aliases=()):
        self.operands, self.out_shape, self.sems, self.descs, self.aliases = operands, out_shape, sems, descs, aliases

    def start(self, ins, outs, sems):
        local, sends, _ = self.descs(ins, outs, sems)
        for cp in sends + local:
            cp.start()

    def wait(self, ins, outs, sems):
        local, sends, recvs = self.descs(ins, outs, sems)
        for cp in recvs:
            cp.wait_recv()
        for cp in sends:
            cp.wait_send()
        for cp in local:
            cp.wait()


def _gather_first(shards):
    n = len(shards)

    def descs(ins, outs, sems):
        send, recv, loc = sems
        x, y, c, chips = _place()
        me = _dev(x, y, c)
        targets = [(x, y, 1 - c)] + [(*chip, c) for chip in chips]
        local, sends, recvs = [], [], []
        for t in range(n):
            local.append(_local(ins[t], outs[t].at[me], loc.at[t]))
            for k, to in enumerate(targets):
                i = 4 * t + k
                sends.append(_remote(ins[t], outs[t].at[me], send.at[i], recv.at[i], to))
                recvs.append(_remote(ins[t], outs[t].at[_dev(*to)], send.at[i], recv.at[i], to))
        return local, sends, recvs

    return _Comm(list(shards), [jax.ShapeDtypeStruct((N_DEV,) + s.shape, s.dtype) for s in shards],
                 [pltpu.SemaphoreType.DMA((4 * n,)), pltpu.SemaphoreType.DMA((4 * n,)), pltpu.SemaphoreType.DMA((n,))],
                 descs)


def _gather_second(gathered):
    n = len(gathered)

    def descs(ins, outs, sems):
        send, recv = sems
        x, y, c, chips = _place()
        sends, recvs = [], []
        for t in range(n):
            for j, chip in enumerate(chips):
                i = 3 * t + j
                have, get = _dev(*chip, c), _dev(*chip, 1 - c)
                sends.append(_remote(outs[t].at[have], outs[t].at[have], send.at[i], recv.at[i], (x, y, 1 - c)))
                recvs.append(_remote(outs[t].at[have], outs[t].at[get], send.at[i], recv.at[i], (x, y, 1 - c)))
        return [], sends, recvs

    return _Comm(list(gathered), [jax.ShapeDtypeStruct(g.shape, g.dtype) for g in gathered],
                 [pltpu.SemaphoreType.DMA((3 * n,)), pltpu.SemaphoreType.DMA((3 * n,))], descs,
                 aliases=[(t, t) for t in range(n)])


def _pair_exchange(blocks):
    n = len(blocks)

    def descs(ins, outs, sems):
        send, recv = sems
        x, y, c, _ = _place()
        sends, recvs = [], []
        for t in range(n):
            for q in range(N_CHIPS):
                i = N_CHIPS * t + q
                cp = _remote(ins[t].at[2 * q + 1 - c], outs[t].at[q], send.at[i], recv.at[i], (x, y, 1 - c))
                sends.append(cp)
                recvs.append(cp)
        return [], sends, recvs

    return _Comm(list(blocks), [jax.ShapeDtypeStruct((N_CHIPS,) + b.shape[1:], b.dtype) for b in blocks],
                 [pltpu.SemaphoreType.DMA((N_CHIPS * n,)), pltpu.SemaphoreType.DMA((N_CHIPS * n,))], descs)


def _chip_exchange(blocks):
    n = len(blocks)

    def descs(ins, outs, sems):
        send, recv, loc = sems
        x, y, c, chips = _place()
        me = 2 * x + y
        local, sends, recvs = [], [], []
        for t in range(n):
            local.append(_local(ins[t].at[me], outs[t].at[me], loc.at[t]))
            for j, (px, py) in enumerate(chips):
                i = 3 * t + j
                q = 2 * px + py
                sends.append(_remote(ins[t].at[q], outs[t].at[me], send.at[i], recv.at[i], (px, py, c)))
                recvs.append(_remote(ins[t].at[q], outs[t].at[q], send.at[i], recv.at[i], (px, py, c)))
        return local, sends, recvs

    return _Comm(list(blocks), [jax.ShapeDtypeStruct(b.shape, b.dtype) for b in blocks],
                 [pltpu.SemaphoreType.DMA((3 * n,)), pltpu.SemaphoreType.DMA((3 * n,)), pltpu.SemaphoreType.DMA((n,))],
                 descs)


def _both(a, b):
    na, oa, sa = len(a.operands), len(a.out_shape), len(a.sems)

    def descs(ins, outs, sems):
        local_a, sends_a, recvs_a = a.descs(ins[:na], outs[:oa], sems[:sa])
        local_b, sends_b, recvs_b = b.descs(ins[na:], outs[oa:], sems[sa:])
        return local_a + local_b, sends_a + sends_b, recvs_a + recvs_b

    return _Comm(a.operands + b.operands, a.out_shape + b.out_shape, a.sems + b.sems, descs,
                 aliases=list(a.aliases) + [(na + i, oa + o) for i, o in b.aliases])


def _run_comms(comms, name):
    first = comms[0]
    n_in, n_out = len(first.operands), len(first.out_shape)

    def body(*refs):
        ins, outs, sems = refs[:n_in], refs[n_in:n_in + n_out], list(refs[n_in + n_out:])
        for k, comm in enumerate(comms):
            mine = [sems.pop(0) for _ in comm.sems]
            comm.start(ins if k == 0 else outs, outs, mine)
            comm.wait(ins if k == 0 else outs, outs, mine)

    outs = pl.pallas_call(
        body, name=name, out_shape=first.out_shape, in_specs=[ANY] * n_in, out_specs=[ANY] * n_out,
        scratch_shapes=[s for comm in comms for s in comm.sems], input_output_aliases=dict(first.aliases),
    )(*first.operands)
    return list(outs)


def _call(body, *, name, grid, in_specs, out_specs, out_shape, operands, scratch_shapes=(), comm=None, aliases=None,
          vmem_limit=VMEM_LIMIT):
    sem = ("arbitrary",) * len(grid)
    params = pltpu.CompilerParams(dimension_semantics=sem, vmem_limit_bytes=vmem_limit)
    aliases = dict(aliases or {})
    if comm is None:
        return pl.pallas_call(body, name=name, grid=grid, in_specs=in_specs, out_specs=out_specs, out_shape=out_shape,
                              scratch_shapes=list(scratch_shapes), input_output_aliases=aliases,
                              compiler_params=params)(*operands)
    n_in, n_out, n_scr = len(in_specs), len(out_specs), len(scratch_shapes)
    c_in, c_out = len(comm.operands), len(comm.out_shape)

    def wrapped(*refs):
        refs = list(refs)
        ins, refs = refs[:n_in], refs[n_in:]
        cins, refs = refs[:c_in], refs[c_in:]
        outs, refs = refs[:n_out], refs[n_out:]
        couts, refs = refs[:c_out], refs[c_out:]
        scr, csems = refs[:n_scr], refs[n_scr:]
        first = last = None
        for axis, size in enumerate(grid):
            at_first, at_last = pl.program_id(axis) == 0, pl.program_id(axis) == size - 1
            first = at_first if first is None else first & at_first
            last = at_last if last is None else last & at_last

        @pl.when(first)
        def _():
            comm.start(cins, couts, csems)

        body(*ins, *outs, *scr)

        @pl.when(last)
        def _():
            comm.wait(cins, couts, csems)

    res = pl.pallas_call(
        wrapped, name=name, grid=grid, in_specs=list(in_specs) + [ANY] * c_in, out_specs=list(out_specs) + [ANY] * c_out,
        out_shape=list(out_shape) + list(comm.out_shape), scratch_shapes=list(scratch_shapes) + list(comm.sems),
        input_output_aliases={**aliases, **{n_in + i: n_out + o for i, o in comm.aliases}}, compiler_params=params,
    )(*operands, *comm.operands)
    return list(res[:n_out]), list(res[n_out:])


def _mm_tn(a, b, name, blocks=1, comm=None, room=None, before=None):
    t, m = a.shape
    n = b.shape[1]
    tk = min(WGRAD_ROWS, t)
    nk = t // tk
    cb = n // blocks
    per = max(1, WGRAD_BLOCK_COLUMNS // cb) if blocks > 1 else 1
    tn = per * cb if blocks > 1 else min(WGRAD_TILE, n)
    tm = min(WGRAD_TILE, m)
    assert blocks == 1 or tm == m
    total = blocks if before is None and room is None else (room if before is None else before.shape[0])
    first = (total - blocks) // per if before is not None else 0
    assert blocks > 1 or total == 1

    ni, nj = m // tm, n // tn
    steps = ni * nj * nk

    def body(*refs):
        (a_hbm, b_hbm), (o_ref, acc, a_buf, b_buf, sems) = refs[:2], refs[-5:]
        k = pl.program_id(2)
        step = (pl.program_id(0) * nj + pl.program_id(1)) * nk + k

        def fetch(s):
            slot = s % WGRAD_SLOTS
            rows = pl.ds((s % nk) * tk, tk)
            return (pltpu.make_async_copy(a_hbm.at[rows, pl.ds((s // (nk * nj)) * tm, tm)], a_buf.at[slot], sems.at[0, slot]),
                    pltpu.make_async_copy(b_hbm.at[rows, pl.ds(((s // nk) % nj) * tn, tn)], b_buf.at[slot], sems.at[1, slot]))

        def start(s):
            for thread, copy in enumerate(fetch(s)):
                copy.start(priority=thread)

        @pl.when(step == 0)
        def _():
            for s in range(min(WGRAD_SLOTS - 1, steps)):
                start(s)

        @pl.when(step + WGRAD_SLOTS - 1 < steps)
        def _():
            start(step + WGRAD_SLOTS - 1)

        for copy in fetch(step):
            copy.wait()

        @pl.when(k == 0)
        def _():
            acc[...] = jnp.zeros_like(acc)
        slot = step % WGRAD_SLOTS
        acc[...] += _dot_tn(a_buf[slot], b_buf[slot])

        @pl.when(k == nk - 1)
        def _():
            if blocks == 1:
                o_ref[...] = acc[...].astype(o_ref.dtype)
            else:
                for s in range(per):
                    o_ref[s] = acc[:, s * cb:(s + 1) * cb].astype(o_ref.dtype)

    if blocks == 1:
        out_spec = pl.BlockSpec((tm, tn), lambda i, j, k: (i, j))
        out_shape = jax.ShapeDtypeStruct((m, n), GRAD_DTYPE)
    else:
        out_spec = pl.BlockSpec((per, m, cb), lambda i, j, k: (first + j, 0, 0))
        out_shape = jax.ShapeDtypeStruct((total, m, cb), GRAD_DTYPE)
    given = [] if before is None else [before]
    res = _call(body, name=name, grid=(ni, nj, nk), comm=comm, in_specs=[ANY] * (2 + len(given)),
                out_specs=[out_spec], out_shape=[out_shape], operands=[a, b] + given,
                aliases={2: 0} if given else None,
                scratch_shapes=[pltpu.VMEM((tm, tn), F32), pltpu.VMEM((WGRAD_SLOTS, tk, tm), a.dtype),
                                pltpu.VMEM((WGRAD_SLOTS, tk, tn), b.dtype), pltpu.SemaphoreType.DMA((2, WGRAD_SLOTS))])
    return res[0] if comm is None else (res[0][0], res[1])


INPROJ_TN = 1024


def _inproj_fwd(x, g1, w_blocks, comm):
    t = x.shape[0]
    tm = min(PROJ_ROW_TILE, t)
    nb, _, cb = w_blocks.shape

    def body(x_ref, g_ref, w_hbm, u_ref, p_ref, w_all, sems):
        @pl.when(pl.program_id(0) == 0)
        def _():
            copies = [pltpu.make_async_copy(w_hbm.at[d], w_all.at[:, pl.ds(d * cb, cb)], sems.at[d]) for d in range(nb)]
            for cp in copies:
                cp.start()
            for cp in copies:
                cp.wait()

        _, _, u = _rms_fwd(x_ref[...], g_ref[...])
        u = u.astype(MXU_DTYPE)
        u_ref[...] = u
        for lo in range(0, D_IN, INPROJ_TN):
            p_ref[:, lo:lo + INPROJ_TN] = _dot(u, w_all[:, lo:lo + INPROJ_TN]).astype(p_ref.dtype)

    return _call(body, name="inproj_fwd", grid=(t // tm,), comm=comm,
                 in_specs=[pl.BlockSpec((tm, D_MODEL), lambda i: (i, 0)), pl.BlockSpec((1, D_MODEL), lambda i: (0, 0)), ANY],
                 out_specs=[pl.BlockSpec((tm, D_MODEL), lambda i: (i, 0)), pl.BlockSpec((tm, D_IN), lambda i: (i, 0))],
                 out_shape=[jax.ShapeDtypeStruct((t, D_MODEL), MXU_DTYPE), jax.ShapeDtypeStruct((t, D_IN), MXU_DTYPE)],
                 scratch_shapes=[pltpu.VMEM((D_MODEL, nb * cb), w_blocks.dtype), pltpu.SemaphoreType.DMA((nb,))],
                 operands=[x, g1, w_blocks])


def _lru_gates(xc, wa, ba, wx, bx, sp, fill=lambda: None):
    r = _sigmoid(_dot(xc, wa) + ba)
    fill()
    ig = _sigmoid(_dot(xc, wx) + bx)
    fill()
    log_a = (-LRU_C) * r * sp
    a = jnp.exp(log_a)
    m = jnp.sqrt(-jnp.tanh(log_a) * (a * a + 1.0))
    return r, ig, a, m


def _fused(parts, name, comm=None):
    grid = parts[0]["grid"]
    assert all(p["grid"] == grid for p in parts)
    counts = [(len(p["in_specs"]), len(p["out_specs"]), len(p.get("scratch_shapes", ()))) for p in parts]

    def body(*refs):
        refs = list(refs)
        groups = []
        for kind in range(3):
            taken = []
            for c in counts:
                taken.append(refs[:c[kind]])
                refs = refs[c[kind]:]
            groups.append(taken)
        ins, outs, scr = groups
        pending = []

        def fill(n=None):
            for _ in range(share if n is None else n):
                if pending:
                    pending.pop(0)()

        ctx = dict(outs=outs, scratch=scr, fill=fill)
        run = lambda key: [p[key](*ins[k], *outs[k], *scr[k], ctx) for k, p in enumerate(parts) if key in p]
        run("head")
        for pieces in run("units"):
            pending.extend(pieces)
        points = sum(p.get("fill_points", 0) for p in parts)
        share = -(-len(pending) // max(points, 1))
        run("body")
        fill(len(pending))
        run("tail")

    cat = lambda key: [x for p in parts for x in p.get(key, ())]
    res = _call(body, name=name, grid=grid, comm=comm, vmem_limit=FUSED_VMEM_LIMIT,
                in_specs=cat("in_specs"), out_specs=cat("out_specs"),
                out_shape=cat("out_shape"), scratch_shapes=cat("scratch_shapes"), operands=cat("operands"))
    outs, side = (res if comm is not None else (res, None))
    split, at = [], 0
    for _, n_out, _ in counts:
        split.append(list(outs[at:at + n_out]))
        at += n_out
    return split if comm is None else (split, side)


def _lru_fwd(proj, conv_w, conv_b, wa, ba, wx, bx, lam, gain):
    t = proj.shape[0]
    tm = min(MIX_ROW_TILE, t)
    c = D_LRU

    def body(x_ref, xh_ref, g_ref, cw_ref, cb_ref, wa_ref, ba_ref, wx_ref, bx_ref, lam_ref, gain_ref,
             xc_ref, h_ref, y_ref, a_scr, b_scr, carry, ctx):
        fill = ctx["fill"]
        i = pl.program_id(0)

        @pl.when(i == 0)
        def _():
            carry[...] = jnp.zeros_like(carry)

        fill()
        x = x_ref[...].astype(F32)
        prev = jnp.where(i == 0, 0.0, xh_ref[...].astype(F32)[-SUBLANES:, :])
        cw = cw_ref[...]
        xc = cb_ref[...] + cw[LRU_CONV - 1:LRU_CONV, :] * x
        for k in range(LRU_CONV - 1):
            xc = xc + cw[k:k + 1, :] * _shift_down(prev, x, LRU_CONV - 1 - k)
        xc_ref[...] = xc
        fill()
        sp = _softplus(-lam_ref[...])
        _, ig, a, m = _lru_gates(xc, wa_ref[...], ba_ref[...], wx_ref[...], bx_ref[...], sp, fill)
        fill()
        ga, gb = _group_scan(a, m * (ig * xc), reverse=False, fill=fill)
        a_scr[...] = ga
        b_scr[...] = gb
        fill()
        carry[...] = _carry_scan(a_scr, b_scr, h_ref, carry[...], reverse=False)
        fill()
        z = h_ref[...] * _gelu(g_ref[...].astype(F32))
        fill()
        _, _, y = _rms_fwd(z, gain_ref[...])
        y_ref[...] = y.astype(y_ref.dtype)

    row = lambda i: (i, 0)
    full = lambda i: (0, 0)
    vec = pl.BlockSpec((1, c), full)
    hb = _halo_rows(proj.dtype)
    return dict(body=body, grid=(t // tm,), fill_points=6,
                in_specs=[pl.BlockSpec((tm, c), row), pl.BlockSpec((hb, c), _halo_map(tm, 0, hb)),
                          pl.BlockSpec((tm, c), lambda i: (i, 1)),
                          pl.BlockSpec((LRU_CONV, c), full), vec, pl.BlockSpec((c, c), full), vec,
                          pl.BlockSpec((c, c), full), vec, vec, vec],
                out_specs=[pl.BlockSpec((tm, c), row), pl.BlockSpec((tm, c), row), pl.BlockSpec((tm, c), row)],
                out_shape=[jax.ShapeDtypeStruct((t, c), F32), jax.ShapeDtypeStruct((t, c), F32),
                           jax.ShapeDtypeStruct((t, c), MXU_DTYPE)],
                scratch_shapes=[pltpu.VMEM((tm, c), F32), pltpu.VMEM((tm, c), F32), pltpu.VMEM((SUBLANES, c), F32)],
                operands=[proj, proj, proj, conv_w, conv_b, wa, ba, wx, bx, lam, gain])


def _ret_consts():
    c = RET_CHUNK
    log_g = jnp.log1p(-jnp.exp2(-5.0 - jnp.arange(RET_HEADS, dtype=F32)))
    idx = jnp.arange(c, dtype=F32)
    diff = idx[:, None] - idx[None, :]
    decay = jnp.where(diff[None] >= 0, jnp.exp(jnp.maximum(diff, 0.0)[None] * log_g[:, None, None]), 0.0)
    zeta = jnp.exp((c - 1 - idx)[None, :] * log_g[:, None])
    xi = jnp.exp((idx + 1.0)[None, :] * log_g[:, None])
    spread = lambda v: jnp.repeat(v.T, RET_HEAD_DIM, axis=1)
    log_g_np = np.log1p(-np.exp2(-5.0 - np.arange(RET_HEADS, dtype=np.float32))).astype(np.float32)
    g_chunk = [float(np.exp(np.float32(c) * lg)) for lg in log_g_np]
    return decay, spread(xi), spread(zeta), g_chunk


def _rope_tables(t):
    pos = np.arange(t, dtype=np.float32)
    inv_freq = np.float32(ROPE_BASE) ** (-np.arange(0, RET_HEAD_DIM, 2, dtype=np.float32) / np.float32(RET_HEAD_DIM))
    ang = (pos[:, None] * inv_freq.astype(np.float32)[None, :]).astype(np.float32).astype(np.float64)
    cos, sin = np.cos(ang).astype(np.float32), np.sin(ang).astype(np.float32)
    return jnp.asarray(np.concatenate([cos, cos], axis=-1)), jnp.asarray(np.concatenate([-sin, sin], axis=-1))


def _rope(x, cos2, sin_signed):
    return x * cos2 + pltpu.roll(x, RET_HEAD_DIM // 2, 1) * sin_signed


def _rope_bwd(d, cos2, sin_signed):
    return d * cos2 + pltpu.roll(d * sin_signed, RET_HEAD_DIM // 2, 1)


RET_SCALE = RET_HEAD_DIM ** -0.5


RET_CHUNKS_PER_STEP = MIX_ROW_TILE // RET_CHUNK


def _ret_fwd(proj, cos2, sin_signed, gain):
    t = proj.shape[0]
    c, d, nh = RET_CHUNK, RET_HEAD_DIM, RET_HEADS
    n_chunks = t // c
    per = RET_CHUNKS_PER_STEP if n_chunks % RET_CHUNKS_PER_STEP == 0 else 1
    rows = per * c
    decay, xi, zeta, g_chunk = _ret_consts()

    def units(qk_ref, vg_ref, cos_ref, sin_ref, dec_ref, xi_ref, zeta_ref, gain_ref, o_ref, y_ref, st_ref, state, ctx):
        cur = [None] * nh

        def start():
            @pl.when(pl.program_id(0) == 0)
            def _():
                state[...] = jnp.zeros_like(state)
            for h in range(nh):
                cur[h] = state[h]

        def retain(s, h, keep):
            rs = slice(s * c, (s + 1) * c)
            cos2, sin_s = cos_ref[rs, :], sin_ref[rs, :]
            lo = h * d
            q = _rope(qk_ref[rs, lo:lo + d].astype(F32), cos2, sin_s)
            k = _rope(qk_ref[rs, D_RET + lo:D_RET + lo + d].astype(F32), cos2, sin_s) * RET_SCALE
            v = vg_ref[rs, lo:lo + d]
            s_prev = cur[h]
            st_ref[s, h] = s_prev
            scores = _dot_nt(q, k) * dec_ref[h]
            o = _dot(scores, v) + _dot(q * xi_ref[:, lo:lo + d], s_prev)
            cur[h] = s_prev * g_chunk[h] + _dot_tn(k * zeta_ref[:, lo:lo + d], v)
            o_ref[rs, lo:lo + d] = o
            keep["o"] = o

        def normalise(s, h, keep):
            rs = slice(s * c, (s + 1) * c)
            lo = h * d
            o = keep["o"]
            g = vg_ref[rs, D_RET + lo:D_RET + lo + d].astype(F32)
            mu = jnp.mean(o, axis=-1, keepdims=True)
            oc = o - mu
            on = oc * lax.rsqrt(jnp.mean(oc * oc, axis=-1, keepdims=True) + NORM_EPS)
            y_ref[rs, lo:lo + d] = (on * gain_ref[:, lo:lo + d] * (g * _sigmoid(g))).astype(y_ref.dtype)

        def end():
            for h in range(nh):
                state[h] = cur[h]

        pieces = [start]
        for s in range(per):
            for h in range(nh):
                keep = {}
                pieces += [lambda s=s, h=h, keep=keep: retain(s, h, keep),
                           lambda s=s, h=h, keep=keep: normalise(s, h, keep)]
        return pieces + [end]

    full2 = lambda i: (0, 0)
    return dict(units=units, grid=(n_chunks // per,),
                in_specs=[pl.BlockSpec((rows, 2 * D_RET), lambda i: (i, 1)),
                          pl.BlockSpec((rows, 2 * D_RET), lambda i: (i, 2)),
                          pl.BlockSpec((rows, d), lambda i: (i, 0)), pl.BlockSpec((rows, d), lambda i: (i, 0)),
                          pl.BlockSpec((nh, c, c), lambda i: (0, 0, 0)), pl.BlockSpec((c, D_RET), full2),
                          pl.BlockSpec((c, D_RET), full2), pl.BlockSpec((1, D_RET), full2)],
                out_specs=[pl.BlockSpec((rows, D_RET), lambda i: (i, 0)), pl.BlockSpec((rows, D_RET), lambda i: (i, 0)),
                           pl.BlockSpec((per, nh, d, d), lambda i: (i, 0, 0, 0))],
                out_shape=[jax.ShapeDtypeStruct((t, D_RET), F32), jax.ShapeDtypeStruct((t, D_RET), MXU_DTYPE),
                           jax.ShapeDtypeStruct((n_chunks, nh, d, d), F32)],
                scratch_shapes=[pltpu.VMEM((nh, d, d), F32)],
                operands=[proj, proj, cos2, sin_signed, decay, xi, zeta, gain])


def _outproj_fwd(x, y_lru, y_ret, w_out, g2, comm):
    t = x.shape[0]
    tm = min(PROJ_ROW_TILE, t)

    def body(x_ref, yl_ref, yr_ref, w_ref, g_ref, h1_ref, u2_ref):
        h1 = x_ref[...] + _dot(yl_ref[...], w_ref[:D_LRU, :]) + _dot(yr_ref[...], w_ref[D_LRU:, :])
        h1_ref[...] = h1
        _, _, u = _rms_fwd(h1, g_ref[...])
        u2_ref[...] = u.astype(u2_ref.dtype)

    row = lambda i: (i, 0)
    return _call(body, name="outproj_fwd", grid=(t // tm,), comm=comm,
                 in_specs=[pl.BlockSpec((tm, D_MODEL), row), pl.BlockSpec((tm, D_LRU), row), pl.BlockSpec((tm, D_RET), row),
                           _resident((D_MODEL, D_MODEL)), pl.BlockSpec((1, D_MODEL), lambda i: (0, 0))],
                 out_specs=[pl.BlockSpec((tm, D_MODEL), row), pl.BlockSpec((tm, D_MODEL), row)],
                 out_shape=[jax.ShapeDtypeStruct((t, D_MODEL), F32), jax.ShapeDtypeStruct((t, D_MODEL), MXU_DTYPE)],
                 operands=[x, y_lru, y_ret, w_out, g2])


FFN_TN = 768
FFN_NJ = D_FF // FFN_TN
FFN_GROUP = 4


def _ffn_fwd(u2, w_blocks, conv_w, conv_b, w_down, h1, gf, target):
    t = u2.shape[0]
    tm = min(ROW_TILE, t)
    tn, nj, group = FFN_TN, FFN_NJ, FFN_GROUP
    ng, tw = nj // group, group * tn
    hb = _halo_rows(u2.dtype)
    assert w_blocks.shape == (2 * nj, D_MODEL, tn)

    def conv(ext, col, up_ref, conv_ref, cw_ref, cb_ref, first):
        x = ext[hb:, :]
        up_ref[:, col] = x.astype(up_ref.dtype)
        prev = jnp.where(first, 0.0, ext[hb - SUBLANES:hb, :])
        cw = cw_ref[:, col]
        y = cb_ref[:, col] + cw[FFN_CONV - 1:FFN_CONV, :] * x
        for k in range(FFN_CONV - 1):
            y = y + cw[k:k + 1, :] * _shift_down(prev, x, FFN_CONV - 1 - k)
        conv_ref[:, col] = y.astype(conv_ref.dtype)
        return y

    def body(u_ref, uh_ref, w_ref, cwa_ref, cwv_ref, cba_ref, cbv_ref, wd_ref, h1_ref, gf_ref, tg_ref,
             upa_ref, upv_ref, ca_ref, cv_ref, act_ref, dh_ref, dhb_ref, dgf_ref, loss_ref, acc):
        i, jg = pl.program_id(0), pl.program_id(1)

        @pl.when((i == 0) & (jg == 0))
        def _():
            dgf_ref[...] = jnp.zeros_like(dgf_ref)
            loss_ref[...] = jnp.zeros_like(loss_ref)

        @pl.when(jg == 0)
        def _():
            acc[...] = jnp.zeros_like(acc)

        u_ext = jnp.concatenate([uh_ref[...], u_ref[...]], axis=0)

        def project(jj):
            j = jg * group + jj
            return _dot(u_ext, w_ref[j]), _dot(u_ext, w_ref[nj + j])

        down, ahead = None, project(0)
        for jj in range(group):
            col = slice(jj * tn, (jj + 1) * tn)
            j = jg * group + jj
            ext_a, ext_v = ahead
            if jj + 1 < group:
                ahead = project(jj + 1)
            a = conv(ext_a, col, upa_ref, ca_ref, cwa_ref, cba_ref, i == 0)
            v = conv(ext_v, col, upv_ref, cv_ref, cwv_ref, cbv_ref, i == 0)
            act = (_gelu(a) * v).astype(act_ref.dtype)
            act_ref[:, col] = act
            part = _dot(act, wd_ref[pl.ds(pl.multiple_of(j * tn, tn), tn), :])
            down = part if down is None else down + part
        acc[...] += down

        @pl.when(jg == ng - 1)
        def _():
            n, rstd, y = _rms_fwd(h1_ref[...] + acc[...], gf_ref[...])
            err = y - tg_ref[...]
            loss_ref[...] += (0.5 / D_MODEL) * jnp.sum(err * err)
            dh, dgf = _rms_bwd(err * (1.0 / D_MODEL), n, rstd, gf_ref[...])
            dgf_ref[...] += dgf
            dh_ref[...] = dh
            dhb_ref[...] = dh.astype(dhb_ref.dtype)

    per = tm // hb
    row = lambda i, j: (i, 0)
    const = lambda i, j: (0, 0)
    tile = pl.BlockSpec((tm, tw), lambda i, j: (i, j))
    return _call(body, name="ffn_fwd", grid=(t // tm, ng), vmem_limit=FUSED_VMEM_LIMIT,
                 in_specs=[pl.BlockSpec((tm, D_MODEL), row),
                           pl.BlockSpec((hb, D_MODEL), lambda i, j: (jnp.maximum(i * per - 1, 0), 0)),
                           _resident(w_blocks.shape),
                           pl.BlockSpec((FFN_CONV, tw), lambda i, j: (0, j)),
                           pl.BlockSpec((FFN_CONV, tw), lambda i, j: (0, j + ng)),
                           pl.BlockSpec((1, tw), lambda i, j: (0, j)), pl.BlockSpec((1, tw), lambda i, j: (0, j + ng)),
                           _resident((D_FF, D_MODEL)),
                           pl.BlockSpec((tm, D_MODEL), row), pl.BlockSpec((1, D_MODEL), const),
                           pl.BlockSpec((tm, D_MODEL), row)],
                 out_specs=[tile] * 5 + [pl.BlockSpec((tm, D_MODEL), row),
                            pl.BlockSpec((tm, D_MODEL), row), pl.BlockSpec((SUBLANES, D_MODEL), const),
                            pl.BlockSpec((SUBLANES, LANES), const)],
                 out_shape=[jax.ShapeDtypeStruct((t, D_FF), MXU_DTYPE)] * 5 + [
                            jax.ShapeDtypeStruct((t, D_MODEL), F32),
                            jax.ShapeDtypeStruct((t, D_MODEL), MXU_DTYPE), jax.ShapeDtypeStruct((SUBLANES, D_MODEL), F32),
                            jax.ShapeDtypeStruct((SUBLANES, LANES), F32)],
                 scratch_shapes=[pltpu.VMEM((tm, D_MODEL), F32)],
                 operands=[u2, u2, w_blocks, conv_w, conv_w, conv_b, conv_b, w_down, h1, gf, target])


FFN_ACC_ROWS = SUBLANES * (FFN_CONV + 1)


def _ffn_bwd(dh2, dh2_b, w_down, up_a, up_v, conv_a, conv_v, conv_w, w_up_blocks, h1, g2, comm):
    t = up_a.shape[0]
    tm = min(ROW_TILE, t)
    tn, nj, group = FFN_TN, FFN_NJ, FFN_GROUP
    ng, tw = nj // group, group * tn
    ni = t // tm
    assert w_up_blocks.shape == (2 * nj, D_MODEL, tn)

    def conv_bwd(dy, x, cw, acc_ref, carry_ref, dup_ref, col):
        nxt = carry_ref[...]
        carry_ref[...] = dy[:SUBLANES, :]
        ahead = [_shift_up(dy, nxt, FFN_CONV - 1 - k) for k in range(FFN_CONV)]
        dx = cw[FFN_CONV - 1:FFN_CONV, :] * dy
        for k in range(FFN_CONV - 1):
            dx = dx + cw[k:k + 1, :] * ahead[k]
        dx = dx.astype(dup_ref.dtype)
        dup_ref[:, col] = dx
        for k in range(FFN_CONV):
            acc_ref[k * SUBLANES:(k + 1) * SUBLANES, :] += _colsum8(ahead[k] * x)
        acc_ref[FFN_CONV * SUBLANES:, :] += _colsum8(dy)
        return dx

    def body(dh_ref, dhb_ref, wd_ref, ua_ref, uv_ref, ca_ref, cv_ref, cwa_ref, cwv_ref, wu_ref, h1_ref, g2_ref,
             dua_ref, duv_ref, acca_ref, accv_ref, dh1_ref, dh1b_ref, dg2_ref, carry_a, carry_v, du):
        i, jg = pl.program_id(0), pl.program_id(1)

        @pl.when((i == 0) & (jg == 0))
        def _():
            for ref in (acca_ref, accv_ref, carry_a, carry_v, dg2_ref):
                ref[...] = jnp.zeros_like(ref)

        dhb = dhb_ref[...]

        def through_down(jj):
            j = jg * group + jj
            return _dot_nt(dhb, wd_ref[pl.ds(pl.multiple_of(j * tn, tn), tn), :])

        part, ahead = None, through_down(0)
        for jj in range(group):
            col = slice(jj * tn, (jj + 1) * tn)
            j = jg * group + jj
            dact = ahead
            if jj + 1 < group:
                ahead = through_down(jj + 1)
            v = cv_ref[:, col].astype(F32)
            g, dg = _gelu_parts(ca_ref[:, col].astype(F32))
            da = conv_bwd(dact * v * dg, ua_ref[:, col].astype(F32), cwa_ref[:, col], acca_ref.at[j], carry_a.at[j],
                          dua_ref, col)
            dv = conv_bwd(dact * g, uv_ref[:, col].astype(F32), cwv_ref[:, col], accv_ref.at[j], carry_v.at[j],
                          duv_ref, col)
            term = _dot_nt(da, wu_ref[j]) + _dot_nt(dv, wu_ref[nj + j])
            part = term if part is None else part + term

        @pl.when(jg == 0)
        def _():
            du[...] = part

        @pl.when(jg > 0)
        def _():
            du[...] += part

        @pl.when(jg == ng - 1)
        def _():
            n, rstd, _ = _rms_fwd(h1_ref[...], g2_ref[...])
            dh1, dg2 = _rms_bwd(du[...], n, rstd, g2_ref[...])
            dh1 = dh1 + dh_ref[...]
            dg2_ref[...] += dg2
            dh1_ref[...] = dh1
            dh1b_ref[...] = dh1.astype(dh1b_ref.dtype)

    row = lambda i, j: (ni - 1 - i, 0)
    const = lambda i, j: (0, 0)
    tile = pl.BlockSpec((tm, tw), lambda i, j: (ni - 1 - i, j))
    acc = pl.BlockSpec((nj, FFN_ACC_ROWS, tn), lambda i, j: (0, 0, 0))
    return _call(body, name="ffn_bwd", grid=(ni, ng), comm=comm, vmem_limit=FFN_BWD_VMEM_LIMIT,
                 in_specs=[pl.BlockSpec((tm, D_MODEL), row), pl.BlockSpec((tm, D_MODEL), row),
                           _resident((D_FF, D_MODEL)), tile, tile, tile, tile,
                           pl.BlockSpec((FFN_CONV, tw), lambda i, j: (0, j)),
                           pl.BlockSpec((FFN_CONV, tw), lambda i, j: (0, j + ng)),
                           _resident(w_up_blocks.shape), pl.BlockSpec((tm, D_MODEL), row),
                           pl.BlockSpec((1, D_MODEL), const)],
                 out_specs=[tile, tile, acc, acc, pl.BlockSpec((tm, D_MODEL), row), pl.BlockSpec((tm, D_MODEL), row),
                            pl.BlockSpec((SUBLANES, D_MODEL), const)],
                 out_shape=[jax.ShapeDtypeStruct((t, D_FF), MXU_DTYPE), jax.ShapeDtypeStruct((t, D_FF), MXU_DTYPE),
                            jax.ShapeDtypeStruct((nj, FFN_ACC_ROWS, tn), F32),
                            jax.ShapeDtypeStruct((nj, FFN_ACC_ROWS, tn), F32),
                            jax.ShapeDtypeStruct((t, D_MODEL), F32), jax.ShapeDtypeStruct((t, D_MODEL), MXU_DTYPE),
                            jax.ShapeDtypeStruct((SUBLANES, D_MODEL), F32)],
                 scratch_shapes=[pltpu.VMEM((nj, SUBLANES, tn), F32), pltpu.VMEM((nj, SUBLANES, tn), F32),
                                 pltpu.VMEM((tm, D_MODEL), F32)],
                 operands=[dh2, dh2_b, w_down, up_a, up_v, conv_a, conv_v, conv_w, conv_w, w_up_blocks, h1, g2])


def _ret_bwd(proj, cos2, sin_signed, gain, o, states, dmix_at):
    t = proj.shape[0]
    c, d, nh = RET_CHUNK, RET_HEAD_DIM, RET_HEADS
    n_chunks = t // c
    per = RET_CHUNKS_PER_STEP if n_chunks % RET_CHUNKS_PER_STEP == 0 else 1
    rows = per * c
    n_steps = n_chunks // per
    decay, xi, zeta, g_chunk = _ret_consts()
    base = 2 * D_LRU

    def units(qk_ref, vg_ref, cos_ref, sin_ref, dec_ref, xi_ref, zeta_ref, gain_ref, o_ref, st_ref,
              dp_ref, dgain_ref, gstate, ctx):
        cur = [None] * nh
        dmix = ctx["scratch"][dmix_at[0]][dmix_at[1]]

        def start():
            @pl.when(pl.program_id(0) == 0)
            def _():
                gstate[...] = jnp.zeros_like(gstate)
                dgain_ref[...] = jnp.zeros_like(dgain_ref)
            for h in range(nh):
                cur[h] = gstate[h]

        def gate_and_norm(s, h, keep):
            rs = slice(s * c, (s + 1) * c)
            lo = h * d
            g = vg_ref[rs, D_RET + lo:D_RET + lo + d].astype(F32)
            gain_h = gain_ref[:, lo:lo + d]
            dy = dmix[rs, D_LRU + lo:D_LRU + lo + d]
            sg = _sigmoid(g)
            o_h = o_ref[rs, lo:lo + d]
            oc = o_h - jnp.mean(o_h, axis=-1, keepdims=True)
            rstd = lax.rsqrt(jnp.mean(oc * oc, axis=-1, keepdims=True) + NORM_EPS)
            on = oc * rstd
            at = base + 3 * D_RET + lo
            dp_ref[rs, at:at + d] = (dy * on * gain_h * (sg * (1.0 + g * (1.0 - sg)))).astype(dp_ref.dtype)
            don_g = dy * (g * sg)
            dgain_ref[:, lo:lo + d] += _colsum8(don_g * on)
            don = don_g * gain_h
            keep["do"] = rstd * (don - jnp.mean(don, axis=-1, keepdims=True)
                                 - on * jnp.mean(don * on, axis=-1, keepdims=True))

        def retain(s, h, keep):
            rs = slice(s * c, (s + 1) * c)
            cos2, sin_s = cos_ref[rs, :], sin_ref[rs, :]
            lo = h * d
            q = _rope(qk_ref[rs, lo:lo + d].astype(F32), cos2, sin_s)
            k = _rope(qk_ref[rs, D_RET + lo:D_RET + lo + d].astype(F32), cos2, sin_s) * RET_SCALE
            v = vg_ref[rs, lo:lo + d]
            xi_h, zeta_h, dec = xi_ref[:, lo:lo + d], zeta_ref[:, lo:lo + d], dec_ref[h]
            do = keep["do"]
            s_prev = st_ref[s, h]
            g_next = cur[h]
            p = _dot_nt(q, k) * dec
            dpm = _dot_nt(do, v) * dec
            keep["dq"] = _dot(dpm, k) + _dot_nt(do, s_prev) * xi_h
            keep["dk"] = _dot_tn(dpm, q) + _dot_nt(v, g_next) * zeta_h
            dv = _dot_tn(p, do) + _dot(k * zeta_h, g_next)
            cur[h] = g_next * g_chunk[h] + _dot_tn(q * xi_h, do)
            at = base + 2 * D_RET + lo
            dp_ref[rs, at:at + d] = dv.astype(dp_ref.dtype)

        def unrope(s, h, keep):
            rs = slice(s * c, (s + 1) * c)
            cos2, sin_s = cos_ref[rs, :], sin_ref[rs, :]
            lo = h * d
            dp_ref[rs, base + lo:base + lo + d] = _rope_bwd(keep["dq"], cos2, sin_s).astype(dp_ref.dtype)
            at = base + D_RET + lo
            dp_ref[rs, at:at + d] = _rope_bwd(keep["dk"] * RET_SCALE, cos2, sin_s).astype(dp_ref.dtype)

        def end():
            for h in range(nh):
                gstate[h] = cur[h]

        pieces = [start]
        for s in reversed(range(per)):
            for h in range(nh):
                keep = {}
                pieces += [lambda s=s, h=h, keep=keep, f=f: f(s, h, keep) for f in (gate_and_norm, retain, unrope)]
        return pieces + [end]

    rev = lambda col: (lambda i: (n_steps - 1 - i, col))
    full2 = lambda i: (0, 0)
    return dict(units=units, grid=(n_steps,),
                in_specs=[pl.BlockSpec((rows, 2 * D_RET), rev(1)), pl.BlockSpec((rows, 2 * D_RET), rev(2)),
                          pl.BlockSpec((rows, d), rev(0)), pl.BlockSpec((rows, d), rev(0)),
                          pl.BlockSpec((nh, c, c), lambda i: (0, 0, 0)), pl.BlockSpec((c, D_RET), full2),
                          pl.BlockSpec((c, D_RET), full2), pl.BlockSpec((1, D_RET), full2),
                          pl.BlockSpec((rows, D_RET), rev(0)),
                          pl.BlockSpec((per, nh, d, d), lambda i: (n_steps - 1 - i, 0, 0, 0))],
                out_specs=[pl.BlockSpec((rows, D_IN), rev(0)), pl.BlockSpec((SUBLANES, D_RET), full2)],
                out_shape=[jax.ShapeDtypeStruct((t, D_IN), MXU_DTYPE), jax.ShapeDtypeStruct((SUBLANES, D_RET), F32)],
                scratch_shapes=[pltpu.VMEM((nh, d, d), F32)],
                operands=[proj, proj, cos2, sin_signed, decay, xi, zeta, gain, o, states])


LRU_ACC = {"conv_w": 0, "conv_b": LRU_CONV, "gate_a_b": LRU_CONV + 1, "gate_x_b": LRU_CONV + 2,
           "lambda": LRU_CONV + 3, "norm_gain": LRU_CONV + 4}
LRU_ACC_ROWS = SUBLANES * (LRU_CONV + 5)


def _lru_bwd(proj, xc_all, h_all, conv_w, wa, ba, wx, bx, lam, gain, dproj_part, dmix_at):
    t = proj.shape[0]
    tm = min(MIX_ROW_TILE, t)
    c = D_LRU
    ni = t // tm

    def body(x_ref, xh_ref, g_ref, xc_ref, h_ref, hh_ref, cw_ref, wa_ref, ba_ref, wx_ref, bx_ref, lam_ref,
             gain_ref, acc_ref, dwa_ref, dwx_ref, a_scr, b_scr, mu_scr, carry_mu, carry_dxc, ctx):
        dp_ref = ctx["outs"][dproj_part][0]
        dmix = ctx["scratch"][dmix_at[0]][dmix_at[1]]
        fill = ctx["fill"]
        i = pl.program_id(0)
        r = ni - 1 - i

        @pl.when(i == 0)
        def _():
            acc_ref[...] = jnp.zeros_like(acc_ref)
            dwa_ref[...] = jnp.zeros_like(dwa_ref)
            dwx_ref[...] = jnp.zeros_like(dwx_ref)
            carry_mu[...] = jnp.zeros_like(carry_mu)
            carry_dxc[...] = jnp.zeros_like(carry_dxc)

        def add(name, val, k=0):
            lo = (LRU_ACC[name] + k) * SUBLANES
            acc_ref[lo:lo + SUBLANES, :] += _colsum8(val)

        fill()
        xc, h = xc_ref[...], h_ref[...]
        lam_v = lam_ref[...]
        sp = _softplus(-lam_v)
        rg, ig, a, m = _lru_gates(xc, wa_ref[...], ba_ref[...], wx_ref[...], bx_ref[...], sp, fill)
        gl, dgl = _gelu_parts(g_ref[...].astype(F32))
        fill()
        zn, rstd, _ = _rms_fwd(h * gl, gain_ref[...])
        dy = dmix[:, :c]
        dz, dgain = _rms_bwd(dy, zn, rstd, gain_ref[...])
        lo = LRU_ACC["norm_gain"] * SUBLANES
        acc_ref[lo:lo + SUBLANES, :] += dgain
        dp_ref[:, c:2 * c] = (dz * h * dgl).astype(dp_ref.dtype)
        dh = dz * gl
        fill()
        ga, gb = _group_scan(a, a * dh, reverse=True, fill=fill)
        a_scr[...] = ga
        b_scr[...] = gb
        mu_next_tile = carry_mu[...]
        carry_mu[...] = _carry_scan(a_scr, b_scr, mu_scr, mu_next_tile, reverse=True)
        fill()
        lam_t = dh + _shift_up(mu_scr[...], mu_next_tile, 1)
        h_prev = _shift_down(jnp.where(r == 0, 0.0, hh_ref[...]), h, 1)
        da = lam_t * h_prev
        dig = lam_t * m * xc
        dxc = lam_t * m * ig
        dlog_a = da * a - (lam_t * ig * xc) * (a * a) / m
        fill()
        dpr = dlog_a * ((-LRU_C) * sp) * rg * (1.0 - rg)
        add("lambda", dlog_a * ((-LRU_C) * rg) * (-_sigmoid(-lam_v)))
        dpi = dig * ig * (1.0 - ig)
        add("gate_a_b", dpr)
        add("gate_x_b", dpi)
        fill()
        dwa_ref[...] += _dot_tn(xc, dpr)
        dwx_ref[...] += _dot_tn(xc, dpi)
        dxc = dxc + _dot_nt(dpr, wa_ref[...]) + _dot_nt(dpi, wx_ref[...])
        fill()
        add("conv_b", dxc)
        x = x_ref[...].astype(F32)
        prev = jnp.where(r == 0, 0.0, xh_ref[...].astype(F32)[-SUBLANES:, :])
        cw = cw_ref[...]
        nxt = carry_dxc[...]
        carry_dxc[...] = dxc[:SUBLANES, :]
        dx = cw[LRU_CONV - 1:LRU_CONV, :] * dxc
        for k in range(LRU_CONV - 1):
            dx = dx + cw[k:k + 1, :] * _shift_up(dxc, nxt, LRU_CONV - 1 - k)
        fill()
        for k in range(LRU_CONV):
            add("conv_w", dxc * _shift_down(prev, x, LRU_CONV - 1 - k), k)
        dp_ref[:, :c] = dx.astype(dp_ref.dtype)

    hb = _halo_rows(proj.dtype)
    rev = lambda col: (lambda i: (ni - 1 - i, col))
    halo = lambda rows: (lambda i: (jnp.maximum((ni - 1 - i) * (tm // rows) - 1, 0), 0))
    full = lambda i: (0, 0)
    vec = pl.BlockSpec((1, c), full)
    mat = pl.BlockSpec((c, c), full)
    return dict(body=body, grid=(ni,), fill_points=16,
                in_specs=[pl.BlockSpec((tm, c), rev(0)), pl.BlockSpec((hb, c), halo(hb)), pl.BlockSpec((tm, c), rev(1)),
                          pl.BlockSpec((tm, c), rev(0)), pl.BlockSpec((tm, c), rev(0)),
                          pl.BlockSpec((SUBLANES, c), halo(SUBLANES)),
                          pl.BlockSpec((LRU_CONV, c), full), mat, vec, mat, vec, vec, vec],
                out_specs=[pl.BlockSpec((LRU_ACC_ROWS, c), full), mat, mat],
                out_shape=[jax.ShapeDtypeStruct((LRU_ACC_ROWS, c), F32), jax.ShapeDtypeStruct((c, c), F32),
                           jax.ShapeDtypeStruct((c, c), F32)],
                scratch_shapes=[pltpu.VMEM((tm, c), F32), pltpu.VMEM((tm, c), F32), pltpu.VMEM((tm, c), F32),
                                pltpu.VMEM((SUBLANES, c), F32), pltpu.VMEM((SUBLANES, c), F32)],
                operands=[proj, proj, proj, xc_all, h_all, h_all, conv_w, wa, ba, wx, bx, lam, gain])


def _mix_proj_bwd(dh1, dh1_b, w_out, w_in_blocks, x, g1, dproj_part):
    t = x.shape[0]
    tm = min(MIX_ROW_TILE, t)
    ni = t // tm
    nb, _, cb = w_in_blocks.shape
    first_free = -(-2 * D_LRU // cb)
    du = [None]

    def term(dp_ref, w_ref, d):
        part = _dot_nt(dp_ref[:, d * cb:(d + 1) * cb], w_ref[d])
        du[0] = part if du[0] is None else du[0] + part

    def head(dh_ref, dhb_ref, wo_ref, wi_ref, x_ref, g_ref, gx_ref, dg_ref, dmix, ctx):
        @pl.when(pl.program_id(0) == 0)
        def _():
            dg_ref[...] = jnp.zeros_like(dg_ref)
        dmix[...] = _dot_nt(dhb_ref[...], wo_ref[...])
        du[0] = None

    def units(dh_ref, dhb_ref, wo_ref, wi_ref, x_ref, g_ref, gx_ref, dg_ref, dmix, ctx):
        dp_ref = ctx["outs"][dproj_part][0]
        return [lambda d=d: term(dp_ref, wi_ref, d) for d in range(first_free, nb)]

    def tail(dh_ref, dhb_ref, wo_ref, wi_ref, x_ref, g_ref, gx_ref, dg_ref, dmix, ctx):
        dp_ref = ctx["outs"][dproj_part][0]
        for d in range(first_free):
            term(dp_ref, wi_ref, d)
        n, rstd, _ = _rms_fwd(x_ref[...], g_ref[...])
        dx, dg = _rms_bwd(du[0], n, rstd, g_ref[...])
        dg_ref[...] += dg
        gx_ref[...] = dx + dh_ref[...]

    row = lambda i: (ni - 1 - i, 0)
    const = lambda i: (0, 0)
    tile = pl.BlockSpec((tm, D_MODEL), row)
    return dict(head=head, units=units, tail=tail, grid=(ni,),
                in_specs=[tile, tile, _resident(w_out.shape), _resident(w_in_blocks.shape), tile,
                          pl.BlockSpec((1, D_MODEL), const)],
                out_specs=[tile, pl.BlockSpec((SUBLANES, D_MODEL), const)],
                out_shape=[jax.ShapeDtypeStruct((t, D_MODEL), F32), jax.ShapeDtypeStruct((SUBLANES, D_MODEL), F32)],
                scratch_shapes=[pltpu.VMEM((tm, D_MODEL), F32)],
                operands=[dh1, dh1_b, w_out, w_in_blocks, x, g1])


def _pair_sum(core, a, b, name):
    n, r, c = b.shape
    spec = pl.BlockSpec((None, r, c), lambda q, core: (q, 0, 0))

    def body(core_ref, a_ref, b_ref, o_ref):
        o_ref[...] = (a_ref[...].astype(F32) + b_ref[...].astype(F32)).astype(o_ref.dtype)

    return pl.pallas_call(
        body, name=name,
        grid_spec=pltpu.PrefetchScalarGridSpec(
            num_scalar_prefetch=1, grid=(n,),
            in_specs=[pl.BlockSpec((None, r, c), lambda q, core: (2 * q + core[0], 0, 0)), spec], out_specs=spec),
        out_shape=jax.ShapeDtypeStruct(b.shape, b.dtype),
        compiler_params=pltpu.CompilerParams(dimension_semantics=("arbitrary",), vmem_limit_bytes=VMEM_LIMIT),
    )(core, a, b)


ADAMW_BLOCK_BYTES = 4 * 1024 * 1024


def _sum_adamw(parts, w, m, v, name):
    n_parts, r, c = parts.shape
    tr = r
    while n_parts * tr * c * parts.dtype.itemsize > ADAMW_BLOCK_BYTES and tr % (4 * SUBLANES) == 0:
        tr //= 2

    def body(p_ref, w_ref, m_ref, v_ref, g_ref, d_ref, nm_ref, nv_ref):
        g = p_ref[0].astype(F32)
        for s in range(1, n_parts):
            g = g + p_ref[s].astype(F32)
        nm = ADAM_B1 * m_ref[...] + (1.0 - ADAM_B1) * g
        nv = ADAM_B2 * v_ref[...] + (1.0 - ADAM_B2) * (g * g)
        m_hat = nm / (1.0 - ADAM_B1 ** ADAM_STEP)
        v_hat = nv / (1.0 - ADAM_B2 ** ADAM_STEP)
        g_ref[...] = g
        d_ref[...] = -ADAM_LR * (m_hat / (jnp.sqrt(v_hat) + ADAM_EPS) + ADAM_WD * w_ref[...])
        nm_ref[...] = nm
        nv_ref[...] = nv

    row = pl.BlockSpec((tr, c), lambda i: (i, 0))
    return _call(body, name=name, grid=(r // tr,),
                 in_specs=[pl.BlockSpec((n_parts, tr, c), lambda i: (0, i, 0)), row, row, row],
                 out_specs=[row, row, row, row], out_shape=[jax.ShapeDtypeStruct((r, c), F32)] * 4,
                 operands=[parts, w, m, v])


MATRICES = ("w_in", "w_out", "ffn_up_w", "ffn_down_w")
CONVS = ("lru_conv_w", "ffn_conv_w")
REPLICATED = ("norm1_gain", "lru_conv_b", "lru_gate_a_w", "lru_gate_a_b", "lru_gate_x_w", "lru_gate_x_b", "lru_lambda",
              "lru_norm_gain", "ret_norm_gain", "norm2_gain", "ffn_conv_b", "final_norm_gain")
WEIGHTS = ("norm1_gain", "w_in", "lru_conv_w", "lru_conv_b", "lru_gate_a_w", "lru_gate_a_b", "lru_gate_x_w",
           "lru_gate_x_b", "lru_lambda", "lru_norm_gain", "ret_norm_gain", "w_out", "norm2_gain", "ffn_up_w",
           "ffn_conv_w", "ffn_conv_b", "ffn_down_w", "final_norm_gain")


def _rows(a, pad_to):
    a = a.reshape(-1, LANES)
    pad = (-a.shape[0]) % pad_to
    return jnp.pad(a, ((0, pad), (0, 0))) if pad else a


def _pack(arrays, pad_to):
    rows, layout, at = [], [], 0
    for a in arrays:
        r = _rows(a, pad_to)
        layout.append((at, a.size // LANES, a.shape))
        rows.append(r)
        at += r.shape[0]
    return jnp.concatenate(rows, axis=0), layout


def _unpack(packed, layout):
    lead = packed.shape[:-2]
    return [packed[..., at:at + n, :].reshape(lead + shape) for at, n, shape in layout]


def _conv_rows(lru, ffn, dtype, pad_to):
    lead = lru.shape[:-2]
    flat = jnp.concatenate([lru.reshape(lead + (-1,)), ffn.reshape(lead + (-1,))], axis=-1).astype(dtype)
    rows = flat.shape[-1] // LANES
    pad = (-rows) % pad_to
    return jnp.pad(flat.reshape(lead + (rows, LANES)), [(0, 0)] * len(lead) + [(0, pad), (0, 0)])


def _column_blocks(full):
    r, c = full.shape
    return full.reshape(r, N_DEV, c // N_DEV).transpose(1, 0, 2)


def _block_diag(w):
    nh, d, _ = w.shape
    eye = jnp.eye(nh, dtype=w.dtype)
    return (w[:, :, None, :] * eye[:, None, :, None]).reshape(nh * d, nh * d)


def _diag_blocks(dense, nh):
    d = dense.shape[0] // nh
    blocks = dense.reshape(nh, d, nh, d)
    return jnp.stack([blocks[h, :, h, :] for h in range(nh)], axis=0)


def kernel(x, norm1_gain, w_in, lru_conv_w, lru_conv_b, lru_gate_a_w, lru_gate_a_b, lru_gate_x_w, lru_gate_x_b, lru_lambda, lru_norm_gain, ret_norm_gain, w_out, norm2_gain, ffn_up_w, ffn_conv_w, ffn_conv_b, ffn_down_w, final_norm_gain, loss_target, m_norm1_gain, m_w_in, m_lru_conv_w, m_lru_conv_b, m_lru_gate_a_w, m_lru_gate_a_b, m_lru_gate_x_w, m_lru_gate_x_b, m_lru_lambda, m_lru_norm_gain, m_ret_norm_gain, m_w_out, m_norm2_gain, m_ffn_up_w, m_ffn_conv_w, m_ffn_conv_b, m_ffn_down_w, m_final_norm_gain, v_norm1_gain, v_w_in, v_lru_conv_w, v_lru_conv_b, v_lru_gate_a_w, v_lru_gate_a_b, v_lru_gate_x_w, v_lru_gate_x_b, v_lru_lambda, v_lru_norm_gain, v_ret_norm_gain, v_w_out, v_norm2_gain, v_ffn_up_w, v_ffn_conv_w, v_ffn_conv_b, v_ffn_down_w, v_final_norm_gain):
    args = dict(locals())
    given = {n: args[n] for n in WEIGHTS}
    out_shape = {n: given[n].shape for n in WEIGHTS}

    def plain(a):
        return a.reshape(1, -1) if a.ndim <= 2 else a[0]

    w = {n: plain(given[n]) for n in WEIGHTS}
    mom_m = {n: plain(args["m_" + n]) for n in WEIGHTS}
    mom_v = {n: plain(args["v_" + n]) for n in WEIGHTS}
    x2, target = x[0], loss_target[0]
    t = x2.shape[0]
    core = lax.axis_index("c").astype(jnp.int32).reshape(1)
    res = {}

    conv_pad = _conv_rows(w["lru_conv_w"], w["ffn_conv_w"], F32, SUBLANES)
    first = _gather_first([w["w_in"].astype(MXU_DTYPE), conv_pad])
    w_in_blocks, conv_all = _run_comms([first, _gather_second(first.out_shape)], "w_in_all_gather")
    n_lru = w["lru_conv_w"].size
    conv_flat = conv_all.reshape(N_DEV, -1)
    lru_cw = conv_flat[:, :n_lru].reshape((N_DEV,) + w["lru_conv_w"].shape).transpose(1, 0, 2).reshape(LRU_CONV, D_LRU)
    ffn_cw = conv_flat[:, n_lru:n_lru + w["ffn_conv_w"].size].reshape((N_DEV,) + w["ffn_conv_w"].shape)
    ffn_cw = ffn_cw.transpose(1, 0, 2).reshape(FFN_CONV, 2 * D_FF)

    cos2, sin_signed = _rope_tables(t)
    wa = _block_diag(w["lru_gate_a_w"]).astype(MXU_DTYPE)
    wx = _block_diag(w["lru_gate_x_w"]).astype(MXU_DTYPE)
    gf = w["final_norm_gain"]

    early = _gather_first([w["w_out"].astype(MXU_DTYPE), w["ffn_down_w"].astype(MXU_DTYPE)])
    (u1, proj), (w_out_part, down_part) = _inproj_fwd(x2, w["norm1_gain"], w_in_blocks, early)
    ((xc, h_lru, y_lru), (o_ret, y_ret, states)), (w_out_blocks, down_blocks, up_part) = _fused(
        [_lru_fwd(proj, lru_cw, w["lru_conv_b"], wa, w["lru_gate_a_b"], wx, w["lru_gate_x_b"], w["lru_lambda"],
                  w["lru_norm_gain"]),
         _ret_fwd(proj, cos2, sin_signed, w["ret_norm_gain"])],
        "mix_fwd", _both(_gather_second([w_out_part, down_part]), _gather_first([w["ffn_up_w"].astype(MXU_DTYPE)])))
    w_out_full = w_out_blocks.reshape(D_MODEL, D_MODEL)
    w_down_full = down_blocks.reshape(D_FF, D_MODEL)

    (h1, u2), (up_blocks,) = _outproj_fwd(x2, y_lru, y_ret, w_out_full, w["norm2_gain"], _gather_second([up_part]))
    up_a, up_v, conv_a, conv_v, act, dh2, dh2_b, dgf, loss_local = _ffn_fwd(u2, up_blocks, ffn_cw, w["ffn_conv_b"],
                                                                            w_down_full, h1, gf, target)

    def to_owner_chips(blocks, names, tag):
        theirs = _run_comms([_pair_exchange(blocks)], "grads_pair_exchange_" + tag)
        return [_pair_sum(core, a, b, "grads_pair_sum_" + n) for n, a, b in zip(names, blocks, theirs)]

    def adamw(name, parts):
        res[name] = _sum_adamw(parts, w[name], mom_m[name], mom_v[name], "adamw_" + name)

    g = {"final_norm_gain": dgf[0]}
    dup_a, dup_v, acc_a, acc_v, dh1, dh1_b, dg2 = _ffn_bwd(
        dh2, dh2_b, w_down_full, up_a, up_v, conv_a, conv_v, ffn_cw, up_blocks, h1, w["norm2_gain"], None)
    per_col = lambda a: a[:, ::SUBLANES].transpose(1, 0, 2).reshape(FFN_CONV + 1, D_FF)
    acc = jnp.concatenate([per_col(acc_a), per_col(acc_v)], axis=1)
    g_ffn_cw, g["ffn_conv_b"] = acc[:FFN_CONV], acc[FFN_CONV:]
    g["norm2_gain"] = dg2[:1]
    g_up = _mm_tn(u2, dup_a, "ffn_up_wgrad_a", blocks=N_DEV // 2, room=N_DEV)
    g_up = _mm_tn(u2, dup_v, "ffn_up_wgrad_v", blocks=N_DEV // 2, before=g_up)
    g_out_lru, (up_theirs,) = _mm_tn(y_lru, dh1_b, "w_out_wgrad_lru", comm=_pair_exchange([g_up]))
    up_sums = [_pair_sum(core, g_up, up_theirs, "grads_pair_sum_ffn_up_w")]
    g_down, (up_parts,) = _mm_tn(act, dh2_b, "ffn_down_wgrad", comm=_chip_exchange(up_sums))
    adamw("ffn_up_w", up_parts)
    g_out = jnp.concatenate([g_out_lru, _mm_tn(y_ret, dh1_b, "w_out_wgrad_ret")], axis=0)
    low_sums = to_owner_chips([g_down.reshape(N_DEV, D_FF // N_DEV, D_MODEL),
                               g_out.reshape(N_DEV, D_MODEL // N_DEV, D_MODEL)], ["ffn_down_w", "w_out"], "low")
    (dproj, dgain_ret), (grad_x, dg1), (lru_acc, dwa, dwx) = _fused(
        [_ret_bwd(proj, cos2, sin_signed, w["ret_norm_gain"], o_ret, states, dmix_at=(1, 0)),
         _mix_proj_bwd(dh1, dh1_b, w_out_full, w_in_blocks, x2, w["norm1_gain"], dproj_part=0),
         _lru_bwd(proj, xc, h_lru, lru_cw, wa, w["lru_gate_a_b"], wx, w["lru_gate_x_b"], w["lru_lambda"],
                  w["lru_norm_gain"], dproj_part=0, dmix_at=(1, 0))],
        "mix_bwd")
    g["norm1_gain"] = dg1[:1]
    g["ret_norm_gain"] = dgain_ret[:1]
    lru_acc = lru_acc[::SUBLANES]
    g_lru_cw = lru_acc[:LRU_CONV]
    for name in ("conv_b", "gate_a_b", "gate_x_b", "lambda", "norm_gain"):
        g["lru_" + name] = lru_acc[LRU_ACC[name]:LRU_ACC[name] + 1]
    g["lru_gate_a_w"] = _diag_blocks(dwa, LRU_HEADS)
    g["lru_gate_x_w"] = _diag_blocks(dwx, LRU_HEADS)
    rep_packed, rep_layout = _pack([g[n] for n in REPLICATED] + [loss_local], SUBLANES)
    g_in, (down_parts, out_parts, rep_part) = _mm_tn(u1, dproj, "w_in_wgrad", blocks=N_DEV,
                                                     comm=_both(_chip_exchange(low_sums), _gather_first([rep_packed])))
    adamw("ffn_down_w", down_parts)
    adamw("w_out", out_parts)
    g_conv = _conv_rows(_column_blocks(g_lru_cw), _column_blocks(g_ffn_cw), GRAD_DTYPE, 2 * SUBLANES)
    in_sums = to_owner_chips([g_in, g_conv], ["w_in", "conv"], "in")
    in_parts, conv_parts, rep_parts = _run_comms([_both(_chip_exchange(in_sums), _gather_second([rep_part]))],
                                                 "last_grads_exchange")
    adamw("w_in", in_parts)
    pad16 = lambda d: _conv_rows(d["lru_conv_w"], d["ffn_conv_w"], F32, 2 * SUBLANES)
    conv_res = _sum_adamw(conv_parts, pad16(w), pad16(mom_m), pad16(mom_v), "adamw_conv")
    for n, lo, hi in (("lru_conv_w", 0, n_lru), ("ffn_conv_w", n_lru, n_lru + w["ffn_conv_w"].size)):
        res[n] = [r.reshape(-1)[lo:hi].reshape(w[n].shape) for r in conv_res]
    no_state = jnp.zeros_like(loss_local)
    rep_res = _sum_adamw(rep_parts, *[_pack([d[n] for n in REPLICATED] + [no_state], SUBLANES)[0]
                                      for d in (w, mom_m, mom_v)], "adamw_replicated")
    for k in range(4):
        for n, a in zip(REPLICATED, _unpack(rep_res[k], rep_layout)):
            res.setdefault(n, [None] * 4)[k] = a
    loss = _unpack(rep_res[0], rep_layout)[-1][0, 0]

    outs = [loss, grad_x[None]]
    for k in range(4):
        outs += [res[n][k].reshape(out_shape[n]) for n in WEIGHTS]
    return tuple(outs)
```
